```python
import jax, jax.numpy as jnp
from jax import lax
import numpy as np

D_MODEL = 1024
BATCH = 8
SEQ = 2048
DEPTH = 1

D_MIX = D_MODEL
D_REC = D_MIX // 2
D_ATT = D_MIX - D_REC
N_LRU_BLOCKS = 8
LRU_BLOCK = D_REC // N_LRU_BLOCKS
LRU_C = 8.0
CONV_WIDTH = 4
HEAD_DIM = 64
N_HEADS = D_ATT // HEAD_DIM
DILATED_PATTERNS = ((128, 1), (512, 4), (2048, 16))
ROPE_THETA = 10000.0
NORM_EPS = 1e-6
NEG_INF = -1e30
D_IN_PROJ = 2 * D_REC + 4 * D_ATT
SPLIT_IDX = (D_REC, 2 * D_REC, 2 * D_REC + D_ATT, 2 * D_REC + 2 * D_ATT, 2 * D_REC + 3 * D_ATT)

kernel_name = 'hymba_rglru_dilated_attn_layer'


def rms_norm(x, g):
    xf = x.astype(jnp.float32)
    y = xf * lax.rsqrt(jnp.mean(xf * xf, axis=-1, keepdims=True) + NORM_EPS)
    return (y * g.astype(jnp.float32)).astype(x.dtype)


def rotary(x, positions):
    half = HEAD_DIM // 2
    inv_freq = ROPE_THETA ** (-jnp.arange(half, dtype=jnp.float32) / half)
    ang = positions.astype(jnp.float32)[..., None] * inv_freq
    cos = jnp.cos(ang)[:, :, None, :]
    sin = jnp.sin(ang)[:, :, None, :]
    xf = x.astype(jnp.float32)
    x1, x2 = xf[..., :half], xf[..., half:]
    out = jnp.concatenate([x1 * cos - x2 * sin, x2 * cos + x1 * sin], axis=-1)
    return out.astype(x.dtype)


def causal_depthwise_conv(x, w, b):
    S = x.shape[1]
    xp = jnp.pad(x, ((0, 0), (CONV_WIDTH - 1, 0), (0, 0)))
    return sum(xp[:, k:k + S, :] * w[k] for k in range(CONV_WIDTH)) + b


def rg_lru(x, w_a, b_a, w_x, b_x, lam):
    B, S, C = x.shape
    xb = x.reshape(B, S, N_LRU_BLOCKS, LRU_BLOCK)
    r = jax.nn.sigmoid(jnp.einsum('bshi,hij->bshj', xb, w_a).reshape(B, S, C) + b_a)
    i = jax.nn.sigmoid(jnp.einsum('bshi,hij->bshj', xb, w_x).reshape(B, S, C) + b_x)
    log_a = -LRU_C * r.astype(jnp.float32) * jax.nn.softplus(-lam.astype(jnp.float32))
    a = jnp.exp(log_a)
    mult = jnp.sqrt(-jnp.expm1(2.0 * log_a))
    u = mult * (i * x).astype(jnp.float32)

    def combine(left, right):
        a_l, b_l = left
        a_r, b_r = right
        return a_l * a_r, a_r * b_l + b_r

    _, h = lax.associative_scan(combine, (a, u), axis=1)
    return h.astype(x.dtype)


def dilated_attention(q, k, v, window, dilation):
    B, H, S, Dh = q.shape
    L = S // dilation
    blk = window // dilation
    n_blk = -(-L // blk)
    pad = n_blk * blk - L

    def to_blocks(t):
        t = t.reshape(B, H, L, dilation, Dh).transpose(0, 1, 3, 2, 4)
        t = jnp.pad(t, ((0, 0), (0, 0), (0, 0), (0, pad), (0, 0)))
        return t.reshape(B, H, dilation, n_blk, blk, Dh)

    def with_prev(t):
        prev = jnp.pad(t, ((0, 0), (0, 0), (0, 0), (1, 0), (0, 0), (0, 0)))[:, :, :, :-1]
        return jnp.concatenate([prev, t], axis=4)

    qb = to_blocks(q)
    kc = with_prev(to_blocks(k))
    vc = with_prev(to_blocks(v))
    scores = jnp.einsum('bhrnqd,bhrnkd->bhrnqk', qb, kc, preferred_element_type=jnp.float32)
    qi = jnp.arange(blk)[:, None]
    ki = jnp.arange(2 * blk)[None, :]
    dist = qi + blk - ki
    blk_idx = jnp.arange(n_blk)[:, None, None]
    valid = ((dist >= 0) & (dist <= blk))[None] & (blk_idx * blk + ki[None] - blk >= 0)
    scores = jnp.where(valid, scores, NEG_INF)
    m = jnp.max(scores, axis=-1, keepdims=True)
    p = jnp.exp(scores - m)
    denom = jnp.sum(p, axis=-1, keepdims=True)
    out = jnp.einsum('bhrnqk,bhrnkd->bhrnqd', p, vc.astype(jnp.float32)) / denom
    lse = (m + jnp.log(denom))[..., 0]

    def from_blocks(t):
        t = t.reshape(B, H, dilation, n_blk * blk, *t.shape[5:])[:, :, :, :L]
        t = jnp.moveaxis(t, 2, 3)
        return t.reshape(B, H, S, *t.shape[4:])

    return from_blocks(out), from_blocks(lse)


def _fwd_setup_inputs(seed: int = 0) -> dict:
    key = jax.random.key(seed)
    ks = jax.random.split(key, 20)
    D = D_MODEL
    nrm = lambda k, shape, fan_in: jax.random.normal(k, shape, jnp.float32) * fan_in ** -0.5
    x = jax.random.normal(ks[0], (BATCH, SEQ, D), jnp.float32)
    c = jax.random.normal(ks[1], (BATCH, D), jnp.float32)
    offsets = jax.random.randint(ks[2], (BATCH, 1), 0, 4096, dtype=jnp.int32)
    positions = offsets + jnp.arange(SEQ, dtype=jnp.int32)[None, :]
    w_ada = nrm(ks[3], (DEPTH, D, 3 * D), D) * 0.5
    b_ada = 0.02 * jax.random.normal(ks[4], (DEPTH, 3 * D), jnp.float32)
    norm_pre = 1.0 + 0.1 * jax.random.normal(ks[5], (DEPTH, D), jnp.float32)
    norm_post = 1.0 + 0.1 * jax.random.normal(ks[6], (DEPTH, D), jnp.float32)
    w_in = nrm(ks[7], (DEPTH, D, D_IN_PROJ), D)
    conv_w = nrm(ks[8], (DEPTH, CONV_WIDTH, D_REC), CONV_WIDTH)
    conv_b = 0.02 * jax.random.normal(ks[9], (DEPTH, D_REC), jnp.float32)
    w_rg_a = nrm(ks[10], (DEPTH, N_LRU_BLOCKS, LRU_BLOCK, LRU_BLOCK), LRU_BLOCK)
    b_rg_a = 0.02 * jax.random.normal(ks[11], (DEPTH, D_REC), jnp.float32)
    w_rg_x = nrm(ks[12], (DEPTH, N_LRU_BLOCKS, LRU_BLOCK, LRU_BLOCK), LRU_BLOCK)
    b_rg_x = 0.02 * jax.random.normal(ks[13], (DEPTH, D_REC), jnp.float32)
    a_c = jax.random.uniform(ks[14], (DEPTH, D_REC), jnp.float32, 0.9, 0.999)
    a_base = a_c ** (1.0 / LRU_C)
    lru_lambda = jnp.log(a_base) - jnp.log1p(-a_base)
    norm_rec = 1.0 + 0.1 * jax.random.normal(ks[15], (DEPTH, D_REC), jnp.float32)
    norm_att = 1.0 + 0.1 * jax.random.normal(ks[16], (DEPTH, D_ATT), jnp.float32)
    w_out = nrm(ks[17], (DEPTH, D_MIX, D), D_MIX)
    return {'x': x, 'c': c, 'positions': positions, 'w_ada': w_ada, 'b_ada': b_ada,
            'norm_pre': norm_pre, 'norm_post': norm_post, 'w_in': w_in, 'conv_w': conv_w,
            'conv_b': conv_b, 'w_rg_a': w_rg_a, 'b_rg_a': b_rg_a, 'w_rg_x': w_rg_x,
            'b_rg_x': b_rg_x, 'lru_lambda': lru_lambda, 'norm_rec': norm_rec,
            'norm_att': norm_att, 'w_out': w_out}


def _fwd_reference(x, c, positions, w_ada, b_ada, norm_pre, norm_post, w_in, conv_w, conv_b,
              w_rg_a, b_rg_a, w_rg_x, b_rg_x, lru_lambda, norm_rec, norm_att, w_out):
    B, S, _ = x.shape
    for l in range(DEPTH):
        mod = jnp.einsum('bd,de->be', jax.nn.silu(c), w_ada[l]) + b_ada[l]
        shift, scale, gate = jnp.split(mod, 3, axis=-1)
        h = rms_norm(x, norm_pre[l]) * (1.0 + scale[:, None, :]) + shift[:, None, :]

        proj = jnp.einsum('bsd,de->bse', h, w_in[l])
        xa, ga, q, k, v, gb = jnp.split(proj, SPLIT_IDX, axis=-1)

        xa = causal_depthwise_conv(xa, conv_w[l], conv_b[l])
        ya = rg_lru(xa, w_rg_a[l], b_rg_a[l], w_rg_x[l], b_rg_x[l], lru_lambda[l]) * jax.nn.silu(ga)
        ya = rms_norm(ya, norm_rec[l])

        heads = lambda t: t.reshape(B, S, N_HEADS, HEAD_DIM)
        qh = (rotary(heads(q), positions) * HEAD_DIM ** -0.5).transpose(0, 2, 1, 3)
        kh = rotary(heads(k), positions).transpose(0, 2, 1, 3)
        vh = heads(v).transpose(0, 2, 1, 3)
        outs, lses = zip(*[dilated_attention(qh, kh, vh, w, d) for (w, d) in DILATED_PATTERNS])
        wts = jax.nn.softmax(jnp.stack(lses, axis=0), axis=0)
        att = jnp.einsum('pbhs,pbhsd->bhsd', wts, jnp.stack(outs, axis=0))
        yb = att.transpose(0, 2, 1, 3).reshape(B, S, D_ATT).astype(x.dtype) * jax.nn.silu(gb)
        yb = rms_norm(yb, norm_att[l])

        mix = jnp.einsum('bse,ed->bsd', jnp.concatenate([ya, yb], axis=-1), w_out[l])
        x = x + gate[:, None, :] * rms_norm(mix, norm_post[l])
    return x


import jax as _jax
import jax.numpy as _jnp

TWIN_FORMAT = 'train_step'
FWD_PARAMS = ['x', 'c', 'positions', 'w_ada', 'b_ada', 'norm_pre', 'norm_post', 'w_in', 'conv_w', 'conv_b', 'w_rg_a', 'b_rg_a', 'w_rg_x', 'b_rg_x', 'lru_lambda', 'norm_rec', 'norm_att', 'w_out']
TWIN_WEIGHTS = ['w_ada', 'b_ada', 'norm_pre', 'norm_post', 'w_in', 'conv_w', 'conv_b', 'w_rg_a', 'b_rg_a', 'w_rg_x', 'b_rg_x', 'lru_lambda', 'norm_rec', 'norm_att', 'w_out']
TWIN_DIFF_INPUT = 'x'
TWIN_INPUTS = ['x', 'c', 'positions', 'w_ada', 'b_ada', 'norm_pre', 'norm_post', 'w_in', 'conv_w', 'conv_b', 'w_rg_a', 'b_rg_a', 'w_rg_x', 'b_rg_x', 'lru_lambda', 'norm_rec', 'norm_att', 'w_out', 'loss_target', 'm_w_ada', 'm_b_ada', 'm_norm_pre', 'm_norm_post', 'm_w_in', 'm_conv_w', 'm_conv_b', 'm_w_rg_a', 'm_b_rg_a', 'm_w_rg_x', 'm_b_rg_x', 'm_lru_lambda', 'm_norm_rec', 'm_norm_att', 'm_w_out', 'v_w_ada', 'v_b_ada', 'v_norm_pre', 'v_norm_post', 'v_w_in', 'v_conv_w', 'v_conv_b', 'v_w_rg_a', 'v_b_rg_a', 'v_w_rg_x', 'v_b_rg_x', 'v_lru_lambda', 'v_norm_rec', 'v_norm_att', 'v_w_out']
TWIN_OUTPUTS = ['loss', 'grad_x', 'grad_w_ada', 'grad_b_ada', 'grad_norm_pre', 'grad_norm_post', 'grad_w_in', 'grad_conv_w', 'grad_conv_b', 'grad_w_rg_a', 'grad_b_rg_a', 'grad_w_rg_x', 'grad_b_rg_x', 'grad_lru_lambda', 'grad_norm_rec', 'grad_norm_att', 'grad_w_out', 'delta_w_ada', 'delta_b_ada', 'delta_norm_pre', 'delta_norm_post', 'delta_w_in', 'delta_conv_w', 'delta_conv_b', 'delta_w_rg_a', 'delta_b_rg_a', 'delta_w_rg_x', 'delta_b_rg_x', 'delta_lru_lambda', 'delta_norm_rec', 'delta_norm_att', 'delta_w_out', 'new_m_w_ada', 'new_m_b_ada', 'new_m_norm_pre', 'new_m_norm_post', 'new_m_w_in', 'new_m_conv_w', 'new_m_conv_b', 'new_m_w_rg_a', 'new_m_b_rg_a', 'new_m_w_rg_x', 'new_m_b_rg_x', 'new_m_lru_lambda', 'new_m_norm_rec', 'new_m_norm_att', 'new_m_w_out', 'new_v_w_ada', 'new_v_b_ada', 'new_v_norm_pre', 'new_v_norm_post', 'new_v_w_in', 'new_v_conv_w', 'new_v_conv_b', 'new_v_w_rg_a', 'new_v_b_rg_a', 'new_v_w_rg_x', 'new_v_b_rg_x', 'new_v_lru_lambda', 'new_v_norm_rec', 'new_v_norm_att', 'new_v_w_out']
TWIN_LEAF_KINDS = {'loss': 'loss', 'grad_x': 'grad_x', 'grad_w_ada': 'grad_w', 'grad_b_ada': 'grad_w', 'grad_norm_pre': 'grad_w', 'grad_norm_post': 'grad_w', 'grad_w_in': 'grad_w', 'grad_conv_w': 'grad_w', 'grad_conv_b': 'grad_w', 'grad_w_rg_a': 'grad_w', 'grad_b_rg_a': 'grad_w', 'grad_w_rg_x': 'grad_w', 'grad_b_rg_x': 'grad_w', 'grad_lru_lambda': 'grad_w', 'grad_norm_rec': 'grad_w', 'grad_norm_att': 'grad_w', 'grad_w_out': 'grad_w', 'delta_w_ada': 'delta_w', 'delta_b_ada': 'delta_w', 'delta_norm_pre': 'delta_w', 'delta_norm_post': 'delta_w', 'delta_w_in': 'delta_w', 'delta_conv_w': 'delta_w', 'delta_conv_b': 'delta_w', 'delta_w_rg_a': 'delta_w', 'delta_b_rg_a': 'delta_w', 'delta_w_rg_x': 'delta_w', 'delta_b_rg_x': 'delta_w', 'delta_lru_lambda': 'delta_w', 'delta_norm_rec': 'delta_w', 'delta_norm_att': 'delta_w', 'delta_w_out': 'delta_w', 'new_m_w_ada': 'new_m', 'new_m_b_ada': 'new_m', 'new_m_norm_pre': 'new_m', 'new_m_norm_post': 'new_m', 'new_m_w_in': 'new_m', 'new_m_conv_w': 'new_m', 'new_m_conv_b': 'new_m', 'new_m_w_rg_a': 'new_m', 'new_m_b_rg_a': 'new_m', 'new_m_w_rg_x': 'new_m', 'new_m_b_rg_x': 'new_m', 'new_m_lru_lambda': 'new_m', 'new_m_norm_rec': 'new_m', 'new_m_norm_att': 'new_m', 'new_m_w_out': 'new_m', 'new_v_w_ada': 'new_v', 'new_v_b_ada': 'new_v', 'new_v_norm_pre': 'new_v', 'new_v_norm_post': 'new_v', 'new_v_w_in': 'new_v', 'new_v_conv_w': 'new_v', 'new_v_conv_b': 'new_v', 'new_v_w_rg_a': 'new_v', 'new_v_b_rg_a': 'new_v', 'new_v_w_rg_x': 'new_v', 'new_v_b_rg_x': 'new_v', 'new_v_lru_lambda': 'new_v', 'new_v_norm_rec': 'new_v', 'new_v_norm_att': 'new_v', 'new_v_w_out': 'new_v'}


def _forward(args):
    return _fwd_reference(*[args[k] for k in FWD_PARAMS])


def _output_shape():
    out = _jax.eval_shape(lambda: _forward(_fwd_setup_inputs(0)))
    return out.shape, out.dtype

N_MICROBATCH = 1
ADAM_LR = 0.001
ADAM_B1 = 0.9
ADAM_B2 = 0.999
ADAM_EPS = 1e-08
ADAM_WD = 0.01
ADAM_STEP = 10
PER_EXAMPLE_BATCH_AXIS = {'x': 0, 'c': 0, 'positions': 0, 'loss_target': 0}
SHARED_INPUTS = []
_WEIGHT_DTYPES = {'w_ada': _jnp.float32, 'b_ada': _jnp.float32, 'norm_pre': _jnp.float32, 'norm_post': _jnp.float32, 'w_in': _jnp.float32, 'conv_w': _jnp.float32, 'conv_b': _jnp.float32, 'w_rg_a': _jnp.float32, 'b_rg_a': _jnp.float32, 'w_rg_x': _jnp.float32, 'b_rg_x': _jnp.float32, 'lru_lambda': _jnp.float32, 'norm_rec': _jnp.float32, 'norm_att': _jnp.float32, 'w_out': _jnp.float32}
MOMENT_SCALE = {'w_ada': 9.512075e-01, 'b_ada': 1.774040e+00, 'norm_pre': 8.026119e-02, 'norm_post': 1.917810e+00, 'w_in': 1.039674e-01, 'conv_w': 1.622226e-01, 'conv_b': 6.019703e-01, 'w_rg_a': 1.897138e-02, 'b_rg_a': 3.768006e-02, 'w_rg_x': 4.000720e-02, 'b_rg_x': 7.759593e-02, 'lru_lambda': 8.506982e-02, 'norm_rec': 2.146717e-01, 'norm_att': 1.444855e-01, 'w_out': 1.730503e-01}


def _to_microbatches(a, axis):
    t = _jnp.moveaxis(a, axis, 0)
    t = t.reshape((N_MICROBATCH, t.shape[0] // N_MICROBATCH) + t.shape[1:])
    return _jnp.moveaxis(t, 1, axis + 1)


def setup_inputs(seed: int = 0) -> dict:
    inp = _fwd_setup_inputs(seed)
    key = _jax.random.fold_in(_jax.random.key(seed), 7919)
    shape, _ = _output_shape()
    out = dict(inp)
    out["loss_target"] = _jax.random.normal(_jax.random.fold_in(key, 0), shape, _jnp.float32)
    for i, name in enumerate(TWIN_WEIGHTS):
        w = inp[name].astype(_jnp.float32)
        if MOMENT_SCALE is None:
            s = _jnp.sqrt(_jnp.mean(_jnp.square(w)) + 1e-30)
        else:
            s = MOMENT_SCALE[name]
        km, kv = _jax.random.split(_jax.random.fold_in(key, i + 1))
        out[name] = w
        out["m_" + name] = s * _jax.random.normal(km, w.shape, _jnp.float32)
        out["v_" + name] = (s * s) * _jax.random.uniform(kv, w.shape, _jnp.float32, 0.5, 1.5)
    if N_MICROBATCH > 1:
        for name, axis in PER_EXAMPLE_BATCH_AXIS.items():
            out[name] = _to_microbatches(out[name], axis)
    return {'x': out['x'], 'c': out['c'], 'positions': out['positions'], 'w_ada': out['w_ada'], 'b_ada': out['b_ada'], 'norm_pre': out['norm_pre'], 'norm_post': out['norm_post'], 'w_in': out['w_in'], 'conv_w': out['conv_w'], 'conv_b': out['conv_b'], 'w_rg_a': out['w_rg_a'], 'b_rg_a': out['b_rg_a'], 'w_rg_x': out['w_rg_x'], 'b_rg_x': out['b_rg_x'], 'lru_lambda': out['lru_lambda'], 'norm_rec': out['norm_rec'], 'norm_att': out['norm_att'], 'w_out': out['w_out'], 'loss_target': out['loss_target'], 'm_w_ada': out['m_w_ada'], 'm_b_ada': out['m_b_ada'], 'm_norm_pre': out['m_norm_pre'], 'm_norm_post': out['m_norm_post'], 'm_w_in': out['m_w_in'], 'm_conv_w': out['m_conv_w'], 'm_conv_b': out['m_conv_b'], 'm_w_rg_a': out['m_w_rg_a'], 'm_b_rg_a': out['m_b_rg_a'], 'm_w_rg_x': out['m_w_rg_x'], 'm_b_rg_x': out['m_b_rg_x'], 'm_lru_lambda': out['m_lru_lambda'], 'm_norm_rec': out['m_norm_rec'], 'm_norm_att': out['m_norm_att'], 'm_w_out': out['m_w_out'], 'v_w_ada': out['v_w_ada'], 'v_b_ada': out['v_b_ada'], 'v_norm_pre': out['v_norm_pre'], 'v_norm_post': out['v_norm_post'], 'v_w_in': out['v_w_in'], 'v_conv_w': out['v_conv_w'], 'v_conv_b': out['v_conv_b'], 'v_w_rg_a': out['v_w_rg_a'], 'v_b_rg_a': out['v_b_rg_a'], 'v_w_rg_x': out['v_w_rg_x'], 'v_b_rg_x': out['v_b_rg_x'], 'v_lru_lambda': out['v_lru_lambda'], 'v_norm_rec': out['v_norm_rec'], 'v_norm_att': out['v_norm_att'], 'v_w_out': out['v_w_out']}


def _loss(weights, diff, rest, loss_target):
    with _jax.named_scope("forward"):
        args = {**rest, TWIN_DIFF_INPUT: diff, **{k: w.astype(_WEIGHT_DTYPES[k]) for k, w in weights.items()}}
        y = _forward(args)
    with _jax.named_scope("loss_head"):
        err = _jnp.square(y.astype(_jnp.float32) - loss_target)
        return 0.5 * _jnp.sum(_jnp.mean(err, axis=-1)) if err.ndim else 0.5 * err


def _adamw(w, g, m, v):
    m = ADAM_B1 * m + (1.0 - ADAM_B1) * g
    v = ADAM_B2 * v + (1.0 - ADAM_B2) * _jnp.square(g)
    m_hat = m / (1.0 - ADAM_B1 ** ADAM_STEP)
    v_hat = v / (1.0 - ADAM_B2 ** ADAM_STEP)
    delta = -ADAM_LR * (m_hat / (_jnp.sqrt(v_hat) + ADAM_EPS) + ADAM_WD * w)
    return delta, m, v


def reference(x, c, positions, w_ada, b_ada, norm_pre, norm_post, w_in, conv_w, conv_b, w_rg_a, b_rg_a, w_rg_x, b_rg_x, lru_lambda, norm_rec, norm_att, w_out, loss_target, m_w_ada, m_b_ada, m_norm_pre, m_norm_post, m_w_in, m_conv_w, m_conv_b, m_w_rg_a, m_b_rg_a, m_w_rg_x, m_b_rg_x, m_lru_lambda, m_norm_rec, m_norm_att, m_w_out, v_w_ada, v_b_ada, v_norm_pre, v_norm_post, v_w_in, v_conv_w, v_conv_b, v_w_rg_a, v_b_rg_a, v_w_rg_x, v_b_rg_x, v_lru_lambda, v_norm_rec, v_norm_att, v_w_out):
    given = dict(x=x, c=c, positions=positions, w_ada=w_ada, b_ada=b_ada, norm_pre=norm_pre, norm_post=norm_post, w_in=w_in, conv_w=conv_w, conv_b=conv_b, w_rg_a=w_rg_a, b_rg_a=b_rg_a, w_rg_x=w_rg_x, b_rg_x=b_rg_x, lru_lambda=lru_lambda, norm_rec=norm_rec, norm_att=norm_att, w_out=w_out, loss_target=loss_target, m_w_ada=m_w_ada, m_b_ada=m_b_ada, m_norm_pre=m_norm_pre, m_norm_post=m_norm_post, m_w_in=m_w_in, m_conv_w=m_conv_w, m_conv_b=m_conv_b, m_w_rg_a=m_w_rg_a, m_b_rg_a=m_b_rg_a, m_w_rg_x=m_w_rg_x, m_b_rg_x=m_b_rg_x, m_lru_lambda=m_lru_lambda, m_norm_rec=m_norm_rec, m_norm_att=m_norm_att, m_w_out=m_w_out, v_w_ada=v_w_ada, v_b_ada=v_b_ada, v_norm_pre=v_norm_pre, v_norm_post=v_norm_post, v_w_in=v_w_in, v_conv_w=v_conv_w, v_conv_b=v_conv_b, v_w_rg_a=v_w_rg_a, v_b_rg_a=v_b_rg_a, v_w_rg_x=v_w_rg_x, v_b_rg_x=v_b_rg_x, v_lru_lambda=v_lru_lambda, v_norm_rec=v_norm_rec, v_norm_att=v_norm_att, v_w_out=v_w_out)
    weights = {n: given[n] for n in TWIN_WEIGHTS}
    shared = {n: given[n] for n in SHARED_INPUTS}
    per_example = {n: given[n] for n in ['x', 'c', 'positions']}
    grad_fn = _jax.value_and_grad(_loss, argnums=(0, 1))

    def one_microbatch(ex, loss_target):
        ex = dict(ex)
        diff = ex.pop(TWIN_DIFF_INPUT)
        return grad_fn(weights, diff, {**shared, **ex}, loss_target)

    if N_MICROBATCH == 1:
        loss, (grad_w, grad_x) = one_microbatch(per_example, given["loss_target"])
    else:
        def body(carry, xs):
            loss_sum, grad_sum = carry
            l_k, (gw_k, gx_k) = one_microbatch(xs[0], xs[1])
            with _jax.named_scope("update"):
                return (loss_sum + l_k, _jax.tree.map(_jnp.add, grad_sum, gw_k)), gx_k

        init = (_jnp.zeros((), _jnp.float32), _jax.tree.map(_jnp.zeros_like, weights))
        (loss, grad_w), grad_x = _jax.lax.scan(body, init, (per_example, given["loss_target"]))
    with _jax.named_scope("update"):
        delta_w, new_m, new_v = {}, {}, {}
        for n in TWIN_WEIGHTS:
            delta_w[n], new_m[n], new_v[n] = _adamw(weights[n], grad_w[n], given["m_" + n], given["v_" + n])
    return (loss, grad_x, *[grad_w[n] for n in TWIN_WEIGHTS], *[delta_w[n] for n in TWIN_WEIGHTS],
            *[new_m[n] for n in TWIN_WEIGHTS], *[new_v[n] for n in TWIN_WEIGHTS])
```

```python
import functools

import numpy as np
import jax
import jax.numpy as jnp
from jax import lax
from jax.experimental import pallas as pl
from jax.experimental.pallas import tpu as pltpu

F32 = jnp.float32
BF16 = jnp.bfloat16

S = 2048
D = 1024
E = 3072
R = 512
NDEV = 8
NCHIP = 4
EC = 768
LRU_C = 8.0
EPS = 1e-6
NEG = -1e30
HEAD = 64
BLK = 128
PATTERNS = (1, 4, 16)
ROPE_THETA = 10000.0
LANES = 128
VMEM_LIMIT = 56 * 1024 * 1024

B1, B2, LR, WD, ADAM_EPS, STEP = 0.9, 0.999, 0.001, 0.01, 1e-8, 10
MESH = pl.DeviceIdType.MESH


def _cp(sem=None, **kw):
    return pltpu.CompilerParams(dimension_semantics=sem, vmem_limit_bytes=VMEM_LIMIT, **kw)


def _dot(a, b):
    return jnp.dot(a, b, preferred_element_type=F32)


def _dot_nt(a, b):
    return lax.dot_general(a, b, (((1,), (1,)), ((), ())), preferred_element_type=F32)


def _dot_tn(a, b):
    return lax.dot_general(a, b, (((0,), (0,)), ((), ())), preferred_element_type=F32)


def _sigmoid(x):
    return 1.0 / (1.0 + jnp.exp(-x))


def _expm1(x):
    poly = x * (1.0 + x * (0.5 + x * (1.0 / 6 + x * (1.0 / 24 + x * (1.0 / 120 + x * (1.0 / 720))))))
    return jnp.where(jnp.abs(x) < 0.3, poly, jnp.exp(x) - 1.0)


def _rms_fwd(v, g):
    rstd = lax.rsqrt(jnp.mean(v * v, axis=-1, keepdims=True) + EPS)
    vn = v * rstd
    return vn * g, vn, rstd


def _rms_bwd(dy, vn, rstd, g):
    dvn = dy * g
    dv = rstd * (dvn - vn * jnp.mean(dvn * vn, axis=-1, keepdims=True))
    return dv, jnp.sum(dy * vn, axis=0, keepdims=True)


def _in_proj_fwd(x, mod, norm_pre, w_in_bf):
    ts = 256

    def body(x_ref, mod_ref, np_ref, w_ref, proj_ref, hb_ref):
        hp, _, _ = _rms_fwd(x_ref[...], np_ref[...])
        h = hp * (1.0 + mod_ref[:, D:2 * D]) + mod_ref[:, 0:D]
        hb = h.astype(BF16)
        hb_ref[...] = hb
        for j in range(NCHIP):
            proj_ref[:, j * EC:(j + 1) * EC] = _dot(hb, w_ref[j])

    return pl.pallas_call(
        body, name="in_proj_fwd", grid=(S // ts,),
        in_specs=[pl.BlockSpec((ts, D), lambda i: (i, 0)), pl.BlockSpec((1, 3 * D), lambda i: (0, 0)),
                  pl.BlockSpec((1, D), lambda i: (0, 0)), pl.BlockSpec((NCHIP, D, EC), lambda i: (0, 0, 0))],
        out_specs=[pl.BlockSpec((ts, E), lambda i: (i, 0)), pl.BlockSpec((ts, D), lambda i: (i, 0))],
        out_shape=[jax.ShapeDtypeStruct((S, E), F32), jax.ShapeDtypeStruct((S, D), BF16)],
        compiler_params=_cp(("parallel",)),
    )(x, mod, norm_pre, w_in_bf)


RT = 256


def _shift_down(cur, prev8, j, row):
    if j == 0:
        return cur
    top = jnp.tile(pltpu.roll(prev8, j, 0), (RT // 8, 1))
    return jnp.where(row >= j, pltpu.roll(cur, j, 0), top)


def _shift_up(cur, next8, j, row):
    if j == 0:
        return cur
    bot = jnp.tile(pltpu.roll(next8, 8 - j, 0), (RT // 8, 1))
    return jnp.where(row < RT - j, pltpu.roll(cur, RT - j, 0), bot)


def _rec_gates(xp, xprev8, row, cw_ref, cb_ref, wa_ref, ba_ref, wx_ref, bx_ref, lam_ref):
    xa = cb_ref[...] + sum(cw_ref[3 - j:4 - j, :] * _shift_down(xp, xprev8, j, row) for j in range(4))
    xab = xa.astype(BF16)
    r = _sigmoid(_dot(xab, wa_ref[...]) + ba_ref[...])
    ig = _sigmoid(_dot(xab, wx_ref[...]) + bx_ref[...])
    nl = -lam_ref[...]
    sp = jnp.maximum(nl, 0.0) + jnp.log1p(jnp.exp(-jnp.abs(nl)))
    la = (-LRU_C) * r * sp
    a = jnp.exp(la)
    mult = jnp.sqrt(-_expm1(2.0 * la))
    return dict(xa=xa, xab=xab, r=r, ig=ig, sp=sp, la=la, a=a, mult=mult)


def _scan_fwd(a, u, row):
    sh = 1
    while sh < RT:
        a_s = jnp.where(row >= sh, pltpu.roll(a, sh, 0), 1.0)
        u_s = jnp.where(row >= sh, pltpu.roll(u, sh, 0), 0.0)
        u = a * u_s + u
        a = a * a_s
        sh *= 2
    return a, u


def _scan_bwd(al, g, row):
    sh = 1
    while sh < RT:
        al_s = jnp.where(row < RT - sh, pltpu.roll(al, RT - sh, 0), 1.0)
        g_s = jnp.where(row < RT - sh, pltpu.roll(g, RT - sh, 0), 0.0)
        g = g + al * g_s
        al = al * al_s
        sh *= 2
    return g


def _rec_fwd(proj, conv_w, conv_b, wa_d, ba, wx_d, bx, lam, norm_rec):
    nt = S // RT

    def body(p_ref, cw_ref, cb_ref, wa_ref, ba_ref, wx_ref, bx_ref, lam_ref, nr_ref,
             h_ref, ya_ref, prev8, hc):
        i = pl.program_id(0)

        @pl.when(i == 0)
        def _():
            prev8[...] = jnp.zeros_like(prev8)
            hc[...] = jnp.zeros_like(hc)

        row = lax.broadcasted_iota(jnp.int32, (RT, R), 0)
        xp = p_ref[:, 0:R]
        ga = p_ref[:, R:2 * R]
        f = _rec_gates(xp, prev8[...], row, cw_ref, cb_ref, wa_ref, ba_ref, wx_ref, bx_ref, lam_ref)
        u = f["mult"] * (f["ig"] * f["xa"])
        acum, hh = _scan_fwd(f["a"], u, row)
        h = hh + acum * hc[0:1, :]
        h_ref[...] = h
        hc[0:1, :] = h_ref[RT - 1:RT, :]
        prev8[...] = p_ref[RT - 8:RT, 0:R]
        yp = h * (ga * _sigmoid(ga))
        ya, _, _ = _rms_fwd(yp, nr_ref[...])
        ya_ref[...] = ya.astype(BF16)

    row1 = lambda n: pl.BlockSpec((1, n), lambda i: (0, 0))
    return pl.pallas_call(
        body, name="rec_fwd", grid=(nt,),
        in_specs=[pl.BlockSpec((RT, 2 * R), lambda i: (i, 0)), pl.BlockSpec((4, R), lambda i: (0, 0)), row1(R),
                  pl.BlockSpec((R, R), lambda i: (0, 0)), row1(R), pl.BlockSpec((R, R), lambda i: (0, 0)), row1(R),
                  row1(R), row1(R)],
        out_specs=[pl.BlockSpec((RT, R), lambda i: (i, 0)), pl.BlockSpec((RT, R), lambda i: (i, 0))],
        out_shape=[jax.ShapeDtypeStruct((S, R), F32), jax.ShapeDtypeStruct((S, R), BF16)],
        scratch_shapes=[pltpu.VMEM((8, R), F32), pltpu.VMEM((8, R), F32)],
        compiler_params=_cp(("arbitrary",)),
    )(proj, conv_w, conv_b, wa_d, ba, wx_d, bx, lam, norm_rec)


def _rec_bwd(dproj, d_ya, proj, h_all, conv_w, conv_b, wa_d, ba, wx_d, bx, lam, norm_rec):
    nt = S // RT

    def body(dp_in, dya_ref, p_ref, pprev_ref, h_ref, hprev_ref, cw_ref, cb_ref, wa_ref, ba_ref, wx_ref, bx_ref,
             lam_ref, nr_ref, dp_ref, dwa_ref, dwx_ref, sm_ref, nxt8, cg):
        i = pl.program_id(0)
        ti = nt - 1 - i

        @pl.when(i == 0)
        def _():
            nxt8[...] = jnp.zeros_like(nxt8)
            cg[...] = jnp.zeros_like(cg)
            dwa_ref[...] = jnp.zeros_like(dwa_ref)
            dwx_ref[...] = jnp.zeros_like(dwx_ref)
            sm_ref[...] = jnp.zeros_like(sm_ref)

        row = lax.broadcasted_iota(jnp.int32, (RT, R), 0)
        first = (ti > 0).astype(F32)
        xprev8 = pprev_ref[...] * first
        hprev8 = hprev_ref[...] * first
        xp = p_ref[:, 0:R]
        ga = p_ref[:, R:2 * R]
        f = _rec_gates(xp, xprev8, row, cw_ref, cb_ref, wa_ref, ba_ref, wx_ref, bx_ref, lam_ref)
        xa, r, ig, a, mult = f["xa"], f["r"], f["ig"], f["a"], f["mult"]
        h = h_ref[...]
        sg = _sigmoid(ga)
        gate = ga * sg
        yp = h * gate
        _, ypn, rstd = _rms_fwd(yp, nr_ref[...])
        d_yp, dnr = _rms_bwd(dya_ref[...], ypn, rstd, nr_ref[...])
        d_ga = d_yp * h * (sg * (1.0 + ga * (1.0 - sg)))
        dh = d_yp * gate + jnp.where(row == RT - 1, cg[0:1, :], 0.0)
        al = jnp.where(row < RT - 1, pltpu.roll(a, RT - 1, 0), 0.0)
        g = _scan_bwd(al, dh, row)
        cg[0:1, :] = jnp.sum(jnp.where(row == 0, a * g, 0.0), axis=0, keepdims=True)
        h_m1 = _shift_down(h, hprev8, 1, row)
        da = g * h_m1
        ix = ig * xa
        d_mult = g * ix
        d_ig = g * mult * xa
        d_xa = g * mult * ig
        d_la = da * a - d_mult * (a * a) / mult
        d_r = d_la * ((-LRU_C) * f["sp"])
        dsp = jnp.sum(d_la * ((-LRU_C) * r), axis=0, keepdims=True)
        dlam = dsp * (-_sigmoid(-lam_ref[...]))
        d_za = d_r * r * (1.0 - r)
        d_zx = d_ig * ig * (1.0 - ig)
        dzab = d_za.astype(BF16)
        dzxb = d_zx.astype(BF16)
        dwa_ref[...] += _dot_tn(f["xab"], dzab)
        dwx_ref[...] += _dot_tn(f["xab"], dzxb)
        d_xa = d_xa + _dot_nt(dzab, wa_ref[...]) + _dot_nt(dzxb, wx_ref[...])
        d_xp = sum(cw_ref[3 - j:4 - j, :] * _shift_up(d_xa, nxt8[...], j, row) for j in range(4))
        dcw = [jnp.sum(d_xa * _shift_down(xp, xprev8, 3 - k, row), axis=0, keepdims=True) for k in range(4)]
        dp_ref[:, 0:R] = d_xp.astype(BF16)
        dp_ref[:, R:2 * R] = d_ga.astype(BF16)
        dp8 = d_xa[0:8, :]
        nxt8[...] = dp8
        sm_ref[0:1, :] += jnp.sum(d_za, axis=0, keepdims=True)
        sm_ref[1:2, :] += jnp.sum(d_zx, axis=0, keepdims=True)
        sm_ref[2:3, :] += dlam
        sm_ref[3:4, :] += dnr
        sm_ref[4:5, :] += jnp.sum(d_xa, axis=0, keepdims=True)
        for k in range(4):
            sm_ref[8 + k:9 + k, :] += dcw[k]

    c0 = lambda shape: pl.BlockSpec(shape, lambda i: (0, 0))
    rev = lambda i: nt - 1 - i
    prev8 = lambda i: (jnp.maximum((nt - 1 - i) * (RT // 8) - 1, 0), 0)
    return pl.pallas_call(
        body, name="rec_bwd", grid=(nt,),
        in_specs=[pl.BlockSpec(memory_space=pl.ANY),
                  pl.BlockSpec((RT, R), lambda i: (rev(i), 0)),
                  pl.BlockSpec((RT, 2 * R), lambda i: (rev(i), 0)), pl.BlockSpec((8, R), prev8),
                  pl.BlockSpec((RT, R), lambda i: (rev(i), 0)), pl.BlockSpec((8, R), prev8),
                  c0((4, R)), c0((1, R)), c0((R, R)), c0((1, R)), c0((R, R)), c0((1, R)), c0((1, R)), c0((1, R))],
        out_specs=[pl.BlockSpec((RT, 2 * R), lambda i: (rev(i), 0)), c0((R, R)), c0((R, R)), c0((16, R))],
        out_shape=[jax.ShapeDtypeStruct((S, E), BF16), jax.ShapeDtypeStruct((R, R), F32),
                   jax.ShapeDtypeStruct((R, R), F32), jax.ShapeDtypeStruct((16, R), F32)],
        scratch_shapes=[pltpu.VMEM((8, R), F32), pltpu.VMEM((8, R), F32)],
        input_output_aliases={0: 0},
        compiler_params=_cp(("arbitrary",)),
    )(dproj, d_ya, proj, proj, h_all, h_all, conv_w, conv_b, wa_d, ba, wx_d, bx, lam, norm_rec)


NPAIR = R // LANES
QB, KB, VB, GB = 2 * R // LANES, 3 * R // LANES, 4 * R // LANES, 5 * R // LANES


def _rope_freq():
    half = HEAD // 2
    inv = np.float32(ROPE_THETA) ** (-(np.arange(half, dtype=np.float32) / np.float32(half)))
    return jnp.asarray(np.tile(inv.astype(np.float32), LANES // half)[None, :])


def _rot_half(x, first):
    return jnp.where(first, -pltpu.roll(x, LANES - HEAD // 2, 1), pltpu.roll(x, HEAD // 2, 1))


def _cos_sin(pos_ref, freq_ref):
    ang = pos_ref[...].astype(F32) * freq_ref[...]
    return jnp.cos(ang), jnp.sin(ang)


def _deint(src_ref, dst_ref, d):
    n = S // d
    for r in range(d):
        v = src_ref[pl.ds(r, n, stride=d), :] if d > 1 else src_ref[...]
        dst_ref[r * n:(r + 1) * n, :] = v.astype(dst_ref.dtype)


def _reint(src_ref, dst_ref, d, accumulate):
    n = S // d
    for r in range(d):
        idx = (pl.ds(r, n, stride=d), slice(None)) if d > 1 else (slice(None), slice(None))
        v = src_ref[r * n:(r + 1) * n, :]
        if accumulate:
            dst_ref[idx] = dst_ref[idx] + v
        else:
            dst_ref[idx] = v


def _blk_masks(b, nb):
    qi = lax.broadcasted_iota(jnp.int32, (BLK, BLK), 0)
    ki = lax.broadcasted_iota(jnp.int32, (BLK, BLK), 1)
    has_prev = lax.rem(b, nb) != 0
    return ki <= qi, jnp.logical_and(ki >= qi, has_prev)


def _att_fwd(proj, pos, freq):
    def body(q_ref, k_ref, v_ref, pos_ref, freq_ref, att_ref, qr_ref, kr_ref, lse_ref,
             qd, kd, vd, od, ld, on, ln):
        lane = lax.broadcasted_iota(jnp.int32, (S, LANES), 1)
        first = (lane & (HEAD // 2)) == 0
        cos, sin = _cos_sin(pos_ref, freq_ref)
        q = q_ref[...]
        k = k_ref[...]
        qr_ref[...] = (q * cos + _rot_half(q, first) * sin) * (HEAD ** -0.5)
        kr_ref[...] = k * cos + _rot_half(k, first) * sin
        hm0 = lax.broadcasted_iota(jnp.int32, (BLK, LANES), 1) < HEAD

        for pi, d in enumerate(PATTERNS):
            nb = S // d // BLK
            _deint(qr_ref, qd, d)
            _deint(kr_ref, kd, d)
            _deint(v_ref, vd, d)

            def blk(b, carry):
                st = pl.multiple_of(b * BLK, BLK)
                stp = pl.multiple_of(jnp.maximum(b - 1, 0) * BLK, BLK)
                mc, mp = _blk_masks(b, nb)
                qb = qd[pl.ds(st, BLK), :]
                kc, kp = kd[pl.ds(st, BLK), :], kd[pl.ds(stp, BLK), :]
                vc, vp = vd[pl.ds(st, BLK), :], vd[pl.ds(stp, BLK), :]
                outs, lses = [], []
                for hm in (hm0, jnp.logical_not(hm0)):
                    qm = jnp.where(hm, qb, jnp.zeros_like(qb))
                    sc = jnp.where(mc, _dot_nt(qm, kc), NEG)
                    sp = jnp.where(mp, _dot_nt(qm, kp), NEG)
                    m = jnp.maximum(jnp.max(sc, axis=1, keepdims=True), jnp.max(sp, axis=1, keepdims=True))
                    pc, pp = jnp.exp(sc - m), jnp.exp(sp - m)
                    l = jnp.sum(pc, axis=1, keepdims=True) + jnp.sum(pp, axis=1, keepdims=True)
                    o = _dot(pc.astype(BF16), vc) + _dot(pp.astype(BF16), vp)
                    outs.append(o / l)
                    lses.append(m + jnp.log(l))
                od[pl.ds(st, BLK), :] = jnp.where(hm0, outs[0], outs[1])
                ld[pl.ds(st, BLK), :] = jnp.where(hm0, lses[0], lses[1])
                return carry

            lax.fori_loop(0, S // BLK, blk, 0)
            _reint(od, on.at[pi], d, False)
            _reint(ld, ln.at[pi], d, False)

        l0, l1, l2 = ln[0], ln[1], ln[2]
        m = jnp.maximum(jnp.maximum(l0, l1), l2)
        e0, e1, e2 = jnp.exp(l0 - m), jnp.exp(l1 - m), jnp.exp(l2 - m)
        den = e0 + e1 + e2
        att_ref[...] = (e0 * on[0] + e1 * on[1] + e2 * on[2]) / den
        lse_ref[...] = m + jnp.log(den)

    col = lambda c0: pl.BlockSpec((S, LANES), lambda p: (0, c0 + p))
    out = pl.BlockSpec((S, LANES), lambda p: (0, p))
    return pl.pallas_call(
        body, name="att_fwd", grid=(NPAIR,),
        in_specs=[col(QB), col(KB), col(VB), pl.BlockSpec((S, 1), lambda p: (0, 0)),
                  pl.BlockSpec((1, LANES), lambda p: (0, 0))],
        out_specs=[out, out, out, out],
        out_shape=[jax.ShapeDtypeStruct((S, R), F32)] * 4,
        scratch_shapes=[pltpu.VMEM((S, LANES), BF16)] * 3 + [pltpu.VMEM((S, LANES), F32)] * 2
        + [pltpu.VMEM((3, S, LANES), F32)] * 2,
        compiler_params=_cp(("parallel",)),
    )(proj, proj, proj, pos, freq)


def _att_bwd(dproj, d_att, att, lse, qr, kr, proj, pos, freq):
    def body(dp_in, do_ref, o_ref, lse_ref, qr_ref, kr_ref, v_ref, pos_ref, freq_ref, dp_ref,
             qd, kd, vd, dod, lsd, prd, dqd, dkd, dvd, dqn, dkn, dvn, prn, stage, sems):
        p = pl.program_id(0)
        prn[...] = do_ref[...] * o_ref[...]
        dqn[...] = jnp.zeros_like(dqn)
        dkn[...] = jnp.zeros_like(dkn)
        dvn[...] = jnp.zeros_like(dvn)
        hm0 = lax.broadcasted_iota(jnp.int32, (BLK, LANES), 1) < HEAD

        for d in PATTERNS:
            nb = S // d // BLK
            _deint(qr_ref, qd, d)
            _deint(kr_ref, kd, d)
            _deint(v_ref, vd, d)
            _deint(do_ref, dod, d)
            _deint(lse_ref, lsd, d)
            _deint(prn, prd, d)
            dkd[...] = jnp.zeros_like(dkd)
            dvd[...] = jnp.zeros_like(dvd)

            def blk(b, carry):
                st = pl.multiple_of(b * BLK, BLK)
                stp = pl.multiple_of(jnp.maximum(b - 1, 0) * BLK, BLK)
                mc, mp = _blk_masks(b, nb)
                qb, dob = qd[pl.ds(st, BLK), :], dod[pl.ds(st, BLK), :]
                kc, kp = kd[pl.ds(st, BLK), :], kd[pl.ds(stp, BLK), :]
                vc, vp = vd[pl.ds(st, BLK), :], vd[pl.ds(stp, BLK), :]
                lsb, prb = lsd[pl.ds(st, BLK), :], prd[pl.ds(st, BLK), :]
                dqs = []
                dkc = dkp = dvc = dvp = None
                for hm in (hm0, jnp.logical_not(hm0)):
                    qm = jnp.where(hm, qb, jnp.zeros_like(qb))
                    dom = jnp.where(hm, dob, jnp.zeros_like(dob))
                    lh = jnp.max(jnp.where(hm, lsb, -3e38), axis=1, keepdims=True)
                    delta = jnp.sum(jnp.where(hm, prb, 0.0), axis=1, keepdims=True)
                    pc = jnp.where(mc, jnp.exp(_dot_nt(qm, kc) - lh), 0.0)
                    pp = jnp.where(mp, jnp.exp(_dot_nt(qm, kp) - lh), 0.0)
                    dsc = (pc * (_dot_nt(dom, vc) - delta)).astype(BF16)
                    dsp = (pp * (_dot_nt(dom, vp) - delta)).astype(BF16)
                    dqs.append(_dot(dsc, kc) + _dot(dsp, kp))
                    acc = lambda t, n: n if t is None else t + n
                    dkc, dkp = acc(dkc, _dot_tn(dsc, qm)), acc(dkp, _dot_tn(dsp, qm))
                    dvc, dvp = acc(dvc, _dot_tn(pc.astype(BF16), dom)), acc(dvp, _dot_tn(pp.astype(BF16), dom))
                dqd[pl.ds(st, BLK), :] = jnp.where(hm0, dqs[0], dqs[1])
                dkd[pl.ds(stp, BLK), :] += dkp
                dvd[pl.ds(stp, BLK), :] += dvp
                dkd[pl.ds(st, BLK), :] += dkc
                dvd[pl.ds(st, BLK), :] += dvc
                return carry

            lax.fori_loop(0, S // BLK, blk, 0)
            _reint(dqd, dqn, d, True)
            _reint(dkd, dkn, d, True)
            _reint(dvd, dvn, d, True)

        lane = lax.broadcasted_iota(jnp.int32, (S, LANES), 1)
        first = (lane & (HEAD // 2)) == 0
        cos, sin = _cos_sin(pos_ref, freq_ref)
        dq = dqn[...] * (HEAD ** -0.5)
        dk = dkn[...]
        stage[0] = (dq * cos - _rot_half(dq, first) * sin).astype(BF16)
        stage[1] = (dk * cos - _rot_half(dk, first) * sin).astype(BF16)
        stage[2] = dvn[...].astype(BF16)
        copies = [pltpu.make_async_copy(stage.at[j], dp_ref.at[:, pl.ds((2 + j) * R + p * LANES, LANES)], sems.at[j])
                  for j in range(3)]
        for cp in copies:
            cp.start()
        for cp in copies:
            cp.wait()

    blk = pl.BlockSpec((S, LANES), lambda p: (0, p))
    return pl.pallas_call(
        body, name="att_bwd", grid=(NPAIR,),
        in_specs=[pl.BlockSpec(memory_space=pl.ANY), blk, blk, blk, blk, blk,
                  pl.BlockSpec((S, LANES), lambda p: (0, VB + p)), pl.BlockSpec((S, 1), lambda p: (0, 0)),
                  pl.BlockSpec((1, LANES), lambda p: (0, 0))],
        out_specs=pl.BlockSpec(memory_space=pl.ANY),
        out_shape=jax.ShapeDtypeStruct((S, E), BF16),
        scratch_shapes=[pltpu.VMEM((S, LANES), BF16)] * 4 + [pltpu.VMEM((S, LANES), F32)] * 9
        + [pltpu.VMEM((3, S, LANES), BF16), pltpu.SemaphoreType.DMA((3,))],
        input_output_aliases={0: 0},
        compiler_params=_cp(("arbitrary",)),
    )(dproj, d_att, att, lse, qr, kr, proj, pos, freq)


def _out_fwd_bwd(ya, att, proj, w_out_bf, x, target, mod, norm_post, norm_att):
    ts = 256

    def body(ya_ref, att_ref, gb_ref, w_ref, x_ref, t_ref, mod_ref, npost_ref, natt_ref,
             gx_ref, dya_ref, datt_ref, dgb_ref, gw_ref, acc_ref):
        i = pl.program_id(0)

        @pl.when(i == 0)
        def _():
            gw_ref[...] = jnp.zeros_like(gw_ref)
            acc_ref[...] = jnp.zeros_like(acc_ref)

        gate = mod_ref[:, 2 * D:3 * D]
        att = att_ref[...]
        gb = gb_ref[...]
        sg = _sigmoid(gb)
        silu = gb * sg
        ybp = att * silu
        yb, ybn, rstd_b = _rms_fwd(ybp, natt_ref[...])
        cat = jnp.concatenate([ya_ref[...], yb.astype(BF16)], axis=1)
        mix = _dot(cat, w_ref[...])
        rn, mn, rstd_m = _rms_fwd(mix, npost_ref[...])
        err = x_ref[...] + gate * rn - t_ref[...]
        dy = err * (1.0 / D)
        gx_ref[...] = dy
        dmix, dnpost = _rms_bwd(dy * gate, mn, rstd_m, npost_ref[...])
        dmb = dmix.astype(BF16)
        gw_ref[...] += _dot_tn(cat, dmb)
        dcat = _dot_nt(dmb, w_ref[...])
        dya_ref[...] = dcat[:, 0:R]
        dybp, dnatt = _rms_bwd(dcat[:, R:2 * R], ybn, rstd_b, natt_ref[...])
        datt_ref[...] = dybp * silu
        dgb_ref[...] = (dybp * att * (sg * (1.0 + gb * (1.0 - sg)))).astype(BF16)
        acc_ref[0:1, :] += jnp.sum(dy * rn, axis=0, keepdims=True)
        acc_ref[1:2, :] += dnpost
        acc_ref[2:3, 0:R] += dnatt
        acc_ref[3:4, :] += jnp.sum(jnp.sum(err * err, axis=1, keepdims=True), axis=0, keepdims=True)

    tile = lambda w: pl.BlockSpec((ts, w), lambda i: (i, 0))
    c0 = lambda shape: pl.BlockSpec(shape, lambda i: (0, 0))
    return pl.pallas_call(
        body, name="out_fwd_bwd", grid=(S // ts,),
        in_specs=[tile(R), tile(R), pl.BlockSpec((ts, R), lambda i: (i, 5)), c0((D, D)), tile(D), tile(D),
                  c0((1, 3 * D)), c0((1, D)), c0((1, R))],
        out_specs=[tile(D), tile(R), tile(R), pl.BlockSpec((ts, R), lambda i: (i, 5)), c0((D, D)), c0((8, D))],
        out_shape=[jax.ShapeDtypeStruct((S, D), F32), jax.ShapeDtypeStruct((S, R), F32),
                   jax.ShapeDtypeStruct((S, R), F32), jax.ShapeDtypeStruct((S, E), BF16),
                   jax.ShapeDtypeStruct((D, D), F32), jax.ShapeDtypeStruct((8, D), F32)],
        compiler_params=_cp(("arbitrary",)),
    )(ya, att, proj, w_out_bf, x, target, mod, norm_post, norm_att)


def _grad_w_in(hb, dproj):
    ts = 512

    def body(h_ref, dp_ref, gw_ref):
        @pl.when(pl.program_id(1) == 0)
        def _():
            gw_ref[...] = jnp.zeros_like(gw_ref)

        gw_ref[...] += _dot_tn(h_ref[...], dp_ref[...])

    return pl.pallas_call(
        body, name="grad_w_in", grid=(NCHIP, S // ts),
        in_specs=[pl.BlockSpec((ts, D), lambda j, s: (s, 0)), pl.BlockSpec((ts, EC), lambda j, s: (s, j))],
        out_specs=pl.BlockSpec((None, D, EC), lambda j, s: (j, 0, 0)),
        out_shape=jax.ShapeDtypeStruct((NCHIP, D, EC), F32),
        compiler_params=_cp(("parallel", "arbitrary")),
    )(hb, dproj)


def _in_proj_bwd(dproj, w_in_bf, x, gx1, mod, norm_pre):
    ts = 256

    def body(dp_ref, w_ref, x_ref, gx1_ref, mod_ref, np_ref, gx_ref, acc_ref):
        @pl.when(pl.program_id(0) == 0)
        def _():
            acc_ref[...] = jnp.zeros_like(acc_ref)

        dh = sum(_dot_nt(dp_ref[:, j * EC:(j + 1) * EC], w_ref[j]) for j in range(NCHIP))
        hp, xn, rstd = _rms_fwd(x_ref[...], np_ref[...])
        dx, dnp = _rms_bwd(dh * (1.0 + mod_ref[:, D:2 * D]), xn, rstd, np_ref[...])
        gx_ref[...] = gx1_ref[...] + dx
        acc_ref[0:1, :] += jnp.sum(dh, axis=0, keepdims=True)
        acc_ref[1:2, :] += jnp.sum(dh * hp, axis=0, keepdims=True)
        acc_ref[2:3, :] += dnp

    tile = lambda w: pl.BlockSpec((ts, w), lambda i: (i, 0))
    c0 = lambda shape: pl.BlockSpec(shape, lambda i: (0, 0))
    return pl.pallas_call(
        body, name="in_proj_bwd", grid=(S // ts,),
        in_specs=[tile(E), pl.BlockSpec((NCHIP, D, EC), lambda i: (0, 0, 0)), tile(D), tile(D), c0((1, 3 * D)),
                  c0((1, D))],
        out_specs=[tile(D), c0((8, D))],
        out_shape=[jax.ShapeDtypeStruct((S, D), F32), jax.ShapeDtypeStruct((8, D), F32)],
        compiler_params=_cp(("arbitrary",)),
    )(dproj, w_in_bf, x, gx1, mod, norm_pre)


def _block_diag(w):
    n, b, _ = w.shape
    eye = jnp.eye(n, dtype=w.dtype)
    return (eye[:, None, :, None] * w[:, :, None, :]).reshape(n * b, n * b)


def _diag_blocks(m):
    n, b = R // HEAD, HEAD
    return jnp.stack([m[h * b:(h + 1) * b, h * b:(h + 1) * b] for h in range(n)])


def _local_step(x, pos, target, mod, w_in_bf, w_out_bf, conv_w, p):
    wa_d = _block_diag(p["w_rg_a"]).astype(BF16)
    wx_d = _block_diag(p["w_rg_x"]).astype(BF16)
    rec_p = (conv_w, p["conv_b"], wa_d, p["b_rg_a"], wx_d, p["b_rg_x"], p["lru_lambda"], p["norm_rec"])
    freq = _rope_freq()
    proj, hb = _in_proj_fwd(x, mod, p["norm_pre"], w_in_bf)
    h_all, ya = _rec_fwd(proj, *rec_p)
    att, qr, kr, lse = _att_fwd(proj, pos, freq)
    gx1, d_ya, d_att, dproj, gw_out, acc_o = _out_fwd_bwd(ya, att, proj, w_out_bf, x, target, mod,
                                                           p["norm_post"], p["norm_att"])
    dproj = _att_bwd(dproj, d_att, att, lse, qr, kr, proj, pos, freq)
    dproj, dwa, dwx, sm = _rec_bwd(dproj, d_ya, proj, h_all, *rec_p)
    gw_in = _grad_w_in(hb, dproj)
    grad_x, acc_i = _in_proj_bwd(dproj, w_in_bf, x, gx1, mod, p["norm_pre"])
    dmod = jnp.concatenate([acc_i[0:1], acc_i[1:2], acc_o[0:1]], axis=1)
    small = dict(norm_pre=acc_i[2:3], norm_post=acc_o[1:2], conv_w=sm[8:12], conv_b=sm[4:5],
                 w_rg_a=_diag_blocks(dwa), b_rg_a=sm[0:1], w_rg_x=_diag_blocks(dwx), b_rg_x=sm[1:2],
                 lru_lambda=sm[2:3], norm_rec=sm[3:4], norm_att=acc_o[2:3, 0:R])
    return acc_o[3, 0], grad_x, gw_in, gw_out, dmod, small


def _me():
    return lax.axis_index("x"), lax.axis_index("y"), lax.axis_index("c")


def _flip(v, bit):
    return 1 - v if bit else v


def _peer(rel):
    x, y, c = _me()
    return (_flip(x, rel & 4), _flip(y, rel & 2), _flip(c, rel & 1))


def _remote(src, dst, send_sem, recv_sem, rel):
    return pltpu.make_async_remote_copy(src_ref=src, dst_ref=dst, send_sem=send_sem, recv_sem=recv_sem,
                                        device_id=_peer(rel), device_id_type=MESH)


def _allgather_rows(row, name):
    w = row.shape[1]

    def body(row_ref, out_ref, send_sems, recv_sems, local_sem):
        x, y, c = _me()
        me = 4 * x + 2 * y + c
        mine = pltpu.make_async_copy(row_ref, out_ref.at[pl.ds(me, 1), :], local_sem)
        mine.start()
        sends = [_remote(row_ref, out_ref.at[pl.ds(me, 1), :], send_sems.at[r - 1], recv_sems.at[r - 1], r)
                 for r in range(1, NDEV)]
        for cp in sends:
            cp.start()
        for r in range(1, NDEV):
            px, py, pc = _peer(r)
            src = 4 * px + 2 * py + pc
            _remote(row_ref, out_ref.at[pl.ds(src, 1), :], send_sems.at[r - 1], recv_sems.at[r - 1], r).wait_recv()
        for cp in sends:
            cp.wait_send()
        mine.wait()

    return pl.pallas_call(
        body, name=name,
        in_specs=[pl.BlockSpec(memory_space=pltpu.VMEM)],
        out_specs=pl.BlockSpec(memory_space=pltpu.VMEM),
        out_shape=jax.ShapeDtypeStruct((NDEV, w), row.dtype),
        scratch_shapes=[pltpu.SemaphoreType.DMA((NDEV - 1,)), pltpu.SemaphoreType.DMA((NDEV - 1,)),
                        pltpu.SemaphoreType.DMA],
        compiler_params=pltpu.CompilerParams(vmem_limit_bytes=VMEM_LIMIT),
    )(row)


def _gather_weight(w, name):
    rows, cols = w.shape
    half = rows // 2

    def body(w_ref, out_ref, send_sems, recv_sems, fsend_sems, frecv_sems):
        x, y, c = _me()
        ci = 2 * x + y
        out_ref[ci] = w_ref[...].astype(BF16)
        r0 = pl.multiple_of(c * half, half)
        r1 = pl.multiple_of((1 - c) * half, half)

        def mine_half(chip, start):
            return out_ref.at[chip, pl.ds(start, half), :]

        sends = [_remote(mine_half(ci, r0), mine_half(ci, r0), send_sems.at[k - 1], recv_sems.at[k - 1], 2 * k)
                 for k in range(1, NCHIP)]
        for cp in sends:
            cp.start()
        fwds = []
        for k in range(1, NCHIP):
            src_chip = ci ^ k
            got = mine_half(src_chip, r0)
            _remote(got, got, send_sems.at[k - 1], recv_sems.at[k - 1], 2 * k).wait_recv()
            fwd = _remote(got, got, fsend_sems.at[k - 1], frecv_sems.at[k - 1], 1)
            fwd.start()
            fwds.append(fwd)
        for k in range(1, NCHIP):
            got = mine_half(ci ^ k, r1)
            _remote(got, got, fsend_sems.at[k - 1], frecv_sems.at[k - 1], 1).wait_recv()
        for cp in sends + fwds:
            cp.wait_send()

    return pl.pallas_call(
        body, name=name,
        in_specs=[pl.BlockSpec(memory_space=pltpu.VMEM)],
        out_specs=pl.BlockSpec(memory_space=pltpu.VMEM),
        out_shape=jax.ShapeDtypeStruct((NCHIP, rows, cols), BF16),
        scratch_shapes=[pltpu.SemaphoreType.DMA((NCHIP - 1,))] * 4,
        compiler_params=pltpu.CompilerParams(vmem_limit_bytes=VMEM_LIMIT),
    )(w)


def _reduce_scatter(g4, name):
    _, rows, cols = g4.shape
    half = rows // 2

    def body(g_ref, out_ref, sib, stage, got, sems_s, sems_r):
        x, y, c = _me()
        ci = 2 * x + y
        r0 = pl.multiple_of(c * half, half)
        r1 = pl.multiple_of((1 - c) * half, half)
        to_sib = _remote(g_ref.at[:, pl.ds(r1, half), :], sib, sems_s.at[0], sems_r.at[0], 1)
        to_sib.start()
        to_sib.wait_recv()
        sends = []
        for k in range(1, NCHIP):
            j = ci ^ k
            stage[k - 1] = (g_ref[j, pl.ds(r0, half), :] + sib[j]).astype(BF16)
            cp = _remote(stage.at[k - 1], got.at[k - 1], sems_s.at[k], sems_r.at[k], 2 * k)
            cp.start()
            sends.append(cp)
        out_ref[pl.ds(r0, half), :] = g_ref[ci, pl.ds(r0, half), :] + sib[ci]
        for k in range(1, NCHIP):
            _remote(stage.at[k - 1], got.at[k - 1], sems_s.at[k], sems_r.at[k], 2 * k).wait_recv()
            out_ref[pl.ds(r0, half), :] += got[k - 1].astype(F32)
        mine = out_ref.at[pl.ds(r0, half), :]
        back = _remote(mine, mine, sems_s.at[NCHIP], sems_r.at[NCHIP], 1)
        back.start()
        theirs = out_ref.at[pl.ds(r1, half), :]
        _remote(theirs, theirs, sems_s.at[NCHIP], sems_r.at[NCHIP], 1).wait_recv()
        for cp in [to_sib] + sends + [back]:
            cp.wait_send()

    return pl.pallas_call(
        body, name=name,
        in_specs=[pl.BlockSpec(memory_space=pltpu.VMEM)],
        out_specs=pl.BlockSpec(memory_space=pltpu.VMEM),
        out_shape=jax.ShapeDtypeStruct((rows, cols), F32),
        scratch_shapes=[pltpu.VMEM((NCHIP, half, cols), F32), pltpu.VMEM((NCHIP - 1, half, cols), BF16),
                        pltpu.VMEM((NCHIP - 1, half, cols), BF16),
                        pltpu.SemaphoreType.DMA((NCHIP + 1,)), pltpu.SemaphoreType.DMA((NCHIP + 1,))],
        compiler_params=pltpu.CompilerParams(vmem_limit_bytes=VMEM_LIMIT),
    )(g4)


def _silu_rows(c_ref):
    cv = c_ref[...]
    sc = cv * _sigmoid(cv)
    return jnp.concatenate([sc, jnp.zeros_like(sc)], axis=0).astype(BF16)


def _ada_fwd(cg, w_ada, b_cols):
    def body(c_ref, w_ref, b_ref, o_ref):
        o_ref[...] = _dot(_silu_rows(c_ref), w_ref[...].astype(BF16))[0:NDEV, :] + b_ref[...]

    return pl.pallas_call(body, name="ada_fwd", out_shape=jax.ShapeDtypeStruct((NDEV, EC), F32),
                          compiler_params=_cp())(cg, w_ada, b_cols)


def _ada_bwd(cg, dmod_cols):
    def body(c_ref, d_ref, o_ref):
        dm = d_ref[...]
        dmb = jnp.concatenate([dm, jnp.zeros_like(dm)], axis=0).astype(BF16)
        o_ref[...] = _dot_tn(_silu_rows(c_ref), dmb)

    return pl.pallas_call(body, name="ada_bwd", out_shape=jax.ShapeDtypeStruct((D, EC), F32),
                          compiler_params=_cp())(cg, dmod_cols)


def _sum_rows(g):
    def body(g_ref, o_ref):
        acc = g_ref[0:1, :]
        for r in range(1, NDEV):
            acc = acc + g_ref[r:r + 1, :]
        o_ref[...] = acc

    return pl.pallas_call(body, name="sum_rows", out_shape=jax.ShapeDtypeStruct((1, g.shape[1]), F32),
                          compiler_params=_cp())(g)


def _adamw(w, g, m, v, name):
    rows, cols = w.shape
    tr = 256 if rows % 256 == 0 else rows

    def body(w_ref, g_ref, m_ref, v_ref, d_ref, nm_ref, nv_ref):
        gv = g_ref[...]
        nm = B1 * m_ref[...] + (1.0 - B1) * gv
        nv = B2 * v_ref[...] + (1.0 - B2) * (gv * gv)
        m_hat = nm / (1.0 - B1 ** STEP)
        v_hat = nv / (1.0 - B2 ** STEP)
        d_ref[...] = (-LR) * (m_hat / (jnp.sqrt(v_hat) + ADAM_EPS) + WD * w_ref[...])
        nm_ref[...] = nm
        nv_ref[...] = nv

    spec = pl.BlockSpec((tr, cols), lambda i: (i, 0))
    return pl.pallas_call(
        body, name=name, grid=(rows // tr,), in_specs=[spec] * 4, out_specs=[spec] * 3,
        out_shape=[jax.ShapeDtypeStruct((rows, cols), F32)] * 3,
        compiler_params=_cp(("parallel",)),
    )(w, g, m, v)


SMALL = (("b_ada", 3 * D), ("norm_pre", D), ("norm_post", D), ("conv_w", 4 * R), ("conv_b", R),
         ("w_rg_a", R * HEAD), ("b_rg_a", R), ("w_rg_x", R * HEAD), ("b_rg_x", R), ("lru_lambda", R),
         ("norm_rec", R), ("norm_att", R))
BIG = ("w_ada", "w_in", "w_out")
WEIGHTS = ("w_ada", "b_ada", "norm_pre", "norm_post", "w_in", "conv_w", "conv_b", "w_rg_a", "b_rg_a", "w_rg_x",
           "b_rg_x", "lru_lambda", "norm_rec", "norm_att", "w_out")


def kernel(x, c, positions, w_ada, b_ada, norm_pre, norm_post, w_in, conv_w, conv_b, w_rg_a, b_rg_a, w_rg_x, b_rg_x, lru_lambda, norm_rec, norm_att, w_out, loss_target, m_w_ada, m_b_ada, m_norm_pre, m_norm_post, m_w_in, m_conv_w, m_conv_b, m_w_rg_a, m_b_rg_a, m_w_rg_x, m_b_rg_x, m_lru_lambda, m_norm_rec, m_norm_att, m_w_out, v_w_ada, v_b_ada, v_norm_pre, v_norm_post, v_w_in, v_conv_w, v_conv_b, v_w_rg_a, v_b_rg_a, v_w_rg_x, v_b_rg_x, v_lru_lambda, v_norm_rec, v_norm_att, v_w_out):
    given = dict(locals())
    wts = {n: given[n] for n in WEIGHTS}
    ms = {n: given["m_" + n] for n in WEIGHTS}
    vs = {n: given["v_" + n] for n in WEIGHTS}
    xi, yi, cc = _me()
    chip = 2 * xi + yi
    me = 2 * chip + cc
    cw_loc = R // NCHIP

    g0 = _allgather_rows(jnp.concatenate([c, conv_w.reshape(1, 4 * cw_loc)], axis=1), "gather_c")
    cg = g0[:, 0:D]
    conv_full = g0[0::2, D:].reshape(NCHIP, 4, cw_loc).transpose(1, 0, 2).reshape(4, R)
    b_cols = lax.dynamic_slice(b_ada, (0, chip * EC), (1, EC))
    mod_part = _ada_fwd(cg, w_ada[0], b_cols)
    g1 = _allgather_rows(mod_part.reshape(1, NDEV * EC), "gather_mod")
    mod = lax.dynamic_index_in_dim(g1[0::2].reshape(NCHIP, NDEV, EC), me, axis=1, keepdims=False).reshape(1, 3 * D)

    w_in_bf = _gather_weight(w_in[0], "gather_w_in")
    w_out_bf = _gather_weight(w_out[0], "gather_w_out").reshape(D, D)

    p = dict(norm_pre=norm_pre, norm_post=norm_post, conv_b=conv_b, b_rg_a=b_rg_a, b_rg_x=b_rg_x,
             lru_lambda=lru_lambda, norm_rec=norm_rec, norm_att=norm_att, w_rg_a=w_rg_a[0], w_rg_x=w_rg_x[0])
    loss_sum, grad_x, gw_in, gw_out, dmod, small = _local_step(
        x[0], positions.reshape(S, 1), loss_target[0], mod, w_in_bf, w_out_bf, conv_full, p)
    loss = lax.psum(loss_sum, ("x", "y", "c")) * (0.5 / D)

    grads = {}
    grads["w_in"] = _reduce_scatter(gw_in, "rs_w_in")
    grads["w_out"] = _reduce_scatter(gw_out.reshape(NCHIP, D // NCHIP, D), "rs_w_out")
    small["b_ada"] = dmod
    row = jnp.concatenate([small[n].reshape(1, k) for n, k in SMALL], axis=1)
    g2 = _allgather_rows(row, "gather_small")
    tot = _sum_rows(g2)
    grads["w_ada"] = _ada_bwd(cg, lax.dynamic_slice(g2, (0, chip * EC), (NDEV, EC)))
    off = 0
    for n, k in SMALL:
        grads[n] = tot[:, off:off + k]
        off += k
    grads["conv_w"] = lax.dynamic_slice(grads["conv_w"].reshape(4, R), (0, chip * cw_loc), (4, cw_loc))

    delta, new_m, new_v = {}, {}, {}
    for n in BIG:
        delta[n], new_m[n], new_v[n] = _adamw(wts[n][0], grads[n], ms[n][0], vs[n][0], "adamw_" + n)
    pack = lambda d: jnp.concatenate([d[n].reshape(1, -1) for n, _ in SMALL], axis=1).reshape(-1, LANES)
    pd, pm, pv = _adamw(pack(wts), pack(grads), pack(ms), pack(vs), "adamw_small")
    off = 0
    for n, _ in SMALL:
        k = wts[n].size
        for dst, src in ((delta, pd), (new_m, pm), (new_v, pv)):
            dst[n] = src.reshape(1, -1)[:, off:off + k]
        off += k
    out = lambda d: [d[n].reshape(wts[n].shape) for n in WEIGHTS]
    return (loss, grad_x.reshape(x.shape), *out(grads), *out(delta), *out(new_m), *out(new_v))
```

```python
import functools

import numpy as np
import jax
import jax.numpy as jnp
from jax import lax
from jax.experimental import pallas as pl
from jax.experimental.pallas import tpu as pltpu

F32 = jnp.float32
BF16 = jnp.bfloat16

S = 2048
D = 1024
E = 3072
R = 512
NDEV = 8
NCHIP = 4
EC = 768
LRU_C = 8.0
EPS = 1e-6
NEG = -1e30
HEAD = 64
BLK = 128
PATTERNS = (1, 4, 16)
ROPE_THETA = 10000.0
LANES = 128
VMEM_LIMIT = 56 * 1024 * 1024

B1, B2, LR, WD, ADAM_EPS, STEP = 0.9, 0.999, 0.001, 0.01, 1e-8, 10
MESH = pl.DeviceIdType.MESH


def _cp(sem=None, **kw):
    return pltpu.CompilerParams(dimension_semantics=sem, vmem_limit_bytes=VMEM_LIMIT, **kw)


def _dot(a, b):
    return jnp.dot(a, b, preferred_element_type=F32)


def _dot_nt(a, b):
    return lax.dot_general(a, b, (((1,), (1,)), ((), ())), preferred_element_type=F32)


def _dot_tn(a, b):
    return lax.dot_general(a, b, (((0,), (0,)), ((), ())), preferred_element_type=F32)


def _sigmoid(x):
    return 1.0 / (1.0 + jnp.exp(-x))


def _expm1(x):
    poly = x * (1.0 + x * (0.5 + x * (1.0 / 6 + x * (1.0 / 24 + x * (1.0 / 120 + x * (1.0 / 720))))))
    return jnp.where(jnp.abs(x) < 0.3, poly, jnp.exp(x) - 1.0)


def _rms_fwd(v, g):
    rstd = lax.rsqrt(jnp.mean(v * v, axis=-1, keepdims=True) + EPS)
    vn = v * rstd
    return vn * g, vn, rstd


def _rms_bwd(dy, vn, rstd, g):
    dvn = dy * g
    dv = rstd * (dvn - vn * jnp.mean(dvn * vn, axis=-1, keepdims=True))
    return dv, jnp.sum(dy * vn, axis=0, keepdims=True)


def _in_proj_fwd(x, mod, norm_pre, w_in_bf):
    ts = 256

    def body(x_ref, mod_ref, np_ref, w_ref, proj_ref, hb_ref):
        hp, _, _ = _rms_fwd(x_ref[...], np_ref[...])
        h = hp * (1.0 + mod_ref[:, D:2 * D]) + mod_ref[:, 0:D]
        hb = h.astype(BF16)
        hb_ref[...] = hb
        for j in range(NCHIP):
            proj_ref[:, j * EC:(j + 1) * EC] = _dot(hb, w_ref[j])

    return pl.pallas_call(
        body, name="in_proj_fwd", grid=(S // ts,),
        in_specs=[pl.BlockSpec((ts, D), lambda i: (i, 0)), pl.BlockSpec((1, 3 * D), lambda i: (0, 0)),
                  pl.BlockSpec((1, D), lambda i: (0, 0)), pl.BlockSpec((NCHIP, D, EC), lambda i: (0, 0, 0))],
        out_specs=[pl.BlockSpec((ts, E), lambda i: (i, 0)), pl.BlockSpec((ts, D), lambda i: (i, 0))],
        out_shape=[jax.ShapeDtypeStruct((S, E), F32), jax.ShapeDtypeStruct((S, D), BF16)],
        compiler_params=_cp(("parallel",)),
    )(x, mod, norm_pre, w_in_bf)


RT = 256


def _shift_down(cur, prev8, j, row):
    if j == 0:
        return cur
    top = jnp.tile(pltpu.roll(prev8, j, 0), (RT // 8, 1))
    return jnp.where(row >= j, pltpu.roll(cur, j, 0), top)


def _shift_up(cur, next8, j, row):
    if j == 0:
        return cur
    bot = jnp.tile(pltpu.roll(next8, 8 - j, 0), (RT // 8, 1))
    return jnp.where(row < RT - j, pltpu.roll(cur, RT - j, 0), bot)


def _rec_gates(xp, xprev8, row, cw_ref, cb_ref, wa_ref, ba_ref, wx_ref, bx_ref, lam_ref):
    xa = cb_ref[...] + sum(cw_ref[3 - j:4 - j, :] * _shift_down(xp, xprev8, j, row) for j in range(4))
    xab = xa.astype(BF16)
    r = _sigmoid(_dot(xab, wa_ref[...]) + ba_ref[...])
    ig = _sigmoid(_dot(xab, wx_ref[...]) + bx_ref[...])
    nl = -lam_ref[...]
    sp = jnp.maximum(nl, 0.0) + jnp.log1p(jnp.exp(-jnp.abs(nl)))
    la = (-LRU_C) * r * sp
    a = jnp.exp(la)
    mult = jnp.sqrt(-_expm1(2.0 * la))
    return dict(xa=xa, xab=xab, r=r, ig=ig, sp=sp, la=la, a=a, mult=mult)


def _scan_fwd(a, u, row):
    sh = 1
    while sh < RT:
        a_s = jnp.where(row >= sh, pltpu.roll(a, sh, 0), 1.0)
        u_s = jnp.where(row >= sh, pltpu.roll(u, sh, 0), 0.0)
        u = a * u_s + u
        a = a * a_s
        sh *= 2
    return a, u


def _scan_bwd(al, g, row):
    sh = 1
    while sh < RT:
        al_s = jnp.where(row < RT - sh, pltpu.roll(al, RT - sh, 0), 1.0)
        g_s = jnp.where(row < RT - sh, pltpu.roll(g, RT - sh, 0), 0.0)
        g = g + al * g_s
        al = al * al_s
        sh *= 2
    return g


def _rec_fwd(proj, conv_w, conv_b, wa_d, ba, wx_d, bx, lam, norm_rec):
    nt = S // RT

    def body(p_ref, cw_ref, cb_ref, wa_ref, ba_ref, wx_ref, bx_ref, lam_ref, nr_ref,
             h_ref, ya_ref, prev8, hc):
        i = pl.program_id(0)

        @pl.when(i == 0)
        def _():
            prev8[...] = jnp.zeros_like(prev8)
            hc[...] = jnp.zeros_like(hc)

        row = lax.broadcasted_iota(jnp.int32, (RT, R), 0)
        xp = p_ref[:, 0:R]
        ga = p_ref[:, R:2 * R]
        f = _rec_gates(xp, prev8[...], row, cw_ref, cb_ref, wa_ref, ba_ref, wx_ref, bx_ref, lam_ref)
        u = f["mult"] * (f["ig"] * f["xa"])
        acum, hh = _scan_fwd(f["a"], u, row)
        h = hh + acum * hc[0:1, :]
        h_ref[...] = h
        hc[0:1, :] = h_ref[RT - 1:RT, :]
        prev8[...] = p_ref[RT - 8:RT, 0:R]
        yp = h * (ga * _sigmoid(ga))
        ya, _, _ = _rms_fwd(yp, nr_ref[...])
        ya_ref[...] = ya.astype(BF16)

    row1 = lambda n: pl.BlockSpec((1, n), lambda i: (0, 0))
    return pl.pallas_call(
        body, name="rec_fwd", grid=(nt,),
        in_specs=[pl.BlockSpec((RT, 2 * R), lambda i: (i, 0)), pl.BlockSpec((4, R), lambda i: (0, 0)), row1(R),
                  pl.BlockSpec((R, R), lambda i: (0, 0)), row1(R), pl.BlockSpec((R, R), lambda i: (0, 0)), row1(R),
                  row1(R), row1(R)],
        out_specs=[pl.BlockSpec((RT, R), lambda i: (i, 0)), pl.BlockSpec((RT, R), lambda i: (i, 0))],
        out_shape=[jax.ShapeDtypeStruct((S, R), F32), jax.ShapeDtypeStruct((S, R), BF16)],
        scratch_shapes=[pltpu.VMEM((8, R), F32), pltpu.VMEM((8, R), F32)],
        compiler_params=_cp(("arbitrary",)),
    )(proj, conv_w, conv_b, wa_d, ba, wx_d, bx, lam, norm_rec)


def _rec_bwd(dproj, d_ya, proj, h_all, conv_w, conv_b, wa_d, ba, wx_d, bx, lam, norm_rec):
    nt = S // RT

    def body(dp_in, dya_ref, p_ref, pprev_ref, h_ref, hprev_ref, cw_ref, cb_ref, wa_ref, ba_ref, wx_ref, bx_ref,
             lam_ref, nr_ref, dp_ref, dwa_ref, dwx_ref, sm_ref, nxt8, cg):
        i = pl.program_id(0)
        ti = nt - 1 - i

        @pl.when(i == 0)
        def _():
            nxt8[...] = jnp.zeros_like(nxt8)
            cg[...] = jnp.zeros_like(cg)
            dwa_ref[...] = jnp.zeros_like(dwa_ref)
            dwx_ref[...] = jnp.zeros_like(dwx_ref)
            sm_ref[...] = jnp.zeros_like(sm_ref)

        row = lax.broadcasted_iota(jnp.int32, (RT, R), 0)
        first = (ti > 0).astype(F32)
        xprev8 = pprev_ref[...] * first
        hprev8 = hprev_ref[...] * first
        xp = p_ref[:, 0:R]
        ga = p_ref[:, R:2 * R]
        f = _rec_gates(xp, xprev8, row, cw_ref, cb_ref, wa_ref, ba_ref, wx_ref, bx_ref, lam_ref)
        xa, r, ig, a, mult = f["xa"], f["r"], f["ig"], f["a"], f["mult"]
        h = h_ref[...]
        sg = _sigmoid(ga)
        gate = ga * sg
        yp = h * gate
        _, ypn, rstd = _rms_fwd(yp, nr_ref[...])
        d_yp, dnr = _rms_bwd(dya_ref[...], ypn, rstd, nr_ref[...])
        d_ga = d_yp * h * (sg * (1.0 + ga * (1.0 - sg)))
        dh = d_yp * gate + jnp.where(row == RT - 1, cg[0:1, :], 0.0)
        al = jnp.where(row < RT - 1, pltpu.roll(a, RT - 1, 0), 0.0)
        g = _scan_bwd(al, dh, row)
        cg[0:1, :] = jnp.sum(jnp.where(row == 0, a * g, 0.0), axis=0, keepdims=True)
        h_m1 = _shift_down(h, hprev8, 1, row)
        da = g * h_m1
        ix = ig * xa
        d_mult = g * ix
        d_ig = g * mult * xa
        d_xa = g * mult * ig
        d_la = da * a - d_mult * (a * a) / mult
        d_r = d_la * ((-LRU_C) * f["sp"])
        dsp = jnp.sum(d_la * ((-LRU_C) * r), axis=0, keepdims=True)
        dlam = dsp * (-_sigmoid(-lam_ref[...]))
        d_za = d_r * r * (1.0 - r)
        d_zx = d_ig * ig * (1.0 - ig)
        dzab = d_za.astype(BF16)
        dzxb = d_zx.astype(BF16)
        dwa_ref[...] += _dot_tn(f["xab"], dzab)
        dwx_ref[...] += _dot_tn(f["xab"], dzxb)
        d_xa = d_xa + _dot_nt(dzab, wa_ref[...]) + _dot_nt(dzxb, wx_ref[...])
        d_xp = sum(cw_ref[3 - j:4 - j, :] * _shift_up(d_xa, nxt8[...], j, row) for j in range(4))
        dcw = [jnp.sum(d_xa * _shift_down(xp, xprev8, 3 - k, row), axis=0, keepdims=True) for k in range(4)]
        dp_ref[:, 0:R] = d_xp.astype(BF16)
        dp_ref[:, R:2 * R] = d_ga.astype(BF16)
        dp8 = d_xa[0:8, :]
        nxt8[...] = dp8
        sm_ref[0:1, :] += jnp.sum(d_za, axis=0, keepdims=True)
        sm_ref[1:2, :] += jnp.sum(d_zx, axis=0, keepdims=True)
        sm_ref[2:3, :] += dlam
        sm_ref[3:4, :] += dnr
        sm_ref[4:5, :] += jnp.sum(d_xa, axis=0, keepdims=True)
        for k in range(4):
            sm_ref[8 + k:9 + k, :] += dcw[k]

    c0 = lambda shape: pl.BlockSpec(shape, lambda i: (0, 0))
    rev = lambda i: nt - 1 - i
    prev8 = lambda i: (jnp.maximum((nt - 1 - i) * (RT // 8) - 1, 0), 0)
    return pl.pallas_call(
        body, name="rec_bwd", grid=(nt,),
        in_specs=[pl.BlockSpec(memory_space=pl.ANY),
                  pl.BlockSpec((RT, R), lambda i: (rev(i), 0)),
                  pl.BlockSpec((RT, 2 * R), lambda i: (rev(i), 0)), pl.BlockSpec((8, R), prev8),
                  pl.BlockSpec((RT, R), lambda i: (rev(i), 0)), pl.BlockSpec((8, R), prev8),
                  c0((4, R)), c0((1, R)), c0((R, R)), c0((1, R)), c0((R, R)), c0((1, R)), c0((1, R)), c0((1, R))],
        out_specs=[pl.BlockSpec((RT, 2 * R), lambda i: (rev(i), 0)), c0((R, R)), c0((R, R)), c0((16, R))],
        out_shape=[jax.ShapeDtypeStruct((S, E), BF16), jax.ShapeDtypeStruct((R, R), F32),
                   jax.ShapeDtypeStruct((R, R), F32), jax.ShapeDtypeStruct((16, R), F32)],
        scratch_shapes=[pltpu.VMEM((8, R), F32), pltpu.VMEM((8, R), F32)],
        input_output_aliases={0: 0},
        compiler_params=_cp(("arbitrary",)),
    )(dproj, d_ya, proj, proj, h_all, h_all, conv_w, conv_b, wa_d, ba, wx_d, bx, lam, norm_rec)


NPAIR = R // LANES
QB, KB, VB, GB = 2 * R // LANES, 3 * R // LANES, 4 * R // LANES, 5 * R // LANES


def _rope_freq():
    half = HEAD // 2
    inv = np.float32(ROPE_THETA) ** (-(np.arange(half, dtype=np.float32) / np.float32(half)))
    return jnp.asarray(np.tile(inv.astype(np.float32), LANES // half)[None, :])


def _rot_half(x, first):
    return jnp.where(first, -pltpu.roll(x, LANES - HEAD // 2, 1), pltpu.roll(x, HEAD // 2, 1))


def _cos_sin(pos_ref, freq_ref):
    ang = pos_ref[...].astype(F32) * freq_ref[...]
    return jnp.cos(ang), jnp.sin(ang)


def _deint(src_ref, dst_ref, d):
    n = S // d
    for r in range(d):
        v = src_ref[pl.ds(r, n, stride=d), :] if d > 1 else src_ref[...]
        dst_ref[r * n:(r + 1) * n, :] = v.astype(dst_ref.dtype)


def _reint(src_ref, dst_ref, d, accumulate):
    n = S // d
    for r in range(d):
        idx = (pl.ds(r, n, stride=d), slice(None)) if d > 1 else (slice(None), slice(None))
        v = src_ref[r * n:(r + 1) * n, :]
        if accumulate:
            dst_ref[idx] = dst_ref[idx] + v
        else:
            dst_ref[idx] = v


def _blk_masks(b, nb):
    qi = lax.broadcasted_iota(jnp.int32, (BLK, BLK), 0)
    ki = lax.broadcasted_iota(jnp.int32, (BLK, BLK), 1)
    has_prev = lax.rem(b, nb) != 0
    return ki <= qi, jnp.logical_and(ki >= qi, has_prev)


def _rope_table(pos, freq):
    def body(pos_ref, freq_ref, cos_ref, sin_ref):
        cos_ref[...], sin_ref[...] = _cos_sin(pos_ref, freq_ref)

    return pl.pallas_call(body, name="rope_table", out_shape=[jax.ShapeDtypeStruct((S, LANES), F32)] * 2,
                          compiler_params=_cp())(pos, freq)


def _deint_heads(src_ref, dst0, dst1, d):
    n = S // d
    hm0 = lax.broadcasted_iota(jnp.int32, (n, LANES), 1) < HEAD
    for r in range(d):
        v = src_ref[pl.ds(r, n, stride=d), :] if d > 1 else src_ref[...]
        dst0[r * n:(r + 1) * n, :] = jnp.where(hm0, v, 0.0).astype(BF16)
        dst1[r * n:(r + 1) * n, :] = jnp.where(hm0, 0.0, v).astype(BF16)


def _reint_prev(src_ref, dst_ref, d):
    n = S // d
    if n == BLK:
        return
    for r in range(d):
        idx = (pl.ds(r, n - BLK, stride=d), slice(None)) if d > 1 else (slice(0, n - BLK), slice(None))
        dst_ref[idx] = dst_ref[idx] + src_ref[r * n + BLK:(r + 1) * n, :]


def _pair_masks():
    qi = lax.broadcasted_iota(jnp.int32, (BLK, 2 * BLK), 0)
    ki = lax.broadcasted_iota(jnp.int32, (BLK, 2 * BLK), 1) & (BLK - 1)
    return ki <= qi, ki >= qi


def _two(ref0, ref1, st, axis):
    return jnp.concatenate([ref0[pl.ds(st, BLK), :], ref1[pl.ds(st, BLK), :]], axis=axis)


ATT_UNROLL = 4


def _att_fwd(proj, cos, sin):
    def body(q_ref, k_ref, v_ref, cos_ref, sin_ref, att_ref, qr_ref, kr_ref, lse_ref,
             qd, kd0, kd1, vd0, vd1, od, ld, on, ln):
        lane = lax.broadcasted_iota(jnp.int32, (S, LANES), 1)
        first = (lane & (HEAD // 2)) == 0
        cos, sin = cos_ref[...], sin_ref[...]
        q = q_ref[...]
        k = k_ref[...]
        qr_ref[...] = (q * cos + _rot_half(q, first) * sin) * (HEAD ** -0.5)
        kr_ref[...] = k * cos + _rot_half(k, first) * sin
        hm0 = lax.broadcasted_iota(jnp.int32, (BLK, LANES), 1) < HEAD
        top = lax.broadcasted_iota(jnp.int32, (2 * BLK, LANES), 0) < BLK
        ones2 = (top == (lax.broadcasted_iota(jnp.int32, (2 * BLK, LANES), 1) < HEAD)).astype(BF16)
        mc2, mp2 = _pair_masks()

        for pi, d in enumerate(PATTERNS):
            nb = S // d // BLK
            _deint(qr_ref, qd, d)
            _deint_heads(kr_ref, kd0, kd1, d)
            _deint_heads(v_ref, vd0, vd1, d)

            def blk(b, carry):
                st = pl.multiple_of(b * BLK, BLK)
                qb = qd[pl.ds(st, BLK), :]
                sc = jnp.where(mc2, _dot_nt(qb, _two(kd0, kd1, st, 0)), NEG)
                mx = sc
                if nb > 1:
                    stp = pl.multiple_of(jnp.maximum(b - 1, 0) * BLK, BLK)
                    mp = jnp.logical_and(mp2, lax.rem(b, nb) != 0)
                    sp = jnp.where(mp, _dot_nt(qb, _two(kd0, kd1, stp, 0)), NEG)
                    mx = jnp.maximum(sc, sp)
                m0 = jnp.max(mx[:, 0:BLK], axis=1, keepdims=True)
                m1 = jnp.max(mx[:, BLK:2 * BLK], axis=1, keepdims=True)
                mf = jnp.concatenate([jnp.broadcast_to(m0, (BLK, BLK)), jnp.broadcast_to(m1, (BLK, BLK))], axis=1)
                o = _dot(jnp.exp(sc - mf).astype(BF16), jnp.concatenate([_two(vd0, vd1, st, 0), ones2], axis=1))
                if nb > 1:
                    o = o + _dot(jnp.exp(sp - mf).astype(BF16), jnp.concatenate([_two(vd0, vd1, stp, 0), ones2], axis=1))
                l = o[:, LANES:2 * LANES]
                od[pl.ds(st, BLK), :] = o[:, 0:LANES] / l
                ld[pl.ds(st, BLK), :] = jnp.where(hm0, m0, m1) + jnp.log(l)
                return carry

            lax.fori_loop(0, S // BLK, blk, 0, unroll=ATT_UNROLL)
            _reint(od, on.at[pi], d, False)
            _reint(ld, ln.at[pi], d, False)

        l0, l1, l2 = ln[0], ln[1], ln[2]
        m = jnp.maximum(jnp.maximum(l0, l1), l2)
        e0, e1, e2 = jnp.exp(l0 - m), jnp.exp(l1 - m), jnp.exp(l2 - m)
        den = e0 + e1 + e2
        att_ref[...] = (e0 * on[0] + e1 * on[1] + e2 * on[2]) / den
        lse_ref[...] = m + jnp.log(den)

    col = lambda c0: pl.BlockSpec((S, LANES), lambda p: (0, c0 + p))
    out = pl.BlockSpec((S, LANES), lambda p: (0, p))
    tab = pl.BlockSpec((S, LANES), lambda p: (0, 0))
    return pl.pallas_call(
        body, name="att_fwd", grid=(NPAIR,),
        in_specs=[col(QB), col(KB), col(VB), tab, tab],
        out_specs=[out, out, out, out],
        out_shape=[jax.ShapeDtypeStruct((S, R), F32)] * 4,
        scratch_shapes=[pltpu.VMEM((S, LANES), BF16)] * 5 + [pltpu.VMEM((S, LANES), F32)] * 2
        + [pltpu.VMEM((3, S, LANES), F32)] * 2,
        compiler_params=_cp(("parallel",)),
    )(proj, proj, proj, cos, sin)


def _att_bwd(dproj, d_att, att, lse, qr, kr, proj, cos, sin):
    def body(dp_in, do_ref, o_ref, lse_ref, qr_ref, kr_ref, v_ref, cos_ref, sin_ref, dp_ref,
             qd, kd0, kd1, vd0, vd1, dod, lb0d, lb1d, dl0d, dl1d, dqd, dkcd, dkpd, dvcd, dvpd,
             dqn, dkn, dvn, lb0n, lb1n, dl0n, dl1n, stage, sems):
        p = pl.program_id(0)
        hms = lax.broadcasted_iota(jnp.int32, (S, LANES), 1) < HEAD
        prod = do_ref[...] * o_ref[...]
        dl0n[...] = jnp.broadcast_to(jnp.sum(jnp.where(hms, prod, 0.0), axis=1, keepdims=True), (S, LANES))
        dl1n[...] = jnp.broadcast_to(jnp.sum(jnp.where(hms, 0.0, prod), axis=1, keepdims=True), (S, LANES))
        lse = lse_ref[...]
        lsw = pltpu.roll(lse, HEAD, 1)
        lb0n[...] = jnp.where(hms, lse, lsw)
        lb1n[...] = jnp.where(hms, lsw, lse)
        dqn[...] = jnp.zeros_like(dqn)
        dkn[...] = jnp.zeros_like(dkn)
        dvn[...] = jnp.zeros_like(dvn)
        hm0 = lax.broadcasted_iota(jnp.int32, (BLK, LANES), 1) < HEAD
        mc2, mp2 = _pair_masks()

        for d in PATTERNS:
            nb = S // d // BLK
            _deint(qr_ref, qd, d)
            _deint_heads(kr_ref, kd0, kd1, d)
            _deint_heads(v_ref, vd0, vd1, d)
            _deint(do_ref, dod, d)
            for src, dst in ((lb0n, lb0d), (lb1n, lb1d), (dl0n, dl0d), (dl1n, dl1d)):
                _deint(src, dst, d)

            def blk(b, carry):
                st = pl.multiple_of(b * BLK, BLK)
                qb, dob = qd[pl.ds(st, BLK), :], dod[pl.ds(st, BLK), :]
                lb, dl = _two(lb0d, lb1d, st, 1), _two(dl0d, dl1d, st, 1)

                def side(stk, mask):
                    k2, v2 = _two(kd0, kd1, stk, 0), _two(vd0, vd1, stk, 0)
                    pk = jnp.where(mask, jnp.exp(_dot_nt(qb, k2) - lb), 0.0)
                    ds = (pk * (_dot_nt(dob, v2) - dl)).astype(BF16)
                    rk, rv = _dot_tn(ds, qb), _dot_tn(pk.astype(BF16), dob)
                    return (_dot(ds, k2), jnp.where(hm0, rk[0:BLK], rk[BLK:2 * BLK]),
                            jnp.where(hm0, rv[0:BLK], rv[BLK:2 * BLK]))

                dq, dkc, dvc = side(st, mc2)
                if nb > 1:
                    stp = pl.multiple_of(jnp.maximum(b - 1, 0) * BLK, BLK)
                    dqp, dkp, dvp = side(stp, jnp.logical_and(mp2, lax.rem(b, nb) != 0))
                    dq = dq + dqp
                    dkpd[pl.ds(st, BLK), :] = dkp
                    dvpd[pl.ds(st, BLK), :] = dvp
                dqd[pl.ds(st, BLK), :] = dq
                dkcd[pl.ds(st, BLK), :] = dkc
                dvcd[pl.ds(st, BLK), :] = dvc
                return carry

            lax.fori_loop(0, S // BLK, blk, 0, unroll=ATT_UNROLL)
            _reint(dqd, dqn, d, True)
            _reint(dkcd, dkn, d, True)
            _reint(dvcd, dvn, d, True)
            _reint_prev(dkpd, dkn, d)
            _reint_prev(dvpd, dvn, d)

        lane = lax.broadcasted_iota(jnp.int32, (S, LANES), 1)
        first = (lane & (HEAD // 2)) == 0
        cos, sin = cos_ref[...], sin_ref[...]
        dq = dqn[...] * (HEAD ** -0.5)
        dk = dkn[...]
        stage[0] = (dq * cos - _rot_half(dq, first) * sin).astype(BF16)
        stage[1] = (dk * cos - _rot_half(dk, first) * sin).astype(BF16)
        stage[2] = dvn[...].astype(BF16)
        copies = [pltpu.make_async_copy(stage.at[j], dp_ref.at[:, pl.ds((2 + j) * R + p * LANES, LANES)], sems.at[j])
                  for j in range(3)]
        for cp in copies:
            cp.start()
        for cp in copies:
            cp.wait()

    blk = pl.BlockSpec((S, LANES), lambda p: (0, p))
    tab = pl.BlockSpec((S, LANES), lambda p: (0, 0))
    return pl.pallas_call(
        body, name="att_bwd", grid=(NPAIR,),
        in_specs=[pl.BlockSpec(memory_space=pl.ANY), blk, blk, blk, blk, blk,
                  pl.BlockSpec((S, LANES), lambda p: (0, VB + p)), tab, tab],
        out_specs=pl.BlockSpec(memory_space=pl.ANY),
        out_shape=jax.ShapeDtypeStruct((S, E), BF16),
        scratch_shapes=[pltpu.VMEM((S, LANES), BF16)] * 6 + [pltpu.VMEM((S, LANES), F32)] * 16
        + [pltpu.VMEM((3, S, LANES), BF16), pltpu.SemaphoreType.DMA((3,))],
        input_output_aliases={0: 0},
        compiler_params=_cp(("arbitrary",)),
    )(dproj, d_att, att, lse, qr, kr, proj, cos, sin)


def _att_fwd_old(proj, pos, freq):
    def body(q_ref, k_ref, v_ref, pos_ref, freq_ref, att_ref, qr_ref, kr_ref, lse_ref,
             qd, kd, vd, od, ld, on, ln):
        lane = lax.broadcasted_iota(jnp.int32, (S, LANES), 1)
        first = (lane & (HEAD // 2)) == 0
        cos, sin = _cos_sin(pos_ref, freq_ref)
        q = q_ref[...]
        k = k_ref[...]
        qr_ref[...] = (q * cos + _rot_half(q, first) * sin) * (HEAD ** -0.5)
        kr_ref[...] = k * cos + _rot_half(k, first) * sin
        hm0 = lax.broadcasted_iota(jnp.int32, (BLK, LANES), 1) < HEAD

        for pi, d in enumerate(PATTERNS):
            nb = S // d // BLK
            _deint(qr_ref, qd, d)
            _deint(kr_ref, kd, d)
            _deint(v_ref, vd, d)

            def blk(b, carry):
                st = pl.multiple_of(b * BLK, BLK)
                stp = pl.multiple_of(jnp.maximum(b - 1, 0) * BLK, BLK)
                mc, mp = _blk_masks(b, nb)
                qb = qd[pl.ds(st, BLK), :]
                kc, kp = kd[pl.ds(st, BLK), :], kd[pl.ds(stp, BLK), :]
                vc, vp = vd[pl.ds(st, BLK), :], vd[pl.ds(stp, BLK), :]
                outs, lses = [], []
                for hm in (hm0, jnp.logical_not(hm0)):
                    qm = jnp.where(hm, qb, jnp.zeros_like(qb))
                    sc = jnp.where(mc, _dot_nt(qm, kc), NEG)
                    sp = jnp.where(mp, _dot_nt(qm, kp), NEG)
                    m = jnp.maximum(jnp.max(sc, axis=1, keepdims=True), jnp.max(sp, axis=1, keepdims=True))
                    pc, pp = jnp.exp(sc - m), jnp.exp(sp - m)
                    l = jnp.sum(pc, axis=1, keepdims=True) + jnp.sum(pp, axis=1, keepdims=True)
                    o = _dot(pc.astype(BF16), vc) + _dot(pp.astype(BF16), vp)
                    outs.append(o / l)
                    lses.append(m + jnp.log(l))
                od[pl.ds(st, BLK), :] = jnp.where(hm0, outs[0], outs[1])
                ld[pl.ds(st, BLK), :] = jnp.where(hm0, lses[0], lses[1])
                return carry

            lax.fori_loop(0, S // BLK, blk, 0)
            _reint(od, on.at[pi], d, False)
            _reint(ld, ln.at[pi], d, False)

        l0, l1, l2 = ln[0], ln[1], ln[2]
        m = jnp.maximum(jnp.maximum(l0, l1), l2)
        e0, e1, e2 = jnp.exp(l0 - m), jnp.exp(l1 - m), jnp.exp(l2 - m)
        den = e0 + e1 + e2
        att_ref[...] = (e0 * on[0] + e1 * on[1] + e2 * on[2]) / den
        lse_ref[...] = m + jnp.log(den)

    col = lambda c0: pl.BlockSpec((S, LANES), lambda p: (0, c0 + p))
    out = pl.BlockSpec((S, LANES), lambda p: (0, p))
    return pl.pallas_call(
        body, name="att_fwd", grid=(NPAIR,),
        in_specs=[col(QB), col(KB), col(VB), pl.BlockSpec((S, 1), lambda p: (0, 0)),
                  pl.BlockSpec((1, LANES), lambda p: (0, 0))],
        out_specs=[out, out, out, out],
        out_shape=[jax.ShapeDtypeStruct((S, R), F32)] * 4,
        scratch_shapes=[pltpu.VMEM((S, LANES), BF16)] * 3 + [pltpu.VMEM((S, LANES), F32)] * 2
        + [pltpu.VMEM((3, S, LANES), F32)] * 2,
        compiler_params=_cp(("parallel",)),
    )(proj, proj, proj, pos, freq)


def _att_bwd_old(dproj, d_att, att, lse, qr, kr, proj, pos, freq):
    def body(dp_in, do_ref, o_ref, lse_ref, qr_ref, kr_ref, v_ref, pos_ref, freq_ref, dp_ref,
             qd, kd, vd, dod, lsd, prd, dqd, dkd, dvd, dqn, dkn, dvn, prn, stage, sems):
        p = pl.program_id(0)
        prn[...] = do_ref[...] * o_ref[...]
        dqn[...] = jnp.zeros_like(dqn)
        dkn[...] = jnp.zeros_like(dkn)
        dvn[...] = jnp.zeros_like(dvn)
        hm0 = lax.broadcasted_iota(jnp.int32, (BLK, LANES), 1) < HEAD

        for d in PATTERNS:
            nb = S // d // BLK
            _deint(qr_ref, qd, d)
            _deint(kr_ref, kd, d)
            _deint(v_ref, vd, d)
            _deint(do_ref, dod, d)
            _deint(lse_ref, lsd, d)
            _deint(prn, prd, d)
            dkd[...] = jnp.zeros_like(dkd)
            dvd[...] = jnp.zeros_like(dvd)

            def blk(b, carry):
                st = pl.multiple_of(b * BLK, BLK)
                stp = pl.multiple_of(jnp.maximum(b - 1, 0) * BLK, BLK)
                mc, mp = _blk_masks(b, nb)
                qb, dob = qd[pl.ds(st, BLK), :], dod[pl.ds(st, BLK), :]
                kc, kp = kd[pl.ds(st, BLK), :], kd[pl.ds(stp, BLK), :]
                vc, vp = vd[pl.ds(st, BLK), :], vd[pl.ds(stp, BLK), :]
                lsb, prb = lsd[pl.ds(st, BLK), :], prd[pl.ds(st, BLK), :]
                dqs = []
                dkc = dkp = dvc = dvp = None
                for hm in (hm0, jnp.logical_not(hm0)):
                    qm = jnp.where(hm, qb, jnp.zeros_like(qb))
                    dom = jnp.where(hm, dob, jnp.zeros_like(dob))
                    lh = jnp.max(jnp.where(hm, lsb, -3e38), axis=1, keepdims=True)
                    delta = jnp.sum(jnp.where(hm, prb, 0.0), axis=1, keepdims=True)
                    pc = jnp.where(mc, jnp.exp(_dot_nt(qm, kc) - lh), 0.0)
                    pp = jnp.where(mp, jnp.exp(_dot_nt(qm, kp) - lh), 0.0)
                    dsc = (pc * (_dot_nt(dom, vc) - delta)).astype(BF16)
                    dsp = (pp * (_dot_nt(dom, vp) - delta)).astype(BF16)
                    dqs.append(_dot(dsc, kc) + _dot(dsp, kp))
                    acc = lambda t, n: n if t is None else t + n
                    dkc, dkp = acc(dkc, _dot_tn(dsc, qm)), acc(dkp, _dot_tn(dsp, qm))
                    dvc, dvp = acc(dvc, _dot_tn(pc.astype(BF16), dom)), acc(dvp, _dot_tn(pp.astype(BF16), dom))
                dqd[pl.ds(st, BLK), :] = jnp.where(hm0, dqs[0], dqs[1])
                dkd[pl.ds(stp, BLK), :] += dkp
                dvd[pl.ds(stp, BLK), :] += dvp
                dkd[pl.ds(st, BLK), :] += dkc
                dvd[pl.ds(st, BLK), :] += dvc
                return carry

            lax.fori_loop(0, S // BLK, blk, 0)
            _reint(dqd, dqn, d, True)
            _reint(dkd, dkn, d, True)
            _reint(dvd, dvn, d, True)

        lane = lax.broadcasted_iota(jnp.int32, (S, LANES), 1)
        first = (lane & (HEAD // 2)) == 0
        cos, sin = _cos_sin(pos_ref, freq_ref)
        dq = dqn[...] * (HEAD ** -0.5)
        dk = dkn[...]
        stage[0] = (dq * cos - _rot_half(dq, first) * sin).astype(BF16)
        stage[1] = (dk * cos - _rot_half(dk, first) * sin).astype(BF16)
        stage[2] = dvn[...].astype(BF16)
        copies = [pltpu.make_async_copy(stage.at[j], dp_ref.at[:, pl.ds((2 + j) * R + p * LANES, LANES)], sems.at[j])
                  for j in range(3)]
        for cp in copies:
            cp.start()
        for cp in copies:
            cp.wait()

    blk = pl.BlockSpec((S, LANES), lambda p: (0, p))
    return pl.pallas_call(
        body, name="att_bwd", grid=(NPAIR,),
        in_specs=[pl.BlockSpec(memory_space=pl.ANY), blk, blk, blk, blk, blk,
                  pl.BlockSpec((S, LANES), lambda p: (0, VB + p)), pl.BlockSpec((S, 1), lambda p: (0, 0)),
                  pl.BlockSpec((1, LANES), lambda p: (0, 0))],
        out_specs=pl.BlockSpec(memory_space=pl.ANY),
        out_shape=jax.ShapeDtypeStruct((S, E), BF16),
        scratch_shapes=[pltpu.VMEM((S, LANES), BF16)] * 4 + [pltpu.VMEM((S, LANES), F32)] * 9
        + [pltpu.VMEM((3, S, LANES), BF16), pltpu.SemaphoreType.DMA((3,))],
        input_output_aliases={0: 0},
        compiler_params=_cp(("arbitrary",)),
    )(dproj, d_att, att, lse, qr, kr, proj, pos, freq)


def _out_fwd_bwd(ya, att, proj, w_out_bf, x, target, mod, norm_post, norm_att):
    ts = 256

    def body(ya_ref, att_ref, gb_ref, w_ref, x_ref, t_ref, mod_ref, npost_ref, natt_ref,
             gx_ref, dya_ref, datt_ref, dgb_ref, gw_ref, acc_ref):
        i = pl.program_id(0)

        @pl.when(i == 0)
        def _():
            gw_ref[...] = jnp.zeros_like(gw_ref)
            acc_ref[...] = jnp.zeros_like(acc_ref)

        gate = mod_ref[:, 2 * D:3 * D]
        att = att_ref[...]
        gb = gb_ref[...]
        sg = _sigmoid(gb)
        silu = gb * sg
        ybp = att * silu
        yb, ybn, rstd_b = _rms_fwd(ybp, natt_ref[...])
        cat = jnp.concatenate([ya_ref[...], yb.astype(BF16)], axis=1)
        mix = _dot(cat, w_ref[...])
        rn, mn, rstd_m = _rms_fwd(mix, npost_ref[...])
        err = x_ref[...] + gate * rn - t_ref[...]
        dy = err * (1.0 / D)
        gx_ref[...] = dy
        dmix, dnpost = _rms_bwd(dy * gate, mn, rstd_m, npost_ref[...])
        dmb = dmix.astype(BF16)
        gw_ref[...] += _dot_tn(cat, dmb)
        dcat = _dot_nt(dmb, w_ref[...])
        dya_ref[...] = dcat[:, 0:R]
        dybp, dnatt = _rms_bwd(dcat[:, R:2 * R], ybn, rstd_b, natt_ref[...])
        datt_ref[...] = dybp * silu
        dgb_ref[...] = (dybp * att * (sg * (1.0 + gb * (1.0 - sg)))).astype(BF16)
        acc_ref[0:1, :] += jnp.sum(dy * rn, axis=0, keepdims=True)
        acc_ref[1:2, :] += dnpost
        acc_ref[2:3, 0:R] += dnatt
        acc_ref[3:4, :] += jnp.sum(jnp.sum(err * err, axis=1, keepdims=True), axis=0, keepdims=True)

    tile = lambda w: pl.BlockSpec((ts, w), lambda i: (i, 0))
    c0 = lambda shape: pl.BlockSpec(shape, lambda i: (0, 0))
    return pl.pallas_call(
        body, name="out_fwd_bwd", grid=(S // ts,),
        in_specs=[tile(R), tile(R), pl.BlockSpec((ts, R), lambda i: (i, 5)), c0((D, D)), tile(D), tile(D),
                  c0((1, 3 * D)), c0((1, D)), c0((1, R))],
        out_specs=[tile(D), tile(R), tile(R), pl.BlockSpec((ts, R), lambda i: (i, 5)), c0((D, D)), c0((8, D))],
        out_shape=[jax.ShapeDtypeStruct((S, D), F32), jax.ShapeDtypeStruct((S, R), F32),
                   jax.ShapeDtypeStruct((S, R), F32), jax.ShapeDtypeStruct((S, E), BF16),
                   jax.ShapeDtypeStruct((D, D), F32), jax.ShapeDtypeStruct((8, D), F32)],
        compiler_params=_cp(("arbitrary",)),
    )(ya, att, proj, w_out_bf, x, target, mod, norm_post, norm_att)


def _grad_w_in(hb, dproj):
    ts = 512

    def body(h_ref, dp_ref, gw_ref):
        @pl.when(pl.program_id(1) == 0)
        def _():
            gw_ref[...] = jnp.zeros_like(gw_ref)

        gw_ref[...] += _dot_tn(h_ref[...], dp_ref[...])

    return pl.pallas_call(
        body, name="grad_w_in", grid=(NCHIP, S // ts),
        in_specs=[pl.BlockSpec((ts, D), lambda j, s: (s, 0)), pl.BlockSpec((ts, EC), lambda j, s: (s, j))],
        out_specs=pl.BlockSpec((None, D, EC), lambda j, s: (j, 0, 0)),
        out_shape=jax.ShapeDtypeStruct((NCHIP, D, EC), F32),
        compiler_params=_cp(("parallel", "arbitrary")),
    )(hb, dproj)


def _in_proj_bwd(dproj, w_in_bf, x, gx1, mod, norm_pre):
    ts = 256

    def body(dp_ref, w_ref, x_ref, gx1_ref, mod_ref, np_ref, gx_ref, acc_ref):
        @pl.when(pl.program_id(0) == 0)
        def _():
            acc_ref[...] = jnp.zeros_like(acc_ref)

        dh = sum(_dot_nt(dp_ref[:, j * EC:(j + 1) * EC], w_ref[j]) for j in range(NCHIP))
        hp, xn, rstd = _rms_fwd(x_ref[...], np_ref[...])
        dx, dnp = _rms_bwd(dh * (1.0 + mod_ref[:, D:2 * D]), xn, rstd, np_ref[...])
        gx_ref[...] = gx1_ref[...] + dx
        acc_ref[0:1, :] += jnp.sum(dh, axis=0, keepdims=True)
        acc_ref[1:2, :] += jnp.sum(dh * hp, axis=0, keepdims=True)
        acc_ref[2:3, :] += dnp

    tile = lambda w: pl.BlockSpec((ts, w), lambda i: (i, 0))
    c0 = lambda shape: pl.BlockSpec(shape, lambda i: (0, 0))
    return pl.pallas_call(
        body, name="in_proj_bwd", grid=(S // ts,),
        in_specs=[tile(E), pl.BlockSpec((NCHIP, D, EC), lambda i: (0, 0, 0)), tile(D), tile(D), c0((1, 3 * D)),
                  c0((1, D))],
        out_specs=[tile(D), c0((8, D))],
        out_shape=[jax.ShapeDtypeStruct((S, D), F32), jax.ShapeDtypeStruct((8, D), F32)],
        compiler_params=_cp(("arbitrary",)),
    )(dproj, w_in_bf, x, gx1, mod, norm_pre)


def _block_diag(w):
    n, b, _ = w.shape
    eye = jnp.eye(n, dtype=w.dtype)
    return (eye[:, None, :, None] * w[:, :, None, :]).reshape(n * b, n * b)


def _diag_blocks(m):
    n, b = R // HEAD, HEAD
    return jnp.stack([m[h * b:(h + 1) * b, h * b:(h + 1) * b] for h in range(n)])


def _local_step(x, pos, target, mod, w_in_bf, w_out_bf, conv_w, p):
    wa_d = _block_diag(p["w_rg_a"]).astype(BF16)
    wx_d = _block_diag(p["w_rg_x"]).astype(BF16)
    rec_p = (conv_w, p["conv_b"], wa_d, p["b_rg_a"], wx_d, p["b_rg_x"], p["lru_lambda"], p["norm_rec"])
    cos, sin = _rope_table(pos, _rope_freq())
    proj, hb = _in_proj_fwd(x, mod, p["norm_pre"], w_in_bf)
    h_all, ya = _rec_fwd(proj, *rec_p)
    att, qr, kr, lse = _att_fwd(proj, cos, sin)
    gx1, d_ya, d_att, dproj, gw_out, acc_o = _out_fwd_bwd(ya, att, proj, w_out_bf, x, target, mod,
                                                           p["norm_post"], p["norm_att"])
    dproj = _att_bwd(dproj, d_att, att, lse, qr, kr, proj, cos, sin)
    dproj, dwa, dwx, sm = _rec_bwd(dproj, d_ya, proj, h_all, *rec_p)
    gw_in = _grad_w_in(hb, dproj)
    grad_x, acc_i = _in_proj_bwd(dproj, w_in_bf, x, gx1, mod, p["norm_pre"])
    dmod = jnp.concatenate([acc_i[0:1], acc_i[1:2], acc_o[0:1]], axis=1)
    small = dict(norm_pre=acc_i[2:3], norm_post=acc_o[1:2], conv_w=sm[8:12], conv_b=sm[4:5],
                 w_rg_a=_diag_blocks(dwa), b_rg_a=sm[0:1], w_rg_x=_diag_blocks(dwx), b_rg_x=sm[1:2],
                 lru_lambda=sm[2:3], norm_rec=sm[3:4], norm_att=acc_o[2:3, 0:R])
    return acc_o[3, 0], grad_x, gw_in, gw_out, dmod, small


def _me():
    return lax.axis_index("x"), lax.axis_index("y"), lax.axis_index("c")


def _flip(v, bit):
    return 1 - v if bit else v


def _peer(rel):
    x, y, c = _me()
    return (_flip(x, rel & 4), _flip(y, rel & 2), _flip(c, rel & 1))


def _remote(src, dst, send_sem, recv_sem, rel):
    return pltpu.make_async_remote_copy(src_ref=src, dst_ref=dst, send_sem=send_sem, recv_sem=recv_sem,
                                        device_id=_peer(rel), device_id_type=MESH)


def _allgather_rows(row, name):
    w = row.shape[1]

    def body(row_ref, out_ref, send_sems, recv_sems, local_sem):
        x, y, c = _me()
        me = 4 * x + 2 * y + c
        mine = pltpu.make_async_copy(row_ref, out_ref.at[pl.ds(me, 1), :], local_sem)
        mine.start()
        sends = [_remote(row_ref, out_ref.at[pl.ds(me, 1), :], send_sems.at[r - 1], recv_sems.at[r - 1], r)
                 for r in range(1, NDEV)]
        for cp in sends:
            cp.start()
        for r in range(1, NDEV):
            px, py, pc = _peer(r)
            src = 4 * px + 2 * py + pc
            _remote(row_ref, out_ref.at[pl.ds(src, 1), :], send_sems.at[r - 1], recv_sems.at[r - 1], r).wait_recv()
        for cp in sends:
            cp.wait_send()
        mine.wait()

    return pl.pallas_call(
        body, name=name,
        in_specs=[pl.BlockSpec(memory_space=pltpu.VMEM)],
        out_specs=pl.BlockSpec(memory_space=pltpu.VMEM),
        out_shape=jax.ShapeDtypeStruct((NDEV, w), row.dtype),
        scratch_shapes=[pltpu.SemaphoreType.DMA((NDEV - 1,)), pltpu.SemaphoreType.DMA((NDEV - 1,)),
                        pltpu.SemaphoreType.DMA],
        compiler_params=pltpu.CompilerParams(vmem_limit_bytes=VMEM_LIMIT),
    )(row)


def _gather_weight(w, name):
    rows, cols = w.shape
    half = rows // 2

    def body(w_ref, out_ref, send_sems, recv_sems, fsend_sems, frecv_sems):
        x, y, c = _me()
        ci = 2 * x + y
        out_ref[ci] = w_ref[...].astype(BF16)
        r0 = pl.multiple_of(c * half, half)
        r1 = pl.multiple_of((1 - c) * half, half)

        def mine_half(chip, start):
            return out_ref.at[chip, pl.ds(start, half), :]

        sends = [_remote(mine_half(ci, r0), mine_half(ci, r0), send_sems.at[k - 1], recv_sems.at[k - 1], 2 * k)
                 for k in range(1, NCHIP)]
        for cp in sends:
            cp.start()
        fwds = []
        for k in range(1, NCHIP):
            src_chip = ci ^ k
            got = mine_half(src_chip, r0)
            _remote(got, got, send_sems.at[k - 1], recv_sems.at[k - 1], 2 * k).wait_recv()
            fwd = _remote(got, got, fsend_sems.at[k - 1], frecv_sems.at[k - 1], 1)
            fwd.start()
            fwds.append(fwd)
        for k in range(1, NCHIP):
            got = mine_half(ci ^ k, r1)
            _remote(got, got, fsend_sems.at[k - 1], frecv_sems.at[k - 1], 1).wait_recv()
        for cp in sends + fwds:
            cp.wait_send()

    return pl.pallas_call(
        body, name=name,
        in_specs=[pl.BlockSpec(memory_space=pltpu.VMEM)],
        out_specs=pl.BlockSpec(memory_space=pltpu.VMEM),
        out_shape=jax.ShapeDtypeStruct((NCHIP, rows, cols), BF16),
        scratch_shapes=[pltpu.SemaphoreType.DMA((NCHIP - 1,))] * 4,
        compiler_params=pltpu.CompilerParams(vmem_limit_bytes=VMEM_LIMIT),
    )(w)


def _reduce_scatter(g4, name):
    _, rows, cols = g4.shape
    half = rows // 2

    def body(g_ref, out_ref, sib, stage, got, sems_s, sems_r):
        x, y, c = _me()
        ci = 2 * x + y
        r0 = pl.multiple_of(c * half, half)
        r1 = pl.multiple_of((1 - c) * half, half)
        to_sib = _remote(g_ref.at[:, pl.ds(r1, half), :], sib, sems_s.at[0], sems_r.at[0], 1)
        to_sib.start()
        to_sib.wait_recv()
        sends = []
        for k in range(1, NCHIP):
            j = ci ^ k
            stage[k - 1] = (g_ref[j, pl.ds(r0, half), :] + sib[j]).astype(BF16)
            cp = _remote(stage.at[k - 1], got.at[k - 1], sems_s.at[k], sems_r.at[k], 2 * k)
            cp.start()
            sends.append(cp)
        out_ref[pl.ds(r0, half), :] = g_ref[ci, pl.ds(r0, half), :] + sib[ci]
        for k in range(1, NCHIP):
            _remote(stage.at[k - 1], got.at[k - 1], sems_s.at[k], sems_r.at[k], 2 * k).wait_recv()
            out_ref[pl.ds(r0, half), :] += got[k - 1].astype(F32)
        mine = out_ref.at[pl.ds(r0, half), :]
        back = _remote(mine, mine, sems_s.at[NCHIP], sems_r.at[NCHIP], 1)
        back.start()
        theirs = out_ref.at[pl.ds(r1, half), :]
        _remote(theirs, theirs, sems_s.at[NCHIP], sems_r.at[NCHIP], 1).wait_recv()
        for cp in [to_sib] + sends + [back]:
            cp.wait_send()

    return pl.pallas_call(
        body, name=name,
        in_specs=[pl.BlockSpec(memory_space=pltpu.VMEM)],
        out_specs=pl.BlockSpec(memory_space=pltpu.VMEM),
        out_shape=jax.ShapeDtypeStruct((rows, cols), F32),
        scratch_shapes=[pltpu.VMEM((NCHIP, half, cols), F32), pltpu.VMEM((NCHIP - 1, half, cols), BF16),
                        pltpu.VMEM((NCHIP - 1, half, cols), BF16),
                        pltpu.SemaphoreType.DMA((NCHIP + 1,)), pltpu.SemaphoreType.DMA((NCHIP + 1,))],
        compiler_params=pltpu.CompilerParams(vmem_limit_bytes=VMEM_LIMIT),
    )(g4)


def _silu_rows(c_ref):
    cv = c_ref[...]
    sc = cv * _sigmoid(cv)
    return jnp.concatenate([sc, jnp.zeros_like(sc)], axis=0).astype(BF16)


def _ada_fwd(cg, w_ada, b_cols):
    def body(c_ref, w_ref, b_ref, o_ref):
        o_ref[...] = _dot(_silu_rows(c_ref), w_ref[...].astype(BF16))[0:NDEV, :] + b_ref[...]

    return pl.pallas_call(body, name="ada_fwd", out_shape=jax.ShapeDtypeStruct((NDEV, EC), F32),
                          compiler_params=_cp())(cg, w_ada, b_cols)


def _ada_bwd(cg, dmod_cols):
    def body(c_ref, d_ref, o_ref):
        dm = d_ref[...]
        dmb = jnp.concatenate([dm, jnp.zeros_like(dm)], axis=0).astype(BF16)
        o_ref[...] = _dot_tn(_silu_rows(c_ref), dmb)

    return pl.pallas_call(body, name="ada_bwd", out_shape=jax.ShapeDtypeStruct((D, EC), F32),
                          compiler_params=_cp())(cg, dmod_cols)


def _sum_rows(g):
    def body(g_ref, o_ref):
        acc = g_ref[0:1, :]
        for r in range(1, NDEV):
            acc = acc + g_ref[r:r + 1, :]
        o_ref[...] = acc

    return pl.pallas_call(body, name="sum_rows", out_shape=jax.ShapeDtypeStruct((1, g.shape[1]), F32),
                          compiler_params=_cp())(g)


def _adamw(w, g, m, v, name):
    rows, cols = w.shape
    tr = 256 if rows % 256 == 0 else rows

    def body(w_ref, g_ref, m_ref, v_ref, d_ref, nm_ref, nv_ref):
        gv = g_ref[...]
        nm = B1 * m_ref[...] + (1.0 - B1) * gv
        nv = B2 * v_ref[...] + (1.0 - B2) * (gv * gv)
        m_hat = nm / (1.0 - B1 ** STEP)
        v_hat = nv / (1.0 - B2 ** STEP)
        d_ref[...] = (-LR) * (m_hat / (jnp.sqrt(v_hat) + ADAM_EPS) + WD * w_ref[...])
        nm_ref[...] = nm
        nv_ref[...] = nv

    spec = pl.BlockSpec((tr, cols), lambda i: (i, 0))
    return pl.pallas_call(
        body, name=name, grid=(rows // tr,), in_specs=[spec] * 4, out_specs=[spec] * 3,
        out_shape=[jax.ShapeDtypeStruct((rows, cols), F32)] * 3,
        compiler_params=_cp(("parallel",)),
    )(w, g, m, v)


SMALL = (("b_ada", 3 * D), ("norm_pre", D), ("norm_post", D), ("conv_w", 4 * R), ("conv_b", R),
         ("w_rg_a", R * HEAD), ("b_rg_a", R), ("w_rg_x", R * HEAD), ("b_rg_x", R), ("lru_lambda", R),
         ("norm_rec", R), ("norm_att", R))
BIG = ("w_ada", "w_in", "w_out")
WEIGHTS = ("w_ada", "b_ada", "norm_pre", "norm_post", "w_in", "conv_w", "conv_b", "w_rg_a", "b_rg_a", "w_rg_x",
           "b_rg_x", "lru_lambda", "norm_rec", "norm_att", "w_out")


def kernel(x, c, positions, w_ada, b_ada, norm_pre, norm_post, w_in, conv_w, conv_b, w_rg_a, b_rg_a, w_rg_x, b_rg_x, lru_lambda, norm_rec, norm_att, w_out, loss_target, m_w_ada, m_b_ada, m_norm_pre, m_norm_post, m_w_in, m_conv_w, m_conv_b, m_w_rg_a, m_b_rg_a, m_w_rg_x, m_b_rg_x, m_lru_lambda, m_norm_rec, m_norm_att, m_w_out, v_w_ada, v_b_ada, v_norm_pre, v_norm_post, v_w_in, v_conv_w, v_conv_b, v_w_rg_a, v_b_rg_a, v_w_rg_x, v_b_rg_x, v_lru_lambda, v_norm_rec, v_norm_att, v_w_out):
    given = dict(locals())
    wts = {n: given[n] for n in WEIGHTS}
    ms = {n: given["m_" + n] for n in WEIGHTS}
    vs = {n: given["v_" + n] for n in WEIGHTS}
    xi, yi, cc = _me()
    chip = 2 * xi + yi
    me = 2 * chip + cc
    cw_loc = R // NCHIP

    g0 = _allgather_rows(jnp.concatenate([c, conv_w.reshape(1, 4 * cw_loc)], axis=1), "gather_c")
    cg = g0[:, 0:D]
    conv_full = g0[0::2, D:].reshape(NCHIP, 4, cw_loc).transpose(1, 0, 2).reshape(4, R)
    b_cols = lax.dynamic_slice(b_ada, (0, chip * EC), (1, EC))
    mod_part = _ada_fwd(cg, w_ada[0], b_cols)
    g1 = _allgather_rows(mod_part.reshape(1, NDEV * EC), "gather_mod")
    mod = lax.dynamic_index_in_dim(g1[0::2].reshape(NCHIP, NDEV, EC), me, axis=1, keepdims=False).reshape(1, 3 * D)

    w_in_bf = _gather_weight(w_in[0], "gather_w_in")
    w_out_bf = _gather_weight(w_out[0], "gather_w_out").reshape(D, D)

    p = dict(norm_pre=norm_pre, norm_post=norm_post, conv_b=conv_b, b_rg_a=b_rg_a, b_rg_x=b_rg_x,
             lru_lambda=lru_lambda, norm_rec=norm_rec, norm_att=norm_att, w_rg_a=w_rg_a[0], w_rg_x=w_rg_x[0])
    loss_sum, grad_x, gw_in, gw_out, dmod, small = _local_step(
        x[0], positions.reshape(S, 1), loss_target[0], mod, w_in_bf, w_out_bf, conv_full, p)
    loss = lax.psum(loss_sum, ("x", "y", "c")) * (0.5 / D)

    grads = {}
    grads["w_in"] = _reduce_scatter(gw_in, "rs_w_in")
    grads["w_out"] = _reduce_scatter(gw_out.reshape(NCHIP, D // NCHIP, D), "rs_w_out")
    small["b_ada"] = dmod
    row = jnp.concatenate([small[n].reshape(1, k) for n, k in SMALL], axis=1)
    g2 = _allgather_rows(row, "gather_small")
    tot = _sum_rows(g2)
    grads["w_ada"] = _ada_bwd(cg, lax.dynamic_slice(g2, (0, chip * EC), (NDEV, EC)))
    off = 0
    for n, k in SMALL:
        grads[n] = tot[:, off:off + k]
        off += k
    grads["conv_w"] = lax.dynamic_slice(grads["conv_w"].reshape(4, R), (0, chip * cw_loc), (4, cw_loc))

    delta, new_m, new_v = {}, {}, {}
    for n in BIG:
        delta[n], new_m[n], new_v[n] = _adamw(wts[n][0], grads[n], ms[n][0], vs[n][0], "adamw_" + n)
    pack = lambda d: jnp.concatenate([d[n].reshape(1, -1) for n, _ in SMALL], axis=1).reshape(-1, LANES)
    pd, pm, pv = _adamw(pack(wts), pack(grads), pack(ms), pack(vs), "adamw_small")
    off = 0
    for n, _ in SMALL:
        k = wts[n].size
        for dst, src in ((delta, pd), (new_m, pm), (new_v, pv)):
            dst[n] = src.reshape(1, -1)[:, off:off + k]
        off += k
    out = lambda d: [d[n].reshape(wts[n].shape) for n in WEIGHTS]
    return (loss, grad_x.reshape(x.shape), *out(grads), *out(delta), *out(new_m), *out(new_v))
```

```python
import functools

import numpy as np
import jax
import jax.numpy as jnp
from jax import lax
from jax.experimental import pallas as pl
from jax.experimental.pallas import tpu as pltpu

F32 = jnp.float32
BF16 = jnp.bfloat16

S = 2048
D = 1024
E = 3072
R = 512
NDEV = 8
NCHIP = 4
EC = 768
LRU_C = 8.0
EPS = 1e-6
NEG = -1e30
HEAD = 64
BLK = 128
PATTERNS = (1, 4, 16)
ROPE_THETA = 10000.0
LANES = 128
VMEM_LIMIT = 56 * 1024 * 1024

B1, B2, LR, WD, ADAM_EPS, STEP = 0.9, 0.999, 0.001, 0.01, 1e-8, 10
MESH = pl.DeviceIdType.MESH


def _cp(sem=None, **kw):
    return pltpu.CompilerParams(dimension_semantics=sem, vmem_limit_bytes=VMEM_LIMIT, **kw)


def _dot(a, b):
    return jnp.dot(a, b, preferred_element_type=F32)


def _dot_nt(a, b):
    return lax.dot_general(a, b, (((1,), (1,)), ((), ())), preferred_element_type=F32)


def _dot_tn(a, b):
    return lax.dot_general(a, b, (((0,), (0,)), ((), ())), preferred_element_type=F32)


def _sigmoid(x):
    return 1.0 / (1.0 + jnp.exp(-x))


def _expm1(x):
    poly = x * (1.0 + x * (0.5 + x * (1.0 / 6 + x * (1.0 / 24 + x * (1.0 / 120 + x * (1.0 / 720))))))
    return jnp.where(jnp.abs(x) < 0.3, poly, jnp.exp(x) - 1.0)


def _rms_fwd(v, g):
    rstd = lax.rsqrt(jnp.mean(v * v, axis=-1, keepdims=True) + EPS)
    vn = v * rstd
    return vn * g, vn, rstd


def _rms_bwd(dy, vn, rstd, g):
    dvn = dy * g
    dv = rstd * (dvn - vn * jnp.mean(dvn * vn, axis=-1, keepdims=True))
    return dv, jnp.sum(dy * vn, axis=0, keepdims=True)


def _in_proj_fwd(x, mod, norm_pre, w_in_bf):
    ts = 256

    def body(x_ref, mod_ref, np_ref, w_ref, proj_ref, hb_ref):
        hp, _, _ = _rms_fwd(x_ref[...], np_ref[...])
        h = hp * (1.0 + mod_ref[:, D:2 * D]) + mod_ref[:, 0:D]
        hb = h.astype(BF16)
        hb_ref[...] = hb
        for j in range(NCHIP):
            proj_ref[:, j * EC:(j + 1) * EC] = _dot(hb, w_ref[j])

    return pl.pallas_call(
        body, name="in_proj_fwd", grid=(S // ts,),
        in_specs=[pl.BlockSpec((ts, D), lambda i: (i, 0)), pl.BlockSpec((1, 3 * D), lambda i: (0, 0)),
                  pl.BlockSpec((1, D), lambda i: (0, 0)), pl.BlockSpec((NCHIP, D, EC), lambda i: (0, 0, 0))],
        out_specs=[pl.BlockSpec((ts, E), lambda i: (i, 0)), pl.BlockSpec((ts, D), lambda i: (i, 0))],
        out_shape=[jax.ShapeDtypeStruct((S, E), F32), jax.ShapeDtypeStruct((S, D), BF16)],
        compiler_params=_cp(("parallel",)),
    )(x, mod, norm_pre, w_in_bf)


RT = 256


def _shift_down(cur, prev8, j, row):
    if j == 0:
        return cur
    top = jnp.tile(pltpu.roll(prev8, j, 0), (RT // 8, 1))
    return jnp.where(row >= j, pltpu.roll(cur, j, 0), top)


def _shift_up(cur, next8, j, row):
    if j == 0:
        return cur
    bot = jnp.tile(pltpu.roll(next8, 8 - j, 0), (RT // 8, 1))
    return jnp.where(row < RT - j, pltpu.roll(cur, RT - j, 0), bot)


def _rec_gates(xp, xprev8, row, cw_ref, cb_ref, wa_ref, ba_ref, wx_ref, bx_ref, lam_ref):
    xa = cb_ref[...] + sum(cw_ref[3 - j:4 - j, :] * _shift_down(xp, xprev8, j, row) for j in range(4))
    xab = xa.astype(BF16)
    r = _sigmoid(_dot(xab, wa_ref[...]) + ba_ref[...])
    ig = _sigmoid(_dot(xab, wx_ref[...]) + bx_ref[...])
    nl = -lam_ref[...]
    sp = jnp.maximum(nl, 0.0) + jnp.log1p(jnp.exp(-jnp.abs(nl)))
    la = (-LRU_C) * r * sp
    a = jnp.exp(la)
    mult = jnp.sqrt(-_expm1(2.0 * la))
    return dict(xa=xa, xab=xab, r=r, ig=ig, sp=sp, la=la, a=a, mult=mult)


def _scan_fwd(a, u, row):
    sh = 1
    while sh < RT:
        a_s = jnp.where(row >= sh, pltpu.roll(a, sh, 0), 1.0)
        u_s = jnp.where(row >= sh, pltpu.roll(u, sh, 0), 0.0)
        u = a * u_s + u
        a = a * a_s
        sh *= 2
    return a, u


def _scan_bwd(al, g, row):
    sh = 1
    while sh < RT:
        al_s = jnp.where(row < RT - sh, pltpu.roll(al, RT - sh, 0), 1.0)
        g_s = jnp.where(row < RT - sh, pltpu.roll(g, RT - sh, 0), 0.0)
        g = g + al * g_s
        al = al * al_s
        sh *= 2
    return g


def _rec_fwd(proj, conv_w, conv_b, wa_d, ba, wx_d, bx, lam, norm_rec):
    nt = S // RT

    def body(p_ref, cw_ref, cb_ref, wa_ref, ba_ref, wx_ref, bx_ref, lam_ref, nr_ref,
             h_ref, ya_ref, prev8, hc):
        i = pl.program_id(0)

        @pl.when(i == 0)
        def _():
            prev8[...] = jnp.zeros_like(prev8)
            hc[...] = jnp.zeros_like(hc)

        row = lax.broadcasted_iota(jnp.int32, (RT, R), 0)
        xp = p_ref[:, 0:R]
        ga = p_ref[:, R:2 * R]
        f = _rec_gates(xp, prev8[...], row, cw_ref, cb_ref, wa_ref, ba_ref, wx_ref, bx_ref, lam_ref)
        u = f["mult"] * (f["ig"] * f["xa"])
        acum, hh = _scan_fwd(f["a"], u, row)
        h = hh + acum * hc[0:1, :]
        h_ref[...] = h
        hc[0:1, :] = h_ref[RT - 1:RT, :]
        prev8[...] = p_ref[RT - 8:RT, 0:R]
        yp = h * (ga * _sigmoid(ga))
        ya, _, _ = _rms_fwd(yp, nr_ref[...])
        ya_ref[...] = ya.astype(BF16)

    row1 = lambda n: pl.BlockSpec((1, n), lambda i: (0, 0))
    return pl.pallas_call(
        body, name="rec_fwd", grid=(nt,),
        in_specs=[pl.BlockSpec((RT, 2 * R), lambda i: (i, 0)), pl.BlockSpec((4, R), lambda i: (0, 0)), row1(R),
                  pl.BlockSpec((R, R), lambda i: (0, 0)), row1(R), pl.BlockSpec((R, R), lambda i: (0, 0)), row1(R),
                  row1(R), row1(R)],
        out_specs=[pl.BlockSpec((RT, R), lambda i: (i, 0)), pl.BlockSpec((RT, R), lambda i: (i, 0))],
        out_shape=[jax.ShapeDtypeStruct((S, R), F32), jax.ShapeDtypeStruct((S, R), BF16)],
        scratch_shapes=[pltpu.VMEM((8, R), F32), pltpu.VMEM((8, R), F32)],
        compiler_params=_cp(("arbitrary",)),
    )(proj, conv_w, conv_b, wa_d, ba, wx_d, bx, lam, norm_rec)


def _rec_bwd(dproj, d_ya, proj, h_all, conv_w, conv_b, wa_d, ba, wx_d, bx, lam, norm_rec):
    nt = S // RT

    def body(dp_in, dya_ref, p_ref, pprev_ref, h_ref, hprev_ref, cw_ref, cb_ref, wa_ref, ba_ref, wx_ref, bx_ref,
             lam_ref, nr_ref, dp_ref, dwa_ref, dwx_ref, sm_ref, nxt8, cg):
        i = pl.program_id(0)
        ti = nt - 1 - i

        @pl.when(i == 0)
        def _():
            nxt8[...] = jnp.zeros_like(nxt8)
            cg[...] = jnp.zeros_like(cg)
            dwa_ref[...] = jnp.zeros_like(dwa_ref)
            dwx_ref[...] = jnp.zeros_like(dwx_ref)
            sm_ref[...] = jnp.zeros_like(sm_ref)

        row = lax.broadcasted_iota(jnp.int32, (RT, R), 0)
        first = (ti > 0).astype(F32)
        xprev8 = pprev_ref[...] * first
        hprev8 = hprev_ref[...] * first
        xp = p_ref[:, 0:R]
        ga = p_ref[:, R:2 * R]
        f = _rec_gates(xp, xprev8, row, cw_ref, cb_ref, wa_ref, ba_ref, wx_ref, bx_ref, lam_ref)
        xa, r, ig, a, mult = f["xa"], f["r"], f["ig"], f["a"], f["mult"]
        h = h_ref[...]
        sg = _sigmoid(ga)
        gate = ga * sg
        yp = h * gate
        _, ypn, rstd = _rms_fwd(yp, nr_ref[...])
        d_yp, dnr = _rms_bwd(dya_ref[...], ypn, rstd, nr_ref[...])
        d_ga = d_yp * h * (sg * (1.0 + ga * (1.0 - sg)))
        dh = d_yp * gate + jnp.where(row == RT - 1, cg[0:1, :], 0.0)
        al = jnp.where(row < RT - 1, pltpu.roll(a, RT - 1, 0), 0.0)
        g = _scan_bwd(al, dh, row)
        cg[0:1, :] = jnp.sum(jnp.where(row == 0, a * g, 0.0), axis=0, keepdims=True)
        h_m1 = _shift_down(h, hprev8, 1, row)
        da = g * h_m1
        ix = ig * xa
        d_mult = g * ix
        d_ig = g * mult * xa
        d_xa = g * mult * ig
        d_la = da * a - d_mult * (a * a) / mult
        d_r = d_la * ((-LRU_C) * f["sp"])
        dsp = jnp.sum(d_la * ((-LRU_C) * r), axis=0, keepdims=True)
        dlam = dsp * (-_sigmoid(-lam_ref[...]))
        d_za = d_r * r * (1.0 - r)
        d_zx = d_ig * ig * (1.0 - ig)
        dzab = d_za.astype(BF16)
        dzxb = d_zx.astype(BF16)
        dwa_ref[...] += _dot_tn(f["xab"], dzab)
        dwx_ref[...] += _dot_tn(f["xab"], dzxb)
        d_xa = d_xa + _dot_nt(dzab, wa_ref[...]) + _dot_nt(dzxb, wx_ref[...])
        d_xp = sum(cw_ref[3 - j:4 - j, :] * _shift_up(d_xa, nxt8[...], j, row) for j in range(4))
        dcw = [jnp.sum(d_xa * _shift_down(xp, xprev8, 3 - k, row), axis=0, keepdims=True) for k in range(4)]
        dp_ref[:, 0:R] = d_xp.astype(BF16)
        dp_ref[:, R:2 * R] = d_ga.astype(BF16)
        dp8 = d_xa[0:8, :]
        nxt8[...] = dp8
        sm_ref[0:1, :] += jnp.sum(d_za, axis=0, keepdims=True)
        sm_ref[1:2, :] += jnp.sum(d_zx, axis=0, keepdims=True)
        sm_ref[2:3, :] += dlam
        sm_ref[3:4, :] += dnr
        sm_ref[4:5, :] += jnp.sum(d_xa, axis=0, keepdims=True)
        for k in range(4):
            sm_ref[8 + k:9 + k, :] += dcw[k]

    c0 = lambda shape: pl.BlockSpec(shape, lambda i: (0, 0))
    rev = lambda i: nt - 1 - i
    prev8 = lambda i: (jnp.maximum((nt - 1 - i) * (RT // 8) - 1, 0), 0)
    return pl.pallas_call(
        body, name="rec_bwd", grid=(nt,),
        in_specs=[pl.BlockSpec(memory_space=pl.ANY),
                  pl.BlockSpec((RT, R), lambda i: (rev(i), 0)),
                  pl.BlockSpec((RT, 2 * R), lambda i: (rev(i), 0)), pl.BlockSpec((8, R), prev8),
                  pl.BlockSpec((RT, R), lambda i: (rev(i), 0)), pl.BlockSpec((8, R), prev8),
                  c0((4, R)), c0((1, R)), c0((R, R)), c0((1, R)), c0((R, R)), c0((1, R)), c0((1, R)), c0((1, R))],
        out_specs=[pl.BlockSpec((RT, 2 * R), lambda i: (rev(i), 0)), c0((R, R)), c0((R, R)), c0((16, R))],
        out_shape=[jax.ShapeDtypeStruct((S, E), BF16), jax.ShapeDtypeStruct((R, R), F32),
                   jax.ShapeDtypeStruct((R, R), F32), jax.ShapeDtypeStruct((16, R), F32)],
        scratch_shapes=[pltpu.VMEM((8, R), F32), pltpu.VMEM((8, R), F32)],
        input_output_aliases={0: 0},
        compiler_params=_cp(("arbitrary",)),
    )(dproj, d_ya, proj, proj, h_all, h_all, conv_w, conv_b, wa_d, ba, wx_d, bx, lam, norm_rec)


NPAIR = R // LANES
QB, KB, VB, GB = 2 * R // LANES, 3 * R // LANES, 4 * R // LANES, 5 * R // LANES


def _rope_freq():
    half = HEAD // 2
    inv = np.float32(ROPE_THETA) ** (-(np.arange(half, dtype=np.float32) / np.float32(half)))
    return jnp.asarray(np.tile(inv.astype(np.float32), LANES // half)[None, :])


def _rot_half(x, first):
    return jnp.where(first, -pltpu.roll(x, LANES - HEAD // 2, 1), pltpu.roll(x, HEAD // 2, 1))


def _cos_sin(pos_ref, freq_ref):
    ang = pos_ref[...].astype(F32) * freq_ref[...]
    return jnp.cos(ang), jnp.sin(ang)


def _deint(src_ref, dst_ref, d):
    n = S // d
    for r in range(d):
        v = src_ref[pl.ds(r, n, stride=d), :] if d > 1 else src_ref[...]
        dst_ref[r * n:(r + 1) * n, :] = v.astype(dst_ref.dtype)


def _reint(src_ref, dst_ref, d, accumulate):
    n = S // d
    for r in range(d):
        idx = (pl.ds(r, n, stride=d), slice(None)) if d > 1 else (slice(None), slice(None))
        v = src_ref[r * n:(r + 1) * n, :]
        if accumulate:
            dst_ref[idx] = dst_ref[idx] + v
        else:
            dst_ref[idx] = v


def _blk_masks(b, nb):
    qi = lax.broadcasted_iota(jnp.int32, (BLK, BLK), 0)
    ki = lax.broadcasted_iota(jnp.int32, (BLK, BLK), 1)
    has_prev = lax.rem(b, nb) != 0
    return ki <= qi, jnp.logical_and(ki >= qi, has_prev)


def _rope_table(pos, freq):
    def body(pos_ref, freq_ref, cos_ref, sin_ref):
        cos_ref[...], sin_ref[...] = _cos_sin(pos_ref, freq_ref)

    return pl.pallas_call(body, name="rope_table", out_shape=[jax.ShapeDtypeStruct((S, LANES), F32)] * 2,
                          compiler_params=_cp())(pos, freq)


def _deint_heads(src_ref, dst0, dst1, d):
    n = S // d
    hm0 = lax.broadcasted_iota(jnp.int32, (n, LANES), 1) < HEAD
    for r in range(d):
        v = src_ref[pl.ds(r, n, stride=d), :] if d > 1 else src_ref[...]
        dst0[r * n:(r + 1) * n, :] = jnp.where(hm0, v, 0.0).astype(BF16)
        dst1[r * n:(r + 1) * n, :] = jnp.where(hm0, 0.0, v).astype(BF16)


def _reint_prev(src_ref, dst_ref, d):
    n = S // d
    if n == BLK:
        return
    for r in range(d):
        idx = (pl.ds(r, n - BLK, stride=d), slice(None)) if d > 1 else (slice(0, n - BLK), slice(None))
        dst_ref[idx] = dst_ref[idx] + src_ref[r * n + BLK:(r + 1) * n, :]


def _pair_masks():
    qi = lax.broadcasted_iota(jnp.int32, (BLK, 2 * BLK), 0)
    ki = lax.broadcasted_iota(jnp.int32, (BLK, 2 * BLK), 1) & (BLK - 1)
    return ki <= qi, ki >= qi


def _two(ref0, ref1, st, axis):
    return jnp.concatenate([ref0[pl.ds(st, BLK), :], ref1[pl.ds(st, BLK), :]], axis=axis)


ATT_UNROLL = 4


def _att_fwd(proj, cos, sin, w_out):
    def body(q_ref, k_ref, v_ref, cos_ref, sin_ref, w_ref, att_ref, qr_ref, kr_ref, lse_ref, wbf_ref,
             qd, kd0, kd1, vd0, vd1, od, ld, on, ln, wbuf, *wsems):
        wg = _WeightGather(w_ref, wbuf, *wsems)
        pl.when(pl.program_id(0) == 0)(wg.start)
        pl.when(pl.program_id(0) == 1)(wg.forward)
        lane = lax.broadcasted_iota(jnp.int32, (S, LANES), 1)
        first = (lane & (HEAD // 2)) == 0
        cos, sin = cos_ref[...], sin_ref[...]
        q = q_ref[...]
        k = k_ref[...]
        qr_ref[...] = (q * cos + _rot_half(q, first) * sin) * (HEAD ** -0.5)
        kr_ref[...] = k * cos + _rot_half(k, first) * sin
        hm0 = lax.broadcasted_iota(jnp.int32, (BLK, LANES), 1) < HEAD
        top = lax.broadcasted_iota(jnp.int32, (2 * BLK, LANES), 0) < BLK
        ones2 = (top == (lax.broadcasted_iota(jnp.int32, (2 * BLK, LANES), 1) < HEAD)).astype(BF16)
        mc2, mp2 = _pair_masks()

        for pi, d in enumerate(PATTERNS):
            nb = S // d // BLK
            _deint(qr_ref, qd, d)
            _deint_heads(kr_ref, kd0, kd1, d)
            _deint_heads(v_ref, vd0, vd1, d)

            def blk(b, carry):
                st = pl.multiple_of(b * BLK, BLK)
                qb = qd[pl.ds(st, BLK), :]
                sc = jnp.where(mc2, _dot_nt(qb, _two(kd0, kd1, st, 0)), NEG)
                mx = sc
                if nb > 1:
                    stp = pl.multiple_of(jnp.maximum(b - 1, 0) * BLK, BLK)
                    mp = jnp.logical_and(mp2, lax.rem(b, nb) != 0)
                    sp = jnp.where(mp, _dot_nt(qb, _two(kd0, kd1, stp, 0)), NEG)
                    mx = jnp.maximum(sc, sp)
                m0 = jnp.max(mx[:, 0:BLK], axis=1, keepdims=True)
                m1 = jnp.max(mx[:, BLK:2 * BLK], axis=1, keepdims=True)
                mf = jnp.concatenate([jnp.broadcast_to(m0, (BLK, BLK)), jnp.broadcast_to(m1, (BLK, BLK))], axis=1)
                o = _dot(jnp.exp(sc - mf).astype(BF16), jnp.concatenate([_two(vd0, vd1, st, 0), ones2], axis=1))
                if nb > 1:
                    o = o + _dot(jnp.exp(sp - mf).astype(BF16), jnp.concatenate([_two(vd0, vd1, stp, 0), ones2], axis=1))
                l = o[:, LANES:2 * LANES]
                od[pl.ds(st, BLK), :] = o[:, 0:LANES] / l
                ld[pl.ds(st, BLK), :] = jnp.where(hm0, m0, m1) + jnp.log(l)
                return carry

            lax.fori_loop(0, S // BLK, blk, 0, unroll=ATT_UNROLL)
            _reint(od, on.at[pi], d, False)
            _reint(ld, ln.at[pi], d, False)

        l0, l1, l2 = ln[0], ln[1], ln[2]
        m = jnp.maximum(jnp.maximum(l0, l1), l2)
        e0, e1, e2 = jnp.exp(l0 - m), jnp.exp(l1 - m), jnp.exp(l2 - m)
        den = e0 + e1 + e2
        att_ref[...] = (e0 * on[0] + e1 * on[1] + e2 * on[2]) / den
        lse_ref[...] = m + jnp.log(den)

        @pl.when(pl.program_id(0) == NPAIR - 1)
        def _():
            wg.finish()
            wbf_ref[...] = wbuf[...]

    col = lambda c0: pl.BlockSpec((S, LANES), lambda p: (0, c0 + p))
    out = pl.BlockSpec((S, LANES), lambda p: (0, p))
    tab = pl.BlockSpec((S, LANES), lambda p: (0, 0))
    vm = pl.BlockSpec(memory_space=pltpu.VMEM)
    return pl.pallas_call(
        body, name="att_fwd", grid=(NPAIR,),
        in_specs=[col(QB), col(KB), col(VB), tab, tab, vm],
        out_specs=[out, out, out, out, vm],
        out_shape=[jax.ShapeDtypeStruct((S, R), F32)] * 4 + [jax.ShapeDtypeStruct((NCHIP,) + w_out.shape, BF16)],
        scratch_shapes=[pltpu.VMEM((S, LANES), BF16)] * 5 + [pltpu.VMEM((S, LANES), F32)] * 2
        + [pltpu.VMEM((3, S, LANES), F32)] * 2 + [pltpu.VMEM((NCHIP,) + w_out.shape, BF16)] + _WeightGather.SEMS,
        compiler_params=_cp(("arbitrary",)),
    )(proj, proj, proj, cos, sin, w_out)


def _att_bwd(dproj, d_att, att, lse, qr, kr, proj, cos, sin, gw_out4):
    out_units = [(j, j, 0) for j in range(NCHIP)]

    def body(dp_in, do_ref, o_ref, lse_ref, qr_ref, kr_ref, v_ref, cos_ref, sin_ref, gw_ref, dp_ref, gout_ref,
             qd, kd0, kd1, vd0, vd1, dod, lb0d, lb1d, dl0d, dl1d, dqd, dkcd, dkpd, dvcd, dvpd,
             dqn, dkn, dvn, lb0n, lb1n, dl0n, dl1n, stage, sems, gred, *rs_scratch):
        p = pl.program_id(0)
        rs = _ReduceScatter(gw_ref, gred, out_units, *rs_scratch)
        for step, piece in enumerate((rs.start_halves, rs.send_partials, rs.reduce_owned)):
            pl.when(p == step)(piece)

        @pl.when(p == NPAIR - 1)
        def _():
            rs.finish()
            gout_ref[...] = gred[...]
        hms = lax.broadcasted_iota(jnp.int32, (S, LANES), 1) < HEAD
        prod = do_ref[...] * o_ref[...]
        dl0n[...] = jnp.broadcast_to(jnp.sum(jnp.where(hms, prod, 0.0), axis=1, keepdims=True), (S, LANES))
        dl1n[...] = jnp.broadcast_to(jnp.sum(jnp.where(hms, 0.0, prod), axis=1, keepdims=True), (S, LANES))
        lse = lse_ref[...]
        lsw = pltpu.roll(lse, HEAD, 1)
        lb0n[...] = jnp.where(hms, lse, lsw)
        lb1n[...] = jnp.where(hms, lsw, lse)
        dqn[...] = jnp.zeros_like(dqn)
        dkn[...] = jnp.zeros_like(dkn)
        dvn[...] = jnp.zeros_like(dvn)
        hm0 = lax.broadcasted_iota(jnp.int32, (BLK, LANES), 1) < HEAD
        mc2, mp2 = _pair_masks()

        for d in PATTERNS:
            nb = S // d // BLK
            _deint(qr_ref, qd, d)
            _deint_heads(kr_ref, kd0, kd1, d)
            _deint_heads(v_ref, vd0, vd1, d)
            _deint(do_ref, dod, d)
            for src, dst in ((lb0n, lb0d), (lb1n, lb1d), (dl0n, dl0d), (dl1n, dl1d)):
                _deint(src, dst, d)

            def blk(b, carry):
                st = pl.multiple_of(b * BLK, BLK)
                qb, dob = qd[pl.ds(st, BLK), :], dod[pl.ds(st, BLK), :]
                lb, dl = _two(lb0d, lb1d, st, 1), _two(dl0d, dl1d, st, 1)

                def side(stk, mask):
                    k2, v2 = _two(kd0, kd1, stk, 0), _two(vd0, vd1, stk, 0)
                    pk = jnp.where(mask, jnp.exp(_dot_nt(qb, k2) - lb), 0.0)
                    ds = (pk * (_dot_nt(dob, v2) - dl)).astype(BF16)
                    rk, rv = _dot_tn(ds, qb), _dot_tn(pk.astype(BF16), dob)
                    return (_dot(ds, k2), jnp.where(hm0, rk[0:BLK], rk[BLK:2 * BLK]),
                            jnp.where(hm0, rv[0:BLK], rv[BLK:2 * BLK]))

                dq, dkc, dvc = side(st, mc2)
                if nb > 1:
                    stp = pl.multiple_of(jnp.maximum(b - 1, 0) * BLK, BLK)
                    dqp, dkp, dvp = side(stp, jnp.logical_and(mp2, lax.rem(b, nb) != 0))
                    dq = dq + dqp
                    dkpd[pl.ds(st, BLK), :] = dkp
                    dvpd[pl.ds(st, BLK), :] = dvp
                dqd[pl.ds(st, BLK), :] = dq
                dkcd[pl.ds(st, BLK), :] = dkc
                dvcd[pl.ds(st, BLK), :] = dvc
                return carry

            lax.fori_loop(0, S // BLK, blk, 0, unroll=ATT_UNROLL)
            _reint(dqd, dqn, d, True)
            _reint(dkcd, dkn, d, True)
            _reint(dvcd, dvn, d, True)
            _reint_prev(dkpd, dkn, d)
            _reint_prev(dvpd, dvn, d)

        lane = lax.broadcasted_iota(jnp.int32, (S, LANES), 1)
        first = (lane & (HEAD // 2)) == 0
        cos, sin = cos_ref[...], sin_ref[...]
        dq = dqn[...] * (HEAD ** -0.5)
        dk = dkn[...]
        stage[0] = (dq * cos - _rot_half(dq, first) * sin).astype(BF16)
        stage[1] = (dk * cos - _rot_half(dk, first) * sin).astype(BF16)
        stage[2] = dvn[...].astype(BF16)
        copies = [pltpu.make_async_copy(stage.at[j], dp_ref.at[:, pl.ds((2 + j) * R + p * LANES, LANES)], sems.at[j])
                  for j in range(3)]
        for cp in copies:
            cp.start()
        for cp in copies:
            cp.wait()

    blk = pl.BlockSpec((S, LANES), lambda p: (0, p))
    tab = pl.BlockSpec((S, LANES), lambda p: (0, 0))
    vm = pl.BlockSpec(memory_space=pltpu.VMEM)
    _, orows, ocols = gw_out4.shape
    return pl.pallas_call(
        body, name="att_bwd", grid=(NPAIR,),
        in_specs=[pl.BlockSpec(memory_space=pl.ANY), blk, blk, blk, blk, blk,
                  pl.BlockSpec((S, LANES), lambda p: (0, VB + p)), tab, tab, vm],
        out_specs=[pl.BlockSpec(memory_space=pl.ANY), vm],
        out_shape=[jax.ShapeDtypeStruct((S, E), BF16), jax.ShapeDtypeStruct((orows, ocols), F32)],
        scratch_shapes=[pltpu.VMEM((S, LANES), BF16)] * 6 + [pltpu.VMEM((S, LANES), F32)] * 16
        + [pltpu.VMEM((3, S, LANES), BF16), pltpu.SemaphoreType.DMA((3,)), pltpu.VMEM((orows, ocols), F32)]
        + _ReduceScatter.scratch(NCHIP, orows, ocols, 1),
        input_output_aliases={0: 0},
        compiler_params=_cp(("arbitrary",)),
    )(dproj, d_att, att, lse, qr, kr, proj, cos, sin, gw_out4)


def _att_fwd_old(proj, pos, freq):
    def body(q_ref, k_ref, v_ref, pos_ref, freq_ref, att_ref, qr_ref, kr_ref, lse_ref,
             qd, kd, vd, od, ld, on, ln):
        lane = lax.broadcasted_iota(jnp.int32, (S, LANES), 1)
        first = (lane & (HEAD // 2)) == 0
        cos, sin = _cos_sin(pos_ref, freq_ref)
        q = q_ref[...]
        k = k_ref[...]
        qr_ref[...] = (q * cos + _rot_half(q, first) * sin) * (HEAD ** -0.5)
        kr_ref[...] = k * cos + _rot_half(k, first) * sin
        hm0 = lax.broadcasted_iota(jnp.int32, (BLK, LANES), 1) < HEAD

        for pi, d in enumerate(PATTERNS):
            nb = S // d // BLK
            _deint(qr_ref, qd, d)
            _deint(kr_ref, kd, d)
            _deint(v_ref, vd, d)

            def blk(b, carry):
                st = pl.multiple_of(b * BLK, BLK)
                stp = pl.multiple_of(jnp.maximum(b - 1, 0) * BLK, BLK)
                mc, mp = _blk_masks(b, nb)
                qb = qd[pl.ds(st, BLK), :]
                kc, kp = kd[pl.ds(st, BLK), :], kd[pl.ds(stp, BLK), :]
                vc, vp = vd[pl.ds(st, BLK), :], vd[pl.ds(stp, BLK), :]
                outs, lses = [], []
                for hm in (hm0, jnp.logical_not(hm0)):
                    qm = jnp.where(hm, qb, jnp.zeros_like(qb))
                    sc = jnp.where(mc, _dot_nt(qm, kc), NEG)
                    sp = jnp.where(mp, _dot_nt(qm, kp), NEG)
                    m = jnp.maximum(jnp.max(sc, axis=1, keepdims=True), jnp.max(sp, axis=1, keepdims=True))
                    pc, pp = jnp.exp(sc - m), jnp.exp(sp - m)
                    l = jnp.sum(pc, axis=1, keepdims=True) + jnp.sum(pp, axis=1, keepdims=True)
                    o = _dot(pc.astype(BF16), vc) + _dot(pp.astype(BF16), vp)
                    outs.append(o / l)
                    lses.append(m + jnp.log(l))
                od[pl.ds(st, BLK), :] = jnp.where(hm0, outs[0], outs[1])
                ld[pl.ds(st, BLK), :] = jnp.where(hm0, lses[0], lses[1])
                return carry

            lax.fori_loop(0, S // BLK, blk, 0)
            _reint(od, on.at[pi], d, False)
            _reint(ld, ln.at[pi], d, False)

        l0, l1, l2 = ln[0], ln[1], ln[2]
        m = jnp.maximum(jnp.maximum(l0, l1), l2)
        e0, e1, e2 = jnp.exp(l0 - m), jnp.exp(l1 - m), jnp.exp(l2 - m)
        den = e0 + e1 + e2
        att_ref[...] = (e0 * on[0] + e1 * on[1] + e2 * on[2]) / den
        lse_ref[...] = m + jnp.log(den)

    col = lambda c0: pl.BlockSpec((S, LANES), lambda p: (0, c0 + p))
    out = pl.BlockSpec((S, LANES), lambda p: (0, p))
    return pl.pallas_call(
        body, name="att_fwd", grid=(NPAIR,),
        in_specs=[col(QB), col(KB), col(VB), pl.BlockSpec((S, 1), lambda p: (0, 0)),
                  pl.BlockSpec((1, LANES), lambda p: (0, 0))],
        out_specs=[out, out, out, out],
        out_shape=[jax.ShapeDtypeStruct((S, R), F32)] * 4,
        scratch_shapes=[pltpu.VMEM((S, LANES), BF16)] * 3 + [pltpu.VMEM((S, LANES), F32)] * 2
        + [pltpu.VMEM((3, S, LANES), F32)] * 2,
        compiler_params=_cp(("parallel",)),
    )(proj, proj, proj, pos, freq)


def _att_bwd_old(dproj, d_att, att, lse, qr, kr, proj, pos, freq):
    def body(dp_in, do_ref, o_ref, lse_ref, qr_ref, kr_ref, v_ref, pos_ref, freq_ref, dp_ref,
             qd, kd, vd, dod, lsd, prd, dqd, dkd, dvd, dqn, dkn, dvn, prn, stage, sems):
        p = pl.program_id(0)
        prn[...] = do_ref[...] * o_ref[...]
        dqn[...] = jnp.zeros_like(dqn)
        dkn[...] = jnp.zeros_like(dkn)
        dvn[...] = jnp.zeros_like(dvn)
        hm0 = lax.broadcasted_iota(jnp.int32, (BLK, LANES), 1) < HEAD

        for d in PATTERNS:
            nb = S // d // BLK
            _deint(qr_ref, qd, d)
            _deint(kr_ref, kd, d)
            _deint(v_ref, vd, d)
            _deint(do_ref, dod, d)
            _deint(lse_ref, lsd, d)
            _deint(prn, prd, d)
            dkd[...] = jnp.zeros_like(dkd)
            dvd[...] = jnp.zeros_like(dvd)

            def blk(b, carry):
                st = pl.multiple_of(b * BLK, BLK)
                stp = pl.multiple_of(jnp.maximum(b - 1, 0) * BLK, BLK)
                mc, mp = _blk_masks(b, nb)
                qb, dob = qd[pl.ds(st, BLK), :], dod[pl.ds(st, BLK), :]
                kc, kp = kd[pl.ds(st, BLK), :], kd[pl.ds(stp, BLK), :]
                vc, vp = vd[pl.ds(st, BLK), :], vd[pl.ds(stp, BLK), :]
                lsb, prb = lsd[pl.ds(st, BLK), :], prd[pl.ds(st, BLK), :]
                dqs = []
                dkc = dkp = dvc = dvp = None
                for hm in (hm0, jnp.logical_not(hm0)):
                    qm = jnp.where(hm, qb, jnp.zeros_like(qb))
                    dom = jnp.where(hm, dob, jnp.zeros_like(dob))
                    lh = jnp.max(jnp.where(hm, lsb, -3e38), axis=1, keepdims=True)
                    delta = jnp.sum(jnp.where(hm, prb, 0.0), axis=1, keepdims=True)
                    pc = jnp.where(mc, jnp.exp(_dot_nt(qm, kc) - lh), 0.0)
                    pp = jnp.where(mp, jnp.exp(_dot_nt(qm, kp) - lh), 0.0)
                    dsc = (pc * (_dot_nt(dom, vc) - delta)).astype(BF16)
                    dsp = (pp * (_dot_nt(dom, vp) - delta)).astype(BF16)
                    dqs.append(_dot(dsc, kc) + _dot(dsp, kp))
                    acc = lambda t, n: n if t is None else t + n
                    dkc, dkp = acc(dkc, _dot_tn(dsc, qm)), acc(dkp, _dot_tn(dsp, qm))
                    dvc, dvp = acc(dvc, _dot_tn(pc.astype(BF16), dom)), acc(dvp, _dot_tn(pp.astype(BF16), dom))
                dqd[pl.ds(st, BLK), :] = jnp.where(hm0, dqs[0], dqs[1])
                dkd[pl.ds(stp, BLK), :] += dkp
                dvd[pl.ds(stp, BLK), :] += dvp
                dkd[pl.ds(st, BLK), :] += dkc
                dvd[pl.ds(st, BLK), :] += dvc
                return carry

            lax.fori_loop(0, S // BLK, blk, 0)
            _reint(dqd, dqn, d, True)
            _reint(dkd, dkn, d, True)
            _reint(dvd, dvn, d, True)

        lane = lax.broadcasted_iota(jnp.int32, (S, LANES), 1)
        first = (lane & (HEAD // 2)) == 0
        cos, sin = _cos_sin(pos_ref, freq_ref)
        dq = dqn[...] * (HEAD ** -0.5)
        dk = dkn[...]
        stage[0] = (dq * cos - _rot_half(dq, first) * sin).astype(BF16)
        stage[1] = (dk * cos - _rot_half(dk, first) * sin).astype(BF16)
        stage[2] = dvn[...].astype(BF16)
        copies = [pltpu.make_async_copy(stage.at[j], dp_ref.at[:, pl.ds((2 + j) * R + p * LANES, LANES)], sems.at[j])
                  for j in range(3)]
        for cp in copies:
            cp.start()
        for cp in copies:
            cp.wait()

    blk = pl.BlockSpec((S, LANES), lambda p: (0, p))
    return pl.pallas_call(
        body, name="att_bwd", grid=(NPAIR,),
        in_specs=[pl.BlockSpec(memory_space=pl.ANY), blk, blk, blk, blk, blk,
                  pl.BlockSpec((S, LANES), lambda p: (0, VB + p)), pl.BlockSpec((S, 1), lambda p: (0, 0)),
                  pl.BlockSpec((1, LANES), lambda p: (0, 0))],
        out_specs=pl.BlockSpec(memory_space=pl.ANY),
        out_shape=jax.ShapeDtypeStruct((S, E), BF16),
        scratch_shapes=[pltpu.VMEM((S, LANES), BF16)] * 4 + [pltpu.VMEM((S, LANES), F32)] * 9
        + [pltpu.VMEM((3, S, LANES), BF16), pltpu.SemaphoreType.DMA((3,))],
        input_output_aliases={0: 0},
        compiler_params=_cp(("arbitrary",)),
    )(dproj, d_att, att, lse, qr, kr, proj, pos, freq)


def _out_fwd_bwd(ya, att, proj, w_out_bf, x, target, mod, norm_post, norm_att):
    ts = 256

    def body(ya_ref, att_ref, gb_ref, w_ref, x_ref, t_ref, mod_ref, npost_ref, natt_ref,
             gx_ref, dya_ref, datt_ref, dgb_ref, gw_ref, acc_ref):
        i = pl.program_id(0)

        @pl.when(i == 0)
        def _():
            gw_ref[...] = jnp.zeros_like(gw_ref)
            acc_ref[...] = jnp.zeros_like(acc_ref)

        gate = mod_ref[:, 2 * D:3 * D]
        att = att_ref[...]
        gb = gb_ref[...]
        sg = _sigmoid(gb)
        silu = gb * sg
        ybp = att * silu
        yb, ybn, rstd_b = _rms_fwd(ybp, natt_ref[...])
        cat = jnp.concatenate([ya_ref[...], yb.astype(BF16)], axis=1)
        mix = _dot(cat, w_ref[...])
        rn, mn, rstd_m = _rms_fwd(mix, npost_ref[...])
        err = x_ref[...] + gate * rn - t_ref[...]
        dy = err * (1.0 / D)
        gx_ref[...] = dy
        dmix, dnpost = _rms_bwd(dy * gate, mn, rstd_m, npost_ref[...])
        dmb = dmix.astype(BF16)
        gw_ref[...] += _dot_tn(cat, dmb)
        dcat = _dot_nt(dmb, w_ref[...])
        dya_ref[...] = dcat[:, 0:R]
        dybp, dnatt = _rms_bwd(dcat[:, R:2 * R], ybn, rstd_b, natt_ref[...])
        datt_ref[...] = dybp * silu
        dgb_ref[...] = (dybp * att * (sg * (1.0 + gb * (1.0 - sg)))).astype(BF16)
        acc_ref[0:1, :] += jnp.sum(dy * rn, axis=0, keepdims=True)
        acc_ref[1:2, :] += dnpost
        acc_ref[2:3, 0:R] += dnatt
        acc_ref[3:4, :] += jnp.sum(jnp.sum(err * err, axis=1, keepdims=True), axis=0, keepdims=True)

    tile = lambda w: pl.BlockSpec((ts, w), lambda i: (i, 0))
    c0 = lambda shape: pl.BlockSpec(shape, lambda i: (0, 0))
    return pl.pallas_call(
        body, name="out_fwd_bwd", grid=(S // ts,),
        in_specs=[tile(R), tile(R), pl.BlockSpec((ts, R), lambda i: (i, 5)), c0((D, D)), tile(D), tile(D),
                  c0((1, 3 * D)), c0((1, D)), c0((1, R))],
        out_specs=[tile(D), tile(R), tile(R), pl.BlockSpec((ts, R), lambda i: (i, 5)), c0((D, D)), c0((8, D))],
        out_shape=[jax.ShapeDtypeStruct((S, D), F32), jax.ShapeDtypeStruct((S, R), F32),
                   jax.ShapeDtypeStruct((S, R), F32), jax.ShapeDtypeStruct((S, E), BF16),
                   jax.ShapeDtypeStruct((D, D), F32), jax.ShapeDtypeStruct((8, D), F32)],
        compiler_params=_cp(("arbitrary",)),
    )(ya, att, proj, w_out_bf, x, target, mod, norm_post, norm_att)


def _grad_w_in(hb, dproj):
    ts = 512

    def body(h_ref, dp_ref, gw_ref):
        @pl.when(pl.program_id(1) == 0)
        def _():
            gw_ref[...] = jnp.zeros_like(gw_ref)

        gw_ref[...] += _dot_tn(h_ref[...], dp_ref[...])

    return pl.pallas_call(
        body, name="grad_w_in", grid=(NCHIP, S // ts),
        in_specs=[pl.BlockSpec((ts, D), lambda j, s: (s, 0)), pl.BlockSpec((ts, EC), lambda j, s: (s, j))],
        out_specs=pl.BlockSpec((None, D, EC), lambda j, s: (j, 0, 0)),
        out_shape=jax.ShapeDtypeStruct((NCHIP, D, EC), F32),
        compiler_params=_cp(("parallel", "arbitrary")),
    )(hb, dproj)


def _in_proj_bwd(dproj, w_in_bf, x, gx1, mod, norm_pre):
    ts = 256

    def body(dp_ref, w_ref, x_ref, gx1_ref, mod_ref, np_ref, gx_ref, acc_ref):
        @pl.when(pl.program_id(0) == 0)
        def _():
            acc_ref[...] = jnp.zeros_like(acc_ref)

        dh = sum(_dot_nt(dp_ref[:, j * EC:(j + 1) * EC], w_ref[j]) for j in range(NCHIP))
        hp, xn, rstd = _rms_fwd(x_ref[...], np_ref[...])
        dx, dnp = _rms_bwd(dh * (1.0 + mod_ref[:, D:2 * D]), xn, rstd, np_ref[...])
        gx_ref[...] = gx1_ref[...] + dx
        acc_ref[0:1, :] += jnp.sum(dh, axis=0, keepdims=True)
        acc_ref[1:2, :] += jnp.sum(dh * hp, axis=0, keepdims=True)
        acc_ref[2:3, :] += dnp

    tile = lambda w: pl.BlockSpec((ts, w), lambda i: (i, 0))
    c0 = lambda shape: pl.BlockSpec(shape, lambda i: (0, 0))
    return pl.pallas_call(
        body, name="in_proj_bwd", grid=(S // ts,),
        in_specs=[tile(E), pl.BlockSpec((NCHIP, D, EC), lambda i: (0, 0, 0)), tile(D), tile(D), c0((1, 3 * D)),
                  c0((1, D))],
        out_specs=[tile(D), c0((8, D))],
        out_shape=[jax.ShapeDtypeStruct((S, D), F32), jax.ShapeDtypeStruct((8, D), F32)],
        compiler_params=_cp(("arbitrary",)),
    )(dproj, w_in_bf, x, gx1, mod, norm_pre)


def _block_diag(w):
    n, b, _ = w.shape
    eye = jnp.eye(n, dtype=w.dtype)
    return (eye[:, None, :, None] * w[:, :, None, :]).reshape(n * b, n * b)


def _diag_blocks(m):
    n, b = R // HEAD, HEAD
    return jnp.stack([m[h * b:(h + 1) * b, h * b:(h + 1) * b] for h in range(n)])


def _local_step(x, pos, target, mod, w_in_bf, w_out, conv_w, p):
    wa_d = _block_diag(p["w_rg_a"]).astype(BF16)
    wx_d = _block_diag(p["w_rg_x"]).astype(BF16)
    rec_p = (conv_w, p["conv_b"], wa_d, p["b_rg_a"], wx_d, p["b_rg_x"], p["lru_lambda"], p["norm_rec"])
    cos, sin = _rope_table(pos, _rope_freq())
    proj, hb = _in_proj_fwd(x, mod, p["norm_pre"], w_in_bf)
    h_all, ya = _rec_fwd(proj, *rec_p)
    att, qr, kr, lse, w_out_bf = _att_fwd(proj, cos, sin, w_out)
    gx1, d_ya, d_att, dproj, gw_out, acc_o = _out_fwd_bwd(ya, att, proj, w_out_bf.reshape(D, D), x, target, mod,
                                                           p["norm_post"], p["norm_att"])
    dproj, g_out = _att_bwd(dproj, d_att, att, lse, qr, kr, proj, cos, sin, gw_out.reshape(NCHIP, D // NCHIP, D))
    dproj, dwa, dwx, sm = _rec_bwd(dproj, d_ya, proj, h_all, *rec_p)
    gw_in = _grad_w_in(hb, dproj)
    grad_x, acc_i = _in_proj_bwd(dproj, w_in_bf, x, gx1, mod, p["norm_pre"])
    small = dict(b_ada=jnp.concatenate([acc_i[0:1], acc_i[1:2], acc_o[0:1]], axis=1),
                 norm_pre=acc_i[2:3], norm_post=acc_o[1:2], conv_w=sm[8:12], conv_b=sm[4:5],
                 w_rg_a=_diag_blocks(dwa), b_rg_a=sm[0:1], w_rg_x=_diag_blocks(dwx), b_rg_x=sm[1:2],
                 lru_lambda=sm[2:3], norm_rec=sm[3:4], norm_att=acc_o[2:3, 0:R], loss=acc_o[3:4, 0:LANES])
    return grad_x, gw_in, g_out, small


def _me():
    return lax.axis_index("x"), lax.axis_index("y"), lax.axis_index("c")


def _flip(v, bit):
    return 1 - v if bit else v


def _peer(rel):
    x, y, c = _me()
    return (_flip(x, rel & 4), _flip(y, rel & 2), _flip(c, rel & 1))


def _remote(src, dst, send_sem, recv_sem, rel):
    return pltpu.make_async_remote_copy(src_ref=src, dst_ref=dst, send_sem=send_sem, recv_sem=recv_sem,
                                        device_id=_peer(rel), device_id_type=MESH)


def _allgather_rows(row, name):
    w = row.shape[1]

    def body(row_ref, out_ref, send_sems, recv_sems, local_sem):
        x, y, c = _me()
        me = 4 * x + 2 * y + c
        mine = pltpu.make_async_copy(row_ref, out_ref.at[pl.ds(me, 1), :], local_sem)
        mine.start()
        sends = [_remote(row_ref, out_ref.at[pl.ds(me, 1), :], send_sems.at[r - 1], recv_sems.at[r - 1], r)
                 for r in range(1, NDEV)]
        for cp in sends:
            cp.start()
        for r in range(1, NDEV):
            px, py, pc = _peer(r)
            src = 4 * px + 2 * py + pc
            _remote(row_ref, out_ref.at[pl.ds(src, 1), :], send_sems.at[r - 1], recv_sems.at[r - 1], r).wait_recv()
        for cp in sends:
            cp.wait_send()
        mine.wait()

    return pl.pallas_call(
        body, name=name,
        in_specs=[pl.BlockSpec(memory_space=pltpu.VMEM)],
        out_specs=pl.BlockSpec(memory_space=pltpu.VMEM),
        out_shape=jax.ShapeDtypeStruct((NDEV, w), row.dtype),
        scratch_shapes=[pltpu.SemaphoreType.DMA((NDEV - 1,)), pltpu.SemaphoreType.DMA((NDEV - 1,)),
                        pltpu.SemaphoreType.DMA],
        compiler_params=pltpu.CompilerParams(vmem_limit_bytes=VMEM_LIMIT),
    )(row)


class _WeightGather:
    SEMS = [pltpu.SemaphoreType.DMA((NCHIP - 1,))] * 4

    def __init__(self, w_ref, out_ref, send_sems, recv_sems, fsend_sems, frecv_sems):
        x, y, c = _me()
        self.w, self.out, self.ci = w_ref, out_ref, 2 * x + y
        self.half = w_ref.shape[0] // 2
        self.r0 = pl.multiple_of(c * self.half, self.half)
        self.r1 = pl.multiple_of((1 - c) * self.half, self.half)
        self.sems = (send_sems, recv_sems, fsend_sems, frecv_sems)

    def _ici(self, chip, k):
        blk = self.out.at[chip, pl.ds(self.r0, self.half), :]
        return _remote(blk, blk, self.sems[0].at[k - 1], self.sems[1].at[k - 1], 2 * k)

    def _d2d(self, chip, start, k):
        blk = self.out.at[chip, pl.ds(start, self.half), :]
        return _remote(blk, blk, self.sems[2].at[k - 1], self.sems[3].at[k - 1], 1)

    def start(self):
        self.out[self.ci] = self.w[...].astype(BF16)
        for k in range(1, NCHIP):
            self._ici(self.ci, k).start()

    def forward(self):
        for k in range(1, NCHIP):
            self._ici(self.ci ^ k, k).wait_recv()
            self._d2d(self.ci ^ k, self.r0, k).start()

    def finish(self):
        for k in range(1, NCHIP):
            self._d2d(self.ci ^ k, self.r1, k).wait_recv()
        for k in range(1, NCHIP):
            self._ici(self.ci, k).wait_send()
            self._d2d(self.ci ^ k, self.r0, k).wait_send()


def _start_gather(crow, w_ada, b_cols, w_in):
    wc = crow.shape[1]

    def body(crow_ref, wada_ref, b_ref, win_ref, g0_ref, mod_ref, wbf_ref,
             modp, modb, cs, cr, ms, mr, ws, wr, fs, fr, local_sems):
        x, y, c = _me()
        ci = 2 * x + y
        me = 2 * ci + c
        wg = _WeightGather(win_ref, wbf_ref, ws, wr, fs, fr)
        mine = pltpu.make_async_copy(crow_ref, g0_ref.at[pl.ds(me, 1), :], local_sems.at[0])
        mine.start()
        csend = [_remote(crow_ref, g0_ref.at[pl.ds(me, 1), :], cs.at[r - 1], cr.at[r - 1], r) for r in range(1, NDEV)]
        for cp in csend:
            cp.start()
        wg.start()
        for r in range(1, NDEV):
            px, py, pc = _peer(r)
            _remote(crow_ref, g0_ref.at[pl.ds(4 * px + 2 * py + pc, 1), :], cs.at[r - 1], cr.at[r - 1], r).wait_recv()
        mine.wait()
        cv = g0_ref[:, 0:D]
        sc = cv * _sigmoid(cv)
        scb = jnp.concatenate([sc, jnp.zeros_like(sc)], axis=0).astype(BF16)
        modp[...] = _dot(scb, wada_ref[...].astype(BF16))[0:NDEV, :] + b_ref[...]
        own = pltpu.make_async_copy(modp.at[pl.ds(me, 1), :], modb.at[ci], local_sems.at[1])
        own.start()
        msend = []
        for k in range(1, NCHIP):
            dst = 2 * (ci ^ k) + c
            cp = _remote(modp.at[pl.ds(dst, 1), :], modb.at[ci], ms.at[k - 1], mr.at[k - 1], 2 * k)
            cp.start()
            msend.append(cp)
        for k in range(1, NCHIP):
            _remote(modp.at[pl.ds(me, 1), :], modb.at[ci ^ k], ms.at[k - 1], mr.at[k - 1], 2 * k).wait_recv()
        own.wait()
        for j in range(NCHIP):
            mod_ref[:, j * EC:(j + 1) * EC] = modb[j]
        wg.forward()
        wg.finish()
        for cp in csend + msend:
            cp.wait_send()

    vm = pl.BlockSpec(memory_space=pltpu.VMEM)
    return pl.pallas_call(
        body, name="start_gather",
        in_specs=[vm] * 4, out_specs=[vm] * 3,
        out_shape=[jax.ShapeDtypeStruct((NDEV, wc), F32), jax.ShapeDtypeStruct((1, 3 * D), F32),
                   jax.ShapeDtypeStruct((NCHIP, D, EC), BF16)],
        scratch_shapes=[pltpu.VMEM((NDEV, EC), F32), pltpu.VMEM((NCHIP, 1, EC), F32),
                        pltpu.SemaphoreType.DMA((NDEV - 1,)), pltpu.SemaphoreType.DMA((NDEV - 1,)),
                        pltpu.SemaphoreType.DMA((NCHIP - 1,)), pltpu.SemaphoreType.DMA((NCHIP - 1,))]
        + _WeightGather.SEMS + [pltpu.SemaphoreType.DMA((2,))],
        compiler_params=pltpu.CompilerParams(vmem_limit_bytes=VMEM_LIMIT),
    )(crow, w_ada, b_cols, w_in)


class _ReduceScatter:
    @staticmethod
    def scratch(n_units, rows, ucols, max_owned):
        half = rows // 2
        return [pltpu.VMEM((n_units, half, ucols), F32), pltpu.VMEM((n_units, half, ucols), BF16),
                pltpu.VMEM((max_owned, NCHIP, half, ucols), BF16),
                pltpu.SemaphoreType.DMA((2,)), pltpu.SemaphoreType.DMA((n_units,)),
                pltpu.SemaphoreType.DMA((n_units, NCHIP)), pltpu.SemaphoreType.DMA((n_units,)),
                pltpu.SemaphoreType.DMA((n_units,))]

    def __init__(self, g_ref, out_ref, units, sib, stage, got, sem1, send2, recv2, send3, recv3):
        x, y, c = _me()
        self.c, self.ci = c, 2 * x + y
        self.g, self.out, self.units = g_ref, out_ref, units
        self.sib, self.stage, self.got = sib, stage, got
        self.sem1, self.send2, self.recv2, self.send3, self.recv3 = sem1, send2, recv2, send3, recv3
        self.half = g_ref.shape[1] // 2
        self.ucols = g_ref.shape[2]
        self.r0 = pl.multiple_of(c * self.half, self.half)
        self.r1 = pl.multiple_of((1 - c) * self.half, self.half)
        self.slot0 = units[0][0]
        assert [u[0] for u in units] == list(range(self.slot0, self.slot0 + len(units)))
        seen = {}
        self.local = []
        for _, owner, _ in units:
            self.local.append(seen.get(owner, 0))
            seen[owner] = seen.get(owner, 0) + 1

    def _halves(self):
        n = len(self.units)
        return _remote(self.g.at[pl.ds(self.slot0, n), pl.ds(self.r1, self.half), :], self.sib,
                       self.sem1.at[0], self.sem1.at[1], 1)

    def _partial(self, i, sender):
        _, owner, _ = self.units[i]
        return pltpu.make_async_remote_copy(
            src_ref=self.stage.at[i], dst_ref=self.got.at[self.local[i], sender],
            send_sem=self.send2.at[i], recv_sem=self.recv2.at[i, sender],
            device_id=(owner // 2, owner % 2, self.c), device_id_type=MESH)

    def _back(self, i, start):
        off = self.units[i][2]
        blk = self.out.at[pl.ds(start, self.half), off:off + self.ucols]
        return _remote(blk, blk, self.send3.at[i], self.recv3.at[i], 1)

    def start_halves(self):
        self._halves().start()

    def send_partials(self):
        self._halves().wait_recv()
        for i, (slot, owner, _) in enumerate(self.units):
            @pl.when(self.ci != owner)
            def _():
                self.stage[i] = (self.g[slot, pl.ds(self.r0, self.half), :] + self.sib[i]).astype(BF16)
                self._partial(i, self.ci).start()

    def reduce_owned(self):
        for i, (slot, owner, off) in enumerate(self.units):
            @pl.when(self.ci == owner)
            def _():
                rows, cols = pl.ds(self.r0, self.half), slice(off, off + self.ucols)
                self.out[rows, cols] = self.g[slot, pl.ds(self.r0, self.half), :] + self.sib[i]
                for s in range(NCHIP):
                    if s != owner:
                        self._partial(i, s).wait_recv()
                        self.out[rows, cols] += self.got[self.local[i], s].astype(F32)
                self._back(i, self.r0).start()

    def finish(self):
        self._halves().wait_send()
        for i, (_, owner, _) in enumerate(self.units):
            @pl.when(self.ci == owner)
            def _():
                self._back(i, self.r1).wait_recv()
                self._back(i, self.r0).wait_send()

            @pl.when(self.ci != owner)
            def _():
                self._partial(i, self.ci).wait_send()


def _reduce_scatter(g4, name):
    _, rows, cols = g4.shape
    units = [(j, j, 0) for j in range(NCHIP)]

    def body(g_ref, out_ref, *scratch):
        rs = _ReduceScatter(g_ref, out_ref, units, *scratch)
        rs.start_halves()
        rs.send_partials()
        rs.reduce_owned()
        rs.finish()

    return pl.pallas_call(
        body, name=name,
        in_specs=[pl.BlockSpec(memory_space=pltpu.VMEM)],
        out_specs=pl.BlockSpec(memory_space=pltpu.VMEM),
        out_shape=jax.ShapeDtypeStruct((rows, cols), F32),
        scratch_shapes=_ReduceScatter.scratch(NCHIP, rows, cols, 1),
        compiler_params=pltpu.CompilerParams(vmem_limit_bytes=VMEM_LIMIT),
    )(g4)


def _silu_rows(c_ref):
    cv = c_ref[...]
    sc = cv * _sigmoid(cv)
    return jnp.concatenate([sc, jnp.zeros_like(sc)], axis=0).astype(BF16)


def _ada_fwd(cg, w_ada, b_cols):
    def body(c_ref, w_ref, b_ref, o_ref):
        o_ref[...] = _dot(_silu_rows(c_ref), w_ref[...].astype(BF16))[0:NDEV, :] + b_ref[...]

    return pl.pallas_call(body, name="ada_fwd", out_shape=jax.ShapeDtypeStruct((NDEV, EC), F32),
                          compiler_params=_cp())(cg, w_ada, b_cols)


def _ada_bwd(cg, dmod_cols):
    def body(c_ref, d_ref, o_ref):
        dm = d_ref[...]
        dmb = jnp.concatenate([dm, jnp.zeros_like(dm)], axis=0).astype(BF16)
        o_ref[...] = _dot_tn(_silu_rows(c_ref), dmb)

    return pl.pallas_call(body, name="ada_bwd", out_shape=jax.ShapeDtypeStruct((D, EC), F32),
                          compiler_params=_cp())(cg, dmod_cols)


def _sum_rows(g):
    def body(g_ref, o_ref):
        acc = g_ref[0:1, :]
        for r in range(1, NDEV):
            acc = acc + g_ref[r:r + 1, :]
        o_ref[...] = acc

    return pl.pallas_call(body, name="sum_rows", out_shape=jax.ShapeDtypeStruct((1, g.shape[1]), F32),
                          compiler_params=_cp())(g)


def _adamw(w, g, m, v, name):
    rows, cols = w.shape
    tr = 256 if rows % 256 == 0 else rows

    def body(w_ref, g_ref, m_ref, v_ref, d_ref, nm_ref, nv_ref):
        gv = g_ref[...]
        nm = B1 * m_ref[...] + (1.0 - B1) * gv
        nv = B2 * v_ref[...] + (1.0 - B2) * (gv * gv)
        m_hat = nm / (1.0 - B1 ** STEP)
        v_hat = nv / (1.0 - B2 ** STEP)
        d_ref[...] = (-LR) * (m_hat / (jnp.sqrt(v_hat) + ADAM_EPS) + WD * w_ref[...])
        nm_ref[...] = nm
        nv_ref[...] = nv

    spec = pl.BlockSpec((tr, cols), lambda i: (i, 0))
    return pl.pallas_call(
        body, name=name, grid=(rows // tr,), in_specs=[spec] * 4, out_specs=[spec] * 3,
        out_shape=[jax.ShapeDtypeStruct((rows, cols), F32)] * 3,
        compiler_params=_cp(("parallel",)),
    )(w, g, m, v)


SMALL = (("b_ada", 3 * D), ("norm_pre", D), ("norm_post", D), ("conv_w", 4 * R), ("conv_b", R),
         ("w_rg_a", R * HEAD), ("b_rg_a", R), ("w_rg_x", R * HEAD), ("b_rg_x", R), ("lru_lambda", R),
         ("norm_rec", R), ("norm_att", R))
BIG = ("w_ada", "w_in", "w_out")
WEIGHTS = ("w_ada", "b_ada", "norm_pre", "norm_post", "w_in", "conv_w", "conv_b", "w_rg_a", "b_rg_a", "w_rg_x",
           "b_rg_x", "lru_lambda", "norm_rec", "norm_att", "w_out")


def kernel(x, c, positions, w_ada, b_ada, norm_pre, norm_post, w_in, conv_w, conv_b, w_rg_a, b_rg_a, w_rg_x, b_rg_x, lru_lambda, norm_rec, norm_att, w_out, loss_target, m_w_ada, m_b_ada, m_norm_pre, m_norm_post, m_w_in, m_conv_w, m_conv_b, m_w_rg_a, m_b_rg_a, m_w_rg_x, m_b_rg_x, m_lru_lambda, m_norm_rec, m_norm_att, m_w_out, v_w_ada, v_b_ada, v_norm_pre, v_norm_post, v_w_in, v_conv_w, v_conv_b, v_w_rg_a, v_b_rg_a, v_w_rg_x, v_b_rg_x, v_lru_lambda, v_norm_rec, v_norm_att, v_w_out):
    given = dict(locals())
    wts = {n: given[n] for n in WEIGHTS}
    ms = {n: given["m_" + n] for n in WEIGHTS}
    vs = {n: given["v_" + n] for n in WEIGHTS}
    xi, yi, cc = _me()
    chip = 2 * xi + yi
    me = 2 * chip + cc
    cw_loc = R // NCHIP

    b_cols = lax.dynamic_slice(b_ada, (0, chip * EC), (1, EC))
    g0, mod, w_in_bf = _start_gather(jnp.concatenate([c, conv_w.reshape(1, 4 * cw_loc)], axis=1),
                                     w_ada[0], b_cols, w_in[0])
    cg = g0[:, 0:D]
    conv_full = g0[0::2, D:].reshape(NCHIP, 4, cw_loc).transpose(1, 0, 2).reshape(4, R)

    p = dict(norm_pre=norm_pre, norm_post=norm_post, conv_b=conv_b, b_rg_a=b_rg_a, b_rg_x=b_rg_x,
             lru_lambda=lru_lambda, norm_rec=norm_rec, norm_att=norm_att, w_rg_a=w_rg_a[0], w_rg_x=w_rg_x[0])
    grad_x, gw_in, g_out, small = _local_step(
        x[0], positions.reshape(S, 1), loss_target[0], mod, w_in_bf, w_out[0], conv_full, p)

    grads = {"w_out": g_out}
    grads["w_in"] = _reduce_scatter(gw_in, "rs_w_in")
    row = jnp.concatenate([small[n].reshape(1, k) for n, k in SMALL + (("loss", LANES),)], axis=1)
    g2 = _allgather_rows(row, "gather_small")
    tot = _sum_rows(g2)
    grads["w_ada"] = _ada_bwd(cg, lax.dynamic_slice(g2, (0, chip * EC), (NDEV, EC)))
    off = 0
    for n, k in SMALL:
        grads[n] = tot[:, off:off + k]
        off += k
    loss = tot[0, off] * (0.5 / D)
    grads["conv_w"] = lax.dynamic_slice(grads["conv_w"].reshape(4, R), (0, chip * cw_loc), (4, cw_loc))

    delta, new_m, new_v = {}, {}, {}
    for n in BIG:
        delta[n], new_m[n], new_v[n] = _adamw(wts[n][0], grads[n], ms[n][0], vs[n][0], "adamw_" + n)
    pack = lambda d: jnp.concatenate([d[n].reshape(1, -1) for n, _ in SMALL], axis=1).reshape(-1, LANES)
    pd, pm, pv = _adamw(pack(wts), pack(grads), pack(ms), pack(vs), "adamw_small")
    off = 0
    for n, _ in SMALL:
        k = wts[n].size
        for dst, src in ((delta, pd), (new_m, pm), (new_v, pv)):
            dst[n] = src.reshape(1, -1)[:, off:off + k]
        off += k
    out = lambda d: [d[n].reshape(wts[n].shape) for n in WEIGHTS]
    return (loss, grad_x.reshape(x.shape), *out(grads), *out(delta), *out(new_m), *out(new_v))
```

```python
import functools

import numpy as np
import jax
import jax.numpy as jnp
from jax import lax
from jax.experimental import pallas as pl
from jax.experimental.pallas import tpu as pltpu

F32 = jnp.float32
BF16 = jnp.bfloat16

S = 2048
D = 1024
E = 3072
R = 512
NDEV = 8
NCHIP = 4
EC = 768
LRU_C = 8.0
EPS = 1e-6
NEG = -1e30
HEAD = 64
BLK = 128
PATTERNS = (1, 4, 16)
ROPE_THETA = 10000.0
LANES = 128
VMEM_LIMIT = 56 * 1024 * 1024

B1, B2, LR, WD, ADAM_EPS, STEP = 0.9, 0.999, 0.001, 0.01, 1e-8, 10
MESH = pl.DeviceIdType.MESH


def _cp(sem=None, **kw):
    return pltpu.CompilerParams(dimension_semantics=sem, vmem_limit_bytes=VMEM_LIMIT, **kw)


def _dot(a, b):
    return jnp.dot(a, b, preferred_element_type=F32)


def _dot_nt(a, b):
    return lax.dot_general(a, b, (((1,), (1,)), ((), ())), preferred_element_type=F32)


def _dot_tn(a, b):
    return lax.dot_general(a, b, (((0,), (0,)), ((), ())), preferred_element_type=F32)


def _sigmoid(x):
    return 1.0 / (1.0 + jnp.exp(-x))


def _expm1(x):
    poly = x * (1.0 + x * (0.5 + x * (1.0 / 6 + x * (1.0 / 24 + x * (1.0 / 120 + x * (1.0 / 720))))))
    return jnp.where(jnp.abs(x) < 0.3, poly, jnp.exp(x) - 1.0)


def _rms_fwd(v, g):
    rstd = lax.rsqrt(jnp.mean(v * v, axis=-1, keepdims=True) + EPS)
    vn = v * rstd
    return vn * g, vn, rstd


def _rms_bwd(dy, vn, rstd, g):
    dvn = dy * g
    dv = rstd * (dvn - vn * jnp.mean(dvn * vn, axis=-1, keepdims=True))
    return dv, jnp.sum(dy * vn, axis=0, keepdims=True)


def _in_proj_fwd(x, mod, norm_pre, w_in_bf):
    ts = 256

    def body(x_ref, mod_ref, np_ref, w_ref, proj_ref, hb_ref):
        hp, _, _ = _rms_fwd(x_ref[...], np_ref[...])
        h = hp * (1.0 + mod_ref[:, D:2 * D]) + mod_ref[:, 0:D]
        hb = h.astype(BF16)
        hb_ref[...] = hb
        for j in range(NCHIP):
            proj_ref[:, j * EC:(j + 1) * EC] = _dot(hb, w_ref[j])

    return pl.pallas_call(
        body, name="in_proj_fwd", grid=(S // ts,),
        in_specs=[pl.BlockSpec((ts, D), lambda i: (i, 0)), pl.BlockSpec((1, 3 * D), lambda i: (0, 0)),
                  pl.BlockSpec((1, D), lambda i: (0, 0)), pl.BlockSpec((NCHIP, D, EC), lambda i: (0, 0, 0))],
        out_specs=[pl.BlockSpec((ts, E), lambda i: (i, 0)), pl.BlockSpec((ts, D), lambda i: (i, 0))],
        out_shape=[jax.ShapeDtypeStruct((S, E), F32), jax.ShapeDtypeStruct((S, D), BF16)],
        compiler_params=_cp(("parallel",)),
    )(x, mod, norm_pre, w_in_bf)


RT = 256


def _shift_down(cur, prev8, j, row):
    if j == 0:
        return cur
    top = jnp.tile(pltpu.roll(prev8, j, 0), (RT // 8, 1))
    return jnp.where(row >= j, pltpu.roll(cur, j, 0), top)


def _shift_up(cur, next8, j, row):
    if j == 0:
        return cur
    bot = jnp.tile(pltpu.roll(next8, 8 - j, 0), (RT // 8, 1))
    return jnp.where(row < RT - j, pltpu.roll(cur, RT - j, 0), bot)


def _rec_gates(xp, xprev8, row, cw_ref, cb_ref, wa_ref, ba_ref, wx_ref, bx_ref, lam_ref):
    xa = cb_ref[...] + sum(cw_ref[3 - j:4 - j, :] * _shift_down(xp, xprev8, j, row) for j in range(4))
    xab = xa.astype(BF16)
    r = _sigmoid(_dot(xab, wa_ref[...]) + ba_ref[...])
    ig = _sigmoid(_dot(xab, wx_ref[...]) + bx_ref[...])
    nl = -lam_ref[...]
    sp = jnp.maximum(nl, 0.0) + jnp.log1p(jnp.exp(-jnp.abs(nl)))
    la = (-LRU_C) * r * sp
    a = jnp.exp(la)
    mult = jnp.sqrt(-_expm1(2.0 * la))
    return dict(xa=xa, xab=xab, r=r, ig=ig, sp=sp, la=la, a=a, mult=mult)


def _scan_fwd(a, u, row):
    sh = 1
    while sh < RT:
        a_s = jnp.where(row >= sh, pltpu.roll(a, sh, 0), 1.0)
        u_s = jnp.where(row >= sh, pltpu.roll(u, sh, 0), 0.0)
        u = a * u_s + u
        a = a * a_s
        sh *= 2
    return a, u


def _scan_bwd(al, g, row):
    sh = 1
    while sh < RT:
        al_s = jnp.where(row < RT - sh, pltpu.roll(al, RT - sh, 0), 1.0)
        g_s = jnp.where(row < RT - sh, pltpu.roll(g, RT - sh, 0), 0.0)
        g = g + al * g_s
        al = al * al_s
        sh *= 2
    return g


def _rec_fwd(proj, conv_w, conv_b, wa_d, ba, wx_d, bx, lam, norm_rec):
    nt = S // RT

    def body(p_ref, cw_ref, cb_ref, wa_ref, ba_ref, wx_ref, bx_ref, lam_ref, nr_ref,
             h_ref, ya_ref, prev8, hc):
        i = pl.program_id(0)

        @pl.when(i == 0)
        def _():
            prev8[...] = jnp.zeros_like(prev8)
            hc[...] = jnp.zeros_like(hc)

        row = lax.broadcasted_iota(jnp.int32, (RT, R), 0)
        xp = p_ref[:, 0:R]
        ga = p_ref[:, R:2 * R]
        f = _rec_gates(xp, prev8[...], row, cw_ref, cb_ref, wa_ref, ba_ref, wx_ref, bx_ref, lam_ref)
        u = f["mult"] * (f["ig"] * f["xa"])
        acum, hh = _scan_fwd(f["a"], u, row)
        h = hh + acum * hc[0:1, :]
        h_ref[...] = h
        hc[0:1, :] = h_ref[RT - 1:RT, :]
        prev8[...] = p_ref[RT - 8:RT, 0:R]
        yp = h * (ga * _sigmoid(ga))
        ya, _, _ = _rms_fwd(yp, nr_ref[...])
        ya_ref[...] = ya.astype(BF16)

    row1 = lambda n: pl.BlockSpec((1, n), lambda i: (0, 0))
    return pl.pallas_call(
        body, name="rec_fwd", grid=(nt,),
        in_specs=[pl.BlockSpec((RT, 2 * R), lambda i: (i, 0)), pl.BlockSpec((4, R), lambda i: (0, 0)), row1(R),
                  pl.BlockSpec((R, R), lambda i: (0, 0)), row1(R), pl.BlockSpec((R, R), lambda i: (0, 0)), row1(R),
                  row1(R), row1(R)],
        out_specs=[pl.BlockSpec((RT, R), lambda i: (i, 0)), pl.BlockSpec((RT, R), lambda i: (i, 0))],
        out_shape=[jax.ShapeDtypeStruct((S, R), F32), jax.ShapeDtypeStruct((S, R), BF16)],
        scratch_shapes=[pltpu.VMEM((8, R), F32), pltpu.VMEM((8, R), F32)],
        compiler_params=_cp(("arbitrary",)),
    )(proj, conv_w, conv_b, wa_d, ba, wx_d, bx, lam, norm_rec)


def _rec_bwd(dproj, d_ya, proj, h_all, conv_w, conv_b, wa_d, ba, wx_d, bx, lam, norm_rec, gw, gw_units):
    nt = S // RT

    def body(dp_in, dya_ref, p_ref, pprev_ref, h_ref, hprev_ref, cw_ref, cb_ref, wa_ref, ba_ref, wx_ref, bx_ref,
             lam_ref, nr_ref, gw_ref, dp_ref, dwa_ref, dwx_ref, sm_ref, gsum_ref, nxt8, cg, gred, *rs_scratch):
        i = pl.program_id(0)
        ti = nt - 1 - i
        _ReduceScatter(gw_ref, gred, gw_units, *rs_scratch).at_steps(i, 0, 2, nt - 1, nt - 1, gsum_ref)

        @pl.when(i == 0)
        def _():
            nxt8[...] = jnp.zeros_like(nxt8)
            cg[...] = jnp.zeros_like(cg)
            dwa_ref[...] = jnp.zeros_like(dwa_ref)
            dwx_ref[...] = jnp.zeros_like(dwx_ref)
            sm_ref[...] = jnp.zeros_like(sm_ref)

        row = lax.broadcasted_iota(jnp.int32, (RT, R), 0)
        first = (ti > 0).astype(F32)
        xprev8 = pprev_ref[...] * first
        hprev8 = hprev_ref[...] * first
        xp = p_ref[:, 0:R]
        ga = p_ref[:, R:2 * R]
        f = _rec_gates(xp, xprev8, row, cw_ref, cb_ref, wa_ref, ba_ref, wx_ref, bx_ref, lam_ref)
        xa, r, ig, a, mult = f["xa"], f["r"], f["ig"], f["a"], f["mult"]
        h = h_ref[...]
        sg = _sigmoid(ga)
        gate = ga * sg
        yp = h * gate
        _, ypn, rstd = _rms_fwd(yp, nr_ref[...])
        d_yp, dnr = _rms_bwd(dya_ref[...], ypn, rstd, nr_ref[...])
        d_ga = d_yp * h * (sg * (1.0 + ga * (1.0 - sg)))
        dh = d_yp * gate + jnp.where(row == RT - 1, cg[0:1, :], 0.0)
        al = jnp.where(row < RT - 1, pltpu.roll(a, RT - 1, 0), 0.0)
        g = _scan_bwd(al, dh, row)
        cg[0:1, :] = jnp.sum(jnp.where(row == 0, a * g, 0.0), axis=0, keepdims=True)
        h_m1 = _shift_down(h, hprev8, 1, row)
        da = g * h_m1
        ix = ig * xa
        d_mult = g * ix
        d_ig = g * mult * xa
        d_xa = g * mult * ig
        d_la = da * a - d_mult * (a * a) / mult
        d_r = d_la * ((-LRU_C) * f["sp"])
        dsp = jnp.sum(d_la * ((-LRU_C) * r), axis=0, keepdims=True)
        dlam = dsp * (-_sigmoid(-lam_ref[...]))
        d_za = d_r * r * (1.0 - r)
        d_zx = d_ig * ig * (1.0 - ig)
        dzab = d_za.astype(BF16)
        dzxb = d_zx.astype(BF16)
        dwa_ref[...] += _dot_tn(f["xab"], dzab)
        dwx_ref[...] += _dot_tn(f["xab"], dzxb)
        d_xa = d_xa + _dot_nt(dzab, wa_ref[...]) + _dot_nt(dzxb, wx_ref[...])
        d_xp = sum(cw_ref[3 - j:4 - j, :] * _shift_up(d_xa, nxt8[...], j, row) for j in range(4))
        dcw = [jnp.sum(d_xa * _shift_down(xp, xprev8, 3 - k, row), axis=0, keepdims=True) for k in range(4)]
        dp_ref[:, 0:R] = d_xp.astype(BF16)
        dp_ref[:, R:2 * R] = d_ga.astype(BF16)
        dp8 = d_xa[0:8, :]
        nxt8[...] = dp8
        sm_ref[0:1, :] += jnp.sum(d_za, axis=0, keepdims=True)
        sm_ref[1:2, :] += jnp.sum(d_zx, axis=0, keepdims=True)
        sm_ref[2:3, :] += dlam
        sm_ref[3:4, :] += dnr
        sm_ref[4:5, :] += jnp.sum(d_xa, axis=0, keepdims=True)
        for k in range(4):
            sm_ref[8 + k:9 + k, :] += dcw[k]

    c0 = lambda shape: pl.BlockSpec(shape, lambda i: (0, 0))
    rev = lambda i: nt - 1 - i
    prev8 = lambda i: (jnp.maximum((nt - 1 - i) * (RT // 8) - 1, 0), 0)
    return pl.pallas_call(
        body, name="rec_bwd", grid=(nt,),
        in_specs=[pl.BlockSpec(memory_space=pl.ANY),
                  pl.BlockSpec((RT, R), lambda i: (rev(i), 0)),
                  pl.BlockSpec((RT, 2 * R), lambda i: (rev(i), 0)), pl.BlockSpec((8, R), prev8),
                  pl.BlockSpec((RT, R), lambda i: (rev(i), 0)), pl.BlockSpec((8, R), prev8),
                  c0((4, R)), c0((1, R)), c0((R, R)), c0((1, R)), c0((R, R)), c0((1, R)), c0((1, R)), c0((1, R)),
                  pl.BlockSpec(memory_space=pltpu.VMEM)],
        out_specs=[pl.BlockSpec((RT, 2 * R), lambda i: (rev(i), 0)), c0((R, R)), c0((R, R)), c0((16, R)),
                   pl.BlockSpec(memory_space=pltpu.VMEM)],
        out_shape=[jax.ShapeDtypeStruct((S, E), BF16), jax.ShapeDtypeStruct((R, R), F32),
                   jax.ShapeDtypeStruct((R, R), F32), jax.ShapeDtypeStruct((16, R), F32),
                   jax.ShapeDtypeStruct((D, EC), F32)],
        scratch_shapes=[pltpu.VMEM((8, R), F32), pltpu.VMEM((8, R), F32), pltpu.VMEM((D, EC), F32)]
        + _ReduceScatter.scratch(len(gw_units), D, UC, UPC),
        input_output_aliases={0: 0},
        compiler_params=_cp(("arbitrary",)),
    )(dproj, d_ya, proj, proj, h_all, h_all, conv_w, conv_b, wa_d, ba, wx_d, bx, lam, norm_rec, gw)


NPAIR = R // LANES
QB, KB, VB, GB = 2 * R // LANES, 3 * R // LANES, 4 * R // LANES, 5 * R // LANES


def _rope_freq():
    half = HEAD // 2
    inv = np.float32(ROPE_THETA) ** (-(np.arange(half, dtype=np.float32) / np.float32(half)))
    return jnp.asarray(np.tile(inv.astype(np.float32), LANES // half)[None, :])


def _rot_half(x, first):
    return jnp.where(first, -pltpu.roll(x, LANES - HEAD // 2, 1), pltpu.roll(x, HEAD // 2, 1))


def _cos_sin(pos_ref, freq_ref):
    ang = pos_ref[...].astype(F32) * freq_ref[...]
    return jnp.cos(ang), jnp.sin(ang)


def _deint(src_ref, dst_ref, d):
    n = S // d
    for r in range(d):
        v = src_ref[pl.ds(r, n, stride=d), :] if d > 1 else src_ref[...]
        dst_ref[r * n:(r + 1) * n, :] = v.astype(dst_ref.dtype)


def _reint(src_ref, dst_ref, d, accumulate):
    n = S // d
    for r in range(d):
        idx = (pl.ds(r, n, stride=d), slice(None)) if d > 1 else (slice(None), slice(None))
        v = src_ref[r * n:(r + 1) * n, :]
        if accumulate:
            dst_ref[idx] = dst_ref[idx] + v
        else:
            dst_ref[idx] = v


def _blk_masks(b, nb):
    qi = lax.broadcasted_iota(jnp.int32, (BLK, BLK), 0)
    ki = lax.broadcasted_iota(jnp.int32, (BLK, BLK), 1)
    has_prev = lax.rem(b, nb) != 0
    return ki <= qi, jnp.logical_and(ki >= qi, has_prev)


def _rope_table(pos, freq):
    def body(pos_ref, freq_ref, cos_ref, sin_ref):
        cos_ref[...], sin_ref[...] = _cos_sin(pos_ref, freq_ref)

    return pl.pallas_call(body, name="rope_table", out_shape=[jax.ShapeDtypeStruct((S, LANES), F32)] * 2,
                          compiler_params=_cp())(pos, freq)


def _deint_heads(src_ref, dst0, dst1, d):
    n = S // d
    hm0 = lax.broadcasted_iota(jnp.int32, (n, LANES), 1) < HEAD
    for r in range(d):
        v = src_ref[pl.ds(r, n, stride=d), :] if d > 1 else src_ref[...]
        dst0[r * n:(r + 1) * n, :] = jnp.where(hm0, v, 0.0).astype(BF16)
        dst1[r * n:(r + 1) * n, :] = jnp.where(hm0, 0.0, v).astype(BF16)


def _reint_prev(src_ref, dst_ref, d):
    n = S // d
    if n == BLK:
        return
    for r in range(d):
        idx = (pl.ds(r, n - BLK, stride=d), slice(None)) if d > 1 else (slice(0, n - BLK), slice(None))
        dst_ref[idx] = dst_ref[idx] + src_ref[r * n + BLK:(r + 1) * n, :]


def _pair_masks():
    qi = lax.broadcasted_iota(jnp.int32, (BLK, 2 * BLK), 0)
    ki = lax.broadcasted_iota(jnp.int32, (BLK, 2 * BLK), 1) & (BLK - 1)
    return ki <= qi, ki >= qi


def _two(ref0, ref1, st, axis):
    return jnp.concatenate([ref0[pl.ds(st, BLK), :], ref1[pl.ds(st, BLK), :]], axis=axis)


ATT_UNROLL = 4


def _att_fwd(proj, cos, sin, w_out):
    def body(q_ref, k_ref, v_ref, cos_ref, sin_ref, w_ref, att_ref, qr_ref, kr_ref, lse_ref, wbf_ref,
             qd, kd0, kd1, vd0, vd1, od, ld, on, ln, wbuf, *wsems):
        wg = _WeightGather(w_ref, wbuf, *wsems)
        pl.when(pl.program_id(0) == 0)(wg.start)
        pl.when(pl.program_id(0) == 1)(wg.forward)
        lane = lax.broadcasted_iota(jnp.int32, (S, LANES), 1)
        first = (lane & (HEAD // 2)) == 0
        cos, sin = cos_ref[...], sin_ref[...]
        q = q_ref[...]
        k = k_ref[...]
        qr_ref[...] = (q * cos + _rot_half(q, first) * sin) * (HEAD ** -0.5)
        kr_ref[...] = k * cos + _rot_half(k, first) * sin
        hm0 = lax.broadcasted_iota(jnp.int32, (BLK, LANES), 1) < HEAD
        top = lax.broadcasted_iota(jnp.int32, (2 * BLK, LANES), 0) < BLK
        ones2 = (top == (lax.broadcasted_iota(jnp.int32, (2 * BLK, LANES), 1) < HEAD)).astype(BF16)
        mc2, mp2 = _pair_masks()

        for pi, d in enumerate(PATTERNS):
            nb = S // d // BLK
            _deint(qr_ref, qd, d)
            _deint_heads(kr_ref, kd0, kd1, d)
            _deint_heads(v_ref, vd0, vd1, d)

            def blk(b, carry):
                st = pl.multiple_of(b * BLK, BLK)
                qb = qd[pl.ds(st, BLK), :]
                sc = jnp.where(mc2, _dot_nt(qb, _two(kd0, kd1, st, 0)), NEG)
                mx = sc
                if nb > 1:
                    stp = pl.multiple_of(jnp.maximum(b - 1, 0) * BLK, BLK)
                    mp = jnp.logical_and(mp2, lax.rem(b, nb) != 0)
                    sp = jnp.where(mp, _dot_nt(qb, _two(kd0, kd1, stp, 0)), NEG)
                    mx = jnp.maximum(sc, sp)
                m0 = jnp.max(mx[:, 0:BLK], axis=1, keepdims=True)
                m1 = jnp.max(mx[:, BLK:2 * BLK], axis=1, keepdims=True)
                mf = jnp.concatenate([jnp.broadcast_to(m0, (BLK, BLK)), jnp.broadcast_to(m1, (BLK, BLK))], axis=1)
                o = _dot(jnp.exp(sc - mf).astype(BF16), jnp.concatenate([_two(vd0, vd1, st, 0), ones2], axis=1))
                if nb > 1:
                    o = o + _dot(jnp.exp(sp - mf).astype(BF16), jnp.concatenate([_two(vd0, vd1, stp, 0), ones2], axis=1))
                l = o[:, LANES:2 * LANES]
                od[pl.ds(st, BLK), :] = o[:, 0:LANES] / l
                ld[pl.ds(st, BLK), :] = jnp.where(hm0, m0, m1) + jnp.log(l)
                return carry

            lax.fori_loop(0, S // BLK, blk, 0, unroll=ATT_UNROLL)
            _reint(od, on.at[pi], d, False)
            _reint(ld, ln.at[pi], d, False)

        l0, l1, l2 = ln[0], ln[1], ln[2]
        m = jnp.maximum(jnp.maximum(l0, l1), l2)
        e0, e1, e2 = jnp.exp(l0 - m), jnp.exp(l1 - m), jnp.exp(l2 - m)
        den = e0 + e1 + e2
        att_ref[...] = (e0 * on[0] + e1 * on[1] + e2 * on[2]) / den
        lse_ref[...] = m + jnp.log(den)

        @pl.when(pl.program_id(0) == NPAIR - 1)
        def _():
            wg.finish()
            wbf_ref[...] = wbuf[...]

    col = lambda c0: pl.BlockSpec((S, LANES), lambda p: (0, c0 + p))
    out = pl.BlockSpec((S, LANES), lambda p: (0, p))
    tab = pl.BlockSpec((S, LANES), lambda p: (0, 0))
    vm = pl.BlockSpec(memory_space=pltpu.VMEM)
    return pl.pallas_call(
        body, name="att_fwd", grid=(NPAIR,),
        in_specs=[col(QB), col(KB), col(VB), tab, tab, vm],
        out_specs=[out, out, out, out, vm],
        out_shape=[jax.ShapeDtypeStruct((S, R), F32)] * 4 + [jax.ShapeDtypeStruct((NCHIP,) + w_out.shape, BF16)],
        scratch_shapes=[pltpu.VMEM((S, LANES), BF16)] * 5 + [pltpu.VMEM((S, LANES), F32)] * 2
        + [pltpu.VMEM((3, S, LANES), F32)] * 2 + [pltpu.VMEM((NCHIP,) + w_out.shape, BF16)] + _WeightGather.SEMS,
        compiler_params=_cp(("arbitrary",)),
    )(proj, proj, proj, cos, sin, w_out)


def _att_bwd(dproj, d_att, att, lse, qr, kr, proj, cos, sin, gw_out4):
    out_units = [(j, j, 0) for j in range(NCHIP)]

    def body(dp_in, do_ref, o_ref, lse_ref, qr_ref, kr_ref, v_ref, cos_ref, sin_ref, gw_ref, dp_ref, gout_ref,
             qd, kd0, kd1, vd0, vd1, dod, lb0d, lb1d, dl0d, dl1d, dqd, dkcd, dkpd, dvcd, dvpd,
             dqn, dkn, dvn, lb0n, lb1n, dl0n, dl1n, stage, sems, gred, *rs_scratch):
        p = pl.program_id(0)
        rs = _ReduceScatter(gw_ref, gred, out_units, *rs_scratch)
        for step, piece in enumerate((rs.start_halves, rs.send_partials, rs.reduce_owned)):
            pl.when(p == step)(piece)

        @pl.when(p == NPAIR - 1)
        def _():
            rs.finish()
            gout_ref[...] = gred[...]
        hms = lax.broadcasted_iota(jnp.int32, (S, LANES), 1) < HEAD
        prod = do_ref[...] * o_ref[...]
        dl0n[...] = jnp.broadcast_to(jnp.sum(jnp.where(hms, prod, 0.0), axis=1, keepdims=True), (S, LANES))
        dl1n[...] = jnp.broadcast_to(jnp.sum(jnp.where(hms, 0.0, prod), axis=1, keepdims=True), (S, LANES))
        lse = lse_ref[...]
        lsw = pltpu.roll(lse, HEAD, 1)
        lb0n[...] = jnp.where(hms, lse, lsw)
        lb1n[...] = jnp.where(hms, lsw, lse)
        dqn[...] = jnp.zeros_like(dqn)
        dkn[...] = jnp.zeros_like(dkn)
        dvn[...] = jnp.zeros_like(dvn)
        hm0 = lax.broadcasted_iota(jnp.int32, (BLK, LANES), 1) < HEAD
        mc2, mp2 = _pair_masks()

        for d in PATTERNS:
            nb = S // d // BLK
            _deint(qr_ref, qd, d)
            _deint_heads(kr_ref, kd0, kd1, d)
            _deint_heads(v_ref, vd0, vd1, d)
            _deint(do_ref, dod, d)
            for src, dst in ((lb0n, lb0d), (lb1n, lb1d), (dl0n, dl0d), (dl1n, dl1d)):
                _deint(src, dst, d)

            def blk(b, carry):
                st = pl.multiple_of(b * BLK, BLK)
                qb, dob = qd[pl.ds(st, BLK), :], dod[pl.ds(st, BLK), :]
                lb, dl = _two(lb0d, lb1d, st, 1), _two(dl0d, dl1d, st, 1)

                def side(stk, mask):
                    k2, v2 = _two(kd0, kd1, stk, 0), _two(vd0, vd1, stk, 0)
                    pk = jnp.where(mask, jnp.exp(_dot_nt(qb, k2) - lb), 0.0)
                    ds = (pk * (_dot_nt(dob, v2) - dl)).astype(BF16)
                    rk, rv = _dot_tn(ds, qb), _dot_tn(pk.astype(BF16), dob)
                    return (_dot(ds, k2), jnp.where(hm0, rk[0:BLK], rk[BLK:2 * BLK]),
                            jnp.where(hm0, rv[0:BLK], rv[BLK:2 * BLK]))

                dq, dkc, dvc = side(st, mc2)
                if nb > 1:
                    stp = pl.multiple_of(jnp.maximum(b - 1, 0) * BLK, BLK)
                    dqp, dkp, dvp = side(stp, jnp.logical_and(mp2, lax.rem(b, nb) != 0))
                    dq = dq + dqp
                    dkpd[pl.ds(st, BLK), :] = dkp
                    dvpd[pl.ds(st, BLK), :] = dvp
                dqd[pl.ds(st, BLK), :] = dq
                dkcd[pl.ds(st, BLK), :] = dkc
                dvcd[pl.ds(st, BLK), :] = dvc
                return carry

            lax.fori_loop(0, S // BLK, blk, 0, unroll=ATT_UNROLL)
            _reint(dqd, dqn, d, True)
            _reint(dkcd, dkn, d, True)
            _reint(dvcd, dvn, d, True)
            _reint_prev(dkpd, dkn, d)
            _reint_prev(dvpd, dvn, d)

        lane = lax.broadcasted_iota(jnp.int32, (S, LANES), 1)
        first = (lane & (HEAD // 2)) == 0
        cos, sin = cos_ref[...], sin_ref[...]
        dq = dqn[...] * (HEAD ** -0.5)
        dk = dkn[...]
        stage[0] = (dq * cos - _rot_half(dq, first) * sin).astype(BF16)
        stage[1] = (dk * cos - _rot_half(dk, first) * sin).astype(BF16)
        stage[2] = dvn[...].astype(BF16)
        copies = [pltpu.make_async_copy(stage.at[j], dp_ref.at[:, pl.ds((2 + j) * R + p * LANES, LANES)], sems.at[j])
                  for j in range(3)]
        for cp in copies:
            cp.start()
        for cp in copies:
            cp.wait()

    blk = pl.BlockSpec((S, LANES), lambda p: (0, p))
    tab = pl.BlockSpec((S, LANES), lambda p: (0, 0))
    vm = pl.BlockSpec(memory_space=pltpu.VMEM)
    _, orows, ocols = gw_out4.shape
    return pl.pallas_call(
        body, name="att_bwd", grid=(NPAIR,),
        in_specs=[pl.BlockSpec(memory_space=pl.ANY), blk, blk, blk, blk, blk,
                  pl.BlockSpec((S, LANES), lambda p: (0, VB + p)), tab, tab, vm],
        out_specs=[pl.BlockSpec(memory_space=pl.ANY), vm],
        out_shape=[jax.ShapeDtypeStruct((S, E), BF16), jax.ShapeDtypeStruct((orows, ocols), F32)],
        scratch_shapes=[pltpu.VMEM((S, LANES), BF16)] * 6 + [pltpu.VMEM((S, LANES), F32)] * 16
        + [pltpu.VMEM((3, S, LANES), BF16), pltpu.SemaphoreType.DMA((3,)), pltpu.VMEM((orows, ocols), F32)]
        + _ReduceScatter.scratch(NCHIP, orows, ocols, 1),
        input_output_aliases={0: 0},
        compiler_params=_cp(("arbitrary",)),
    )(dproj, d_att, att, lse, qr, kr, proj, cos, sin, gw_out4)


def _att_fwd_old(proj, pos, freq):
    def body(q_ref, k_ref, v_ref, pos_ref, freq_ref, att_ref, qr_ref, kr_ref, lse_ref,
             qd, kd, vd, od, ld, on, ln):
        lane = lax.broadcasted_iota(jnp.int32, (S, LANES), 1)
        first = (lane & (HEAD // 2)) == 0
        cos, sin = _cos_sin(pos_ref, freq_ref)
        q = q_ref[...]
        k = k_ref[...]
        qr_ref[...] = (q * cos + _rot_half(q, first) * sin) * (HEAD ** -0.5)
        kr_ref[...] = k * cos + _rot_half(k, first) * sin
        hm0 = lax.broadcasted_iota(jnp.int32, (BLK, LANES), 1) < HEAD

        for pi, d in enumerate(PATTERNS):
            nb = S // d // BLK
            _deint(qr_ref, qd, d)
            _deint(kr_ref, kd, d)
            _deint(v_ref, vd, d)

            def blk(b, carry):
                st = pl.multiple_of(b * BLK, BLK)
                stp = pl.multiple_of(jnp.maximum(b - 1, 0) * BLK, BLK)
                mc, mp = _blk_masks(b, nb)
                qb = qd[pl.ds(st, BLK), :]
                kc, kp = kd[pl.ds(st, BLK), :], kd[pl.ds(stp, BLK), :]
                vc, vp = vd[pl.ds(st, BLK), :], vd[pl.ds(stp, BLK), :]
                outs, lses = [], []
                for hm in (hm0, jnp.logical_not(hm0)):
                    qm = jnp.where(hm, qb, jnp.zeros_like(qb))
                    sc = jnp.where(mc, _dot_nt(qm, kc), NEG)
                    sp = jnp.where(mp, _dot_nt(qm, kp), NEG)
                    m = jnp.maximum(jnp.max(sc, axis=1, keepdims=True), jnp.max(sp, axis=1, keepdims=True))
                    pc, pp = jnp.exp(sc - m), jnp.exp(sp - m)
                    l = jnp.sum(pc, axis=1, keepdims=True) + jnp.sum(pp, axis=1, keepdims=True)
                    o = _dot(pc.astype(BF16), vc) + _dot(pp.astype(BF16), vp)
                    outs.append(o / l)
                    lses.append(m + jnp.log(l))
                od[pl.ds(st, BLK), :] = jnp.where(hm0, outs[0], outs[1])
                ld[pl.ds(st, BLK), :] = jnp.where(hm0, lses[0], lses[1])
                return carry

            lax.fori_loop(0, S // BLK, blk, 0)
            _reint(od, on.at[pi], d, False)
            _reint(ld, ln.at[pi], d, False)

        l0, l1, l2 = ln[0], ln[1], ln[2]
        m = jnp.maximum(jnp.maximum(l0, l1), l2)
        e0, e1, e2 = jnp.exp(l0 - m), jnp.exp(l1 - m), jnp.exp(l2 - m)
        den = e0 + e1 + e2
        att_ref[...] = (e0 * on[0] + e1 * on[1] + e2 * on[2]) / den
        lse_ref[...] = m + jnp.log(den)

    col = lambda c0: pl.BlockSpec((S, LANES), lambda p: (0, c0 + p))
    out = pl.BlockSpec((S, LANES), lambda p: (0, p))
    return pl.pallas_call(
        body, name="att_fwd", grid=(NPAIR,),
        in_specs=[col(QB), col(KB), col(VB), pl.BlockSpec((S, 1), lambda p: (0, 0)),
                  pl.BlockSpec((1, LANES), lambda p: (0, 0))],
        out_specs=[out, out, out, out],
        out_shape=[jax.ShapeDtypeStruct((S, R), F32)] * 4,
        scratch_shapes=[pltpu.VMEM((S, LANES), BF16)] * 3 + [pltpu.VMEM((S, LANES), F32)] * 2
        + [pltpu.VMEM((3, S, LANES), F32)] * 2,
        compiler_params=_cp(("parallel",)),
    )(proj, proj, proj, pos, freq)


def _att_bwd_old(dproj, d_att, att, lse, qr, kr, proj, pos, freq):
    def body(dp_in, do_ref, o_ref, lse_ref, qr_ref, kr_ref, v_ref, pos_ref, freq_ref, dp_ref,
             qd, kd, vd, dod, lsd, prd, dqd, dkd, dvd, dqn, dkn, dvn, prn, stage, sems):
        p = pl.program_id(0)
        prn[...] = do_ref[...] * o_ref[...]
        dqn[...] = jnp.zeros_like(dqn)
        dkn[...] = jnp.zeros_like(dkn)
        dvn[...] = jnp.zeros_like(dvn)
        hm0 = lax.broadcasted_iota(jnp.int32, (BLK, LANES), 1) < HEAD

        for d in PATTERNS:
            nb = S // d // BLK
            _deint(qr_ref, qd, d)
            _deint(kr_ref, kd, d)
            _deint(v_ref, vd, d)
            _deint(do_ref, dod, d)
            _deint(lse_ref, lsd, d)
            _deint(prn, prd, d)
            dkd[...] = jnp.zeros_like(dkd)
            dvd[...] = jnp.zeros_like(dvd)

            def blk(b, carry):
                st = pl.multiple_of(b * BLK, BLK)
                stp = pl.multiple_of(jnp.maximum(b - 1, 0) * BLK, BLK)
                mc, mp = _blk_masks(b, nb)
                qb, dob = qd[pl.ds(st, BLK), :], dod[pl.ds(st, BLK), :]
                kc, kp = kd[pl.ds(st, BLK), :], kd[pl.ds(stp, BLK), :]
                vc, vp = vd[pl.ds(st, BLK), :], vd[pl.ds(stp, BLK), :]
                lsb, prb = lsd[pl.ds(st, BLK), :], prd[pl.ds(st, BLK), :]
                dqs = []
                dkc = dkp = dvc = dvp = None
                for hm in (hm0, jnp.logical_not(hm0)):
                    qm = jnp.where(hm, qb, jnp.zeros_like(qb))
                    dom = jnp.where(hm, dob, jnp.zeros_like(dob))
                    lh = jnp.max(jnp.where(hm, lsb, -3e38), axis=1, keepdims=True)
                    delta = jnp.sum(jnp.where(hm, prb, 0.0), axis=1, keepdims=True)
                    pc = jnp.where(mc, jnp.exp(_dot_nt(qm, kc) - lh), 0.0)
                    pp = jnp.where(mp, jnp.exp(_dot_nt(qm, kp) - lh), 0.0)
                    dsc = (pc * (_dot_nt(dom, vc) - delta)).astype(BF16)
                    dsp = (pp * (_dot_nt(dom, vp) - delta)).astype(BF16)
                    dqs.append(_dot(dsc, kc) + _dot(dsp, kp))
                    acc = lambda t, n: n if t is None else t + n
                    dkc, dkp = acc(dkc, _dot_tn(dsc, qm)), acc(dkp, _dot_tn(dsp, qm))
                    dvc, dvp = acc(dvc, _dot_tn(pc.astype(BF16), dom)), acc(dvp, _dot_tn(pp.astype(BF16), dom))
                dqd[pl.ds(st, BLK), :] = jnp.where(hm0, dqs[0], dqs[1])
                dkd[pl.ds(stp, BLK), :] += dkp
                dvd[pl.ds(stp, BLK), :] += dvp
                dkd[pl.ds(st, BLK), :] += dkc
                dvd[pl.ds(st, BLK), :] += dvc
                return carry

            lax.fori_loop(0, S // BLK, blk, 0)
            _reint(dqd, dqn, d, True)
            _reint(dkd, dkn, d, True)
            _reint(dvd, dvn, d, True)

        lane = lax.broadcasted_iota(jnp.int32, (S, LANES), 1)
        first = (lane & (HEAD // 2)) == 0
        cos, sin = _cos_sin(pos_ref, freq_ref)
        dq = dqn[...] * (HEAD ** -0.5)
        dk = dkn[...]
        stage[0] = (dq * cos - _rot_half(dq, first) * sin).astype(BF16)
        stage[1] = (dk * cos - _rot_half(dk, first) * sin).astype(BF16)
        stage[2] = dvn[...].astype(BF16)
        copies = [pltpu.make_async_copy(stage.at[j], dp_ref.at[:, pl.ds((2 + j) * R + p * LANES, LANES)], sems.at[j])
                  for j in range(3)]
        for cp in copies:
            cp.start()
        for cp in copies:
            cp.wait()

    blk = pl.BlockSpec((S, LANES), lambda p: (0, p))
    return pl.pallas_call(
        body, name="att_bwd", grid=(NPAIR,),
        in_specs=[pl.BlockSpec(memory_space=pl.ANY), blk, blk, blk, blk, blk,
                  pl.BlockSpec((S, LANES), lambda p: (0, VB + p)), pl.BlockSpec((S, 1), lambda p: (0, 0)),
                  pl.BlockSpec((1, LANES), lambda p: (0, 0))],
        out_specs=pl.BlockSpec(memory_space=pl.ANY),
        out_shape=jax.ShapeDtypeStruct((S, E), BF16),
        scratch_shapes=[pltpu.VMEM((S, LANES), BF16)] * 4 + [pltpu.VMEM((S, LANES), F32)] * 9
        + [pltpu.VMEM((3, S, LANES), BF16), pltpu.SemaphoreType.DMA((3,))],
        input_output_aliases={0: 0},
        compiler_params=_cp(("arbitrary",)),
    )(dproj, d_att, att, lse, qr, kr, proj, pos, freq)


def _out_fwd_bwd(ya, att, proj, w_out_bf, x, target, mod, norm_post, norm_att):
    ts = 256

    def body(ya_ref, att_ref, gb_ref, w_ref, x_ref, t_ref, mod_ref, npost_ref, natt_ref,
             gx_ref, dya_ref, datt_ref, dgb_ref, gw_ref, acc_ref):
        i = pl.program_id(0)

        @pl.when(i == 0)
        def _():
            gw_ref[...] = jnp.zeros_like(gw_ref)
            acc_ref[...] = jnp.zeros_like(acc_ref)

        gate = mod_ref[:, 2 * D:3 * D]
        att = att_ref[...]
        gb = gb_ref[...]
        sg = _sigmoid(gb)
        silu = gb * sg
        ybp = att * silu
        yb, ybn, rstd_b = _rms_fwd(ybp, natt_ref[...])
        cat = jnp.concatenate([ya_ref[...], yb.astype(BF16)], axis=1)
        mix = _dot(cat, w_ref[...])
        rn, mn, rstd_m = _rms_fwd(mix, npost_ref[...])
        err = x_ref[...] + gate * rn - t_ref[...]
        dy = err * (1.0 / D)
        gx_ref[...] = dy
        dmix, dnpost = _rms_bwd(dy * gate, mn, rstd_m, npost_ref[...])
        dmb = dmix.astype(BF16)
        gw_ref[...] += _dot_tn(cat, dmb)
        dcat = _dot_nt(dmb, w_ref[...])
        dya_ref[...] = dcat[:, 0:R]
        dybp, dnatt = _rms_bwd(dcat[:, R:2 * R], ybn, rstd_b, natt_ref[...])
        datt_ref[...] = dybp * silu
        dgb_ref[...] = (dybp * att * (sg * (1.0 + gb * (1.0 - sg)))).astype(BF16)
        acc_ref[0:1, :] += jnp.sum(dy * rn, axis=0, keepdims=True)
        acc_ref[1:2, :] += dnpost
        acc_ref[2:3, 0:R] += dnatt
        acc_ref[3:4, :] += jnp.sum(jnp.sum(err * err, axis=1, keepdims=True), axis=0, keepdims=True)

    tile = lambda w: pl.BlockSpec((ts, w), lambda i: (i, 0))
    c0 = lambda shape: pl.BlockSpec(shape, lambda i: (0, 0))
    return pl.pallas_call(
        body, name="out_fwd_bwd", grid=(S // ts,),
        in_specs=[tile(R), tile(R), pl.BlockSpec((ts, R), lambda i: (i, 5)), c0((D, D)), tile(D), tile(D),
                  c0((1, 3 * D)), c0((1, D)), c0((1, R))],
        out_specs=[tile(D), tile(R), tile(R), pl.BlockSpec((ts, R), lambda i: (i, 5)), c0((D, D)), c0((8, D))],
        out_shape=[jax.ShapeDtypeStruct((S, D), F32), jax.ShapeDtypeStruct((S, R), F32),
                   jax.ShapeDtypeStruct((S, R), F32), jax.ShapeDtypeStruct((S, E), BF16),
                   jax.ShapeDtypeStruct((D, D), F32), jax.ShapeDtypeStruct((8, D), F32)],
        compiler_params=_cp(("arbitrary",)),
    )(ya, att, proj, w_out_bf, x, target, mod, norm_post, norm_att)


UC = 256
UPC = EC // UC


def _w_in_units(first, n):
    return [(u - first, u // UPC, (u % UPC) * UC) for u in range(first, first + n)]


def _grad_w_in(hb, dproj, first, n, name):
    ts = 1024

    def body(h_ref, dp_ref, gw_ref):
        @pl.when(pl.program_id(1) == 0)
        def _():
            gw_ref[...] = jnp.zeros_like(gw_ref)

        gw_ref[...] += _dot_tn(h_ref[...], dp_ref[...])

    return pl.pallas_call(
        body, name=name, grid=(n, S // ts),
        in_specs=[pl.BlockSpec((ts, D), lambda j, s: (s, 0)), pl.BlockSpec((ts, UC), lambda j, s: (s, first + j))],
        out_specs=pl.BlockSpec((None, D, UC), lambda j, s: (j, 0, 0)),
        out_shape=jax.ShapeDtypeStruct((n, D, UC), F32),
        compiler_params=_cp(("parallel", "arbitrary")),
    )(hb, dproj)


def _in_proj_bwd(dproj, w_in_bf, x, gx1, mod, norm_pre, gw, gw_units, g_other):
    ts = 256
    nt = S // ts

    def body(dp_ref, w_ref, x_ref, gx1_ref, mod_ref, np_ref, gw_ref, go_ref, gx_ref, acc_ref, gsum_ref,
             gred, *rs_scratch):
        i = pl.program_id(0)
        _ReduceScatter(gw_ref, gred, gw_units, *rs_scratch).at_steps(i, 0, 2, nt - 1, nt - 1, gsum_ref)

        @pl.when(i == nt - 1)
        def _():
            gsum_ref[...] += go_ref[...]

        @pl.when(i == 0)
        def _():
            acc_ref[...] = jnp.zeros_like(acc_ref)

        dh = sum(_dot_nt(dp_ref[:, j * EC:(j + 1) * EC], w_ref[j]) for j in range(NCHIP))
        hp, xn, rstd = _rms_fwd(x_ref[...], np_ref[...])
        dx, dnp = _rms_bwd(dh * (1.0 + mod_ref[:, D:2 * D]), xn, rstd, np_ref[...])
        gx_ref[...] = gx1_ref[...] + dx
        acc_ref[0:1, :] += jnp.sum(dh, axis=0, keepdims=True)
        acc_ref[1:2, :] += jnp.sum(dh * hp, axis=0, keepdims=True)
        acc_ref[2:3, :] += dnp

    tile = lambda w: pl.BlockSpec((ts, w), lambda i: (i, 0))
    c0 = lambda shape: pl.BlockSpec(shape, lambda i: (0, 0))
    vm = pl.BlockSpec(memory_space=pltpu.VMEM)
    return pl.pallas_call(
        body, name="in_proj_bwd", grid=(nt,),
        in_specs=[tile(E), pl.BlockSpec((NCHIP, D, EC), lambda i: (0, 0, 0)), tile(D), tile(D), c0((1, 3 * D)),
                  c0((1, D)), vm, vm],
        out_specs=[tile(D), c0((8, D)), vm],
        out_shape=[jax.ShapeDtypeStruct((S, D), F32), jax.ShapeDtypeStruct((8, D), F32),
                   jax.ShapeDtypeStruct((D, EC), F32)],
        scratch_shapes=[pltpu.VMEM((D, EC), F32)] + _ReduceScatter.scratch(len(gw_units), D, UC, UPC),
        compiler_params=_cp(("arbitrary",)),
    )(dproj, w_in_bf, x, gx1, mod, norm_pre, gw, g_other)


def _block_diag(w):
    n, b, _ = w.shape
    eye = jnp.eye(n, dtype=w.dtype)
    return (eye[:, None, :, None] * w[:, :, None, :]).reshape(n * b, n * b)


def _diag_blocks(m):
    n, b = R // HEAD, HEAD
    return jnp.stack([m[h * b:(h + 1) * b, h * b:(h + 1) * b] for h in range(n)])


def _local_step(x, pos, target, mod, w_in_bf, w_out, conv_w, p):
    wa_d = _block_diag(p["w_rg_a"]).astype(BF16)
    wx_d = _block_diag(p["w_rg_x"]).astype(BF16)
    rec_p = (conv_w, p["conv_b"], wa_d, p["b_rg_a"], wx_d, p["b_rg_x"], p["lru_lambda"], p["norm_rec"])
    cos, sin = _rope_table(pos, _rope_freq())
    proj, hb = _in_proj_fwd(x, mod, p["norm_pre"], w_in_bf)
    h_all, ya = _rec_fwd(proj, *rec_p)
    att, qr, kr, lse, w_out_bf = _att_fwd(proj, cos, sin, w_out)
    gx1, d_ya, d_att, dproj, gw_out, acc_o = _out_fwd_bwd(ya, att, proj, w_out_bf.reshape(D, D), x, target, mod,
                                                           p["norm_post"], p["norm_att"])
    dproj, g_out = _att_bwd(dproj, d_att, att, lse, qr, kr, proj, cos, sin, gw_out.reshape(NCHIP, D // NCHIP, D))
    n_lo = 2 * R // UC
    gw_hi = _grad_w_in(hb, dproj, n_lo, E // UC - n_lo, "grad_w_in_hi")
    dproj, dwa, dwx, sm, g_hi = _rec_bwd(dproj, d_ya, proj, h_all, *rec_p, gw_hi, _w_in_units(n_lo, E // UC - n_lo))
    gw_lo = _grad_w_in(hb, dproj, 0, n_lo, "grad_w_in_lo")
    grad_x, acc_i, g_in = _in_proj_bwd(dproj, w_in_bf, x, gx1, mod, p["norm_pre"], gw_lo, _w_in_units(0, n_lo), g_hi)
    small = dict(b_ada=jnp.concatenate([acc_i[0:1], acc_i[1:2], acc_o[0:1]], axis=1),
                 norm_pre=acc_i[2:3], norm_post=acc_o[1:2], conv_w=sm[8:12], conv_b=sm[4:5],
                 w_rg_a=_diag_blocks(dwa), b_rg_a=sm[0:1], w_rg_x=_diag_blocks(dwx), b_rg_x=sm[1:2],
                 lru_lambda=sm[2:3], norm_rec=sm[3:4], norm_att=acc_o[2:3, 0:R], loss=acc_o[3:4, 0:LANES])
    return grad_x, g_in, g_out, small


def _me():
    return lax.axis_index("x"), lax.axis_index("y"), lax.axis_index("c")


def _flip(v, bit):
    return 1 - v if bit else v


def _peer(rel):
    x, y, c = _me()
    return (_flip(x, rel & 4), _flip(y, rel & 2), _flip(c, rel & 1))


def _remote(src, dst, send_sem, recv_sem, rel):
    return pltpu.make_async_remote_copy(src_ref=src, dst_ref=dst, send_sem=send_sem, recv_sem=recv_sem,
                                        device_id=_peer(rel), device_id_type=MESH)


def _allgather_rows(row, name):
    w = row.shape[1]

    def body(row_ref, out_ref, send_sems, recv_sems, local_sem):
        x, y, c = _me()
        me = 4 * x + 2 * y + c
        mine = pltpu.make_async_copy(row_ref, out_ref.at[pl.ds(me, 1), :], local_sem)
        mine.start()
        sends = [_remote(row_ref, out_ref.at[pl.ds(me, 1), :], send_sems.at[r - 1], recv_sems.at[r - 1], r)
                 for r in range(1, NDEV)]
        for cp in sends:
            cp.start()
        for r in range(1, NDEV):
            px, py, pc = _peer(r)
            src = 4 * px + 2 * py + pc
            _remote(row_ref, out_ref.at[pl.ds(src, 1), :], send_sems.at[r - 1], recv_sems.at[r - 1], r).wait_recv()
        for cp in sends:
            cp.wait_send()
        mine.wait()

    return pl.pallas_call(
        body, name=name,
        in_specs=[pl.BlockSpec(memory_space=pltpu.VMEM)],
        out_specs=pl.BlockSpec(memory_space=pltpu.VMEM),
        out_shape=jax.ShapeDtypeStruct((NDEV, w), row.dtype),
        scratch_shapes=[pltpu.SemaphoreType.DMA((NDEV - 1,)), pltpu.SemaphoreType.DMA((NDEV - 1,)),
                        pltpu.SemaphoreType.DMA],
        compiler_params=pltpu.CompilerParams(vmem_limit_bytes=VMEM_LIMIT),
    )(row)


class _WeightGather:
    SEMS = [pltpu.SemaphoreType.DMA((NCHIP - 1,))] * 4

    def __init__(self, w_ref, out_ref, send_sems, recv_sems, fsend_sems, frecv_sems):
        x, y, c = _me()
        self.w, self.out, self.ci = w_ref, out_ref, 2 * x + y
        self.half = w_ref.shape[0] // 2
        self.r0 = pl.multiple_of(c * self.half, self.half)
        self.r1 = pl.multiple_of((1 - c) * self.half, self.half)
        self.sems = (send_sems, recv_sems, fsend_sems, frecv_sems)

    def _ici(self, chip, k):
        blk = self.out.at[chip, pl.ds(self.r0, self.half), :]
        return _remote(blk, blk, self.sems[0].at[k - 1], self.sems[1].at[k - 1], 2 * k)

    def _d2d(self, chip, start, k):
        blk = self.out.at[chip, pl.ds(start, self.half), :]
        return _remote(blk, blk, self.sems[2].at[k - 1], self.sems[3].at[k - 1], 1)

    def start(self):
        self.out[self.ci] = self.w[...].astype(BF16)
        for k in range(1, NCHIP):
            self._ici(self.ci, k).start()

    def forward(self):
        for k in range(1, NCHIP):
            self._ici(self.ci ^ k, k).wait_recv()
            self._d2d(self.ci ^ k, self.r0, k).start()

    def finish(self):
        for k in range(1, NCHIP):
            self._d2d(self.ci ^ k, self.r1, k).wait_recv()
        for k in range(1, NCHIP):
            self._ici(self.ci, k).wait_send()
            self._d2d(self.ci ^ k, self.r0, k).wait_send()


def _start_gather(crow, w_ada, b_cols, w_in):
    wc = crow.shape[1]

    def body(crow_ref, wada_ref, b_ref, win_ref, g0_ref, mod_ref, wbf_ref,
             modp, modb, cs, cr, ms, mr, ws, wr, fs, fr, local_sems):
        x, y, c = _me()
        ci = 2 * x + y
        me = 2 * ci + c
        wg = _WeightGather(win_ref, wbf_ref, ws, wr, fs, fr)
        mine = pltpu.make_async_copy(crow_ref, g0_ref.at[pl.ds(me, 1), :], local_sems.at[0])
        mine.start()
        csend = [_remote(crow_ref, g0_ref.at[pl.ds(me, 1), :], cs.at[r - 1], cr.at[r - 1], r) for r in range(1, NDEV)]
        for cp in csend:
            cp.start()
        wg.start()
        for r in range(1, NDEV):
            px, py, pc = _peer(r)
            _remote(crow_ref, g0_ref.at[pl.ds(4 * px + 2 * py + pc, 1), :], cs.at[r - 1], cr.at[r - 1], r).wait_recv()
        mine.wait()
        cv = g0_ref[:, 0:D]
        sc = cv * _sigmoid(cv)
        scb = jnp.concatenate([sc, jnp.zeros_like(sc)], axis=0).astype(BF16)
        modp[...] = _dot(scb, wada_ref[...].astype(BF16))[0:NDEV, :] + b_ref[...]
        own = pltpu.make_async_copy(modp.at[pl.ds(me, 1), :], modb.at[ci], local_sems.at[1])
        own.start()
        msend = []
        for k in range(1, NCHIP):
            dst = 2 * (ci ^ k) + c
            cp = _remote(modp.at[pl.ds(dst, 1), :], modb.at[ci], ms.at[k - 1], mr.at[k - 1], 2 * k)
            cp.start()
            msend.append(cp)
        for k in range(1, NCHIP):
            _remote(modp.at[pl.ds(me, 1), :], modb.at[ci ^ k], ms.at[k - 1], mr.at[k - 1], 2 * k).wait_recv()
        own.wait()
        for j in range(NCHIP):
            mod_ref[:, j * EC:(j + 1) * EC] = modb[j]
        wg.forward()
        wg.finish()
        for cp in csend + msend:
            cp.wait_send()

    vm = pl.BlockSpec(memory_space=pltpu.VMEM)
    return pl.pallas_call(
        body, name="start_gather",
        in_specs=[vm] * 4, out_specs=[vm] * 3,
        out_shape=[jax.ShapeDtypeStruct((NDEV, wc), F32), jax.ShapeDtypeStruct((1, 3 * D), F32),
                   jax.ShapeDtypeStruct((NCHIP, D, EC), BF16)],
        scratch_shapes=[pltpu.VMEM((NDEV, EC), F32), pltpu.VMEM((NCHIP, 1, EC), F32),
                        pltpu.SemaphoreType.DMA((NDEV - 1,)), pltpu.SemaphoreType.DMA((NDEV - 1,)),
                        pltpu.SemaphoreType.DMA((NCHIP - 1,)), pltpu.SemaphoreType.DMA((NCHIP - 1,))]
        + _WeightGather.SEMS + [pltpu.SemaphoreType.DMA((2,))],
        compiler_params=pltpu.CompilerParams(vmem_limit_bytes=VMEM_LIMIT),
    )(crow, w_ada, b_cols, w_in)


class _ReduceScatter:
    @staticmethod
    def scratch(n_units, rows, ucols, max_owned):
        half = rows // 2
        return [pltpu.VMEM((n_units, half, ucols), F32), pltpu.VMEM((n_units, half, ucols), BF16),
                pltpu.VMEM((max_owned, NCHIP, half, ucols), BF16),
                pltpu.SemaphoreType.DMA((2,)), pltpu.SemaphoreType.DMA((n_units,)),
                pltpu.SemaphoreType.DMA((n_units, NCHIP)), pltpu.SemaphoreType.DMA((n_units,)),
                pltpu.SemaphoreType.DMA((n_units,))]

    def __init__(self, g_ref, out_ref, units, sib, stage, got, sem1, send2, recv2, send3, recv3):
        x, y, c = _me()
        self.c, self.ci = c, 2 * x + y
        self.g, self.out, self.units = g_ref, out_ref, units
        self.sib, self.stage, self.got = sib, stage, got
        self.sem1, self.send2, self.recv2, self.send3, self.recv3 = sem1, send2, recv2, send3, recv3
        self.half = g_ref.shape[1] // 2
        self.ucols = g_ref.shape[2]
        self.r0 = pl.multiple_of(c * self.half, self.half)
        self.r1 = pl.multiple_of((1 - c) * self.half, self.half)
        self.slot0 = units[0][0]
        assert [u[0] for u in units] == list(range(self.slot0, self.slot0 + len(units)))
        seen = {}
        self.local = []
        for _, owner, _ in units:
            self.local.append(seen.get(owner, 0))
            seen[owner] = seen.get(owner, 0) + 1

    def _halves(self):
        n = len(self.units)
        return _remote(self.g.at[pl.ds(self.slot0, n), pl.ds(self.r1, self.half), :], self.sib,
                       self.sem1.at[0], self.sem1.at[1], 1)

    def _partial(self, i, sender):
        _, owner, _ = self.units[i]
        return pltpu.make_async_remote_copy(
            src_ref=self.stage.at[i], dst_ref=self.got.at[self.local[i], sender],
            send_sem=self.send2.at[i], recv_sem=self.recv2.at[i, sender],
            device_id=(owner // 2, owner % 2, self.c), device_id_type=MESH)

    def _back(self, i, start):
        off = self.units[i][2]
        blk = self.out.at[pl.ds(start, self.half), off:off + self.ucols]
        return _remote(blk, blk, self.send3.at[i], self.recv3.at[i], 1)

    def at_steps(self, step, start, send, reduce, finish, out_ref):
        @pl.when(step == start)
        def _():
            self.out[...] = jnp.zeros_like(self.out)
            self.start_halves()

        pl.when(step == send)(self.send_partials)
        pl.when(step == reduce)(self.reduce_owned)

        @pl.when(step == finish)
        def _():
            self.finish()
            out_ref[...] = self.out[...]

    def start_halves(self):
        self._halves().start()

    def send_partials(self):
        self._halves().wait_recv()
        for i, (slot, owner, _) in enumerate(self.units):
            @pl.when(self.ci != owner)
            def _():
                self.stage[i] = (self.g[slot, pl.ds(self.r0, self.half), :] + self.sib[i]).astype(BF16)
                self._partial(i, self.ci).start()

    def reduce_owned(self):
        for i, (slot, owner, off) in enumerate(self.units):
            @pl.when(self.ci == owner)
            def _():
                rows, cols = pl.ds(self.r0, self.half), slice(off, off + self.ucols)
                self.out[rows, cols] = self.g[slot, pl.ds(self.r0, self.half), :] + self.sib[i]
                for s in range(NCHIP):
                    if s != owner:
                        self._partial(i, s).wait_recv()
                        self.out[rows, cols] += self.got[self.local[i], s].astype(F32)
                self._back(i, self.r0).start()

    def finish(self):
        self._halves().wait_send()
        for i, (_, owner, _) in enumerate(self.units):
            @pl.when(self.ci == owner)
            def _():
                self._back(i, self.r1).wait_recv()
                self._back(i, self.r0).wait_send()

            @pl.when(self.ci != owner)
            def _():
                self._partial(i, self.ci).wait_send()


def _reduce_scatter(g4, name):
    _, rows, cols = g4.shape
    units = [(j, j, 0) for j in range(NCHIP)]

    def body(g_ref, out_ref, *scratch):
        rs = _ReduceScatter(g_ref, out_ref, units, *scratch)
        rs.start_halves()
        rs.send_partials()
        rs.reduce_owned()
        rs.finish()

    return pl.pallas_call(
        body, name=name,
        in_specs=[pl.BlockSpec(memory_space=pltpu.VMEM)],
        out_specs=pl.BlockSpec(memory_space=pltpu.VMEM),
        out_shape=jax.ShapeDtypeStruct((rows, cols), F32),
        scratch_shapes=_ReduceScatter.scratch(NCHIP, rows, cols, 1),
        compiler_params=pltpu.CompilerParams(vmem_limit_bytes=VMEM_LIMIT),
    )(g4)


def _silu_rows(c_ref):
    cv = c_ref[...]
    sc = cv * _sigmoid(cv)
    return jnp.concatenate([sc, jnp.zeros_like(sc)], axis=0).astype(BF16)


def _ada_fwd(cg, w_ada, b_cols):
    def body(c_ref, w_ref, b_ref, o_ref):
        o_ref[...] = _dot(_silu_rows(c_ref), w_ref[...].astype(BF16))[0:NDEV, :] + b_ref[...]

    return pl.pallas_call(body, name="ada_fwd", out_shape=jax.ShapeDtypeStruct((NDEV, EC), F32),
                          compiler_params=_cp())(cg, w_ada, b_cols)


def _ada_bwd(cg, dmod_cols):
    def body(c_ref, d_ref, o_ref):
        dm = d_ref[...]
        dmb = jnp.concatenate([dm, jnp.zeros_like(dm)], axis=0).astype(BF16)
        o_ref[...] = _dot_tn(_silu_rows(c_ref), dmb)

    return pl.pallas_call(body, name="ada_bwd", out_shape=jax.ShapeDtypeStruct((D, EC), F32),
                          compiler_params=_cp())(cg, dmod_cols)


def _sum_rows(g):
    def body(g_ref, o_ref):
        acc = g_ref[0:1, :]
        for r in range(1, NDEV):
            acc = acc + g_ref[r:r + 1, :]
        o_ref[...] = acc

    return pl.pallas_call(body, name="sum_rows", out_shape=jax.ShapeDtypeStruct((1, g.shape[1]), F32),
                          compiler_params=_cp())(g)


def _adamw(w, g, m, v, name):
    rows, cols = w.shape
    tr = 256 if rows % 256 == 0 else rows

    def body(w_ref, g_ref, m_ref, v_ref, d_ref, nm_ref, nv_ref):
        gv = g_ref[...]
        nm = B1 * m_ref[...] + (1.0 - B1) * gv
        nv = B2 * v_ref[...] + (1.0 - B2) * (gv * gv)
        m_hat = nm / (1.0 - B1 ** STEP)
        v_hat = nv / (1.0 - B2 ** STEP)
        d_ref[...] = (-LR) * (m_hat / (jnp.sqrt(v_hat) + ADAM_EPS) + WD * w_ref[...])
        nm_ref[...] = nm
        nv_ref[...] = nv

    spec = pl.BlockSpec((tr, cols), lambda i: (i, 0))
    return pl.pallas_call(
        body, name=name, grid=(rows // tr,), in_specs=[spec] * 4, out_specs=[spec] * 3,
        out_shape=[jax.ShapeDtypeStruct((rows, cols), F32)] * 3,
        compiler_params=_cp(("parallel",)),
    )(w, g, m, v)


SMALL = (("b_ada", 3 * D), ("norm_pre", D), ("norm_post", D), ("conv_w", 4 * R), ("conv_b", R),
         ("w_rg_a", R * HEAD), ("b_rg_a", R), ("w_rg_x", R * HEAD), ("b_rg_x", R), ("lru_lambda", R),
         ("norm_rec", R), ("norm_att", R))
BIG = ("w_ada", "w_in", "w_out")
WEIGHTS = ("w_ada", "b_ada", "norm_pre", "norm_post", "w_in", "conv_w", "conv_b", "w_rg_a", "b_rg_a", "w_rg_x",
           "b_rg_x", "lru_lambda", "norm_rec", "norm_att", "w_out")


def kernel(x, c, positions, w_ada, b_ada, norm_pre, norm_post, w_in, conv_w, conv_b, w_rg_a, b_rg_a, w_rg_x, b_rg_x, lru_lambda, norm_rec, norm_att, w_out, loss_target, m_w_ada, m_b_ada, m_norm_pre, m_norm_post, m_w_in, m_conv_w, m_conv_b, m_w_rg_a, m_b_rg_a, m_w_rg_x, m_b_rg_x, m_lru_lambda, m_norm_rec, m_norm_att, m_w_out, v_w_ada, v_b_ada, v_norm_pre, v_norm_post, v_w_in, v_conv_w, v_conv_b, v_w_rg_a, v_b_rg_a, v_w_rg_x, v_b_rg_x, v_lru_lambda, v_norm_rec, v_norm_att, v_w_out):
    given = dict(locals())
    wts = {n: given[n] for n in WEIGHTS}
    ms = {n: given["m_" + n] for n in WEIGHTS}
    vs = {n: given["v_" + n] for n in WEIGHTS}
    xi, yi, cc = _me()
    chip = 2 * xi + yi
    me = 2 * chip + cc
    cw_loc = R // NCHIP

    b_cols = lax.dynamic_slice(b_ada, (0, chip * EC), (1, EC))
    g0, mod, w_in_bf = _start_gather(jnp.concatenate([c, conv_w.reshape(1, 4 * cw_loc)], axis=1),
                                     w_ada[0], b_cols, w_in[0])
    cg = g0[:, 0:D]
    conv_full = g0[0::2, D:].reshape(NCHIP, 4, cw_loc).transpose(1, 0, 2).reshape(4, R)

    p = dict(norm_pre=norm_pre, norm_post=norm_post, conv_b=conv_b, b_rg_a=b_rg_a, b_rg_x=b_rg_x,
             lru_lambda=lru_lambda, norm_rec=norm_rec, norm_att=norm_att, w_rg_a=w_rg_a[0], w_rg_x=w_rg_x[0])
    grad_x, g_in, g_out, small = _local_step(
        x[0], positions.reshape(S, 1), loss_target[0], mod, w_in_bf, w_out[0], conv_full, p)

    grads = {"w_out": g_out, "w_in": g_in}
    row = jnp.concatenate([small[n].reshape(1, k) for n, k in SMALL + (("loss", LANES),)], axis=1)
    g2 = _allgather_rows(row, "gather_small")
    tot = _sum_rows(g2)
    grads["w_ada"] = _ada_bwd(cg, lax.dynamic_slice(g2, (0, chip * EC), (NDEV, EC)))
    off = 0
    for n, k in SMALL:
        grads[n] = tot[:, off:off + k]
        off += k
    loss = tot[0, off] * (0.5 / D)
    grads["conv_w"] = lax.dynamic_slice(grads["conv_w"].reshape(4, R), (0, chip * cw_loc), (4, cw_loc))

    delta, new_m, new_v = {}, {}, {}
    for n in BIG:
        delta[n], new_m[n], new_v[n] = _adamw(wts[n][0], grads[n], ms[n][0], vs[n][0], "adamw_" + n)
    pack = lambda d: jnp.concatenate([d[n].reshape(1, -1) for n, _ in SMALL], axis=1).reshape(-1, LANES)
    pd, pm, pv = _adamw(pack(wts), pack(grads), pack(ms), pack(vs), "adamw_small")
    off = 0
    for n, _ in SMALL:
        k = wts[n].size
        for dst, src in ((delta, pd), (new_m, pm), (new_v, pv)):
            dst[n] = src.reshape(1, -1)[:, off:off + k]
        off += k
    out = lambda d: [d[n].reshape(wts[n].shape) for n in WEIGHTS]
    return (loss, grad_x.reshape(x.shape), *out(grads), *out(delta), *out(new_m), *out(new_v))
```

```python
import functools

import numpy as np
import jax
import jax.numpy as jnp
from jax import lax
from jax.experimental import pallas as pl
from jax.experimental.pallas import tpu as pltpu

F32 = jnp.float32
BF16 = jnp.bfloat16

S = 2048
D = 1024
E = 3072
R = 512
NDEV = 8
NCHIP = 4
EC = 768
LRU_C = 8.0
EPS = 1e-6
NEG = -1e30
HEAD = 64
BLK = 128
PATTERNS = (1, 4, 16)
ROPE_THETA = 10000.0
LANES = 128
VMEM_LIMIT = 56 * 1024 * 1024

B1, B2, LR, WD, ADAM_EPS, STEP = 0.9, 0.999, 0.001, 0.01, 1e-8, 10
MESH = pl.DeviceIdType.MESH


def _cp(sem=None, **kw):
    return pltpu.CompilerParams(dimension_semantics=sem, vmem_limit_bytes=VMEM_LIMIT, **kw)


def _dot(a, b):
    return jnp.dot(a, b, preferred_element_type=F32)


def _dot_nt(a, b):
    return lax.dot_general(a, b, (((1,), (1,)), ((), ())), preferred_element_type=F32)


def _dot_tn(a, b):
    return lax.dot_general(a, b, (((0,), (0,)), ((), ())), preferred_element_type=F32)


def _sigmoid(x):
    return 1.0 / (1.0 + jnp.exp(-x))


def _expm1(x):
    poly = x * (1.0 + x * (0.5 + x * (1.0 / 6 + x * (1.0 / 24 + x * (1.0 / 120 + x * (1.0 / 720))))))
    return jnp.where(jnp.abs(x) < 0.3, poly, jnp.exp(x) - 1.0)


def _rms_fwd(v, g):
    rstd = lax.rsqrt(jnp.mean(v * v, axis=-1, keepdims=True) + EPS)
    vn = v * rstd
    return vn * g, vn, rstd


def _rms_bwd(dy, vn, rstd, g):
    dvn = dy * g
    dv = rstd * (dvn - vn * jnp.mean(dvn * vn, axis=-1, keepdims=True))
    return dv, jnp.sum(dy * vn, axis=0, keepdims=True)


def _in_proj_fwd(x, mod, norm_pre, w_in_bf):
    ts = 256

    def body(x_ref, mod_ref, np_ref, w_ref, proj_ref, ht_ref):
        hp, _, _ = _rms_fwd(x_ref[...], np_ref[...])
        h = hp * (1.0 + mod_ref[:, D:2 * D]) + mod_ref[:, 0:D]
        hb = h.astype(BF16)
        ht_ref[...] = h.T.astype(BF16)
        for j in range(NCHIP):
            proj_ref[:, j * EC:(j + 1) * EC] = _dot(hb, w_ref[j])

    return pl.pallas_call(
        body, name="in_proj_fwd", grid=(S // ts,),
        in_specs=[pl.BlockSpec((ts, D), lambda i: (i, 0)), pl.BlockSpec((1, 3 * D), lambda i: (0, 0)),
                  pl.BlockSpec((1, D), lambda i: (0, 0)), pl.BlockSpec((NCHIP, D, EC), lambda i: (0, 0, 0))],
        out_specs=[pl.BlockSpec((ts, E), lambda i: (i, 0)), pl.BlockSpec((D, ts), lambda i: (0, i))],
        out_shape=[jax.ShapeDtypeStruct((S, E), F32), jax.ShapeDtypeStruct((D, S), BF16)],
        compiler_params=_cp(("parallel",)),
    )(x, mod, norm_pre, w_in_bf)


RT = 256


def _shift_down(cur, prev8, j, row):
    if j == 0:
        return cur
    top = jnp.tile(pltpu.roll(prev8, j, 0), (RT // 8, 1))
    return jnp.where(row >= j, pltpu.roll(cur, j, 0), top)


def _shift_up(cur, next8, j, row):
    if j == 0:
        return cur
    bot = jnp.tile(pltpu.roll(next8, 8 - j, 0), (RT // 8, 1))
    return jnp.where(row < RT - j, pltpu.roll(cur, RT - j, 0), bot)


def _rec_gates(xp, xprev8, row, cw_ref, cb_ref, wa_ref, ba_ref, wx_ref, bx_ref, lam_ref):
    xa = cb_ref[...] + sum(cw_ref[3 - j:4 - j, :] * _shift_down(xp, xprev8, j, row) for j in range(4))
    xab = xa.astype(BF16)
    r = _sigmoid(_dot(xab, wa_ref[...]) + ba_ref[...])
    ig = _sigmoid(_dot(xab, wx_ref[...]) + bx_ref[...])
    nl = -lam_ref[...]
    sp = jnp.maximum(nl, 0.0) + jnp.log1p(jnp.exp(-jnp.abs(nl)))
    la = (-LRU_C) * r * sp
    a = jnp.exp(la)
    mult = jnp.sqrt(-_expm1(2.0 * la))
    return dict(xa=xa, xab=xab, r=r, ig=ig, sp=sp, la=la, a=a, mult=mult)


def _scan_fwd(a, u, row):
    sh = 1
    while sh < RT:
        a_s = jnp.where(row >= sh, pltpu.roll(a, sh, 0), 1.0)
        u_s = jnp.where(row >= sh, pltpu.roll(u, sh, 0), 0.0)
        u = a * u_s + u
        a = a * a_s
        sh *= 2
    return a, u


def _scan_bwd(al, g, row):
    sh = 1
    while sh < RT:
        al_s = jnp.where(row < RT - sh, pltpu.roll(al, RT - sh, 0), 1.0)
        g_s = jnp.where(row < RT - sh, pltpu.roll(g, RT - sh, 0), 0.0)
        g = g + al * g_s
        al = al * al_s
        sh *= 2
    return g


def _rec_fwd(proj, conv_w, conv_b, wa_d, ba, wx_d, bx, lam, norm_rec):
    nt = S // RT

    def body(p_ref, cw_ref, cb_ref, wa_ref, ba_ref, wx_ref, bx_ref, lam_ref, nr_ref,
             h_ref, ya_ref, prev8, hc):
        i = pl.program_id(0)

        @pl.when(i == 0)
        def _():
            prev8[...] = jnp.zeros_like(prev8)
            hc[...] = jnp.zeros_like(hc)

        row = lax.broadcasted_iota(jnp.int32, (RT, R), 0)
        xp = p_ref[:, 0:R]
        ga = p_ref[:, R:2 * R]
        f = _rec_gates(xp, prev8[...], row, cw_ref, cb_ref, wa_ref, ba_ref, wx_ref, bx_ref, lam_ref)
        u = f["mult"] * (f["ig"] * f["xa"])
        acum, hh = _scan_fwd(f["a"], u, row)
        h = hh + acum * hc[0:1, :]
        h_ref[...] = h
        hc[0:1, :] = h_ref[RT - 1:RT, :]
        prev8[...] = p_ref[RT - 8:RT, 0:R]
        yp = h * (ga * _sigmoid(ga))
        ya, _, _ = _rms_fwd(yp, nr_ref[...])
        ya_ref[...] = ya.astype(BF16)

    row1 = lambda n: pl.BlockSpec((1, n), lambda i: (0, 0))
    return pl.pallas_call(
        body, name="rec_fwd", grid=(nt,),
        in_specs=[pl.BlockSpec((RT, 2 * R), lambda i: (i, 0)), pl.BlockSpec((4, R), lambda i: (0, 0)), row1(R),
                  pl.BlockSpec((R, R), lambda i: (0, 0)), row1(R), pl.BlockSpec((R, R), lambda i: (0, 0)), row1(R),
                  row1(R), row1(R)],
        out_specs=[pl.BlockSpec((RT, R), lambda i: (i, 0)), pl.BlockSpec((RT, R), lambda i: (i, 0))],
        out_shape=[jax.ShapeDtypeStruct((S, R), F32), jax.ShapeDtypeStruct((S, R), BF16)],
        scratch_shapes=[pltpu.VMEM((8, R), F32), pltpu.VMEM((8, R), F32)],
        compiler_params=_cp(("arbitrary",)),
    )(proj, conv_w, conv_b, wa_d, ba, wx_d, bx, lam, norm_rec)


def _rec_bwd(dproj, d_ya, proj, h_all, conv_w, conv_b, wa_d, ba, wx_d, bx, lam, norm_rec):
    nt = S // RT

    def body(dp_in, dya_ref, p_ref, pprev_ref, h_ref, hprev_ref, cw_ref, cb_ref, wa_ref, ba_ref, wx_ref, bx_ref,
             lam_ref, nr_ref, dp_ref, dwa_ref, dwx_ref, sm_ref, nxt8, cg):
        i = pl.program_id(0)
        ti = nt - 1 - i

        @pl.when(i == 0)
        def _():
            nxt8[...] = jnp.zeros_like(nxt8)
            cg[...] = jnp.zeros_like(cg)
            dwa_ref[...] = jnp.zeros_like(dwa_ref)
            dwx_ref[...] = jnp.zeros_like(dwx_ref)
            sm_ref[...] = jnp.zeros_like(sm_ref)

        row = lax.broadcasted_iota(jnp.int32, (RT, R), 0)
        first = (ti > 0).astype(F32)
        xprev8 = pprev_ref[...] * first
        hprev8 = hprev_ref[...] * first
        xp = p_ref[:, 0:R]
        ga = p_ref[:, R:2 * R]
        f = _rec_gates(xp, xprev8, row, cw_ref, cb_ref, wa_ref, ba_ref, wx_ref, bx_ref, lam_ref)
        xa, r, ig, a, mult = f["xa"], f["r"], f["ig"], f["a"], f["mult"]
        h = h_ref[...]
        sg = _sigmoid(ga)
        gate = ga * sg
        yp = h * gate
        _, ypn, rstd = _rms_fwd(yp, nr_ref[...])
        d_yp, dnr = _rms_bwd(dya_ref[...], ypn, rstd, nr_ref[...])
        d_ga = d_yp * h * (sg * (1.0 + ga * (1.0 - sg)))
        dh = d_yp * gate + jnp.where(row == RT - 1, cg[0:1, :], 0.0)
        al = jnp.where(row < RT - 1, pltpu.roll(a, RT - 1, 0), 0.0)
        g = _scan_bwd(al, dh, row)
        cg[0:1, :] = jnp.sum(jnp.where(row == 0, a * g, 0.0), axis=0, keepdims=True)
        h_m1 = _shift_down(h, hprev8, 1, row)
        da = g * h_m1
        ix = ig * xa
        d_mult = g * ix
        d_ig = g * mult * xa
        d_xa = g * mult * ig
        d_la = da * a - d_mult * (a * a) / mult
        d_r = d_la * ((-LRU_C) * f["sp"])
        dsp = jnp.sum(d_la * ((-LRU_C) * r), axis=0, keepdims=True)
        dlam = dsp * (-_sigmoid(-lam_ref[...]))
        d_za = d_r * r * (1.0 - r)
        d_zx = d_ig * ig * (1.0 - ig)
        dzab = d_za.astype(BF16)
        dzxb = d_zx.astype(BF16)
        dwa_ref[...] += _dot_tn(f["xab"], dzab)
        dwx_ref[...] += _dot_tn(f["xab"], dzxb)
        d_xa = d_xa + _dot_nt(dzab, wa_ref[...]) + _dot_nt(dzxb, wx_ref[...])
        d_xp = sum(cw_ref[3 - j:4 - j, :] * _shift_up(d_xa, nxt8[...], j, row) for j in range(4))
        dcw = [jnp.sum(d_xa * _shift_down(xp, xprev8, 3 - k, row), axis=0, keepdims=True) for k in range(4)]
        dp_ref[:, 0:R] = d_xp.astype(BF16)
        dp_ref[:, R:2 * R] = d_ga.astype(BF16)
        dp8 = d_xa[0:8, :]
        nxt8[...] = dp8
        sm_ref[0:1, :] += jnp.sum(d_za, axis=0, keepdims=True)
        sm_ref[1:2, :] += jnp.sum(d_zx, axis=0, keepdims=True)
        sm_ref[2:3, :] += dlam
        sm_ref[3:4, :] += dnr
        sm_ref[4:5, :] += jnp.sum(d_xa, axis=0, keepdims=True)
        for k in range(4):
            sm_ref[8 + k:9 + k, :] += dcw[k]

    c0 = lambda shape: pl.BlockSpec(shape, lambda i: (0, 0))
    rev = lambda i: nt - 1 - i
    prev8 = lambda i: (jnp.maximum((nt - 1 - i) * (RT // 8) - 1, 0), 0)
    return pl.pallas_call(
        body, name="rec_bwd", grid=(nt,),
        in_specs=[pl.BlockSpec(memory_space=pl.ANY),
                  pl.BlockSpec((RT, R), lambda i: (rev(i), 0)),
                  pl.BlockSpec((RT, 2 * R), lambda i: (rev(i), 0)), pl.BlockSpec((8, R), prev8),
                  pl.BlockSpec((RT, R), lambda i: (rev(i), 0)), pl.BlockSpec((8, R), prev8),
                  c0((4, R)), c0((1, R)), c0((R, R)), c0((1, R)), c0((R, R)), c0((1, R)), c0((1, R)), c0((1, R))],
        out_specs=[pl.BlockSpec((RT, 2 * R), lambda i: (rev(i), 0)), c0((R, R)), c0((R, R)), c0((16, R))],
        out_shape=[jax.ShapeDtypeStruct((S, E), BF16), jax.ShapeDtypeStruct((R, R), F32),
                   jax.ShapeDtypeStruct((R, R), F32), jax.ShapeDtypeStruct((16, R), F32)],
        scratch_shapes=[pltpu.VMEM((8, R), F32), pltpu.VMEM((8, R), F32)],
        input_output_aliases={0: 0},
        compiler_params=_cp(("arbitrary",)),
    )(dproj, d_ya, proj, proj, h_all, h_all, conv_w, conv_b, wa_d, ba, wx_d, bx, lam, norm_rec)


NPAIR = R // LANES
QB, KB, VB, GB = 2 * R // LANES, 3 * R // LANES, 4 * R // LANES, 5 * R // LANES


def _rope_freq():
    half = HEAD // 2
    inv = np.float32(ROPE_THETA) ** (-(np.arange(half, dtype=np.float32) / np.float32(half)))
    return jnp.asarray(np.tile(inv.astype(np.float32), LANES // half)[None, :])


def _rot_half(x, first):
    return jnp.where(first, -pltpu.roll(x, LANES - HEAD // 2, 1), pltpu.roll(x, HEAD // 2, 1))


def _cos_sin(pos_ref, freq_ref):
    ang = pos_ref[...].astype(F32) * freq_ref[...]
    return jnp.cos(ang), jnp.sin(ang)


def _deint(src_ref, dst_ref, d):
    n = S // d
    for r in range(d):
        v = src_ref[pl.ds(r, n, stride=d), :] if d > 1 else src_ref[...]
        dst_ref[r * n:(r + 1) * n, :] = v.astype(dst_ref.dtype)


def _reint(src_ref, dst_ref, d, accumulate):
    n = S // d
    for r in range(d):
        idx = (pl.ds(r, n, stride=d), slice(None)) if d > 1 else (slice(None), slice(None))
        v = src_ref[r * n:(r + 1) * n, :]
        if accumulate:
            dst_ref[idx] = dst_ref[idx] + v
        else:
            dst_ref[idx] = v


def _blk_masks(b, nb):
    qi = lax.broadcasted_iota(jnp.int32, (BLK, BLK), 0)
    ki = lax.broadcasted_iota(jnp.int32, (BLK, BLK), 1)
    has_prev = lax.rem(b, nb) != 0
    return ki <= qi, jnp.logical_and(ki >= qi, has_prev)


def _rope_table(pos, freq):
    def body(pos_ref, freq_ref, cos_ref, sin_ref):
        cos_ref[...], sin_ref[...] = _cos_sin(pos_ref, freq_ref)

    return pl.pallas_call(body, name="rope_table", out_shape=[jax.ShapeDtypeStruct((S, LANES), F32)] * 2,
                          compiler_params=_cp())(pos, freq)


def _deint_heads(src_ref, dst0, dst1, d):
    n = S // d
    hm0 = lax.broadcasted_iota(jnp.int32, (n, LANES), 1) < HEAD
    for r in range(d):
        v = src_ref[pl.ds(r, n, stride=d), :] if d > 1 else src_ref[...]
        dst0[r * n:(r + 1) * n, :] = jnp.where(hm0, v, 0.0).astype(BF16)
        dst1[r * n:(r + 1) * n, :] = jnp.where(hm0, 0.0, v).astype(BF16)


def _reint_prev(src_ref, dst_ref, d):
    n = S // d
    if n == BLK:
        return
    for r in range(d):
        idx = (pl.ds(r, n - BLK, stride=d), slice(None)) if d > 1 else (slice(0, n - BLK), slice(None))
        dst_ref[idx] = dst_ref[idx] + src_ref[r * n + BLK:(r + 1) * n, :]


def _pair_masks():
    qi = lax.broadcasted_iota(jnp.int32, (BLK, 2 * BLK), 0)
    ki = lax.broadcasted_iota(jnp.int32, (BLK, 2 * BLK), 1) & (BLK - 1)
    return ki <= qi, ki >= qi


def _two(ref0, ref1, st, axis):
    return jnp.concatenate([ref0[pl.ds(st, BLK), :], ref1[pl.ds(st, BLK), :]], axis=axis)


ATT_UNROLL = 4


def _att_fwd(proj, cos, sin, w_out):
    def body(q_ref, k_ref, v_ref, cos_ref, sin_ref, w_ref, att_ref, qr_ref, kr_ref, lse_ref, wbf_ref,
             qd, kd0, kd1, vd0, vd1, od, ld, on, ln, wbuf, *wsems):
        wg = _WeightGather(w_ref, wbuf, *wsems)
        pl.when(pl.program_id(0) == 0)(wg.start)
        pl.when(pl.program_id(0) == 1)(wg.forward)
        lane = lax.broadcasted_iota(jnp.int32, (S, LANES), 1)
        first = (lane & (HEAD // 2)) == 0
        cos, sin = cos_ref[...], sin_ref[...]
        q = q_ref[...]
        k = k_ref[...]
        qr_ref[...] = (q * cos + _rot_half(q, first) * sin) * (HEAD ** -0.5)
        kr_ref[...] = k * cos + _rot_half(k, first) * sin
        hm0 = lax.broadcasted_iota(jnp.int32, (BLK, LANES), 1) < HEAD
        top = lax.broadcasted_iota(jnp.int32, (2 * BLK, LANES), 0) < BLK
        ones2 = (top == (lax.broadcasted_iota(jnp.int32, (2 * BLK, LANES), 1) < HEAD)).astype(BF16)
        mc2, mp2 = _pair_masks()

        for pi, d in enumerate(PATTERNS):
            nb = S // d // BLK
            _deint(qr_ref, qd, d)
            _deint_heads(kr_ref, kd0, kd1, d)
            _deint_heads(v_ref, vd0, vd1, d)

            def blk(b, carry):
                st = pl.multiple_of(b * BLK, BLK)
                qb = qd[pl.ds(st, BLK), :]
                sc = jnp.where(mc2, _dot_nt(qb, _two(kd0, kd1, st, 0)), NEG)
                mx = sc
                if nb > 1:
                    stp = pl.multiple_of(jnp.maximum(b - 1, 0) * BLK, BLK)
                    mp = jnp.logical_and(mp2, lax.rem(b, nb) != 0)
                    sp = jnp.where(mp, _dot_nt(qb, _two(kd0, kd1, stp, 0)), NEG)
                    mx = jnp.maximum(sc, sp)
                m0 = jnp.max(mx[:, 0:BLK], axis=1, keepdims=True)
                m1 = jnp.max(mx[:, BLK:2 * BLK], axis=1, keepdims=True)
                mf = jnp.concatenate([jnp.broadcast_to(m0, (BLK, BLK)), jnp.broadcast_to(m1, (BLK, BLK))], axis=1)
                o = _dot(jnp.exp(sc - mf).astype(BF16), jnp.concatenate([_two(vd0, vd1, st, 0), ones2], axis=1))
                if nb > 1:
                    o = o + _dot(jnp.exp(sp - mf).astype(BF16), jnp.concatenate([_two(vd0, vd1, stp, 0), ones2], axis=1))
                l = o[:, LANES:2 * LANES]
                od[pl.ds(st, BLK), :] = o[:, 0:LANES] / l
                ld[pl.ds(st, BLK), :] = jnp.where(hm0, m0, m1) + jnp.log(l)
                return carry

            lax.fori_loop(0, S // BLK, blk, 0, unroll=ATT_UNROLL)
            _reint(od, on.at[pi], d, False)
            _reint(ld, ln.at[pi], d, False)

        l0, l1, l2 = ln[0], ln[1], ln[2]
        m = jnp.maximum(jnp.maximum(l0, l1), l2)
        e0, e1, e2 = jnp.exp(l0 - m), jnp.exp(l1 - m), jnp.exp(l2 - m)
        den = e0 + e1 + e2
        att_ref[...] = (e0 * on[0] + e1 * on[1] + e2 * on[2]) / den
        lse_ref[...] = m + jnp.log(den)

        @pl.when(pl.program_id(0) == NPAIR - 1)
        def _():
            wg.finish()
            wbf_ref[...] = wbuf[...]

    col = lambda c0: pl.BlockSpec((S, LANES), lambda p: (0, c0 + p))
    out = pl.BlockSpec((S, LANES), lambda p: (0, p))
    tab = pl.BlockSpec((S, LANES), lambda p: (0, 0))
    vm = pl.BlockSpec(memory_space=pltpu.VMEM)
    return pl.pallas_call(
        body, name="att_fwd", grid=(NPAIR,),
        in_specs=[col(QB), col(KB), col(VB), tab, tab, vm],
        out_specs=[out, out, out, out, vm],
        out_shape=[jax.ShapeDtypeStruct((S, R), F32)] * 4 + [jax.ShapeDtypeStruct((NCHIP,) + w_out.shape, BF16)],
        scratch_shapes=[pltpu.VMEM((S, LANES), BF16)] * 5 + [pltpu.VMEM((S, LANES), F32)] * 2
        + [pltpu.VMEM((3, S, LANES), F32)] * 2 + [pltpu.VMEM((NCHIP,) + w_out.shape, BF16)] + _WeightGather.SEMS,
        compiler_params=_cp(("arbitrary",)),
    )(proj, proj, proj, cos, sin, w_out)


def _att_bwd(dproj, d_att, att, lse, qr, kr, proj, cos, sin, gw_out4):
    out_units = [(j, j, 0) for j in range(NCHIP)]

    def body(dp_in, do_ref, o_ref, lse_ref, qr_ref, kr_ref, v_ref, cos_ref, sin_ref, gw_ref, dp_ref, gout_ref,
             qd, kd0, kd1, vd0, vd1, dod, lb0d, lb1d, dl0d, dl1d, dqd, dkcd, dkpd, dvcd, dvpd,
             dqn, dkn, dvn, lb0n, lb1n, dl0n, dl1n, stage, sems, gred, *rs_scratch):
        p = pl.program_id(0)
        rs = _ReduceScatter(gw_ref, gred, out_units, *rs_scratch)
        for step, piece in enumerate((rs.start_halves, rs.send_partials, rs.reduce_owned)):
            pl.when(p == step)(piece)

        @pl.when(p == NPAIR - 1)
        def _():
            rs.finish()
            gout_ref[...] = gred[...]
        hms = lax.broadcasted_iota(jnp.int32, (S, LANES), 1) < HEAD
        prod = do_ref[...] * o_ref[...]
        dl0n[...] = jnp.broadcast_to(jnp.sum(jnp.where(hms, prod, 0.0), axis=1, keepdims=True), (S, LANES))
        dl1n[...] = jnp.broadcast_to(jnp.sum(jnp.where(hms, 0.0, prod), axis=1, keepdims=True), (S, LANES))
        lse = lse_ref[...]
        lsw = pltpu.roll(lse, HEAD, 1)
        lb0n[...] = jnp.where(hms, lse, lsw)
        lb1n[...] = jnp.where(hms, lsw, lse)
        dqn[...] = jnp.zeros_like(dqn)
        dkn[...] = jnp.zeros_like(dkn)
        dvn[...] = jnp.zeros_like(dvn)
        hm0 = lax.broadcasted_iota(jnp.int32, (BLK, LANES), 1) < HEAD
        mc2, mp2 = _pair_masks()

        for d in PATTERNS:
            nb = S // d // BLK
            _deint(qr_ref, qd, d)
            _deint_heads(kr_ref, kd0, kd1, d)
            _deint_heads(v_ref, vd0, vd1, d)
            _deint(do_ref, dod, d)
            for src, dst in ((lb0n, lb0d), (lb1n, lb1d), (dl0n, dl0d), (dl1n, dl1d)):
                _deint(src, dst, d)

            def blk(b, carry):
                st = pl.multiple_of(b * BLK, BLK)
                qb, dob = qd[pl.ds(st, BLK), :], dod[pl.ds(st, BLK), :]
                lb, dl = _two(lb0d, lb1d, st, 1), _two(dl0d, dl1d, st, 1)

                def side(stk, mask):
                    k2, v2 = _two(kd0, kd1, stk, 0), _two(vd0, vd1, stk, 0)
                    pk = jnp.where(mask, jnp.exp(_dot_nt(qb, k2) - lb), 0.0)
                    ds = (pk * (_dot_nt(dob, v2) - dl)).astype(BF16)
                    rk, rv = _dot_tn(ds, qb), _dot_tn(pk.astype(BF16), dob)
                    return (_dot(ds, k2), jnp.where(hm0, rk[0:BLK], rk[BLK:2 * BLK]),
                            jnp.where(hm0, rv[0:BLK], rv[BLK:2 * BLK]))

                dq, dkc, dvc = side(st, mc2)
                if nb > 1:
                    stp = pl.multiple_of(jnp.maximum(b - 1, 0) * BLK, BLK)
                    dqp, dkp, dvp = side(stp, jnp.logical_and(mp2, lax.rem(b, nb) != 0))
                    dq = dq + dqp
                    dkpd[pl.ds(st, BLK), :] = dkp
                    dvpd[pl.ds(st, BLK), :] = dvp
                dqd[pl.ds(st, BLK), :] = dq
                dkcd[pl.ds(st, BLK), :] = dkc
                dvcd[pl.ds(st, BLK), :] = dvc
                return carry

            lax.fori_loop(0, S // BLK, blk, 0, unroll=ATT_UNROLL)
            _reint(dqd, dqn, d, True)
            _reint(dkcd, dkn, d, True)
            _reint(dvcd, dvn, d, True)
            _reint_prev(dkpd, dkn, d)
            _reint_prev(dvpd, dvn, d)

        lane = lax.broadcasted_iota(jnp.int32, (S, LANES), 1)
        first = (lane & (HEAD // 2)) == 0
        cos, sin = cos_ref[...], sin_ref[...]
        dq = dqn[...] * (HEAD ** -0.5)
        dk = dkn[...]
        stage[0] = (dq * cos - _rot_half(dq, first) * sin).astype(BF16)
        stage[1] = (dk * cos - _rot_half(dk, first) * sin).astype(BF16)
        stage[2] = dvn[...].astype(BF16)
        copies = [pltpu.make_async_copy(stage.at[j], dp_ref.at[:, pl.ds((2 + j) * R + p * LANES, LANES)], sems.at[j])
                  for j in range(3)]
        for cp in copies:
            cp.start()
        for cp in copies:
            cp.wait()

    blk = pl.BlockSpec((S, LANES), lambda p: (0, p))
    tab = pl.BlockSpec((S, LANES), lambda p: (0, 0))
    vm = pl.BlockSpec(memory_space=pltpu.VMEM)
    _, orows, ocols = gw_out4.shape
    return pl.pallas_call(
        body, name="att_bwd", grid=(NPAIR,),
        in_specs=[pl.BlockSpec(memory_space=pl.ANY), blk, blk, blk, blk, blk,
                  pl.BlockSpec((S, LANES), lambda p: (0, VB + p)), tab, tab, vm],
        out_specs=[pl.BlockSpec(memory_space=pl.ANY), vm],
        out_shape=[jax.ShapeDtypeStruct((S, E), BF16), jax.ShapeDtypeStruct((orows, ocols), F32)],
        scratch_shapes=[pltpu.VMEM((S, LANES), BF16)] * 6 + [pltpu.VMEM((S, LANES), F32)] * 16
        + [pltpu.VMEM((3, S, LANES), BF16), pltpu.SemaphoreType.DMA((3,)), pltpu.VMEM((orows, ocols), F32)]
        + _ReduceScatter.scratch(NCHIP, orows, ocols, 1),
        input_output_aliases={0: 0},
        compiler_params=_cp(("arbitrary",)),
    )(dproj, d_att, att, lse, qr, kr, proj, cos, sin, gw_out4)


def _att_fwd_old(proj, pos, freq):
    def body(q_ref, k_ref, v_ref, pos_ref, freq_ref, att_ref, qr_ref, kr_ref, lse_ref,
             qd, kd, vd, od, ld, on, ln):
        lane = lax.broadcasted_iota(jnp.int32, (S, LANES), 1)
        first = (lane & (HEAD // 2)) == 0
        cos, sin = _cos_sin(pos_ref, freq_ref)
        q = q_ref[...]
        k = k_ref[...]
        qr_ref[...] = (q * cos + _rot_half(q, first) * sin) * (HEAD ** -0.5)
        kr_ref[...] = k * cos + _rot_half(k, first) * sin
        hm0 = lax.broadcasted_iota(jnp.int32, (BLK, LANES), 1) < HEAD

        for pi, d in enumerate(PATTERNS):
            nb = S // d // BLK
            _deint(qr_ref, qd, d)
            _deint(kr_ref, kd, d)
            _deint(v_ref, vd, d)

            def blk(b, carry):
                st = pl.multiple_of(b * BLK, BLK)
                stp = pl.multiple_of(jnp.maximum(b - 1, 0) * BLK, BLK)
                mc, mp = _blk_masks(b, nb)
                qb = qd[pl.ds(st, BLK), :]
                kc, kp = kd[pl.ds(st, BLK), :], kd[pl.ds(stp, BLK), :]
                vc, vp = vd[pl.ds(st, BLK), :], vd[pl.ds(stp, BLK), :]
                outs, lses = [], []
                for hm in (hm0, jnp.logical_not(hm0)):
                    qm = jnp.where(hm, qb, jnp.zeros_like(qb))
                    sc = jnp.where(mc, _dot_nt(qm, kc), NEG)
                    sp = jnp.where(mp, _dot_nt(qm, kp), NEG)
                    m = jnp.maximum(jnp.max(sc, axis=1, keepdims=True), jnp.max(sp, axis=1, keepdims=True))
                    pc, pp = jnp.exp(sc - m), jnp.exp(sp - m)
                    l = jnp.sum(pc, axis=1, keepdims=True) + jnp.sum(pp, axis=1, keepdims=True)
                    o = _dot(pc.astype(BF16), vc) + _dot(pp.astype(BF16), vp)
                    outs.append(o / l)
                    lses.append(m + jnp.log(l))
                od[pl.ds(st, BLK), :] = jnp.where(hm0, outs[0], outs[1])
                ld[pl.ds(st, BLK), :] = jnp.where(hm0, lses[0], lses[1])
                return carry

            lax.fori_loop(0, S // BLK, blk, 0)
            _reint(od, on.at[pi], d, False)
            _reint(ld, ln.at[pi], d, False)

        l0, l1, l2 = ln[0], ln[1], ln[2]
        m = jnp.maximum(jnp.maximum(l0, l1), l2)
        e0, e1, e2 = jnp.exp(l0 - m), jnp.exp(l1 - m), jnp.exp(l2 - m)
        den = e0 + e1 + e2
        att_ref[...] = (e0 * on[0] + e1 * on[1] + e2 * on[2]) / den
        lse_ref[...] = m + jnp.log(den)

    col = lambda c0: pl.BlockSpec((S, LANES), lambda p: (0, c0 + p))
    out = pl.BlockSpec((S, LANES), lambda p: (0, p))
    return pl.pallas_call(
        body, name="att_fwd", grid=(NPAIR,),
        in_specs=[col(QB), col(KB), col(VB), pl.BlockSpec((S, 1), lambda p: (0, 0)),
                  pl.BlockSpec((1, LANES), lambda p: (0, 0))],
        out_specs=[out, out, out, out],
        out_shape=[jax.ShapeDtypeStruct((S, R), F32)] * 4,
        scratch_shapes=[pltpu.VMEM((S, LANES), BF16)] * 3 + [pltpu.VMEM((S, LANES), F32)] * 2
        + [pltpu.VMEM((3, S, LANES), F32)] * 2,
        compiler_params=_cp(("parallel",)),
    )(proj, proj, proj, pos, freq)


def _att_bwd_old(dproj, d_att, att, lse, qr, kr, proj, pos, freq):
    def body(dp_in, do_ref, o_ref, lse_ref, qr_ref, kr_ref, v_ref, pos_ref, freq_ref, dp_ref,
             qd, kd, vd, dod, lsd, prd, dqd, dkd, dvd, dqn, dkn, dvn, prn, stage, sems):
        p = pl.program_id(0)
        prn[...] = do_ref[...] * o_ref[...]
        dqn[...] = jnp.zeros_like(dqn)
        dkn[...] = jnp.zeros_like(dkn)
        dvn[...] = jnp.zeros_like(dvn)
        hm0 = lax.broadcasted_iota(jnp.int32, (BLK, LANES), 1) < HEAD

        for d in PATTERNS:
            nb = S // d // BLK
            _deint(qr_ref, qd, d)
            _deint(kr_ref, kd, d)
            _deint(v_ref, vd, d)
            _deint(do_ref, dod, d)
            _deint(lse_ref, lsd, d)
            _deint(prn, prd, d)
            dkd[...] = jnp.zeros_like(dkd)
            dvd[...] = jnp.zeros_like(dvd)

            def blk(b, carry):
                st = pl.multiple_of(b * BLK, BLK)
                stp = pl.multiple_of(jnp.maximum(b - 1, 0) * BLK, BLK)
                mc, mp = _blk_masks(b, nb)
                qb, dob = qd[pl.ds(st, BLK), :], dod[pl.ds(st, BLK), :]
                kc, kp = kd[pl.ds(st, BLK), :], kd[pl.ds(stp, BLK), :]
                vc, vp = vd[pl.ds(st, BLK), :], vd[pl.ds(stp, BLK), :]
                lsb, prb = lsd[pl.ds(st, BLK), :], prd[pl.ds(st, BLK), :]
                dqs = []
                dkc = dkp = dvc = dvp = None
                for hm in (hm0, jnp.logical_not(hm0)):
                    qm = jnp.where(hm, qb, jnp.zeros_like(qb))
                    dom = jnp.where(hm, dob, jnp.zeros_like(dob))
                    lh = jnp.max(jnp.where(hm, lsb, -3e38), axis=1, keepdims=True)
                    delta = jnp.sum(jnp.where(hm, prb, 0.0), axis=1, keepdims=True)
                    pc = jnp.where(mc, jnp.exp(_dot_nt(qm, kc) - lh), 0.0)
                    pp = jnp.where(mp, jnp.exp(_dot_nt(qm, kp) - lh), 0.0)
                    dsc = (pc * (_dot_nt(dom, vc) - delta)).astype(BF16)
                    dsp = (pp * (_dot_nt(dom, vp) - delta)).astype(BF16)
                    dqs.append(_dot(dsc, kc) + _dot(dsp, kp))
                    acc = lambda t, n: n if t is None else t + n
                    dkc, dkp = acc(dkc, _dot_tn(dsc, qm)), acc(dkp, _dot_tn(dsp, qm))
                    dvc, dvp = acc(dvc, _dot_tn(pc.astype(BF16), dom)), acc(dvp, _dot_tn(pp.astype(BF16), dom))
                dqd[pl.ds(st, BLK), :] = jnp.where(hm0, dqs[0], dqs[1])
                dkd[pl.ds(stp, BLK), :] += dkp
                dvd[pl.ds(stp, BLK), :] += dvp
                dkd[pl.ds(st, BLK), :] += dkc
                dvd[pl.ds(st, BLK), :] += dvc
                return carry

            lax.fori_loop(0, S // BLK, blk, 0)
            _reint(dqd, dqn, d, True)
            _reint(dkd, dkn, d, True)
            _reint(dvd, dvn, d, True)

        lane = lax.broadcasted_iota(jnp.int32, (S, LANES), 1)
        first = (lane & (HEAD // 2)) == 0
        cos, sin = _cos_sin(pos_ref, freq_ref)
        dq = dqn[...] * (HEAD ** -0.5)
        dk = dkn[...]
        stage[0] = (dq * cos - _rot_half(dq, first) * sin).astype(BF16)
        stage[1] = (dk * cos - _rot_half(dk, first) * sin).astype(BF16)
        stage[2] = dvn[...].astype(BF16)
        copies = [pltpu.make_async_copy(stage.at[j], dp_ref.at[:, pl.ds((2 + j) * R + p * LANES, LANES)], sems.at[j])
                  for j in range(3)]
        for cp in copies:
            cp.start()
        for cp in copies:
            cp.wait()

    blk = pl.BlockSpec((S, LANES), lambda p: (0, p))
    return pl.pallas_call(
        body, name="att_bwd", grid=(NPAIR,),
        in_specs=[pl.BlockSpec(memory_space=pl.ANY), blk, blk, blk, blk, blk,
                  pl.BlockSpec((S, LANES), lambda p: (0, VB + p)), pl.BlockSpec((S, 1), lambda p: (0, 0)),
                  pl.BlockSpec((1, LANES), lambda p: (0, 0))],
        out_specs=pl.BlockSpec(memory_space=pl.ANY),
        out_shape=jax.ShapeDtypeStruct((S, E), BF16),
        scratch_shapes=[pltpu.VMEM((S, LANES), BF16)] * 4 + [pltpu.VMEM((S, LANES), F32)] * 9
        + [pltpu.VMEM((3, S, LANES), BF16), pltpu.SemaphoreType.DMA((3,))],
        input_output_aliases={0: 0},
        compiler_params=_cp(("arbitrary",)),
    )(dproj, d_att, att, lse, qr, kr, proj, pos, freq)


def _out_fwd_bwd(ya, att, proj, w_out_bf, x, target, mod, norm_post, norm_att):
    ts = 256

    def body(ya_ref, att_ref, gb_ref, w_ref, x_ref, t_ref, mod_ref, npost_ref, natt_ref,
             gx_ref, dya_ref, datt_ref, dgb_ref, gw_ref, acc_ref):
        i = pl.program_id(0)

        @pl.when(i == 0)
        def _():
            gw_ref[...] = jnp.zeros_like(gw_ref)
            acc_ref[...] = jnp.zeros_like(acc_ref)

        gate = mod_ref[:, 2 * D:3 * D]
        att = att_ref[...]
        gb = gb_ref[...]
        sg = _sigmoid(gb)
        silu = gb * sg
        ybp = att * silu
        yb, ybn, rstd_b = _rms_fwd(ybp, natt_ref[...])
        cat = jnp.concatenate([ya_ref[...], yb.astype(BF16)], axis=1)
        mix = _dot(cat, w_ref[...])
        rn, mn, rstd_m = _rms_fwd(mix, npost_ref[...])
        err = x_ref[...] + gate * rn - t_ref[...]
        dy = err * (1.0 / D)
        gx_ref[...] = dy
        dmix, dnpost = _rms_bwd(dy * gate, mn, rstd_m, npost_ref[...])
        dmb = dmix.astype(BF16)
        gw_ref[...] += _dot_tn(cat, dmb)
        dcat = _dot_nt(dmb, w_ref[...])
        dya_ref[...] = dcat[:, 0:R]
        dybp, dnatt = _rms_bwd(dcat[:, R:2 * R], ybn, rstd_b, natt_ref[...])
        datt_ref[...] = dybp * silu
        dgb_ref[...] = (dybp * att * (sg * (1.0 + gb * (1.0 - sg)))).astype(BF16)
        acc_ref[0:1, :] += jnp.sum(dy * rn, axis=0, keepdims=True)
        acc_ref[1:2, :] += dnpost
        acc_ref[2:3, 0:R] += dnatt
        acc_ref[3:4, :] += jnp.sum(jnp.sum(err * err, axis=1, keepdims=True), axis=0, keepdims=True)

    tile = lambda w: pl.BlockSpec((ts, w), lambda i: (i, 0))
    c0 = lambda shape: pl.BlockSpec(shape, lambda i: (0, 0))
    return pl.pallas_call(
        body, name="out_fwd_bwd", grid=(S // ts,),
        in_specs=[tile(R), tile(R), pl.BlockSpec((ts, R), lambda i: (i, 5)), c0((D, D)), tile(D), tile(D),
                  c0((1, 3 * D)), c0((1, D)), c0((1, R))],
        out_specs=[tile(D), tile(R), tile(R), pl.BlockSpec((ts, R), lambda i: (i, 5)), c0((D, D)), c0((8, D))],
        out_shape=[jax.ShapeDtypeStruct((S, D), F32), jax.ShapeDtypeStruct((S, R), F32),
                   jax.ShapeDtypeStruct((S, R), F32), jax.ShapeDtypeStruct((S, E), BF16),
                   jax.ShapeDtypeStruct((D, D), F32), jax.ShapeDtypeStruct((8, D), F32)],
        compiler_params=_cp(("arbitrary",)),
    )(ya, att, proj, w_out_bf, x, target, mod, norm_post, norm_att)


UC = 256
UPC = EC // UC


NU = E // UC


def _unit_of_step(i):
    return (i % NCHIP) * UPC + i // NCHIP


def _in_proj_bwd(ht, dproj, w_in_bf, x, gx1, mod, norm_pre):
    ts = 256
    nt = S // ts
    half = D // 2
    units = [_unit_of_step(k) for k in range(NU)]
    owners = [u // UPC for u in units]

    def body(ht_ref, dpu_ref, dp_ref, w_ref, x_ref, gx1_ref, mod_ref, np_ref, gx_ref, acc_ref, gin_ref,
             mine, sib, tmp, stage, got, red, hs, hr, ps, pr, bs, br):
        i = pl.program_id(0)
        xx, yy, c = _me()
        ci = 2 * xx + yy
        r0 = pl.multiple_of(c * half, half)
        r1 = pl.multiple_of((1 - c) * half, half)

        def exch(k):
            return _remote(tmp.at[k % 2], sib.at[k], hs.at[k], hr.at[k], 1)

        def partial(k, sender):
            return pltpu.make_async_remote_copy(
                src_ref=stage.at[k], dst_ref=got.at[units[k] % UPC, sender], send_sem=ps.at[k],
                recv_sem=pr.at[k, sender], device_id=(owners[k] // 2, owners[k] % 2, c), device_id_type=MESH)

        def back(k, start):
            off = (units[k] % UPC) * UC
            blk = red.at[pl.ds(start, half), off:off + UC]
            return _remote(blk, blk, bs.at[k], br.at[k], 1)

        for k in range(NU + 1):
            @pl.when(i == k)
            def _():
                if k >= 1:
                    exch(k - 1).wait_recv()
                    mine[k - 1] += sib[k - 1]

                    @pl.when(ci != owners[k - 1])
                    def _():
                        stage[k - 1] = mine[k - 1].astype(BF16)
                        partial(k - 1, ci).start()
                if k < NU:
                    if k >= 2:
                        exch(k - 2).wait_send()
                    dpu = dpu_ref[...]
                    mine[k] = _dot(ht_ref[pl.ds(r0, half), :], dpu)
                    tmp[k % 2] = _dot(ht_ref[pl.ds(r1, half), :], dpu)
                    exch(k).start()

        @pl.when(i == NU)
        def _():
            acc_ref[...] = jnp.zeros_like(acc_ref)

        @pl.when(i >= NU)
        def _():
            dh = sum(_dot_nt(dp_ref[:, j * EC:(j + 1) * EC], w_ref[j]) for j in range(NCHIP))
            hp, xn, rstd = _rms_fwd(x_ref[...], np_ref[...])
            dx, dnp = _rms_bwd(dh * (1.0 + mod_ref[:, D:2 * D]), xn, rstd, np_ref[...])
            gx_ref[...] = gx1_ref[...] + dx
            acc_ref[0:1, :] += jnp.sum(dh, axis=0, keepdims=True)
            acc_ref[1:2, :] += jnp.sum(dh * hp, axis=0, keepdims=True)
            acc_ref[2:3, :] += dnp

        @pl.when(i == NU + nt - 1)
        def _():
            exch(NU - 2).wait_send()
            exch(NU - 1).wait_send()
            for k in range(NU):
                @pl.when(ci == owners[k])
                def _():
                    off = (units[k] % UPC) * UC
                    red[pl.ds(r0, half), off:off + UC] = mine[k]
                    for s in range(NCHIP):
                        if s != owners[k]:
                            partial(k, s).wait_recv()
                            red[pl.ds(r0, half), off:off + UC] += got[units[k] % UPC, s].astype(F32)
                    back(k, r0).start()
            for k in range(NU):
                @pl.when(ci == owners[k])
                def _():
                    back(k, r1).wait_recv()
                    back(k, r0).wait_send()

                @pl.when(ci != owners[k])
                def _():
                    partial(k, ci).wait_send()
            gin_ref[...] = red[...]

    tile = lambda w: pl.BlockSpec((ts, w), lambda i: (jnp.maximum(i - NU, 0), 0))
    c0 = lambda shape: pl.BlockSpec(shape, lambda i: (0, 0))
    vm = pl.BlockSpec(memory_space=pltpu.VMEM)
    return pl.pallas_call(
        body, name="in_proj_bwd", grid=(NU + nt,),
        in_specs=[vm, pl.BlockSpec((S, UC), lambda i: (0, _unit_of_step(jnp.minimum(i, NU - 1)))), tile(E),
                  pl.BlockSpec((NCHIP, D, EC), lambda i: (0, 0, 0)), tile(D), tile(D), c0((1, 3 * D)), c0((1, D))],
        out_specs=[tile(D), c0((8, D)), vm],
        out_shape=[jax.ShapeDtypeStruct((S, D), F32), jax.ShapeDtypeStruct((8, D), F32),
                   jax.ShapeDtypeStruct((D, EC), F32)],
        scratch_shapes=[pltpu.VMEM((NU, half, UC), F32), pltpu.VMEM((NU, half, UC), F32),
                        pltpu.VMEM((2, half, UC), F32), pltpu.VMEM((NU, half, UC), BF16),
                        pltpu.VMEM((UPC, NCHIP, half, UC), BF16), pltpu.VMEM((D, EC), F32),
                        pltpu.SemaphoreType.DMA((NU,)), pltpu.SemaphoreType.DMA((NU,)),
                        pltpu.SemaphoreType.DMA((NU,)), pltpu.SemaphoreType.DMA((NU, NCHIP)),
                        pltpu.SemaphoreType.DMA((NU,)), pltpu.SemaphoreType.DMA((NU,))],
        compiler_params=_cp(("arbitrary",)),
    )(ht, dproj, dproj, w_in_bf, x, gx1, mod, norm_pre)


def _block_diag(w):
    n, b, _ = w.shape
    eye = jnp.eye(n, dtype=w.dtype)
    return (eye[:, None, :, None] * w[:, :, None, :]).reshape(n * b, n * b)


def _diag_blocks(m):
    n, b = R // HEAD, HEAD
    return jnp.stack([m[h * b:(h + 1) * b, h * b:(h + 1) * b] for h in range(n)])


def _local_step(x, pos, target, mod, w_in_bf, w_out, conv_w, p):
    wa_d = _block_diag(p["w_rg_a"]).astype(BF16)
    wx_d = _block_diag(p["w_rg_x"]).astype(BF16)
    rec_p = (conv_w, p["conv_b"], wa_d, p["b_rg_a"], wx_d, p["b_rg_x"], p["lru_lambda"], p["norm_rec"])
    cos, sin = _rope_table(pos, _rope_freq())
    proj, ht = _in_proj_fwd(x, mod, p["norm_pre"], w_in_bf)
    h_all, ya = _rec_fwd(proj, *rec_p)
    att, qr, kr, lse, w_out_bf = _att_fwd(proj, cos, sin, w_out)
    gx1, d_ya, d_att, dproj, gw_out, acc_o = _out_fwd_bwd(ya, att, proj, w_out_bf.reshape(D, D), x, target, mod,
                                                           p["norm_post"], p["norm_att"])
    dproj, g_out = _att_bwd(dproj, d_att, att, lse, qr, kr, proj, cos, sin, gw_out.reshape(NCHIP, D // NCHIP, D))
    dproj, dwa, dwx, sm = _rec_bwd(dproj, d_ya, proj, h_all, *rec_p)
    grad_x, acc_i, g_in = _in_proj_bwd(ht, dproj, w_in_bf, x, gx1, mod, p["norm_pre"])
    small = dict(b_ada=jnp.concatenate([acc_i[0:1], acc_i[1:2], acc_o[0:1]], axis=1),
                 norm_pre=acc_i[2:3], norm_post=acc_o[1:2], conv_w=sm[8:12], conv_b=sm[4:5],
                 w_rg_a=_diag_blocks(dwa), b_rg_a=sm[0:1], w_rg_x=_diag_blocks(dwx), b_rg_x=sm[1:2],
                 lru_lambda=sm[2:3], norm_rec=sm[3:4], norm_att=acc_o[2:3, 0:R], loss=acc_o[3:4, 0:LANES])
    return grad_x, g_in, g_out, small


def _me():
    return lax.axis_index("x"), lax.axis_index("y"), lax.axis_index("c")


def _flip(v, bit):
    return 1 - v if bit else v


def _peer(rel):
    x, y, c = _me()
    return (_flip(x, rel & 4), _flip(y, rel & 2), _flip(c, rel & 1))


def _remote(src, dst, send_sem, recv_sem, rel):
    return pltpu.make_async_remote_copy(src_ref=src, dst_ref=dst, send_sem=send_sem, recv_sem=recv_sem,
                                        device_id=_peer(rel), device_id_type=MESH)


def _allgather_rows(row, name):
    w = row.shape[1]

    def body(row_ref, out_ref, send_sems, recv_sems, local_sem):
        x, y, c = _me()
        me = 4 * x + 2 * y + c
        mine = pltpu.make_async_copy(row_ref, out_ref.at[pl.ds(me, 1), :], local_sem)
        mine.start()
        sends = [_remote(row_ref, out_ref.at[pl.ds(me, 1), :], send_sems.at[r - 1], recv_sems.at[r - 1], r)
                 for r in range(1, NDEV)]
        for cp in sends:
            cp.start()
        for r in range(1, NDEV):
            px, py, pc = _peer(r)
            src = 4 * px + 2 * py + pc
            _remote(row_ref, out_ref.at[pl.ds(src, 1), :], send_sems.at[r - 1], recv_sems.at[r - 1], r).wait_recv()
        for cp in sends:
            cp.wait_send()
        mine.wait()

    return pl.pallas_call(
        body, name=name,
        in_specs=[pl.BlockSpec(memory_space=pltpu.VMEM)],
        out_specs=pl.BlockSpec(memory_space=pltpu.VMEM),
        out_shape=jax.ShapeDtypeStruct((NDEV, w), row.dtype),
        scratch_shapes=[pltpu.SemaphoreType.DMA((NDEV - 1,)), pltpu.SemaphoreType.DMA((NDEV - 1,)),
                        pltpu.SemaphoreType.DMA],
        compiler_params=pltpu.CompilerParams(vmem_limit_bytes=VMEM_LIMIT),
    )(row)


class _WeightGather:
    SEMS = [pltpu.SemaphoreType.DMA((NCHIP - 1,))] * 4

    def __init__(self, w_ref, out_ref, send_sems, recv_sems, fsend_sems, frecv_sems):
        x, y, c = _me()
        self.w, self.out, self.ci = w_ref, out_ref, 2 * x + y
        self.half = w_ref.shape[0] // 2
        self.r0 = pl.multiple_of(c * self.half, self.half)
        self.r1 = pl.multiple_of((1 - c) * self.half, self.half)
        self.sems = (send_sems, recv_sems, fsend_sems, frecv_sems)

    def _ici(self, chip, k):
        blk = self.out.at[chip, pl.ds(self.r0, self.half), :]
        return _remote(blk, blk, self.sems[0].at[k - 1], self.sems[1].at[k - 1], 2 * k)

    def _d2d(self, chip, start, k):
        blk = self.out.at[chip, pl.ds(start, self.half), :]
        return _remote(blk, blk, self.sems[2].at[k - 1], self.sems[3].at[k - 1], 1)

    def start(self):
        self.out[self.ci] = self.w[...].astype(BF16)
        for k in range(1, NCHIP):
            self._ici(self.ci, k).start()

    def forward(self):
        for k in range(1, NCHIP):
            self._ici(self.ci ^ k, k).wait_recv()
            self._d2d(self.ci ^ k, self.r0, k).start()

    def finish(self):
        for k in range(1, NCHIP):
            self._d2d(self.ci ^ k, self.r1, k).wait_recv()
        for k in range(1, NCHIP):
            self._ici(self.ci, k).wait_send()
            self._d2d(self.ci ^ k, self.r0, k).wait_send()


def _start_gather(crow, w_ada, b_cols, w_in):
    wc = crow.shape[1]

    def body(crow_ref, wada_ref, b_ref, win_ref, g0_ref, mod_ref, wbf_ref,
             modp, modb, cs, cr, ms, mr, ws, wr, fs, fr, local_sems):
        x, y, c = _me()
        ci = 2 * x + y
        me = 2 * ci + c
        wg = _WeightGather(win_ref, wbf_ref, ws, wr, fs, fr)
        mine = pltpu.make_async_copy(crow_ref, g0_ref.at[pl.ds(me, 1), :], local_sems.at[0])
        mine.start()
        csend = [_remote(crow_ref, g0_ref.at[pl.ds(me, 1), :], cs.at[r - 1], cr.at[r - 1], r) for r in range(1, NDEV)]
        for cp in csend:
            cp.start()
        wg.start()
        for r in range(1, NDEV):
            px, py, pc = _peer(r)
            _remote(crow_ref, g0_ref.at[pl.ds(4 * px + 2 * py + pc, 1), :], cs.at[r - 1], cr.at[r - 1], r).wait_recv()
        mine.wait()
        cv = g0_ref[:, 0:D]
        sc = cv * _sigmoid(cv)
        scb = jnp.concatenate([sc, jnp.zeros_like(sc)], axis=0).astype(BF16)
        modp[...] = _dot(scb, wada_ref[...].astype(BF16))[0:NDEV, :] + b_ref[...]
        own = pltpu.make_async_copy(modp.at[pl.ds(me, 1), :], modb.at[ci], local_sems.at[1])
        own.start()
        msend = []
        for k in range(1, NCHIP):
            dst = 2 * (ci ^ k) + c
            cp = _remote(modp.at[pl.ds(dst, 1), :], modb.at[ci], ms.at[k - 1], mr.at[k - 1], 2 * k)
            cp.start()
            msend.append(cp)
        for k in range(1, NCHIP):
            _remote(modp.at[pl.ds(me, 1), :], modb.at[ci ^ k], ms.at[k - 1], mr.at[k - 1], 2 * k).wait_recv()
        own.wait()
        for j in range(NCHIP):
            mod_ref[:, j * EC:(j + 1) * EC] = modb[j]
        wg.forward()
        wg.finish()
        for cp in csend + msend:
            cp.wait_send()

    vm = pl.BlockSpec(memory_space=pltpu.VMEM)
    return pl.pallas_call(
        body, name="start_gather",
        in_specs=[vm] * 4, out_specs=[vm] * 3,
        out_shape=[jax.ShapeDtypeStruct((NDEV, wc), F32), jax.ShapeDtypeStruct((1, 3 * D), F32),
                   jax.ShapeDtypeStruct((NCHIP, D, EC), BF16)],
        scratch_shapes=[pltpu.VMEM((NDEV, EC), F32), pltpu.VMEM((NCHIP, 1, EC), F32),
                        pltpu.SemaphoreType.DMA((NDEV - 1,)), pltpu.SemaphoreType.DMA((NDEV - 1,)),
                        pltpu.SemaphoreType.DMA((NCHIP - 1,)), pltpu.SemaphoreType.DMA((NCHIP - 1,))]
        + _WeightGather.SEMS + [pltpu.SemaphoreType.DMA((2,))],
        compiler_params=pltpu.CompilerParams(vmem_limit_bytes=VMEM_LIMIT),
    )(crow, w_ada, b_cols, w_in)


class _ReduceScatter:
    @staticmethod
    def scratch(n_units, rows, ucols, max_owned):
        half = rows // 2
        return [pltpu.VMEM((n_units, half, ucols), F32), pltpu.VMEM((n_units, half, ucols), BF16),
                pltpu.VMEM((max_owned, NCHIP, half, ucols), BF16),
                pltpu.SemaphoreType.DMA((2,)), pltpu.SemaphoreType.DMA((n_units,)),
                pltpu.SemaphoreType.DMA((n_units, NCHIP)), pltpu.SemaphoreType.DMA((n_units,)),
                pltpu.SemaphoreType.DMA((n_units,))]

    def __init__(self, g_ref, out_ref, units, sib, stage, got, sem1, send2, recv2, send3, recv3):
        x, y, c = _me()
        self.c, self.ci = c, 2 * x + y
        self.g, self.out, self.units = g_ref, out_ref, units
        self.sib, self.stage, self.got = sib, stage, got
        self.sem1, self.send2, self.recv2, self.send3, self.recv3 = sem1, send2, recv2, send3, recv3
        self.half = g_ref.shape[1] // 2
        self.ucols = g_ref.shape[2]
        self.r0 = pl.multiple_of(c * self.half, self.half)
        self.r1 = pl.multiple_of((1 - c) * self.half, self.half)
        self.slot0 = units[0][0]
        assert [u[0] for u in units] == list(range(self.slot0, self.slot0 + len(units)))
        seen = {}
        self.local = []
        for _, owner, _ in units:
            self.local.append(seen.get(owner, 0))
            seen[owner] = seen.get(owner, 0) + 1

    def _halves(self):
        n = len(self.units)
        return _remote(self.g.at[pl.ds(self.slot0, n), pl.ds(self.r1, self.half), :], self.sib,
                       self.sem1.at[0], self.sem1.at[1], 1)

    def _partial(self, i, sender):
        _, owner, _ = self.units[i]
        return pltpu.make_async_remote_copy(
            src_ref=self.stage.at[i], dst_ref=self.got.at[self.local[i], sender],
            send_sem=self.send2.at[i], recv_sem=self.recv2.at[i, sender],
            device_id=(owner // 2, owner % 2, self.c), device_id_type=MESH)

    def _back(self, i, start):
        off = self.units[i][2]
        blk = self.out.at[pl.ds(start, self.half), off:off + self.ucols]
        return _remote(blk, blk, self.send3.at[i], self.recv3.at[i], 1)

    def at_steps(self, step, start, send, reduce, finish, out_ref):
        @pl.when(step == start)
        def _():
            self.out[...] = jnp.zeros_like(self.out)
            self.start_halves()

        pl.when(step == send)(self.send_partials)
        pl.when(step == reduce)(self.reduce_owned)

        @pl.when(step == finish)
        def _():
            self.finish()
            out_ref[...] = self.out[...]

    def start_halves(self):
        self._halves().start()

    def send_partials(self):
        self._halves().wait_recv()
        for i, (slot, owner, _) in enumerate(self.units):
            @pl.when(self.ci != owner)
            def _():
                self.stage[i] = (self.g[slot, pl.ds(self.r0, self.half), :] + self.sib[i]).astype(BF16)
                self._partial(i, self.ci).start()

    def reduce_owned(self):
        for i, (slot, owner, off) in enumerate(self.units):
            @pl.when(self.ci == owner)
            def _():
                rows, cols = pl.ds(self.r0, self.half), slice(off, off + self.ucols)
                self.out[rows, cols] = self.g[slot, pl.ds(self.r0, self.half), :] + self.sib[i]
                for s in range(NCHIP):
                    if s != owner:
                        self._partial(i, s).wait_recv()
                        self.out[rows, cols] += self.got[self.local[i], s].astype(F32)
                self._back(i, self.r0).start()

    def finish(self):
        self._halves().wait_send()
        for i, (_, owner, _) in enumerate(self.units):
            @pl.when(self.ci == owner)
            def _():
                self._back(i, self.r1).wait_recv()
                self._back(i, self.r0).wait_send()

            @pl.when(self.ci != owner)
            def _():
                self._partial(i, self.ci).wait_send()


def _reduce_scatter(g4, name):
    _, rows, cols = g4.shape
    units = [(j, j, 0) for j in range(NCHIP)]

    def body(g_ref, out_ref, *scratch):
        rs = _ReduceScatter(g_ref, out_ref, units, *scratch)
        rs.start_halves()
        rs.send_partials()
        rs.reduce_owned()
        rs.finish()

    return pl.pallas_call(
        body, name=name,
        in_specs=[pl.BlockSpec(memory_space=pltpu.VMEM)],
        out_specs=pl.BlockSpec(memory_space=pltpu.VMEM),
        out_shape=jax.ShapeDtypeStruct((rows, cols), F32),
        scratch_shapes=_ReduceScatter.scratch(NCHIP, rows, cols, 1),
        compiler_params=pltpu.CompilerParams(vmem_limit_bytes=VMEM_LIMIT),
    )(g4)


def _silu_rows(c_ref):
    cv = c_ref[...]
    sc = cv * _sigmoid(cv)
    return jnp.concatenate([sc, jnp.zeros_like(sc)], axis=0).astype(BF16)


def _ada_fwd(cg, w_ada, b_cols):
    def body(c_ref, w_ref, b_ref, o_ref):
        o_ref[...] = _dot(_silu_rows(c_ref), w_ref[...].astype(BF16))[0:NDEV, :] + b_ref[...]

    return pl.pallas_call(body, name="ada_fwd", out_shape=jax.ShapeDtypeStruct((NDEV, EC), F32),
                          compiler_params=_cp())(cg, w_ada, b_cols)


def _ada_bwd(cg, dmod_cols):
    def body(c_ref, d_ref, o_ref):
        dm = d_ref[...]
        dmb = jnp.concatenate([dm, jnp.zeros_like(dm)], axis=0).astype(BF16)
        o_ref[...] = _dot_tn(_silu_rows(c_ref), dmb)

    return pl.pallas_call(body, name="ada_bwd", out_shape=jax.ShapeDtypeStruct((D, EC), F32),
                          compiler_params=_cp())(cg, dmod_cols)


def _sum_rows(g):
    def body(g_ref, o_ref):
        acc = g_ref[0:1, :]
        for r in range(1, NDEV):
            acc = acc + g_ref[r:r + 1, :]
        o_ref[...] = acc

    return pl.pallas_call(body, name="sum_rows", out_shape=jax.ShapeDtypeStruct((1, g.shape[1]), F32),
                          compiler_params=_cp())(g)


def _adamw(w, g, m, v, name):
    rows, cols = w.shape
    tr = 256 if rows % 256 == 0 else rows

    def body(w_ref, g_ref, m_ref, v_ref, d_ref, nm_ref, nv_ref):
        gv = g_ref[...]
        nm = B1 * m_ref[...] + (1.0 - B1) * gv
        nv = B2 * v_ref[...] + (1.0 - B2) * (gv * gv)
        m_hat = nm / (1.0 - B1 ** STEP)
        v_hat = nv / (1.0 - B2 ** STEP)
        d_ref[...] = (-LR) * (m_hat / (jnp.sqrt(v_hat) + ADAM_EPS) + WD * w_ref[...])
        nm_ref[...] = nm
        nv_ref[...] = nv

    spec = pl.BlockSpec((tr, cols), lambda i: (i, 0))
    return pl.pallas_call(
        body, name=name, grid=(rows // tr,), in_specs=[spec] * 4, out_specs=[spec] * 3,
        out_shape=[jax.ShapeDtypeStruct((rows, cols), F32)] * 3,
        compiler_params=_cp(("parallel",)),
    )(w, g, m, v)


SMALL = (("b_ada", 3 * D), ("norm_pre", D), ("norm_post", D), ("conv_w", 4 * R), ("conv_b", R),
         ("w_rg_a", R * HEAD), ("b_rg_a", R), ("w_rg_x", R * HEAD), ("b_rg_x", R), ("lru_lambda", R),
         ("norm_rec", R), ("norm_att", R))
BIG = ("w_ada", "w_in", "w_out")
WEIGHTS = ("w_ada", "b_ada", "norm_pre", "norm_post", "w_in", "conv_w", "conv_b", "w_rg_a", "b_rg_a", "w_rg_x",
           "b_rg_x", "lru_lambda", "norm_rec", "norm_att", "w_out")


def kernel(x, c, positions, w_ada, b_ada, norm_pre, norm_post, w_in, conv_w, conv_b, w_rg_a, b_rg_a, w_rg_x, b_rg_x, lru_lambda, norm_rec, norm_att, w_out, loss_target, m_w_ada, m_b_ada, m_norm_pre, m_norm_post, m_w_in, m_conv_w, m_conv_b, m_w_rg_a, m_b_rg_a, m_w_rg_x, m_b_rg_x, m_lru_lambda, m_norm_rec, m_norm_att, m_w_out, v_w_ada, v_b_ada, v_norm_pre, v_norm_post, v_w_in, v_conv_w, v_conv_b, v_w_rg_a, v_b_rg_a, v_w_rg_x, v_b_rg_x, v_lru_lambda, v_norm_rec, v_norm_att, v_w_out):
    given = dict(locals())
    wts = {n: given[n] for n in WEIGHTS}
    ms = {n: given["m_" + n] for n in WEIGHTS}
    vs = {n: given["v_" + n] for n in WEIGHTS}
    xi, yi, cc = _me()
    chip = 2 * xi + yi
    me = 2 * chip + cc
    cw_loc = R // NCHIP

    b_cols = lax.dynamic_slice(b_ada, (0, chip * EC), (1, EC))
    g0, mod, w_in_bf = _start_gather(jnp.concatenate([c, conv_w.reshape(1, 4 * cw_loc)], axis=1),
                                     w_ada[0], b_cols, w_in[0])
    cg = g0[:, 0:D]
    conv_full = g0[0::2, D:].reshape(NCHIP, 4, cw_loc).transpose(1, 0, 2).reshape(4, R)

    p = dict(norm_pre=norm_pre, norm_post=norm_post, conv_b=conv_b, b_rg_a=b_rg_a, b_rg_x=b_rg_x,
             lru_lambda=lru_lambda, norm_rec=norm_rec, norm_att=norm_att, w_rg_a=w_rg_a[0], w_rg_x=w_rg_x[0])
    grad_x, g_in, g_out, small = _local_step(
        x[0], positions.reshape(S, 1), loss_target[0], mod, w_in_bf, w_out[0], conv_full, p)

    grads = {"w_out": g_out, "w_in": g_in}
    row = jnp.concatenate([small[n].reshape(1, k) for n, k in SMALL + (("loss", LANES),)], axis=1)
    g2 = _allgather_rows(row, "gather_small")
    tot = _sum_rows(g2)
    grads["w_ada"] = _ada_bwd(cg, lax.dynamic_slice(g2, (0, chip * EC), (NDEV, EC)))
    off = 0
    for n, k in SMALL:
        grads[n] = tot[:, off:off + k]
        off += k
    loss = tot[0, off] * (0.5 / D)
    grads["conv_w"] = lax.dynamic_slice(grads["conv_w"].reshape(4, R), (0, chip * cw_loc), (4, cw_loc))

    delta, new_m, new_v = {}, {}, {}
    for n in BIG:
        delta[n], new_m[n], new_v[n] = _adamw(wts[n][0], grads[n], ms[n][0], vs[n][0], "adamw_" + n)
    pack = lambda d: jnp.concatenate([d[n].reshape(1, -1) for n, _ in SMALL], axis=1).reshape(-1, LANES)
    pd, pm, pv = _adamw(pack(wts), pack(grads), pack(ms), pack(vs), "adamw_small")
    off = 0
    for n, _ in SMALL:
        k = wts[n].size
        for dst, src in ((delta, pd), (new_m, pm), (new_v, pv)):
            dst[n] = src.reshape(1, -1)[:, off:off + k]
        off += k
    out = lambda d: [d[n].reshape(wts[n].shape) for n in WEIGHTS]
    return (loss, grad_x.reshape(x.shape), *out(grads), *out(delta), *out(new_m), *out(new_v))
```

```python
import functools

import numpy as np
import jax
import jax.numpy as jnp
from jax import lax
from jax.experimental import pallas as pl
from jax.experimental.pallas import tpu as pltpu

F32 = jnp.float32
BF16 = jnp.bfloat16

S = 2048
D = 1024
E = 3072
R = 512
NDEV = 8
NCHIP = 4
EC = 768
LRU_C = 8.0
EPS = 1e-6
NEG = -1e30
HEAD = 64
BLK = 128
PATTERNS = (1, 4, 16)
ROPE_THETA = 10000.0
LANES = 128
VMEM_LIMIT = 56 * 1024 * 1024

B1, B2, LR, WD, ADAM_EPS, STEP = 0.9, 0.999, 0.001, 0.01, 1e-8, 10
MESH = pl.DeviceIdType.MESH


def _cp(sem=None, **kw):
    return pltpu.CompilerParams(dimension_semantics=sem, vmem_limit_bytes=VMEM_LIMIT, **kw)


def _dot(a, b):
    return jnp.dot(a, b, preferred_element_type=F32)


def _dot_nt(a, b):
    return lax.dot_general(a, b, (((1,), (1,)), ((), ())), preferred_element_type=F32)


def _dot_tn(a, b):
    return lax.dot_general(a, b, (((0,), (0,)), ((), ())), preferred_element_type=F32)


def _sigmoid(x):
    return 1.0 / (1.0 + jnp.exp(-x))


def _expm1(x):
    poly = x * (1.0 + x * (0.5 + x * (1.0 / 6 + x * (1.0 / 24 + x * (1.0 / 120 + x * (1.0 / 720))))))
    return jnp.where(jnp.abs(x) < 0.3, poly, jnp.exp(x) - 1.0)


def _rms_fwd(v, g):
    rstd = lax.rsqrt(jnp.mean(v * v, axis=-1, keepdims=True) + EPS)
    vn = v * rstd
    return vn * g, vn, rstd


def _rms_bwd(dy, vn, rstd, g):
    dvn = dy * g
    dv = rstd * (dvn - vn * jnp.mean(dvn * vn, axis=-1, keepdims=True))
    return dv, jnp.sum(dy * vn, axis=0, keepdims=True)


def _in_proj_fwd(x, mod, norm_pre, w_in_bf):
    ts = 256

    def body(x_ref, mod_ref, np_ref, w_ref, proj_ref, ht_ref):
        hp, _, _ = _rms_fwd(x_ref[...], np_ref[...])
        h = hp * (1.0 + mod_ref[:, D:2 * D]) + mod_ref[:, 0:D]
        hb = h.astype(BF16)
        ht_ref[...] = h.T.astype(BF16)
        for j in range(NCHIP):
            proj_ref[:, j * EC:(j + 1) * EC] = _dot(hb, w_ref[j])

    return pl.pallas_call(
        body, name="in_proj_fwd", grid=(S // ts,),
        in_specs=[pl.BlockSpec((ts, D), lambda i: (i, 0)), pl.BlockSpec((1, 3 * D), lambda i: (0, 0)),
                  pl.BlockSpec((1, D), lambda i: (0, 0)), pl.BlockSpec((NCHIP, D, EC), lambda i: (0, 0, 0))],
        out_specs=[pl.BlockSpec((ts, E), lambda i: (i, 0)), pl.BlockSpec((D, ts), lambda i: (0, i))],
        out_shape=[jax.ShapeDtypeStruct((S, E), F32), jax.ShapeDtypeStruct((D, S), BF16)],
        compiler_params=_cp(("parallel",)),
    )(x, mod, norm_pre, w_in_bf)


RT = 256


def _shift_down(cur, prev8, j, row):
    if j == 0:
        return cur
    top = jnp.tile(pltpu.roll(prev8, j, 0), (RT // 8, 1))
    return jnp.where(row >= j, pltpu.roll(cur, j, 0), top)


def _shift_up(cur, next8, j, row):
    if j == 0:
        return cur
    bot = jnp.tile(pltpu.roll(next8, 8 - j, 0), (RT // 8, 1))
    return jnp.where(row < RT - j, pltpu.roll(cur, RT - j, 0), bot)


def _rec_gates(xp, xprev8, row, cw_ref, cb_ref, wa_ref, ba_ref, wx_ref, bx_ref, lam_ref):
    xa = cb_ref[...] + sum(cw_ref[3 - j:4 - j, :] * _shift_down(xp, xprev8, j, row) for j in range(4))
    xab = xa.astype(BF16)
    r = _sigmoid(_dot(xab, wa_ref[...]) + ba_ref[...])
    ig = _sigmoid(_dot(xab, wx_ref[...]) + bx_ref[...])
    nl = -lam_ref[...]
    sp = jnp.maximum(nl, 0.0) + jnp.log1p(jnp.exp(-jnp.abs(nl)))
    la = (-LRU_C) * r * sp
    a = jnp.exp(la)
    mult = jnp.sqrt(-_expm1(2.0 * la))
    return dict(xa=xa, xab=xab, r=r, ig=ig, sp=sp, la=la, a=a, mult=mult)


def _scan_fwd(a, u, row):
    sh = 1
    while sh < RT:
        a_s = jnp.where(row >= sh, pltpu.roll(a, sh, 0), 1.0)
        u_s = jnp.where(row >= sh, pltpu.roll(u, sh, 0), 0.0)
        u = a * u_s + u
        a = a * a_s
        sh *= 2
    return a, u


def _scan_bwd(al, g, row):
    sh = 1
    while sh < RT:
        al_s = jnp.where(row < RT - sh, pltpu.roll(al, RT - sh, 0), 1.0)
        g_s = jnp.where(row < RT - sh, pltpu.roll(g, RT - sh, 0), 0.0)
        g = g + al * g_s
        al = al * al_s
        sh *= 2
    return g


def _dense_from_blocks(blocks_ref, dense_ref):
    dense_ref[...] = jnp.zeros_like(dense_ref)
    for h in range(R // HEAD):
        dense_ref[h * HEAD:(h + 1) * HEAD, h * HEAD:(h + 1) * HEAD] = blocks_ref[h].astype(dense_ref.dtype)


def _rec_fwd(proj, conv_w, conv_b, wa_b, ba, wx_b, bx, lam, norm_rec):
    nt = S // RT

    def body(p_ref, cw_ref, cb_ref, wa_ref, ba_ref, wx_ref, bx_ref, lam_ref, nr_ref,
             h_ref, ya_ref, prev8, hc, wad, wxd):
        i = pl.program_id(0)

        @pl.when(i == 0)
        def _():
            prev8[...] = jnp.zeros_like(prev8)
            hc[...] = jnp.zeros_like(hc)
            _dense_from_blocks(wa_ref, wad)
            _dense_from_blocks(wx_ref, wxd)

        row = lax.broadcasted_iota(jnp.int32, (RT, R), 0)
        xp = p_ref[:, 0:R]
        ga = p_ref[:, R:2 * R]
        f = _rec_gates(xp, prev8[...], row, cw_ref, cb_ref, wad, ba_ref, wxd, bx_ref, lam_ref)
        u = f["mult"] * (f["ig"] * f["xa"])
        acum, hh = _scan_fwd(f["a"], u, row)
        h = hh + acum * hc[0:1, :]
        h_ref[...] = h
        hc[0:1, :] = h_ref[RT - 1:RT, :]
        prev8[...] = p_ref[RT - 8:RT, 0:R]
        yp = h * (ga * _sigmoid(ga))
        ya, _, _ = _rms_fwd(yp, nr_ref[...])
        ya_ref[...] = ya.astype(BF16)

    row1 = lambda n: pl.BlockSpec((1, n), lambda i: (0, 0))
    blocks = pl.BlockSpec((R // HEAD, HEAD, HEAD), lambda i: (0, 0, 0))
    return pl.pallas_call(
        body, name="rec_fwd", grid=(nt,),
        in_specs=[pl.BlockSpec((RT, 2 * R), lambda i: (i, 0)), pl.BlockSpec((4, R), lambda i: (0, 0)), row1(R),
                  blocks, row1(R), blocks, row1(R), row1(R), row1(R)],
        out_specs=[pl.BlockSpec((RT, R), lambda i: (i, 0)), pl.BlockSpec((RT, R), lambda i: (i, 0))],
        out_shape=[jax.ShapeDtypeStruct((S, R), F32), jax.ShapeDtypeStruct((S, R), BF16)],
        scratch_shapes=[pltpu.VMEM((8, R), F32), pltpu.VMEM((8, R), F32), pltpu.VMEM((R, R), BF16),
                        pltpu.VMEM((R, R), BF16)],
        compiler_params=_cp(("arbitrary",)),
    )(proj, conv_w, conv_b, wa_b, ba, wx_b, bx, lam, norm_rec)


def _rec_bwd(dproj, d_ya, proj, h_all, conv_w, conv_b, wa_b, ba, wx_b, bx, lam, norm_rec):
    nt = S // RT

    def body(dp_in, dya_ref, p_ref, pprev_ref, h_ref, hprev_ref, cw_ref, cb_ref, wab_ref, ba_ref, wxb_ref, bx_ref,
             lam_ref, nr_ref, dp_ref, dwab_ref, dwxb_ref, sm_ref, nxt8, cg, wa_ref, wx_ref, dwa_ref, dwx_ref):
        i = pl.program_id(0)
        ti = nt - 1 - i

        @pl.when(i == 0)
        def _():
            nxt8[...] = jnp.zeros_like(nxt8)
            cg[...] = jnp.zeros_like(cg)
            dwa_ref[...] = jnp.zeros_like(dwa_ref)
            dwx_ref[...] = jnp.zeros_like(dwx_ref)
            sm_ref[...] = jnp.zeros_like(sm_ref)
            _dense_from_blocks(wab_ref, wa_ref)
            _dense_from_blocks(wxb_ref, wx_ref)

        row = lax.broadcasted_iota(jnp.int32, (RT, R), 0)
        first = (ti > 0).astype(F32)
        xprev8 = pprev_ref[...] * first
        hprev8 = hprev_ref[...] * first
        xp = p_ref[:, 0:R]
        ga = p_ref[:, R:2 * R]
        f = _rec_gates(xp, xprev8, row, cw_ref, cb_ref, wa_ref, ba_ref, wx_ref, bx_ref, lam_ref)
        xa, r, ig, a, mult = f["xa"], f["r"], f["ig"], f["a"], f["mult"]
        h = h_ref[...]
        sg = _sigmoid(ga)
        gate = ga * sg
        yp = h * gate
        _, ypn, rstd = _rms_fwd(yp, nr_ref[...])
        d_yp, dnr = _rms_bwd(dya_ref[...], ypn, rstd, nr_ref[...])
        d_ga = d_yp * h * (sg * (1.0 + ga * (1.0 - sg)))
        dh = d_yp * gate + jnp.where(row == RT - 1, cg[0:1, :], 0.0)
        al = jnp.where(row < RT - 1, pltpu.roll(a, RT - 1, 0), 0.0)
        g = _scan_bwd(al, dh, row)
        cg[0:1, :] = jnp.sum(jnp.where(row == 0, a * g, 0.0), axis=0, keepdims=True)
        h_m1 = _shift_down(h, hprev8, 1, row)
        da = g * h_m1
        ix = ig * xa
        d_mult = g * ix
        d_ig = g * mult * xa
        d_xa = g * mult * ig
        d_la = da * a - d_mult * (a * a) / mult
        d_r = d_la * ((-LRU_C) * f["sp"])
        dsp = jnp.sum(d_la * ((-LRU_C) * r), axis=0, keepdims=True)
        dlam = dsp * (-_sigmoid(-lam_ref[...]))
        d_za = d_r * r * (1.0 - r)
        d_zx = d_ig * ig * (1.0 - ig)
        dzab = d_za.astype(BF16)
        dzxb = d_zx.astype(BF16)
        dwa_ref[...] += _dot_tn(f["xab"], dzab)
        dwx_ref[...] += _dot_tn(f["xab"], dzxb)
        d_xa = d_xa + _dot_nt(dzab, wa_ref[...]) + _dot_nt(dzxb, wx_ref[...])
        d_xp = sum(cw_ref[3 - j:4 - j, :] * _shift_up(d_xa, nxt8[...], j, row) for j in range(4))
        dcw = [jnp.sum(d_xa * _shift_down(xp, xprev8, 3 - k, row), axis=0, keepdims=True) for k in range(4)]
        dp_ref[:, 0:R] = d_xp.astype(BF16)
        dp_ref[:, R:2 * R] = d_ga.astype(BF16)
        dp8 = d_xa[0:8, :]
        nxt8[...] = dp8
        sm_ref[0:1, :] += jnp.sum(d_za, axis=0, keepdims=True)
        sm_ref[1:2, :] += jnp.sum(d_zx, axis=0, keepdims=True)
        sm_ref[2:3, :] += dlam
        sm_ref[3:4, :] += dnr
        sm_ref[4:5, :] += jnp.sum(d_xa, axis=0, keepdims=True)
        for k in range(4):
            sm_ref[8 + k:9 + k, :] += dcw[k]

        @pl.when(i == nt - 1)
        def _():
            for h in range(R // HEAD):
                dwab_ref[h] = dwa_ref[h * HEAD:(h + 1) * HEAD, h * HEAD:(h + 1) * HEAD]
                dwxb_ref[h] = dwx_ref[h * HEAD:(h + 1) * HEAD, h * HEAD:(h + 1) * HEAD]

    c0 = lambda shape: pl.BlockSpec(shape, lambda i: (0, 0))
    blocks = pl.BlockSpec((R // HEAD, HEAD, HEAD), lambda i: (0, 0, 0))
    rev = lambda i: nt - 1 - i
    prev8 = lambda i: (jnp.maximum((nt - 1 - i) * (RT // 8) - 1, 0), 0)
    return pl.pallas_call(
        body, name="rec_bwd", grid=(nt,),
        in_specs=[pl.BlockSpec(memory_space=pl.ANY),
                  pl.BlockSpec((RT, R), lambda i: (rev(i), 0)),
                  pl.BlockSpec((RT, 2 * R), lambda i: (rev(i), 0)), pl.BlockSpec((8, R), prev8),
                  pl.BlockSpec((RT, R), lambda i: (rev(i), 0)), pl.BlockSpec((8, R), prev8),
                  c0((4, R)), c0((1, R)), blocks, c0((1, R)), blocks, c0((1, R)), c0((1, R)), c0((1, R))],
        out_specs=[pl.BlockSpec((RT, 2 * R), lambda i: (rev(i), 0)), blocks, blocks, c0((16, R))],
        out_shape=[jax.ShapeDtypeStruct((S, E), BF16), jax.ShapeDtypeStruct((R // HEAD, HEAD, HEAD), F32),
                   jax.ShapeDtypeStruct((R // HEAD, HEAD, HEAD), F32), jax.ShapeDtypeStruct((16, R), F32)],
        scratch_shapes=[pltpu.VMEM((8, R), F32), pltpu.VMEM((8, R), F32), pltpu.VMEM((R, R), BF16),
                        pltpu.VMEM((R, R), BF16), pltpu.VMEM((R, R), F32), pltpu.VMEM((R, R), F32)],
        input_output_aliases={0: 0},
        compiler_params=_cp(("arbitrary",)),
    )(dproj, d_ya, proj, proj, h_all, h_all, conv_w, conv_b, wa_b, ba, wx_b, bx, lam, norm_rec)


NPAIR = R // LANES
QB, KB, VB, GB = 2 * R // LANES, 3 * R // LANES, 4 * R // LANES, 5 * R // LANES


def _rope_freq():
    half = HEAD // 2
    inv = np.float32(ROPE_THETA) ** (-(np.arange(half, dtype=np.float32) / np.float32(half)))
    return jnp.asarray(np.tile(inv.astype(np.float32), LANES // half)[None, :])


def _rot_half(x, first):
    return jnp.where(first, -pltpu.roll(x, LANES - HEAD // 2, 1), pltpu.roll(x, HEAD // 2, 1))


def _cos_sin(pos_ref, freq_ref):
    ang = pos_ref[...].astype(F32) * freq_ref[...]
    return jnp.cos(ang), jnp.sin(ang)


def _deint(src_ref, dst_ref, d):
    n = S // d
    for r in range(d):
        v = src_ref[pl.ds(r, n, stride=d), :] if d > 1 else src_ref[...]
        dst_ref[r * n:(r + 1) * n, :] = v.astype(dst_ref.dtype)


def _reint(src_ref, dst_ref, d, accumulate):
    n = S // d
    for r in range(d):
        idx = (pl.ds(r, n, stride=d), slice(None)) if d > 1 else (slice(None), slice(None))
        v = src_ref[r * n:(r + 1) * n, :]
        if accumulate:
            dst_ref[idx] = dst_ref[idx] + v
        else:
            dst_ref[idx] = v


def _blk_masks(b, nb):
    qi = lax.broadcasted_iota(jnp.int32, (BLK, BLK), 0)
    ki = lax.broadcasted_iota(jnp.int32, (BLK, BLK), 1)
    has_prev = lax.rem(b, nb) != 0
    return ki <= qi, jnp.logical_and(ki >= qi, has_prev)


def _rope_table(pos, freq):
    def body(pos_ref, freq_ref, cos_ref, sin_ref):
        cos_ref[...], sin_ref[...] = _cos_sin(pos_ref, freq_ref)

    return pl.pallas_call(body, name="rope_table", out_shape=[jax.ShapeDtypeStruct((S, LANES), F32)] * 2,
                          compiler_params=_cp())(pos, freq)


def _deint_heads(src_ref, dst0, dst1, d):
    n = S // d
    hm0 = lax.broadcasted_iota(jnp.int32, (n, LANES), 1) < HEAD
    for r in range(d):
        v = src_ref[pl.ds(r, n, stride=d), :] if d > 1 else src_ref[...]
        dst0[r * n:(r + 1) * n, :] = jnp.where(hm0, v, 0.0).astype(BF16)
        dst1[r * n:(r + 1) * n, :] = jnp.where(hm0, 0.0, v).astype(BF16)


def _reint_prev(src_ref, dst_ref, d):
    n = S // d
    if n == BLK:
        return
    for r in range(d):
        idx = (pl.ds(r, n - BLK, stride=d), slice(None)) if d > 1 else (slice(0, n - BLK), slice(None))
        dst_ref[idx] = dst_ref[idx] + src_ref[r * n + BLK:(r + 1) * n, :]


def _pair_masks():
    qi = lax.broadcasted_iota(jnp.int32, (BLK, 2 * BLK), 0)
    ki = lax.broadcasted_iota(jnp.int32, (BLK, 2 * BLK), 1) & (BLK - 1)
    return ki <= qi, ki >= qi


def _two(ref0, ref1, st, axis):
    return jnp.concatenate([ref0[pl.ds(st, BLK), :], ref1[pl.ds(st, BLK), :]], axis=axis)


ATT_UNROLL = 4


def _att_fwd(proj, cos, sin, w_out):
    def body(q_ref, k_ref, v_ref, cos_ref, sin_ref, w_ref, att_ref, qr_ref, kr_ref, lse_ref, wbf_ref,
             qd, kd0, kd1, vd0, vd1, od, ld, on, ln, wbuf, *wsems):
        wg = _WeightGather(w_ref, wbuf, *wsems)
        pl.when(pl.program_id(0) == 0)(wg.start)
        pl.when(pl.program_id(0) == 1)(wg.forward)
        lane = lax.broadcasted_iota(jnp.int32, (S, LANES), 1)
        first = (lane & (HEAD // 2)) == 0
        cos, sin = cos_ref[...], sin_ref[...]
        q = q_ref[...]
        k = k_ref[...]
        qr_ref[...] = (q * cos + _rot_half(q, first) * sin) * (HEAD ** -0.5)
        kr_ref[...] = k * cos + _rot_half(k, first) * sin
        hm0 = lax.broadcasted_iota(jnp.int32, (BLK, LANES), 1) < HEAD
        top = lax.broadcasted_iota(jnp.int32, (2 * BLK, LANES), 0) < BLK
        ones2 = (top == (lax.broadcasted_iota(jnp.int32, (2 * BLK, LANES), 1) < HEAD)).astype(BF16)
        mc2, mp2 = _pair_masks()

        for pi, d in enumerate(PATTERNS):
            nb = S // d // BLK
            _deint(qr_ref, qd, d)
            _deint_heads(kr_ref, kd0, kd1, d)
            _deint_heads(v_ref, vd0, vd1, d)

            def blk(b, carry):
                st = pl.multiple_of(b * BLK, BLK)
                qb = qd[pl.ds(st, BLK), :]
                sc = jnp.where(mc2, _dot_nt(qb, _two(kd0, kd1, st, 0)), NEG)
                mx = sc
                if nb > 1:
                    stp = pl.multiple_of(jnp.maximum(b - 1, 0) * BLK, BLK)
                    mp = jnp.logical_and(mp2, lax.rem(b, nb) != 0)
                    sp = jnp.where(mp, _dot_nt(qb, _two(kd0, kd1, stp, 0)), NEG)
                    mx = jnp.maximum(sc, sp)
                m0 = jnp.max(mx[:, 0:BLK], axis=1, keepdims=True)
                m1 = jnp.max(mx[:, BLK:2 * BLK], axis=1, keepdims=True)
                mf = jnp.concatenate([jnp.broadcast_to(m0, (BLK, BLK)), jnp.broadcast_to(m1, (BLK, BLK))], axis=1)
                o = _dot(jnp.exp(sc - mf).astype(BF16), jnp.concatenate([_two(vd0, vd1, st, 0), ones2], axis=1))
                if nb > 1:
                    o = o + _dot(jnp.exp(sp - mf).astype(BF16), jnp.concatenate([_two(vd0, vd1, stp, 0), ones2], axis=1))
                l = o[:, LANES:2 * LANES]
                od[pl.ds(st, BLK), :] = o[:, 0:LANES] / l
                ld[pl.ds(st, BLK), :] = jnp.where(hm0, m0, m1) + jnp.log(l)
                return carry

            lax.fori_loop(0, S // BLK, blk, 0, unroll=ATT_UNROLL)
            _reint(od, on.at[pi], d, False)
            _reint(ld, ln.at[pi], d, False)

        l0, l1, l2 = ln[0], ln[1], ln[2]
        m = jnp.maximum(jnp.maximum(l0, l1), l2)
        e0, e1, e2 = jnp.exp(l0 - m), jnp.exp(l1 - m), jnp.exp(l2 - m)
        den = e0 + e1 + e2
        att_ref[...] = (e0 * on[0] + e1 * on[1] + e2 * on[2]) / den
        lse_ref[...] = m + jnp.log(den)

        @pl.when(pl.program_id(0) == NPAIR - 1)
        def _():
            wg.finish()
            wbf_ref[...] = wbuf[...]

    col = lambda c0: pl.BlockSpec((S, LANES), lambda p: (0, c0 + p))
    out = pl.BlockSpec((S, LANES), lambda p: (0, p))
    tab = pl.BlockSpec((S, LANES), lambda p: (0, 0))
    vm = pl.BlockSpec(memory_space=pltpu.VMEM)
    return pl.pallas_call(
        body, name="att_fwd", grid=(NPAIR,),
        in_specs=[col(QB), col(KB), col(VB), tab, tab, vm],
        out_specs=[out, out, out, out, vm],
        out_shape=[jax.ShapeDtypeStruct((S, R), F32)] * 4 + [jax.ShapeDtypeStruct((NCHIP,) + w_out.shape, BF16)],
        scratch_shapes=[pltpu.VMEM((S, LANES), BF16)] * 5 + [pltpu.VMEM((S, LANES), F32)] * 2
        + [pltpu.VMEM((3, S, LANES), F32)] * 2 + [pltpu.VMEM((NCHIP,) + w_out.shape, BF16)] + _WeightGather.SEMS,
        compiler_params=_cp(("arbitrary",)),
    )(proj, proj, proj, cos, sin, w_out)


def _att_bwd(dproj, d_att, att, lse, qr, kr, proj, cos, sin, gw_out4):
    out_units = [(j, j, 0) for j in range(NCHIP)]

    def body(dp_in, do_ref, o_ref, lse_ref, qr_ref, kr_ref, v_ref, cos_ref, sin_ref, gw_ref, dp_ref, gout_ref,
             qd, kd0, kd1, vd0, vd1, dod, lb0d, lb1d, dl0d, dl1d, dqd, dkcd, dkpd, dvcd, dvpd,
             dqn, dkn, dvn, lb0n, lb1n, dl0n, dl1n, stage, sems, gred, *rs_scratch):
        p = pl.program_id(0)
        rs = _ReduceScatter(gw_ref, gred, out_units, *rs_scratch)
        for step, piece in enumerate((rs.start_halves, rs.send_partials, rs.reduce_owned)):
            pl.when(p == step)(piece)

        @pl.when(p == NPAIR - 1)
        def _():
            rs.finish()
            gout_ref[...] = gred[...]
        hms = lax.broadcasted_iota(jnp.int32, (S, LANES), 1) < HEAD
        prod = do_ref[...] * o_ref[...]
        dl0n[...] = jnp.broadcast_to(jnp.sum(jnp.where(hms, prod, 0.0), axis=1, keepdims=True), (S, LANES))
        dl1n[...] = jnp.broadcast_to(jnp.sum(jnp.where(hms, 0.0, prod), axis=1, keepdims=True), (S, LANES))
        lse = lse_ref[...]
        lsw = pltpu.roll(lse, HEAD, 1)
        lb0n[...] = jnp.where(hms, lse, lsw)
        lb1n[...] = jnp.where(hms, lsw, lse)
        dqn[...] = jnp.zeros_like(dqn)
        dkn[...] = jnp.zeros_like(dkn)
        dvn[...] = jnp.zeros_like(dvn)
        hm0 = lax.broadcasted_iota(jnp.int32, (BLK, LANES), 1) < HEAD
        mc2, mp2 = _pair_masks()

        for d in PATTERNS:
            nb = S // d // BLK
            _deint(qr_ref, qd, d)
            _deint_heads(kr_ref, kd0, kd1, d)
            _deint_heads(v_ref, vd0, vd1, d)
            _deint(do_ref, dod, d)
            for src, dst in ((lb0n, lb0d), (lb1n, lb1d), (dl0n, dl0d), (dl1n, dl1d)):
                _deint(src, dst, d)

            def blk(b, carry):
                st = pl.multiple_of(b * BLK, BLK)
                qb, dob = qd[pl.ds(st, BLK), :], dod[pl.ds(st, BLK), :]
                lb, dl = _two(lb0d, lb1d, st, 1), _two(dl0d, dl1d, st, 1)

                def side(stk, mask):
                    k2, v2 = _two(kd0, kd1, stk, 0), _two(vd0, vd1, stk, 0)
                    pk = jnp.where(mask, jnp.exp(_dot_nt(qb, k2) - lb), 0.0)
                    ds = (pk * (_dot_nt(dob, v2) - dl)).astype(BF16)
                    rk, rv = _dot_tn(ds, qb), _dot_tn(pk.astype(BF16), dob)
                    return (_dot(ds, k2), jnp.where(hm0, rk[0:BLK], rk[BLK:2 * BLK]),
                            jnp.where(hm0, rv[0:BLK], rv[BLK:2 * BLK]))

                dq, dkc, dvc = side(st, mc2)
                if nb > 1:
                    stp = pl.multiple_of(jnp.maximum(b - 1, 0) * BLK, BLK)
                    dqp, dkp, dvp = side(stp, jnp.logical_and(mp2, lax.rem(b, nb) != 0))
                    dq = dq + dqp
                    dkpd[pl.ds(st, BLK), :] = dkp
                    dvpd[pl.ds(st, BLK), :] = dvp
                dqd[pl.ds(st, BLK), :] = dq
                dkcd[pl.ds(st, BLK), :] = dkc
                dvcd[pl.ds(st, BLK), :] = dvc
                return carry

            lax.fori_loop(0, S // BLK, blk, 0, unroll=ATT_UNROLL)
            _reint(dqd, dqn, d, True)
            _reint(dkcd, dkn, d, True)
            _reint(dvcd, dvn, d, True)
            _reint_prev(dkpd, dkn, d)
            _reint_prev(dvpd, dvn, d)

        lane = lax.broadcasted_iota(jnp.int32, (S, LANES), 1)
        first = (lane & (HEAD // 2)) == 0
        cos, sin = cos_ref[...], sin_ref[...]
        dq = dqn[...] * (HEAD ** -0.5)
        dk = dkn[...]
        stage[0] = (dq * cos - _rot_half(dq, first) * sin).astype(BF16)
        stage[1] = (dk * cos - _rot_half(dk, first) * sin).astype(BF16)
        stage[2] = dvn[...].astype(BF16)
        copies = [pltpu.make_async_copy(stage.at[j], dp_ref.at[:, pl.ds((2 + j) * R + p * LANES, LANES)], sems.at[j])
                  for j in range(3)]
        for cp in copies:
            cp.start()
        for cp in copies:
            cp.wait()

    blk = pl.BlockSpec((S, LANES), lambda p: (0, p))
    tab = pl.BlockSpec((S, LANES), lambda p: (0, 0))
    vm = pl.BlockSpec(memory_space=pltpu.VMEM)
    _, orows, ocols = gw_out4.shape
    return pl.pallas_call(
        body, name="att_bwd", grid=(NPAIR,),
        in_specs=[pl.BlockSpec(memory_space=pl.ANY), blk, blk, blk, blk, blk,
                  pl.BlockSpec((S, LANES), lambda p: (0, VB + p)), tab, tab, vm],
        out_specs=[pl.BlockSpec(memory_space=pl.ANY), vm],
        out_shape=[jax.ShapeDtypeStruct((S, E), BF16), jax.ShapeDtypeStruct((orows, ocols), F32)],
        scratch_shapes=[pltpu.VMEM((S, LANES), BF16)] * 6 + [pltpu.VMEM((S, LANES), F32)] * 16
        + [pltpu.VMEM((3, S, LANES), BF16), pltpu.SemaphoreType.DMA((3,)), pltpu.VMEM((orows, ocols), F32)]
        + _ReduceScatter.scratch(NCHIP, orows, ocols, 1),
        input_output_aliases={0: 0},
        compiler_params=_cp(("arbitrary",)),
    )(dproj, d_att, att, lse, qr, kr, proj, cos, sin, gw_out4)


def _att_fwd_old(proj, pos, freq):
    def body(q_ref, k_ref, v_ref, pos_ref, freq_ref, att_ref, qr_ref, kr_ref, lse_ref,
             qd, kd, vd, od, ld, on, ln):
        lane = lax.broadcasted_iota(jnp.int32, (S, LANES), 1)
        first = (lane & (HEAD // 2)) == 0
        cos, sin = _cos_sin(pos_ref, freq_ref)
        q = q_ref[...]
        k = k_ref[...]
        qr_ref[...] = (q * cos + _rot_half(q, first) * sin) * (HEAD ** -0.5)
        kr_ref[...] = k * cos + _rot_half(k, first) * sin
        hm0 = lax.broadcasted_iota(jnp.int32, (BLK, LANES), 1) < HEAD

        for pi, d in enumerate(PATTERNS):
            nb = S // d // BLK
            _deint(qr_ref, qd, d)
            _deint(kr_ref, kd, d)
            _deint(v_ref, vd, d)

            def blk(b, carry):
                st = pl.multiple_of(b * BLK, BLK)
                stp = pl.multiple_of(jnp.maximum(b - 1, 0) * BLK, BLK)
                mc, mp = _blk_masks(b, nb)
                qb = qd[pl.ds(st, BLK), :]
                kc, kp = kd[pl.ds(st, BLK), :], kd[pl.ds(stp, BLK), :]
                vc, vp = vd[pl.ds(st, BLK), :], vd[pl.ds(stp, BLK), :]
                outs, lses = [], []
                for hm in (hm0, jnp.logical_not(hm0)):
                    qm = jnp.where(hm, qb, jnp.zeros_like(qb))
                    sc = jnp.where(mc, _dot_nt(qm, kc), NEG)
                    sp = jnp.where(mp, _dot_nt(qm, kp), NEG)
                    m = jnp.maximum(jnp.max(sc, axis=1, keepdims=True), jnp.max(sp, axis=1, keepdims=True))
                    pc, pp = jnp.exp(sc - m), jnp.exp(sp - m)
                    l = jnp.sum(pc, axis=1, keepdims=True) + jnp.sum(pp, axis=1, keepdims=True)
                    o = _dot(pc.astype(BF16), vc) + _dot(pp.astype(BF16), vp)
                    outs.append(o / l)
                    lses.append(m + jnp.log(l))
                od[pl.ds(st, BLK), :] = jnp.where(hm0, outs[0], outs[1])
                ld[pl.ds(st, BLK), :] = jnp.where(hm0, lses[0], lses[1])
                return carry

            lax.fori_loop(0, S // BLK, blk, 0)
            _reint(od, on.at[pi], d, False)
            _reint(ld, ln.at[pi], d, False)

        l0, l1, l2 = ln[0], ln[1], ln[2]
        m = jnp.maximum(jnp.maximum(l0, l1), l2)
        e0, e1, e2 = jnp.exp(l0 - m), jnp.exp(l1 - m), jnp.exp(l2 - m)
        den = e0 + e1 + e2
        att_ref[...] = (e0 * on[0] + e1 * on[1] + e2 * on[2]) / den
        lse_ref[...] = m + jnp.log(den)

    col = lambda c0: pl.BlockSpec((S, LANES), lambda p: (0, c0 + p))
    out = pl.BlockSpec((S, LANES), lambda p: (0, p))
    return pl.pallas_call(
        body, name="att_fwd", grid=(NPAIR,),
        in_specs=[col(QB), col(KB), col(VB), pl.BlockSpec((S, 1), lambda p: (0, 0)),
                  pl.BlockSpec((1, LANES), lambda p: (0, 0))],
        out_specs=[out, out, out, out],
        out_shape=[jax.ShapeDtypeStruct((S, R), F32)] * 4,
        scratch_shapes=[pltpu.VMEM((S, LANES), BF16)] * 3 + [pltpu.VMEM((S, LANES), F32)] * 2
        + [pltpu.VMEM((3, S, LANES), F32)] * 2,
        compiler_params=_cp(("parallel",)),
    )(proj, proj, proj, pos, freq)


def _att_bwd_old(dproj, d_att, att, lse, qr, kr, proj, pos, freq):
    def body(dp_in, do_ref, o_ref, lse_ref, qr_ref, kr_ref, v_ref, pos_ref, freq_ref, dp_ref,
             qd, kd, vd, dod, lsd, prd, dqd, dkd, dvd, dqn, dkn, dvn, prn, stage, sems):
        p = pl.program_id(0)
        prn[...] = do_ref[...] * o_ref[...]
        dqn[...] = jnp.zeros_like(dqn)
        dkn[...] = jnp.zeros_like(dkn)
        dvn[...] = jnp.zeros_like(dvn)
        hm0 = lax.broadcasted_iota(jnp.int32, (BLK, LANES), 1) < HEAD

        for d in PATTERNS:
            nb = S // d // BLK
            _deint(qr_ref, qd, d)
            _deint(kr_ref, kd, d)
            _deint(v_ref, vd, d)
            _deint(do_ref, dod, d)
            _deint(lse_ref, lsd, d)
            _deint(prn, prd, d)
            dkd[...] = jnp.zeros_like(dkd)
            dvd[...] = jnp.zeros_like(dvd)

            def blk(b, carry):
                st = pl.multiple_of(b * BLK, BLK)
                stp = pl.multiple_of(jnp.maximum(b - 1, 0) * BLK, BLK)
                mc, mp = _blk_masks(b, nb)
                qb, dob = qd[pl.ds(st, BLK), :], dod[pl.ds(st, BLK), :]
                kc, kp = kd[pl.ds(st, BLK), :], kd[pl.ds(stp, BLK), :]
                vc, vp = vd[pl.ds(st, BLK), :], vd[pl.ds(stp, BLK), :]
                lsb, prb = lsd[pl.ds(st, BLK), :], prd[pl.ds(st, BLK), :]
                dqs = []
                dkc = dkp = dvc = dvp = None
                for hm in (hm0, jnp.logical_not(hm0)):
                    qm = jnp.where(hm, qb, jnp.zeros_like(qb))
                    dom = jnp.where(hm, dob, jnp.zeros_like(dob))
                    lh = jnp.max(jnp.where(hm, lsb, -3e38), axis=1, keepdims=True)
                    delta = jnp.sum(jnp.where(hm, prb, 0.0), axis=1, keepdims=True)
                    pc = jnp.where(mc, jnp.exp(_dot_nt(qm, kc) - lh), 0.0)
                    pp = jnp.where(mp, jnp.exp(_dot_nt(qm, kp) - lh), 0.0)
                    dsc = (pc * (_dot_nt(dom, vc) - delta)).astype(BF16)
                    dsp = (pp * (_dot_nt(dom, vp) - delta)).astype(BF16)
                    dqs.append(_dot(dsc, kc) + _dot(dsp, kp))
                    acc = lambda t, n: n if t is None else t + n
                    dkc, dkp = acc(dkc, _dot_tn(dsc, qm)), acc(dkp, _dot_tn(dsp, qm))
                    dvc, dvp = acc(dvc, _dot_tn(pc.astype(BF16), dom)), acc(dvp, _dot_tn(pp.astype(BF16), dom))
                dqd[pl.ds(st, BLK), :] = jnp.where(hm0, dqs[0], dqs[1])
                dkd[pl.ds(stp, BLK), :] += dkp
                dvd[pl.ds(stp, BLK), :] += dvp
                dkd[pl.ds(st, BLK), :] += dkc
                dvd[pl.ds(st, BLK), :] += dvc
                return carry

            lax.fori_loop(0, S // BLK, blk, 0)
            _reint(dqd, dqn, d, True)
            _reint(dkd, dkn, d, True)
            _reint(dvd, dvn, d, True)

        lane = lax.broadcasted_iota(jnp.int32, (S, LANES), 1)
        first = (lane & (HEAD // 2)) == 0
        cos, sin = _cos_sin(pos_ref, freq_ref)
        dq = dqn[...] * (HEAD ** -0.5)
        dk = dkn[...]
        stage[0] = (dq * cos - _rot_half(dq, first) * sin).astype(BF16)
        stage[1] = (dk * cos - _rot_half(dk, first) * sin).astype(BF16)
        stage[2] = dvn[...].astype(BF16)
        copies = [pltpu.make_async_copy(stage.at[j], dp_ref.at[:, pl.ds((2 + j) * R + p * LANES, LANES)], sems.at[j])
                  for j in range(3)]
        for cp in copies:
            cp.start()
        for cp in copies:
            cp.wait()

    blk = pl.BlockSpec((S, LANES), lambda p: (0, p))
    return pl.pallas_call(
        body, name="att_bwd", grid=(NPAIR,),
        in_specs=[pl.BlockSpec(memory_space=pl.ANY), blk, blk, blk, blk, blk,
                  pl.BlockSpec((S, LANES), lambda p: (0, VB + p)), pl.BlockSpec((S, 1), lambda p: (0, 0)),
                  pl.BlockSpec((1, LANES), lambda p: (0, 0))],
        out_specs=pl.BlockSpec(memory_space=pl.ANY),
        out_shape=jax.ShapeDtypeStruct((S, E), BF16),
        scratch_shapes=[pltpu.VMEM((S, LANES), BF16)] * 4 + [pltpu.VMEM((S, LANES), F32)] * 9
        + [pltpu.VMEM((3, S, LANES), BF16), pltpu.SemaphoreType.DMA((3,))],
        input_output_aliases={0: 0},
        compiler_params=_cp(("arbitrary",)),
    )(dproj, d_att, att, lse, qr, kr, proj, pos, freq)


def _out_fwd_bwd(ya, att, proj, w_out_bf, x, target, mod, norm_post, norm_att):
    ts = 256

    def body(ya_ref, att_ref, gb_ref, w_ref, x_ref, t_ref, mod_ref, npost_ref, natt_ref,
             gx_ref, dya_ref, datt_ref, dgb_ref, gw_ref, acc_ref):
        i = pl.program_id(0)

        @pl.when(i == 0)
        def _():
            gw_ref[...] = jnp.zeros_like(gw_ref)
            acc_ref[...] = jnp.zeros_like(acc_ref)

        gate = mod_ref[:, 2 * D:3 * D]
        att = att_ref[...]
        gb = gb_ref[...]
        sg = _sigmoid(gb)
        silu = gb * sg
        ybp = att * silu
        yb, ybn, rstd_b = _rms_fwd(ybp, natt_ref[...])
        cat = jnp.concatenate([ya_ref[...], yb.astype(BF16)], axis=1)
        mix = _dot(cat, w_ref[...])
        rn, mn, rstd_m = _rms_fwd(mix, npost_ref[...])
        err = x_ref[...] + gate * rn - t_ref[...]
        dy = err * (1.0 / D)
        gx_ref[...] = dy
        dmix, dnpost = _rms_bwd(dy * gate, mn, rstd_m, npost_ref[...])
        dmb = dmix.astype(BF16)
        gw_ref[...] += _dot_tn(cat, dmb)
        dcat = _dot_nt(dmb, w_ref[...])
        dya_ref[...] = dcat[:, 0:R]
        dybp, dnatt = _rms_bwd(dcat[:, R:2 * R], ybn, rstd_b, natt_ref[...])
        datt_ref[...] = dybp * silu
        dgb_ref[...] = (dybp * att * (sg * (1.0 + gb * (1.0 - sg)))).astype(BF16)
        acc_ref[0:1, :] += jnp.sum(dy * rn, axis=0, keepdims=True)
        acc_ref[1:2, :] += dnpost
        acc_ref[2:3, 0:R] += dnatt
        acc_ref[3:4, :] += jnp.sum(jnp.sum(err * err, axis=1, keepdims=True), axis=0, keepdims=True)

    tile = lambda w: pl.BlockSpec((ts, w), lambda i: (i, 0))
    c0 = lambda shape: pl.BlockSpec(shape, lambda i: (0, 0))
    return pl.pallas_call(
        body, name="out_fwd_bwd", grid=(S // ts,),
        in_specs=[tile(R), tile(R), pl.BlockSpec((ts, R), lambda i: (i, 5)), c0((D, D)), tile(D), tile(D),
                  c0((1, 3 * D)), c0((1, D)), c0((1, R))],
        out_specs=[tile(D), tile(R), tile(R), pl.BlockSpec((ts, R), lambda i: (i, 5)), c0((D, D)), c0((8, D))],
        out_shape=[jax.ShapeDtypeStruct((S, D), F32), jax.ShapeDtypeStruct((S, R), F32),
                   jax.ShapeDtypeStruct((S, R), F32), jax.ShapeDtypeStruct((S, E), BF16),
                   jax.ShapeDtypeStruct((D, D), F32), jax.ShapeDtypeStruct((8, D), F32)],
        compiler_params=_cp(("arbitrary",)),
    )(ya, att, proj, w_out_bf, x, target, mod, norm_post, norm_att)


UC = 256
UPC = EC // UC


NU = E // UC


def _unit_of_step(i):
    return (i % NCHIP) * UPC + i // NCHIP


def _in_proj_bwd(ht, dproj, w_in_bf, x, gx1, mod, norm_pre, smalls):
    ts = 256
    nt = S // ts
    half = D // 2
    units = [_unit_of_step(k) for k in range(NU)]
    owners = [u // UPC for u in units]
    ns = len(smalls)

    def body(*refs):
        (ht_ref, dpu_ref, dp_ref, w_ref, x_ref, gx1_ref, mod_ref, np_ref), refs = refs[:8], refs[8:]
        small_in, refs = refs[:ns], refs[ns:]
        (gx_ref, gin_ref), refs = refs[:2], refs[2:]
        small_out, (acc_out,), refs = refs[:ns], refs[ns:ns + 1], refs[ns + 1:]
        mine, sib, tmp, stage, got, red, acc_ref, hs, hr, ps, pr, bs, br = refs[:13]
        early = _SmallGather(small_in, small_out, *refs[13:16])
        late = _SmallGather([acc_ref], [acc_out], *refs[16:19])
        i = pl.program_id(0)
        xx, yy, c = _me()
        ci = 2 * xx + yy
        r0 = pl.multiple_of(c * half, half)
        r1 = pl.multiple_of((1 - c) * half, half)
        pl.when(i == 0)(early.start)
        pl.when(i == NU)(early.forward)

        def exch(k):
            return _remote(tmp.at[k % 2], sib.at[k], hs.at[k], hr.at[k], 1)

        def partial(k, sender):
            return pltpu.make_async_remote_copy(
                src_ref=stage.at[k], dst_ref=got.at[units[k] % UPC, sender], send_sem=ps.at[k],
                recv_sem=pr.at[k, sender], device_id=(owners[k] // 2, owners[k] % 2, c), device_id_type=MESH)

        def back(k, start):
            off = (units[k] % UPC) * UC
            blk = red.at[pl.ds(start, half), off:off + UC]
            return _remote(blk, blk, bs.at[k], br.at[k], 1)

        for k in range(NU + 1):
            @pl.when(i == k)
            def _():
                if k >= 1:
                    exch(k - 1).wait_recv()
                    mine[k - 1] += sib[k - 1]

                    @pl.when(ci != owners[k - 1])
                    def _():
                        stage[k - 1] = mine[k - 1].astype(BF16)
                        partial(k - 1, ci).start()
                if k < NU:
                    if k >= 2:
                        exch(k - 2).wait_send()
                    dpu = dpu_ref[...]
                    mine[k] = _dot(ht_ref[pl.ds(r0, half), :], dpu)
                    tmp[k % 2] = _dot(ht_ref[pl.ds(r1, half), :], dpu)
                    exch(k).start()

        @pl.when(i == NU)
        def _():
            acc_ref[...] = jnp.zeros_like(acc_ref)

        @pl.when(i >= NU)
        def _():
            dh = sum(_dot_nt(dp_ref[:, j * EC:(j + 1) * EC], w_ref[j]) for j in range(NCHIP))
            hp, xn, rstd = _rms_fwd(x_ref[...], np_ref[...])
            dx, dnp = _rms_bwd(dh * (1.0 + mod_ref[:, D:2 * D]), xn, rstd, np_ref[...])
            gx_ref[...] = gx1_ref[...] + dx
            acc_ref[0:1, :] += jnp.sum(dh, axis=0, keepdims=True)
            acc_ref[1:2, :] += jnp.sum(dh * hp, axis=0, keepdims=True)
            acc_ref[2:3, :] += dnp

        for t in range(UPC):
            @pl.when(i == NU + 1 + 2 * t)
            def _():
                for k in range(NCHIP * t, NCHIP * (t + 1)):
                    @pl.when(ci == owners[k])
                    def _():
                        off = (units[k] % UPC) * UC
                        red[pl.ds(r0, half), off:off + UC] = mine[k]
                        for s in range(NCHIP):
                            if s != owners[k]:
                                partial(k, s).wait_recv()
                                red[pl.ds(r0, half), off:off + UC] += got[units[k] % UPC, s].astype(F32)
                        back(k, r0).start()

        @pl.when(i == NU + nt - 1)
        def _():
            late.start()
            exch(NU - 2).wait_send()
            exch(NU - 1).wait_send()
            for k in range(NU):
                @pl.when(ci == owners[k])
                def _():
                    back(k, r1).wait_recv()
                    back(k, r0).wait_send()

                @pl.when(ci != owners[k])
                def _():
                    partial(k, ci).wait_send()
            gin_ref[...] = red[...]
            early.finish()
            late.forward()
            late.finish()

    tile = lambda w: pl.BlockSpec((ts, w), lambda i: (jnp.maximum(i - NU, 0), 0))
    c0 = lambda shape: pl.BlockSpec(shape, lambda i: (0, 0))
    vm = pl.BlockSpec(memory_space=pltpu.VMEM)
    hbm = pl.BlockSpec(memory_space=pl.ANY)
    gathered = [jax.ShapeDtypeStruct((NDEV,) + a.shape, F32) for a in smalls] + [jax.ShapeDtypeStruct((NDEV, 8, D), F32)]
    return pl.pallas_call(
        body, name="in_proj_bwd", grid=(NU + nt,),
        in_specs=[vm, pl.BlockSpec((S, UC), lambda i: (0, _unit_of_step(jnp.minimum(i, NU - 1)))), tile(E),
                  vm, tile(D), tile(D), c0((1, 3 * D)), c0((1, D))] + [vm] * ns,
        out_specs=[tile(D), vm] + [hbm] * (ns + 1),
        out_shape=[jax.ShapeDtypeStruct((S, D), F32), jax.ShapeDtypeStruct((D, EC), F32)] + gathered,
        scratch_shapes=[pltpu.VMEM((NU, half, UC), F32), pltpu.VMEM((NU, half, UC), F32),
                        pltpu.VMEM((2, half, UC), F32), pltpu.VMEM((NU, half, UC), BF16),
                        pltpu.VMEM((UPC, NCHIP, half, UC), BF16), pltpu.VMEM((D, EC), F32), pltpu.VMEM((8, D), F32),
                        pltpu.SemaphoreType.DMA((NU,)), pltpu.SemaphoreType.DMA((NU,)),
                        pltpu.SemaphoreType.DMA((NU,)), pltpu.SemaphoreType.DMA((NU, NCHIP)),
                        pltpu.SemaphoreType.DMA((NU,)), pltpu.SemaphoreType.DMA((NU,))]
        + _SmallGather.sems(ns) + _SmallGather.sems(1),
        compiler_params=_cp(("arbitrary",)),
    )(ht, dproj, dproj, w_in_bf, x, gx1, mod, norm_pre, *smalls)


def _block_diag(w):
    n, b, _ = w.shape
    eye = jnp.eye(n, dtype=w.dtype)
    return (eye[:, None, :, None] * w[:, :, None, :]).reshape(n * b, n * b)


def _diag_blocks(m):
    n, b = R // HEAD, HEAD
    return jnp.stack([m[h * b:(h + 1) * b, h * b:(h + 1) * b] for h in range(n)])


def _local_step(x, pos, target, mod, w_in_bf, w_out, conv_w, p):
    rec_p = (conv_w, p["conv_b"], p["w_rg_a"], p["b_rg_a"], p["w_rg_x"], p["b_rg_x"], p["lru_lambda"], p["norm_rec"])
    cos, sin = _rope_table(pos, _rope_freq())
    proj, ht = _in_proj_fwd(x, mod, p["norm_pre"], w_in_bf)
    h_all, ya = _rec_fwd(proj, *rec_p)
    att, qr, kr, lse, w_out_bf = _att_fwd(proj, cos, sin, w_out)
    gx1, d_ya, d_att, dproj, gw_out, acc_o = _out_fwd_bwd(ya, att, proj, w_out_bf.reshape(D, D), x, target, mod,
                                                           p["norm_post"], p["norm_att"])
    dproj, g_out = _att_bwd(dproj, d_att, att, lse, qr, kr, proj, cos, sin, gw_out.reshape(NCHIP, D // NCHIP, D))
    dproj, dwa, dwx, sm = _rec_bwd(dproj, d_ya, proj, h_all, *rec_p)
    grad_x, g_in, *gathered = _in_proj_bwd(ht, dproj, w_in_bf, x, gx1, mod, p["norm_pre"], [acc_o, sm, dwa, dwx])
    return grad_x, g_in, g_out, gathered


def _me():
    return lax.axis_index("x"), lax.axis_index("y"), lax.axis_index("c")


def _flip(v, bit):
    return 1 - v if bit else v


def _peer(rel):
    x, y, c = _me()
    return (_flip(x, rel & 4), _flip(y, rel & 2), _flip(c, rel & 1))


def _remote(src, dst, send_sem, recv_sem, rel):
    return pltpu.make_async_remote_copy(src_ref=src, dst_ref=dst, send_sem=send_sem, recv_sem=recv_sem,
                                        device_id=_peer(rel), device_id_type=MESH)


def _allgather_rows(row, name):
    w = row.shape[1]

    def body(row_ref, out_ref, send_sems, recv_sems, local_sem):
        x, y, c = _me()
        me = 4 * x + 2 * y + c
        mine = pltpu.make_async_copy(row_ref, out_ref.at[pl.ds(me, 1), :], local_sem)
        mine.start()
        sends = [_remote(row_ref, out_ref.at[pl.ds(me, 1), :], send_sems.at[r - 1], recv_sems.at[r - 1], r)
                 for r in range(1, NDEV)]
        for cp in sends:
            cp.start()
        for r in range(1, NDEV):
            px, py, pc = _peer(r)
            src = 4 * px + 2 * py + pc
            _remote(row_ref, out_ref.at[pl.ds(src, 1), :], send_sems.at[r - 1], recv_sems.at[r - 1], r).wait_recv()
        for cp in sends:
            cp.wait_send()
        mine.wait()

    return pl.pallas_call(
        body, name=name,
        in_specs=[pl.BlockSpec(memory_space=pltpu.VMEM)],
        out_specs=pl.BlockSpec(memory_space=pltpu.VMEM),
        out_shape=jax.ShapeDtypeStruct((NDEV, w), row.dtype),
        scratch_shapes=[pltpu.SemaphoreType.DMA((NDEV - 1,)), pltpu.SemaphoreType.DMA((NDEV - 1,)),
                        pltpu.SemaphoreType.DMA],
        compiler_params=pltpu.CompilerParams(vmem_limit_bytes=VMEM_LIMIT),
    )(row)


class _WeightGather:
    SEMS = [pltpu.SemaphoreType.DMA((NCHIP - 1,))] * 4

    def __init__(self, w_ref, out_ref, send_sems, recv_sems, fsend_sems, frecv_sems):
        x, y, c = _me()
        self.w, self.out, self.ci = w_ref, out_ref, 2 * x + y
        self.half = w_ref.shape[0] // 2
        self.r0 = pl.multiple_of(c * self.half, self.half)
        self.r1 = pl.multiple_of((1 - c) * self.half, self.half)
        self.sems = (send_sems, recv_sems, fsend_sems, frecv_sems)

    def _ici(self, chip, k):
        blk = self.out.at[chip, pl.ds(self.r0, self.half), :]
        return _remote(blk, blk, self.sems[0].at[k - 1], self.sems[1].at[k - 1], 2 * k)

    def _d2d(self, chip, start, k):
        blk = self.out.at[chip, pl.ds(start, self.half), :]
        return _remote(blk, blk, self.sems[2].at[k - 1], self.sems[3].at[k - 1], 1)

    def start(self):
        self.out[self.ci] = self.w[...].astype(BF16)
        for k in range(1, NCHIP):
            self._ici(self.ci, k).start()

    def forward(self):
        for k in range(1, NCHIP):
            self._ici(self.ci ^ k, k).wait_recv()
            self._d2d(self.ci ^ k, self.r0, k).start()

    def finish(self):
        for k in range(1, NCHIP):
            self._d2d(self.ci ^ k, self.r1, k).wait_recv()
        for k in range(1, NCHIP):
            self._ici(self.ci, k).wait_send()
            self._d2d(self.ci ^ k, self.r0, k).wait_send()


class _SmallGather:
    @staticmethod
    def sems(n):
        return [pltpu.SemaphoreType.DMA((n, 7)), pltpu.SemaphoreType.DMA((n, 7)), pltpu.SemaphoreType.DMA((n,))]

    def __init__(self, srcs, outs, send_sems, recv_sems, local_sems):
        x, y, c = _me()
        self.srcs, self.outs = list(srcs), list(outs)
        self.ss, self.rs, self.ls = send_sems, recv_sems, local_sems
        self.ci, self.c = 2 * x + y, c
        self.me = 2 * self.ci + c

    def _own(self, a, slot, rel):
        return _remote(self.srcs[a], self.outs[a].at[self.me], self.ss.at[a, slot], self.rs.at[a, slot], rel)

    def _block(self, a, idx, slot, rel):
        blk = self.outs[a].at[idx]
        return _remote(blk, blk, self.ss.at[a, slot], self.rs.at[a, slot], rel)

    def _local(self, a):
        return pltpu.make_async_copy(self.srcs[a], self.outs[a].at[self.me], self.ls.at[a])

    def start(self):
        for a in range(len(self.srcs)):
            self._local(a).start()
            self._own(a, 0, 1).start()
            for k in range(1, NCHIP):
                self._own(a, k, 2 * k).start()

    def forward(self):
        for a in range(len(self.srcs)):
            for k in range(1, NCHIP):
                idx = 2 * (self.ci ^ k) + self.c
                self._block(a, idx, k, 2 * k).wait_recv()
                self._block(a, idx, 3 + k, 1).start()

    def finish(self):
        for a in range(len(self.srcs)):
            self._block(a, 2 * self.ci + 1 - self.c, 0, 1).wait_recv()
            for k in range(1, NCHIP):
                self._block(a, 2 * (self.ci ^ k) + 1 - self.c, 3 + k, 1).wait_recv()
            self._own(a, 0, 1).wait_send()
            for k in range(1, NCHIP):
                self._own(a, k, 2 * k).wait_send()
                self._block(a, 2 * (self.ci ^ k) + self.c, 3 + k, 1).wait_send()
            self._local(a).wait()


def _start_gather(crow, w_ada, b_cols, w_in):
    wc = crow.shape[1]

    def body(crow_ref, wada_ref, b_ref, win_ref, g0_ref, mod_ref, wbf_ref,
             modp, modb, cs, cr, ms, mr, ws, wr, fs, fr, local_sems):
        x, y, c = _me()
        ci = 2 * x + y
        me = 2 * ci + c
        wg = _WeightGather(win_ref, wbf_ref, ws, wr, fs, fr)
        mine = pltpu.make_async_copy(crow_ref, g0_ref.at[pl.ds(me, 1), :], local_sems.at[0])
        mine.start()
        csend = [_remote(crow_ref, g0_ref.at[pl.ds(me, 1), :], cs.at[r - 1], cr.at[r - 1], r) for r in range(1, NDEV)]
        for cp in csend:
            cp.start()
        wg.start()
        for r in range(1, NDEV):
            px, py, pc = _peer(r)
            _remote(crow_ref, g0_ref.at[pl.ds(4 * px + 2 * py + pc, 1), :], cs.at[r - 1], cr.at[r - 1], r).wait_recv()
        mine.wait()
        cv = g0_ref[:, 0:D]
        sc = cv * _sigmoid(cv)
        scb = jnp.concatenate([sc, jnp.zeros_like(sc)], axis=0).astype(BF16)
        modp[...] = _dot(scb, wada_ref[...].astype(BF16))[0:NDEV, :] + b_ref[...]
        own = pltpu.make_async_copy(modp.at[pl.ds(me, 1), :], modb.at[ci], local_sems.at[1])
        own.start()
        msend = []
        for k in range(1, NCHIP):
            dst = 2 * (ci ^ k) + c
            cp = _remote(modp.at[pl.ds(dst, 1), :], modb.at[ci], ms.at[k - 1], mr.at[k - 1], 2 * k)
            cp.start()
            msend.append(cp)
        for k in range(1, NCHIP):
            _remote(modp.at[pl.ds(me, 1), :], modb.at[ci ^ k], ms.at[k - 1], mr.at[k - 1], 2 * k).wait_recv()
        own.wait()
        for j in range(NCHIP):
            mod_ref[:, j * EC:(j + 1) * EC] = modb[j]
        wg.forward()
        wg.finish()
        for cp in csend + msend:
            cp.wait_send()

    vm = pl.BlockSpec(memory_space=pltpu.VMEM)
    return pl.pallas_call(
        body, name="start_gather",
        in_specs=[vm] * 4, out_specs=[vm] * 3,
        out_shape=[jax.ShapeDtypeStruct((NDEV, wc), F32), jax.ShapeDtypeStruct((1, 3 * D), F32),
                   jax.ShapeDtypeStruct((NCHIP, D, EC), BF16)],
        scratch_shapes=[pltpu.VMEM((NDEV, EC), F32), pltpu.VMEM((NCHIP, 1, EC), F32),
                        pltpu.SemaphoreType.DMA((NDEV - 1,)), pltpu.SemaphoreType.DMA((NDEV - 1,)),
                        pltpu.SemaphoreType.DMA((NCHIP - 1,)), pltpu.SemaphoreType.DMA((NCHIP - 1,))]
        + _WeightGather.SEMS + [pltpu.SemaphoreType.DMA((2,))],
        compiler_params=pltpu.CompilerParams(vmem_limit_bytes=VMEM_LIMIT),
    )(crow, w_ada, b_cols, w_in)


class _ReduceScatter:
    @staticmethod
    def scratch(n_units, rows, ucols, max_owned):
        half = rows // 2
        return [pltpu.VMEM((n_units, half, ucols), F32), pltpu.VMEM((n_units, half, ucols), BF16),
                pltpu.VMEM((max_owned, NCHIP, half, ucols), BF16),
                pltpu.SemaphoreType.DMA((2,)), pltpu.SemaphoreType.DMA((n_units,)),
                pltpu.SemaphoreType.DMA((n_units, NCHIP)), pltpu.SemaphoreType.DMA((n_units,)),
                pltpu.SemaphoreType.DMA((n_units,))]

    def __init__(self, g_ref, out_ref, units, sib, stage, got, sem1, send2, recv2, send3, recv3):
        x, y, c = _me()
        self.c, self.ci = c, 2 * x + y
        self.g, self.out, self.units = g_ref, out_ref, units
        self.sib, self.stage, self.got = sib, stage, got
        self.sem1, self.send2, self.recv2, self.send3, self.recv3 = sem1, send2, recv2, send3, recv3
        self.half = g_ref.shape[1] // 2
        self.ucols = g_ref.shape[2]
        self.r0 = pl.multiple_of(c * self.half, self.half)
        self.r1 = pl.multiple_of((1 - c) * self.half, self.half)
        self.slot0 = units[0][0]
        assert [u[0] for u in units] == list(range(self.slot0, self.slot0 + len(units)))
        seen = {}
        self.local = []
        for _, owner, _ in units:
            self.local.append(seen.get(owner, 0))
            seen[owner] = seen.get(owner, 0) + 1

    def _halves(self):
        n = len(self.units)
        return _remote(self.g.at[pl.ds(self.slot0, n), pl.ds(self.r1, self.half), :], self.sib,
                       self.sem1.at[0], self.sem1.at[1], 1)

    def _partial(self, i, sender):
        _, owner, _ = self.units[i]
        return pltpu.make_async_remote_copy(
            src_ref=self.stage.at[i], dst_ref=self.got.at[self.local[i], sender],
            send_sem=self.send2.at[i], recv_sem=self.recv2.at[i, sender],
            device_id=(owner // 2, owner % 2, self.c), device_id_type=MESH)

    def _back(self, i, start):
        off = self.units[i][2]
        blk = self.out.at[pl.ds(start, self.half), off:off + self.ucols]
        return _remote(blk, blk, self.send3.at[i], self.recv3.at[i], 1)

    def at_steps(self, step, start, send, reduce, finish, out_ref):
        @pl.when(step == start)
        def _():
            self.out[...] = jnp.zeros_like(self.out)
            self.start_halves()

        pl.when(step == send)(self.send_partials)
        pl.when(step == reduce)(self.reduce_owned)

        @pl.when(step == finish)
        def _():
            self.finish()
            out_ref[...] = self.out[...]

    def start_halves(self):
        self._halves().start()

    def send_partials(self):
        self._halves().wait_recv()
        for i, (slot, owner, _) in enumerate(self.units):
            @pl.when(self.ci != owner)
            def _():
                self.stage[i] = (self.g[slot, pl.ds(self.r0, self.half), :] + self.sib[i]).astype(BF16)
                self._partial(i, self.ci).start()

    def reduce_owned(self):
        for i, (slot, owner, off) in enumerate(self.units):
            @pl.when(self.ci == owner)
            def _():
                rows, cols = pl.ds(self.r0, self.half), slice(off, off + self.ucols)
                self.out[rows, cols] = self.g[slot, pl.ds(self.r0, self.half), :] + self.sib[i]
                for s in range(NCHIP):
                    if s != owner:
                        self._partial(i, s).wait_recv()
                        self.out[rows, cols] += self.got[self.local[i], s].astype(F32)
                self._back(i, self.r0).start()

    def finish(self):
        self._halves().wait_send()
        for i, (_, owner, _) in enumerate(self.units):
            @pl.when(self.ci == owner)
            def _():
                self._back(i, self.r1).wait_recv()
                self._back(i, self.r0).wait_send()

            @pl.when(self.ci != owner)
            def _():
                self._partial(i, self.ci).wait_send()


def _reduce_scatter(g4, name):
    _, rows, cols = g4.shape
    units = [(j, j, 0) for j in range(NCHIP)]

    def body(g_ref, out_ref, *scratch):
        rs = _ReduceScatter(g_ref, out_ref, units, *scratch)
        rs.start_halves()
        rs.send_partials()
        rs.reduce_owned()
        rs.finish()

    return pl.pallas_call(
        body, name=name,
        in_specs=[pl.BlockSpec(memory_space=pltpu.VMEM)],
        out_specs=pl.BlockSpec(memory_space=pltpu.VMEM),
        out_shape=jax.ShapeDtypeStruct((rows, cols), F32),
        scratch_shapes=_ReduceScatter.scratch(NCHIP, rows, cols, 1),
        compiler_params=pltpu.CompilerParams(vmem_limit_bytes=VMEM_LIMIT),
    )(g4)


def _silu_rows(c_ref):
    cv = c_ref[...]
    sc = cv * _sigmoid(cv)
    return jnp.concatenate([sc, jnp.zeros_like(sc)], axis=0).astype(BF16)


def _ada_fwd(cg, w_ada, b_cols):
    def body(c_ref, w_ref, b_ref, o_ref):
        o_ref[...] = _dot(_silu_rows(c_ref), w_ref[...].astype(BF16))[0:NDEV, :] + b_ref[...]

    return pl.pallas_call(body, name="ada_fwd", out_shape=jax.ShapeDtypeStruct((NDEV, EC), F32),
                          compiler_params=_cp())(cg, w_ada, b_cols)


def _ada_bwd(cg, dmod_cols):
    def body(c_ref, d_ref, o_ref):
        dm = d_ref[...]
        dmb = jnp.concatenate([dm, jnp.zeros_like(dm)], axis=0).astype(BF16)
        o_ref[...] = _dot_tn(_silu_rows(c_ref), dmb)

    return pl.pallas_call(body, name="ada_bwd", out_shape=jax.ShapeDtypeStruct((D, EC), F32),
                          compiler_params=_cp())(cg, dmod_cols)


def _sum_rows(g):
    def body(g_ref, o_ref):
        acc = g_ref[0:1, :]
        for r in range(1, NDEV):
            acc = acc + g_ref[r:r + 1, :]
        o_ref[...] = acc

    return pl.pallas_call(body, name="sum_rows", out_shape=jax.ShapeDtypeStruct((1, g.shape[1]), F32),
                          compiler_params=_cp())(g)


def _adamw(w, g, m, v, name):
    rows, cols = w.shape
    tr = 256 if rows % 256 == 0 else rows

    def body(w_ref, g_ref, m_ref, v_ref, d_ref, nm_ref, nv_ref):
        gv = g_ref[...]
        nm = B1 * m_ref[...] + (1.0 - B1) * gv
        nv = B2 * v_ref[...] + (1.0 - B2) * (gv * gv)
        m_hat = nm / (1.0 - B1 ** STEP)
        v_hat = nv / (1.0 - B2 ** STEP)
        d_ref[...] = (-LR) * (m_hat / (jnp.sqrt(v_hat) + ADAM_EPS) + WD * w_ref[...])
        nm_ref[...] = nm
        nv_ref[...] = nv

    spec = pl.BlockSpec((tr, cols), lambda i: (i, 0))
    return pl.pallas_call(
        body, name=name, grid=(rows // tr,), in_specs=[spec] * 4, out_specs=[spec] * 3,
        out_shape=[jax.ShapeDtypeStruct((rows, cols), F32)] * 3,
        compiler_params=_cp(("parallel",)),
    )(w, g, m, v)


def _adamw_values(w, g, m, v):
    nm = B1 * m + (1.0 - B1) * g
    nv = B2 * v + (1.0 - B2) * (g * g)
    m_hat = nm / (1.0 - B1 ** STEP)
    v_hat = nv / (1.0 - B2 ** STEP)
    return (-LR) * (m_hat / (jnp.sqrt(v_hat) + ADAM_EPS) + WD * w), nm, nv


NB = R // HEAD
SMALL = (("b_ada", (1, 3 * D)), ("norm_pre", (1, D)), ("norm_post", (1, D)), ("conv_w", (4, R // NCHIP)),
         ("conv_b", (1, R)), ("w_rg_a", (NB, HEAD, HEAD)), ("b_rg_a", (1, R)), ("w_rg_x", (NB, HEAD, HEAD)),
         ("b_rg_x", (1, R)), ("lru_lambda", (1, R)), ("norm_rec", (1, R)), ("norm_att", (1, R)))


def _small_update(ao8, sm8, dwa8, dwx8, ai8, cg, params):
    n = len(SMALL)

    def body(ao_ref, sm_ref, dwa_ref, dwx_ref, ai_ref, cg_ref, *refs):
        pin, pout, (gada_ref, loss_ref, dmod) = refs[:3 * n], refs[3 * n:7 * n], refs[7 * n:]
        xx, yy, _ = _me()
        ci = 2 * xx + yy

        def total(ref, *idx):
            acc = ref[(0,) + idx]
            for d in range(1, NDEV):
                acc = acc + ref[(d,) + idx]
            return acc

        row = lambda ref, r, lanes=slice(None): total(ref, slice(r, r + 1), lanes)
        mine = lambda parts: sum(jnp.where(ci == j, part, 0.0) for j, part in enumerate(parts))
        cw = R // NCHIP
        grads = {
            "b_ada": [jnp.concatenate([row(ai_ref, 0), row(ai_ref, 1), row(ao_ref, 0)], axis=1)],
            "norm_pre": [row(ai_ref, 2)], "norm_post": [row(ao_ref, 1)],
            "conv_w": [mine([row(sm_ref, 8 + r, slice(j * cw, (j + 1) * cw)) for j in range(NCHIP)]) for r in range(4)],
            "conv_b": [row(sm_ref, 4)], "b_rg_a": [row(sm_ref, 0)], "b_rg_x": [row(sm_ref, 1)],
            "lru_lambda": [row(sm_ref, 2)], "norm_rec": [row(sm_ref, 3)], "norm_att": [row(ao_ref, 2, slice(0, R))],
            "w_rg_a": [total(dwa_ref, h) for h in range(NB)], "w_rg_x": [total(dwx_ref, h) for h in range(NB)],
        }
        loss_ref[...] = row(ao_ref, 3, slice(0, LANES)) * (0.5 / D)
        for k, (name, shape) in enumerate(SMALL):
            w_ref, m_ref, v_ref = pin[3 * k:3 * k + 3]
            outs = pout[4 * k:4 * k + 4]
            for r, g in enumerate(grads[name]):
                at = (slice(None),) if len(grads[name]) == 1 else ((r,) if len(shape) == 3 else (slice(r, r + 1),))
                res = (g,) + _adamw_values(w_ref[at], g, m_ref[at], v_ref[at])
                for o_ref, val in zip(outs, res):
                    o_ref[at] = val
        for d in range(NDEV):
            dmod[d:d + 1, :] = jnp.concatenate([ai_ref[d, 0:1, :], ai_ref[d, 1:2, :], ao_ref[d, 0:1, :]], axis=1)
        cols = mine([dmod[:, j * EC:(j + 1) * EC] for j in range(NCHIP)])
        colsb = jnp.concatenate([cols, jnp.zeros_like(cols)], axis=0).astype(BF16)
        gada_ref[...] = _dot_tn(_silu_rows(cg_ref), colsb)

    shapes = [jax.ShapeDtypeStruct(s, F32) for _, s in SMALL]
    outs = pl.pallas_call(
        body, name="small_update",
        out_shape=[s for s in shapes for _ in range(4)] + [jax.ShapeDtypeStruct((D, EC), F32),
                                                           jax.ShapeDtypeStruct((1, LANES), F32)],
        scratch_shapes=[pltpu.VMEM((NDEV, 3 * D), F32)],
        compiler_params=_cp(),
    )(ao8, sm8, dwa8, dwx8, ai8, cg, *params)
    return outs[:4 * n], outs[4 * n], outs[4 * n + 1]


BIG = ("w_ada", "w_in", "w_out")
WEIGHTS = ("w_ada", "b_ada", "norm_pre", "norm_post", "w_in", "conv_w", "conv_b", "w_rg_a", "b_rg_a", "w_rg_x",
           "b_rg_x", "lru_lambda", "norm_rec", "norm_att", "w_out")


def kernel(x, c, positions, w_ada, b_ada, norm_pre, norm_post, w_in, conv_w, conv_b, w_rg_a, b_rg_a, w_rg_x, b_rg_x, lru_lambda, norm_rec, norm_att, w_out, loss_target, m_w_ada, m_b_ada, m_norm_pre, m_norm_post, m_w_in, m_conv_w, m_conv_b, m_w_rg_a, m_b_rg_a, m_w_rg_x, m_b_rg_x, m_lru_lambda, m_norm_rec, m_norm_att, m_w_out, v_w_ada, v_b_ada, v_norm_pre, v_norm_post, v_w_in, v_conv_w, v_conv_b, v_w_rg_a, v_b_rg_a, v_w_rg_x, v_b_rg_x, v_lru_lambda, v_norm_rec, v_norm_att, v_w_out):
    given = dict(locals())
    wts = {n: given[n] for n in WEIGHTS}
    ms = {n: given["m_" + n] for n in WEIGHTS}
    vs = {n: given["v_" + n] for n in WEIGHTS}
    xi, yi, _ = _me()
    chip = 2 * xi + yi
    cw_loc = R // NCHIP

    b_cols = lax.dynamic_slice(b_ada, (0, chip * EC), (1, EC))
    g0, mod, w_in_bf = _start_gather(jnp.concatenate([c, conv_w.reshape(1, 4 * cw_loc)], axis=1),
                                     w_ada[0], b_cols, w_in[0])
    cg = g0[:, 0:D]
    conv_full = g0[0::2, D:].reshape(NCHIP, 4, cw_loc).transpose(1, 0, 2).reshape(4, R)

    p = dict(norm_pre=norm_pre, norm_post=norm_post, conv_b=conv_b, b_rg_a=b_rg_a, b_rg_x=b_rg_x,
             lru_lambda=lru_lambda, norm_rec=norm_rec, norm_att=norm_att, w_rg_a=w_rg_a[0], w_rg_x=w_rg_x[0])
    grad_x, g_in, g_out, gathered = _local_step(
        x[0], positions.reshape(S, 1), loss_target[0], mod, w_in_bf, w_out[0], conv_full, p)

    params = [d[n].reshape(shape) for n, shape in SMALL for d in (wts, ms, vs)]
    small_out, g_ada, loss_row = _small_update(*gathered, cg, params)
    grads = {"w_out": g_out, "w_in": g_in, "w_ada": g_ada}
    delta, new_m, new_v = {}, {}, {}
    for k, (n, _) in enumerate(SMALL):
        grads[n], delta[n], new_m[n], new_v[n] = small_out[4 * k:4 * k + 4]
    for n in BIG:
        delta[n], new_m[n], new_v[n] = _adamw(wts[n][0], grads[n], ms[n][0], vs[n][0], "adamw_" + n)
    out = lambda d: [d[n].reshape(wts[n].shape) for n in WEIGHTS]
    return (loss_row[0, 0], grad_x.reshape(x.shape), *out(grads), *out(delta), *out(new_m), *out(new_v))
```

```python
import functools

import numpy as np
import jax
import jax.numpy as jnp
from jax import lax
from jax.experimental import pallas as pl
from jax.experimental.pallas import tpu as pltpu

F32 = jnp.float32
BF16 = jnp.bfloat16

S = 2048
D = 1024
E = 3072
R = 512
NDEV = 8
NCHIP = 4
EC = 768
LRU_C = 8.0
EPS = 1e-6
NEG = -1e30
HEAD = 64
BLK = 128
PATTERNS = (1, 4, 16)
ROPE_THETA = 10000.0
LANES = 128
VMEM_LIMIT = 56 * 1024 * 1024

B1, B2, LR, WD, ADAM_EPS, STEP = 0.9, 0.999, 0.001, 0.01, 1e-8, 10
MESH = pl.DeviceIdType.MESH


def _cp(sem=None, **kw):
    return pltpu.CompilerParams(dimension_semantics=sem, vmem_limit_bytes=VMEM_LIMIT, **kw)


def _dot(a, b):
    return jnp.dot(a, b, preferred_element_type=F32)


def _dot_nt(a, b):
    return lax.dot_general(a, b, (((1,), (1,)), ((), ())), preferred_element_type=F32)


def _dot_tn(a, b):
    return lax.dot_general(a, b, (((0,), (0,)), ((), ())), preferred_element_type=F32)


def _sigmoid(x):
    return 1.0 / (1.0 + jnp.exp(-x))


def _expm1(x):
    poly = x * (1.0 + x * (0.5 + x * (1.0 / 6 + x * (1.0 / 24 + x * (1.0 / 120 + x * (1.0 / 720))))))
    return jnp.where(jnp.abs(x) < 0.3, poly, jnp.exp(x) - 1.0)


def _rms_fwd(v, g):
    rstd = lax.rsqrt(jnp.mean(v * v, axis=-1, keepdims=True) + EPS)
    vn = v * rstd
    return vn * g, vn, rstd


def _rms_bwd(dy, vn, rstd, g):
    dvn = dy * g
    dv = rstd * (dvn - vn * jnp.mean(dvn * vn, axis=-1, keepdims=True))
    return dv, jnp.sum(dy * vn, axis=0, keepdims=True)


def _in_proj_fwd(x, mod, norm_pre, w_in_bf):
    ts = 256

    def body(x_ref, mod_ref, np_ref, w_ref, proj_ref, ht_ref):
        hp, _, _ = _rms_fwd(x_ref[...], np_ref[...])
        h = hp * (1.0 + mod_ref[:, D:2 * D]) + mod_ref[:, 0:D]
        hb = h.astype(BF16)
        ht_ref[...] = h.T.astype(BF16)
        for j in range(NCHIP):
            proj_ref[:, j * EC:(j + 1) * EC] = _dot(hb, w_ref[j])

    return pl.pallas_call(
        body, name="in_proj_fwd", grid=(S // ts,),
        in_specs=[pl.BlockSpec((ts, D), lambda i: (i, 0)), pl.BlockSpec((1, 3 * D), lambda i: (0, 0)),
                  pl.BlockSpec((1, D), lambda i: (0, 0)), pl.BlockSpec((NCHIP, D, EC), lambda i: (0, 0, 0))],
        out_specs=[pl.BlockSpec((ts, E), lambda i: (i, 0)), pl.BlockSpec((D, ts), lambda i: (0, i))],
        out_shape=[jax.ShapeDtypeStruct((S, E), F32), jax.ShapeDtypeStruct((D, S), BF16)],
        compiler_params=_cp(("parallel",)),
    )(x, mod, norm_pre, w_in_bf)


RT = 256


def _shift_down(cur, prev8, j, row):
    if j == 0:
        return cur
    top = jnp.tile(pltpu.roll(prev8, j, 0), (RT // 8, 1))
    return jnp.where(row >= j, pltpu.roll(cur, j, 0), top)


def _shift_up(cur, next8, j, row):
    if j == 0:
        return cur
    bot = jnp.tile(pltpu.roll(next8, 8 - j, 0), (RT // 8, 1))
    return jnp.where(row < RT - j, pltpu.roll(cur, RT - j, 0), bot)


def _rec_gates(xp, xprev8, row, cw_ref, cb_ref, wa_ref, ba_ref, wx_ref, bx_ref, lam_ref):
    xa = cb_ref[...] + sum(cw_ref[3 - j:4 - j, :] * _shift_down(xp, xprev8, j, row) for j in range(4))
    xab = xa.astype(BF16)
    r = _sigmoid(_dot(xab, wa_ref[...]) + ba_ref[...])
    ig = _sigmoid(_dot(xab, wx_ref[...]) + bx_ref[...])
    nl = -lam_ref[...]
    sp = jnp.maximum(nl, 0.0) + jnp.log1p(jnp.exp(-jnp.abs(nl)))
    la = (-LRU_C) * r * sp
    a = jnp.exp(la)
    mult = jnp.sqrt(-_expm1(2.0 * la))
    return dict(xa=xa, xab=xab, r=r, ig=ig, sp=sp, la=la, a=a, mult=mult)


def _scan_fwd(a, u, row):
    sh = 1
    while sh < RT:
        a_s = jnp.where(row >= sh, pltpu.roll(a, sh, 0), 1.0)
        u_s = jnp.where(row >= sh, pltpu.roll(u, sh, 0), 0.0)
        u = a * u_s + u
        a = a * a_s
        sh *= 2
    return a, u


def _scan_bwd(al, g, row):
    sh = 1
    while sh < RT:
        al_s = jnp.where(row < RT - sh, pltpu.roll(al, RT - sh, 0), 1.0)
        g_s = jnp.where(row < RT - sh, pltpu.roll(g, RT - sh, 0), 0.0)
        g = g + al * g_s
        al = al * al_s
        sh *= 2
    return g


def _dense_from_blocks(blocks_ref, dense_ref):
    dense_ref[...] = jnp.zeros_like(dense_ref)
    for h in range(R // HEAD):
        dense_ref[h * HEAD:(h + 1) * HEAD, h * HEAD:(h + 1) * HEAD] = blocks_ref[h].astype(dense_ref.dtype)


def _rec_fwd(proj, conv_w, conv_b, wa_b, ba, wx_b, bx, lam, norm_rec):
    nt = S // RT

    def body(p_ref, cw_ref, cb_ref, wa_ref, ba_ref, wx_ref, bx_ref, lam_ref, nr_ref,
             h_ref, ya_ref, prev8, hc, wad, wxd):
        i = pl.program_id(0)

        @pl.when(i == 0)
        def _():
            prev8[...] = jnp.zeros_like(prev8)
            hc[...] = jnp.zeros_like(hc)
            _dense_from_blocks(wa_ref, wad)
            _dense_from_blocks(wx_ref, wxd)

        row = lax.broadcasted_iota(jnp.int32, (RT, R), 0)
        xp = p_ref[:, 0:R]
        ga = p_ref[:, R:2 * R]
        f = _rec_gates(xp, prev8[...], row, cw_ref, cb_ref, wad, ba_ref, wxd, bx_ref, lam_ref)
        u = f["mult"] * (f["ig"] * f["xa"])
        acum, hh = _scan_fwd(f["a"], u, row)
        h = hh + acum * hc[0:1, :]
        h_ref[...] = h
        hc[0:1, :] = h_ref[RT - 1:RT, :]
        prev8[...] = p_ref[RT - 8:RT, 0:R]
        yp = h * (ga * _sigmoid(ga))
        ya, _, _ = _rms_fwd(yp, nr_ref[...])
        ya_ref[...] = ya.astype(BF16)

    row1 = lambda n: pl.BlockSpec((1, n), lambda i: (0, 0))
    blocks = pl.BlockSpec((R // HEAD, HEAD, HEAD), lambda i: (0, 0, 0))
    return pl.pallas_call(
        body, name="rec_fwd", grid=(nt,),
        in_specs=[pl.BlockSpec((RT, 2 * R), lambda i: (i, 0)), pl.BlockSpec((4, R), lambda i: (0, 0)), row1(R),
                  blocks, row1(R), blocks, row1(R), row1(R), row1(R)],
        out_specs=[pl.BlockSpec((RT, R), lambda i: (i, 0)), pl.BlockSpec((RT, R), lambda i: (i, 0))],
        out_shape=[jax.ShapeDtypeStruct((S, R), F32), jax.ShapeDtypeStruct((S, R), BF16)],
        scratch_shapes=[pltpu.VMEM((8, R), F32), pltpu.VMEM((8, R), F32), pltpu.VMEM((R, R), BF16),
                        pltpu.VMEM((R, R), BF16)],
        compiler_params=_cp(("arbitrary",)),
    )(proj, conv_w, conv_b, wa_b, ba, wx_b, bx, lam, norm_rec)


def _rec_bwd(dproj, d_ya, proj, h_all, conv_w, conv_b, wa_b, ba, wx_b, bx, lam, norm_rec):
    nt = S // RT

    def body(dp_in, dya_ref, p_ref, pprev_ref, h_ref, hprev_ref, cw_ref, cb_ref, wab_ref, ba_ref, wxb_ref, bx_ref,
             lam_ref, nr_ref, dp_ref, dwab_ref, dwxb_ref, sm_ref, nxt8, cg, wa_ref, wx_ref, dwa_ref, dwx_ref):
        i = pl.program_id(0)
        ti = nt - 1 - i

        @pl.when(i == 0)
        def _():
            nxt8[...] = jnp.zeros_like(nxt8)
            cg[...] = jnp.zeros_like(cg)
            dwa_ref[...] = jnp.zeros_like(dwa_ref)
            dwx_ref[...] = jnp.zeros_like(dwx_ref)
            sm_ref[...] = jnp.zeros_like(sm_ref)
            _dense_from_blocks(wab_ref, wa_ref)
            _dense_from_blocks(wxb_ref, wx_ref)

        row = lax.broadcasted_iota(jnp.int32, (RT, R), 0)
        first = (ti > 0).astype(F32)
        xprev8 = pprev_ref[...] * first
        hprev8 = hprev_ref[...] * first
        xp = p_ref[:, 0:R]
        ga = p_ref[:, R:2 * R]
        f = _rec_gates(xp, xprev8, row, cw_ref, cb_ref, wa_ref, ba_ref, wx_ref, bx_ref, lam_ref)
        xa, r, ig, a, mult = f["xa"], f["r"], f["ig"], f["a"], f["mult"]
        h = h_ref[...]
        sg = _sigmoid(ga)
        gate = ga * sg
        yp = h * gate
        _, ypn, rstd = _rms_fwd(yp, nr_ref[...])
        d_yp, dnr = _rms_bwd(dya_ref[...], ypn, rstd, nr_ref[...])
        d_ga = d_yp * h * (sg * (1.0 + ga * (1.0 - sg)))
        dh = d_yp * gate + jnp.where(row == RT - 1, cg[0:1, :], 0.0)
        al = jnp.where(row < RT - 1, pltpu.roll(a, RT - 1, 0), 0.0)
        g = _scan_bwd(al, dh, row)
        cg[0:1, :] = jnp.sum(jnp.where(row == 0, a * g, 0.0), axis=0, keepdims=True)
        h_m1 = _shift_down(h, hprev8, 1, row)
        da = g * h_m1
        ix = ig * xa
        d_mult = g * ix
        d_ig = g * mult * xa
        d_xa = g * mult * ig
        d_la = da * a - d_mult * (a * a) / mult
        d_r = d_la * ((-LRU_C) * f["sp"])
        dsp = jnp.sum(d_la * ((-LRU_C) * r), axis=0, keepdims=True)
        dlam = dsp * (-_sigmoid(-lam_ref[...]))
        d_za = d_r * r * (1.0 - r)
        d_zx = d_ig * ig * (1.0 - ig)
        dzab = d_za.astype(BF16)
        dzxb = d_zx.astype(BF16)
        dwa_ref[...] += _dot_tn(f["xab"], dzab)
        dwx_ref[...] += _dot_tn(f["xab"], dzxb)
        d_xa = d_xa + _dot_nt(dzab, wa_ref[...]) + _dot_nt(dzxb, wx_ref[...])
        d_xp = sum(cw_ref[3 - j:4 - j, :] * _shift_up(d_xa, nxt8[...], j, row) for j in range(4))
        dcw = [jnp.sum(d_xa * _shift_down(xp, xprev8, 3 - k, row), axis=0, keepdims=True) for k in range(4)]
        dp_ref[:, 0:R] = d_xp.astype(BF16)
        dp_ref[:, R:2 * R] = d_ga.astype(BF16)
        dp8 = d_xa[0:8, :]
        nxt8[...] = dp8
        sm_ref[0:1, :] += jnp.sum(d_za, axis=0, keepdims=True)
        sm_ref[1:2, :] += jnp.sum(d_zx, axis=0, keepdims=True)
        sm_ref[2:3, :] += dlam
        sm_ref[3:4, :] += dnr
        sm_ref[4:5, :] += jnp.sum(d_xa, axis=0, keepdims=True)
        for k in range(4):
            sm_ref[8 + k:9 + k, :] += dcw[k]

        @pl.when(i == nt - 1)
        def _():
            for h in range(R // HEAD):
                dwab_ref[h] = dwa_ref[h * HEAD:(h + 1) * HEAD, h * HEAD:(h + 1) * HEAD]
                dwxb_ref[h] = dwx_ref[h * HEAD:(h + 1) * HEAD, h * HEAD:(h + 1) * HEAD]

    c0 = lambda shape: pl.BlockSpec(shape, lambda i: (0, 0))
    blocks = pl.BlockSpec((R // HEAD, HEAD, HEAD), lambda i: (0, 0, 0))
    rev = lambda i: nt - 1 - i
    prev8 = lambda i: (jnp.maximum((nt - 1 - i) * (RT // 8) - 1, 0), 0)
    return pl.pallas_call(
        body, name="rec_bwd", grid=(nt,),
        in_specs=[pl.BlockSpec(memory_space=pl.ANY),
                  pl.BlockSpec((RT, R), lambda i: (rev(i), 0)),
                  pl.BlockSpec((RT, 2 * R), lambda i: (rev(i), 0)), pl.BlockSpec((8, R), prev8),
                  pl.BlockSpec((RT, R), lambda i: (rev(i), 0)), pl.BlockSpec((8, R), prev8),
                  c0((4, R)), c0((1, R)), blocks, c0((1, R)), blocks, c0((1, R)), c0((1, R)), c0((1, R))],
        out_specs=[pl.BlockSpec((RT, 2 * R), lambda i: (rev(i), 0)), blocks, blocks, c0((16, R))],
        out_shape=[jax.ShapeDtypeStruct((S, E), BF16), jax.ShapeDtypeStruct((R // HEAD, HEAD, HEAD), F32),
                   jax.ShapeDtypeStruct((R // HEAD, HEAD, HEAD), F32), jax.ShapeDtypeStruct((16, R), F32)],
        scratch_shapes=[pltpu.VMEM((8, R), F32), pltpu.VMEM((8, R), F32), pltpu.VMEM((R, R), BF16),
                        pltpu.VMEM((R, R), BF16), pltpu.VMEM((R, R), F32), pltpu.VMEM((R, R), F32)],
        input_output_aliases={0: 0},
        compiler_params=_cp(("arbitrary",)),
    )(dproj, d_ya, proj, proj, h_all, h_all, conv_w, conv_b, wa_b, ba, wx_b, bx, lam, norm_rec)


NPAIR = R // LANES
QB, KB, VB, GB = 2 * R // LANES, 3 * R // LANES, 4 * R // LANES, 5 * R // LANES


def _rope_freq():
    half = HEAD // 2
    inv = np.float32(ROPE_THETA) ** (-(np.arange(half, dtype=np.float32) / np.float32(half)))
    return jnp.asarray(np.tile(inv.astype(np.float32), LANES // half)[None, :])


def _rot_half(x, first):
    return jnp.where(first, -pltpu.roll(x, LANES - HEAD // 2, 1), pltpu.roll(x, HEAD // 2, 1))


def _cos_sin(pos_ref, freq_ref):
    ang = pos_ref[...].astype(F32) * freq_ref[...]
    return jnp.cos(ang), jnp.sin(ang)


def _deint(src_ref, dst_ref, d):
    n = S // d
    for r in range(d):
        v = src_ref[pl.ds(r, n, stride=d), :] if d > 1 else src_ref[...]
        dst_ref[r * n:(r + 1) * n, :] = v.astype(dst_ref.dtype)


def _reint(src_ref, dst_ref, d, accumulate):
    n = S // d
    for r in range(d):
        idx = (pl.ds(r, n, stride=d), slice(None)) if d > 1 else (slice(None), slice(None))
        v = src_ref[r * n:(r + 1) * n, :]
        if accumulate:
            dst_ref[idx] = dst_ref[idx] + v
        else:
            dst_ref[idx] = v


def _blk_masks(b, nb):
    qi = lax.broadcasted_iota(jnp.int32, (BLK, BLK), 0)
    ki = lax.broadcasted_iota(jnp.int32, (BLK, BLK), 1)
    has_prev = lax.rem(b, nb) != 0
    return ki <= qi, jnp.logical_and(ki >= qi, has_prev)


def _rope_table(pos, freq):
    def body(pos_ref, freq_ref, cos_ref, sin_ref):
        cos_ref[...], sin_ref[...] = _cos_sin(pos_ref, freq_ref)

    return pl.pallas_call(body, name="rope_table", out_shape=[jax.ShapeDtypeStruct((S, LANES), F32)] * 2,
                          compiler_params=_cp())(pos, freq)


def _deint_heads(src_ref, dst0, dst1, d):
    n = S // d
    hm0 = lax.broadcasted_iota(jnp.int32, (n, LANES), 1) < HEAD
    for r in range(d):
        v = src_ref[pl.ds(r, n, stride=d), :] if d > 1 else src_ref[...]
        dst0[r * n:(r + 1) * n, :] = jnp.where(hm0, v, 0.0).astype(BF16)
        dst1[r * n:(r + 1) * n, :] = jnp.where(hm0, 0.0, v).astype(BF16)


def _reint_prev(src_ref, dst_ref, d):
    n = S // d
    if n == BLK:
        return
    for r in range(d):
        idx = (pl.ds(r, n - BLK, stride=d), slice(None)) if d > 1 else (slice(0, n - BLK), slice(None))
        dst_ref[idx] = dst_ref[idx] + src_ref[r * n + BLK:(r + 1) * n, :]


def _pair_masks():
    qi = lax.broadcasted_iota(jnp.int32, (BLK, 2 * BLK), 0)
    ki = lax.broadcasted_iota(jnp.int32, (BLK, 2 * BLK), 1) & (BLK - 1)
    return ki <= qi, ki >= qi


def _two(ref0, ref1, st, axis):
    return jnp.concatenate([ref0[pl.ds(st, BLK), :], ref1[pl.ds(st, BLK), :]], axis=axis)


ATT_UNROLL = 4


def _att_fwd(proj, cos, sin, w_out):
    def body(q_ref, k_ref, v_ref, cos_ref, sin_ref, w_ref, att_ref, qr_ref, kr_ref, lse_ref, wbf_ref,
             qd, kd0, kd1, vd0, vd1, od, ld, on, ln, wbuf, *wsems):
        wg = _WeightGather(w_ref, wbuf, *wsems)
        pl.when(pl.program_id(0) == 0)(wg.start)
        pl.when(pl.program_id(0) == 1)(wg.forward)
        lane = lax.broadcasted_iota(jnp.int32, (S, LANES), 1)
        first = (lane & (HEAD // 2)) == 0
        cos, sin = cos_ref[...], sin_ref[...]
        q = q_ref[...]
        k = k_ref[...]
        qr_ref[...] = (q * cos + _rot_half(q, first) * sin) * (HEAD ** -0.5)
        kr_ref[...] = k * cos + _rot_half(k, first) * sin
        hm0 = lax.broadcasted_iota(jnp.int32, (BLK, LANES), 1) < HEAD
        top = lax.broadcasted_iota(jnp.int32, (2 * BLK, LANES), 0) < BLK
        ones2 = (top == (lax.broadcasted_iota(jnp.int32, (2 * BLK, LANES), 1) < HEAD)).astype(BF16)
        mc2, mp2 = _pair_masks()

        for pi, d in enumerate(PATTERNS):
            nb = S // d // BLK
            _deint(qr_ref, qd, d)
            _deint_heads(kr_ref, kd0, kd1, d)
            _deint_heads(v_ref, vd0, vd1, d)

            def blk(b, carry):
                st = pl.multiple_of(b * BLK, BLK)
                qb = qd[pl.ds(st, BLK), :]
                sc = jnp.where(mc2, _dot_nt(qb, _two(kd0, kd1, st, 0)), NEG)
                mx = sc
                if nb > 1:
                    stp = pl.multiple_of(jnp.maximum(b - 1, 0) * BLK, BLK)
                    mp = jnp.logical_and(mp2, lax.rem(b, nb) != 0)
                    sp = jnp.where(mp, _dot_nt(qb, _two(kd0, kd1, stp, 0)), NEG)
                    mx = jnp.maximum(sc, sp)
                m0 = jnp.max(mx[:, 0:BLK], axis=1, keepdims=True)
                m1 = jnp.max(mx[:, BLK:2 * BLK], axis=1, keepdims=True)
                mf = jnp.concatenate([jnp.broadcast_to(m0, (BLK, BLK)), jnp.broadcast_to(m1, (BLK, BLK))], axis=1)
                o = _dot(jnp.exp(sc - mf).astype(BF16), jnp.concatenate([_two(vd0, vd1, st, 0), ones2], axis=1))
                if nb > 1:
                    o = o + _dot(jnp.exp(sp - mf).astype(BF16), jnp.concatenate([_two(vd0, vd1, stp, 0), ones2], axis=1))
                l = o[:, LANES:2 * LANES]
                od[pl.ds(st, BLK), :] = o[:, 0:LANES] / l
                ld[pl.ds(st, BLK), :] = jnp.where(hm0, m0, m1) + jnp.log(l)
                return carry

            lax.fori_loop(0, S // BLK, blk, 0, unroll=ATT_UNROLL)
            _reint(od, on.at[pi], d, False)
            _reint(ld, ln.at[pi], d, False)

        l0, l1, l2 = ln[0], ln[1], ln[2]
        m = jnp.maximum(jnp.maximum(l0, l1), l2)
        e0, e1, e2 = jnp.exp(l0 - m), jnp.exp(l1 - m), jnp.exp(l2 - m)
        den = e0 + e1 + e2
        att_ref[...] = (e0 * on[0] + e1 * on[1] + e2 * on[2]) / den
        lse_ref[...] = m + jnp.log(den)

        @pl.when(pl.program_id(0) == NPAIR - 1)
        def _():
            wg.finish()
            wbf_ref[...] = wbuf[...]

    col = lambda c0: pl.BlockSpec((S, LANES), lambda p: (0, c0 + p))
    out = pl.BlockSpec((S, LANES), lambda p: (0, p))
    tab = pl.BlockSpec((S, LANES), lambda p: (0, 0))
    vm = pl.BlockSpec(memory_space=pltpu.VMEM)
    return pl.pallas_call(
        body, name="att_fwd", grid=(NPAIR,),
        in_specs=[col(QB), col(KB), col(VB), tab, tab, vm],
        out_specs=[out, out, out, out, vm],
        out_shape=[jax.ShapeDtypeStruct((S, R), F32)] * 4 + [jax.ShapeDtypeStruct((NCHIP,) + w_out.shape, BF16)],
        scratch_shapes=[pltpu.VMEM((S, LANES), BF16)] * 5 + [pltpu.VMEM((S, LANES), F32)] * 2
        + [pltpu.VMEM((3, S, LANES), F32)] * 2 + [pltpu.VMEM((NCHIP,) + w_out.shape, BF16)] + _WeightGather.SEMS,
        compiler_params=_cp(("arbitrary",)),
    )(proj, proj, proj, cos, sin, w_out)


def _att_bwd(dproj, d_att, att, lse, qr, kr, proj, cos, sin, gw_out4):
    out_units = [(j, j, 0) for j in range(NCHIP)]

    def body(dp_in, do_ref, o_ref, lse_ref, qr_ref, kr_ref, v_ref, cos_ref, sin_ref, gw_ref, dp_ref, gout_ref,
             qd, kd0, kd1, vd0, vd1, dod, lb0d, lb1d, dl0d, dl1d, dqd, dkcd, dkpd, dvcd, dvpd,
             dqn, dkn, dvn, lb0n, lb1n, dl0n, dl1n, stage, sems, gred, *rs_scratch):
        p = pl.program_id(0)
        rs = _ReduceScatter(gw_ref, gred, out_units, *rs_scratch)
        for step, piece in enumerate((rs.start_halves, rs.send_partials, rs.reduce_owned)):
            pl.when(p == step)(piece)

        @pl.when(p == NPAIR - 1)
        def _():
            rs.finish()
            gout_ref[...] = gred[...]
        hms = lax.broadcasted_iota(jnp.int32, (S, LANES), 1) < HEAD
        prod = do_ref[...] * o_ref[...]
        dl0n[...] = jnp.broadcast_to(jnp.sum(jnp.where(hms, prod, 0.0), axis=1, keepdims=True), (S, LANES))
        dl1n[...] = jnp.broadcast_to(jnp.sum(jnp.where(hms, 0.0, prod), axis=1, keepdims=True), (S, LANES))
        lse = lse_ref[...]
        lsw = pltpu.roll(lse, HEAD, 1)
        lb0n[...] = jnp.where(hms, lse, lsw)
        lb1n[...] = jnp.where(hms, lsw, lse)
        dqn[...] = jnp.zeros_like(dqn)
        dkn[...] = jnp.zeros_like(dkn)
        dvn[...] = jnp.zeros_like(dvn)
        hm0 = lax.broadcasted_iota(jnp.int32, (BLK, LANES), 1) < HEAD
        mc2, mp2 = _pair_masks()

        for d in PATTERNS:
            nb = S // d // BLK
            _deint(qr_ref, qd, d)
            _deint_heads(kr_ref, kd0, kd1, d)
            _deint_heads(v_ref, vd0, vd1, d)
            _deint(do_ref, dod, d)
            for src, dst in ((lb0n, lb0d), (lb1n, lb1d), (dl0n, dl0d), (dl1n, dl1d)):
                _deint(src, dst, d)

            def blk(b, carry):
                st = pl.multiple_of(b * BLK, BLK)
                qb, dob = qd[pl.ds(st, BLK), :], dod[pl.ds(st, BLK), :]
                lb, dl = _two(lb0d, lb1d, st, 1), _two(dl0d, dl1d, st, 1)

                def side(stk, mask):
                    k2, v2 = _two(kd0, kd1, stk, 0), _two(vd0, vd1, stk, 0)
                    pk = jnp.where(mask, jnp.exp(_dot_nt(qb, k2) - lb), 0.0)
                    ds = (pk * (_dot_nt(dob, v2) - dl)).astype(BF16)
                    rk, rv = _dot_tn(ds, qb), _dot_tn(pk.astype(BF16), dob)
                    return (_dot(ds, k2), jnp.where(hm0, rk[0:BLK], rk[BLK:2 * BLK]),
                            jnp.where(hm0, rv[0:BLK], rv[BLK:2 * BLK]))

                dq, dkc, dvc = side(st, mc2)
                if nb > 1:
                    stp = pl.multiple_of(jnp.maximum(b - 1, 0) * BLK, BLK)
                    dqp, dkp, dvp = side(stp, jnp.logical_and(mp2, lax.rem(b, nb) != 0))
                    dq = dq + dqp
                    dkpd[pl.ds(st, BLK), :] = dkp
                    dvpd[pl.ds(st, BLK), :] = dvp
                dqd[pl.ds(st, BLK), :] = dq
                dkcd[pl.ds(st, BLK), :] = dkc
                dvcd[pl.ds(st, BLK), :] = dvc
                return carry

            lax.fori_loop(0, S // BLK, blk, 0, unroll=ATT_UNROLL)
            _reint(dqd, dqn, d, True)
            _reint(dkcd, dkn, d, True)
            _reint(dvcd, dvn, d, True)
            _reint_prev(dkpd, dkn, d)
            _reint_prev(dvpd, dvn, d)

        lane = lax.broadcasted_iota(jnp.int32, (S, LANES), 1)
        first = (lane & (HEAD // 2)) == 0
        cos, sin = cos_ref[...], sin_ref[...]
        dq = dqn[...] * (HEAD ** -0.5)
        dk = dkn[...]
        stage[0] = (dq * cos - _rot_half(dq, first) * sin).astype(BF16)
        stage[1] = (dk * cos - _rot_half(dk, first) * sin).astype(BF16)
        stage[2] = dvn[...].astype(BF16)
        copies = [pltpu.make_async_copy(stage.at[j], dp_ref.at[:, pl.ds((2 + j) * R + p * LANES, LANES)], sems.at[j])
                  for j in range(3)]
        for cp in copies:
            cp.start()
        for cp in copies:
            cp.wait()

    blk = pl.BlockSpec((S, LANES), lambda p: (0, p))
    tab = pl.BlockSpec((S, LANES), lambda p: (0, 0))
    vm = pl.BlockSpec(memory_space=pltpu.VMEM)
    _, orows, ocols = gw_out4.shape
    return pl.pallas_call(
        body, name="att_bwd", grid=(NPAIR,),
        in_specs=[pl.BlockSpec(memory_space=pl.ANY), blk, blk, blk, blk, blk,
                  pl.BlockSpec((S, LANES), lambda p: (0, VB + p)), tab, tab, vm],
        out_specs=[pl.BlockSpec(memory_space=pl.ANY), vm],
        out_shape=[jax.ShapeDtypeStruct((S, E), BF16), jax.ShapeDtypeStruct((orows, ocols), F32)],
        scratch_shapes=[pltpu.VMEM((S, LANES), BF16)] * 6 + [pltpu.VMEM((S, LANES), F32)] * 16
        + [pltpu.VMEM((3, S, LANES), BF16), pltpu.SemaphoreType.DMA((3,)), pltpu.VMEM((orows, ocols), F32)]
        + _ReduceScatter.scratch(NCHIP, orows, ocols, 1),
        input_output_aliases={0: 0},
        compiler_params=_cp(("arbitrary",)),
    )(dproj, d_att, att, lse, qr, kr, proj, cos, sin, gw_out4)


def _att_fwd_old(proj, pos, freq):
    def body(q_ref, k_ref, v_ref, pos_ref, freq_ref, att_ref, qr_ref, kr_ref, lse_ref,
             qd, kd, vd, od, ld, on, ln):
        lane = lax.broadcasted_iota(jnp.int32, (S, LANES), 1)
        first = (lane & (HEAD // 2)) == 0
        cos, sin = _cos_sin(pos_ref, freq_ref)
        q = q_ref[...]
        k = k_ref[...]
        qr_ref[...] = (q * cos + _rot_half(q, first) * sin) * (HEAD ** -0.5)
        kr_ref[...] = k * cos + _rot_half(k, first) * sin
        hm0 = lax.broadcasted_iota(jnp.int32, (BLK, LANES), 1) < HEAD

        for pi, d in enumerate(PATTERNS):
            nb = S // d // BLK
            _deint(qr_ref, qd, d)
            _deint(kr_ref, kd, d)
            _deint(v_ref, vd, d)

            def blk(b, carry):
                st = pl.multiple_of(b * BLK, BLK)
                stp = pl.multiple_of(jnp.maximum(b - 1, 0) * BLK, BLK)
                mc, mp = _blk_masks(b, nb)
                qb = qd[pl.ds(st, BLK), :]
                kc, kp = kd[pl.ds(st, BLK), :], kd[pl.ds(stp, BLK), :]
                vc, vp = vd[pl.ds(st, BLK), :], vd[pl.ds(stp, BLK), :]
                outs, lses = [], []
                for hm in (hm0, jnp.logical_not(hm0)):
                    qm = jnp.where(hm, qb, jnp.zeros_like(qb))
                    sc = jnp.where(mc, _dot_nt(qm, kc), NEG)
                    sp = jnp.where(mp, _dot_nt(qm, kp), NEG)
                    m = jnp.maximum(jnp.max(sc, axis=1, keepdims=True), jnp.max(sp, axis=1, keepdims=True))
                    pc, pp = jnp.exp(sc - m), jnp.exp(sp - m)
                    l = jnp.sum(pc, axis=1, keepdims=True) + jnp.sum(pp, axis=1, keepdims=True)
                    o = _dot(pc.astype(BF16), vc) + _dot(pp.astype(BF16), vp)
                    outs.append(o / l)
                    lses.append(m + jnp.log(l))
                od[pl.ds(st, BLK), :] = jnp.where(hm0, outs[0], outs[1])
                ld[pl.ds(st, BLK), :] = jnp.where(hm0, lses[0], lses[1])
                return carry

            lax.fori_loop(0, S // BLK, blk, 0)
            _reint(od, on.at[pi], d, False)
            _reint(ld, ln.at[pi], d, False)

        l0, l1, l2 = ln[0], ln[1], ln[2]
        m = jnp.maximum(jnp.maximum(l0, l1), l2)
        e0, e1, e2 = jnp.exp(l0 - m), jnp.exp(l1 - m), jnp.exp(l2 - m)
        den = e0 + e1 + e2
        att_ref[...] = (e0 * on[0] + e1 * on[1] + e2 * on[2]) / den
        lse_ref[...] = m + jnp.log(den)

    col = lambda c0: pl.BlockSpec((S, LANES), lambda p: (0, c0 + p))
    out = pl.BlockSpec((S, LANES), lambda p: (0, p))
    return pl.pallas_call(
        body, name="att_fwd", grid=(NPAIR,),
        in_specs=[col(QB), col(KB), col(VB), pl.BlockSpec((S, 1), lambda p: (0, 0)),
                  pl.BlockSpec((1, LANES), lambda p: (0, 0))],
        out_specs=[out, out, out, out],
        out_shape=[jax.ShapeDtypeStruct((S, R), F32)] * 4,
        scratch_shapes=[pltpu.VMEM((S, LANES), BF16)] * 3 + [pltpu.VMEM((S, LANES), F32)] * 2
        + [pltpu.VMEM((3, S, LANES), F32)] * 2,
        compiler_params=_cp(("parallel",)),
    )(proj, proj, proj, pos, freq)


def _att_bwd_old(dproj, d_att, att, lse, qr, kr, proj, pos, freq):
    def body(dp_in, do_ref, o_ref, lse_ref, qr_ref, kr_ref, v_ref, pos_ref, freq_ref, dp_ref,
             qd, kd, vd, dod, lsd, prd, dqd, dkd, dvd, dqn, dkn, dvn, prn, stage, sems):
        p = pl.program_id(0)
        prn[...] = do_ref[...] * o_ref[...]
        dqn[...] = jnp.zeros_like(dqn)
        dkn[...] = jnp.zeros_like(dkn)
        dvn[...] = jnp.zeros_like(dvn)
        hm0 = lax.broadcasted_iota(jnp.int32, (BLK, LANES), 1) < HEAD

        for d in PATTERNS:
            nb = S // d // BLK
            _deint(qr_ref, qd, d)
            _deint(kr_ref, kd, d)
            _deint(v_ref, vd, d)
            _deint(do_ref, dod, d)
            _deint(lse_ref, lsd, d)
            _deint(prn, prd, d)
            dkd[...] = jnp.zeros_like(dkd)
            dvd[...] = jnp.zeros_like(dvd)

            def blk(b, carry):
                st = pl.multiple_of(b * BLK, BLK)
                stp = pl.multiple_of(jnp.maximum(b - 1, 0) * BLK, BLK)
                mc, mp = _blk_masks(b, nb)
                qb, dob = qd[pl.ds(st, BLK), :], dod[pl.ds(st, BLK), :]
                kc, kp = kd[pl.ds(st, BLK), :], kd[pl.ds(stp, BLK), :]
                vc, vp = vd[pl.ds(st, BLK), :], vd[pl.ds(stp, BLK), :]
                lsb, prb = lsd[pl.ds(st, BLK), :], prd[pl.ds(st, BLK), :]
                dqs = []
                dkc = dkp = dvc = dvp = None
                for hm in (hm0, jnp.logical_not(hm0)):
                    qm = jnp.where(hm, qb, jnp.zeros_like(qb))
                    dom = jnp.where(hm, dob, jnp.zeros_like(dob))
                    lh = jnp.max(jnp.where(hm, lsb, -3e38), axis=1, keepdims=True)
                    delta = jnp.sum(jnp.where(hm, prb, 0.0), axis=1, keepdims=True)
                    pc = jnp.where(mc, jnp.exp(_dot_nt(qm, kc) - lh), 0.0)
                    pp = jnp.where(mp, jnp.exp(_dot_nt(qm, kp) - lh), 0.0)
                    dsc = (pc * (_dot_nt(dom, vc) - delta)).astype(BF16)
                    dsp = (pp * (_dot_nt(dom, vp) - delta)).astype(BF16)
                    dqs.append(_dot(dsc, kc) + _dot(dsp, kp))
                    acc = lambda t, n: n if t is None else t + n
                    dkc, dkp = acc(dkc, _dot_tn(dsc, qm)), acc(dkp, _dot_tn(dsp, qm))
                    dvc, dvp = acc(dvc, _dot_tn(pc.astype(BF16), dom)), acc(dvp, _dot_tn(pp.astype(BF16), dom))
                dqd[pl.ds(st, BLK), :] = jnp.where(hm0, dqs[0], dqs[1])
                dkd[pl.ds(stp, BLK), :] += dkp
                dvd[pl.ds(stp, BLK), :] += dvp
                dkd[pl.ds(st, BLK), :] += dkc
                dvd[pl.ds(st, BLK), :] += dvc
                return carry

            lax.fori_loop(0, S // BLK, blk, 0)
            _reint(dqd, dqn, d, True)
            _reint(dkd, dkn, d, True)
            _reint(dvd, dvn, d, True)

        lane = lax.broadcasted_iota(jnp.int32, (S, LANES), 1)
        first = (lane & (HEAD // 2)) == 0
        cos, sin = _cos_sin(pos_ref, freq_ref)
        dq = dqn[...] * (HEAD ** -0.5)
        dk = dkn[...]
        stage[0] = (dq * cos - _rot_half(dq, first) * sin).astype(BF16)
        stage[1] = (dk * cos - _rot_half(dk, first) * sin).astype(BF16)
        stage[2] = dvn[...].astype(BF16)
        copies = [pltpu.make_async_copy(stage.at[j], dp_ref.at[:, pl.ds((2 + j) * R + p * LANES, LANES)], sems.at[j])
                  for j in range(3)]
        for cp in copies:
            cp.start()
        for cp in copies:
            cp.wait()

    blk = pl.BlockSpec((S, LANES), lambda p: (0, p))
    return pl.pallas_call(
        body, name="att_bwd", grid=(NPAIR,),
        in_specs=[pl.BlockSpec(memory_space=pl.ANY), blk, blk, blk, blk, blk,
                  pl.BlockSpec((S, LANES), lambda p: (0, VB + p)), pl.BlockSpec((S, 1), lambda p: (0, 0)),
                  pl.BlockSpec((1, LANES), lambda p: (0, 0))],
        out_specs=pl.BlockSpec(memory_space=pl.ANY),
        out_shape=jax.ShapeDtypeStruct((S, E), BF16),
        scratch_shapes=[pltpu.VMEM((S, LANES), BF16)] * 4 + [pltpu.VMEM((S, LANES), F32)] * 9
        + [pltpu.VMEM((3, S, LANES), BF16), pltpu.SemaphoreType.DMA((3,))],
        input_output_aliases={0: 0},
        compiler_params=_cp(("arbitrary",)),
    )(dproj, d_att, att, lse, qr, kr, proj, pos, freq)


def _out_fwd_bwd(ya, att, proj, w_out_bf, x, target, mod, norm_post, norm_att):
    ts = 256

    def body(ya_ref, att_ref, gb_ref, w_ref, x_ref, t_ref, mod_ref, npost_ref, natt_ref,
             gx_ref, dya_ref, datt_ref, dgb_ref, gw_ref, acc_ref):
        i = pl.program_id(0)

        @pl.when(i == 0)
        def _():
            gw_ref[...] = jnp.zeros_like(gw_ref)
            acc_ref[...] = jnp.zeros_like(acc_ref)

        gate = mod_ref[:, 2 * D:3 * D]
        att = att_ref[...]
        gb = gb_ref[...]
        sg = _sigmoid(gb)
        silu = gb * sg
        ybp = att * silu
        yb, ybn, rstd_b = _rms_fwd(ybp, natt_ref[...])
        cat = jnp.concatenate([ya_ref[...], yb.astype(BF16)], axis=1)
        mix = _dot(cat, w_ref[...])
        rn, mn, rstd_m = _rms_fwd(mix, npost_ref[...])
        err = x_ref[...] + gate * rn - t_ref[...]
        dy = err * (1.0 / D)
        gx_ref[...] = dy
        dmix, dnpost = _rms_bwd(dy * gate, mn, rstd_m, npost_ref[...])
        dmb = dmix.astype(BF16)
        gw_ref[...] += _dot_tn(cat, dmb)
        dcat = _dot_nt(dmb, w_ref[...])
        dya_ref[...] = dcat[:, 0:R]
        dybp, dnatt = _rms_bwd(dcat[:, R:2 * R], ybn, rstd_b, natt_ref[...])
        datt_ref[...] = dybp * silu
        dgb_ref[...] = (dybp * att * (sg * (1.0 + gb * (1.0 - sg)))).astype(BF16)
        acc_ref[0:1, :] += jnp.sum(dy * rn, axis=0, keepdims=True)
        acc_ref[1:2, :] += dnpost
        acc_ref[2:3, 0:R] += dnatt
        acc_ref[3:4, :] += jnp.sum(jnp.sum(err * err, axis=1, keepdims=True), axis=0, keepdims=True)

    tile = lambda w: pl.BlockSpec((ts, w), lambda i: (i, 0))
    c0 = lambda shape: pl.BlockSpec(shape, lambda i: (0, 0))
    return pl.pallas_call(
        body, name="out_fwd_bwd", grid=(S // ts,),
        in_specs=[tile(R), tile(R), pl.BlockSpec((ts, R), lambda i: (i, 5)), c0((D, D)), tile(D), tile(D),
                  c0((1, 3 * D)), c0((1, D)), c0((1, R))],
        out_specs=[tile(D), tile(R), tile(R), pl.BlockSpec((ts, R), lambda i: (i, 5)), c0((D, D)), c0((8, D))],
        out_shape=[jax.ShapeDtypeStruct((S, D), F32), jax.ShapeDtypeStruct((S, R), F32),
                   jax.ShapeDtypeStruct((S, R), F32), jax.ShapeDtypeStruct((S, E), BF16),
                   jax.ShapeDtypeStruct((D, D), F32), jax.ShapeDtypeStruct((8, D), F32)],
        compiler_params=_cp(("arbitrary",)),
    )(ya, att, proj, w_out_bf, x, target, mod, norm_post, norm_att)


UC = 256
UPC = EC // UC


NU = E // UC


def _unit_of_step(i):
    return (i % NCHIP) * UPC + i // NCHIP


def _in_proj_bwd(ht, dproj, w_in_bf, x, gx1, mod, norm_pre, smalls):
    ts = 256
    nt = S // ts
    half = D // 2
    units = [_unit_of_step(k) for k in range(NU)]
    owners = [u // UPC for u in units]
    ns = len(smalls)

    def body(*refs):
        (ht_ref, dpu_ref, dp_ref, w_ref, x_ref, gx1_ref, mod_ref, np_ref), refs = refs[:8], refs[8:]
        small_in, refs = refs[:ns], refs[ns:]
        (gx_ref, gin_ref), refs = refs[:2], refs[2:]
        small_out, (acc_out,), refs = refs[:ns], refs[ns:ns + 1], refs[ns + 1:]
        mine, sib, tmp, stage, got, red, acc_ref, hs, hr, ps, pr, bs, br = refs[:13]
        early = _SmallGather(small_in, small_out, *refs[13:16])
        late = _SmallGather([acc_ref], [acc_out], *refs[16:19])
        i = pl.program_id(0)
        xx, yy, c = _me()
        ci = 2 * xx + yy
        r0 = pl.multiple_of(c * half, half)
        r1 = pl.multiple_of((1 - c) * half, half)
        pl.when(i == 0)(early.start)
        pl.when(i == NU)(early.forward)

        def exch(k):
            return _remote(tmp.at[k % 2], sib.at[k], hs.at[k], hr.at[k], 1)

        def partial(k, sender):
            return pltpu.make_async_remote_copy(
                src_ref=stage.at[k], dst_ref=got.at[units[k] % UPC, sender], send_sem=ps.at[k],
                recv_sem=pr.at[k, sender], device_id=(owners[k] // 2, owners[k] % 2, c), device_id_type=MESH)

        def back(k, start):
            off = (units[k] % UPC) * UC
            blk = red.at[pl.ds(start, half), off:off + UC]
            return _remote(blk, blk, bs.at[k], br.at[k], 1)

        for k in range(NU + 1):
            @pl.when(i == k)
            def _():
                if k < NU:
                    if k >= 2:
                        exch(k - 2).wait_send()
                    dpu = dpu_ref[...]
                    tmp[k % 2] = _dot(ht_ref[pl.ds(r1, half), :], dpu)
                    exch(k).start()
                    mine[k] = _dot(ht_ref[pl.ds(r0, half), :], dpu)
                if k >= 1:
                    exch(k - 1).wait_recv()
                    mine[k - 1] += sib[k - 1]

                    @pl.when(ci != owners[k - 1])
                    def _():
                        stage[k - 1] = mine[k - 1].astype(BF16)
                        partial(k - 1, ci).start()

        @pl.when(i == NU)
        def _():
            acc_ref[...] = jnp.zeros_like(acc_ref)

        @pl.when(i >= NU)
        def _():
            dh = sum(_dot_nt(dp_ref[:, j * EC:(j + 1) * EC], w_ref[j]) for j in range(NCHIP))
            hp, xn, rstd = _rms_fwd(x_ref[...], np_ref[...])
            dx, dnp = _rms_bwd(dh * (1.0 + mod_ref[:, D:2 * D]), xn, rstd, np_ref[...])
            gx_ref[...] = gx1_ref[...] + dx
            acc_ref[0:1, :] += jnp.sum(dh, axis=0, keepdims=True)
            acc_ref[1:2, :] += jnp.sum(dh * hp, axis=0, keepdims=True)
            acc_ref[2:3, :] += dnp

        for t in range(UPC):
            @pl.when(i == NU + 1 + 2 * t)
            def _():
                for k in range(NCHIP * t, NCHIP * (t + 1)):
                    @pl.when(ci == owners[k])
                    def _():
                        off = (units[k] % UPC) * UC
                        red[pl.ds(r0, half), off:off + UC] = mine[k]
                        for s in range(NCHIP):
                            if s != owners[k]:
                                partial(k, s).wait_recv()
                                red[pl.ds(r0, half), off:off + UC] += got[units[k] % UPC, s].astype(F32)
                        back(k, r0).start()

        @pl.when(i == NU + nt - 1)
        def _():
            late.start()
            exch(NU - 2).wait_send()
            exch(NU - 1).wait_send()
            for k in range(NU):
                @pl.when(ci == owners[k])
                def _():
                    back(k, r1).wait_recv()
                    back(k, r0).wait_send()

                @pl.when(ci != owners[k])
                def _():
                    partial(k, ci).wait_send()
            gin_ref[...] = red[...]
            early.finish()
            late.forward()
            late.finish()

    tile = lambda w: pl.BlockSpec((ts, w), lambda i: (jnp.maximum(i - NU, 0), 0))
    c0 = lambda shape: pl.BlockSpec(shape, lambda i: (0, 0))
    vm = pl.BlockSpec(memory_space=pltpu.VMEM)
    hbm = pl.BlockSpec(memory_space=pl.ANY)
    gathered = [jax.ShapeDtypeStruct((NDEV,) + a.shape, F32) for a in smalls] + [jax.ShapeDtypeStruct((NDEV, 8, D), F32)]
    return pl.pallas_call(
        body, name="in_proj_bwd", grid=(NU + nt,),
        in_specs=[vm, pl.BlockSpec((S, UC), lambda i: (0, _unit_of_step(jnp.minimum(i, NU - 1)))), tile(E),
                  vm, tile(D), tile(D), c0((1, 3 * D)), c0((1, D))] + [vm] * ns,
        out_specs=[tile(D), vm] + [hbm] * (ns + 1),
        out_shape=[jax.ShapeDtypeStruct((S, D), F32), jax.ShapeDtypeStruct((D, EC), F32)] + gathered,
        scratch_shapes=[pltpu.VMEM((NU, half, UC), F32), pltpu.VMEM((NU, half, UC), F32),
                        pltpu.VMEM((2, half, UC), F32), pltpu.VMEM((NU, half, UC), BF16),
                        pltpu.VMEM((UPC, NCHIP, half, UC), BF16), pltpu.VMEM((D, EC), F32), pltpu.VMEM((8, D), F32),
                        pltpu.SemaphoreType.DMA((NU,)), pltpu.SemaphoreType.DMA((NU,)),
                        pltpu.SemaphoreType.DMA((NU,)), pltpu.SemaphoreType.DMA((NU, NCHIP)),
                        pltpu.SemaphoreType.DMA((NU,)), pltpu.SemaphoreType.DMA((NU,))]
        + _SmallGather.sems(ns) + _SmallGather.sems(1),
        compiler_params=_cp(("arbitrary",)),
    )(ht, dproj, dproj, w_in_bf, x, gx1, mod, norm_pre, *smalls)


def _block_diag(w):
    n, b, _ = w.shape
    eye = jnp.eye(n, dtype=w.dtype)
    return (eye[:, None, :, None] * w[:, :, None, :]).reshape(n * b, n * b)


def _diag_blocks(m):
    n, b = R // HEAD, HEAD
    return jnp.stack([m[h * b:(h + 1) * b, h * b:(h + 1) * b] for h in range(n)])


def _local_step(x, pos, target, mod, w_in_bf, proj, ht, w_out, conv_w, p):
    rec_p = (conv_w, p["conv_b"], p["w_rg_a"], p["b_rg_a"], p["w_rg_x"], p["b_rg_x"], p["lru_lambda"], p["norm_rec"])
    cos, sin = _rope_table(pos, _rope_freq())
    h_all, ya = _rec_fwd(proj, *rec_p)
    att, qr, kr, lse, w_out_bf = _att_fwd(proj, cos, sin, w_out)
    gx1, d_ya, d_att, dproj, gw_out, acc_o = _out_fwd_bwd(ya, att, proj, w_out_bf.reshape(D, D), x, target, mod,
                                                           p["norm_post"], p["norm_att"])
    dproj, g_out = _att_bwd(dproj, d_att, att, lse, qr, kr, proj, cos, sin, gw_out.reshape(NCHIP, D // NCHIP, D))
    dproj, dwa, dwx, sm = _rec_bwd(dproj, d_ya, proj, h_all, *rec_p)
    grad_x, g_in, *gathered = _in_proj_bwd(ht, dproj, w_in_bf, x, gx1, mod, p["norm_pre"], [acc_o, sm, dwa, dwx])
    return grad_x, g_in, g_out, gathered


def _me():
    return lax.axis_index("x"), lax.axis_index("y"), lax.axis_index("c")


def _flip(v, bit):
    return 1 - v if bit else v


def _peer(rel):
    x, y, c = _me()
    return (_flip(x, rel & 4), _flip(y, rel & 2), _flip(c, rel & 1))


def _remote(src, dst, send_sem, recv_sem, rel):
    return pltpu.make_async_remote_copy(src_ref=src, dst_ref=dst, send_sem=send_sem, recv_sem=recv_sem,
                                        device_id=_peer(rel), device_id_type=MESH)


def _allgather_rows(row, name):
    w = row.shape[1]

    def body(row_ref, out_ref, send_sems, recv_sems, local_sem):
        x, y, c = _me()
        me = 4 * x + 2 * y + c
        mine = pltpu.make_async_copy(row_ref, out_ref.at[pl.ds(me, 1), :], local_sem)
        mine.start()
        sends = [_remote(row_ref, out_ref.at[pl.ds(me, 1), :], send_sems.at[r - 1], recv_sems.at[r - 1], r)
                 for r in range(1, NDEV)]
        for cp in sends:
            cp.start()
        for r in range(1, NDEV):
            px, py, pc = _peer(r)
            src = 4 * px + 2 * py + pc
            _remote(row_ref, out_ref.at[pl.ds(src, 1), :], send_sems.at[r - 1], recv_sems.at[r - 1], r).wait_recv()
        for cp in sends:
            cp.wait_send()
        mine.wait()

    return pl.pallas_call(
        body, name=name,
        in_specs=[pl.BlockSpec(memory_space=pltpu.VMEM)],
        out_specs=pl.BlockSpec(memory_space=pltpu.VMEM),
        out_shape=jax.ShapeDtypeStruct((NDEV, w), row.dtype),
        scratch_shapes=[pltpu.SemaphoreType.DMA((NDEV - 1,)), pltpu.SemaphoreType.DMA((NDEV - 1,)),
                        pltpu.SemaphoreType.DMA],
        compiler_params=pltpu.CompilerParams(vmem_limit_bytes=VMEM_LIMIT),
    )(row)


class _WeightGather:
    SEMS = [pltpu.SemaphoreType.DMA((NCHIP - 1,))] * 4

    def __init__(self, w_ref, out_ref, send_sems, recv_sems, fsend_sems, frecv_sems):
        x, y, c = _me()
        self.w, self.out, self.ci = w_ref, out_ref, 2 * x + y
        self.half = w_ref.shape[0] // 2
        self.r0 = pl.multiple_of(c * self.half, self.half)
        self.r1 = pl.multiple_of((1 - c) * self.half, self.half)
        self.sems = (send_sems, recv_sems, fsend_sems, frecv_sems)

    def _ici(self, chip, k):
        blk = self.out.at[chip, pl.ds(self.r0, self.half), :]
        return _remote(blk, blk, self.sems[0].at[k - 1], self.sems[1].at[k - 1], 2 * k)

    def _d2d(self, chip, start, k):
        blk = self.out.at[chip, pl.ds(start, self.half), :]
        return _remote(blk, blk, self.sems[2].at[k - 1], self.sems[3].at[k - 1], 1)

    def start(self):
        self.out[self.ci] = self.w[...].astype(BF16)
        for k in range(1, NCHIP):
            self._ici(self.ci, k).start()

    def forward(self):
        for k in range(1, NCHIP):
            self._ici(self.ci ^ k, k).wait_recv()
            self._d2d(self.ci ^ k, self.r0, k).start()

    def finish(self):
        for k in range(1, NCHIP):
            self._d2d(self.ci ^ k, self.r1, k).wait_recv()
        self.finish_sends()

    def arrive(self, k):
        self._ici(self.ci ^ k, k).wait_recv()
        self._d2d(self.ci ^ k, self.r0, k).start()
        self._d2d(self.ci ^ k, self.r1, k).wait_recv()

    def finish_sends(self):
        for k in range(1, NCHIP):
            self._ici(self.ci, k).wait_send()
            self._d2d(self.ci ^ k, self.r0, k).wait_send()


class _SmallGather:
    @staticmethod
    def sems(n):
        return [pltpu.SemaphoreType.DMA((n, 7)), pltpu.SemaphoreType.DMA((n, 7)), pltpu.SemaphoreType.DMA((n,))]

    def __init__(self, srcs, outs, send_sems, recv_sems, local_sems):
        x, y, c = _me()
        self.srcs, self.outs = list(srcs), list(outs)
        self.ss, self.rs, self.ls = send_sems, recv_sems, local_sems
        self.ci, self.c = 2 * x + y, c
        self.me = 2 * self.ci + c

    def _own(self, a, slot, rel):
        return _remote(self.srcs[a], self.outs[a].at[self.me], self.ss.at[a, slot], self.rs.at[a, slot], rel)

    def _block(self, a, idx, slot, rel):
        blk = self.outs[a].at[idx]
        return _remote(blk, blk, self.ss.at[a, slot], self.rs.at[a, slot], rel)

    def _local(self, a):
        return pltpu.make_async_copy(self.srcs[a], self.outs[a].at[self.me], self.ls.at[a])

    def start(self):
        for a in range(len(self.srcs)):
            self._local(a).start()
            self._own(a, 0, 1).start()
            for k in range(1, NCHIP):
                self._own(a, k, 2 * k).start()

    def forward(self):
        for a in range(len(self.srcs)):
            for k in range(1, NCHIP):
                idx = 2 * (self.ci ^ k) + self.c
                self._block(a, idx, k, 2 * k).wait_recv()
                self._block(a, idx, 3 + k, 1).start()

    def finish(self):
        for a in range(len(self.srcs)):
            self._block(a, 2 * self.ci + 1 - self.c, 0, 1).wait_recv()
            for k in range(1, NCHIP):
                self._block(a, 2 * (self.ci ^ k) + 1 - self.c, 3 + k, 1).wait_recv()
            self._own(a, 0, 1).wait_send()
            for k in range(1, NCHIP):
                self._own(a, k, 2 * k).wait_send()
                self._block(a, 2 * (self.ci ^ k) + self.c, 3 + k, 1).wait_send()
            self._local(a).wait()


def _start_in_proj(crow, w_ada, b_cols, w_in, x, norm_pre, order):
    ts = 256
    nt = S // ts
    wc = crow.shape[1]

    def body(order_ref, crow_ref, wada_ref, b_ref, win_ref, x_ref, np_ref,
             g0_ref, mod_ref, wbf_ref, proj_ref, ht_ref,
             g0s, modp, modb, wbuf, hb_all, cs, cr, ms, mr, ws, wr, fs, fr, local_sems):
        s, t = pl.program_id(0), pl.program_id(1)
        x, y, c = _me()
        ci = 2 * x + y
        me = 2 * ci + c
        wg = _WeightGather(win_ref, wbuf, ws, wr, fs, fr)

        @pl.when(jnp.logical_and(s == 0, t == 0))
        def _():
            wg.start()
            mine = pltpu.make_async_copy(crow_ref, g0s.at[pl.ds(me, 1), :], local_sems.at[0])
            mine.start()
            csend = [_remote(crow_ref, g0s.at[pl.ds(me, 1), :], cs.at[r - 1], cr.at[r - 1], r) for r in range(1, NDEV)]
            for cp in csend:
                cp.start()
            for r in range(1, NDEV):
                px, py, pc = _peer(r)
                _remote(crow_ref, g0s.at[pl.ds(4 * px + 2 * py + pc, 1), :], cs.at[r - 1], cr.at[r - 1], r).wait_recv()
            mine.wait()
            cv = g0s[:, 0:D]
            sc = cv * _sigmoid(cv)
            scb = jnp.concatenate([sc, jnp.zeros_like(sc)], axis=0).astype(BF16)
            modp[...] = _dot(scb, wada_ref[...].astype(BF16))[0:NDEV, :] + b_ref[...]
            own = pltpu.make_async_copy(modp.at[pl.ds(me, 1), :], modb.at[ci], local_sems.at[1])
            own.start()
            msend = []
            for k in range(1, NCHIP):
                cp = _remote(modp.at[pl.ds(2 * (ci ^ k) + c, 1), :], modb.at[ci], ms.at[k - 1], mr.at[k - 1], 2 * k)
                cp.start()
                msend.append(cp)
            for k in range(1, NCHIP):
                _remote(modp.at[pl.ds(me, 1), :], modb.at[ci ^ k], ms.at[k - 1], mr.at[k - 1], 2 * k).wait_recv()
            own.wait()
            for j in range(NCHIP):
                mod_ref[:, j * EC:(j + 1) * EC] = modb[j]
            for cp in csend + msend:
                cp.wait_send()
            g0_ref[...] = g0s[...]

        for k in range(1, NCHIP):
            @pl.when(jnp.logical_and(s == k, t == 0))
            def _():
                wg.arrive(k)

        rows = pl.ds(pl.multiple_of(t * ts, ts), ts)

        @pl.when(s == 0)
        def _():
            hp, _, _ = _rms_fwd(x_ref[...], np_ref[...])
            h = hp * (1.0 + mod_ref[:, D:2 * D]) + mod_ref[:, 0:D]
            hb_all[rows, :] = h.astype(BF16)
            ht_ref[...] = h.T.astype(BF16)

        proj_ref[...] = _dot(hb_all[rows, :], wbuf[ci ^ s])

        @pl.when(jnp.logical_and(s == NCHIP - 1, t == nt - 1))
        def _():
            wg.finish_sends()
            wbf_ref[...] = wbuf[...]

    vm = pl.BlockSpec(memory_space=pltpu.VMEM)
    first_pass = lambda s, t: jnp.where(s == 0, t, nt - 1)
    grid_spec = pltpu.PrefetchScalarGridSpec(
        num_scalar_prefetch=1, grid=(NCHIP, nt),
        in_specs=[vm, vm, vm, vm, pl.BlockSpec((ts, D), lambda s, t, o: (first_pass(s, t), 0)),
                  pl.BlockSpec((1, D), lambda s, t, o: (0, 0))],
        out_specs=[vm, vm, vm, pl.BlockSpec((ts, EC), lambda s, t, o: (t, o[s])),
                   pl.BlockSpec((D, ts), lambda s, t, o: (0, first_pass(s, t)))],
        scratch_shapes=[pltpu.VMEM((NDEV, wc), F32), pltpu.VMEM((NDEV, EC), F32), pltpu.VMEM((NCHIP, 1, EC), F32),
                        pltpu.VMEM((NCHIP, D, EC), BF16), pltpu.VMEM((S, D), BF16),
                        pltpu.SemaphoreType.DMA((NDEV - 1,)), pltpu.SemaphoreType.DMA((NDEV - 1,)),
                        pltpu.SemaphoreType.DMA((NCHIP - 1,)), pltpu.SemaphoreType.DMA((NCHIP - 1,))]
        + _WeightGather.SEMS + [pltpu.SemaphoreType.DMA((2,))])
    return pl.pallas_call(
        body, name="start_in_proj", grid_spec=grid_spec,
        out_shape=[jax.ShapeDtypeStruct((NDEV, wc), F32), jax.ShapeDtypeStruct((1, 3 * D), F32),
                   jax.ShapeDtypeStruct((NCHIP, D, EC), BF16), jax.ShapeDtypeStruct((S, E), F32),
                   jax.ShapeDtypeStruct((D, S), BF16)],
        compiler_params=_cp(("arbitrary", "arbitrary")),
    )(order, crow, w_ada, b_cols, w_in, x, norm_pre)


def _start_gather(crow, w_ada, b_cols, w_in):
    wc = crow.shape[1]

    def body(crow_ref, wada_ref, b_ref, win_ref, g0_ref, mod_ref, wbf_ref,
             modp, modb, cs, cr, ms, mr, ws, wr, fs, fr, local_sems):
        x, y, c = _me()
        ci = 2 * x + y
        me = 2 * ci + c
        wg = _WeightGather(win_ref, wbf_ref, ws, wr, fs, fr)
        mine = pltpu.make_async_copy(crow_ref, g0_ref.at[pl.ds(me, 1), :], local_sems.at[0])
        mine.start()
        csend = [_remote(crow_ref, g0_ref.at[pl.ds(me, 1), :], cs.at[r - 1], cr.at[r - 1], r) for r in range(1, NDEV)]
        for cp in csend:
            cp.start()
        wg.start()
        for r in range(1, NDEV):
            px, py, pc = _peer(r)
            _remote(crow_ref, g0_ref.at[pl.ds(4 * px + 2 * py + pc, 1), :], cs.at[r - 1], cr.at[r - 1], r).wait_recv()
        mine.wait()
        cv = g0_ref[:, 0:D]
        sc = cv * _sigmoid(cv)
        scb = jnp.concatenate([sc, jnp.zeros_like(sc)], axis=0).astype(BF16)
        modp[...] = _dot(scb, wada_ref[...].astype(BF16))[0:NDEV, :] + b_ref[...]
        own = pltpu.make_async_copy(modp.at[pl.ds(me, 1), :], modb.at[ci], local_sems.at[1])
        own.start()
        msend = []
        for k in range(1, NCHIP):
            dst = 2 * (ci ^ k) + c
            cp = _remote(modp.at[pl.ds(dst, 1), :], modb.at[ci], ms.at[k - 1], mr.at[k - 1], 2 * k)
            cp.start()
            msend.append(cp)
        for k in range(1, NCHIP):
            _remote(modp.at[pl.ds(me, 1), :], modb.at[ci ^ k], ms.at[k - 1], mr.at[k - 1], 2 * k).wait_recv()
        own.wait()
        for j in range(NCHIP):
            mod_ref[:, j * EC:(j + 1) * EC] = modb[j]
        wg.forward()
        wg.finish()
        for cp in csend + msend:
            cp.wait_send()

    vm = pl.BlockSpec(memory_space=pltpu.VMEM)
    return pl.pallas_call(
        body, name="start_gather",
        in_specs=[vm] * 4, out_specs=[vm] * 3,
        out_shape=[jax.ShapeDtypeStruct((NDEV, wc), F32), jax.ShapeDtypeStruct((1, 3 * D), F32),
                   jax.ShapeDtypeStruct((NCHIP, D, EC), BF16)],
        scratch_shapes=[pltpu.VMEM((NDEV, EC), F32), pltpu.VMEM((NCHIP, 1, EC), F32),
                        pltpu.SemaphoreType.DMA((NDEV - 1,)), pltpu.SemaphoreType.DMA((NDEV - 1,)),
                        pltpu.SemaphoreType.DMA((NCHIP - 1,)), pltpu.SemaphoreType.DMA((NCHIP - 1,))]
        + _WeightGather.SEMS + [pltpu.SemaphoreType.DMA((2,))],
        compiler_params=pltpu.CompilerParams(vmem_limit_bytes=VMEM_LIMIT),
    )(crow, w_ada, b_cols, w_in)


class _ReduceScatter:
    @staticmethod
    def scratch(n_units, rows, ucols, max_owned):
        half = rows // 2
        return [pltpu.VMEM((n_units, half, ucols), F32), pltpu.VMEM((n_units, half, ucols), BF16),
                pltpu.VMEM((max_owned, NCHIP, half, ucols), BF16),
                pltpu.SemaphoreType.DMA((2,)), pltpu.SemaphoreType.DMA((n_units,)),
                pltpu.SemaphoreType.DMA((n_units, NCHIP)), pltpu.SemaphoreType.DMA((n_units,)),
                pltpu.SemaphoreType.DMA((n_units,))]

    def __init__(self, g_ref, out_ref, units, sib, stage, got, sem1, send2, recv2, send3, recv3):
        x, y, c = _me()
        self.c, self.ci = c, 2 * x + y
        self.g, self.out, self.units = g_ref, out_ref, units
        self.sib, self.stage, self.got = sib, stage, got
        self.sem1, self.send2, self.recv2, self.send3, self.recv3 = sem1, send2, recv2, send3, recv3
        self.half = g_ref.shape[1] // 2
        self.ucols = g_ref.shape[2]
        self.r0 = pl.multiple_of(c * self.half, self.half)
        self.r1 = pl.multiple_of((1 - c) * self.half, self.half)
        self.slot0 = units[0][0]
        assert [u[0] for u in units] == list(range(self.slot0, self.slot0 + len(units)))
        seen = {}
        self.local = []
        for _, owner, _ in units:
            self.local.append(seen.get(owner, 0))
            seen[owner] = seen.get(owner, 0) + 1

    def _halves(self):
        n = len(self.units)
        return _remote(self.g.at[pl.ds(self.slot0, n), pl.ds(self.r1, self.half), :], self.sib,
                       self.sem1.at[0], self.sem1.at[1], 1)

    def _partial(self, i, sender):
        _, owner, _ = self.units[i]
        return pltpu.make_async_remote_copy(
            src_ref=self.stage.at[i], dst_ref=self.got.at[self.local[i], sender],
            send_sem=self.send2.at[i], recv_sem=self.recv2.at[i, sender],
            device_id=(owner // 2, owner % 2, self.c), device_id_type=MESH)

    def _back(self, i, start):
        off = self.units[i][2]
        blk = self.out.at[pl.ds(start, self.half), off:off + self.ucols]
        return _remote(blk, blk, self.send3.at[i], self.recv3.at[i], 1)

    def at_steps(self, step, start, send, reduce, finish, out_ref):
        @pl.when(step == start)
        def _():
            self.out[...] = jnp.zeros_like(self.out)
            self.start_halves()

        pl.when(step == send)(self.send_partials)
        pl.when(step == reduce)(self.reduce_owned)

        @pl.when(step == finish)
        def _():
            self.finish()
            out_ref[...] = self.out[...]

    def start_halves(self):
        self._halves().start()

    def send_partials(self):
        self._halves().wait_recv()
        for i, (slot, owner, _) in enumerate(self.units):
            @pl.when(self.ci != owner)
            def _():
                self.stage[i] = (self.g[slot, pl.ds(self.r0, self.half), :] + self.sib[i]).astype(BF16)
                self._partial(i, self.ci).start()

    def reduce_owned(self):
        for i, (slot, owner, off) in enumerate(self.units):
            @pl.when(self.ci == owner)
            def _():
                rows, cols = pl.ds(self.r0, self.half), slice(off, off + self.ucols)
                self.out[rows, cols] = self.g[slot, pl.ds(self.r0, self.half), :] + self.sib[i]
                for s in range(NCHIP):
                    if s != owner:
                        self._partial(i, s).wait_recv()
                        self.out[rows, cols] += self.got[self.local[i], s].astype(F32)
                self._back(i, self.r0).start()

    def finish(self):
        self._halves().wait_send()
        for i, (_, owner, _) in enumerate(self.units):
            @pl.when(self.ci == owner)
            def _():
                self._back(i, self.r1).wait_recv()
                self._back(i, self.r0).wait_send()

            @pl.when(self.ci != owner)
            def _():
                self._partial(i, self.ci).wait_send()


def _reduce_scatter(g4, name):
    _, rows, cols = g4.shape
    units = [(j, j, 0) for j in range(NCHIP)]

    def body(g_ref, out_ref, *scratch):
        rs = _ReduceScatter(g_ref, out_ref, units, *scratch)
        rs.start_halves()
        rs.send_partials()
        rs.reduce_owned()
        rs.finish()

    return pl.pallas_call(
        body, name=name,
        in_specs=[pl.BlockSpec(memory_space=pltpu.VMEM)],
        out_specs=pl.BlockSpec(memory_space=pltpu.VMEM),
        out_shape=jax.ShapeDtypeStruct((rows, cols), F32),
        scratch_shapes=_ReduceScatter.scratch(NCHIP, rows, cols, 1),
        compiler_params=pltpu.CompilerParams(vmem_limit_bytes=VMEM_LIMIT),
    )(g4)


def _silu_rows(c_ref):
    cv = c_ref[...]
    sc = cv * _sigmoid(cv)
    return jnp.concatenate([sc, jnp.zeros_like(sc)], axis=0).astype(BF16)


def _ada_fwd(cg, w_ada, b_cols):
    def body(c_ref, w_ref, b_ref, o_ref):
        o_ref[...] = _dot(_silu_rows(c_ref), w_ref[...].astype(BF16))[0:NDEV, :] + b_ref[...]

    return pl.pallas_call(body, name="ada_fwd", out_shape=jax.ShapeDtypeStruct((NDEV, EC), F32),
                          compiler_params=_cp())(cg, w_ada, b_cols)


def _ada_bwd(cg, dmod_cols):
    def body(c_ref, d_ref, o_ref):
        dm = d_ref[...]
        dmb = jnp.concatenate([dm, jnp.zeros_like(dm)], axis=0).astype(BF16)
        o_ref[...] = _dot_tn(_silu_rows(c_ref), dmb)

    return pl.pallas_call(body, name="ada_bwd", out_shape=jax.ShapeDtypeStruct((D, EC), F32),
                          compiler_params=_cp())(cg, dmod_cols)


def _sum_rows(g):
    def body(g_ref, o_ref):
        acc = g_ref[0:1, :]
        for r in range(1, NDEV):
            acc = acc + g_ref[r:r + 1, :]
        o_ref[...] = acc

    return pl.pallas_call(body, name="sum_rows", out_shape=jax.ShapeDtypeStruct((1, g.shape[1]), F32),
                          compiler_params=_cp())(g)


def _adamw(w, g, m, v, name):
    rows, cols = w.shape
    tr = 256 if rows % 256 == 0 else rows

    def body(w_ref, g_ref, m_ref, v_ref, d_ref, nm_ref, nv_ref):
        gv = g_ref[...]
        nm = B1 * m_ref[...] + (1.0 - B1) * gv
        nv = B2 * v_ref[...] + (1.0 - B2) * (gv * gv)
        m_hat = nm / (1.0 - B1 ** STEP)
        v_hat = nv / (1.0 - B2 ** STEP)
        d_ref[...] = (-LR) * (m_hat / (jnp.sqrt(v_hat) + ADAM_EPS) + WD * w_ref[...])
        nm_ref[...] = nm
        nv_ref[...] = nv

    spec = pl.BlockSpec((tr, cols), lambda i: (i, 0))
    return pl.pallas_call(
        body, name=name, grid=(rows // tr,), in_specs=[spec] * 4, out_specs=[spec] * 3,
        out_shape=[jax.ShapeDtypeStruct((rows, cols), F32)] * 3,
        compiler_params=_cp(("parallel",)),
    )(w, g, m, v)


def _adamw_values(w, g, m, v):
    nm = B1 * m + (1.0 - B1) * g
    nv = B2 * v + (1.0 - B2) * (g * g)
    m_hat = nm / (1.0 - B1 ** STEP)
    v_hat = nv / (1.0 - B2 ** STEP)
    return (-LR) * (m_hat / (jnp.sqrt(v_hat) + ADAM_EPS) + WD * w), nm, nv


NB = R // HEAD
SMALL = (("b_ada", (1, 3 * D)), ("norm_pre", (1, D)), ("norm_post", (1, D)), ("conv_w", (4, R // NCHIP)),
         ("conv_b", (1, R)), ("w_rg_a", (NB, HEAD, HEAD)), ("b_rg_a", (1, R)), ("w_rg_x", (NB, HEAD, HEAD)),
         ("b_rg_x", (1, R)), ("lru_lambda", (1, R)), ("norm_rec", (1, R)), ("norm_att", (1, R)))


def _small_update(ao8, sm8, dwa8, dwx8, ai8, cg, params):
    n = len(SMALL)

    def body(ao_ref, sm_ref, dwa_ref, dwx_ref, ai_ref, cg_ref, *refs):
        pin, pout, (gada_ref, loss_ref, dmod) = refs[:3 * n], refs[3 * n:7 * n], refs[7 * n:]
        xx, yy, _ = _me()
        ci = 2 * xx + yy

        def total(ref, *idx):
            acc = ref[(0,) + idx]
            for d in range(1, NDEV):
                acc = acc + ref[(d,) + idx]
            return acc

        row = lambda ref, r, lanes=slice(None): total(ref, slice(r, r + 1), lanes)
        mine = lambda parts: sum(jnp.where(ci == j, part, 0.0) for j, part in enumerate(parts))
        cw = R // NCHIP
        grads = {
            "b_ada": [jnp.concatenate([row(ai_ref, 0), row(ai_ref, 1), row(ao_ref, 0)], axis=1)],
            "norm_pre": [row(ai_ref, 2)], "norm_post": [row(ao_ref, 1)],
            "conv_w": [mine([row(sm_ref, 8 + r, slice(j * cw, (j + 1) * cw)) for j in range(NCHIP)]) for r in range(4)],
            "conv_b": [row(sm_ref, 4)], "b_rg_a": [row(sm_ref, 0)], "b_rg_x": [row(sm_ref, 1)],
            "lru_lambda": [row(sm_ref, 2)], "norm_rec": [row(sm_ref, 3)], "norm_att": [row(ao_ref, 2, slice(0, R))],
            "w_rg_a": [total(dwa_ref, h) for h in range(NB)], "w_rg_x": [total(dwx_ref, h) for h in range(NB)],
        }
        loss_ref[...] = row(ao_ref, 3, slice(0, LANES)) * (0.5 / D)
        for k, (name, shape) in enumerate(SMALL):
            w_ref, m_ref, v_ref = pin[3 * k:3 * k + 3]
            outs = pout[4 * k:4 * k + 4]
            for r, g in enumerate(grads[name]):
                at = (slice(None),) if len(grads[name]) == 1 else ((r,) if len(shape) == 3 else (slice(r, r + 1),))
                res = (g,) + _adamw_values(w_ref[at], g, m_ref[at], v_ref[at])
                for o_ref, val in zip(outs, res):
                    o_ref[at] = val
        for d in range(NDEV):
            dmod[d:d + 1, :] = jnp.concatenate([ai_ref[d, 0:1, :], ai_ref[d, 1:2, :], ao_ref[d, 0:1, :]], axis=1)
        cols = mine([dmod[:, j * EC:(j + 1) * EC] for j in range(NCHIP)])
        colsb = jnp.concatenate([cols, jnp.zeros_like(cols)], axis=0).astype(BF16)
        gada_ref[...] = _dot_tn(_silu_rows(cg_ref), colsb)

    shapes = [jax.ShapeDtypeStruct(s, F32) for _, s in SMALL]
    outs = pl.pallas_call(
        body, name="small_update",
        out_shape=[s for s in shapes for _ in range(4)] + [jax.ShapeDtypeStruct((D, EC), F32),
                                                           jax.ShapeDtypeStruct((1, LANES), F32)],
        scratch_shapes=[pltpu.VMEM((NDEV, 3 * D), F32)],
        compiler_params=_cp(),
    )(ao8, sm8, dwa8, dwx8, ai8, cg, *params)
    return outs[:4 * n], outs[4 * n], outs[4 * n + 1]


BIG = ("w_ada", "w_in", "w_out")
WEIGHTS = ("w_ada", "b_ada", "norm_pre", "norm_post", "w_in", "conv_w", "conv_b", "w_rg_a", "b_rg_a", "w_rg_x",
           "b_rg_x", "lru_lambda", "norm_rec", "norm_att", "w_out")


def kernel(x, c, positions, w_ada, b_ada, norm_pre, norm_post, w_in, conv_w, conv_b, w_rg_a, b_rg_a, w_rg_x, b_rg_x, lru_lambda, norm_rec, norm_att, w_out, loss_target, m_w_ada, m_b_ada, m_norm_pre, m_norm_post, m_w_in, m_conv_w, m_conv_b, m_w_rg_a, m_b_rg_a, m_w_rg_x, m_b_rg_x, m_lru_lambda, m_norm_rec, m_norm_att, m_w_out, v_w_ada, v_b_ada, v_norm_pre, v_norm_post, v_w_in, v_conv_w, v_conv_b, v_w_rg_a, v_b_rg_a, v_w_rg_x, v_b_rg_x, v_lru_lambda, v_norm_rec, v_norm_att, v_w_out):
    given = dict(locals())
    wts = {n: given[n] for n in WEIGHTS}
    ms = {n: given["m_" + n] for n in WEIGHTS}
    vs = {n: given["v_" + n] for n in WEIGHTS}
    xi, yi, _ = _me()
    chip = 2 * xi + yi
    cw_loc = R // NCHIP

    b_cols = lax.dynamic_slice(b_ada, (0, chip * EC), (1, EC))
    order = (chip ^ jnp.arange(NCHIP, dtype=jnp.int32)).astype(jnp.int32)
    g0, mod, w_in_bf, proj, ht = _start_in_proj(jnp.concatenate([c, conv_w.reshape(1, 4 * cw_loc)], axis=1),
                                                w_ada[0], b_cols, w_in[0], x[0], norm_pre, order)
    cg = g0[:, 0:D]
    conv_full = g0[0::2, D:].reshape(NCHIP, 4, cw_loc).transpose(1, 0, 2).reshape(4, R)

    p = dict(norm_pre=norm_pre, norm_post=norm_post, conv_b=conv_b, b_rg_a=b_rg_a, b_rg_x=b_rg_x,
             lru_lambda=lru_lambda, norm_rec=norm_rec, norm_att=norm_att, w_rg_a=w_rg_a[0], w_rg_x=w_rg_x[0])
    grad_x, g_in, g_out, gathered = _local_step(
        x[0], positions.reshape(S, 1), loss_target[0], mod, w_in_bf, proj, ht, w_out[0], conv_full, p)

    params = [d[n].reshape(shape) for n, shape in SMALL for d in (wts, ms, vs)]
    small_out, g_ada, loss_row = _small_update(*gathered, cg, params)
    grads = {"w_out": g_out, "w_in": g_in, "w_ada": g_ada}
    delta, new_m, new_v = {}, {}, {}
    for k, (n, _) in enumerate(SMALL):
        grads[n], delta[n], new_m[n], new_v[n] = small_out[4 * k:4 * k + 4]
    for n in BIG:
        delta[n], new_m[n], new_v[n] = _adamw(wts[n][0], grads[n], ms[n][0], vs[n][0], "adamw_" + n)
    out = lambda d: [d[n].reshape(wts[n].shape) for n in WEIGHTS]
    return (loss_row[0, 0], grad_x.reshape(x.shape), *out(grads), *out(delta), *out(new_m), *out(new_v))
```

```python
import functools

import numpy as np
import jax
import jax.numpy as jnp
from jax import lax
from jax.experimental import pallas as pl
from jax.experimental.pallas import tpu as pltpu

F32 = jnp.float32
BF16 = jnp.bfloat16

S = 2048
D = 1024
E = 3072
R = 512
NDEV = 8
NCHIP = 4
EC = 768
LRU_C = 8.0
EPS = 1e-6
NEG = -1e30
HEAD = 64
BLK = 128
PATTERNS = (1, 4, 16)
ROPE_THETA = 10000.0
LANES = 128
VMEM_LIMIT = 56 * 1024 * 1024

B1, B2, LR, WD, ADAM_EPS, STEP = 0.9, 0.999, 0.001, 0.01, 1e-8, 10
MESH = pl.DeviceIdType.MESH


def _cp(sem=None, **kw):
    return pltpu.CompilerParams(dimension_semantics=sem, vmem_limit_bytes=VMEM_LIMIT, **kw)


def _dot(a, b):
    return jnp.dot(a, b, preferred_element_type=F32)


def _dot_nt(a, b):
    return lax.dot_general(a, b, (((1,), (1,)), ((), ())), preferred_element_type=F32)


def _dot_tn(a, b):
    return lax.dot_general(a, b, (((0,), (0,)), ((), ())), preferred_element_type=F32)


def _sigmoid(x):
    return 1.0 / (1.0 + jnp.exp(-x))


def _expm1(x):
    poly = x * (1.0 + x * (0.5 + x * (1.0 / 6 + x * (1.0 / 24 + x * (1.0 / 120 + x * (1.0 / 720))))))
    return jnp.where(jnp.abs(x) < 0.3, poly, jnp.exp(x) - 1.0)


def _rms_fwd(v, g):
    rstd = lax.rsqrt(jnp.mean(v * v, axis=-1, keepdims=True) + EPS)
    vn = v * rstd
    return vn * g, vn, rstd


def _rms_bwd(dy, vn, rstd, g):
    dvn = dy * g
    dv = rstd * (dvn - vn * jnp.mean(dvn * vn, axis=-1, keepdims=True))
    return dv, jnp.sum(dy * vn, axis=0, keepdims=True)


def _in_proj_fwd(x, mod, norm_pre, w_in_bf):
    ts = 256

    def body(x_ref, mod_ref, np_ref, w_ref, proj_ref, ht_ref):
        hp, _, _ = _rms_fwd(x_ref[...], np_ref[...])
        h = hp * (1.0 + mod_ref[:, D:2 * D]) + mod_ref[:, 0:D]
        hb = h.astype(BF16)
        ht_ref[...] = h.T.astype(BF16)
        for j in range(NCHIP):
            proj_ref[:, j * EC:(j + 1) * EC] = _dot(hb, w_ref[j])

    return pl.pallas_call(
        body, name="in_proj_fwd", grid=(S // ts,),
        in_specs=[pl.BlockSpec((ts, D), lambda i: (i, 0)), pl.BlockSpec((1, 3 * D), lambda i: (0, 0)),
                  pl.BlockSpec((1, D), lambda i: (0, 0)), pl.BlockSpec((NCHIP, D, EC), lambda i: (0, 0, 0))],
        out_specs=[pl.BlockSpec((ts, E), lambda i: (i, 0)), pl.BlockSpec((D, ts), lambda i: (0, i))],
        out_shape=[jax.ShapeDtypeStruct((S, E), F32), jax.ShapeDtypeStruct((D, S), BF16)],
        compiler_params=_cp(("parallel",)),
    )(x, mod, norm_pre, w_in_bf)


RT = 256


def _shift_down(cur, prev8, j, row):
    if j == 0:
        return cur
    top = jnp.tile(pltpu.roll(prev8, j, 0), (RT // 8, 1))
    return jnp.where(row >= j, pltpu.roll(cur, j, 0), top)


def _shift_up(cur, next8, j, row):
    if j == 0:
        return cur
    bot = jnp.tile(pltpu.roll(next8, 8 - j, 0), (RT // 8, 1))
    return jnp.where(row < RT - j, pltpu.roll(cur, RT - j, 0), bot)


def _rec_gates(xp, xprev8, row, cw_ref, cb_ref, wa_ref, ba_ref, wx_ref, bx_ref, lam_ref):
    xa = cb_ref[...] + sum(cw_ref[3 - j:4 - j, :] * _shift_down(xp, xprev8, j, row) for j in range(4))
    xab = xa.astype(BF16)
    r = _sigmoid(_dot(xab, wa_ref[...]) + ba_ref[...])
    ig = _sigmoid(_dot(xab, wx_ref[...]) + bx_ref[...])
    nl = -lam_ref[...]
    sp = jnp.maximum(nl, 0.0) + jnp.log1p(jnp.exp(-jnp.abs(nl)))
    la = (-LRU_C) * r * sp
    a = jnp.exp(la)
    mult = jnp.sqrt(-_expm1(2.0 * la))
    return dict(xa=xa, xab=xab, r=r, ig=ig, sp=sp, la=la, a=a, mult=mult)


def _scan_fwd(a, u, row):
    sh = 1
    while sh < RT:
        a_s = jnp.where(row >= sh, pltpu.roll(a, sh, 0), 1.0)
        u_s = jnp.where(row >= sh, pltpu.roll(u, sh, 0), 0.0)
        u = a * u_s + u
        a = a * a_s
        sh *= 2
    return a, u


def _scan_bwd(al, g, row):
    sh = 1
    while sh < RT:
        al_s = jnp.where(row < RT - sh, pltpu.roll(al, RT - sh, 0), 1.0)
        g_s = jnp.where(row < RT - sh, pltpu.roll(g, RT - sh, 0), 0.0)
        g = g + al * g_s
        al = al * al_s
        sh *= 2
    return g


def _dense_from_blocks(blocks_ref, dense_ref):
    dense_ref[...] = jnp.zeros_like(dense_ref)
    for h in range(R // HEAD):
        dense_ref[h * HEAD:(h + 1) * HEAD, h * HEAD:(h + 1) * HEAD] = blocks_ref[h].astype(dense_ref.dtype)


def _rec_fwd(proj, conv_w, conv_b, wa_b, ba, wx_b, bx, lam, norm_rec):
    nt = S // RT

    def body(p_ref, cw_ref, cb_ref, wa_ref, ba_ref, wx_ref, bx_ref, lam_ref, nr_ref,
             h_ref, ya_ref, prev8, hc, wad, wxd):
        i = pl.program_id(0)

        @pl.when(i == 0)
        def _():
            prev8[...] = jnp.zeros_like(prev8)
            hc[...] = jnp.zeros_like(hc)
            _dense_from_blocks(wa_ref, wad)
            _dense_from_blocks(wx_ref, wxd)

        row = lax.broadcasted_iota(jnp.int32, (RT, R), 0)
        xp = p_ref[:, 0:R]
        ga = p_ref[:, R:2 * R]
        f = _rec_gates(xp, prev8[...], row, cw_ref, cb_ref, wad, ba_ref, wxd, bx_ref, lam_ref)
        u = f["mult"] * (f["ig"] * f["xa"])
        acum, hh = _scan_fwd(f["a"], u, row)
        h = hh + acum * hc[0:1, :]
        h_ref[...] = h
        hc[0:1, :] = h_ref[RT - 1:RT, :]
        prev8[...] = p_ref[RT - 8:RT, 0:R]
        yp = h * (ga * _sigmoid(ga))
        ya, _, _ = _rms_fwd(yp, nr_ref[...])
        ya_ref[...] = ya.astype(BF16)

    row1 = lambda n: pl.BlockSpec((1, n), lambda i: (0, 0))
    blocks = pl.BlockSpec((R // HEAD, HEAD, HEAD), lambda i: (0, 0, 0))
    return pl.pallas_call(
        body, name="rec_fwd", grid=(nt,),
        in_specs=[pl.BlockSpec((RT, 2 * R), lambda i: (i, 0)), pl.BlockSpec((4, R), lambda i: (0, 0)), row1(R),
                  blocks, row1(R), blocks, row1(R), row1(R), row1(R)],
        out_specs=[pl.BlockSpec((RT, R), lambda i: (i, 0)), pl.BlockSpec((RT, R), lambda i: (i, 0))],
        out_shape=[jax.ShapeDtypeStruct((S, R), F32), jax.ShapeDtypeStruct((S, R), BF16)],
        scratch_shapes=[pltpu.VMEM((8, R), F32), pltpu.VMEM((8, R), F32), pltpu.VMEM((R, R), BF16),
                        pltpu.VMEM((R, R), BF16)],
        compiler_params=_cp(("arbitrary",)),
    )(proj, conv_w, conv_b, wa_b, ba, wx_b, bx, lam, norm_rec)


def _rec_bwd(dproj, d_ya, proj, h_all, conv_w, conv_b, wa_b, ba, wx_b, bx, lam, norm_rec):
    nt = S // RT

    def body(dp_in, dya_ref, p_ref, pprev_ref, h_ref, hprev_ref, cw_ref, cb_ref, wab_ref, ba_ref, wxb_ref, bx_ref,
             lam_ref, nr_ref, dp_ref, dwab_ref, dwxb_ref, sm_ref, nxt8, cg, wa_ref, wx_ref, dwa_ref, dwx_ref):
        i = pl.program_id(0)
        ti = nt - 1 - i

        @pl.when(i == 0)
        def _():
            nxt8[...] = jnp.zeros_like(nxt8)
            cg[...] = jnp.zeros_like(cg)
            dwa_ref[...] = jnp.zeros_like(dwa_ref)
            dwx_ref[...] = jnp.zeros_like(dwx_ref)
            sm_ref[...] = jnp.zeros_like(sm_ref)
            _dense_from_blocks(wab_ref, wa_ref)
            _dense_from_blocks(wxb_ref, wx_ref)

        row = lax.broadcasted_iota(jnp.int32, (RT, R), 0)
        first = (ti > 0).astype(F32)
        xprev8 = pprev_ref[...] * first
        hprev8 = hprev_ref[...] * first
        xp = p_ref[:, 0:R]
        ga = p_ref[:, R:2 * R]
        f = _rec_gates(xp, xprev8, row, cw_ref, cb_ref, wa_ref, ba_ref, wx_ref, bx_ref, lam_ref)
        xa, r, ig, a, mult = f["xa"], f["r"], f["ig"], f["a"], f["mult"]
        h = h_ref[...]
        sg = _sigmoid(ga)
        gate = ga * sg
        yp = h * gate
        _, ypn, rstd = _rms_fwd(yp, nr_ref[...])
        d_yp, dnr = _rms_bwd(dya_ref[...], ypn, rstd, nr_ref[...])
        d_ga = d_yp * h * (sg * (1.0 + ga * (1.0 - sg)))
        dh = d_yp * gate + jnp.where(row == RT - 1, cg[0:1, :], 0.0)
        al = jnp.where(row < RT - 1, pltpu.roll(a, RT - 1, 0), 0.0)
        g = _scan_bwd(al, dh, row)
        cg[0:1, :] = jnp.sum(jnp.where(row == 0, a * g, 0.0), axis=0, keepdims=True)
        h_m1 = _shift_down(h, hprev8, 1, row)
        da = g * h_m1
        ix = ig * xa
        d_mult = g * ix
        d_ig = g * mult * xa
        d_xa = g * mult * ig
        d_la = da * a - d_mult * (a * a) / mult
        d_r = d_la * ((-LRU_C) * f["sp"])
        dsp = jnp.sum(d_la * ((-LRU_C) * r), axis=0, keepdims=True)
        dlam = dsp * (-_sigmoid(-lam_ref[...]))
        d_za = d_r * r * (1.0 - r)
        d_zx = d_ig * ig * (1.0 - ig)
        dzab = d_za.astype(BF16)
        dzxb = d_zx.astype(BF16)
        dwa_ref[...] += _dot_tn(f["xab"], dzab)
        dwx_ref[...] += _dot_tn(f["xab"], dzxb)
        d_xa = d_xa + _dot_nt(dzab, wa_ref[...]) + _dot_nt(dzxb, wx_ref[...])
        d_xp = sum(cw_ref[3 - j:4 - j, :] * _shift_up(d_xa, nxt8[...], j, row) for j in range(4))
        dcw = [jnp.sum(d_xa * _shift_down(xp, xprev8, 3 - k, row), axis=0, keepdims=True) for k in range(4)]
        dp_ref[:, 0:R] = d_xp.astype(BF16)
        dp_ref[:, R:2 * R] = d_ga.astype(BF16)
        dp8 = d_xa[0:8, :]
        nxt8[...] = dp8
        sm_ref[0:1, :] += jnp.sum(d_za, axis=0, keepdims=True)
        sm_ref[1:2, :] += jnp.sum(d_zx, axis=0, keepdims=True)
        sm_ref[2:3, :] += dlam
        sm_ref[3:4, :] += dnr
        sm_ref[4:5, :] += jnp.sum(d_xa, axis=0, keepdims=True)
        for k in range(4):
            sm_ref[8 + k:9 + k, :] += dcw[k]

        @pl.when(i == nt - 1)
        def _():
            for h in range(R // HEAD):
                dwab_ref[h] = dwa_ref[h * HEAD:(h + 1) * HEAD, h * HEAD:(h + 1) * HEAD]
                dwxb_ref[h] = dwx_ref[h * HEAD:(h + 1) * HEAD, h * HEAD:(h + 1) * HEAD]

    c0 = lambda shape: pl.BlockSpec(shape, lambda i: (0, 0))
    blocks = pl.BlockSpec((R // HEAD, HEAD, HEAD), lambda i: (0, 0, 0))
    rev = lambda i: nt - 1 - i
    prev8 = lambda i: (jnp.maximum((nt - 1 - i) * (RT // 8) - 1, 0), 0)
    return pl.pallas_call(
        body, name="rec_bwd", grid=(nt,),
        in_specs=[pl.BlockSpec(memory_space=pl.ANY),
                  pl.BlockSpec((RT, R), lambda i: (rev(i), 0)),
                  pl.BlockSpec((RT, 2 * R), lambda i: (rev(i), 0)), pl.BlockSpec((8, R), prev8),
                  pl.BlockSpec((RT, R), lambda i: (rev(i), 0)), pl.BlockSpec((8, R), prev8),
                  c0((4, R)), c0((1, R)), blocks, c0((1, R)), blocks, c0((1, R)), c0((1, R)), c0((1, R))],
        out_specs=[pl.BlockSpec((RT, 2 * R), lambda i: (rev(i), 0)), blocks, blocks, c0((16, R))],
        out_shape=[jax.ShapeDtypeStruct((S, E), BF16), jax.ShapeDtypeStruct((R // HEAD, HEAD, HEAD), F32),
                   jax.ShapeDtypeStruct((R // HEAD, HEAD, HEAD), F32), jax.ShapeDtypeStruct((16, R), F32)],
        scratch_shapes=[pltpu.VMEM((8, R), F32), pltpu.VMEM((8, R), F32), pltpu.VMEM((R, R), BF16),
                        pltpu.VMEM((R, R), BF16), pltpu.VMEM((R, R), F32), pltpu.VMEM((R, R), F32)],
        input_output_aliases={0: 0},
        compiler_params=_cp(("arbitrary",)),
    )(dproj, d_ya, proj, proj, h_all, h_all, conv_w, conv_b, wa_b, ba, wx_b, bx, lam, norm_rec)


NPAIR = R // LANES
QB, KB, VB, GB = 2 * R // LANES, 3 * R // LANES, 4 * R // LANES, 5 * R // LANES


def _rope_freq():
    half = HEAD // 2
    inv = np.float32(ROPE_THETA) ** (-(np.arange(half, dtype=np.float32) / np.float32(half)))
    return jnp.asarray(np.tile(inv.astype(np.float32), LANES // half)[None, :])


def _rot_half(x, first):
    return jnp.where(first, -pltpu.roll(x, LANES - HEAD // 2, 1), pltpu.roll(x, HEAD // 2, 1))


def _cos_sin(pos_ref, freq_ref):
    ang = pos_ref[...].astype(F32) * freq_ref[...]
    return jnp.cos(ang), jnp.sin(ang)


def _deint(src_ref, dst_ref, d):
    n = S // d
    for r in range(d):
        v = src_ref[pl.ds(r, n, stride=d), :] if d > 1 else src_ref[...]
        dst_ref[r * n:(r + 1) * n, :] = v.astype(dst_ref.dtype)


def _reint(src_ref, dst_ref, d, accumulate):
    n = S // d
    for r in range(d):
        idx = (pl.ds(r, n, stride=d), slice(None)) if d > 1 else (slice(None), slice(None))
        v = src_ref[r * n:(r + 1) * n, :]
        if accumulate:
            dst_ref[idx] = dst_ref[idx] + v
        else:
            dst_ref[idx] = v


def _blk_masks(b, nb):
    qi = lax.broadcasted_iota(jnp.int32, (BLK, BLK), 0)
    ki = lax.broadcasted_iota(jnp.int32, (BLK, BLK), 1)
    has_prev = lax.rem(b, nb) != 0
    return ki <= qi, jnp.logical_and(ki >= qi, has_prev)


def _rope_table(pos, freq):
    def body(pos_ref, freq_ref, cos_ref, sin_ref):
        cos_ref[...], sin_ref[...] = _cos_sin(pos_ref, freq_ref)

    return pl.pallas_call(body, name="rope_table", out_shape=[jax.ShapeDtypeStruct((S, LANES), F32)] * 2,
                          compiler_params=_cp())(pos, freq)


def _deint_heads(src_ref, dst0, dst1, d):
    n = S // d
    hm0 = lax.broadcasted_iota(jnp.int32, (n, LANES), 1) < HEAD
    for r in range(d):
        v = src_ref[pl.ds(r, n, stride=d), :] if d > 1 else src_ref[...]
        dst0[r * n:(r + 1) * n, :] = jnp.where(hm0, v, 0.0).astype(BF16)
        dst1[r * n:(r + 1) * n, :] = jnp.where(hm0, 0.0, v).astype(BF16)


def _reint_prev(src_ref, dst_ref, d):
    n = S // d
    if n == BLK:
        return
    for r in range(d):
        idx = (pl.ds(r, n - BLK, stride=d), slice(None)) if d > 1 else (slice(0, n - BLK), slice(None))
        dst_ref[idx] = dst_ref[idx] + src_ref[r * n + BLK:(r + 1) * n, :]


def _pair_masks():
    qi = lax.broadcasted_iota(jnp.int32, (BLK, 2 * BLK), 0)
    ki = lax.broadcasted_iota(jnp.int32, (BLK, 2 * BLK), 1) & (BLK - 1)
    return ki <= qi, ki >= qi


def _two(ref0, ref1, st, axis):
    return jnp.concatenate([ref0[pl.ds(st, BLK), :], ref1[pl.ds(st, BLK), :]], axis=axis)


ATT_UNROLL = 4


def _att_fwd(proj, cos, sin, w_out):
    def body(q_ref, k_ref, v_ref, cos_ref, sin_ref, w_ref, att_ref, qr_ref, kr_ref, lse_ref, wbf_ref,
             qd, kd0, kd1, vd0, vd1, od, ld, on, ln, wbuf, *wsems):
        wg = _WeightGather(w_ref, wbuf, *wsems)
        pl.when(pl.program_id(0) == 0)(wg.start)
        pl.when(pl.program_id(0) == 1)(wg.forward)
        lane = lax.broadcasted_iota(jnp.int32, (S, LANES), 1)
        first = (lane & (HEAD // 2)) == 0
        cos, sin = cos_ref[...], sin_ref[...]
        q = q_ref[...]
        k = k_ref[...]
        qr_ref[...] = (q * cos + _rot_half(q, first) * sin) * (HEAD ** -0.5)
        kr_ref[...] = k * cos + _rot_half(k, first) * sin
        hm0 = lax.broadcasted_iota(jnp.int32, (BLK, LANES), 1) < HEAD
        top = lax.broadcasted_iota(jnp.int32, (2 * BLK, LANES), 0) < BLK
        ones2 = (top == (lax.broadcasted_iota(jnp.int32, (2 * BLK, LANES), 1) < HEAD)).astype(BF16)
        mc2, mp2 = _pair_masks()

        for pi, d in enumerate(PATTERNS):
            nb = S // d // BLK
            _deint(qr_ref, qd, d)
            _deint_heads(kr_ref, kd0, kd1, d)
            _deint_heads(v_ref, vd0, vd1, d)

            def blk(b, carry):
                st = pl.multiple_of(b * BLK, BLK)
                qb = qd[pl.ds(st, BLK), :]
                sc = jnp.where(mc2, _dot_nt(qb, _two(kd0, kd1, st, 0)), NEG)
                mx = sc
                if nb > 1:
                    stp = pl.multiple_of(jnp.maximum(b - 1, 0) * BLK, BLK)
                    mp = jnp.logical_and(mp2, lax.rem(b, nb) != 0)
                    sp = jnp.where(mp, _dot_nt(qb, _two(kd0, kd1, stp, 0)), NEG)
                    mx = jnp.maximum(sc, sp)
                m0 = jnp.max(mx[:, 0:BLK], axis=1, keepdims=True)
                m1 = jnp.max(mx[:, BLK:2 * BLK], axis=1, keepdims=True)
                mf = jnp.concatenate([jnp.broadcast_to(m0, (BLK, BLK)), jnp.broadcast_to(m1, (BLK, BLK))], axis=1)
                o = _dot(jnp.exp(sc - mf).astype(BF16), jnp.concatenate([_two(vd0, vd1, st, 0), ones2], axis=1))
                if nb > 1:
                    o = o + _dot(jnp.exp(sp - mf).astype(BF16), jnp.concatenate([_two(vd0, vd1, stp, 0), ones2], axis=1))
                l = o[:, LANES:2 * LANES]
                od[pl.ds(st, BLK), :] = o[:, 0:LANES] / l
                ld[pl.ds(st, BLK), :] = jnp.where(hm0, m0, m1) + jnp.log(l)
                return carry

            lax.fori_loop(0, S // BLK, blk, 0, unroll=ATT_UNROLL)
            _reint(od, on.at[pi], d, False)
            _reint(ld, ln.at[pi], d, False)

        l0, l1, l2 = ln[0], ln[1], ln[2]
        m = jnp.maximum(jnp.maximum(l0, l1), l2)
        e0, e1, e2 = jnp.exp(l0 - m), jnp.exp(l1 - m), jnp.exp(l2 - m)
        den = e0 + e1 + e2
        att_ref[...] = (e0 * on[0] + e1 * on[1] + e2 * on[2]) / den
        lse_ref[...] = m + jnp.log(den)

        @pl.when(pl.program_id(0) == NPAIR - 1)
        def _():
            wg.finish()
            wbf_ref[...] = wbuf[...]

    col = lambda c0: pl.BlockSpec((S, LANES), lambda p: (0, c0 + p))
    out = pl.BlockSpec((S, LANES), lambda p: (0, p))
    tab = pl.BlockSpec((S, LANES), lambda p: (0, 0))
    vm = pl.BlockSpec(memory_space=pltpu.VMEM)
    return pl.pallas_call(
        body, name="att_fwd", grid=(NPAIR,),
        in_specs=[col(QB), col(KB), col(VB), tab, tab, vm],
        out_specs=[out, out, out, out, vm],
        out_shape=[jax.ShapeDtypeStruct((S, R), F32)] * 4 + [jax.ShapeDtypeStruct((NCHIP,) + w_out.shape, BF16)],
        scratch_shapes=[pltpu.VMEM((S, LANES), BF16)] * 5 + [pltpu.VMEM((S, LANES), F32)] * 2
        + [pltpu.VMEM((3, S, LANES), F32)] * 2 + [pltpu.VMEM((NCHIP,) + w_out.shape, BF16)] + _WeightGather.SEMS,
        compiler_params=_cp(("arbitrary",)),
    )(proj, proj, proj, cos, sin, w_out)


def _att_bwd(dproj, d_att, att, lse, qr, kr, proj, cos, sin, gw_out4):
    out_units = [(j, j, 0) for j in range(NCHIP)]

    def body(dp_in, do_ref, o_ref, lse_ref, qr_ref, kr_ref, v_ref, cos_ref, sin_ref, gw_ref, dp_ref, gout_ref,
             qd, kd0, kd1, vd0, vd1, dod, lb0d, lb1d, dl0d, dl1d, dqd, dkcd, dkpd, dvcd, dvpd,
             dqn, dkn, dvn, lb0n, lb1n, dl0n, dl1n, stage, sems, gred, *rs_scratch):
        p = pl.program_id(0)
        rs = _ReduceScatter(gw_ref, gred, out_units, *rs_scratch)
        for step, piece in enumerate((rs.start_halves, rs.send_partials, rs.reduce_owned)):
            pl.when(p == step)(piece)

        @pl.when(p == NPAIR - 1)
        def _():
            rs.finish()
            gout_ref[...] = gred[...]
        hms = lax.broadcasted_iota(jnp.int32, (S, LANES), 1) < HEAD
        prod = do_ref[...] * o_ref[...]
        dl0n[...] = jnp.broadcast_to(jnp.sum(jnp.where(hms, prod, 0.0), axis=1, keepdims=True), (S, LANES))
        dl1n[...] = jnp.broadcast_to(jnp.sum(jnp.where(hms, 0.0, prod), axis=1, keepdims=True), (S, LANES))
        lse = lse_ref[...]
        lsw = pltpu.roll(lse, HEAD, 1)
        lb0n[...] = jnp.where(hms, lse, lsw)
        lb1n[...] = jnp.where(hms, lsw, lse)
        dqn[...] = jnp.zeros_like(dqn)
        dkn[...] = jnp.zeros_like(dkn)
        dvn[...] = jnp.zeros_like(dvn)
        hm0 = lax.broadcasted_iota(jnp.int32, (BLK, LANES), 1) < HEAD
        mc2, mp2 = _pair_masks()

        for d in PATTERNS:
            nb = S // d // BLK
            _deint(qr_ref, qd, d)
            _deint_heads(kr_ref, kd0, kd1, d)
            _deint_heads(v_ref, vd0, vd1, d)
            _deint(do_ref, dod, d)
            for src, dst in ((lb0n, lb0d), (lb1n, lb1d), (dl0n, dl0d), (dl1n, dl1d)):
                _deint(src, dst, d)

            def blk(b, carry):
                st = pl.multiple_of(b * BLK, BLK)
                qb, dob = qd[pl.ds(st, BLK), :], dod[pl.ds(st, BLK), :]
                lb, dl = _two(lb0d, lb1d, st, 1), _two(dl0d, dl1d, st, 1)

                def side(stk, mask):
                    k2, v2 = _two(kd0, kd1, stk, 0), _two(vd0, vd1, stk, 0)
                    pk = jnp.where(mask, jnp.exp(_dot_nt(qb, k2) - lb), 0.0)
                    ds = (pk * (_dot_nt(dob, v2) - dl)).astype(BF16)
                    rk, rv = _dot_tn(ds, qb), _dot_tn(pk.astype(BF16), dob)
                    return (_dot(ds, k2), jnp.where(hm0, rk[0:BLK], rk[BLK:2 * BLK]),
                            jnp.where(hm0, rv[0:BLK], rv[BLK:2 * BLK]))

                dq, dkc, dvc = side(st, mc2)
                if nb > 1:
                    stp = pl.multiple_of(jnp.maximum(b - 1, 0) * BLK, BLK)
                    dqp, dkp, dvp = side(stp, jnp.logical_and(mp2, lax.rem(b, nb) != 0))
                    dq = dq + dqp
                    dkpd[pl.ds(st, BLK), :] = dkp
                    dvpd[pl.ds(st, BLK), :] = dvp
                dqd[pl.ds(st, BLK), :] = dq
                dkcd[pl.ds(st, BLK), :] = dkc
                dvcd[pl.ds(st, BLK), :] = dvc
                return carry

            lax.fori_loop(0, S // BLK, blk, 0, unroll=ATT_UNROLL)
            _reint(dqd, dqn, d, True)
            _reint(dkcd, dkn, d, True)
            _reint(dvcd, dvn, d, True)
            _reint_prev(dkpd, dkn, d)
            _reint_prev(dvpd, dvn, d)

        lane = lax.broadcasted_iota(jnp.int32, (S, LANES), 1)
        first = (lane & (HEAD // 2)) == 0
        cos, sin = cos_ref[...], sin_ref[...]
        dq = dqn[...] * (HEAD ** -0.5)
        dk = dkn[...]
        stage[0] = (dq * cos - _rot_half(dq, first) * sin).astype(BF16)
        stage[1] = (dk * cos - _rot_half(dk, first) * sin).astype(BF16)
        stage[2] = dvn[...].astype(BF16)
        copies = [pltpu.make_async_copy(stage.at[j], dp_ref.at[:, pl.ds((2 + j) * R + p * LANES, LANES)], sems.at[j])
                  for j in range(3)]
        for cp in copies:
            cp.start()
        for cp in copies:
            cp.wait()

    blk = pl.BlockSpec((S, LANES), lambda p: (0, p))
    tab = pl.BlockSpec((S, LANES), lambda p: (0, 0))
    vm = pl.BlockSpec(memory_space=pltpu.VMEM)
    _, orows, ocols = gw_out4.shape
    return pl.pallas_call(
        body, name="att_bwd", grid=(NPAIR,),
        in_specs=[pl.BlockSpec(memory_space=pl.ANY), blk, blk, blk, blk, blk,
                  pl.BlockSpec((S, LANES), lambda p: (0, VB + p)), tab, tab, vm],
        out_specs=[pl.BlockSpec(memory_space=pl.ANY), vm],
        out_shape=[jax.ShapeDtypeStruct((S, E), BF16), jax.ShapeDtypeStruct((orows, ocols), F32)],
        scratch_shapes=[pltpu.VMEM((S, LANES), BF16)] * 6 + [pltpu.VMEM((S, LANES), F32)] * 16
        + [pltpu.VMEM((3, S, LANES), BF16), pltpu.SemaphoreType.DMA((3,)), pltpu.VMEM((orows, ocols), F32)]
        + _ReduceScatter.scratch(NCHIP, orows, ocols, 1),
        input_output_aliases={0: 0},
        compiler_params=_cp(("arbitrary",)),
    )(dproj, d_att, att, lse, qr, kr, proj, cos, sin, gw_out4)


def _att_fwd_old(proj, pos, freq):
    def body(q_ref, k_ref, v_ref, pos_ref, freq_ref, att_ref, qr_ref, kr_ref, lse_ref,
             qd, kd, vd, od, ld, on, ln):
        lane = lax.broadcasted_iota(jnp.int32, (S, LANES), 1)
        first = (lane & (HEAD // 2)) == 0
        cos, sin = _cos_sin(pos_ref, freq_ref)
        q = q_ref[...]
        k = k_ref[...]
        qr_ref[...] = (q * cos + _rot_half(q, first) * sin) * (HEAD ** -0.5)
        kr_ref[...] = k * cos + _rot_half(k, first) * sin
        hm0 = lax.broadcasted_iota(jnp.int32, (BLK, LANES), 1) < HEAD

        for pi, d in enumerate(PATTERNS):
            nb = S // d // BLK
            _deint(qr_ref, qd, d)
            _deint(kr_ref, kd, d)
            _deint(v_ref, vd, d)

            def blk(b, carry):
                st = pl.multiple_of(b * BLK, BLK)
                stp = pl.multiple_of(jnp.maximum(b - 1, 0) * BLK, BLK)
                mc, mp = _blk_masks(b, nb)
                qb = qd[pl.ds(st, BLK), :]
                kc, kp = kd[pl.ds(st, BLK), :], kd[pl.ds(stp, BLK), :]
                vc, vp = vd[pl.ds(st, BLK), :], vd[pl.ds(stp, BLK), :]
                outs, lses = [], []
                for hm in (hm0, jnp.logical_not(hm0)):
                    qm = jnp.where(hm, qb, jnp.zeros_like(qb))
                    sc = jnp.where(mc, _dot_nt(qm, kc), NEG)
                    sp = jnp.where(mp, _dot_nt(qm, kp), NEG)
                    m = jnp.maximum(jnp.max(sc, axis=1, keepdims=True), jnp.max(sp, axis=1, keepdims=True))
                    pc, pp = jnp.exp(sc - m), jnp.exp(sp - m)
                    l = jnp.sum(pc, axis=1, keepdims=True) + jnp.sum(pp, axis=1, keepdims=True)
                    o = _dot(pc.astype(BF16), vc) + _dot(pp.astype(BF16), vp)
                    outs.append(o / l)
                    lses.append(m + jnp.log(l))
                od[pl.ds(st, BLK), :] = jnp.where(hm0, outs[0], outs[1])
                ld[pl.ds(st, BLK), :] = jnp.where(hm0, lses[0], lses[1])
                return carry

            lax.fori_loop(0, S // BLK, blk, 0)
            _reint(od, on.at[pi], d, False)
            _reint(ld, ln.at[pi], d, False)

        l0, l1, l2 = ln[0], ln[1], ln[2]
        m = jnp.maximum(jnp.maximum(l0, l1), l2)
        e0, e1, e2 = jnp.exp(l0 - m), jnp.exp(l1 - m), jnp.exp(l2 - m)
        den = e0 + e1 + e2
        att_ref[...] = (e0 * on[0] + e1 * on[1] + e2 * on[2]) / den
        lse_ref[...] = m + jnp.log(den)

    col = lambda c0: pl.BlockSpec((S, LANES), lambda p: (0, c0 + p))
    out = pl.BlockSpec((S, LANES), lambda p: (0, p))
    return pl.pallas_call(
        body, name="att_fwd", grid=(NPAIR,),
        in_specs=[col(QB), col(KB), col(VB), pl.BlockSpec((S, 1), lambda p: (0, 0)),
                  pl.BlockSpec((1, LANES), lambda p: (0, 0))],
        out_specs=[out, out, out, out],
        out_shape=[jax.ShapeDtypeStruct((S, R), F32)] * 4,
        scratch_shapes=[pltpu.VMEM((S, LANES), BF16)] * 3 + [pltpu.VMEM((S, LANES), F32)] * 2
        + [pltpu.VMEM((3, S, LANES), F32)] * 2,
        compiler_params=_cp(("parallel",)),
    )(proj, proj, proj, pos, freq)


def _att_bwd_old(dproj, d_att, att, lse, qr, kr, proj, pos, freq):
    def body(dp_in, do_ref, o_ref, lse_ref, qr_ref, kr_ref, v_ref, pos_ref, freq_ref, dp_ref,
             qd, kd, vd, dod, lsd, prd, dqd, dkd, dvd, dqn, dkn, dvn, prn, stage, sems):
        p = pl.program_id(0)
        prn[...] = do_ref[...] * o_ref[...]
        dqn[...] = jnp.zeros_like(dqn)
        dkn[...] = jnp.zeros_like(dkn)
        dvn[...] = jnp.zeros_like(dvn)
        hm0 = lax.broadcasted_iota(jnp.int32, (BLK, LANES), 1) < HEAD

        for d in PATTERNS:
            nb = S // d // BLK
            _deint(qr_ref, qd, d)
            _deint(kr_ref, kd, d)
            _deint(v_ref, vd, d)
            _deint(do_ref, dod, d)
            _deint(lse_ref, lsd, d)
            _deint(prn, prd, d)
            dkd[...] = jnp.zeros_like(dkd)
            dvd[...] = jnp.zeros_like(dvd)

            def blk(b, carry):
                st = pl.multiple_of(b * BLK, BLK)
                stp = pl.multiple_of(jnp.maximum(b - 1, 0) * BLK, BLK)
                mc, mp = _blk_masks(b, nb)
                qb, dob = qd[pl.ds(st, BLK), :], dod[pl.ds(st, BLK), :]
                kc, kp = kd[pl.ds(st, BLK), :], kd[pl.ds(stp, BLK), :]
                vc, vp = vd[pl.ds(st, BLK), :], vd[pl.ds(stp, BLK), :]
                lsb, prb = lsd[pl.ds(st, BLK), :], prd[pl.ds(st, BLK), :]
                dqs = []
                dkc = dkp = dvc = dvp = None
                for hm in (hm0, jnp.logical_not(hm0)):
                    qm = jnp.where(hm, qb, jnp.zeros_like(qb))
                    dom = jnp.where(hm, dob, jnp.zeros_like(dob))
                    lh = jnp.max(jnp.where(hm, lsb, -3e38), axis=1, keepdims=True)
                    delta = jnp.sum(jnp.where(hm, prb, 0.0), axis=1, keepdims=True)
                    pc = jnp.where(mc, jnp.exp(_dot_nt(qm, kc) - lh), 0.0)
                    pp = jnp.where(mp, jnp.exp(_dot_nt(qm, kp) - lh), 0.0)
                    dsc = (pc * (_dot_nt(dom, vc) - delta)).astype(BF16)
                    dsp = (pp * (_dot_nt(dom, vp) - delta)).astype(BF16)
                    dqs.append(_dot(dsc, kc) + _dot(dsp, kp))
                    acc = lambda t, n: n if t is None else t + n
                    dkc, dkp = acc(dkc, _dot_tn(dsc, qm)), acc(dkp, _dot_tn(dsp, qm))
                    dvc, dvp = acc(dvc, _dot_tn(pc.astype(BF16), dom)), acc(dvp, _dot_tn(pp.astype(BF16), dom))
                dqd[pl.ds(st, BLK), :] = jnp.where(hm0, dqs[0], dqs[1])
                dkd[pl.ds(stp, BLK), :] += dkp
                dvd[pl.ds(stp, BLK), :] += dvp
                dkd[pl.ds(st, BLK), :] += dkc
                dvd[pl.ds(st, BLK), :] += dvc
                return carry

            lax.fori_loop(0, S // BLK, blk, 0)
            _reint(dqd, dqn, d, True)
            _reint(dkd, dkn, d, True)
            _reint(dvd, dvn, d, True)

        lane = lax.broadcasted_iota(jnp.int32, (S, LANES), 1)
        first = (lane & (HEAD // 2)) == 0
        cos, sin = _cos_sin(pos_ref, freq_ref)
        dq = dqn[...] * (HEAD ** -0.5)
        dk = dkn[...]
        stage[0] = (dq * cos - _rot_half(dq, first) * sin).astype(BF16)
        stage[1] = (dk * cos - _rot_half(dk, first) * sin).astype(BF16)
        stage[2] = dvn[...].astype(BF16)
        copies = [pltpu.make_async_copy(stage.at[j], dp_ref.at[:, pl.ds((2 + j) * R + p * LANES, LANES)], sems.at[j])
                  for j in range(3)]
        for cp in copies:
            cp.start()
        for cp in copies:
            cp.wait()

    blk = pl.BlockSpec((S, LANES), lambda p: (0, p))
    return pl.pallas_call(
        body, name="att_bwd", grid=(NPAIR,),
        in_specs=[pl.BlockSpec(memory_space=pl.ANY), blk, blk, blk, blk, blk,
                  pl.BlockSpec((S, LANES), lambda p: (0, VB + p)), pl.BlockSpec((S, 1), lambda p: (0, 0)),
                  pl.BlockSpec((1, LANES), lambda p: (0, 0))],
        out_specs=pl.BlockSpec(memory_space=pl.ANY),
        out_shape=jax.ShapeDtypeStruct((S, E), BF16),
        scratch_shapes=[pltpu.VMEM((S, LANES), BF16)] * 4 + [pltpu.VMEM((S, LANES), F32)] * 9
        + [pltpu.VMEM((3, S, LANES), BF16), pltpu.SemaphoreType.DMA((3,))],
        input_output_aliases={0: 0},
        compiler_params=_cp(("arbitrary",)),
    )(dproj, d_att, att, lse, qr, kr, proj, pos, freq)


def _out_fwd_bwd(ya, att, proj, w_out_bf, x, target, mod, norm_post, norm_att):
    ts = 256

    def body(ya_ref, att_ref, gb_ref, w_ref, x_ref, t_ref, mod_ref, npost_ref, natt_ref,
             gx_ref, dya_ref, datt_ref, dgb_ref, gw_ref, acc_ref):
        i = pl.program_id(0)

        @pl.when(i == 0)
        def _():
            gw_ref[...] = jnp.zeros_like(gw_ref)
            acc_ref[...] = jnp.zeros_like(acc_ref)

        gate = mod_ref[:, 2 * D:3 * D]
        att = att_ref[...]
        gb = gb_ref[...]
        sg = _sigmoid(gb)
        silu = gb * sg
        ybp = att * silu
        yb, ybn, rstd_b = _rms_fwd(ybp, natt_ref[...])
        cat = jnp.concatenate([ya_ref[...], yb.astype(BF16)], axis=1)
        mix = _dot(cat, w_ref[...])
        rn, mn, rstd_m = _rms_fwd(mix, npost_ref[...])
        err = x_ref[...] + gate * rn - t_ref[...]
        dy = err * (1.0 / D)
        gx_ref[...] = dy
        dmix, dnpost = _rms_bwd(dy * gate, mn, rstd_m, npost_ref[...])
        dmb = dmix.astype(BF16)
        gw_ref[...] += _dot_tn(cat, dmb)
        dcat = _dot_nt(dmb, w_ref[...])
        dya_ref[...] = dcat[:, 0:R]
        dybp, dnatt = _rms_bwd(dcat[:, R:2 * R], ybn, rstd_b, natt_ref[...])
        datt_ref[...] = dybp * silu
        dgb_ref[...] = (dybp * att * (sg * (1.0 + gb * (1.0 - sg)))).astype(BF16)
        acc_ref[0:1, :] += jnp.sum(dy * rn, axis=0, keepdims=True)
        acc_ref[1:2, :] += dnpost
        acc_ref[2:3, 0:R] += dnatt
        acc_ref[3:4, :] += jnp.sum(jnp.sum(err * err, axis=1, keepdims=True), axis=0, keepdims=True)

    tile = lambda w: pl.BlockSpec((ts, w), lambda i: (i, 0))
    c0 = lambda shape: pl.BlockSpec(shape, lambda i: (0, 0))
    return pl.pallas_call(
        body, name="out_fwd_bwd", grid=(S // ts,),
        in_specs=[tile(R), tile(R), pl.BlockSpec((ts, R), lambda i: (i, 5)), c0((D, D)), tile(D), tile(D),
                  c0((1, 3 * D)), c0((1, D)), c0((1, R))],
        out_specs=[tile(D), tile(R), tile(R), pl.BlockSpec((ts, R), lambda i: (i, 5)), c0((D, D)), c0((8, D))],
        out_shape=[jax.ShapeDtypeStruct((S, D), F32), jax.ShapeDtypeStruct((S, R), F32),
                   jax.ShapeDtypeStruct((S, R), F32), jax.ShapeDtypeStruct((S, E), BF16),
                   jax.ShapeDtypeStruct((D, D), F32), jax.ShapeDtypeStruct((8, D), F32)],
        compiler_params=_cp(("arbitrary",)),
    )(ya, att, proj, w_out_bf, x, target, mod, norm_post, norm_att)


UC = 256
UPC = EC // UC


NU = E // UC


def _unit_of_step(i):
    return (i % NCHIP) * UPC + i // NCHIP


def _in_proj_bwd(ht, dproj, w_in_bf, x, gx1, mod, norm_pre, smalls):
    ts = 256
    nt = S // ts
    half = D // 2
    units = [_unit_of_step(k) for k in range(NU)]
    owners = [u // UPC for u in units]
    ns = len(smalls)

    def body(*refs):
        (ht_ref, dpu_ref, dp_ref, w_ref, x_ref, gx1_ref, mod_ref, np_ref), refs = refs[:8], refs[8:]
        small_in, refs = refs[:ns], refs[ns:]
        (gx_ref, gin_ref), refs = refs[:2], refs[2:]
        small_out, (acc_out,), refs = refs[:ns], refs[ns:ns + 1], refs[ns + 1:]
        mine, sib, tmp, stage, got, red, acc_ref, hs, hr, ps, pr, bs, br = refs[:13]
        early = _SmallGather(small_in, small_out, *refs[13:16])
        late = _SmallGather([acc_ref], [acc_out], *refs[16:19])
        i = pl.program_id(0)
        xx, yy, c = _me()
        ci = 2 * xx + yy
        r0 = pl.multiple_of(c * half, half)
        r1 = pl.multiple_of((1 - c) * half, half)
        pl.when(i == 0)(early.start)
        pl.when(i == NU)(early.forward)

        def exch(k):
            return _remote(tmp.at[k % 2], sib.at[k], hs.at[k], hr.at[k], 1)

        def partial(k, sender):
            return pltpu.make_async_remote_copy(
                src_ref=stage.at[k], dst_ref=got.at[units[k] % UPC, sender], send_sem=ps.at[k],
                recv_sem=pr.at[k, sender], device_id=(owners[k] // 2, owners[k] % 2, c), device_id_type=MESH)

        def back(k, start):
            off = (units[k] % UPC) * UC
            blk = red.at[pl.ds(start, half), off:off + UC]
            return _remote(blk, blk, bs.at[k], br.at[k], 1)

        for k in range(NU + 1):
            @pl.when(i == k)
            def _():
                if k < NU:
                    if k >= 2:
                        exch(k - 2).wait_send()
                    dpu = dpu_ref[...]
                    tmp[k % 2] = _dot(ht_ref[pl.ds(r1, half), :], dpu)
                    exch(k).start()
                    mine[k] = _dot(ht_ref[pl.ds(r0, half), :], dpu)
                if k >= 1:
                    exch(k - 1).wait_recv()
                    mine[k - 1] += sib[k - 1]

                    @pl.when(ci != owners[k - 1])
                    def _():
                        stage[k - 1] = mine[k - 1].astype(BF16)
                        partial(k - 1, ci).start()

        @pl.when(i == NU)
        def _():
            acc_ref[...] = jnp.zeros_like(acc_ref)

        @pl.when(i >= NU)
        def _():
            dh = sum(_dot_nt(dp_ref[:, j * EC:(j + 1) * EC], w_ref[j]) for j in range(NCHIP))
            hp, xn, rstd = _rms_fwd(x_ref[...], np_ref[...])
            dx, dnp = _rms_bwd(dh * (1.0 + mod_ref[:, D:2 * D]), xn, rstd, np_ref[...])
            gx_ref[...] = gx1_ref[...] + dx
            acc_ref[0:1, :] += jnp.sum(dh, axis=0, keepdims=True)
            acc_ref[1:2, :] += jnp.sum(dh * hp, axis=0, keepdims=True)
            acc_ref[2:3, :] += dnp

        for t in range(UPC):
            @pl.when(i == NU + 1 + 2 * t)
            def _():
                for k in range(NCHIP * t, NCHIP * (t + 1)):
                    @pl.when(ci == owners[k])
                    def _():
                        off = (units[k] % UPC) * UC
                        red[pl.ds(r0, half), off:off + UC] = mine[k]
                        for s in range(NCHIP):
                            if s != owners[k]:
                                partial(k, s).wait_recv()
                                red[pl.ds(r0, half), off:off + UC] += got[units[k] % UPC, s].astype(F32)
                        back(k, r0).start()

        @pl.when(i == NU + nt - 1)
        def _():
            late.start()
            exch(NU - 2).wait_send()
            exch(NU - 1).wait_send()
            for k in range(NU):
                @pl.when(ci == owners[k])
                def _():
                    back(k, r1).wait_recv()
                    back(k, r0).wait_send()

                @pl.when(ci != owners[k])
                def _():
                    partial(k, ci).wait_send()
            gin_ref[...] = red[...]
            early.finish()
            late.forward()
            late.finish()

    tile = lambda w: pl.BlockSpec((ts, w), lambda i: (jnp.maximum(i - NU, 0), 0))
    c0 = lambda shape: pl.BlockSpec(shape, lambda i: (0, 0))
    vm = pl.BlockSpec(memory_space=pltpu.VMEM)
    hbm = pl.BlockSpec(memory_space=pl.ANY)
    gathered = [jax.ShapeDtypeStruct((NDEV,) + a.shape, F32) for a in smalls] + [jax.ShapeDtypeStruct((NDEV, 8, D), F32)]
    return pl.pallas_call(
        body, name="in_proj_bwd", grid=(NU + nt,),
        in_specs=[vm, pl.BlockSpec((S, UC), lambda i: (0, _unit_of_step(jnp.minimum(i, NU - 1)))), tile(E),
                  vm, tile(D), tile(D), c0((1, 3 * D)), c0((1, D))] + [vm] * ns,
        out_specs=[tile(D), vm] + [hbm] * (ns + 1),
        out_shape=[jax.ShapeDtypeStruct((S, D), F32), jax.ShapeDtypeStruct((D, EC), F32)] + gathered,
        scratch_shapes=[pltpu.VMEM((NU, half, UC), F32), pltpu.VMEM((NU, half, UC), F32),
                        pltpu.VMEM((2, half, UC), F32), pltpu.VMEM((NU, half, UC), BF16),
                        pltpu.VMEM((UPC, NCHIP, half, UC), BF16), pltpu.VMEM((D, EC), F32), pltpu.VMEM((8, D), F32),
                        pltpu.SemaphoreType.DMA((NU,)), pltpu.SemaphoreType.DMA((NU,)),
                        pltpu.SemaphoreType.DMA((NU,)), pltpu.SemaphoreType.DMA((NU, NCHIP)),
                        pltpu.SemaphoreType.DMA((NU,)), pltpu.SemaphoreType.DMA((NU,))]
        + _SmallGather.sems(ns) + _SmallGather.sems(1),
        compiler_params=_cp(("arbitrary",)),
    )(ht, dproj, dproj, w_in_bf, x, gx1, mod, norm_pre, *smalls)


def _block_diag(w):
    n, b, _ = w.shape
    eye = jnp.eye(n, dtype=w.dtype)
    return (eye[:, None, :, None] * w[:, :, None, :]).reshape(n * b, n * b)


def _diag_blocks(m):
    n, b = R // HEAD, HEAD
    return jnp.stack([m[h * b:(h + 1) * b, h * b:(h + 1) * b] for h in range(n)])


def _local_step(x, cos, sin, target, mod, w_in_bf, proj, ht, w_out, conv_w, p):
    rec_p = (conv_w, p["conv_b"], p["w_rg_a"], p["b_rg_a"], p["w_rg_x"], p["b_rg_x"], p["lru_lambda"], p["norm_rec"])
    h_all, ya = _rec_fwd(proj, *rec_p)
    att, qr, kr, lse, w_out_bf = _att_fwd(proj, cos, sin, w_out)
    gx1, d_ya, d_att, dproj, gw_out, acc_o = _out_fwd_bwd(ya, att, proj, w_out_bf.reshape(D, D), x, target, mod,
                                                           p["norm_post"], p["norm_att"])
    dproj, g_out = _att_bwd(dproj, d_att, att, lse, qr, kr, proj, cos, sin, gw_out.reshape(NCHIP, D // NCHIP, D))
    dproj, dwa, dwx, sm = _rec_bwd(dproj, d_ya, proj, h_all, *rec_p)
    grad_x, g_in, *gathered = _in_proj_bwd(ht, dproj, w_in_bf, x, gx1, mod, p["norm_pre"], [acc_o, sm, dwa, dwx])
    return grad_x, g_in, g_out, gathered


def _me():
    return lax.axis_index("x"), lax.axis_index("y"), lax.axis_index("c")


def _flip(v, bit):
    return 1 - v if bit else v


def _peer(rel):
    x, y, c = _me()
    return (_flip(x, rel & 4), _flip(y, rel & 2), _flip(c, rel & 1))


def _remote(src, dst, send_sem, recv_sem, rel):
    return pltpu.make_async_remote_copy(src_ref=src, dst_ref=dst, send_sem=send_sem, recv_sem=recv_sem,
                                        device_id=_peer(rel), device_id_type=MESH)


def _allgather_rows(row, name):
    w = row.shape[1]

    def body(row_ref, out_ref, send_sems, recv_sems, local_sem):
        x, y, c = _me()
        me = 4 * x + 2 * y + c
        mine = pltpu.make_async_copy(row_ref, out_ref.at[pl.ds(me, 1), :], local_sem)
        mine.start()
        sends = [_remote(row_ref, out_ref.at[pl.ds(me, 1), :], send_sems.at[r - 1], recv_sems.at[r - 1], r)
                 for r in range(1, NDEV)]
        for cp in sends:
            cp.start()
        for r in range(1, NDEV):
            px, py, pc = _peer(r)
            src = 4 * px + 2 * py + pc
            _remote(row_ref, out_ref.at[pl.ds(src, 1), :], send_sems.at[r - 1], recv_sems.at[r - 1], r).wait_recv()
        for cp in sends:
            cp.wait_send()
        mine.wait()

    return pl.pallas_call(
        body, name=name,
        in_specs=[pl.BlockSpec(memory_space=pltpu.VMEM)],
        out_specs=pl.BlockSpec(memory_space=pltpu.VMEM),
        out_shape=jax.ShapeDtypeStruct((NDEV, w), row.dtype),
        scratch_shapes=[pltpu.SemaphoreType.DMA((NDEV - 1,)), pltpu.SemaphoreType.DMA((NDEV - 1,)),
                        pltpu.SemaphoreType.DMA],
        compiler_params=pltpu.CompilerParams(vmem_limit_bytes=VMEM_LIMIT),
    )(row)


class _WeightGather:
    SEMS = [pltpu.SemaphoreType.DMA((NCHIP - 1,))] * 4

    def __init__(self, w_ref, out_ref, send_sems, recv_sems, fsend_sems, frecv_sems):
        x, y, c = _me()
        self.w, self.out, self.ci = w_ref, out_ref, 2 * x + y
        self.half = w_ref.shape[0] // 2
        self.r0 = pl.multiple_of(c * self.half, self.half)
        self.r1 = pl.multiple_of((1 - c) * self.half, self.half)
        self.sems = (send_sems, recv_sems, fsend_sems, frecv_sems)

    def _ici(self, chip, k):
        blk = self.out.at[chip, pl.ds(self.r0, self.half), :]
        return _remote(blk, blk, self.sems[0].at[k - 1], self.sems[1].at[k - 1], 2 * k)

    def _d2d(self, chip, start, k):
        blk = self.out.at[chip, pl.ds(start, self.half), :]
        return _remote(blk, blk, self.sems[2].at[k - 1], self.sems[3].at[k - 1], 1)

    def start(self, diagonal=True):
        self.out[self.ci] = self.w[...].astype(BF16)
        for k in range(1, NCHIP if diagonal else NCHIP - 1):
            self._ici(self.ci, k).start()

    def start_diagonal(self):
        for k in range(1, NCHIP - 1):
            self._ici(self.ci, k).wait_send()
        self._ici(self.ci, NCHIP - 1).start()

    def forward(self):
        for k in range(1, NCHIP):
            self._ici(self.ci ^ k, k).wait_recv()
            self._d2d(self.ci ^ k, self.r0, k).start()

    def finish(self):
        for k in range(1, NCHIP):
            self._d2d(self.ci ^ k, self.r1, k).wait_recv()
        self.finish_sends()

    def arrive(self, k):
        self._ici(self.ci ^ k, k).wait_recv()
        self._d2d(self.ci ^ k, self.r0, k).start()
        self._d2d(self.ci ^ k, self.r1, k).wait_recv()

    def finish_sends(self, after_start_diagonal=False):
        for k in range(1, NCHIP):
            if not (after_start_diagonal and k < NCHIP - 1):
                self._ici(self.ci, k).wait_send()
            self._d2d(self.ci ^ k, self.r0, k).wait_send()


class _SmallGather:
    @staticmethod
    def sems(n):
        return [pltpu.SemaphoreType.DMA((n, 7)), pltpu.SemaphoreType.DMA((n, 7)), pltpu.SemaphoreType.DMA((n,))]

    def __init__(self, srcs, outs, send_sems, recv_sems, local_sems):
        x, y, c = _me()
        self.srcs, self.outs = list(srcs), list(outs)
        self.ss, self.rs, self.ls = send_sems, recv_sems, local_sems
        self.ci, self.c = 2 * x + y, c
        self.me = 2 * self.ci + c

    def _own(self, a, slot, rel):
        return _remote(self.srcs[a], self.outs[a].at[self.me], self.ss.at[a, slot], self.rs.at[a, slot], rel)

    def _block(self, a, idx, slot, rel):
        blk = self.outs[a].at[idx]
        return _remote(blk, blk, self.ss.at[a, slot], self.rs.at[a, slot], rel)

    def _local(self, a):
        return pltpu.make_async_copy(self.srcs[a], self.outs[a].at[self.me], self.ls.at[a])

    def start(self):
        for a in range(len(self.srcs)):
            self._local(a).start()
            self._own(a, 0, 1).start()
            for k in range(1, NCHIP):
                self._own(a, k, 2 * k).start()

    def forward(self):
        for a in range(len(self.srcs)):
            for k in range(1, NCHIP):
                idx = 2 * (self.ci ^ k) + self.c
                self._block(a, idx, k, 2 * k).wait_recv()
                self._block(a, idx, 3 + k, 1).start()

    def finish(self):
        for a in range(len(self.srcs)):
            self._block(a, 2 * self.ci + 1 - self.c, 0, 1).wait_recv()
            for k in range(1, NCHIP):
                self._block(a, 2 * (self.ci ^ k) + 1 - self.c, 3 + k, 1).wait_recv()
            self._own(a, 0, 1).wait_send()
            for k in range(1, NCHIP):
                self._own(a, k, 2 * k).wait_send()
                self._block(a, 2 * (self.ci ^ k) + self.c, 3 + k, 1).wait_send()
            self._local(a).wait()


def _start_in_proj(crow, w_ada, b_cols, w_in, pos, x, norm_pre, order):
    ts = 256
    nt = S // ts
    wc = crow.shape[1]

    def body(order_ref, crow_ref, wada_ref, b_ref, win_ref, pos_ref, freq_ref, x_ref, np_ref,
             g0_ref, mod_ref, wbf_ref, cos_ref, sin_ref, proj_ref, ht_ref,
             g0s, modp, modb, wbuf, hb_all, cs, cr, ms, mr, ws, wr, fs, fr, local_sems):
        s, t = pl.program_id(0), pl.program_id(1)
        x, y, c = _me()
        ci = 2 * x + y
        me = 2 * ci + c
        wg = _WeightGather(win_ref, wbuf, ws, wr, fs, fr)

        @pl.when(jnp.logical_and(s == 0, t == 0))
        def _():
            wg.start(diagonal=False)
            mine = pltpu.make_async_copy(crow_ref, g0s.at[pl.ds(me, 1), :], local_sems.at[0])
            mine.start()
            csend = [_remote(crow_ref, g0s.at[pl.ds(me, 1), :], cs.at[r - 1], cr.at[r - 1], r) for r in range(1, NDEV)]
            for cp in csend:
                cp.start()
            cos_ref[...], sin_ref[...] = _cos_sin(pos_ref, freq_ref)
            for r in range(1, NDEV):
                px, py, pc = _peer(r)
                _remote(crow_ref, g0s.at[pl.ds(4 * px + 2 * py + pc, 1), :], cs.at[r - 1], cr.at[r - 1], r).wait_recv()
            mine.wait()
            cv = g0s[:, 0:D]
            sc = cv * _sigmoid(cv)
            scb = jnp.concatenate([sc, jnp.zeros_like(sc)], axis=0).astype(BF16)
            modp[...] = _dot(scb, wada_ref[...].astype(BF16))[0:NDEV, :] + b_ref[...]
            own = pltpu.make_async_copy(modp.at[pl.ds(me, 1), :], modb.at[ci], local_sems.at[1])
            own.start()
            msend = []
            for k in range(1, NCHIP):
                cp = _remote(modp.at[pl.ds(2 * (ci ^ k) + c, 1), :], modb.at[ci], ms.at[k - 1], mr.at[k - 1], 2 * k)
                cp.start()
                msend.append(cp)
            for k in range(1, NCHIP):
                _remote(modp.at[pl.ds(me, 1), :], modb.at[ci ^ k], ms.at[k - 1], mr.at[k - 1], 2 * k).wait_recv()
            own.wait()
            for j in range(NCHIP):
                mod_ref[:, j * EC:(j + 1) * EC] = modb[j]
            for cp in csend + msend:
                cp.wait_send()
            g0_ref[...] = g0s[...]

        for k in range(1, NCHIP):
            @pl.when(jnp.logical_and(s == k, t == 0))
            def _():
                if k == 1:
                    wg.start_diagonal()
                wg.arrive(k)

        rows = pl.ds(pl.multiple_of(t * ts, ts), ts)

        @pl.when(s == 0)
        def _():
            hp, _, _ = _rms_fwd(x_ref[...], np_ref[...])
            h = hp * (1.0 + mod_ref[:, D:2 * D]) + mod_ref[:, 0:D]
            hb_all[rows, :] = h.astype(BF16)
            ht_ref[...] = h.T.astype(BF16)

        proj_ref[...] = _dot(hb_all[rows, :], wbuf[ci ^ s])

        @pl.when(jnp.logical_and(s == NCHIP - 1, t == nt - 1))
        def _():
            wg.finish_sends(after_start_diagonal=True)
            wbf_ref[...] = wbuf[...]

    vm = pl.BlockSpec(memory_space=pltpu.VMEM)
    first_pass = lambda s, t: jnp.where(s == 0, t, nt - 1)
    grid_spec = pltpu.PrefetchScalarGridSpec(
        num_scalar_prefetch=1, grid=(NCHIP, nt),
        in_specs=[vm, vm, vm, vm, vm, vm, pl.BlockSpec((ts, D), lambda s, t, o: (first_pass(s, t), 0)),
                  pl.BlockSpec((1, D), lambda s, t, o: (0, 0))],
        out_specs=[vm, vm, vm, vm, vm, pl.BlockSpec((ts, EC), lambda s, t, o: (t, o[s])),
                   pl.BlockSpec((D, ts), lambda s, t, o: (0, first_pass(s, t)))],
        scratch_shapes=[pltpu.VMEM((NDEV, wc), F32), pltpu.VMEM((NDEV, EC), F32), pltpu.VMEM((NCHIP, 1, EC), F32),
                        pltpu.VMEM((NCHIP, D, EC), BF16), pltpu.VMEM((S, D), BF16),
                        pltpu.SemaphoreType.DMA((NDEV - 1,)), pltpu.SemaphoreType.DMA((NDEV - 1,)),
                        pltpu.SemaphoreType.DMA((NCHIP - 1,)), pltpu.SemaphoreType.DMA((NCHIP - 1,))]
        + _WeightGather.SEMS + [pltpu.SemaphoreType.DMA((2,))])
    return pl.pallas_call(
        body, name="start_in_proj", grid_spec=grid_spec,
        out_shape=[jax.ShapeDtypeStruct((NDEV, wc), F32), jax.ShapeDtypeStruct((1, 3 * D), F32),
                   jax.ShapeDtypeStruct((NCHIP, D, EC), BF16), jax.ShapeDtypeStruct((S, LANES), F32),
                   jax.ShapeDtypeStruct((S, LANES), F32), jax.ShapeDtypeStruct((S, E), F32),
                   jax.ShapeDtypeStruct((D, S), BF16)],
        compiler_params=_cp(("arbitrary", "arbitrary")),
    )(order, crow, w_ada, b_cols, w_in, pos, _rope_freq(), x, norm_pre)


def _start_gather(crow, w_ada, b_cols, w_in):
    wc = crow.shape[1]

    def body(crow_ref, wada_ref, b_ref, win_ref, g0_ref, mod_ref, wbf_ref,
             modp, modb, cs, cr, ms, mr, ws, wr, fs, fr, local_sems):
        x, y, c = _me()
        ci = 2 * x + y
        me = 2 * ci + c
        wg = _WeightGather(win_ref, wbf_ref, ws, wr, fs, fr)
        mine = pltpu.make_async_copy(crow_ref, g0_ref.at[pl.ds(me, 1), :], local_sems.at[0])
        mine.start()
        csend = [_remote(crow_ref, g0_ref.at[pl.ds(me, 1), :], cs.at[r - 1], cr.at[r - 1], r) for r in range(1, NDEV)]
        for cp in csend:
            cp.start()
        wg.start()
        for r in range(1, NDEV):
            px, py, pc = _peer(r)
            _remote(crow_ref, g0_ref.at[pl.ds(4 * px + 2 * py + pc, 1), :], cs.at[r - 1], cr.at[r - 1], r).wait_recv()
        mine.wait()
        cv = g0_ref[:, 0:D]
        sc = cv * _sigmoid(cv)
        scb = jnp.concatenate([sc, jnp.zeros_like(sc)], axis=0).astype(BF16)
        modp[...] = _dot(scb, wada_ref[...].astype(BF16))[0:NDEV, :] + b_ref[...]
        own = pltpu.make_async_copy(modp.at[pl.ds(me, 1), :], modb.at[ci], local_sems.at[1])
        own.start()
        msend = []
        for k in range(1, NCHIP):
            dst = 2 * (ci ^ k) + c
            cp = _remote(modp.at[pl.ds(dst, 1), :], modb.at[ci], ms.at[k - 1], mr.at[k - 1], 2 * k)
            cp.start()
            msend.append(cp)
        for k in range(1, NCHIP):
            _remote(modp.at[pl.ds(me, 1), :], modb.at[ci ^ k], ms.at[k - 1], mr.at[k - 1], 2 * k).wait_recv()
        own.wait()
        for j in range(NCHIP):
            mod_ref[:, j * EC:(j + 1) * EC] = modb[j]
        wg.forward()
        wg.finish()
        for cp in csend + msend:
            cp.wait_send()

    vm = pl.BlockSpec(memory_space=pltpu.VMEM)
    return pl.pallas_call(
        body, name="start_gather",
        in_specs=[vm] * 4, out_specs=[vm] * 3,
        out_shape=[jax.ShapeDtypeStruct((NDEV, wc), F32), jax.ShapeDtypeStruct((1, 3 * D), F32),
                   jax.ShapeDtypeStruct((NCHIP, D, EC), BF16)],
        scratch_shapes=[pltpu.VMEM((NDEV, EC), F32), pltpu.VMEM((NCHIP, 1, EC), F32),
                        pltpu.SemaphoreType.DMA((NDEV - 1,)), pltpu.SemaphoreType.DMA((NDEV - 1,)),
                        pltpu.SemaphoreType.DMA((NCHIP - 1,)), pltpu.SemaphoreType.DMA((NCHIP - 1,))]
        + _WeightGather.SEMS + [pltpu.SemaphoreType.DMA((2,))],
        compiler_params=pltpu.CompilerParams(vmem_limit_bytes=VMEM_LIMIT),
    )(crow, w_ada, b_cols, w_in)


class _ReduceScatter:
    @staticmethod
    def scratch(n_units, rows, ucols, max_owned):
        half = rows // 2
        return [pltpu.VMEM((n_units, half, ucols), F32), pltpu.VMEM((n_units, half, ucols), BF16),
                pltpu.VMEM((max_owned, NCHIP, half, ucols), BF16),
                pltpu.SemaphoreType.DMA((2,)), pltpu.SemaphoreType.DMA((n_units,)),
                pltpu.SemaphoreType.DMA((n_units, NCHIP)), pltpu.SemaphoreType.DMA((n_units,)),
                pltpu.SemaphoreType.DMA((n_units,))]

    def __init__(self, g_ref, out_ref, units, sib, stage, got, sem1, send2, recv2, send3, recv3):
        x, y, c = _me()
        self.c, self.ci = c, 2 * x + y
        self.g, self.out, self.units = g_ref, out_ref, units
        self.sib, self.stage, self.got = sib, stage, got
        self.sem1, self.send2, self.recv2, self.send3, self.recv3 = sem1, send2, recv2, send3, recv3
        self.half = g_ref.shape[1] // 2
        self.ucols = g_ref.shape[2]
        self.r0 = pl.multiple_of(c * self.half, self.half)
        self.r1 = pl.multiple_of((1 - c) * self.half, self.half)
        self.slot0 = units[0][0]
        assert [u[0] for u in units] == list(range(self.slot0, self.slot0 + len(units)))
        seen = {}
        self.local = []
        for _, owner, _ in units:
            self.local.append(seen.get(owner, 0))
            seen[owner] = seen.get(owner, 0) + 1

    def _halves(self):
        n = len(self.units)
        return _remote(self.g.at[pl.ds(self.slot0, n), pl.ds(self.r1, self.half), :], self.sib,
                       self.sem1.at[0], self.sem1.at[1], 1)

    def _partial(self, i, sender):
        _, owner, _ = self.units[i]
        return pltpu.make_async_remote_copy(
            src_ref=self.stage.at[i], dst_ref=self.got.at[self.local[i], sender],
            send_sem=self.send2.at[i], recv_sem=self.recv2.at[i, sender],
            device_id=(owner // 2, owner % 2, self.c), device_id_type=MESH)

    def _back(self, i, start):
        off = self.units[i][2]
        blk = self.out.at[pl.ds(start, self.half), off:off + self.ucols]
        return _remote(blk, blk, self.send3.at[i], self.recv3.at[i], 1)

    def at_steps(self, step, start, send, reduce, finish, out_ref):
        @pl.when(step == start)
        def _():
            self.out[...] = jnp.zeros_like(self.out)
            self.start_halves()

        pl.when(step == send)(self.send_partials)
        pl.when(step == reduce)(self.reduce_owned)

        @pl.when(step == finish)
        def _():
            self.finish()
            out_ref[...] = self.out[...]

    def start_halves(self):
        self._halves().start()

    def send_partials(self):
        self._halves().wait_recv()
        for i, (slot, owner, _) in enumerate(self.units):
            @pl.when(self.ci != owner)
            def _():
                self.stage[i] = (self.g[slot, pl.ds(self.r0, self.half), :] + self.sib[i]).astype(BF16)
                self._partial(i, self.ci).start()

    def reduce_owned(self):
        for i, (slot, owner, off) in enumerate(self.units):
            @pl.when(self.ci == owner)
            def _():
                rows, cols = pl.ds(self.r0, self.half), slice(off, off + self.ucols)
                self.out[rows, cols] = self.g[slot, pl.ds(self.r0, self.half), :] + self.sib[i]
                for s in range(NCHIP):
                    if s != owner:
                        self._partial(i, s).wait_recv()
                        self.out[rows, cols] += self.got[self.local[i], s].astype(F32)
                self._back(i, self.r0).start()

    def finish(self):
        self._halves().wait_send()
        for i, (_, owner, _) in enumerate(self.units):
            @pl.when(self.ci == owner)
            def _():
                self._back(i, self.r1).wait_recv()
                self._back(i, self.r0).wait_send()

            @pl.when(self.ci != owner)
            def _():
                self._partial(i, self.ci).wait_send()


def _reduce_scatter(g4, name):
    _, rows, cols = g4.shape
    units = [(j, j, 0) for j in range(NCHIP)]

    def body(g_ref, out_ref, *scratch):
        rs = _ReduceScatter(g_ref, out_ref, units, *scratch)
        rs.start_halves()
        rs.send_partials()
        rs.reduce_owned()
        rs.finish()

    return pl.pallas_call(
        body, name=name,
        in_specs=[pl.BlockSpec(memory_space=pltpu.VMEM)],
        out_specs=pl.BlockSpec(memory_space=pltpu.VMEM),
        out_shape=jax.ShapeDtypeStruct((rows, cols), F32),
        scratch_shapes=_ReduceScatter.scratch(NCHIP, rows, cols, 1),
        compiler_params=pltpu.CompilerParams(vmem_limit_bytes=VMEM_LIMIT),
    )(g4)


def _silu_rows(c_ref):
    cv = c_ref[...]
    sc = cv * _sigmoid(cv)
    return jnp.concatenate([sc, jnp.zeros_like(sc)], axis=0).astype(BF16)


def _ada_fwd(cg, w_ada, b_cols):
    def body(c_ref, w_ref, b_ref, o_ref):
        o_ref[...] = _dot(_silu_rows(c_ref), w_ref[...].astype(BF16))[0:NDEV, :] + b_ref[...]

    return pl.pallas_call(body, name="ada_fwd", out_shape=jax.ShapeDtypeStruct((NDEV, EC), F32),
                          compiler_params=_cp())(cg, w_ada, b_cols)


def _ada_bwd(cg, dmod_cols):
    def body(c_ref, d_ref, o_ref):
        dm = d_ref[...]
        dmb = jnp.concatenate([dm, jnp.zeros_like(dm)], axis=0).astype(BF16)
        o_ref[...] = _dot_tn(_silu_rows(c_ref), dmb)

    return pl.pallas_call(body, name="ada_bwd", out_shape=jax.ShapeDtypeStruct((D, EC), F32),
                          compiler_params=_cp())(cg, dmod_cols)


def _sum_rows(g):
    def body(g_ref, o_ref):
        acc = g_ref[0:1, :]
        for r in range(1, NDEV):
            acc = acc + g_ref[r:r + 1, :]
        o_ref[...] = acc

    return pl.pallas_call(body, name="sum_rows", out_shape=jax.ShapeDtypeStruct((1, g.shape[1]), F32),
                          compiler_params=_cp())(g)


def _adamw(w, g, m, v, name):
    rows, cols = w.shape
    tr = 256 if rows % 256 == 0 else rows

    def body(w_ref, g_ref, m_ref, v_ref, d_ref, nm_ref, nv_ref):
        gv = g_ref[...]
        nm = B1 * m_ref[...] + (1.0 - B1) * gv
        nv = B2 * v_ref[...] + (1.0 - B2) * (gv * gv)
        m_hat = nm / (1.0 - B1 ** STEP)
        v_hat = nv / (1.0 - B2 ** STEP)
        d_ref[...] = (-LR) * (m_hat / (jnp.sqrt(v_hat) + ADAM_EPS) + WD * w_ref[...])
        nm_ref[...] = nm
        nv_ref[...] = nv

    spec = pl.BlockSpec((tr, cols), lambda i: (i, 0))
    return pl.pallas_call(
        body, name=name, grid=(rows // tr,), in_specs=[spec] * 4, out_specs=[spec] * 3,
        out_shape=[jax.ShapeDtypeStruct((rows, cols), F32)] * 3,
        compiler_params=_cp(("parallel",)),
    )(w, g, m, v)


def _adamw_values(w, g, m, v):
    nm = B1 * m + (1.0 - B1) * g
    nv = B2 * v + (1.0 - B2) * (g * g)
    m_hat = nm / (1.0 - B1 ** STEP)
    v_hat = nv / (1.0 - B2 ** STEP)
    return (-LR) * (m_hat / (jnp.sqrt(v_hat) + ADAM_EPS) + WD * w), nm, nv


NB = R // HEAD
SMALL = (("b_ada", (1, 3 * D)), ("norm_pre", (1, D)), ("norm_post", (1, D)), ("conv_w", (4, R // NCHIP)),
         ("conv_b", (1, R)), ("w_rg_a", (NB, HEAD, HEAD)), ("b_rg_a", (1, R)), ("w_rg_x", (NB, HEAD, HEAD)),
         ("b_rg_x", (1, R)), ("lru_lambda", (1, R)), ("norm_rec", (1, R)), ("norm_att", (1, R)))


def _small_update(ao8, sm8, dwa8, dwx8, ai8, cg, params):
    n = len(SMALL)

    def body(ao_ref, sm_ref, dwa_ref, dwx_ref, ai_ref, cg_ref, *refs):
        pin, pout, (gada_ref, loss_ref, dmod) = refs[:3 * n], refs[3 * n:7 * n], refs[7 * n:]
        xx, yy, _ = _me()
        ci = 2 * xx + yy

        def total(ref, *idx):
            acc = ref[(0,) + idx]
            for d in range(1, NDEV):
                acc = acc + ref[(d,) + idx]
            return acc

        row = lambda ref, r, lanes=slice(None): total(ref, slice(r, r + 1), lanes)
        mine = lambda parts: sum(jnp.where(ci == j, part, 0.0) for j, part in enumerate(parts))
        cw = R // NCHIP
        grads = {
            "b_ada": [jnp.concatenate([row(ai_ref, 0), row(ai_ref, 1), row(ao_ref, 0)], axis=1)],
            "norm_pre": [row(ai_ref, 2)], "norm_post": [row(ao_ref, 1)],
            "conv_w": [mine([row(sm_ref, 8 + r, slice(j * cw, (j + 1) * cw)) for j in range(NCHIP)]) for r in range(4)],
            "conv_b": [row(sm_ref, 4)], "b_rg_a": [row(sm_ref, 0)], "b_rg_x": [row(sm_ref, 1)],
            "lru_lambda": [row(sm_ref, 2)], "norm_rec": [row(sm_ref, 3)], "norm_att": [row(ao_ref, 2, slice(0, R))],
            "w_rg_a": [total(dwa_ref, h) for h in range(NB)], "w_rg_x": [total(dwx_ref, h) for h in range(NB)],
        }
        loss_ref[...] = row(ao_ref, 3, slice(0, LANES)) * (0.5 / D)
        for k, (name, shape) in enumerate(SMALL):
            w_ref, m_ref, v_ref = pin[3 * k:3 * k + 3]
            outs = pout[4 * k:4 * k + 4]
            for r, g in enumerate(grads[name]):
                at = (slice(None),) if len(grads[name]) == 1 else ((r,) if len(shape) == 3 else (slice(r, r + 1),))
                res = (g,) + _adamw_values(w_ref[at], g, m_ref[at], v_ref[at])
                for o_ref, val in zip(outs, res):
                    o_ref[at] = val
        for d in range(NDEV):
            dmod[d:d + 1, :] = jnp.concatenate([ai_ref[d, 0:1, :], ai_ref[d, 1:2, :], ao_ref[d, 0:1, :]], axis=1)
        cols = mine([dmod[:, j * EC:(j + 1) * EC] for j in range(NCHIP)])
        colsb = jnp.concatenate([cols, jnp.zeros_like(cols)], axis=0).astype(BF16)
        gada_ref[...] = _dot_tn(_silu_rows(cg_ref), colsb)

    shapes = [jax.ShapeDtypeStruct(s, F32) for _, s in SMALL]
    outs = pl.pallas_call(
        body, name="small_update",
        out_shape=[s for s in shapes for _ in range(4)] + [jax.ShapeDtypeStruct((D, EC), F32),
                                                           jax.ShapeDtypeStruct((1, LANES), F32)],
        scratch_shapes=[pltpu.VMEM((NDEV, 3 * D), F32)],
        compiler_params=_cp(),
    )(ao8, sm8, dwa8, dwx8, ai8, cg, *params)
    return outs[:4 * n], outs[4 * n], outs[4 * n + 1]


BIG = ("w_ada", "w_in", "w_out")
WEIGHTS = ("w_ada", "b_ada", "norm_pre", "norm_post", "w_in", "conv_w", "conv_b", "w_rg_a", "b_rg_a", "w_rg_x",
           "b_rg_x", "lru_lambda", "norm_rec", "norm_att", "w_out")


def kernel(x, c, positions, w_ada, b_ada, norm_pre, norm_post, w_in, conv_w, conv_b, w_rg_a, b_rg_a, w_rg_x, b_rg_x, lru_lambda, norm_rec, norm_att, w_out, loss_target, m_w_ada, m_b_ada, m_norm_pre, m_norm_post, m_w_in, m_conv_w, m_conv_b, m_w_rg_a, m_b_rg_a, m_w_rg_x, m_b_rg_x, m_lru_lambda, m_norm_rec, m_norm_att, m_w_out, v_w_ada, v_b_ada, v_norm_pre, v_norm_post, v_w_in, v_conv_w, v_conv_b, v_w_rg_a, v_b_rg_a, v_w_rg_x, v_b_rg_x, v_lru_lambda, v_norm_rec, v_norm_att, v_w_out):
    given = dict(locals())
    wts = {n: given[n] for n in WEIGHTS}
    ms = {n: given["m_" + n] for n in WEIGHTS}
    vs = {n: given["v_" + n] for n in WEIGHTS}
    xi, yi, _ = _me()
    chip = 2 * xi + yi
    cw_loc = R // NCHIP

    b_cols = lax.dynamic_slice(b_ada, (0, chip * EC), (1, EC))
    order = (chip ^ jnp.arange(NCHIP, dtype=jnp.int32)).astype(jnp.int32)
    g0, mod, w_in_bf, cos, sin, proj, ht = _start_in_proj(
        jnp.concatenate([c, conv_w.reshape(1, 4 * cw_loc)], axis=1), w_ada[0], b_cols, w_in[0],
        positions.reshape(S, 1), x[0], norm_pre, order)
    cg = g0[:, 0:D]
    conv_full = g0[0::2, D:].reshape(NCHIP, 4, cw_loc).transpose(1, 0, 2).reshape(4, R)

    p = dict(norm_pre=norm_pre, norm_post=norm_post, conv_b=conv_b, b_rg_a=b_rg_a, b_rg_x=b_rg_x,
             lru_lambda=lru_lambda, norm_rec=norm_rec, norm_att=norm_att, w_rg_a=w_rg_a[0], w_rg_x=w_rg_x[0])
    grad_x, g_in, g_out, gathered = _local_step(
        x[0], cos, sin, loss_target[0], mod, w_in_bf, proj, ht, w_out[0], conv_full, p)

    params = [d[n].reshape(shape) for n, shape in SMALL for d in (wts, ms, vs)]
    small_out, g_ada, loss_row = _small_update(*gathered, cg, params)
    grads = {"w_out": g_out, "w_in": g_in, "w_ada": g_ada}
    delta, new_m, new_v = {}, {}, {}
    for k, (n, _) in enumerate(SMALL):
        grads[n], delta[n], new_m[n], new_v[n] = small_out[4 * k:4 * k + 4]
    for n in BIG:
        delta[n], new_m[n], new_v[n] = _adamw(wts[n][0], grads[n], ms[n][0], vs[n][0], "adamw_" + n)
    out = lambda d: [d[n].reshape(wts[n].shape) for n in WEIGHTS]
    return (loss_row[0, 0], grad_x.reshape(x.shape), *out(grads), *out(delta), *out(new_m), *out(new_v))
```

```python
import functools

import numpy as np
import jax
import jax.numpy as jnp
from jax import lax
from jax.experimental import pallas as pl
from jax.experimental.pallas import tpu as pltpu

F32 = jnp.float32
BF16 = jnp.bfloat16

S = 2048
D = 1024
E = 3072
R = 512
NDEV = 8
NCHIP = 4
EC = 768
LRU_C = 8.0
EPS = 1e-6
NEG = -1e30
HEAD = 64
BLK = 128
PATTERNS = (1, 4, 16)
ROPE_THETA = 10000.0
LANES = 128
VMEM_LIMIT = 56 * 1024 * 1024

B1, B2, LR, WD, ADAM_EPS, STEP = 0.9, 0.999, 0.001, 0.01, 1e-8, 10
MESH = pl.DeviceIdType.MESH


def _cp(sem=None, **kw):
    return pltpu.CompilerParams(dimension_semantics=sem, vmem_limit_bytes=VMEM_LIMIT, **kw)


def _dot(a, b):
    return jnp.dot(a, b, preferred_element_type=F32)


def _dot_nt(a, b):
    return lax.dot_general(a, b, (((1,), (1,)), ((), ())), preferred_element_type=F32)


def _dot_tn(a, b):
    return lax.dot_general(a, b, (((0,), (0,)), ((), ())), preferred_element_type=F32)


def _sigmoid(x):
    return 1.0 / (1.0 + jnp.exp(-x))


def _expm1(x):
    poly = x * (1.0 + x * (0.5 + x * (1.0 / 6 + x * (1.0 / 24 + x * (1.0 / 120 + x * (1.0 / 720))))))
    return jnp.where(jnp.abs(x) < 0.3, poly, jnp.exp(x) - 1.0)


def _rms_fwd(v, g):
    rstd = lax.rsqrt(jnp.mean(v * v, axis=-1, keepdims=True) + EPS)
    vn = v * rstd
    return vn * g, vn, rstd


def _rms_bwd(dy, vn, rstd, g):
    dvn = dy * g
    dv = rstd * (dvn - vn * jnp.mean(dvn * vn, axis=-1, keepdims=True))
    return dv, jnp.sum(dy * vn, axis=0, keepdims=True)


def _in_proj_fwd(x, mod, norm_pre, w_in_bf):
    ts = 256

    def body(x_ref, mod_ref, np_ref, w_ref, proj_ref, ht_ref):
        hp, _, _ = _rms_fwd(x_ref[...], np_ref[...])
        h = hp * (1.0 + mod_ref[:, D:2 * D]) + mod_ref[:, 0:D]
        hb = h.astype(BF16)
        ht_ref[...] = h.T.astype(BF16)
        for j in range(NCHIP):
            proj_ref[:, j * EC:(j + 1) * EC] = _dot(hb, w_ref[j])

    return pl.pallas_call(
        body, name="in_proj_fwd", grid=(S // ts,),
        in_specs=[pl.BlockSpec((ts, D), lambda i: (i, 0)), pl.BlockSpec((1, 3 * D), lambda i: (0, 0)),
                  pl.BlockSpec((1, D), lambda i: (0, 0)), pl.BlockSpec((NCHIP, D, EC), lambda i: (0, 0, 0))],
        out_specs=[pl.BlockSpec((ts, E), lambda i: (i, 0)), pl.BlockSpec((D, ts), lambda i: (0, i))],
        out_shape=[jax.ShapeDtypeStruct((S, E), F32), jax.ShapeDtypeStruct((D, S), BF16)],
        compiler_params=_cp(("parallel",)),
    )(x, mod, norm_pre, w_in_bf)


RT = 256


def _shift_down(cur, prev8, j, row):
    if j == 0:
        return cur
    top = jnp.tile(pltpu.roll(prev8, j, 0), (RT // 8, 1))
    return jnp.where(row >= j, pltpu.roll(cur, j, 0), top)


def _shift_up(cur, next8, j, row):
    if j == 0:
        return cur
    bot = jnp.tile(pltpu.roll(next8, 8 - j, 0), (RT // 8, 1))
    return jnp.where(row < RT - j, pltpu.roll(cur, RT - j, 0), bot)


def _rec_gates(xp, xprev8, row, cw_ref, cb_ref, wa_ref, ba_ref, wx_ref, bx_ref, lam_ref):
    xa = cb_ref[...] + sum(cw_ref[3 - j:4 - j, :] * _shift_down(xp, xprev8, j, row) for j in range(4))
    xab = xa.astype(BF16)
    r = _sigmoid(_dot(xab, wa_ref[...]) + ba_ref[...])
    ig = _sigmoid(_dot(xab, wx_ref[...]) + bx_ref[...])
    nl = -lam_ref[...]
    sp = jnp.maximum(nl, 0.0) + jnp.log1p(jnp.exp(-jnp.abs(nl)))
    la = (-LRU_C) * r * sp
    a = jnp.exp(la)
    mult = jnp.sqrt(-_expm1(2.0 * la))
    return dict(xa=xa, xab=xab, r=r, ig=ig, sp=sp, la=la, a=a, mult=mult)


def _scan_fwd(a, u, row):
    sh = 1
    while sh < RT:
        a_s = jnp.where(row >= sh, pltpu.roll(a, sh, 0), 1.0)
        u_s = jnp.where(row >= sh, pltpu.roll(u, sh, 0), 0.0)
        u = a * u_s + u
        a = a * a_s
        sh *= 2
    return a, u


def _scan_bwd(al, g, row):
    sh = 1
    while sh < RT:
        al_s = jnp.where(row < RT - sh, pltpu.roll(al, RT - sh, 0), 1.0)
        g_s = jnp.where(row < RT - sh, pltpu.roll(g, RT - sh, 0), 0.0)
        g = g + al * g_s
        al = al * al_s
        sh *= 2
    return g


def _dense_from_blocks(blocks_ref, dense_ref):
    dense_ref[...] = jnp.zeros_like(dense_ref)
    for h in range(R // HEAD):
        dense_ref[h * HEAD:(h + 1) * HEAD, h * HEAD:(h + 1) * HEAD] = blocks_ref[h].astype(dense_ref.dtype)


def _rec_fwd(proj, conv_w, conv_b, wa_b, ba, wx_b, bx, lam, norm_rec):
    nt = S // RT

    def body(p_ref, cw_ref, cb_ref, wa_ref, ba_ref, wx_ref, bx_ref, lam_ref, nr_ref,
             h_ref, ya_ref, prev8, hc, wad, wxd):
        i = pl.program_id(0)

        @pl.when(i == 0)
        def _():
            prev8[...] = jnp.zeros_like(prev8)
            hc[...] = jnp.zeros_like(hc)
            _dense_from_blocks(wa_ref, wad)
            _dense_from_blocks(wx_ref, wxd)

        row = lax.broadcasted_iota(jnp.int32, (RT, R), 0)
        xp = p_ref[:, 0:R]
        ga = p_ref[:, R:2 * R]
        f = _rec_gates(xp, prev8[...], row, cw_ref, cb_ref, wad, ba_ref, wxd, bx_ref, lam_ref)
        u = f["mult"] * (f["ig"] * f["xa"])
        acum, hh = _scan_fwd(f["a"], u, row)
        h = hh + acum * hc[0:1, :]
        h_ref[...] = h
        hc[0:1, :] = h_ref[RT - 1:RT, :]
        prev8[...] = p_ref[RT - 8:RT, 0:R]
        yp = h * (ga * _sigmoid(ga))
        ya, _, _ = _rms_fwd(yp, nr_ref[...])
        ya_ref[...] = ya.astype(BF16)

    row1 = lambda n: pl.BlockSpec((1, n), lambda i: (0, 0))
    blocks = pl.BlockSpec((R // HEAD, HEAD, HEAD), lambda i: (0, 0, 0))
    return pl.pallas_call(
        body, name="rec_fwd", grid=(nt,),
        in_specs=[pl.BlockSpec((RT, 2 * R), lambda i: (i, 0)), pl.BlockSpec((4, R), lambda i: (0, 0)), row1(R),
                  blocks, row1(R), blocks, row1(R), row1(R), row1(R)],
        out_specs=[pl.BlockSpec((RT, R), lambda i: (i, 0)), pl.BlockSpec((RT, R), lambda i: (i, 0))],
        out_shape=[jax.ShapeDtypeStruct((S, R), F32), jax.ShapeDtypeStruct((S, R), BF16)],
        scratch_shapes=[pltpu.VMEM((8, R), F32), pltpu.VMEM((8, R), F32), pltpu.VMEM((R, R), BF16),
                        pltpu.VMEM((R, R), BF16)],
        compiler_params=_cp(("arbitrary",)),
    )(proj, conv_w, conv_b, wa_b, ba, wx_b, bx, lam, norm_rec)


def _rec_bwd(dproj, d_ya, proj, h_all, conv_w, conv_b, wa_b, ba, wx_b, bx, lam, norm_rec):
    nt = S // RT

    def body(dp_in, dya_ref, p_ref, pprev_ref, h_ref, hprev_ref, cw_ref, cb_ref, wab_ref, ba_ref, wxb_ref, bx_ref,
             lam_ref, nr_ref, dp_ref, dwab_ref, dwxb_ref, sm_ref, nxt8, cg, wa_ref, wx_ref, dwa_ref, dwx_ref):
        i = pl.program_id(0)
        ti = nt - 1 - i

        @pl.when(i == 0)
        def _():
            nxt8[...] = jnp.zeros_like(nxt8)
            cg[...] = jnp.zeros_like(cg)
            dwa_ref[...] = jnp.zeros_like(dwa_ref)
            dwx_ref[...] = jnp.zeros_like(dwx_ref)
            sm_ref[...] = jnp.zeros_like(sm_ref)
            _dense_from_blocks(wab_ref, wa_ref)
            _dense_from_blocks(wxb_ref, wx_ref)

        row = lax.broadcasted_iota(jnp.int32, (RT, R), 0)
        first = (ti > 0).astype(F32)
        xprev8 = pprev_ref[...] * first
        hprev8 = hprev_ref[...] * first
        xp = p_ref[:, 0:R]
        ga = p_ref[:, R:2 * R]
        f = _rec_gates(xp, xprev8, row, cw_ref, cb_ref, wa_ref, ba_ref, wx_ref, bx_ref, lam_ref)
        xa, r, ig, a, mult = f["xa"], f["r"], f["ig"], f["a"], f["mult"]
        h = h_ref[...]
        sg = _sigmoid(ga)
        gate = ga * sg
        yp = h * gate
        _, ypn, rstd = _rms_fwd(yp, nr_ref[...])
        d_yp, dnr = _rms_bwd(dya_ref[...], ypn, rstd, nr_ref[...])
        d_ga = d_yp * h * (sg * (1.0 + ga * (1.0 - sg)))
        dh = d_yp * gate + jnp.where(row == RT - 1, cg[0:1, :], 0.0)
        al = jnp.where(row < RT - 1, pltpu.roll(a, RT - 1, 0), 0.0)
        g = _scan_bwd(al, dh, row)
        cg[0:1, :] = jnp.sum(jnp.where(row == 0, a * g, 0.0), axis=0, keepdims=True)
        h_m1 = _shift_down(h, hprev8, 1, row)
        da = g * h_m1
        ix = ig * xa
        d_mult = g * ix
        d_ig = g * mult * xa
        d_xa = g * mult * ig
        d_la = da * a - d_mult * (a * a) / mult
        d_r = d_la * ((-LRU_C) * f["sp"])
        dsp = jnp.sum(d_la * ((-LRU_C) * r), axis=0, keepdims=True)
        dlam = dsp * (-_sigmoid(-lam_ref[...]))
        d_za = d_r * r * (1.0 - r)
        d_zx = d_ig * ig * (1.0 - ig)
        dzab = d_za.astype(BF16)
        dzxb = d_zx.astype(BF16)
        dwa_ref[...] += _dot_tn(f["xab"], dzab)
        dwx_ref[...] += _dot_tn(f["xab"], dzxb)
        d_xa = d_xa + _dot_nt(dzab, wa_ref[...]) + _dot_nt(dzxb, wx_ref[...])
        d_xp = sum(cw_ref[3 - j:4 - j, :] * _shift_up(d_xa, nxt8[...], j, row) for j in range(4))
        dcw = [jnp.sum(d_xa * _shift_down(xp, xprev8, 3 - k, row), axis=0, keepdims=True) for k in range(4)]
        dp_ref[:, 0:R] = d_xp.astype(BF16)
        dp_ref[:, R:2 * R] = d_ga.astype(BF16)
        dp8 = d_xa[0:8, :]
        nxt8[...] = dp8
        sm_ref[0:1, :] += jnp.sum(d_za, axis=0, keepdims=True)
        sm_ref[1:2, :] += jnp.sum(d_zx, axis=0, keepdims=True)
        sm_ref[2:3, :] += dlam
        sm_ref[3:4, :] += dnr
        sm_ref[4:5, :] += jnp.sum(d_xa, axis=0, keepdims=True)
        for k in range(4):
            sm_ref[8 + k:9 + k, :] += dcw[k]

        @pl.when(i == nt - 1)
        def _():
            for h in range(R // HEAD):
                dwab_ref[h] = dwa_ref[h * HEAD:(h + 1) * HEAD, h * HEAD:(h + 1) * HEAD]
                dwxb_ref[h] = dwx_ref[h * HEAD:(h + 1) * HEAD, h * HEAD:(h + 1) * HEAD]

    c0 = lambda shape: pl.BlockSpec(shape, lambda i: (0, 0))
    blocks = pl.BlockSpec((R // HEAD, HEAD, HEAD), lambda i: (0, 0, 0))
    rev = lambda i: nt - 1 - i
    prev8 = lambda i: (jnp.maximum((nt - 1 - i) * (RT // 8) - 1, 0), 0)
    return pl.pallas_call(
        body, name="rec_bwd", grid=(nt,),
        in_specs=[pl.BlockSpec(memory_space=pl.ANY),
                  pl.BlockSpec((RT, R), lambda i: (rev(i), 0)),
                  pl.BlockSpec((RT, 2 * R), lambda i: (rev(i), 0)), pl.BlockSpec((8, R), prev8),
                  pl.BlockSpec((RT, R), lambda i: (rev(i), 0)), pl.BlockSpec((8, R), prev8),
                  c0((4, R)), c0((1, R)), blocks, c0((1, R)), blocks, c0((1, R)), c0((1, R)), c0((1, R))],
        out_specs=[pl.BlockSpec((RT, 2 * R), lambda i: (rev(i), 0)), blocks, blocks, c0((16, R))],
        out_shape=[jax.ShapeDtypeStruct((S, E), BF16), jax.ShapeDtypeStruct((R // HEAD, HEAD, HEAD), F32),
                   jax.ShapeDtypeStruct((R // HEAD, HEAD, HEAD), F32), jax.ShapeDtypeStruct((16, R), F32)],
        scratch_shapes=[pltpu.VMEM((8, R), F32), pltpu.VMEM((8, R), F32), pltpu.VMEM((R, R), BF16),
                        pltpu.VMEM((R, R), BF16), pltpu.VMEM((R, R), F32), pltpu.VMEM((R, R), F32)],
        input_output_aliases={0: 0},
        compiler_params=_cp(("arbitrary",)),
    )(dproj, d_ya, proj, proj, h_all, h_all, conv_w, conv_b, wa_b, ba, wx_b, bx, lam, norm_rec)


NPAIR = R // LANES
QB, KB, VB, GB = 2 * R // LANES, 3 * R // LANES, 4 * R // LANES, 5 * R // LANES


def _rope_freq():
    half = HEAD // 2
    inv = np.float32(ROPE_THETA) ** (-(np.arange(half, dtype=np.float32) / np.float32(half)))
    return jnp.asarray(np.tile(inv.astype(np.float32), LANES // half)[None, :])


def _rot_half(x, first):
    return jnp.where(first, -pltpu.roll(x, LANES - HEAD // 2, 1), pltpu.roll(x, HEAD // 2, 1))


def _cos_sin(pos_ref, freq_ref):
    ang = pos_ref[...].astype(F32) * freq_ref[...]
    return jnp.cos(ang), jnp.sin(ang)


def _deint(src_ref, dst_ref, d):
    n = S // d
    for r in range(d):
        v = src_ref[pl.ds(r, n, stride=d), :] if d > 1 else src_ref[...]
        dst_ref[r * n:(r + 1) * n, :] = v.astype(dst_ref.dtype)


def _reint(src_ref, dst_ref, d, accumulate):
    n = S // d
    for r in range(d):
        idx = (pl.ds(r, n, stride=d), slice(None)) if d > 1 else (slice(None), slice(None))
        v = src_ref[r * n:(r + 1) * n, :]
        if accumulate:
            dst_ref[idx] = dst_ref[idx] + v
        else:
            dst_ref[idx] = v


def _blk_masks(b, nb):
    qi = lax.broadcasted_iota(jnp.int32, (BLK, BLK), 0)
    ki = lax.broadcasted_iota(jnp.int32, (BLK, BLK), 1)
    has_prev = lax.rem(b, nb) != 0
    return ki <= qi, jnp.logical_and(ki >= qi, has_prev)


def _rope_table(pos, freq):
    def body(pos_ref, freq_ref, cos_ref, sin_ref):
        cos_ref[...], sin_ref[...] = _cos_sin(pos_ref, freq_ref)

    return pl.pallas_call(body, name="rope_table", out_shape=[jax.ShapeDtypeStruct((S, LANES), F32)] * 2,
                          compiler_params=_cp())(pos, freq)


def _deint_heads(src_ref, dst0, dst1, d):
    n = S // d
    hm0 = lax.broadcasted_iota(jnp.int32, (n, LANES), 1) < HEAD
    for r in range(d):
        v = src_ref[pl.ds(r, n, stride=d), :] if d > 1 else src_ref[...]
        dst0[r * n:(r + 1) * n, :] = jnp.where(hm0, v, 0.0).astype(BF16)
        dst1[r * n:(r + 1) * n, :] = jnp.where(hm0, 0.0, v).astype(BF16)


def _reint_prev(src_ref, dst_ref, d):
    n = S // d
    if n == BLK:
        return
    for r in range(d):
        idx = (pl.ds(r, n - BLK, stride=d), slice(None)) if d > 1 else (slice(0, n - BLK), slice(None))
        dst_ref[idx] = dst_ref[idx] + src_ref[r * n + BLK:(r + 1) * n, :]


def _pair_masks():
    qi = lax.broadcasted_iota(jnp.int32, (BLK, 2 * BLK), 0)
    ki = lax.broadcasted_iota(jnp.int32, (BLK, 2 * BLK), 1) & (BLK - 1)
    return ki <= qi, ki >= qi


def _two(ref0, ref1, st, axis):
    return jnp.concatenate([ref0[pl.ds(st, BLK), :], ref1[pl.ds(st, BLK), :]], axis=axis)


ATT_UNROLL = 4


def _att_fwd(proj, cos, sin, w_out):
    def body(q_ref, k_ref, v_ref, cos_ref, sin_ref, w_ref, att_ref, qr_ref, kr_ref, lse_ref, wbf_ref,
             qd, kd0, kd1, vd0, vd1, od, ld, on, ln, wbuf, *wsems):
        wg = _WeightGather(w_ref, wbuf, *wsems)
        pl.when(pl.program_id(0) == 0)(wg.start)
        pl.when(pl.program_id(0) == 1)(wg.forward)
        lane = lax.broadcasted_iota(jnp.int32, (S, LANES), 1)
        first = (lane & (HEAD // 2)) == 0
        cos, sin = cos_ref[...], sin_ref[...]
        q = q_ref[...]
        k = k_ref[...]
        qr_ref[...] = (q * cos + _rot_half(q, first) * sin) * (HEAD ** -0.5)
        kr_ref[...] = k * cos + _rot_half(k, first) * sin
        hm0 = lax.broadcasted_iota(jnp.int32, (BLK, LANES), 1) < HEAD
        top = lax.broadcasted_iota(jnp.int32, (2 * BLK, LANES), 0) < BLK
        ones2 = (top == (lax.broadcasted_iota(jnp.int32, (2 * BLK, LANES), 1) < HEAD)).astype(BF16)
        mc2, mp2 = _pair_masks()

        for pi, d in enumerate(PATTERNS):
            nb = S // d // BLK
            _deint(qr_ref, qd, d)
            _deint_heads(kr_ref, kd0, kd1, d)
            _deint_heads(v_ref, vd0, vd1, d)

            def blk(b, carry):
                st = pl.multiple_of(b * BLK, BLK)
                qb = qd[pl.ds(st, BLK), :]
                sc = jnp.where(mc2, _dot_nt(qb, _two(kd0, kd1, st, 0)), NEG)
                mx = sc
                if nb > 1:
                    stp = pl.multiple_of(jnp.maximum(b - 1, 0) * BLK, BLK)
                    mp = jnp.logical_and(mp2, lax.rem(b, nb) != 0)
                    sp = jnp.where(mp, _dot_nt(qb, _two(kd0, kd1, stp, 0)), NEG)
                    mx = jnp.maximum(sc, sp)
                m0 = jnp.max(mx[:, 0:BLK], axis=1, keepdims=True)
                m1 = jnp.max(mx[:, BLK:2 * BLK], axis=1, keepdims=True)
                mf = jnp.concatenate([jnp.broadcast_to(m0, (BLK, BLK)), jnp.broadcast_to(m1, (BLK, BLK))], axis=1)
                o = _dot(jnp.exp(sc - mf).astype(BF16), jnp.concatenate([_two(vd0, vd1, st, 0), ones2], axis=1))
                if nb > 1:
                    o = o + _dot(jnp.exp(sp - mf).astype(BF16), jnp.concatenate([_two(vd0, vd1, stp, 0), ones2], axis=1))
                l = o[:, LANES:2 * LANES]
                od[pl.ds(st, BLK), :] = o[:, 0:LANES] / l
                ld[pl.ds(st, BLK), :] = jnp.where(hm0, m0, m1) + jnp.log(l)
                return carry

            lax.fori_loop(0, S // BLK, blk, 0, unroll=ATT_UNROLL)
            _reint(od, on.at[pi], d, False)
            _reint(ld, ln.at[pi], d, False)

        l0, l1, l2 = ln[0], ln[1], ln[2]
        m = jnp.maximum(jnp.maximum(l0, l1), l2)
        e0, e1, e2 = jnp.exp(l0 - m), jnp.exp(l1 - m), jnp.exp(l2 - m)
        den = e0 + e1 + e2
        att_ref[...] = (e0 * on[0] + e1 * on[1] + e2 * on[2]) / den
        lse_ref[...] = m + jnp.log(den)

        @pl.when(pl.program_id(0) == NPAIR - 1)
        def _():
            wg.finish()
            wbf_ref[...] = wbuf[...]

    col = lambda c0: pl.BlockSpec((S, LANES), lambda p: (0, c0 + p))
    out = pl.BlockSpec((S, LANES), lambda p: (0, p))
    tab = pl.BlockSpec((S, LANES), lambda p: (0, 0))
    vm = pl.BlockSpec(memory_space=pltpu.VMEM)
    return pl.pallas_call(
        body, name="att_fwd", grid=(NPAIR,),
        in_specs=[col(QB), col(KB), col(VB), tab, tab, vm],
        out_specs=[out, out, out, out, vm],
        out_shape=[jax.ShapeDtypeStruct((S, R), F32)] * 4 + [jax.ShapeDtypeStruct((NCHIP,) + w_out.shape, BF16)],
        scratch_shapes=[pltpu.VMEM((S, LANES), BF16)] * 5 + [pltpu.VMEM((S, LANES), F32)] * 2
        + [pltpu.VMEM((3, S, LANES), F32)] * 2 + [pltpu.VMEM((NCHIP,) + w_out.shape, BF16)] + _WeightGather.SEMS,
        compiler_params=_cp(("arbitrary",)),
    )(proj, proj, proj, cos, sin, w_out)


def _att_bwd(dproj, d_att, att, lse, qr, kr, proj, cos, sin, gw_out4):
    out_units = [(j, j, 0) for j in range(NCHIP)]

    nblk = S // BLK

    def body(dp_in, do_ref, o_ref, lse_ref, qr_ref, kr_ref, v_ref, cos_ref, sin_ref, gw_ref, dp_ref, gout_ref,
             qd, kd0, kd1, vd0, vd1, dod, kt, packn, packd, dqd, dkcd, dkpd, dvcd, dvpd,
             dqn, dkn, dvn, rows, trs, stage, sems, gred, *rs_scratch):
        p = pl.program_id(0)
        rs = _ReduceScatter(gw_ref, gred, out_units, *rs_scratch)
        for step, piece in enumerate((rs.start_halves, rs.send_partials, rs.reduce_owned)):
            pl.when(p == step)(piece)

        @pl.when(p == NPAIR - 1)
        def _():
            rs.finish()
            gout_ref[...] = gred[...]

        lane = lax.broadcasted_iota(jnp.int32, (S, LANES), 1)
        hms = lane < HEAD
        prod = do_ref[...] * o_ref[...]
        d0 = jnp.sum(jnp.where(hms, prod, 0.0), axis=1, keepdims=True)
        d1 = jnp.sum(jnp.where(hms, 0.0, prod), axis=1, keepdims=True)
        lse = lse_ref[...]
        quarter = HEAD // 2
        packn[...] = jnp.where(lane < quarter, lse,
                               jnp.where(hms, pltpu.roll(lse, LANES - quarter, 1), jnp.where(lane < 3 * quarter, d0, d1)))
        dqn[...] = jnp.zeros_like(dqn)
        dkn[...] = jnp.zeros_like(dkn)
        dvn[...] = jnp.zeros_like(dvn)
        hm0 = lax.broadcasted_iota(jnp.int32, (BLK, LANES), 1) < HEAD
        key = lax.broadcasted_iota(jnp.int32, (2 * BLK, BLK), 0) & (BLK - 1)
        qry = lax.broadcasted_iota(jnp.int32, (2 * BLK, BLK), 1)
        mct, mpt = key <= qry, key >= qry

        for d in PATTERNS:
            nb = S // d // BLK
            _deint(qr_ref, qd, d)
            _deint_heads(kr_ref, kd0, kd1, d)
            _deint_heads(v_ref, vd0, vd1, d)
            _deint(do_ref, dod, d)
            _deint(packn, packd, d)

            def blk(b, carry):
                st = pl.multiple_of(b * BLK, BLK)
                kt[b] = _two(kd0, kd1, st, 0).astype(F32).T.astype(BF16)
                trs[b] = packd[pl.ds(st, BLK), :].T
                for j in range(4):
                    rows[b, j:j + 1, :] = trs[b, j * quarter:j * quarter + 1, :]
                qb, dob = qd[pl.ds(st, BLK), :], dod[pl.ds(st, BLK), :]
                both = lambda j: jnp.concatenate([jnp.broadcast_to(rows[b, j:j + 1, :], (BLK, BLK)),
                                                  jnp.broadcast_to(rows[b, j + 1:j + 2, :], (BLK, BLK))], axis=0)
                lbt, dlt = both(0), both(2)

                def side(bk, mask):
                    stk = pl.multiple_of(bk * BLK, BLK)
                    k2, v2 = _two(kd0, kd1, stk, 0), _two(vd0, vd1, stk, 0)
                    pt = jnp.where(mask, jnp.exp(_dot_nt(k2, qb) - lbt), 0.0)
                    dst = (pt * (_dot_nt(v2, dob) - dlt)).astype(BF16)
                    rk, rv = _dot(dst, qb), _dot(pt.astype(BF16), dob)
                    return (_dot(kt[bk], dst), jnp.where(hm0, rk[0:BLK], rk[BLK:2 * BLK]),
                            jnp.where(hm0, rv[0:BLK], rv[BLK:2 * BLK]))

                dq_t, dkc, dvc = side(b, mct)
                if nb > 1:
                    dqp_t, dkp, dvp = side(jnp.maximum(b - 1, 0), jnp.logical_and(mpt, lax.rem(b, nb) != 0))
                    dq_t = dq_t + dqp_t
                    dkpd[pl.ds(st, BLK), :] = dkp
                    dvpd[pl.ds(st, BLK), :] = dvp
                dqd[pl.ds(st, BLK), :] = dq_t.T
                dkcd[pl.ds(st, BLK), :] = dkc
                dvcd[pl.ds(st, BLK), :] = dvc
                return carry

            lax.fori_loop(0, nblk, blk, 0, unroll=ATT_UNROLL)
            _reint(dqd, dqn, d, True)
            _reint(dkcd, dkn, d, True)
            _reint(dvcd, dvn, d, True)
            _reint_prev(dkpd, dkn, d)
            _reint_prev(dvpd, dvn, d)

        lane = lax.broadcasted_iota(jnp.int32, (S, LANES), 1)
        first = (lane & (HEAD // 2)) == 0
        cos, sin = cos_ref[...], sin_ref[...]
        dq = dqn[...] * (HEAD ** -0.5)
        dk = dkn[...]
        stage[0] = (dq * cos - _rot_half(dq, first) * sin).astype(BF16)
        stage[1] = (dk * cos - _rot_half(dk, first) * sin).astype(BF16)
        stage[2] = dvn[...].astype(BF16)
        copies = [pltpu.make_async_copy(stage.at[j], dp_ref.at[:, pl.ds((2 + j) * R + p * LANES, LANES)], sems.at[j])
                  for j in range(3)]
        for cp in copies:
            cp.start()
        for cp in copies:
            cp.wait()

    blk = pl.BlockSpec((S, LANES), lambda p: (0, p))
    tab = pl.BlockSpec((S, LANES), lambda p: (0, 0))
    vm = pl.BlockSpec(memory_space=pltpu.VMEM)
    _, orows, ocols = gw_out4.shape
    return pl.pallas_call(
        body, name="att_bwd", grid=(NPAIR,),
        in_specs=[pl.BlockSpec(memory_space=pl.ANY), blk, blk, blk, blk, blk,
                  pl.BlockSpec((S, LANES), lambda p: (0, VB + p)), tab, tab, vm],
        out_specs=[pl.BlockSpec(memory_space=pl.ANY), vm],
        out_shape=[jax.ShapeDtypeStruct((S, E), BF16), jax.ShapeDtypeStruct((orows, ocols), F32)],
        scratch_shapes=[pltpu.VMEM((S, LANES), BF16)] * 6 + [pltpu.VMEM((nblk, LANES, 2 * BLK), BF16)]
        + [pltpu.VMEM((S, LANES), F32)] * 10
        + [pltpu.VMEM((nblk, 8, BLK), F32), pltpu.VMEM((nblk, LANES, BLK), F32)]
        + [pltpu.VMEM((3, S, LANES), BF16), pltpu.SemaphoreType.DMA((3,)), pltpu.VMEM((orows, ocols), F32)]
        + _ReduceScatter.scratch(NCHIP, orows, ocols, 1),
        input_output_aliases={0: 0},
        compiler_params=_cp(("arbitrary",)),
    )(dproj, d_att, att, lse, qr, kr, proj, cos, sin, gw_out4)


def _att_fwd_old(proj, pos, freq):
    def body(q_ref, k_ref, v_ref, pos_ref, freq_ref, att_ref, qr_ref, kr_ref, lse_ref,
             qd, kd, vd, od, ld, on, ln):
        lane = lax.broadcasted_iota(jnp.int32, (S, LANES), 1)
        first = (lane & (HEAD // 2)) == 0
        cos, sin = _cos_sin(pos_ref, freq_ref)
        q = q_ref[...]
        k = k_ref[...]
        qr_ref[...] = (q * cos + _rot_half(q, first) * sin) * (HEAD ** -0.5)
        kr_ref[...] = k * cos + _rot_half(k, first) * sin
        hm0 = lax.broadcasted_iota(jnp.int32, (BLK, LANES), 1) < HEAD

        for pi, d in enumerate(PATTERNS):
            nb = S // d // BLK
            _deint(qr_ref, qd, d)
            _deint(kr_ref, kd, d)
            _deint(v_ref, vd, d)

            def blk(b, carry):
                st = pl.multiple_of(b * BLK, BLK)
                stp = pl.multiple_of(jnp.maximum(b - 1, 0) * BLK, BLK)
                mc, mp = _blk_masks(b, nb)
                qb = qd[pl.ds(st, BLK), :]
                kc, kp = kd[pl.ds(st, BLK), :], kd[pl.ds(stp, BLK), :]
                vc, vp = vd[pl.ds(st, BLK), :], vd[pl.ds(stp, BLK), :]
                outs, lses = [], []
                for hm in (hm0, jnp.logical_not(hm0)):
                    qm = jnp.where(hm, qb, jnp.zeros_like(qb))
                    sc = jnp.where(mc, _dot_nt(qm, kc), NEG)
                    sp = jnp.where(mp, _dot_nt(qm, kp), NEG)
                    m = jnp.maximum(jnp.max(sc, axis=1, keepdims=True), jnp.max(sp, axis=1, keepdims=True))
                    pc, pp = jnp.exp(sc - m), jnp.exp(sp - m)
                    l = jnp.sum(pc, axis=1, keepdims=True) + jnp.sum(pp, axis=1, keepdims=True)
                    o = _dot(pc.astype(BF16), vc) + _dot(pp.astype(BF16), vp)
                    outs.append(o / l)
                    lses.append(m + jnp.log(l))
                od[pl.ds(st, BLK), :] = jnp.where(hm0, outs[0], outs[1])
                ld[pl.ds(st, BLK), :] = jnp.where(hm0, lses[0], lses[1])
                return carry

            lax.fori_loop(0, S // BLK, blk, 0)
            _reint(od, on.at[pi], d, False)
            _reint(ld, ln.at[pi], d, False)

        l0, l1, l2 = ln[0], ln[1], ln[2]
        m = jnp.maximum(jnp.maximum(l0, l1), l2)
        e0, e1, e2 = jnp.exp(l0 - m), jnp.exp(l1 - m), jnp.exp(l2 - m)
        den = e0 + e1 + e2
        att_ref[...] = (e0 * on[0] + e1 * on[1] + e2 * on[2]) / den
        lse_ref[...] = m + jnp.log(den)

    col = lambda c0: pl.BlockSpec((S, LANES), lambda p: (0, c0 + p))
    out = pl.BlockSpec((S, LANES), lambda p: (0, p))
    return pl.pallas_call(
        body, name="att_fwd", grid=(NPAIR,),
        in_specs=[col(QB), col(KB), col(VB), pl.BlockSpec((S, 1), lambda p: (0, 0)),
                  pl.BlockSpec((1, LANES), lambda p: (0, 0))],
        out_specs=[out, out, out, out],
        out_shape=[jax.ShapeDtypeStruct((S, R), F32)] * 4,
        scratch_shapes=[pltpu.VMEM((S, LANES), BF16)] * 3 + [pltpu.VMEM((S, LANES), F32)] * 2
        + [pltpu.VMEM((3, S, LANES), F32)] * 2,
        compiler_params=_cp(("parallel",)),
    )(proj, proj, proj, pos, freq)


def _att_bwd_old(dproj, d_att, att, lse, qr, kr, proj, pos, freq):
    def body(dp_in, do_ref, o_ref, lse_ref, qr_ref, kr_ref, v_ref, pos_ref, freq_ref, dp_ref,
             qd, kd, vd, dod, lsd, prd, dqd, dkd, dvd, dqn, dkn, dvn, prn, stage, sems):
        p = pl.program_id(0)
        prn[...] = do_ref[...] * o_ref[...]
        dqn[...] = jnp.zeros_like(dqn)
        dkn[...] = jnp.zeros_like(dkn)
        dvn[...] = jnp.zeros_like(dvn)
        hm0 = lax.broadcasted_iota(jnp.int32, (BLK, LANES), 1) < HEAD

        for d in PATTERNS:
            nb = S // d // BLK
            _deint(qr_ref, qd, d)
            _deint(kr_ref, kd, d)
            _deint(v_ref, vd, d)
            _deint(do_ref, dod, d)
            _deint(lse_ref, lsd, d)
            _deint(prn, prd, d)
            dkd[...] = jnp.zeros_like(dkd)
            dvd[...] = jnp.zeros_like(dvd)

            def blk(b, carry):
                st = pl.multiple_of(b * BLK, BLK)
                stp = pl.multiple_of(jnp.maximum(b - 1, 0) * BLK, BLK)
                mc, mp = _blk_masks(b, nb)
                qb, dob = qd[pl.ds(st, BLK), :], dod[pl.ds(st, BLK), :]
                kc, kp = kd[pl.ds(st, BLK), :], kd[pl.ds(stp, BLK), :]
                vc, vp = vd[pl.ds(st, BLK), :], vd[pl.ds(stp, BLK), :]
                lsb, prb = lsd[pl.ds(st, BLK), :], prd[pl.ds(st, BLK), :]
                dqs = []
                dkc = dkp = dvc = dvp = None
                for hm in (hm0, jnp.logical_not(hm0)):
                    qm = jnp.where(hm, qb, jnp.zeros_like(qb))
                    dom = jnp.where(hm, dob, jnp.zeros_like(dob))
                    lh = jnp.max(jnp.where(hm, lsb, -3e38), axis=1, keepdims=True)
                    delta = jnp.sum(jnp.where(hm, prb, 0.0), axis=1, keepdims=True)
                    pc = jnp.where(mc, jnp.exp(_dot_nt(qm, kc) - lh), 0.0)
                    pp = jnp.where(mp, jnp.exp(_dot_nt(qm, kp) - lh), 0.0)
                    dsc = (pc * (_dot_nt(dom, vc) - delta)).astype(BF16)
                    dsp = (pp * (_dot_nt(dom, vp) - delta)).astype(BF16)
                    dqs.append(_dot(dsc, kc) + _dot(dsp, kp))
                    acc = lambda t, n: n if t is None else t + n
                    dkc, dkp = acc(dkc, _dot_tn(dsc, qm)), acc(dkp, _dot_tn(dsp, qm))
                    dvc, dvp = acc(dvc, _dot_tn(pc.astype(BF16), dom)), acc(dvp, _dot_tn(pp.astype(BF16), dom))
                dqd[pl.ds(st, BLK), :] = jnp.where(hm0, dqs[0], dqs[1])
                dkd[pl.ds(stp, BLK), :] += dkp
                dvd[pl.ds(stp, BLK), :] += dvp
                dkd[pl.ds(st, BLK), :] += dkc
                dvd[pl.ds(st, BLK), :] += dvc
                return carry

            lax.fori_loop(0, S // BLK, blk, 0)
            _reint(dqd, dqn, d, True)
            _reint(dkd, dkn, d, True)
            _reint(dvd, dvn, d, True)

        lane = lax.broadcasted_iota(jnp.int32, (S, LANES), 1)
        first = (lane & (HEAD // 2)) == 0
        cos, sin = _cos_sin(pos_ref, freq_ref)
        dq = dqn[...] * (HEAD ** -0.5)
        dk = dkn[...]
        stage[0] = (dq * cos - _rot_half(dq, first) * sin).astype(BF16)
        stage[1] = (dk * cos - _rot_half(dk, first) * sin).astype(BF16)
        stage[2] = dvn[...].astype(BF16)
        copies = [pltpu.make_async_copy(stage.at[j], dp_ref.at[:, pl.ds((2 + j) * R + p * LANES, LANES)], sems.at[j])
                  for j in range(3)]
        for cp in copies:
            cp.start()
        for cp in copies:
            cp.wait()

    blk = pl.BlockSpec((S, LANES), lambda p: (0, p))
    return pl.pallas_call(
        body, name="att_bwd", grid=(NPAIR,),
        in_specs=[pl.BlockSpec(memory_space=pl.ANY), blk, blk, blk, blk, blk,
                  pl.BlockSpec((S, LANES), lambda p: (0, VB + p)), pl.BlockSpec((S, 1), lambda p: (0, 0)),
                  pl.BlockSpec((1, LANES), lambda p: (0, 0))],
        out_specs=pl.BlockSpec(memory_space=pl.ANY),
        out_shape=jax.ShapeDtypeStruct((S, E), BF16),
        scratch_shapes=[pltpu.VMEM((S, LANES), BF16)] * 4 + [pltpu.VMEM((S, LANES), F32)] * 9
        + [pltpu.VMEM((3, S, LANES), BF16), pltpu.SemaphoreType.DMA((3,))],
        input_output_aliases={0: 0},
        compiler_params=_cp(("arbitrary",)),
    )(dproj, d_att, att, lse, qr, kr, proj, pos, freq)


def _out_fwd_bwd(ya, att, proj, w_out_bf, x, target, mod, norm_post, norm_att):
    ts = 256

    def body(ya_ref, att_ref, gb_ref, w_ref, x_ref, t_ref, mod_ref, npost_ref, natt_ref,
             gx_ref, dya_ref, datt_ref, dgb_ref, gw_ref, acc_ref):
        i = pl.program_id(0)

        @pl.when(i == 0)
        def _():
            gw_ref[...] = jnp.zeros_like(gw_ref)
            acc_ref[...] = jnp.zeros_like(acc_ref)

        gate = mod_ref[:, 2 * D:3 * D]
        att = att_ref[...]
        gb = gb_ref[...]
        sg = _sigmoid(gb)
        silu = gb * sg
        ybp = att * silu
        yb, ybn, rstd_b = _rms_fwd(ybp, natt_ref[...])
        cat = jnp.concatenate([ya_ref[...], yb.astype(BF16)], axis=1)
        mix = _dot(cat, w_ref[...])
        rn, mn, rstd_m = _rms_fwd(mix, npost_ref[...])
        err = x_ref[...] + gate * rn - t_ref[...]
        dy = err * (1.0 / D)
        gx_ref[...] = dy
        dmix, dnpost = _rms_bwd(dy * gate, mn, rstd_m, npost_ref[...])
        dmb = dmix.astype(BF16)
        gw_ref[...] += _dot_tn(cat, dmb)
        dcat = _dot_nt(dmb, w_ref[...])
        dya_ref[...] = dcat[:, 0:R]
        dybp, dnatt = _rms_bwd(dcat[:, R:2 * R], ybn, rstd_b, natt_ref[...])
        datt_ref[...] = dybp * silu
        dgb_ref[...] = (dybp * att * (sg * (1.0 + gb * (1.0 - sg)))).astype(BF16)
        acc_ref[0:1, :] += jnp.sum(dy * rn, axis=0, keepdims=True)
        acc_ref[1:2, :] += dnpost
        acc_ref[2:3, 0:R] += dnatt
        acc_ref[3:4, :] += jnp.sum(jnp.sum(err * err, axis=1, keepdims=True), axis=0, keepdims=True)

    tile = lambda w: pl.BlockSpec((ts, w), lambda i: (i, 0))
    c0 = lambda shape: pl.BlockSpec(shape, lambda i: (0, 0))
    return pl.pallas_call(
        body, name="out_fwd_bwd", grid=(S // ts,),
        in_specs=[tile(R), tile(R), pl.BlockSpec((ts, R), lambda i: (i, 5)), c0((D, D)), tile(D), tile(D),
                  c0((1, 3 * D)), c0((1, D)), c0((1, R))],
        out_specs=[tile(D), tile(R), tile(R), pl.BlockSpec((ts, R), lambda i: (i, 5)), c0((D, D)), c0((8, D))],
        out_shape=[jax.ShapeDtypeStruct((S, D), F32), jax.ShapeDtypeStruct((S, R), F32),
                   jax.ShapeDtypeStruct((S, R), F32), jax.ShapeDtypeStruct((S, E), BF16),
                   jax.ShapeDtypeStruct((D, D), F32), jax.ShapeDtypeStruct((8, D), F32)],
        compiler_params=_cp(("arbitrary",)),
    )(ya, att, proj, w_out_bf, x, target, mod, norm_post, norm_att)


UC = 256
UPC = EC // UC


NU = E // UC


def _unit_of_step(i):
    return (i % NCHIP) * UPC + i // NCHIP


def _in_proj_bwd(ht, dproj, w_in_bf, x, gx1, mod, norm_pre, smalls):
    ts = 256
    nt = S // ts
    half = D // 2
    units = [_unit_of_step(k) for k in range(NU)]
    owners = [u // UPC for u in units]
    ns = len(smalls)

    def body(*refs):
        (ht_ref, dpu_ref, dp_ref, w_ref, x_ref, gx1_ref, mod_ref, np_ref), refs = refs[:8], refs[8:]
        small_in, refs = refs[:ns], refs[ns:]
        (gx_ref, gin_ref), refs = refs[:2], refs[2:]
        small_out, (acc_out,), refs = refs[:ns], refs[ns:ns + 1], refs[ns + 1:]
        mine, sib, tmp, stage, got, red, acc_ref, hs, hr, ps, pr, bs, br = refs[:13]
        early = _SmallGather(small_in, small_out, *refs[13:16])
        late = _SmallGather([acc_ref], [acc_out], *refs[16:19])
        i = pl.program_id(0)
        xx, yy, c = _me()
        ci = 2 * xx + yy
        r0 = pl.multiple_of(c * half, half)
        r1 = pl.multiple_of((1 - c) * half, half)
        pl.when(i == 0)(early.start)
        pl.when(i == NU)(early.forward)

        def exch(k):
            return _remote(tmp.at[k % 2], sib.at[k], hs.at[k], hr.at[k], 1)

        def partial(k, sender):
            return pltpu.make_async_remote_copy(
                src_ref=stage.at[k], dst_ref=got.at[units[k] % UPC, sender], send_sem=ps.at[k],
                recv_sem=pr.at[k, sender], device_id=(owners[k] // 2, owners[k] % 2, c), device_id_type=MESH)

        def back(k, start):
            off = (units[k] % UPC) * UC
            blk = red.at[pl.ds(start, half), off:off + UC]
            return _remote(blk, blk, bs.at[k], br.at[k], 1)

        for k in range(NU + 1):
            @pl.when(i == k)
            def _():
                if k < NU:
                    if k >= 2:
                        exch(k - 2).wait_send()
                    dpu = dpu_ref[...]
                    tmp[k % 2] = _dot(ht_ref[pl.ds(r1, half), :], dpu)
                    exch(k).start()
                    mine[k] = _dot(ht_ref[pl.ds(r0, half), :], dpu)
                if k >= 1:
                    exch(k - 1).wait_recv()
                    mine[k - 1] += sib[k - 1]

                    @pl.when(ci != owners[k - 1])
                    def _():
                        stage[k - 1] = mine[k - 1].astype(BF16)
                        partial(k - 1, ci).start()

        @pl.when(i == NU)
        def _():
            acc_ref[...] = jnp.zeros_like(acc_ref)

        @pl.when(i >= NU)
        def _():
            dh = sum(_dot_nt(dp_ref[:, j * EC:(j + 1) * EC], w_ref[j]) for j in range(NCHIP))
            hp, xn, rstd = _rms_fwd(x_ref[...], np_ref[...])
            dx, dnp = _rms_bwd(dh * (1.0 + mod_ref[:, D:2 * D]), xn, rstd, np_ref[...])
            gx_ref[...] = gx1_ref[...] + dx
            acc_ref[0:1, :] += jnp.sum(dh, axis=0, keepdims=True)
            acc_ref[1:2, :] += jnp.sum(dh * hp, axis=0, keepdims=True)
            acc_ref[2:3, :] += dnp

        for t in range(UPC):
            @pl.when(i == NU + 1 + 2 * t)
            def _():
                for k in range(NCHIP * t, NCHIP * (t + 1)):
                    @pl.when(ci == owners[k])
                    def _():
                        off = (units[k] % UPC) * UC
                        red[pl.ds(r0, half), off:off + UC] = mine[k]
                        for s in range(NCHIP):
                            if s != owners[k]:
                                partial(k, s).wait_recv()
                                red[pl.ds(r0, half), off:off + UC] += got[units[k] % UPC, s].astype(F32)
                        back(k, r0).start()

        @pl.when(i == NU + nt - 1)
        def _():
            late.start()
            exch(NU - 2).wait_send()
            exch(NU - 1).wait_send()
            for k in range(NU):
                @pl.when(ci == owners[k])
                def _():
                    back(k, r1).wait_recv()
                    back(k, r0).wait_send()

                @pl.when(ci != owners[k])
                def _():
                    partial(k, ci).wait_send()
            gin_ref[...] = red[...]
            early.finish()
            late.forward()
            late.finish()

    tile = lambda w: pl.BlockSpec((ts, w), lambda i: (jnp.maximum(i - NU, 0), 0))
    c0 = lambda shape: pl.BlockSpec(shape, lambda i: (0, 0))
    vm = pl.BlockSpec(memory_space=pltpu.VMEM)
    hbm = pl.BlockSpec(memory_space=pl.ANY)
    gathered = [jax.ShapeDtypeStruct((NDEV,) + a.shape, F32) for a in smalls] + [jax.ShapeDtypeStruct((NDEV, 8, D), F32)]
    return pl.pallas_call(
        body, name="in_proj_bwd", grid=(NU + nt,),
        in_specs=[vm, pl.BlockSpec((S, UC), lambda i: (0, _unit_of_step(jnp.minimum(i, NU - 1)))), tile(E),
                  vm, tile(D), tile(D), c0((1, 3 * D)), c0((1, D))] + [vm] * ns,
        out_specs=[tile(D), vm] + [hbm] * (ns + 1),
        out_shape=[jax.ShapeDtypeStruct((S, D), F32), jax.ShapeDtypeStruct((D, EC), F32)] + gathered,
        scratch_shapes=[pltpu.VMEM((NU, half, UC), F32), pltpu.VMEM((NU, half, UC), F32),
                        pltpu.VMEM((2, half, UC), F32), pltpu.VMEM((NU, half, UC), BF16),
                        pltpu.VMEM((UPC, NCHIP, half, UC), BF16), pltpu.VMEM((D, EC), F32), pltpu.VMEM((8, D), F32),
                        pltpu.SemaphoreType.DMA((NU,)), pltpu.SemaphoreType.DMA((NU,)),
                        pltpu.SemaphoreType.DMA((NU,)), pltpu.SemaphoreType.DMA((NU, NCHIP)),
                        pltpu.SemaphoreType.DMA((NU,)), pltpu.SemaphoreType.DMA((NU,))]
        + _SmallGather.sems(ns) + _SmallGather.sems(1),
        compiler_params=_cp(("arbitrary",)),
    )(ht, dproj, dproj, w_in_bf, x, gx1, mod, norm_pre, *smalls)


def _block_diag(w):
    n, b, _ = w.shape
    eye = jnp.eye(n, dtype=w.dtype)
    return (eye[:, None, :, None] * w[:, :, None, :]).reshape(n * b, n * b)


def _diag_blocks(m):
    n, b = R // HEAD, HEAD
    return jnp.stack([m[h * b:(h + 1) * b, h * b:(h + 1) * b] for h in range(n)])


def _local_step(x, cos, sin, target, mod, w_in_bf, proj, ht, w_out, conv_w, p):
    rec_p = (conv_w, p["conv_b"], p["w_rg_a"], p["b_rg_a"], p["w_rg_x"], p["b_rg_x"], p["lru_lambda"], p["norm_rec"])
    h_all, ya = _rec_fwd(proj, *rec_p)
    att, qr, kr, lse, w_out_bf = _att_fwd(proj, cos, sin, w_out)
    gx1, d_ya, d_att, dproj, gw_out, acc_o = _out_fwd_bwd(ya, att, proj, w_out_bf.reshape(D, D), x, target, mod,
                                                           p["norm_post"], p["norm_att"])
    dproj, g_out = _att_bwd(dproj, d_att, att, lse, qr, kr, proj, cos, sin, gw_out.reshape(NCHIP, D // NCHIP, D))
    dproj, dwa, dwx, sm = _rec_bwd(dproj, d_ya, proj, h_all, *rec_p)
    grad_x, g_in, *gathered = _in_proj_bwd(ht, dproj, w_in_bf, x, gx1, mod, p["norm_pre"], [acc_o, sm, dwa, dwx])
    return grad_x, g_in, g_out, gathered


def _me():
    return lax.axis_index("x"), lax.axis_index("y"), lax.axis_index("c")


def _flip(v, bit):
    return 1 - v if bit else v


def _peer(rel):
    x, y, c = _me()
    return (_flip(x, rel & 4), _flip(y, rel & 2), _flip(c, rel & 1))


def _remote(src, dst, send_sem, recv_sem, rel):
    return pltpu.make_async_remote_copy(src_ref=src, dst_ref=dst, send_sem=send_sem, recv_sem=recv_sem,
                                        device_id=_peer(rel), device_id_type=MESH)


def _allgather_rows(row, name):
    w = row.shape[1]

    def body(row_ref, out_ref, send_sems, recv_sems, local_sem):
        x, y, c = _me()
        me = 4 * x + 2 * y + c
        mine = pltpu.make_async_copy(row_ref, out_ref.at[pl.ds(me, 1), :], local_sem)
        mine.start()
        sends = [_remote(row_ref, out_ref.at[pl.ds(me, 1), :], send_sems.at[r - 1], recv_sems.at[r - 1], r)
                 for r in range(1, NDEV)]
        for cp in sends:
            cp.start()
        for r in range(1, NDEV):
            px, py, pc = _peer(r)
            src = 4 * px + 2 * py + pc
            _remote(row_ref, out_ref.at[pl.ds(src, 1), :], send_sems.at[r - 1], recv_sems.at[r - 1], r).wait_recv()
        for cp in sends:
            cp.wait_send()
        mine.wait()

    return pl.pallas_call(
        body, name=name,
        in_specs=[pl.BlockSpec(memory_space=pltpu.VMEM)],
        out_specs=pl.BlockSpec(memory_space=pltpu.VMEM),
        out_shape=jax.ShapeDtypeStruct((NDEV, w), row.dtype),
        scratch_shapes=[pltpu.SemaphoreType.DMA((NDEV - 1,)), pltpu.SemaphoreType.DMA((NDEV - 1,)),
                        pltpu.SemaphoreType.DMA],
        compiler_params=pltpu.CompilerParams(vmem_limit_bytes=VMEM_LIMIT),
    )(row)


class _WeightGather:
    SEMS = [pltpu.SemaphoreType.DMA((NCHIP - 1,))] * 4

    def __init__(self, w_ref, out_ref, send_sems, recv_sems, fsend_sems, frecv_sems):
        x, y, c = _me()
        self.w, self.out, self.ci = w_ref, out_ref, 2 * x + y
        self.half = w_ref.shape[0] // 2
        self.r0 = pl.multiple_of(c * self.half, self.half)
        self.r1 = pl.multiple_of((1 - c) * self.half, self.half)
        self.sems = (send_sems, recv_sems, fsend_sems, frecv_sems)

    def _ici(self, chip, k):
        blk = self.out.at[chip, pl.ds(self.r0, self.half), :]
        return _remote(blk, blk, self.sems[0].at[k - 1], self.sems[1].at[k - 1], 2 * k)

    def _d2d(self, chip, start, k):
        blk = self.out.at[chip, pl.ds(start, self.half), :]
        return _remote(blk, blk, self.sems[2].at[k - 1], self.sems[3].at[k - 1], 1)

    def start(self, diagonal=True):
        self.out[self.ci] = self.w[...].astype(BF16)
        for k in range(1, NCHIP if diagonal else NCHIP - 1):
            self._ici(self.ci, k).start()

    def start_diagonal(self):
        for k in range(1, NCHIP - 1):
            self._ici(self.ci, k).wait_send()
        self._ici(self.ci, NCHIP - 1).start()

    def forward(self):
        for k in range(1, NCHIP):
            self._ici(self.ci ^ k, k).wait_recv()
            self._d2d(self.ci ^ k, self.r0, k).start()

    def finish(self):
        for k in range(1, NCHIP):
            self._d2d(self.ci ^ k, self.r1, k).wait_recv()
        self.finish_sends()

    def arrive(self, k):
        self._ici(self.ci ^ k, k).wait_recv()
        self._d2d(self.ci ^ k, self.r0, k).start()
        self._d2d(self.ci ^ k, self.r1, k).wait_recv()

    def finish_sends(self, after_start_diagonal=False):
        for k in range(1, NCHIP):
            if not (after_start_diagonal and k < NCHIP - 1):
                self._ici(self.ci, k).wait_send()
            self._d2d(self.ci ^ k, self.r0, k).wait_send()


class _SmallGather:
    @staticmethod
    def sems(n):
        return [pltpu.SemaphoreType.DMA((n, 7)), pltpu.SemaphoreType.DMA((n, 7)), pltpu.SemaphoreType.DMA((n,))]

    def __init__(self, srcs, outs, send_sems, recv_sems, local_sems):
        x, y, c = _me()
        self.srcs, self.outs = list(srcs), list(outs)
        self.ss, self.rs, self.ls = send_sems, recv_sems, local_sems
        self.ci, self.c = 2 * x + y, c
        self.me = 2 * self.ci + c

    def _own(self, a, slot, rel):
        return _remote(self.srcs[a], self.outs[a].at[self.me], self.ss.at[a, slot], self.rs.at[a, slot], rel)

    def _block(self, a, idx, slot, rel):
        blk = self.outs[a].at[idx]
        return _remote(blk, blk, self.ss.at[a, slot], self.rs.at[a, slot], rel)

    def _local(self, a):
        return pltpu.make_async_copy(self.srcs[a], self.outs[a].at[self.me], self.ls.at[a])

    def start(self):
        for a in range(len(self.srcs)):
            self._local(a).start()
            self._own(a, 0, 1).start()
            for k in range(1, NCHIP):
                self._own(a, k, 2 * k).start()

    def forward(self):
        for a in range(len(self.srcs)):
            for k in range(1, NCHIP):
                idx = 2 * (self.ci ^ k) + self.c
                self._block(a, idx, k, 2 * k).wait_recv()
                self._block(a, idx, 3 + k, 1).start()

    def finish(self):
        for a in range(len(self.srcs)):
            self._block(a, 2 * self.ci + 1 - self.c, 0, 1).wait_recv()
            for k in range(1, NCHIP):
                self._block(a, 2 * (self.ci ^ k) + 1 - self.c, 3 + k, 1).wait_recv()
            self._own(a, 0, 1).wait_send()
            for k in range(1, NCHIP):
                self._own(a, k, 2 * k).wait_send()
                self._block(a, 2 * (self.ci ^ k) + self.c, 3 + k, 1).wait_send()
            self._local(a).wait()


def _start_in_proj(crow, w_ada, b_cols, w_in, pos, x, norm_pre, order):
    ts = 256
    nt = S // ts
    wc = crow.shape[1]

    def body(order_ref, crow_ref, wada_ref, b_ref, win_ref, pos_ref, freq_ref, x_ref, np_ref,
             g0_ref, mod_ref, wbf_ref, cos_ref, sin_ref, proj_ref, ht_ref,
             g0s, modp, modb, wbuf, hb_all, cs, cr, ms, mr, ws, wr, fs, fr, local_sems):
        s, t = pl.program_id(0), pl.program_id(1)
        x, y, c = _me()
        ci = 2 * x + y
        me = 2 * ci + c
        wg = _WeightGather(win_ref, wbuf, ws, wr, fs, fr)

        @pl.when(jnp.logical_and(s == 0, t == 0))
        def _():
            wg.start(diagonal=False)
            mine = pltpu.make_async_copy(crow_ref, g0s.at[pl.ds(me, 1), :], local_sems.at[0])
            mine.start()
            csend = [_remote(crow_ref, g0s.at[pl.ds(me, 1), :], cs.at[r - 1], cr.at[r - 1], r) for r in range(1, NDEV)]
            for cp in csend:
                cp.start()
            cos_ref[...], sin_ref[...] = _cos_sin(pos_ref, freq_ref)
            for r in range(1, NDEV):
                px, py, pc = _peer(r)
                _remote(crow_ref, g0s.at[pl.ds(4 * px + 2 * py + pc, 1), :], cs.at[r - 1], cr.at[r - 1], r).wait_recv()
            mine.wait()
            cv = g0s[:, 0:D]
            sc = cv * _sigmoid(cv)
            scb = jnp.concatenate([sc, jnp.zeros_like(sc)], axis=0).astype(BF16)
            modp[...] = _dot(scb, wada_ref[...].astype(BF16))[0:NDEV, :] + b_ref[...]
            own = pltpu.make_async_copy(modp.at[pl.ds(me, 1), :], modb.at[ci], local_sems.at[1])
            own.start()
            msend = []
            for k in range(1, NCHIP):
                cp = _remote(modp.at[pl.ds(2 * (ci ^ k) + c, 1), :], modb.at[ci], ms.at[k - 1], mr.at[k - 1], 2 * k)
                cp.start()
                msend.append(cp)
            for k in range(1, NCHIP):
                _remote(modp.at[pl.ds(me, 1), :], modb.at[ci ^ k], ms.at[k - 1], mr.at[k - 1], 2 * k).wait_recv()
            own.wait()
            for j in range(NCHIP):
                mod_ref[:, j * EC:(j + 1) * EC] = modb[j]
            for cp in csend + msend:
                cp.wait_send()
            g0_ref[...] = g0s[...]

        for k in range(1, NCHIP):
            @pl.when(jnp.logical_and(s == k, t == 0))
            def _():
                if k == 1:
                    wg.start_diagonal()
                wg.arrive(k)

        rows = pl.ds(pl.multiple_of(t * ts, ts), ts)

        @pl.when(s == 0)
        def _():
            hp, _, _ = _rms_fwd(x_ref[...], np_ref[...])
            h = hp * (1.0 + mod_ref[:, D:2 * D]) + mod_ref[:, 0:D]
            hb_all[rows, :] = h.astype(BF16)
            ht_ref[...] = h.T.astype(BF16)

        proj_ref[...] = _dot(hb_all[rows, :], wbuf[ci ^ s])

        @pl.when(jnp.logical_and(s == NCHIP - 1, t == nt - 1))
        def _():
            wg.finish_sends(after_start_diagonal=True)
            wbf_ref[...] = wbuf[...]

    vm = pl.BlockSpec(memory_space=pltpu.VMEM)
    first_pass = lambda s, t: jnp.where(s == 0, t, nt - 1)
    grid_spec = pltpu.PrefetchScalarGridSpec(
        num_scalar_prefetch=1, grid=(NCHIP, nt),
        in_specs=[vm, vm, vm, vm, vm, vm, pl.BlockSpec((ts, D), lambda s, t, o: (first_pass(s, t), 0)),
                  pl.BlockSpec((1, D), lambda s, t, o: (0, 0))],
        out_specs=[vm, vm, vm, vm, vm, pl.BlockSpec((ts, EC), lambda s, t, o: (t, o[s])),
                   pl.BlockSpec((D, ts), lambda s, t, o: (0, first_pass(s, t)))],
        scratch_shapes=[pltpu.VMEM((NDEV, wc), F32), pltpu.VMEM((NDEV, EC), F32), pltpu.VMEM((NCHIP, 1, EC), F32),
                        pltpu.VMEM((NCHIP, D, EC), BF16), pltpu.VMEM((S, D), BF16),
                        pltpu.SemaphoreType.DMA((NDEV - 1,)), pltpu.SemaphoreType.DMA((NDEV - 1,)),
                        pltpu.SemaphoreType.DMA((NCHIP - 1,)), pltpu.SemaphoreType.DMA((NCHIP - 1,))]
        + _WeightGather.SEMS + [pltpu.SemaphoreType.DMA((2,))])
    return pl.pallas_call(
        body, name="start_in_proj", grid_spec=grid_spec,
        out_shape=[jax.ShapeDtypeStruct((NDEV, wc), F32), jax.ShapeDtypeStruct((1, 3 * D), F32),
                   jax.ShapeDtypeStruct((NCHIP, D, EC), BF16), jax.ShapeDtypeStruct((S, LANES), F32),
                   jax.ShapeDtypeStruct((S, LANES), F32), jax.ShapeDtypeStruct((S, E), F32),
                   jax.ShapeDtypeStruct((D, S), BF16)],
        compiler_params=_cp(("arbitrary", "arbitrary")),
    )(order, crow, w_ada, b_cols, w_in, pos, _rope_freq(), x, norm_pre)


def _start_gather(crow, w_ada, b_cols, w_in):
    wc = crow.shape[1]

    def body(crow_ref, wada_ref, b_ref, win_ref, g0_ref, mod_ref, wbf_ref,
             modp, modb, cs, cr, ms, mr, ws, wr, fs, fr, local_sems):
        x, y, c = _me()
        ci = 2 * x + y
        me = 2 * ci + c
        wg = _WeightGather(win_ref, wbf_ref, ws, wr, fs, fr)
        mine = pltpu.make_async_copy(crow_ref, g0_ref.at[pl.ds(me, 1), :], local_sems.at[0])
        mine.start()
        csend = [_remote(crow_ref, g0_ref.at[pl.ds(me, 1), :], cs.at[r - 1], cr.at[r - 1], r) for r in range(1, NDEV)]
        for cp in csend:
            cp.start()
        wg.start()
        for r in range(1, NDEV):
            px, py, pc = _peer(r)
            _remote(crow_ref, g0_ref.at[pl.ds(4 * px + 2 * py + pc, 1), :], cs.at[r - 1], cr.at[r - 1], r).wait_recv()
        mine.wait()
        cv = g0_ref[:, 0:D]
        sc = cv * _sigmoid(cv)
        scb = jnp.concatenate([sc, jnp.zeros_like(sc)], axis=0).astype(BF16)
        modp[...] = _dot(scb, wada_ref[...].astype(BF16))[0:NDEV, :] + b_ref[...]
        own = pltpu.make_async_copy(modp.at[pl.ds(me, 1), :], modb.at[ci], local_sems.at[1])
        own.start()
        msend = []
        for k in range(1, NCHIP):
            dst = 2 * (ci ^ k) + c
            cp = _remote(modp.at[pl.ds(dst, 1), :], modb.at[ci], ms.at[k - 1], mr.at[k - 1], 2 * k)
            cp.start()
            msend.append(cp)
        for k in range(1, NCHIP):
            _remote(modp.at[pl.ds(me, 1), :], modb.at[ci ^ k], ms.at[k - 1], mr.at[k - 1], 2 * k).wait_recv()
        own.wait()
        for j in range(NCHIP):
            mod_ref[:, j * EC:(j + 1) * EC] = modb[j]
        wg.forward()
        wg.finish()
        for cp in csend + msend:
            cp.wait_send()

    vm = pl.BlockSpec(memory_space=pltpu.VMEM)
    return pl.pallas_call(
        body, name="start_gather",
        in_specs=[vm] * 4, out_specs=[vm] * 3,
        out_shape=[jax.ShapeDtypeStruct((NDEV, wc), F32), jax.ShapeDtypeStruct((1, 3 * D), F32),
                   jax.ShapeDtypeStruct((NCHIP, D, EC), BF16)],
        scratch_shapes=[pltpu.VMEM((NDEV, EC), F32), pltpu.VMEM((NCHIP, 1, EC), F32),
                        pltpu.SemaphoreType.DMA((NDEV - 1,)), pltpu.SemaphoreType.DMA((NDEV - 1,)),
                        pltpu.SemaphoreType.DMA((NCHIP - 1,)), pltpu.SemaphoreType.DMA((NCHIP - 1,))]
        + _WeightGather.SEMS + [pltpu.SemaphoreType.DMA((2,))],
        compiler_params=pltpu.CompilerParams(vmem_limit_bytes=VMEM_LIMIT),
    )(crow, w_ada, b_cols, w_in)


class _ReduceScatter:
    @staticmethod
    def scratch(n_units, rows, ucols, max_owned):
        half = rows // 2
        return [pltpu.VMEM((n_units, half, ucols), F32), pltpu.VMEM((n_units, half, ucols), BF16),
                pltpu.VMEM((max_owned, NCHIP, half, ucols), BF16),
                pltpu.SemaphoreType.DMA((2,)), pltpu.SemaphoreType.DMA((n_units,)),
                pltpu.SemaphoreType.DMA((n_units, NCHIP)), pltpu.SemaphoreType.DMA((n_units,)),
                pltpu.SemaphoreType.DMA((n_units,))]

    def __init__(self, g_ref, out_ref, units, sib, stage, got, sem1, send2, recv2, send3, recv3):
        x, y, c = _me()
        self.c, self.ci = c, 2 * x + y
        self.g, self.out, self.units = g_ref, out_ref, units
        self.sib, self.stage, self.got = sib, stage, got
        self.sem1, self.send2, self.recv2, self.send3, self.recv3 = sem1, send2, recv2, send3, recv3
        self.half = g_ref.shape[1] // 2
        self.ucols = g_ref.shape[2]
        self.r0 = pl.multiple_of(c * self.half, self.half)
        self.r1 = pl.multiple_of((1 - c) * self.half, self.half)
        self.slot0 = units[0][0]
        assert [u[0] for u in units] == list(range(self.slot0, self.slot0 + len(units)))
        seen = {}
        self.local = []
        for _, owner, _ in units:
            self.local.append(seen.get(owner, 0))
            seen[owner] = seen.get(owner, 0) + 1

    def _halves(self):
        n = len(self.units)
        return _remote(self.g.at[pl.ds(self.slot0, n), pl.ds(self.r1, self.half), :], self.sib,
                       self.sem1.at[0], self.sem1.at[1], 1)

    def _partial(self, i, sender):
        _, owner, _ = self.units[i]
        return pltpu.make_async_remote_copy(
            src_ref=self.stage.at[i], dst_ref=self.got.at[self.local[i], sender],
            send_sem=self.send2.at[i], recv_sem=self.recv2.at[i, sender],
            device_id=(owner // 2, owner % 2, self.c), device_id_type=MESH)

    def _back(self, i, start):
        off = self.units[i][2]
        blk = self.out.at[pl.ds(start, self.half), off:off + self.ucols]
        return _remote(blk, blk, self.send3.at[i], self.recv3.at[i], 1)

    def at_steps(self, step, start, send, reduce, finish, out_ref):
        @pl.when(step == start)
        def _():
            self.out[...] = jnp.zeros_like(self.out)
            self.start_halves()

        pl.when(step == send)(self.send_partials)
        pl.when(step == reduce)(self.reduce_owned)

        @pl.when(step == finish)
        def _():
            self.finish()
            out_ref[...] = self.out[...]

    def start_halves(self):
        self._halves().start()

    def send_partials(self):
        self._halves().wait_recv()
        for i, (slot, owner, _) in enumerate(self.units):
            @pl.when(self.ci != owner)
            def _():
                self.stage[i] = (self.g[slot, pl.ds(self.r0, self.half), :] + self.sib[i]).astype(BF16)
                self._partial(i, self.ci).start()

    def reduce_owned(self):
        for i, (slot, owner, off) in enumerate(self.units):
            @pl.when(self.ci == owner)
            def _():
                rows, cols = pl.ds(self.r0, self.half), slice(off, off + self.ucols)
                self.out[rows, cols] = self.g[slot, pl.ds(self.r0, self.half), :] + self.sib[i]
                for s in range(NCHIP):
                    if s != owner:
                        self._partial(i, s).wait_recv()
                        self.out[rows, cols] += self.got[self.local[i], s].astype(F32)
                self._back(i, self.r0).start()

    def finish(self):
        self._halves().wait_send()
        for i, (_, owner, _) in enumerate(self.units):
            @pl.when(self.ci == owner)
            def _():
                self._back(i, self.r1).wait_recv()
                self._back(i, self.r0).wait_send()

            @pl.when(self.ci != owner)
            def _():
                self._partial(i, self.ci).wait_send()


def _reduce_scatter(g4, name):
    _, rows, cols = g4.shape
    units = [(j, j, 0) for j in range(NCHIP)]

    def body(g_ref, out_ref, *scratch):
        rs = _ReduceScatter(g_ref, out_ref, units, *scratch)
        rs.start_halves()
        rs.send_partials()
        rs.reduce_owned()
        rs.finish()

    return pl.pallas_call(
        body, name=name,
        in_specs=[pl.BlockSpec(memory_space=pltpu.VMEM)],
        out_specs=pl.BlockSpec(memory_space=pltpu.VMEM),
        out_shape=jax.ShapeDtypeStruct((rows, cols), F32),
        scratch_shapes=_ReduceScatter.scratch(NCHIP, rows, cols, 1),
        compiler_params=pltpu.CompilerParams(vmem_limit_bytes=VMEM_LIMIT),
    )(g4)


def _silu_rows(c_ref):
    cv = c_ref[...]
    sc = cv * _sigmoid(cv)
    return jnp.concatenate([sc, jnp.zeros_like(sc)], axis=0).astype(BF16)


def _ada_fwd(cg, w_ada, b_cols):
    def body(c_ref, w_ref, b_ref, o_ref):
        o_ref[...] = _dot(_silu_rows(c_ref), w_ref[...].astype(BF16))[0:NDEV, :] + b_ref[...]

    return pl.pallas_call(body, name="ada_fwd", out_shape=jax.ShapeDtypeStruct((NDEV, EC), F32),
                          compiler_params=_cp())(cg, w_ada, b_cols)


def _ada_bwd(cg, dmod_cols):
    def body(c_ref, d_ref, o_ref):
        dm = d_ref[...]
        dmb = jnp.concatenate([dm, jnp.zeros_like(dm)], axis=0).astype(BF16)
        o_ref[...] = _dot_tn(_silu_rows(c_ref), dmb)

    return pl.pallas_call(body, name="ada_bwd", out_shape=jax.ShapeDtypeStruct((D, EC), F32),
                          compiler_params=_cp())(cg, dmod_cols)


def _sum_rows(g):
    def body(g_ref, o_ref):
        acc = g_ref[0:1, :]
        for r in range(1, NDEV):
            acc = acc + g_ref[r:r + 1, :]
        o_ref[...] = acc

    return pl.pallas_call(body, name="sum_rows", out_shape=jax.ShapeDtypeStruct((1, g.shape[1]), F32),
                          compiler_params=_cp())(g)


def _adamw(w, g, m, v, name):
    rows, cols = w.shape
    tr = 256 if rows % 256 == 0 else rows

    def body(w_ref, g_ref, m_ref, v_ref, d_ref, nm_ref, nv_ref):
        gv = g_ref[...]
        nm = B1 * m_ref[...] + (1.0 - B1) * gv
        nv = B2 * v_ref[...] + (1.0 - B2) * (gv * gv)
        m_hat = nm / (1.0 - B1 ** STEP)
        v_hat = nv / (1.0 - B2 ** STEP)
        d_ref[...] = (-LR) * (m_hat / (jnp.sqrt(v_hat) + ADAM_EPS) + WD * w_ref[...])
        nm_ref[...] = nm
        nv_ref[...] = nv

    spec = pl.BlockSpec((tr, cols), lambda i: (i, 0))
    return pl.pallas_call(
        body, name=name, grid=(rows // tr,), in_specs=[spec] * 4, out_specs=[spec] * 3,
        out_shape=[jax.ShapeDtypeStruct((rows, cols), F32)] * 3,
        compiler_params=_cp(("parallel",)),
    )(w, g, m, v)


def _adamw_values(w, g, m, v):
    nm = B1 * m + (1.0 - B1) * g
    nv = B2 * v + (1.0 - B2) * (g * g)
    m_hat = nm / (1.0 - B1 ** STEP)
    v_hat = nv / (1.0 - B2 ** STEP)
    return (-LR) * (m_hat / (jnp.sqrt(v_hat) + ADAM_EPS) + WD * w), nm, nv


NB = R // HEAD
SMALL = (("b_ada", (1, 3 * D)), ("norm_pre", (1, D)), ("norm_post", (1, D)), ("conv_w", (4, R // NCHIP)),
         ("conv_b", (1, R)), ("w_rg_a", (NB, HEAD, HEAD)), ("b_rg_a", (1, R)), ("w_rg_x", (NB, HEAD, HEAD)),
         ("b_rg_x", (1, R)), ("lru_lambda", (1, R)), ("norm_rec", (1, R)), ("norm_att", (1, R)))


def _small_update(ao8, sm8, dwa8, dwx8, ai8, cg, params):
    n = len(SMALL)

    def body(ao_ref, sm_ref, dwa_ref, dwx_ref, ai_ref, cg_ref, *refs):
        pin, pout, (gada_ref, loss_ref, dmod) = refs[:3 * n], refs[3 * n:7 * n], refs[7 * n:]
        xx, yy, _ = _me()
        ci = 2 * xx + yy

        def total(ref, *idx):
            acc = ref[(0,) + idx]
            for d in range(1, NDEV):
                acc = acc + ref[(d,) + idx]
            return acc

        row = lambda ref, r, lanes=slice(None): total(ref, slice(r, r + 1), lanes)
        mine = lambda parts: sum(jnp.where(ci == j, part, 0.0) for j, part in enumerate(parts))
        cw = R // NCHIP
        grads = {
            "b_ada": [jnp.concatenate([row(ai_ref, 0), row(ai_ref, 1), row(ao_ref, 0)], axis=1)],
            "norm_pre": [row(ai_ref, 2)], "norm_post": [row(ao_ref, 1)],
            "conv_w": [mine([row(sm_ref, 8 + r, slice(j * cw, (j + 1) * cw)) for j in range(NCHIP)]) for r in range(4)],
            "conv_b": [row(sm_ref, 4)], "b_rg_a": [row(sm_ref, 0)], "b_rg_x": [row(sm_ref, 1)],
            "lru_lambda": [row(sm_ref, 2)], "norm_rec": [row(sm_ref, 3)], "norm_att": [row(ao_ref, 2, slice(0, R))],
            "w_rg_a": [total(dwa_ref, h) for h in range(NB)], "w_rg_x": [total(dwx_ref, h) for h in range(NB)],
        }
        loss_ref[...] = row(ao_ref, 3, slice(0, LANES)) * (0.5 / D)
        for k, (name, shape) in enumerate(SMALL):
            w_ref, m_ref, v_ref = pin[3 * k:3 * k + 3]
            outs = pout[4 * k:4 * k + 4]
            for r, g in enumerate(grads[name]):
                at = (slice(None),) if len(grads[name]) == 1 else ((r,) if len(shape) == 3 else (slice(r, r + 1),))
                res = (g,) + _adamw_values(w_ref[at], g, m_ref[at], v_ref[at])
                for o_ref, val in zip(outs, res):
                    o_ref[at] = val
        for d in range(NDEV):
            dmod[d:d + 1, :] = jnp.concatenate([ai_ref[d, 0:1, :], ai_ref[d, 1:2, :], ao_ref[d, 0:1, :]], axis=1)
        cols = mine([dmod[:, j * EC:(j + 1) * EC] for j in range(NCHIP)])
        colsb = jnp.concatenate([cols, jnp.zeros_like(cols)], axis=0).astype(BF16)
        gada_ref[...] = _dot_tn(_silu_rows(cg_ref), colsb)

    shapes = [jax.ShapeDtypeStruct(s, F32) for _, s in SMALL]
    outs = pl.pallas_call(
        body, name="small_update",
        out_shape=[s for s in shapes for _ in range(4)] + [jax.ShapeDtypeStruct((D, EC), F32),
                                                           jax.ShapeDtypeStruct((1, LANES), F32)],
        scratch_shapes=[pltpu.VMEM((NDEV, 3 * D), F32)],
        compiler_params=_cp(),
    )(ao8, sm8, dwa8, dwx8, ai8, cg, *params)
    return outs[:4 * n], outs[4 * n], outs[4 * n + 1]


BIG = ("w_ada", "w_in", "w_out")
WEIGHTS = ("w_ada", "b_ada", "norm_pre", "norm_post", "w_in", "conv_w", "conv_b", "w_rg_a", "b_rg_a", "w_rg_x",
           "b_rg_x", "lru_lambda", "norm_rec", "norm_att", "w_out")


def kernel(x, c, positions, w_ada, b_ada, norm_pre, norm_post, w_in, conv_w, conv_b, w_rg_a, b_rg_a, w_rg_x, b_rg_x, lru_lambda, norm_rec, norm_att, w_out, loss_target, m_w_ada, m_b_ada, m_norm_pre, m_norm_post, m_w_in, m_conv_w, m_conv_b, m_w_rg_a, m_b_rg_a, m_w_rg_x, m_b_rg_x, m_lru_lambda, m_norm_rec, m_norm_att, m_w_out, v_w_ada, v_b_ada, v_norm_pre, v_norm_post, v_w_in, v_conv_w, v_conv_b, v_w_rg_a, v_b_rg_a, v_w_rg_x, v_b_rg_x, v_lru_lambda, v_norm_rec, v_norm_att, v_w_out):
    given = dict(locals())
    wts = {n: given[n] for n in WEIGHTS}
    ms = {n: given["m_" + n] for n in WEIGHTS}
    vs = {n: given["v_" + n] for n in WEIGHTS}
    xi, yi, _ = _me()
    chip = 2 * xi + yi
    cw_loc = R // NCHIP

    b_cols = lax.dynamic_slice(b_ada, (0, chip * EC), (1, EC))
    order = (chip ^ jnp.arange(NCHIP, dtype=jnp.int32)).astype(jnp.int32)
    g0, mod, w_in_bf, cos, sin, proj, ht = _start_in_proj(
        jnp.concatenate([c, conv_w.reshape(1, 4 * cw_loc)], axis=1), w_ada[0], b_cols, w_in[0],
        positions.reshape(S, 1), x[0], norm_pre, order)
    cg = g0[:, 0:D]
    conv_full = g0[0::2, D:].reshape(NCHIP, 4, cw_loc).transpose(1, 0, 2).reshape(4, R)

    p = dict(norm_pre=norm_pre, norm_post=norm_post, conv_b=conv_b, b_rg_a=b_rg_a, b_rg_x=b_rg_x,
             lru_lambda=lru_lambda, norm_rec=norm_rec, norm_att=norm_att, w_rg_a=w_rg_a[0], w_rg_x=w_rg_x[0])
    grad_x, g_in, g_out, gathered = _local_step(
        x[0], cos, sin, loss_target[0], mod, w_in_bf, proj, ht, w_out[0], conv_full, p)

    params = [d[n].reshape(shape) for n, shape in SMALL for d in (wts, ms, vs)]
    small_out, g_ada, loss_row = _small_update(*gathered, cg, params)
    grads = {"w_out": g_out, "w_in": g_in, "w_ada": g_ada}
    delta, new_m, new_v = {}, {}, {}
    for k, (n, _) in enumerate(SMALL):
        grads[n], delta[n], new_m[n], new_v[n] = small_out[4 * k:4 * k + 4]
    for n in BIG:
        delta[n], new_m[n], new_v[n] = _adamw(wts[n][0], grads[n], ms[n][0], vs[n][0], "adamw_" + n)
    out = lambda d: [d[n].reshape(wts[n].shape) for n in WEIGHTS]
    return (loss_row[0, 0], grad_x.reshape(x.shape), *out(grads), *out(delta), *out(new_m), *out(new_v))
```

```python
import functools

import numpy as np
import jax
import jax.numpy as jnp
from jax import lax
from jax.experimental import pallas as pl
from jax.experimental.pallas import tpu as pltpu

F32 = jnp.float32
BF16 = jnp.bfloat16

S = 2048
D = 1024
E = 3072
R = 512
NDEV = 8
NCHIP = 4
EC = 768
LRU_C = 8.0
EPS = 1e-6
NEG = -1e30
HEAD = 64
BLK = 128
PATTERNS = (1, 4, 16)
ROPE_THETA = 10000.0
LANES = 128
VMEM_LIMIT = 56 * 1024 * 1024

B1, B2, LR, WD, ADAM_EPS, STEP = 0.9, 0.999, 0.001, 0.01, 1e-8, 10
MESH = pl.DeviceIdType.MESH


def _cp(sem=None, **kw):
    return pltpu.CompilerParams(dimension_semantics=sem, vmem_limit_bytes=VMEM_LIMIT, **kw)


def _dot(a, b):
    return jnp.dot(a, b, preferred_element_type=F32)


def _dot_nt(a, b):
    return lax.dot_general(a, b, (((1,), (1,)), ((), ())), preferred_element_type=F32)


def _dot_tn(a, b):
    return lax.dot_general(a, b, (((0,), (0,)), ((), ())), preferred_element_type=F32)


def _sigmoid(x):
    return 1.0 / (1.0 + jnp.exp(-x))


def _expm1(x):
    poly = x * (1.0 + x * (0.5 + x * (1.0 / 6 + x * (1.0 / 24 + x * (1.0 / 120 + x * (1.0 / 720))))))
    return jnp.where(jnp.abs(x) < 0.3, poly, jnp.exp(x) - 1.0)


def _rms_fwd(v, g):
    rstd = lax.rsqrt(jnp.mean(v * v, axis=-1, keepdims=True) + EPS)
    vn = v * rstd
    return vn * g, vn, rstd


def _rms_bwd(dy, vn, rstd, g):
    dvn = dy * g
    dv = rstd * (dvn - vn * jnp.mean(dvn * vn, axis=-1, keepdims=True))
    return dv, jnp.sum(dy * vn, axis=0, keepdims=True)


def _in_proj_fwd(x, mod, norm_pre, w_in_bf):
    ts = 256

    def body(x_ref, mod_ref, np_ref, w_ref, proj_ref, ht_ref):
        hp, _, _ = _rms_fwd(x_ref[...], np_ref[...])
        h = hp * (1.0 + mod_ref[:, D:2 * D]) + mod_ref[:, 0:D]
        hb = h.astype(BF16)
        ht_ref[...] = h.T.astype(BF16)
        for j in range(NCHIP):
            proj_ref[:, j * EC:(j + 1) * EC] = _dot(hb, w_ref[j])

    return pl.pallas_call(
        body, name="in_proj_fwd", grid=(S // ts,),
        in_specs=[pl.BlockSpec((ts, D), lambda i: (i, 0)), pl.BlockSpec((1, 3 * D), lambda i: (0, 0)),
                  pl.BlockSpec((1, D), lambda i: (0, 0)), pl.BlockSpec((NCHIP, D, EC), lambda i: (0, 0, 0))],
        out_specs=[pl.BlockSpec((ts, E), lambda i: (i, 0)), pl.BlockSpec((D, ts), lambda i: (0, i))],
        out_shape=[jax.ShapeDtypeStruct((S, E), F32), jax.ShapeDtypeStruct((D, S), BF16)],
        compiler_params=_cp(("parallel",)),
    )(x, mod, norm_pre, w_in_bf)


RT = 256


def _shift_down(cur, prev8, j, row):
    if j == 0:
        return cur
    top = jnp.tile(pltpu.roll(prev8, j, 0), (RT // 8, 1))
    return jnp.where(row >= j, pltpu.roll(cur, j, 0), top)


def _shift_up(cur, next8, j, row):
    if j == 0:
        return cur
    bot = jnp.tile(pltpu.roll(next8, 8 - j, 0), (RT // 8, 1))
    return jnp.where(row < RT - j, pltpu.roll(cur, RT - j, 0), bot)


def _rec_gates(xp, xprev8, row, cw_ref, cb_ref, wa_ref, ba_ref, wx_ref, bx_ref, lam_ref):
    xa = cb_ref[...] + sum(cw_ref[3 - j:4 - j, :] * _shift_down(xp, xprev8, j, row) for j in range(4))
    xab = xa.astype(BF16)
    r = _sigmoid(_dot(xab, wa_ref[...]) + ba_ref[...])
    ig = _sigmoid(_dot(xab, wx_ref[...]) + bx_ref[...])
    nl = -lam_ref[...]
    sp = jnp.maximum(nl, 0.0) + jnp.log1p(jnp.exp(-jnp.abs(nl)))
    la = (-LRU_C) * r * sp
    a = jnp.exp(la)
    mult = jnp.sqrt(-_expm1(2.0 * la))
    return dict(xa=xa, xab=xab, r=r, ig=ig, sp=sp, la=la, a=a, mult=mult)


def _scan_fwd(a, u, row):
    sh = 1
    while sh < RT:
        a_s = jnp.where(row >= sh, pltpu.roll(a, sh, 0), 1.0)
        u_s = jnp.where(row >= sh, pltpu.roll(u, sh, 0), 0.0)
        u = a * u_s + u
        a = a * a_s
        sh *= 2
    return a, u


def _scan_bwd(al, g, row):
    sh = 1
    while sh < RT:
        al_s = jnp.where(row < RT - sh, pltpu.roll(al, RT - sh, 0), 1.0)
        g_s = jnp.where(row < RT - sh, pltpu.roll(g, RT - sh, 0), 0.0)
        g = g + al * g_s
        al = al * al_s
        sh *= 2
    return g


def _dense_from_blocks(blocks_ref, dense_ref):
    dense_ref[...] = jnp.zeros_like(dense_ref)
    for h in range(R // HEAD):
        dense_ref[h * HEAD:(h + 1) * HEAD, h * HEAD:(h + 1) * HEAD] = blocks_ref[h].astype(dense_ref.dtype)


def _rec_fwd(proj, conv_w, conv_b, wa_b, ba, wx_b, bx, lam, norm_rec):
    nt = S // RT

    def body(p_ref, cw_ref, cb_ref, wa_ref, ba_ref, wx_ref, bx_ref, lam_ref, nr_ref,
             h_ref, ya_ref, prev8, hc, wad, wxd):
        i = pl.program_id(0)

        @pl.when(i == 0)
        def _():
            prev8[...] = jnp.zeros_like(prev8)
            hc[...] = jnp.zeros_like(hc)
            _dense_from_blocks(wa_ref, wad)
            _dense_from_blocks(wx_ref, wxd)

        row = lax.broadcasted_iota(jnp.int32, (RT, R), 0)
        xp = p_ref[:, 0:R]
        ga = p_ref[:, R:2 * R]
        f = _rec_gates(xp, prev8[...], row, cw_ref, cb_ref, wad, ba_ref, wxd, bx_ref, lam_ref)
        u = f["mult"] * (f["ig"] * f["xa"])
        acum, hh = _scan_fwd(f["a"], u, row)
        h = hh + acum * hc[0:1, :]
        h_ref[...] = h
        hc[0:1, :] = h_ref[RT - 1:RT, :]
        prev8[...] = p_ref[RT - 8:RT, 0:R]
        yp = h * (ga * _sigmoid(ga))
        ya, _, _ = _rms_fwd(yp, nr_ref[...])
        ya_ref[...] = ya.astype(BF16)

    row1 = lambda n: pl.BlockSpec((1, n), lambda i: (0, 0))
    blocks = pl.BlockSpec((R // HEAD, HEAD, HEAD), lambda i: (0, 0, 0))
    return pl.pallas_call(
        body, name="rec_fwd", grid=(nt,),
        in_specs=[pl.BlockSpec((RT, 2 * R), lambda i: (i, 0)), pl.BlockSpec((4, R), lambda i: (0, 0)), row1(R),
                  blocks, row1(R), blocks, row1(R), row1(R), row1(R)],
        out_specs=[pl.BlockSpec((RT, R), lambda i: (i, 0)), pl.BlockSpec((RT, R), lambda i: (i, 0))],
        out_shape=[jax.ShapeDtypeStruct((S, R), F32), jax.ShapeDtypeStruct((S, R), BF16)],
        scratch_shapes=[pltpu.VMEM((8, R), F32), pltpu.VMEM((8, R), F32), pltpu.VMEM((R, R), BF16),
                        pltpu.VMEM((R, R), BF16)],
        compiler_params=_cp(("arbitrary",)),
    )(proj, conv_w, conv_b, wa_b, ba, wx_b, bx, lam, norm_rec)


def _rec_bwd(dproj, d_ya, proj, h_all, conv_w, conv_b, wa_b, ba, wx_b, bx, lam, norm_rec):
    nt = S // RT

    def body(dp_in, dya_ref, p_ref, pprev_ref, h_ref, hprev_ref, cw_ref, cb_ref, wab_ref, ba_ref, wxb_ref, bx_ref,
             lam_ref, nr_ref, dp_ref, dwab_ref, dwxb_ref, sm_ref, nxt8, cg, wa_ref, wx_ref, dwa_ref, dwx_ref):
        i = pl.program_id(0)
        ti = nt - 1 - i

        @pl.when(i == 0)
        def _():
            nxt8[...] = jnp.zeros_like(nxt8)
            cg[...] = jnp.zeros_like(cg)
            dwa_ref[...] = jnp.zeros_like(dwa_ref)
            dwx_ref[...] = jnp.zeros_like(dwx_ref)
            sm_ref[...] = jnp.zeros_like(sm_ref)
            _dense_from_blocks(wab_ref, wa_ref)
            _dense_from_blocks(wxb_ref, wx_ref)

        row = lax.broadcasted_iota(jnp.int32, (RT, R), 0)
        first = (ti > 0).astype(F32)
        xprev8 = pprev_ref[...] * first
        hprev8 = hprev_ref[...] * first
        xp = p_ref[:, 0:R]
        ga = p_ref[:, R:2 * R]
        f = _rec_gates(xp, xprev8, row, cw_ref, cb_ref, wa_ref, ba_ref, wx_ref, bx_ref, lam_ref)
        xa, r, ig, a, mult = f["xa"], f["r"], f["ig"], f["a"], f["mult"]
        h = h_ref[...]
        sg = _sigmoid(ga)
        gate = ga * sg
        yp = h * gate
        _, ypn, rstd = _rms_fwd(yp, nr_ref[...])
        d_yp, dnr = _rms_bwd(dya_ref[...], ypn, rstd, nr_ref[...])
        d_ga = d_yp * h * (sg * (1.0 + ga * (1.0 - sg)))
        dh = d_yp * gate + jnp.where(row == RT - 1, cg[0:1, :], 0.0)
        al = jnp.where(row < RT - 1, pltpu.roll(a, RT - 1, 0), 0.0)
        g = _scan_bwd(al, dh, row)
        cg[0:1, :] = jnp.sum(jnp.where(row == 0, a * g, 0.0), axis=0, keepdims=True)
        h_m1 = _shift_down(h, hprev8, 1, row)
        da = g * h_m1
        ix = ig * xa
        d_mult = g * ix
        d_ig = g * mult * xa
        d_xa = g * mult * ig
        d_la = da * a - d_mult * (a * a) / mult
        d_r = d_la * ((-LRU_C) * f["sp"])
        dsp = jnp.sum(d_la * ((-LRU_C) * r), axis=0, keepdims=True)
        dlam = dsp * (-_sigmoid(-lam_ref[...]))
        d_za = d_r * r * (1.0 - r)
        d_zx = d_ig * ig * (1.0 - ig)
        dzab = d_za.astype(BF16)
        dzxb = d_zx.astype(BF16)
        dwa_ref[...] += _dot_tn(f["xab"], dzab)
        dwx_ref[...] += _dot_tn(f["xab"], dzxb)
        d_xa = d_xa + _dot_nt(dzab, wa_ref[...]) + _dot_nt(dzxb, wx_ref[...])
        d_xp = sum(cw_ref[3 - j:4 - j, :] * _shift_up(d_xa, nxt8[...], j, row) for j in range(4))
        dcw = [jnp.sum(d_xa * _shift_down(xp, xprev8, 3 - k, row), axis=0, keepdims=True) for k in range(4)]
        dp_ref[:, 0:R] = d_xp.astype(BF16)
        dp_ref[:, R:2 * R] = d_ga.astype(BF16)
        dp8 = d_xa[0:8, :]
        nxt8[...] = dp8
        sm_ref[0:1, :] += jnp.sum(d_za, axis=0, keepdims=True)
        sm_ref[1:2, :] += jnp.sum(d_zx, axis=0, keepdims=True)
        sm_ref[2:3, :] += dlam
        sm_ref[3:4, :] += dnr
        sm_ref[4:5, :] += jnp.sum(d_xa, axis=0, keepdims=True)
        for k in range(4):
            sm_ref[8 + k:9 + k, :] += dcw[k]

        @pl.when(i == nt - 1)
        def _():
            for h in range(R // HEAD):
                dwab_ref[h] = dwa_ref[h * HEAD:(h + 1) * HEAD, h * HEAD:(h + 1) * HEAD].astype(BF16)
                dwxb_ref[h] = dwx_ref[h * HEAD:(h + 1) * HEAD, h * HEAD:(h + 1) * HEAD].astype(BF16)

    c0 = lambda shape: pl.BlockSpec(shape, lambda i: (0, 0))
    blocks = pl.BlockSpec((R // HEAD, HEAD, HEAD), lambda i: (0, 0, 0))
    rev = lambda i: nt - 1 - i
    prev8 = lambda i: (jnp.maximum((nt - 1 - i) * (RT // 8) - 1, 0), 0)
    return pl.pallas_call(
        body, name="rec_bwd", grid=(nt,),
        in_specs=[pl.BlockSpec(memory_space=pl.ANY),
                  pl.BlockSpec((RT, R), lambda i: (rev(i), 0)),
                  pl.BlockSpec((RT, 2 * R), lambda i: (rev(i), 0)), pl.BlockSpec((8, R), prev8),
                  pl.BlockSpec((RT, R), lambda i: (rev(i), 0)), pl.BlockSpec((8, R), prev8),
                  c0((4, R)), c0((1, R)), blocks, c0((1, R)), blocks, c0((1, R)), c0((1, R)), c0((1, R))],
        out_specs=[pl.BlockSpec((RT, 2 * R), lambda i: (rev(i), 0)), blocks, blocks, c0((16, R))],
        out_shape=[jax.ShapeDtypeStruct((S, E), BF16), jax.ShapeDtypeStruct((R // HEAD, HEAD, HEAD), BF16),
                   jax.ShapeDtypeStruct((R // HEAD, HEAD, HEAD), BF16), jax.ShapeDtypeStruct((16, R), F32)],
        scratch_shapes=[pltpu.VMEM((8, R), F32), pltpu.VMEM((8, R), F32), pltpu.VMEM((R, R), BF16),
                        pltpu.VMEM((R, R), BF16), pltpu.VMEM((R, R), F32), pltpu.VMEM((R, R), F32)],
        input_output_aliases={0: 0},
        compiler_params=_cp(("arbitrary",)),
    )(dproj, d_ya, proj, proj, h_all, h_all, conv_w, conv_b, wa_b, ba, wx_b, bx, lam, norm_rec)


NPAIR = R // LANES
QB, KB, VB, GB = 2 * R // LANES, 3 * R // LANES, 4 * R // LANES, 5 * R // LANES


def _rope_freq():
    half = HEAD // 2
    inv = np.float32(ROPE_THETA) ** (-(np.arange(half, dtype=np.float32) / np.float32(half)))
    return jnp.asarray(np.tile(inv.astype(np.float32), LANES // half)[None, :])


def _rot_half(x, first):
    return jnp.where(first, -pltpu.roll(x, LANES - HEAD // 2, 1), pltpu.roll(x, HEAD // 2, 1))


def _cos_sin(pos_ref, freq_ref):
    ang = pos_ref[...].astype(F32) * freq_ref[...]
    return jnp.cos(ang), jnp.sin(ang)


def _deint(src_ref, dst_ref, d):
    n = S // d
    for r in range(d):
        v = src_ref[pl.ds(r, n, stride=d), :] if d > 1 else src_ref[...]
        dst_ref[r * n:(r + 1) * n, :] = v.astype(dst_ref.dtype)


def _reint(src_ref, dst_ref, d, accumulate):
    n = S // d
    for r in range(d):
        idx = (pl.ds(r, n, stride=d), slice(None)) if d > 1 else (slice(None), slice(None))
        v = src_ref[r * n:(r + 1) * n, :]
        if accumulate:
            dst_ref[idx] = dst_ref[idx] + v
        else:
            dst_ref[idx] = v


def _blk_masks(b, nb):
    qi = lax.broadcasted_iota(jnp.int32, (BLK, BLK), 0)
    ki = lax.broadcasted_iota(jnp.int32, (BLK, BLK), 1)
    has_prev = lax.rem(b, nb) != 0
    return ki <= qi, jnp.logical_and(ki >= qi, has_prev)


def _rope_table(pos, freq):
    def body(pos_ref, freq_ref, cos_ref, sin_ref):
        cos_ref[...], sin_ref[...] = _cos_sin(pos_ref, freq_ref)

    return pl.pallas_call(body, name="rope_table", out_shape=[jax.ShapeDtypeStruct((S, LANES), F32)] * 2,
                          compiler_params=_cp())(pos, freq)


def _deint_heads(src_ref, dst0, dst1, d):
    n = S // d
    hm0 = lax.broadcasted_iota(jnp.int32, (n, LANES), 1) < HEAD
    for r in range(d):
        v = src_ref[pl.ds(r, n, stride=d), :] if d > 1 else src_ref[...]
        dst0[r * n:(r + 1) * n, :] = jnp.where(hm0, v, 0.0).astype(BF16)
        dst1[r * n:(r + 1) * n, :] = jnp.where(hm0, 0.0, v).astype(BF16)


def _reint_prev(src_ref, dst_ref, d):
    n = S // d
    if n == BLK:
        return
    for r in range(d):
        idx = (pl.ds(r, n - BLK, stride=d), slice(None)) if d > 1 else (slice(0, n - BLK), slice(None))
        dst_ref[idx] = dst_ref[idx] + src_ref[r * n + BLK:(r + 1) * n, :]


def _pair_masks():
    qi = lax.broadcasted_iota(jnp.int32, (BLK, 2 * BLK), 0)
    ki = lax.broadcasted_iota(jnp.int32, (BLK, 2 * BLK), 1) & (BLK - 1)
    return ki <= qi, ki >= qi


def _two(ref0, ref1, st, axis):
    return jnp.concatenate([ref0[pl.ds(st, BLK), :], ref1[pl.ds(st, BLK), :]], axis=axis)


ATT_UNROLL = 4


def _att_fwd(proj, cos, sin, w_out):
    def body(q_ref, k_ref, v_ref, cos_ref, sin_ref, w_ref, att_ref, qr_ref, kr_ref, lse_ref, wbf_ref,
             qd, kd0, kd1, vd0, vd1, od, ld, on, ln, wbuf, *wsems):
        wg = _WeightGather(w_ref, wbuf, *wsems)
        pl.when(pl.program_id(0) == 0)(wg.start)
        pl.when(pl.program_id(0) == 1)(wg.forward)
        lane = lax.broadcasted_iota(jnp.int32, (S, LANES), 1)
        first = (lane & (HEAD // 2)) == 0
        cos, sin = cos_ref[...], sin_ref[...]
        q = q_ref[...]
        k = k_ref[...]
        qr_ref[...] = (q * cos + _rot_half(q, first) * sin) * (HEAD ** -0.5)
        kr_ref[...] = k * cos + _rot_half(k, first) * sin
        hm0 = lax.broadcasted_iota(jnp.int32, (BLK, LANES), 1) < HEAD
        top = lax.broadcasted_iota(jnp.int32, (2 * BLK, LANES), 0) < BLK
        ones2 = (top == (lax.broadcasted_iota(jnp.int32, (2 * BLK, LANES), 1) < HEAD)).astype(BF16)
        mc2, mp2 = _pair_masks()

        for pi, d in enumerate(PATTERNS):
            nb = S // d // BLK
            _deint(qr_ref, qd, d)
            _deint_heads(kr_ref, kd0, kd1, d)
            _deint_heads(v_ref, vd0, vd1, d)

            def blk(b, carry):
                st = pl.multiple_of(b * BLK, BLK)
                qb = qd[pl.ds(st, BLK), :]
                sc = jnp.where(mc2, _dot_nt(qb, _two(kd0, kd1, st, 0)), NEG)
                mx = sc
                if nb > 1:
                    stp = pl.multiple_of(jnp.maximum(b - 1, 0) * BLK, BLK)
                    mp = jnp.logical_and(mp2, lax.rem(b, nb) != 0)
                    sp = jnp.where(mp, _dot_nt(qb, _two(kd0, kd1, stp, 0)), NEG)
                    mx = jnp.maximum(sc, sp)
                m0 = jnp.max(mx[:, 0:BLK], axis=1, keepdims=True)
                m1 = jnp.max(mx[:, BLK:2 * BLK], axis=1, keepdims=True)
                mf = jnp.concatenate([jnp.broadcast_to(m0, (BLK, BLK)), jnp.broadcast_to(m1, (BLK, BLK))], axis=1)
                o = _dot(jnp.exp(sc - mf).astype(BF16), jnp.concatenate([_two(vd0, vd1, st, 0), ones2], axis=1))
                if nb > 1:
                    o = o + _dot(jnp.exp(sp - mf).astype(BF16), jnp.concatenate([_two(vd0, vd1, stp, 0), ones2], axis=1))
                l = o[:, LANES:2 * LANES]
                od[pl.ds(st, BLK), :] = o[:, 0:LANES] / l
                ld[pl.ds(st, BLK), :] = jnp.where(hm0, m0, m1) + jnp.log(l)
                return carry

            lax.fori_loop(0, S // BLK, blk, 0, unroll=ATT_UNROLL)
            _reint(od, on.at[pi], d, False)
            _reint(ld, ln.at[pi], d, False)

        l0, l1, l2 = ln[0], ln[1], ln[2]
        m = jnp.maximum(jnp.maximum(l0, l1), l2)
        e0, e1, e2 = jnp.exp(l0 - m), jnp.exp(l1 - m), jnp.exp(l2 - m)
        den = e0 + e1 + e2
        att_ref[...] = (e0 * on[0] + e1 * on[1] + e2 * on[2]) / den
        lse_ref[...] = m + jnp.log(den)

        @pl.when(pl.program_id(0) == NPAIR - 1)
        def _():
            wg.finish()
            wbf_ref[...] = wbuf[...]

    col = lambda c0: pl.BlockSpec((S, LANES), lambda p: (0, c0 + p))
    out = pl.BlockSpec((S, LANES), lambda p: (0, p))
    tab = pl.BlockSpec((S, LANES), lambda p: (0, 0))
    vm = pl.BlockSpec(memory_space=pltpu.VMEM)
    return pl.pallas_call(
        body, name="att_fwd", grid=(NPAIR,),
        in_specs=[col(QB), col(KB), col(VB), tab, tab, vm],
        out_specs=[out, out, out, out, vm],
        out_shape=[jax.ShapeDtypeStruct((S, R), F32)] * 4 + [jax.ShapeDtypeStruct((NCHIP,) + w_out.shape, BF16)],
        scratch_shapes=[pltpu.VMEM((S, LANES), BF16)] * 5 + [pltpu.VMEM((S, LANES), F32)] * 2
        + [pltpu.VMEM((3, S, LANES), F32)] * 2 + [pltpu.VMEM((NCHIP,) + w_out.shape, BF16)] + _WeightGather.SEMS,
        compiler_params=_cp(("arbitrary",)),
    )(proj, proj, proj, cos, sin, w_out)


def _att_bwd(dproj, d_att, att, lse, qr, kr, proj, cos, sin, gw_out4):
    out_units = [(j, j, 0) for j in range(NCHIP)]

    nblk = S // BLK

    def body(dp_in, do_ref, o_ref, lse_ref, qr_ref, kr_ref, v_ref, cos_ref, sin_ref, gw_ref, dp_ref, gout_ref,
             qd, kd0, kd1, vd0, vd1, dod, kt, packn, packd, dqd, dkcd, dkpd, dvcd, dvpd,
             dqn, dkn, dvn, rows, trs, stage, sems, gred, *rs_scratch):
        p = pl.program_id(0)
        rs = _ReduceScatter(gw_ref, gred, out_units, *rs_scratch)
        for step, piece in enumerate((rs.start_halves, rs.send_partials, rs.reduce_owned)):
            pl.when(p == step)(piece)

        @pl.when(p == NPAIR - 1)
        def _():
            rs.finish()
            gout_ref[...] = gred[...]

        lane = lax.broadcasted_iota(jnp.int32, (S, LANES), 1)
        hms = lane < HEAD
        prod = do_ref[...] * o_ref[...]
        d0 = jnp.sum(jnp.where(hms, prod, 0.0), axis=1, keepdims=True)
        d1 = jnp.sum(jnp.where(hms, 0.0, prod), axis=1, keepdims=True)
        lse = lse_ref[...]
        quarter = HEAD // 2
        packn[...] = jnp.where(lane < quarter, lse,
                               jnp.where(hms, pltpu.roll(lse, LANES - quarter, 1), jnp.where(lane < 3 * quarter, d0, d1)))
        dqn[...] = jnp.zeros_like(dqn)
        dkn[...] = jnp.zeros_like(dkn)
        dvn[...] = jnp.zeros_like(dvn)
        hm0 = lax.broadcasted_iota(jnp.int32, (BLK, LANES), 1) < HEAD
        key = lax.broadcasted_iota(jnp.int32, (2 * BLK, BLK), 0) & (BLK - 1)
        qry = lax.broadcasted_iota(jnp.int32, (2 * BLK, BLK), 1)
        mct, mpt = key <= qry, key >= qry

        for d in PATTERNS:
            nb = S // d // BLK
            _deint(qr_ref, qd, d)
            _deint_heads(kr_ref, kd0, kd1, d)
            _deint_heads(v_ref, vd0, vd1, d)
            _deint(do_ref, dod, d)
            _deint(packn, packd, d)

            def blk(b, carry):
                st = pl.multiple_of(b * BLK, BLK)
                kt[b] = _two(kd0, kd1, st, 0).astype(F32).T.astype(BF16)
                trs[b] = packd[pl.ds(st, BLK), :].T
                for j in range(4):
                    rows[b, j:j + 1, :] = trs[b, j * quarter:j * quarter + 1, :]
                qb, dob = qd[pl.ds(st, BLK), :], dod[pl.ds(st, BLK), :]
                both = lambda j: jnp.concatenate([jnp.broadcast_to(rows[b, j:j + 1, :], (BLK, BLK)),
                                                  jnp.broadcast_to(rows[b, j + 1:j + 2, :], (BLK, BLK))], axis=0)
                lbt, dlt = both(0), both(2)

                def side(bk, mask):
                    stk = pl.multiple_of(bk * BLK, BLK)
                    k2, v2 = _two(kd0, kd1, stk, 0), _two(vd0, vd1, stk, 0)
                    pt = jnp.where(mask, jnp.exp(_dot_nt(k2, qb) - lbt), 0.0)
                    dst = (pt * (_dot_nt(v2, dob) - dlt)).astype(BF16)
                    rk, rv = _dot(dst, qb), _dot(pt.astype(BF16), dob)
                    return (_dot(kt[bk], dst), jnp.where(hm0, rk[0:BLK], rk[BLK:2 * BLK]),
                            jnp.where(hm0, rv[0:BLK], rv[BLK:2 * BLK]))

                dq_t, dkc, dvc = side(b, mct)
                if nb > 1:
                    dqp_t, dkp, dvp = side(jnp.maximum(b - 1, 0), jnp.logical_and(mpt, lax.rem(b, nb) != 0))
                    dq_t = dq_t + dqp_t
                    dkpd[pl.ds(st, BLK), :] = dkp
                    dvpd[pl.ds(st, BLK), :] = dvp
                dqd[pl.ds(st, BLK), :] = dq_t.T
                dkcd[pl.ds(st, BLK), :] = dkc
                dvcd[pl.ds(st, BLK), :] = dvc
                return carry

            lax.fori_loop(0, nblk, blk, 0, unroll=2 * ATT_UNROLL)
            _reint(dqd, dqn, d, True)
            _reint(dkcd, dkn, d, True)
            _reint(dvcd, dvn, d, True)
            _reint_prev(dkpd, dkn, d)
            _reint_prev(dvpd, dvn, d)

        lane = lax.broadcasted_iota(jnp.int32, (S, LANES), 1)
        first = (lane & (HEAD // 2)) == 0
        cos, sin = cos_ref[...], sin_ref[...]
        dq = dqn[...] * (HEAD ** -0.5)
        dk = dkn[...]
        stage[0] = (dq * cos - _rot_half(dq, first) * sin).astype(BF16)
        stage[1] = (dk * cos - _rot_half(dk, first) * sin).astype(BF16)
        stage[2] = dvn[...].astype(BF16)
        copies = [pltpu.make_async_copy(stage.at[j], dp_ref.at[:, pl.ds((2 + j) * R + p * LANES, LANES)], sems.at[j])
                  for j in range(3)]
        for cp in copies:
            cp.start()
        for cp in copies:
            cp.wait()

    blk = pl.BlockSpec((S, LANES), lambda p: (0, p))
    tab = pl.BlockSpec((S, LANES), lambda p: (0, 0))
    vm = pl.BlockSpec(memory_space=pltpu.VMEM)
    _, orows, ocols = gw_out4.shape
    return pl.pallas_call(
        body, name="att_bwd", grid=(NPAIR,),
        in_specs=[pl.BlockSpec(memory_space=pl.ANY), blk, blk, blk, blk, blk,
                  pl.BlockSpec((S, LANES), lambda p: (0, VB + p)), tab, tab, vm],
        out_specs=[pl.BlockSpec(memory_space=pl.ANY), vm],
        out_shape=[jax.ShapeDtypeStruct((S, E), BF16), jax.ShapeDtypeStruct((orows, ocols), F32)],
        scratch_shapes=[pltpu.VMEM((S, LANES), BF16)] * 6 + [pltpu.VMEM((nblk, LANES, 2 * BLK), BF16)]
        + [pltpu.VMEM((S, LANES), F32)] * 10
        + [pltpu.VMEM((nblk, 8, BLK), F32), pltpu.VMEM((nblk, LANES, BLK), F32)]
        + [pltpu.VMEM((3, S, LANES), BF16), pltpu.SemaphoreType.DMA((3,)), pltpu.VMEM((orows, ocols), F32)]
        + _ReduceScatter.scratch(NCHIP, orows, ocols, 1),
        input_output_aliases={0: 0},
        compiler_params=_cp(("arbitrary",)),
    )(dproj, d_att, att, lse, qr, kr, proj, cos, sin, gw_out4)


def _att_fwd_old(proj, pos, freq):
    def body(q_ref, k_ref, v_ref, pos_ref, freq_ref, att_ref, qr_ref, kr_ref, lse_ref,
             qd, kd, vd, od, ld, on, ln):
        lane = lax.broadcasted_iota(jnp.int32, (S, LANES), 1)
        first = (lane & (HEAD // 2)) == 0
        cos, sin = _cos_sin(pos_ref, freq_ref)
        q = q_ref[...]
        k = k_ref[...]
        qr_ref[...] = (q * cos + _rot_half(q, first) * sin) * (HEAD ** -0.5)
        kr_ref[...] = k * cos + _rot_half(k, first) * sin
        hm0 = lax.broadcasted_iota(jnp.int32, (BLK, LANES), 1) < HEAD

        for pi, d in enumerate(PATTERNS):
            nb = S // d // BLK
            _deint(qr_ref, qd, d)
            _deint(kr_ref, kd, d)
            _deint(v_ref, vd, d)

            def blk(b, carry):
                st = pl.multiple_of(b * BLK, BLK)
                stp = pl.multiple_of(jnp.maximum(b - 1, 0) * BLK, BLK)
                mc, mp = _blk_masks(b, nb)
                qb = qd[pl.ds(st, BLK), :]
                kc, kp = kd[pl.ds(st, BLK), :], kd[pl.ds(stp, BLK), :]
                vc, vp = vd[pl.ds(st, BLK), :], vd[pl.ds(stp, BLK), :]
                outs, lses = [], []
                for hm in (hm0, jnp.logical_not(hm0)):
                    qm = jnp.where(hm, qb, jnp.zeros_like(qb))
                    sc = jnp.where(mc, _dot_nt(qm, kc), NEG)
                    sp = jnp.where(mp, _dot_nt(qm, kp), NEG)
                    m = jnp.maximum(jnp.max(sc, axis=1, keepdims=True), jnp.max(sp, axis=1, keepdims=True))
                    pc, pp = jnp.exp(sc - m), jnp.exp(sp - m)
                    l = jnp.sum(pc, axis=1, keepdims=True) + jnp.sum(pp, axis=1, keepdims=True)
                    o = _dot(pc.astype(BF16), vc) + _dot(pp.astype(BF16), vp)
                    outs.append(o / l)
                    lses.append(m + jnp.log(l))
                od[pl.ds(st, BLK), :] = jnp.where(hm0, outs[0], outs[1])
                ld[pl.ds(st, BLK), :] = jnp.where(hm0, lses[0], lses[1])
                return carry

            lax.fori_loop(0, S // BLK, blk, 0)
            _reint(od, on.at[pi], d, False)
            _reint(ld, ln.at[pi], d, False)

        l0, l1, l2 = ln[0], ln[1], ln[2]
        m = jnp.maximum(jnp.maximum(l0, l1), l2)
        e0, e1, e2 = jnp.exp(l0 - m), jnp.exp(l1 - m), jnp.exp(l2 - m)
        den = e0 + e1 + e2
        att_ref[...] = (e0 * on[0] + e1 * on[1] + e2 * on[2]) / den
        lse_ref[...] = m + jnp.log(den)

    col = lambda c0: pl.BlockSpec((S, LANES), lambda p: (0, c0 + p))
    out = pl.BlockSpec((S, LANES), lambda p: (0, p))
    return pl.pallas_call(
        body, name="att_fwd", grid=(NPAIR,),
        in_specs=[col(QB), col(KB), col(VB), pl.BlockSpec((S, 1), lambda p: (0, 0)),
                  pl.BlockSpec((1, LANES), lambda p: (0, 0))],
        out_specs=[out, out, out, out],
        out_shape=[jax.ShapeDtypeStruct((S, R), F32)] * 4,
        scratch_shapes=[pltpu.VMEM((S, LANES), BF16)] * 3 + [pltpu.VMEM((S, LANES), F32)] * 2
        + [pltpu.VMEM((3, S, LANES), F32)] * 2,
        compiler_params=_cp(("parallel",)),
    )(proj, proj, proj, pos, freq)


def _att_bwd_old(dproj, d_att, att, lse, qr, kr, proj, pos, freq):
    def body(dp_in, do_ref, o_ref, lse_ref, qr_ref, kr_ref, v_ref, pos_ref, freq_ref, dp_ref,
             qd, kd, vd, dod, lsd, prd, dqd, dkd, dvd, dqn, dkn, dvn, prn, stage, sems):
        p = pl.program_id(0)
        prn[...] = do_ref[...] * o_ref[...]
        dqn[...] = jnp.zeros_like(dqn)
        dkn[...] = jnp.zeros_like(dkn)
        dvn[...] = jnp.zeros_like(dvn)
        hm0 = lax.broadcasted_iota(jnp.int32, (BLK, LANES), 1) < HEAD

        for d in PATTERNS:
            nb = S // d // BLK
            _deint(qr_ref, qd, d)
            _deint(kr_ref, kd, d)
            _deint(v_ref, vd, d)
            _deint(do_ref, dod, d)
            _deint(lse_ref, lsd, d)
            _deint(prn, prd, d)
            dkd[...] = jnp.zeros_like(dkd)
            dvd[...] = jnp.zeros_like(dvd)

            def blk(b, carry):
                st = pl.multiple_of(b * BLK, BLK)
                stp = pl.multiple_of(jnp.maximum(b - 1, 0) * BLK, BLK)
                mc, mp = _blk_masks(b, nb)
                qb, dob = qd[pl.ds(st, BLK), :], dod[pl.ds(st, BLK), :]
                kc, kp = kd[pl.ds(st, BLK), :], kd[pl.ds(stp, BLK), :]
                vc, vp = vd[pl.ds(st, BLK), :], vd[pl.ds(stp, BLK), :]
                lsb, prb = lsd[pl.ds(st, BLK), :], prd[pl.ds(st, BLK), :]
                dqs = []
                dkc = dkp = dvc = dvp = None
                for hm in (hm0, jnp.logical_not(hm0)):
                    qm = jnp.where(hm, qb, jnp.zeros_like(qb))
                    dom = jnp.where(hm, dob, jnp.zeros_like(dob))
                    lh = jnp.max(jnp.where(hm, lsb, -3e38), axis=1, keepdims=True)
                    delta = jnp.sum(jnp.where(hm, prb, 0.0), axis=1, keepdims=True)
                    pc = jnp.where(mc, jnp.exp(_dot_nt(qm, kc) - lh), 0.0)
                    pp = jnp.where(mp, jnp.exp(_dot_nt(qm, kp) - lh), 0.0)
                    dsc = (pc * (_dot_nt(dom, vc) - delta)).astype(BF16)
                    dsp = (pp * (_dot_nt(dom, vp) - delta)).astype(BF16)
                    dqs.append(_dot(dsc, kc) + _dot(dsp, kp))
                    acc = lambda t, n: n if t is None else t + n
                    dkc, dkp = acc(dkc, _dot_tn(dsc, qm)), acc(dkp, _dot_tn(dsp, qm))
                    dvc, dvp = acc(dvc, _dot_tn(pc.astype(BF16), dom)), acc(dvp, _dot_tn(pp.astype(BF16), dom))
                dqd[pl.ds(st, BLK), :] = jnp.where(hm0, dqs[0], dqs[1])
                dkd[pl.ds(stp, BLK), :] += dkp
                dvd[pl.ds(stp, BLK), :] += dvp
                dkd[pl.ds(st, BLK), :] += dkc
                dvd[pl.ds(st, BLK), :] += dvc
                return carry

            lax.fori_loop(0, S // BLK, blk, 0)
            _reint(dqd, dqn, d, True)
            _reint(dkd, dkn, d, True)
            _reint(dvd, dvn, d, True)

        lane = lax.broadcasted_iota(jnp.int32, (S, LANES), 1)
        first = (lane & (HEAD // 2)) == 0
        cos, sin = _cos_sin(pos_ref, freq_ref)
        dq = dqn[...] * (HEAD ** -0.5)
        dk = dkn[...]
        stage[0] = (dq * cos - _rot_half(dq, first) * sin).astype(BF16)
        stage[1] = (dk * cos - _rot_half(dk, first) * sin).astype(BF16)
        stage[2] = dvn[...].astype(BF16)
        copies = [pltpu.make_async_copy(stage.at[j], dp_ref.at[:, pl.ds((2 + j) * R + p * LANES, LANES)], sems.at[j])
                  for j in range(3)]
        for cp in copies:
            cp.start()
        for cp in copies:
            cp.wait()

    blk = pl.BlockSpec((S, LANES), lambda p: (0, p))
    return pl.pallas_call(
        body, name="att_bwd", grid=(NPAIR,),
        in_specs=[pl.BlockSpec(memory_space=pl.ANY), blk, blk, blk, blk, blk,
                  pl.BlockSpec((S, LANES), lambda p: (0, VB + p)), pl.BlockSpec((S, 1), lambda p: (0, 0)),
                  pl.BlockSpec((1, LANES), lambda p: (0, 0))],
        out_specs=pl.BlockSpec(memory_space=pl.ANY),
        out_shape=jax.ShapeDtypeStruct((S, E), BF16),
        scratch_shapes=[pltpu.VMEM((S, LANES), BF16)] * 4 + [pltpu.VMEM((S, LANES), F32)] * 9
        + [pltpu.VMEM((3, S, LANES), BF16), pltpu.SemaphoreType.DMA((3,))],
        input_output_aliases={0: 0},
        compiler_params=_cp(("arbitrary",)),
    )(dproj, d_att, att, lse, qr, kr, proj, pos, freq)


def _out_fwd_bwd(ya, att, proj, w_out_bf, x, target, mod, norm_post, norm_att):
    ts = 256

    def body(ya_ref, att_ref, gb_ref, w_ref, x_ref, t_ref, mod_ref, npost_ref, natt_ref,
             gx_ref, dya_ref, datt_ref, dgb_ref, gw_ref, acc_ref):
        i = pl.program_id(0)

        @pl.when(i == 0)
        def _():
            gw_ref[...] = jnp.zeros_like(gw_ref)
            acc_ref[...] = jnp.zeros_like(acc_ref)

        gate = mod_ref[:, 2 * D:3 * D]
        att = att_ref[...]
        gb = gb_ref[...]
        sg = _sigmoid(gb)
        silu = gb * sg
        ybp = att * silu
        yb, ybn, rstd_b = _rms_fwd(ybp, natt_ref[...])
        cat = jnp.concatenate([ya_ref[...], yb.astype(BF16)], axis=1)
        mix = _dot(cat, w_ref[...])
        rn, mn, rstd_m = _rms_fwd(mix, npost_ref[...])
        err = x_ref[...] + gate * rn - t_ref[...]
        dy = err * (1.0 / D)
        gx_ref[...] = dy
        dmix, dnpost = _rms_bwd(dy * gate, mn, rstd_m, npost_ref[...])
        dmb = dmix.astype(BF16)
        gw_ref[...] += _dot_tn(cat, dmb)
        dcat = _dot_nt(dmb, w_ref[...])
        dya_ref[...] = dcat[:, 0:R]
        dybp, dnatt = _rms_bwd(dcat[:, R:2 * R], ybn, rstd_b, natt_ref[...])
        datt_ref[...] = dybp * silu
        dgb_ref[...] = (dybp * att * (sg * (1.0 + gb * (1.0 - sg)))).astype(BF16)
        acc_ref[0:1, :] += jnp.sum(dy * rn, axis=0, keepdims=True)
        acc_ref[1:2, :] += dnpost
        acc_ref[2:3, 0:R] += dnatt
        acc_ref[3:4, :] += jnp.sum(jnp.sum(err * err, axis=1, keepdims=True), axis=0, keepdims=True)

    tile = lambda w: pl.BlockSpec((ts, w), lambda i: (i, 0))
    c0 = lambda shape: pl.BlockSpec(shape, lambda i: (0, 0))
    return pl.pallas_call(
        body, name="out_fwd_bwd", grid=(S // ts,),
        in_specs=[tile(R), tile(R), pl.BlockSpec((ts, R), lambda i: (i, 5)), c0((D, D)), tile(D), tile(D),
                  c0((1, 3 * D)), c0((1, D)), c0((1, R))],
        out_specs=[tile(D), tile(R), tile(R), pl.BlockSpec((ts, R), lambda i: (i, 5)), c0((D, D)), c0((8, D))],
        out_shape=[jax.ShapeDtypeStruct((S, D), F32), jax.ShapeDtypeStruct((S, R), F32),
                   jax.ShapeDtypeStruct((S, R), F32), jax.ShapeDtypeStruct((S, E), BF16),
                   jax.ShapeDtypeStruct((D, D), F32), jax.ShapeDtypeStruct((8, D), F32)],
        compiler_params=_cp(("arbitrary",)),
    )(ya, att, proj, w_out_bf, x, target, mod, norm_post, norm_att)


UC = 256
UPC = EC // UC


NU = E // UC


def _unit_of_step(i):
    return (i % NCHIP) * UPC + i // NCHIP


def _in_proj_bwd(ht, dproj, w_in_bf, x, gx1, mod, norm_pre, smalls):
    ts = 256
    nt = S // ts
    half = D // 2
    units = [_unit_of_step(k) for k in range(NU)]
    owners = [u // UPC for u in units]
    ns = len(smalls)

    def body(*refs):
        (ht_ref, dpu_ref, dp_ref, w_ref, x_ref, gx1_ref, mod_ref, np_ref), refs = refs[:8], refs[8:]
        small_in, refs = refs[:ns], refs[ns:]
        (gx_ref, gin_ref), refs = refs[:2], refs[2:]
        small_out, (acc_out,), refs = refs[:ns], refs[ns:ns + 1], refs[ns + 1:]
        mine, sib, tmp, stage, got, red, acc_ref, hs, hr, ps, pr, bs, br = refs[:13]
        early = _SmallGather(small_in, small_out, *refs[13:16])
        late = _SmallGather([acc_ref], [acc_out], *refs[16:19])
        i = pl.program_id(0)
        xx, yy, c = _me()
        ci = 2 * xx + yy
        r0 = pl.multiple_of(c * half, half)
        r1 = pl.multiple_of((1 - c) * half, half)
        pl.when(i == 0)(early.start)
        pl.when(i == NU)(early.forward)

        def exch(k):
            return _remote(tmp.at[k % 2], sib.at[k], hs.at[k], hr.at[k], 1)

        def partial(k, sender):
            return pltpu.make_async_remote_copy(
                src_ref=stage.at[k], dst_ref=got.at[units[k] % UPC, sender], send_sem=ps.at[k],
                recv_sem=pr.at[k, sender], device_id=(owners[k] // 2, owners[k] % 2, c), device_id_type=MESH)

        def back(k, start):
            off = (units[k] % UPC) * UC
            blk = red.at[pl.ds(start, half), off:off + UC]
            return _remote(blk, blk, bs.at[k], br.at[k], 1)

        for k in range(NU + 1):
            @pl.when(i == k)
            def _():
                if k < NU:
                    if k >= 2:
                        exch(k - 2).wait_send()
                    dpu = dpu_ref[...]
                    tmp[k % 2] = _dot(ht_ref[pl.ds(r1, half), :], dpu)
                    exch(k).start()
                    mine[k] = _dot(ht_ref[pl.ds(r0, half), :], dpu)
                if k >= 1:
                    exch(k - 1).wait_recv()
                    mine[k - 1] += sib[k - 1]

                    @pl.when(ci != owners[k - 1])
                    def _():
                        stage[k - 1] = mine[k - 1].astype(BF16)
                        partial(k - 1, ci).start()

        @pl.when(i == NU)
        def _():
            acc_ref[...] = jnp.zeros_like(acc_ref)

        @pl.when(i >= NU)
        def _():
            dh = sum(_dot_nt(dp_ref[:, j * EC:(j + 1) * EC], w_ref[j]) for j in range(NCHIP))
            hp, xn, rstd = _rms_fwd(x_ref[...], np_ref[...])
            dx, dnp = _rms_bwd(dh * (1.0 + mod_ref[:, D:2 * D]), xn, rstd, np_ref[...])
            gx_ref[...] = gx1_ref[...] + dx
            acc_ref[0:1, :] += jnp.sum(dh, axis=0, keepdims=True)
            acc_ref[1:2, :] += jnp.sum(dh * hp, axis=0, keepdims=True)
            acc_ref[2:3, :] += dnp

        for t in range(UPC):
            @pl.when(i == NU + 1 + 2 * t)
            def _():
                for k in range(NCHIP * t, NCHIP * (t + 1)):
                    @pl.when(ci == owners[k])
                    def _():
                        off = (units[k] % UPC) * UC
                        red[pl.ds(r0, half), off:off + UC] = mine[k]
                        for s in range(NCHIP):
                            if s != owners[k]:
                                partial(k, s).wait_recv()
                                red[pl.ds(r0, half), off:off + UC] += got[units[k] % UPC, s].astype(F32)
                        back(k, r0).start()

        @pl.when(i == NU + nt - 1)
        def _():
            late.start()
            exch(NU - 2).wait_send()
            exch(NU - 1).wait_send()
            for k in range(NU):
                @pl.when(ci == owners[k])
                def _():
                    back(k, r1).wait_recv()
                    back(k, r0).wait_send()

                @pl.when(ci != owners[k])
                def _():
                    partial(k, ci).wait_send()
            gin_ref[...] = red[...]
            early.finish()
            late.forward()
            late.finish()

    tile = lambda w: pl.BlockSpec((ts, w), lambda i: (jnp.maximum(i - NU, 0), 0))
    c0 = lambda shape: pl.BlockSpec(shape, lambda i: (0, 0))
    vm = pl.BlockSpec(memory_space=pltpu.VMEM)
    hbm = pl.BlockSpec(memory_space=pl.ANY)
    gathered = [jax.ShapeDtypeStruct((NDEV,) + a.shape, a.dtype) for a in smalls] + [jax.ShapeDtypeStruct((NDEV, 8, D), F32)]
    return pl.pallas_call(
        body, name="in_proj_bwd", grid=(NU + nt,),
        in_specs=[vm, pl.BlockSpec((S, UC), lambda i: (0, _unit_of_step(jnp.minimum(i, NU - 1)))), tile(E),
                  vm, tile(D), tile(D), c0((1, 3 * D)), c0((1, D))] + [vm] * ns,
        out_specs=[tile(D), vm] + [hbm] * (ns + 1),
        out_shape=[jax.ShapeDtypeStruct((S, D), F32), jax.ShapeDtypeStruct((D, EC), F32)] + gathered,
        scratch_shapes=[pltpu.VMEM((NU, half, UC), F32), pltpu.VMEM((NU, half, UC), F32),
                        pltpu.VMEM((2, half, UC), F32), pltpu.VMEM((NU, half, UC), BF16),
                        pltpu.VMEM((UPC, NCHIP, half, UC), BF16), pltpu.VMEM((D, EC), F32), pltpu.VMEM((8, D), F32),
                        pltpu.SemaphoreType.DMA((NU,)), pltpu.SemaphoreType.DMA((NU,)),
                        pltpu.SemaphoreType.DMA((NU,)), pltpu.SemaphoreType.DMA((NU, NCHIP)),
                        pltpu.SemaphoreType.DMA((NU,)), pltpu.SemaphoreType.DMA((NU,))]
        + _SmallGather.sems(ns) + _SmallGather.sems(1),
        compiler_params=_cp(("arbitrary",)),
    )(ht, dproj, dproj, w_in_bf, x, gx1, mod, norm_pre, *smalls)


def _block_diag(w):
    n, b, _ = w.shape
    eye = jnp.eye(n, dtype=w.dtype)
    return (eye[:, None, :, None] * w[:, :, None, :]).reshape(n * b, n * b)


def _diag_blocks(m):
    n, b = R // HEAD, HEAD
    return jnp.stack([m[h * b:(h + 1) * b, h * b:(h + 1) * b] for h in range(n)])


def _local_step(x, cos, sin, target, mod, w_in_bf, proj, ht, w_out, conv_w, p):
    rec_p = (conv_w, p["conv_b"], p["w_rg_a"], p["b_rg_a"], p["w_rg_x"], p["b_rg_x"], p["lru_lambda"], p["norm_rec"])
    h_all, ya = _rec_fwd(proj, *rec_p)
    att, qr, kr, lse, w_out_bf = _att_fwd(proj, cos, sin, w_out)
    gx1, d_ya, d_att, dproj, gw_out, acc_o = _out_fwd_bwd(ya, att, proj, w_out_bf.reshape(D, D), x, target, mod,
                                                           p["norm_post"], p["norm_att"])
    dproj, g_out = _att_bwd(dproj, d_att, att, lse, qr, kr, proj, cos, sin, gw_out.reshape(NCHIP, D // NCHIP, D))
    dproj, dwa, dwx, sm = _rec_bwd(dproj, d_ya, proj, h_all, *rec_p)
    grad_x, g_in, *gathered = _in_proj_bwd(ht, dproj, w_in_bf, x, gx1, mod, p["norm_pre"], [acc_o, sm, dwa, dwx])
    return grad_x, g_in, g_out, gathered


def _me():
    return lax.axis_index("x"), lax.axis_index("y"), lax.axis_index("c")


def _flip(v, bit):
    return 1 - v if bit else v


def _peer(rel):
    x, y, c = _me()
    return (_flip(x, rel & 4), _flip(y, rel & 2), _flip(c, rel & 1))


def _remote(src, dst, send_sem, recv_sem, rel):
    return pltpu.make_async_remote_copy(src_ref=src, dst_ref=dst, send_sem=send_sem, recv_sem=recv_sem,
                                        device_id=_peer(rel), device_id_type=MESH)


def _allgather_rows(row, name):
    w = row.shape[1]

    def body(row_ref, out_ref, send_sems, recv_sems, local_sem):
        x, y, c = _me()
        me = 4 * x + 2 * y + c
        mine = pltpu.make_async_copy(row_ref, out_ref.at[pl.ds(me, 1), :], local_sem)
        mine.start()
        sends = [_remote(row_ref, out_ref.at[pl.ds(me, 1), :], send_sems.at[r - 1], recv_sems.at[r - 1], r)
                 for r in range(1, NDEV)]
        for cp in sends:
            cp.start()
        for r in range(1, NDEV):
            px, py, pc = _peer(r)
            src = 4 * px + 2 * py + pc
            _remote(row_ref, out_ref.at[pl.ds(src, 1), :], send_sems.at[r - 1], recv_sems.at[r - 1], r).wait_recv()
        for cp in sends:
            cp.wait_send()
        mine.wait()

    return pl.pallas_call(
        body, name=name,
        in_specs=[pl.BlockSpec(memory_space=pltpu.VMEM)],
        out_specs=pl.BlockSpec(memory_space=pltpu.VMEM),
        out_shape=jax.ShapeDtypeStruct((NDEV, w), row.dtype),
        scratch_shapes=[pltpu.SemaphoreType.DMA((NDEV - 1,)), pltpu.SemaphoreType.DMA((NDEV - 1,)),
                        pltpu.SemaphoreType.DMA],
        compiler_params=pltpu.CompilerParams(vmem_limit_bytes=VMEM_LIMIT),
    )(row)


class _WeightGather:
    SEMS = [pltpu.SemaphoreType.DMA((NCHIP - 1,))] * 4

    def __init__(self, w_ref, out_ref, send_sems, recv_sems, fsend_sems, frecv_sems):
        x, y, c = _me()
        self.w, self.out, self.ci = w_ref, out_ref, 2 * x + y
        self.half = w_ref.shape[0] // 2
        self.r0 = pl.multiple_of(c * self.half, self.half)
        self.r1 = pl.multiple_of((1 - c) * self.half, self.half)
        self.sems = (send_sems, recv_sems, fsend_sems, frecv_sems)

    def _ici(self, chip, k):
        blk = self.out.at[chip, pl.ds(self.r0, self.half), :]
        return _remote(blk, blk, self.sems[0].at[k - 1], self.sems[1].at[k - 1], 2 * k)

    def _d2d(self, chip, start, k):
        blk = self.out.at[chip, pl.ds(start, self.half), :]
        return _remote(blk, blk, self.sems[2].at[k - 1], self.sems[3].at[k - 1], 1)

    def start(self, diagonal=True):
        self.out[self.ci] = self.w[...].astype(BF16)
        for k in range(1, NCHIP if diagonal else NCHIP - 1):
            self._ici(self.ci, k).start()

    def _relay(self, chip, piece, k):
        q = self.half // 2
        blk = self.out.at[chip, pl.ds(self.r0 + piece * q, q), :]
        return _remote(blk, blk, self.relay_sems[0].at[piece], self.relay_sems[1].at[piece], 2 * k)

    def neighbours_landed(self, relay_send_sems, relay_recv_sems):
        self.relay_sems = (relay_send_sems, relay_recv_sems)
        for k in (1, 2):
            self._ici(self.ci ^ k, k).wait_recv()
        self._relay(self.ci ^ 2, 0, 1).start()
        self._relay(self.ci ^ 1, 1, 2).start()
        for k in (1, 2):
            self._d2d(self.ci ^ k, self.r0, k).start()

    def sibling_landed(self, k):
        self._d2d(self.ci ^ k, self.r1, k).wait_recv()

    def diagonal_landed(self):
        for piece, k in ((0, 1), (1, 2)):
            self._relay(self.ci ^ 3, piece, k).wait_recv()
        self._d2d(self.ci ^ 3, self.r0, 3).start()
        self._d2d(self.ci ^ 3, self.r1, 3).wait_recv()

    def finish_relayed(self):
        for k in (1, 2):
            self._ici(self.ci, k).wait_send()
        self._relay(self.ci ^ 2, 0, 1).wait_send()
        self._relay(self.ci ^ 1, 1, 2).wait_send()
        for k in range(1, NCHIP):
            self._d2d(self.ci ^ k, self.r0, k).wait_send()

    def forward(self):
        for k in range(1, NCHIP):
            self._ici(self.ci ^ k, k).wait_recv()
            self._d2d(self.ci ^ k, self.r0, k).start()

    def finish(self):
        for k in range(1, NCHIP):
            self._d2d(self.ci ^ k, self.r1, k).wait_recv()
        self.finish_sends()

    def arrive(self, k):
        self._ici(self.ci ^ k, k).wait_recv()
        self._d2d(self.ci ^ k, self.r0, k).start()
        self._d2d(self.ci ^ k, self.r1, k).wait_recv()

    def finish_sends(self, after_start_diagonal=False):
        for k in range(1, NCHIP):
            if not (after_start_diagonal and k < NCHIP - 1):
                self._ici(self.ci, k).wait_send()
            self._d2d(self.ci ^ k, self.r0, k).wait_send()


class _SmallGather:
    @staticmethod
    def sems(n):
        return [pltpu.SemaphoreType.DMA((n, 7)), pltpu.SemaphoreType.DMA((n, 7)), pltpu.SemaphoreType.DMA((n,))]

    def __init__(self, srcs, outs, send_sems, recv_sems, local_sems):
        x, y, c = _me()
        self.srcs, self.outs = list(srcs), list(outs)
        self.ss, self.rs, self.ls = send_sems, recv_sems, local_sems
        self.ci, self.c = 2 * x + y, c
        self.me = 2 * self.ci + c

    def _own(self, a, slot, rel):
        return _remote(self.srcs[a], self.outs[a].at[self.me], self.ss.at[a, slot], self.rs.at[a, slot], rel)

    def _block(self, a, idx, slot, rel):
        blk = self.outs[a].at[idx]
        return _remote(blk, blk, self.ss.at[a, slot], self.rs.at[a, slot], rel)

    def _local(self, a):
        return pltpu.make_async_copy(self.srcs[a], self.outs[a].at[self.me], self.ls.at[a])

    def start(self):
        for a in range(len(self.srcs)):
            self._local(a).start()
            self._own(a, 0, 1).start()
            for k in range(1, NCHIP):
                self._own(a, k, 2 * k).start()

    def forward(self):
        for a in range(len(self.srcs)):
            for k in range(1, NCHIP):
                idx = 2 * (self.ci ^ k) + self.c
                self._block(a, idx, k, 2 * k).wait_recv()
                self._block(a, idx, 3 + k, 1).start()

    def finish(self):
        for a in range(len(self.srcs)):
            self._block(a, 2 * self.ci + 1 - self.c, 0, 1).wait_recv()
            for k in range(1, NCHIP):
                self._block(a, 2 * (self.ci ^ k) + 1 - self.c, 3 + k, 1).wait_recv()
            self._own(a, 0, 1).wait_send()
            for k in range(1, NCHIP):
                self._own(a, k, 2 * k).wait_send()
                self._block(a, 2 * (self.ci ^ k) + self.c, 3 + k, 1).wait_send()
            self._local(a).wait()


def _start_in_proj(crow, w_ada, b_cols, w_in, pos, x, norm_pre, order):
    ts = 256
    nt = S // ts
    wc = crow.shape[1]

    def body(order_ref, crow_ref, wada_ref, b_ref, win_ref, pos_ref, freq_ref, x_ref, np_ref,
             g0_ref, mod_ref, wbf_ref, cos_ref, sin_ref, proj_ref, ht_ref,
             g0s, modp, modb, wbuf, hb_all, cs, cr, ms, mr, ws, wr, fs, fr, local_sems, ys, yr):
        s, t = pl.program_id(0), pl.program_id(1)
        x, y, c = _me()
        ci = 2 * x + y
        me = 2 * ci + c
        wg = _WeightGather(win_ref, wbuf, ws, wr, fs, fr)

        @pl.when(jnp.logical_and(s == 0, t == 0))
        def _():
            wg.start(diagonal=False)
            mine = pltpu.make_async_copy(crow_ref, g0s.at[pl.ds(me, 1), :], local_sems.at[0])
            mine.start()
            csend = [_remote(crow_ref, g0s.at[pl.ds(me, 1), :], cs.at[r - 1], cr.at[r - 1], r) for r in range(1, NDEV)]
            for cp in csend:
                cp.start()
            cos_ref[...], sin_ref[...] = _cos_sin(pos_ref, freq_ref)
            for r in range(1, NDEV):
                px, py, pc = _peer(r)
                _remote(crow_ref, g0s.at[pl.ds(4 * px + 2 * py + pc, 1), :], cs.at[r - 1], cr.at[r - 1], r).wait_recv()
            mine.wait()
            cv = g0s[:, 0:D]
            sc = cv * _sigmoid(cv)
            scb = jnp.concatenate([sc, jnp.zeros_like(sc)], axis=0).astype(BF16)
            modp[...] = _dot(scb, wada_ref[...].astype(BF16))[0:NDEV, :] + b_ref[...]
            own = pltpu.make_async_copy(modp.at[pl.ds(me, 1), :], modb.at[ci], local_sems.at[1])
            own.start()
            msend = []
            for k in range(1, NCHIP):
                cp = _remote(modp.at[pl.ds(2 * (ci ^ k) + c, 1), :], modb.at[ci], ms.at[k - 1], mr.at[k - 1], 2 * k)
                cp.start()
                msend.append(cp)
            for k in range(1, NCHIP):
                _remote(modp.at[pl.ds(me, 1), :], modb.at[ci ^ k], ms.at[k - 1], mr.at[k - 1], 2 * k).wait_recv()
            own.wait()
            for j in range(NCHIP):
                mod_ref[:, j * EC:(j + 1) * EC] = modb[j]
            for cp in csend + msend:
                cp.wait_send()
            g0_ref[...] = g0s[...]

        @pl.when(jnp.logical_and(s == 1, t == 0))
        def _():
            wg.neighbours_landed(ys, yr)
            wg.sibling_landed(1)

        @pl.when(jnp.logical_and(s == 2, t == 0))
        def _():
            wg.sibling_landed(2)

        @pl.when(jnp.logical_and(s == 3, t == 0))
        def _():
            wg.relay_sems = (ys, yr)
            wg.diagonal_landed()

        rows = pl.ds(pl.multiple_of(t * ts, ts), ts)

        @pl.when(s == 0)
        def _():
            hp, _, _ = _rms_fwd(x_ref[...], np_ref[...])
            h = hp * (1.0 + mod_ref[:, D:2 * D]) + mod_ref[:, 0:D]
            hb_all[rows, :] = h.astype(BF16)
            ht_ref[...] = h.T.astype(BF16)

        proj_ref[...] = _dot(hb_all[rows, :], wbuf[ci ^ s])

        @pl.when(jnp.logical_and(s == NCHIP - 1, t == nt - 1))
        def _():
            wg.relay_sems = (ys, yr)
            wg.finish_relayed()
            wbf_ref[...] = wbuf[...]

    vm = pl.BlockSpec(memory_space=pltpu.VMEM)
    first_pass = lambda s, t: jnp.where(s == 0, t, nt - 1)
    grid_spec = pltpu.PrefetchScalarGridSpec(
        num_scalar_prefetch=1, grid=(NCHIP, nt),
        in_specs=[vm, vm, vm, vm, vm, vm, pl.BlockSpec((ts, D), lambda s, t, o: (first_pass(s, t), 0)),
                  pl.BlockSpec((1, D), lambda s, t, o: (0, 0))],
        out_specs=[vm, vm, vm, vm, vm, pl.BlockSpec((ts, EC), lambda s, t, o: (t, o[s])),
                   pl.BlockSpec((D, ts), lambda s, t, o: (0, first_pass(s, t)))],
        scratch_shapes=[pltpu.VMEM((NDEV, wc), F32), pltpu.VMEM((NDEV, EC), F32), pltpu.VMEM((NCHIP, 1, EC), F32),
                        pltpu.VMEM((NCHIP, D, EC), BF16), pltpu.VMEM((S, D), BF16),
                        pltpu.SemaphoreType.DMA((NDEV - 1,)), pltpu.SemaphoreType.DMA((NDEV - 1,)),
                        pltpu.SemaphoreType.DMA((NCHIP - 1,)), pltpu.SemaphoreType.DMA((NCHIP - 1,))]
        + _WeightGather.SEMS + [pltpu.SemaphoreType.DMA((2,))] * 3)
    return pl.pallas_call(
        body, name="start_in_proj", grid_spec=grid_spec,
        out_shape=[jax.ShapeDtypeStruct((NDEV, wc), F32), jax.ShapeDtypeStruct((1, 3 * D), F32),
                   jax.ShapeDtypeStruct((NCHIP, D, EC), BF16), jax.ShapeDtypeStruct((S, LANES), F32),
                   jax.ShapeDtypeStruct((S, LANES), F32), jax.ShapeDtypeStruct((S, E), F32),
                   jax.ShapeDtypeStruct((D, S), BF16)],
        compiler_params=_cp(("arbitrary", "arbitrary")),
    )(order, crow, w_ada, b_cols, w_in, pos, _rope_freq(), x, norm_pre)


def _start_gather(crow, w_ada, b_cols, w_in):
    wc = crow.shape[1]

    def body(crow_ref, wada_ref, b_ref, win_ref, g0_ref, mod_ref, wbf_ref,
             modp, modb, cs, cr, ms, mr, ws, wr, fs, fr, local_sems):
        x, y, c = _me()
        ci = 2 * x + y
        me = 2 * ci + c
        wg = _WeightGather(win_ref, wbf_ref, ws, wr, fs, fr)
        mine = pltpu.make_async_copy(crow_ref, g0_ref.at[pl.ds(me, 1), :], local_sems.at[0])
        mine.start()
        csend = [_remote(crow_ref, g0_ref.at[pl.ds(me, 1), :], cs.at[r - 1], cr.at[r - 1], r) for r in range(1, NDEV)]
        for cp in csend:
            cp.start()
        wg.start()
        for r in range(1, NDEV):
            px, py, pc = _peer(r)
            _remote(crow_ref, g0_ref.at[pl.ds(4 * px + 2 * py + pc, 1), :], cs.at[r - 1], cr.at[r - 1], r).wait_recv()
        mine.wait()
        cv = g0_ref[:, 0:D]
        sc = cv * _sigmoid(cv)
        scb = jnp.concatenate([sc, jnp.zeros_like(sc)], axis=0).astype(BF16)
        modp[...] = _dot(scb, wada_ref[...].astype(BF16))[0:NDEV, :] + b_ref[...]
        own = pltpu.make_async_copy(modp.at[pl.ds(me, 1), :], modb.at[ci], local_sems.at[1])
        own.start()
        msend = []
        for k in range(1, NCHIP):
            dst = 2 * (ci ^ k) + c
            cp = _remote(modp.at[pl.ds(dst, 1), :], modb.at[ci], ms.at[k - 1], mr.at[k - 1], 2 * k)
            cp.start()
            msend.append(cp)
        for k in range(1, NCHIP):
            _remote(modp.at[pl.ds(me, 1), :], modb.at[ci ^ k], ms.at[k - 1], mr.at[k - 1], 2 * k).wait_recv()
        own.wait()
        for j in range(NCHIP):
            mod_ref[:, j * EC:(j + 1) * EC] = modb[j]
        wg.forward()
        wg.finish()
        for cp in csend + msend:
            cp.wait_send()

    vm = pl.BlockSpec(memory_space=pltpu.VMEM)
    return pl.pallas_call(
        body, name="start_gather",
        in_specs=[vm] * 4, out_specs=[vm] * 3,
        out_shape=[jax.ShapeDtypeStruct((NDEV, wc), F32), jax.ShapeDtypeStruct((1, 3 * D), F32),
                   jax.ShapeDtypeStruct((NCHIP, D, EC), BF16)],
        scratch_shapes=[pltpu.VMEM((NDEV, EC), F32), pltpu.VMEM((NCHIP, 1, EC), F32),
                        pltpu.SemaphoreType.DMA((NDEV - 1,)), pltpu.SemaphoreType.DMA((NDEV - 1,)),
                        pltpu.SemaphoreType.DMA((NCHIP - 1,)), pltpu.SemaphoreType.DMA((NCHIP - 1,))]
        + _WeightGather.SEMS + [pltpu.SemaphoreType.DMA((2,))],
        compiler_params=pltpu.CompilerParams(vmem_limit_bytes=VMEM_LIMIT),
    )(crow, w_ada, b_cols, w_in)


class _ReduceScatter:
    @staticmethod
    def scratch(n_units, rows, ucols, max_owned):
        half = rows // 2
        return [pltpu.VMEM((n_units, half, ucols), F32), pltpu.VMEM((n_units, half, ucols), BF16),
                pltpu.VMEM((max_owned, NCHIP, half, ucols), BF16),
                pltpu.SemaphoreType.DMA((2,)), pltpu.SemaphoreType.DMA((n_units,)),
                pltpu.SemaphoreType.DMA((n_units, NCHIP)), pltpu.SemaphoreType.DMA((n_units,)),
                pltpu.SemaphoreType.DMA((n_units,))]

    def __init__(self, g_ref, out_ref, units, sib, stage, got, sem1, send2, recv2, send3, recv3):
        x, y, c = _me()
        self.c, self.ci = c, 2 * x + y
        self.g, self.out, self.units = g_ref, out_ref, units
        self.sib, self.stage, self.got = sib, stage, got
        self.sem1, self.send2, self.recv2, self.send3, self.recv3 = sem1, send2, recv2, send3, recv3
        self.half = g_ref.shape[1] // 2
        self.ucols = g_ref.shape[2]
        self.r0 = pl.multiple_of(c * self.half, self.half)
        self.r1 = pl.multiple_of((1 - c) * self.half, self.half)
        self.slot0 = units[0][0]
        assert [u[0] for u in units] == list(range(self.slot0, self.slot0 + len(units)))
        seen = {}
        self.local = []
        for _, owner, _ in units:
            self.local.append(seen.get(owner, 0))
            seen[owner] = seen.get(owner, 0) + 1

    def _halves(self):
        n = len(self.units)
        return _remote(self.g.at[pl.ds(self.slot0, n), pl.ds(self.r1, self.half), :], self.sib,
                       self.sem1.at[0], self.sem1.at[1], 1)

    def _partial(self, i, sender):
        _, owner, _ = self.units[i]
        return pltpu.make_async_remote_copy(
            src_ref=self.stage.at[i], dst_ref=self.got.at[self.local[i], sender],
            send_sem=self.send2.at[i], recv_sem=self.recv2.at[i, sender],
            device_id=(owner // 2, owner % 2, self.c), device_id_type=MESH)

    def _back(self, i, start):
        off = self.units[i][2]
        blk = self.out.at[pl.ds(start, self.half), off:off + self.ucols]
        return _remote(blk, blk, self.send3.at[i], self.recv3.at[i], 1)

    def at_steps(self, step, start, send, reduce, finish, out_ref):
        @pl.when(step == start)
        def _():
            self.out[...] = jnp.zeros_like(self.out)
            self.start_halves()

        pl.when(step == send)(self.send_partials)
        pl.when(step == reduce)(self.reduce_owned)

        @pl.when(step == finish)
        def _():
            self.finish()
            out_ref[...] = self.out[...]

    def start_halves(self):
        self._halves().start()

    def send_partials(self):
        self._halves().wait_recv()
        for i, (slot, owner, _) in enumerate(self.units):
            @pl.when(self.ci != owner)
            def _():
                self.stage[i] = (self.g[slot, pl.ds(self.r0, self.half), :] + self.sib[i]).astype(BF16)
                self._partial(i, self.ci).start()

    def reduce_owned(self):
        for i, (slot, owner, off) in enumerate(self.units):
            @pl.when(self.ci == owner)
            def _():
                rows, cols = pl.ds(self.r0, self.half), slice(off, off + self.ucols)
                self.out[rows, cols] = self.g[slot, pl.ds(self.r0, self.half), :] + self.sib[i]
                for s in range(NCHIP):
                    if s != owner:
                        self._partial(i, s).wait_recv()
                        self.out[rows, cols] += self.got[self.local[i], s].astype(F32)
                self._back(i, self.r0).start()

    def finish(self):
        self._halves().wait_send()
        for i, (_, owner, _) in enumerate(self.units):
            @pl.when(self.ci == owner)
            def _():
                self._back(i, self.r1).wait_recv()
                self._back(i, self.r0).wait_send()

            @pl.when(self.ci != owner)
            def _():
                self._partial(i, self.ci).wait_send()


def _reduce_scatter(g4, name):
    _, rows, cols = g4.shape
    units = [(j, j, 0) for j in range(NCHIP)]

    def body(g_ref, out_ref, *scratch):
        rs = _ReduceScatter(g_ref, out_ref, units, *scratch)
        rs.start_halves()
        rs.send_partials()
        rs.reduce_owned()
        rs.finish()

    return pl.pallas_call(
        body, name=name,
        in_specs=[pl.BlockSpec(memory_space=pltpu.VMEM)],
        out_specs=pl.BlockSpec(memory_space=pltpu.VMEM),
        out_shape=jax.ShapeDtypeStruct((rows, cols), F32),
        scratch_shapes=_ReduceScatter.scratch(NCHIP, rows, cols, 1),
        compiler_params=pltpu.CompilerParams(vmem_limit_bytes=VMEM_LIMIT),
    )(g4)


def _silu_rows(c_ref):
    cv = c_ref[...]
    sc = cv * _sigmoid(cv)
    return jnp.concatenate([sc, jnp.zeros_like(sc)], axis=0).astype(BF16)


def _ada_fwd(cg, w_ada, b_cols):
    def body(c_ref, w_ref, b_ref, o_ref):
        o_ref[...] = _dot(_silu_rows(c_ref), w_ref[...].astype(BF16))[0:NDEV, :] + b_ref[...]

    return pl.pallas_call(body, name="ada_fwd", out_shape=jax.ShapeDtypeStruct((NDEV, EC), F32),
                          compiler_params=_cp())(cg, w_ada, b_cols)


def _ada_bwd(cg, dmod_cols):
    def body(c_ref, d_ref, o_ref):
        dm = d_ref[...]
        dmb = jnp.concatenate([dm, jnp.zeros_like(dm)], axis=0).astype(BF16)
        o_ref[...] = _dot_tn(_silu_rows(c_ref), dmb)

    return pl.pallas_call(body, name="ada_bwd", out_shape=jax.ShapeDtypeStruct((D, EC), F32),
                          compiler_params=_cp())(cg, dmod_cols)


def _sum_rows(g):
    def body(g_ref, o_ref):
        acc = g_ref[0:1, :]
        for r in range(1, NDEV):
            acc = acc + g_ref[r:r + 1, :]
        o_ref[...] = acc

    return pl.pallas_call(body, name="sum_rows", out_shape=jax.ShapeDtypeStruct((1, g.shape[1]), F32),
                          compiler_params=_cp())(g)


def _adamw(w, g, m, v, name):
    rows, cols = w.shape
    tr = 256 if rows % 256 == 0 else rows

    def body(w_ref, g_ref, m_ref, v_ref, d_ref, nm_ref, nv_ref):
        gv = g_ref[...]
        nm = B1 * m_ref[...] + (1.0 - B1) * gv
        nv = B2 * v_ref[...] + (1.0 - B2) * (gv * gv)
        m_hat = nm / (1.0 - B1 ** STEP)
        v_hat = nv / (1.0 - B2 ** STEP)
        d_ref[...] = (-LR) * (m_hat / (jnp.sqrt(v_hat) + ADAM_EPS) + WD * w_ref[...])
        nm_ref[...] = nm
        nv_ref[...] = nv

    spec = pl.BlockSpec((tr, cols), lambda i: (i, 0))
    return pl.pallas_call(
        body, name=name, grid=(rows // tr,), in_specs=[spec] * 4, out_specs=[spec] * 3,
        out_shape=[jax.ShapeDtypeStruct((rows, cols), F32)] * 3,
        compiler_params=_cp(("parallel",)),
    )(w, g, m, v)


def _adamw_values(w, g, m, v):
    nm = B1 * m + (1.0 - B1) * g
    nv = B2 * v + (1.0 - B2) * (g * g)
    m_hat = nm / (1.0 - B1 ** STEP)
    v_hat = nv / (1.0 - B2 ** STEP)
    return (-LR) * (m_hat / (jnp.sqrt(v_hat) + ADAM_EPS) + WD * w), nm, nv


NB = R // HEAD
SMALL = (("b_ada", (1, 3 * D)), ("norm_pre", (1, D)), ("norm_post", (1, D)), ("conv_w", (4, R // NCHIP)),
         ("conv_b", (1, R)), ("w_rg_a", (NB, HEAD, HEAD)), ("b_rg_a", (1, R)), ("w_rg_x", (NB, HEAD, HEAD)),
         ("b_rg_x", (1, R)), ("lru_lambda", (1, R)), ("norm_rec", (1, R)), ("norm_att", (1, R)))


def _small_update(ao8, sm8, dwa8, dwx8, ai8, cg, params):
    n = len(SMALL)

    def body(ao_ref, sm_ref, dwa_ref, dwx_ref, ai_ref, cg_ref, *refs):
        pin, pout, (gada_ref, loss_ref, dmod) = refs[:3 * n], refs[3 * n:7 * n], refs[7 * n:]
        xx, yy, _ = _me()
        ci = 2 * xx + yy

        def total(ref, *idx):
            acc = ref[(0,) + idx].astype(F32)
            for d in range(1, NDEV):
                acc = acc + ref[(d,) + idx].astype(F32)
            return acc

        row = lambda ref, r, lanes=slice(None): total(ref, slice(r, r + 1), lanes)
        mine = lambda parts: sum(jnp.where(ci == j, part, 0.0) for j, part in enumerate(parts))
        cw = R // NCHIP
        grads = {
            "b_ada": [jnp.concatenate([row(ai_ref, 0), row(ai_ref, 1), row(ao_ref, 0)], axis=1)],
            "norm_pre": [row(ai_ref, 2)], "norm_post": [row(ao_ref, 1)],
            "conv_w": [mine([row(sm_ref, 8 + r, slice(j * cw, (j + 1) * cw)) for j in range(NCHIP)]) for r in range(4)],
            "conv_b": [row(sm_ref, 4)], "b_rg_a": [row(sm_ref, 0)], "b_rg_x": [row(sm_ref, 1)],
            "lru_lambda": [row(sm_ref, 2)], "norm_rec": [row(sm_ref, 3)], "norm_att": [row(ao_ref, 2, slice(0, R))],
            "w_rg_a": [total(dwa_ref, h) for h in range(NB)], "w_rg_x": [total(dwx_ref, h) for h in range(NB)],
        }
        loss_ref[...] = row(ao_ref, 3, slice(0, LANES)) * (0.5 / D)
        for k, (name, shape) in enumerate(SMALL):
            w_ref, m_ref, v_ref = pin[3 * k:3 * k + 3]
            outs = pout[4 * k:4 * k + 4]
            for r, g in enumerate(grads[name]):
                at = (slice(None),) if len(grads[name]) == 1 else ((r,) if len(shape) == 3 else (slice(r, r + 1),))
                res = (g,) + _adamw_values(w_ref[at], g, m_ref[at], v_ref[at])
                for o_ref, val in zip(outs, res):
                    o_ref[at] = val
        for d in range(NDEV):
            dmod[d:d + 1, :] = jnp.concatenate([ai_ref[d, 0:1, :], ai_ref[d, 1:2, :], ao_ref[d, 0:1, :]], axis=1)
        cols = mine([dmod[:, j * EC:(j + 1) * EC] for j in range(NCHIP)])
        colsb = jnp.concatenate([cols, jnp.zeros_like(cols)], axis=0).astype(BF16)
        gada_ref[...] = _dot_tn(_silu_rows(cg_ref), colsb)

    shapes = [jax.ShapeDtypeStruct(s, F32) for _, s in SMALL]
    outs = pl.pallas_call(
        body, name="small_update",
        out_shape=[s for s in shapes for _ in range(4)] + [jax.ShapeDtypeStruct((D, EC), F32),
                                                           jax.ShapeDtypeStruct((1, LANES), F32)],
        scratch_shapes=[pltpu.VMEM((NDEV, 3 * D), F32)],
        compiler_params=_cp(),
    )(ao8, sm8, dwa8, dwx8, ai8, cg, *params)
    return outs[:4 * n], outs[4 * n], outs[4 * n + 1]


BIG = ("w_ada", "w_in", "w_out")
WEIGHTS = ("w_ada", "b_ada", "norm_pre", "norm_post", "w_in", "conv_w", "conv_b", "w_rg_a", "b_rg_a", "w_rg_x",
           "b_rg_x", "lru_lambda", "norm_rec", "norm_att", "w_out")


def kernel(x, c, positions, w_ada, b_ada, norm_pre, norm_post, w_in, conv_w, conv_b, w_rg_a, b_rg_a, w_rg_x, b_rg_x, lru_lambda, norm_rec, norm_att, w_out, loss_target, m_w_ada, m_b_ada, m_norm_pre, m_norm_post, m_w_in, m_conv_w, m_conv_b, m_w_rg_a, m_b_rg_a, m_w_rg_x, m_b_rg_x, m_lru_lambda, m_norm_rec, m_norm_att, m_w_out, v_w_ada, v_b_ada, v_norm_pre, v_norm_post, v_w_in, v_conv_w, v_conv_b, v_w_rg_a, v_b_rg_a, v_w_rg_x, v_b_rg_x, v_lru_lambda, v_norm_rec, v_norm_att, v_w_out):
    given = dict(locals())
    wts = {n: given[n] for n in WEIGHTS}
    ms = {n: given["m_" + n] for n in WEIGHTS}
    vs = {n: given["v_" + n] for n in WEIGHTS}
    xi, yi, _ = _me()
    chip = 2 * xi + yi
    cw_loc = R // NCHIP

    b_cols = lax.dynamic_slice(b_ada, (0, chip * EC), (1, EC))
    order = (chip ^ jnp.arange(NCHIP, dtype=jnp.int32)).astype(jnp.int32)
    g0, mod, w_in_bf, cos, sin, proj, ht = _start_in_proj(
        jnp.concatenate([c, conv_w.reshape(1, 4 * cw_loc)], axis=1), w_ada[0], b_cols, w_in[0],
        positions.reshape(S, 1), x[0], norm_pre, order)
    cg = g0[:, 0:D]
    conv_full = g0[0::2, D:].reshape(NCHIP, 4, cw_loc).transpose(1, 0, 2).reshape(4, R)

    p = dict(norm_pre=norm_pre, norm_post=norm_post, conv_b=conv_b, b_rg_a=b_rg_a, b_rg_x=b_rg_x,
             lru_lambda=lru_lambda, norm_rec=norm_rec, norm_att=norm_att, w_rg_a=w_rg_a[0], w_rg_x=w_rg_x[0])
    grad_x, g_in, g_out, gathered = _local_step(
        x[0], cos, sin, loss_target[0], mod, w_in_bf, proj, ht, w_out[0], conv_full, p)

    params = [d[n].reshape(shape) for n, shape in SMALL for d in (wts, ms, vs)]
    small_out, g_ada, loss_row = _small_update(*gathered, cg, params)
    grads = {"w_out": g_out, "w_in": g_in, "w_ada": g_ada}
    delta, new_m, new_v = {}, {}, {}
    for k, (n, _) in enumerate(SMALL):
        grads[n], delta[n], new_m[n], new_v[n] = small_out[4 * k:4 * k + 4]
    for n in BIG:
        delta[n], new_m[n], new_v[n] = _adamw(wts[n][0], grads[n], ms[n][0], vs[n][0], "adamw_" + n)
    out = lambda d: [d[n].reshape(wts[n].shape) for n in WEIGHTS]
    return (loss_row[0, 0], grad_x.reshape(x.shape), *out(grads), *out(delta), *out(new_m), *out(new_v))
```

```python
import functools

import numpy as np
import jax
import jax.numpy as jnp
from jax import lax
from jax.experimental import pallas as pl
from jax.experimental.pallas import tpu as pltpu

F32 = jnp.float32
BF16 = jnp.bfloat16

S = 2048
D = 1024
E = 3072
R = 512
NDEV = 8
NCHIP = 4
EC = 768
LRU_C = 8.0
EPS = 1e-6
NEG = -1e30
HEAD = 64
BLK = 128
PATTERNS = (1, 4, 16)
ROPE_THETA = 10000.0
LANES = 128
VMEM_LIMIT = 56 * 1024 * 1024

B1, B2, LR, WD, ADAM_EPS, STEP = 0.9, 0.999, 0.001, 0.01, 1e-8, 10
MESH = pl.DeviceIdType.MESH


def _cp(sem=None, **kw):
    return pltpu.CompilerParams(dimension_semantics=sem, vmem_limit_bytes=VMEM_LIMIT, **kw)


def _dot(a, b):
    return jnp.dot(a, b, preferred_element_type=F32)


def _dot_nt(a, b):
    return lax.dot_general(a, b, (((1,), (1,)), ((), ())), preferred_element_type=F32)


def _dot_tn(a, b):
    return lax.dot_general(a, b, (((0,), (0,)), ((), ())), preferred_element_type=F32)


def _sigmoid(x):
    return 1.0 / (1.0 + jnp.exp(-x))


def _expm1(x):
    poly = x * (1.0 + x * (0.5 + x * (1.0 / 6 + x * (1.0 / 24 + x * (1.0 / 120 + x * (1.0 / 720))))))
    return jnp.where(jnp.abs(x) < 0.3, poly, jnp.exp(x) - 1.0)


def _rms_fwd(v, g):
    rstd = lax.rsqrt(jnp.mean(v * v, axis=-1, keepdims=True) + EPS)
    vn = v * rstd
    return vn * g, vn, rstd


def _rms_bwd(dy, vn, rstd, g):
    dvn = dy * g
    dv = rstd * (dvn - vn * jnp.mean(dvn * vn, axis=-1, keepdims=True))
    return dv, jnp.sum(dy * vn, axis=0, keepdims=True)


def _in_proj_fwd(x, mod, norm_pre, w_in_bf):
    ts = 256

    def body(x_ref, mod_ref, np_ref, w_ref, proj_ref, ht_ref):
        hp, _, _ = _rms_fwd(x_ref[...], np_ref[...])
        h = hp * (1.0 + mod_ref[:, D:2 * D]) + mod_ref[:, 0:D]
        hb = h.astype(BF16)
        ht_ref[...] = h.T.astype(BF16)
        for j in range(NCHIP):
            proj_ref[:, j * EC:(j + 1) * EC] = _dot(hb, w_ref[j])

    return pl.pallas_call(
        body, name="in_proj_fwd", grid=(S // ts,),
        in_specs=[pl.BlockSpec((ts, D), lambda i: (i, 0)), pl.BlockSpec((1, 3 * D), lambda i: (0, 0)),
                  pl.BlockSpec((1, D), lambda i: (0, 0)), pl.BlockSpec((NCHIP, D, EC), lambda i: (0, 0, 0))],
        out_specs=[pl.BlockSpec((ts, E), lambda i: (i, 0)), pl.BlockSpec((D, ts), lambda i: (0, i))],
        out_shape=[jax.ShapeDtypeStruct((S, E), F32), jax.ShapeDtypeStruct((D, S), BF16)],
        compiler_params=_cp(("parallel",)),
    )(x, mod, norm_pre, w_in_bf)


RT = 256


def _shift_down(cur, prev8, j, row):
    if j == 0:
        return cur
    top = jnp.tile(pltpu.roll(prev8, j, 0), (RT // 8, 1))
    return jnp.where(row >= j, pltpu.roll(cur, j, 0), top)


def _shift_up(cur, next8, j, row):
    if j == 0:
        return cur
    bot = jnp.tile(pltpu.roll(next8, 8 - j, 0), (RT // 8, 1))
    return jnp.where(row < RT - j, pltpu.roll(cur, RT - j, 0), bot)


def _rec_gates(xp, xprev8, row, cw_ref, cb_ref, wa_ref, ba_ref, wx_ref, bx_ref, lam_ref):
    xa = cb_ref[...] + sum(cw_ref[3 - j:4 - j, :] * _shift_down(xp, xprev8, j, row) for j in range(4))
    xab = xa.astype(BF16)
    r = _sigmoid(_dot(xab, wa_ref[...]) + ba_ref[...])
    ig = _sigmoid(_dot(xab, wx_ref[...]) + bx_ref[...])
    nl = -lam_ref[...]
    sp = jnp.maximum(nl, 0.0) + jnp.log1p(jnp.exp(-jnp.abs(nl)))
    la = (-LRU_C) * r * sp
    a = jnp.exp(la)
    mult = jnp.sqrt(-_expm1(2.0 * la))
    return dict(xa=xa, xab=xab, r=r, ig=ig, sp=sp, la=la, a=a, mult=mult)


def _scan_fwd(a, u, row):
    sh = 1
    while sh < RT:
        a_s = jnp.where(row >= sh, pltpu.roll(a, sh, 0), 1.0)
        u_s = jnp.where(row >= sh, pltpu.roll(u, sh, 0), 0.0)
        u = a * u_s + u
        a = a * a_s
        sh *= 2
    return a, u


def _scan_bwd(al, g, row):
    sh = 1
    while sh < RT:
        al_s = jnp.where(row < RT - sh, pltpu.roll(al, RT - sh, 0), 1.0)
        g_s = jnp.where(row < RT - sh, pltpu.roll(g, RT - sh, 0), 0.0)
        g = g + al * g_s
        al = al * al_s
        sh *= 2
    return g


def _dense_from_blocks(blocks_ref, dense_ref):
    dense_ref[...] = jnp.zeros_like(dense_ref)
    for h in range(R // HEAD):
        dense_ref[h * HEAD:(h + 1) * HEAD, h * HEAD:(h + 1) * HEAD] = blocks_ref[h].astype(dense_ref.dtype)


def _rec_fwd(proj, conv_w, conv_b, wa_b, ba, wx_b, bx, lam, norm_rec):
    nt = S // RT

    def body(p_ref, cw_ref, cb_ref, wa_ref, ba_ref, wx_ref, bx_ref, lam_ref, nr_ref,
             h_ref, ya_ref, prev8, hc, wad, wxd):
        i = pl.program_id(0)

        @pl.when(i == 0)
        def _():
            prev8[...] = jnp.zeros_like(prev8)
            hc[...] = jnp.zeros_like(hc)
            _dense_from_blocks(wa_ref, wad)
            _dense_from_blocks(wx_ref, wxd)

        row = lax.broadcasted_iota(jnp.int32, (RT, R), 0)
        xp = p_ref[:, 0:R]
        ga = p_ref[:, R:2 * R]
        f = _rec_gates(xp, prev8[...], row, cw_ref, cb_ref, wad, ba_ref, wxd, bx_ref, lam_ref)
        u = f["mult"] * (f["ig"] * f["xa"])
        acum, hh = _scan_fwd(f["a"], u, row)
        h = hh + acum * hc[0:1, :]
        h_ref[...] = h
        hc[0:1, :] = h_ref[RT - 1:RT, :]
        prev8[...] = p_ref[RT - 8:RT, 0:R]
        yp = h * (ga * _sigmoid(ga))
        ya, _, _ = _rms_fwd(yp, nr_ref[...])
        ya_ref[...] = ya.astype(BF16)

    row1 = lambda n: pl.BlockSpec((1, n), lambda i: (0, 0))
    blocks = pl.BlockSpec((R // HEAD, HEAD, HEAD), lambda i: (0, 0, 0))
    return pl.pallas_call(
        body, name="rec_fwd", grid=(nt,),
        in_specs=[pl.BlockSpec((RT, 2 * R), lambda i: (i, 0)), pl.BlockSpec((4, R), lambda i: (0, 0)), row1(R),
                  blocks, row1(R), blocks, row1(R), row1(R), row1(R)],
        out_specs=[pl.BlockSpec((RT, R), lambda i: (i, 0)), pl.BlockSpec((RT, R), lambda i: (i, 0))],
        out_shape=[jax.ShapeDtypeStruct((S, R), F32), jax.ShapeDtypeStruct((S, R), BF16)],
        scratch_shapes=[pltpu.VMEM((8, R), F32), pltpu.VMEM((8, R), F32), pltpu.VMEM((R, R), BF16),
                        pltpu.VMEM((R, R), BF16)],
        compiler_params=_cp(("arbitrary",)),
    )(proj, conv_w, conv_b, wa_b, ba, wx_b, bx, lam, norm_rec)


def _rec_bwd(dproj, d_ya, proj, h_all, conv_w, conv_b, wa_b, ba, wx_b, bx, lam, norm_rec):
    nt = S // RT

    def body(dp_in, dya_ref, p_ref, pprev_ref, h_ref, hprev_ref, cw_ref, cb_ref, wab_ref, ba_ref, wxb_ref, bx_ref,
             lam_ref, nr_ref, dp_ref, dwab_ref, dwxb_ref, sm_ref, nxt8, cg, wa_ref, wx_ref, dwa_ref, dwx_ref):
        i = pl.program_id(0)
        ti = nt - 1 - i

        @pl.when(i == 0)
        def _():
            nxt8[...] = jnp.zeros_like(nxt8)
            cg[...] = jnp.zeros_like(cg)
            dwa_ref[...] = jnp.zeros_like(dwa_ref)
            dwx_ref[...] = jnp.zeros_like(dwx_ref)
            sm_ref[...] = jnp.zeros_like(sm_ref)
            _dense_from_blocks(wab_ref, wa_ref)
            _dense_from_blocks(wxb_ref, wx_ref)

        row = lax.broadcasted_iota(jnp.int32, (RT, R), 0)
        first = (ti > 0).astype(F32)
        xprev8 = pprev_ref[...] * first
        hprev8 = hprev_ref[...] * first
        xp = p_ref[:, 0:R]
        ga = p_ref[:, R:2 * R]
        f = _rec_gates(xp, xprev8, row, cw_ref, cb_ref, wa_ref, ba_ref, wx_ref, bx_ref, lam_ref)
        xa, r, ig, a, mult = f["xa"], f["r"], f["ig"], f["a"], f["mult"]
        h = h_ref[...]
        sg = _sigmoid(ga)
        gate = ga * sg
        yp = h * gate
        _, ypn, rstd = _rms_fwd(yp, nr_ref[...])
        d_yp, dnr = _rms_bwd(dya_ref[...], ypn, rstd, nr_ref[...])
        d_ga = d_yp * h * (sg * (1.0 + ga * (1.0 - sg)))
        dh = d_yp * gate + jnp.where(row == RT - 1, cg[0:1, :], 0.0)
        al = jnp.where(row < RT - 1, pltpu.roll(a, RT - 1, 0), 0.0)
        g = _scan_bwd(al, dh, row)
        cg[0:1, :] = jnp.sum(jnp.where(row == 0, a * g, 0.0), axis=0, keepdims=True)
        h_m1 = _shift_down(h, hprev8, 1, row)
        da = g * h_m1
        ix = ig * xa
        d_mult = g * ix
        d_ig = g * mult * xa
        d_xa = g * mult * ig
        d_la = da * a - d_mult * (a * a) / mult
        d_r = d_la * ((-LRU_C) * f["sp"])
        dsp = jnp.sum(d_la * ((-LRU_C) * r), axis=0, keepdims=True)
        dlam = dsp * (-_sigmoid(-lam_ref[...]))
        d_za = d_r * r * (1.0 - r)
        d_zx = d_ig * ig * (1.0 - ig)
        dzab = d_za.astype(BF16)
        dzxb = d_zx.astype(BF16)
        dwa_ref[...] += _dot_tn(f["xab"], dzab)
        dwx_ref[...] += _dot_tn(f["xab"], dzxb)
        d_xa = d_xa + _dot_nt(dzab, wa_ref[...]) + _dot_nt(dzxb, wx_ref[...])
        d_xp = sum(cw_ref[3 - j:4 - j, :] * _shift_up(d_xa, nxt8[...], j, row) for j in range(4))
        dcw = [jnp.sum(d_xa * _shift_down(xp, xprev8, 3 - k, row), axis=0, keepdims=True) for k in range(4)]
        dp_ref[:, 0:R] = d_xp.astype(BF16)
        dp_ref[:, R:2 * R] = d_ga.astype(BF16)
        dp8 = d_xa[0:8, :]
        nxt8[...] = dp8
        sm_ref[0:1, :] += jnp.sum(d_za, axis=0, keepdims=True)
        sm_ref[1:2, :] += jnp.sum(d_zx, axis=0, keepdims=True)
        sm_ref[2:3, :] += dlam
        sm_ref[3:4, :] += dnr
        sm_ref[4:5, :] += jnp.sum(d_xa, axis=0, keepdims=True)
        for k in range(4):
            sm_ref[8 + k:9 + k, :] += dcw[k]

        @pl.when(i == nt - 1)
        def _():
            for h in range(R // HEAD):
                dwab_ref[h] = dwa_ref[h * HEAD:(h + 1) * HEAD, h * HEAD:(h + 1) * HEAD].astype(BF16)
                dwxb_ref[h] = dwx_ref[h * HEAD:(h + 1) * HEAD, h * HEAD:(h + 1) * HEAD].astype(BF16)

    c0 = lambda shape: pl.BlockSpec(shape, lambda i: (0, 0))
    blocks = pl.BlockSpec((R // HEAD, HEAD, HEAD), lambda i: (0, 0, 0))
    rev = lambda i: nt - 1 - i
    prev8 = lambda i: (jnp.maximum((nt - 1 - i) * (RT // 8) - 1, 0), 0)
    return pl.pallas_call(
        body, name="rec_bwd", grid=(nt,),
        in_specs=[pl.BlockSpec(memory_space=pl.ANY),
                  pl.BlockSpec((RT, R), lambda i: (rev(i), 0)),
                  pl.BlockSpec((RT, 2 * R), lambda i: (rev(i), 0)), pl.BlockSpec((8, R), prev8),
                  pl.BlockSpec((RT, R), lambda i: (rev(i), 0)), pl.BlockSpec((8, R), prev8),
                  c0((4, R)), c0((1, R)), blocks, c0((1, R)), blocks, c0((1, R)), c0((1, R)), c0((1, R))],
        out_specs=[pl.BlockSpec((RT, 2 * R), lambda i: (rev(i), 0)), blocks, blocks, c0((16, R))],
        out_shape=[jax.ShapeDtypeStruct((S, E), BF16), jax.ShapeDtypeStruct((R // HEAD, HEAD, HEAD), BF16),
                   jax.ShapeDtypeStruct((R // HEAD, HEAD, HEAD), BF16), jax.ShapeDtypeStruct((16, R), F32)],
        scratch_shapes=[pltpu.VMEM((8, R), F32), pltpu.VMEM((8, R), F32), pltpu.VMEM((R, R), BF16),
                        pltpu.VMEM((R, R), BF16), pltpu.VMEM((R, R), F32), pltpu.VMEM((R, R), F32)],
        input_output_aliases={0: 0},
        compiler_params=_cp(("arbitrary",)),
    )(dproj, d_ya, proj, proj, h_all, h_all, conv_w, conv_b, wa_b, ba, wx_b, bx, lam, norm_rec)


NPAIR = R // LANES
QB, KB, VB, GB = 2 * R // LANES, 3 * R // LANES, 4 * R // LANES, 5 * R // LANES


def _rope_freq():
    half = HEAD // 2
    inv = np.float32(ROPE_THETA) ** (-(np.arange(half, dtype=np.float32) / np.float32(half)))
    return jnp.asarray(np.tile(inv.astype(np.float32), LANES // half)[None, :])


def _rot_half(x, first):
    return jnp.where(first, -pltpu.roll(x, LANES - HEAD // 2, 1), pltpu.roll(x, HEAD // 2, 1))


def _cos_sin(pos_ref, freq_ref):
    ang = pos_ref[...].astype(F32) * freq_ref[...]
    return jnp.cos(ang), jnp.sin(ang)


def _deint(src_ref, dst_ref, d):
    n = S // d
    for r in range(d):
        v = src_ref[pl.ds(r, n, stride=d), :] if d > 1 else src_ref[...]
        dst_ref[r * n:(r + 1) * n, :] = v.astype(dst_ref.dtype)


def _reint(src_ref, dst_ref, d, accumulate):
    n = S // d
    for r in range(d):
        idx = (pl.ds(r, n, stride=d), slice(None)) if d > 1 else (slice(None), slice(None))
        v = src_ref[r * n:(r + 1) * n, :]
        if accumulate:
            dst_ref[idx] = dst_ref[idx] + v
        else:
            dst_ref[idx] = v


def _blk_masks(b, nb):
    qi = lax.broadcasted_iota(jnp.int32, (BLK, BLK), 0)
    ki = lax.broadcasted_iota(jnp.int32, (BLK, BLK), 1)
    has_prev = lax.rem(b, nb) != 0
    return ki <= qi, jnp.logical_and(ki >= qi, has_prev)


def _rope_table(pos, freq):
    def body(pos_ref, freq_ref, cos_ref, sin_ref):
        cos_ref[...], sin_ref[...] = _cos_sin(pos_ref, freq_ref)

    return pl.pallas_call(body, name="rope_table", out_shape=[jax.ShapeDtypeStruct((S, LANES), F32)] * 2,
                          compiler_params=_cp())(pos, freq)


def _deint_heads(src_ref, dst0, dst1, d):
    n = S // d
    hm0 = lax.broadcasted_iota(jnp.int32, (n, LANES), 1) < HEAD
    for r in range(d):
        v = src_ref[pl.ds(r, n, stride=d), :] if d > 1 else src_ref[...]
        dst0[r * n:(r + 1) * n, :] = jnp.where(hm0, v, 0.0).astype(BF16)
        dst1[r * n:(r + 1) * n, :] = jnp.where(hm0, 0.0, v).astype(BF16)


def _reint_prev(src_ref, dst_ref, d):
    n = S // d
    if n == BLK:
        return
    for r in range(d):
        idx = (pl.ds(r, n - BLK, stride=d), slice(None)) if d > 1 else (slice(0, n - BLK), slice(None))
        dst_ref[idx] = dst_ref[idx] + src_ref[r * n + BLK:(r + 1) * n, :]


def _pair_masks():
    qi = lax.broadcasted_iota(jnp.int32, (BLK, 2 * BLK), 0)
    ki = lax.broadcasted_iota(jnp.int32, (BLK, 2 * BLK), 1) & (BLK - 1)
    return ki <= qi, ki >= qi


def _two(ref0, ref1, st, axis):
    return jnp.concatenate([ref0[pl.ds(st, BLK), :], ref1[pl.ds(st, BLK), :]], axis=axis)


ATT_UNROLL = 4


def _att_fwd(proj, cos, sin, w_out):
    def body(q_ref, k_ref, v_ref, cos_ref, sin_ref, w_ref, att_ref, qr_ref, kr_ref, lse_ref, wbf_ref,
             qd, kd0, kd1, vd0, vd1, od, ld, on, ln, wbuf, *wsems):
        wg = _WeightGather(w_ref, wbuf, *wsems)
        pl.when(pl.program_id(0) == 0)(wg.start)
        pl.when(pl.program_id(0) == 1)(wg.forward)
        lane = lax.broadcasted_iota(jnp.int32, (S, LANES), 1)
        first = (lane & (HEAD // 2)) == 0
        cos, sin = cos_ref[...], sin_ref[...]
        q = q_ref[...]
        k = k_ref[...]
        qr_ref[...] = (q * cos + _rot_half(q, first) * sin) * (HEAD ** -0.5)
        kr_ref[...] = k * cos + _rot_half(k, first) * sin
        hm0 = lax.broadcasted_iota(jnp.int32, (BLK, LANES), 1) < HEAD
        top = lax.broadcasted_iota(jnp.int32, (2 * BLK, LANES), 0) < BLK
        ones2 = (top == (lax.broadcasted_iota(jnp.int32, (2 * BLK, LANES), 1) < HEAD)).astype(BF16)
        mc2, mp2 = _pair_masks()

        for pi, d in enumerate(PATTERNS):
            nb = S // d // BLK
            _deint(qr_ref, qd, d)
            _deint_heads(kr_ref, kd0, kd1, d)
            _deint_heads(v_ref, vd0, vd1, d)

            def blk(b, carry):
                st = pl.multiple_of(b * BLK, BLK)
                qb = qd[pl.ds(st, BLK), :]
                sc = jnp.where(mc2, _dot_nt(qb, _two(kd0, kd1, st, 0)), NEG)
                mx = sc
                if nb > 1:
                    stp = pl.multiple_of(jnp.maximum(b - 1, 0) * BLK, BLK)
                    mp = jnp.logical_and(mp2, lax.rem(b, nb) != 0)
                    sp = jnp.where(mp, _dot_nt(qb, _two(kd0, kd1, stp, 0)), NEG)
                    mx = jnp.maximum(sc, sp)
                m0 = jnp.max(mx[:, 0:BLK], axis=1, keepdims=True)
                m1 = jnp.max(mx[:, BLK:2 * BLK], axis=1, keepdims=True)
                mf = jnp.concatenate([jnp.broadcast_to(m0, (BLK, BLK)), jnp.broadcast_to(m1, (BLK, BLK))], axis=1)
                o = _dot(jnp.exp(sc - mf).astype(BF16), jnp.concatenate([_two(vd0, vd1, st, 0), ones2], axis=1))
                if nb > 1:
                    o = o + _dot(jnp.exp(sp - mf).astype(BF16), jnp.concatenate([_two(vd0, vd1, stp, 0), ones2], axis=1))
                l = o[:, LANES:2 * LANES]
                od[pl.ds(st, BLK), :] = o[:, 0:LANES] / l
                ld[pl.ds(st, BLK), :] = jnp.where(hm0, m0, m1) + jnp.log(l)
                return carry

            lax.fori_loop(0, S // BLK, blk, 0, unroll=ATT_UNROLL)
            _reint(od, on.at[pi], d, False)
            _reint(ld, ln.at[pi], d, False)

        l0, l1, l2 = ln[0], ln[1], ln[2]
        m = jnp.maximum(jnp.maximum(l0, l1), l2)
        e0, e1, e2 = jnp.exp(l0 - m), jnp.exp(l1 - m), jnp.exp(l2 - m)
        den = e0 + e1 + e2
        att_ref[...] = (e0 * on[0] + e1 * on[1] + e2 * on[2]) / den
        lse_ref[...] = m + jnp.log(den)

        @pl.when(pl.program_id(0) == NPAIR - 1)
        def _():
            wg.finish()
            wbf_ref[...] = wbuf[...]

    col = lambda c0: pl.BlockSpec((S, LANES), lambda p: (0, c0 + p))
    out = pl.BlockSpec((S, LANES), lambda p: (0, p))
    tab = pl.BlockSpec((S, LANES), lambda p: (0, 0))
    vm = pl.BlockSpec(memory_space=pltpu.VMEM)
    return pl.pallas_call(
        body, name="att_fwd", grid=(NPAIR,),
        in_specs=[col(QB), col(KB), col(VB), tab, tab, vm],
        out_specs=[out, out, out, out, vm],
        out_shape=[jax.ShapeDtypeStruct((S, R), F32)] * 4 + [jax.ShapeDtypeStruct((NCHIP,) + w_out.shape, BF16)],
        scratch_shapes=[pltpu.VMEM((S, LANES), BF16)] * 5 + [pltpu.VMEM((S, LANES), F32)] * 2
        + [pltpu.VMEM((3, S, LANES), F32)] * 2 + [pltpu.VMEM((NCHIP,) + w_out.shape, BF16)] + _WeightGather.SEMS,
        compiler_params=_cp(("arbitrary",)),
    )(proj, proj, proj, cos, sin, w_out)


def _att_bwd(dproj, d_att, att, lse, qr, kr, proj, cos, sin, gw_out4):
    out_units = [(j, j, 0) for j in range(NCHIP)]

    nblk = S // BLK

    def body(dp_in, do_ref, o_ref, lse_ref, qr_ref, kr_ref, v_ref, cos_ref, sin_ref, gw_ref, dp_ref, gout_ref,
             qd, kd0, kd1, vd0, vd1, dod, kt, packn, packd, dqd, dkcd, dkpd, dvcd, dvpd,
             dqn, dkn, dvn, rows, trs, stage, sems, gred, *rs_scratch):
        p = pl.program_id(0)
        rs = _ReduceScatter(gw_ref, gred, out_units, *rs_scratch)
        for step, piece in enumerate((rs.start_halves, rs.send_partials, rs.reduce_owned)):
            pl.when(p == step)(piece)

        @pl.when(p == NPAIR - 1)
        def _():
            rs.finish()
            gout_ref[...] = gred[...]

        lane = lax.broadcasted_iota(jnp.int32, (S, LANES), 1)
        hms = lane < HEAD
        prod = do_ref[...] * o_ref[...]
        d0 = jnp.sum(jnp.where(hms, prod, 0.0), axis=1, keepdims=True)
        d1 = jnp.sum(jnp.where(hms, 0.0, prod), axis=1, keepdims=True)
        lse = lse_ref[...]
        quarter = HEAD // 2
        packn[...] = jnp.where(lane < quarter, lse,
                               jnp.where(hms, pltpu.roll(lse, LANES - quarter, 1), jnp.where(lane < 3 * quarter, d0, d1)))
        dqn[...] = jnp.zeros_like(dqn)
        dkn[...] = jnp.zeros_like(dkn)
        dvn[...] = jnp.zeros_like(dvn)
        hm0 = lax.broadcasted_iota(jnp.int32, (BLK, LANES), 1) < HEAD
        key = lax.broadcasted_iota(jnp.int32, (2 * BLK, BLK), 0) & (BLK - 1)
        qry = lax.broadcasted_iota(jnp.int32, (2 * BLK, BLK), 1)
        mct, mpt = key <= qry, key >= qry

        for d in PATTERNS:
            nb = S // d // BLK
            _deint(qr_ref, qd, d)
            _deint_heads(kr_ref, kd0, kd1, d)
            _deint_heads(v_ref, vd0, vd1, d)
            _deint(do_ref, dod, d)
            _deint(packn, packd, d)

            def blk(b, carry):
                st = pl.multiple_of(b * BLK, BLK)
                kt[b] = _two(kd0, kd1, st, 0).astype(F32).T.astype(BF16)
                trs[b] = packd[pl.ds(st, BLK), :].T
                for j in range(4):
                    rows[b, j:j + 1, :] = trs[b, j * quarter:j * quarter + 1, :]
                qb, dob = qd[pl.ds(st, BLK), :], dod[pl.ds(st, BLK), :]
                both = lambda j: jnp.concatenate([jnp.broadcast_to(rows[b, j:j + 1, :], (BLK, BLK)),
                                                  jnp.broadcast_to(rows[b, j + 1:j + 2, :], (BLK, BLK))], axis=0)
                lbt, dlt = both(0), both(2)

                def side(bk, mask):
                    stk = pl.multiple_of(bk * BLK, BLK)
                    k2, v2 = _two(kd0, kd1, stk, 0), _two(vd0, vd1, stk, 0)
                    pt = jnp.where(mask, jnp.exp(_dot_nt(k2, qb) - lbt), 0.0)
                    dst = (pt * (_dot_nt(v2, dob) - dlt)).astype(BF16)
                    rk, rv = _dot(dst, qb), _dot(pt.astype(BF16), dob)
                    return (_dot(kt[bk], dst), jnp.where(hm0, rk[0:BLK], rk[BLK:2 * BLK]),
                            jnp.where(hm0, rv[0:BLK], rv[BLK:2 * BLK]))

                dq_t, dkc, dvc = side(b, mct)
                if nb > 1:
                    dqp_t, dkp, dvp = side(jnp.maximum(b - 1, 0), jnp.logical_and(mpt, lax.rem(b, nb) != 0))
                    dq_t = dq_t + dqp_t
                    dkpd[pl.ds(st, BLK), :] = dkp
                    dvpd[pl.ds(st, BLK), :] = dvp
                dqd[pl.ds(st, BLK), :] = dq_t.T
                dkcd[pl.ds(st, BLK), :] = dkc
                dvcd[pl.ds(st, BLK), :] = dvc
                return carry

            lax.fori_loop(0, nblk, blk, 0, unroll=2 * ATT_UNROLL)
            _reint(dqd, dqn, d, True)
            _reint(dkcd, dkn, d, True)
            _reint(dvcd, dvn, d, True)
            _reint_prev(dkpd, dkn, d)
            _reint_prev(dvpd, dvn, d)

        lane = lax.broadcasted_iota(jnp.int32, (S, LANES), 1)
        first = (lane & (HEAD // 2)) == 0
        cos, sin = cos_ref[...], sin_ref[...]
        dq = dqn[...] * (HEAD ** -0.5)
        dk = dkn[...]
        stage[0] = (dq * cos - _rot_half(dq, first) * sin).astype(BF16)
        stage[1] = (dk * cos - _rot_half(dk, first) * sin).astype(BF16)
        stage[2] = dvn[...].astype(BF16)
        copies = [pltpu.make_async_copy(stage.at[j], dp_ref.at[:, pl.ds((2 + j) * R + p * LANES, LANES)], sems.at[j])
                  for j in range(3)]
        for cp in copies:
            cp.start()
        for cp in copies:
            cp.wait()

    blk = pl.BlockSpec((S, LANES), lambda p: (0, p))
    tab = pl.BlockSpec((S, LANES), lambda p: (0, 0))
    vm = pl.BlockSpec(memory_space=pltpu.VMEM)
    _, orows, ocols = gw_out4.shape
    return pl.pallas_call(
        body, name="att_bwd", grid=(NPAIR,),
        in_specs=[pl.BlockSpec(memory_space=pl.ANY), blk, blk, blk, blk, blk,
                  pl.BlockSpec((S, LANES), lambda p: (0, VB + p)), tab, tab, vm],
        out_specs=[pl.BlockSpec(memory_space=pl.ANY), vm],
        out_shape=[jax.ShapeDtypeStruct((S, E), BF16), jax.ShapeDtypeStruct((orows, ocols), F32)],
        scratch_shapes=[pltpu.VMEM((S, LANES), BF16)] * 6 + [pltpu.VMEM((nblk, LANES, 2 * BLK), BF16)]
        + [pltpu.VMEM((S, LANES), F32)] * 10
        + [pltpu.VMEM((nblk, 8, BLK), F32), pltpu.VMEM((nblk, LANES, BLK), F32)]
        + [pltpu.VMEM((3, S, LANES), BF16), pltpu.SemaphoreType.DMA((3,)), pltpu.VMEM((orows, ocols), F32)]
        + _ReduceScatter.scratch(NCHIP, orows, ocols, 1),
        input_output_aliases={0: 0},
        compiler_params=_cp(("arbitrary",)),
    )(dproj, d_att, att, lse, qr, kr, proj, cos, sin, gw_out4)


def _att_fwd_old(proj, pos, freq):
    def body(q_ref, k_ref, v_ref, pos_ref, freq_ref, att_ref, qr_ref, kr_ref, lse_ref,
             qd, kd, vd, od, ld, on, ln):
        lane = lax.broadcasted_iota(jnp.int32, (S, LANES), 1)
        first = (lane & (HEAD // 2)) == 0
        cos, sin = _cos_sin(pos_ref, freq_ref)
        q = q_ref[...]
        k = k_ref[...]
        qr_ref[...] = (q * cos + _rot_half(q, first) * sin) * (HEAD ** -0.5)
        kr_ref[...] = k * cos + _rot_half(k, first) * sin
        hm0 = lax.broadcasted_iota(jnp.int32, (BLK, LANES), 1) < HEAD

        for pi, d in enumerate(PATTERNS):
            nb = S // d // BLK
            _deint(qr_ref, qd, d)
            _deint(kr_ref, kd, d)
            _deint(v_ref, vd, d)

            def blk(b, carry):
                st = pl.multiple_of(b * BLK, BLK)
                stp = pl.multiple_of(jnp.maximum(b - 1, 0) * BLK, BLK)
                mc, mp = _blk_masks(b, nb)
                qb = qd[pl.ds(st, BLK), :]
                kc, kp = kd[pl.ds(st, BLK), :], kd[pl.ds(stp, BLK), :]
                vc, vp = vd[pl.ds(st, BLK), :], vd[pl.ds(stp, BLK), :]
                outs, lses = [], []
                for hm in (hm0, jnp.logical_not(hm0)):
                    qm = jnp.where(hm, qb, jnp.zeros_like(qb))
                    sc = jnp.where(mc, _dot_nt(qm, kc), NEG)
                    sp = jnp.where(mp, _dot_nt(qm, kp), NEG)
                    m = jnp.maximum(jnp.max(sc, axis=1, keepdims=True), jnp.max(sp, axis=1, keepdims=True))
                    pc, pp = jnp.exp(sc - m), jnp.exp(sp - m)
                    l = jnp.sum(pc, axis=1, keepdims=True) + jnp.sum(pp, axis=1, keepdims=True)
                    o = _dot(pc.astype(BF16), vc) + _dot(pp.astype(BF16), vp)
                    outs.append(o / l)
                    lses.append(m + jnp.log(l))
                od[pl.ds(st, BLK), :] = jnp.where(hm0, outs[0], outs[1])
                ld[pl.ds(st, BLK), :] = jnp.where(hm0, lses[0], lses[1])
                return carry

            lax.fori_loop(0, S // BLK, blk, 0)
            _reint(od, on.at[pi], d, False)
            _reint(ld, ln.at[pi], d, False)

        l0, l1, l2 = ln[0], ln[1], ln[2]
        m = jnp.maximum(jnp.maximum(l0, l1), l2)
        e0, e1, e2 = jnp.exp(l0 - m), jnp.exp(l1 - m), jnp.exp(l2 - m)
        den = e0 + e1 + e2
        att_ref[...] = (e0 * on[0] + e1 * on[1] + e2 * on[2]) / den
        lse_ref[...] = m + jnp.log(den)

    col = lambda c0: pl.BlockSpec((S, LANES), lambda p: (0, c0 + p))
    out = pl.BlockSpec((S, LANES), lambda p: (0, p))
    return pl.pallas_call(
        body, name="att_fwd", grid=(NPAIR,),
        in_specs=[col(QB), col(KB), col(VB), pl.BlockSpec((S, 1), lambda p: (0, 0)),
                  pl.BlockSpec((1, LANES), lambda p: (0, 0))],
        out_specs=[out, out, out, out],
        out_shape=[jax.ShapeDtypeStruct((S, R), F32)] * 4,
        scratch_shapes=[pltpu.VMEM((S, LANES), BF16)] * 3 + [pltpu.VMEM((S, LANES), F32)] * 2
        + [pltpu.VMEM((3, S, LANES), F32)] * 2,
        compiler_params=_cp(("parallel",)),
    )(proj, proj, proj, pos, freq)


def _att_bwd_old(dproj, d_att, att, lse, qr, kr, proj, pos, freq):
    def body(dp_in, do_ref, o_ref, lse_ref, qr_ref, kr_ref, v_ref, pos_ref, freq_ref, dp_ref,
             qd, kd, vd, dod, lsd, prd, dqd, dkd, dvd, dqn, dkn, dvn, prn, stage, sems):
        p = pl.program_id(0)
        prn[...] = do_ref[...] * o_ref[...]
        dqn[...] = jnp.zeros_like(dqn)
        dkn[...] = jnp.zeros_like(dkn)
        dvn[...] = jnp.zeros_like(dvn)
        hm0 = lax.broadcasted_iota(jnp.int32, (BLK, LANES), 1) < HEAD

        for d in PATTERNS:
            nb = S // d // BLK
            _deint(qr_ref, qd, d)
            _deint(kr_ref, kd, d)
            _deint(v_ref, vd, d)
            _deint(do_ref, dod, d)
            _deint(lse_ref, lsd, d)
            _deint(prn, prd, d)
            dkd[...] = jnp.zeros_like(dkd)
            dvd[...] = jnp.zeros_like(dvd)

            def blk(b, carry):
                st = pl.multiple_of(b * BLK, BLK)
                stp = pl.multiple_of(jnp.maximum(b - 1, 0) * BLK, BLK)
                mc, mp = _blk_masks(b, nb)
                qb, dob = qd[pl.ds(st, BLK), :], dod[pl.ds(st, BLK), :]
                kc, kp = kd[pl.ds(st, BLK), :], kd[pl.ds(stp, BLK), :]
                vc, vp = vd[pl.ds(st, BLK), :], vd[pl.ds(stp, BLK), :]
                lsb, prb = lsd[pl.ds(st, BLK), :], prd[pl.ds(st, BLK), :]
                dqs = []
                dkc = dkp = dvc = dvp = None
                for hm in (hm0, jnp.logical_not(hm0)):
                    qm = jnp.where(hm, qb, jnp.zeros_like(qb))
                    dom = jnp.where(hm, dob, jnp.zeros_like(dob))
                    lh = jnp.max(jnp.where(hm, lsb, -3e38), axis=1, keepdims=True)
                    delta = jnp.sum(jnp.where(hm, prb, 0.0), axis=1, keepdims=True)
                    pc = jnp.where(mc, jnp.exp(_dot_nt(qm, kc) - lh), 0.0)
                    pp = jnp.where(mp, jnp.exp(_dot_nt(qm, kp) - lh), 0.0)
                    dsc = (pc * (_dot_nt(dom, vc) - delta)).astype(BF16)
                    dsp = (pp * (_dot_nt(dom, vp) - delta)).astype(BF16)
                    dqs.append(_dot(dsc, kc) + _dot(dsp, kp))
                    acc = lambda t, n: n if t is None else t + n
                    dkc, dkp = acc(dkc, _dot_tn(dsc, qm)), acc(dkp, _dot_tn(dsp, qm))
                    dvc, dvp = acc(dvc, _dot_tn(pc.astype(BF16), dom)), acc(dvp, _dot_tn(pp.astype(BF16), dom))
                dqd[pl.ds(st, BLK), :] = jnp.where(hm0, dqs[0], dqs[1])
                dkd[pl.ds(stp, BLK), :] += dkp
                dvd[pl.ds(stp, BLK), :] += dvp
                dkd[pl.ds(st, BLK), :] += dkc
                dvd[pl.ds(st, BLK), :] += dvc
                return carry

            lax.fori_loop(0, S // BLK, blk, 0)
            _reint(dqd, dqn, d, True)
            _reint(dkd, dkn, d, True)
            _reint(dvd, dvn, d, True)

        lane = lax.broadcasted_iota(jnp.int32, (S, LANES), 1)
        first = (lane & (HEAD // 2)) == 0
        cos, sin = _cos_sin(pos_ref, freq_ref)
        dq = dqn[...] * (HEAD ** -0.5)
        dk = dkn[...]
        stage[0] = (dq * cos - _rot_half(dq, first) * sin).astype(BF16)
        stage[1] = (dk * cos - _rot_half(dk, first) * sin).astype(BF16)
        stage[2] = dvn[...].astype(BF16)
        copies = [pltpu.make_async_copy(stage.at[j], dp_ref.at[:, pl.ds((2 + j) * R + p * LANES, LANES)], sems.at[j])
                  for j in range(3)]
        for cp in copies:
            cp.start()
        for cp in copies:
            cp.wait()

    blk = pl.BlockSpec((S, LANES), lambda p: (0, p))
    return pl.pallas_call(
        body, name="att_bwd", grid=(NPAIR,),
        in_specs=[pl.BlockSpec(memory_space=pl.ANY), blk, blk, blk, blk, blk,
                  pl.BlockSpec((S, LANES), lambda p: (0, VB + p)), pl.BlockSpec((S, 1), lambda p: (0, 0)),
                  pl.BlockSpec((1, LANES), lambda p: (0, 0))],
        out_specs=pl.BlockSpec(memory_space=pl.ANY),
        out_shape=jax.ShapeDtypeStruct((S, E), BF16),
        scratch_shapes=[pltpu.VMEM((S, LANES), BF16)] * 4 + [pltpu.VMEM((S, LANES), F32)] * 9
        + [pltpu.VMEM((3, S, LANES), BF16), pltpu.SemaphoreType.DMA((3,))],
        input_output_aliases={0: 0},
        compiler_params=_cp(("arbitrary",)),
    )(dproj, d_att, att, lse, qr, kr, proj, pos, freq)


def _out_fwd_bwd(ya, att, proj, w_out_bf, x, target, mod, norm_post, norm_att):
    ts = 256

    def body(ya_ref, att_ref, gb_ref, w_ref, x_ref, t_ref, mod_ref, npost_ref, natt_ref,
             gx_ref, dya_ref, datt_ref, dgb_ref, gw_ref, acc_ref):
        i = pl.program_id(0)

        @pl.when(i == 0)
        def _():
            gw_ref[...] = jnp.zeros_like(gw_ref)
            acc_ref[...] = jnp.zeros_like(acc_ref)

        gate = mod_ref[:, 2 * D:3 * D]
        att = att_ref[...]
        gb = gb_ref[...]
        sg = _sigmoid(gb)
        silu = gb * sg
        ybp = att * silu
        yb, ybn, rstd_b = _rms_fwd(ybp, natt_ref[...])
        cat = jnp.concatenate([ya_ref[...], yb.astype(BF16)], axis=1)
        mix = _dot(cat, w_ref[...])
        rn, mn, rstd_m = _rms_fwd(mix, npost_ref[...])
        err = x_ref[...] + gate * rn - t_ref[...]
        dy = err * (1.0 / D)
        gx_ref[...] = dy
        dmix, dnpost = _rms_bwd(dy * gate, mn, rstd_m, npost_ref[...])
        dmb = dmix.astype(BF16)
        gw_ref[...] += _dot_tn(cat, dmb)
        dcat = _dot_nt(dmb, w_ref[...])
        dya_ref[...] = dcat[:, 0:R]
        dybp, dnatt = _rms_bwd(dcat[:, R:2 * R], ybn, rstd_b, natt_ref[...])
        datt_ref[...] = dybp * silu
        dgb_ref[...] = (dybp * att * (sg * (1.0 + gb * (1.0 - sg)))).astype(BF16)
        acc_ref[0:1, :] += jnp.sum(dy * rn, axis=0, keepdims=True)
        acc_ref[1:2, :] += dnpost
        acc_ref[2:3, 0:R] += dnatt
        acc_ref[3:4, :] += jnp.sum(jnp.sum(err * err, axis=1, keepdims=True), axis=0, keepdims=True)

    tile = lambda w: pl.BlockSpec((ts, w), lambda i: (i, 0))
    c0 = lambda shape: pl.BlockSpec(shape, lambda i: (0, 0))
    return pl.pallas_call(
        body, name="out_fwd_bwd", grid=(S // ts,),
        in_specs=[tile(R), tile(R), pl.BlockSpec((ts, R), lambda i: (i, 5)), c0((D, D)), tile(D), tile(D),
                  c0((1, 3 * D)), c0((1, D)), c0((1, R))],
        out_specs=[tile(D), tile(R), tile(R), pl.BlockSpec((ts, R), lambda i: (i, 5)), c0((D, D)), c0((8, D))],
        out_shape=[jax.ShapeDtypeStruct((S, D), F32), jax.ShapeDtypeStruct((S, R), F32),
                   jax.ShapeDtypeStruct((S, R), F32), jax.ShapeDtypeStruct((S, E), BF16),
                   jax.ShapeDtypeStruct((D, D), F32), jax.ShapeDtypeStruct((8, D), F32)],
        compiler_params=_cp(("arbitrary",)),
    )(ya, att, proj, w_out_bf, x, target, mod, norm_post, norm_att)


UC = 256
UPC = EC // UC


NU = E // UC


def _unit_of_step(i):
    return (i % NCHIP) * UPC + i // NCHIP


def _in_proj_bwd(ht, dproj, w_in_bf, x, gx1, mod, norm_pre, smalls):
    ts = 256
    nt = S // ts
    half = D // 2
    units = [_unit_of_step(k) for k in range(NU)]
    owners = [u // UPC for u in units]
    ns = len(smalls)

    def body(*refs):
        (ht_ref, dpu_ref, dp_ref, w_hbm, x_ref, gx1_ref, mod_ref, np_ref), refs = refs[:8], refs[8:]
        small_in, refs = refs[:ns], refs[ns:]
        (gx_ref, gin_ref), refs = refs[:2], refs[2:]
        small_out, (acc_out,), refs = refs[:ns], refs[ns:ns + 1], refs[ns + 1:]
        mine, sib, tmp, stage, got, red, acc_ref, hs, hr, ps, pr, bs, br = refs[:13]
        early = _SmallGather(small_in, small_out, *refs[13:16])
        late = _SmallGather([acc_ref], [acc_out], *refs[16:19])
        w_ref, w_sem = refs[19:21]
        i = pl.program_id(0)
        w_copy = pltpu.make_async_copy(w_hbm, w_ref, w_sem)
        pl.when(i == 0)(w_copy.start)
        pl.when(i == NU)(w_copy.wait)
        xx, yy, c = _me()
        ci = 2 * xx + yy
        r0 = pl.multiple_of(c * half, half)
        r1 = pl.multiple_of((1 - c) * half, half)
        pl.when(i == 0)(early.start)
        pl.when(i == NU)(early.forward)

        def exch(k):
            return _remote(tmp.at[k % 2], sib.at[k], hs.at[k], hr.at[k], 1)

        def partial(k, sender):
            return pltpu.make_async_remote_copy(
                src_ref=stage.at[k], dst_ref=got.at[units[k] % UPC, sender], send_sem=ps.at[k],
                recv_sem=pr.at[k, sender], device_id=(owners[k] // 2, owners[k] % 2, c), device_id_type=MESH)

        def back(k, start):
            off = (units[k] % UPC) * UC
            blk = red.at[pl.ds(start, half), off:off + UC]
            return _remote(blk, blk, bs.at[k], br.at[k], 1)

        for k in range(NU + 1):
            @pl.when(i == k)
            def _():
                if k < NU:
                    if k >= 2:
                        exch(k - 2).wait_send()
                    dpu = dpu_ref[...]
                    tmp[k % 2] = _dot(ht_ref[pl.ds(r1, half), :], dpu)
                    exch(k).start()
                    mine[k] = _dot(ht_ref[pl.ds(r0, half), :], dpu)
                if k >= 1:
                    exch(k - 1).wait_recv()
                    mine[k - 1] += sib[k - 1]

                    @pl.when(ci != owners[k - 1])
                    def _():
                        stage[k - 1] = mine[k - 1].astype(BF16)
                        partial(k - 1, ci).start()

        @pl.when(i == NU)
        def _():
            acc_ref[...] = jnp.zeros_like(acc_ref)

        @pl.when(i >= NU)
        def _():
            dh = sum(_dot_nt(dp_ref[:, j * EC:(j + 1) * EC], w_ref[j]) for j in range(NCHIP))
            hp, xn, rstd = _rms_fwd(x_ref[...], np_ref[...])
            dx, dnp = _rms_bwd(dh * (1.0 + mod_ref[:, D:2 * D]), xn, rstd, np_ref[...])
            gx_ref[...] = gx1_ref[...] + dx
            acc_ref[0:1, :] += jnp.sum(dh, axis=0, keepdims=True)
            acc_ref[1:2, :] += jnp.sum(dh * hp, axis=0, keepdims=True)
            acc_ref[2:3, :] += dnp

        for t in range(UPC):
            @pl.when(i == NU + 1 + 2 * t)
            def _():
                for k in range(NCHIP * t, NCHIP * (t + 1)):
                    @pl.when(ci == owners[k])
                    def _():
                        off = (units[k] % UPC) * UC
                        red[pl.ds(r0, half), off:off + UC] = mine[k]
                        for s in range(NCHIP):
                            if s != owners[k]:
                                partial(k, s).wait_recv()
                                red[pl.ds(r0, half), off:off + UC] += got[units[k] % UPC, s].astype(F32)
                        back(k, r0).start()

        @pl.when(i == NU + nt - 1)
        def _():
            late.start()
            exch(NU - 2).wait_send()
            exch(NU - 1).wait_send()
            for k in range(NU):
                @pl.when(ci == owners[k])
                def _():
                    back(k, r1).wait_recv()
                    back(k, r0).wait_send()

                @pl.when(ci != owners[k])
                def _():
                    partial(k, ci).wait_send()
            gin_ref[...] = red[...]
            early.finish()
            late.forward()
            late.finish()

    tile = lambda w: pl.BlockSpec((ts, w), lambda i: (jnp.maximum(i - NU, 0), 0))
    c0 = lambda shape: pl.BlockSpec(shape, lambda i: (0, 0))
    vm = pl.BlockSpec(memory_space=pltpu.VMEM)
    hbm = pl.BlockSpec(memory_space=pl.ANY)
    gathered = [jax.ShapeDtypeStruct((NDEV,) + a.shape, a.dtype) for a in smalls] + [jax.ShapeDtypeStruct((NDEV, 8, D), F32)]
    return pl.pallas_call(
        body, name="in_proj_bwd", grid=(NU + nt,),
        in_specs=[vm, pl.BlockSpec((S, UC), lambda i: (0, _unit_of_step(jnp.minimum(i, NU - 1)))), tile(E),
                  hbm, tile(D), tile(D), c0((1, 3 * D)), c0((1, D))] + [vm] * ns,
        out_specs=[tile(D), vm] + [hbm] * (ns + 1),
        out_shape=[jax.ShapeDtypeStruct((S, D), F32), jax.ShapeDtypeStruct((D, EC), F32)] + gathered,
        scratch_shapes=[pltpu.VMEM((NU, half, UC), F32), pltpu.VMEM((NU, half, UC), F32),
                        pltpu.VMEM((2, half, UC), F32), pltpu.VMEM((NU, half, UC), BF16),
                        pltpu.VMEM((UPC, NCHIP, half, UC), BF16), pltpu.VMEM((D, EC), F32), pltpu.VMEM((8, D), F32),
                        pltpu.SemaphoreType.DMA((NU,)), pltpu.SemaphoreType.DMA((NU,)),
                        pltpu.SemaphoreType.DMA((NU,)), pltpu.SemaphoreType.DMA((NU, NCHIP)),
                        pltpu.SemaphoreType.DMA((NU,)), pltpu.SemaphoreType.DMA((NU,))]
        + _SmallGather.sems(ns) + _SmallGather.sems(1)
        + [pltpu.VMEM((NCHIP, D, EC), BF16), pltpu.SemaphoreType.DMA],
        compiler_params=_cp(("arbitrary",)),
    )(ht, dproj, dproj, w_in_bf, x, gx1, mod, norm_pre, *smalls)


def _block_diag(w):
    n, b, _ = w.shape
    eye = jnp.eye(n, dtype=w.dtype)
    return (eye[:, None, :, None] * w[:, :, None, :]).reshape(n * b, n * b)


def _diag_blocks(m):
    n, b = R // HEAD, HEAD
    return jnp.stack([m[h * b:(h + 1) * b, h * b:(h + 1) * b] for h in range(n)])


def _local_step(x, cos, sin, target, mod, w_in_bf, proj, ht, w_out, conv_w, p):
    rec_p = (conv_w, p["conv_b"], p["w_rg_a"], p["b_rg_a"], p["w_rg_x"], p["b_rg_x"], p["lru_lambda"], p["norm_rec"])
    h_all, ya = _rec_fwd(proj, *rec_p)
    att, qr, kr, lse, w_out_bf = _att_fwd(proj, cos, sin, w_out)
    gx1, d_ya, d_att, dproj, gw_out, acc_o = _out_fwd_bwd(ya, att, proj, w_out_bf.reshape(D, D), x, target, mod,
                                                           p["norm_post"], p["norm_att"])
    dproj, g_out = _att_bwd(dproj, d_att, att, lse, qr, kr, proj, cos, sin, gw_out.reshape(NCHIP, D // NCHIP, D))
    dproj, dwa, dwx, sm = _rec_bwd(dproj, d_ya, proj, h_all, *rec_p)
    grad_x, g_in, *gathered = _in_proj_bwd(ht, dproj, w_in_bf, x, gx1, mod, p["norm_pre"], [acc_o, sm, dwa, dwx])
    return grad_x, g_in, g_out, gathered


def _me():
    return lax.axis_index("x"), lax.axis_index("y"), lax.axis_index("c")


def _flip(v, bit):
    return 1 - v if bit else v


def _peer(rel):
    x, y, c = _me()
    return (_flip(x, rel & 4), _flip(y, rel & 2), _flip(c, rel & 1))


def _remote(src, dst, send_sem, recv_sem, rel):
    return pltpu.make_async_remote_copy(src_ref=src, dst_ref=dst, send_sem=send_sem, recv_sem=recv_sem,
                                        device_id=_peer(rel), device_id_type=MESH)


def _allgather_rows(row, name):
    w = row.shape[1]

    def body(row_ref, out_ref, send_sems, recv_sems, local_sem):
        x, y, c = _me()
        me = 4 * x + 2 * y + c
        mine = pltpu.make_async_copy(row_ref, out_ref.at[pl.ds(me, 1), :], local_sem)
        mine.start()
        sends = [_remote(row_ref, out_ref.at[pl.ds(me, 1), :], send_sems.at[r - 1], recv_sems.at[r - 1], r)
                 for r in range(1, NDEV)]
        for cp in sends:
            cp.start()
        for r in range(1, NDEV):
            px, py, pc = _peer(r)
            src = 4 * px + 2 * py + pc
            _remote(row_ref, out_ref.at[pl.ds(src, 1), :], send_sems.at[r - 1], recv_sems.at[r - 1], r).wait_recv()
        for cp in sends:
            cp.wait_send()
        mine.wait()

    return pl.pallas_call(
        body, name=name,
        in_specs=[pl.BlockSpec(memory_space=pltpu.VMEM)],
        out_specs=pl.BlockSpec(memory_space=pltpu.VMEM),
        out_shape=jax.ShapeDtypeStruct((NDEV, w), row.dtype),
        scratch_shapes=[pltpu.SemaphoreType.DMA((NDEV - 1,)), pltpu.SemaphoreType.DMA((NDEV - 1,)),
                        pltpu.SemaphoreType.DMA],
        compiler_params=pltpu.CompilerParams(vmem_limit_bytes=VMEM_LIMIT),
    )(row)


class _WeightGather:
    SEMS = [pltpu.SemaphoreType.DMA((NCHIP - 1,))] * 4

    def __init__(self, w_ref, out_ref, send_sems, recv_sems, fsend_sems, frecv_sems):
        x, y, c = _me()
        self.w, self.out, self.ci = w_ref, out_ref, 2 * x + y
        self.half = w_ref.shape[0] // 2
        self.r0 = pl.multiple_of(c * self.half, self.half)
        self.r1 = pl.multiple_of((1 - c) * self.half, self.half)
        self.sems = (send_sems, recv_sems, fsend_sems, frecv_sems)

    def _ici(self, chip, k):
        blk = self.out.at[chip, pl.ds(self.r0, self.half), :]
        return _remote(blk, blk, self.sems[0].at[k - 1], self.sems[1].at[k - 1], 2 * k)

    def _d2d(self, chip, start, k):
        blk = self.out.at[chip, pl.ds(start, self.half), :]
        return _remote(blk, blk, self.sems[2].at[k - 1], self.sems[3].at[k - 1], 1)

    def start(self, diagonal=True):
        self.out[self.ci] = self.w[...].astype(BF16)
        for k in range(1, NCHIP if diagonal else NCHIP - 1):
            self._ici(self.ci, k).start()

    def _relay(self, chip, piece, k):
        q = self.half // 2
        blk = self.out.at[chip, pl.ds(self.r0 + piece * q, q), :]
        return _remote(blk, blk, self.relay_sems[0].at[piece], self.relay_sems[1].at[piece], 2 * k)

    def neighbours_landed(self, relay_send_sems, relay_recv_sems):
        self.relay_sems = (relay_send_sems, relay_recv_sems)
        for k in (1, 2):
            self._ici(self.ci ^ k, k).wait_recv()
        self._relay(self.ci ^ 2, 0, 1).start()
        self._relay(self.ci ^ 1, 1, 2).start()
        for k in (1, 2):
            self._d2d(self.ci ^ k, self.r0, k).start()

    def sibling_landed(self, k):
        self._d2d(self.ci ^ k, self.r1, k).wait_recv()

    def diagonal_landed(self):
        for piece, k in ((0, 1), (1, 2)):
            self._relay(self.ci ^ 3, piece, k).wait_recv()
        self._d2d(self.ci ^ 3, self.r0, 3).start()
        self._d2d(self.ci ^ 3, self.r1, 3).wait_recv()

    def finish_relayed(self):
        for k in (1, 2):
            self._ici(self.ci, k).wait_send()
        self._relay(self.ci ^ 2, 0, 1).wait_send()
        self._relay(self.ci ^ 1, 1, 2).wait_send()
        for k in range(1, NCHIP):
            self._d2d(self.ci ^ k, self.r0, k).wait_send()

    def forward(self):
        for k in range(1, NCHIP):
            self._ici(self.ci ^ k, k).wait_recv()
            self._d2d(self.ci ^ k, self.r0, k).start()

    def finish(self):
        for k in range(1, NCHIP):
            self._d2d(self.ci ^ k, self.r1, k).wait_recv()
        self.finish_sends()

    def arrive(self, k):
        self._ici(self.ci ^ k, k).wait_recv()
        self._d2d(self.ci ^ k, self.r0, k).start()
        self._d2d(self.ci ^ k, self.r1, k).wait_recv()

    def finish_sends(self, after_start_diagonal=False):
        for k in range(1, NCHIP):
            if not (after_start_diagonal and k < NCHIP - 1):
                self._ici(self.ci, k).wait_send()
            self._d2d(self.ci ^ k, self.r0, k).wait_send()


class _SmallGather:
    @staticmethod
    def sems(n):
        return [pltpu.SemaphoreType.DMA((n, 7)), pltpu.SemaphoreType.DMA((n, 7)), pltpu.SemaphoreType.DMA((n,))]

    def __init__(self, srcs, outs, send_sems, recv_sems, local_sems):
        x, y, c = _me()
        self.srcs, self.outs = list(srcs), list(outs)
        self.ss, self.rs, self.ls = send_sems, recv_sems, local_sems
        self.ci, self.c = 2 * x + y, c
        self.me = 2 * self.ci + c

    def _own(self, a, slot, rel):
        return _remote(self.srcs[a], self.outs[a].at[self.me], self.ss.at[a, slot], self.rs.at[a, slot], rel)

    def _block(self, a, idx, slot, rel):
        blk = self.outs[a].at[idx]
        return _remote(blk, blk, self.ss.at[a, slot], self.rs.at[a, slot], rel)

    def _local(self, a):
        return pltpu.make_async_copy(self.srcs[a], self.outs[a].at[self.me], self.ls.at[a])

    def start(self):
        for a in range(len(self.srcs)):
            self._local(a).start()
            self._own(a, 0, 1).start()
            for k in range(1, NCHIP):
                self._own(a, k, 2 * k).start()

    def forward(self):
        for a in range(len(self.srcs)):
            for k in range(1, NCHIP):
                idx = 2 * (self.ci ^ k) + self.c
                self._block(a, idx, k, 2 * k).wait_recv()
                self._block(a, idx, 3 + k, 1).start()

    def finish(self):
        for a in range(len(self.srcs)):
            self._block(a, 2 * self.ci + 1 - self.c, 0, 1).wait_recv()
            for k in range(1, NCHIP):
                self._block(a, 2 * (self.ci ^ k) + 1 - self.c, 3 + k, 1).wait_recv()
            self._own(a, 0, 1).wait_send()
            for k in range(1, NCHIP):
                self._own(a, k, 2 * k).wait_send()
                self._block(a, 2 * (self.ci ^ k) + self.c, 3 + k, 1).wait_send()
            self._local(a).wait()


def _start_in_proj(crow, w_ada, b_cols, w_in, pos, x, norm_pre, order):
    ts = 512
    nt = S // ts
    wc = crow.shape[1]

    def body(order_ref, crow_ref, wada_ref, b_ref, win_ref, pos_ref, freq_ref, x_ref, np_ref,
             g0_ref, mod_ref, wbf_ref, cos_ref, sin_ref, proj_ref, ht_ref,
             g0s, modp, modb, wbuf, hb_all, cs, cr, ms, mr, ws, wr, fs, fr, local_sems, ys, yr, osem):
        s, t = pl.program_id(0), pl.program_id(1)
        x, y, c = _me()
        ci = 2 * x + y
        me = 2 * ci + c
        wg = _WeightGather(win_ref, wbuf, ws, wr, fs, fr)

        @pl.when(jnp.logical_and(s == 0, t == 0))
        def _():
            wg.start(diagonal=False)
            mine = pltpu.make_async_copy(crow_ref, g0s.at[pl.ds(me, 1), :], local_sems.at[0])
            mine.start()
            csend = [_remote(crow_ref, g0s.at[pl.ds(me, 1), :], cs.at[r - 1], cr.at[r - 1], r) for r in range(1, NDEV)]
            for cp in csend:
                cp.start()
            cos_ref[...], sin_ref[...] = _cos_sin(pos_ref, freq_ref)
            for r in range(1, NDEV):
                px, py, pc = _peer(r)
                _remote(crow_ref, g0s.at[pl.ds(4 * px + 2 * py + pc, 1), :], cs.at[r - 1], cr.at[r - 1], r).wait_recv()
            mine.wait()
            cv = g0s[:, 0:D]
            sc = cv * _sigmoid(cv)
            scb = jnp.concatenate([sc, jnp.zeros_like(sc)], axis=0).astype(BF16)
            modp[...] = _dot(scb, wada_ref[...].astype(BF16))[0:NDEV, :] + b_ref[...]
            own = pltpu.make_async_copy(modp.at[pl.ds(me, 1), :], modb.at[ci], local_sems.at[1])
            own.start()
            msend = []
            for k in range(1, NCHIP):
                cp = _remote(modp.at[pl.ds(2 * (ci ^ k) + c, 1), :], modb.at[ci], ms.at[k - 1], mr.at[k - 1], 2 * k)
                cp.start()
                msend.append(cp)
            for k in range(1, NCHIP):
                _remote(modp.at[pl.ds(me, 1), :], modb.at[ci ^ k], ms.at[k - 1], mr.at[k - 1], 2 * k).wait_recv()
            own.wait()
            for j in range(NCHIP):
                mod_ref[:, j * EC:(j + 1) * EC] = modb[j]
            for cp in csend + msend:
                cp.wait_send()
            g0_ref[...] = g0s[...]

        def keep(k):
            return pltpu.make_async_copy(wbuf.at[ci ^ k], wbf_ref.at[ci ^ k], osem.at[k])

        @pl.when(jnp.logical_and(s == 1, t == 0))
        def _():
            keep(0).start()
            wg.neighbours_landed(ys, yr)
            wg.sibling_landed(1)
            keep(1).start()

        @pl.when(jnp.logical_and(s == 2, t == 0))
        def _():
            wg.sibling_landed(2)
            keep(2).start()

        @pl.when(jnp.logical_and(s == 3, t == 0))
        def _():
            wg.relay_sems = (ys, yr)
            wg.diagonal_landed()
            keep(3).start()

        rows = pl.ds(pl.multiple_of(t * ts, ts), ts)

        @pl.when(s == 0)
        def _():
            hp, _, _ = _rms_fwd(x_ref[...], np_ref[...])
            h = hp * (1.0 + mod_ref[:, D:2 * D]) + mod_ref[:, 0:D]
            hb_all[rows, :] = h.astype(BF16)
            ht_ref[...] = h.T.astype(BF16)

        proj_ref[...] = _dot(hb_all[rows, :], wbuf[ci ^ s])

        @pl.when(jnp.logical_and(s == NCHIP - 1, t == nt - 1))
        def _():
            wg.relay_sems = (ys, yr)
            wg.finish_relayed()
            for k in range(NCHIP):
                keep(k).wait()

    vm = pl.BlockSpec(memory_space=pltpu.VMEM)
    first_pass = lambda s, t: jnp.where(s == 0, t, nt - 1)
    grid_spec = pltpu.PrefetchScalarGridSpec(
        num_scalar_prefetch=1, grid=(NCHIP, nt),
        in_specs=[vm, vm, vm, vm, vm, vm, pl.BlockSpec((ts, D), lambda s, t, o: (first_pass(s, t), 0)),
                  pl.BlockSpec((1, D), lambda s, t, o: (0, 0))],
        out_specs=[vm, vm, pl.BlockSpec(memory_space=pl.ANY), vm, vm, pl.BlockSpec((ts, EC), lambda s, t, o: (t, o[s])),
                   pl.BlockSpec((D, ts), lambda s, t, o: (0, first_pass(s, t)))],
        scratch_shapes=[pltpu.VMEM((NDEV, wc), F32), pltpu.VMEM((NDEV, EC), F32), pltpu.VMEM((NCHIP, 1, EC), F32),
                        pltpu.VMEM((NCHIP, D, EC), BF16), pltpu.VMEM((S, D), BF16),
                        pltpu.SemaphoreType.DMA((NDEV - 1,)), pltpu.SemaphoreType.DMA((NDEV - 1,)),
                        pltpu.SemaphoreType.DMA((NCHIP - 1,)), pltpu.SemaphoreType.DMA((NCHIP - 1,))]
        + _WeightGather.SEMS + [pltpu.SemaphoreType.DMA((2,))] * 3 + [pltpu.SemaphoreType.DMA((NCHIP,))])
    return pl.pallas_call(
        body, name="start_in_proj", grid_spec=grid_spec,
        out_shape=[jax.ShapeDtypeStruct((NDEV, wc), F32), jax.ShapeDtypeStruct((1, 3 * D), F32),
                   jax.ShapeDtypeStruct((NCHIP, D, EC), BF16), jax.ShapeDtypeStruct((S, LANES), F32),
                   jax.ShapeDtypeStruct((S, LANES), F32), jax.ShapeDtypeStruct((S, E), F32),
                   jax.ShapeDtypeStruct((D, S), BF16)],
        compiler_params=_cp(("arbitrary", "arbitrary")),
    )(order, crow, w_ada, b_cols, w_in, pos, _rope_freq(), x, norm_pre)


def _start_gather(crow, w_ada, b_cols, w_in):
    wc = crow.shape[1]

    def body(crow_ref, wada_ref, b_ref, win_ref, g0_ref, mod_ref, wbf_ref,
             modp, modb, cs, cr, ms, mr, ws, wr, fs, fr, local_sems):
        x, y, c = _me()
        ci = 2 * x + y
        me = 2 * ci + c
        wg = _WeightGather(win_ref, wbf_ref, ws, wr, fs, fr)
        mine = pltpu.make_async_copy(crow_ref, g0_ref.at[pl.ds(me, 1), :], local_sems.at[0])
        mine.start()
        csend = [_remote(crow_ref, g0_ref.at[pl.ds(me, 1), :], cs.at[r - 1], cr.at[r - 1], r) for r in range(1, NDEV)]
        for cp in csend:
            cp.start()
        wg.start()
        for r in range(1, NDEV):
            px, py, pc = _peer(r)
            _remote(crow_ref, g0_ref.at[pl.ds(4 * px + 2 * py + pc, 1), :], cs.at[r - 1], cr.at[r - 1], r).wait_recv()
        mine.wait()
        cv = g0_ref[:, 0:D]
        sc = cv * _sigmoid(cv)
        scb = jnp.concatenate([sc, jnp.zeros_like(sc)], axis=0).astype(BF16)
        modp[...] = _dot(scb, wada_ref[...].astype(BF16))[0:NDEV, :] + b_ref[...]
        own = pltpu.make_async_copy(modp.at[pl.ds(me, 1), :], modb.at[ci], local_sems.at[1])
        own.start()
        msend = []
        for k in range(1, NCHIP):
            dst = 2 * (ci ^ k) + c
            cp = _remote(modp.at[pl.ds(dst, 1), :], modb.at[ci], ms.at[k - 1], mr.at[k - 1], 2 * k)
            cp.start()
            msend.append(cp)
        for k in range(1, NCHIP):
            _remote(modp.at[pl.ds(me, 1), :], modb.at[ci ^ k], ms.at[k - 1], mr.at[k - 1], 2 * k).wait_recv()
        own.wait()
        for j in range(NCHIP):
            mod_ref[:, j * EC:(j + 1) * EC] = modb[j]
        wg.forward()
        wg.finish()
        for cp in csend + msend:
            cp.wait_send()

    vm = pl.BlockSpec(memory_space=pltpu.VMEM)
    return pl.pallas_call(
        body, name="start_gather",
        in_specs=[vm] * 4, out_specs=[vm] * 3,
        out_shape=[jax.ShapeDtypeStruct((NDEV, wc), F32), jax.ShapeDtypeStruct((1, 3 * D), F32),
                   jax.ShapeDtypeStruct((NCHIP, D, EC), BF16)],
        scratch_shapes=[pltpu.VMEM((NDEV, EC), F32), pltpu.VMEM((NCHIP, 1, EC), F32),
                        pltpu.SemaphoreType.DMA((NDEV - 1,)), pltpu.SemaphoreType.DMA((NDEV - 1,)),
                        pltpu.SemaphoreType.DMA((NCHIP - 1,)), pltpu.SemaphoreType.DMA((NCHIP - 1,))]
        + _WeightGather.SEMS + [pltpu.SemaphoreType.DMA((2,))],
        compiler_params=pltpu.CompilerParams(vmem_limit_bytes=VMEM_LIMIT),
    )(crow, w_ada, b_cols, w_in)


class _ReduceScatter:
    @staticmethod
    def scratch(n_units, rows, ucols, max_owned):
        half = rows // 2
        return [pltpu.VMEM((n_units, half, ucols), F32), pltpu.VMEM((n_units, half, ucols), BF16),
                pltpu.VMEM((max_owned, NCHIP, half, ucols), BF16),
                pltpu.SemaphoreType.DMA((2,)), pltpu.SemaphoreType.DMA((n_units,)),
                pltpu.SemaphoreType.DMA((n_units, NCHIP)), pltpu.SemaphoreType.DMA((n_units,)),
                pltpu.SemaphoreType.DMA((n_units,))]

    def __init__(self, g_ref, out_ref, units, sib, stage, got, sem1, send2, recv2, send3, recv3):
        x, y, c = _me()
        self.c, self.ci = c, 2 * x + y
        self.g, self.out, self.units = g_ref, out_ref, units
        self.sib, self.stage, self.got = sib, stage, got
        self.sem1, self.send2, self.recv2, self.send3, self.recv3 = sem1, send2, recv2, send3, recv3
        self.half = g_ref.shape[1] // 2
        self.ucols = g_ref.shape[2]
        self.r0 = pl.multiple_of(c * self.half, self.half)
        self.r1 = pl.multiple_of((1 - c) * self.half, self.half)
        self.slot0 = units[0][0]
        assert [u[0] for u in units] == list(range(self.slot0, self.slot0 + len(units)))
        seen = {}
        self.local = []
        for _, owner, _ in units:
            self.local.append(seen.get(owner, 0))
            seen[owner] = seen.get(owner, 0) + 1

    def _halves(self):
        n = len(self.units)
        return _remote(self.g.at[pl.ds(self.slot0, n), pl.ds(self.r1, self.half), :], self.sib,
                       self.sem1.at[0], self.sem1.at[1], 1)

    def _partial(self, i, sender):
        _, owner, _ = self.units[i]
        return pltpu.make_async_remote_copy(
            src_ref=self.stage.at[i], dst_ref=self.got.at[self.local[i], sender],
            send_sem=self.send2.at[i], recv_sem=self.recv2.at[i, sender],
            device_id=(owner // 2, owner % 2, self.c), device_id_type=MESH)

    def _back(self, i, start):
        off = self.units[i][2]
        blk = self.out.at[pl.ds(start, self.half), off:off + self.ucols]
        return _remote(blk, blk, self.send3.at[i], self.recv3.at[i], 1)

    def at_steps(self, step, start, send, reduce, finish, out_ref):
        @pl.when(step == start)
        def _():
            self.out[...] = jnp.zeros_like(self.out)
            self.start_halves()

        pl.when(step == send)(self.send_partials)
        pl.when(step == reduce)(self.reduce_owned)

        @pl.when(step == finish)
        def _():
            self.finish()
            out_ref[...] = self.out[...]

    def start_halves(self):
        self._halves().start()

    def send_partials(self):
        self._halves().wait_recv()
        for i, (slot, owner, _) in enumerate(self.units):
            @pl.when(self.ci != owner)
            def _():
                self.stage[i] = (self.g[slot, pl.ds(self.r0, self.half), :] + self.sib[i]).astype(BF16)
                self._partial(i, self.ci).start()

    def reduce_owned(self):
        for i, (slot, owner, off) in enumerate(self.units):
            @pl.when(self.ci == owner)
            def _():
                rows, cols = pl.ds(self.r0, self.half), slice(off, off + self.ucols)
                self.out[rows, cols] = self.g[slot, pl.ds(self.r0, self.half), :] + self.sib[i]
                for s in range(NCHIP):
                    if s != owner:
                        self._partial(i, s).wait_recv()
                        self.out[rows, cols] += self.got[self.local[i], s].astype(F32)
                self._back(i, self.r0).start()

    def finish(self):
        self._halves().wait_send()
        for i, (_, owner, _) in enumerate(self.units):
            @pl.when(self.ci == owner)
            def _():
                self._back(i, self.r1).wait_recv()
                self._back(i, self.r0).wait_send()

            @pl.when(self.ci != owner)
            def _():
                self._partial(i, self.ci).wait_send()


def _reduce_scatter(g4, name):
    _, rows, cols = g4.shape
    units = [(j, j, 0) for j in range(NCHIP)]

    def body(g_ref, out_ref, *scratch):
        rs = _ReduceScatter(g_ref, out_ref, units, *scratch)
        rs.start_halves()
        rs.send_partials()
        rs.reduce_owned()
        rs.finish()

    return pl.pallas_call(
        body, name=name,
        in_specs=[pl.BlockSpec(memory_space=pltpu.VMEM)],
        out_specs=pl.BlockSpec(memory_space=pltpu.VMEM),
        out_shape=jax.ShapeDtypeStruct((rows, cols), F32),
        scratch_shapes=_ReduceScatter.scratch(NCHIP, rows, cols, 1),
        compiler_params=pltpu.CompilerParams(vmem_limit_bytes=VMEM_LIMIT),
    )(g4)


def _silu_rows(c_ref):
    cv = c_ref[...]
    sc = cv * _sigmoid(cv)
    return jnp.concatenate([sc, jnp.zeros_like(sc)], axis=0).astype(BF16)


def _ada_fwd(cg, w_ada, b_cols):
    def body(c_ref, w_ref, b_ref, o_ref):
        o_ref[...] = _dot(_silu_rows(c_ref), w_ref[...].astype(BF16))[0:NDEV, :] + b_ref[...]

    return pl.pallas_call(body, name="ada_fwd", out_shape=jax.ShapeDtypeStruct((NDEV, EC), F32),
                          compiler_params=_cp())(cg, w_ada, b_cols)


def _ada_bwd(cg, dmod_cols):
    def body(c_ref, d_ref, o_ref):
        dm = d_ref[...]
        dmb = jnp.concatenate([dm, jnp.zeros_like(dm)], axis=0).astype(BF16)
        o_ref[...] = _dot_tn(_silu_rows(c_ref), dmb)

    return pl.pallas_call(body, name="ada_bwd", out_shape=jax.ShapeDtypeStruct((D, EC), F32),
                          compiler_params=_cp())(cg, dmod_cols)


def _sum_rows(g):
    def body(g_ref, o_ref):
        acc = g_ref[0:1, :]
        for r in range(1, NDEV):
            acc = acc + g_ref[r:r + 1, :]
        o_ref[...] = acc

    return pl.pallas_call(body, name="sum_rows", out_shape=jax.ShapeDtypeStruct((1, g.shape[1]), F32),
                          compiler_params=_cp())(g)


def _adamw(w, g, m, v, name):
    rows, cols = w.shape
    tr = 256 if rows % 256 == 0 else rows

    def body(w_ref, g_ref, m_ref, v_ref, d_ref, nm_ref, nv_ref):
        gv = g_ref[...]
        nm = B1 * m_ref[...] + (1.0 - B1) * gv
        nv = B2 * v_ref[...] + (1.0 - B2) * (gv * gv)
        m_hat = nm / (1.0 - B1 ** STEP)
        v_hat = nv / (1.0 - B2 ** STEP)
        d_ref[...] = (-LR) * (m_hat / (jnp.sqrt(v_hat) + ADAM_EPS) + WD * w_ref[...])
        nm_ref[...] = nm
        nv_ref[...] = nv

    spec = pl.BlockSpec((tr, cols), lambda i: (i, 0))
    return pl.pallas_call(
        body, name=name, grid=(rows // tr,), in_specs=[spec] * 4, out_specs=[spec] * 3,
        out_shape=[jax.ShapeDtypeStruct((rows, cols), F32)] * 3,
        compiler_params=_cp(("parallel",)),
    )(w, g, m, v)


def _adamw_values(w, g, m, v):
    nm = B1 * m + (1.0 - B1) * g
    nv = B2 * v + (1.0 - B2) * (g * g)
    m_hat = nm / (1.0 - B1 ** STEP)
    v_hat = nv / (1.0 - B2 ** STEP)
    return (-LR) * (m_hat / (jnp.sqrt(v_hat) + ADAM_EPS) + WD * w), nm, nv


NB = R // HEAD
SMALL = (("b_ada", (1, 3 * D)), ("norm_pre", (1, D)), ("norm_post", (1, D)), ("conv_w", (4, R // NCHIP)),
         ("conv_b", (1, R)), ("w_rg_a", (NB, HEAD, HEAD)), ("b_rg_a", (1, R)), ("w_rg_x", (NB, HEAD, HEAD)),
         ("b_rg_x", (1, R)), ("lru_lambda", (1, R)), ("norm_rec", (1, R)), ("norm_att", (1, R)))


def _small_update(ao8, sm8, dwa8, dwx8, ai8, cg, params):
    n = len(SMALL)

    def body(ao_ref, sm_ref, dwa_ref, dwx_ref, ai_ref, cg_ref, *refs):
        pin, pout, (gada_ref, loss_ref, dmod) = refs[:3 * n], refs[3 * n:7 * n], refs[7 * n:]
        xx, yy, _ = _me()
        ci = 2 * xx + yy

        def total(ref, *idx):
            acc = ref[(0,) + idx].astype(F32)
            for d in range(1, NDEV):
                acc = acc + ref[(d,) + idx].astype(F32)
            return acc

        row = lambda ref, r, lanes=slice(None): total(ref, slice(r, r + 1), lanes)
        mine = lambda parts: sum(jnp.where(ci == j, part, 0.0) for j, part in enumerate(parts))
        cw = R // NCHIP
        grads = {
            "b_ada": [jnp.concatenate([row(ai_ref, 0), row(ai_ref, 1), row(ao_ref, 0)], axis=1)],
            "norm_pre": [row(ai_ref, 2)], "norm_post": [row(ao_ref, 1)],
            "conv_w": [mine([row(sm_ref, 8 + r, slice(j * cw, (j + 1) * cw)) for j in range(NCHIP)]) for r in range(4)],
            "conv_b": [row(sm_ref, 4)], "b_rg_a": [row(sm_ref, 0)], "b_rg_x": [row(sm_ref, 1)],
            "lru_lambda": [row(sm_ref, 2)], "norm_rec": [row(sm_ref, 3)], "norm_att": [row(ao_ref, 2, slice(0, R))],
            "w_rg_a": [total(dwa_ref, h) for h in range(NB)], "w_rg_x": [total(dwx_ref, h) for h in range(NB)],
        }
        loss_ref[...] = row(ao_ref, 3, slice(0, LANES)) * (0.5 / D)
        for k, (name, shape) in enumerate(SMALL):
            w_ref, m_ref, v_ref = pin[3 * k:3 * k + 3]
            outs = pout[4 * k:4 * k + 4]
            for r, g in enumerate(grads[name]):
                at = (slice(None),) if len(grads[name]) == 1 else ((r,) if len(shape) == 3 else (slice(r, r + 1),))
                res = (g,) + _adamw_values(w_ref[at], g, m_ref[at], v_ref[at])
                for o_ref, val in zip(outs, res):
                    o_ref[at] = val
        for d in range(NDEV):
            dmod[d:d + 1, :] = jnp.concatenate([ai_ref[d, 0:1, :], ai_ref[d, 1:2, :], ao_ref[d, 0:1, :]], axis=1)
        cols = mine([dmod[:, j * EC:(j + 1) * EC] for j in range(NCHIP)])
        colsb = jnp.concatenate([cols, jnp.zeros_like(cols)], axis=0).astype(BF16)
        gada_ref[...] = _dot_tn(_silu_rows(cg_ref), colsb)

    shapes = [jax.ShapeDtypeStruct(s, F32) for _, s in SMALL]
    outs = pl.pallas_call(
        body, name="small_update",
        out_shape=[s for s in shapes for _ in range(4)] + [jax.ShapeDtypeStruct((D, EC), F32),
                                                           jax.ShapeDtypeStruct((1, LANES), F32)],
        scratch_shapes=[pltpu.VMEM((NDEV, 3 * D), F32)],
        compiler_params=_cp(),
    )(ao8, sm8, dwa8, dwx8, ai8, cg, *params)
    return outs[:4 * n], outs[4 * n], outs[4 * n + 1]


BIG = ("w_ada", "w_in", "w_out")
WEIGHTS = ("w_ada", "b_ada", "norm_pre", "norm_post", "w_in", "conv_w", "conv_b", "w_rg_a", "b_rg_a", "w_rg_x",
           "b_rg_x", "lru_lambda", "norm_rec", "norm_att", "w_out")


def kernel(x, c, positions, w_ada, b_ada, norm_pre, norm_post, w_in, conv_w, conv_b, w_rg_a, b_rg_a, w_rg_x, b_rg_x, lru_lambda, norm_rec, norm_att, w_out, loss_target, m_w_ada, m_b_ada, m_norm_pre, m_norm_post, m_w_in, m_conv_w, m_conv_b, m_w_rg_a, m_b_rg_a, m_w_rg_x, m_b_rg_x, m_lru_lambda, m_norm_rec, m_norm_att, m_w_out, v_w_ada, v_b_ada, v_norm_pre, v_norm_post, v_w_in, v_conv_w, v_conv_b, v_w_rg_a, v_b_rg_a, v_w_rg_x, v_b_rg_x, v_lru_lambda, v_norm_rec, v_norm_att, v_w_out):
    given = dict(locals())
    wts = {n: given[n] for n in WEIGHTS}
    ms = {n: given["m_" + n] for n in WEIGHTS}
    vs = {n: given["v_" + n] for n in WEIGHTS}
    xi, yi, _ = _me()
    chip = 2 * xi + yi
    cw_loc = R // NCHIP

    b_cols = lax.dynamic_slice(b_ada, (0, chip * EC), (1, EC))
    order = (chip ^ jnp.arange(NCHIP, dtype=jnp.int32)).astype(jnp.int32)
    g0, mod, w_in_bf, cos, sin, proj, ht = _start_in_proj(
        jnp.concatenate([c, conv_w.reshape(1, 4 * cw_loc)], axis=1), w_ada[0], b_cols, w_in[0],
        positions.reshape(S, 1), x[0], norm_pre, order)
    cg = g0[:, 0:D]
    conv_full = g0[0::2, D:].reshape(NCHIP, 4, cw_loc).transpose(1, 0, 2).reshape(4, R)

    p = dict(norm_pre=norm_pre, norm_post=norm_post, conv_b=conv_b, b_rg_a=b_rg_a, b_rg_x=b_rg_x,
             lru_lambda=lru_lambda, norm_rec=norm_rec, norm_att=norm_att, w_rg_a=w_rg_a[0], w_rg_x=w_rg_x[0])
    grad_x, g_in, g_out, gathered = _local_step(
        x[0], cos, sin, loss_target[0], mod, w_in_bf, proj, ht, w_out[0], conv_full, p)

    params = [d[n].reshape(shape) for n, shape in SMALL for d in (wts, ms, vs)]
    small_out, g_ada, loss_row = _small_update(*gathered, cg, params)
    grads = {"w_out": g_out, "w_in": g_in, "w_ada": g_ada}
    delta, new_m, new_v = {}, {}, {}
    for k, (n, _) in enumerate(SMALL):
        grads[n], delta[n], new_m[n], new_v[n] = small_out[4 * k:4 * k + 4]
    for n in BIG:
        delta[n], new_m[n], new_v[n] = _adamw(wts[n][0], grads[n], ms[n][0], vs[n][0], "adamw_" + n)
    out = lambda d: [d[n].reshape(wts[n].shape) for n in WEIGHTS]
    return (loss_row[0, 0], grad_x.reshape(x.shape), *out(grads), *out(delta), *out(new_m), *out(new_v))
```

```python
import numpy as np
import jax
import jax.numpy as jnp
from jax import lax
from jax.experimental import pallas as pl
from jax.experimental.pallas import tpu as pltpu

F32 = jnp.float32
BF16 = jnp.bfloat16

S = 2048
D = 1024
E = 3072
R = 512
NDEV = 8
NCHIP = 4
EC = 768
LRU_C = 8.0
EPS = 1e-6
NEG = -1e30
HEAD = 64
BLK = 128
PATTERNS = (1, 4, 16)
ROPE_THETA = 10000.0
LANES = 128
VMEM_LIMIT = 56 * 1024 * 1024

B1, B2, LR, WD, ADAM_EPS, STEP = 0.9, 0.999, 0.001, 0.01, 1e-8, 10
MESH = pl.DeviceIdType.MESH


def _cp(sem=None, **kw):
    return pltpu.CompilerParams(dimension_semantics=sem, vmem_limit_bytes=VMEM_LIMIT, **kw)


def _dot(a, b):
    return jnp.dot(a, b, preferred_element_type=F32)


def _dot_nt(a, b):
    return lax.dot_general(a, b, (((1,), (1,)), ((), ())), preferred_element_type=F32)


def _dot_tn(a, b):
    return lax.dot_general(a, b, (((0,), (0,)), ((), ())), preferred_element_type=F32)


def _sigmoid(x):
    return 1.0 / (1.0 + jnp.exp(-x))


def _expm1(x):
    poly = x * (1.0 + x * (0.5 + x * (1.0 / 6 + x * (1.0 / 24 + x * (1.0 / 120 + x * (1.0 / 720))))))
    return jnp.where(jnp.abs(x) < 0.3, poly, jnp.exp(x) - 1.0)


def _rms_fwd(v, g):
    rstd = lax.rsqrt(jnp.mean(v * v, axis=-1, keepdims=True) + EPS)
    vn = v * rstd
    return vn * g, vn, rstd


def _rms_bwd(dy, vn, rstd, g):
    dvn = dy * g
    dv = rstd * (dvn - vn * jnp.mean(dvn * vn, axis=-1, keepdims=True))
    return dv, jnp.sum(dy * vn, axis=0, keepdims=True)


RT = 256


def _shift_down(cur, prev8, j, row):
    if j == 0:
        return cur
    top = jnp.tile(pltpu.roll(prev8, j, 0), (RT // 8, 1))
    return jnp.where(row >= j, pltpu.roll(cur, j, 0), top)


def _shift_up(cur, next8, j, row):
    if j == 0:
        return cur
    bot = jnp.tile(pltpu.roll(next8, 8 - j, 0), (RT // 8, 1))
    return jnp.where(row < RT - j, pltpu.roll(cur, RT - j, 0), bot)


def _rec_gates(xp, xprev8, row, cw_ref, cb_ref, wa_ref, ba_ref, wx_ref, bx_ref, lam_ref):
    xa = cb_ref[...] + sum(cw_ref[3 - j:4 - j, :] * _shift_down(xp, xprev8, j, row) for j in range(4))
    xab = xa.astype(BF16)
    r = _sigmoid(_dot(xab, wa_ref[...]) + ba_ref[...])
    ig = _sigmoid(_dot(xab, wx_ref[...]) + bx_ref[...])
    nl = -lam_ref[...]
    sp = jnp.maximum(nl, 0.0) + jnp.log1p(jnp.exp(-jnp.abs(nl)))
    la = (-LRU_C) * r * sp
    a = jnp.exp(la)
    mult = jnp.sqrt(-_expm1(2.0 * la))
    return dict(xa=xa, xab=xab, r=r, ig=ig, sp=sp, la=la, a=a, mult=mult)


def _scan_fwd(a, u, row):
    sh = 1
    while sh < RT:
        a_s = jnp.where(row >= sh, pltpu.roll(a, sh, 0), 1.0)
        u_s = jnp.where(row >= sh, pltpu.roll(u, sh, 0), 0.0)
        u = a * u_s + u
        a = a * a_s
        sh *= 2
    return a, u


def _scan_bwd(al, g, row):
    sh = 1
    while sh < RT:
        al_s = jnp.where(row < RT - sh, pltpu.roll(al, RT - sh, 0), 1.0)
        g_s = jnp.where(row < RT - sh, pltpu.roll(g, RT - sh, 0), 0.0)
        g = g + al * g_s
        al = al * al_s
        sh *= 2
    return g


def _dense_from_blocks(blocks_ref, dense_ref):
    dense_ref[...] = jnp.zeros_like(dense_ref)
    for h in range(R // HEAD):
        dense_ref[h * HEAD:(h + 1) * HEAD, h * HEAD:(h + 1) * HEAD] = blocks_ref[h].astype(dense_ref.dtype)


def _rec_fwd(proj, conv_w, conv_b, wa_b, ba, wx_b, bx, lam, norm_rec):
    nt = S // RT

    def body(p_ref, cw_ref, cb_ref, wa_ref, ba_ref, wx_ref, bx_ref, lam_ref, nr_ref,
             h_ref, ya_ref, prev8, hc, wad, wxd):
        i = pl.program_id(0)

        @pl.when(i == 0)
        def _():
            prev8[...] = jnp.zeros_like(prev8)
            hc[...] = jnp.zeros_like(hc)
            _dense_from_blocks(wa_ref, wad)
            _dense_from_blocks(wx_ref, wxd)

        row = lax.broadcasted_iota(jnp.int32, (RT, R), 0)
        xp = p_ref[:, 0:R]
        ga = p_ref[:, R:2 * R]
        f = _rec_gates(xp, prev8[...], row, cw_ref, cb_ref, wad, ba_ref, wxd, bx_ref, lam_ref)
        u = f["mult"] * (f["ig"] * f["xa"])
        acum, hh = _scan_fwd(f["a"], u, row)
        h = hh + acum * hc[0:1, :]
        h_ref[...] = h
        hc[0:1, :] = h_ref[RT - 1:RT, :]
        prev8[...] = p_ref[RT - 8:RT, 0:R]
        yp = h * (ga * _sigmoid(ga))
        ya, _, _ = _rms_fwd(yp, nr_ref[...])
        ya_ref[...] = ya.astype(BF16)

    row1 = lambda n: pl.BlockSpec((1, n), lambda i: (0, 0))
    blocks = pl.BlockSpec((R // HEAD, HEAD, HEAD), lambda i: (0, 0, 0))
    return pl.pallas_call(
        body, name="rec_fwd", grid=(nt,),
        in_specs=[pl.BlockSpec((RT, 2 * R), lambda i: (i, 0)), pl.BlockSpec((4, R), lambda i: (0, 0)), row1(R),
                  blocks, row1(R), blocks, row1(R), row1(R), row1(R)],
        out_specs=[pl.BlockSpec((RT, R), lambda i: (i, 0)), pl.BlockSpec((RT, R), lambda i: (i, 0))],
        out_shape=[jax.ShapeDtypeStruct((S, R), F32), jax.ShapeDtypeStruct((S, R), BF16)],
        scratch_shapes=[pltpu.VMEM((8, R), F32), pltpu.VMEM((8, R), F32), pltpu.VMEM((R, R), BF16),
                        pltpu.VMEM((R, R), BF16)],
        compiler_params=_cp(("arbitrary",)),
    )(proj, conv_w, conv_b, wa_b, ba, wx_b, bx, lam, norm_rec)


def _rec_bwd(dproj, d_ya, proj, h_all, conv_w, conv_b, wa_b, ba, wx_b, bx, lam, norm_rec):
    nt = S // RT

    def body(dp_in, dya_ref, p_ref, pprev_ref, h_ref, hprev_ref, cw_ref, cb_ref, wab_ref, ba_ref, wxb_ref, bx_ref,
             lam_ref, nr_ref, dp_ref, dwab_ref, dwxb_ref, sm_ref, nxt8, cg, wa_ref, wx_ref, dwa_ref, dwx_ref):
        i = pl.program_id(0)
        ti = nt - 1 - i

        @pl.when(i == 0)
        def _():
            nxt8[...] = jnp.zeros_like(nxt8)
            cg[...] = jnp.zeros_like(cg)
            dwa_ref[...] = jnp.zeros_like(dwa_ref)
            dwx_ref[...] = jnp.zeros_like(dwx_ref)
            sm_ref[...] = jnp.zeros_like(sm_ref)
            _dense_from_blocks(wab_ref, wa_ref)
            _dense_from_blocks(wxb_ref, wx_ref)

        row = lax.broadcasted_iota(jnp.int32, (RT, R), 0)
        first = (ti > 0).astype(F32)
        xprev8 = pprev_ref[...] * first
        hprev8 = hprev_ref[...] * first
        xp = p_ref[:, 0:R]
        ga = p_ref[:, R:2 * R]
        f = _rec_gates(xp, xprev8, row, cw_ref, cb_ref, wa_ref, ba_ref, wx_ref, bx_ref, lam_ref)
        xa, r, ig, a, mult = f["xa"], f["r"], f["ig"], f["a"], f["mult"]
        h = h_ref[...]
        sg = _sigmoid(ga)
        gate = ga * sg
        yp = h * gate
        _, ypn, rstd = _rms_fwd(yp, nr_ref[...])
        d_yp, dnr = _rms_bwd(dya_ref[...], ypn, rstd, nr_ref[...])
        d_ga = d_yp * h * (sg * (1.0 + ga * (1.0 - sg)))
        dh = d_yp * gate + jnp.where(row == RT - 1, cg[0:1, :], 0.0)
        al = jnp.where(row < RT - 1, pltpu.roll(a, RT - 1, 0), 0.0)
        g = _scan_bwd(al, dh, row)
        cg[0:1, :] = jnp.sum(jnp.where(row == 0, a * g, 0.0), axis=0, keepdims=True)
        h_m1 = _shift_down(h, hprev8, 1, row)
        da = g * h_m1
        ix = ig * xa
        d_mult = g * ix
        d_ig = g * mult * xa
        d_xa = g * mult * ig
        d_la = da * a - d_mult * (a * a) / mult
        d_r = d_la * ((-LRU_C) * f["sp"])
        dsp = jnp.sum(d_la * ((-LRU_C) * r), axis=0, keepdims=True)
        dlam = dsp * (-_sigmoid(-lam_ref[...]))
        d_za = d_r * r * (1.0 - r)
        d_zx = d_ig * ig * (1.0 - ig)
        dzab = d_za.astype(BF16)
        dzxb = d_zx.astype(BF16)
        dwa_ref[...] += _dot_tn(f["xab"], dzab)
        dwx_ref[...] += _dot_tn(f["xab"], dzxb)
        d_xa = d_xa + _dot_nt(dzab, wa_ref[...]) + _dot_nt(dzxb, wx_ref[...])
        d_xp = sum(cw_ref[3 - j:4 - j, :] * _shift_up(d_xa, nxt8[...], j, row) for j in range(4))
        dcw = [jnp.sum(d_xa * _shift_down(xp, xprev8, 3 - k, row), axis=0, keepdims=True) for k in range(4)]
        dp_ref[:, 0:R] = d_xp.astype(BF16)
        dp_ref[:, R:2 * R] = d_ga.astype(BF16)
        dp8 = d_xa[0:8, :]
        nxt8[...] = dp8
        sm_ref[0:1, :] += jnp.sum(d_za, axis=0, keepdims=True)
        sm_ref[1:2, :] += jnp.sum(d_zx, axis=0, keepdims=True)
        sm_ref[2:3, :] += dlam
        sm_ref[3:4, :] += dnr
        sm_ref[4:5, :] += jnp.sum(d_xa, axis=0, keepdims=True)
        for k in range(4):
            sm_ref[8 + k:9 + k, :] += dcw[k]

        @pl.when(i == nt - 1)
        def _():
            for h in range(R // HEAD):
                dwab_ref[h] = dwa_ref[h * HEAD:(h + 1) * HEAD, h * HEAD:(h + 1) * HEAD].astype(BF16)
                dwxb_ref[h] = dwx_ref[h * HEAD:(h + 1) * HEAD, h * HEAD:(h + 1) * HEAD].astype(BF16)

    c0 = lambda shape: pl.BlockSpec(shape, lambda i: (0, 0))
    blocks = pl.BlockSpec((R // HEAD, HEAD, HEAD), lambda i: (0, 0, 0))
    rev = lambda i: nt - 1 - i
    prev8 = lambda i: (jnp.maximum((nt - 1 - i) * (RT // 8) - 1, 0), 0)
    return pl.pallas_call(
        body, name="rec_bwd", grid=(nt,),
        in_specs=[pl.BlockSpec(memory_space=pl.ANY),
                  pl.BlockSpec((RT, R), lambda i: (rev(i), 0)),
                  pl.BlockSpec((RT, 2 * R), lambda i: (rev(i), 0)), pl.BlockSpec((8, R), prev8),
                  pl.BlockSpec((RT, R), lambda i: (rev(i), 0)), pl.BlockSpec((8, R), prev8),
                  c0((4, R)), c0((1, R)), blocks, c0((1, R)), blocks, c0((1, R)), c0((1, R)), c0((1, R))],
        out_specs=[pl.BlockSpec((RT, 2 * R), lambda i: (rev(i), 0)), blocks, blocks, c0((16, R))],
        out_shape=[jax.ShapeDtypeStruct((S, E), BF16), jax.ShapeDtypeStruct((R // HEAD, HEAD, HEAD), BF16),
                   jax.ShapeDtypeStruct((R // HEAD, HEAD, HEAD), BF16), jax.ShapeDtypeStruct((16, R), F32)],
        scratch_shapes=[pltpu.VMEM((8, R), F32), pltpu.VMEM((8, R), F32), pltpu.VMEM((R, R), BF16),
                        pltpu.VMEM((R, R), BF16), pltpu.VMEM((R, R), F32), pltpu.VMEM((R, R), F32)],
        input_output_aliases={0: 0},
        compiler_params=_cp(("arbitrary",)),
    )(dproj, d_ya, proj, proj, h_all, h_all, conv_w, conv_b, wa_b, ba, wx_b, bx, lam, norm_rec)


NPAIR = R // LANES
QB, KB, VB, GB = 2 * R // LANES, 3 * R // LANES, 4 * R // LANES, 5 * R // LANES


def _rope_freq():
    half = HEAD // 2
    inv = np.float32(ROPE_THETA) ** (-(np.arange(half, dtype=np.float32) / np.float32(half)))
    return jnp.asarray(np.tile(inv.astype(np.float32), LANES // half)[None, :])


def _rot_half(x, first):
    return jnp.where(first, -pltpu.roll(x, LANES - HEAD // 2, 1), pltpu.roll(x, HEAD // 2, 1))


def _cos_sin(pos_ref, freq_ref):
    ang = pos_ref[...].astype(F32) * freq_ref[...]
    return jnp.cos(ang), jnp.sin(ang)


def _deint(src_ref, dst_ref, d):
    n = S // d
    for r in range(d):
        v = src_ref[pl.ds(r, n, stride=d), :] if d > 1 else src_ref[...]
        dst_ref[r * n:(r + 1) * n, :] = v.astype(dst_ref.dtype)


def _reint(src_ref, dst_ref, d, accumulate):
    n = S // d
    for r in range(d):
        idx = (pl.ds(r, n, stride=d), slice(None)) if d > 1 else (slice(None), slice(None))
        v = src_ref[r * n:(r + 1) * n, :]
        if accumulate:
            dst_ref[idx] = dst_ref[idx] + v
        else:
            dst_ref[idx] = v


def _deint_heads(src_ref, dst0, dst1, d):
    n = S // d
    hm0 = lax.broadcasted_iota(jnp.int32, (n, LANES), 1) < HEAD
    for r in range(d):
        v = src_ref[pl.ds(r, n, stride=d), :] if d > 1 else src_ref[...]
        dst0[r * n:(r + 1) * n, :] = jnp.where(hm0, v, 0.0).astype(BF16)
        dst1[r * n:(r + 1) * n, :] = jnp.where(hm0, 0.0, v).astype(BF16)


def _reint_prev(src_ref, dst_ref, d):
    n = S // d
    if n == BLK:
        return
    for r in range(d):
        idx = (pl.ds(r, n - BLK, stride=d), slice(None)) if d > 1 else (slice(0, n - BLK), slice(None))
        dst_ref[idx] = dst_ref[idx] + src_ref[r * n + BLK:(r + 1) * n, :]


def _pair_masks():
    qi = lax.broadcasted_iota(jnp.int32, (BLK, 2 * BLK), 0)
    ki = lax.broadcasted_iota(jnp.int32, (BLK, 2 * BLK), 1) & (BLK - 1)
    return ki <= qi, ki >= qi


def _two(ref0, ref1, st, axis):
    return jnp.concatenate([ref0[pl.ds(st, BLK), :], ref1[pl.ds(st, BLK), :]], axis=axis)


ATT_UNROLL = 4


def _att_fwd(proj, cos, sin, w_out):
    def body(q_ref, k_ref, v_ref, cos_ref, sin_ref, w_ref, att_ref, qr_ref, kr_ref, lse_ref, wbf_ref,
             qd, kd0, kd1, vd0, vd1, od, ld, on, ln, wbuf, *wsems):
        wg = _WeightGather(w_ref, wbuf, *wsems)
        pl.when(pl.program_id(0) == 0)(wg.start)
        pl.when(pl.program_id(0) == 1)(wg.forward)
        lane = lax.broadcasted_iota(jnp.int32, (S, LANES), 1)
        first = (lane & (HEAD // 2)) == 0
        cos, sin = cos_ref[...], sin_ref[...]
        q = q_ref[...]
        k = k_ref[...]
        qr_ref[...] = (q * cos + _rot_half(q, first) * sin) * (HEAD ** -0.5)
        kr_ref[...] = k * cos + _rot_half(k, first) * sin
        hm0 = lax.broadcasted_iota(jnp.int32, (BLK, LANES), 1) < HEAD
        top = lax.broadcasted_iota(jnp.int32, (2 * BLK, LANES), 0) < BLK
        ones2 = (top == (lax.broadcasted_iota(jnp.int32, (2 * BLK, LANES), 1) < HEAD)).astype(BF16)
        mc2, mp2 = _pair_masks()

        for pi, d in enumerate(PATTERNS):
            nb = S // d // BLK
            _deint(qr_ref, qd, d)
            _deint_heads(kr_ref, kd0, kd1, d)
            _deint_heads(v_ref, vd0, vd1, d)

            def blk(b, carry):
                st = pl.multiple_of(b * BLK, BLK)
                qb = qd[pl.ds(st, BLK), :]
                sc = jnp.where(mc2, _dot_nt(qb, _two(kd0, kd1, st, 0)), NEG)
                mx = sc
                if nb > 1:
                    stp = pl.multiple_of(jnp.maximum(b - 1, 0) * BLK, BLK)
                    mp = jnp.logical_and(mp2, lax.rem(b, nb) != 0)
                    sp = jnp.where(mp, _dot_nt(qb, _two(kd0, kd1, stp, 0)), NEG)
                    mx = jnp.maximum(sc, sp)
                m0 = jnp.max(mx[:, 0:BLK], axis=1, keepdims=True)
                m1 = jnp.max(mx[:, BLK:2 * BLK], axis=1, keepdims=True)
                mf = jnp.concatenate([jnp.broadcast_to(m0, (BLK, BLK)), jnp.broadcast_to(m1, (BLK, BLK))], axis=1)
                o = _dot(jnp.exp(sc - mf).astype(BF16), jnp.concatenate([_two(vd0, vd1, st, 0), ones2], axis=1))
                if nb > 1:
                    o = o + _dot(jnp.exp(sp - mf).astype(BF16), jnp.concatenate([_two(vd0, vd1, stp, 0), ones2], axis=1))
                l = o[:, LANES:2 * LANES]
                od[pl.ds(st, BLK), :] = o[:, 0:LANES] / l
                ld[pl.ds(st, BLK), :] = jnp.where(hm0, m0, m1) + jnp.log(l)
                return carry

            lax.fori_loop(0, S // BLK, blk, 0, unroll=ATT_UNROLL)
            _reint(od, on.at[pi], d, False)
            _reint(ld, ln.at[pi], d, False)

        l0, l1, l2 = ln[0], ln[1], ln[2]
        m = jnp.maximum(jnp.maximum(l0, l1), l2)
        e0, e1, e2 = jnp.exp(l0 - m), jnp.exp(l1 - m), jnp.exp(l2 - m)
        den = e0 + e1 + e2
        att_ref[...] = (e0 * on[0] + e1 * on[1] + e2 * on[2]) / den
        lse_ref[...] = m + jnp.log(den)

        @pl.when(pl.program_id(0) == NPAIR - 1)
        def _():
            wg.finish()
            wbf_ref[...] = wbuf[...]

    col = lambda c0: pl.BlockSpec((S, LANES), lambda p: (0, c0 + p))
    out = pl.BlockSpec((S, LANES), lambda p: (0, p))
    tab = pl.BlockSpec((S, LANES), lambda p: (0, 0))
    vm = pl.BlockSpec(memory_space=pltpu.VMEM)
    return pl.pallas_call(
        body, name="att_fwd", grid=(NPAIR,),
        in_specs=[col(QB), col(KB), col(VB), tab, tab, vm],
        out_specs=[out, out, out, out, vm],
        out_shape=[jax.ShapeDtypeStruct((S, R), F32)] * 4 + [jax.ShapeDtypeStruct((NCHIP,) + w_out.shape, BF16)],
        scratch_shapes=[pltpu.VMEM((S, LANES), BF16)] * 5 + [pltpu.VMEM((S, LANES), F32)] * 2
        + [pltpu.VMEM((3, S, LANES), F32)] * 2 + [pltpu.VMEM((NCHIP,) + w_out.shape, BF16)] + _WeightGather.SEMS,
        compiler_params=_cp(("arbitrary",)),
    )(proj, proj, proj, cos, sin, w_out)


def _att_bwd(dproj, d_att, att, lse, qr, kr, proj, cos, sin, gw_out4):
    out_units = [(j, j, 0) for j in range(NCHIP)]

    nblk = S // BLK

    def body(dp_in, do_ref, o_ref, lse_ref, qr_ref, kr_ref, v_ref, cos_ref, sin_ref, gw_ref, dp_ref, gout_ref,
             qd, kd0, kd1, vd0, vd1, dod, kt, packn, packd, dqd, dkcd, dkpd, dvcd, dvpd,
             dqn, dkn, dvn, rows, trs, stage, sems, gred, *rs_scratch):
        p = pl.program_id(0)
        rs = _ReduceScatter(gw_ref, gred, out_units, *rs_scratch)
        for step, piece in enumerate((rs.start_halves, rs.send_partials, rs.reduce_owned)):
            pl.when(p == step)(piece)

        @pl.when(p == NPAIR - 1)
        def _():
            rs.finish()
            gout_ref[...] = gred[...]

        lane = lax.broadcasted_iota(jnp.int32, (S, LANES), 1)
        hms = lane < HEAD
        prod = do_ref[...] * o_ref[...]
        d0 = jnp.sum(jnp.where(hms, prod, 0.0), axis=1, keepdims=True)
        d1 = jnp.sum(jnp.where(hms, 0.0, prod), axis=1, keepdims=True)
        lse = lse_ref[...]
        quarter = HEAD // 2
        packn[...] = jnp.where(lane < quarter, lse,
                               jnp.where(hms, pltpu.roll(lse, LANES - quarter, 1), jnp.where(lane < 3 * quarter, d0, d1)))
        dqn[...] = jnp.zeros_like(dqn)
        dkn[...] = jnp.zeros_like(dkn)
        dvn[...] = jnp.zeros_like(dvn)
        hm0 = lax.broadcasted_iota(jnp.int32, (BLK, LANES), 1) < HEAD
        key = lax.broadcasted_iota(jnp.int32, (2 * BLK, BLK), 0) & (BLK - 1)
        qry = lax.broadcasted_iota(jnp.int32, (2 * BLK, BLK), 1)
        mct, mpt = key <= qry, key >= qry

        for d in PATTERNS:
            nb = S // d // BLK
            _deint(qr_ref, qd, d)
            _deint_heads(kr_ref, kd0, kd1, d)
            _deint_heads(v_ref, vd0, vd1, d)
            _deint(do_ref, dod, d)
            _deint(packn, packd, d)

            def blk(b, carry):
                st = pl.multiple_of(b * BLK, BLK)
                kt[b] = _two(kd0, kd1, st, 0).astype(F32).T.astype(BF16)
                trs[b] = packd[pl.ds(st, BLK), :].T
                for j in range(4):
                    rows[b, j:j + 1, :] = trs[b, j * quarter:j * quarter + 1, :]
                qb, dob = qd[pl.ds(st, BLK), :], dod[pl.ds(st, BLK), :]
                both = lambda j: jnp.concatenate([jnp.broadcast_to(rows[b, j:j + 1, :], (BLK, BLK)),
                                                  jnp.broadcast_to(rows[b, j + 1:j + 2, :], (BLK, BLK))], axis=0)
                lbt, dlt = both(0), both(2)

                def side(bk, mask):
                    stk = pl.multiple_of(bk * BLK, BLK)
                    k2, v2 = _two(kd0, kd1, stk, 0), _two(vd0, vd1, stk, 0)
                    pt = jnp.where(mask, jnp.exp(_dot_nt(k2, qb) - lbt), 0.0)
                    dst = (pt * (_dot_nt(v2, dob) - dlt)).astype(BF16)
                    rk, rv = _dot(dst, qb), _dot(pt.astype(BF16), dob)
                    return (_dot(kt[bk], dst), jnp.where(hm0, rk[0:BLK], rk[BLK:2 * BLK]),
                            jnp.where(hm0, rv[0:BLK], rv[BLK:2 * BLK]))

                dq_t, dkc, dvc = side(b, mct)
                if nb > 1:
                    dqp_t, dkp, dvp = side(jnp.maximum(b - 1, 0), jnp.logical_and(mpt, lax.rem(b, nb) != 0))
                    dq_t = dq_t + dqp_t
                    dkpd[pl.ds(st, BLK), :] = dkp
                    dvpd[pl.ds(st, BLK), :] = dvp
                dqd[pl.ds(st, BLK), :] = dq_t.T
                dkcd[pl.ds(st, BLK), :] = dkc
                dvcd[pl.ds(st, BLK), :] = dvc
                return carry

            lax.fori_loop(0, nblk, blk, 0, unroll=2 * ATT_UNROLL)
            _reint(dqd, dqn, d, True)
            _reint(dkcd, dkn, d, True)
            _reint(dvcd, dvn, d, True)
            _reint_prev(dkpd, dkn, d)
            _reint_prev(dvpd, dvn, d)

        lane = lax.broadcasted_iota(jnp.int32, (S, LANES), 1)
        first = (lane & (HEAD // 2)) == 0
        cos, sin = cos_ref[...], sin_ref[...]
        dq = dqn[...] * (HEAD ** -0.5)
        dk = dkn[...]
        stage[0] = (dq * cos - _rot_half(dq, first) * sin).astype(BF16)
        stage[1] = (dk * cos - _rot_half(dk, first) * sin).astype(BF16)
        stage[2] = dvn[...].astype(BF16)
        copies = [pltpu.make_async_copy(stage.at[j], dp_ref.at[:, pl.ds((2 + j) * R + p * LANES, LANES)], sems.at[j])
                  for j in range(3)]
        for cp in copies:
            cp.start()
        for cp in copies:
            cp.wait()

    blk = pl.BlockSpec((S, LANES), lambda p: (0, p))
    tab = pl.BlockSpec((S, LANES), lambda p: (0, 0))
    vm = pl.BlockSpec(memory_space=pltpu.VMEM)
    _, orows, ocols = gw_out4.shape
    return pl.pallas_call(
        body, name="att_bwd", grid=(NPAIR,),
        in_specs=[pl.BlockSpec(memory_space=pl.ANY), blk, blk, blk, blk, blk,
                  pl.BlockSpec((S, LANES), lambda p: (0, VB + p)), tab, tab, vm],
        out_specs=[pl.BlockSpec(memory_space=pl.ANY), vm],
        out_shape=[jax.ShapeDtypeStruct((S, E), BF16), jax.ShapeDtypeStruct((orows, ocols), F32)],
        scratch_shapes=[pltpu.VMEM((S, LANES), BF16)] * 6 + [pltpu.VMEM((nblk, LANES, 2 * BLK), BF16)]
        + [pltpu.VMEM((S, LANES), F32)] * 10
        + [pltpu.VMEM((nblk, 8, BLK), F32), pltpu.VMEM((nblk, LANES, BLK), F32)]
        + [pltpu.VMEM((3, S, LANES), BF16), pltpu.SemaphoreType.DMA((3,)), pltpu.VMEM((orows, ocols), F32)]
        + _ReduceScatter.scratch(NCHIP, orows, ocols, 1),
        input_output_aliases={0: 0},
        compiler_params=_cp(("arbitrary",)),
    )(dproj, d_att, att, lse, qr, kr, proj, cos, sin, gw_out4)


def _out_fwd_bwd(ya, att, proj, w_out_bf, x, target, mod, norm_post, norm_att):
    ts = 512

    def body(ya_ref, att_ref, gb_ref, w_ref, x_ref, t_ref, mod_ref, npost_ref, natt_ref,
             gx_ref, dya_ref, datt_ref, dgb_ref, gw_ref, acc_ref):
        i = pl.program_id(0)

        @pl.when(i == 0)
        def _():
            gw_ref[...] = jnp.zeros_like(gw_ref)
            acc_ref[...] = jnp.zeros_like(acc_ref)

        gate = mod_ref[:, 2 * D:3 * D]
        att = att_ref[...]
        gb = gb_ref[...]
        sg = _sigmoid(gb)
        silu = gb * sg
        ybp = att * silu
        yb, ybn, rstd_b = _rms_fwd(ybp, natt_ref[...])
        cat = jnp.concatenate([ya_ref[...], yb.astype(BF16)], axis=1)
        mix = _dot(cat, w_ref[...])
        rn, mn, rstd_m = _rms_fwd(mix, npost_ref[...])
        err = x_ref[...] + gate * rn - t_ref[...]
        dy = err * (1.0 / D)
        gx_ref[...] = dy
        dmix, dnpost = _rms_bwd(dy * gate, mn, rstd_m, npost_ref[...])
        dmb = dmix.astype(BF16)
        gw_ref[...] += _dot_tn(cat, dmb)
        dcat = _dot_nt(dmb, w_ref[...])
        dya_ref[...] = dcat[:, 0:R]
        dybp, dnatt = _rms_bwd(dcat[:, R:2 * R], ybn, rstd_b, natt_ref[...])
        datt_ref[...] = dybp * silu
        dgb_ref[...] = (dybp * att * (sg * (1.0 + gb * (1.0 - sg)))).astype(BF16)
        acc_ref[0:1, :] += jnp.sum(dy * rn, axis=0, keepdims=True)
        acc_ref[1:2, :] += dnpost
        acc_ref[2:3, 0:R] += dnatt
        acc_ref[3:4, :] += jnp.sum(jnp.sum(err * err, axis=1, keepdims=True), axis=0, keepdims=True)

    tile = lambda w: pl.BlockSpec((ts, w), lambda i: (i, 0))
    c0 = lambda shape: pl.BlockSpec(shape, lambda i: (0, 0))
    return pl.pallas_call(
        body, name="out_fwd_bwd", grid=(S // ts,),
        in_specs=[tile(R), tile(R), pl.BlockSpec((ts, R), lambda i: (i, 5)), c0((D, D)), tile(D), tile(D),
                  c0((1, 3 * D)), c0((1, D)), c0((1, R))],
        out_specs=[tile(D), tile(R), tile(R), pl.BlockSpec((ts, R), lambda i: (i, 5)), c0((D, D)), c0((8, D))],
        out_shape=[jax.ShapeDtypeStruct((S, D), F32), jax.ShapeDtypeStruct((S, R), F32),
                   jax.ShapeDtypeStruct((S, R), F32), jax.ShapeDtypeStruct((S, E), BF16),
                   jax.ShapeDtypeStruct((D, D), F32), jax.ShapeDtypeStruct((8, D), F32)],
        compiler_params=_cp(("arbitrary",)),
    )(ya, att, proj, w_out_bf, x, target, mod, norm_post, norm_att)


UC = 256
UPC = EC // UC


NU = E // UC


def _unit_of_step(i):
    return (i % NCHIP) * UPC + i // NCHIP


def _in_proj_bwd(ht, dproj, w_in_bf, x, gx1, mod, norm_pre, smalls):
    ts = 256
    nt = S // ts
    half = D // 2
    units = [_unit_of_step(k) for k in range(NU)]
    owners = [u // UPC for u in units]
    ns = len(smalls)

    def body(*refs):
        (ht_ref, dpu_ref, dp_ref, w_hbm, x_ref, gx1_ref, mod_ref, np_ref), refs = refs[:8], refs[8:]
        small_in, refs = refs[:ns], refs[ns:]
        (gx_ref, gin_ref), refs = refs[:2], refs[2:]
        small_out, (acc_out,), refs = refs[:ns], refs[ns:ns + 1], refs[ns + 1:]
        mine, sib, tmp, stage, got, red, acc_ref, hs, hr, ps, pr, bs, br = refs[:13]
        early = _SmallGather(small_in, small_out, *refs[13:16])
        late = _SmallGather([acc_ref], [acc_out], *refs[16:19])
        w_ref, w_sem = refs[19:21]
        i = pl.program_id(0)
        w_copy = pltpu.make_async_copy(w_hbm, w_ref, w_sem)
        pl.when(i == 0)(w_copy.start)
        pl.when(i == NU)(w_copy.wait)
        xx, yy, c = _me()
        ci = 2 * xx + yy
        r0 = pl.multiple_of(c * half, half)
        r1 = pl.multiple_of((1 - c) * half, half)
        pl.when(i == 0)(early.start)
        pl.when(i == NU)(early.forward)

        def exch(k):
            return _remote(tmp.at[k % 2], sib.at[k], hs.at[k], hr.at[k], 1)

        def partial(k, sender):
            return pltpu.make_async_remote_copy(
                src_ref=stage.at[k], dst_ref=got.at[units[k] % UPC, sender], send_sem=ps.at[k],
                recv_sem=pr.at[k, sender], device_id=(owners[k] // 2, owners[k] % 2, c), device_id_type=MESH)

        def back(k, start):
            off = (units[k] % UPC) * UC
            blk = red.at[pl.ds(start, half), off:off + UC]
            return _remote(blk, blk, bs.at[k], br.at[k], 1)

        for k in range(NU + 1):
            @pl.when(i == k)
            def _():
                if k < NU:
                    if k >= 2:
                        exch(k - 2).wait_send()
                    dpu = dpu_ref[...]
                    tmp[k % 2] = _dot(ht_ref[pl.ds(r1, half), :], dpu)
                    exch(k).start()
                    mine[k] = _dot(ht_ref[pl.ds(r0, half), :], dpu)
                if k >= 1:
                    exch(k - 1).wait_recv()
                    mine[k - 1] += sib[k - 1]

                    @pl.when(ci != owners[k - 1])
                    def _():
                        stage[k - 1] = mine[k - 1].astype(BF16)
                        partial(k - 1, ci).start()

        @pl.when(i == NU)
        def _():
            acc_ref[...] = jnp.zeros_like(acc_ref)

        @pl.when(i >= NU)
        def _():
            dh = sum(_dot_nt(dp_ref[:, j * EC:(j + 1) * EC], w_ref[j]) for j in range(NCHIP))
            hp, xn, rstd = _rms_fwd(x_ref[...], np_ref[...])
            dx, dnp = _rms_bwd(dh * (1.0 + mod_ref[:, D:2 * D]), xn, rstd, np_ref[...])
            gx_ref[...] = gx1_ref[...] + dx
            acc_ref[0:1, :] += jnp.sum(dh, axis=0, keepdims=True)
            acc_ref[1:2, :] += jnp.sum(dh * hp, axis=0, keepdims=True)
            acc_ref[2:3, :] += dnp

        for t in range(UPC):
            @pl.when(i == NU + 1 + 2 * t)
            def _():
                for k in range(NCHIP * t, NCHIP * (t + 1)):
                    @pl.when(ci == owners[k])
                    def _():
                        off = (units[k] % UPC) * UC
                        red[pl.ds(r0, half), off:off + UC] = mine[k]
                        for s in range(NCHIP):
                            if s != owners[k]:
                                partial(k, s).wait_recv()
                                red[pl.ds(r0, half), off:off + UC] += got[units[k] % UPC, s].astype(F32)
                        back(k, r0).start()

        @pl.when(i == NU + nt - 1)
        def _():
            late.start()
            exch(NU - 2).wait_send()
            exch(NU - 1).wait_send()
            for k in range(NU):
                @pl.when(ci == owners[k])
                def _():
                    back(k, r1).wait_recv()
                    back(k, r0).wait_send()

                @pl.when(ci != owners[k])
                def _():
                    partial(k, ci).wait_send()
            gin_ref[...] = red[...]
            early.finish()
            late.forward()
            late.finish()

    tile = lambda w: pl.BlockSpec((ts, w), lambda i: (jnp.maximum(i - NU, 0), 0))
    c0 = lambda shape: pl.BlockSpec(shape, lambda i: (0, 0))
    vm = pl.BlockSpec(memory_space=pltpu.VMEM)
    hbm = pl.BlockSpec(memory_space=pl.ANY)
    gathered = [jax.ShapeDtypeStruct((NDEV,) + a.shape, a.dtype) for a in smalls] + [jax.ShapeDtypeStruct((NDEV, 8, D), F32)]
    return pl.pallas_call(
        body, name="in_proj_bwd", grid=(NU + nt,),
        in_specs=[vm, pl.BlockSpec((S, UC), lambda i: (0, _unit_of_step(jnp.minimum(i, NU - 1)))), tile(E),
                  hbm, tile(D), tile(D), c0((1, 3 * D)), c0((1, D))] + [vm] * ns,
        out_specs=[tile(D), vm] + [hbm] * (ns + 1),
        out_shape=[jax.ShapeDtypeStruct((S, D), F32), jax.ShapeDtypeStruct((D, EC), F32)] + gathered,
        scratch_shapes=[pltpu.VMEM((NU, half, UC), F32), pltpu.VMEM((NU, half, UC), F32),
                        pltpu.VMEM((2, half, UC), F32), pltpu.VMEM((NU, half, UC), BF16),
                        pltpu.VMEM((UPC, NCHIP, half, UC), BF16), pltpu.VMEM((D, EC), F32), pltpu.VMEM((8, D), F32),
                        pltpu.SemaphoreType.DMA((NU,)), pltpu.SemaphoreType.DMA((NU,)),
                        pltpu.SemaphoreType.DMA((NU,)), pltpu.SemaphoreType.DMA((NU, NCHIP)),
                        pltpu.SemaphoreType.DMA((NU,)), pltpu.SemaphoreType.DMA((NU,))]
        + _SmallGather.sems(ns) + _SmallGather.sems(1)
        + [pltpu.VMEM((NCHIP, D, EC), BF16), pltpu.SemaphoreType.DMA],
        compiler_params=_cp(("arbitrary",)),
    )(ht, dproj, dproj, w_in_bf, x, gx1, mod, norm_pre, *smalls)


def _local_step(x, cos, sin, target, mod, w_in_bf, proj, ht, w_out, conv_w, p):
    rec_p = (conv_w, p["conv_b"], p["w_rg_a"], p["b_rg_a"], p["w_rg_x"], p["b_rg_x"], p["lru_lambda"], p["norm_rec"])
    h_all, ya = _rec_fwd(proj, *rec_p)
    att, qr, kr, lse, w_out_bf = _att_fwd(proj, cos, sin, w_out)
    gx1, d_ya, d_att, dproj, gw_out, acc_o = _out_fwd_bwd(ya, att, proj, w_out_bf.reshape(D, D), x, target, mod,
                                                           p["norm_post"], p["norm_att"])
    dproj, g_out = _att_bwd(dproj, d_att, att, lse, qr, kr, proj, cos, sin, gw_out.reshape(NCHIP, D // NCHIP, D))
    dproj, dwa, dwx, sm = _rec_bwd(dproj, d_ya, proj, h_all, *rec_p)
    grad_x, g_in, *gathered = _in_proj_bwd(ht, dproj, w_in_bf, x, gx1, mod, p["norm_pre"], [acc_o, sm, dwa, dwx])
    return grad_x, g_in, g_out, gathered


def _me():
    return lax.axis_index("x"), lax.axis_index("y"), lax.axis_index("c")


def _flip(v, bit):
    return 1 - v if bit else v


def _peer(rel):
    x, y, c = _me()
    return (_flip(x, rel & 4), _flip(y, rel & 2), _flip(c, rel & 1))


def _remote(src, dst, send_sem, recv_sem, rel):
    return pltpu.make_async_remote_copy(src_ref=src, dst_ref=dst, send_sem=send_sem, recv_sem=recv_sem,
                                        device_id=_peer(rel), device_id_type=MESH)


class _WeightGather:
    SEMS = [pltpu.SemaphoreType.DMA((NCHIP - 1,))] * 4

    def __init__(self, w_ref, out_ref, send_sems, recv_sems, fsend_sems, frecv_sems):
        x, y, c = _me()
        self.w, self.out, self.ci = w_ref, out_ref, 2 * x + y
        self.half = w_ref.shape[0] // 2
        self.r0 = pl.multiple_of(c * self.half, self.half)
        self.r1 = pl.multiple_of((1 - c) * self.half, self.half)
        self.sems = (send_sems, recv_sems, fsend_sems, frecv_sems)

    def _ici(self, chip, k):
        blk = self.out.at[chip, pl.ds(self.r0, self.half), :]
        return _remote(blk, blk, self.sems[0].at[k - 1], self.sems[1].at[k - 1], 2 * k)

    def _d2d(self, chip, start, k):
        blk = self.out.at[chip, pl.ds(start, self.half), :]
        return _remote(blk, blk, self.sems[2].at[k - 1], self.sems[3].at[k - 1], 1)

    def start(self, diagonal=True):
        self.out[self.ci] = self.w[...].astype(BF16)
        for k in range(1, NCHIP if diagonal else NCHIP - 1):
            self._ici(self.ci, k).start()

    def _relay(self, chip, piece, k):
        q = self.half // 2
        blk = self.out.at[chip, pl.ds(self.r0 + piece * q, q), :]
        return _remote(blk, blk, self.relay_sems[0].at[piece], self.relay_sems[1].at[piece], 2 * k)

    def neighbours_landed(self, relay_send_sems, relay_recv_sems):
        self.relay_sems = (relay_send_sems, relay_recv_sems)
        for k in (1, 2):
            self._ici(self.ci ^ k, k).wait_recv()
        self._relay(self.ci ^ 2, 0, 1).start()
        self._relay(self.ci ^ 1, 1, 2).start()
        for k in (1, 2):
            self._d2d(self.ci ^ k, self.r0, k).start()

    def sibling_landed(self, k):
        self._d2d(self.ci ^ k, self.r1, k).wait_recv()

    def diagonal_landed(self):
        for piece, k in ((0, 1), (1, 2)):
            self._relay(self.ci ^ 3, piece, k).wait_recv()
        self._d2d(self.ci ^ 3, self.r0, 3).start()
        self._d2d(self.ci ^ 3, self.r1, 3).wait_recv()

    def finish_relayed(self):
        for k in (1, 2):
            self._ici(self.ci, k).wait_send()
        self._relay(self.ci ^ 2, 0, 1).wait_send()
        self._relay(self.ci ^ 1, 1, 2).wait_send()
        for k in range(1, NCHIP):
            self._d2d(self.ci ^ k, self.r0, k).wait_send()

    def forward(self):
        for k in range(1, NCHIP):
            self._ici(self.ci ^ k, k).wait_recv()
            self._d2d(self.ci ^ k, self.r0, k).start()

    def finish(self):
        for k in range(1, NCHIP):
            self._d2d(self.ci ^ k, self.r1, k).wait_recv()
        self.finish_sends()

    def finish_sends(self):
        for k in range(1, NCHIP):
            self._ici(self.ci, k).wait_send()
            self._d2d(self.ci ^ k, self.r0, k).wait_send()


class _SmallGather:
    @staticmethod
    def sems(n):
        return [pltpu.SemaphoreType.DMA((n, 7)), pltpu.SemaphoreType.DMA((n, 7)), pltpu.SemaphoreType.DMA((n,))]

    def __init__(self, srcs, outs, send_sems, recv_sems, local_sems):
        x, y, c = _me()
        self.srcs, self.outs = list(srcs), list(outs)
        self.ss, self.rs, self.ls = send_sems, recv_sems, local_sems
        self.ci, self.c = 2 * x + y, c
        self.me = 2 * self.ci + c

    def _own(self, a, slot, rel):
        return _remote(self.srcs[a], self.outs[a].at[self.me], self.ss.at[a, slot], self.rs.at[a, slot], rel)

    def _block(self, a, idx, slot, rel):
        blk = self.outs[a].at[idx]
        return _remote(blk, blk, self.ss.at[a, slot], self.rs.at[a, slot], rel)

    def _local(self, a):
        return pltpu.make_async_copy(self.srcs[a], self.outs[a].at[self.me], self.ls.at[a])

    def start(self):
        for a in range(len(self.srcs)):
            self._local(a).start()
            self._own(a, 0, 1).start()
            for k in range(1, NCHIP):
                self._own(a, k, 2 * k).start()

    def forward(self):
        for a in range(len(self.srcs)):
            for k in range(1, NCHIP):
                idx = 2 * (self.ci ^ k) + self.c
                self._block(a, idx, k, 2 * k).wait_recv()
                self._block(a, idx, 3 + k, 1).start()

    def finish(self):
        for a in range(len(self.srcs)):
            self._block(a, 2 * self.ci + 1 - self.c, 0, 1).wait_recv()
            for k in range(1, NCHIP):
                self._block(a, 2 * (self.ci ^ k) + 1 - self.c, 3 + k, 1).wait_recv()
            self._own(a, 0, 1).wait_send()
            for k in range(1, NCHIP):
                self._own(a, k, 2 * k).wait_send()
                self._block(a, 2 * (self.ci ^ k) + self.c, 3 + k, 1).wait_send()
            self._local(a).wait()


def _start_in_proj(crow, w_ada, b_cols, w_in, pos, x, norm_pre, order):
    ts = 512
    nt = S // ts
    wc = crow.shape[1]

    def body(order_ref, crow_ref, wada_ref, b_ref, win_ref, pos_ref, freq_ref, x_ref, np_ref,
             g0_ref, mod_ref, wbf_ref, cos_ref, sin_ref, proj_ref, ht_ref,
             g0s, modp, modb, wbuf, hb_all, cs, cr, ms, mr, ws, wr, fs, fr, local_sems, ys, yr, osem):
        s, t = pl.program_id(0), pl.program_id(1)
        x, y, c = _me()
        ci = 2 * x + y
        me = 2 * ci + c
        wg = _WeightGather(win_ref, wbuf, ws, wr, fs, fr)

        @pl.when(jnp.logical_and(s == 0, t == 0))
        def _():
            wg.start(diagonal=False)
            mine = pltpu.make_async_copy(crow_ref, g0s.at[pl.ds(me, 1), :], local_sems.at[0])
            mine.start()
            csend = [_remote(crow_ref, g0s.at[pl.ds(me, 1), :], cs.at[r - 1], cr.at[r - 1], r) for r in range(1, NDEV)]
            for cp in csend:
                cp.start()
            cos_ref[...], sin_ref[...] = _cos_sin(pos_ref, freq_ref)
            for r in range(1, NDEV):
                px, py, pc = _peer(r)
                _remote(crow_ref, g0s.at[pl.ds(4 * px + 2 * py + pc, 1), :], cs.at[r - 1], cr.at[r - 1], r).wait_recv()
            mine.wait()
            cv = g0s[:, 0:D]
            sc = cv * _sigmoid(cv)
            scb = jnp.concatenate([sc, jnp.zeros_like(sc)], axis=0).astype(BF16)
            modp[...] = _dot(scb, wada_ref[...].astype(BF16))[0:NDEV, :] + b_ref[...]
            own = pltpu.make_async_copy(modp.at[pl.ds(me, 1), :], modb.at[ci], local_sems.at[1])
            own.start()
            msend = []
            for k in range(1, NCHIP):
                cp = _remote(modp.at[pl.ds(2 * (ci ^ k) + c, 1), :], modb.at[ci], ms.at[k - 1], mr.at[k - 1], 2 * k)
                cp.start()
                msend.append(cp)
            for k in range(1, NCHIP):
                _remote(modp.at[pl.ds(me, 1), :], modb.at[ci ^ k], ms.at[k - 1], mr.at[k - 1], 2 * k).wait_recv()
            own.wait()
            for j in range(NCHIP):
                mod_ref[:, j * EC:(j + 1) * EC] = modb[j]
            for cp in csend + msend:
                cp.wait_send()
            g0_ref[...] = g0s[...]

        def keep(k):
            return pltpu.make_async_copy(wbuf.at[ci ^ k], wbf_ref.at[ci ^ k], osem.at[k])

        @pl.when(jnp.logical_and(s == 1, t == 0))
        def _():
            keep(0).start()
            wg.neighbours_landed(ys, yr)
            wg.sibling_landed(1)
            keep(1).start()

        @pl.when(jnp.logical_and(s == 2, t == 0))
        def _():
            wg.sibling_landed(2)
            keep(2).start()

        @pl.when(jnp.logical_and(s == 3, t == 0))
        def _():
            wg.relay_sems = (ys, yr)
            wg.diagonal_landed()
            keep(3).start()

        rows = pl.ds(pl.multiple_of(t * ts, ts), ts)

        @pl.when(s == 0)
        def _():
            hp, _, _ = _rms_fwd(x_ref[...], np_ref[...])
            h = hp * (1.0 + mod_ref[:, D:2 * D]) + mod_ref[:, 0:D]
            hb_all[rows, :] = h.astype(BF16)
            ht_ref[...] = h.T.astype(BF16)

        proj_ref[...] = _dot(hb_all[rows, :], wbuf[ci ^ s])

        @pl.when(jnp.logical_and(s == NCHIP - 1, t == nt - 1))
        def _():
            wg.relay_sems = (ys, yr)
            wg.finish_relayed()
            for k in range(NCHIP):
                keep(k).wait()

    vm = pl.BlockSpec(memory_space=pltpu.VMEM)
    first_pass = lambda s, t: jnp.where(s == 0, t, nt - 1)
    grid_spec = pltpu.PrefetchScalarGridSpec(
        num_scalar_prefetch=1, grid=(NCHIP, nt),
        in_specs=[vm, vm, vm, vm, vm, vm, pl.BlockSpec((ts, D), lambda s, t, o: (first_pass(s, t), 0)),
                  pl.BlockSpec((1, D), lambda s, t, o: (0, 0))],
        out_specs=[vm, vm, pl.BlockSpec(memory_space=pl.ANY), vm, vm, pl.BlockSpec((ts, EC), lambda s, t, o: (t, o[s])),
                   pl.BlockSpec((D, ts), lambda s, t, o: (0, first_pass(s, t)))],
        scratch_shapes=[pltpu.VMEM((NDEV, wc), F32), pltpu.VMEM((NDEV, EC), F32), pltpu.VMEM((NCHIP, 1, EC), F32),
                        pltpu.VMEM((NCHIP, D, EC), BF16), pltpu.VMEM((S, D), BF16),
                        pltpu.SemaphoreType.DMA((NDEV - 1,)), pltpu.SemaphoreType.DMA((NDEV - 1,)),
                        pltpu.SemaphoreType.DMA((NCHIP - 1,)), pltpu.SemaphoreType.DMA((NCHIP - 1,))]
        + _WeightGather.SEMS + [pltpu.SemaphoreType.DMA((2,))] * 3 + [pltpu.SemaphoreType.DMA((NCHIP,))])
    return pl.pallas_call(
        body, name="start_in_proj", grid_spec=grid_spec,
        out_shape=[jax.ShapeDtypeStruct((NDEV, wc), F32), jax.ShapeDtypeStruct((1, 3 * D), F32),
                   jax.ShapeDtypeStruct((NCHIP, D, EC), BF16), jax.ShapeDtypeStruct((S, LANES), F32),
                   jax.ShapeDtypeStruct((S, LANES), F32), jax.ShapeDtypeStruct((S, E), F32),
                   jax.ShapeDtypeStruct((D, S), BF16)],
        compiler_params=_cp(("arbitrary", "arbitrary")),
    )(order, crow, w_ada, b_cols, w_in, pos, _rope_freq(), x, norm_pre)


class _ReduceScatter:
    @staticmethod
    def scratch(n_units, rows, ucols, max_owned):
        half = rows // 2
        return [pltpu.VMEM((n_units, half, ucols), F32), pltpu.VMEM((n_units, half, ucols), BF16),
                pltpu.VMEM((max_owned, NCHIP, half, ucols), BF16),
                pltpu.SemaphoreType.DMA((2,)), pltpu.SemaphoreType.DMA((n_units,)),
                pltpu.SemaphoreType.DMA((n_units, NCHIP)), pltpu.SemaphoreType.DMA((n_units,)),
                pltpu.SemaphoreType.DMA((n_units,))]

    def __init__(self, g_ref, out_ref, units, sib, stage, got, sem1, send2, recv2, send3, recv3):
        x, y, c = _me()
        self.c, self.ci = c, 2 * x + y
        self.g, self.out, self.units = g_ref, out_ref, units
        self.sib, self.stage, self.got = sib, stage, got
        self.sem1, self.send2, self.recv2, self.send3, self.recv3 = sem1, send2, recv2, send3, recv3
        self.half = g_ref.shape[1] // 2
        self.ucols = g_ref.shape[2]
        self.r0 = pl.multiple_of(c * self.half, self.half)
        self.r1 = pl.multiple_of((1 - c) * self.half, self.half)
        self.slot0 = units[0][0]
        assert [u[0] for u in units] == list(range(self.slot0, self.slot0 + len(units)))
        seen = {}
        self.local = []
        for _, owner, _ in units:
            self.local.append(seen.get(owner, 0))
            seen[owner] = seen.get(owner, 0) + 1

    def _halves(self):
        n = len(self.units)
        return _remote(self.g.at[pl.ds(self.slot0, n), pl.ds(self.r1, self.half), :], self.sib,
                       self.sem1.at[0], self.sem1.at[1], 1)

    def _partial(self, i, sender):
        _, owner, _ = self.units[i]
        return pltpu.make_async_remote_copy(
            src_ref=self.stage.at[i], dst_ref=self.got.at[self.local[i], sender],
            send_sem=self.send2.at[i], recv_sem=self.recv2.at[i, sender],
            device_id=(owner // 2, owner % 2, self.c), device_id_type=MESH)

    def _back(self, i, start):
        off = self.units[i][2]
        blk = self.out.at[pl.ds(start, self.half), off:off + self.ucols]
        return _remote(blk, blk, self.send3.at[i], self.recv3.at[i], 1)

    def start_halves(self):
        self._halves().start()

    def send_partials(self):
        self._halves().wait_recv()
        for i, (slot, owner, _) in enumerate(self.units):
            @pl.when(self.ci != owner)
            def _():
                self.stage[i] = (self.g[slot, pl.ds(self.r0, self.half), :] + self.sib[i]).astype(BF16)
                self._partial(i, self.ci).start()

    def reduce_owned(self):
        for i, (slot, owner, off) in enumerate(self.units):
            @pl.when(self.ci == owner)
            def _():
                rows, cols = pl.ds(self.r0, self.half), slice(off, off + self.ucols)
                self.out[rows, cols] = self.g[slot, pl.ds(self.r0, self.half), :] + self.sib[i]
                for s in range(NCHIP):
                    if s != owner:
                        self._partial(i, s).wait_recv()
                        self.out[rows, cols] += self.got[self.local[i], s].astype(F32)
                self._back(i, self.r0).start()

    def finish(self):
        self._halves().wait_send()
        for i, (_, owner, _) in enumerate(self.units):
            @pl.when(self.ci == owner)
            def _():
                self._back(i, self.r1).wait_recv()
                self._back(i, self.r0).wait_send()

            @pl.when(self.ci != owner)
            def _():
                self._partial(i, self.ci).wait_send()


def _silu_rows(c_ref):
    cv = c_ref[...]
    sc = cv * _sigmoid(cv)
    return jnp.concatenate([sc, jnp.zeros_like(sc)], axis=0).astype(BF16)


def _adamw(w, g, m, v, name):
    rows, cols = w.shape
    tr = 256 if rows % 256 == 0 else rows

    def body(w_ref, g_ref, m_ref, v_ref, d_ref, nm_ref, nv_ref):
        d_ref[...], nm_ref[...], nv_ref[...] = _adamw_values(w_ref[...], g_ref[...], m_ref[...], v_ref[...])

    spec = pl.BlockSpec((tr, cols), lambda i: (i, 0))
    return pl.pallas_call(
        body, name=name, grid=(rows // tr,), in_specs=[spec] * 4, out_specs=[spec] * 3,
        out_shape=[jax.ShapeDtypeStruct((rows, cols), F32)] * 3,
        compiler_params=_cp(("parallel",)),
    )(w, g, m, v)


def _adamw_values(w, g, m, v):
    nm = B1 * m + (1.0 - B1) * g
    nv = B2 * v + (1.0 - B2) * (g * g)
    m_hat = nm / (1.0 - B1 ** STEP)
    v_hat = nv / (1.0 - B2 ** STEP)
    return (-LR) * (m_hat / (jnp.sqrt(v_hat) + ADAM_EPS) + WD * w), nm, nv


NB = R // HEAD
SMALL = (("b_ada", (1, 3 * D)), ("norm_pre", (1, D)), ("norm_post", (1, D)), ("conv_w", (4, R // NCHIP)),
         ("conv_b", (1, R)), ("w_rg_a", (NB, HEAD, HEAD)), ("b_rg_a", (1, R)), ("w_rg_x", (NB, HEAD, HEAD)),
         ("b_rg_x", (1, R)), ("lru_lambda", (1, R)), ("norm_rec", (1, R)), ("norm_att", (1, R)))


def _small_update(ao8, sm8, dwa8, dwx8, ai8, cg, params):
    n = len(SMALL)

    def body(ao_ref, sm_ref, dwa_ref, dwx_ref, ai_ref, cg_ref, *refs):
        pin, pout, (gada_ref, loss_ref, dmod) = refs[:3 * n], refs[3 * n:7 * n], refs[7 * n:]
        xx, yy, _ = _me()
        ci = 2 * xx + yy

        def total(ref, *idx):
            acc = ref[(0,) + idx].astype(F32)
            for d in range(1, NDEV):
                acc = acc + ref[(d,) + idx].astype(F32)
            return acc

        row = lambda ref, r, lanes=slice(None): total(ref, slice(r, r + 1), lanes)
        mine = lambda parts: sum(jnp.where(ci == j, part, 0.0) for j, part in enumerate(parts))
        cw = R // NCHIP
        grads = {
            "b_ada": [jnp.concatenate([row(ai_ref, 0), row(ai_ref, 1), row(ao_ref, 0)], axis=1)],
            "norm_pre": [row(ai_ref, 2)], "norm_post": [row(ao_ref, 1)],
            "conv_w": [mine([row(sm_ref, 8 + r, slice(j * cw, (j + 1) * cw)) for j in range(NCHIP)]) for r in range(4)],
            "conv_b": [row(sm_ref, 4)], "b_rg_a": [row(sm_ref, 0)], "b_rg_x": [row(sm_ref, 1)],
            "lru_lambda": [row(sm_ref, 2)], "norm_rec": [row(sm_ref, 3)], "norm_att": [row(ao_ref, 2, slice(0, R))],
            "w_rg_a": [total(dwa_ref, h) for h in range(NB)], "w_rg_x": [total(dwx_ref, h) for h in range(NB)],
        }
        loss_ref[...] = row(ao_ref, 3, slice(0, LANES)) * (0.5 / D)
        for k, (name, shape) in enumerate(SMALL):
            w_ref, m_ref, v_ref = pin[3 * k:3 * k + 3]
            outs = pout[4 * k:4 * k + 4]
            for r, g in enumerate(grads[name]):
                at = (slice(None),) if len(grads[name]) == 1 else ((r,) if len(shape) == 3 else (slice(r, r + 1),))
                res = (g,) + _adamw_values(w_ref[at], g, m_ref[at], v_ref[at])
                for o_ref, val in zip(outs, res):
                    o_ref[at] = val
        for d in range(NDEV):
            dmod[d:d + 1, :] = jnp.concatenate([ai_ref[d, 0:1, :], ai_ref[d, 1:2, :], ao_ref[d, 0:1, :]], axis=1)
        cols = mine([dmod[:, j * EC:(j + 1) * EC] for j in range(NCHIP)])
        colsb = jnp.concatenate([cols, jnp.zeros_like(cols)], axis=0).astype(BF16)
        gada_ref[...] = _dot_tn(_silu_rows(cg_ref), colsb)

    shapes = [jax.ShapeDtypeStruct(s, F32) for _, s in SMALL]
    outs = pl.pallas_call(
        body, name="small_update",
        out_shape=[s for s in shapes for _ in range(4)] + [jax.ShapeDtypeStruct((D, EC), F32),
                                                           jax.ShapeDtypeStruct((1, LANES), F32)],
        scratch_shapes=[pltpu.VMEM((NDEV, 3 * D), F32)],
        compiler_params=_cp(),
    )(ao8, sm8, dwa8, dwx8, ai8, cg, *params)
    return outs[:4 * n], outs[4 * n], outs[4 * n + 1]


BIG = ("w_ada", "w_in", "w_out")
WEIGHTS = ("w_ada", "b_ada", "norm_pre", "norm_post", "w_in", "conv_w", "conv_b", "w_rg_a", "b_rg_a", "w_rg_x",
           "b_rg_x", "lru_lambda", "norm_rec", "norm_att", "w_out")


def kernel(x, c, positions, w_ada, b_ada, norm_pre, norm_post, w_in, conv_w, conv_b, w_rg_a, b_rg_a, w_rg_x, b_rg_x, lru_lambda, norm_rec, norm_att, w_out, loss_target, m_w_ada, m_b_ada, m_norm_pre, m_norm_post, m_w_in, m_conv_w, m_conv_b, m_w_rg_a, m_b_rg_a, m_w_rg_x, m_b_rg_x, m_lru_lambda, m_norm_rec, m_norm_att, m_w_out, v_w_ada, v_b_ada, v_norm_pre, v_norm_post, v_w_in, v_conv_w, v_conv_b, v_w_rg_a, v_b_rg_a, v_w_rg_x, v_b_rg_x, v_lru_lambda, v_norm_rec, v_norm_att, v_w_out):
    given = dict(locals())
    wts = {n: given[n] for n in WEIGHTS}
    ms = {n: given["m_" + n] for n in WEIGHTS}
    vs = {n: given["v_" + n] for n in WEIGHTS}
    xi, yi, _ = _me()
    chip = 2 * xi + yi
    cw_loc = R // NCHIP

    b_cols = lax.dynamic_slice(b_ada, (0, chip * EC), (1, EC))
    order = (chip ^ jnp.arange(NCHIP, dtype=jnp.int32)).astype(jnp.int32)
    g0, mod, w_in_bf, cos, sin, proj, ht = _start_in_proj(
        jnp.concatenate([c, conv_w.reshape(1, 4 * cw_loc)], axis=1), w_ada[0], b_cols, w_in[0],
        positions.reshape(S, 1), x[0], norm_pre, order)
    cg = g0[:, 0:D]
    conv_full = g0[0::2, D:].reshape(NCHIP, 4, cw_loc).transpose(1, 0, 2).reshape(4, R)

    p = dict(norm_pre=norm_pre, norm_post=norm_post, conv_b=conv_b, b_rg_a=b_rg_a, b_rg_x=b_rg_x,
             lru_lambda=lru_lambda, norm_rec=norm_rec, norm_att=norm_att, w_rg_a=w_rg_a[0], w_rg_x=w_rg_x[0])
    grad_x, g_in, g_out, gathered = _local_step(
        x[0], cos, sin, loss_target[0], mod, w_in_bf, proj, ht, w_out[0], conv_full, p)

    params = [d[n].reshape(shape) for n, shape in SMALL for d in (wts, ms, vs)]
    small_out, g_ada, loss_row = _small_update(*gathered, cg, params)
    grads = {"w_out": g_out, "w_in": g_in, "w_ada": g_ada}
    delta, new_m, new_v = {}, {}, {}
    for k, (n, _) in enumerate(SMALL):
        grads[n], delta[n], new_m[n], new_v[n] = small_out[4 * k:4 * k + 4]
    for n in BIG:
        delta[n], new_m[n], new_v[n] = _adamw(wts[n][0], grads[n], ms[n][0], vs[n][0], "adamw_" + n)
    out = lambda d: [d[n].reshape(wts[n].shape) for n in WEIGHTS]
    return (loss_row[0, 0], grad_x.reshape(x.shape), *out(grads), *out(delta), *out(new_m), *out(new_v))
```

```python
import numpy as np
import jax
import jax.numpy as jnp
from jax import lax
from jax.experimental import pallas as pl
from jax.experimental.pallas import tpu as pltpu

F32 = jnp.float32
BF16 = jnp.bfloat16

S = 2048
D = 1024
E = 3072
R = 512
NDEV = 8
NCHIP = 4
EC = 768
LRU_C = 8.0
EPS = 1e-6
NEG = -1e30
HEAD = 64
BLK = 128
PATTERNS = (1, 4, 16)
ROPE_THETA = 10000.0
LANES = 128
VMEM_LIMIT = 56 * 1024 * 1024

B1, B2, LR, WD, ADAM_EPS, STEP = 0.9, 0.999, 0.001, 0.01, 1e-8, 10
MESH = pl.DeviceIdType.MESH


def _cp(sem=None, **kw):
    return pltpu.CompilerParams(dimension_semantics=sem, vmem_limit_bytes=VMEM_LIMIT, **kw)


def _dot(a, b):
    return jnp.dot(a, b, preferred_element_type=F32)


def _dot_nt(a, b):
    return lax.dot_general(a, b, (((1,), (1,)), ((), ())), preferred_element_type=F32)


def _dot_tn(a, b):
    return lax.dot_general(a, b, (((0,), (0,)), ((), ())), preferred_element_type=F32)


def _sigmoid(x):
    return 1.0 / (1.0 + jnp.exp(-x))


def _expm1(x):
    poly = x * (1.0 + x * (0.5 + x * (1.0 / 6 + x * (1.0 / 24 + x * (1.0 / 120 + x * (1.0 / 720))))))
    return jnp.where(jnp.abs(x) < 0.3, poly, jnp.exp(x) - 1.0)


def _rms_fwd(v, g):
    rstd = lax.rsqrt(jnp.mean(v * v, axis=-1, keepdims=True) + EPS)
    vn = v * rstd
    return vn * g, vn, rstd


def _rms_bwd(dy, vn, rstd, g):
    dvn = dy * g
    dv = rstd * (dvn - vn * jnp.mean(dvn * vn, axis=-1, keepdims=True))
    return dv, jnp.sum(dy * vn, axis=0, keepdims=True)


RT = 256


def _shift_down(cur, prev8, j, row):
    if j == 0:
        return cur
    top = jnp.tile(pltpu.roll(prev8, j, 0), (RT // 8, 1))
    return jnp.where(row >= j, pltpu.roll(cur, j, 0), top)


def _shift_up(cur, next8, j, row):
    if j == 0:
        return cur
    bot = jnp.tile(pltpu.roll(next8, 8 - j, 0), (RT // 8, 1))
    return jnp.where(row < RT - j, pltpu.roll(cur, RT - j, 0), bot)


def _rec_gates(xp, xprev8, row, cw_ref, cb_ref, wa_ref, ba_ref, wx_ref, bx_ref, lam_ref):
    xa = cb_ref[...] + sum(cw_ref[3 - j:4 - j, :] * _shift_down(xp, xprev8, j, row) for j in range(4))
    xab = xa.astype(BF16)
    r = _sigmoid(_dot(xab, wa_ref[...]) + ba_ref[...])
    ig = _sigmoid(_dot(xab, wx_ref[...]) + bx_ref[...])
    nl = -lam_ref[...]
    sp = jnp.maximum(nl, 0.0) + jnp.log1p(jnp.exp(-jnp.abs(nl)))
    la = (-LRU_C) * r * sp
    a = jnp.exp(la)
    mult = jnp.sqrt(-_expm1(2.0 * la))
    return dict(xa=xa, xab=xab, r=r, ig=ig, sp=sp, la=la, a=a, mult=mult)


def _scan_fwd(a, u, row):
    sh = 1
    while sh < RT:
        a_s = jnp.where(row >= sh, pltpu.roll(a, sh, 0), 1.0)
        u_s = jnp.where(row >= sh, pltpu.roll(u, sh, 0), 0.0)
        u = a * u_s + u
        a = a * a_s
        sh *= 2
    return a, u


def _scan_bwd(al, g, row):
    sh = 1
    while sh < RT:
        al_s = jnp.where(row < RT - sh, pltpu.roll(al, RT - sh, 0), 1.0)
        g_s = jnp.where(row < RT - sh, pltpu.roll(g, RT - sh, 0), 0.0)
        g = g + al * g_s
        al = al * al_s
        sh *= 2
    return g


def _dense_from_blocks(blocks_ref, dense_ref):
    dense_ref[...] = jnp.zeros_like(dense_ref)
    for h in range(R // HEAD):
        dense_ref[h * HEAD:(h + 1) * HEAD, h * HEAD:(h + 1) * HEAD] = blocks_ref[h].astype(dense_ref.dtype)


def _rec_fwd(proj, conv_w, conv_b, wa_b, ba, wx_b, bx, lam, norm_rec):
    nt = S // RT

    def body(p_ref, cw_ref, cb_ref, wa_ref, ba_ref, wx_ref, bx_ref, lam_ref, nr_ref,
             h_ref, ya_ref, prev8, hc, wad, wxd):
        i = pl.program_id(0)

        @pl.when(i == 0)
        def _():
            prev8[...] = jnp.zeros_like(prev8)
            hc[...] = jnp.zeros_like(hc)
            _dense_from_blocks(wa_ref, wad)
            _dense_from_blocks(wx_ref, wxd)

        row = lax.broadcasted_iota(jnp.int32, (RT, R), 0)
        xp = p_ref[:, 0:R]
        ga = p_ref[:, R:2 * R]
        f = _rec_gates(xp, prev8[...], row, cw_ref, cb_ref, wad, ba_ref, wxd, bx_ref, lam_ref)
        u = f["mult"] * (f["ig"] * f["xa"])
        acum, hh = _scan_fwd(f["a"], u, row)
        h = hh + acum * hc[0:1, :]
        h_ref[...] = h
        hc[0:1, :] = h_ref[RT - 1:RT, :]
        prev8[...] = p_ref[RT - 8:RT, 0:R]
        yp = h * (ga * _sigmoid(ga))
        ya, _, _ = _rms_fwd(yp, nr_ref[...])
        ya_ref[...] = ya.astype(BF16)

    row1 = lambda n: pl.BlockSpec((1, n), lambda i: (0, 0))
    blocks = pl.BlockSpec((R // HEAD, HEAD, HEAD), lambda i: (0, 0, 0))
    return pl.pallas_call(
        body, name="rec_fwd", grid=(nt,),
        in_specs=[pl.BlockSpec((RT, 2 * R), lambda i: (i, 0)), pl.BlockSpec((4, R), lambda i: (0, 0)), row1(R),
                  blocks, row1(R), blocks, row1(R), row1(R), row1(R)],
        out_specs=[pl.BlockSpec((RT, R), lambda i: (i, 0)), pl.BlockSpec((RT, R), lambda i: (i, 0))],
        out_shape=[jax.ShapeDtypeStruct((S, R), F32), jax.ShapeDtypeStruct((S, R), BF16)],
        scratch_shapes=[pltpu.VMEM((8, R), F32), pltpu.VMEM((8, R), F32), pltpu.VMEM((R, R), BF16),
                        pltpu.VMEM((R, R), BF16)],
        compiler_params=_cp(("arbitrary",)),
    )(proj, conv_w, conv_b, wa_b, ba, wx_b, bx, lam, norm_rec)


def _rec_bwd(dproj, d_ya, proj, h_all, conv_w, conv_b, wa_b, ba, wx_b, bx, lam, norm_rec):
    nt = S // RT

    def body(dp_in, dya_ref, p_ref, pprev_ref, h_ref, hprev_ref, cw_ref, cb_ref, wab_ref, ba_ref, wxb_ref, bx_ref,
             lam_ref, nr_ref, dp_ref, dwab_ref, dwxb_ref, sm_ref, nxt8, cg, wa_ref, wx_ref, dwa_ref, dwx_ref):
        i = pl.program_id(0)
        ti = nt - 1 - i

        @pl.when(i == 0)
        def _():
            nxt8[...] = jnp.zeros_like(nxt8)
            cg[...] = jnp.zeros_like(cg)
            dwa_ref[...] = jnp.zeros_like(dwa_ref)
            dwx_ref[...] = jnp.zeros_like(dwx_ref)
            sm_ref[...] = jnp.zeros_like(sm_ref)
            _dense_from_blocks(wab_ref, wa_ref)
            _dense_from_blocks(wxb_ref, wx_ref)

        row = lax.broadcasted_iota(jnp.int32, (RT, R), 0)
        first = (ti > 0).astype(F32)
        xprev8 = pprev_ref[...] * first
        hprev8 = hprev_ref[...] * first
        xp = p_ref[:, 0:R]
        ga = p_ref[:, R:2 * R]
        f = _rec_gates(xp, xprev8, row, cw_ref, cb_ref, wa_ref, ba_ref, wx_ref, bx_ref, lam_ref)
        xa, r, ig, a, mult = f["xa"], f["r"], f["ig"], f["a"], f["mult"]
        h = h_ref[...]
        sg = _sigmoid(ga)
        gate = ga * sg
        yp = h * gate
        _, ypn, rstd = _rms_fwd(yp, nr_ref[...])
        d_yp, dnr = _rms_bwd(dya_ref[...], ypn, rstd, nr_ref[...])
        d_ga = d_yp * h * (sg * (1.0 + ga * (1.0 - sg)))
        dh = d_yp * gate + jnp.where(row == RT - 1, cg[0:1, :], 0.0)
        al = jnp.where(row < RT - 1, pltpu.roll(a, RT - 1, 0), 0.0)
        g = _scan_bwd(al, dh, row)
        cg[0:1, :] = jnp.sum(jnp.where(row == 0, a * g, 0.0), axis=0, keepdims=True)
        h_m1 = _shift_down(h, hprev8, 1, row)
        da = g * h_m1
        ix = ig * xa
        d_mult = g * ix
        d_ig = g * mult * xa
        d_xa = g * mult * ig
        d_la = da * a - d_mult * (a * a) / mult
        d_r = d_la * ((-LRU_C) * f["sp"])
        dsp = jnp.sum(d_la * ((-LRU_C) * r), axis=0, keepdims=True)
        dlam = dsp * (-_sigmoid(-lam_ref[...]))
        d_za = d_r * r * (1.0 - r)
        d_zx = d_ig * ig * (1.0 - ig)
        dzab = d_za.astype(BF16)
        dzxb = d_zx.astype(BF16)
        dwa_ref[...] += _dot_tn(f["xab"], dzab)
        dwx_ref[...] += _dot_tn(f["xab"], dzxb)
        d_xa = d_xa + _dot_nt(dzab, wa_ref[...]) + _dot_nt(dzxb, wx_ref[...])
        d_xp = sum(cw_ref[3 - j:4 - j, :] * _shift_up(d_xa, nxt8[...], j, row) for j in range(4))
        dcw = [jnp.sum(d_xa * _shift_down(xp, xprev8, 3 - k, row), axis=0, keepdims=True) for k in range(4)]
        dp_ref[:, 0:R] = d_xp.astype(BF16)
        dp_ref[:, R:2 * R] = d_ga.astype(BF16)
        dp8 = d_xa[0:8, :]
        nxt8[...] = dp8
        sm_ref[0:1, :] += jnp.sum(d_za, axis=0, keepdims=True)
        sm_ref[1:2, :] += jnp.sum(d_zx, axis=0, keepdims=True)
        sm_ref[2:3, :] += dlam
        sm_ref[3:4, :] += dnr
        sm_ref[4:5, :] += jnp.sum(d_xa, axis=0, keepdims=True)
        for k in range(4):
            sm_ref[8 + k:9 + k, :] += dcw[k]

        @pl.when(i == nt - 1)
        def _():
            for h in range(R // HEAD):
                dwab_ref[h] = dwa_ref[h * HEAD:(h + 1) * HEAD, h * HEAD:(h + 1) * HEAD].astype(BF16)
                dwxb_ref[h] = dwx_ref[h * HEAD:(h + 1) * HEAD, h * HEAD:(h + 1) * HEAD].astype(BF16)

    c0 = lambda shape: pl.BlockSpec(shape, lambda i: (0, 0))
    blocks = pl.BlockSpec((R // HEAD, HEAD, HEAD), lambda i: (0, 0, 0))
    rev = lambda i: nt - 1 - i
    prev8 = lambda i: (jnp.maximum((nt - 1 - i) * (RT // 8) - 1, 0), 0)
    return pl.pallas_call(
        body, name="rec_bwd", grid=(nt,),
        in_specs=[pl.BlockSpec(memory_space=pl.ANY),
                  pl.BlockSpec((RT, R), lambda i: (rev(i), 0)),
                  pl.BlockSpec((RT, 2 * R), lambda i: (rev(i), 0)), pl.BlockSpec((8, R), prev8),
                  pl.BlockSpec((RT, R), lambda i: (rev(i), 0)), pl.BlockSpec((8, R), prev8),
                  c0((4, R)), c0((1, R)), blocks, c0((1, R)), blocks, c0((1, R)), c0((1, R)), c0((1, R))],
        out_specs=[pl.BlockSpec((RT, 2 * R), lambda i: (rev(i), 0)), blocks, blocks, c0((16, R))],
        out_shape=[jax.ShapeDtypeStruct((S, E), BF16), jax.ShapeDtypeStruct((R // HEAD, HEAD, HEAD), BF16),
                   jax.ShapeDtypeStruct((R // HEAD, HEAD, HEAD), BF16), jax.ShapeDtypeStruct((16, R), F32)],
        scratch_shapes=[pltpu.VMEM((8, R), F32), pltpu.VMEM((8, R), F32), pltpu.VMEM((R, R), BF16),
                        pltpu.VMEM((R, R), BF16), pltpu.VMEM((R, R), F32), pltpu.VMEM((R, R), F32)],
        input_output_aliases={0: 0},
        compiler_params=_cp(("arbitrary",)),
    )(dproj, d_ya, proj, proj, h_all, h_all, conv_w, conv_b, wa_b, ba, wx_b, bx, lam, norm_rec)


NPAIR = R // LANES
QB, KB, VB, GB = 2 * R // LANES, 3 * R // LANES, 4 * R // LANES, 5 * R // LANES


def _rope_freq():
    half = HEAD // 2
    inv = np.float32(ROPE_THETA) ** (-(np.arange(half, dtype=np.float32) / np.float32(half)))
    return jnp.asarray(np.tile(inv.astype(np.float32), LANES // half)[None, :])


def _rot_half(x, first):
    return jnp.where(first, -pltpu.roll(x, LANES - HEAD // 2, 1), pltpu.roll(x, HEAD // 2, 1))


def _cos_sin(pos_ref, freq_ref):
    ang = pos_ref[...].astype(F32) * freq_ref[...]
    return jnp.cos(ang), jnp.sin(ang)


def _deint(src_ref, dst_ref, d):
    n = S // d
    for r in range(d):
        v = src_ref[pl.ds(r, n, stride=d), :] if d > 1 else src_ref[...]
        dst_ref[r * n:(r + 1) * n, :] = v.astype(dst_ref.dtype)


def _reint(src_ref, dst_ref, d, accumulate):
    n = S // d
    for r in range(d):
        idx = (pl.ds(r, n, stride=d), slice(None)) if d > 1 else (slice(None), slice(None))
        v = src_ref[r * n:(r + 1) * n, :]
        if accumulate:
            dst_ref[idx] = dst_ref[idx] + v
        else:
            dst_ref[idx] = v


def _deint_heads(src_ref, dst0, dst1, d):
    n = S // d
    hm0 = lax.broadcasted_iota(jnp.int32, (n, LANES), 1) < HEAD
    for r in range(d):
        v = src_ref[pl.ds(r, n, stride=d), :] if d > 1 else src_ref[...]
        dst0[r * n:(r + 1) * n, :] = jnp.where(hm0, v, 0.0).astype(BF16)
        dst1[r * n:(r + 1) * n, :] = jnp.where(hm0, 0.0, v).astype(BF16)


def _reint_prev(src_ref, dst_ref, d):
    n = S // d
    if n == BLK:
        return
    for r in range(d):
        idx = (pl.ds(r, n - BLK, stride=d), slice(None)) if d > 1 else (slice(0, n - BLK), slice(None))
        dst_ref[idx] = dst_ref[idx] + src_ref[r * n + BLK:(r + 1) * n, :]


def _pair_masks():
    qi = lax.broadcasted_iota(jnp.int32, (BLK, 2 * BLK), 0)
    ki = lax.broadcasted_iota(jnp.int32, (BLK, 2 * BLK), 1) & (BLK - 1)
    return ki <= qi, ki >= qi


def _two(ref0, ref1, st, axis):
    return jnp.concatenate([ref0[pl.ds(st, BLK), :], ref1[pl.ds(st, BLK), :]], axis=axis)


ATT_UNROLL = 8


def _att_fwd(proj, cos, sin, w_out):
    def body(q_ref, k_ref, v_ref, cos_ref, sin_ref, w_ref, att_ref, qr_ref, kr_ref, lse_ref, wbf_ref,
             qd, kd0, kd1, vd0, vd1, od, ld, on, ln, wbuf, *wsems):
        wg = _WeightGather(w_ref, wbuf, *wsems)
        pl.when(pl.program_id(0) == 0)(wg.start)
        pl.when(pl.program_id(0) == 1)(wg.forward)
        lane = lax.broadcasted_iota(jnp.int32, (S, LANES), 1)
        first = (lane & (HEAD // 2)) == 0
        cos, sin = cos_ref[...], sin_ref[...]
        q = q_ref[...]
        k = k_ref[...]
        qr_ref[...] = (q * cos + _rot_half(q, first) * sin) * (HEAD ** -0.5)
        kr_ref[...] = k * cos + _rot_half(k, first) * sin
        hm0 = lax.broadcasted_iota(jnp.int32, (BLK, LANES), 1) < HEAD
        top = lax.broadcasted_iota(jnp.int32, (2 * BLK, LANES), 0) < BLK
        ones2 = (top == (lax.broadcasted_iota(jnp.int32, (2 * BLK, LANES), 1) < HEAD)).astype(BF16)
        mc2, mp2 = _pair_masks()

        for pi, d in enumerate(PATTERNS):
            nb = S // d // BLK
            _deint(qr_ref, qd, d)
            _deint_heads(kr_ref, kd0, kd1, d)
            _deint_heads(v_ref, vd0, vd1, d)

            def blk(b, carry):
                st = pl.multiple_of(b * BLK, BLK)
                qb = qd[pl.ds(st, BLK), :]
                sc = jnp.where(mc2, _dot_nt(qb, _two(kd0, kd1, st, 0)), NEG)
                mx = sc
                if nb > 1:
                    stp = pl.multiple_of(jnp.maximum(b - 1, 0) * BLK, BLK)
                    mp = jnp.logical_and(mp2, lax.rem(b, nb) != 0)
                    sp = jnp.where(mp, _dot_nt(qb, _two(kd0, kd1, stp, 0)), NEG)
                    mx = jnp.maximum(sc, sp)
                m0 = jnp.max(mx[:, 0:BLK], axis=1, keepdims=True)
                m1 = jnp.max(mx[:, BLK:2 * BLK], axis=1, keepdims=True)
                mf = jnp.concatenate([jnp.broadcast_to(m0, (BLK, BLK)), jnp.broadcast_to(m1, (BLK, BLK))], axis=1)
                o = _dot(jnp.exp(sc - mf).astype(BF16), jnp.concatenate([_two(vd0, vd1, st, 0), ones2], axis=1))
                if nb > 1:
                    o = o + _dot(jnp.exp(sp - mf).astype(BF16), jnp.concatenate([_two(vd0, vd1, stp, 0), ones2], axis=1))
                l = o[:, LANES:2 * LANES]
                od[pl.ds(st, BLK), :] = o[:, 0:LANES] / l
                ld[pl.ds(st, BLK), :] = jnp.where(hm0, m0, m1) + jnp.log(l)
                return carry

            lax.fori_loop(0, S // BLK, blk, 0, unroll=ATT_UNROLL)
            _reint(od, on.at[pi], d, False)
            _reint(ld, ln.at[pi], d, False)

        l0, l1, l2 = ln[0], ln[1], ln[2]
        m = jnp.maximum(jnp.maximum(l0, l1), l2)
        e0, e1, e2 = jnp.exp(l0 - m), jnp.exp(l1 - m), jnp.exp(l2 - m)
        den = e0 + e1 + e2
        att_ref[...] = (e0 * on[0] + e1 * on[1] + e2 * on[2]) / den
        lse_ref[...] = m + jnp.log(den)

        @pl.when(pl.program_id(0) == NPAIR - 1)
        def _():
            wg.finish()
            wbf_ref[...] = wbuf[...]

    col = lambda c0: pl.BlockSpec((S, LANES), lambda p: (0, c0 + p))
    out = pl.BlockSpec((S, LANES), lambda p: (0, p))
    tab = pl.BlockSpec((S, LANES), lambda p: (0, 0))
    vm = pl.BlockSpec(memory_space=pltpu.VMEM)
    return pl.pallas_call(
        body, name="att_fwd", grid=(NPAIR,),
        in_specs=[col(QB), col(KB), col(VB), tab, tab, vm],
        out_specs=[out, out, out, out, vm],
        out_shape=[jax.ShapeDtypeStruct((S, R), F32)] * 4 + [jax.ShapeDtypeStruct((NCHIP,) + w_out.shape, BF16)],
        scratch_shapes=[pltpu.VMEM((S, LANES), BF16)] * 5 + [pltpu.VMEM((S, LANES), F32)] * 2
        + [pltpu.VMEM((3, S, LANES), F32)] * 2 + [pltpu.VMEM((NCHIP,) + w_out.shape, BF16)] + _WeightGather.SEMS,
        compiler_params=_cp(("arbitrary",)),
    )(proj, proj, proj, cos, sin, w_out)


def _att_bwd(dproj, d_att, att, lse, qr, kr, proj, cos, sin, gw_out4):
    out_units = [(j, j, 0) for j in range(NCHIP)]

    nblk = S // BLK

    def body(dp_in, do_ref, o_ref, lse_ref, qr_ref, kr_ref, v_ref, cos_ref, sin_ref, gw_ref, dp_ref, gout_ref,
             qd, kd0, kd1, vd0, vd1, dod, kt, packn, packd, dqd, dkcd, dkpd, dvcd, dvpd,
             dqn, dkn, dvn, rows, trs, stage, sems, gred, *rs_scratch):
        p = pl.program_id(0)
        rs = _ReduceScatter(gw_ref, gred, out_units, *rs_scratch)
        for step, piece in enumerate((rs.start_halves, rs.send_partials, rs.reduce_owned)):
            pl.when(p == step)(piece)

        @pl.when(p == NPAIR - 1)
        def _():
            rs.finish()
            gout_ref[...] = gred[...]

        lane = lax.broadcasted_iota(jnp.int32, (S, LANES), 1)
        hms = lane < HEAD
        prod = do_ref[...] * o_ref[...]
        d0 = jnp.sum(jnp.where(hms, prod, 0.0), axis=1, keepdims=True)
        d1 = jnp.sum(jnp.where(hms, 0.0, prod), axis=1, keepdims=True)
        lse = lse_ref[...]
        quarter = HEAD // 2
        packn[...] = jnp.where(lane < quarter, lse,
                               jnp.where(hms, pltpu.roll(lse, LANES - quarter, 1), jnp.where(lane < 3 * quarter, d0, d1)))
        dqn[...] = jnp.zeros_like(dqn)
        dkn[...] = jnp.zeros_like(dkn)
        dvn[...] = jnp.zeros_like(dvn)
        hm0 = lax.broadcasted_iota(jnp.int32, (BLK, LANES), 1) < HEAD
        key = lax.broadcasted_iota(jnp.int32, (2 * BLK, BLK), 0) & (BLK - 1)
        qry = lax.broadcasted_iota(jnp.int32, (2 * BLK, BLK), 1)
        mct, mpt = key <= qry, key >= qry

        for d in PATTERNS:
            nb = S // d // BLK
            _deint(qr_ref, qd, d)
            _deint_heads(kr_ref, kd0, kd1, d)
            _deint_heads(v_ref, vd0, vd1, d)
            _deint(do_ref, dod, d)
            _deint(packn, packd, d)

            def blk(b, carry):
                st = pl.multiple_of(b * BLK, BLK)
                kt[b] = _two(kd0, kd1, st, 0).astype(F32).T.astype(BF16)
                trs[b] = packd[pl.ds(st, BLK), :].T
                for j in range(4):
                    rows[b, j:j + 1, :] = trs[b, j * quarter:j * quarter + 1, :]
                qb, dob = qd[pl.ds(st, BLK), :], dod[pl.ds(st, BLK), :]
                both = lambda j: jnp.concatenate([jnp.broadcast_to(rows[b, j:j + 1, :], (BLK, BLK)),
                                                  jnp.broadcast_to(rows[b, j + 1:j + 2, :], (BLK, BLK))], axis=0)
                lbt, dlt = both(0), both(2)

                def side(bk, mask):
                    stk = pl.multiple_of(bk * BLK, BLK)
                    k2, v2 = _two(kd0, kd1, stk, 0), _two(vd0, vd1, stk, 0)
                    pt = jnp.where(mask, jnp.exp(_dot_nt(k2, qb) - lbt), 0.0)
                    dst = (pt * (_dot_nt(v2, dob) - dlt)).astype(BF16)
                    rk, rv = _dot(dst, qb), _dot(pt.astype(BF16), dob)
                    return (_dot(kt[bk], dst), jnp.where(hm0, rk[0:BLK], rk[BLK:2 * BLK]),
                            jnp.where(hm0, rv[0:BLK], rv[BLK:2 * BLK]))

                dq_t, dkc, dvc = side(b, mct)
                if nb > 1:
                    dqp_t, dkp, dvp = side(jnp.maximum(b - 1, 0), jnp.logical_and(mpt, lax.rem(b, nb) != 0))
                    dq_t = dq_t + dqp_t
                    dkpd[pl.ds(st, BLK), :] = dkp
                    dvpd[pl.ds(st, BLK), :] = dvp
                dqd[pl.ds(st, BLK), :] = dq_t.T
                dkcd[pl.ds(st, BLK), :] = dkc
                dvcd[pl.ds(st, BLK), :] = dvc
                return carry

            lax.fori_loop(0, nblk, blk, 0, unroll=ATT_UNROLL)
            _reint(dqd, dqn, d, True)
            _reint(dkcd, dkn, d, True)
            _reint(dvcd, dvn, d, True)
            _reint_prev(dkpd, dkn, d)
            _reint_prev(dvpd, dvn, d)

        lane = lax.broadcasted_iota(jnp.int32, (S, LANES), 1)
        first = (lane & (HEAD // 2)) == 0
        cos, sin = cos_ref[...], sin_ref[...]
        dq = dqn[...] * (HEAD ** -0.5)
        dk = dkn[...]
        stage[0] = (dq * cos - _rot_half(dq, first) * sin).astype(BF16)
        stage[1] = (dk * cos - _rot_half(dk, first) * sin).astype(BF16)
        stage[2] = dvn[...].astype(BF16)
        copies = [pltpu.make_async_copy(stage.at[j], dp_ref.at[:, pl.ds((2 + j) * R + p * LANES, LANES)], sems.at[j])
                  for j in range(3)]
        for cp in copies:
            cp.start()
        for cp in copies:
            cp.wait()

    blk = pl.BlockSpec((S, LANES), lambda p: (0, p))
    tab = pl.BlockSpec((S, LANES), lambda p: (0, 0))
    vm = pl.BlockSpec(memory_space=pltpu.VMEM)
    _, orows, ocols = gw_out4.shape
    return pl.pallas_call(
        body, name="att_bwd", grid=(NPAIR,),
        in_specs=[pl.BlockSpec(memory_space=pl.ANY), blk, blk, blk, blk, blk,
                  pl.BlockSpec((S, LANES), lambda p: (0, VB + p)), tab, tab, vm],
        out_specs=[pl.BlockSpec(memory_space=pl.ANY), vm],
        out_shape=[jax.ShapeDtypeStruct((S, E), BF16), jax.ShapeDtypeStruct((orows, ocols), F32)],
        scratch_shapes=[pltpu.VMEM((S, LANES), BF16)] * 6 + [pltpu.VMEM((nblk, LANES, 2 * BLK), BF16)]
        + [pltpu.VMEM((S, LANES), F32)] * 10
        + [pltpu.VMEM((nblk, 8, BLK), F32), pltpu.VMEM((nblk, LANES, BLK), F32)]
        + [pltpu.VMEM((3, S, LANES), BF16), pltpu.SemaphoreType.DMA((3,)), pltpu.VMEM((orows, ocols), F32)]
        + _ReduceScatter.scratch(NCHIP, orows, ocols, 1),
        input_output_aliases={0: 0},
        compiler_params=_cp(("arbitrary",)),
    )(dproj, d_att, att, lse, qr, kr, proj, cos, sin, gw_out4)


def _out_fwd_bwd(ya, att, proj, w_out_bf, x, target, mod, norm_post, norm_att):
    ts = 512

    def body(ya_ref, att_ref, gb_ref, w_ref, x_ref, t_ref, mod_ref, npost_ref, natt_ref,
             gx_ref, dya_ref, datt_ref, dgb_ref, gw_ref, acc_ref):
        i = pl.program_id(0)

        @pl.when(i == 0)
        def _():
            gw_ref[...] = jnp.zeros_like(gw_ref)
            acc_ref[...] = jnp.zeros_like(acc_ref)

        gate = mod_ref[:, 2 * D:3 * D]
        att = att_ref[...]
        gb = gb_ref[...]
        sg = _sigmoid(gb)
        silu = gb * sg
        ybp = att * silu
        yb, ybn, rstd_b = _rms_fwd(ybp, natt_ref[...])
        cat = jnp.concatenate([ya_ref[...], yb.astype(BF16)], axis=1)
        mix = _dot(cat, w_ref[...])
        rn, mn, rstd_m = _rms_fwd(mix, npost_ref[...])
        err = x_ref[...] + gate * rn - t_ref[...]
        dy = err * (1.0 / D)
        gx_ref[...] = dy
        dmix, dnpost = _rms_bwd(dy * gate, mn, rstd_m, npost_ref[...])
        dmb = dmix.astype(BF16)
        gw_ref[...] += _dot_tn(cat, dmb)
        dcat = _dot_nt(dmb, w_ref[...])
        dya_ref[...] = dcat[:, 0:R]
        dybp, dnatt = _rms_bwd(dcat[:, R:2 * R], ybn, rstd_b, natt_ref[...])
        datt_ref[...] = dybp * silu
        dgb_ref[...] = (dybp * att * (sg * (1.0 + gb * (1.0 - sg)))).astype(BF16)
        acc_ref[0:1, :] += jnp.sum(dy * rn, axis=0, keepdims=True)
        acc_ref[1:2, :] += dnpost
        acc_ref[2:3, 0:R] += dnatt
        acc_ref[3:4, :] += jnp.sum(jnp.sum(err * err, axis=1, keepdims=True), axis=0, keepdims=True)

    tile = lambda w: pl.BlockSpec((ts, w), lambda i: (i, 0))
    c0 = lambda shape: pl.BlockSpec(shape, lambda i: (0, 0))
    return pl.pallas_call(
        body, name="out_fwd_bwd", grid=(S // ts,),
        in_specs=[tile(R), tile(R), pl.BlockSpec((ts, R), lambda i: (i, 5)), c0((D, D)), tile(D), tile(D),
                  c0((1, 3 * D)), c0((1, D)), c0((1, R))],
        out_specs=[tile(D), tile(R), tile(R), pl.BlockSpec((ts, R), lambda i: (i, 5)), c0((D, D)), c0((8, D))],
        out_shape=[jax.ShapeDtypeStruct((S, D), F32), jax.ShapeDtypeStruct((S, R), F32),
                   jax.ShapeDtypeStruct((S, R), F32), jax.ShapeDtypeStruct((S, E), BF16),
                   jax.ShapeDtypeStruct((D, D), F32), jax.ShapeDtypeStruct((8, D), F32)],
        compiler_params=_cp(("arbitrary",)),
    )(ya, att, proj, w_out_bf, x, target, mod, norm_post, norm_att)


UC = 256
UPC = EC // UC


NU = E // UC


def _unit_of_step(i):
    return (i % NCHIP) * UPC + i // NCHIP


def _in_proj_bwd(ht, dproj, w_in_bf, x, gx1, mod, norm_pre, smalls):
    ts = 256
    nt = S // ts
    half = D // 2
    units = [_unit_of_step(k) for k in range(NU)]
    owners = [u // UPC for u in units]
    ns = len(smalls)

    def body(*refs):
        (ht_ref, dpu_ref, dp_ref, w_hbm, x_ref, gx1_ref, mod_ref, np_ref), refs = refs[:8], refs[8:]
        small_in, refs = refs[:ns], refs[ns:]
        (gx_ref, gin_ref), refs = refs[:2], refs[2:]
        small_out, (acc_out,), refs = refs[:ns], refs[ns:ns + 1], refs[ns + 1:]
        mine, sib, tmp, stage, got, red, acc_ref, hs, hr, ps, pr, bs, br = refs[:13]
        early = _SmallGather(small_in, small_out, *refs[13:16])
        late = _SmallGather([acc_ref], [acc_out], *refs[16:19])
        w_ref, w_sem = refs[19:21]
        i = pl.program_id(0)
        w_copy = pltpu.make_async_copy(w_hbm, w_ref, w_sem)
        pl.when(i == 0)(w_copy.start)
        pl.when(i == NU)(w_copy.wait)
        xx, yy, c = _me()
        ci = 2 * xx + yy
        r0 = pl.multiple_of(c * half, half)
        r1 = pl.multiple_of((1 - c) * half, half)
        pl.when(i == 0)(early.start)
        pl.when(i == NU)(early.forward)

        def exch(k):
            return _remote(tmp.at[k % 2], sib.at[k], hs.at[k], hr.at[k], 1)

        def partial(k, sender):
            return pltpu.make_async_remote_copy(
                src_ref=stage.at[k], dst_ref=got.at[units[k] % UPC, sender], send_sem=ps.at[k],
                recv_sem=pr.at[k, sender], device_id=(owners[k] // 2, owners[k] % 2, c), device_id_type=MESH)

        def back(k, start):
            off = (units[k] % UPC) * UC
            blk = red.at[pl.ds(start, half), off:off + UC]
            return _remote(blk, blk, bs.at[k], br.at[k], 1)

        for k in range(NU + 1):
            @pl.when(i == k)
            def _():
                if k < NU:
                    if k >= 2:
                        exch(k - 2).wait_send()
                    dpu = dpu_ref[...]
                    tmp[k % 2] = _dot(ht_ref[pl.ds(r1, half), :], dpu)
                    exch(k).start()
                    mine[k] = _dot(ht_ref[pl.ds(r0, half), :], dpu)
                if k >= 1:
                    exch(k - 1).wait_recv()
                    mine[k - 1] += sib[k - 1]

                    @pl.when(ci != owners[k - 1])
                    def _():
                        stage[k - 1] = mine[k - 1].astype(BF16)
                        partial(k - 1, ci).start()

        @pl.when(i == NU)
        def _():
            acc_ref[...] = jnp.zeros_like(acc_ref)

        @pl.when(i >= NU)
        def _():
            dh = sum(_dot_nt(dp_ref[:, j * EC:(j + 1) * EC], w_ref[j]) for j in range(NCHIP))
            hp, xn, rstd = _rms_fwd(x_ref[...], np_ref[...])
            dx, dnp = _rms_bwd(dh * (1.0 + mod_ref[:, D:2 * D]), xn, rstd, np_ref[...])
            gx_ref[...] = gx1_ref[...] + dx
            acc_ref[0:1, :] += jnp.sum(dh, axis=0, keepdims=True)
            acc_ref[1:2, :] += jnp.sum(dh * hp, axis=0, keepdims=True)
            acc_ref[2:3, :] += dnp

        for t in range(UPC):
            @pl.when(i == NU + 1 + 2 * t)
            def _():
                for k in range(NCHIP * t, NCHIP * (t + 1)):
                    @pl.when(ci == owners[k])
                    def _():
                        off = (units[k] % UPC) * UC
                        red[pl.ds(r0, half), off:off + UC] = mine[k]
                        for s in range(NCHIP):
                            if s != owners[k]:
                                partial(k, s).wait_recv()
                                red[pl.ds(r0, half), off:off + UC] += got[units[k] % UPC, s].astype(F32)
                        back(k, r0).start()

        @pl.when(i == NU + nt - 1)
        def _():
            late.start()
            exch(NU - 2).wait_send()
            exch(NU - 1).wait_send()
            for k in range(NU):
                @pl.when(ci == owners[k])
                def _():
                    back(k, r1).wait_recv()
                    back(k, r0).wait_send()

                @pl.when(ci != owners[k])
                def _():
                    partial(k, ci).wait_send()
            gin_ref[...] = red[...]
            early.finish()
            late.forward()
            late.finish()

    tile = lambda w: pl.BlockSpec((ts, w), lambda i: (jnp.maximum(i - NU, 0), 0))
    c0 = lambda shape: pl.BlockSpec(shape, lambda i: (0, 0))
    vm = pl.BlockSpec(memory_space=pltpu.VMEM)
    hbm = pl.BlockSpec(memory_space=pl.ANY)
    gathered = [jax.ShapeDtypeStruct((NDEV,) + a.shape, a.dtype) for a in smalls] + [jax.ShapeDtypeStruct((NDEV, 8, D), F32)]
    return pl.pallas_call(
        body, name="in_proj_bwd", grid=(NU + nt,),
        in_specs=[vm, pl.BlockSpec((S, UC), lambda i: (0, _unit_of_step(jnp.minimum(i, NU - 1)))), tile(E),
                  hbm, tile(D), tile(D), c0((1, 3 * D)), c0((1, D))] + [vm] * ns,
        out_specs=[tile(D), vm] + [hbm] * (ns + 1),
        out_shape=[jax.ShapeDtypeStruct((S, D), F32), jax.ShapeDtypeStruct((D, EC), F32)] + gathered,
        scratch_shapes=[pltpu.VMEM((NU, half, UC), F32), pltpu.VMEM((NU, half, UC), F32),
                        pltpu.VMEM((2, half, UC), F32), pltpu.VMEM((NU, half, UC), BF16),
                        pltpu.VMEM((UPC, NCHIP, half, UC), BF16), pltpu.VMEM((D, EC), F32), pltpu.VMEM((8, D), F32),
                        pltpu.SemaphoreType.DMA((NU,)), pltpu.SemaphoreType.DMA((NU,)),
                        pltpu.SemaphoreType.DMA((NU,)), pltpu.SemaphoreType.DMA((NU, NCHIP)),
                        pltpu.SemaphoreType.DMA((NU,)), pltpu.SemaphoreType.DMA((NU,))]
        + _SmallGather.sems(ns) + _SmallGather.sems(1)
        + [pltpu.VMEM((NCHIP, D, EC), BF16), pltpu.SemaphoreType.DMA],
        compiler_params=_cp(("arbitrary",)),
    )(ht, dproj, dproj, w_in_bf, x, gx1, mod, norm_pre, *smalls)


def _local_step(x, cos, sin, target, mod, w_in_bf, proj, ht, w_out, conv_w, p):
    rec_p = (conv_w, p["conv_b"], p["w_rg_a"], p["b_rg_a"], p["w_rg_x"], p["b_rg_x"], p["lru_lambda"], p["norm_rec"])
    h_all, ya = _rec_fwd(proj, *rec_p)
    att, qr, kr, lse, w_out_bf = _att_fwd(proj, cos, sin, w_out)
    gx1, d_ya, d_att, dproj, gw_out, acc_o = _out_fwd_bwd(ya, att, proj, w_out_bf.reshape(D, D), x, target, mod,
                                                           p["norm_post"], p["norm_att"])
    dproj, g_out = _att_bwd(dproj, d_att, att, lse, qr, kr, proj, cos, sin, gw_out.reshape(NCHIP, D // NCHIP, D))
    dproj, dwa, dwx, sm = _rec_bwd(dproj, d_ya, proj, h_all, *rec_p)
    grad_x, g_in, *gathered = _in_proj_bwd(ht, dproj, w_in_bf, x, gx1, mod, p["norm_pre"], [acc_o, sm, dwa, dwx])
    return grad_x, g_in, g_out, gathered


def _me():
    return lax.axis_index("x"), lax.axis_index("y"), lax.axis_index("c")


def _flip(v, bit):
    return 1 - v if bit else v


def _peer(rel):
    x, y, c = _me()
    return (_flip(x, rel & 4), _flip(y, rel & 2), _flip(c, rel & 1))


def _remote(src, dst, send_sem, recv_sem, rel):
    return pltpu.make_async_remote_copy(src_ref=src, dst_ref=dst, send_sem=send_sem, recv_sem=recv_sem,
                                        device_id=_peer(rel), device_id_type=MESH)


class _WeightGather:
    SEMS = [pltpu.SemaphoreType.DMA((NCHIP - 1,))] * 4

    def __init__(self, w_ref, out_ref, send_sems, recv_sems, fsend_sems, frecv_sems):
        x, y, c = _me()
        self.w, self.out, self.ci = w_ref, out_ref, 2 * x + y
        self.half = w_ref.shape[0] // 2
        self.r0 = pl.multiple_of(c * self.half, self.half)
        self.r1 = pl.multiple_of((1 - c) * self.half, self.half)
        self.sems = (send_sems, recv_sems, fsend_sems, frecv_sems)

    def _ici(self, chip, k):
        blk = self.out.at[chip, pl.ds(self.r0, self.half), :]
        return _remote(blk, blk, self.sems[0].at[k - 1], self.sems[1].at[k - 1], 2 * k)

    def _d2d(self, chip, start, k):
        blk = self.out.at[chip, pl.ds(start, self.half), :]
        return _remote(blk, blk, self.sems[2].at[k - 1], self.sems[3].at[k - 1], 1)

    def start(self, diagonal=True):
        self.out[self.ci] = self.w[...].astype(BF16)
        for k in range(1, NCHIP if diagonal else NCHIP - 1):
            self._ici(self.ci, k).start()

    def _relay(self, chip, piece, k):
        q = self.half // 2
        blk = self.out.at[chip, pl.ds(self.r0 + piece * q, q), :]
        return _remote(blk, blk, self.relay_sems[0].at[piece], self.relay_sems[1].at[piece], 2 * k)

    def neighbours_landed(self, relay_send_sems, relay_recv_sems):
        self.relay_sems = (relay_send_sems, relay_recv_sems)
        for k in (1, 2):
            self._ici(self.ci ^ k, k).wait_recv()
        self._relay(self.ci ^ 2, 0, 1).start()
        self._relay(self.ci ^ 1, 1, 2).start()
        for k in (1, 2):
            self._d2d(self.ci ^ k, self.r0, k).start()

    def sibling_landed(self, k):
        self._d2d(self.ci ^ k, self.r1, k).wait_recv()

    def diagonal_landed(self):
        for piece, k in ((0, 1), (1, 2)):
            self._relay(self.ci ^ 3, piece, k).wait_recv()
        self._d2d(self.ci ^ 3, self.r0, 3).start()
        self._d2d(self.ci ^ 3, self.r1, 3).wait_recv()

    def finish_relayed(self):
        for k in (1, 2):
            self._ici(self.ci, k).wait_send()
        self._relay(self.ci ^ 2, 0, 1).wait_send()
        self._relay(self.ci ^ 1, 1, 2).wait_send()
        for k in range(1, NCHIP):
            self._d2d(self.ci ^ k, self.r0, k).wait_send()

    def forward(self):
        for k in range(1, NCHIP):
            self._ici(self.ci ^ k, k).wait_recv()
            self._d2d(self.ci ^ k, self.r0, k).start()

    def finish(self):
        for k in range(1, NCHIP):
            self._d2d(self.ci ^ k, self.r1, k).wait_recv()
        self.finish_sends()

    def finish_sends(self):
        for k in range(1, NCHIP):
            self._ici(self.ci, k).wait_send()
            self._d2d(self.ci ^ k, self.r0, k).wait_send()


class _SmallGather:
    @staticmethod
    def sems(n):
        return [pltpu.SemaphoreType.DMA((n, 7)), pltpu.SemaphoreType.DMA((n, 7)), pltpu.SemaphoreType.DMA((n,))]

    def __init__(self, srcs, outs, send_sems, recv_sems, local_sems):
        x, y, c = _me()
        self.srcs, self.outs = list(srcs), list(outs)
        self.ss, self.rs, self.ls = send_sems, recv_sems, local_sems
        self.ci, self.c = 2 * x + y, c
        self.me = 2 * self.ci + c

    def _own(self, a, slot, rel):
        return _remote(self.srcs[a], self.outs[a].at[self.me], self.ss.at[a, slot], self.rs.at[a, slot], rel)

    def _block(self, a, idx, slot, rel):
        blk = self.outs[a].at[idx]
        return _remote(blk, blk, self.ss.at[a, slot], self.rs.at[a, slot], rel)

    def _local(self, a):
        return pltpu.make_async_copy(self.srcs[a], self.outs[a].at[self.me], self.ls.at[a])

    def start(self):
        for a in range(len(self.srcs)):
            self._local(a).start()
            self._own(a, 0, 1).start()
            for k in range(1, NCHIP):
                self._own(a, k, 2 * k).start()

    def forward(self):
        for a in range(len(self.srcs)):
            for k in range(1, NCHIP):
                idx = 2 * (self.ci ^ k) + self.c
                self._block(a, idx, k, 2 * k).wait_recv()
                self._block(a, idx, 3 + k, 1).start()

    def finish(self):
        for a in range(len(self.srcs)):
            self._block(a, 2 * self.ci + 1 - self.c, 0, 1).wait_recv()
            for k in range(1, NCHIP):
                self._block(a, 2 * (self.ci ^ k) + 1 - self.c, 3 + k, 1).wait_recv()
            self._own(a, 0, 1).wait_send()
            for k in range(1, NCHIP):
                self._own(a, k, 2 * k).wait_send()
                self._block(a, 2 * (self.ci ^ k) + self.c, 3 + k, 1).wait_send()
            self._local(a).wait()


def _start_in_proj(crow, w_ada, b_cols, w_in, pos, x, norm_pre, order):
    ts = 512
    nt = S // ts
    wc = crow.shape[1]

    def body(order_ref, crow_ref, wada_ref, b_ref, win_ref, pos_ref, freq_ref, x_ref, np_ref,
             g0_ref, mod_ref, wbf_ref, cos_ref, sin_ref, proj_ref, ht_ref,
             g0s, modp, modb, wbuf, hb_all, cs, cr, ms, mr, ws, wr, fs, fr, local_sems, ys, yr, osem):
        s, t = pl.program_id(0), pl.program_id(1)
        x, y, c = _me()
        ci = 2 * x + y
        me = 2 * ci + c
        wg = _WeightGather(win_ref, wbuf, ws, wr, fs, fr)

        @pl.when(jnp.logical_and(s == 0, t == 0))
        def _():
            wg.start(diagonal=False)
            mine = pltpu.make_async_copy(crow_ref, g0s.at[pl.ds(me, 1), :], local_sems.at[0])
            mine.start()
            csend = [_remote(crow_ref, g0s.at[pl.ds(me, 1), :], cs.at[r - 1], cr.at[r - 1], r) for r in range(1, NDEV)]
            for cp in csend:
                cp.start()
            cos_ref[...], sin_ref[...] = _cos_sin(pos_ref, freq_ref)
            for r in range(1, NDEV):
                px, py, pc = _peer(r)
                _remote(crow_ref, g0s.at[pl.ds(4 * px + 2 * py + pc, 1), :], cs.at[r - 1], cr.at[r - 1], r).wait_recv()
            mine.wait()
            cv = g0s[:, 0:D]
            sc = cv * _sigmoid(cv)
            scb = jnp.concatenate([sc, jnp.zeros_like(sc)], axis=0).astype(BF16)
            modp[...] = _dot(scb, wada_ref[...].astype(BF16))[0:NDEV, :] + b_ref[...]
            own = pltpu.make_async_copy(modp.at[pl.ds(me, 1), :], modb.at[ci], local_sems.at[1])
            own.start()
            msend = []
            for k in range(1, NCHIP):
                cp = _remote(modp.at[pl.ds(2 * (ci ^ k) + c, 1), :], modb.at[ci], ms.at[k - 1], mr.at[k - 1], 2 * k)
                cp.start()
                msend.append(cp)
            for k in range(1, NCHIP):
                _remote(modp.at[pl.ds(me, 1), :], modb.at[ci ^ k], ms.at[k - 1], mr.at[k - 1], 2 * k).wait_recv()
            own.wait()
            for j in range(NCHIP):
                mod_ref[:, j * EC:(j + 1) * EC] = modb[j]
            for cp in csend + msend:
                cp.wait_send()
            g0_ref[...] = g0s[...]

        def keep(k):
            return pltpu.make_async_copy(wbuf.at[ci ^ k], wbf_ref.at[ci ^ k], osem.at[k])

        @pl.when(jnp.logical_and(s == 1, t == 0))
        def _():
            keep(0).start()
            wg.neighbours_landed(ys, yr)
            wg.sibling_landed(1)
            keep(1).start()

        @pl.when(jnp.logical_and(s == 2, t == 0))
        def _():
            wg.sibling_landed(2)
            keep(2).start()

        @pl.when(jnp.logical_and(s == 3, t == 0))
        def _():
            wg.relay_sems = (ys, yr)
            wg.diagonal_landed()
            keep(3).start()

        rows = pl.ds(pl.multiple_of(t * ts, ts), ts)

        @pl.when(s == 0)
        def _():
            hp, _, _ = _rms_fwd(x_ref[...], np_ref[...])
            h = hp * (1.0 + mod_ref[:, D:2 * D]) + mod_ref[:, 0:D]
            hb_all[rows, :] = h.astype(BF16)
            ht_ref[...] = h.T.astype(BF16)

        proj_ref[...] = _dot(hb_all[rows, :], wbuf[ci ^ s])

        @pl.when(jnp.logical_and(s == NCHIP - 1, t == nt - 1))
        def _():
            wg.relay_sems = (ys, yr)
            wg.finish_relayed()
            for k in range(NCHIP):
                keep(k).wait()

    vm = pl.BlockSpec(memory_space=pltpu.VMEM)
    first_pass = lambda s, t: jnp.where(s == 0, t, nt - 1)
    grid_spec = pltpu.PrefetchScalarGridSpec(
        num_scalar_prefetch=1, grid=(NCHIP, nt),
        in_specs=[vm, vm, vm, vm, vm, vm, pl.BlockSpec((ts, D), lambda s, t, o: (first_pass(s, t), 0)),
                  pl.BlockSpec((1, D), lambda s, t, o: (0, 0))],
        out_specs=[vm, vm, pl.BlockSpec(memory_space=pl.ANY), vm, vm, pl.BlockSpec((ts, EC), lambda s, t, o: (t, o[s])),
                   pl.BlockSpec((D, ts), lambda s, t, o: (0, first_pass(s, t)))],
        scratch_shapes=[pltpu.VMEM((NDEV, wc), F32), pltpu.VMEM((NDEV, EC), F32), pltpu.VMEM((NCHIP, 1, EC), F32),
                        pltpu.VMEM((NCHIP, D, EC), BF16), pltpu.VMEM((S, D), BF16),
                        pltpu.SemaphoreType.DMA((NDEV - 1,)), pltpu.SemaphoreType.DMA((NDEV - 1,)),
                        pltpu.SemaphoreType.DMA((NCHIP - 1,)), pltpu.SemaphoreType.DMA((NCHIP - 1,))]
        + _WeightGather.SEMS + [pltpu.SemaphoreType.DMA((2,))] * 3 + [pltpu.SemaphoreType.DMA((NCHIP,))])
    return pl.pallas_call(
        body, name="start_in_proj", grid_spec=grid_spec,
        out_shape=[jax.ShapeDtypeStruct((NDEV, wc), F32), jax.ShapeDtypeStruct((1, 3 * D), F32),
                   jax.ShapeDtypeStruct((NCHIP, D, EC), BF16), jax.ShapeDtypeStruct((S, LANES), F32),
                   jax.ShapeDtypeStruct((S, LANES), F32), jax.ShapeDtypeStruct((S, E), F32),
                   jax.ShapeDtypeStruct((D, S), BF16)],
        compiler_params=_cp(("arbitrary", "arbitrary")),
    )(order, crow, w_ada, b_cols, w_in, pos, _rope_freq(), x, norm_pre)


class _ReduceScatter:
    @staticmethod
    def scratch(n_units, rows, ucols, max_owned):
        half = rows // 2
        return [pltpu.VMEM((n_units, half, ucols), F32), pltpu.VMEM((n_units, half, ucols), BF16),
                pltpu.VMEM((max_owned, NCHIP, half, ucols), BF16),
                pltpu.SemaphoreType.DMA((2,)), pltpu.SemaphoreType.DMA((n_units,)),
                pltpu.SemaphoreType.DMA((n_units, NCHIP)), pltpu.SemaphoreType.DMA((n_units,)),
                pltpu.SemaphoreType.DMA((n_units,))]

    def __init__(self, g_ref, out_ref, units, sib, stage, got, sem1, send2, recv2, send3, recv3):
        x, y, c = _me()
        self.c, self.ci = c, 2 * x + y
        self.g, self.out, self.units = g_ref, out_ref, units
        self.sib, self.stage, self.got = sib, stage, got
        self.sem1, self.send2, self.recv2, self.send3, self.recv3 = sem1, send2, recv2, send3, recv3
        self.half = g_ref.shape[1] // 2
        self.ucols = g_ref.shape[2]
        self.r0 = pl.multiple_of(c * self.half, self.half)
        self.r1 = pl.multiple_of((1 - c) * self.half, self.half)
        self.slot0 = units[0][0]
        assert [u[0] for u in units] == list(range(self.slot0, self.slot0 + len(units)))
        seen = {}
        self.local = []
        for _, owner, _ in units:
            self.local.append(seen.get(owner, 0))
            seen[owner] = seen.get(owner, 0) + 1

    def _halves(self):
        n = len(self.units)
        return _remote(self.g.at[pl.ds(self.slot0, n), pl.ds(self.r1, self.half), :], self.sib,
                       self.sem1.at[0], self.sem1.at[1], 1)

    def _partial(self, i, sender):
        _, owner, _ = self.units[i]
        return pltpu.make_async_remote_copy(
            src_ref=self.stage.at[i], dst_ref=self.got.at[self.local[i], sender],
            send_sem=self.send2.at[i], recv_sem=self.recv2.at[i, sender],
            device_id=(owner // 2, owner % 2, self.c), device_id_type=MESH)

    def _back(self, i, start):
        off = self.units[i][2]
        blk = self.out.at[pl.ds(start, self.half), off:off + self.ucols]
        return _remote(blk, blk, self.send3.at[i], self.recv3.at[i], 1)

    def start_halves(self):
        self._halves().start()

    def send_partials(self):
        self._halves().wait_recv()
        for i, (slot, owner, _) in enumerate(self.units):
            @pl.when(self.ci != owner)
            def _():
                self.stage[i] = (self.g[slot, pl.ds(self.r0, self.half), :] + self.sib[i]).astype(BF16)
                self._partial(i, self.ci).start()

    def reduce_owned(self):
        for i, (slot, owner, off) in enumerate(self.units):
            @pl.when(self.ci == owner)
            def _():
                rows, cols = pl.ds(self.r0, self.half), slice(off, off + self.ucols)
                self.out[rows, cols] = self.g[slot, pl.ds(self.r0, self.half), :] + self.sib[i]
                for s in range(NCHIP):
                    if s != owner:
                        self._partial(i, s).wait_recv()
                        self.out[rows, cols] += self.got[self.local[i], s].astype(F32)
                self._back(i, self.r0).start()

    def finish(self):
        self._halves().wait_send()
        for i, (_, owner, _) in enumerate(self.units):
            @pl.when(self.ci == owner)
            def _():
                self._back(i, self.r1).wait_recv()
                self._back(i, self.r0).wait_send()

            @pl.when(self.ci != owner)
            def _():
                self._partial(i, self.ci).wait_send()


def _silu_rows(c_ref):
    cv = c_ref[...]
    sc = cv * _sigmoid(cv)
    return jnp.concatenate([sc, jnp.zeros_like(sc)], axis=0).astype(BF16)


def _adamw(w, g, m, v, name):
    rows, cols = w.shape
    tr = 256 if rows % 256 == 0 else rows

    def body(w_ref, g_ref, m_ref, v_ref, d_ref, nm_ref, nv_ref):
        d_ref[...], nm_ref[...], nv_ref[...] = _adamw_values(w_ref[...], g_ref[...], m_ref[...], v_ref[...])

    spec = pl.BlockSpec((tr, cols), lambda i: (i, 0))
    return pl.pallas_call(
        body, name=name, grid=(rows // tr,), in_specs=[spec] * 4, out_specs=[spec] * 3,
        out_shape=[jax.ShapeDtypeStruct((rows, cols), F32)] * 3,
        compiler_params=_cp(("parallel",)),
    )(w, g, m, v)


def _adamw_values(w, g, m, v):
    nm = B1 * m + (1.0 - B1) * g
    nv = B2 * v + (1.0 - B2) * (g * g)
    m_hat = nm / (1.0 - B1 ** STEP)
    v_hat = nv / (1.0 - B2 ** STEP)
    return (-LR) * (m_hat / (jnp.sqrt(v_hat) + ADAM_EPS) + WD * w), nm, nv


NB = R // HEAD
SMALL = (("b_ada", (1, 3 * D)), ("norm_pre", (1, D)), ("norm_post", (1, D)), ("conv_w", (4, R // NCHIP)),
         ("conv_b", (1, R)), ("w_rg_a", (NB, HEAD, HEAD)), ("b_rg_a", (1, R)), ("w_rg_x", (NB, HEAD, HEAD)),
         ("b_rg_x", (1, R)), ("lru_lambda", (1, R)), ("norm_rec", (1, R)), ("norm_att", (1, R)))


def _small_update(ao8, sm8, dwa8, dwx8, ai8, cg, params):
    n = len(SMALL)

    def body(ao_ref, sm_ref, dwa_ref, dwx_ref, ai_ref, cg_ref, *refs):
        pin, pout, (gada_ref, loss_ref, dmod) = refs[:3 * n], refs[3 * n:7 * n], refs[7 * n:]
        xx, yy, _ = _me()
        ci = 2 * xx + yy

        def total(ref, *idx):
            acc = ref[(0,) + idx].astype(F32)
            for d in range(1, NDEV):
                acc = acc + ref[(d,) + idx].astype(F32)
            return acc

        row = lambda ref, r, lanes=slice(None): total(ref, slice(r, r + 1), lanes)
        mine = lambda parts: sum(jnp.where(ci == j, part, 0.0) for j, part in enumerate(parts))
        cw = R // NCHIP
        grads = {
            "b_ada": [jnp.concatenate([row(ai_ref, 0), row(ai_ref, 1), row(ao_ref, 0)], axis=1)],
            "norm_pre": [row(ai_ref, 2)], "norm_post": [row(ao_ref, 1)],
            "conv_w": [mine([row(sm_ref, 8 + r, slice(j * cw, (j + 1) * cw)) for j in range(NCHIP)]) for r in range(4)],
            "conv_b": [row(sm_ref, 4)], "b_rg_a": [row(sm_ref, 0)], "b_rg_x": [row(sm_ref, 1)],
            "lru_lambda": [row(sm_ref, 2)], "norm_rec": [row(sm_ref, 3)], "norm_att": [row(ao_ref, 2, slice(0, R))],
            "w_rg_a": [total(dwa_ref, h) for h in range(NB)], "w_rg_x": [total(dwx_ref, h) for h in range(NB)],
        }
        loss_ref[...] = row(ao_ref, 3, slice(0, LANES)) * (0.5 / D)
        for k, (name, shape) in enumerate(SMALL):
            w_ref, m_ref, v_ref = pin[3 * k:3 * k + 3]
            outs = pout[4 * k:4 * k + 4]
            for r, g in enumerate(grads[name]):
                at = (slice(None),) if len(grads[name]) == 1 else ((r,) if len(shape) == 3 else (slice(r, r + 1),))
                res = (g,) + _adamw_values(w_ref[at], g, m_ref[at], v_ref[at])
                for o_ref, val in zip(outs, res):
                    o_ref[at] = val
        for d in range(NDEV):
            dmod[d:d + 1, :] = jnp.concatenate([ai_ref[d, 0:1, :], ai_ref[d, 1:2, :], ao_ref[d, 0:1, :]], axis=1)
        cols = mine([dmod[:, j * EC:(j + 1) * EC] for j in range(NCHIP)])
        colsb = jnp.concatenate([cols, jnp.zeros_like(cols)], axis=0).astype(BF16)
        gada_ref[...] = _dot_tn(_silu_rows(cg_ref), colsb)

    shapes = [jax.ShapeDtypeStruct(s, F32) for _, s in SMALL]
    outs = pl.pallas_call(
        body, name="small_update",
        out_shape=[s for s in shapes for _ in range(4)] + [jax.ShapeDtypeStruct((D, EC), F32),
                                                           jax.ShapeDtypeStruct((1, LANES), F32)],
        scratch_shapes=[pltpu.VMEM((NDEV, 3 * D), F32)],
        compiler_params=_cp(),
    )(ao8, sm8, dwa8, dwx8, ai8, cg, *params)
    return outs[:4 * n], outs[4 * n], outs[4 * n + 1]


BIG = ("w_ada", "w_in", "w_out")
WEIGHTS = ("w_ada", "b_ada", "norm_pre", "norm_post", "w_in", "conv_w", "conv_b", "w_rg_a", "b_rg_a", "w_rg_x",
           "b_rg_x", "lru_lambda", "norm_rec", "norm_att", "w_out")


def kernel(x, c, positions, w_ada, b_ada, norm_pre, norm_post, w_in, conv_w, conv_b, w_rg_a, b_rg_a, w_rg_x, b_rg_x, lru_lambda, norm_rec, norm_att, w_out, loss_target, m_w_ada, m_b_ada, m_norm_pre, m_norm_post, m_w_in, m_conv_w, m_conv_b, m_w_rg_a, m_b_rg_a, m_w_rg_x, m_b_rg_x, m_lru_lambda, m_norm_rec, m_norm_att, m_w_out, v_w_ada, v_b_ada, v_norm_pre, v_norm_post, v_w_in, v_conv_w, v_conv_b, v_w_rg_a, v_b_rg_a, v_w_rg_x, v_b_rg_x, v_lru_lambda, v_norm_rec, v_norm_att, v_w_out):
    given = dict(locals())
    wts = {n: given[n] for n in WEIGHTS}
    ms = {n: given["m_" + n] for n in WEIGHTS}
    vs = {n: given["v_" + n] for n in WEIGHTS}
    xi, yi, _ = _me()
    chip = 2 * xi + yi
    cw_loc = R // NCHIP

    b_cols = lax.dynamic_slice(b_ada, (0, chip * EC), (1, EC))
    order = (chip ^ jnp.arange(NCHIP, dtype=jnp.int32)).astype(jnp.int32)
    g0, mod, w_in_bf, cos, sin, proj, ht = _start_in_proj(
        jnp.concatenate([c, conv_w.reshape(1, 4 * cw_loc)], axis=1), w_ada[0], b_cols, w_in[0],
        positions.reshape(S, 1), x[0], norm_pre, order)
    cg = g0[:, 0:D]
    conv_full = g0[0::2, D:].reshape(NCHIP, 4, cw_loc).transpose(1, 0, 2).reshape(4, R)

    p = dict(norm_pre=norm_pre, norm_post=norm_post, conv_b=conv_b, b_rg_a=b_rg_a, b_rg_x=b_rg_x,
             lru_lambda=lru_lambda, norm_rec=norm_rec, norm_att=norm_att, w_rg_a=w_rg_a[0], w_rg_x=w_rg_x[0])
    grad_x, g_in, g_out, gathered = _local_step(
        x[0], cos, sin, loss_target[0], mod, w_in_bf, proj, ht, w_out[0], conv_full, p)

    params = [d[n].reshape(shape) for n, shape in SMALL for d in (wts, ms, vs)]
    small_out, g_ada, loss_row = _small_update(*gathered, cg, params)
    grads = {"w_out": g_out, "w_in": g_in, "w_ada": g_ada}
    delta, new_m, new_v = {}, {}, {}
    for k, (n, _) in enumerate(SMALL):
        grads[n], delta[n], new_m[n], new_v[n] = small_out[4 * k:4 * k + 4]
    for n in BIG:
        delta[n], new_m[n], new_v[n] = _adamw(wts[n][0], grads[n], ms[n][0], vs[n][0], "adamw_" + n)
    out = lambda d: [d[n].reshape(wts[n].shape) for n in WEIGHTS]
    return (loss_row[0, 0], grad_x.reshape(x.shape), *out(grads), *out(delta), *out(new_m), *out(new_v))
```

```python
import numpy as np
import jax
import jax.numpy as jnp
from jax import lax
from jax.experimental import pallas as pl
from jax.experimental.pallas import tpu as pltpu

F32 = jnp.float32
BF16 = jnp.bfloat16

S = 2048
D = 1024
E = 3072
R = 512
NDEV = 8
NCHIP = 4
EC = 768
LRU_C = 8.0
EPS = 1e-6
NEG = -1e30
HEAD = 64
BLK = 128
PATTERNS = (1, 4, 16)
ROPE_THETA = 10000.0
LANES = 128
VMEM_LIMIT = 56 * 1024 * 1024

B1, B2, LR, WD, ADAM_EPS, STEP = 0.9, 0.999, 0.001, 0.01, 1e-8, 10
MESH = pl.DeviceIdType.MESH


def _cp(sem=None, **kw):
    return pltpu.CompilerParams(dimension_semantics=sem, vmem_limit_bytes=VMEM_LIMIT, **kw)


def _dot(a, b):
    return jnp.dot(a, b, preferred_element_type=F32)


def _dot_nt(a, b):
    return lax.dot_general(a, b, (((1,), (1,)), ((), ())), preferred_element_type=F32)


def _dot_tn(a, b):
    return lax.dot_general(a, b, (((0,), (0,)), ((), ())), preferred_element_type=F32)


def _sigmoid(x):
    return 1.0 / (1.0 + jnp.exp(-x))


def _expm1(x):
    poly = x * (1.0 + x * (0.5 + x * (1.0 / 6 + x * (1.0 / 24 + x * (1.0 / 120 + x * (1.0 / 720))))))
    return jnp.where(jnp.abs(x) < 0.3, poly, jnp.exp(x) - 1.0)


def _rms_fwd(v, g):
    rstd = lax.rsqrt(jnp.mean(v * v, axis=-1, keepdims=True) + EPS)
    vn = v * rstd
    return vn * g, vn, rstd


def _rms_bwd(dy, vn, rstd, g):
    dvn = dy * g
    dv = rstd * (dvn - vn * jnp.mean(dvn * vn, axis=-1, keepdims=True))
    return dv, jnp.sum(dy * vn, axis=0, keepdims=True)


RT = 256


def _shift_down(cur, prev8, j, row):
    if j == 0:
        return cur
    top = jnp.tile(pltpu.roll(prev8, j, 0), (RT // 8, 1))
    return jnp.where(row >= j, pltpu.roll(cur, j, 0), top)


def _shift_up(cur, next8, j, row):
    if j == 0:
        return cur
    bot = jnp.tile(pltpu.roll(next8, 8 - j, 0), (RT // 8, 1))
    return jnp.where(row < RT - j, pltpu.roll(cur, RT - j, 0), bot)


def _rec_gates(xp, xprev8, row, cw_ref, cb_ref, wa_ref, ba_ref, wx_ref, bx_ref, lam_ref):
    xa = cb_ref[...] + sum(cw_ref[3 - j:4 - j, :] * _shift_down(xp, xprev8, j, row) for j in range(4))
    xab = xa.astype(BF16)
    r = _sigmoid(_dot(xab, wa_ref[...]) + ba_ref[...])
    ig = _sigmoid(_dot(xab, wx_ref[...]) + bx_ref[...])
    nl = -lam_ref[...]
    sp = jnp.maximum(nl, 0.0) + jnp.log1p(jnp.exp(-jnp.abs(nl)))
    la = (-LRU_C) * r * sp
    a = jnp.exp(la)
    mult = jnp.sqrt(-_expm1(2.0 * la))
    return dict(xa=xa, xab=xab, r=r, ig=ig, sp=sp, la=la, a=a, mult=mult)


def _scan_fwd(a, u, row):
    sh = 1
    while sh < RT:
        a_s = jnp.where(row >= sh, pltpu.roll(a, sh, 0), 1.0)
        u_s = jnp.where(row >= sh, pltpu.roll(u, sh, 0), 0.0)
        u = a * u_s + u
        a = a * a_s
        sh *= 2
    return a, u


def _scan_bwd(al, g, row):
    sh = 1
    while sh < RT:
        al_s = jnp.where(row < RT - sh, pltpu.roll(al, RT - sh, 0), 1.0)
        g_s = jnp.where(row < RT - sh, pltpu.roll(g, RT - sh, 0), 0.0)
        g = g + al * g_s
        al = al * al_s
        sh *= 2
    return g


def _dense_from_blocks(blocks_ref, dense_ref):
    dense_ref[...] = jnp.zeros_like(dense_ref)
    for h in range(R // HEAD):
        dense_ref[h * HEAD:(h + 1) * HEAD, h * HEAD:(h + 1) * HEAD] = blocks_ref[h].astype(dense_ref.dtype)


def _rec_fwd(proj, conv_w, conv_b, wa_b, ba, wx_b, bx, lam, norm_rec):
    nt = S // RT

    def body(p_ref, cw_ref, cb_ref, wa_ref, ba_ref, wx_ref, bx_ref, lam_ref, nr_ref,
             h_ref, ya_ref, prev8, hc, wad, wxd):
        i = pl.program_id(0)

        @pl.when(i == 0)
        def _():
            prev8[...] = jnp.zeros_like(prev8)
            hc[...] = jnp.zeros_like(hc)
            _dense_from_blocks(wa_ref, wad)
            _dense_from_blocks(wx_ref, wxd)

        row = lax.broadcasted_iota(jnp.int32, (RT, R), 0)
        xp = p_ref[:, 0:R]
        ga = p_ref[:, R:2 * R]
        f = _rec_gates(xp, prev8[...], row, cw_ref, cb_ref, wad, ba_ref, wxd, bx_ref, lam_ref)
        u = f["mult"] * (f["ig"] * f["xa"])
        acum, hh = _scan_fwd(f["a"], u, row)
        h = hh + acum * hc[0:1, :]
        h_ref[...] = h
        hc[0:1, :] = h_ref[RT - 1:RT, :]
        prev8[...] = p_ref[RT - 8:RT, 0:R]
        yp = h * (ga * _sigmoid(ga))
        ya, _, _ = _rms_fwd(yp, nr_ref[...])
        ya_ref[...] = ya.astype(BF16)

    row1 = lambda n: pl.BlockSpec((1, n), lambda i: (0, 0))
    blocks = pl.BlockSpec((R // HEAD, HEAD, HEAD), lambda i: (0, 0, 0))
    return pl.pallas_call(
        body, name="rec_fwd", grid=(nt,),
        in_specs=[pl.BlockSpec((RT, 2 * R), lambda i: (i, 0)), pl.BlockSpec((4, R), lambda i: (0, 0)), row1(R),
                  blocks, row1(R), blocks, row1(R), row1(R), row1(R)],
        out_specs=[pl.BlockSpec((RT, R), lambda i: (i, 0)), pl.BlockSpec((RT, R), lambda i: (i, 0))],
        out_shape=[jax.ShapeDtypeStruct((S, R), F32), jax.ShapeDtypeStruct((S, R), BF16)],
        scratch_shapes=[pltpu.VMEM((8, R), F32), pltpu.VMEM((8, R), F32), pltpu.VMEM((R, R), BF16),
                        pltpu.VMEM((R, R), BF16)],
        compiler_params=_cp(("arbitrary",)),
    )(proj, conv_w, conv_b, wa_b, ba, wx_b, bx, lam, norm_rec)


def _rec_bwd(dproj, d_ya, proj, h_all, conv_w, conv_b, wa_b, ba, wx_b, bx, lam, norm_rec):
    nt = S // RT

    def body(dp_in, dya_ref, p_ref, pprev_ref, h_ref, hprev_ref, cw_ref, cb_ref, wab_ref, ba_ref, wxb_ref, bx_ref,
             lam_ref, nr_ref, dp_ref, dwab_ref, dwxb_ref, sm_ref, nxt8, cg, wa_ref, wx_ref, dwa_ref, dwx_ref):
        i = pl.program_id(0)
        ti = nt - 1 - i

        @pl.when(i == 0)
        def _():
            nxt8[...] = jnp.zeros_like(nxt8)
            cg[...] = jnp.zeros_like(cg)
            dwa_ref[...] = jnp.zeros_like(dwa_ref)
            dwx_ref[...] = jnp.zeros_like(dwx_ref)
            sm_ref[...] = jnp.zeros_like(sm_ref)
            _dense_from_blocks(wab_ref, wa_ref)
            _dense_from_blocks(wxb_ref, wx_ref)

        row = lax.broadcasted_iota(jnp.int32, (RT, R), 0)
        first = (ti > 0).astype(F32)
        xprev8 = pprev_ref[...] * first
        hprev8 = hprev_ref[...] * first
        xp = p_ref[:, 0:R]
        ga = p_ref[:, R:2 * R]
        f = _rec_gates(xp, xprev8, row, cw_ref, cb_ref, wa_ref, ba_ref, wx_ref, bx_ref, lam_ref)
        xa, r, ig, a, mult = f["xa"], f["r"], f["ig"], f["a"], f["mult"]
        h = h_ref[...]
        sg = _sigmoid(ga)
        gate = ga * sg
        yp = h * gate
        _, ypn, rstd = _rms_fwd(yp, nr_ref[...])
        d_yp, dnr = _rms_bwd(dya_ref[...], ypn, rstd, nr_ref[...])
        d_ga = d_yp * h * (sg * (1.0 + ga * (1.0 - sg)))
        dh = d_yp * gate + jnp.where(row == RT - 1, cg[0:1, :], 0.0)
        al = jnp.where(row < RT - 1, pltpu.roll(a, RT - 1, 0), 0.0)
        g = _scan_bwd(al, dh, row)
        cg[0:1, :] = jnp.sum(jnp.where(row == 0, a * g, 0.0), axis=0, keepdims=True)
        h_m1 = _shift_down(h, hprev8, 1, row)
        da = g * h_m1
        ix = ig * xa
        d_mult = g * ix
        d_ig = g * mult * xa
        d_xa = g * mult * ig
        d_la = da * a - d_mult * (a * a) / mult
        d_r = d_la * ((-LRU_C) * f["sp"])
        dsp = jnp.sum(d_la * ((-LRU_C) * r), axis=0, keepdims=True)
        dlam = dsp * (-_sigmoid(-lam_ref[...]))
        d_za = d_r * r * (1.0 - r)
        d_zx = d_ig * ig * (1.0 - ig)
        dzab = d_za.astype(BF16)
        dzxb = d_zx.astype(BF16)
        dwa_ref[...] += _dot_tn(f["xab"], dzab)
        dwx_ref[...] += _dot_tn(f["xab"], dzxb)
        d_xa = d_xa + _dot_nt(dzab, wa_ref[...]) + _dot_nt(dzxb, wx_ref[...])
        d_xp = sum(cw_ref[3 - j:4 - j, :] * _shift_up(d_xa, nxt8[...], j, row) for j in range(4))
        dcw = [jnp.sum(d_xa * _shift_down(xp, xprev8, 3 - k, row), axis=0, keepdims=True) for k in range(4)]
        dp_ref[:, 0:R] = d_xp.astype(BF16)
        dp_ref[:, R:2 * R] = d_ga.astype(BF16)
        dp8 = d_xa[0:8, :]
        nxt8[...] = dp8
        sm_ref[0:1, :] += jnp.sum(d_za, axis=0, keepdims=True)
        sm_ref[1:2, :] += jnp.sum(d_zx, axis=0, keepdims=True)
        sm_ref[2:3, :] += dlam
        sm_ref[3:4, :] += dnr
        sm_ref[4:5, :] += jnp.sum(d_xa, axis=0, keepdims=True)
        for k in range(4):
            sm_ref[8 + k:9 + k, :] += dcw[k]

        @pl.when(i == nt - 1)
        def _():
            for h in range(R // HEAD):
                dwab_ref[h] = dwa_ref[h * HEAD:(h + 1) * HEAD, h * HEAD:(h + 1) * HEAD].astype(BF16)
                dwxb_ref[h] = dwx_ref[h * HEAD:(h + 1) * HEAD, h * HEAD:(h + 1) * HEAD].astype(BF16)

    c0 = lambda shape: pl.BlockSpec(shape, lambda i: (0, 0))
    blocks = pl.BlockSpec((R // HEAD, HEAD, HEAD), lambda i: (0, 0, 0))
    rev = lambda i: nt - 1 - i
    prev8 = lambda i: (jnp.maximum((nt - 1 - i) * (RT // 8) - 1, 0), 0)
    return pl.pallas_call(
        body, name="rec_bwd", grid=(nt,),
        in_specs=[pl.BlockSpec(memory_space=pl.ANY),
                  pl.BlockSpec((RT, R), lambda i: (rev(i), 0)),
                  pl.BlockSpec((RT, 2 * R), lambda i: (rev(i), 0)), pl.BlockSpec((8, R), prev8),
                  pl.BlockSpec((RT, R), lambda i: (rev(i), 0)), pl.BlockSpec((8, R), prev8),
                  c0((4, R)), c0((1, R)), blocks, c0((1, R)), blocks, c0((1, R)), c0((1, R)), c0((1, R))],
        out_specs=[pl.BlockSpec((RT, 2 * R), lambda i: (rev(i), 0)), blocks, blocks, c0((16, R))],
        out_shape=[jax.ShapeDtypeStruct((S, E), BF16), jax.ShapeDtypeStruct((R // HEAD, HEAD, HEAD), BF16),
                   jax.ShapeDtypeStruct((R // HEAD, HEAD, HEAD), BF16), jax.ShapeDtypeStruct((16, R), F32)],
        scratch_shapes=[pltpu.VMEM((8, R), F32), pltpu.VMEM((8, R), F32), pltpu.VMEM((R, R), BF16),
                        pltpu.VMEM((R, R), BF16), pltpu.VMEM((R, R), F32), pltpu.VMEM((R, R), F32)],
        input_output_aliases={0: 0},
        compiler_params=_cp(("arbitrary",)),
    )(dproj, d_ya, proj, proj, h_all, h_all, conv_w, conv_b, wa_b, ba, wx_b, bx, lam, norm_rec)


NPAIR = R // LANES
QB, KB, VB, GB = 2 * R // LANES, 3 * R // LANES, 4 * R // LANES, 5 * R // LANES


def _rope_freq():
    half = HEAD // 2
    inv = np.float32(ROPE_THETA) ** (-(np.arange(half, dtype=np.float32) / np.float32(half)))
    return jnp.asarray(np.tile(inv.astype(np.float32), LANES // half)[None, :])


def _rot_half(x, first):
    return jnp.where(first, -pltpu.roll(x, LANES - HEAD // 2, 1), pltpu.roll(x, HEAD // 2, 1))


def _cos_sin(pos_ref, freq_ref):
    ang = pos_ref[...].astype(F32) * freq_ref[...]
    return jnp.cos(ang), jnp.sin(ang)


def _deint(src_ref, dst_ref, d):
    n = S // d
    for r in range(d):
        v = src_ref[pl.ds(r, n, stride=d), :] if d > 1 else src_ref[...]
        dst_ref[r * n:(r + 1) * n, :] = v.astype(dst_ref.dtype)


def _reint(src_ref, dst_ref, d, accumulate):
    n = S // d
    for r in range(d):
        idx = (pl.ds(r, n, stride=d), slice(None)) if d > 1 else (slice(None), slice(None))
        v = src_ref[r * n:(r + 1) * n, :]
        if accumulate:
            dst_ref[idx] = dst_ref[idx] + v
        else:
            dst_ref[idx] = v


def _deint_heads(src_ref, dst0, dst1, d):
    n = S // d
    hm0 = lax.broadcasted_iota(jnp.int32, (n, LANES), 1) < HEAD
    for r in range(d):
        v = src_ref[pl.ds(r, n, stride=d), :] if d > 1 else src_ref[...]
        dst0[r * n:(r + 1) * n, :] = jnp.where(hm0, v, 0.0).astype(BF16)
        dst1[r * n:(r + 1) * n, :] = jnp.where(hm0, 0.0, v).astype(BF16)


def _reint_prev(src_ref, dst_ref, d):
    n = S // d
    if n == BLK:
        return
    for r in range(d):
        idx = (pl.ds(r, n - BLK, stride=d), slice(None)) if d > 1 else (slice(0, n - BLK), slice(None))
        dst_ref[idx] = dst_ref[idx] + src_ref[r * n + BLK:(r + 1) * n, :]


def _pair_masks():
    qi = lax.broadcasted_iota(jnp.int32, (BLK, 2 * BLK), 0)
    ki = lax.broadcasted_iota(jnp.int32, (BLK, 2 * BLK), 1) & (BLK - 1)
    return ki <= qi, ki >= qi


def _two(ref0, ref1, st, axis):
    return jnp.concatenate([ref0[pl.ds(st, BLK), :], ref1[pl.ds(st, BLK), :]], axis=axis)


ATT_UNROLL = 8


def _att_fwd(proj, cos, sin, w_out):
    def body(q_ref, k_ref, v_ref, cos_ref, sin_ref, w_ref, att_ref, qr_ref, kr_ref, lse_ref, wbf_ref,
             qd, kd0, kd1, vd0, vd1, od, ld, on, ln, wbuf, *wsems):
        wg = _WeightGather(w_ref, wbuf, *wsems)
        pl.when(pl.program_id(0) == 0)(wg.start)
        pl.when(pl.program_id(0) == 1)(wg.forward)
        lane = lax.broadcasted_iota(jnp.int32, (S, LANES), 1)
        first = (lane & (HEAD // 2)) == 0
        cos, sin = cos_ref[...], sin_ref[...]
        q = q_ref[...]
        k = k_ref[...]
        qr_ref[...] = (q * cos + _rot_half(q, first) * sin) * (HEAD ** -0.5)
        kr_ref[...] = k * cos + _rot_half(k, first) * sin
        hm0 = lax.broadcasted_iota(jnp.int32, (BLK, LANES), 1) < HEAD
        top = lax.broadcasted_iota(jnp.int32, (2 * BLK, LANES), 0) < BLK
        ones2 = (top == (lax.broadcasted_iota(jnp.int32, (2 * BLK, LANES), 1) < HEAD)).astype(BF16)
        mc2, mp2 = _pair_masks()

        for pi, d in enumerate(PATTERNS):
            nb = S // d // BLK
            _deint(qr_ref, qd, d)
            _deint_heads(kr_ref, kd0, kd1, d)
            _deint_heads(v_ref, vd0, vd1, d)

            def blk(b, carry):
                st = pl.multiple_of(b * BLK, BLK)
                qb = qd[pl.ds(st, BLK), :]
                sc = jnp.where(mc2, _dot_nt(qb, _two(kd0, kd1, st, 0)), NEG)
                mx = sc
                if nb > 1:
                    stp = pl.multiple_of(jnp.maximum(b - 1, 0) * BLK, BLK)
                    mp = jnp.logical_and(mp2, lax.rem(b, nb) != 0)
                    sp = jnp.where(mp, _dot_nt(qb, _two(kd0, kd1, stp, 0)), NEG)
                    mx = jnp.maximum(sc, sp)
                m0 = jnp.max(mx[:, 0:BLK], axis=1, keepdims=True)
                m1 = jnp.max(mx[:, BLK:2 * BLK], axis=1, keepdims=True)
                mf = jnp.concatenate([jnp.broadcast_to(m0, (BLK, BLK)), jnp.broadcast_to(m1, (BLK, BLK))], axis=1)
                o = _dot(jnp.exp(sc - mf).astype(BF16), jnp.concatenate([_two(vd0, vd1, st, 0), ones2], axis=1))
                if nb > 1:
                    o = o + _dot(jnp.exp(sp - mf).astype(BF16), jnp.concatenate([_two(vd0, vd1, stp, 0), ones2], axis=1))
                l = o[:, LANES:2 * LANES]
                od[pl.ds(st, BLK), :] = o[:, 0:LANES] / l
                ld[pl.ds(st, BLK), :] = jnp.where(hm0, m0, m1) + jnp.log(l)
                return carry

            lax.fori_loop(0, S // BLK, blk, 0, unroll=ATT_UNROLL)
            _reint(od, on.at[pi], d, False)
            _reint(ld, ln.at[pi], d, False)

        l0, l1, l2 = ln[0], ln[1], ln[2]
        m = jnp.maximum(jnp.maximum(l0, l1), l2)
        e0, e1, e2 = jnp.exp(l0 - m), jnp.exp(l1 - m), jnp.exp(l2 - m)
        den = e0 + e1 + e2
        att_ref[...] = (e0 * on[0] + e1 * on[1] + e2 * on[2]) / den
        lse_ref[...] = m + jnp.log(den)

        @pl.when(pl.program_id(0) == NPAIR - 1)
        def _():
            wg.finish()
            wbf_ref[...] = wbuf[...]

    col = lambda c0: pl.BlockSpec((S, LANES), lambda p: (0, c0 + p))
    out = pl.BlockSpec((S, LANES), lambda p: (0, p))
    tab = pl.BlockSpec((S, LANES), lambda p: (0, 0))
    vm = pl.BlockSpec(memory_space=pltpu.VMEM)
    return pl.pallas_call(
        body, name="att_fwd", grid=(NPAIR,),
        in_specs=[col(QB), col(KB), col(VB), tab, tab, vm],
        out_specs=[out, out, out, out, vm],
        out_shape=[jax.ShapeDtypeStruct((S, R), F32)] * 4 + [jax.ShapeDtypeStruct((NCHIP,) + w_out.shape, BF16)],
        scratch_shapes=[pltpu.VMEM((S, LANES), BF16)] * 5 + [pltpu.VMEM((S, LANES), F32)] * 2
        + [pltpu.VMEM((3, S, LANES), F32)] * 2 + [pltpu.VMEM((NCHIP,) + w_out.shape, BF16)] + _WeightGather.SEMS,
        compiler_params=_cp(("arbitrary",)),
    )(proj, proj, proj, cos, sin, w_out)


def _att_bwd(dproj, d_att, att, lse, qr, kr, proj, cos, sin, gw_out4):
    out_units = [(j, j, 0) for j in range(NCHIP)]

    nblk = S // BLK

    def body(dp_in, do_ref, o_ref, lse_ref, qr_ref, kr_ref, v_ref, cos_ref, sin_ref, gw_ref, dp_ref, gout_ref,
             qd, kd0, kd1, vd0, vd1, dod, kt, packn, packd, dqd, dkcd, dkpd, dvcd, dvpd,
             dqn, dkn, dvn, rows, trs, pts, dss, stage, sems, gred, *rs_scratch):
        p = pl.program_id(0)
        rs = _ReduceScatter(gw_ref, gred, out_units, *rs_scratch)
        for step, piece in enumerate((rs.start_halves, rs.send_partials, rs.reduce_owned)):
            pl.when(p == step)(piece)

        @pl.when(p == NPAIR - 1)
        def _():
            rs.finish()
            gout_ref[...] = gred[...]

        lane = lax.broadcasted_iota(jnp.int32, (S, LANES), 1)
        hms = lane < HEAD
        prod = do_ref[...] * o_ref[...]
        d0 = jnp.sum(jnp.where(hms, prod, 0.0), axis=1, keepdims=True)
        d1 = jnp.sum(jnp.where(hms, 0.0, prod), axis=1, keepdims=True)
        lse = lse_ref[...]
        quarter = HEAD // 2
        packn[...] = jnp.where(lane < quarter, lse,
                               jnp.where(hms, pltpu.roll(lse, LANES - quarter, 1), jnp.where(lane < 3 * quarter, d0, d1)))
        dqn[...] = jnp.zeros_like(dqn)
        dkn[...] = jnp.zeros_like(dkn)
        dvn[...] = jnp.zeros_like(dvn)
        hm0 = lax.broadcasted_iota(jnp.int32, (BLK, LANES), 1) < HEAD
        key = lax.broadcasted_iota(jnp.int32, (2 * BLK, BLK), 0) & (BLK - 1)
        qry = lax.broadcasted_iota(jnp.int32, (2 * BLK, BLK), 1)
        mct, mpt = key <= qry, key >= qry

        for d in PATTERNS:
            nb = S // d // BLK
            _deint(qr_ref, qd, d)
            _deint_heads(kr_ref, kd0, kd1, d)
            _deint_heads(v_ref, vd0, vd1, d)
            _deint(do_ref, dod, d)
            _deint(packn, packd, d)

            sides = (0, 1) if nb > 1 else (0,)

            def probs(b, carry):
                st = pl.multiple_of(b * BLK, BLK)
                kt[b] = _two(kd0, kd1, st, 0).astype(F32).T.astype(BF16)
                trs[b] = packd[pl.ds(st, BLK), :].T
                for j in range(4):
                    rows[b, j:j + 1, :] = trs[b, j * quarter:j * quarter + 1, :]
                qb, dob = qd[pl.ds(st, BLK), :], dod[pl.ds(st, BLK), :]
                both = lambda j: jnp.concatenate([jnp.broadcast_to(rows[b, j:j + 1, :], (BLK, BLK)),
                                                  jnp.broadcast_to(rows[b, j + 1:j + 2, :], (BLK, BLK))], axis=0)
                lbt, dlt = both(0), both(2)
                for sd in sides:
                    stk = pl.multiple_of(jnp.maximum(b - sd, 0) * BLK, BLK)
                    mask = mct if sd == 0 else jnp.logical_and(mpt, lax.rem(b, nb) != 0)
                    k2, v2 = _two(kd0, kd1, stk, 0), _two(vd0, vd1, stk, 0)
                    pt = jnp.where(mask, jnp.exp(_dot_nt(k2, qb) - lbt), 0.0)
                    pts[b, sd] = pt.astype(BF16)
                    dss[b, sd] = (pt * (_dot_nt(v2, dob) - dlt)).astype(BF16)
                return carry

            lax.fori_loop(0, nblk, probs, 0, unroll=ATT_UNROLL)

            def prods(b, carry):
                st = pl.multiple_of(b * BLK, BLK)
                qb, dob = qd[pl.ds(st, BLK), :], dod[pl.ds(st, BLK), :]
                dq_t = None
                for sd in sides:
                    dst, ptb = dss[b, sd], pts[b, sd]
                    rk, rv = _dot(dst, qb), _dot(ptb, dob)
                    dqs = _dot(kt[jnp.maximum(b - sd, 0)], dst)
                    dq_t = dqs if dq_t is None else dq_t + dqs
                    dk, dv = (dkcd, dvcd) if sd == 0 else (dkpd, dvpd)
                    dk[pl.ds(st, BLK), :] = jnp.where(hm0, rk[0:BLK], rk[BLK:2 * BLK])
                    dv[pl.ds(st, BLK), :] = jnp.where(hm0, rv[0:BLK], rv[BLK:2 * BLK])
                dqd[pl.ds(st, BLK), :] = dq_t.T
                return carry

            lax.fori_loop(0, nblk, prods, 0, unroll=ATT_UNROLL)
            _reint(dqd, dqn, d, True)
            _reint(dkcd, dkn, d, True)
            _reint(dvcd, dvn, d, True)
            _reint_prev(dkpd, dkn, d)
            _reint_prev(dvpd, dvn, d)

        lane = lax.broadcasted_iota(jnp.int32, (S, LANES), 1)
        first = (lane & (HEAD // 2)) == 0
        cos, sin = cos_ref[...], sin_ref[...]
        dq = dqn[...] * (HEAD ** -0.5)
        dk = dkn[...]
        stage[0] = (dq * cos - _rot_half(dq, first) * sin).astype(BF16)
        stage[1] = (dk * cos - _rot_half(dk, first) * sin).astype(BF16)
        stage[2] = dvn[...].astype(BF16)
        copies = [pltpu.make_async_copy(stage.at[j], dp_ref.at[:, pl.ds((2 + j) * R + p * LANES, LANES)], sems.at[j])
                  for j in range(3)]
        for cp in copies:
            cp.start()
        for cp in copies:
            cp.wait()

    blk = pl.BlockSpec((S, LANES), lambda p: (0, p))
    tab = pl.BlockSpec((S, LANES), lambda p: (0, 0))
    vm = pl.BlockSpec(memory_space=pltpu.VMEM)
    _, orows, ocols = gw_out4.shape
    return pl.pallas_call(
        body, name="att_bwd", grid=(NPAIR,),
        in_specs=[pl.BlockSpec(memory_space=pl.ANY), blk, blk, blk, blk, blk,
                  pl.BlockSpec((S, LANES), lambda p: (0, VB + p)), tab, tab, vm],
        out_specs=[pl.BlockSpec(memory_space=pl.ANY), vm],
        out_shape=[jax.ShapeDtypeStruct((S, E), BF16), jax.ShapeDtypeStruct((orows, ocols), F32)],
        scratch_shapes=[pltpu.VMEM((S, LANES), BF16)] * 6 + [pltpu.VMEM((nblk, LANES, 2 * BLK), BF16)]
        + [pltpu.VMEM((S, LANES), F32)] * 10
        + [pltpu.VMEM((nblk, 8, BLK), F32), pltpu.VMEM((nblk, LANES, BLK), F32)]
        + [pltpu.VMEM((nblk, 2, 2 * BLK, BLK), BF16)] * 2
        + [pltpu.VMEM((3, S, LANES), BF16), pltpu.SemaphoreType.DMA((3,)), pltpu.VMEM((orows, ocols), F32)]
        + _ReduceScatter.scratch(NCHIP, orows, ocols, 1),
        input_output_aliases={0: 0},
        compiler_params=_cp(("arbitrary",)),
    )(dproj, d_att, att, lse, qr, kr, proj, cos, sin, gw_out4)


def _out_fwd_bwd(ya, att, proj, w_out_bf, x, target, mod, norm_post, norm_att):
    ts = 512

    def body(ya_ref, att_ref, gb_ref, w_ref, x_ref, t_ref, mod_ref, npost_ref, natt_ref,
             gx_ref, dya_ref, datt_ref, dgb_ref, gw_ref, acc_ref):
        i = pl.program_id(0)

        @pl.when(i == 0)
        def _():
            gw_ref[...] = jnp.zeros_like(gw_ref)
            acc_ref[...] = jnp.zeros_like(acc_ref)

        gate = mod_ref[:, 2 * D:3 * D]
        att = att_ref[...]
        gb = gb_ref[...]
        sg = _sigmoid(gb)
        silu = gb * sg
        ybp = att * silu
        yb, ybn, rstd_b = _rms_fwd(ybp, natt_ref[...])
        cat = jnp.concatenate([ya_ref[...], yb.astype(BF16)], axis=1)
        mix = _dot(cat, w_ref[...])
        rn, mn, rstd_m = _rms_fwd(mix, npost_ref[...])
        err = x_ref[...] + gate * rn - t_ref[...]
        dy = err * (1.0 / D)
        gx_ref[...] = dy
        dmix, dnpost = _rms_bwd(dy * gate, mn, rstd_m, npost_ref[...])
        dmb = dmix.astype(BF16)
        gw_ref[...] += _dot_tn(cat, dmb)
        dcat = _dot_nt(dmb, w_ref[...])
        dya_ref[...] = dcat[:, 0:R]
        dybp, dnatt = _rms_bwd(dcat[:, R:2 * R], ybn, rstd_b, natt_ref[...])
        datt_ref[...] = dybp * silu
        dgb_ref[...] = (dybp * att * (sg * (1.0 + gb * (1.0 - sg)))).astype(BF16)
        acc_ref[0:1, :] += jnp.sum(dy * rn, axis=0, keepdims=True)
        acc_ref[1:2, :] += dnpost
        acc_ref[2:3, 0:R] += dnatt
        acc_ref[3:4, :] += jnp.sum(jnp.sum(err * err, axis=1, keepdims=True), axis=0, keepdims=True)

    tile = lambda w: pl.BlockSpec((ts, w), lambda i: (i, 0))
    c0 = lambda shape: pl.BlockSpec(shape, lambda i: (0, 0))
    return pl.pallas_call(
        body, name="out_fwd_bwd", grid=(S // ts,),
        in_specs=[tile(R), tile(R), pl.BlockSpec((ts, R), lambda i: (i, 5)), c0((D, D)), tile(D), tile(D),
                  c0((1, 3 * D)), c0((1, D)), c0((1, R))],
        out_specs=[tile(D), tile(R), tile(R), pl.BlockSpec((ts, R), lambda i: (i, 5)), c0((D, D)), c0((8, D))],
        out_shape=[jax.ShapeDtypeStruct((S, D), F32), jax.ShapeDtypeStruct((S, R), F32),
                   jax.ShapeDtypeStruct((S, R), F32), jax.ShapeDtypeStruct((S, E), BF16),
                   jax.ShapeDtypeStruct((D, D), F32), jax.ShapeDtypeStruct((8, D), F32)],
        compiler_params=_cp(("arbitrary",)),
    )(ya, att, proj, w_out_bf, x, target, mod, norm_post, norm_att)


UC = 256
UPC = EC // UC


NU = E // UC


def _unit_of_step(i):
    return (i % NCHIP) * UPC + i // NCHIP


def _in_proj_bwd(ht, dproj, w_in_bf, x, gx1, mod, norm_pre, smalls):
    ts = 256
    nt = S // ts
    half = D // 2
    units = [_unit_of_step(k) for k in range(NU)]
    owners = [u // UPC for u in units]
    ns = len(smalls)

    def body(*refs):
        (ht_ref, dpu_ref, dp_ref, w_hbm, x_ref, gx1_ref, mod_ref, np_ref), refs = refs[:8], refs[8:]
        small_in, refs = refs[:ns], refs[ns:]
        (gx_ref, gin_ref), refs = refs[:2], refs[2:]
        small_out, (acc_out,), refs = refs[:ns], refs[ns:ns + 1], refs[ns + 1:]
        mine, sib, tmp, stage, got, red, acc_ref, hs, hr, ps, pr, bs, br = refs[:13]
        early = _SmallGather(small_in, small_out, *refs[13:16])
        late = _SmallGather([acc_ref], [acc_out], *refs[16:19])
        w_ref, w_sem = refs[19:21]
        i = pl.program_id(0)
        w_copy = pltpu.make_async_copy(w_hbm, w_ref, w_sem)
        pl.when(i == 0)(w_copy.start)
        pl.when(i == NU)(w_copy.wait)
        xx, yy, c = _me()
        ci = 2 * xx + yy
        r0 = pl.multiple_of(c * half, half)
        r1 = pl.multiple_of((1 - c) * half, half)
        pl.when(i == 0)(early.start)
        pl.when(i == NU)(early.forward)

        def exch(k):
            return _remote(tmp.at[k % 2], sib.at[k], hs.at[k], hr.at[k], 1)

        def partial(k, sender):
            return pltpu.make_async_remote_copy(
                src_ref=stage.at[k], dst_ref=got.at[units[k] % UPC, sender], send_sem=ps.at[k],
                recv_sem=pr.at[k, sender], device_id=(owners[k] // 2, owners[k] % 2, c), device_id_type=MESH)

        def back(k, start):
            off = (units[k] % UPC) * UC
            blk = red.at[pl.ds(start, half), off:off + UC]
            return _remote(blk, blk, bs.at[k], br.at[k], 1)

        for k in range(NU + 1):
            @pl.when(i == k)
            def _():
                if k < NU:
                    if k >= 2:
                        exch(k - 2).wait_send()
                    dpu = dpu_ref[...]
                    tmp[k % 2] = _dot(ht_ref[pl.ds(r1, half), :], dpu)
                    exch(k).start()
                    mine[k] = _dot(ht_ref[pl.ds(r0, half), :], dpu)
                if k >= 1:
                    exch(k - 1).wait_recv()
                    mine[k - 1] += sib[k - 1]

                    @pl.when(ci != owners[k - 1])
                    def _():
                        stage[k - 1] = mine[k - 1].astype(BF16)
                        partial(k - 1, ci).start()

        @pl.when(i == NU)
        def _():
            acc_ref[...] = jnp.zeros_like(acc_ref)

        @pl.when(i >= NU)
        def _():
            dh = sum(_dot_nt(dp_ref[:, j * EC:(j + 1) * EC], w_ref[j]) for j in range(NCHIP))
            hp, xn, rstd = _rms_fwd(x_ref[...], np_ref[...])
            dx, dnp = _rms_bwd(dh * (1.0 + mod_ref[:, D:2 * D]), xn, rstd, np_ref[...])
            gx_ref[...] = gx1_ref[...] + dx
            acc_ref[0:1, :] += jnp.sum(dh, axis=0, keepdims=True)
            acc_ref[1:2, :] += jnp.sum(dh * hp, axis=0, keepdims=True)
            acc_ref[2:3, :] += dnp

        for t in range(UPC):
            @pl.when(i == NU + 1 + 2 * t)
            def _():
                for k in range(NCHIP * t, NCHIP * (t + 1)):
                    @pl.when(ci == owners[k])
                    def _():
                        off = (units[k] % UPC) * UC
                        red[pl.ds(r0, half), off:off + UC] = mine[k]
                        for s in range(NCHIP):
                            if s != owners[k]:
                                partial(k, s).wait_recv()
                                red[pl.ds(r0, half), off:off + UC] += got[units[k] % UPC, s].astype(F32)
                        back(k, r0).start()

        @pl.when(i == NU + nt - 1)
        def _():
            late.start()
            exch(NU - 2).wait_send()
            exch(NU - 1).wait_send()
            for k in range(NU):
                @pl.when(ci == owners[k])
                def _():
                    back(k, r1).wait_recv()
                    back(k, r0).wait_send()

                @pl.when(ci != owners[k])
                def _():
                    partial(k, ci).wait_send()
            gin_ref[...] = red[...]
            early.finish()
            late.forward()
            late.finish()

    tile = lambda w: pl.BlockSpec((ts, w), lambda i: (jnp.maximum(i - NU, 0), 0))
    c0 = lambda shape: pl.BlockSpec(shape, lambda i: (0, 0))
    vm = pl.BlockSpec(memory_space=pltpu.VMEM)
    hbm = pl.BlockSpec(memory_space=pl.ANY)
    gathered = [jax.ShapeDtypeStruct((NDEV,) + a.shape, a.dtype) for a in smalls] + [jax.ShapeDtypeStruct((NDEV, 8, D), F32)]
    return pl.pallas_call(
        body, name="in_proj_bwd", grid=(NU + nt,),
        in_specs=[vm, pl.BlockSpec((S, UC), lambda i: (0, _unit_of_step(jnp.minimum(i, NU - 1)))), tile(E),
                  hbm, tile(D), tile(D), c0((1, 3 * D)), c0((1, D))] + [vm] * ns,
        out_specs=[tile(D), vm] + [hbm] * (ns + 1),
        out_shape=[jax.ShapeDtypeStruct((S, D), F32), jax.ShapeDtypeStruct((D, EC), F32)] + gathered,
        scratch_shapes=[pltpu.VMEM((NU, half, UC), F32), pltpu.VMEM((NU, half, UC), F32),
                        pltpu.VMEM((2, half, UC), F32), pltpu.VMEM((NU, half, UC), BF16),
                        pltpu.VMEM((UPC, NCHIP, half, UC), BF16), pltpu.VMEM((D, EC), F32), pltpu.VMEM((8, D), F32),
                        pltpu.SemaphoreType.DMA((NU,)), pltpu.SemaphoreType.DMA((NU,)),
                        pltpu.SemaphoreType.DMA((NU,)), pltpu.SemaphoreType.DMA((NU, NCHIP)),
                        pltpu.SemaphoreType.DMA((NU,)), pltpu.SemaphoreType.DMA((NU,))]
        + _SmallGather.sems(ns) + _SmallGather.sems(1)
        + [pltpu.VMEM((NCHIP, D, EC), BF16), pltpu.SemaphoreType.DMA],
        compiler_params=_cp(("arbitrary",)),
    )(ht, dproj, dproj, w_in_bf, x, gx1, mod, norm_pre, *smalls)


def _local_step(x, cos, sin, target, mod, w_in_bf, proj, ht, w_out, conv_w, p):
    rec_p = (conv_w, p["conv_b"], p["w_rg_a"], p["b_rg_a"], p["w_rg_x"], p["b_rg_x"], p["lru_lambda"], p["norm_rec"])
    h_all, ya = _rec_fwd(proj, *rec_p)
    att, qr, kr, lse, w_out_bf = _att_fwd(proj, cos, sin, w_out)
    gx1, d_ya, d_att, dproj, gw_out, acc_o = _out_fwd_bwd(ya, att, proj, w_out_bf.reshape(D, D), x, target, mod,
                                                           p["norm_post"], p["norm_att"])
    dproj, g_out = _att_bwd(dproj, d_att, att, lse, qr, kr, proj, cos, sin, gw_out.reshape(NCHIP, D // NCHIP, D))
    dproj, dwa, dwx, sm = _rec_bwd(dproj, d_ya, proj, h_all, *rec_p)
    grad_x, g_in, *gathered = _in_proj_bwd(ht, dproj, w_in_bf, x, gx1, mod, p["norm_pre"], [acc_o, sm, dwa, dwx])
    return grad_x, g_in, g_out, gathered


def _me():
    return lax.axis_index("x"), lax.axis_index("y"), lax.axis_index("c")


def _flip(v, bit):
    return 1 - v if bit else v


def _peer(rel):
    x, y, c = _me()
    return (_flip(x, rel & 4), _flip(y, rel & 2), _flip(c, rel & 1))


def _remote(src, dst, send_sem, recv_sem, rel):
    return pltpu.make_async_remote_copy(src_ref=src, dst_ref=dst, send_sem=send_sem, recv_sem=recv_sem,
                                        device_id=_peer(rel), device_id_type=MESH)


class _WeightGather:
    SEMS = [pltpu.SemaphoreType.DMA((NCHIP - 1,))] * 4

    def __init__(self, w_ref, out_ref, send_sems, recv_sems, fsend_sems, frecv_sems):
        x, y, c = _me()
        self.w, self.out, self.ci = w_ref, out_ref, 2 * x + y
        self.half = w_ref.shape[0] // 2
        self.r0 = pl.multiple_of(c * self.half, self.half)
        self.r1 = pl.multiple_of((1 - c) * self.half, self.half)
        self.sems = (send_sems, recv_sems, fsend_sems, frecv_sems)

    def _ici(self, chip, k):
        blk = self.out.at[chip, pl.ds(self.r0, self.half), :]
        return _remote(blk, blk, self.sems[0].at[k - 1], self.sems[1].at[k - 1], 2 * k)

    def _d2d(self, chip, start, k):
        blk = self.out.at[chip, pl.ds(start, self.half), :]
        return _remote(blk, blk, self.sems[2].at[k - 1], self.sems[3].at[k - 1], 1)

    def start(self, diagonal=True):
        self.out[self.ci] = self.w[...].astype(BF16)
        for k in range(1, NCHIP if diagonal else NCHIP - 1):
            self._ici(self.ci, k).start()

    def _relay(self, chip, piece, k):
        q = self.half // 2
        blk = self.out.at[chip, pl.ds(self.r0 + piece * q, q), :]
        return _remote(blk, blk, self.relay_sems[0].at[piece], self.relay_sems[1].at[piece], 2 * k)

    def neighbours_landed(self, relay_send_sems, relay_recv_sems):
        self.relay_sems = (relay_send_sems, relay_recv_sems)
        for k in (1, 2):
            self._ici(self.ci ^ k, k).wait_recv()
        self._relay(self.ci ^ 2, 0, 1).start()
        self._relay(self.ci ^ 1, 1, 2).start()
        for k in (1, 2):
            self._d2d(self.ci ^ k, self.r0, k).start()

    def sibling_landed(self, k):
        self._d2d(self.ci ^ k, self.r1, k).wait_recv()

    def diagonal_landed(self):
        for piece, k in ((0, 1), (1, 2)):
            self._relay(self.ci ^ 3, piece, k).wait_recv()
        self._d2d(self.ci ^ 3, self.r0, 3).start()
        self._d2d(self.ci ^ 3, self.r1, 3).wait_recv()

    def finish_relayed(self):
        for k in (1, 2):
            self._ici(self.ci, k).wait_send()
        self._relay(self.ci ^ 2, 0, 1).wait_send()
        self._relay(self.ci ^ 1, 1, 2).wait_send()
        for k in range(1, NCHIP):
            self._d2d(self.ci ^ k, self.r0, k).wait_send()

    def forward(self):
        for k in range(1, NCHIP):
            self._ici(self.ci ^ k, k).wait_recv()
            self._d2d(self.ci ^ k, self.r0, k).start()

    def finish(self):
        for k in range(1, NCHIP):
            self._d2d(self.ci ^ k, self.r1, k).wait_recv()
        self.finish_sends()

    def finish_sends(self):
        for k in range(1, NCHIP):
            self._ici(self.ci, k).wait_send()
            self._d2d(self.ci ^ k, self.r0, k).wait_send()


class _SmallGather:
    @staticmethod
    def sems(n):
        return [pltpu.SemaphoreType.DMA((n, 7)), pltpu.SemaphoreType.DMA((n, 7)), pltpu.SemaphoreType.DMA((n,))]

    def __init__(self, srcs, outs, send_sems, recv_sems, local_sems):
        x, y, c = _me()
        self.srcs, self.outs = list(srcs), list(outs)
        self.ss, self.rs, self.ls = send_sems, recv_sems, local_sems
        self.ci, self.c = 2 * x + y, c
        self.me = 2 * self.ci + c

    def _own(self, a, slot, rel):
        return _remote(self.srcs[a], self.outs[a].at[self.me], self.ss.at[a, slot], self.rs.at[a, slot], rel)

    def _block(self, a, idx, slot, rel):
        blk = self.outs[a].at[idx]
        return _remote(blk, blk, self.ss.at[a, slot], self.rs.at[a, slot], rel)

    def _local(self, a):
        return pltpu.make_async_copy(self.srcs[a], self.outs[a].at[self.me], self.ls.at[a])

    def start(self):
        for a in range(len(self.srcs)):
            self._local(a).start()
            self._own(a, 0, 1).start()
            for k in range(1, NCHIP):
                self._own(a, k, 2 * k).start()

    def forward(self):
        for a in range(len(self.srcs)):
            for k in range(1, NCHIP):
                idx = 2 * (self.ci ^ k) + self.c
                self._block(a, idx, k, 2 * k).wait_recv()
                self._block(a, idx, 3 + k, 1).start()

    def finish(self):
        for a in range(len(self.srcs)):
            self._block(a, 2 * self.ci + 1 - self.c, 0, 1).wait_recv()
            for k in range(1, NCHIP):
                self._block(a, 2 * (self.ci ^ k) + 1 - self.c, 3 + k, 1).wait_recv()
            self._own(a, 0, 1).wait_send()
            for k in range(1, NCHIP):
                self._own(a, k, 2 * k).wait_send()
                self._block(a, 2 * (self.ci ^ k) + self.c, 3 + k, 1).wait_send()
            self._local(a).wait()


def _start_in_proj(crow, w_ada, b_cols, w_in, pos, x, norm_pre, order):
    ts = 512
    nt = S // ts
    wc = crow.shape[1]

    def body(order_ref, crow_ref, wada_ref, b_ref, win_ref, pos_ref, freq_ref, x_ref, np_ref,
             g0_ref, mod_ref, wbf_ref, cos_ref, sin_ref, proj_ref, ht_ref,
             g0s, modp, modb, wbuf, hb_all, cs, cr, ms, mr, ws, wr, fs, fr, local_sems, ys, yr, osem):
        s, t = pl.program_id(0), pl.program_id(1)
        x, y, c = _me()
        ci = 2 * x + y
        me = 2 * ci + c
        wg = _WeightGather(win_ref, wbuf, ws, wr, fs, fr)

        @pl.when(jnp.logical_and(s == 0, t == 0))
        def _():
            wg.start(diagonal=False)
            mine = pltpu.make_async_copy(crow_ref, g0s.at[pl.ds(me, 1), :], local_sems.at[0])
            mine.start()
            csend = [_remote(crow_ref, g0s.at[pl.ds(me, 1), :], cs.at[r - 1], cr.at[r - 1], r) for r in range(1, NDEV)]
            for cp in csend:
                cp.start()
            cos_ref[...], sin_ref[...] = _cos_sin(pos_ref, freq_ref)
            for r in range(1, NDEV):
                px, py, pc = _peer(r)
                _remote(crow_ref, g0s.at[pl.ds(4 * px + 2 * py + pc, 1), :], cs.at[r - 1], cr.at[r - 1], r).wait_recv()
            mine.wait()
            cv = g0s[:, 0:D]
            sc = cv * _sigmoid(cv)
            scb = jnp.concatenate([sc, jnp.zeros_like(sc)], axis=0).astype(BF16)
            modp[...] = _dot(scb, wada_ref[...].astype(BF16))[0:NDEV, :] + b_ref[...]
            own = pltpu.make_async_copy(modp.at[pl.ds(me, 1), :], modb.at[ci], local_sems.at[1])
            own.start()
            msend = []
            for k in range(1, NCHIP):
                cp = _remote(modp.at[pl.ds(2 * (ci ^ k) + c, 1), :], modb.at[ci], ms.at[k - 1], mr.at[k - 1], 2 * k)
                cp.start()
                msend.append(cp)
            for k in range(1, NCHIP):
                _remote(modp.at[pl.ds(me, 1), :], modb.at[ci ^ k], ms.at[k - 1], mr.at[k - 1], 2 * k).wait_recv()
            own.wait()
            for j in range(NCHIP):
                mod_ref[:, j * EC:(j + 1) * EC] = modb[j]
            for cp in csend + msend:
                cp.wait_send()
            g0_ref[...] = g0s[...]

        def keep(k):
            return pltpu.make_async_copy(wbuf.at[ci ^ k], wbf_ref.at[ci ^ k], osem.at[k])

        @pl.when(jnp.logical_and(s == 1, t == 0))
        def _():
            keep(0).start()
            wg.neighbours_landed(ys, yr)
            wg.sibling_landed(1)
            keep(1).start()

        @pl.when(jnp.logical_and(s == 2, t == 0))
        def _():
            wg.sibling_landed(2)
            keep(2).start()

        @pl.when(jnp.logical_and(s == 3, t == 0))
        def _():
            wg.relay_sems = (ys, yr)
            wg.diagonal_landed()
            keep(3).start()

        rows = pl.ds(pl.multiple_of(t * ts, ts), ts)

        @pl.when(s == 0)
        def _():
            hp, _, _ = _rms_fwd(x_ref[...], np_ref[...])
            h = hp * (1.0 + mod_ref[:, D:2 * D]) + mod_ref[:, 0:D]
            hb_all[rows, :] = h.astype(BF16)
            ht_ref[...] = h.T.astype(BF16)

        proj_ref[...] = _dot(hb_all[rows, :], wbuf[ci ^ s])

        @pl.when(jnp.logical_and(s == NCHIP - 1, t == nt - 1))
        def _():
            wg.relay_sems = (ys, yr)
            wg.finish_relayed()
            for k in range(NCHIP):
                keep(k).wait()

    vm = pl.BlockSpec(memory_space=pltpu.VMEM)
    first_pass = lambda s, t: jnp.where(s == 0, t, nt - 1)
    grid_spec = pltpu.PrefetchScalarGridSpec(
        num_scalar_prefetch=1, grid=(NCHIP, nt),
        in_specs=[vm, vm, vm, vm, vm, vm, pl.BlockSpec((ts, D), lambda s, t, o: (first_pass(s, t), 0)),
                  pl.BlockSpec((1, D), lambda s, t, o: (0, 0))],
        out_specs=[vm, vm, pl.BlockSpec(memory_space=pl.ANY), vm, vm, pl.BlockSpec((ts, EC), lambda s, t, o: (t, o[s])),
                   pl.BlockSpec((D, ts), lambda s, t, o: (0, first_pass(s, t)))],
        scratch_shapes=[pltpu.VMEM((NDEV, wc), F32), pltpu.VMEM((NDEV, EC), F32), pltpu.VMEM((NCHIP, 1, EC), F32),
                        pltpu.VMEM((NCHIP, D, EC), BF16), pltpu.VMEM((S, D), BF16),
                        pltpu.SemaphoreType.DMA((NDEV - 1,)), pltpu.SemaphoreType.DMA((NDEV - 1,)),
                        pltpu.SemaphoreType.DMA((NCHIP - 1,)), pltpu.SemaphoreType.DMA((NCHIP - 1,))]
        + _WeightGather.SEMS + [pltpu.SemaphoreType.DMA((2,))] * 3 + [pltpu.SemaphoreType.DMA((NCHIP,))])
    return pl.pallas_call(
        body, name="start_in_proj", grid_spec=grid_spec,
        out_shape=[jax.ShapeDtypeStruct((NDEV, wc), F32), jax.ShapeDtypeStruct((1, 3 * D), F32),
                   jax.ShapeDtypeStruct((NCHIP, D, EC), BF16), jax.ShapeDtypeStruct((S, LANES), F32),
                   jax.ShapeDtypeStruct((S, LANES), F32), jax.ShapeDtypeStruct((S, E), F32),
                   jax.ShapeDtypeStruct((D, S), BF16)],
        compiler_params=_cp(("arbitrary", "arbitrary")),
    )(order, crow, w_ada, b_cols, w_in, pos, _rope_freq(), x, norm_pre)


class _ReduceScatter:
    @staticmethod
    def scratch(n_units, rows, ucols, max_owned):
        half = rows // 2
        return [pltpu.VMEM((n_units, half, ucols), F32), pltpu.VMEM((n_units, half, ucols), BF16),
                pltpu.VMEM((max_owned, NCHIP, half, ucols), BF16),
                pltpu.SemaphoreType.DMA((2,)), pltpu.SemaphoreType.DMA((n_units,)),
                pltpu.SemaphoreType.DMA((n_units, NCHIP)), pltpu.SemaphoreType.DMA((n_units,)),
                pltpu.SemaphoreType.DMA((n_units,))]

    def __init__(self, g_ref, out_ref, units, sib, stage, got, sem1, send2, recv2, send3, recv3):
        x, y, c = _me()
        self.c, self.ci = c, 2 * x + y
        self.g, self.out, self.units = g_ref, out_ref, units
        self.sib, self.stage, self.got = sib, stage, got
        self.sem1, self.send2, self.recv2, self.send3, self.recv3 = sem1, send2, recv2, send3, recv3
        self.half = g_ref.shape[1] // 2
        self.ucols = g_ref.shape[2]
        self.r0 = pl.multiple_of(c * self.half, self.half)
        self.r1 = pl.multiple_of((1 - c) * self.half, self.half)
        self.slot0 = units[0][0]
        assert [u[0] for u in units] == list(range(self.slot0, self.slot0 + len(units)))
        seen = {}
        self.local = []
        for _, owner, _ in units:
            self.local.append(seen.get(owner, 0))
            seen[owner] = seen.get(owner, 0) + 1

    def _halves(self):
        n = len(self.units)
        return _remote(self.g.at[pl.ds(self.slot0, n), pl.ds(self.r1, self.half), :], self.sib,
                       self.sem1.at[0], self.sem1.at[1], 1)

    def _partial(self, i, sender):
        _, owner, _ = self.units[i]
        return pltpu.make_async_remote_copy(
            src_ref=self.stage.at[i], dst_ref=self.got.at[self.local[i], sender],
            send_sem=self.send2.at[i], recv_sem=self.recv2.at[i, sender],
            device_id=(owner // 2, owner % 2, self.c), device_id_type=MESH)

    def _back(self, i, start):
        off = self.units[i][2]
        blk = self.out.at[pl.ds(start, self.half), off:off + self.ucols]
        return _remote(blk, blk, self.send3.at[i], self.recv3.at[i], 1)

    def start_halves(self):
        self._halves().start()

    def send_partials(self):
        self._halves().wait_recv()
        for i, (slot, owner, _) in enumerate(self.units):
            @pl.when(self.ci != owner)
            def _():
                self.stage[i] = (self.g[slot, pl.ds(self.r0, self.half), :] + self.sib[i]).astype(BF16)
                self._partial(i, self.ci).start()

    def reduce_owned(self):
        for i, (slot, owner, off) in enumerate(self.units):
            @pl.when(self.ci == owner)
            def _():
                rows, cols = pl.ds(self.r0, self.half), slice(off, off + self.ucols)
                self.out[rows, cols] = self.g[slot, pl.ds(self.r0, self.half), :] + self.sib[i]
                for s in range(NCHIP):
                    if s != owner:
                        self._partial(i, s).wait_recv()
                        self.out[rows, cols] += self.got[self.local[i], s].astype(F32)
                self._back(i, self.r0).start()

    def finish(self):
        self._halves().wait_send()
        for i, (_, owner, _) in enumerate(self.units):
            @pl.when(self.ci == owner)
            def _():
                self._back(i, self.r1).wait_recv()
                self._back(i, self.r0).wait_send()

            @pl.when(self.ci != owner)
            def _():
                self._partial(i, self.ci).wait_send()


def _silu_rows(c_ref):
    cv = c_ref[...]
    sc = cv * _sigmoid(cv)
    return jnp.concatenate([sc, jnp.zeros_like(sc)], axis=0).astype(BF16)


def _adamw(w, g, m, v, name):
    rows, cols = w.shape
    tr = 256 if rows % 256 == 0 else rows

    def body(w_ref, g_ref, m_ref, v_ref, d_ref, nm_ref, nv_ref):
        d_ref[...], nm_ref[...], nv_ref[...] = _adamw_values(w_ref[...], g_ref[...], m_ref[...], v_ref[...])

    spec = pl.BlockSpec((tr, cols), lambda i: (i, 0))
    return pl.pallas_call(
        body, name=name, grid=(rows // tr,), in_specs=[spec] * 4, out_specs=[spec] * 3,
        out_shape=[jax.ShapeDtypeStruct((rows, cols), F32)] * 3,
        compiler_params=_cp(("parallel",)),
    )(w, g, m, v)


def _adamw_values(w, g, m, v):
    nm = B1 * m + (1.0 - B1) * g
    nv = B2 * v + (1.0 - B2) * (g * g)
    m_hat = nm / (1.0 - B1 ** STEP)
    v_hat = nv / (1.0 - B2 ** STEP)
    return (-LR) * (m_hat / (jnp.sqrt(v_hat) + ADAM_EPS) + WD * w), nm, nv


NB = R // HEAD
SMALL = (("b_ada", (1, 3 * D)), ("norm_pre", (1, D)), ("norm_post", (1, D)), ("conv_w", (4, R // NCHIP)),
         ("conv_b", (1, R)), ("w_rg_a", (NB, HEAD, HEAD)), ("b_rg_a", (1, R)), ("w_rg_x", (NB, HEAD, HEAD)),
         ("b_rg_x", (1, R)), ("lru_lambda", (1, R)), ("norm_rec", (1, R)), ("norm_att", (1, R)))


def _small_update(ao8, sm8, dwa8, dwx8, ai8, cg, params):
    n = len(SMALL)

    def body(ao_ref, sm_ref, dwa_ref, dwx_ref, ai_ref, cg_ref, *refs):
        pin, pout, (gada_ref, loss_ref, dmod) = refs[:3 * n], refs[3 * n:7 * n], refs[7 * n:]
        xx, yy, _ = _me()
        ci = 2 * xx + yy

        def total(ref, *idx):
            acc = ref[(0,) + idx].astype(F32)
            for d in range(1, NDEV):
                acc = acc + ref[(d,) + idx].astype(F32)
            return acc

        row = lambda ref, r, lanes=slice(None): total(ref, slice(r, r + 1), lanes)
        mine = lambda parts: sum(jnp.where(ci == j, part, 0.0) for j, part in enumerate(parts))
        cw = R // NCHIP
        grads = {
            "b_ada": [jnp.concatenate([row(ai_ref, 0), row(ai_ref, 1), row(ao_ref, 0)], axis=1)],
            "norm_pre": [row(ai_ref, 2)], "norm_post": [row(ao_ref, 1)],
            "conv_w": [mine([row(sm_ref, 8 + r, slice(j * cw, (j + 1) * cw)) for j in range(NCHIP)]) for r in range(4)],
            "conv_b": [row(sm_ref, 4)], "b_rg_a": [row(sm_ref, 0)], "b_rg_x": [row(sm_ref, 1)],
            "lru_lambda": [row(sm_ref, 2)], "norm_rec": [row(sm_ref, 3)], "norm_att": [row(ao_ref, 2, slice(0, R))],
            "w_rg_a": [total(dwa_ref, h) for h in range(NB)], "w_rg_x": [total(dwx_ref, h) for h in range(NB)],
        }
        loss_ref[...] = row(ao_ref, 3, slice(0, LANES)) * (0.5 / D)
        for k, (name, shape) in enumerate(SMALL):
            w_ref, m_ref, v_ref = pin[3 * k:3 * k + 3]
            outs = pout[4 * k:4 * k + 4]
            for r, g in enumerate(grads[name]):
                at = (slice(None),) if len(grads[name]) == 1 else ((r,) if len(shape) == 3 else (slice(r, r + 1),))
                res = (g,) + _adamw_values(w_ref[at], g, m_ref[at], v_ref[at])
                for o_ref, val in zip(outs, res):
                    o_ref[at] = val
        for d in range(NDEV):
            dmod[d:d + 1, :] = jnp.concatenate([ai_ref[d, 0:1, :], ai_ref[d, 1:2, :], ao_ref[d, 0:1, :]], axis=1)
        cols = mine([dmod[:, j * EC:(j + 1) * EC] for j in range(NCHIP)])
        colsb = jnp.concatenate([cols, jnp.zeros_like(cols)], axis=0).astype(BF16)
        gada_ref[...] = _dot_tn(_silu_rows(cg_ref), colsb)

    shapes = [jax.ShapeDtypeStruct(s, F32) for _, s in SMALL]
    outs = pl.pallas_call(
        body, name="small_update",
        out_shape=[s for s in shapes for _ in range(4)] + [jax.ShapeDtypeStruct((D, EC), F32),
                                                           jax.ShapeDtypeStruct((1, LANES), F32)],
        scratch_shapes=[pltpu.VMEM((NDEV, 3 * D), F32)],
        compiler_params=_cp(),
    )(ao8, sm8, dwa8, dwx8, ai8, cg, *params)
    return outs[:4 * n], outs[4 * n], outs[4 * n + 1]


BIG = ("w_ada", "w_in", "w_out")
WEIGHTS = ("w_ada", "b_ada", "norm_pre", "norm_post", "w_in", "conv_w", "conv_b", "w_rg_a", "b_rg_a", "w_rg_x",
           "b_rg_x", "lru_lambda", "norm_rec", "norm_att", "w_out")


def kernel(x, c, positions, w_ada, b_ada, norm_pre, norm_post, w_in, conv_w, conv_b, w_rg_a, b_rg_a, w_rg_x, b_rg_x, lru_lambda, norm_rec, norm_att, w_out, loss_target, m_w_ada, m_b_ada, m_norm_pre, m_norm_post, m_w_in, m_conv_w, m_conv_b, m_w_rg_a, m_b_rg_a, m_w_rg_x, m_b_rg_x, m_lru_lambda, m_norm_rec, m_norm_att, m_w_out, v_w_ada, v_b_ada, v_norm_pre, v_norm_post, v_w_in, v_conv_w, v_conv_b, v_w_rg_a, v_b_rg_a, v_w_rg_x, v_b_rg_x, v_lru_lambda, v_norm_rec, v_norm_att, v_w_out):
    given = dict(locals())
    wts = {n: given[n] for n in WEIGHTS}
    ms = {n: given["m_" + n] for n in WEIGHTS}
    vs = {n: given["v_" + n] for n in WEIGHTS}
    xi, yi, _ = _me()
    chip = 2 * xi + yi
    cw_loc = R // NCHIP

    b_cols = lax.dynamic_slice(b_ada, (0, chip * EC), (1, EC))
    order = (chip ^ jnp.arange(NCHIP, dtype=jnp.int32)).astype(jnp.int32)
    g0, mod, w_in_bf, cos, sin, proj, ht = _start_in_proj(
        jnp.concatenate([c, conv_w.reshape(1, 4 * cw_loc)], axis=1), w_ada[0], b_cols, w_in[0],
        positions.reshape(S, 1), x[0], norm_pre, order)
    cg = g0[:, 0:D]
    conv_full = g0[0::2, D:].reshape(NCHIP, 4, cw_loc).transpose(1, 0, 2).reshape(4, R)

    p = dict(norm_pre=norm_pre, norm_post=norm_post, conv_b=conv_b, b_rg_a=b_rg_a, b_rg_x=b_rg_x,
             lru_lambda=lru_lambda, norm_rec=norm_rec, norm_att=norm_att, w_rg_a=w_rg_a[0], w_rg_x=w_rg_x[0])
    grad_x, g_in, g_out, gathered = _local_step(
        x[0], cos, sin, loss_target[0], mod, w_in_bf, proj, ht, w_out[0], conv_full, p)

    params = [d[n].reshape(shape) for n, shape in SMALL for d in (wts, ms, vs)]
    small_out, g_ada, loss_row = _small_update(*gathered, cg, params)
    grads = {"w_out": g_out, "w_in": g_in, "w_ada": g_ada}
    delta, new_m, new_v = {}, {}, {}
    for k, (n, _) in enumerate(SMALL):
        grads[n], delta[n], new_m[n], new_v[n] = small_out[4 * k:4 * k + 4]
    for n in BIG:
        delta[n], new_m[n], new_v[n] = _adamw(wts[n][0], grads[n], ms[n][0], vs[n][0], "adamw_" + n)
    out = lambda d: [d[n].reshape(wts[n].shape) for n in WEIGHTS]
    return (loss_row[0, 0], grad_x.reshape(x.shape), *out(grads), *out(delta), *out(new_m), *out(new_v))
```

```python
import numpy as np
import jax
import jax.numpy as jnp
from jax import lax
from jax.experimental import pallas as pl
from jax.experimental.pallas import tpu as pltpu

F32 = jnp.float32
BF16 = jnp.bfloat16

S = 2048
D = 1024
E = 3072
R = 512
NDEV = 8
NCHIP = 4
EC = 768
LRU_C = 8.0
EPS = 1e-6
NEG = -1e30
HEAD = 64
BLK = 128
PATTERNS = (1, 4, 16)
ROPE_THETA = 10000.0
LANES = 128
VMEM_LIMIT = 56 * 1024 * 1024

B1, B2, LR, WD, ADAM_EPS, STEP = 0.9, 0.999, 0.001, 0.01, 1e-8, 10
MESH = pl.DeviceIdType.MESH


def _cp(sem=None, **kw):
    return pltpu.CompilerParams(dimension_semantics=sem, vmem_limit_bytes=VMEM_LIMIT, **kw)


def _dot(a, b):
    return jnp.dot(a, b, preferred_element_type=F32)


def _dot_nt(a, b):
    return lax.dot_general(a, b, (((1,), (1,)), ((), ())), preferred_element_type=F32)


def _dot_tn(a, b):
    return lax.dot_general(a, b, (((0,), (0,)), ((), ())), preferred_element_type=F32)


def _sigmoid(x):
    return 1.0 / (1.0 + jnp.exp(-x))


def _expm1(x):
    poly = x * (1.0 + x * (0.5 + x * (1.0 / 6 + x * (1.0 / 24 + x * (1.0 / 120 + x * (1.0 / 720))))))
    return jnp.where(jnp.abs(x) < 0.3, poly, jnp.exp(x) - 1.0)


def _rms_fwd(v, g):
    rstd = lax.rsqrt(jnp.mean(v * v, axis=-1, keepdims=True) + EPS)
    vn = v * rstd
    return vn * g, vn, rstd


def _rms_bwd(dy, vn, rstd, g):
    dvn = dy * g
    dv = rstd * (dvn - vn * jnp.mean(dvn * vn, axis=-1, keepdims=True))
    return dv, jnp.sum(dy * vn, axis=0, keepdims=True)


RT = 256


def _shift_down(cur, prev8, j, row):
    if j == 0:
        return cur
    top = jnp.tile(pltpu.roll(prev8, j, 0), (RT // 8, 1))
    return jnp.where(row >= j, pltpu.roll(cur, j, 0), top)


def _shift_up(cur, next8, j, row):
    if j == 0:
        return cur
    bot = jnp.tile(pltpu.roll(next8, 8 - j, 0), (RT // 8, 1))
    return jnp.where(row < RT - j, pltpu.roll(cur, RT - j, 0), bot)


def _rec_gates(xp, xprev8, row, cw_ref, cb_ref, wa_ref, ba_ref, wx_ref, bx_ref, lam_ref):
    xa = cb_ref[...] + sum(cw_ref[3 - j:4 - j, :] * _shift_down(xp, xprev8, j, row) for j in range(4))
    xab = xa.astype(BF16)
    r = _sigmoid(_dot(xab, wa_ref[...]) + ba_ref[...])
    ig = _sigmoid(_dot(xab, wx_ref[...]) + bx_ref[...])
    nl = -lam_ref[...]
    sp = jnp.maximum(nl, 0.0) + jnp.log1p(jnp.exp(-jnp.abs(nl)))
    la = (-LRU_C) * r * sp
    a = jnp.exp(la)
    mult = jnp.sqrt(-_expm1(2.0 * la))
    return dict(xa=xa, xab=xab, r=r, ig=ig, sp=sp, la=la, a=a, mult=mult)


def _scan_fwd(a, u, row):
    sh = 1
    while sh < RT:
        a_s = jnp.where(row >= sh, pltpu.roll(a, sh, 0), 1.0)
        u_s = jnp.where(row >= sh, pltpu.roll(u, sh, 0), 0.0)
        u = a * u_s + u
        a = a * a_s
        sh *= 2
    return a, u


def _scan_bwd(al, g, row):
    sh = 1
    while sh < RT:
        al_s = jnp.where(row < RT - sh, pltpu.roll(al, RT - sh, 0), 1.0)
        g_s = jnp.where(row < RT - sh, pltpu.roll(g, RT - sh, 0), 0.0)
        g = g + al * g_s
        al = al * al_s
        sh *= 2
    return g


def _dense_from_blocks(blocks_ref, dense_ref):
    dense_ref[...] = jnp.zeros_like(dense_ref)
    for h in range(R // HEAD):
        dense_ref[h * HEAD:(h + 1) * HEAD, h * HEAD:(h + 1) * HEAD] = blocks_ref[h].astype(dense_ref.dtype)


def _rec_fwd(proj, conv_w, conv_b, wa_b, ba, wx_b, bx, lam, norm_rec):
    nt = S // RT

    def body(p_ref, cw_ref, cb_ref, wa_ref, ba_ref, wx_ref, bx_ref, lam_ref, nr_ref,
             h_ref, ya_ref, prev8, hc, wad, wxd):
        i = pl.program_id(0)

        @pl.when(i == 0)
        def _():
            prev8[...] = jnp.zeros_like(prev8)
            hc[...] = jnp.zeros_like(hc)
            _dense_from_blocks(wa_ref, wad)
            _dense_from_blocks(wx_ref, wxd)

        row = lax.broadcasted_iota(jnp.int32, (RT, R), 0)
        xp = p_ref[:, 0:R]
        ga = p_ref[:, R:2 * R]
        f = _rec_gates(xp, prev8[...], row, cw_ref, cb_ref, wad, ba_ref, wxd, bx_ref, lam_ref)
        u = f["mult"] * (f["ig"] * f["xa"])
        acum, hh = _scan_fwd(f["a"], u, row)
        h = hh + acum * hc[0:1, :]
        h_ref[...] = h
        hc[0:1, :] = h_ref[RT - 1:RT, :]
        prev8[...] = p_ref[RT - 8:RT, 0:R]
        yp = h * (ga * _sigmoid(ga))
        ya, _, _ = _rms_fwd(yp, nr_ref[...])
        ya_ref[...] = ya.astype(BF16)

    row1 = lambda n: pl.BlockSpec((1, n), lambda i: (0, 0))
    blocks = pl.BlockSpec((R // HEAD, HEAD, HEAD), lambda i: (0, 0, 0))
    return pl.pallas_call(
        body, name="rec_fwd", grid=(nt,),
        in_specs=[pl.BlockSpec((RT, 2 * R), lambda i: (i, 0)), pl.BlockSpec((4, R), lambda i: (0, 0)), row1(R),
                  blocks, row1(R), blocks, row1(R), row1(R), row1(R)],
        out_specs=[pl.BlockSpec((RT, R), lambda i: (i, 0)), pl.BlockSpec((RT, R), lambda i: (i, 0))],
        out_shape=[jax.ShapeDtypeStruct((S, R), F32), jax.ShapeDtypeStruct((S, R), BF16)],
        scratch_shapes=[pltpu.VMEM((8, R), F32), pltpu.VMEM((8, R), F32), pltpu.VMEM((R, R), BF16),
                        pltpu.VMEM((R, R), BF16)],
        compiler_params=_cp(("arbitrary",)),
    )(proj, conv_w, conv_b, wa_b, ba, wx_b, bx, lam, norm_rec)


def _rec_bwd(dproj, d_ya, proj, h_all, conv_w, conv_b, wa_b, ba, wx_b, bx, lam, norm_rec):
    nt = S // RT

    def body(dp_in, dya_ref, p_ref, pprev_ref, h_ref, hprev_ref, cw_ref, cb_ref, wab_ref, ba_ref, wxb_ref, bx_ref,
             lam_ref, nr_ref, dp_ref, dwab_ref, dwxb_ref, sm_ref, nxt8, cg, wa_ref, wx_ref, dwa_ref, dwx_ref):
        i = pl.program_id(0)
        ti = nt - 1 - i

        @pl.when(i == 0)
        def _():
            nxt8[...] = jnp.zeros_like(nxt8)
            cg[...] = jnp.zeros_like(cg)
            dwa_ref[...] = jnp.zeros_like(dwa_ref)
            dwx_ref[...] = jnp.zeros_like(dwx_ref)
            sm_ref[...] = jnp.zeros_like(sm_ref)
            _dense_from_blocks(wab_ref, wa_ref)
            _dense_from_blocks(wxb_ref, wx_ref)

        row = lax.broadcasted_iota(jnp.int32, (RT, R), 0)
        first = (ti > 0).astype(F32)
        xprev8 = pprev_ref[...] * first
        hprev8 = hprev_ref[...] * first
        xp = p_ref[:, 0:R]
        ga = p_ref[:, R:2 * R]
        f = _rec_gates(xp, xprev8, row, cw_ref, cb_ref, wa_ref, ba_ref, wx_ref, bx_ref, lam_ref)
        xa, r, ig, a, mult = f["xa"], f["r"], f["ig"], f["a"], f["mult"]
        h = h_ref[...]
        sg = _sigmoid(ga)
        gate = ga * sg
        yp = h * gate
        _, ypn, rstd = _rms_fwd(yp, nr_ref[...])
        d_yp, dnr = _rms_bwd(dya_ref[...], ypn, rstd, nr_ref[...])
        d_ga = d_yp * h * (sg * (1.0 + ga * (1.0 - sg)))
        dh = d_yp * gate + jnp.where(row == RT - 1, cg[0:1, :], 0.0)
        al = jnp.where(row < RT - 1, pltpu.roll(a, RT - 1, 0), 0.0)
        g = _scan_bwd(al, dh, row)
        cg[0:1, :] = jnp.sum(jnp.where(row == 0, a * g, 0.0), axis=0, keepdims=True)
        h_m1 = _shift_down(h, hprev8, 1, row)
        da = g * h_m1
        ix = ig * xa
        d_mult = g * ix
        d_ig = g * mult * xa
        d_xa = g * mult * ig
        d_la = da * a - d_mult * (a * a) / mult
        d_r = d_la * ((-LRU_C) * f["sp"])
        dsp = jnp.sum(d_la * ((-LRU_C) * r), axis=0, keepdims=True)
        dlam = dsp * (-_sigmoid(-lam_ref[...]))
        d_za = d_r * r * (1.0 - r)
        d_zx = d_ig * ig * (1.0 - ig)
        dzab = d_za.astype(BF16)
        dzxb = d_zx.astype(BF16)
        dwa_ref[...] += _dot_tn(f["xab"], dzab)
        dwx_ref[...] += _dot_tn(f["xab"], dzxb)
        d_xa = d_xa + _dot_nt(dzab, wa_ref[...]) + _dot_nt(dzxb, wx_ref[...])
        d_xp = sum(cw_ref[3 - j:4 - j, :] * _shift_up(d_xa, nxt8[...], j, row) for j in range(4))
        dcw = [jnp.sum(d_xa * _shift_down(xp, xprev8, 3 - k, row), axis=0, keepdims=True) for k in range(4)]
        dp_ref[:, 0:R] = d_xp.astype(BF16)
        dp_ref[:, R:2 * R] = d_ga.astype(BF16)
        dp8 = d_xa[0:8, :]
        nxt8[...] = dp8
        sm_ref[0:1, :] += jnp.sum(d_za, axis=0, keepdims=True)
        sm_ref[1:2, :] += jnp.sum(d_zx, axis=0, keepdims=True)
        sm_ref[2:3, :] += dlam
        sm_ref[3:4, :] += dnr
        sm_ref[4:5, :] += jnp.sum(d_xa, axis=0, keepdims=True)
        for k in range(4):
            sm_ref[8 + k:9 + k, :] += dcw[k]

        @pl.when(i == nt - 1)
        def _():
            for h in range(R // HEAD):
                dwab_ref[h] = dwa_ref[h * HEAD:(h + 1) * HEAD, h * HEAD:(h + 1) * HEAD].astype(BF16)
                dwxb_ref[h] = dwx_ref[h * HEAD:(h + 1) * HEAD, h * HEAD:(h + 1) * HEAD].astype(BF16)

    c0 = lambda shape: pl.BlockSpec(shape, lambda i: (0, 0))
    blocks = pl.BlockSpec((R // HEAD, HEAD, HEAD), lambda i: (0, 0, 0))
    rev = lambda i: nt - 1 - i
    prev8 = lambda i: (jnp.maximum((nt - 1 - i) * (RT // 8) - 1, 0), 0)
    return pl.pallas_call(
        body, name="rec_bwd", grid=(nt,),
        in_specs=[pl.BlockSpec(memory_space=pl.ANY),
                  pl.BlockSpec((RT, R), lambda i: (rev(i), 0)),
                  pl.BlockSpec((RT, 2 * R), lambda i: (rev(i), 0)), pl.BlockSpec((8, R), prev8),
                  pl.BlockSpec((RT, R), lambda i: (rev(i), 0)), pl.BlockSpec((8, R), prev8),
                  c0((4, R)), c0((1, R)), blocks, c0((1, R)), blocks, c0((1, R)), c0((1, R)), c0((1, R))],
        out_specs=[pl.BlockSpec((RT, 2 * R), lambda i: (rev(i), 0)), blocks, blocks, c0((16, R))],
        out_shape=[jax.ShapeDtypeStruct((S, E), BF16), jax.ShapeDtypeStruct((R // HEAD, HEAD, HEAD), BF16),
                   jax.ShapeDtypeStruct((R // HEAD, HEAD, HEAD), BF16), jax.ShapeDtypeStruct((16, R), F32)],
        scratch_shapes=[pltpu.VMEM((8, R), F32), pltpu.VMEM((8, R), F32), pltpu.VMEM((R, R), BF16),
                        pltpu.VMEM((R, R), BF16), pltpu.VMEM((R, R), F32), pltpu.VMEM((R, R), F32)],
        input_output_aliases={0: 0},
        compiler_params=_cp(("arbitrary",)),
    )(dproj, d_ya, proj, proj, h_all, h_all, conv_w, conv_b, wa_b, ba, wx_b, bx, lam, norm_rec)


NPAIR = R // LANES
QB, KB, VB, GB = 2 * R // LANES, 3 * R // LANES, 4 * R // LANES, 5 * R // LANES


def _rope_freq():
    half = HEAD // 2
    inv = np.float32(ROPE_THETA) ** (-(np.arange(half, dtype=np.float32) / np.float32(half)))
    return jnp.asarray(np.tile(inv.astype(np.float32), LANES // half)[None, :])


def _rot_half(x, first):
    return jnp.where(first, -pltpu.roll(x, LANES - HEAD // 2, 1), pltpu.roll(x, HEAD // 2, 1))


def _cos_sin(pos_ref, freq_ref):
    ang = pos_ref[...].astype(F32) * freq_ref[...]
    return jnp.cos(ang), jnp.sin(ang)


def _deint(src_ref, dst_ref, d):
    n = S // d
    for r in range(d):
        v = src_ref[pl.ds(r, n, stride=d), :] if d > 1 else src_ref[...]
        dst_ref[r * n:(r + 1) * n, :] = v.astype(dst_ref.dtype)


def _reint(src_ref, dst_ref, d, accumulate):
    n = S // d
    for r in range(d):
        idx = (pl.ds(r, n, stride=d), slice(None)) if d > 1 else (slice(None), slice(None))
        v = src_ref[r * n:(r + 1) * n, :]
        if accumulate:
            dst_ref[idx] = dst_ref[idx] + v
        else:
            dst_ref[idx] = v


def _deint_heads(src_ref, dst0, dst1, d, both=None):
    n = S // d
    hm0 = lax.broadcasted_iota(jnp.int32, (n, LANES), 1) < HEAD
    for r in range(d):
        v = src_ref[pl.ds(r, n, stride=d), :] if d > 1 else src_ref[...]
        dst0[r * n:(r + 1) * n, :] = jnp.where(hm0, v, 0.0).astype(BF16)
        dst1[r * n:(r + 1) * n, :] = jnp.where(hm0, 0.0, v).astype(BF16)
        if both is not None:
            both[r * n:(r + 1) * n, :] = v.astype(BF16)


def _heads(ref, pi, st):
    x = ref[pi, pl.ds(st, BLK), :]
    hm0 = lax.broadcasted_iota(jnp.int32, (BLK, LANES), 1) < HEAD
    zero = jnp.zeros_like(x)
    return jnp.concatenate([jnp.where(hm0, x, zero), jnp.where(hm0, zero, x)], axis=0)


def _reint_prev(src_ref, dst_ref, d):
    n = S // d
    if n == BLK:
        return
    for r in range(d):
        idx = (pl.ds(r, n - BLK, stride=d), slice(None)) if d > 1 else (slice(0, n - BLK), slice(None))
        dst_ref[idx] = dst_ref[idx] + src_ref[r * n + BLK:(r + 1) * n, :]


def _pair_masks():
    qi = lax.broadcasted_iota(jnp.int32, (BLK, 2 * BLK), 0)
    ki = lax.broadcasted_iota(jnp.int32, (BLK, 2 * BLK), 1) & (BLK - 1)
    return ki <= qi, ki >= qi


def _two(ref0, ref1, st, axis):
    return jnp.concatenate([ref0[pl.ds(st, BLK), :], ref1[pl.ds(st, BLK), :]], axis=axis)


ATT_UNROLL = 8


def _att_fwd(proj, cos, sin, w_out):
    def body(q_ref, k_ref, v_ref, cos_ref, sin_ref, w_ref, att_ref, lse_ref, qd_ref, kd_ref, vd_ref, wbf_ref,
             qr_ref, kr_ref, kd0, kd1, vd0, vd1, od, ld, on, ln, wbuf, *wsems):
        wg = _WeightGather(w_ref, wbuf, *wsems)
        pl.when(pl.program_id(0) == 0)(wg.start)
        pl.when(pl.program_id(0) == 1)(wg.forward)
        lane = lax.broadcasted_iota(jnp.int32, (S, LANES), 1)
        first = (lane & (HEAD // 2)) == 0
        cos, sin = cos_ref[...], sin_ref[...]
        q = q_ref[...]
        k = k_ref[...]
        qr_ref[...] = (q * cos + _rot_half(q, first) * sin) * (HEAD ** -0.5)
        kr_ref[...] = k * cos + _rot_half(k, first) * sin
        hm0 = lax.broadcasted_iota(jnp.int32, (BLK, LANES), 1) < HEAD
        top = lax.broadcasted_iota(jnp.int32, (2 * BLK, LANES), 0) < BLK
        ones2 = (top == (lax.broadcasted_iota(jnp.int32, (2 * BLK, LANES), 1) < HEAD)).astype(BF16)
        mc2, mp2 = _pair_masks()

        for pi, d in enumerate(PATTERNS):
            nb = S // d // BLK
            qd = qd_ref.at[pi]
            _deint(qr_ref, qd, d)
            _deint_heads(kr_ref, kd0, kd1, d, kd_ref.at[pi])
            _deint_heads(v_ref, vd0, vd1, d, vd_ref.at[pi])

            def blk(b, carry):
                st = pl.multiple_of(b * BLK, BLK)
                qb = qd[pl.ds(st, BLK), :]
                sc = jnp.where(mc2, _dot_nt(qb, _two(kd0, kd1, st, 0)), NEG)
                mx = sc
                if nb > 1:
                    stp = pl.multiple_of(jnp.maximum(b - 1, 0) * BLK, BLK)
                    mp = jnp.logical_and(mp2, lax.rem(b, nb) != 0)
                    sp = jnp.where(mp, _dot_nt(qb, _two(kd0, kd1, stp, 0)), NEG)
                    mx = jnp.maximum(sc, sp)
                m0 = jnp.max(mx[:, 0:BLK], axis=1, keepdims=True)
                m1 = jnp.max(mx[:, BLK:2 * BLK], axis=1, keepdims=True)
                mf = jnp.concatenate([jnp.broadcast_to(m0, (BLK, BLK)), jnp.broadcast_to(m1, (BLK, BLK))], axis=1)
                o = _dot(jnp.exp(sc - mf).astype(BF16), jnp.concatenate([_two(vd0, vd1, st, 0), ones2], axis=1))
                if nb > 1:
                    o = o + _dot(jnp.exp(sp - mf).astype(BF16), jnp.concatenate([_two(vd0, vd1, stp, 0), ones2], axis=1))
                l = o[:, LANES:2 * LANES]
                od[pl.ds(st, BLK), :] = o[:, 0:LANES] / l
                ld[pl.ds(st, BLK), :] = jnp.where(hm0, m0, m1) + jnp.log(l)
                return carry

            lax.fori_loop(0, S // BLK, blk, 0, unroll=ATT_UNROLL)
            _reint(od, on.at[pi], d, False)
            _reint(ld, ln.at[pi], d, False)

        l0, l1, l2 = ln[0], ln[1], ln[2]
        m = jnp.maximum(jnp.maximum(l0, l1), l2)
        e0, e1, e2 = jnp.exp(l0 - m), jnp.exp(l1 - m), jnp.exp(l2 - m)
        den = e0 + e1 + e2
        att_ref[...] = (e0 * on[0] + e1 * on[1] + e2 * on[2]) / den
        lse_ref[...] = m + jnp.log(den)

        @pl.when(pl.program_id(0) == NPAIR - 1)
        def _():
            wg.finish()
            wbf_ref[...] = wbuf[...]

    col = lambda c0: pl.BlockSpec((S, LANES), lambda p: (0, c0 + p))
    out = pl.BlockSpec((S, LANES), lambda p: (0, p))
    tab = pl.BlockSpec((S, LANES), lambda p: (0, 0))
    vm = pl.BlockSpec(memory_space=pltpu.VMEM)
    saved = pl.BlockSpec((len(PATTERNS), S, LANES), lambda p: (0, 0, p))
    return pl.pallas_call(
        body, name="att_fwd", grid=(NPAIR,),
        in_specs=[col(QB), col(KB), col(VB), tab, tab, vm],
        out_specs=[out, out, saved, saved, saved, vm],
        out_shape=[jax.ShapeDtypeStruct((S, R), F32)] * 2 + [jax.ShapeDtypeStruct((len(PATTERNS), S, R), BF16)] * 3
        + [jax.ShapeDtypeStruct((NCHIP,) + w_out.shape, BF16)],
        scratch_shapes=[pltpu.VMEM((S, LANES), F32)] * 2 + [pltpu.VMEM((S, LANES), BF16)] * 4
        + [pltpu.VMEM((S, LANES), F32)] * 2
        + [pltpu.VMEM((3, S, LANES), F32)] * 2 + [pltpu.VMEM((NCHIP,) + w_out.shape, BF16)] + _WeightGather.SEMS,
        compiler_params=_cp(("arbitrary",)),
    )(proj, proj, proj, cos, sin, w_out)


def _att_bwd(dproj, d_att, att, lse, qd_all, kd_all, vd_all, cos, sin, gw_out4):
    out_units = [(j, j, 0) for j in range(NCHIP)]

    nblk = S // BLK

    def body(dp_in, do_ref, o_ref, lse_ref, qd_ref, kd_ref, vd_ref, cos_ref, sin_ref, gw_ref, dp_ref, gout_ref,
             dod, kt, packn, packd, dqd, dkcd, dkpd, dvcd, dvpd,
             dqn, dkn, dvn, rows, trs, pts, dss, stage, sems, gred, *rs_scratch):
        p = pl.program_id(0)
        rs = _ReduceScatter(gw_ref, gred, out_units, *rs_scratch)
        for step, piece in enumerate((rs.start_halves, rs.send_partials, rs.reduce_owned)):
            pl.when(p == step)(piece)

        @pl.when(p == NPAIR - 1)
        def _():
            rs.finish()
            gout_ref[...] = gred[...]

        lane = lax.broadcasted_iota(jnp.int32, (S, LANES), 1)
        hms = lane < HEAD
        prod = do_ref[...] * o_ref[...]
        d0 = jnp.sum(jnp.where(hms, prod, 0.0), axis=1, keepdims=True)
        d1 = jnp.sum(jnp.where(hms, 0.0, prod), axis=1, keepdims=True)
        lse = lse_ref[...]
        quarter = HEAD // 2
        packn[...] = jnp.where(lane < quarter, lse,
                               jnp.where(hms, pltpu.roll(lse, LANES - quarter, 1), jnp.where(lane < 3 * quarter, d0, d1)))
        dqn[...] = jnp.zeros_like(dqn)
        dkn[...] = jnp.zeros_like(dkn)
        dvn[...] = jnp.zeros_like(dvn)
        hm0 = lax.broadcasted_iota(jnp.int32, (BLK, LANES), 1) < HEAD
        key = lax.broadcasted_iota(jnp.int32, (2 * BLK, BLK), 0) & (BLK - 1)
        qry = lax.broadcasted_iota(jnp.int32, (2 * BLK, BLK), 1)
        mct, mpt = key <= qry, key >= qry

        for pi, d in enumerate(PATTERNS):
            nb = S // d // BLK
            _deint(do_ref, dod, d)
            _deint(packn, packd, d)

            sides = (0, 1) if nb > 1 else (0,)

            def probs(b, carry):
                st = pl.multiple_of(b * BLK, BLK)
                kt[b] = _heads(kd_ref, pi, st).astype(F32).T.astype(BF16)
                trs[b] = packd[pl.ds(st, BLK), :].T
                for j in range(4):
                    rows[b, j:j + 1, :] = trs[b, j * quarter:j * quarter + 1, :]
                qb, dob = qd_ref[pi, pl.ds(st, BLK), :], dod[pl.ds(st, BLK), :]
                both = lambda j: jnp.concatenate([jnp.broadcast_to(rows[b, j:j + 1, :], (BLK, BLK)),
                                                  jnp.broadcast_to(rows[b, j + 1:j + 2, :], (BLK, BLK))], axis=0)
                lbt, dlt = both(0), both(2)
                for sd in sides:
                    stk = pl.multiple_of(jnp.maximum(b - sd, 0) * BLK, BLK)
                    mask = mct if sd == 0 else jnp.logical_and(mpt, lax.rem(b, nb) != 0)
                    k2, v2 = _heads(kd_ref, pi, stk), _heads(vd_ref, pi, stk)
                    pt = jnp.where(mask, jnp.exp(_dot_nt(k2, qb) - lbt), 0.0)
                    pts[b, sd] = pt.astype(BF16)
                    dss[b, sd] = (pt * (_dot_nt(v2, dob) - dlt)).astype(BF16)
                return carry

            lax.fori_loop(0, nblk, probs, 0, unroll=ATT_UNROLL)

            def prods(b, carry):
                st = pl.multiple_of(b * BLK, BLK)
                qb, dob = qd_ref[pi, pl.ds(st, BLK), :], dod[pl.ds(st, BLK), :]
                dq_t = None
                for sd in sides:
                    dst, ptb = dss[b, sd], pts[b, sd]
                    rk, rv = _dot(dst, qb), _dot(ptb, dob)
                    dqs = _dot(kt[jnp.maximum(b - sd, 0)], dst)
                    dq_t = dqs if dq_t is None else dq_t + dqs
                    dk, dv = (dkcd, dvcd) if sd == 0 else (dkpd, dvpd)
                    dk[pl.ds(st, BLK), :] = jnp.where(hm0, rk[0:BLK], rk[BLK:2 * BLK])
                    dv[pl.ds(st, BLK), :] = jnp.where(hm0, rv[0:BLK], rv[BLK:2 * BLK])
                dqd[pl.ds(st, BLK), :] = dq_t.T
                return carry

            lax.fori_loop(0, nblk, prods, 0, unroll=ATT_UNROLL)
            _reint(dqd, dqn, d, True)
            _reint(dkcd, dkn, d, True)
            _reint(dvcd, dvn, d, True)
            _reint_prev(dkpd, dkn, d)
            _reint_prev(dvpd, dvn, d)

        lane = lax.broadcasted_iota(jnp.int32, (S, LANES), 1)
        first = (lane & (HEAD // 2)) == 0
        cos, sin = cos_ref[...], sin_ref[...]
        dq = dqn[...] * (HEAD ** -0.5)
        dk = dkn[...]
        stage[0] = (dq * cos - _rot_half(dq, first) * sin).astype(BF16)
        stage[1] = (dk * cos - _rot_half(dk, first) * sin).astype(BF16)
        stage[2] = dvn[...].astype(BF16)
        copies = [pltpu.make_async_copy(stage.at[j], dp_ref.at[:, pl.ds((2 + j) * R + p * LANES, LANES)], sems.at[j])
                  for j in range(3)]
        for cp in copies:
            cp.start()
        for cp in copies:
            cp.wait()

    blk = pl.BlockSpec((S, LANES), lambda p: (0, p))
    tab = pl.BlockSpec((S, LANES), lambda p: (0, 0))
    vm = pl.BlockSpec(memory_space=pltpu.VMEM)
    saved = pl.BlockSpec((len(PATTERNS), S, LANES), lambda p: (0, 0, p))
    _, orows, ocols = gw_out4.shape
    return pl.pallas_call(
        body, name="att_bwd", grid=(NPAIR,),
        in_specs=[pl.BlockSpec(memory_space=pl.ANY), blk, blk, blk, saved, saved, saved, tab, tab, vm],
        out_specs=[pl.BlockSpec(memory_space=pl.ANY), vm],
        out_shape=[jax.ShapeDtypeStruct((S, E), BF16), jax.ShapeDtypeStruct((orows, ocols), F32)],
        scratch_shapes=[pltpu.VMEM((S, LANES), BF16)] + [pltpu.VMEM((nblk, LANES, 2 * BLK), BF16)]
        + [pltpu.VMEM((S, LANES), F32)] * 10
        + [pltpu.VMEM((nblk, 8, BLK), F32), pltpu.VMEM((nblk, LANES, BLK), F32)]
        + [pltpu.VMEM((nblk, 2, 2 * BLK, BLK), BF16)] * 2
        + [pltpu.VMEM((3, S, LANES), BF16), pltpu.SemaphoreType.DMA((3,)), pltpu.VMEM((orows, ocols), F32)]
        + _ReduceScatter.scratch(NCHIP, orows, ocols, 1),
        input_output_aliases={0: 0},
        compiler_params=_cp(("arbitrary",)),
    )(dproj, d_att, att, lse, qd_all, kd_all, vd_all, cos, sin, gw_out4)


def _out_fwd_bwd(ya, att, proj, w_out_bf, x, target, mod, norm_post, norm_att):
    ts = 512

    def body(ya_ref, att_ref, gb_ref, w_ref, x_ref, t_ref, mod_ref, npost_ref, natt_ref,
             gx_ref, dya_ref, datt_ref, dgb_ref, gw_ref, acc_ref):
        i = pl.program_id(0)

        @pl.when(i == 0)
        def _():
            gw_ref[...] = jnp.zeros_like(gw_ref)
            acc_ref[...] = jnp.zeros_like(acc_ref)

        gate = mod_ref[:, 2 * D:3 * D]
        att = att_ref[...]
        gb = gb_ref[...]
        sg = _sigmoid(gb)
        silu = gb * sg
        ybp = att * silu
        yb, ybn, rstd_b = _rms_fwd(ybp, natt_ref[...])
        cat = jnp.concatenate([ya_ref[...], yb.astype(BF16)], axis=1)
        mix = _dot(cat, w_ref[...])
        rn, mn, rstd_m = _rms_fwd(mix, npost_ref[...])
        err = x_ref[...] + gate * rn - t_ref[...]
        dy = err * (1.0 / D)
        gx_ref[...] = dy
        dmix, dnpost = _rms_bwd(dy * gate, mn, rstd_m, npost_ref[...])
        dmb = dmix.astype(BF16)
        gw_ref[...] += _dot_tn(cat, dmb)
        dcat = _dot_nt(dmb, w_ref[...])
        dya_ref[...] = dcat[:, 0:R]
        dybp, dnatt = _rms_bwd(dcat[:, R:2 * R], ybn, rstd_b, natt_ref[...])
        datt_ref[...] = dybp * silu
        dgb_ref[...] = (dybp * att * (sg * (1.0 + gb * (1.0 - sg)))).astype(BF16)
        acc_ref[0:1, :] += jnp.sum(dy * rn, axis=0, keepdims=True)
        acc_ref[1:2, :] += dnpost
        acc_ref[2:3, 0:R] += dnatt
        acc_ref[3:4, :] += jnp.sum(jnp.sum(err * err, axis=1, keepdims=True), axis=0, keepdims=True)

    tile = lambda w: pl.BlockSpec((ts, w), lambda i: (i, 0))
    c0 = lambda shape: pl.BlockSpec(shape, lambda i: (0, 0))
    return pl.pallas_call(
        body, name="out_fwd_bwd", grid=(S // ts,),
        in_specs=[tile(R), tile(R), pl.BlockSpec((ts, R), lambda i: (i, 5)), c0((D, D)), tile(D), tile(D),
                  c0((1, 3 * D)), c0((1, D)), c0((1, R))],
        out_specs=[tile(D), tile(R), tile(R), pl.BlockSpec((ts, R), lambda i: (i, 5)), c0((D, D)), c0((8, D))],
        out_shape=[jax.ShapeDtypeStruct((S, D), F32), jax.ShapeDtypeStruct((S, R), F32),
                   jax.ShapeDtypeStruct((S, R), F32), jax.ShapeDtypeStruct((S, E), BF16),
                   jax.ShapeDtypeStruct((D, D), F32), jax.ShapeDtypeStruct((8, D), F32)],
        compiler_params=_cp(("arbitrary",)),
    )(ya, att, proj, w_out_bf, x, target, mod, norm_post, norm_att)


UC = 256
UPC = EC // UC


NU = E // UC


def _unit_of_step(i):
    return (i % NCHIP) * UPC + i // NCHIP


def _in_proj_bwd(ht, dproj, w_in_bf, x, gx1, mod, norm_pre, smalls):
    ts = 256
    nt = S // ts
    half = D // 2
    units = [_unit_of_step(k) for k in range(NU)]
    owners = [u // UPC for u in units]
    ns = len(smalls)

    def body(*refs):
        (ht_ref, dpu_ref, dp_ref, w_hbm, x_ref, gx1_ref, mod_ref, np_ref), refs = refs[:8], refs[8:]
        small_in, refs = refs[:ns], refs[ns:]
        (gx_ref, gin_ref), refs = refs[:2], refs[2:]
        small_out, (acc_out,), refs = refs[:ns], refs[ns:ns + 1], refs[ns + 1:]
        mine, sib, tmp, stage, got, red, acc_ref, hs, hr, ps, pr, bs, br = refs[:13]
        early = _SmallGather(small_in, small_out, *refs[13:16])
        late = _SmallGather([acc_ref], [acc_out], *refs[16:19])
        w_ref, w_sem = refs[19:21]
        i = pl.program_id(0)
        w_copy = pltpu.make_async_copy(w_hbm, w_ref, w_sem)
        pl.when(i == 0)(w_copy.start)
        pl.when(i == NU)(w_copy.wait)
        xx, yy, c = _me()
        ci = 2 * xx + yy
        r0 = pl.multiple_of(c * half, half)
        r1 = pl.multiple_of((1 - c) * half, half)
        pl.when(i == 0)(early.start)
        pl.when(i == NU)(early.forward)

        def exch(k):
            return _remote(tmp.at[k % 2], sib.at[k], hs.at[k], hr.at[k], 1)

        def partial(k, sender):
            return pltpu.make_async_remote_copy(
                src_ref=stage.at[k], dst_ref=got.at[units[k] % UPC, sender], send_sem=ps.at[k],
                recv_sem=pr.at[k, sender], device_id=(owners[k] // 2, owners[k] % 2, c), device_id_type=MESH)

        def back(k, start):
            off = (units[k] % UPC) * UC
            blk = red.at[pl.ds(start, half), off:off + UC]
            return _remote(blk, blk, bs.at[k], br.at[k], 1)

        for k in range(NU + 1):
            @pl.when(i == k)
            def _():
                if k < NU:
                    if k >= 2:
                        exch(k - 2).wait_send()
                    dpu = dpu_ref[...]
                    tmp[k % 2] = _dot(ht_ref[pl.ds(r1, half), :], dpu)
                    exch(k).start()
                    mine[k] = _dot(ht_ref[pl.ds(r0, half), :], dpu)
                if k >= 1:
                    exch(k - 1).wait_recv()
                    mine[k - 1] += sib[k - 1]

                    @pl.when(ci != owners[k - 1])
                    def _():
                        stage[k - 1] = mine[k - 1].astype(BF16)
                        partial(k - 1, ci).start()

        @pl.when(i == NU)
        def _():
            acc_ref[...] = jnp.zeros_like(acc_ref)

        @pl.when(i >= NU)
        def _():
            dh = sum(_dot_nt(dp_ref[:, j * EC:(j + 1) * EC], w_ref[j]) for j in range(NCHIP))
            hp, xn, rstd = _rms_fwd(x_ref[...], np_ref[...])
            dx, dnp = _rms_bwd(dh * (1.0 + mod_ref[:, D:2 * D]), xn, rstd, np_ref[...])
            gx_ref[...] = gx1_ref[...] + dx
            acc_ref[0:1, :] += jnp.sum(dh, axis=0, keepdims=True)
            acc_ref[1:2, :] += jnp.sum(dh * hp, axis=0, keepdims=True)
            acc_ref[2:3, :] += dnp

        for t in range(UPC):
            @pl.when(i == NU + 1 + 2 * t)
            def _():
                for k in range(NCHIP * t, NCHIP * (t + 1)):
                    @pl.when(ci == owners[k])
                    def _():
                        off = (units[k] % UPC) * UC
                        red[pl.ds(r0, half), off:off + UC] = mine[k]
                        for s in range(NCHIP):
                            if s != owners[k]:
                                partial(k, s).wait_recv()
                                red[pl.ds(r0, half), off:off + UC] += got[units[k] % UPC, s].astype(F32)
                        back(k, r0).start()

        @pl.when(i == NU + nt - 1)
        def _():
            late.start()
            exch(NU - 2).wait_send()
            exch(NU - 1).wait_send()
            for k in range(NU):
                @pl.when(ci == owners[k])
                def _():
                    back(k, r1).wait_recv()
                    back(k, r0).wait_send()

                @pl.when(ci != owners[k])
                def _():
                    partial(k, ci).wait_send()
            gin_ref[...] = red[...]
            early.finish()
            late.forward()
            late.finish()

    tile = lambda w: pl.BlockSpec((ts, w), lambda i: (jnp.maximum(i - NU, 0), 0))
    c0 = lambda shape: pl.BlockSpec(shape, lambda i: (0, 0))
    vm = pl.BlockSpec(memory_space=pltpu.VMEM)
    hbm = pl.BlockSpec(memory_space=pl.ANY)
    gathered = [jax.ShapeDtypeStruct((NDEV,) + a.shape, a.dtype) for a in smalls] + [jax.ShapeDtypeStruct((NDEV, 8, D), F32)]
    return pl.pallas_call(
        body, name="in_proj_bwd", grid=(NU + nt,),
        in_specs=[vm, pl.BlockSpec((S, UC), lambda i: (0, _unit_of_step(jnp.minimum(i, NU - 1)))), tile(E),
                  hbm, tile(D), tile(D), c0((1, 3 * D)), c0((1, D))] + [vm] * ns,
        out_specs=[tile(D), vm] + [hbm] * (ns + 1),
        out_shape=[jax.ShapeDtypeStruct((S, D), F32), jax.ShapeDtypeStruct((D, EC), F32)] + gathered,
        scratch_shapes=[pltpu.VMEM((NU, half, UC), F32), pltpu.VMEM((NU, half, UC), F32),
                        pltpu.VMEM((2, half, UC), F32), pltpu.VMEM((NU, half, UC), BF16),
                        pltpu.VMEM((UPC, NCHIP, half, UC), BF16), pltpu.VMEM((D, EC), F32), pltpu.VMEM((8, D), F32),
                        pltpu.SemaphoreType.DMA((NU,)), pltpu.SemaphoreType.DMA((NU,)),
                        pltpu.SemaphoreType.DMA((NU,)), pltpu.SemaphoreType.DMA((NU, NCHIP)),
                        pltpu.SemaphoreType.DMA((NU,)), pltpu.SemaphoreType.DMA((NU,))]
        + _SmallGather.sems(ns) + _SmallGather.sems(1)
        + [pltpu.VMEM((NCHIP, D, EC), BF16), pltpu.SemaphoreType.DMA],
        compiler_params=_cp(("arbitrary",)),
    )(ht, dproj, dproj, w_in_bf, x, gx1, mod, norm_pre, *smalls)


def _local_step(x, cos, sin, target, mod, w_in_bf, proj, ht, w_out, conv_w, p):
    rec_p = (conv_w, p["conv_b"], p["w_rg_a"], p["b_rg_a"], p["w_rg_x"], p["b_rg_x"], p["lru_lambda"], p["norm_rec"])
    h_all, ya = _rec_fwd(proj, *rec_p)
    att, lse, qd, kd, vd, w_out_bf = _att_fwd(proj, cos, sin, w_out)
    gx1, d_ya, d_att, dproj, gw_out, acc_o = _out_fwd_bwd(ya, att, proj, w_out_bf.reshape(D, D), x, target, mod,
                                                           p["norm_post"], p["norm_att"])
    dproj, g_out = _att_bwd(dproj, d_att, att, lse, qd, kd, vd, cos, sin, gw_out.reshape(NCHIP, D // NCHIP, D))
    dproj, dwa, dwx, sm = _rec_bwd(dproj, d_ya, proj, h_all, *rec_p)
    grad_x, g_in, *gathered = _in_proj_bwd(ht, dproj, w_in_bf, x, gx1, mod, p["norm_pre"], [acc_o, sm, dwa, dwx])
    return grad_x, g_in, g_out, gathered


def _me():
    return lax.axis_index("x"), lax.axis_index("y"), lax.axis_index("c")


def _flip(v, bit):
    return 1 - v if bit else v


def _peer(rel):
    x, y, c = _me()
    return (_flip(x, rel & 4), _flip(y, rel & 2), _flip(c, rel & 1))


def _remote(src, dst, send_sem, recv_sem, rel):
    return pltpu.make_async_remote_copy(src_ref=src, dst_ref=dst, send_sem=send_sem, recv_sem=recv_sem,
                                        device_id=_peer(rel), device_id_type=MESH)


class _WeightGather:
    SEMS = [pltpu.SemaphoreType.DMA((NCHIP - 1,))] * 4

    def __init__(self, w_ref, out_ref, send_sems, recv_sems, fsend_sems, frecv_sems):
        x, y, c = _me()
        self.w, self.out, self.ci = w_ref, out_ref, 2 * x + y
        self.half = w_ref.shape[0] // 2
        self.r0 = pl.multiple_of(c * self.half, self.half)
        self.r1 = pl.multiple_of((1 - c) * self.half, self.half)
        self.sems = (send_sems, recv_sems, fsend_sems, frecv_sems)

    def _ici(self, chip, k):
        blk = self.out.at[chip, pl.ds(self.r0, self.half), :]
        return _remote(blk, blk, self.sems[0].at[k - 1], self.sems[1].at[k - 1], 2 * k)

    def _d2d(self, chip, start, k):
        blk = self.out.at[chip, pl.ds(start, self.half), :]
        return _remote(blk, blk, self.sems[2].at[k - 1], self.sems[3].at[k - 1], 1)

    def start(self, diagonal=True):
        self.out[self.ci] = self.w[...].astype(BF16)
        for k in range(1, NCHIP if diagonal else NCHIP - 1):
            self._ici(self.ci, k).start()

    def _relay(self, chip, piece, k):
        q = self.half // 2
        blk = self.out.at[chip, pl.ds(self.r0 + piece * q, q), :]
        return _remote(blk, blk, self.relay_sems[0].at[piece], self.relay_sems[1].at[piece], 2 * k)

    def neighbours_landed(self, relay_send_sems, relay_recv_sems):
        self.relay_sems = (relay_send_sems, relay_recv_sems)
        for k in (1, 2):
            self._ici(self.ci ^ k, k).wait_recv()
        self._relay(self.ci ^ 2, 0, 1).start()
        self._relay(self.ci ^ 1, 1, 2).start()
        for k in (1, 2):
            self._d2d(self.ci ^ k, self.r0, k).start()

    def sibling_landed(self, k):
        self._d2d(self.ci ^ k, self.r1, k).wait_recv()

    def diagonal_landed(self):
        for piece, k in ((0, 1), (1, 2)):
            self._relay(self.ci ^ 3, piece, k).wait_recv()
        self._d2d(self.ci ^ 3, self.r0, 3).start()
        self._d2d(self.ci ^ 3, self.r1, 3).wait_recv()

    def finish_relayed(self):
        for k in (1, 2):
            self._ici(self.ci, k).wait_send()
        self._relay(self.ci ^ 2, 0, 1).wait_send()
        self._relay(self.ci ^ 1, 1, 2).wait_send()
        for k in range(1, NCHIP):
            self._d2d(self.ci ^ k, self.r0, k).wait_send()

    def forward(self):
        for k in range(1, NCHIP):
            self._ici(self.ci ^ k, k).wait_recv()
            self._d2d(self.ci ^ k, self.r0, k).start()

    def finish(self):
        for k in range(1, NCHIP):
            self._d2d(self.ci ^ k, self.r1, k).wait_recv()
        self.finish_sends()

    def finish_sends(self):
        for k in range(1, NCHIP):
            self._ici(self.ci, k).wait_send()
            self._d2d(self.ci ^ k, self.r0, k).wait_send()


class _SmallGather:
    @staticmethod
    def sems(n):
        return [pltpu.SemaphoreType.DMA((n, 7)), pltpu.SemaphoreType.DMA((n, 7)), pltpu.SemaphoreType.DMA((n,))]

    def __init__(self, srcs, outs, send_sems, recv_sems, local_sems):
        x, y, c = _me()
        self.srcs, self.outs = list(srcs), list(outs)
        self.ss, self.rs, self.ls = send_sems, recv_sems, local_sems
        self.ci, self.c = 2 * x + y, c
        self.me = 2 * self.ci + c

    def _own(self, a, slot, rel):
        return _remote(self.srcs[a], self.outs[a].at[self.me], self.ss.at[a, slot], self.rs.at[a, slot], rel)

    def _block(self, a, idx, slot, rel):
        blk = self.outs[a].at[idx]
        return _remote(blk, blk, self.ss.at[a, slot], self.rs.at[a, slot], rel)

    def _local(self, a):
        return pltpu.make_async_copy(self.srcs[a], self.outs[a].at[self.me], self.ls.at[a])

    def start(self):
        for a in range(len(self.srcs)):
            self._local(a).start()
            self._own(a, 0, 1).start()
            for k in range(1, NCHIP):
                self._own(a, k, 2 * k).start()

    def forward(self):
        for a in range(len(self.srcs)):
            for k in range(1, NCHIP):
                idx = 2 * (self.ci ^ k) + self.c
                self._block(a, idx, k, 2 * k).wait_recv()
                self._block(a, idx, 3 + k, 1).start()

    def finish(self):
        for a in range(len(self.srcs)):
            self._block(a, 2 * self.ci + 1 - self.c, 0, 1).wait_recv()
            for k in range(1, NCHIP):
                self._block(a, 2 * (self.ci ^ k) + 1 - self.c, 3 + k, 1).wait_recv()
            self._own(a, 0, 1).wait_send()
            for k in range(1, NCHIP):
                self._own(a, k, 2 * k).wait_send()
                self._block(a, 2 * (self.ci ^ k) + self.c, 3 + k, 1).wait_send()
            self._local(a).wait()


def _start_in_proj(crow, w_ada, b_cols, w_in, pos, x, norm_pre, order):
    ts = 512
    nt = S // ts
    wc = crow.shape[1]

    def body(order_ref, crow_ref, wada_ref, b_ref, win_ref, pos_ref, freq_ref, x_ref, np_ref,
             g0_ref, mod_ref, wbf_ref, cos_ref, sin_ref, proj_ref, ht_ref,
             g0s, modp, modb, wbuf, hb_all, cs, cr, ms, mr, ws, wr, fs, fr, local_sems, ys, yr, osem):
        s, t = pl.program_id(0), pl.program_id(1)
        x, y, c = _me()
        ci = 2 * x + y
        me = 2 * ci + c
        wg = _WeightGather(win_ref, wbuf, ws, wr, fs, fr)

        @pl.when(jnp.logical_and(s == 0, t == 0))
        def _():
            wg.start(diagonal=False)
            mine = pltpu.make_async_copy(crow_ref, g0s.at[pl.ds(me, 1), :], local_sems.at[0])
            mine.start()
            csend = [_remote(crow_ref, g0s.at[pl.ds(me, 1), :], cs.at[r - 1], cr.at[r - 1], r) for r in range(1, NDEV)]
            for cp in csend:
                cp.start()
            cos_ref[...], sin_ref[...] = _cos_sin(pos_ref, freq_ref)
            for r in range(1, NDEV):
                px, py, pc = _peer(r)
                _remote(crow_ref, g0s.at[pl.ds(4 * px + 2 * py + pc, 1), :], cs.at[r - 1], cr.at[r - 1], r).wait_recv()
            mine.wait()
            cv = g0s[:, 0:D]
            sc = cv * _sigmoid(cv)
            scb = jnp.concatenate([sc, jnp.zeros_like(sc)], axis=0).astype(BF16)
            modp[...] = _dot(scb, wada_ref[...].astype(BF16))[0:NDEV, :] + b_ref[...]
            own = pltpu.make_async_copy(modp.at[pl.ds(me, 1), :], modb.at[ci], local_sems.at[1])
            own.start()
            msend = []
            for k in range(1, NCHIP):
                cp = _remote(modp.at[pl.ds(2 * (ci ^ k) + c, 1), :], modb.at[ci], ms.at[k - 1], mr.at[k - 1], 2 * k)
                cp.start()
                msend.append(cp)
            for k in range(1, NCHIP):
                _remote(modp.at[pl.ds(me, 1), :], modb.at[ci ^ k], ms.at[k - 1], mr.at[k - 1], 2 * k).wait_recv()
            own.wait()
            for j in range(NCHIP):
                mod_ref[:, j * EC:(j + 1) * EC] = modb[j]
            for cp in csend + msend:
                cp.wait_send()
            g0_ref[...] = g0s[...]

        def keep(k):
            return pltpu.make_async_copy(wbuf.at[ci ^ k], wbf_ref.at[ci ^ k], osem.at[k])

        @pl.when(jnp.logical_and(s == 1, t == 0))
        def _():
            keep(0).start()
            wg.neighbours_landed(ys, yr)
            wg.sibling_landed(1)
            keep(1).start()

        @pl.when(jnp.logical_and(s == 2, t == 0))
        def _():
            wg.sibling_landed(2)
            keep(2).start()

        @pl.when(jnp.logical_and(s == 3, t == 0))
        def _():
            wg.relay_sems = (ys, yr)
            wg.diagonal_landed()
            keep(3).start()

        rows = pl.ds(pl.multiple_of(t * ts, ts), ts)

        @pl.when(s == 0)
        def _():
            hp, _, _ = _rms_fwd(x_ref[...], np_ref[...])
            h = hp * (1.0 + mod_ref[:, D:2 * D]) + mod_ref[:, 0:D]
            hb_all[rows, :] = h.astype(BF16)
            ht_ref[...] = h.T.astype(BF16)

        proj_ref[...] = _dot(hb_all[rows, :], wbuf[ci ^ s])

        @pl.when(jnp.logical_and(s == NCHIP - 1, t == nt - 1))
        def _():
            wg.relay_sems = (ys, yr)
            wg.finish_relayed()
            for k in range(NCHIP):
                keep(k).wait()

    vm = pl.BlockSpec(memory_space=pltpu.VMEM)
    first_pass = lambda s, t: jnp.where(s == 0, t, nt - 1)
    grid_spec = pltpu.PrefetchScalarGridSpec(
        num_scalar_prefetch=1, grid=(NCHIP, nt),
        in_specs=[vm, vm, vm, vm, vm, vm, pl.BlockSpec((ts, D), lambda s, t, o: (first_pass(s, t), 0)),
                  pl.BlockSpec((1, D), lambda s, t, o: (0, 0))],
        out_specs=[vm, vm, pl.BlockSpec(memory_space=pl.ANY), vm, vm, pl.BlockSpec((ts, EC), lambda s, t, o: (t, o[s])),
                   pl.BlockSpec((D, ts), lambda s, t, o: (0, first_pass(s, t)))],
        scratch_shapes=[pltpu.VMEM((NDEV, wc), F32), pltpu.VMEM((NDEV, EC), F32), pltpu.VMEM((NCHIP, 1, EC), F32),
                        pltpu.VMEM((NCHIP, D, EC), BF16), pltpu.VMEM((S, D), BF16),
                        pltpu.SemaphoreType.DMA((NDEV - 1,)), pltpu.SemaphoreType.DMA((NDEV - 1,)),
                        pltpu.SemaphoreType.DMA((NCHIP - 1,)), pltpu.SemaphoreType.DMA((NCHIP - 1,))]
        + _WeightGather.SEMS + [pltpu.SemaphoreType.DMA((2,))] * 3 + [pltpu.SemaphoreType.DMA((NCHIP,))])
    return pl.pallas_call(
        body, name="start_in_proj", grid_spec=grid_spec,
        out_shape=[jax.ShapeDtypeStruct((NDEV, wc), F32), jax.ShapeDtypeStruct((1, 3 * D), F32),
                   jax.ShapeDtypeStruct((NCHIP, D, EC), BF16), jax.ShapeDtypeStruct((S, LANES), F32),
                   jax.ShapeDtypeStruct((S, LANES), F32), jax.ShapeDtypeStruct((S, E), F32),
                   jax.ShapeDtypeStruct((D, S), BF16)],
        compiler_params=_cp(("arbitrary", "arbitrary")),
    )(order, crow, w_ada, b_cols, w_in, pos, _rope_freq(), x, norm_pre)


class _ReduceScatter:
    @staticmethod
    def scratch(n_units, rows, ucols, max_owned):
        half = rows // 2
        return [pltpu.VMEM((n_units, half, ucols), F32), pltpu.VMEM((n_units, half, ucols), BF16),
                pltpu.VMEM((max_owned, NCHIP, half, ucols), BF16),
                pltpu.SemaphoreType.DMA((2,)), pltpu.SemaphoreType.DMA((n_units,)),
                pltpu.SemaphoreType.DMA((n_units, NCHIP)), pltpu.SemaphoreType.DMA((n_units,)),
                pltpu.SemaphoreType.DMA((n_units,))]

    def __init__(self, g_ref, out_ref, units, sib, stage, got, sem1, send2, recv2, send3, recv3):
        x, y, c = _me()
        self.c, self.ci = c, 2 * x + y
        self.g, self.out, self.units = g_ref, out_ref, units
        self.sib, self.stage, self.got = sib, stage, got
        self.sem1, self.send2, self.recv2, self.send3, self.recv3 = sem1, send2, recv2, send3, recv3
        self.half = g_ref.shape[1] // 2
        self.ucols = g_ref.shape[2]
        self.r0 = pl.multiple_of(c * self.half, self.half)
        self.r1 = pl.multiple_of((1 - c) * self.half, self.half)
        self.slot0 = units[0][0]
        assert [u[0] for u in units] == list(range(self.slot0, self.slot0 + len(units)))
        seen = {}
        self.local = []
        for _, owner, _ in units:
            self.local.append(seen.get(owner, 0))
            seen[owner] = seen.get(owner, 0) + 1

    def _halves(self):
        n = len(self.units)
        return _remote(self.g.at[pl.ds(self.slot0, n), pl.ds(self.r1, self.half), :], self.sib,
                       self.sem1.at[0], self.sem1.at[1], 1)

    def _partial(self, i, sender):
        _, owner, _ = self.units[i]
        return pltpu.make_async_remote_copy(
            src_ref=self.stage.at[i], dst_ref=self.got.at[self.local[i], sender],
            send_sem=self.send2.at[i], recv_sem=self.recv2.at[i, sender],
            device_id=(owner // 2, owner % 2, self.c), device_id_type=MESH)

    def _back(self, i, start):
        off = self.units[i][2]
        blk = self.out.at[pl.ds(start, self.half), off:off + self.ucols]
        return _remote(blk, blk, self.send3.at[i], self.recv3.at[i], 1)

    def start_halves(self):
        self._halves().start()

    def send_partials(self):
        self._halves().wait_recv()
        for i, (slot, owner, _) in enumerate(self.units):
            @pl.when(self.ci != owner)
            def _():
                self.stage[i] = (self.g[slot, pl.ds(self.r0, self.half), :] + self.sib[i]).astype(BF16)
                self._partial(i, self.ci).start()

    def reduce_owned(self):
        for i, (slot, owner, off) in enumerate(self.units):
            @pl.when(self.ci == owner)
            def _():
                rows, cols = pl.ds(self.r0, self.half), slice(off, off + self.ucols)
                self.out[rows, cols] = self.g[slot, pl.ds(self.r0, self.half), :] + self.sib[i]
                for s in range(NCHIP):
                    if s != owner:
                        self._partial(i, s).wait_recv()
                        self.out[rows, cols] += self.got[self.local[i], s].astype(F32)
                self._back(i, self.r0).start()

    def finish(self):
        self._halves().wait_send()
        for i, (_, owner, _) in enumerate(self.units):
            @pl.when(self.ci == owner)
            def _():
                self._back(i, self.r1).wait_recv()
                self._back(i, self.r0).wait_send()

            @pl.when(self.ci != owner)
            def _():
                self._partial(i, self.ci).wait_send()


def _silu_rows(c_ref):
    cv = c_ref[...]
    sc = cv * _sigmoid(cv)
    return jnp.concatenate([sc, jnp.zeros_like(sc)], axis=0).astype(BF16)


def _adamw(w, g, m, v, name):
    rows, cols = w.shape
    tr = 256 if rows % 256 == 0 else rows

    def body(w_ref, g_ref, m_ref, v_ref, d_ref, nm_ref, nv_ref):
        d_ref[...], nm_ref[...], nv_ref[...] = _adamw_values(w_ref[...], g_ref[...], m_ref[...], v_ref[...])

    spec = pl.BlockSpec((tr, cols), lambda i: (i, 0))
    return pl.pallas_call(
        body, name=name, grid=(rows // tr,), in_specs=[spec] * 4, out_specs=[spec] * 3,
        out_shape=[jax.ShapeDtypeStruct((rows, cols), F32)] * 3,
        compiler_params=_cp(("parallel",)),
    )(w, g, m, v)


def _adamw_values(w, g, m, v):
    nm = B1 * m + (1.0 - B1) * g
    nv = B2 * v + (1.0 - B2) * (g * g)
    m_hat = nm / (1.0 - B1 ** STEP)
    v_hat = nv / (1.0 - B2 ** STEP)
    return (-LR) * (m_hat / (jnp.sqrt(v_hat) + ADAM_EPS) + WD * w), nm, nv


NB = R // HEAD
SMALL = (("b_ada", (1, 3 * D)), ("norm_pre", (1, D)), ("norm_post", (1, D)), ("conv_w", (4, R // NCHIP)),
         ("conv_b", (1, R)), ("w_rg_a", (NB, HEAD, HEAD)), ("b_rg_a", (1, R)), ("w_rg_x", (NB, HEAD, HEAD)),
         ("b_rg_x", (1, R)), ("lru_lambda", (1, R)), ("norm_rec", (1, R)), ("norm_att", (1, R)))


def _small_update(ao8, sm8, dwa8, dwx8, ai8, cg, params):
    n = len(SMALL)

    def body(ao_ref, sm_ref, dwa_ref, dwx_ref, ai_ref, cg_ref, *refs):
        pin, pout, (gada_ref, loss_ref, dmod) = refs[:3 * n], refs[3 * n:7 * n], refs[7 * n:]
        xx, yy, _ = _me()
        ci = 2 * xx + yy

        def total(ref, *idx):
            acc = ref[(0,) + idx].astype(F32)
            for d in range(1, NDEV):
                acc = acc + ref[(d,) + idx].astype(F32)
            return acc

        row = lambda ref, r, lanes=slice(None): total(ref, slice(r, r + 1), lanes)
        mine = lambda parts: sum(jnp.where(ci == j, part, 0.0) for j, part in enumerate(parts))
        cw = R // NCHIP
        grads = {
            "b_ada": [jnp.concatenate([row(ai_ref, 0), row(ai_ref, 1), row(ao_ref, 0)], axis=1)],
            "norm_pre": [row(ai_ref, 2)], "norm_post": [row(ao_ref, 1)],
            "conv_w": [mine([row(sm_ref, 8 + r, slice(j * cw, (j + 1) * cw)) for j in range(NCHIP)]) for r in range(4)],
            "conv_b": [row(sm_ref, 4)], "b_rg_a": [row(sm_ref, 0)], "b_rg_x": [row(sm_ref, 1)],
            "lru_lambda": [row(sm_ref, 2)], "norm_rec": [row(sm_ref, 3)], "norm_att": [row(ao_ref, 2, slice(0, R))],
            "w_rg_a": [total(dwa_ref, h) for h in range(NB)], "w_rg_x": [total(dwx_ref, h) for h in range(NB)],
        }
        loss_ref[...] = row(ao_ref, 3, slice(0, LANES)) * (0.5 / D)
        for k, (name, shape) in enumerate(SMALL):
            w_ref, m_ref, v_ref = pin[3 * k:3 * k + 3]
            outs = pout[4 * k:4 * k + 4]
            for r, g in enumerate(grads[name]):
                at = (slice(None),) if len(grads[name]) == 1 else ((r,) if len(shape) == 3 else (slice(r, r + 1),))
                res = (g,) + _adamw_values(w_ref[at], g, m_ref[at], v_ref[at])
                for o_ref, val in zip(outs, res):
                    o_ref[at] = val
        for d in range(NDEV):
            dmod[d:d + 1, :] = jnp.concatenate([ai_ref[d, 0:1, :], ai_ref[d, 1:2, :], ao_ref[d, 0:1, :]], axis=1)
        cols = mine([dmod[:, j * EC:(j + 1) * EC] for j in range(NCHIP)])
        colsb = jnp.concatenate([cols, jnp.zeros_like(cols)], axis=0).astype(BF16)
        gada_ref[...] = _dot_tn(_silu_rows(cg_ref), colsb)

    shapes = [jax.ShapeDtypeStruct(s, F32) for _, s in SMALL]
    outs = pl.pallas_call(
        body, name="small_update",
        out_shape=[s for s in shapes for _ in range(4)] + [jax.ShapeDtypeStruct((D, EC), F32),
                                                           jax.ShapeDtypeStruct((1, LANES), F32)],
        scratch_shapes=[pltpu.VMEM((NDEV, 3 * D), F32)],
        compiler_params=_cp(),
    )(ao8, sm8, dwa8, dwx8, ai8, cg, *params)
    return outs[:4 * n], outs[4 * n], outs[4 * n + 1]


BIG = ("w_ada", "w_in", "w_out")
WEIGHTS = ("w_ada", "b_ada", "norm_pre", "norm_post", "w_in", "conv_w", "conv_b", "w_rg_a", "b_rg_a", "w_rg_x",
           "b_rg_x", "lru_lambda", "norm_rec", "norm_att", "w_out")


def kernel(x, c, positions, w_ada, b_ada, norm_pre, norm_post, w_in, conv_w, conv_b, w_rg_a, b_rg_a, w_rg_x, b_rg_x, lru_lambda, norm_rec, norm_att, w_out, loss_target, m_w_ada, m_b_ada, m_norm_pre, m_norm_post, m_w_in, m_conv_w, m_conv_b, m_w_rg_a, m_b_rg_a, m_w_rg_x, m_b_rg_x, m_lru_lambda, m_norm_rec, m_norm_att, m_w_out, v_w_ada, v_b_ada, v_norm_pre, v_norm_post, v_w_in, v_conv_w, v_conv_b, v_w_rg_a, v_b_rg_a, v_w_rg_x, v_b_rg_x, v_lru_lambda, v_norm_rec, v_norm_att, v_w_out):
    given = dict(locals())
    wts = {n: given[n] for n in WEIGHTS}
    ms = {n: given["m_" + n] for n in WEIGHTS}
    vs = {n: given["v_" + n] for n in WEIGHTS}
    xi, yi, _ = _me()
    chip = 2 * xi + yi
    cw_loc = R // NCHIP

    b_cols = lax.dynamic_slice(b_ada, (0, chip * EC), (1, EC))
    order = (chip ^ jnp.arange(NCHIP, dtype=jnp.int32)).astype(jnp.int32)
    g0, mod, w_in_bf, cos, sin, proj, ht = _start_in_proj(
        jnp.concatenate([c, conv_w.reshape(1, 4 * cw_loc)], axis=1), w_ada[0], b_cols, w_in[0],
        positions.reshape(S, 1), x[0], norm_pre, order)
    cg = g0[:, 0:D]
    conv_full = g0[0::2, D:].reshape(NCHIP, 4, cw_loc).transpose(1, 0, 2).reshape(4, R)

    p = dict(norm_pre=norm_pre, norm_post=norm_post, conv_b=conv_b, b_rg_a=b_rg_a, b_rg_x=b_rg_x,
             lru_lambda=lru_lambda, norm_rec=norm_rec, norm_att=norm_att, w_rg_a=w_rg_a[0], w_rg_x=w_rg_x[0])
    grad_x, g_in, g_out, gathered = _local_step(
        x[0], cos, sin, loss_target[0], mod, w_in_bf, proj, ht, w_out[0], conv_full, p)

    params = [d[n].reshape(shape) for n, shape in SMALL for d in (wts, ms, vs)]
    small_out, g_ada, loss_row = _small_update(*gathered, cg, params)
    grads = {"w_out": g_out, "w_in": g_in, "w_ada": g_ada}
    delta, new_m, new_v = {}, {}, {}
    for k, (n, _) in enumerate(SMALL):
        grads[n], delta[n], new_m[n], new_v[n] = small_out[4 * k:4 * k + 4]
    for n in BIG:
        delta[n], new_m[n], new_v[n] = _adamw(wts[n][0], grads[n], ms[n][0], vs[n][0], "adamw_" + n)
    out = lambda d: [d[n].reshape(wts[n].shape) for n in WEIGHTS]
    return (loss_row[0, 0], grad_x.reshape(x.shape), *out(grads), *out(delta), *out(new_m), *out(new_v))
```

```python
import numpy as np
import jax
import jax.numpy as jnp
from jax import lax
from jax.experimental import pallas as pl
from jax.experimental.pallas import tpu as pltpu

F32 = jnp.float32
BF16 = jnp.bfloat16

S = 2048
D = 1024
E = 3072
R = 512
NDEV = 8
NCHIP = 4
EC = 768
LRU_C = 8.0
EPS = 1e-6
NEG = -1e30
HEAD = 64
BLK = 128
PATTERNS = (1, 4, 16)
ROPE_THETA = 10000.0
LANES = 128
VMEM_LIMIT = 56 * 1024 * 1024

B1, B2, LR, WD, ADAM_EPS, STEP = 0.9, 0.999, 0.001, 0.01, 1e-8, 10
MESH = pl.DeviceIdType.MESH


def _cp(sem=None, **kw):
    return pltpu.CompilerParams(dimension_semantics=sem, vmem_limit_bytes=VMEM_LIMIT, **kw)


def _dot(a, b):
    return jnp.dot(a, b, preferred_element_type=F32)


def _dot_nt(a, b):
    return lax.dot_general(a, b, (((1,), (1,)), ((), ())), preferred_element_type=F32)


def _dot_tn(a, b):
    return lax.dot_general(a, b, (((0,), (0,)), ((), ())), preferred_element_type=F32)


def _sigmoid(x):
    return 1.0 / (1.0 + jnp.exp(-x))


def _expm1(x):
    poly = x * (1.0 + x * (0.5 + x * (1.0 / 6 + x * (1.0 / 24 + x * (1.0 / 120 + x * (1.0 / 720))))))
    return jnp.where(jnp.abs(x) < 0.3, poly, jnp.exp(x) - 1.0)


def _rms_fwd(v, g):
    rstd = lax.rsqrt(jnp.mean(v * v, axis=-1, keepdims=True) + EPS)
    vn = v * rstd
    return vn * g, vn, rstd


def _rms_bwd(dy, vn, rstd, g):
    dvn = dy * g
    dv = rstd * (dvn - vn * jnp.mean(dvn * vn, axis=-1, keepdims=True))
    return dv, jnp.sum(dy * vn, axis=0, keepdims=True)


RT = 256


def _shift_down(cur, prev8, j, row):
    if j == 0:
        return cur
    top = jnp.tile(pltpu.roll(prev8, j, 0), (RT // 8, 1))
    return jnp.where(row >= j, pltpu.roll(cur, j, 0), top)


def _shift_up(cur, next8, j, row):
    if j == 0:
        return cur
    bot = jnp.tile(pltpu.roll(next8, 8 - j, 0), (RT // 8, 1))
    return jnp.where(row < RT - j, pltpu.roll(cur, RT - j, 0), bot)


def _rec_gates(xp, xprev8, row, cw_ref, cb_ref, wa_ref, ba_ref, wx_ref, bx_ref, lam_ref):
    xa = cb_ref[...] + sum(cw_ref[3 - j:4 - j, :] * _shift_down(xp, xprev8, j, row) for j in range(4))
    xab = xa.astype(BF16)
    r = _sigmoid(_dot(xab, wa_ref[...]) + ba_ref[...])
    ig = _sigmoid(_dot(xab, wx_ref[...]) + bx_ref[...])
    nl = -lam_ref[...]
    sp = jnp.maximum(nl, 0.0) + jnp.log1p(jnp.exp(-jnp.abs(nl)))
    la = (-LRU_C) * r * sp
    a = jnp.exp(la)
    mult = jnp.sqrt(-_expm1(2.0 * la))
    return dict(xa=xa, xab=xab, r=r, ig=ig, sp=sp, la=la, a=a, mult=mult)


def _scan_fwd(a, u, row):
    sh = 1
    while sh < RT:
        a_s = jnp.where(row >= sh, pltpu.roll(a, sh, 0), 1.0)
        u_s = jnp.where(row >= sh, pltpu.roll(u, sh, 0), 0.0)
        u = a * u_s + u
        a = a * a_s
        sh *= 2
    return a, u


def _scan_bwd(al, g, row):
    sh = 1
    while sh < RT:
        al_s = jnp.where(row < RT - sh, pltpu.roll(al, RT - sh, 0), 1.0)
        g_s = jnp.where(row < RT - sh, pltpu.roll(g, RT - sh, 0), 0.0)
        g = g + al * g_s
        al = al * al_s
        sh *= 2
    return g


def _dense_from_blocks(blocks_ref, dense_ref):
    dense_ref[...] = jnp.zeros_like(dense_ref)
    for h in range(R // HEAD):
        dense_ref[h * HEAD:(h + 1) * HEAD, h * HEAD:(h + 1) * HEAD] = blocks_ref[h].astype(dense_ref.dtype)


def _rec_fwd(proj, conv_w, conv_b, wa_b, ba, wx_b, bx, lam, norm_rec):
    nt = S // RT

    def body(p_ref, cw_ref, cb_ref, wa_ref, ba_ref, wx_ref, bx_ref, lam_ref, nr_ref,
             h_ref, ya_ref, prev8, hc, wad, wxd):
        i = pl.program_id(0)

        @pl.when(i == 0)
        def _():
            prev8[...] = jnp.zeros_like(prev8)
            hc[...] = jnp.zeros_like(hc)
            _dense_from_blocks(wa_ref, wad)
            _dense_from_blocks(wx_ref, wxd)

        row = lax.broadcasted_iota(jnp.int32, (RT, R), 0)
        xp = p_ref[:, 0:R]
        ga = p_ref[:, R:2 * R]
        f = _rec_gates(xp, prev8[...], row, cw_ref, cb_ref, wad, ba_ref, wxd, bx_ref, lam_ref)
        u = f["mult"] * (f["ig"] * f["xa"])
        acum, hh = _scan_fwd(f["a"], u, row)
        h = hh + acum * hc[0:1, :]
        h_ref[...] = h
        hc[0:1, :] = h_ref[RT - 1:RT, :]
        prev8[...] = p_ref[RT - 8:RT, 0:R]
        yp = h * (ga * _sigmoid(ga))
        ya, _, _ = _rms_fwd(yp, nr_ref[...])
        ya_ref[...] = ya.astype(BF16)

    row1 = lambda n: pl.BlockSpec((1, n), lambda i: (0, 0))
    blocks = pl.BlockSpec((R // HEAD, HEAD, HEAD), lambda i: (0, 0, 0))
    return pl.pallas_call(
        body, name="rec_fwd", grid=(nt,),
        in_specs=[pl.BlockSpec((RT, 2 * R), lambda i: (i, 0)), pl.BlockSpec((4, R), lambda i: (0, 0)), row1(R),
                  blocks, row1(R), blocks, row1(R), row1(R), row1(R)],
        out_specs=[pl.BlockSpec((RT, R), lambda i: (i, 0)), pl.BlockSpec((RT, R), lambda i: (i, 0))],
        out_shape=[jax.ShapeDtypeStruct((S, R), F32), jax.ShapeDtypeStruct((S, R), BF16)],
        scratch_shapes=[pltpu.VMEM((8, R), F32), pltpu.VMEM((8, R), F32), pltpu.VMEM((R, R), BF16),
                        pltpu.VMEM((R, R), BF16)],
        compiler_params=_cp(("arbitrary",)),
    )(proj, conv_w, conv_b, wa_b, ba, wx_b, bx, lam, norm_rec)


def _rec_bwd(dproj, d_ya, proj, h_all, conv_w, conv_b, wa_b, ba, wx_b, bx, lam, norm_rec):
    nt = S // RT

    def body(dp_in, dya_ref, p_ref, pprev_ref, h_ref, hprev_ref, cw_ref, cb_ref, wab_ref, ba_ref, wxb_ref, bx_ref,
             lam_ref, nr_ref, dp_ref, dwab_ref, dwxb_ref, sm_ref, nxt8, cg, wa_ref, wx_ref, dwa_ref, dwx_ref):
        i = pl.program_id(0)
        ti = nt - 1 - i

        @pl.when(i == 0)
        def _():
            nxt8[...] = jnp.zeros_like(nxt8)
            cg[...] = jnp.zeros_like(cg)
            dwa_ref[...] = jnp.zeros_like(dwa_ref)
            dwx_ref[...] = jnp.zeros_like(dwx_ref)
            sm_ref[...] = jnp.zeros_like(sm_ref)
            _dense_from_blocks(wab_ref, wa_ref)
            _dense_from_blocks(wxb_ref, wx_ref)

        row = lax.broadcasted_iota(jnp.int32, (RT, R), 0)
        first = (ti > 0).astype(F32)
        xprev8 = pprev_ref[...] * first
        hprev8 = hprev_ref[...] * first
        xp = p_ref[:, 0:R]
        ga = p_ref[:, R:2 * R]
        f = _rec_gates(xp, xprev8, row, cw_ref, cb_ref, wa_ref, ba_ref, wx_ref, bx_ref, lam_ref)
        xa, r, ig, a, mult = f["xa"], f["r"], f["ig"], f["a"], f["mult"]
        h = h_ref[...]
        sg = _sigmoid(ga)
        gate = ga * sg
        yp = h * gate
        _, ypn, rstd = _rms_fwd(yp, nr_ref[...])
        d_yp, dnr = _rms_bwd(dya_ref[...], ypn, rstd, nr_ref[...])
        d_ga = d_yp * h * (sg * (1.0 + ga * (1.0 - sg)))
        dh = d_yp * gate + jnp.where(row == RT - 1, cg[0:1, :], 0.0)
        al = jnp.where(row < RT - 1, pltpu.roll(a, RT - 1, 0), 0.0)
        g = _scan_bwd(al, dh, row)
        cg[0:1, :] = jnp.sum(jnp.where(row == 0, a * g, 0.0), axis=0, keepdims=True)
        h_m1 = _shift_down(h, hprev8, 1, row)
        da = g * h_m1
        ix = ig * xa
        d_mult = g * ix
        d_ig = g * mult * xa
        d_xa = g * mult * ig
        d_la = da * a - d_mult * (a * a) / mult
        d_r = d_la * ((-LRU_C) * f["sp"])
        dsp = jnp.sum(d_la * ((-LRU_C) * r), axis=0, keepdims=True)
        dlam = dsp * (-_sigmoid(-lam_ref[...]))
        d_za = d_r * r * (1.0 - r)
        d_zx = d_ig * ig * (1.0 - ig)
        dzab = d_za.astype(BF16)
        dzxb = d_zx.astype(BF16)
        dwa_ref[...] += _dot_tn(f["xab"], dzab)
        dwx_ref[...] += _dot_tn(f["xab"], dzxb)
        d_xa = d_xa + _dot_nt(dzab, wa_ref[...]) + _dot_nt(dzxb, wx_ref[...])
        d_xp = sum(cw_ref[3 - j:4 - j, :] * _shift_up(d_xa, nxt8[...], j, row) for j in range(4))
        dcw = [jnp.sum(d_xa * _shift_down(xp, xprev8, 3 - k, row), axis=0, keepdims=True) for k in range(4)]
        dp_ref[:, 0:R] = d_xp.astype(BF16)
        dp_ref[:, R:2 * R] = d_ga.astype(BF16)
        dp8 = d_xa[0:8, :]
        nxt8[...] = dp8
        sm_ref[0:1, :] += jnp.sum(d_za, axis=0, keepdims=True)
        sm_ref[1:2, :] += jnp.sum(d_zx, axis=0, keepdims=True)
        sm_ref[2:3, :] += dlam
        sm_ref[3:4, :] += dnr
        sm_ref[4:5, :] += jnp.sum(d_xa, axis=0, keepdims=True)
        for k in range(4):
            sm_ref[8 + k:9 + k, :] += dcw[k]

        @pl.when(i == nt - 1)
        def _():
            for h in range(R // HEAD):
                dwab_ref[h] = dwa_ref[h * HEAD:(h + 1) * HEAD, h * HEAD:(h + 1) * HEAD].astype(BF16)
                dwxb_ref[h] = dwx_ref[h * HEAD:(h + 1) * HEAD, h * HEAD:(h + 1) * HEAD].astype(BF16)

    c0 = lambda shape: pl.BlockSpec(shape, lambda i: (0, 0))
    blocks = pl.BlockSpec((R // HEAD, HEAD, HEAD), lambda i: (0, 0, 0))
    rev = lambda i: nt - 1 - i
    prev8 = lambda i: (jnp.maximum((nt - 1 - i) * (RT // 8) - 1, 0), 0)
    return pl.pallas_call(
        body, name="rec_bwd", grid=(nt,),
        in_specs=[pl.BlockSpec(memory_space=pl.ANY),
                  pl.BlockSpec((RT, R), lambda i: (rev(i), 0)),
                  pl.BlockSpec((RT, 2 * R), lambda i: (rev(i), 0)), pl.BlockSpec((8, R), prev8),
                  pl.BlockSpec((RT, R), lambda i: (rev(i), 0)), pl.BlockSpec((8, R), prev8),
                  c0((4, R)), c0((1, R)), blocks, c0((1, R)), blocks, c0((1, R)), c0((1, R)), c0((1, R))],
        out_specs=[pl.BlockSpec((RT, 2 * R), lambda i: (rev(i), 0)), blocks, blocks, c0((16, R))],
        out_shape=[jax.ShapeDtypeStruct((S, E), BF16), jax.ShapeDtypeStruct((R // HEAD, HEAD, HEAD), BF16),
                   jax.ShapeDtypeStruct((R // HEAD, HEAD, HEAD), BF16), jax.ShapeDtypeStruct((16, R), F32)],
        scratch_shapes=[pltpu.VMEM((8, R), F32), pltpu.VMEM((8, R), F32), pltpu.VMEM((R, R), BF16),
                        pltpu.VMEM((R, R), BF16), pltpu.VMEM((R, R), F32), pltpu.VMEM((R, R), F32)],
        input_output_aliases={0: 0},
        compiler_params=_cp(("arbitrary",)),
    )(dproj, d_ya, proj, proj, h_all, h_all, conv_w, conv_b, wa_b, ba, wx_b, bx, lam, norm_rec)


NPAIR = R // LANES
QB, KB, VB, GB = 2 * R // LANES, 3 * R // LANES, 4 * R // LANES, 5 * R // LANES


def _rope_freq():
    half = HEAD // 2
    inv = np.float32(ROPE_THETA) ** (-(np.arange(half, dtype=np.float32) / np.float32(half)))
    return jnp.asarray(np.tile(inv.astype(np.float32), LANES // half)[None, :])


def _rot_half(x, first):
    return jnp.where(first, -pltpu.roll(x, LANES - HEAD // 2, 1), pltpu.roll(x, HEAD // 2, 1))


def _cos_sin(pos_ref, freq_ref):
    ang = pos_ref[...].astype(F32) * freq_ref[...]
    return jnp.cos(ang), jnp.sin(ang)


SUB = 4


def _stages(d):
    assert d in (1, SUB, SUB * SUB)
    return d > SUB


def _strided_rows(src_ref, d, tmp):
    n = S // d
    if not _stages(d):
        for r in range(d):
            yield r * n, (src_ref[pl.ds(r, n, stride=d), :] if d > 1 else src_ref[...])
        return
    m = S // SUB
    for r in range(SUB):
        tmp[r * m:(r + 1) * m, :] = src_ref[pl.ds(r, m, stride=SUB), :]
    for r in range(SUB):
        for q in range(SUB):
            yield (r + SUB * q) * n, tmp[pl.ds(r * m + q, n, stride=SUB), :]


def _deint(src_ref, dst_ref, d, tmp):
    n = S // d
    for row0, v in _strided_rows(src_ref, d, tmp):
        dst_ref[row0:row0 + n, :] = v.astype(dst_ref.dtype)


def _reint(src_ref, dst_ref, d, accumulate, tmp):
    if _stages(d):
        n, m = S // d, S // SUB
        for r in range(SUB):
            for q in range(SUB):
                tmp[pl.ds(r * m + q, n, stride=SUB), :] = src_ref[(r + SUB * q) * n:(r + SUB * q + 1) * n, :]
        src_ref, d = tmp, SUB
    n = S // d
    for r in range(d):
        idx = (pl.ds(r, n, stride=d), slice(None)) if d > 1 else (slice(None), slice(None))
        v = src_ref[r * n:(r + 1) * n, :]
        if accumulate:
            dst_ref[idx] = dst_ref[idx] + v
        else:
            dst_ref[idx] = v


def _deint_heads(src_ref, dst0, dst1, d, tmp):
    n = S // d
    hm0 = lax.broadcasted_iota(jnp.int32, (n, LANES), 1) < HEAD
    for row0, v in _strided_rows(src_ref, d, tmp):
        dst0[row0:row0 + n, :] = jnp.where(hm0, v, 0.0).astype(BF16)
        dst1[row0:row0 + n, :] = jnp.where(hm0, 0.0, v).astype(BF16)


def _reint_prev(src_ref, dst_ref, d):
    n = S // d
    if n == BLK:
        return
    for r in range(d):
        idx = (pl.ds(r, n - BLK, stride=d), slice(None)) if d > 1 else (slice(0, n - BLK), slice(None))
        dst_ref[idx] = dst_ref[idx] + src_ref[r * n + BLK:(r + 1) * n, :]


def _pair_masks():
    qi = lax.broadcasted_iota(jnp.int32, (BLK, 2 * BLK), 0)
    ki = lax.broadcasted_iota(jnp.int32, (BLK, 2 * BLK), 1) & (BLK - 1)
    return ki <= qi, ki >= qi


def _two(ref0, ref1, st, axis):
    return jnp.concatenate([ref0[pl.ds(st, BLK), :], ref1[pl.ds(st, BLK), :]], axis=axis)


ATT_UNROLL = 8


def _att_fwd(proj, cos, sin, w_out):
    def body(q_ref, k_ref, v_ref, cos_ref, sin_ref, w_ref, att_ref, qr_ref, kr_ref, lse_ref, wbf_ref,
             qd, kd0, kd1, vd0, vd1, od, ld, tmp, on, ln, wbuf, *wsems):
        wg = _WeightGather(w_ref, wbuf, *wsems)
        pl.when(pl.program_id(0) == 0)(wg.start)
        pl.when(pl.program_id(0) == 1)(wg.forward)
        lane = lax.broadcasted_iota(jnp.int32, (S, LANES), 1)
        first = (lane & (HEAD // 2)) == 0
        cos, sin = cos_ref[...], sin_ref[...]
        q = q_ref[...]
        k = k_ref[...]
        qr_ref[...] = (q * cos + _rot_half(q, first) * sin) * (HEAD ** -0.5)
        kr_ref[...] = k * cos + _rot_half(k, first) * sin
        hm0 = lax.broadcasted_iota(jnp.int32, (BLK, LANES), 1) < HEAD
        top = lax.broadcasted_iota(jnp.int32, (2 * BLK, LANES), 0) < BLK
        ones2 = (top == (lax.broadcasted_iota(jnp.int32, (2 * BLK, LANES), 1) < HEAD)).astype(BF16)
        mc2, mp2 = _pair_masks()

        for pi, d in enumerate(PATTERNS):
            nb = S // d // BLK
            _deint(qr_ref, qd, d, tmp)
            _deint_heads(kr_ref, kd0, kd1, d, tmp)
            _deint_heads(v_ref, vd0, vd1, d, tmp)

            def blk(b, carry):
                st = pl.multiple_of(b * BLK, BLK)
                qb = qd[pl.ds(st, BLK), :]
                sc = jnp.where(mc2, _dot_nt(qb, _two(kd0, kd1, st, 0)), NEG)
                mx = sc
                if nb > 1:
                    stp = pl.multiple_of(jnp.maximum(b - 1, 0) * BLK, BLK)
                    mp = jnp.logical_and(mp2, lax.rem(b, nb) != 0)
                    sp = jnp.where(mp, _dot_nt(qb, _two(kd0, kd1, stp, 0)), NEG)
                    mx = jnp.maximum(sc, sp)
                m0 = jnp.max(mx[:, 0:BLK], axis=1, keepdims=True)
                m1 = jnp.max(mx[:, BLK:2 * BLK], axis=1, keepdims=True)
                mf = jnp.concatenate([jnp.broadcast_to(m0, (BLK, BLK)), jnp.broadcast_to(m1, (BLK, BLK))], axis=1)
                o = _dot(jnp.exp(sc - mf).astype(BF16), jnp.concatenate([_two(vd0, vd1, st, 0), ones2], axis=1))
                if nb > 1:
                    o = o + _dot(jnp.exp(sp - mf).astype(BF16), jnp.concatenate([_two(vd0, vd1, stp, 0), ones2], axis=1))
                l = o[:, LANES:2 * LANES]
                od[pl.ds(st, BLK), :] = o[:, 0:LANES] / l
                ld[pl.ds(st, BLK), :] = jnp.where(hm0, m0, m1) + jnp.log(l)
                return carry

            lax.fori_loop(0, S // BLK, blk, 0, unroll=ATT_UNROLL)
            _reint(od, on.at[pi], d, False, tmp)
            _reint(ld, ln.at[pi], d, False, tmp)

        l0, l1, l2 = ln[0], ln[1], ln[2]
        m = jnp.maximum(jnp.maximum(l0, l1), l2)
        e0, e1, e2 = jnp.exp(l0 - m), jnp.exp(l1 - m), jnp.exp(l2 - m)
        den = e0 + e1 + e2
        att_ref[...] = (e0 * on[0] + e1 * on[1] + e2 * on[2]) / den
        lse_ref[...] = m + jnp.log(den)

        @pl.when(pl.program_id(0) == NPAIR - 1)
        def _():
            wg.finish()
            wbf_ref[...] = wbuf[...]

    col = lambda c0: pl.BlockSpec((S, LANES), lambda p: (0, c0 + p))
    out = pl.BlockSpec((S, LANES), lambda p: (0, p))
    tab = pl.BlockSpec((S, LANES), lambda p: (0, 0))
    vm = pl.BlockSpec(memory_space=pltpu.VMEM)
    return pl.pallas_call(
        body, name="att_fwd", grid=(NPAIR,),
        in_specs=[col(QB), col(KB), col(VB), tab, tab, vm],
        out_specs=[out, out, out, out, vm],
        out_shape=[jax.ShapeDtypeStruct((S, R), F32)] * 4 + [jax.ShapeDtypeStruct((NCHIP,) + w_out.shape, BF16)],
        scratch_shapes=[pltpu.VMEM((S, LANES), BF16)] * 5 + [pltpu.VMEM((S, LANES), F32)] * 3
        + [pltpu.VMEM((3, S, LANES), F32)] * 2 + [pltpu.VMEM((NCHIP,) + w_out.shape, BF16)] + _WeightGather.SEMS,
        compiler_params=_cp(("arbitrary",)),
    )(proj, proj, proj, cos, sin, w_out)


def _att_bwd(dproj, d_att, att, lse, qr, kr, proj, cos, sin, gw_out4):
    out_units = [(j, j, 0) for j in range(NCHIP)]

    nblk = S // BLK

    def body(dp_in, do_ref, o_ref, lse_ref, qr_ref, kr_ref, v_ref, cos_ref, sin_ref, gw_ref, dp_ref, gout_ref,
             qd, kd0, kd1, vd0, vd1, dod, kt, packn, packd, dqd, dkcd, dkpd, dvcd, dvpd,
             dqn, dkn, dvn, tmp, rows, trs, pts, dss, stage, sems, gred, *rs_scratch):
        p = pl.program_id(0)
        rs = _ReduceScatter(gw_ref, gred, out_units, *rs_scratch)
        for step, piece in enumerate((rs.start_halves, rs.send_partials, rs.reduce_owned)):
            pl.when(p == step)(piece)

        @pl.when(p == NPAIR - 1)
        def _():
            rs.finish()
            gout_ref[...] = gred[...]

        lane = lax.broadcasted_iota(jnp.int32, (S, LANES), 1)
        hms = lane < HEAD
        prod = do_ref[...] * o_ref[...]
        d0 = jnp.sum(jnp.where(hms, prod, 0.0), axis=1, keepdims=True)
        d1 = jnp.sum(jnp.where(hms, 0.0, prod), axis=1, keepdims=True)
        lse = lse_ref[...]
        quarter = HEAD // 2
        packn[...] = jnp.where(lane < quarter, lse,
                               jnp.where(hms, pltpu.roll(lse, LANES - quarter, 1), jnp.where(lane < 3 * quarter, d0, d1)))
        dqn[...] = jnp.zeros_like(dqn)
        dkn[...] = jnp.zeros_like(dkn)
        dvn[...] = jnp.zeros_like(dvn)
        hm0 = lax.broadcasted_iota(jnp.int32, (BLK, LANES), 1) < HEAD
        key = lax.broadcasted_iota(jnp.int32, (2 * BLK, BLK), 0) & (BLK - 1)
        qry = lax.broadcasted_iota(jnp.int32, (2 * BLK, BLK), 1)
        mct, mpt = key <= qry, key >= qry

        for d in PATTERNS:
            nb = S // d // BLK
            _deint(qr_ref, qd, d, tmp)
            _deint_heads(kr_ref, kd0, kd1, d, tmp)
            _deint_heads(v_ref, vd0, vd1, d, tmp)
            _deint(do_ref, dod, d, tmp)
            _deint(packn, packd, d, tmp)

            sides = (0, 1) if nb > 1 else (0,)

            def probs(b, carry):
                st = pl.multiple_of(b * BLK, BLK)
                kt[b] = _two(kd0, kd1, st, 0).astype(F32).T.astype(BF16)
                trs[b] = packd[pl.ds(st, BLK), :].T
                for j in range(4):
                    rows[b, j:j + 1, :] = trs[b, j * quarter:j * quarter + 1, :]
                qb, dob = qd[pl.ds(st, BLK), :], dod[pl.ds(st, BLK), :]
                both = lambda j: jnp.concatenate([jnp.broadcast_to(rows[b, j:j + 1, :], (BLK, BLK)),
                                                  jnp.broadcast_to(rows[b, j + 1:j + 2, :], (BLK, BLK))], axis=0)
                lbt, dlt = both(0), both(2)
                for sd in sides:
                    stk = pl.multiple_of(jnp.maximum(b - sd, 0) * BLK, BLK)
                    mask = mct if sd == 0 else jnp.logical_and(mpt, lax.rem(b, nb) != 0)
                    k2, v2 = _two(kd0, kd1, stk, 0), _two(vd0, vd1, stk, 0)
                    pt = jnp.where(mask, jnp.exp(_dot_nt(k2, qb) - lbt), 0.0)
                    pts[b, sd] = pt.astype(BF16)
                    dss[b, sd] = (pt * (_dot_nt(v2, dob) - dlt)).astype(BF16)
                return carry

            lax.fori_loop(0, nblk, probs, 0, unroll=ATT_UNROLL)

            def prods(b, carry):
                st = pl.multiple_of(b * BLK, BLK)
                qb, dob = qd[pl.ds(st, BLK), :], dod[pl.ds(st, BLK), :]
                dq_t = None
                for sd in sides:
                    dst, ptb = dss[b, sd], pts[b, sd]
                    rk, rv = _dot(dst, qb), _dot(ptb, dob)
                    dqs = _dot(kt[jnp.maximum(b - sd, 0)], dst)
                    dq_t = dqs if dq_t is None else dq_t + dqs
                    dk, dv = (dkcd, dvcd) if sd == 0 else (dkpd, dvpd)
                    dk[pl.ds(st, BLK), :] = jnp.where(hm0, rk[0:BLK], rk[BLK:2 * BLK])
                    dv[pl.ds(st, BLK), :] = jnp.where(hm0, rv[0:BLK], rv[BLK:2 * BLK])
                dqd[pl.ds(st, BLK), :] = dq_t.T
                return carry

            lax.fori_loop(0, nblk, prods, 0, unroll=ATT_UNROLL)
            _reint(dqd, dqn, d, True, tmp)
            _reint(dkcd, dkn, d, True, tmp)
            _reint(dvcd, dvn, d, True, tmp)
            _reint_prev(dkpd, dkn, d)
            _reint_prev(dvpd, dvn, d)

        lane = lax.broadcasted_iota(jnp.int32, (S, LANES), 1)
        first = (lane & (HEAD // 2)) == 0
        cos, sin = cos_ref[...], sin_ref[...]
        dq = dqn[...] * (HEAD ** -0.5)
        dk = dkn[...]
        stage[0] = (dq * cos - _rot_half(dq, first) * sin).astype(BF16)
        stage[1] = (dk * cos - _rot_half(dk, first) * sin).astype(BF16)
        stage[2] = dvn[...].astype(BF16)
        copies = [pltpu.make_async_copy(stage.at[j], dp_ref.at[:, pl.ds((2 + j) * R + p * LANES, LANES)], sems.at[j])
                  for j in range(3)]
        for cp in copies:
            cp.start()
        for cp in copies:
            cp.wait()

    blk = pl.BlockSpec((S, LANES), lambda p: (0, p))
    tab = pl.BlockSpec((S, LANES), lambda p: (0, 0))
    vm = pl.BlockSpec(memory_space=pltpu.VMEM)
    _, orows, ocols = gw_out4.shape
    return pl.pallas_call(
        body, name="att_bwd", grid=(NPAIR,),
        in_specs=[pl.BlockSpec(memory_space=pl.ANY), blk, blk, blk, blk, blk,
                  pl.BlockSpec((S, LANES), lambda p: (0, VB + p)), tab, tab, vm],
        out_specs=[pl.BlockSpec(memory_space=pl.ANY), vm],
        out_shape=[jax.ShapeDtypeStruct((S, E), BF16), jax.ShapeDtypeStruct((orows, ocols), F32)],
        scratch_shapes=[pltpu.VMEM((S, LANES), BF16)] * 6 + [pltpu.VMEM((nblk, LANES, 2 * BLK), BF16)]
        + [pltpu.VMEM((S, LANES), F32)] * 11
        + [pltpu.VMEM((nblk, 8, BLK), F32), pltpu.VMEM((nblk, LANES, BLK), F32)]
        + [pltpu.VMEM((nblk, 2, 2 * BLK, BLK), BF16)] * 2
        + [pltpu.VMEM((3, S, LANES), BF16), pltpu.SemaphoreType.DMA((3,)), pltpu.VMEM((orows, ocols), F32)]
        + _ReduceScatter.scratch(NCHIP, orows, ocols, 1),
        input_output_aliases={0: 0},
        compiler_params=_cp(("arbitrary",)),
    )(dproj, d_att, att, lse, qr, kr, proj, cos, sin, gw_out4)


def _out_fwd_bwd(ya, att, proj, w_out_bf, x, target, mod, norm_post, norm_att):
    ts = 512

    def body(ya_ref, att_ref, gb_ref, w_ref, x_ref, t_ref, mod_ref, npost_ref, natt_ref,
             gx_ref, dya_ref, datt_ref, dgb_ref, gw_ref, acc_ref):
        i = pl.program_id(0)

        @pl.when(i == 0)
        def _():
            gw_ref[...] = jnp.zeros_like(gw_ref)
            acc_ref[...] = jnp.zeros_like(acc_ref)

        gate = mod_ref[:, 2 * D:3 * D]
        att = att_ref[...]
        gb = gb_ref[...]
        sg = _sigmoid(gb)
        silu = gb * sg
        ybp = att * silu
        yb, ybn, rstd_b = _rms_fwd(ybp, natt_ref[...])
        cat = jnp.concatenate([ya_ref[...], yb.astype(BF16)], axis=1)
        mix = _dot(cat, w_ref[...])
        rn, mn, rstd_m = _rms_fwd(mix, npost_ref[...])
        err = x_ref[...] + gate * rn - t_ref[...]
        dy = err * (1.0 / D)
        gx_ref[...] = dy
        dmix, dnpost = _rms_bwd(dy * gate, mn, rstd_m, npost_ref[...])
        dmb = dmix.astype(BF16)
        gw_ref[...] += _dot_tn(cat, dmb)
        dcat = _dot_nt(dmb, w_ref[...])
        dya_ref[...] = dcat[:, 0:R]
        dybp, dnatt = _rms_bwd(dcat[:, R:2 * R], ybn, rstd_b, natt_ref[...])
        datt_ref[...] = dybp * silu
        dgb_ref[...] = (dybp * att * (sg * (1.0 + gb * (1.0 - sg)))).astype(BF16)
        acc_ref[0:1, :] += jnp.sum(dy * rn, axis=0, keepdims=True)
        acc_ref[1:2, :] += dnpost
        acc_ref[2:3, 0:R] += dnatt
        acc_ref[3:4, :] += jnp.sum(jnp.sum(err * err, axis=1, keepdims=True), axis=0, keepdims=True)

    tile = lambda w: pl.BlockSpec((ts, w), lambda i: (i, 0))
    c0 = lambda shape: pl.BlockSpec(shape, lambda i: (0, 0))
    return pl.pallas_call(
        body, name="out_fwd_bwd", grid=(S // ts,),
        in_specs=[tile(R), tile(R), pl.BlockSpec((ts, R), lambda i: (i, 5)), c0((D, D)), tile(D), tile(D),
                  c0((1, 3 * D)), c0((1, D)), c0((1, R))],
        out_specs=[tile(D), tile(R), tile(R), pl.BlockSpec((ts, R), lambda i: (i, 5)), c0((D, D)), c0((8, D))],
        out_shape=[jax.ShapeDtypeStruct((S, D), F32), jax.ShapeDtypeStruct((S, R), F32),
                   jax.ShapeDtypeStruct((S, R), F32), jax.ShapeDtypeStruct((S, E), BF16),
                   jax.ShapeDtypeStruct((D, D), F32), jax.ShapeDtypeStruct((8, D), F32)],
        compiler_params=_cp(("arbitrary",)),
    )(ya, att, proj, w_out_bf, x, target, mod, norm_post, norm_att)


UC = 256
UPC = EC // UC


NU = E // UC


def _unit_of_step(i):
    return (i % NCHIP) * UPC + i // NCHIP


def _in_proj_bwd(ht, dproj, w_in_bf, x, gx1, mod, norm_pre, smalls):
    ts = 256
    nt = S // ts
    half = D // 2
    units = [_unit_of_step(k) for k in range(NU)]
    owners = [u // UPC for u in units]
    ns = len(smalls)

    def body(*refs):
        (ht_ref, dpu_ref, dp_ref, w_hbm, x_ref, gx1_ref, mod_ref, np_ref), refs = refs[:8], refs[8:]
        small_in, refs = refs[:ns], refs[ns:]
        (gx_ref, gin_ref), refs = refs[:2], refs[2:]
        small_out, (acc_out,), refs = refs[:ns], refs[ns:ns + 1], refs[ns + 1:]
        mine, sib, tmp, stage, got, red, acc_ref, hs, hr, ps, pr, bs, br = refs[:13]
        early = _SmallGather(small_in, small_out, *refs[13:16])
        late = _SmallGather([acc_ref], [acc_out], *refs[16:19])
        w_ref, w_sem = refs[19:21]
        i = pl.program_id(0)
        w_copy = pltpu.make_async_copy(w_hbm, w_ref, w_sem)
        pl.when(i == 0)(w_copy.start)
        pl.when(i == NU)(w_copy.wait)
        xx, yy, c = _me()
        ci = 2 * xx + yy
        r0 = pl.multiple_of(c * half, half)
        r1 = pl.multiple_of((1 - c) * half, half)
        pl.when(i == 0)(early.start)
        pl.when(i == NU)(early.forward)

        def exch(k):
            return _remote(tmp.at[k % 2], sib.at[k], hs.at[k], hr.at[k], 1)

        def partial(k, sender):
            return pltpu.make_async_remote_copy(
                src_ref=stage.at[k], dst_ref=got.at[units[k] % UPC, sender], send_sem=ps.at[k],
                recv_sem=pr.at[k, sender], device_id=(owners[k] // 2, owners[k] % 2, c), device_id_type=MESH)

        def back(k, start):
            off = (units[k] % UPC) * UC
            blk = red.at[pl.ds(start, half), off:off + UC]
            return _remote(blk, blk, bs.at[k], br.at[k], 1)

        for k in range(NU + 1):
            @pl.when(i == k)
            def _():
                if k < NU:
                    if k >= 2:
                        exch(k - 2).wait_send()
                    dpu = dpu_ref[...]
                    tmp[k % 2] = _dot(ht_ref[pl.ds(r1, half), :], dpu)
                    exch(k).start()
                    mine[k] = _dot(ht_ref[pl.ds(r0, half), :], dpu)
                if k >= 1:
                    exch(k - 1).wait_recv()
                    mine[k - 1] += sib[k - 1]

                    @pl.when(ci != owners[k - 1])
                    def _():
                        stage[k - 1] = mine[k - 1].astype(BF16)
                        partial(k - 1, ci).start()

        @pl.when(i == NU)
        def _():
            acc_ref[...] = jnp.zeros_like(acc_ref)

        @pl.when(i >= NU)
        def _():
            dh = sum(_dot_nt(dp_ref[:, j * EC:(j + 1) * EC], w_ref[j]) for j in range(NCHIP))
            hp, xn, rstd = _rms_fwd(x_ref[...], np_ref[...])
            dx, dnp = _rms_bwd(dh * (1.0 + mod_ref[:, D:2 * D]), xn, rstd, np_ref[...])
            gx_ref[...] = gx1_ref[...] + dx
            acc_ref[0:1, :] += jnp.sum(dh, axis=0, keepdims=True)
            acc_ref[1:2, :] += jnp.sum(dh * hp, axis=0, keepdims=True)
            acc_ref[2:3, :] += dnp

        for t in range(UPC):
            @pl.when(i == NU + 1 + 2 * t)
            def _():
                for k in range(NCHIP * t, NCHIP * (t + 1)):
                    @pl.when(ci == owners[k])
                    def _():
                        off = (units[k] % UPC) * UC
                        red[pl.ds(r0, half), off:off + UC] = mine[k]
                        for s in range(NCHIP):
                            if s != owners[k]:
                                partial(k, s).wait_recv()
                                red[pl.ds(r0, half), off:off + UC] += got[units[k] % UPC, s].astype(F32)
                        back(k, r0).start()

        @pl.when(i == NU + nt - 1)
        def _():
            late.start()
            exch(NU - 2).wait_send()
            exch(NU - 1).wait_send()
            for k in range(NU):
                @pl.when(ci == owners[k])
                def _():
                    back(k, r1).wait_recv()
                    back(k, r0).wait_send()

                @pl.when(ci != owners[k])
                def _():
                    partial(k, ci).wait_send()
            gin_ref[...] = red[...]
            early.finish()
            late.forward()
            late.finish()

    tile = lambda w: pl.BlockSpec((ts, w), lambda i: (jnp.maximum(i - NU, 0), 0))
    c0 = lambda shape: pl.BlockSpec(shape, lambda i: (0, 0))
    vm = pl.BlockSpec(memory_space=pltpu.VMEM)
    hbm = pl.BlockSpec(memory_space=pl.ANY)
    gathered = [jax.ShapeDtypeStruct((NDEV,) + a.shape, a.dtype) for a in smalls] + [jax.ShapeDtypeStruct((NDEV, 8, D), F32)]
    return pl.pallas_call(
        body, name="in_proj_bwd", grid=(NU + nt,),
        in_specs=[vm, pl.BlockSpec((S, UC), lambda i: (0, _unit_of_step(jnp.minimum(i, NU - 1)))), tile(E),
                  hbm, tile(D), tile(D), c0((1, 3 * D)), c0((1, D))] + [vm] * ns,
        out_specs=[tile(D), vm] + [hbm] * (ns + 1),
        out_shape=[jax.ShapeDtypeStruct((S, D), F32), jax.ShapeDtypeStruct((D, EC), F32)] + gathered,
        scratch_shapes=[pltpu.VMEM((NU, half, UC), F32), pltpu.VMEM((NU, half, UC), F32),
                        pltpu.VMEM((2, half, UC), F32), pltpu.VMEM((NU, half, UC), BF16),
                        pltpu.VMEM((UPC, NCHIP, half, UC), BF16), pltpu.VMEM((D, EC), F32), pltpu.VMEM((8, D), F32),
                        pltpu.SemaphoreType.DMA((NU,)), pltpu.SemaphoreType.DMA((NU,)),
                        pltpu.SemaphoreType.DMA((NU,)), pltpu.SemaphoreType.DMA((NU, NCHIP)),
                        pltpu.SemaphoreType.DMA((NU,)), pltpu.SemaphoreType.DMA((NU,))]
        + _SmallGather.sems(ns) + _SmallGather.sems(1)
        + [pltpu.VMEM((NCHIP, D, EC), BF16), pltpu.SemaphoreType.DMA],
        compiler_params=_cp(("arbitrary",)),
    )(ht, dproj, dproj, w_in_bf, x, gx1, mod, norm_pre, *smalls)


def _local_step(x, cos, sin, target, mod, w_in_bf, proj, ht, w_out, conv_w, p):
    rec_p = (conv_w, p["conv_b"], p["w_rg_a"], p["b_rg_a"], p["w_rg_x"], p["b_rg_x"], p["lru_lambda"], p["norm_rec"])
    h_all, ya = _rec_fwd(proj, *rec_p)
    att, qr, kr, lse, w_out_bf = _att_fwd(proj, cos, sin, w_out)
    gx1, d_ya, d_att, dproj, gw_out, acc_o = _out_fwd_bwd(ya, att, proj, w_out_bf.reshape(D, D), x, target, mod,
                                                           p["norm_post"], p["norm_att"])
    dproj, g_out = _att_bwd(dproj, d_att, att, lse, qr, kr, proj, cos, sin, gw_out.reshape(NCHIP, D // NCHIP, D))
    dproj, dwa, dwx, sm = _rec_bwd(dproj, d_ya, proj, h_all, *rec_p)
    grad_x, g_in, *gathered = _in_proj_bwd(ht, dproj, w_in_bf, x, gx1, mod, p["norm_pre"], [acc_o, sm, dwa, dwx])
    return grad_x, g_in, g_out, gathered


def _me():
    return lax.axis_index("x"), lax.axis_index("y"), lax.axis_index("c")


def _flip(v, bit):
    return 1 - v if bit else v


def _peer(rel):
    x, y, c = _me()
    return (_flip(x, rel & 4), _flip(y, rel & 2), _flip(c, rel & 1))


def _remote(src, dst, send_sem, recv_sem, rel):
    return pltpu.make_async_remote_copy(src_ref=src, dst_ref=dst, send_sem=send_sem, recv_sem=recv_sem,
                                        device_id=_peer(rel), device_id_type=MESH)


class _WeightGather:
    SEMS = [pltpu.SemaphoreType.DMA((NCHIP - 1,))] * 4

    def __init__(self, w_ref, out_ref, send_sems, recv_sems, fsend_sems, frecv_sems):
        x, y, c = _me()
        self.w, self.out, self.ci = w_ref, out_ref, 2 * x + y
        self.half = w_ref.shape[0] // 2
        self.r0 = pl.multiple_of(c * self.half, self.half)
        self.r1 = pl.multiple_of((1 - c) * self.half, self.half)
        self.sems = (send_sems, recv_sems, fsend_sems, frecv_sems)

    def _ici(self, chip, k):
        blk = self.out.at[chip, pl.ds(self.r0, self.half), :]
        return _remote(blk, blk, self.sems[0].at[k - 1], self.sems[1].at[k - 1], 2 * k)

    def _d2d(self, chip, start, k):
        blk = self.out.at[chip, pl.ds(start, self.half), :]
        return _remote(blk, blk, self.sems[2].at[k - 1], self.sems[3].at[k - 1], 1)

    def start(self, diagonal=True):
        self.out[self.ci] = self.w[...].astype(BF16)
        for k in range(1, NCHIP if diagonal else NCHIP - 1):
            self._ici(self.ci, k).start()

    def _relay(self, chip, piece, k):
        q = self.half // 2
        blk = self.out.at[chip, pl.ds(self.r0 + piece * q, q), :]
        return _remote(blk, blk, self.relay_sems[0].at[piece], self.relay_sems[1].at[piece], 2 * k)

    def neighbours_landed(self, relay_send_sems, relay_recv_sems):
        self.relay_sems = (relay_send_sems, relay_recv_sems)
        for k in (1, 2):
            self._ici(self.ci ^ k, k).wait_recv()
        self._relay(self.ci ^ 2, 0, 1).start()
        self._relay(self.ci ^ 1, 1, 2).start()
        for k in (1, 2):
            self._d2d(self.ci ^ k, self.r0, k).start()

    def sibling_landed(self, k):
        self._d2d(self.ci ^ k, self.r1, k).wait_recv()

    def diagonal_landed(self):
        for piece, k in ((0, 1), (1, 2)):
            self._relay(self.ci ^ 3, piece, k).wait_recv()
        self._d2d(self.ci ^ 3, self.r0, 3).start()
        self._d2d(self.ci ^ 3, self.r1, 3).wait_recv()

    def finish_relayed(self):
        for k in (1, 2):
            self._ici(self.ci, k).wait_send()
        self._relay(self.ci ^ 2, 0, 1).wait_send()
        self._relay(self.ci ^ 1, 1, 2).wait_send()
        for k in range(1, NCHIP):
            self._d2d(self.ci ^ k, self.r0, k).wait_send()

    def forward(self):
        for k in range(1, NCHIP):
            self._ici(self.ci ^ k, k).wait_recv()
            self._d2d(self.ci ^ k, self.r0, k).start()

    def finish(self):
        for k in range(1, NCHIP):
            self._d2d(self.ci ^ k, self.r1, k).wait_recv()
        self.finish_sends()

    def finish_sends(self):
        for k in range(1, NCHIP):
            self._ici(self.ci, k).wait_send()
            self._d2d(self.ci ^ k, self.r0, k).wait_send()


class _SmallGather:
    @staticmethod
    def sems(n):
        return [pltpu.SemaphoreType.DMA((n, 7)), pltpu.SemaphoreType.DMA((n, 7)), pltpu.SemaphoreType.DMA((n,))]

    def __init__(self, srcs, outs, send_sems, recv_sems, local_sems):
        x, y, c = _me()
        self.srcs, self.outs = list(srcs), list(outs)
        self.ss, self.rs, self.ls = send_sems, recv_sems, local_sems
        self.ci, self.c = 2 * x + y, c
        self.me = 2 * self.ci + c

    def _own(self, a, slot, rel):
        return _remote(self.srcs[a], self.outs[a].at[self.me], self.ss.at[a, slot], self.rs.at[a, slot], rel)

    def _block(self, a, idx, slot, rel):
        blk = self.outs[a].at[idx]
        return _remote(blk, blk, self.ss.at[a, slot], self.rs.at[a, slot], rel)

    def _local(self, a):
        return pltpu.make_async_copy(self.srcs[a], self.outs[a].at[self.me], self.ls.at[a])

    def start(self):
        for a in range(len(self.srcs)):
            self._local(a).start()
            self._own(a, 0, 1).start()
            for k in range(1, NCHIP):
                self._own(a, k, 2 * k).start()

    def forward(self):
        for a in range(len(self.srcs)):
            for k in range(1, NCHIP):
                idx = 2 * (self.ci ^ k) + self.c
                self._block(a, idx, k, 2 * k).wait_recv()
                self._block(a, idx, 3 + k, 1).start()

    def finish(self):
        for a in range(len(self.srcs)):
            self._block(a, 2 * self.ci + 1 - self.c, 0, 1).wait_recv()
            for k in range(1, NCHIP):
                self._block(a, 2 * (self.ci ^ k) + 1 - self.c, 3 + k, 1).wait_recv()
            self._own(a, 0, 1).wait_send()
            for k in range(1, NCHIP):
                self._own(a, k, 2 * k).wait_send()
                self._block(a, 2 * (self.ci ^ k) + self.c, 3 + k, 1).wait_send()
            self._local(a).wait()


def _start_in_proj(crow, w_ada, b_cols, w_in, pos, x, norm_pre, order):
    ts = 512
    nt = S // ts
    wc = crow.shape[1]

    def body(order_ref, crow_ref, wada_ref, b_ref, win_ref, pos_ref, freq_ref, x_ref, np_ref,
             g0_ref, mod_ref, wbf_ref, cos_ref, sin_ref, proj_ref, ht_ref,
             g0s, modp, modb, wbuf, hb_all, cs, cr, ms, mr, ws, wr, fs, fr, local_sems, ys, yr, osem):
        s, t = pl.program_id(0), pl.program_id(1)
        x, y, c = _me()
        ci = 2 * x + y
        me = 2 * ci + c
        wg = _WeightGather(win_ref, wbuf, ws, wr, fs, fr)

        @pl.when(jnp.logical_and(s == 0, t == 0))
        def _():
            wg.start(diagonal=False)
            mine = pltpu.make_async_copy(crow_ref, g0s.at[pl.ds(me, 1), :], local_sems.at[0])
            mine.start()
            csend = [_remote(crow_ref, g0s.at[pl.ds(me, 1), :], cs.at[r - 1], cr.at[r - 1], r) for r in range(1, NDEV)]
            for cp in csend:
                cp.start()
            cos_ref[...], sin_ref[...] = _cos_sin(pos_ref, freq_ref)
            for r in range(1, NDEV):
                px, py, pc = _peer(r)
                _remote(crow_ref, g0s.at[pl.ds(4 * px + 2 * py + pc, 1), :], cs.at[r - 1], cr.at[r - 1], r).wait_recv()
            mine.wait()
            cv = g0s[:, 0:D]
            sc = cv * _sigmoid(cv)
            scb = jnp.concatenate([sc, jnp.zeros_like(sc)], axis=0).astype(BF16)
            modp[...] = _dot(scb, wada_ref[...].astype(BF16))[0:NDEV, :] + b_ref[...]
            own = pltpu.make_async_copy(modp.at[pl.ds(me, 1), :], modb.at[ci], local_sems.at[1])
            own.start()
            msend = []
            for k in range(1, NCHIP):
                cp = _remote(modp.at[pl.ds(2 * (ci ^ k) + c, 1), :], modb.at[ci], ms.at[k - 1], mr.at[k - 1], 2 * k)
                cp.start()
                msend.append(cp)
            for k in range(1, NCHIP):
                _remote(modp.at[pl.ds(me, 1), :], modb.at[ci ^ k], ms.at[k - 1], mr.at[k - 1], 2 * k).wait_recv()
            own.wait()
            for j in range(NCHIP):
                mod_ref[:, j * EC:(j + 1) * EC] = modb[j]
            for cp in csend + msend:
                cp.wait_send()
            g0_ref[...] = g0s[...]

        def keep(k):
            return pltpu.make_async_copy(wbuf.at[ci ^ k], wbf_ref.at[ci ^ k], osem.at[k])

        @pl.when(jnp.logical_and(s == 1, t == 0))
        def _():
            keep(0).start()
            wg.neighbours_landed(ys, yr)
            wg.sibling_landed(1)
            keep(1).start()

        @pl.when(jnp.logical_and(s == 2, t == 0))
        def _():
            wg.sibling_landed(2)
            keep(2).start()

        @pl.when(jnp.logical_and(s == 3, t == 0))
        def _():
            wg.relay_sems = (ys, yr)
            wg.diagonal_landed()
            keep(3).start()

        rows = pl.ds(pl.multiple_of(t * ts, ts), ts)

        @pl.when(s == 0)
        def _():
            hp, _, _ = _rms_fwd(x_ref[...], np_ref[...])
            h = hp * (1.0 + mod_ref[:, D:2 * D]) + mod_ref[:, 0:D]
            hb_all[rows, :] = h.astype(BF16)
            ht_ref[...] = h.T.astype(BF16)

        proj_ref[...] = _dot(hb_all[rows, :], wbuf[ci ^ s])

        @pl.when(jnp.logical_and(s == NCHIP - 1, t == nt - 1))
        def _():
            wg.relay_sems = (ys, yr)
            wg.finish_relayed()
            for k in range(NCHIP):
                keep(k).wait()

    vm = pl.BlockSpec(memory_space=pltpu.VMEM)
    first_pass = lambda s, t: jnp.where(s == 0, t, nt - 1)
    grid_spec = pltpu.PrefetchScalarGridSpec(
        num_scalar_prefetch=1, grid=(NCHIP, nt),
        in_specs=[vm, vm, vm, vm, vm, vm, pl.BlockSpec((ts, D), lambda s, t, o: (first_pass(s, t), 0)),
                  pl.BlockSpec((1, D), lambda s, t, o: (0, 0))],
        out_specs=[vm, vm, pl.BlockSpec(memory_space=pl.ANY), vm, vm, pl.BlockSpec((ts, EC), lambda s, t, o: (t, o[s])),
                   pl.BlockSpec((D, ts), lambda s, t, o: (0, first_pass(s, t)))],
        scratch_shapes=[pltpu.VMEM((NDEV, wc), F32), pltpu.VMEM((NDEV, EC), F32), pltpu.VMEM((NCHIP, 1, EC), F32),
                        pltpu.VMEM((NCHIP, D, EC), BF16), pltpu.VMEM((S, D), BF16),
                        pltpu.SemaphoreType.DMA((NDEV - 1,)), pltpu.SemaphoreType.DMA((NDEV - 1,)),
                        pltpu.SemaphoreType.DMA((NCHIP - 1,)), pltpu.SemaphoreType.DMA((NCHIP - 1,))]
        + _WeightGather.SEMS + [pltpu.SemaphoreType.DMA((2,))] * 3 + [pltpu.SemaphoreType.DMA((NCHIP,))])
    return pl.pallas_call(
        body, name="start_in_proj", grid_spec=grid_spec,
        out_shape=[jax.ShapeDtypeStruct((NDEV, wc), F32), jax.ShapeDtypeStruct((1, 3 * D), F32),
                   jax.ShapeDtypeStruct((NCHIP, D, EC), BF16), jax.ShapeDtypeStruct((S, LANES), F32),
                   jax.ShapeDtypeStruct((S, LANES), F32), jax.ShapeDtypeStruct((S, E), F32),
                   jax.ShapeDtypeStruct((D, S), BF16)],
        compiler_params=_cp(("arbitrary", "arbitrary")),
    )(order, crow, w_ada, b_cols, w_in, pos, _rope_freq(), x, norm_pre)


class _ReduceScatter:
    @staticmethod
    def scratch(n_units, rows, ucols, max_owned):
        half = rows // 2
        return [pltpu.VMEM((n_units, half, ucols), F32), pltpu.VMEM((n_units, half, ucols), BF16),
                pltpu.VMEM((max_owned, NCHIP, half, ucols), BF16),
                pltpu.SemaphoreType.DMA((2,)), pltpu.SemaphoreType.DMA((n_units,)),
                pltpu.SemaphoreType.DMA((n_units, NCHIP)), pltpu.SemaphoreType.DMA((n_units,)),
                pltpu.SemaphoreType.DMA((n_units,))]

    def __init__(self, g_ref, out_ref, units, sib, stage, got, sem1, send2, recv2, send3, recv3):
        x, y, c = _me()
        self.c, self.ci = c, 2 * x + y
        self.g, self.out, self.units = g_ref, out_ref, units
        self.sib, self.stage, self.got = sib, stage, got
        self.sem1, self.send2, self.recv2, self.send3, self.recv3 = sem1, send2, recv2, send3, recv3
        self.half = g_ref.shape[1] // 2
        self.ucols = g_ref.shape[2]
        self.r0 = pl.multiple_of(c * self.half, self.half)
        self.r1 = pl.multiple_of((1 - c) * self.half, self.half)
        self.slot0 = units[0][0]
        assert [u[0] for u in units] == list(range(self.slot0, self.slot0 + len(units)))
        seen = {}
        self.local = []
        for _, owner, _ in units:
            self.local.append(seen.get(owner, 0))
            seen[owner] = seen.get(owner, 0) + 1

    def _halves(self):
        n = len(self.units)
        return _remote(self.g.at[pl.ds(self.slot0, n), pl.ds(self.r1, self.half), :], self.sib,
                       self.sem1.at[0], self.sem1.at[1], 1)

    def _partial(self, i, sender):
        _, owner, _ = self.units[i]
        return pltpu.make_async_remote_copy(
            src_ref=self.stage.at[i], dst_ref=self.got.at[self.local[i], sender],
            send_sem=self.send2.at[i], recv_sem=self.recv2.at[i, sender],
            device_id=(owner // 2, owner % 2, self.c), device_id_type=MESH)

    def _back(self, i, start):
        off = self.units[i][2]
        blk = self.out.at[pl.ds(start, self.half), off:off + self.ucols]
        return _remote(blk, blk, self.send3.at[i], self.recv3.at[i], 1)

    def start_halves(self):
        self._halves().start()

    def send_partials(self):
        self._halves().wait_recv()
        for i, (slot, owner, _) in enumerate(self.units):
            @pl.when(self.ci != owner)
            def _():
                self.stage[i] = (self.g[slot, pl.ds(self.r0, self.half), :] + self.sib[i]).astype(BF16)
                self._partial(i, self.ci).start()

    def reduce_owned(self):
        for i, (slot, owner, off) in enumerate(self.units):
            @pl.when(self.ci == owner)
            def _():
                rows, cols = pl.ds(self.r0, self.half), slice(off, off + self.ucols)
                self.out[rows, cols] = self.g[slot, pl.ds(self.r0, self.half), :] + self.sib[i]
                for s in range(NCHIP):
                    if s != owner:
                        self._partial(i, s).wait_recv()
                        self.out[rows, cols] += self.got[self.local[i], s].astype(F32)
                self._back(i, self.r0).start()

    def finish(self):
        self._halves().wait_send()
        for i, (_, owner, _) in enumerate(self.units):
            @pl.when(self.ci == owner)
            def _():
                self._back(i, self.r1).wait_recv()
                self._back(i, self.r0).wait_send()

            @pl.when(self.ci != owner)
            def _():
                self._partial(i, self.ci).wait_send()


def _silu_rows(c_ref):
    cv = c_ref[...]
    sc = cv * _sigmoid(cv)
    return jnp.concatenate([sc, jnp.zeros_like(sc)], axis=0).astype(BF16)


def _adamw(w, g, m, v, name):
    rows, cols = w.shape
    tr = 256 if rows % 256 == 0 else rows

    def body(w_ref, g_ref, m_ref, v_ref, d_ref, nm_ref, nv_ref):
        d_ref[...], nm_ref[...], nv_ref[...] = _adamw_values(w_ref[...], g_ref[...], m_ref[...], v_ref[...])

    spec = pl.BlockSpec((tr, cols), lambda i: (i, 0))
    return pl.pallas_call(
        body, name=name, grid=(rows // tr,), in_specs=[spec] * 4, out_specs=[spec] * 3,
        out_shape=[jax.ShapeDtypeStruct((rows, cols), F32)] * 3,
        compiler_params=_cp(("parallel",)),
    )(w, g, m, v)


def _adamw_values(w, g, m, v):
    nm = B1 * m + (1.0 - B1) * g
    nv = B2 * v + (1.0 - B2) * (g * g)
    m_hat = nm / (1.0 - B1 ** STEP)
    v_hat = nv / (1.0 - B2 ** STEP)
    return (-LR) * (m_hat / (jnp.sqrt(v_hat) + ADAM_EPS) + WD * w), nm, nv


NB = R // HEAD
SMALL = (("b_ada", (1, 3 * D)), ("norm_pre", (1, D)), ("norm_post", (1, D)), ("conv_w", (4, R // NCHIP)),
         ("conv_b", (1, R)), ("w_rg_a", (NB, HEAD, HEAD)), ("b_rg_a", (1, R)), ("w_rg_x", (NB, HEAD, HEAD)),
         ("b_rg_x", (1, R)), ("lru_lambda", (1, R)), ("norm_rec", (1, R)), ("norm_att", (1, R)))


def _small_update(ao8, sm8, dwa8, dwx8, ai8, cg, params):
    n = len(SMALL)

    def body(ao_ref, sm_ref, dwa_ref, dwx_ref, ai_ref, cg_ref, *refs):
        pin, pout, (gada_ref, loss_ref, dmod) = refs[:3 * n], refs[3 * n:7 * n], refs[7 * n:]
        xx, yy, _ = _me()
        ci = 2 * xx + yy

        def total(ref, *idx):
            acc = ref[(0,) + idx].astype(F32)
            for d in range(1, NDEV):
                acc = acc + ref[(d,) + idx].astype(F32)
            return acc

        row = lambda ref, r, lanes=slice(None): total(ref, slice(r, r + 1), lanes)
        mine = lambda parts: sum(jnp.where(ci == j, part, 0.0) for j, part in enumerate(parts))
        cw = R // NCHIP
        grads = {
            "b_ada": [jnp.concatenate([row(ai_ref, 0), row(ai_ref, 1), row(ao_ref, 0)], axis=1)],
            "norm_pre": [row(ai_ref, 2)], "norm_post": [row(ao_ref, 1)],
            "conv_w": [mine([row(sm_ref, 8 + r, slice(j * cw, (j + 1) * cw)) for j in range(NCHIP)]) for r in range(4)],
            "conv_b": [row(sm_ref, 4)], "b_rg_a": [row(sm_ref, 0)], "b_rg_x": [row(sm_ref, 1)],
            "lru_lambda": [row(sm_ref, 2)], "norm_rec": [row(sm_ref, 3)], "norm_att": [row(ao_ref, 2, slice(0, R))],
            "w_rg_a": [total(dwa_ref, h) for h in range(NB)], "w_rg_x": [total(dwx_ref, h) for h in range(NB)],
        }
        loss_ref[...] = row(ao_ref, 3, slice(0, LANES)) * (0.5 / D)
        for k, (name, shape) in enumerate(SMALL):
            w_ref, m_ref, v_ref = pin[3 * k:3 * k + 3]
            outs = pout[4 * k:4 * k + 4]
            for r, g in enumerate(grads[name]):
                at = (slice(None),) if len(grads[name]) == 1 else ((r,) if len(shape) == 3 else (slice(r, r + 1),))
                res = (g,) + _adamw_values(w_ref[at], g, m_ref[at], v_ref[at])
                for o_ref, val in zip(outs, res):
                    o_ref[at] = val
        for d in range(NDEV):
            dmod[d:d + 1, :] = jnp.concatenate([ai_ref[d, 0:1, :], ai_ref[d, 1:2, :], ao_ref[d, 0:1, :]], axis=1)
        cols = mine([dmod[:, j * EC:(j + 1) * EC] for j in range(NCHIP)])
        colsb = jnp.concatenate([cols, jnp.zeros_like(cols)], axis=0).astype(BF16)
        gada_ref[...] = _dot_tn(_silu_rows(cg_ref), colsb)

    shapes = [jax.ShapeDtypeStruct(s, F32) for _, s in SMALL]
    outs = pl.pallas_call(
        body, name="small_update",
        out_shape=[s for s in shapes for _ in range(4)] + [jax.ShapeDtypeStruct((D, EC), F32),
                                                           jax.ShapeDtypeStruct((1, LANES), F32)],
        scratch_shapes=[pltpu.VMEM((NDEV, 3 * D), F32)],
        compiler_params=_cp(),
    )(ao8, sm8, dwa8, dwx8, ai8, cg, *params)
    return outs[:4 * n], outs[4 * n], outs[4 * n + 1]


BIG = ("w_ada", "w_in", "w_out")
WEIGHTS = ("w_ada", "b_ada", "norm_pre", "norm_post", "w_in", "conv_w", "conv_b", "w_rg_a", "b_rg_a", "w_rg_x",
           "b_rg_x", "lru_lambda", "norm_rec", "norm_att", "w_out")


def kernel(x, c, positions, w_ada, b_ada, norm_pre, norm_post, w_in, conv_w, conv_b, w_rg_a, b_rg_a, w_rg_x, b_rg_x, lru_lambda, norm_rec, norm_att, w_out, loss_target, m_w_ada, m_b_ada, m_norm_pre, m_norm_post, m_w_in, m_conv_w, m_conv_b, m_w_rg_a, m_b_rg_a, m_w_rg_x, m_b_rg_x, m_lru_lambda, m_norm_rec, m_norm_att, m_w_out, v_w_ada, v_b_ada, v_norm_pre, v_norm_post, v_w_in, v_conv_w, v_conv_b, v_w_rg_a, v_b_rg_a, v_w_rg_x, v_b_rg_x, v_lru_lambda, v_norm_rec, v_norm_att, v_w_out):
    given = dict(locals())
    wts = {n: given[n] for n in WEIGHTS}
    ms = {n: given["m_" + n] for n in WEIGHTS}
    vs = {n: given["v_" + n] for n in WEIGHTS}
    xi, yi, _ = _me()
    chip = 2 * xi + yi
    cw_loc = R // NCHIP

    b_cols = lax.dynamic_slice(b_ada, (0, chip * EC), (1, EC))
    order = (chip ^ jnp.arange(NCHIP, dtype=jnp.int32)).astype(jnp.int32)
    g0, mod, w_in_bf, cos, sin, proj, ht = _start_in_proj(
        jnp.concatenate([c, conv_w.reshape(1, 4 * cw_loc)], axis=1), w_ada[0], b_cols, w_in[0],
        positions.reshape(S, 1), x[0], norm_pre, order)
    cg = g0[:, 0:D]
    conv_full = g0[0::2, D:].reshape(NCHIP, 4, cw_loc).transpose(1, 0, 2).reshape(4, R)

    p = dict(norm_pre=norm_pre, norm_post=norm_post, conv_b=conv_b, b_rg_a=b_rg_a, b_rg_x=b_rg_x,
             lru_lambda=lru_lambda, norm_rec=norm_rec, norm_att=norm_att, w_rg_a=w_rg_a[0], w_rg_x=w_rg_x[0])
    grad_x, g_in, g_out, gathered = _local_step(
        x[0], cos, sin, loss_target[0], mod, w_in_bf, proj, ht, w_out[0], conv_full, p)

    params = [d[n].reshape(shape) for n, shape in SMALL for d in (wts, ms, vs)]
    small_out, g_ada, loss_row = _small_update(*gathered, cg, params)
    grads = {"w_out": g_out, "w_in": g_in, "w_ada": g_ada}
    delta, new_m, new_v = {}, {}, {}
    for k, (n, _) in enumerate(SMALL):
        grads[n], delta[n], new_m[n], new_v[n] = small_out[4 * k:4 * k + 4]
    for n in BIG:
        delta[n], new_m[n], new_v[n] = _adamw(wts[n][0], grads[n], ms[n][0], vs[n][0], "adamw_" + n)
    out = lambda d: [d[n].reshape(wts[n].shape) for n in WEIGHTS]
    return (loss_row[0, 0], grad_x.reshape(x.shape), *out(grads), *out(delta), *out(new_m), *out(new_v))
```

```python
import numpy as np
import jax
import jax.numpy as jnp
from jax import lax
from jax.experimental import pallas as pl
from jax.experimental.pallas import tpu as pltpu

F32 = jnp.float32
BF16 = jnp.bfloat16

S = 2048
D = 1024
E = 3072
R = 512
NDEV = 8
NCHIP = 4
EC = 768
LRU_C = 8.0
EPS = 1e-6
NEG = -1e30
HEAD = 64
BLK = 128
PATTERNS = (1, 4, 16)
ROPE_THETA = 10000.0
LANES = 128
VMEM_LIMIT = 56 * 1024 * 1024

B1, B2, LR, WD, ADAM_EPS, STEP = 0.9, 0.999, 0.001, 0.01, 1e-8, 10
MESH = pl.DeviceIdType.MESH


def _cp(sem=None, **kw):
    return pltpu.CompilerParams(dimension_semantics=sem, vmem_limit_bytes=VMEM_LIMIT, **kw)


def _dot(a, b):
    return jnp.dot(a, b, preferred_element_type=F32)


def _dot_nt(a, b):
    return lax.dot_general(a, b, (((1,), (1,)), ((), ())), preferred_element_type=F32)


def _dot_tn(a, b):
    return lax.dot_general(a, b, (((0,), (0,)), ((), ())), preferred_element_type=F32)


def _sigmoid(x):
    return 1.0 / (1.0 + jnp.exp(-x))


def _one_minus_exp(x, ex):
    poly = -x * (1.0 + x * (0.5 + x * (1.0 / 6 + x * (1.0 / 24))))
    return jnp.where(x > -1.0 / 16, poly, 1.0 - ex)


def _rms_fwd(v, g):
    rstd = lax.rsqrt(jnp.mean(v * v, axis=-1, keepdims=True) + EPS)
    vn = v * rstd
    return vn * g, vn, rstd


def _rms_bwd(dy, vn, rstd, g):
    dvn = dy * g
    dv = rstd * (dvn - vn * jnp.mean(dvn * vn, axis=-1, keepdims=True))
    return dv, jnp.sum(dy * vn, axis=0, keepdims=True)


RT = 256


def _shift_down(cur, prev8, j, row):
    if j == 0:
        return cur
    rolled = pltpu.roll(cur, j, 0)
    top = jnp.where(row[0:8] >= j, rolled[0:8], pltpu.roll(prev8, j, 0))
    return jnp.concatenate([top, rolled[8:]], axis=0)


def _shift_up(cur, next8, j, row):
    if j == 0:
        return cur
    rolled = pltpu.roll(cur, RT - j, 0)
    bot = jnp.where(row[RT - 8:] < RT - j, rolled[RT - 8:], pltpu.roll(next8, 8 - j, 0))
    return jnp.concatenate([rolled[:RT - 8], bot], axis=0)


def _rec_gates(xp, xprev8, row, cw_ref, cb_ref, wa_ref, ba_ref, wx_ref, bx_ref, lam_ref):
    xa = cb_ref[...] + sum(cw_ref[3 - j:4 - j, :] * _shift_down(xp, xprev8, j, row) for j in range(4))
    xab = xa.astype(BF16)
    r = _sigmoid(_dot(xab, wa_ref[...]) + ba_ref[...])
    ig = _sigmoid(_dot(xab, wx_ref[...]) + bx_ref[...])
    nl = -lam_ref[...]
    sp = jnp.maximum(nl, 0.0) + jnp.log1p(jnp.exp(-jnp.abs(nl)))
    la = (-LRU_C) * r * sp
    a = jnp.exp(la)
    mult = jnp.sqrt(_one_minus_exp(2.0 * la, a * a))
    return dict(xa=xa, xab=xab, r=r, ig=ig, sp=sp, la=la, a=a, mult=mult)


def _scan_fwd(a, u, row):
    sh = 1
    while sh < RT:
        a_s = jnp.where(row >= sh, pltpu.roll(a, sh, 0), 1.0)
        u_s = jnp.where(row >= sh, pltpu.roll(u, sh, 0), 0.0)
        u = a * u_s + u
        a = a * a_s
        sh *= 2
    return a, u


def _scan_bwd(al, g, row):
    sh = 1
    while sh < RT:
        al_s = jnp.where(row < RT - sh, pltpu.roll(al, RT - sh, 0), 1.0)
        g_s = jnp.where(row < RT - sh, pltpu.roll(g, RT - sh, 0), 0.0)
        g = g + al * g_s
        al = al * al_s
        sh *= 2
    return g


def _dense_from_blocks(blocks_ref, dense_ref):
    dense_ref[...] = jnp.zeros_like(dense_ref)
    for h in range(R // HEAD):
        dense_ref[h * HEAD:(h + 1) * HEAD, h * HEAD:(h + 1) * HEAD] = blocks_ref[h].astype(dense_ref.dtype)


def _rec_fwd(proj, conv_w, conv_b, wa_b, ba, wx_b, bx, lam, norm_rec):
    nt = S // RT

    def body(p_ref, cw_ref, cb_ref, wa_ref, ba_ref, wx_ref, bx_ref, lam_ref, nr_ref,
             h_ref, ya_ref, prev8, hc, wad, wxd):
        i = pl.program_id(0)

        @pl.when(i == 0)
        def _():
            prev8[...] = jnp.zeros_like(prev8)
            hc[...] = jnp.zeros_like(hc)
            _dense_from_blocks(wa_ref, wad)
            _dense_from_blocks(wx_ref, wxd)

        row = lax.broadcasted_iota(jnp.int32, (RT, R), 0)
        xp = p_ref[:, 0:R]
        ga = p_ref[:, R:2 * R]
        f = _rec_gates(xp, prev8[...], row, cw_ref, cb_ref, wad, ba_ref, wxd, bx_ref, lam_ref)
        u = f["mult"] * (f["ig"] * f["xa"])
        acum, hh = _scan_fwd(f["a"], u, row)
        h = hh + acum * hc[0:1, :]
        h_ref[...] = h
        hc[0:1, :] = h_ref[RT - 1:RT, :]
        prev8[...] = p_ref[RT - 8:RT, 0:R]
        yp = h * (ga * _sigmoid(ga))
        ya, _, _ = _rms_fwd(yp, nr_ref[...])
        ya_ref[...] = ya.astype(BF16)

    row1 = lambda n: pl.BlockSpec((1, n), lambda i: (0, 0))
    blocks = pl.BlockSpec((R // HEAD, HEAD, HEAD), lambda i: (0, 0, 0))
    return pl.pallas_call(
        body, name="rec_fwd", grid=(nt,),
        in_specs=[pl.BlockSpec((RT, 2 * R), lambda i: (i, 0)), pl.BlockSpec((4, R), lambda i: (0, 0)), row1(R),
                  blocks, row1(R), blocks, row1(R), row1(R), row1(R)],
        out_specs=[pl.BlockSpec((RT, R), lambda i: (i, 0)), pl.BlockSpec((RT, R), lambda i: (i, 0))],
        out_shape=[jax.ShapeDtypeStruct((S, R), F32), jax.ShapeDtypeStruct((S, R), BF16)],
        scratch_shapes=[pltpu.VMEM((8, R), F32), pltpu.VMEM((8, R), F32), pltpu.VMEM((R, R), BF16),
                        pltpu.VMEM((R, R), BF16)],
        compiler_params=_cp(("arbitrary",)),
    )(proj, conv_w, conv_b, wa_b, ba, wx_b, bx, lam, norm_rec)


def _rec_bwd(dproj, d_ya, proj, h_all, conv_w, conv_b, wa_b, ba, wx_b, bx, lam, norm_rec):
    nt = S // RT

    def body(dp_in, dya_ref, p_ref, pprev_ref, h_ref, hprev_ref, cw_ref, cb_ref, wab_ref, ba_ref, wxb_ref, bx_ref,
             lam_ref, nr_ref, dp_ref, dwab_ref, dwxb_ref, sm_ref, nxt8, cg, wa_ref, wx_ref, dwa_ref, dwx_ref):
        i = pl.program_id(0)
        ti = nt - 1 - i

        @pl.when(i == 0)
        def _():
            nxt8[...] = jnp.zeros_like(nxt8)
            cg[...] = jnp.zeros_like(cg)
            dwa_ref[...] = jnp.zeros_like(dwa_ref)
            dwx_ref[...] = jnp.zeros_like(dwx_ref)
            sm_ref[...] = jnp.zeros_like(sm_ref)
            _dense_from_blocks(wab_ref, wa_ref)
            _dense_from_blocks(wxb_ref, wx_ref)

        row = lax.broadcasted_iota(jnp.int32, (RT, R), 0)
        first = (ti > 0).astype(F32)
        xprev8 = pprev_ref[...] * first
        hprev8 = hprev_ref[...] * first
        xp = p_ref[:, 0:R]
        ga = p_ref[:, R:2 * R]
        f = _rec_gates(xp, xprev8, row, cw_ref, cb_ref, wa_ref, ba_ref, wx_ref, bx_ref, lam_ref)
        xa, r, ig, a, mult = f["xa"], f["r"], f["ig"], f["a"], f["mult"]
        h = h_ref[...]
        sg = _sigmoid(ga)
        gate = ga * sg
        yp = h * gate
        _, ypn, rstd = _rms_fwd(yp, nr_ref[...])
        d_yp, dnr = _rms_bwd(dya_ref[...], ypn, rstd, nr_ref[...])
        d_ga = d_yp * h * (sg * (1.0 + ga * (1.0 - sg)))
        dh = d_yp * gate + jnp.where(row == RT - 1, cg[0:1, :], 0.0)
        al = jnp.where(row < RT - 1, pltpu.roll(a, RT - 1, 0), 0.0)
        g = _scan_bwd(al, dh, row)
        cg[0:1, :] = jnp.sum(jnp.where(row == 0, a * g, 0.0), axis=0, keepdims=True)
        h_m1 = _shift_down(h, hprev8, 1, row)
        da = g * h_m1
        ix = ig * xa
        d_mult = g * ix
        d_ig = g * mult * xa
        d_xa = g * mult * ig
        d_la = da * a - d_mult * (a * a) / mult
        d_r = d_la * ((-LRU_C) * f["sp"])
        dsp = jnp.sum(d_la * ((-LRU_C) * r), axis=0, keepdims=True)
        dlam = dsp * (-_sigmoid(-lam_ref[...]))
        d_za = d_r * r * (1.0 - r)
        d_zx = d_ig * ig * (1.0 - ig)
        dzab = d_za.astype(BF16)
        dzxb = d_zx.astype(BF16)
        dwa_ref[...] += _dot_tn(f["xab"], dzab)
        dwx_ref[...] += _dot_tn(f["xab"], dzxb)
        d_xa = d_xa + _dot_nt(dzab, wa_ref[...]) + _dot_nt(dzxb, wx_ref[...])
        d_xp = sum(cw_ref[3 - j:4 - j, :] * _shift_up(d_xa, nxt8[...], j, row) for j in range(4))
        dcw = [jnp.sum(d_xa * _shift_down(xp, xprev8, 3 - k, row), axis=0, keepdims=True) for k in range(4)]
        dp_ref[:, 0:R] = d_xp.astype(BF16)
        dp_ref[:, R:2 * R] = d_ga.astype(BF16)
        dp8 = d_xa[0:8, :]
        nxt8[...] = dp8
        sm_ref[0:1, :] += jnp.sum(d_za, axis=0, keepdims=True)
        sm_ref[1:2, :] += jnp.sum(d_zx, axis=0, keepdims=True)
        sm_ref[2:3, :] += dlam
        sm_ref[3:4, :] += dnr
        sm_ref[4:5, :] += jnp.sum(d_xa, axis=0, keepdims=True)
        for k in range(4):
            sm_ref[8 + k:9 + k, :] += dcw[k]

        @pl.when(i == nt - 1)
        def _():
            for h in range(R // HEAD):
                dwab_ref[h] = dwa_ref[h * HEAD:(h + 1) * HEAD, h * HEAD:(h + 1) * HEAD].astype(BF16)
                dwxb_ref[h] = dwx_ref[h * HEAD:(h + 1) * HEAD, h * HEAD:(h + 1) * HEAD].astype(BF16)

    c0 = lambda shape: pl.BlockSpec(shape, lambda i: (0, 0))
    blocks = pl.BlockSpec((R // HEAD, HEAD, HEAD), lambda i: (0, 0, 0))
    rev = lambda i: nt - 1 - i
    prev8 = lambda i: (jnp.maximum((nt - 1 - i) * (RT // 8) - 1, 0), 0)
    return pl.pallas_call(
        body, name="rec_bwd", grid=(nt,),
        in_specs=[pl.BlockSpec(memory_space=pl.ANY),
                  pl.BlockSpec((RT, R), lambda i: (rev(i), 0)),
                  pl.BlockSpec((RT, 2 * R), lambda i: (rev(i), 0)), pl.BlockSpec((8, R), prev8),
                  pl.BlockSpec((RT, R), lambda i: (rev(i), 0)), pl.BlockSpec((8, R), prev8),
                  c0((4, R)), c0((1, R)), blocks, c0((1, R)), blocks, c0((1, R)), c0((1, R)), c0((1, R))],
        out_specs=[pl.BlockSpec((RT, 2 * R), lambda i: (rev(i), 0)), blocks, blocks, c0((16, R))],
        out_shape=[jax.ShapeDtypeStruct((S, E), BF16), jax.ShapeDtypeStruct((R // HEAD, HEAD, HEAD), BF16),
                   jax.ShapeDtypeStruct((R // HEAD, HEAD, HEAD), BF16), jax.ShapeDtypeStruct((16, R), F32)],
        scratch_shapes=[pltpu.VMEM((8, R), F32), pltpu.VMEM((8, R), F32), pltpu.VMEM((R, R), BF16),
                        pltpu.VMEM((R, R), BF16), pltpu.VMEM((R, R), F32), pltpu.VMEM((R, R), F32)],
        input_output_aliases={0: 0},
        compiler_params=_cp(("arbitrary",)),
    )(dproj, d_ya, proj, proj, h_all, h_all, conv_w, conv_b, wa_b, ba, wx_b, bx, lam, norm_rec)


NPAIR = R // LANES
QB, KB, VB, GB = 2 * R // LANES, 3 * R // LANES, 4 * R // LANES, 5 * R // LANES


def _rope_freq():
    half = HEAD // 2
    inv = np.float32(ROPE_THETA) ** (-(np.arange(half, dtype=np.float32) / np.float32(half)))
    return jnp.asarray(np.tile(inv.astype(np.float32), LANES // half)[None, :])


def _rot_half(x, first):
    return jnp.where(first, -pltpu.roll(x, LANES - HEAD // 2, 1), pltpu.roll(x, HEAD // 2, 1))


def _cos_sin(pos_ref, freq_ref):
    ang = pos_ref[...].astype(F32) * freq_ref[...]
    return jnp.cos(ang), jnp.sin(ang)


SUB = 4


def _stages(d):
    assert d in (1, SUB, SUB * SUB)
    return d > SUB


def _strided_rows(src_ref, d, tmp):
    n = S // d
    if not _stages(d):
        for r in range(d):
            yield r * n, (src_ref[pl.ds(r, n, stride=d), :] if d > 1 else src_ref[...])
        return
    m = S // SUB
    for r in range(SUB):
        tmp[r * m:(r + 1) * m, :] = src_ref[pl.ds(r, m, stride=SUB), :]
    for r in range(SUB):
        for q in range(SUB):
            yield (r + SUB * q) * n, tmp[pl.ds(r * m + q, n, stride=SUB), :]


def _deint(src_ref, dst_ref, d, tmp):
    n = S // d
    for row0, v in _strided_rows(src_ref, d, tmp):
        dst_ref[row0:row0 + n, :] = v.astype(dst_ref.dtype)


def _reint(src_ref, dst_ref, d, accumulate, tmp):
    if _stages(d):
        n, m = S // d, S // SUB
        for r in range(SUB):
            for q in range(SUB):
                tmp[pl.ds(r * m + q, n, stride=SUB), :] = src_ref[(r + SUB * q) * n:(r + SUB * q + 1) * n, :]
        src_ref, d = tmp, SUB
    n = S // d
    for r in range(d):
        idx = (pl.ds(r, n, stride=d), slice(None)) if d > 1 else (slice(None), slice(None))
        v = src_ref[r * n:(r + 1) * n, :]
        if accumulate:
            dst_ref[idx] = dst_ref[idx] + v
        else:
            dst_ref[idx] = v


def _deint_heads(src_ref, dst0, dst1, d, tmp):
    n = S // d
    hm0 = lax.broadcasted_iota(jnp.int32, (n, LANES), 1) < HEAD
    for row0, v in _strided_rows(src_ref, d, tmp):
        dst0[row0:row0 + n, :] = jnp.where(hm0, v, 0.0).astype(BF16)
        dst1[row0:row0 + n, :] = jnp.where(hm0, 0.0, v).astype(BF16)


def _reint_prev(src_ref, dst_ref, d):
    n = S // d
    if n == BLK:
        return
    for r in range(d):
        idx = (pl.ds(r, n - BLK, stride=d), slice(None)) if d > 1 else (slice(0, n - BLK), slice(None))
        dst_ref[idx] = dst_ref[idx] + src_ref[r * n + BLK:(r + 1) * n, :]


def _pair_masks():
    qi = lax.broadcasted_iota(jnp.int32, (BLK, 2 * BLK), 0)
    ki = lax.broadcasted_iota(jnp.int32, (BLK, 2 * BLK), 1) & (BLK - 1)
    return ki <= qi, ki >= qi


def _two(ref0, ref1, st, axis):
    return jnp.concatenate([ref0[pl.ds(st, BLK), :], ref1[pl.ds(st, BLK), :]], axis=axis)


ATT_UNROLL = 8


def _att_fwd(proj, cos, sin, w_out):
    def body(q_ref, k_ref, v_ref, cos_ref, sin_ref, w_ref, att_ref, qr_ref, kr_ref, lse_ref, wbf_ref,
             qd, kd0, kd1, vd0, vd1, od, ld, tmp, on, ln, wbuf, *wsems):
        wg = _WeightGather(w_ref, wbuf, *wsems)
        pl.when(pl.program_id(0) == 0)(wg.start)
        pl.when(pl.program_id(0) == 1)(wg.forward)
        lane = lax.broadcasted_iota(jnp.int32, (S, LANES), 1)
        first = (lane & (HEAD // 2)) == 0
        cos, sin = cos_ref[...], sin_ref[...]
        q = q_ref[...]
        k = k_ref[...]
        qr_ref[...] = (q * cos + _rot_half(q, first) * sin) * (HEAD ** -0.5)
        kr_ref[...] = k * cos + _rot_half(k, first) * sin
        hm0 = lax.broadcasted_iota(jnp.int32, (BLK, LANES), 1) < HEAD
        top = lax.broadcasted_iota(jnp.int32, (2 * BLK, LANES), 0) < BLK
        ones2 = (top == (lax.broadcasted_iota(jnp.int32, (2 * BLK, LANES), 1) < HEAD)).astype(BF16)
        mc2, mp2 = _pair_masks()

        for pi, d in enumerate(PATTERNS):
            nb = S // d // BLK
            _deint(qr_ref, qd, d, tmp)
            _deint_heads(kr_ref, kd0, kd1, d, tmp)
            _deint_heads(v_ref, vd0, vd1, d, tmp)

            def blk(b, carry):
                st = pl.multiple_of(b * BLK, BLK)
                qb = qd[pl.ds(st, BLK), :]
                sc = jnp.where(mc2, _dot_nt(qb, _two(kd0, kd1, st, 0)), NEG)
                mx = sc
                if nb > 1:
                    stp = pl.multiple_of(jnp.maximum(b - 1, 0) * BLK, BLK)
                    mp = jnp.logical_and(mp2, lax.rem(b, nb) != 0)
                    sp = jnp.where(mp, _dot_nt(qb, _two(kd0, kd1, stp, 0)), NEG)
                    mx = jnp.maximum(sc, sp)
                m0 = jnp.max(mx[:, 0:BLK], axis=1, keepdims=True)
                m1 = jnp.max(mx[:, BLK:2 * BLK], axis=1, keepdims=True)
                mf = jnp.concatenate([jnp.broadcast_to(m0, (BLK, BLK)), jnp.broadcast_to(m1, (BLK, BLK))], axis=1)
                o = _dot(jnp.exp(sc - mf).astype(BF16), jnp.concatenate([_two(vd0, vd1, st, 0), ones2], axis=1))
                if nb > 1:
                    o = o + _dot(jnp.exp(sp - mf).astype(BF16), jnp.concatenate([_two(vd0, vd1, stp, 0), ones2], axis=1))
                l = o[:, LANES:2 * LANES]
                od[pl.ds(st, BLK), :] = o[:, 0:LANES] / l
                ld[pl.ds(st, BLK), :] = jnp.where(hm0, m0, m1) + jnp.log(l)
                return carry

            lax.fori_loop(0, S // BLK, blk, 0, unroll=ATT_UNROLL)
            _reint(od, on.at[pi], d, False, tmp)
            _reint(ld, ln.at[pi], d, False, tmp)

        l0, l1, l2 = ln[0], ln[1], ln[2]
        m = jnp.maximum(jnp.maximum(l0, l1), l2)
        e0, e1, e2 = jnp.exp(l0 - m), jnp.exp(l1 - m), jnp.exp(l2 - m)
        den = e0 + e1 + e2
        att_ref[...] = (e0 * on[0] + e1 * on[1] + e2 * on[2]) / den
        lse_ref[...] = m + jnp.log(den)

        @pl.when(pl.program_id(0) == NPAIR - 1)
        def _():
            wg.finish()
            wbf_ref[...] = wbuf[...]

    col = lambda c0: pl.BlockSpec((S, LANES), lambda p: (0, c0 + p))
    out = pl.BlockSpec((S, LANES), lambda p: (0, p))
    tab = pl.BlockSpec((S, LANES), lambda p: (0, 0))
    vm = pl.BlockSpec(memory_space=pltpu.VMEM)
    return pl.pallas_call(
        body, name="att_fwd", grid=(NPAIR,),
        in_specs=[col(QB), col(KB), col(VB), tab, tab, vm],
        out_specs=[out, out, out, out, vm],
        out_shape=[jax.ShapeDtypeStruct((S, R), F32)] * 4 + [jax.ShapeDtypeStruct((NCHIP,) + w_out.shape, BF16)],
        scratch_shapes=[pltpu.VMEM((S, LANES), BF16)] * 5 + [pltpu.VMEM((S, LANES), F32)] * 3
        + [pltpu.VMEM((3, S, LANES), F32)] * 2 + [pltpu.VMEM((NCHIP,) + w_out.shape, BF16)] + _WeightGather.SEMS,
        compiler_params=_cp(("arbitrary",)),
    )(proj, proj, proj, cos, sin, w_out)


def _att_bwd(dproj, d_att, att, lse, qr, kr, proj, cos, sin, gw_out4):
    out_units = [(j, j, 0) for j in range(NCHIP)]

    nblk = S // BLK

    def body(dp_in, do_ref, o_ref, lse_ref, qr_ref, kr_ref, v_ref, cos_ref, sin_ref, gw_ref, dp_ref, gout_ref,
             qd, kd0, kd1, vd0, vd1, dod, kt, packn, packd, dqd, dkcd, dkpd, dvcd, dvpd,
             dqn, dkn, dvn, tmp, rows, trs, pts, dss, stage, sems, gred, *rs_scratch):
        p = pl.program_id(0)
        rs = _ReduceScatter(gw_ref, gred, out_units, *rs_scratch)
        for step, piece in enumerate((rs.start_halves, rs.send_partials, rs.reduce_owned)):
            pl.when(p == step)(piece)

        @pl.when(p == NPAIR - 1)
        def _():
            rs.finish()
            gout_ref[...] = gred[...]

        lane = lax.broadcasted_iota(jnp.int32, (S, LANES), 1)
        hms = lane < HEAD
        prod = do_ref[...] * o_ref[...]
        d0 = jnp.sum(jnp.where(hms, prod, 0.0), axis=1, keepdims=True)
        d1 = jnp.sum(jnp.where(hms, 0.0, prod), axis=1, keepdims=True)
        lse = lse_ref[...]
        quarter = HEAD // 2
        packn[...] = jnp.where(lane < quarter, lse,
                               jnp.where(hms, pltpu.roll(lse, LANES - quarter, 1), jnp.where(lane < 3 * quarter, d0, d1)))
        dqn[...] = jnp.zeros_like(dqn)
        dkn[...] = jnp.zeros_like(dkn)
        dvn[...] = jnp.zeros_like(dvn)
        hm0 = lax.broadcasted_iota(jnp.int32, (BLK, LANES), 1) < HEAD
        key = lax.broadcasted_iota(jnp.int32, (2 * BLK, BLK), 0) & (BLK - 1)
        qry = lax.broadcasted_iota(jnp.int32, (2 * BLK, BLK), 1)
        mct, mpt = key <= qry, key >= qry

        for d in PATTERNS:
            nb = S // d // BLK
            _deint(qr_ref, qd, d, tmp)
            _deint_heads(kr_ref, kd0, kd1, d, tmp)
            _deint_heads(v_ref, vd0, vd1, d, tmp)
            _deint(do_ref, dod, d, tmp)
            _deint(packn, packd, d, tmp)

            sides = (0, 1) if nb > 1 else (0,)

            def probs(b, carry):
                st = pl.multiple_of(b * BLK, BLK)
                kt[b] = _two(kd0, kd1, st, 0).astype(F32).T.astype(BF16)
                trs[b] = packd[pl.ds(st, BLK), :].T
                for j in range(4):
                    rows[b, j:j + 1, :] = trs[b, j * quarter:j * quarter + 1, :]
                qb, dob = qd[pl.ds(st, BLK), :], dod[pl.ds(st, BLK), :]
                both = lambda j: jnp.concatenate([jnp.broadcast_to(rows[b, j:j + 1, :], (BLK, BLK)),
                                                  jnp.broadcast_to(rows[b, j + 1:j + 2, :], (BLK, BLK))], axis=0)
                lbt, dlt = both(0), both(2)
                for sd in sides:
                    stk = pl.multiple_of(jnp.maximum(b - sd, 0) * BLK, BLK)
                    mask = mct if sd == 0 else jnp.logical_and(mpt, lax.rem(b, nb) != 0)
                    k2, v2 = _two(kd0, kd1, stk, 0), _two(vd0, vd1, stk, 0)
                    pt = jnp.where(mask, jnp.exp(_dot_nt(k2, qb) - lbt), 0.0)
                    pts[b, sd] = pt.astype(BF16)
                    dss[b, sd] = (pt * (_dot_nt(v2, dob) - dlt)).astype(BF16)
                return carry

            lax.fori_loop(0, nblk, probs, 0, unroll=ATT_UNROLL)

            def prods(b, carry):
                st = pl.multiple_of(b * BLK, BLK)
                qb, dob = qd[pl.ds(st, BLK), :], dod[pl.ds(st, BLK), :]
                dq_t = None
                for sd in sides:
                    dst, ptb = dss[b, sd], pts[b, sd]
                    rk, rv = _dot(dst, qb), _dot(ptb, dob)
                    dqs = _dot(kt[jnp.maximum(b - sd, 0)], dst)
                    dq_t = dqs if dq_t is None else dq_t + dqs
                    dk, dv = (dkcd, dvcd) if sd == 0 else (dkpd, dvpd)
                    dk[pl.ds(st, BLK), :] = jnp.where(hm0, rk[0:BLK], rk[BLK:2 * BLK])
                    dv[pl.ds(st, BLK), :] = jnp.where(hm0, rv[0:BLK], rv[BLK:2 * BLK])
                dqd[pl.ds(st, BLK), :] = dq_t.T
                return carry

            lax.fori_loop(0, nblk, prods, 0, unroll=ATT_UNROLL)
            _reint(dqd, dqn, d, True, tmp)
            _reint(dkcd, dkn, d, True, tmp)
            _reint(dvcd, dvn, d, True, tmp)
            _reint_prev(dkpd, dkn, d)
            _reint_prev(dvpd, dvn, d)

        lane = lax.broadcasted_iota(jnp.int32, (S, LANES), 1)
        first = (lane & (HEAD // 2)) == 0
        cos, sin = cos_ref[...], sin_ref[...]
        dq = dqn[...] * (HEAD ** -0.5)
        dk = dkn[...]
        stage[0] = (dq * cos - _rot_half(dq, first) * sin).astype(BF16)
        stage[1] = (dk * cos - _rot_half(dk, first) * sin).astype(BF16)
        stage[2] = dvn[...].astype(BF16)
        copies = [pltpu.make_async_copy(stage.at[j], dp_ref.at[:, pl.ds((2 + j) * R + p * LANES, LANES)], sems.at[j])
                  for j in range(3)]
        for cp in copies:
            cp.start()
        for cp in copies:
            cp.wait()

    blk = pl.BlockSpec((S, LANES), lambda p: (0, p))
    tab = pl.BlockSpec((S, LANES), lambda p: (0, 0))
    vm = pl.BlockSpec(memory_space=pltpu.VMEM)
    _, orows, ocols = gw_out4.shape
    return pl.pallas_call(
        body, name="att_bwd", grid=(NPAIR,),
        in_specs=[pl.BlockSpec(memory_space=pl.ANY), blk, blk, blk, blk, blk,
                  pl.BlockSpec((S, LANES), lambda p: (0, VB + p)), tab, tab, vm],
        out_specs=[pl.BlockSpec(memory_space=pl.ANY), vm],
        out_shape=[jax.ShapeDtypeStruct((S, E), BF16), jax.ShapeDtypeStruct((orows, ocols), F32)],
        scratch_shapes=[pltpu.VMEM((S, LANES), BF16)] * 6 + [pltpu.VMEM((nblk, LANES, 2 * BLK), BF16)]
        + [pltpu.VMEM((S, LANES), F32)] * 11
        + [pltpu.VMEM((nblk, 8, BLK), F32), pltpu.VMEM((nblk, LANES, BLK), F32)]
        + [pltpu.VMEM((nblk, 2, 2 * BLK, BLK), BF16)] * 2
        + [pltpu.VMEM((3, S, LANES), BF16), pltpu.SemaphoreType.DMA((3,)), pltpu.VMEM((orows, ocols), F32)]
        + _ReduceScatter.scratch(NCHIP, orows, ocols, 1),
        input_output_aliases={0: 0},
        compiler_params=_cp(("arbitrary",)),
    )(dproj, d_att, att, lse, qr, kr, proj, cos, sin, gw_out4)


def _out_fwd_bwd(ya, att, proj, w_out_bf, x, target, mod, norm_post, norm_att):
    ts = 512

    def body(ya_ref, att_ref, gb_ref, w_ref, x_ref, t_ref, mod_ref, npost_ref, natt_ref,
             gx_ref, dya_ref, datt_ref, dgb_ref, gw_ref, acc_ref):
        i = pl.program_id(0)

        @pl.when(i == 0)
        def _():
            gw_ref[...] = jnp.zeros_like(gw_ref)
            acc_ref[...] = jnp.zeros_like(acc_ref)

        gate = mod_ref[:, 2 * D:3 * D]
        att = att_ref[...]
        gb = gb_ref[...]
        sg = _sigmoid(gb)
        silu = gb * sg
        ybp = att * silu
        yb, ybn, rstd_b = _rms_fwd(ybp, natt_ref[...])
        cat = jnp.concatenate([ya_ref[...], yb.astype(BF16)], axis=1)
        mix = _dot(cat, w_ref[...])
        rn, mn, rstd_m = _rms_fwd(mix, npost_ref[...])
        err = x_ref[...] + gate * rn - t_ref[...]
        dy = err * (1.0 / D)
        gx_ref[...] = dy
        dmix, dnpost = _rms_bwd(dy * gate, mn, rstd_m, npost_ref[...])
        dmb = dmix.astype(BF16)
        gw_ref[...] += _dot_tn(cat, dmb)
        dcat = _dot_nt(dmb, w_ref[...])
        dya_ref[...] = dcat[:, 0:R]
        dybp, dnatt = _rms_bwd(dcat[:, R:2 * R], ybn, rstd_b, natt_ref[...])
        datt_ref[...] = dybp * silu
        dgb_ref[...] = (dybp * att * (sg * (1.0 + gb * (1.0 - sg)))).astype(BF16)
        acc_ref[0:1, :] += jnp.sum(dy * rn, axis=0, keepdims=True)
        acc_ref[1:2, :] += dnpost
        acc_ref[2:3, 0:R] += dnatt
        acc_ref[3:4, :] += jnp.sum(jnp.sum(err * err, axis=1, keepdims=True), axis=0, keepdims=True)

    tile = lambda w: pl.BlockSpec((ts, w), lambda i: (i, 0))
    c0 = lambda shape: pl.BlockSpec(shape, lambda i: (0, 0))
    return pl.pallas_call(
        body, name="out_fwd_bwd", grid=(S // ts,),
        in_specs=[tile(R), tile(R), pl.BlockSpec((ts, R), lambda i: (i, 5)), c0((D, D)), tile(D), tile(D),
                  c0((1, 3 * D)), c0((1, D)), c0((1, R))],
        out_specs=[tile(D), tile(R), tile(R), pl.BlockSpec((ts, R), lambda i: (i, 5)), c0((D, D)), c0((8, D))],
        out_shape=[jax.ShapeDtypeStruct((S, D), F32), jax.ShapeDtypeStruct((S, R), F32),
                   jax.ShapeDtypeStruct((S, R), F32), jax.ShapeDtypeStruct((S, E), BF16),
                   jax.ShapeDtypeStruct((D, D), F32), jax.ShapeDtypeStruct((8, D), F32)],
        compiler_params=_cp(("arbitrary",)),
    )(ya, att, proj, w_out_bf, x, target, mod, norm_post, norm_att)


UC = 256
UPC = EC // UC


NU = E // UC


def _unit_of_step(i):
    return (i % NCHIP) * UPC + i // NCHIP


def _in_proj_bwd(ht, dproj, w_in_bf, x, gx1, mod, norm_pre, smalls):
    ts = 256
    nt = S // ts
    half = D // 2
    units = [_unit_of_step(k) for k in range(NU)]
    owners = [u // UPC for u in units]
    ns = len(smalls)

    def body(*refs):
        (ht_ref, dpu_ref, dp_ref, w_hbm, x_ref, gx1_ref, mod_ref, np_ref), refs = refs[:8], refs[8:]
        small_in, refs = refs[:ns], refs[ns:]
        (gx_ref, gin_ref), refs = refs[:2], refs[2:]
        small_out, (acc_out,), refs = refs[:ns], refs[ns:ns + 1], refs[ns + 1:]
        mine, sib, tmp, stage, got, red, acc_ref, hs, hr, ps, pr, bs, br = refs[:13]
        early = _SmallGather(small_in, small_out, *refs[13:16])
        late = _SmallGather([acc_ref], [acc_out], *refs[16:19])
        w_ref, w_sem = refs[19:21]
        i = pl.program_id(0)
        w_copy = pltpu.make_async_copy(w_hbm, w_ref, w_sem)
        pl.when(i == 0)(w_copy.start)
        pl.when(i == NU)(w_copy.wait)
        xx, yy, c = _me()
        ci = 2 * xx + yy
        r0 = pl.multiple_of(c * half, half)
        r1 = pl.multiple_of((1 - c) * half, half)
        pl.when(i == 0)(early.start)
        pl.when(i == NU)(early.forward)

        def exch(k):
            return _remote(tmp.at[k % 2], sib.at[k], hs.at[k], hr.at[k], 1)

        def partial(k, sender):
            return pltpu.make_async_remote_copy(
                src_ref=stage.at[k], dst_ref=got.at[units[k] % UPC, sender], send_sem=ps.at[k],
                recv_sem=pr.at[k, sender], device_id=(owners[k] // 2, owners[k] % 2, c), device_id_type=MESH)

        def back(k, start):
            off = (units[k] % UPC) * UC
            blk = red.at[pl.ds(start, half), off:off + UC]
            return _remote(blk, blk, bs.at[k], br.at[k], 1)

        for k in range(NU + 1):
            @pl.when(i == k)
            def _():
                if k < NU:
                    if k >= 2:
                        exch(k - 2).wait_send()
                    dpu = dpu_ref[...]
                    tmp[k % 2] = _dot(ht_ref[pl.ds(r1, half), :], dpu)
                    exch(k).start()
                    mine[k] = _dot(ht_ref[pl.ds(r0, half), :], dpu)
                if k >= 1:
                    exch(k - 1).wait_recv()
                    mine[k - 1] += sib[k - 1]

                    @pl.when(ci != owners[k - 1])
                    def _():
                        stage[k - 1] = mine[k - 1].astype(BF16)
                        partial(k - 1, ci).start()

        @pl.when(i == NU)
        def _():
            acc_ref[...] = jnp.zeros_like(acc_ref)

        @pl.when(i >= NU)
        def _():
            dh = sum(_dot_nt(dp_ref[:, j * EC:(j + 1) * EC], w_ref[j]) for j in range(NCHIP))
            hp, xn, rstd = _rms_fwd(x_ref[...], np_ref[...])
            dx, dnp = _rms_bwd(dh * (1.0 + mod_ref[:, D:2 * D]), xn, rstd, np_ref[...])
            gx_ref[...] = gx1_ref[...] + dx
            acc_ref[0:1, :] += jnp.sum(dh, axis=0, keepdims=True)
            acc_ref[1:2, :] += jnp.sum(dh * hp, axis=0, keepdims=True)
            acc_ref[2:3, :] += dnp

        for t in range(UPC):
            @pl.when(i == NU + 1 + 2 * t)
            def _():
                for k in range(NCHIP * t, NCHIP * (t + 1)):
                    @pl.when(ci == owners[k])
                    def _():
                        off = (units[k] % UPC) * UC
                        red[pl.ds(r0, half), off:off + UC] = mine[k]
                        for s in range(NCHIP):
                            if s != owners[k]:
                                partial(k, s).wait_recv()
                                red[pl.ds(r0, half), off:off + UC] += got[units[k] % UPC, s].astype(F32)
                        back(k, r0).start()

        @pl.when(i == NU + nt - 1)
        def _():
            late.start()
            exch(NU - 2).wait_send()
            exch(NU - 1).wait_send()
            for k in range(NU):
                @pl.when(ci == owners[k])
                def _():
                    back(k, r1).wait_recv()
                    back(k, r0).wait_send()

                @pl.when(ci != owners[k])
                def _():
                    partial(k, ci).wait_send()
            gin_ref[...] = red[...]
            early.finish()
            late.forward()
            late.finish()

    tile = lambda w: pl.BlockSpec((ts, w), lambda i: (jnp.maximum(i - NU, 0), 0))
    c0 = lambda shape: pl.BlockSpec(shape, lambda i: (0, 0))
    vm = pl.BlockSpec(memory_space=pltpu.VMEM)
    hbm = pl.BlockSpec(memory_space=pl.ANY)
    gathered = [jax.ShapeDtypeStruct((NDEV,) + a.shape, a.dtype) for a in smalls] + [jax.ShapeDtypeStruct((NDEV, 8, D), F32)]
    return pl.pallas_call(
        body, name="in_proj_bwd", grid=(NU + nt,),
        in_specs=[vm, pl.BlockSpec((S, UC), lambda i: (0, _unit_of_step(jnp.minimum(i, NU - 1)))), tile(E),
                  hbm, tile(D), tile(D), c0((1, 3 * D)), c0((1, D))] + [vm] * ns,
        out_specs=[tile(D), vm] + [hbm] * (ns + 1),
        out_shape=[jax.ShapeDtypeStruct((S, D), F32), jax.ShapeDtypeStruct((D, EC), F32)] + gathered,
        scratch_shapes=[pltpu.VMEM((NU, half, UC), F32), pltpu.VMEM((NU, half, UC), F32),
                        pltpu.VMEM((2, half, UC), F32), pltpu.VMEM((NU, half, UC), BF16),
                        pltpu.VMEM((UPC, NCHIP, half, UC), BF16), pltpu.VMEM((D, EC), F32), pltpu.VMEM((8, D), F32),
                        pltpu.SemaphoreType.DMA((NU,)), pltpu.SemaphoreType.DMA((NU,)),
                        pltpu.SemaphoreType.DMA((NU,)), pltpu.SemaphoreType.DMA((NU, NCHIP)),
                        pltpu.SemaphoreType.DMA((NU,)), pltpu.SemaphoreType.DMA((NU,))]
        + _SmallGather.sems(ns) + _SmallGather.sems(1)
        + [pltpu.VMEM((NCHIP, D, EC), BF16), pltpu.SemaphoreType.DMA],
        compiler_params=_cp(("arbitrary",)),
    )(ht, dproj, dproj, w_in_bf, x, gx1, mod, norm_pre, *smalls)


def _local_step(x, cos, sin, target, mod, w_in_bf, proj, ht, w_out, conv_w, p):
    rec_p = (conv_w, p["conv_b"], p["w_rg_a"], p["b_rg_a"], p["w_rg_x"], p["b_rg_x"], p["lru_lambda"], p["norm_rec"])
    h_all, ya = _rec_fwd(proj, *rec_p)
    att, qr, kr, lse, w_out_bf = _att_fwd(proj, cos, sin, w_out)
    gx1, d_ya, d_att, dproj, gw_out, acc_o = _out_fwd_bwd(ya, att, proj, w_out_bf.reshape(D, D), x, target, mod,
                                                           p["norm_post"], p["norm_att"])
    dproj, g_out = _att_bwd(dproj, d_att, att, lse, qr, kr, proj, cos, sin, gw_out.reshape(NCHIP, D // NCHIP, D))
    dproj, dwa, dwx, sm = _rec_bwd(dproj, d_ya, proj, h_all, *rec_p)
    grad_x, g_in, *gathered = _in_proj_bwd(ht, dproj, w_in_bf, x, gx1, mod, p["norm_pre"], [acc_o, sm, dwa, dwx])
    return grad_x, g_in, g_out, gathered


def _me():
    return lax.axis_index("x"), lax.axis_index("y"), lax.axis_index("c")


def _flip(v, bit):
    return 1 - v if bit else v


def _peer(rel):
    x, y, c = _me()
    return (_flip(x, rel & 4), _flip(y, rel & 2), _flip(c, rel & 1))


def _remote(src, dst, send_sem, recv_sem, rel):
    return pltpu.make_async_remote_copy(src_ref=src, dst_ref=dst, send_sem=send_sem, recv_sem=recv_sem,
                                        device_id=_peer(rel), device_id_type=MESH)


class _WeightGather:
    SEMS = [pltpu.SemaphoreType.DMA((NCHIP - 1,))] * 4

    def __init__(self, w_ref, out_ref, send_sems, recv_sems, fsend_sems, frecv_sems):
        x, y, c = _me()
        self.w, self.out, self.ci = w_ref, out_ref, 2 * x + y
        self.half = w_ref.shape[0] // 2
        self.r0 = pl.multiple_of(c * self.half, self.half)
        self.r1 = pl.multiple_of((1 - c) * self.half, self.half)
        self.sems = (send_sems, recv_sems, fsend_sems, frecv_sems)

    def _ici(self, chip, k):
        blk = self.out.at[chip, pl.ds(self.r0, self.half), :]
        return _remote(blk, blk, self.sems[0].at[k - 1], self.sems[1].at[k - 1], 2 * k)

    def _d2d(self, chip, start, k):
        blk = self.out.at[chip, pl.ds(start, self.half), :]
        return _remote(blk, blk, self.sems[2].at[k - 1], self.sems[3].at[k - 1], 1)

    def start(self, diagonal=True):
        self.out[self.ci] = self.w[...].astype(BF16)
        for k in range(1, NCHIP if diagonal else NCHIP - 1):
            self._ici(self.ci, k).start()

    def _relay(self, chip, piece, k):
        q = self.half // 2
        blk = self.out.at[chip, pl.ds(self.r0 + piece * q, q), :]
        return _remote(blk, blk, self.relay_sems[0].at[piece], self.relay_sems[1].at[piece], 2 * k)

    def neighbours_landed(self, relay_send_sems, relay_recv_sems):
        self.relay_sems = (relay_send_sems, relay_recv_sems)
        for k in (1, 2):
            self._ici(self.ci ^ k, k).wait_recv()
        self._relay(self.ci ^ 2, 0, 1).start()
        self._relay(self.ci ^ 1, 1, 2).start()
        for k in (1, 2):
            self._d2d(self.ci ^ k, self.r0, k).start()

    def sibling_landed(self, k):
        self._d2d(self.ci ^ k, self.r1, k).wait_recv()

    def diagonal_landed(self):
        for piece, k in ((0, 1), (1, 2)):
            self._relay(self.ci ^ 3, piece, k).wait_recv()
        self._d2d(self.ci ^ 3, self.r0, 3).start()
        self._d2d(self.ci ^ 3, self.r1, 3).wait_recv()

    def finish_relayed(self):
        for k in (1, 2):
            self._ici(self.ci, k).wait_send()
        self._relay(self.ci ^ 2, 0, 1).wait_send()
        self._relay(self.ci ^ 1, 1, 2).wait_send()
        for k in range(1, NCHIP):
            self._d2d(self.ci ^ k, self.r0, k).wait_send()

    def forward(self):
        for k in range(1, NCHIP):
            self._ici(self.ci ^ k, k).wait_recv()
            self._d2d(self.ci ^ k, self.r0, k).start()

    def finish(self):
        for k in range(1, NCHIP):
            self._d2d(self.ci ^ k, self.r1, k).wait_recv()
        self.finish_sends()

    def finish_sends(self):
        for k in range(1, NCHIP):
            self._ici(self.ci, k).wait_send()
            self._d2d(self.ci ^ k, self.r0, k).wait_send()


class _SmallGather:
    @staticmethod
    def sems(n):
        return [pltpu.SemaphoreType.DMA((n, 7)), pltpu.SemaphoreType.DMA((n, 7)), pltpu.SemaphoreType.DMA((n,))]

    def __init__(self, srcs, outs, send_sems, recv_sems, local_sems):
        x, y, c = _me()
        self.srcs, self.outs = list(srcs), list(outs)
        self.ss, self.rs, self.ls = send_sems, recv_sems, local_sems
        self.ci, self.c = 2 * x + y, c
        self.me = 2 * self.ci + c

    def _own(self, a, slot, rel):
        return _remote(self.srcs[a], self.outs[a].at[self.me], self.ss.at[a, slot], self.rs.at[a, slot], rel)

    def _block(self, a, idx, slot, rel):
        blk = self.outs[a].at[idx]
        return _remote(blk, blk, self.ss.at[a, slot], self.rs.at[a, slot], rel)

    def _local(self, a):
        return pltpu.make_async_copy(self.srcs[a], self.outs[a].at[self.me], self.ls.at[a])

    def start(self):
        for a in range(len(self.srcs)):
            self._local(a).start()
            self._own(a, 0, 1).start()
            for k in range(1, NCHIP):
                self._own(a, k, 2 * k).start()

    def forward(self):
        for a in range(len(self.srcs)):
            for k in range(1, NCHIP):
                idx = 2 * (self.ci ^ k) + self.c
                self._block(a, idx, k, 2 * k).wait_recv()
                self._block(a, idx, 3 + k, 1).start()

    def finish(self):
        for a in range(len(self.srcs)):
            self._block(a, 2 * self.ci + 1 - self.c, 0, 1).wait_recv()
            for k in range(1, NCHIP):
                self._block(a, 2 * (self.ci ^ k) + 1 - self.c, 3 + k, 1).wait_recv()
            self._own(a, 0, 1).wait_send()
            for k in range(1, NCHIP):
                self._own(a, k, 2 * k).wait_send()
                self._block(a, 2 * (self.ci ^ k) + self.c, 3 + k, 1).wait_send()
            self._local(a).wait()


def _start_in_proj(crow, w_ada, b_cols, w_in, pos, x, norm_pre, order):
    ts = 512
    nt = S // ts
    wc = crow.shape[1]

    def body(order_ref, crow_ref, wada_ref, b_ref, win_ref, pos_ref, freq_ref, x_ref, np_ref,
             g0_ref, mod_ref, wbf_ref, cos_ref, sin_ref, proj_ref, ht_ref,
             g0s, modp, modb, wbuf, hb_all, cs, cr, ms, mr, ws, wr, fs, fr, local_sems, ys, yr, osem):
        s, t = pl.program_id(0), pl.program_id(1)
        x, y, c = _me()
        ci = 2 * x + y
        me = 2 * ci + c
        wg = _WeightGather(win_ref, wbuf, ws, wr, fs, fr)

        @pl.when(jnp.logical_and(s == 0, t == 0))
        def _():
            wg.start(diagonal=False)
            mine = pltpu.make_async_copy(crow_ref, g0s.at[pl.ds(me, 1), :], local_sems.at[0])
            mine.start()
            csend = [_remote(crow_ref, g0s.at[pl.ds(me, 1), :], cs.at[r - 1], cr.at[r - 1], r) for r in range(1, NDEV)]
            for cp in csend:
                cp.start()
            cos_ref[...], sin_ref[...] = _cos_sin(pos_ref, freq_ref)
            for r in range(1, NDEV):
                px, py, pc = _peer(r)
                _remote(crow_ref, g0s.at[pl.ds(4 * px + 2 * py + pc, 1), :], cs.at[r - 1], cr.at[r - 1], r).wait_recv()
            mine.wait()
            cv = g0s[:, 0:D]
            sc = cv * _sigmoid(cv)
            scb = jnp.concatenate([sc, jnp.zeros_like(sc)], axis=0).astype(BF16)
            modp[...] = _dot(scb, wada_ref[...].astype(BF16))[0:NDEV, :] + b_ref[...]
            own = pltpu.make_async_copy(modp.at[pl.ds(me, 1), :], modb.at[ci], local_sems.at[1])
            own.start()
            msend = []
            for k in range(1, NCHIP):
                cp = _remote(modp.at[pl.ds(2 * (ci ^ k) + c, 1), :], modb.at[ci], ms.at[k - 1], mr.at[k - 1], 2 * k)
                cp.start()
                msend.append(cp)
            for k in range(1, NCHIP):
                _remote(modp.at[pl.ds(me, 1), :], modb.at[ci ^ k], ms.at[k - 1], mr.at[k - 1], 2 * k).wait_recv()
            own.wait()
            for j in range(NCHIP):
                mod_ref[:, j * EC:(j + 1) * EC] = modb[j]
            for cp in csend + msend:
                cp.wait_send()
            g0_ref[...] = g0s[...]

        def keep(k):
            return pltpu.make_async_copy(wbuf.at[ci ^ k], wbf_ref.at[ci ^ k], osem.at[k])

        @pl.when(jnp.logical_and(s == 1, t == 0))
        def _():
            keep(0).start()
            wg.neighbours_landed(ys, yr)
            wg.sibling_landed(1)
            keep(1).start()

        @pl.when(jnp.logical_and(s == 2, t == 0))
        def _():
            wg.sibling_landed(2)
            keep(2).start()

        @pl.when(jnp.logical_and(s == 3, t == 0))
        def _():
            wg.relay_sems = (ys, yr)
            wg.diagonal_landed()
            keep(3).start()

        rows = pl.ds(pl.multiple_of(t * ts, ts), ts)

        @pl.when(s == 0)
        def _():
            hp, _, _ = _rms_fwd(x_ref[...], np_ref[...])
            h = hp * (1.0 + mod_ref[:, D:2 * D]) + mod_ref[:, 0:D]
            hb_all[rows, :] = h.astype(BF16)
            ht_ref[...] = h.T.astype(BF16)

        proj_ref[...] = _dot(hb_all[rows, :], wbuf[ci ^ s])

        @pl.when(jnp.logical_and(s == NCHIP - 1, t == nt - 1))
        def _():
            wg.relay_sems = (ys, yr)
            wg.finish_relayed()
            for k in range(NCHIP):
                keep(k).wait()

    vm = pl.BlockSpec(memory_space=pltpu.VMEM)
    first_pass = lambda s, t: jnp.where(s == 0, t, nt - 1)
    grid_spec = pltpu.PrefetchScalarGridSpec(
        num_scalar_prefetch=1, grid=(NCHIP, nt),
        in_specs=[vm, vm, vm, vm, vm, vm, pl.BlockSpec((ts, D), lambda s, t, o: (first_pass(s, t), 0)),
                  pl.BlockSpec((1, D), lambda s, t, o: (0, 0))],
        out_specs=[vm, vm, pl.BlockSpec(memory_space=pl.ANY), vm, vm, pl.BlockSpec((ts, EC), lambda s, t, o: (t, o[s])),
                   pl.BlockSpec((D, ts), lambda s, t, o: (0, first_pass(s, t)))],
        scratch_shapes=[pltpu.VMEM((NDEV, wc), F32), pltpu.VMEM((NDEV, EC), F32), pltpu.VMEM((NCHIP, 1, EC), F32),
                        pltpu.VMEM((NCHIP, D, EC), BF16), pltpu.VMEM((S, D), BF16),
                        pltpu.SemaphoreType.DMA((NDEV - 1,)), pltpu.SemaphoreType.DMA((NDEV - 1,)),
                        pltpu.SemaphoreType.DMA((NCHIP - 1,)), pltpu.SemaphoreType.DMA((NCHIP - 1,))]
        + _WeightGather.SEMS + [pltpu.SemaphoreType.DMA((2,))] * 3 + [pltpu.SemaphoreType.DMA((NCHIP,))])
    return pl.pallas_call(
        body, name="start_in_proj", grid_spec=grid_spec,
        out_shape=[jax.ShapeDtypeStruct((NDEV, wc), F32), jax.ShapeDtypeStruct((1, 3 * D), F32),
                   jax.ShapeDtypeStruct((NCHIP, D, EC), BF16), jax.ShapeDtypeStruct((S, LANES), F32),
                   jax.ShapeDtypeStruct((S, LANES), F32), jax.ShapeDtypeStruct((S, E), F32),
                   jax.ShapeDtypeStruct((D, S), BF16)],
        compiler_params=_cp(("arbitrary", "arbitrary")),
    )(order, crow, w_ada, b_cols, w_in, pos, _rope_freq(), x, norm_pre)


class _ReduceScatter:
    @staticmethod
    def scratch(n_units, rows, ucols, max_owned):
        half = rows // 2
        return [pltpu.VMEM((n_units, half, ucols), F32), pltpu.VMEM((n_units, half, ucols), BF16),
                pltpu.VMEM((max_owned, NCHIP, half, ucols), BF16),
                pltpu.SemaphoreType.DMA((2,)), pltpu.SemaphoreType.DMA((n_units,)),
                pltpu.SemaphoreType.DMA((n_units, NCHIP)), pltpu.SemaphoreType.DMA((n_units,)),
                pltpu.SemaphoreType.DMA((n_units,))]

    def __init__(self, g_ref, out_ref, units, sib, stage, got, sem1, send2, recv2, send3, recv3):
        x, y, c = _me()
        self.c, self.ci = c, 2 * x + y
        self.g, self.out, self.units = g_ref, out_ref, units
        self.sib, self.stage, self.got = sib, stage, got
        self.sem1, self.send2, self.recv2, self.send3, self.recv3 = sem1, send2, recv2, send3, recv3
        self.half = g_ref.shape[1] // 2
        self.ucols = g_ref.shape[2]
        self.r0 = pl.multiple_of(c * self.half, self.half)
        self.r1 = pl.multiple_of((1 - c) * self.half, self.half)
        self.slot0 = units[0][0]
        assert [u[0] for u in units] == list(range(self.slot0, self.slot0 + len(units)))
        seen = {}
        self.local = []
        for _, owner, _ in units:
            self.local.append(seen.get(owner, 0))
            seen[owner] = seen.get(owner, 0) + 1

    def _halves(self):
        n = len(self.units)
        return _remote(self.g.at[pl.ds(self.slot0, n), pl.ds(self.r1, self.half), :], self.sib,
                       self.sem1.at[0], self.sem1.at[1], 1)

    def _partial(self, i, sender):
        _, owner, _ = self.units[i]
        return pltpu.make_async_remote_copy(
            src_ref=self.stage.at[i], dst_ref=self.got.at[self.local[i], sender],
            send_sem=self.send2.at[i], recv_sem=self.recv2.at[i, sender],
            device_id=(owner // 2, owner % 2, self.c), device_id_type=MESH)

    def _back(self, i, start):
        off = self.units[i][2]
        blk = self.out.at[pl.ds(start, self.half), off:off + self.ucols]
        return _remote(blk, blk, self.send3.at[i], self.recv3.at[i], 1)

    def start_halves(self):
        self._halves().start()

    def send_partials(self):
        self._halves().wait_recv()
        for i, (slot, owner, _) in enumerate(self.units):
            @pl.when(self.ci != owner)
            def _():
                self.stage[i] = (self.g[slot, pl.ds(self.r0, self.half), :] + self.sib[i]).astype(BF16)
                self._partial(i, self.ci).start()

    def reduce_owned(self):
        for i, (slot, owner, off) in enumerate(self.units):
            @pl.when(self.ci == owner)
            def _():
                rows, cols = pl.ds(self.r0, self.half), slice(off, off + self.ucols)
                self.out[rows, cols] = self.g[slot, pl.ds(self.r0, self.half), :] + self.sib[i]
                for s in range(NCHIP):
                    if s != owner:
                        self._partial(i, s).wait_recv()
                        self.out[rows, cols] += self.got[self.local[i], s].astype(F32)
                self._back(i, self.r0).start()

    def finish(self):
        self._halves().wait_send()
        for i, (_, owner, _) in enumerate(self.units):
            @pl.when(self.ci == owner)
            def _():
                self._back(i, self.r1).wait_recv()
                self._back(i, self.r0).wait_send()

            @pl.when(self.ci != owner)
            def _():
                self._partial(i, self.ci).wait_send()


def _silu_rows(c_ref):
    cv = c_ref[...]
    sc = cv * _sigmoid(cv)
    return jnp.concatenate([sc, jnp.zeros_like(sc)], axis=0).astype(BF16)


def _adamw(w, g, m, v, name):
    rows, cols = w.shape
    tr = 256 if rows % 256 == 0 else rows

    def body(w_ref, g_ref, m_ref, v_ref, d_ref, nm_ref, nv_ref):
        d_ref[...], nm_ref[...], nv_ref[...] = _adamw_values(w_ref[...], g_ref[...], m_ref[...], v_ref[...])

    spec = pl.BlockSpec((tr, cols), lambda i: (i, 0))
    return pl.pallas_call(
        body, name=name, grid=(rows // tr,), in_specs=[spec] * 4, out_specs=[spec] * 3,
        out_shape=[jax.ShapeDtypeStruct((rows, cols), F32)] * 3,
        compiler_params=_cp(("parallel",)),
    )(w, g, m, v)


def _adamw_values(w, g, m, v):
    nm = B1 * m + (1.0 - B1) * g
    nv = B2 * v + (1.0 - B2) * (g * g)
    m_hat = nm / (1.0 - B1 ** STEP)
    v_hat = nv / (1.0 - B2 ** STEP)
    return (-LR) * (m_hat / (jnp.sqrt(v_hat) + ADAM_EPS) + WD * w), nm, nv


NB = R // HEAD
SMALL = (("b_ada", (1, 3 * D)), ("norm_pre", (1, D)), ("norm_post", (1, D)), ("conv_w", (4, R // NCHIP)),
         ("conv_b", (1, R)), ("w_rg_a", (NB, HEAD, HEAD)), ("b_rg_a", (1, R)), ("w_rg_x", (NB, HEAD, HEAD)),
         ("b_rg_x", (1, R)), ("lru_lambda", (1, R)), ("norm_rec", (1, R)), ("norm_att", (1, R)))


def _small_update(ao8, sm8, dwa8, dwx8, ai8, cg, params):
    n = len(SMALL)

    def body(ao_ref, sm_ref, dwa_ref, dwx_ref, ai_ref, cg_ref, *refs):
        pin, pout, (gada_ref, loss_ref, dmod) = refs[:3 * n], refs[3 * n:7 * n], refs[7 * n:]
        xx, yy, _ = _me()
        ci = 2 * xx + yy

        def total(ref, *idx):
            acc = ref[(0,) + idx].astype(F32)
            for d in range(1, NDEV):
                acc = acc + ref[(d,) + idx].astype(F32)
            return acc

        row = lambda ref, r, lanes=slice(None): total(ref, slice(r, r + 1), lanes)
        mine = lambda parts: sum(jnp.where(ci == j, part, 0.0) for j, part in enumerate(parts))
        cw = R // NCHIP
        grads = {
            "b_ada": [jnp.concatenate([row(ai_ref, 0), row(ai_ref, 1), row(ao_ref, 0)], axis=1)],
            "norm_pre": [row(ai_ref, 2)], "norm_post": [row(ao_ref, 1)],
            "conv_w": [mine([row(sm_ref, 8 + r, slice(j * cw, (j + 1) * cw)) for j in range(NCHIP)]) for r in range(4)],
            "conv_b": [row(sm_ref, 4)], "b_rg_a": [row(sm_ref, 0)], "b_rg_x": [row(sm_ref, 1)],
            "lru_lambda": [row(sm_ref, 2)], "norm_rec": [row(sm_ref, 3)], "norm_att": [row(ao_ref, 2, slice(0, R))],
            "w_rg_a": [total(dwa_ref, h) for h in range(NB)], "w_rg_x": [total(dwx_ref, h) for h in range(NB)],
        }
        loss_ref[...] = row(ao_ref, 3, slice(0, LANES)) * (0.5 / D)
        for k, (name, shape) in enumerate(SMALL):
            w_ref, m_ref, v_ref = pin[3 * k:3 * k + 3]
            outs = pout[4 * k:4 * k + 4]
            for r, g in enumerate(grads[name]):
                at = (slice(None),) if len(grads[name]) == 1 else ((r,) if len(shape) == 3 else (slice(r, r + 1),))
                res = (g,) + _adamw_values(w_ref[at], g, m_ref[at], v_ref[at])
                for o_ref, val in zip(outs, res):
                    o_ref[at] = val
        for d in range(NDEV):
            dmod[d:d + 1, :] = jnp.concatenate([ai_ref[d, 0:1, :], ai_ref[d, 1:2, :], ao_ref[d, 0:1, :]], axis=1)
        cols = mine([dmod[:, j * EC:(j + 1) * EC] for j in range(NCHIP)])
        colsb = jnp.concatenate([cols, jnp.zeros_like(cols)], axis=0).astype(BF16)
        gada_ref[...] = _dot_tn(_silu_rows(cg_ref), colsb)

    shapes = [jax.ShapeDtypeStruct(s, F32) for _, s in SMALL]
    outs = pl.pallas_call(
        body, name="small_update",
        out_shape=[s for s in shapes for _ in range(4)] + [jax.ShapeDtypeStruct((D, EC), F32),
                                                           jax.ShapeDtypeStruct((1, LANES), F32)],
        scratch_shapes=[pltpu.VMEM((NDEV, 3 * D), F32)],
        compiler_params=_cp(),
    )(ao8, sm8, dwa8, dwx8, ai8, cg, *params)
    return outs[:4 * n], outs[4 * n], outs[4 * n + 1]


BIG = ("w_ada", "w_in", "w_out")
WEIGHTS = ("w_ada", "b_ada", "norm_pre", "norm_post", "w_in", "conv_w", "conv_b", "w_rg_a", "b_rg_a", "w_rg_x",
           "b_rg_x", "lru_lambda", "norm_rec", "norm_att", "w_out")


def kernel(x, c, positions, w_ada, b_ada, norm_pre, norm_post, w_in, conv_w, conv_b, w_rg_a, b_rg_a, w_rg_x, b_rg_x, lru_lambda, norm_rec, norm_att, w_out, loss_target, m_w_ada, m_b_ada, m_norm_pre, m_norm_post, m_w_in, m_conv_w, m_conv_b, m_w_rg_a, m_b_rg_a, m_w_rg_x, m_b_rg_x, m_lru_lambda, m_norm_rec, m_norm_att, m_w_out, v_w_ada, v_b_ada, v_norm_pre, v_norm_post, v_w_in, v_conv_w, v_conv_b, v_w_rg_a, v_b_rg_a, v_w_rg_x, v_b_rg_x, v_lru_lambda, v_norm_rec, v_norm_att, v_w_out):
    given = dict(locals())
    wts = {n: given[n] for n in WEIGHTS}
    ms = {n: given["m_" + n] for n in WEIGHTS}
    vs = {n: given["v_" + n] for n in WEIGHTS}
    xi, yi, _ = _me()
    chip = 2 * xi + yi
    cw_loc = R // NCHIP

    b_cols = lax.dynamic_slice(b_ada, (0, chip * EC), (1, EC))
    order = (chip ^ jnp.arange(NCHIP, dtype=jnp.int32)).astype(jnp.int32)
    g0, mod, w_in_bf, cos, sin, proj, ht = _start_in_proj(
        jnp.concatenate([c, conv_w.reshape(1, 4 * cw_loc)], axis=1), w_ada[0], b_cols, w_in[0],
        positions.reshape(S, 1), x[0], norm_pre, order)
    cg = g0[:, 0:D]
    conv_full = g0[0::2, D:].reshape(NCHIP, 4, cw_loc).transpose(1, 0, 2).reshape(4, R)

    p = dict(norm_pre=norm_pre, norm_post=norm_post, conv_b=conv_b, b_rg_a=b_rg_a, b_rg_x=b_rg_x,
             lru_lambda=lru_lambda, norm_rec=norm_rec, norm_att=norm_att, w_rg_a=w_rg_a[0], w_rg_x=w_rg_x[0])
    grad_x, g_in, g_out, gathered = _local_step(
        x[0], cos, sin, loss_target[0], mod, w_in_bf, proj, ht, w_out[0], conv_full, p)

    params = [d[n].reshape(shape) for n, shape in SMALL for d in (wts, ms, vs)]
    small_out, g_ada, loss_row = _small_update(*gathered, cg, params)
    grads = {"w_out": g_out, "w_in": g_in, "w_ada": g_ada}
    delta, new_m, new_v = {}, {}, {}
    for k, (n, _) in enumerate(SMALL):
        grads[n], delta[n], new_m[n], new_v[n] = small_out[4 * k:4 * k + 4]
    for n in BIG:
        delta[n], new_m[n], new_v[n] = _adamw(wts[n][0], grads[n], ms[n][0], vs[n][0], "adamw_" + n)
    out = lambda d: [d[n].reshape(wts[n].shape) for n in WEIGHTS]
    return (loss_row[0, 0], grad_x.reshape(x.shape), *out(grads), *out(delta), *out(new_m), *out(new_v))
```

```python
import numpy as np
import jax
import jax.numpy as jnp
from jax import lax
from jax.experimental import pallas as pl
from jax.experimental.pallas import tpu as pltpu

F32 = jnp.float32
BF16 = jnp.bfloat16

S = 2048
D = 1024
E = 3072
R = 512
NDEV = 8
NCHIP = 4
EC = 768
LRU_C = 8.0
EPS = 1e-6
NEG = -1e30
HEAD = 64
BLK = 128
PATTERNS = (1, 4, 16)
ROPE_THETA = 10000.0
LANES = 128
VMEM_LIMIT = 56 * 1024 * 1024

B1, B2, LR, WD, ADAM_EPS, STEP = 0.9, 0.999, 0.001, 0.01, 1e-8, 10
MESH = pl.DeviceIdType.MESH


def _cp(sem=None, **kw):
    return pltpu.CompilerParams(dimension_semantics=sem, vmem_limit_bytes=VMEM_LIMIT, **kw)


def _dot(a, b):
    return jnp.dot(a, b, preferred_element_type=F32)


def _dot_nt(a, b):
    return lax.dot_general(a, b, (((1,), (1,)), ((), ())), preferred_element_type=F32)


def _dot_tn(a, b):
    return lax.dot_general(a, b, (((0,), (0,)), ((), ())), preferred_element_type=F32)


def _sigmoid(x):
    return 1.0 / (1.0 + jnp.exp(-x))


def _one_minus_exp(x, ex):
    poly = -x * (1.0 + x * (0.5 + x * (1.0 / 6 + x * (1.0 / 24))))
    return jnp.where(x > -1.0 / 16, poly, 1.0 - ex)


def _rms_fwd(v, g):
    rstd = lax.rsqrt(jnp.mean(v * v, axis=-1, keepdims=True) + EPS)
    vn = v * rstd
    return vn * g, vn, rstd


def _rms_bwd(dy, vn, rstd, g):
    dvn = dy * g
    dv = rstd * (dvn - vn * jnp.mean(dvn * vn, axis=-1, keepdims=True))
    return dv, jnp.sum(dy * vn, axis=0, keepdims=True)


RT = 256


def _shift_down(cur, prev8, j, row):
    if j == 0:
        return cur
    rolled = pltpu.roll(cur, j, 0)
    top = jnp.where(row[0:8] >= j, rolled[0:8], pltpu.roll(prev8, j, 0))
    return jnp.concatenate([top, rolled[8:]], axis=0)


def _shift_up(cur, next8, j, row):
    if j == 0:
        return cur
    rolled = pltpu.roll(cur, RT - j, 0)
    bot = jnp.where(row[RT - 8:] < RT - j, rolled[RT - 8:], pltpu.roll(next8, 8 - j, 0))
    return jnp.concatenate([rolled[:RT - 8], bot], axis=0)


def _rec_gates(xp, xprev8, row, cw_ref, cb_ref, wa_ref, ba_ref, wx_ref, bx_ref, lam_ref):
    xa = cb_ref[...] + sum(cw_ref[3 - j:4 - j, :] * _shift_down(xp, xprev8, j, row) for j in range(4))
    xab = xa.astype(BF16)
    r = _sigmoid(_dot(xab, wa_ref[...]) + ba_ref[...])
    ig = _sigmoid(_dot(xab, wx_ref[...]) + bx_ref[...])
    nl = -lam_ref[...]
    sp = jnp.maximum(nl, 0.0) + jnp.log1p(jnp.exp(-jnp.abs(nl)))
    la = (-LRU_C) * r * sp
    a = jnp.exp(la)
    mult = jnp.sqrt(_one_minus_exp(2.0 * la, a * a))
    return dict(xa=xa, xab=xab, r=r, ig=ig, sp=sp, la=la, a=a, mult=mult)


def _scan_fwd(a, u, row):
    sh = 1
    while sh < RT:
        a_s = jnp.where(row >= sh, pltpu.roll(a, sh, 0), 1.0)
        u_s = jnp.where(row >= sh, pltpu.roll(u, sh, 0), 0.0)
        u = a * u_s + u
        a = a * a_s
        sh *= 2
    return a, u


def _scan_bwd(al, g, row):
    sh = 1
    while sh < RT:
        al_s = jnp.where(row < RT - sh, pltpu.roll(al, RT - sh, 0), 1.0)
        g_s = jnp.where(row < RT - sh, pltpu.roll(g, RT - sh, 0), 0.0)
        g = g + al * g_s
        al = al * al_s
        sh *= 2
    return g


def _dense_from_blocks(blocks_ref, dense_ref):
    dense_ref[...] = jnp.zeros_like(dense_ref)
    for h in range(R // HEAD):
        dense_ref[h * HEAD:(h + 1) * HEAD, h * HEAD:(h + 1) * HEAD] = blocks_ref[h].astype(dense_ref.dtype)


def _rec_fwd(proj, conv_w, conv_b, wa_b, ba, wx_b, bx, lam, norm_rec):
    nt = S // RT

    def body(p_ref, cw_ref, cb_ref, wa_ref, ba_ref, wx_ref, bx_ref, lam_ref, nr_ref,
             h_ref, ya_ref, prev8, hc, wad, wxd):
        i = pl.program_id(0)

        @pl.when(i == 0)
        def _():
            prev8[...] = jnp.zeros_like(prev8)
            hc[...] = jnp.zeros_like(hc)
            _dense_from_blocks(wa_ref, wad)
            _dense_from_blocks(wx_ref, wxd)

        row = lax.broadcasted_iota(jnp.int32, (RT, R), 0)
        xp = p_ref[:, 0:R]
        ga = p_ref[:, R:2 * R]
        f = _rec_gates(xp, prev8[...], row, cw_ref, cb_ref, wad, ba_ref, wxd, bx_ref, lam_ref)
        u = f["mult"] * (f["ig"] * f["xa"])
        acum, hh = _scan_fwd(f["a"], u, row)
        h = hh + acum * hc[0:1, :]
        h_ref[...] = h
        hc[0:1, :] = h_ref[RT - 1:RT, :]
        prev8[...] = p_ref[RT - 8:RT, 0:R]
        yp = h * (ga * _sigmoid(ga))
        ya, _, _ = _rms_fwd(yp, nr_ref[...])
        ya_ref[...] = ya.astype(BF16)

    row1 = lambda n: pl.BlockSpec((1, n), lambda i: (0, 0))
    blocks = pl.BlockSpec((R // HEAD, HEAD, HEAD), lambda i: (0, 0, 0))
    return pl.pallas_call(
        body, name="rec_fwd", grid=(nt,),
        in_specs=[pl.BlockSpec((RT, 2 * R), lambda i: (i, 0)), pl.BlockSpec((4, R), lambda i: (0, 0)), row1(R),
                  blocks, row1(R), blocks, row1(R), row1(R), row1(R)],
        out_specs=[pl.BlockSpec((RT, R), lambda i: (i, 0)), pl.BlockSpec((RT, R), lambda i: (i, 0))],
        out_shape=[jax.ShapeDtypeStruct((S, R), F32), jax.ShapeDtypeStruct((S, R), BF16)],
        scratch_shapes=[pltpu.VMEM((8, R), F32), pltpu.VMEM((8, R), F32), pltpu.VMEM((R, R), BF16),
                        pltpu.VMEM((R, R), BF16)],
        compiler_params=_cp(("arbitrary",)),
    )(proj, conv_w, conv_b, wa_b, ba, wx_b, bx, lam, norm_rec)


def _rec_bwd(dproj, d_ya, proj, h_all, conv_w, conv_b, wa_b, ba, wx_b, bx, lam, norm_rec):
    nt = S // RT

    def body(dp_in, dya_ref, p_ref, pprev_ref, h_ref, hprev_ref, cw_ref, cb_ref, wab_ref, ba_ref, wxb_ref, bx_ref,
             lam_ref, nr_ref, dp_ref, dwab_ref, dwxb_ref, sm_ref, nxt8, cg, wa_ref, wx_ref, dwa_ref, dwx_ref):
        i = pl.program_id(0)
        ti = nt - 1 - i

        @pl.when(i == 0)
        def _():
            nxt8[...] = jnp.zeros_like(nxt8)
            cg[...] = jnp.zeros_like(cg)
            dwa_ref[...] = jnp.zeros_like(dwa_ref)
            dwx_ref[...] = jnp.zeros_like(dwx_ref)
            sm_ref[...] = jnp.zeros_like(sm_ref)
            _dense_from_blocks(wab_ref, wa_ref)
            _dense_from_blocks(wxb_ref, wx_ref)

        row = lax.broadcasted_iota(jnp.int32, (RT, R), 0)
        first = (ti > 0).astype(F32)
        xprev8 = pprev_ref[...] * first
        hprev8 = hprev_ref[...] * first
        xp = p_ref[:, 0:R]
        ga = p_ref[:, R:2 * R]
        f = _rec_gates(xp, xprev8, row, cw_ref, cb_ref, wa_ref, ba_ref, wx_ref, bx_ref, lam_ref)
        xa, r, ig, a, mult = f["xa"], f["r"], f["ig"], f["a"], f["mult"]
        h = h_ref[...]
        sg = _sigmoid(ga)
        gate = ga * sg
        yp = h * gate
        _, ypn, rstd = _rms_fwd(yp, nr_ref[...])
        d_yp, dnr = _rms_bwd(dya_ref[...], ypn, rstd, nr_ref[...])
        d_ga = d_yp * h * (sg * (1.0 + ga * (1.0 - sg)))
        dh = d_yp * gate + jnp.where(row == RT - 1, cg[0:1, :], 0.0)
        al = jnp.where(row < RT - 1, pltpu.roll(a, RT - 1, 0), 0.0)
        g = _scan_bwd(al, dh, row)
        cg[0:1, :] = jnp.sum(jnp.where(row == 0, a * g, 0.0), axis=0, keepdims=True)
        h_m1 = _shift_down(h, hprev8, 1, row)
        da = g * h_m1
        ix = ig * xa
        d_mult = g * ix
        d_ig = g * mult * xa
        d_xa = g * mult * ig
        d_la = da * a - d_mult * (a * a) / mult
        d_r = d_la * ((-LRU_C) * f["sp"])
        dsp = jnp.sum(d_la * ((-LRU_C) * r), axis=0, keepdims=True)
        dlam = dsp * (-_sigmoid(-lam_ref[...]))
        d_za = d_r * r * (1.0 - r)
        d_zx = d_ig * ig * (1.0 - ig)
        dzab = d_za.astype(BF16)
        dzxb = d_zx.astype(BF16)
        dwa_ref[...] += _dot_tn(f["xab"], dzab)
        dwx_ref[...] += _dot_tn(f["xab"], dzxb)
        d_xa = d_xa + _dot_nt(dzab, wa_ref[...]) + _dot_nt(dzxb, wx_ref[...])
        d_xp = sum(cw_ref[3 - j:4 - j, :] * _shift_up(d_xa, nxt8[...], j, row) for j in range(4))
        dcw = [jnp.sum(d_xa * _shift_down(xp, xprev8, 3 - k, row), axis=0, keepdims=True) for k in range(4)]
        dp_ref[:, 0:R] = d_xp.astype(BF16)
        dp_ref[:, R:2 * R] = d_ga.astype(BF16)
        dp8 = d_xa[0:8, :]
        nxt8[...] = dp8
        sm_ref[0:1, :] += jnp.sum(d_za, axis=0, keepdims=True)
        sm_ref[1:2, :] += jnp.sum(d_zx, axis=0, keepdims=True)
        sm_ref[2:3, :] += dlam
        sm_ref[3:4, :] += dnr
        sm_ref[4:5, :] += jnp.sum(d_xa, axis=0, keepdims=True)
        for k in range(4):
            sm_ref[8 + k:9 + k, :] += dcw[k]

        @pl.when(i == nt - 1)
        def _():
            for h in range(R // HEAD):
                dwab_ref[h] = dwa_ref[h * HEAD:(h + 1) * HEAD, h * HEAD:(h + 1) * HEAD].astype(BF16)
                dwxb_ref[h] = dwx_ref[h * HEAD:(h + 1) * HEAD, h * HEAD:(h + 1) * HEAD].astype(BF16)

    c0 = lambda shape: pl.BlockSpec(shape, lambda i: (0, 0))
    blocks = pl.BlockSpec((R // HEAD, HEAD, HEAD), lambda i: (0, 0, 0))
    rev = lambda i: nt - 1 - i
    prev8 = lambda i: (jnp.maximum((nt - 1 - i) * (RT // 8) - 1, 0), 0)
    return pl.pallas_call(
        body, name="rec_bwd", grid=(nt,),
        in_specs=[pl.BlockSpec(memory_space=pl.ANY),
                  pl.BlockSpec((RT, R), lambda i: (rev(i), 0)),
                  pl.BlockSpec((RT, 2 * R), lambda i: (rev(i), 0)), pl.BlockSpec((8, R), prev8),
                  pl.BlockSpec((RT, R), lambda i: (rev(i), 0)), pl.BlockSpec((8, R), prev8),
                  c0((4, R)), c0((1, R)), blocks, c0((1, R)), blocks, c0((1, R)), c0((1, R)), c0((1, R))],
        out_specs=[pl.BlockSpec((RT, 2 * R), lambda i: (rev(i), 0)), blocks, blocks, c0((16, R))],
        out_shape=[jax.ShapeDtypeStruct((S, E), BF16), jax.ShapeDtypeStruct((R // HEAD, HEAD, HEAD), BF16),
                   jax.ShapeDtypeStruct((R // HEAD, HEAD, HEAD), BF16), jax.ShapeDtypeStruct((16, R), F32)],
        scratch_shapes=[pltpu.VMEM((8, R), F32), pltpu.VMEM((8, R), F32), pltpu.VMEM((R, R), BF16),
                        pltpu.VMEM((R, R), BF16), pltpu.VMEM((R, R), F32), pltpu.VMEM((R, R), F32)],
        input_output_aliases={0: 0},
        compiler_params=_cp(("arbitrary",)),
    )(dproj, d_ya, proj, proj, h_all, h_all, conv_w, conv_b, wa_b, ba, wx_b, bx, lam, norm_rec)


NPAIR = R // LANES
QB, KB, VB, GB = 2 * R // LANES, 3 * R // LANES, 4 * R // LANES, 5 * R // LANES


def _rope_freq():
    half = HEAD // 2
    inv = np.float32(ROPE_THETA) ** (-(np.arange(half, dtype=np.float32) / np.float32(half)))
    return jnp.asarray(np.tile(inv.astype(np.float32), LANES // half)[None, :])


def _rot_half(x, first):
    return jnp.where(first, -pltpu.roll(x, LANES - HEAD // 2, 1), pltpu.roll(x, HEAD // 2, 1))


def _cos_sin(pos_ref, freq_ref):
    ang = pos_ref[...].astype(F32) * freq_ref[...]
    return jnp.cos(ang), jnp.sin(ang)


SUB = 4


def _stages(d):
    assert d in (1, SUB, SUB * SUB)
    return d > SUB


def _strided_rows(src_ref, d, tmp):
    n = S // d
    if not _stages(d):
        for r in range(d):
            yield r * n, (src_ref[pl.ds(r, n, stride=d), :] if d > 1 else src_ref[...])
        return
    m = S // SUB
    for r in range(SUB):
        tmp[r * m:(r + 1) * m, :] = src_ref[pl.ds(r, m, stride=SUB), :]
    for r in range(SUB):
        for q in range(SUB):
            yield (r + SUB * q) * n, tmp[pl.ds(r * m + q, n, stride=SUB), :]


def _deint(src_ref, dst_ref, d, tmp):
    n = S // d
    for row0, v in _strided_rows(src_ref, d, tmp):
        dst_ref[row0:row0 + n, :] = v.astype(dst_ref.dtype)


def _reint(src_ref, dst_ref, d, accumulate, tmp):
    if _stages(d):
        n, m = S // d, S // SUB
        for r in range(SUB):
            for q in range(SUB):
                tmp[pl.ds(r * m + q, n, stride=SUB), :] = src_ref[(r + SUB * q) * n:(r + SUB * q + 1) * n, :]
        src_ref, d = tmp, SUB
    n = S // d
    for r in range(d):
        idx = (pl.ds(r, n, stride=d), slice(None)) if d > 1 else (slice(None), slice(None))
        v = src_ref[r * n:(r + 1) * n, :]
        if accumulate:
            dst_ref[idx] = dst_ref[idx] + v
        else:
            dst_ref[idx] = v


def _deint_heads(src_ref, dst0, dst1, d, tmp):
    n = S // d
    hm0 = lax.broadcasted_iota(jnp.int32, (n, LANES), 1) < HEAD
    for row0, v in _strided_rows(src_ref, d, tmp):
        dst0[row0:row0 + n, :] = jnp.where(hm0, v, 0.0).astype(BF16)
        dst1[row0:row0 + n, :] = jnp.where(hm0, 0.0, v).astype(BF16)


def _reint_prev(src_ref, dst_ref, d):
    n = S // d
    if n == BLK:
        return
    for r in range(d):
        idx = (pl.ds(r, n - BLK, stride=d), slice(None)) if d > 1 else (slice(0, n - BLK), slice(None))
        dst_ref[idx] = dst_ref[idx] + src_ref[r * n + BLK:(r + 1) * n, :]


def _pair_masks():
    qi = lax.broadcasted_iota(jnp.int32, (BLK, 2 * BLK), 0)
    ki = lax.broadcasted_iota(jnp.int32, (BLK, 2 * BLK), 1) & (BLK - 1)
    return ki <= qi, ki >= qi


def _two(ref0, ref1, st, axis):
    return jnp.concatenate([ref0[pl.ds(st, BLK), :], ref1[pl.ds(st, BLK), :]], axis=axis)


ATT_UNROLL = 8


def _att_fwd(proj, cos, sin, w_out):
    def body(q_ref, k_ref, v_ref, cos_ref, sin_ref, w_ref, att_ref, qr_ref, kr_ref, lse_ref, wbf_ref,
             qd, kd0, kd1, vd0, vd1, od, ld, tmp, on, ln, wbuf, *wsems):
        wg = _WeightGather(w_ref, wbuf, *wsems)
        pl.when(pl.program_id(0) == 0)(wg.start)
        pl.when(pl.program_id(0) == 1)(wg.forward)
        lane = lax.broadcasted_iota(jnp.int32, (S, LANES), 1)
        first = (lane & (HEAD // 2)) == 0
        cos, sin = cos_ref[...], sin_ref[...]
        q = q_ref[...]
        k = k_ref[...]
        qr_ref[...] = (q * cos + _rot_half(q, first) * sin) * (HEAD ** -0.5)
        kr_ref[...] = k * cos + _rot_half(k, first) * sin
        hm0 = lax.broadcasted_iota(jnp.int32, (BLK, LANES), 1) < HEAD
        top = lax.broadcasted_iota(jnp.int32, (2 * BLK, LANES), 0) < BLK
        ones2 = (top == (lax.broadcasted_iota(jnp.int32, (2 * BLK, LANES), 1) < HEAD)).astype(BF16)
        mc2, mp2 = _pair_masks()

        for pi, d in enumerate(PATTERNS):
            nb = S // d // BLK
            _deint(qr_ref, qd, d, tmp)
            _deint_heads(kr_ref, kd0, kd1, d, tmp)
            _deint_heads(v_ref, vd0, vd1, d, tmp)

            def blk(b, carry):
                st = pl.multiple_of(b * BLK, BLK)
                qb = qd[pl.ds(st, BLK), :]
                sc = jnp.where(mc2, _dot_nt(qb, _two(kd0, kd1, st, 0)), NEG)
                mx = sc
                if nb > 1:
                    stp = pl.multiple_of(jnp.maximum(b - 1, 0) * BLK, BLK)
                    mp = jnp.logical_and(mp2, lax.rem(b, nb) != 0)
                    sp = jnp.where(mp, _dot_nt(qb, _two(kd0, kd1, stp, 0)), NEG)
                    mx = jnp.maximum(sc, sp)
                m0 = jnp.max(mx[:, 0:BLK], axis=1, keepdims=True)
                m1 = jnp.max(mx[:, BLK:2 * BLK], axis=1, keepdims=True)
                mf = jnp.concatenate([jnp.broadcast_to(m0, (BLK, BLK)), jnp.broadcast_to(m1, (BLK, BLK))], axis=1)
                o = _dot(jnp.exp(sc - mf).astype(BF16), jnp.concatenate([_two(vd0, vd1, st, 0), ones2], axis=1))
                if nb > 1:
                    o = o + _dot(jnp.exp(sp - mf).astype(BF16), jnp.concatenate([_two(vd0, vd1, stp, 0), ones2], axis=1))
                l = o[:, LANES:2 * LANES]
                od[pl.ds(st, BLK), :] = o[:, 0:LANES] / l
                ld[pl.ds(st, BLK), :] = jnp.where(hm0, m0, m1) + jnp.log(l)
                return carry

            lax.fori_loop(0, S // BLK, blk, 0, unroll=ATT_UNROLL)
            _reint(od, on.at[pi], d, False, tmp)
            _reint(ld, ln.at[pi], d, False, tmp)

        l0, l1, l2 = ln[0], ln[1], ln[2]
        m = jnp.maximum(jnp.maximum(l0, l1), l2)
        e0, e1, e2 = jnp.exp(l0 - m), jnp.exp(l1 - m), jnp.exp(l2 - m)
        den = e0 + e1 + e2
        att_ref[...] = (e0 * on[0] + e1 * on[1] + e2 * on[2]) / den
        lse_ref[...] = m + jnp.log(den)

        @pl.when(pl.program_id(0) == NPAIR - 1)
        def _():
            wg.finish()
            wbf_ref[...] = wbuf[...]

    col = lambda c0: pl.BlockSpec((S, LANES), lambda p: (0, c0 + p))
    out = pl.BlockSpec((S, LANES), lambda p: (0, p))
    tab = pl.BlockSpec((S, LANES), lambda p: (0, 0))
    vm = pl.BlockSpec(memory_space=pltpu.VMEM)
    return pl.pallas_call(
        body, name="att_fwd", grid=(NPAIR,),
        in_specs=[col(QB), col(KB), col(VB), tab, tab, vm],
        out_specs=[out, out, out, out, vm],
        out_shape=[jax.ShapeDtypeStruct((S, R), F32)] * 4 + [jax.ShapeDtypeStruct((NCHIP,) + w_out.shape, BF16)],
        scratch_shapes=[pltpu.VMEM((S, LANES), BF16)] * 5 + [pltpu.VMEM((S, LANES), F32)] * 3
        + [pltpu.VMEM((3, S, LANES), F32)] * 2 + [pltpu.VMEM((NCHIP,) + w_out.shape, BF16)] + _WeightGather.SEMS,
        compiler_params=_cp(("arbitrary",)),
    )(proj, proj, proj, cos, sin, w_out)


def _att_bwd(dproj, d_att, att, lse, qr, kr, proj, cos, sin, gw_out4):
    out_units = [(j, j, 0) for j in range(NCHIP)]

    nblk = S // BLK

    def body(dp_in, do_ref, o_ref, lse_ref, qr_ref, kr_ref, v_ref, cos_ref, sin_ref, gw_ref, dp_ref, gout_ref,
             qd, kd0, kd1, vd0, vd1, dod, kt, packn, packd, dqd, dkcd, dkpd, dvcd, dvpd,
             dqn, dkn, dvn, tmp, rows, trs, pts, dss, stage, sems, gred, *rs_scratch):
        p = pl.program_id(0)
        rs = _ReduceScatter(gw_ref, gred, out_units, *rs_scratch)
        for step, piece in enumerate((rs.start_halves, rs.send_partials, rs.reduce_owned)):
            pl.when(p == step)(piece)

        @pl.when(p == NPAIR - 1)
        def _():
            rs.finish()
            gout_ref[...] = gred[...]

        lane = lax.broadcasted_iota(jnp.int32, (S, LANES), 1)
        hms = lane < HEAD
        prod = do_ref[...] * o_ref[...]
        d0 = jnp.sum(jnp.where(hms, prod, 0.0), axis=1, keepdims=True)
        d1 = jnp.sum(jnp.where(hms, 0.0, prod), axis=1, keepdims=True)
        lse = lse_ref[...]
        quarter = HEAD // 2
        packn[...] = jnp.where(lane < quarter, lse,
                               jnp.where(hms, pltpu.roll(lse, LANES - quarter, 1), jnp.where(lane < 3 * quarter, d0, d1)))
        dqn[...] = jnp.zeros_like(dqn)
        dkn[...] = jnp.zeros_like(dkn)
        dvn[...] = jnp.zeros_like(dvn)
        hm0 = lax.broadcasted_iota(jnp.int32, (BLK, LANES), 1) < HEAD
        key = lax.broadcasted_iota(jnp.int32, (2 * BLK, BLK), 0) & (BLK - 1)
        qry = lax.broadcasted_iota(jnp.int32, (2 * BLK, BLK), 1)
        mct, mpt = key <= qry, key >= qry

        for d in PATTERNS:
            nb = S // d // BLK
            _deint(qr_ref, qd, d, tmp)
            _deint_heads(kr_ref, kd0, kd1, d, tmp)
            _deint_heads(v_ref, vd0, vd1, d, tmp)
            _deint(do_ref, dod, d, tmp)
            _deint(packn, packd, d, tmp)

            sides = (0, 1) if nb > 1 else (0,)

            def probs(b, carry):
                st = pl.multiple_of(b * BLK, BLK)
                kt[b] = _two(kd0, kd1, st, 0).astype(F32).T.astype(BF16)
                trs[b] = packd[pl.ds(st, BLK), :].T
                for j in range(4):
                    rows[b, j:j + 1, :] = trs[b, j * quarter:j * quarter + 1, :]
                qb, dob = qd[pl.ds(st, BLK), :], dod[pl.ds(st, BLK), :]
                both = lambda j: jnp.concatenate([jnp.broadcast_to(rows[b, j:j + 1, :], (BLK, BLK)),
                                                  jnp.broadcast_to(rows[b, j + 1:j + 2, :], (BLK, BLK))], axis=0)
                lbt, dlt = both(0), both(2)
                for sd in sides:
                    stk = pl.multiple_of(jnp.maximum(b - sd, 0) * BLK, BLK)
                    mask = mct if sd == 0 else jnp.logical_and(mpt, lax.rem(b, nb) != 0)
                    k2, v2 = _two(kd0, kd1, stk, 0), _two(vd0, vd1, stk, 0)
                    pt = jnp.where(mask, jnp.exp(_dot_nt(k2, qb) - lbt), 0.0)
                    pts[b, sd] = pt.astype(BF16)
                    dss[b, sd] = (pt * (_dot_nt(v2, dob) - dlt)).astype(BF16)
                return carry

            lax.fori_loop(0, nblk, probs, 0, unroll=ATT_UNROLL)

            def prods(b, carry):
                st = pl.multiple_of(b * BLK, BLK)
                qb, dob = qd[pl.ds(st, BLK), :], dod[pl.ds(st, BLK), :]
                dq_t = None
                for sd in sides:
                    dst, ptb = dss[b, sd], pts[b, sd]
                    rk, rv = _dot(dst, qb), _dot(ptb, dob)
                    dqs = _dot(kt[jnp.maximum(b - sd, 0)], dst)
                    dq_t = dqs if dq_t is None else dq_t + dqs
                    dk, dv = (dkcd, dvcd) if sd == 0 else (dkpd, dvpd)
                    dk[pl.ds(st, BLK), :] = jnp.where(hm0, rk[0:BLK], rk[BLK:2 * BLK])
                    dv[pl.ds(st, BLK), :] = jnp.where(hm0, rv[0:BLK], rv[BLK:2 * BLK])
                dqd[pl.ds(st, BLK), :] = dq_t.T
                return carry

            lax.fori_loop(0, nblk, prods, 0, unroll=ATT_UNROLL)
            _reint(dqd, dqn, d, True, tmp)
            _reint(dkcd, dkn, d, True, tmp)
            _reint(dvcd, dvn, d, True, tmp)
            _reint_prev(dkpd, dkn, d)
            _reint_prev(dvpd, dvn, d)

        lane = lax.broadcasted_iota(jnp.int32, (S, LANES), 1)
        first = (lane & (HEAD // 2)) == 0
        cos, sin = cos_ref[...], sin_ref[...]
        dq = dqn[...] * (HEAD ** -0.5)
        dk = dkn[...]
        stage[0] = (dq * cos - _rot_half(dq, first) * sin).astype(BF16)
        stage[1] = (dk * cos - _rot_half(dk, first) * sin).astype(BF16)
        stage[2] = dvn[...].astype(BF16)
        copies = [pltpu.make_async_copy(stage.at[j], dp_ref.at[:, pl.ds((2 + j) * R + p * LANES, LANES)], sems.at[j])
                  for j in range(3)]
        for cp in copies:
            cp.start()
        for cp in copies:
            cp.wait()

    blk = pl.BlockSpec((S, LANES), lambda p: (0, p))
    tab = pl.BlockSpec((S, LANES), lambda p: (0, 0))
    vm = pl.BlockSpec(memory_space=pltpu.VMEM)
    _, orows, ocols = gw_out4.shape
    return pl.pallas_call(
        body, name="att_bwd", grid=(NPAIR,),
        in_specs=[pl.BlockSpec(memory_space=pl.ANY), blk, blk, blk, blk, blk,
                  pl.BlockSpec((S, LANES), lambda p: (0, VB + p)), tab, tab, vm],
        out_specs=[pl.BlockSpec(memory_space=pl.ANY), vm],
        out_shape=[jax.ShapeDtypeStruct((S, E), BF16), jax.ShapeDtypeStruct((orows, ocols), F32)],
        scratch_shapes=[pltpu.VMEM((S, LANES), BF16)] * 6 + [pltpu.VMEM((nblk, LANES, 2 * BLK), BF16)]
        + [pltpu.VMEM((S, LANES), F32)] * 11
        + [pltpu.VMEM((nblk, 8, BLK), F32), pltpu.VMEM((nblk, LANES, BLK), F32)]
        + [pltpu.VMEM((nblk, 2, 2 * BLK, BLK), BF16)] * 2
        + [pltpu.VMEM((3, S, LANES), BF16), pltpu.SemaphoreType.DMA((3,)), pltpu.VMEM((orows, ocols), F32)]
        + _ReduceScatter.scratch(NCHIP, orows, ocols, 1),
        input_output_aliases={0: 0},
        compiler_params=_cp(("arbitrary",)),
    )(dproj, d_att, att, lse, qr, kr, proj, cos, sin, gw_out4)


def _out_fwd_bwd(ya, att, proj, w_out_bf, x, target, mod, norm_post, norm_att):
    ts = 512

    def body(ya_ref, att_ref, gb_ref, w_ref, x_ref, t_ref, mod_ref, npost_ref, natt_ref,
             gx_ref, dya_ref, datt_ref, dgb_ref, gw_ref, acc_ref):
        i = pl.program_id(0)

        @pl.when(i == 0)
        def _():
            gw_ref[...] = jnp.zeros_like(gw_ref)
            acc_ref[...] = jnp.zeros_like(acc_ref)

        gate = mod_ref[:, 2 * D:3 * D]
        att = att_ref[...]
        gb = gb_ref[...]
        sg = _sigmoid(gb)
        silu = gb * sg
        ybp = att * silu
        yb, ybn, rstd_b = _rms_fwd(ybp, natt_ref[...])
        cat = jnp.concatenate([ya_ref[...], yb.astype(BF16)], axis=1)
        mix = _dot(cat, w_ref[...])
        rn, mn, rstd_m = _rms_fwd(mix, npost_ref[...])
        err = x_ref[...] + gate * rn - t_ref[...]
        dy = err * (1.0 / D)
        gx_ref[...] = dy
        dmix, dnpost = _rms_bwd(dy * gate, mn, rstd_m, npost_ref[...])
        dmb = dmix.astype(BF16)
        gw_ref[...] += _dot_tn(cat, dmb)
        dcat = _dot_nt(dmb, w_ref[...])
        dya_ref[...] = dcat[:, 0:R]
        dybp, dnatt = _rms_bwd(dcat[:, R:2 * R], ybn, rstd_b, natt_ref[...])
        datt_ref[...] = dybp * silu
        dgb_ref[...] = (dybp * att * (sg * (1.0 + gb * (1.0 - sg)))).astype(BF16)
        acc_ref[0:1, :] += jnp.sum(dy * rn, axis=0, keepdims=True)
        acc_ref[1:2, :] += dnpost
        acc_ref[2:3, 0:R] += dnatt
        acc_ref[3:4, :] += jnp.sum(jnp.sum(err * err, axis=1, keepdims=True), axis=0, keepdims=True)

    tile = lambda w: pl.BlockSpec((ts, w), lambda i: (i, 0))
    c0 = lambda shape: pl.BlockSpec(shape, lambda i: (0, 0))
    return pl.pallas_call(
        body, name="out_fwd_bwd", grid=(S // ts,),
        in_specs=[tile(R), tile(R), pl.BlockSpec((ts, R), lambda i: (i, 5)), c0((D, D)), tile(D), tile(D),
                  c0((1, 3 * D)), c0((1, D)), c0((1, R))],
        out_specs=[tile(D), tile(R), tile(R), pl.BlockSpec((ts, R), lambda i: (i, 5)), c0((D, D)), c0((8, D))],
        out_shape=[jax.ShapeDtypeStruct((S, D), F32), jax.ShapeDtypeStruct((S, R), F32),
                   jax.ShapeDtypeStruct((S, R), F32), jax.ShapeDtypeStruct((S, E), BF16),
                   jax.ShapeDtypeStruct((D, D), F32), jax.ShapeDtypeStruct((8, D), F32)],
        compiler_params=_cp(("arbitrary",)),
    )(ya, att, proj, w_out_bf, x, target, mod, norm_post, norm_att)


UC = 256
UPC = EC // UC


NU = E // UC
MERGE_LAG = 3


def _unit_of_step(i):
    return (i % NCHIP) * UPC + i // NCHIP


def _in_proj_bwd(ht, dproj, w_in_bf, x, gx1, mod, norm_pre, smalls):
    ts = 256
    nt = S // ts
    half = D // 2
    units = [_unit_of_step(k) for k in range(NU)]
    owners = [u // UPC for u in units]
    ns = len(smalls)

    def body(*refs):
        (ht_ref, dpu_ref, dp_ref, w_hbm, x_ref, gx1_ref, mod_ref, np_ref), refs = refs[:8], refs[8:]
        small_in, refs = refs[:ns], refs[ns:]
        (gx_ref, gin_ref), refs = refs[:2], refs[2:]
        small_out, (acc_out,), refs = refs[:ns], refs[ns:ns + 1], refs[ns + 1:]
        mine, sib, tmp, stage, got, red, acc_ref, hs, hr, ps, pr, bs, br = refs[:13]
        early = _SmallGather(small_in, small_out, *refs[13:16])
        late = _SmallGather([acc_ref], [acc_out], *refs[16:19])
        w_ref, w_sem, pg, pps, ppr = refs[19:24]
        i = pl.program_id(0)
        w_copy = pltpu.make_async_copy(w_hbm, w_ref, w_sem)
        pl.when(i == 0)(w_copy.start)
        pl.when(i == NU)(w_copy.wait)
        xx, yy, c = _me()
        ci = 2 * xx + yy
        r0 = pl.multiple_of(c * half, half)
        r1 = pl.multiple_of((1 - c) * half, half)
        pl.when(i == 0)(early.start)
        pl.when(i == NU)(early.forward)

        def exch(k):
            return _remote(tmp.at[k % 2], sib.at[k], hs.at[k], hr.at[k], 1)

        def partial(k, sender):
            return pltpu.make_async_remote_copy(
                src_ref=stage.at[k], dst_ref=got.at[units[k] % UPC, sender], send_sem=ps.at[k],
                recv_sem=pr.at[k, sender], device_id=(owners[k] // 2, owners[k] % 2, c), device_id_type=MESH)

        def piece(k, h):
            q = half // 2
            return _remote(stage.at[k, pl.ds(h * q, q), :], pg.at[k], pps.at[k, h], ppr.at[k], 2 if h == 0 else 4)

        def back(k, start):
            off = (units[k] % UPC) * UC
            blk = red.at[pl.ds(start, half), off:off + UC]
            return _remote(blk, blk, bs.at[k], br.at[k], 1)

        for k in range(NU + 1):
            @pl.when(i == k)
            def _():
                if k < NU:
                    if k >= 2:
                        exch(k - 2).wait_send()
                    dpu = dpu_ref[...]
                    tmp[k % 2] = _dot(ht_ref[pl.ds(r1, half), :], dpu)
                    exch(k).start()
                    mine[k] = _dot(ht_ref[pl.ds(r0, half), :], dpu)
                if k >= 1:
                    exch(k - 1).wait_recv()
                    mine[k - 1] += sib[k - 1]

                    @pl.when((ci ^ owners[k - 1]) == NCHIP - 1)
                    def _():
                        stage[k - 1] = mine[k - 1].astype(BF16)
                        piece(k - 1, 0).start()
                        piece(k - 1, 1).start()

        for k in range(NU):
            @pl.when(i == k + MERGE_LAG)
            def _():
                for rel, h in ((1, 1), (2, 0)):
                    @pl.when((ci ^ owners[k]) == rel)
                    def _():
                        piece(k, h).wait_recv()
                        rows = pl.ds(h * (half // 2), half // 2)
                        mine[k, rows, :] += pg[k].astype(F32)
                        stage[k] = mine[k].astype(BF16)
                        partial(k, ci).start()

        @pl.when(i == NU)
        def _():
            acc_ref[...] = jnp.zeros_like(acc_ref)

        @pl.when(i >= NU)
        def _():
            dh = sum(_dot_nt(dp_ref[:, j * EC:(j + 1) * EC], w_ref[j]) for j in range(NCHIP))
            hp, xn, rstd = _rms_fwd(x_ref[...], np_ref[...])
            dx, dnp = _rms_bwd(dh * (1.0 + mod_ref[:, D:2 * D]), xn, rstd, np_ref[...])
            gx_ref[...] = gx1_ref[...] + dx
            acc_ref[0:1, :] += jnp.sum(dh, axis=0, keepdims=True)
            acc_ref[1:2, :] += jnp.sum(dh * hp, axis=0, keepdims=True)
            acc_ref[2:3, :] += dnp

        for t in range(UPC):
            @pl.when(i == NU + 1 + 2 * t)
            def _():
                for k in range(NCHIP * t, NCHIP * (t + 1)):
                    @pl.when(ci == owners[k])
                    def _():
                        off = (units[k] % UPC) * UC
                        red[pl.ds(r0, half), off:off + UC] = mine[k]
                        for s in sorted((owners[k] ^ 1, owners[k] ^ 2)):
                            partial(k, s).wait_recv()
                            red[pl.ds(r0, half), off:off + UC] += got[units[k] % UPC, s].astype(F32)
                        back(k, r0).start()

        @pl.when(i == NU + nt - 1)
        def _():
            late.start()
            exch(NU - 2).wait_send()
            exch(NU - 1).wait_send()
            for k in range(NU):
                @pl.when(ci == owners[k])
                def _():
                    back(k, r1).wait_recv()
                    back(k, r0).wait_send()

                @pl.when((ci ^ owners[k]) == NCHIP - 1)
                def _():
                    piece(k, 0).wait_send()
                    piece(k, 1).wait_send()

                @pl.when(jnp.logical_or((ci ^ owners[k]) == 1, (ci ^ owners[k]) == 2))
                def _():
                    partial(k, ci).wait_send()
            gin_ref[...] = red[...]
            early.finish()
            late.forward()
            late.finish()

    tile = lambda w: pl.BlockSpec((ts, w), lambda i: (jnp.maximum(i - NU, 0), 0))
    c0 = lambda shape: pl.BlockSpec(shape, lambda i: (0, 0))
    vm = pl.BlockSpec(memory_space=pltpu.VMEM)
    hbm = pl.BlockSpec(memory_space=pl.ANY)
    gathered = [jax.ShapeDtypeStruct((NDEV,) + a.shape, a.dtype) for a in smalls] + [jax.ShapeDtypeStruct((NDEV, 8, D), F32)]
    return pl.pallas_call(
        body, name="in_proj_bwd", grid=(NU + nt,),
        in_specs=[vm, pl.BlockSpec((S, UC), lambda i: (0, _unit_of_step(jnp.minimum(i, NU - 1)))), tile(E),
                  hbm, tile(D), tile(D), c0((1, 3 * D)), c0((1, D))] + [vm] * ns,
        out_specs=[tile(D), vm] + [hbm] * (ns + 1),
        out_shape=[jax.ShapeDtypeStruct((S, D), F32), jax.ShapeDtypeStruct((D, EC), F32)] + gathered,
        scratch_shapes=[pltpu.VMEM((NU, half, UC), F32), pltpu.VMEM((NU, half, UC), F32),
                        pltpu.VMEM((2, half, UC), F32), pltpu.VMEM((NU, half, UC), BF16),
                        pltpu.VMEM((UPC, NCHIP, half, UC), BF16), pltpu.VMEM((D, EC), F32), pltpu.VMEM((8, D), F32),
                        pltpu.SemaphoreType.DMA((NU,)), pltpu.SemaphoreType.DMA((NU,)),
                        pltpu.SemaphoreType.DMA((NU,)), pltpu.SemaphoreType.DMA((NU, NCHIP)),
                        pltpu.SemaphoreType.DMA((NU,)), pltpu.SemaphoreType.DMA((NU,))]
        + _SmallGather.sems(ns) + _SmallGather.sems(1)
        + [pltpu.VMEM((NCHIP, D, EC), BF16), pltpu.SemaphoreType.DMA]
        + [pltpu.VMEM((NU, half // 2, UC), BF16), pltpu.SemaphoreType.DMA((NU, 2)), pltpu.SemaphoreType.DMA((NU,))],
        compiler_params=_cp(("arbitrary",)),
    )(ht, dproj, dproj, w_in_bf, x, gx1, mod, norm_pre, *smalls)


def _local_step(x, cos, sin, target, mod, w_in_bf, proj, ht, w_out, conv_w, p):
    rec_p = (conv_w, p["conv_b"], p["w_rg_a"], p["b_rg_a"], p["w_rg_x"], p["b_rg_x"], p["lru_lambda"], p["norm_rec"])
    h_all, ya = _rec_fwd(proj, *rec_p)
    att, qr, kr, lse, w_out_bf = _att_fwd(proj, cos, sin, w_out)
    gx1, d_ya, d_att, dproj, gw_out, acc_o = _out_fwd_bwd(ya, att, proj, w_out_bf.reshape(D, D), x, target, mod,
                                                           p["norm_post"], p["norm_att"])
    dproj, g_out = _att_bwd(dproj, d_att, att, lse, qr, kr, proj, cos, sin, gw_out.reshape(NCHIP, D // NCHIP, D))
    dproj, dwa, dwx, sm = _rec_bwd(dproj, d_ya, proj, h_all, *rec_p)
    grad_x, g_in, *gathered = _in_proj_bwd(ht, dproj, w_in_bf, x, gx1, mod, p["norm_pre"], [acc_o, sm, dwa, dwx])
    return grad_x, g_in, g_out, gathered


def _me():
    return lax.axis_index("x"), lax.axis_index("y"), lax.axis_index("c")


def _flip(v, bit):
    return 1 - v if bit else v


def _peer(rel):
    x, y, c = _me()
    return (_flip(x, rel & 4), _flip(y, rel & 2), _flip(c, rel & 1))


def _remote(src, dst, send_sem, recv_sem, rel):
    return pltpu.make_async_remote_copy(src_ref=src, dst_ref=dst, send_sem=send_sem, recv_sem=recv_sem,
                                        device_id=_peer(rel), device_id_type=MESH)


class _WeightGather:
    SEMS = [pltpu.SemaphoreType.DMA((NCHIP - 1,))] * 4

    def __init__(self, w_ref, out_ref, send_sems, recv_sems, fsend_sems, frecv_sems):
        x, y, c = _me()
        self.w, self.out, self.ci = w_ref, out_ref, 2 * x + y
        self.half = w_ref.shape[0] // 2
        self.r0 = pl.multiple_of(c * self.half, self.half)
        self.r1 = pl.multiple_of((1 - c) * self.half, self.half)
        self.sems = (send_sems, recv_sems, fsend_sems, frecv_sems)

    def _ici(self, chip, k):
        blk = self.out.at[chip, pl.ds(self.r0, self.half), :]
        return _remote(blk, blk, self.sems[0].at[k - 1], self.sems[1].at[k - 1], 2 * k)

    def _d2d(self, chip, start, k):
        blk = self.out.at[chip, pl.ds(start, self.half), :]
        return _remote(blk, blk, self.sems[2].at[k - 1], self.sems[3].at[k - 1], 1)

    def start(self, diagonal=True):
        self.out[self.ci] = self.w[...].astype(BF16)
        for k in range(1, NCHIP if diagonal else NCHIP - 1):
            self._ici(self.ci, k).start()

    def _relay(self, chip, piece, k):
        q = self.half // 2
        blk = self.out.at[chip, pl.ds(self.r0 + piece * q, q), :]
        return _remote(blk, blk, self.relay_sems[0].at[piece], self.relay_sems[1].at[piece], 2 * k)

    def neighbours_landed(self, relay_send_sems, relay_recv_sems):
        self.relay_sems = (relay_send_sems, relay_recv_sems)
        for k in (1, 2):
            self._ici(self.ci ^ k, k).wait_recv()
        self._relay(self.ci ^ 2, 0, 1).start()
        self._relay(self.ci ^ 1, 1, 2).start()
        for k in (1, 2):
            self._d2d(self.ci ^ k, self.r0, k).start()

    def sibling_landed(self, k):
        self._d2d(self.ci ^ k, self.r1, k).wait_recv()

    def diagonal_landed(self):
        for piece, k in ((0, 1), (1, 2)):
            self._relay(self.ci ^ 3, piece, k).wait_recv()
        self._d2d(self.ci ^ 3, self.r0, 3).start()
        self._d2d(self.ci ^ 3, self.r1, 3).wait_recv()

    def finish_relayed(self):
        for k in (1, 2):
            self._ici(self.ci, k).wait_send()
        self._relay(self.ci ^ 2, 0, 1).wait_send()
        self._relay(self.ci ^ 1, 1, 2).wait_send()
        for k in range(1, NCHIP):
            self._d2d(self.ci ^ k, self.r0, k).wait_send()

    def forward(self):
        for k in range(1, NCHIP):
            self._ici(self.ci ^ k, k).wait_recv()
            self._d2d(self.ci ^ k, self.r0, k).start()

    def finish(self):
        for k in range(1, NCHIP):
            self._d2d(self.ci ^ k, self.r1, k).wait_recv()
        self.finish_sends()

    def finish_sends(self):
        for k in range(1, NCHIP):
            self._ici(self.ci, k).wait_send()
            self._d2d(self.ci ^ k, self.r0, k).wait_send()


class _SmallGather:
    @staticmethod
    def sems(n):
        return [pltpu.SemaphoreType.DMA((n, 7)), pltpu.SemaphoreType.DMA((n, 7)), pltpu.SemaphoreType.DMA((n,))]

    def __init__(self, srcs, outs, send_sems, recv_sems, local_sems):
        x, y, c = _me()
        self.srcs, self.outs = list(srcs), list(outs)
        self.ss, self.rs, self.ls = send_sems, recv_sems, local_sems
        self.ci, self.c = 2 * x + y, c
        self.me = 2 * self.ci + c

    def _own(self, a, slot, rel):
        return _remote(self.srcs[a], self.outs[a].at[self.me], self.ss.at[a, slot], self.rs.at[a, slot], rel)

    def _block(self, a, idx, slot, rel):
        blk = self.outs[a].at[idx]
        return _remote(blk, blk, self.ss.at[a, slot], self.rs.at[a, slot], rel)

    def _local(self, a):
        return pltpu.make_async_copy(self.srcs[a], self.outs[a].at[self.me], self.ls.at[a])

    def start(self):
        for a in range(len(self.srcs)):
            self._local(a).start()
            self._own(a, 0, 1).start()
            for k in range(1, NCHIP):
                self._own(a, k, 2 * k).start()

    def forward(self):
        for a in range(len(self.srcs)):
            for k in range(1, NCHIP):
                idx = 2 * (self.ci ^ k) + self.c
                self._block(a, idx, k, 2 * k).wait_recv()
                self._block(a, idx, 3 + k, 1).start()

    def finish(self):
        for a in range(len(self.srcs)):
            self._block(a, 2 * self.ci + 1 - self.c, 0, 1).wait_recv()
            for k in range(1, NCHIP):
                self._block(a, 2 * (self.ci ^ k) + 1 - self.c, 3 + k, 1).wait_recv()
            self._own(a, 0, 1).wait_send()
            for k in range(1, NCHIP):
                self._own(a, k, 2 * k).wait_send()
                self._block(a, 2 * (self.ci ^ k) + self.c, 3 + k, 1).wait_send()
            self._local(a).wait()


def _start_in_proj(crow, w_ada, b_cols, w_in, pos, x, norm_pre, order):
    ts = 512
    nt = S // ts
    wc = crow.shape[1]

    def body(order_ref, crow_ref, wada_ref, b_ref, win_ref, pos_ref, freq_ref, x_ref, np_ref,
             g0_ref, mod_ref, wbf_ref, cos_ref, sin_ref, proj_ref, ht_ref,
             g0s, modp, modb, wbuf, hb_all, cs, cr, ms, mr, ws, wr, fs, fr, local_sems, ys, yr, osem):
        s, t = pl.program_id(0), pl.program_id(1)
        x, y, c = _me()
        ci = 2 * x + y
        me = 2 * ci + c
        wg = _WeightGather(win_ref, wbuf, ws, wr, fs, fr)

        @pl.when(jnp.logical_and(s == 0, t == 0))
        def _():
            wg.start(diagonal=False)
            mine = pltpu.make_async_copy(crow_ref, g0s.at[pl.ds(me, 1), :], local_sems.at[0])
            mine.start()
            csend = [_remote(crow_ref, g0s.at[pl.ds(me, 1), :], cs.at[r - 1], cr.at[r - 1], r) for r in range(1, NDEV)]
            for cp in csend:
                cp.start()
            cos_ref[...], sin_ref[...] = _cos_sin(pos_ref, freq_ref)
            for r in range(1, NDEV):
                px, py, pc = _peer(r)
                _remote(crow_ref, g0s.at[pl.ds(4 * px + 2 * py + pc, 1), :], cs.at[r - 1], cr.at[r - 1], r).wait_recv()
            mine.wait()
            cv = g0s[:, 0:D]
            sc = cv * _sigmoid(cv)
            scb = jnp.concatenate([sc, jnp.zeros_like(sc)], axis=0).astype(BF16)
            modp[...] = _dot(scb, wada_ref[...].astype(BF16))[0:NDEV, :] + b_ref[...]
            own = pltpu.make_async_copy(modp.at[pl.ds(me, 1), :], modb.at[ci], local_sems.at[1])
            own.start()
            msend = []
            for k in range(1, NCHIP):
                cp = _remote(modp.at[pl.ds(2 * (ci ^ k) + c, 1), :], modb.at[ci], ms.at[k - 1], mr.at[k - 1], 2 * k)
                cp.start()
                msend.append(cp)
            for k in range(1, NCHIP):
                _remote(modp.at[pl.ds(me, 1), :], modb.at[ci ^ k], ms.at[k - 1], mr.at[k - 1], 2 * k).wait_recv()
            own.wait()
            for j in range(NCHIP):
                mod_ref[:, j * EC:(j + 1) * EC] = modb[j]
            for cp in csend + msend:
                cp.wait_send()
            g0_ref[...] = g0s[...]

        def keep(k):
            return pltpu.make_async_copy(wbuf.at[ci ^ k], wbf_ref.at[ci ^ k], osem.at[k])

        @pl.when(jnp.logical_and(s == 1, t == 0))
        def _():
            keep(0).start()
            wg.neighbours_landed(ys, yr)
            wg.sibling_landed(1)
            keep(1).start()

        @pl.when(jnp.logical_and(s == 2, t == 0))
        def _():
            wg.sibling_landed(2)
            keep(2).start()

        @pl.when(jnp.logical_and(s == 3, t == 0))
        def _():
            wg.relay_sems = (ys, yr)
            wg.diagonal_landed()
            keep(3).start()

        rows = pl.ds(pl.multiple_of(t * ts, ts), ts)

        @pl.when(s == 0)
        def _():
            hp, _, _ = _rms_fwd(x_ref[...], np_ref[...])
            h = hp * (1.0 + mod_ref[:, D:2 * D]) + mod_ref[:, 0:D]
            hb_all[rows, :] = h.astype(BF16)
            ht_ref[...] = h.T.astype(BF16)

        proj_ref[...] = _dot(hb_all[rows, :], wbuf[ci ^ s])

        @pl.when(jnp.logical_and(s == NCHIP - 1, t == nt - 1))
        def _():
            wg.relay_sems = (ys, yr)
            wg.finish_relayed()
            for k in range(NCHIP):
                keep(k).wait()

    vm = pl.BlockSpec(memory_space=pltpu.VMEM)
    first_pass = lambda s, t: jnp.where(s == 0, t, nt - 1)
    grid_spec = pltpu.PrefetchScalarGridSpec(
        num_scalar_prefetch=1, grid=(NCHIP, nt),
        in_specs=[vm, vm, vm, vm, vm, vm, pl.BlockSpec((ts, D), lambda s, t, o: (first_pass(s, t), 0)),
                  pl.BlockSpec((1, D), lambda s, t, o: (0, 0))],
        out_specs=[vm, vm, pl.BlockSpec(memory_space=pl.ANY), vm, vm, pl.BlockSpec((ts, EC), lambda s, t, o: (t, o[s])),
                   pl.BlockSpec((D, ts), lambda s, t, o: (0, first_pass(s, t)))],
        scratch_shapes=[pltpu.VMEM((NDEV, wc), F32), pltpu.VMEM((NDEV, EC), F32), pltpu.VMEM((NCHIP, 1, EC), F32),
                        pltpu.VMEM((NCHIP, D, EC), BF16), pltpu.VMEM((S, D), BF16),
                        pltpu.SemaphoreType.DMA((NDEV - 1,)), pltpu.SemaphoreType.DMA((NDEV - 1,)),
                        pltpu.SemaphoreType.DMA((NCHIP - 1,)), pltpu.SemaphoreType.DMA((NCHIP - 1,))]
        + _WeightGather.SEMS + [pltpu.SemaphoreType.DMA((2,))] * 3 + [pltpu.SemaphoreType.DMA((NCHIP,))])
    return pl.pallas_call(
        body, name="start_in_proj", grid_spec=grid_spec,
        out_shape=[jax.ShapeDtypeStruct((NDEV, wc), F32), jax.ShapeDtypeStruct((1, 3 * D), F32),
                   jax.ShapeDtypeStruct((NCHIP, D, EC), BF16), jax.ShapeDtypeStruct((S, LANES), F32),
                   jax.ShapeDtypeStruct((S, LANES), F32), jax.ShapeDtypeStruct((S, E), F32),
                   jax.ShapeDtypeStruct((D, S), BF16)],
        compiler_params=_cp(("arbitrary", "arbitrary")),
    )(order, crow, w_ada, b_cols, w_in, pos, _rope_freq(), x, norm_pre)


class _ReduceScatter:
    @staticmethod
    def scratch(n_units, rows, ucols, max_owned):
        half = rows // 2
        return [pltpu.VMEM((n_units, half, ucols), F32), pltpu.VMEM((n_units, half, ucols), BF16),
                pltpu.VMEM((max_owned, NCHIP, half, ucols), BF16),
                pltpu.SemaphoreType.DMA((2,)), pltpu.SemaphoreType.DMA((n_units,)),
                pltpu.SemaphoreType.DMA((n_units, NCHIP)), pltpu.SemaphoreType.DMA((n_units,)),
                pltpu.SemaphoreType.DMA((n_units,))]

    def __init__(self, g_ref, out_ref, units, sib, stage, got, sem1, send2, recv2, send3, recv3):
        x, y, c = _me()
        self.c, self.ci = c, 2 * x + y
        self.g, self.out, self.units = g_ref, out_ref, units
        self.sib, self.stage, self.got = sib, stage, got
        self.sem1, self.send2, self.recv2, self.send3, self.recv3 = sem1, send2, recv2, send3, recv3
        self.half = g_ref.shape[1] // 2
        self.ucols = g_ref.shape[2]
        self.r0 = pl.multiple_of(c * self.half, self.half)
        self.r1 = pl.multiple_of((1 - c) * self.half, self.half)
        self.slot0 = units[0][0]
        assert [u[0] for u in units] == list(range(self.slot0, self.slot0 + len(units)))
        seen = {}
        self.local = []
        for _, owner, _ in units:
            self.local.append(seen.get(owner, 0))
            seen[owner] = seen.get(owner, 0) + 1

    def _halves(self):
        n = len(self.units)
        return _remote(self.g.at[pl.ds(self.slot0, n), pl.ds(self.r1, self.half), :], self.sib,
                       self.sem1.at[0], self.sem1.at[1], 1)

    def _partial(self, i, sender):
        _, owner, _ = self.units[i]
        return pltpu.make_async_remote_copy(
            src_ref=self.stage.at[i], dst_ref=self.got.at[self.local[i], sender],
            send_sem=self.send2.at[i], recv_sem=self.recv2.at[i, sender],
            device_id=(owner // 2, owner % 2, self.c), device_id_type=MESH)

    def _back(self, i, start):
        off = self.units[i][2]
        blk = self.out.at[pl.ds(start, self.half), off:off + self.ucols]
        return _remote(blk, blk, self.send3.at[i], self.recv3.at[i], 1)

    def start_halves(self):
        self._halves().start()

    def send_partials(self):
        self._halves().wait_recv()
        for i, (slot, owner, _) in enumerate(self.units):
            @pl.when(self.ci != owner)
            def _():
                self.stage[i] = (self.g[slot, pl.ds(self.r0, self.half), :] + self.sib[i]).astype(BF16)
                self._partial(i, self.ci).start()

    def reduce_owned(self):
        for i, (slot, owner, off) in enumerate(self.units):
            @pl.when(self.ci == owner)
            def _():
                rows, cols = pl.ds(self.r0, self.half), slice(off, off + self.ucols)
                self.out[rows, cols] = self.g[slot, pl.ds(self.r0, self.half), :] + self.sib[i]
                for s in range(NCHIP):
                    if s != owner:
                        self._partial(i, s).wait_recv()
                        self.out[rows, cols] += self.got[self.local[i], s].astype(F32)
                self._back(i, self.r0).start()

    def finish(self):
        self._halves().wait_send()
        for i, (_, owner, _) in enumerate(self.units):
            @pl.when(self.ci == owner)
            def _():
                self._back(i, self.r1).wait_recv()
                self._back(i, self.r0).wait_send()

            @pl.when(self.ci != owner)
            def _():
                self._partial(i, self.ci).wait_send()


def _silu_rows(c_ref):
    cv = c_ref[...]
    sc = cv * _sigmoid(cv)
    return jnp.concatenate([sc, jnp.zeros_like(sc)], axis=0).astype(BF16)


def _adamw(w, g, m, v, name):
    rows, cols = w.shape
    tr = 256 if rows % 256 == 0 else rows

    def body(w_ref, g_ref, m_ref, v_ref, d_ref, nm_ref, nv_ref):
        d_ref[...], nm_ref[...], nv_ref[...] = _adamw_values(w_ref[...], g_ref[...], m_ref[...], v_ref[...])

    spec = pl.BlockSpec((tr, cols), lambda i: (i, 0))
    return pl.pallas_call(
        body, name=name, grid=(rows // tr,), in_specs=[spec] * 4, out_specs=[spec] * 3,
        out_shape=[jax.ShapeDtypeStruct((rows, cols), F32)] * 3,
        compiler_params=_cp(("parallel",)),
    )(w, g, m, v)


def _adamw_values(w, g, m, v):
    nm = B1 * m + (1.0 - B1) * g
    nv = B2 * v + (1.0 - B2) * (g * g)
    m_hat = nm / (1.0 - B1 ** STEP)
    v_hat = nv / (1.0 - B2 ** STEP)
    return (-LR) * (m_hat / (jnp.sqrt(v_hat) + ADAM_EPS) + WD * w), nm, nv


NB = R // HEAD
SMALL = (("b_ada", (1, 3 * D)), ("norm_pre", (1, D)), ("norm_post", (1, D)), ("conv_w", (4, R // NCHIP)),
         ("conv_b", (1, R)), ("w_rg_a", (NB, HEAD, HEAD)), ("b_rg_a", (1, R)), ("w_rg_x", (NB, HEAD, HEAD)),
         ("b_rg_x", (1, R)), ("lru_lambda", (1, R)), ("norm_rec", (1, R)), ("norm_att", (1, R)))


def _small_update(ao8, sm8, dwa8, dwx8, ai8, cg, params):
    n = len(SMALL)

    def body(ao_ref, sm_ref, dwa_ref, dwx_ref, ai_ref, cg_ref, *refs):
        pin, pout, (gada_ref, loss_ref, dmod) = refs[:3 * n], refs[3 * n:7 * n], refs[7 * n:]
        xx, yy, _ = _me()
        ci = 2 * xx + yy

        def total(ref, *idx):
            acc = ref[(0,) + idx].astype(F32)
            for d in range(1, NDEV):
                acc = acc + ref[(d,) + idx].astype(F32)
            return acc

        row = lambda ref, r, lanes=slice(None): total(ref, slice(r, r + 1), lanes)
        mine = lambda parts: sum(jnp.where(ci == j, part, 0.0) for j, part in enumerate(parts))
        cw = R // NCHIP
        grads = {
            "b_ada": [jnp.concatenate([row(ai_ref, 0), row(ai_ref, 1), row(ao_ref, 0)], axis=1)],
            "norm_pre": [row(ai_ref, 2)], "norm_post": [row(ao_ref, 1)],
            "conv_w": [mine([row(sm_ref, 8 + r, slice(j * cw, (j + 1) * cw)) for j in range(NCHIP)]) for r in range(4)],
            "conv_b": [row(sm_ref, 4)], "b_rg_a": [row(sm_ref, 0)], "b_rg_x": [row(sm_ref, 1)],
            "lru_lambda": [row(sm_ref, 2)], "norm_rec": [row(sm_ref, 3)], "norm_att": [row(ao_ref, 2, slice(0, R))],
            "w_rg_a": [total(dwa_ref, h) for h in range(NB)], "w_rg_x": [total(dwx_ref, h) for h in range(NB)],
        }
        loss_ref[...] = row(ao_ref, 3, slice(0, LANES)) * (0.5 / D)
        for k, (name, shape) in enumerate(SMALL):
            w_ref, m_ref, v_ref = pin[3 * k:3 * k + 3]
            outs = pout[4 * k:4 * k + 4]
            for r, g in enumerate(grads[name]):
                at = (slice(None),) if len(grads[name]) == 1 else ((r,) if len(shape) == 3 else (slice(r, r + 1),))
                res = (g,) + _adamw_values(w_ref[at], g, m_ref[at], v_ref[at])
                for o_ref, val in zip(outs, res):
                    o_ref[at] = val
        for d in range(NDEV):
            dmod[d:d + 1, :] = jnp.concatenate([ai_ref[d, 0:1, :], ai_ref[d, 1:2, :], ao_ref[d, 0:1, :]], axis=1)
        cols = mine([dmod[:, j * EC:(j + 1) * EC] for j in range(NCHIP)])
        colsb = jnp.concatenate([cols, jnp.zeros_like(cols)], axis=0).astype(BF16)
        gada_ref[...] = _dot_tn(_silu_rows(cg_ref), colsb)

    shapes = [jax.ShapeDtypeStruct(s, F32) for _, s in SMALL]
    outs = pl.pallas_call(
        body, name="small_update",
        out_shape=[s for s in shapes for _ in range(4)] + [jax.ShapeDtypeStruct((D, EC), F32),
                                                           jax.ShapeDtypeStruct((1, LANES), F32)],
        scratch_shapes=[pltpu.VMEM((NDEV, 3 * D), F32)],
        compiler_params=_cp(),
    )(ao8, sm8, dwa8, dwx8, ai8, cg, *params)
    return outs[:4 * n], outs[4 * n], outs[4 * n + 1]


BIG = ("w_ada", "w_in", "w_out")
WEIGHTS = ("w_ada", "b_ada", "norm_pre", "norm_post", "w_in", "conv_w", "conv_b", "w_rg_a", "b_rg_a", "w_rg_x",
           "b_rg_x", "lru_lambda", "norm_rec", "norm_att", "w_out")


def kernel(x, c, positions, w_ada, b_ada, norm_pre, norm_post, w_in, conv_w, conv_b, w_rg_a, b_rg_a, w_rg_x, b_rg_x, lru_lambda, norm_rec, norm_att, w_out, loss_target, m_w_ada, m_b_ada, m_norm_pre, m_norm_post, m_w_in, m_conv_w, m_conv_b, m_w_rg_a, m_b_rg_a, m_w_rg_x, m_b_rg_x, m_lru_lambda, m_norm_rec, m_norm_att, m_w_out, v_w_ada, v_b_ada, v_norm_pre, v_norm_post, v_w_in, v_conv_w, v_conv_b, v_w_rg_a, v_b_rg_a, v_w_rg_x, v_b_rg_x, v_lru_lambda, v_norm_rec, v_norm_att, v_w_out):
    given = dict(locals())
    wts = {n: given[n] for n in WEIGHTS}
    ms = {n: given["m_" + n] for n in WEIGHTS}
    vs = {n: given["v_" + n] for n in WEIGHTS}
    xi, yi, _ = _me()
    chip = 2 * xi + yi
    cw_loc = R // NCHIP

    b_cols = lax.dynamic_slice(b_ada, (0, chip * EC), (1, EC))
    order = (chip ^ jnp.arange(NCHIP, dtype=jnp.int32)).astype(jnp.int32)
    g0, mod, w_in_bf, cos, sin, proj, ht = _start_in_proj(
        jnp.concatenate([c, conv_w.reshape(1, 4 * cw_loc)], axis=1), w_ada[0], b_cols, w_in[0],
        positions.reshape(S, 1), x[0], norm_pre, order)
    cg = g0[:, 0:D]
    conv_full = g0[0::2, D:].reshape(NCHIP, 4, cw_loc).transpose(1, 0, 2).reshape(4, R)

    p = dict(norm_pre=norm_pre, norm_post=norm_post, conv_b=conv_b, b_rg_a=b_rg_a, b_rg_x=b_rg_x,
             lru_lambda=lru_lambda, norm_rec=norm_rec, norm_att=norm_att, w_rg_a=w_rg_a[0], w_rg_x=w_rg_x[0])
    grad_x, g_in, g_out, gathered = _local_step(
        x[0], cos, sin, loss_target[0], mod, w_in_bf, proj, ht, w_out[0], conv_full, p)

    params = [d[n].reshape(shape) for n, shape in SMALL for d in (wts, ms, vs)]
    small_out, g_ada, loss_row = _small_update(*gathered, cg, params)
    grads = {"w_out": g_out, "w_in": g_in, "w_ada": g_ada}
    delta, new_m, new_v = {}, {}, {}
    for k, (n, _) in enumerate(SMALL):
        grads[n], delta[n], new_m[n], new_v[n] = small_out[4 * k:4 * k + 4]
    for n in BIG:
        delta[n], new_m[n], new_v[n] = _adamw(wts[n][0], grads[n], ms[n][0], vs[n][0], "adamw_" + n)
    out = lambda d: [d[n].reshape(wts[n].shape) for n in WEIGHTS]
    return (loss_row[0, 0], grad_x.reshape(x.shape), *out(grads), *out(delta), *out(new_m), *out(new_v))
```

```python
import numpy as np
import jax
import jax.numpy as jnp
from jax import lax
from jax.experimental import pallas as pl
from jax.experimental.pallas import tpu as pltpu

F32 = jnp.float32
BF16 = jnp.bfloat16

S = 2048
D = 1024
E = 3072
R = 512
NDEV = 8
NCHIP = 4
EC = 768
LRU_C = 8.0
EPS = 1e-6
NEG = -1e30
HEAD = 64
BLK = 128
PATTERNS = (1, 4, 16)
ROPE_THETA = 10000.0
LANES = 128
VMEM_LIMIT = 56 * 1024 * 1024

B1, B2, LR, WD, ADAM_EPS, STEP = 0.9, 0.999, 0.001, 0.01, 1e-8, 10
MESH = pl.DeviceIdType.MESH


def _cp(sem=None, **kw):
    return pltpu.CompilerParams(dimension_semantics=sem, vmem_limit_bytes=VMEM_LIMIT, **kw)


def _dot(a, b):
    return jnp.dot(a, b, preferred_element_type=F32)


def _dot_nt(a, b):
    return lax.dot_general(a, b, (((1,), (1,)), ((), ())), preferred_element_type=F32)


def _dot_tn(a, b):
    return lax.dot_general(a, b, (((0,), (0,)), ((), ())), preferred_element_type=F32)


def _sigmoid(x):
    return 1.0 / (1.0 + jnp.exp(-x))


def _one_minus_exp(x, ex):
    poly = -x * (1.0 + x * (0.5 + x * (1.0 / 6 + x * (1.0 / 24))))
    return jnp.where(x > -1.0 / 16, poly, 1.0 - ex)


def _rms_fwd(v, g):
    rstd = lax.rsqrt(jnp.mean(v * v, axis=-1, keepdims=True) + EPS)
    vn = v * rstd
    return vn * g, vn, rstd


def _rms_bwd(dy, vn, rstd, g):
    dvn = dy * g
    dv = rstd * (dvn - vn * jnp.mean(dvn * vn, axis=-1, keepdims=True))
    return dv, jnp.sum(dy * vn, axis=0, keepdims=True)


RT = 256


def _shift_down(cur, prev8, j, row):
    if j == 0:
        return cur
    rolled = pltpu.roll(cur, j, 0)
    top = jnp.where(row[0:8] >= j, rolled[0:8], pltpu.roll(prev8, j, 0))
    return jnp.concatenate([top, rolled[8:]], axis=0)


def _shift_up(cur, next8, j, row):
    if j == 0:
        return cur
    rolled = pltpu.roll(cur, RT - j, 0)
    bot = jnp.where(row[RT - 8:] < RT - j, rolled[RT - 8:], pltpu.roll(next8, 8 - j, 0))
    return jnp.concatenate([rolled[:RT - 8], bot], axis=0)


def _rec_gates(xp, xprev8, row, cw_ref, cb_ref, wa_ref, ba_ref, wx_ref, bx_ref, lam_ref):
    xa = cb_ref[...] + sum(cw_ref[3 - j:4 - j, :] * _shift_down(xp, xprev8, j, row) for j in range(4))
    xab = xa.astype(BF16)
    r = _sigmoid(_dot(xab, wa_ref[...]) + ba_ref[...])
    ig = _sigmoid(_dot(xab, wx_ref[...]) + bx_ref[...])
    nl = -lam_ref[...]
    sp = jnp.maximum(nl, 0.0) + jnp.log1p(jnp.exp(-jnp.abs(nl)))
    la = (-LRU_C) * r * sp
    a = jnp.exp(la)
    mult = jnp.sqrt(_one_minus_exp(2.0 * la, a * a))
    return dict(xa=xa, xab=xab, r=r, ig=ig, sp=sp, la=la, a=a, mult=mult)


def _scan_fwd(a, u, row):
    sh = 1
    while sh < RT:
        a_s = jnp.where(row >= sh, pltpu.roll(a, sh, 0), 1.0)
        u_s = jnp.where(row >= sh, pltpu.roll(u, sh, 0), 0.0)
        u = a * u_s + u
        a = a * a_s
        sh *= 2
    return a, u


def _scan_bwd(al, g, row):
    sh = 1
    while sh < RT:
        al_s = jnp.where(row < RT - sh, pltpu.roll(al, RT - sh, 0), 1.0)
        g_s = jnp.where(row < RT - sh, pltpu.roll(g, RT - sh, 0), 0.0)
        g = g + al * g_s
        al = al * al_s
        sh *= 2
    return g


def _dense_from_blocks(blocks_ref, dense_ref):
    dense_ref[...] = jnp.zeros_like(dense_ref)
    for h in range(R // HEAD):
        dense_ref[h * HEAD:(h + 1) * HEAD, h * HEAD:(h + 1) * HEAD] = blocks_ref[h].astype(dense_ref.dtype)


def _rec_fwd(proj, conv_w, conv_b, wa_b, ba, wx_b, bx, lam, norm_rec):
    nt = S // RT

    def body(p_ref, cw_ref, cb_ref, wa_ref, ba_ref, wx_ref, bx_ref, lam_ref, nr_ref,
             h_ref, ya_ref, prev8, hc, wad, wxd):
        i = pl.program_id(0)

        @pl.when(i == 0)
        def _():
            prev8[...] = jnp.zeros_like(prev8)
            hc[...] = jnp.zeros_like(hc)
            _dense_from_blocks(wa_ref, wad)
            _dense_from_blocks(wx_ref, wxd)

        row = lax.broadcasted_iota(jnp.int32, (RT, R), 0)
        xp = p_ref[:, 0:R]
        ga = p_ref[:, R:2 * R]
        f = _rec_gates(xp, prev8[...], row, cw_ref, cb_ref, wad, ba_ref, wxd, bx_ref, lam_ref)
        u = f["mult"] * (f["ig"] * f["xa"])
        acum, hh = _scan_fwd(f["a"], u, row)
        h = hh + acum * hc[0:1, :]
        h_ref[...] = h
        hc[0:1, :] = h_ref[RT - 1:RT, :]
        prev8[...] = p_ref[RT - 8:RT, 0:R]
        yp = h * (ga * _sigmoid(ga))
        ya, _, _ = _rms_fwd(yp, nr_ref[...])
        ya_ref[...] = ya.astype(BF16)

    row1 = lambda n: pl.BlockSpec((1, n), lambda i: (0, 0))
    blocks = pl.BlockSpec((R // HEAD, HEAD, HEAD), lambda i: (0, 0, 0))
    return pl.pallas_call(
        body, name="rec_fwd", grid=(nt,),
        in_specs=[pl.BlockSpec((RT, 2 * R), lambda i: (i, 0)), pl.BlockSpec((4, R), lambda i: (0, 0)), row1(R),
                  blocks, row1(R), blocks, row1(R), row1(R), row1(R)],
        out_specs=[pl.BlockSpec((RT, R), lambda i: (i, 0)), pl.BlockSpec((RT, R), lambda i: (i, 0))],
        out_shape=[jax.ShapeDtypeStruct((S, R), F32), jax.ShapeDtypeStruct((S, R), BF16)],
        scratch_shapes=[pltpu.VMEM((8, R), F32), pltpu.VMEM((8, R), F32), pltpu.VMEM((R, R), BF16),
                        pltpu.VMEM((R, R), BF16)],
        compiler_params=_cp(("arbitrary",)),
    )(proj, conv_w, conv_b, wa_b, ba, wx_b, bx, lam, norm_rec)


def _rec_bwd(dproj, d_ya, proj, h_all, conv_w, conv_b, wa_b, ba, wx_b, bx, lam, norm_rec):
    nt = S // RT

    def body(dp_in, dya_ref, p_ref, pprev_ref, h_ref, hprev_ref, cw_ref, cb_ref, wab_ref, ba_ref, wxb_ref, bx_ref,
             lam_ref, nr_ref, dp_ref, dwab_ref, dwxb_ref, sm_ref, nxt8, cg, wa_ref, wx_ref, dwa_ref, dwx_ref):
        i = pl.program_id(0)
        ti = nt - 1 - i

        @pl.when(i == 0)
        def _():
            nxt8[...] = jnp.zeros_like(nxt8)
            cg[...] = jnp.zeros_like(cg)
            dwa_ref[...] = jnp.zeros_like(dwa_ref)
            dwx_ref[...] = jnp.zeros_like(dwx_ref)
            sm_ref[...] = jnp.zeros_like(sm_ref)
            _dense_from_blocks(wab_ref, wa_ref)
            _dense_from_blocks(wxb_ref, wx_ref)

        row = lax.broadcasted_iota(jnp.int32, (RT, R), 0)
        first = (ti > 0).astype(F32)
        xprev8 = pprev_ref[...] * first
        hprev8 = hprev_ref[...] * first
        xp = p_ref[:, 0:R]
        ga = p_ref[:, R:2 * R]
        f = _rec_gates(xp, xprev8, row, cw_ref, cb_ref, wa_ref, ba_ref, wx_ref, bx_ref, lam_ref)
        xa, r, ig, a, mult = f["xa"], f["r"], f["ig"], f["a"], f["mult"]
        h = h_ref[...]
        sg = _sigmoid(ga)
        gate = ga * sg
        yp = h * gate
        _, ypn, rstd = _rms_fwd(yp, nr_ref[...])
        d_yp, dnr = _rms_bwd(dya_ref[...], ypn, rstd, nr_ref[...])
        d_ga = d_yp * h * (sg * (1.0 + ga * (1.0 - sg)))
        dh = d_yp * gate + jnp.where(row == RT - 1, cg[0:1, :], 0.0)
        al = jnp.where(row < RT - 1, pltpu.roll(a, RT - 1, 0), 0.0)
        g = _scan_bwd(al, dh, row)
        cg[0:1, :] = jnp.sum(jnp.where(row == 0, a * g, 0.0), axis=0, keepdims=True)
        h_m1 = _shift_down(h, hprev8, 1, row)
        da = g * h_m1
        ix = ig * xa
        d_mult = g * ix
        d_ig = g * mult * xa
        d_xa = g * mult * ig
        d_la = da * a - d_mult * (a * a) / mult
        d_r = d_la * ((-LRU_C) * f["sp"])
        dsp = jnp.sum(d_la * ((-LRU_C) * r), axis=0, keepdims=True)
        dlam = dsp * (-_sigmoid(-lam_ref[...]))
        d_za = d_r * r * (1.0 - r)
        d_zx = d_ig * ig * (1.0 - ig)
        dzab = d_za.astype(BF16)
        dzxb = d_zx.astype(BF16)
        dwa_ref[...] += _dot_tn(f["xab"], dzab)
        dwx_ref[...] += _dot_tn(f["xab"], dzxb)
        d_xa = d_xa + _dot_nt(dzab, wa_ref[...]) + _dot_nt(dzxb, wx_ref[...])
        d_xp = sum(cw_ref[3 - j:4 - j, :] * _shift_up(d_xa, nxt8[...], j, row) for j in range(4))
        dcw = [jnp.sum(d_xa * _shift_down(xp, xprev8, 3 - k, row), axis=0, keepdims=True) for k in range(4)]
        dp_ref[:, 0:R] = d_xp.astype(BF16)
        dp_ref[:, R:2 * R] = d_ga.astype(BF16)
        dp8 = d_xa[0:8, :]
        nxt8[...] = dp8
        sm_ref[0:1, :] += jnp.sum(d_za, axis=0, keepdims=True)
        sm_ref[1:2, :] += jnp.sum(d_zx, axis=0, keepdims=True)
        sm_ref[2:3, :] += dlam
        sm_ref[3:4, :] += dnr
        sm_ref[4:5, :] += jnp.sum(d_xa, axis=0, keepdims=True)
        for k in range(4):
            sm_ref[8 + k:9 + k, :] += dcw[k]

        @pl.when(i == nt - 1)
        def _():
            for h in range(R // HEAD):
                dwab_ref[h] = dwa_ref[h * HEAD:(h + 1) * HEAD, h * HEAD:(h + 1) * HEAD].astype(BF16)
                dwxb_ref[h] = dwx_ref[h * HEAD:(h + 1) * HEAD, h * HEAD:(h + 1) * HEAD].astype(BF16)

    c0 = lambda shape: pl.BlockSpec(shape, lambda i: (0, 0))
    blocks = pl.BlockSpec((R // HEAD, HEAD, HEAD), lambda i: (0, 0, 0))
    rev = lambda i: nt - 1 - i
    prev8 = lambda i: (jnp.maximum((nt - 1 - i) * (RT // 8) - 1, 0), 0)
    return pl.pallas_call(
        body, name="rec_bwd", grid=(nt,),
        in_specs=[pl.BlockSpec(memory_space=pl.ANY),
                  pl.BlockSpec((RT, R), lambda i: (rev(i), 0)),
                  pl.BlockSpec((RT, 2 * R), lambda i: (rev(i), 0)), pl.BlockSpec((8, R), prev8),
                  pl.BlockSpec((RT, R), lambda i: (rev(i), 0)), pl.BlockSpec((8, R), prev8),
                  c0((4, R)), c0((1, R)), blocks, c0((1, R)), blocks, c0((1, R)), c0((1, R)), c0((1, R))],
        out_specs=[pl.BlockSpec((RT, 2 * R), lambda i: (rev(i), 0)), blocks, blocks, c0((16, R))],
        out_shape=[jax.ShapeDtypeStruct((S, E), BF16), jax.ShapeDtypeStruct((R // HEAD, HEAD, HEAD), BF16),
                   jax.ShapeDtypeStruct((R // HEAD, HEAD, HEAD), BF16), jax.ShapeDtypeStruct((16, R), F32)],
        scratch_shapes=[pltpu.VMEM((8, R), F32), pltpu.VMEM((8, R), F32), pltpu.VMEM((R, R), BF16),
                        pltpu.VMEM((R, R), BF16), pltpu.VMEM((R, R), F32), pltpu.VMEM((R, R), F32)],
        input_output_aliases={0: 0},
        compiler_params=_cp(("arbitrary",)),
    )(dproj, d_ya, proj, proj, h_all, h_all, conv_w, conv_b, wa_b, ba, wx_b, bx, lam, norm_rec)


NPAIR = R // LANES
QB, KB, VB, GB = 2 * R // LANES, 3 * R // LANES, 4 * R // LANES, 5 * R // LANES


def _rope_freq():
    half = HEAD // 2
    inv = np.float32(ROPE_THETA) ** (-(np.arange(half, dtype=np.float32) / np.float32(half)))
    return jnp.asarray(np.tile(inv.astype(np.float32), LANES // half)[None, :])


def _rot_half(x, first):
    return jnp.where(first, -pltpu.roll(x, LANES - HEAD // 2, 1), pltpu.roll(x, HEAD // 2, 1))


def _cos_sin(pos_ref, freq_ref):
    ang = pos_ref[...].astype(F32) * freq_ref[...]
    return jnp.cos(ang), jnp.sin(ang)


SUB = 4


def _stages(d):
    assert d in (1, SUB, SUB * SUB)
    return d > SUB


def _strided_rows(src_ref, d, tmp):
    n = S // d
    if not _stages(d):
        for r in range(d):
            yield r * n, (src_ref[pl.ds(r, n, stride=d), :] if d > 1 else src_ref[...])
        return
    m = S // SUB
    for r in range(SUB):
        tmp[r * m:(r + 1) * m, :] = src_ref[pl.ds(r, m, stride=SUB), :]
    for r in range(SUB):
        for q in range(SUB):
            yield (r + SUB * q) * n, tmp[pl.ds(r * m + q, n, stride=SUB), :]


def _deint(src_ref, dst_ref, d, tmp):
    n = S // d
    for row0, v in _strided_rows(src_ref, d, tmp):
        dst_ref[row0:row0 + n, :] = v.astype(dst_ref.dtype)


def _reint(src_ref, dst_ref, d, accumulate, tmp):
    if _stages(d):
        n, m = S // d, S // SUB
        for r in range(SUB):
            for q in range(SUB):
                tmp[pl.ds(r * m + q, n, stride=SUB), :] = src_ref[(r + SUB * q) * n:(r + SUB * q + 1) * n, :]
        src_ref, d = tmp, SUB
    n = S // d
    for r in range(d):
        idx = (pl.ds(r, n, stride=d), slice(None)) if d > 1 else (slice(None), slice(None))
        v = src_ref[r * n:(r + 1) * n, :]
        if accumulate:
            dst_ref[idx] = dst_ref[idx] + v
        else:
            dst_ref[idx] = v


def _deint_heads(src_ref, dst0, dst1, d, tmp):
    n = S // d
    hm0 = lax.broadcasted_iota(jnp.int32, (n, LANES), 1) < HEAD
    for row0, v in _strided_rows(src_ref, d, tmp):
        dst0[row0:row0 + n, :] = jnp.where(hm0, v, 0.0).astype(BF16)
        dst1[row0:row0 + n, :] = jnp.where(hm0, 0.0, v).astype(BF16)


def _reint_prev(src_ref, dst_ref, d):
    n = S // d
    if n == BLK:
        return
    for r in range(d):
        idx = (pl.ds(r, n - BLK, stride=d), slice(None)) if d > 1 else (slice(0, n - BLK), slice(None))
        dst_ref[idx] = dst_ref[idx] + src_ref[r * n + BLK:(r + 1) * n, :]


def _pair_masks():
    qi = lax.broadcasted_iota(jnp.int32, (BLK, 2 * BLK), 0)
    ki = lax.broadcasted_iota(jnp.int32, (BLK, 2 * BLK), 1) & (BLK - 1)
    return ki <= qi, ki >= qi


def _two(ref0, ref1, st, axis):
    return jnp.concatenate([ref0[pl.ds(st, BLK), :], ref1[pl.ds(st, BLK), :]], axis=axis)


ATT_UNROLL = 8


def _att_fwd(proj, cos, sin, w_out):
    def body(q_ref, k_ref, v_ref, cos_ref, sin_ref, w_ref, att_ref, qr_ref, kr_ref, lse_ref, wbf_ref,
             qd, kd0, kd1, vd0, vd1, od, ld, tmp, on, ln, wbuf, *wsems):
        wg = _WeightGather(w_ref, wbuf, *wsems)
        pl.when(pl.program_id(0) == 0)(wg.start)
        pl.when(pl.program_id(0) == 1)(wg.forward)
        lane = lax.broadcasted_iota(jnp.int32, (S, LANES), 1)
        first = (lane & (HEAD // 2)) == 0
        cos, sin = cos_ref[...], sin_ref[...]
        q = q_ref[...]
        k = k_ref[...]
        qr_ref[...] = (q * cos + _rot_half(q, first) * sin) * (HEAD ** -0.5)
        kr_ref[...] = k * cos + _rot_half(k, first) * sin
        hm0 = lax.broadcasted_iota(jnp.int32, (BLK, LANES), 1) < HEAD
        top = lax.broadcasted_iota(jnp.int32, (2 * BLK, LANES), 0) < BLK
        ones2 = (top == (lax.broadcasted_iota(jnp.int32, (2 * BLK, LANES), 1) < HEAD)).astype(BF16)
        mc2, mp2 = _pair_masks()

        for pi, d in enumerate(PATTERNS):
            nb = S // d // BLK
            _deint(qr_ref, qd, d, tmp)
            _deint_heads(kr_ref, kd0, kd1, d, tmp)
            _deint_heads(v_ref, vd0, vd1, d, tmp)

            def blk(b, carry):
                st = pl.multiple_of(b * BLK, BLK)
                qb = qd[pl.ds(st, BLK), :]
                sc = jnp.where(mc2, _dot_nt(qb, _two(kd0, kd1, st, 0)), NEG)
                mx = sc
                if nb > 1:
                    stp = pl.multiple_of(jnp.maximum(b - 1, 0) * BLK, BLK)
                    mp = jnp.logical_and(mp2, lax.rem(b, nb) != 0)
                    sp = jnp.where(mp, _dot_nt(qb, _two(kd0, kd1, stp, 0)), NEG)
                    mx = jnp.maximum(sc, sp)
                m0 = jnp.max(mx[:, 0:BLK], axis=1, keepdims=True)
                m1 = jnp.max(mx[:, BLK:2 * BLK], axis=1, keepdims=True)
                mf = jnp.concatenate([jnp.broadcast_to(m0, (BLK, BLK)), jnp.broadcast_to(m1, (BLK, BLK))], axis=1)
                o = _dot(jnp.exp(sc - mf).astype(BF16), jnp.concatenate([_two(vd0, vd1, st, 0), ones2], axis=1))
                if nb > 1:
                    o = o + _dot(jnp.exp(sp - mf).astype(BF16), jnp.concatenate([_two(vd0, vd1, stp, 0), ones2], axis=1))
                l = o[:, LANES:2 * LANES]
                od[pl.ds(st, BLK), :] = o[:, 0:LANES] / l
                ld[pl.ds(st, BLK), :] = jnp.where(hm0, m0, m1) + jnp.log(l)
                return carry

            lax.fori_loop(0, S // BLK, blk, 0, unroll=ATT_UNROLL)
            _reint(od, on.at[pi], d, False, tmp)
            _reint(ld, ln.at[pi], d, False, tmp)

        l0, l1, l2 = ln[0], ln[1], ln[2]
        m = jnp.maximum(jnp.maximum(l0, l1), l2)
        e0, e1, e2 = jnp.exp(l0 - m), jnp.exp(l1 - m), jnp.exp(l2 - m)
        den = e0 + e1 + e2
        att_ref[...] = (e0 * on[0] + e1 * on[1] + e2 * on[2]) / den
        lse_ref[...] = m + jnp.log(den)

        @pl.when(pl.program_id(0) == NPAIR - 1)
        def _():
            wg.finish()
            wbf_ref[...] = wbuf[...]

    col = lambda c0: pl.BlockSpec((S, LANES), lambda p: (0, c0 + p))
    out = pl.BlockSpec((S, LANES), lambda p: (0, p))
    tab = pl.BlockSpec((S, LANES), lambda p: (0, 0))
    vm = pl.BlockSpec(memory_space=pltpu.VMEM)
    return pl.pallas_call(
        body, name="att_fwd", grid=(NPAIR,),
        in_specs=[col(QB), col(KB), col(VB), tab, tab, vm],
        out_specs=[out, out, out, out, vm],
        out_shape=[jax.ShapeDtypeStruct((S, R), F32)] * 4 + [jax.ShapeDtypeStruct((NCHIP,) + w_out.shape, BF16)],
        scratch_shapes=[pltpu.VMEM((S, LANES), BF16)] * 5 + [pltpu.VMEM((S, LANES), F32)] * 3
        + [pltpu.VMEM((3, S, LANES), F32)] * 2 + [pltpu.VMEM((NCHIP,) + w_out.shape, BF16)] + _WeightGather.SEMS,
        compiler_params=_cp(("arbitrary",)),
    )(proj, proj, proj, cos, sin, w_out)


def _att_bwd(dproj, d_att, att, lse, qr, kr, proj, cos, sin, gw_out4):
    out_units = [(j, j, 0) for j in range(NCHIP)]

    nblk = S // BLK

    def body(dp_in, do_ref, o_ref, lse_ref, qr_ref, kr_ref, v_ref, cos_ref, sin_ref, gw_ref, dp_ref, gout_ref,
             qd, kd0, kd1, vd0, vd1, dod, kt, packn, packd, dqd, dkcd, dkpd, dvcd, dvpd,
             dqn, dkn, dvn, tmp, rows, trs, pts, dss, stage, sems, gred, *rs_scratch):
        p = pl.program_id(0)
        rs = _ReduceScatter(gw_ref, gred, out_units, *rs_scratch)
        for step, piece in enumerate((rs.start_halves, rs.send_partials, rs.reduce_owned)):
            pl.when(p == step)(piece)

        @pl.when(p == NPAIR - 1)
        def _():
            rs.finish()
            gout_ref[...] = gred[...]

        lane = lax.broadcasted_iota(jnp.int32, (S, LANES), 1)
        hms = lane < HEAD
        prod = do_ref[...] * o_ref[...]
        d0 = jnp.sum(jnp.where(hms, prod, 0.0), axis=1, keepdims=True)
        d1 = jnp.sum(jnp.where(hms, 0.0, prod), axis=1, keepdims=True)
        lse = lse_ref[...]
        quarter = HEAD // 2
        packn[...] = jnp.where(lane < quarter, lse,
                               jnp.where(hms, pltpu.roll(lse, LANES - quarter, 1), jnp.where(lane < 3 * quarter, d0, d1)))
        dqn[...] = jnp.zeros_like(dqn)
        dkn[...] = jnp.zeros_like(dkn)
        dvn[...] = jnp.zeros_like(dvn)
        hm0 = lax.broadcasted_iota(jnp.int32, (BLK, LANES), 1) < HEAD
        key = lax.broadcasted_iota(jnp.int32, (2 * BLK, BLK), 0) & (BLK - 1)
        qry = lax.broadcasted_iota(jnp.int32, (2 * BLK, BLK), 1)
        mct, mpt = key <= qry, key >= qry

        for d in PATTERNS:
            nb = S // d // BLK
            _deint(qr_ref, qd, d, tmp)
            _deint_heads(kr_ref, kd0, kd1, d, tmp)
            _deint_heads(v_ref, vd0, vd1, d, tmp)
            _deint(do_ref, dod, d, tmp)
            _deint(packn, packd, d, tmp)

            sides = (0, 1) if nb > 1 else (0,)

            def probs(b, carry):
                st = pl.multiple_of(b * BLK, BLK)
                kt[b] = _two(kd0, kd1, st, 0).astype(F32).T.astype(BF16)
                trs[b] = packd[pl.ds(st, BLK), :].T
                for j in range(4):
                    rows[b, j:j + 1, :] = trs[b, j * quarter:j * quarter + 1, :]
                qb, dob = qd[pl.ds(st, BLK), :], dod[pl.ds(st, BLK), :]
                both = lambda j: jnp.concatenate([jnp.broadcast_to(rows[b, j:j + 1, :], (BLK, BLK)),
                                                  jnp.broadcast_to(rows[b, j + 1:j + 2, :], (BLK, BLK))], axis=0)
                lbt, dlt = both(0), both(2)
                for sd in sides:
                    stk = pl.multiple_of(jnp.maximum(b - sd, 0) * BLK, BLK)
                    mask = mct if sd == 0 else jnp.logical_and(mpt, lax.rem(b, nb) != 0)
                    k2, v2 = _two(kd0, kd1, stk, 0), _two(vd0, vd1, stk, 0)
                    pt = jnp.where(mask, jnp.exp(_dot_nt(k2, qb) - lbt), 0.0)
                    pts[b, sd] = pt.astype(BF16)
                    dss[b, sd] = (pt * (_dot_nt(v2, dob) - dlt)).astype(BF16)
                return carry

            lax.fori_loop(0, nblk, probs, 0, unroll=ATT_UNROLL)

            def prods(b, carry):
                st = pl.multiple_of(b * BLK, BLK)
                qb, dob = qd[pl.ds(st, BLK), :], dod[pl.ds(st, BLK), :]
                dq_t = None
                for sd in sides:
                    dst, ptb = dss[b, sd], pts[b, sd]
                    rk, rv = _dot(dst, qb), _dot(ptb, dob)
                    dqs = _dot(kt[jnp.maximum(b - sd, 0)], dst)
                    dq_t = dqs if dq_t is None else dq_t + dqs
                    dk, dv = (dkcd, dvcd) if sd == 0 else (dkpd, dvpd)
                    dk[pl.ds(st, BLK), :] = jnp.where(hm0, rk[0:BLK], rk[BLK:2 * BLK])
                    dv[pl.ds(st, BLK), :] = jnp.where(hm0, rv[0:BLK], rv[BLK:2 * BLK])
                dqd[pl.ds(st, BLK), :] = dq_t.T
                return carry

            lax.fori_loop(0, nblk, prods, 0, unroll=ATT_UNROLL)
            _reint(dqd, dqn, d, True, tmp)
            _reint(dkcd, dkn, d, True, tmp)
            _reint(dvcd, dvn, d, True, tmp)
            _reint_prev(dkpd, dkn, d)
            _reint_prev(dvpd, dvn, d)

        lane = lax.broadcasted_iota(jnp.int32, (S, LANES), 1)
        first = (lane & (HEAD // 2)) == 0
        cos, sin = cos_ref[...], sin_ref[...]
        dq = dqn[...] * (HEAD ** -0.5)
        dk = dkn[...]
        stage[0] = (dq * cos - _rot_half(dq, first) * sin).astype(BF16)
        stage[1] = (dk * cos - _rot_half(dk, first) * sin).astype(BF16)
        stage[2] = dvn[...].astype(BF16)
        copies = [pltpu.make_async_copy(stage.at[j], dp_ref.at[:, pl.ds((2 + j) * R + p * LANES, LANES)], sems.at[j])
                  for j in range(3)]
        for cp in copies:
            cp.start()
        for cp in copies:
            cp.wait()

    blk = pl.BlockSpec((S, LANES), lambda p: (0, p))
    tab = pl.BlockSpec((S, LANES), lambda p: (0, 0))
    vm = pl.BlockSpec(memory_space=pltpu.VMEM)
    _, orows, ocols = gw_out4.shape
    return pl.pallas_call(
        body, name="att_bwd", grid=(NPAIR,),
        in_specs=[pl.BlockSpec(memory_space=pl.ANY), blk, blk, blk, blk, blk,
                  pl.BlockSpec((S, LANES), lambda p: (0, VB + p)), tab, tab, vm],
        out_specs=[pl.BlockSpec(memory_space=pl.ANY), vm],
        out_shape=[jax.ShapeDtypeStruct((S, E), BF16), jax.ShapeDtypeStruct((orows, ocols), F32)],
        scratch_shapes=[pltpu.VMEM((S, LANES), BF16)] * 6 + [pltpu.VMEM((nblk, LANES, 2 * BLK), BF16)]
        + [pltpu.VMEM((S, LANES), F32)] * 11
        + [pltpu.VMEM((nblk, 8, BLK), F32), pltpu.VMEM((nblk, LANES, BLK), F32)]
        + [pltpu.VMEM((nblk, 2, 2 * BLK, BLK), BF16)] * 2
        + [pltpu.VMEM((3, S, LANES), BF16), pltpu.SemaphoreType.DMA((3,)), pltpu.VMEM((orows, ocols), F32)]
        + _ReduceScatter.scratch(NCHIP, orows, ocols, 1),
        input_output_aliases={0: 0},
        compiler_params=_cp(("arbitrary",)),
    )(dproj, d_att, att, lse, qr, kr, proj, cos, sin, gw_out4)


def _out_fwd_bwd(ya, att, proj, w_out_bf, x, target, mod, norm_post, norm_att):
    ts = 512

    def body(ya_ref, att_ref, gb_ref, w_ref, x_ref, t_ref, mod_ref, npost_ref, natt_ref,
             gx_ref, dya_ref, datt_ref, dgb_ref, gw_ref, acc_ref):
        i = pl.program_id(0)

        @pl.when(i == 0)
        def _():
            gw_ref[...] = jnp.zeros_like(gw_ref)
            acc_ref[...] = jnp.zeros_like(acc_ref)

        gate = mod_ref[:, 2 * D:3 * D]
        att = att_ref[...]
        gb = gb_ref[...]
        sg = _sigmoid(gb)
        silu = gb * sg
        ybp = att * silu
        yb, ybn, rstd_b = _rms_fwd(ybp, natt_ref[...])
        cat = jnp.concatenate([ya_ref[...], yb.astype(BF16)], axis=1)
        mix = _dot(cat, w_ref[...])
        rn, mn, rstd_m = _rms_fwd(mix, npost_ref[...])
        err = x_ref[...] + gate * rn - t_ref[...]
        dy = err * (1.0 / D)
        gx_ref[...] = dy
        dmix, dnpost = _rms_bwd(dy * gate, mn, rstd_m, npost_ref[...])
        dmb = dmix.astype(BF16)
        gw_ref[...] += _dot_tn(cat, dmb)
        dcat = _dot_nt(dmb, w_ref[...])
        dya_ref[...] = dcat[:, 0:R]
        dybp, dnatt = _rms_bwd(dcat[:, R:2 * R], ybn, rstd_b, natt_ref[...])
        datt_ref[...] = dybp * silu
        dgb_ref[...] = (dybp * att * (sg * (1.0 + gb * (1.0 - sg)))).astype(BF16)
        acc_ref[0:1, :] += jnp.sum(dy * rn, axis=0, keepdims=True)
        acc_ref[1:2, :] += dnpost
        acc_ref[2:3, 0:R] += dnatt
        acc_ref[3:4, :] += jnp.sum(jnp.sum(err * err, axis=1, keepdims=True), axis=0, keepdims=True)

    tile = lambda w: pl.BlockSpec((ts, w), lambda i: (i, 0))
    c0 = lambda shape: pl.BlockSpec(shape, lambda i: (0, 0))
    return pl.pallas_call(
        body, name="out_fwd_bwd", grid=(S // ts,),
        in_specs=[tile(R), tile(R), pl.BlockSpec((ts, R), lambda i: (i, 5)), c0((D, D)), tile(D), tile(D),
                  c0((1, 3 * D)), c0((1, D)), c0((1, R))],
        out_specs=[tile(D), tile(R), tile(R), pl.BlockSpec((ts, R), lambda i: (i, 5)), c0((D, D)), c0((8, D))],
        out_shape=[jax.ShapeDtypeStruct((S, D), F32), jax.ShapeDtypeStruct((S, R), F32),
                   jax.ShapeDtypeStruct((S, R), F32), jax.ShapeDtypeStruct((S, E), BF16),
                   jax.ShapeDtypeStruct((D, D), F32), jax.ShapeDtypeStruct((8, D), F32)],
        compiler_params=_cp(("arbitrary",)),
    )(ya, att, proj, w_out_bf, x, target, mod, norm_post, norm_att)


UC = 256
UPC = EC // UC


NU = E // UC


def _unit_of_step(i):
    return (i % NCHIP) * UPC + i // NCHIP


def _in_proj_bwd(ht, dproj, w_in_bf, x, gx1, mod, norm_pre, smalls):
    ts = 256
    nt = S // ts
    half = D // 2
    units = [_unit_of_step(k) for k in range(NU)]
    owners = [u // UPC for u in units]
    ns = len(smalls)

    def body(*refs):
        (ht_ref, dpu_ref, dp_ref, w_hbm, x_ref, gx1_ref, mod_ref, np_ref), refs = refs[:8], refs[8:]
        small_in, refs = refs[:ns], refs[ns:]
        (gx_ref, gin_ref), refs = refs[:2], refs[2:]
        small_out, (acc_out,), refs = refs[:ns], refs[ns:ns + 1], refs[ns + 1:]
        mine, sib, tmp, stage, got, red, acc_ref, hs, hr, ps, pr, bs, br = refs[:13]
        early = _SmallGather(small_in, small_out, *refs[13:16])
        late = _SmallGather([acc_ref], [acc_out], *refs[16:19])
        w_ref, w_sem = refs[19:21]
        i = pl.program_id(0)
        w_copy = pltpu.make_async_copy(w_hbm, w_ref, w_sem)
        pl.when(i == 0)(w_copy.start)
        pl.when(i == NU)(w_copy.wait)
        xx, yy, c = _me()
        ci = 2 * xx + yy
        r0 = pl.multiple_of(c * half, half)
        r1 = pl.multiple_of((1 - c) * half, half)
        pl.when(i == 0)(early.start)
        pl.when(i == NU)(early.forward)

        def exch(k):
            return _remote(tmp.at[k % 2], sib.at[k], hs.at[k], hr.at[k], 1)

        def partial(k, sender):
            return pltpu.make_async_remote_copy(
                src_ref=stage.at[k], dst_ref=got.at[units[k] % UPC, sender], send_sem=ps.at[k],
                recv_sem=pr.at[k, sender], device_id=(owners[k] // 2, owners[k] % 2, c), device_id_type=MESH)

        def back(k, start):
            off = (units[k] % UPC) * UC
            blk = red.at[pl.ds(start, half), off:off + UC]
            return _remote(blk, blk, bs.at[k], br.at[k], 1)

        for k in range(NU + 1):
            @pl.when(i == k)
            def _():
                if k < NU:
                    if k >= 2:
                        exch(k - 2).wait_send()
                    dpu = dpu_ref[...]
                    tmp[k % 2] = _dot(ht_ref[pl.ds(r1, half), :], dpu)
                    mine[k] = _dot(ht_ref[pl.ds(r0, half), :], dpu)
                    exch(k).start()
                if k >= 1:
                    exch(k - 1).wait_recv()
                    mine[k - 1] += sib[k - 1]

                    @pl.when(ci != owners[k - 1])
                    def _():
                        stage[k - 1] = mine[k - 1].astype(BF16)
                        partial(k - 1, ci).start()

        @pl.when(i == NU)
        def _():
            acc_ref[...] = jnp.zeros_like(acc_ref)

        @pl.when(i >= NU)
        def _():
            dh = sum(_dot_nt(dp_ref[:, j * EC:(j + 1) * EC], w_ref[j]) for j in range(NCHIP))
            hp, xn, rstd = _rms_fwd(x_ref[...], np_ref[...])
            dx, dnp = _rms_bwd(dh * (1.0 + mod_ref[:, D:2 * D]), xn, rstd, np_ref[...])
            gx_ref[...] = gx1_ref[...] + dx
            acc_ref[0:1, :] += jnp.sum(dh, axis=0, keepdims=True)
            acc_ref[1:2, :] += jnp.sum(dh * hp, axis=0, keepdims=True)
            acc_ref[2:3, :] += dnp

        for t in range(UPC):
            @pl.when(i == NU + 1 + 2 * t)
            def _():
                for k in range(NCHIP * t, NCHIP * (t + 1)):
                    @pl.when(ci == owners[k])
                    def _():
                        off = (units[k] % UPC) * UC
                        red[pl.ds(r0, half), off:off + UC] = mine[k]
                        for s in range(NCHIP):
                            if s != owners[k]:
                                partial(k, s).wait_recv()
                                red[pl.ds(r0, half), off:off + UC] += got[units[k] % UPC, s].astype(F32)
                        back(k, r0).start()

        @pl.when(i == NU + nt - 1)
        def _():
            late.start()
            exch(NU - 2).wait_send()
            exch(NU - 1).wait_send()
            for k in range(NU):
                @pl.when(ci == owners[k])
                def _():
                    back(k, r1).wait_recv()
                    back(k, r0).wait_send()

                @pl.when(ci != owners[k])
                def _():
                    partial(k, ci).wait_send()
            gin_ref[...] = red[...]
            early.finish()
            late.forward()
            late.finish()

    tile = lambda w: pl.BlockSpec((ts, w), lambda i: (jnp.maximum(i - NU, 0), 0))
    c0 = lambda shape: pl.BlockSpec(shape, lambda i: (0, 0))
    vm = pl.BlockSpec(memory_space=pltpu.VMEM)
    hbm = pl.BlockSpec(memory_space=pl.ANY)
    gathered = [jax.ShapeDtypeStruct((NDEV,) + a.shape, a.dtype) for a in smalls] + [jax.ShapeDtypeStruct((NDEV, 8, D), F32)]
    return pl.pallas_call(
        body, name="in_proj_bwd", grid=(NU + nt,),
        in_specs=[vm, pl.BlockSpec((S, UC), lambda i: (0, _unit_of_step(jnp.minimum(i, NU - 1)))), tile(E),
                  hbm, tile(D), tile(D), c0((1, 3 * D)), c0((1, D))] + [vm] * ns,
        out_specs=[tile(D), vm] + [hbm] * (ns + 1),
        out_shape=[jax.ShapeDtypeStruct((S, D), F32), jax.ShapeDtypeStruct((D, EC), F32)] + gathered,
        scratch_shapes=[pltpu.VMEM((NU, half, UC), F32), pltpu.VMEM((NU, half, UC), F32),
                        pltpu.VMEM((2, half, UC), F32), pltpu.VMEM((NU, half, UC), BF16),
                        pltpu.VMEM((UPC, NCHIP, half, UC), BF16), pltpu.VMEM((D, EC), F32), pltpu.VMEM((8, D), F32),
                        pltpu.SemaphoreType.DMA((NU,)), pltpu.SemaphoreType.DMA((NU,)),
                        pltpu.SemaphoreType.DMA((NU,)), pltpu.SemaphoreType.DMA((NU, NCHIP)),
                        pltpu.SemaphoreType.DMA((NU,)), pltpu.SemaphoreType.DMA((NU,))]
        + _SmallGather.sems(ns) + _SmallGather.sems(1)
        + [pltpu.VMEM((NCHIP, D, EC), BF16), pltpu.SemaphoreType.DMA],
        compiler_params=_cp(("arbitrary",)),
    )(ht, dproj, dproj, w_in_bf, x, gx1, mod, norm_pre, *smalls)


def _local_step(x, cos, sin, target, mod, w_in_bf, proj, ht, w_out, conv_w, p):
    rec_p = (conv_w, p["conv_b"], p["w_rg_a"], p["b_rg_a"], p["w_rg_x"], p["b_rg_x"], p["lru_lambda"], p["norm_rec"])
    h_all, ya = _rec_fwd(proj, *rec_p)
    att, qr, kr, lse, w_out_bf = _att_fwd(proj, cos, sin, w_out)
    gx1, d_ya, d_att, dproj, gw_out, acc_o = _out_fwd_bwd(ya, att, proj, w_out_bf.reshape(D, D), x, target, mod,
                                                           p["norm_post"], p["norm_att"])
    dproj, g_out = _att_bwd(dproj, d_att, att, lse, qr, kr, proj, cos, sin, gw_out.reshape(NCHIP, D // NCHIP, D))
    dproj, dwa, dwx, sm = _rec_bwd(dproj, d_ya, proj, h_all, *rec_p)
    grad_x, g_in, *gathered = _in_proj_bwd(ht, dproj, w_in_bf, x, gx1, mod, p["norm_pre"], [acc_o, sm, dwa, dwx])
    return grad_x, g_in, g_out, gathered


def _me():
    return lax.axis_index("x"), lax.axis_index("y"), lax.axis_index("c")


def _flip(v, bit):
    return 1 - v if bit else v


def _peer(rel):
    x, y, c = _me()
    return (_flip(x, rel & 4), _flip(y, rel & 2), _flip(c, rel & 1))


def _remote(src, dst, send_sem, recv_sem, rel):
    return pltpu.make_async_remote_copy(src_ref=src, dst_ref=dst, send_sem=send_sem, recv_sem=recv_sem,
                                        device_id=_peer(rel), device_id_type=MESH)


class _WeightGather:
    SEMS = [pltpu.SemaphoreType.DMA((NCHIP - 1,))] * 4

    def __init__(self, w_ref, out_ref, send_sems, recv_sems, fsend_sems, frecv_sems):
        x, y, c = _me()
        self.w, self.out, self.ci = w_ref, out_ref, 2 * x + y
        self.half = w_ref.shape[0] // 2
        self.r0 = pl.multiple_of(c * self.half, self.half)
        self.r1 = pl.multiple_of((1 - c) * self.half, self.half)
        self.sems = (send_sems, recv_sems, fsend_sems, frecv_sems)

    def _ici(self, chip, k):
        blk = self.out.at[chip, pl.ds(self.r0, self.half), :]
        return _remote(blk, blk, self.sems[0].at[k - 1], self.sems[1].at[k - 1], 2 * k)

    def _d2d(self, chip, start, k):
        blk = self.out.at[chip, pl.ds(start, self.half), :]
        return _remote(blk, blk, self.sems[2].at[k - 1], self.sems[3].at[k - 1], 1)

    def start(self, diagonal=True):
        self.out[self.ci] = self.w[...].astype(BF16)
        for k in range(1, NCHIP if diagonal else NCHIP - 1):
            self._ici(self.ci, k).start()

    def _relay(self, chip, piece, k):
        q = self.half // 2
        blk = self.out.at[chip, pl.ds(self.r0 + piece * q, q), :]
        return _remote(blk, blk, self.relay_sems[0].at[piece], self.relay_sems[1].at[piece], 2 * k)

    def neighbours_landed(self, relay_send_sems, relay_recv_sems):
        self.relay_sems = (relay_send_sems, relay_recv_sems)
        for k in (1, 2):
            self._ici(self.ci ^ k, k).wait_recv()
        self._relay(self.ci ^ 2, 0, 1).start()
        self._relay(self.ci ^ 1, 1, 2).start()
        for k in (1, 2):
            self._d2d(self.ci ^ k, self.r0, k).start()

    def sibling_landed(self, k):
        self._d2d(self.ci ^ k, self.r1, k).wait_recv()

    def diagonal_landed(self):
        for piece, k in ((0, 1), (1, 2)):
            self._relay(self.ci ^ 3, piece, k).wait_recv()
        self._d2d(self.ci ^ 3, self.r0, 3).start()
        self._d2d(self.ci ^ 3, self.r1, 3).wait_recv()

    def finish_relayed(self):
        for k in (1, 2):
            self._ici(self.ci, k).wait_send()
        self._relay(self.ci ^ 2, 0, 1).wait_send()
        self._relay(self.ci ^ 1, 1, 2).wait_send()
        for k in range(1, NCHIP):
            self._d2d(self.ci ^ k, self.r0, k).wait_send()

    def forward(self):
        for k in range(1, NCHIP):
            self._ici(self.ci ^ k, k).wait_recv()
            self._d2d(self.ci ^ k, self.r0, k).start()

    def finish(self):
        for k in range(1, NCHIP):
            self._d2d(self.ci ^ k, self.r1, k).wait_recv()
        self.finish_sends()

    def finish_sends(self):
        for k in range(1, NCHIP):
            self._ici(self.ci, k).wait_send()
            self._d2d(self.ci ^ k, self.r0, k).wait_send()


class _SmallGather:
    @staticmethod
    def sems(n):
        return [pltpu.SemaphoreType.DMA((n, 7)), pltpu.SemaphoreType.DMA((n, 7)), pltpu.SemaphoreType.DMA((n,))]

    def __init__(self, srcs, outs, send_sems, recv_sems, local_sems):
        x, y, c = _me()
        self.srcs, self.outs = list(srcs), list(outs)
        self.ss, self.rs, self.ls = send_sems, recv_sems, local_sems
        self.ci, self.c = 2 * x + y, c
        self.me = 2 * self.ci + c

    def _own(self, a, slot, rel):
        return _remote(self.srcs[a], self.outs[a].at[self.me], self.ss.at[a, slot], self.rs.at[a, slot], rel)

    def _block(self, a, idx, slot, rel):
        blk = self.outs[a].at[idx]
        return _remote(blk, blk, self.ss.at[a, slot], self.rs.at[a, slot], rel)

    def _local(self, a):
        return pltpu.make_async_copy(self.srcs[a], self.outs[a].at[self.me], self.ls.at[a])

    def start(self):
        for a in range(len(self.srcs)):
            self._local(a).start()
            self._own(a, 0, 1).start()
            for k in range(1, NCHIP):
                self._own(a, k, 2 * k).start()

    def forward(self):
        for a in range(len(self.srcs)):
            for k in range(1, NCHIP):
                idx = 2 * (self.ci ^ k) + self.c
                self._block(a, idx, k, 2 * k).wait_recv()
                self._block(a, idx, 3 + k, 1).start()

    def finish(self):
        for a in range(len(self.srcs)):
            self._block(a, 2 * self.ci + 1 - self.c, 0, 1).wait_recv()
            for k in range(1, NCHIP):
                self._block(a, 2 * (self.ci ^ k) + 1 - self.c, 3 + k, 1).wait_recv()
            self._own(a, 0, 1).wait_send()
            for k in range(1, NCHIP):
                self._own(a, k, 2 * k).wait_send()
                self._block(a, 2 * (self.ci ^ k) + self.c, 3 + k, 1).wait_send()
            self._local(a).wait()


def _start_in_proj(crow, w_ada, b_cols, w_in, pos, x, norm_pre, order):
    ts = 512
    nt = S // ts
    wc = crow.shape[1]

    def body(order_ref, crow_ref, wada_ref, b_ref, win_ref, pos_ref, freq_ref, x_ref, np_ref,
             g0_ref, mod_ref, wbf_ref, cos_ref, sin_ref, proj_ref, ht_ref,
             g0s, modp, modb, wbuf, hb_all, cs, cr, ms, mr, ws, wr, fs, fr, local_sems, ys, yr, osem):
        s, t = pl.program_id(0), pl.program_id(1)
        x, y, c = _me()
        ci = 2 * x + y
        me = 2 * ci + c
        wg = _WeightGather(win_ref, wbuf, ws, wr, fs, fr)

        @pl.when(jnp.logical_and(s == 0, t == 0))
        def _():
            wg.start(diagonal=False)
            mine = pltpu.make_async_copy(crow_ref, g0s.at[pl.ds(me, 1), :], local_sems.at[0])
            mine.start()
            csend = [_remote(crow_ref, g0s.at[pl.ds(me, 1), :], cs.at[r - 1], cr.at[r - 1], r) for r in range(1, NDEV)]
            for cp in csend:
                cp.start()
            cos_ref[...], sin_ref[...] = _cos_sin(pos_ref, freq_ref)
            for r in range(1, NDEV):
                px, py, pc = _peer(r)
                _remote(crow_ref, g0s.at[pl.ds(4 * px + 2 * py + pc, 1), :], cs.at[r - 1], cr.at[r - 1], r).wait_recv()
            mine.wait()
            cv = g0s[:, 0:D]
            sc = cv * _sigmoid(cv)
            scb = jnp.concatenate([sc, jnp.zeros_like(sc)], axis=0).astype(BF16)
            modp[...] = _dot(scb, wada_ref[...].astype(BF16))[0:NDEV, :] + b_ref[...]
            own = pltpu.make_async_copy(modp.at[pl.ds(me, 1), :], modb.at[ci], local_sems.at[1])
            own.start()
            msend = []
            for k in range(1, NCHIP):
                cp = _remote(modp.at[pl.ds(2 * (ci ^ k) + c, 1), :], modb.at[ci], ms.at[k - 1], mr.at[k - 1], 2 * k)
                cp.start()
                msend.append(cp)
            for k in range(1, NCHIP):
                _remote(modp.at[pl.ds(me, 1), :], modb.at[ci ^ k], ms.at[k - 1], mr.at[k - 1], 2 * k).wait_recv()
            own.wait()
            for j in range(NCHIP):
                mod_ref[:, j * EC:(j + 1) * EC] = modb[j]
            for cp in csend + msend:
                cp.wait_send()
            g0_ref[...] = g0s[...]

        def keep(k):
            return pltpu.make_async_copy(wbuf.at[ci ^ k], wbf_ref.at[ci ^ k], osem.at[k])

        @pl.when(jnp.logical_and(s == 1, t == 0))
        def _():
            keep(0).start()
            wg.neighbours_landed(ys, yr)
            wg.sibling_landed(1)
            keep(1).start()

        @pl.when(jnp.logical_and(s == 2, t == 0))
        def _():
            wg.sibling_landed(2)
            keep(2).start()

        @pl.when(jnp.logical_and(s == 3, t == 0))
        def _():
            wg.relay_sems = (ys, yr)
            wg.diagonal_landed()
            keep(3).start()

        rows = pl.ds(pl.multiple_of(t * ts, ts), ts)

        @pl.when(s == 0)
        def _():
            hp, _, _ = _rms_fwd(x_ref[...], np_ref[...])
            h = hp * (1.0 + mod_ref[:, D:2 * D]) + mod_ref[:, 0:D]
            hb_all[rows, :] = h.astype(BF16)
            ht_ref[...] = h.T.astype(BF16)

        proj_ref[...] = _dot(hb_all[rows, :], wbuf[ci ^ s])

        @pl.when(jnp.logical_and(s == NCHIP - 1, t == nt - 1))
        def _():
            wg.relay_sems = (ys, yr)
            wg.finish_relayed()
            for k in range(NCHIP):
                keep(k).wait()

    vm = pl.BlockSpec(memory_space=pltpu.VMEM)
    first_pass = lambda s, t: jnp.where(s == 0, t, nt - 1)
    grid_spec = pltpu.PrefetchScalarGridSpec(
        num_scalar_prefetch=1, grid=(NCHIP, nt),
        in_specs=[vm, vm, vm, vm, vm, vm, pl.BlockSpec((ts, D), lambda s, t, o: (first_pass(s, t), 0)),
                  pl.BlockSpec((1, D), lambda s, t, o: (0, 0))],
        out_specs=[vm, vm, pl.BlockSpec(memory_space=pl.ANY), vm, vm, pl.BlockSpec((ts, EC), lambda s, t, o: (t, o[s])),
                   pl.BlockSpec((D, ts), lambda s, t, o: (0, first_pass(s, t)))],
        scratch_shapes=[pltpu.VMEM((NDEV, wc), F32), pltpu.VMEM((NDEV, EC), F32), pltpu.VMEM((NCHIP, 1, EC), F32),
                        pltpu.VMEM((NCHIP, D, EC), BF16), pltpu.VMEM((S, D), BF16),
                        pltpu.SemaphoreType.DMA((NDEV - 1,)), pltpu.SemaphoreType.DMA((NDEV - 1,)),
                        pltpu.SemaphoreType.DMA((NCHIP - 1,)), pltpu.SemaphoreType.DMA((NCHIP - 1,))]
        + _WeightGather.SEMS + [pltpu.SemaphoreType.DMA((2,))] * 3 + [pltpu.SemaphoreType.DMA((NCHIP,))])
    return pl.pallas_call(
        body, name="start_in_proj", grid_spec=grid_spec,
        out_shape=[jax.ShapeDtypeStruct((NDEV, wc), F32), jax.ShapeDtypeStruct((1, 3 * D), F32),
                   jax.ShapeDtypeStruct((NCHIP, D, EC), BF16), jax.ShapeDtypeStruct((S, LANES), F32),
                   jax.ShapeDtypeStruct((S, LANES), F32), jax.ShapeDtypeStruct((S, E), F32),
                   jax.ShapeDtypeStruct((D, S), BF16)],
        compiler_params=_cp(("arbitrary", "arbitrary")),
    )(order, crow, w_ada, b_cols, w_in, pos, _rope_freq(), x, norm_pre)


class _ReduceScatter:
    @staticmethod
    def scratch(n_units, rows, ucols, max_owned):
        half = rows // 2
        return [pltpu.VMEM((n_units, half, ucols), F32), pltpu.VMEM((n_units, half, ucols), BF16),
                pltpu.VMEM((max_owned, NCHIP, half, ucols), BF16),
                pltpu.SemaphoreType.DMA((2,)), pltpu.SemaphoreType.DMA((n_units,)),
                pltpu.SemaphoreType.DMA((n_units, NCHIP)), pltpu.SemaphoreType.DMA((n_units,)),
                pltpu.SemaphoreType.DMA((n_units,))]

    def __init__(self, g_ref, out_ref, units, sib, stage, got, sem1, send2, recv2, send3, recv3):
        x, y, c = _me()
        self.c, self.ci = c, 2 * x + y
        self.g, self.out, self.units = g_ref, out_ref, units
        self.sib, self.stage, self.got = sib, stage, got
        self.sem1, self.send2, self.recv2, self.send3, self.recv3 = sem1, send2, recv2, send3, recv3
        self.half = g_ref.shape[1] // 2
        self.ucols = g_ref.shape[2]
        self.r0 = pl.multiple_of(c * self.half, self.half)
        self.r1 = pl.multiple_of((1 - c) * self.half, self.half)
        self.slot0 = units[0][0]
        assert [u[0] for u in units] == list(range(self.slot0, self.slot0 + len(units)))
        seen = {}
        self.local = []
        for _, owner, _ in units:
            self.local.append(seen.get(owner, 0))
            seen[owner] = seen.get(owner, 0) + 1

    def _halves(self):
        n = len(self.units)
        return _remote(self.g.at[pl.ds(self.slot0, n), pl.ds(self.r1, self.half), :], self.sib,
                       self.sem1.at[0], self.sem1.at[1], 1)

    def _partial(self, i, sender):
        _, owner, _ = self.units[i]
        return pltpu.make_async_remote_copy(
            src_ref=self.stage.at[i], dst_ref=self.got.at[self.local[i], sender],
            send_sem=self.send2.at[i], recv_sem=self.recv2.at[i, sender],
            device_id=(owner // 2, owner % 2, self.c), device_id_type=MESH)

    def _back(self, i, start):
        off = self.units[i][2]
        blk = self.out.at[pl.ds(start, self.half), off:off + self.ucols]
        return _remote(blk, blk, self.send3.at[i], self.recv3.at[i], 1)

    def start_halves(self):
        self._halves().start()

    def send_partials(self):
        self._halves().wait_recv()
        for i, (slot, owner, _) in enumerate(self.units):
            @pl.when(self.ci != owner)
            def _():
                self.stage[i] = (self.g[slot, pl.ds(self.r0, self.half), :] + self.sib[i]).astype(BF16)
                self._partial(i, self.ci).start()

    def reduce_owned(self):
        for i, (slot, owner, off) in enumerate(self.units):
            @pl.when(self.ci == owner)
            def _():
                rows, cols = pl.ds(self.r0, self.half), slice(off, off + self.ucols)
                self.out[rows, cols] = self.g[slot, pl.ds(self.r0, self.half), :] + self.sib[i]
                for s in range(NCHIP):
                    if s != owner:
                        self._partial(i, s).wait_recv()
                        self.out[rows, cols] += self.got[self.local[i], s].astype(F32)
                self._back(i, self.r0).start()

    def finish(self):
        self._halves().wait_send()
        for i, (_, owner, _) in enumerate(self.units):
            @pl.when(self.ci == owner)
            def _():
                self._back(i, self.r1).wait_recv()
                self._back(i, self.r0).wait_send()

            @pl.when(self.ci != owner)
            def _():
                self._partial(i, self.ci).wait_send()


def _silu_rows(c_ref):
    cv = c_ref[...]
    sc = cv * _sigmoid(cv)
    return jnp.concatenate([sc, jnp.zeros_like(sc)], axis=0).astype(BF16)


def _adamw(w, g, m, v, name):
    rows, cols = w.shape
    tr = 256 if rows % 256 == 0 else rows

    def body(w_ref, g_ref, m_ref, v_ref, d_ref, nm_ref, nv_ref):
        d_ref[...], nm_ref[...], nv_ref[...] = _adamw_values(w_ref[...], g_ref[...], m_ref[...], v_ref[...])

    spec = pl.BlockSpec((tr, cols), lambda i: (i, 0))
    return pl.pallas_call(
        body, name=name, grid=(rows // tr,), in_specs=[spec] * 4, out_specs=[spec] * 3,
        out_shape=[jax.ShapeDtypeStruct((rows, cols), F32)] * 3,
        compiler_params=_cp(("parallel",)),
    )(w, g, m, v)


def _adamw_values(w, g, m, v):
    nm = B1 * m + (1.0 - B1) * g
    nv = B2 * v + (1.0 - B2) * (g * g)
    m_hat = nm / (1.0 - B1 ** STEP)
    v_hat = nv / (1.0 - B2 ** STEP)
    return (-LR) * (m_hat / (jnp.sqrt(v_hat) + ADAM_EPS) + WD * w), nm, nv


NB = R // HEAD
SMALL = (("b_ada", (1, 3 * D)), ("norm_pre", (1, D)), ("norm_post", (1, D)), ("conv_w", (4, R // NCHIP)),
         ("conv_b", (1, R)), ("w_rg_a", (NB, HEAD, HEAD)), ("b_rg_a", (1, R)), ("w_rg_x", (NB, HEAD, HEAD)),
         ("b_rg_x", (1, R)), ("lru_lambda", (1, R)), ("norm_rec", (1, R)), ("norm_att", (1, R)))


def _small_update(ao8, sm8, dwa8, dwx8, ai8, cg, params):
    n = len(SMALL)

    def body(ao_ref, sm_ref, dwa_ref, dwx_ref, ai_ref, cg_ref, *refs):
        pin, pout, (gada_ref, loss_ref, dmod) = refs[:3 * n], refs[3 * n:7 * n], refs[7 * n:]
        xx, yy, _ = _me()
        ci = 2 * xx + yy

        def total(ref, *idx):
            acc = ref[(0,) + idx].astype(F32)
            for d in range(1, NDEV):
                acc = acc + ref[(d,) + idx].astype(F32)
            return acc

        row = lambda ref, r, lanes=slice(None): total(ref, slice(r, r + 1), lanes)
        mine = lambda parts: sum(jnp.where(ci == j, part, 0.0) for j, part in enumerate(parts))
        cw = R // NCHIP
        grads = {
            "b_ada": [jnp.concatenate([row(ai_ref, 0), row(ai_ref, 1), row(ao_ref, 0)], axis=1)],
            "norm_pre": [row(ai_ref, 2)], "norm_post": [row(ao_ref, 1)],
            "conv_w": [mine([row(sm_ref, 8 + r, slice(j * cw, (j + 1) * cw)) for j in range(NCHIP)]) for r in range(4)],
            "conv_b": [row(sm_ref, 4)], "b_rg_a": [row(sm_ref, 0)], "b_rg_x": [row(sm_ref, 1)],
            "lru_lambda": [row(sm_ref, 2)], "norm_rec": [row(sm_ref, 3)], "norm_att": [row(ao_ref, 2, slice(0, R))],
            "w_rg_a": [total(dwa_ref, h) for h in range(NB)], "w_rg_x": [total(dwx_ref, h) for h in range(NB)],
        }
        loss_ref[...] = row(ao_ref, 3, slice(0, LANES)) * (0.5 / D)
        for k, (name, shape) in enumerate(SMALL):
            w_ref, m_ref, v_ref = pin[3 * k:3 * k + 3]
            outs = pout[4 * k:4 * k + 4]
            for r, g in enumerate(grads[name]):
                at = (slice(None),) if len(grads[name]) == 1 else ((r,) if len(shape) == 3 else (slice(r, r + 1),))
                res = (g,) + _adamw_values(w_ref[at], g, m_ref[at], v_ref[at])
                for o_ref, val in zip(outs, res):
                    o_ref[at] = val
        for d in range(NDEV):
            dmod[d:d + 1, :] = jnp.concatenate([ai_ref[d, 0:1, :], ai_ref[d, 1:2, :], ao_ref[d, 0:1, :]], axis=1)
        cols = mine([dmod[:, j * EC:(j + 1) * EC] for j in range(NCHIP)])
        colsb = jnp.concatenate([cols, jnp.zeros_like(cols)], axis=0).astype(BF16)
        gada_ref[...] = _dot_tn(_silu_rows(cg_ref), colsb)

    shapes = [jax.ShapeDtypeStruct(s, F32) for _, s in SMALL]
    outs = pl.pallas_call(
        body, name="small_update",
        out_shape=[s for s in shapes for _ in range(4)] + [jax.ShapeDtypeStruct((D, EC), F32),
                                                           jax.ShapeDtypeStruct((1, LANES), F32)],
        scratch_shapes=[pltpu.VMEM((NDEV, 3 * D), F32)],
        compiler_params=_cp(),
    )(ao8, sm8, dwa8, dwx8, ai8, cg, *params)
    return outs[:4 * n], outs[4 * n], outs[4 * n + 1]


BIG = ("w_ada", "w_in", "w_out")
WEIGHTS = ("w_ada", "b_ada", "norm_pre", "norm_post", "w_in", "conv_w", "conv_b", "w_rg_a", "b_rg_a", "w_rg_x",
           "b_rg_x", "lru_lambda", "norm_rec", "norm_att", "w_out")


def kernel(x, c, positions, w_ada, b_ada, norm_pre, norm_post, w_in, conv_w, conv_b, w_rg_a, b_rg_a, w_rg_x, b_rg_x, lru_lambda, norm_rec, norm_att, w_out, loss_target, m_w_ada, m_b_ada, m_norm_pre, m_norm_post, m_w_in, m_conv_w, m_conv_b, m_w_rg_a, m_b_rg_a, m_w_rg_x, m_b_rg_x, m_lru_lambda, m_norm_rec, m_norm_att, m_w_out, v_w_ada, v_b_ada, v_norm_pre, v_norm_post, v_w_in, v_conv_w, v_conv_b, v_w_rg_a, v_b_rg_a, v_w_rg_x, v_b_rg_x, v_lru_lambda, v_norm_rec, v_norm_att, v_w_out):
    given = dict(locals())
    wts = {n: given[n] for n in WEIGHTS}
    ms = {n: given["m_" + n] for n in WEIGHTS}
    vs = {n: given["v_" + n] for n in WEIGHTS}
    xi, yi, _ = _me()
    chip = 2 * xi + yi
    cw_loc = R // NCHIP

    b_cols = lax.dynamic_slice(b_ada, (0, chip * EC), (1, EC))
    order = (chip ^ jnp.arange(NCHIP, dtype=jnp.int32)).astype(jnp.int32)
    g0, mod, w_in_bf, cos, sin, proj, ht = _start_in_proj(
        jnp.concatenate([c, conv_w.reshape(1, 4 * cw_loc)], axis=1), w_ada[0], b_cols, w_in[0],
        positions.reshape(S, 1), x[0], norm_pre, order)
    cg = g0[:, 0:D]
    conv_full = g0[0::2, D:].reshape(NCHIP, 4, cw_loc).transpose(1, 0, 2).reshape(4, R)

    p = dict(norm_pre=norm_pre, norm_post=norm_post, conv_b=conv_b, b_rg_a=b_rg_a, b_rg_x=b_rg_x,
             lru_lambda=lru_lambda, norm_rec=norm_rec, norm_att=norm_att, w_rg_a=w_rg_a[0], w_rg_x=w_rg_x[0])
    grad_x, g_in, g_out, gathered = _local_step(
        x[0], cos, sin, loss_target[0], mod, w_in_bf, proj, ht, w_out[0], conv_full, p)

    params = [d[n].reshape(shape) for n, shape in SMALL for d in (wts, ms, vs)]
    small_out, g_ada, loss_row = _small_update(*gathered, cg, params)
    grads = {"w_out": g_out, "w_in": g_in, "w_ada": g_ada}
    delta, new_m, new_v = {}, {}, {}
    for k, (n, _) in enumerate(SMALL):
        grads[n], delta[n], new_m[n], new_v[n] = small_out[4 * k:4 * k + 4]
    for n in BIG:
        delta[n], new_m[n], new_v[n] = _adamw(wts[n][0], grads[n], ms[n][0], vs[n][0], "adamw_" + n)
    out = lambda d: [d[n].reshape(wts[n].shape) for n in WEIGHTS]
    return (loss_row[0, 0], grad_x.reshape(x.shape), *out(grads), *out(delta), *out(new_m), *out(new_v))
```

```python
import numpy as np
import jax
import jax.numpy as jnp
from jax import lax
from jax.experimental import pallas as pl
from jax.experimental.pallas import tpu as pltpu

F32 = jnp.float32
BF16 = jnp.bfloat16

S = 2048
D = 1024
E = 3072
R = 512
NDEV = 8
NCHIP = 4
EC = 768
LRU_C = 8.0
EPS = 1e-6
NEG = -1e30
HEAD = 64
BLK = 128
PATTERNS = (1, 4, 16)
ROPE_THETA = 10000.0
LANES = 128
VMEM_LIMIT = 56 * 1024 * 1024

B1, B2, LR, WD, ADAM_EPS, STEP = 0.9, 0.999, 0.001, 0.01, 1e-8, 10
MESH = pl.DeviceIdType.MESH


def _cp(sem=None, **kw):
    return pltpu.CompilerParams(dimension_semantics=sem, vmem_limit_bytes=VMEM_LIMIT, **kw)


def _dot(a, b):
    return jnp.dot(a, b, preferred_element_type=F32)


def _dot_nt(a, b):
    return lax.dot_general(a, b, (((1,), (1,)), ((), ())), preferred_element_type=F32)


def _dot_tn(a, b):
    return lax.dot_general(a, b, (((0,), (0,)), ((), ())), preferred_element_type=F32)


def _sigmoid(x):
    return 1.0 / (1.0 + jnp.exp(-x))


def _one_minus_exp(x, ex):
    poly = -x * (1.0 + x * (0.5 + x * (1.0 / 6 + x * (1.0 / 24))))
    return jnp.where(x > -1.0 / 16, poly, 1.0 - ex)


def _rms_fwd(v, g):
    rstd = lax.rsqrt(jnp.mean(v * v, axis=-1, keepdims=True) + EPS)
    vn = v * rstd
    return vn * g, vn, rstd


def _rms_bwd(dy, vn, rstd, g):
    dvn = dy * g
    dv = rstd * (dvn - vn * jnp.mean(dvn * vn, axis=-1, keepdims=True))
    return dv, jnp.sum(dy * vn, axis=0, keepdims=True)


RT = 256


def _shift_down(cur, prev8, j, row):
    if j == 0:
        return cur
    rolled = pltpu.roll(cur, j, 0)
    top = jnp.where(row[0:8] >= j, rolled[0:8], pltpu.roll(prev8, j, 0))
    return jnp.concatenate([top, rolled[8:]], axis=0)


def _shift_up(cur, next8, j, row):
    if j == 0:
        return cur
    rolled = pltpu.roll(cur, RT - j, 0)
    bot = jnp.where(row[RT - 8:] < RT - j, rolled[RT - 8:], pltpu.roll(next8, 8 - j, 0))
    return jnp.concatenate([rolled[:RT - 8], bot], axis=0)


def _rec_gates(xp, xprev8, row, cw_ref, cb_ref, wa_ref, ba_ref, wx_ref, bx_ref, lam_ref):
    xa = cb_ref[...] + sum(cw_ref[3 - j:4 - j, :] * _shift_down(xp, xprev8, j, row) for j in range(4))
    xab = xa.astype(BF16)
    r = _sigmoid(_dot(xab, wa_ref[...]) + ba_ref[...])
    ig = _sigmoid(_dot(xab, wx_ref[...]) + bx_ref[...])
    nl = -lam_ref[...]
    sp = jnp.maximum(nl, 0.0) + jnp.log1p(jnp.exp(-jnp.abs(nl)))
    la = (-LRU_C) * r * sp
    a = jnp.exp(la)
    mult = jnp.sqrt(_one_minus_exp(2.0 * la, a * a))
    return dict(xa=xa, xab=xab, r=r, ig=ig, sp=sp, la=la, a=a, mult=mult)


def _scan_fwd(a, u, row):
    sh = 1
    while sh < RT:
        a_s = jnp.where(row >= sh, pltpu.roll(a, sh, 0), 1.0)
        u_s = jnp.where(row >= sh, pltpu.roll(u, sh, 0), 0.0)
        u = a * u_s + u
        a = a * a_s
        sh *= 2
    return a, u


def _scan_bwd(al, g, row):
    sh = 1
    while sh < RT:
        al_s = jnp.where(row < RT - sh, pltpu.roll(al, RT - sh, 0), 1.0)
        g_s = jnp.where(row < RT - sh, pltpu.roll(g, RT - sh, 0), 0.0)
        g = g + al * g_s
        al = al * al_s
        sh *= 2
    return g


def _dense_from_blocks(blocks_ref, dense_ref):
    dense_ref[...] = jnp.zeros_like(dense_ref)
    for h in range(R // HEAD):
        dense_ref[h * HEAD:(h + 1) * HEAD, h * HEAD:(h + 1) * HEAD] = blocks_ref[h].astype(dense_ref.dtype)


def _rec_fwd(proj, conv_w, conv_b, wa_b, ba, wx_b, bx, lam, norm_rec):
    nt = S // RT

    def body(p_ref, cw_ref, cb_ref, wa_ref, ba_ref, wx_ref, bx_ref, lam_ref, nr_ref,
             h_ref, ya_ref, prev8, hc, wad, wxd):
        i = pl.program_id(0)

        @pl.when(i == 0)
        def _():
            prev8[...] = jnp.zeros_like(prev8)
            hc[...] = jnp.zeros_like(hc)
            _dense_from_blocks(wa_ref, wad)
            _dense_from_blocks(wx_ref, wxd)

        row = lax.broadcasted_iota(jnp.int32, (RT, R), 0)
        xp = p_ref[:, 0:R]
        ga = p_ref[:, R:2 * R]
        f = _rec_gates(xp, prev8[...], row, cw_ref, cb_ref, wad, ba_ref, wxd, bx_ref, lam_ref)
        u = f["mult"] * (f["ig"] * f["xa"])
        acum, hh = _scan_fwd(f["a"], u, row)
        h = hh + acum * hc[0:1, :]
        h_ref[...] = h
        hc[0:1, :] = h_ref[RT - 1:RT, :]
        prev8[...] = p_ref[RT - 8:RT, 0:R]
        yp = h * (ga * _sigmoid(ga))
        ya, _, _ = _rms_fwd(yp, nr_ref[...])
        ya_ref[...] = ya.astype(BF16)

    row1 = lambda n: pl.BlockSpec((1, n), lambda i: (0, 0))
    blocks = pl.BlockSpec((R // HEAD, HEAD, HEAD), lambda i: (0, 0, 0))
    return pl.pallas_call(
        body, name="rec_fwd", grid=(nt,),
        in_specs=[pl.BlockSpec((RT, 2 * R), lambda i: (i, 0)), pl.BlockSpec((4, R), lambda i: (0, 0)), row1(R),
                  blocks, row1(R), blocks, row1(R), row1(R), row1(R)],
        out_specs=[pl.BlockSpec((RT, R), lambda i: (i, 0)), pl.BlockSpec((RT, R), lambda i: (i, 0))],
        out_shape=[jax.ShapeDtypeStruct((S, R), F32), jax.ShapeDtypeStruct((S, R), BF16)],
        scratch_shapes=[pltpu.VMEM((8, R), F32), pltpu.VMEM((8, R), F32), pltpu.VMEM((R, R), BF16),
                        pltpu.VMEM((R, R), BF16)],
        compiler_params=_cp(("arbitrary",)),
    )(proj, conv_w, conv_b, wa_b, ba, wx_b, bx, lam, norm_rec)


def _rec_bwd(dproj, d_ya, proj, h_all, conv_w, conv_b, wa_b, ba, wx_b, bx, lam, norm_rec):
    nt = S // RT

    def body(dp_in, dya_ref, p_ref, pprev_ref, h_ref, hprev_ref, cw_ref, cb_ref, wab_ref, ba_ref, wxb_ref, bx_ref,
             lam_ref, nr_ref, dp_ref, dwab_ref, dwxb_ref, sm_ref, nxt8, cg, wa_ref, wx_ref, dwa_ref, dwx_ref):
        i = pl.program_id(0)
        ti = nt - 1 - i

        @pl.when(i == 0)
        def _():
            nxt8[...] = jnp.zeros_like(nxt8)
            cg[...] = jnp.zeros_like(cg)
            dwa_ref[...] = jnp.zeros_like(dwa_ref)
            dwx_ref[...] = jnp.zeros_like(dwx_ref)
            sm_ref[...] = jnp.zeros_like(sm_ref)
            _dense_from_blocks(wab_ref, wa_ref)
            _dense_from_blocks(wxb_ref, wx_ref)

        row = lax.broadcasted_iota(jnp.int32, (RT, R), 0)
        first = (ti > 0).astype(F32)
        xprev8 = pprev_ref[...] * first
        hprev8 = hprev_ref[...] * first
        xp = p_ref[:, 0:R]
        ga = p_ref[:, R:2 * R]
        f = _rec_gates(xp, xprev8, row, cw_ref, cb_ref, wa_ref, ba_ref, wx_ref, bx_ref, lam_ref)
        xa, r, ig, a, mult = f["xa"], f["r"], f["ig"], f["a"], f["mult"]
        h = h_ref[...]
        sg = _sigmoid(ga)
        gate = ga * sg
        yp = h * gate
        _, ypn, rstd = _rms_fwd(yp, nr_ref[...])
        d_yp, dnr = _rms_bwd(dya_ref[...], ypn, rstd, nr_ref[...])
        d_ga = d_yp * h * (sg * (1.0 + ga * (1.0 - sg)))
        dh = d_yp * gate + jnp.where(row == RT - 1, cg[0:1, :], 0.0)
        al = jnp.where(row < RT - 1, pltpu.roll(a, RT - 1, 0), 0.0)
        g = _scan_bwd(al, dh, row)
        cg[0:1, :] = jnp.sum(jnp.where(row == 0, a * g, 0.0), axis=0, keepdims=True)
        h_m1 = _shift_down(h, hprev8, 1, row)
        da = g * h_m1
        ix = ig * xa
        d_mult = g * ix
        d_ig = g * mult * xa
        d_xa = g * mult * ig
        d_la = da * a - d_mult * (a * a) / mult
        d_r = d_la * ((-LRU_C) * f["sp"])
        dsp = jnp.sum(d_la * ((-LRU_C) * r), axis=0, keepdims=True)
        dlam = dsp * (-_sigmoid(-lam_ref[...]))
        d_za = d_r * r * (1.0 - r)
        d_zx = d_ig * ig * (1.0 - ig)
        dzab = d_za.astype(BF16)
        dzxb = d_zx.astype(BF16)
        dwa_ref[...] += _dot_tn(f["xab"], dzab)
        dwx_ref[...] += _dot_tn(f["xab"], dzxb)
        d_xa = d_xa + _dot_nt(dzab, wa_ref[...]) + _dot_nt(dzxb, wx_ref[...])
        d_xp = sum(cw_ref[3 - j:4 - j, :] * _shift_up(d_xa, nxt8[...], j, row) for j in range(4))
        dcw = [jnp.sum(d_xa * _shift_down(xp, xprev8, 3 - k, row), axis=0, keepdims=True) for k in range(4)]
        dp_ref[:, 0:R] = d_xp.astype(BF16)
        dp_ref[:, R:2 * R] = d_ga.astype(BF16)
        dp8 = d_xa[0:8, :]
        nxt8[...] = dp8
        sm_ref[0:1, :] += jnp.sum(d_za, axis=0, keepdims=True)
        sm_ref[1:2, :] += jnp.sum(d_zx, axis=0, keepdims=True)
        sm_ref[2:3, :] += dlam
        sm_ref[3:4, :] += dnr
        sm_ref[4:5, :] += jnp.sum(d_xa, axis=0, keepdims=True)
        for k in range(4):
            sm_ref[8 + k:9 + k, :] += dcw[k]

        @pl.when(i == nt - 1)
        def _():
            for h in range(R // HEAD):
                dwab_ref[h] = dwa_ref[h * HEAD:(h + 1) * HEAD, h * HEAD:(h + 1) * HEAD].astype(BF16)
                dwxb_ref[h] = dwx_ref[h * HEAD:(h + 1) * HEAD, h * HEAD:(h + 1) * HEAD].astype(BF16)

    c0 = lambda shape: pl.BlockSpec(shape, lambda i: (0, 0))
    blocks = pl.BlockSpec((R // HEAD, HEAD, HEAD), lambda i: (0, 0, 0))
    rev = lambda i: nt - 1 - i
    prev8 = lambda i: (jnp.maximum((nt - 1 - i) * (RT // 8) - 1, 0), 0)
    return pl.pallas_call(
        body, name="rec_bwd", grid=(nt,),
        in_specs=[pl.BlockSpec(memory_space=pl.ANY),
                  pl.BlockSpec((RT, R), lambda i: (rev(i), 0)),
                  pl.BlockSpec((RT, 2 * R), lambda i: (rev(i), 0)), pl.BlockSpec((8, R), prev8),
                  pl.BlockSpec((RT, R), lambda i: (rev(i), 0)), pl.BlockSpec((8, R), prev8),
                  c0((4, R)), c0((1, R)), blocks, c0((1, R)), blocks, c0((1, R)), c0((1, R)), c0((1, R))],
        out_specs=[pl.BlockSpec((RT, 2 * R), lambda i: (rev(i), 0)), blocks, blocks, c0((16, R))],
        out_shape=[jax.ShapeDtypeStruct((S, E), BF16), jax.ShapeDtypeStruct((R // HEAD, HEAD, HEAD), BF16),
                   jax.ShapeDtypeStruct((R // HEAD, HEAD, HEAD), BF16), jax.ShapeDtypeStruct((16, R), F32)],
        scratch_shapes=[pltpu.VMEM((8, R), F32), pltpu.VMEM((8, R), F32), pltpu.VMEM((R, R), BF16),
                        pltpu.VMEM((R, R), BF16), pltpu.VMEM((R, R), F32), pltpu.VMEM((R, R), F32)],
        input_output_aliases={0: 0},
        compiler_params=_cp(("arbitrary",)),
    )(dproj, d_ya, proj, proj, h_all, h_all, conv_w, conv_b, wa_b, ba, wx_b, bx, lam, norm_rec)


NPAIR = R // LANES
QB, KB, VB, GB = 2 * R // LANES, 3 * R // LANES, 4 * R // LANES, 5 * R // LANES


def _rope_freq():
    half = HEAD // 2
    inv = np.float32(ROPE_THETA) ** (-(np.arange(half, dtype=np.float32) / np.float32(half)))
    return jnp.asarray(np.tile(inv.astype(np.float32), LANES // half)[None, :])


def _rot_half(x, first):
    return jnp.where(first, -pltpu.roll(x, LANES - HEAD // 2, 1), pltpu.roll(x, HEAD // 2, 1))


def _cos_sin(pos_ref, freq_ref):
    ang = pos_ref[...].astype(F32) * freq_ref[...]
    return jnp.cos(ang), jnp.sin(ang)


SUB = 4


def _stages(d):
    assert d in (1, SUB, SUB * SUB)
    return d > SUB


def _strided_rows(src_ref, d, tmp):
    n = S // d
    if not _stages(d):
        for r in range(d):
            yield r * n, (src_ref[pl.ds(r, n, stride=d), :] if d > 1 else src_ref[...])
        return
    m = S // SUB
    for r in range(SUB):
        tmp[r * m:(r + 1) * m, :] = src_ref[pl.ds(r, m, stride=SUB), :]
    for r in range(SUB):
        for q in range(SUB):
            yield (r + SUB * q) * n, tmp[pl.ds(r * m + q, n, stride=SUB), :]


def _deint(src_ref, dst_ref, d, tmp):
    n = S // d
    for row0, v in _strided_rows(src_ref, d, tmp):
        dst_ref[row0:row0 + n, :] = v.astype(dst_ref.dtype)


def _reint(src_ref, dst_ref, d, accumulate, tmp):
    if _stages(d):
        n, m = S // d, S // SUB
        for r in range(SUB):
            for q in range(SUB):
                tmp[pl.ds(r * m + q, n, stride=SUB), :] = src_ref[(r + SUB * q) * n:(r + SUB * q + 1) * n, :]
        src_ref, d = tmp, SUB
    n = S // d
    for r in range(d):
        idx = (pl.ds(r, n, stride=d), slice(None)) if d > 1 else (slice(None), slice(None))
        v = src_ref[r * n:(r + 1) * n, :]
        if accumulate:
            dst_ref[idx] = dst_ref[idx] + v
        else:
            dst_ref[idx] = v


def _deint_heads(src_ref, dst0, dst1, d, tmp):
    n = S // d
    hm0 = lax.broadcasted_iota(jnp.int32, (n, LANES), 1) < HEAD
    for row0, v in _strided_rows(src_ref, d, tmp):
        dst0[row0:row0 + n, :] = jnp.where(hm0, v, 0.0).astype(BF16)
        dst1[row0:row0 + n, :] = jnp.where(hm0, 0.0, v).astype(BF16)


def _reint_prev(src_ref, dst_ref, d):
    n = S // d
    if n == BLK:
        return
    for r in range(d):
        idx = (pl.ds(r, n - BLK, stride=d), slice(None)) if d > 1 else (slice(0, n - BLK), slice(None))
        dst_ref[idx] = dst_ref[idx] + src_ref[r * n + BLK:(r + 1) * n, :]


def _pair_masks():
    qi = lax.broadcasted_iota(jnp.int32, (BLK, 2 * BLK), 0)
    ki = lax.broadcasted_iota(jnp.int32, (BLK, 2 * BLK), 1) & (BLK - 1)
    return ki <= qi, ki >= qi


def _two(ref0, ref1, st, axis):
    return jnp.concatenate([ref0[pl.ds(st, BLK), :], ref1[pl.ds(st, BLK), :]], axis=axis)


ATT_UNROLL = 8


def _att_fwd(proj, cos, sin, w_out):
    def body(q_ref, k_ref, v_ref, cos_ref, sin_ref, w_ref, att_ref, qr_ref, kr_ref, lse_ref, wbf_ref,
             qd, kd0, kd1, vd0, vd1, od, ld, tmp, on, ln, wbuf, *wsems):
        wg = _WeightGather(w_ref, wbuf, *wsems)
        pl.when(pl.program_id(0) == 0)(wg.start)
        pl.when(pl.program_id(0) == 1)(wg.forward)
        lane = lax.broadcasted_iota(jnp.int32, (S, LANES), 1)
        first = (lane & (HEAD // 2)) == 0
        cos, sin = cos_ref[...], sin_ref[...]
        q = q_ref[...]
        k = k_ref[...]
        qr_ref[...] = (q * cos + _rot_half(q, first) * sin) * (HEAD ** -0.5)
        kr_ref[...] = k * cos + _rot_half(k, first) * sin
        hm0 = lax.broadcasted_iota(jnp.int32, (BLK, LANES), 1) < HEAD
        top = lax.broadcasted_iota(jnp.int32, (2 * BLK, LANES), 0) < BLK
        ones2 = (top == (lax.broadcasted_iota(jnp.int32, (2 * BLK, LANES), 1) < HEAD)).astype(BF16)
        mc2, mp2 = _pair_masks()

        for pi, d in enumerate(PATTERNS):
            nb = S // d // BLK
            _deint(qr_ref, qd, d, tmp)
            _deint_heads(kr_ref, kd0, kd1, d, tmp)
            _deint_heads(v_ref, vd0, vd1, d, tmp)

            def blk(b, carry):
                st = pl.multiple_of(b * BLK, BLK)
                qb = qd[pl.ds(st, BLK), :]
                sc = jnp.where(mc2, _dot_nt(qb, _two(kd0, kd1, st, 0)), NEG)
                mx = sc
                if nb > 1:
                    stp = pl.multiple_of(jnp.maximum(b - 1, 0) * BLK, BLK)
                    mp = jnp.logical_and(mp2, lax.rem(b, nb) != 0)
                    sp = jnp.where(mp, _dot_nt(qb, _two(kd0, kd1, stp, 0)), NEG)
                    mx = jnp.maximum(sc, sp)
                m0 = jnp.max(mx[:, 0:BLK], axis=1, keepdims=True)
                m1 = jnp.max(mx[:, BLK:2 * BLK], axis=1, keepdims=True)
                mf = jnp.concatenate([jnp.broadcast_to(m0, (BLK, BLK)), jnp.broadcast_to(m1, (BLK, BLK))], axis=1)
                o = _dot(jnp.exp(sc - mf).astype(BF16), jnp.concatenate([_two(vd0, vd1, st, 0), ones2], axis=1))
                if nb > 1:
                    o = o + _dot(jnp.exp(sp - mf).astype(BF16), jnp.concatenate([_two(vd0, vd1, stp, 0), ones2], axis=1))
                l = o[:, LANES:2 * LANES]
                od[pl.ds(st, BLK), :] = o[:, 0:LANES] / l
                ld[pl.ds(st, BLK), :] = jnp.where(hm0, m0, m1) + jnp.log(l)
                return carry

            lax.fori_loop(0, S // BLK, blk, 0, unroll=ATT_UNROLL)
            _reint(od, on.at[pi], d, False, tmp)
            _reint(ld, ln.at[pi], d, False, tmp)

        l0, l1, l2 = ln[0], ln[1], ln[2]
        m = jnp.maximum(jnp.maximum(l0, l1), l2)
        e0, e1, e2 = jnp.exp(l0 - m), jnp.exp(l1 - m), jnp.exp(l2 - m)
        den = e0 + e1 + e2
        att_ref[...] = (e0 * on[0] + e1 * on[1] + e2 * on[2]) / den
        lse_ref[...] = m + jnp.log(den)

        @pl.when(pl.program_id(0) == NPAIR - 1)
        def _():
            wg.finish()
            wbf_ref[...] = wbuf[...]

    col = lambda c0: pl.BlockSpec((S, LANES), lambda p: (0, c0 + p))
    out = pl.BlockSpec((S, LANES), lambda p: (0, p))
    tab = pl.BlockSpec((S, LANES), lambda p: (0, 0))
    vm = pl.BlockSpec(memory_space=pltpu.VMEM)
    return pl.pallas_call(
        body, name="att_fwd", grid=(NPAIR,),
        in_specs=[col(QB), col(KB), col(VB), tab, tab, vm],
        out_specs=[out, out, out, out, vm],
        out_shape=[jax.ShapeDtypeStruct((S, R), F32)] * 4 + [jax.ShapeDtypeStruct((NCHIP,) + w_out.shape, BF16)],
        scratch_shapes=[pltpu.VMEM((S, LANES), BF16)] * 5 + [pltpu.VMEM((S, LANES), F32)] * 3
        + [pltpu.VMEM((3, S, LANES), F32)] * 2 + [pltpu.VMEM((NCHIP,) + w_out.shape, BF16)] + _WeightGather.SEMS,
        compiler_params=_cp(("arbitrary",)),
    )(proj, proj, proj, cos, sin, w_out)


def _att_bwd(dproj, d_att, att, lse, qr, kr, proj, cos, sin, gw_out4):
    out_units = [(j, j, 0) for j in range(NCHIP)]

    nblk = S // BLK

    def body(dp_in, do_ref, o_ref, lse_ref, qr_ref, kr_ref, v_ref, cos_ref, sin_ref, gw_ref, dp_ref, gout_ref,
             qd, kd0, kd1, vd0, vd1, dod, kt, packn, packd, dqd, dkcd, dkpd, dvcd, dvpd,
             dqn, dkn, dvn, tmp, rows, trs, pts, dss, stage, sems, gred, *rs_scratch):
        p = pl.program_id(0)
        rs = _ReduceScatter(gw_ref, gred, out_units, *rs_scratch)
        for step, piece in enumerate((rs.start_halves, rs.send_partials, rs.reduce_owned)):
            pl.when(p == step)(piece)

        @pl.when(p == NPAIR - 1)
        def _():
            rs.finish()
            gout_ref[...] = gred[...]

        lane = lax.broadcasted_iota(jnp.int32, (S, LANES), 1)
        hms = lane < HEAD
        prod = do_ref[...] * o_ref[...]
        d0 = jnp.sum(jnp.where(hms, prod, 0.0), axis=1, keepdims=True)
        d1 = jnp.sum(jnp.where(hms, 0.0, prod), axis=1, keepdims=True)
        lse = lse_ref[...]
        quarter = HEAD // 2
        packn[...] = jnp.where(lane < quarter, lse,
                               jnp.where(hms, pltpu.roll(lse, LANES - quarter, 1), jnp.where(lane < 3 * quarter, d0, d1)))
        dqn[...] = jnp.zeros_like(dqn)
        dkn[...] = jnp.zeros_like(dkn)
        dvn[...] = jnp.zeros_like(dvn)
        hm0 = lax.broadcasted_iota(jnp.int32, (BLK, LANES), 1) < HEAD
        key = lax.broadcasted_iota(jnp.int32, (2 * BLK, BLK), 0) & (BLK - 1)
        qry = lax.broadcasted_iota(jnp.int32, (2 * BLK, BLK), 1)
        mct, mpt = key <= qry, key >= qry

        for d in PATTERNS:
            nb = S // d // BLK
            _deint(qr_ref, qd, d, tmp)
            _deint_heads(kr_ref, kd0, kd1, d, tmp)
            _deint_heads(v_ref, vd0, vd1, d, tmp)
            _deint(do_ref, dod, d, tmp)
            _deint(packn, packd, d, tmp)

            sides = (0, 1) if nb > 1 else (0,)

            def probs(b, carry):
                st = pl.multiple_of(b * BLK, BLK)
                kt[b] = _two(kd0, kd1, st, 0).astype(F32).T.astype(BF16)
                trs[b] = packd[pl.ds(st, BLK), :].T
                for j in range(4):
                    rows[b, j:j + 1, :] = trs[b, j * quarter:j * quarter + 1, :]
                qb, dob = qd[pl.ds(st, BLK), :], dod[pl.ds(st, BLK), :]
                both = lambda j: jnp.concatenate([jnp.broadcast_to(rows[b, j:j + 1, :], (BLK, BLK)),
                                                  jnp.broadcast_to(rows[b, j + 1:j + 2, :], (BLK, BLK))], axis=0)
                lbt, dlt = both(0), both(2)
                for sd in sides:
                    stk = pl.multiple_of(jnp.maximum(b - sd, 0) * BLK, BLK)
                    mask = mct if sd == 0 else jnp.logical_and(mpt, lax.rem(b, nb) != 0)
                    k2, v2 = _two(kd0, kd1, stk, 0), _two(vd0, vd1, stk, 0)
                    pt = jnp.where(mask, jnp.exp(_dot_nt(k2, qb) - lbt), 0.0)
                    pts[b, sd] = pt.astype(BF16)
                    dss[b, sd] = (pt * (_dot_nt(v2, dob) - dlt)).astype(BF16)
                return carry

            lax.fori_loop(0, nblk, probs, 0, unroll=ATT_UNROLL)

            def prods(b, carry):
                st = pl.multiple_of(b * BLK, BLK)
                qb, dob = qd[pl.ds(st, BLK), :], dod[pl.ds(st, BLK), :]
                dq_t = None
                for sd in sides:
                    dst, ptb = dss[b, sd], pts[b, sd]
                    rk, rv = _dot(dst, qb), _dot(ptb, dob)
                    dqs = _dot(kt[jnp.maximum(b - sd, 0)], dst)
                    dq_t = dqs if dq_t is None else dq_t + dqs
                    dk, dv = (dkcd, dvcd) if sd == 0 else (dkpd, dvpd)
                    dk[pl.ds(st, BLK), :] = jnp.where(hm0, rk[0:BLK], rk[BLK:2 * BLK])
                    dv[pl.ds(st, BLK), :] = jnp.where(hm0, rv[0:BLK], rv[BLK:2 * BLK])
                dqd[pl.ds(st, BLK), :] = dq_t.T
                return carry

            lax.fori_loop(0, nblk, prods, 0, unroll=ATT_UNROLL)
            _reint(dqd, dqn, d, True, tmp)
            _reint(dkcd, dkn, d, True, tmp)
            _reint(dvcd, dvn, d, True, tmp)
            _reint_prev(dkpd, dkn, d)
            _reint_prev(dvpd, dvn, d)

        lane = lax.broadcasted_iota(jnp.int32, (S, LANES), 1)
        first = (lane & (HEAD // 2)) == 0
        cos, sin = cos_ref[...], sin_ref[...]
        dq = dqn[...] * (HEAD ** -0.5)
        dk = dkn[...]
        stage[0] = (dq * cos - _rot_half(dq, first) * sin).astype(BF16)
        stage[1] = (dk * cos - _rot_half(dk, first) * sin).astype(BF16)
        stage[2] = dvn[...].astype(BF16)
        copies = [pltpu.make_async_copy(stage.at[j], dp_ref.at[:, pl.ds((2 + j) * R + p * LANES, LANES)], sems.at[j])
                  for j in range(3)]
        for cp in copies:
            cp.start()
        for cp in copies:
            cp.wait()

    blk = pl.BlockSpec((S, LANES), lambda p: (0, p))
    tab = pl.BlockSpec((S, LANES), lambda p: (0, 0))
    vm = pl.BlockSpec(memory_space=pltpu.VMEM)
    _, orows, ocols = gw_out4.shape
    return pl.pallas_call(
        body, name="att_bwd", grid=(NPAIR,),
        in_specs=[pl.BlockSpec(memory_space=pl.ANY), blk, blk, blk, blk, blk,
                  pl.BlockSpec((S, LANES), lambda p: (0, VB + p)), tab, tab, vm],
        out_specs=[pl.BlockSpec(memory_space=pl.ANY), vm],
        out_shape=[jax.ShapeDtypeStruct((S, E), BF16), jax.ShapeDtypeStruct((orows, ocols), F32)],
        scratch_shapes=[pltpu.VMEM((S, LANES), BF16)] * 6 + [pltpu.VMEM((nblk, LANES, 2 * BLK), BF16)]
        + [pltpu.VMEM((S, LANES), F32)] * 11
        + [pltpu.VMEM((nblk, 8, BLK), F32), pltpu.VMEM((nblk, LANES, BLK), F32)]
        + [pltpu.VMEM((nblk, 2, 2 * BLK, BLK), BF16)] * 2
        + [pltpu.VMEM((3, S, LANES), BF16), pltpu.SemaphoreType.DMA((3,)), pltpu.VMEM((orows, ocols), F32)]
        + _ReduceScatter.scratch(NCHIP, orows, ocols, 1),
        input_output_aliases={0: 0},
        compiler_params=_cp(("arbitrary",)),
    )(dproj, d_att, att, lse, qr, kr, proj, cos, sin, gw_out4)


def _out_fwd_bwd(ya, att, proj, w_out_bf, x, target, mod, norm_post, norm_att):
    ts = 512

    def body(ya_ref, att_ref, gb_ref, w_ref, x_ref, t_ref, mod_ref, npost_ref, natt_ref,
             gx_ref, dya_ref, datt_ref, dgb_ref, gw_ref, acc_ref):
        i = pl.program_id(0)

        @pl.when(i == 0)
        def _():
            gw_ref[...] = jnp.zeros_like(gw_ref)
            acc_ref[...] = jnp.zeros_like(acc_ref)

        gate = mod_ref[:, 2 * D:3 * D]
        att = att_ref[...]
        gb = gb_ref[...]
        sg = _sigmoid(gb)
        silu = gb * sg
        ybp = att * silu
        yb, ybn, rstd_b = _rms_fwd(ybp, natt_ref[...])
        cat = jnp.concatenate([ya_ref[...], yb.astype(BF16)], axis=1)
        mix = _dot(cat, w_ref[...])
        rn, mn, rstd_m = _rms_fwd(mix, npost_ref[...])
        err = x_ref[...] + gate * rn - t_ref[...]
        dy = err * (1.0 / D)
        gx_ref[...] = dy
        dmix, dnpost = _rms_bwd(dy * gate, mn, rstd_m, npost_ref[...])
        dmb = dmix.astype(BF16)
        gw_ref[...] += _dot_tn(cat, dmb)
        dcat = _dot_nt(dmb, w_ref[...])
        dya_ref[...] = dcat[:, 0:R]
        dybp, dnatt = _rms_bwd(dcat[:, R:2 * R], ybn, rstd_b, natt_ref[...])
        datt_ref[...] = dybp * silu
        dgb_ref[...] = (dybp * att * (sg * (1.0 + gb * (1.0 - sg)))).astype(BF16)
        acc_ref[0:1, :] += jnp.sum(dy * rn, axis=0, keepdims=True)
        acc_ref[1:2, :] += dnpost
        acc_ref[2:3, 0:R] += dnatt
        acc_ref[3:4, :] += jnp.sum(jnp.sum(err * err, axis=1, keepdims=True), axis=0, keepdims=True)

    tile = lambda w: pl.BlockSpec((ts, w), lambda i: (i, 0))
    c0 = lambda shape: pl.BlockSpec(shape, lambda i: (0, 0))
    return pl.pallas_call(
        body, name="out_fwd_bwd", grid=(S // ts,),
        in_specs=[tile(R), tile(R), pl.BlockSpec((ts, R), lambda i: (i, 5)), c0((D, D)), tile(D), tile(D),
                  c0((1, 3 * D)), c0((1, D)), c0((1, R))],
        out_specs=[tile(D), tile(R), tile(R), pl.BlockSpec((ts, R), lambda i: (i, 5)), c0((D, D)), c0((8, D))],
        out_shape=[jax.ShapeDtypeStruct((S, D), F32), jax.ShapeDtypeStruct((S, R), F32),
                   jax.ShapeDtypeStruct((S, R), F32), jax.ShapeDtypeStruct((S, E), BF16),
                   jax.ShapeDtypeStruct((D, D), F32), jax.ShapeDtypeStruct((8, D), F32)],
        compiler_params=_cp(("arbitrary",)),
    )(ya, att, proj, w_out_bf, x, target, mod, norm_post, norm_att)


UC = 256
UPC = EC // UC


NU = E // UC


def _unit_of_step(i):
    return (i % NCHIP) * UPC + i // NCHIP


def _in_proj_bwd(ht, dproj, w_in_bf, x, gx1, mod, norm_pre, smalls):
    ts = 256
    nt = S // ts
    half = D // 2
    units = [_unit_of_step(k) for k in range(NU)]
    owners = [u // UPC for u in units]
    ns = len(smalls)

    def body(*refs):
        (ht_ref, dpu_ref, dp_ref, w_hbm, x_ref, gx1_ref, mod_ref, np_ref), refs = refs[:8], refs[8:]
        small_in, refs = refs[:ns], refs[ns:]
        (gx_ref, gin_ref), refs = refs[:2], refs[2:]
        small_out, (acc_out,), refs = refs[:ns], refs[ns:ns + 1], refs[ns + 1:]
        mine, sib, tmp, stage, got, red, acc_ref, hs, hr, ps, pr, bs, br = refs[:13]
        early = _SmallGather(small_in, small_out, *refs[13:16])
        late = _SmallGather([acc_ref], [acc_out], *refs[16:19])
        w_ref, w_sem = refs[19:21]
        i = pl.program_id(0)
        w_copy = pltpu.make_async_copy(w_hbm, w_ref, w_sem)
        pl.when(i == 0)(w_copy.start)
        pl.when(i == NU)(w_copy.wait)
        xx, yy, c = _me()
        ci = 2 * xx + yy
        r0 = pl.multiple_of(c * half, half)
        r1 = pl.multiple_of((1 - c) * half, half)
        pl.when(i == 0)(early.start)
        pl.when(i == NU)(early.forward)

        def exch(k):
            return _remote(tmp.at[k % 2], sib.at[k], hs.at[k], hr.at[k], 1)

        def partial(k, sender):
            return pltpu.make_async_remote_copy(
                src_ref=stage.at[k], dst_ref=got.at[units[k] % UPC, sender], send_sem=ps.at[k],
                recv_sem=pr.at[k, sender], device_id=(owners[k] // 2, owners[k] % 2, c), device_id_type=MESH)

        def back(k, start):
            off = (units[k] % UPC) * UC
            blk = red.at[pl.ds(start, half), off:off + UC]
            return _remote(blk, blk, bs.at[k], br.at[k], 1)

        for k in range(NU + 1):
            @pl.when(i == k)
            def _():
                if k < NU:
                    if k >= 2:
                        exch(k - 2).wait_send()
                    dpu = dpu_ref[...]
                    tmp[k % 2] = _dot(ht_ref[pl.ds(r1, half), :], dpu)
                    exch(k).start()
                    mine[k] = _dot(ht_ref[pl.ds(r0, half), :], dpu)
                if k >= 1:
                    exch(k - 1).wait_recv()
                    mine[k - 1] += sib[k - 1]

                    @pl.when(ci != owners[k - 1])
                    def _():
                        stage[k - 1] = mine[k - 1].astype(BF16)
                        partial(k - 1, ci).start()

        @pl.when(i == NU)
        def _():
            acc_ref[...] = jnp.zeros_like(acc_ref)

        @pl.when(i >= NU)
        def _():
            dh = sum(_dot_nt(dp_ref[:, j * EC:(j + 1) * EC], w_ref[j]) for j in range(NCHIP))
            hp, xn, rstd = _rms_fwd(x_ref[...], np_ref[...])
            dx, dnp = _rms_bwd(dh * (1.0 + mod_ref[:, D:2 * D]), xn, rstd, np_ref[...])
            gx_ref[...] = gx1_ref[...] + dx
            acc_ref[0:1, :] += jnp.sum(dh, axis=0, keepdims=True)
            acc_ref[1:2, :] += jnp.sum(dh * hp, axis=0, keepdims=True)
            acc_ref[2:3, :] += dnp

        for t in range(UPC):
            @pl.when(i == NU + 1 + 2 * t)
            def _():
                for k in range(NCHIP * t, NCHIP * (t + 1)):
                    @pl.when(ci == owners[k])
                    def _():
                        off = (units[k] % UPC) * UC
                        red[pl.ds(r0, half), off:off + UC] = mine[k]
                        for s in range(NCHIP):
                            if s != owners[k]:
                                partial(k, s).wait_recv()
                                red[pl.ds(r0, half), off:off + UC] += got[units[k] % UPC, s].astype(F32)
                        back(k, r0).start()

        @pl.when(i == NU + nt - 1)
        def _():
            late.start()
            exch(NU - 2).wait_send()
            exch(NU - 1).wait_send()
            for k in range(NU):
                @pl.when(ci == owners[k])
                def _():
                    back(k, r1).wait_recv()
                    back(k, r0).wait_send()

                @pl.when(ci != owners[k])
                def _():
                    partial(k, ci).wait_send()
            gin_ref[...] = red[...]
            early.finish()
            late.forward()
            late.finish()

    tile = lambda w: pl.BlockSpec((ts, w), lambda i: (jnp.maximum(i - NU, 0), 0))
    c0 = lambda shape: pl.BlockSpec(shape, lambda i: (0, 0))
    vm = pl.BlockSpec(memory_space=pltpu.VMEM)
    hbm = pl.BlockSpec(memory_space=pl.ANY)
    gathered = [jax.ShapeDtypeStruct((NDEV,) + a.shape, a.dtype) for a in smalls] + [jax.ShapeDtypeStruct((NDEV, 8, D), F32)]
    return pl.pallas_call(
        body, name="in_proj_bwd", grid=(NU + nt,),
        in_specs=[vm, pl.BlockSpec((S, UC), lambda i: (0, _unit_of_step(jnp.minimum(i, NU - 1)))), tile(E),
                  hbm, tile(D), tile(D), c0((1, 3 * D)), c0((1, D))] + [vm] * ns,
        out_specs=[tile(D), vm] + [hbm] * (ns + 1),
        out_shape=[jax.ShapeDtypeStruct((S, D), F32), jax.ShapeDtypeStruct((D, EC), F32)] + gathered,
        scratch_shapes=[pltpu.VMEM((NU, half, UC), F32), pltpu.VMEM((NU, half, UC), F32),
                        pltpu.VMEM((2, half, UC), F32), pltpu.VMEM((NU, half, UC), BF16),
                        pltpu.VMEM((UPC, NCHIP, half, UC), BF16), pltpu.VMEM((D, EC), F32), pltpu.VMEM((8, D), F32),
                        pltpu.SemaphoreType.DMA((NU,)), pltpu.SemaphoreType.DMA((NU,)),
                        pltpu.SemaphoreType.DMA((NU,)), pltpu.SemaphoreType.DMA((NU, NCHIP)),
                        pltpu.SemaphoreType.DMA((NU,)), pltpu.SemaphoreType.DMA((NU,))]
        + _SmallGather.sems(ns) + _SmallGather.sems(1)
        + [pltpu.VMEM((NCHIP, D, EC), BF16), pltpu.SemaphoreType.DMA],
        compiler_params=_cp(("arbitrary",)),
    )(ht, dproj, dproj, w_in_bf, x, gx1, mod, norm_pre, *smalls)


def _local_step(x, cos, sin, target, mod, w_in_bf, proj, ht, w_out, conv_w, p):
    rec_p = (conv_w, p["conv_b"], p["w_rg_a"], p["b_rg_a"], p["w_rg_x"], p["b_rg_x"], p["lru_lambda"], p["norm_rec"])
    h_all, ya = _rec_fwd(proj, *rec_p)
    att, qr, kr, lse, w_out_bf = _att_fwd(proj, cos, sin, w_out)
    gx1, d_ya, d_att, dproj, gw_out, acc_o = _out_fwd_bwd(ya, att, proj, w_out_bf.reshape(D, D), x, target, mod,
                                                           p["norm_post"], p["norm_att"])
    dproj, g_out = _att_bwd(dproj, d_att, att, lse, qr, kr, proj, cos, sin, gw_out.reshape(NCHIP, D // NCHIP, D))
    dproj, dwa, dwx, sm = _rec_bwd(dproj, d_ya, proj, h_all, *rec_p)
    grad_x, g_in, *gathered = _in_proj_bwd(ht, dproj, w_in_bf, x, gx1, mod, p["norm_pre"], [acc_o, sm, dwa, dwx])
    return grad_x, g_in, g_out, gathered


def _me():
    return lax.axis_index("x"), lax.axis_index("y"), lax.axis_index("c")


def _flip(v, bit):
    return 1 - v if bit else v


def _peer(rel):
    x, y, c = _me()
    return (_flip(x, rel & 4), _flip(y, rel & 2), _flip(c, rel & 1))


def _remote(src, dst, send_sem, recv_sem, rel):
    return pltpu.make_async_remote_copy(src_ref=src, dst_ref=dst, send_sem=send_sem, recv_sem=recv_sem,
                                        device_id=_peer(rel), device_id_type=MESH)


class _WeightGather:
    SEMS = [pltpu.SemaphoreType.DMA((NCHIP - 1,))] * 4

    def __init__(self, w_ref, out_ref, send_sems, recv_sems, fsend_sems, frecv_sems):
        x, y, c = _me()
        self.w, self.out, self.ci = w_ref, out_ref, 2 * x + y
        self.half = w_ref.shape[0] // 2
        self.r0 = pl.multiple_of(c * self.half, self.half)
        self.r1 = pl.multiple_of((1 - c) * self.half, self.half)
        self.sems = (send_sems, recv_sems, fsend_sems, frecv_sems)

    def _ici(self, chip, k):
        blk = self.out.at[chip, pl.ds(self.r0, self.half), :]
        return _remote(blk, blk, self.sems[0].at[k - 1], self.sems[1].at[k - 1], 2 * k)

    def _d2d(self, chip, start, k):
        blk = self.out.at[chip, pl.ds(start, self.half), :]
        return _remote(blk, blk, self.sems[2].at[k - 1], self.sems[3].at[k - 1], 1)

    def start(self, diagonal=True):
        self.out[self.ci] = self.w[...].astype(BF16)
        for k in range(1, NCHIP if diagonal else NCHIP - 1):
            self._ici(self.ci, k).start()

    def _relay(self, chip, piece, k):
        q = self.half // 2
        blk = self.out.at[chip, pl.ds(self.r0 + piece * q, q), :]
        return _remote(blk, blk, self.relay_sems[0].at[piece], self.relay_sems[1].at[piece], 2 * k)

    def neighbours_landed(self, relay_send_sems, relay_recv_sems):
        self.relay_sems = (relay_send_sems, relay_recv_sems)
        for k in (1, 2):
            self._ici(self.ci ^ k, k).wait_recv()
        self._relay(self.ci ^ 2, 0, 1).start()
        self._relay(self.ci ^ 1, 1, 2).start()
        for k in (1, 2):
            self._d2d(self.ci ^ k, self.r0, k).start()

    def sibling_landed(self, k):
        self._d2d(self.ci ^ k, self.r1, k).wait_recv()

    def diagonal_landed(self):
        for piece, k in ((0, 1), (1, 2)):
            self._relay(self.ci ^ 3, piece, k).wait_recv()
        self._d2d(self.ci ^ 3, self.r0, 3).start()
        self._d2d(self.ci ^ 3, self.r1, 3).wait_recv()

    def finish_relayed(self):
        for k in (1, 2):
            self._ici(self.ci, k).wait_send()
        self._relay(self.ci ^ 2, 0, 1).wait_send()
        self._relay(self.ci ^ 1, 1, 2).wait_send()
        for k in range(1, NCHIP):
            self._d2d(self.ci ^ k, self.r0, k).wait_send()

    def forward(self):
        for k in range(1, NCHIP):
            self._ici(self.ci ^ k, k).wait_recv()
            self._d2d(self.ci ^ k, self.r0, k).start()

    def finish(self):
        for k in range(1, NCHIP):
            self._d2d(self.ci ^ k, self.r1, k).wait_recv()
        self.finish_sends()

    def finish_sends(self):
        for k in range(1, NCHIP):
            self._ici(self.ci, k).wait_send()
            self._d2d(self.ci ^ k, self.r0, k).wait_send()


class _SmallGather:
    @staticmethod
    def sems(n):
        return [pltpu.SemaphoreType.DMA((n, 7)), pltpu.SemaphoreType.DMA((n, 7)), pltpu.SemaphoreType.DMA((n,))]

    def __init__(self, srcs, outs, send_sems, recv_sems, local_sems):
        x, y, c = _me()
        self.srcs, self.outs = list(srcs), list(outs)
        self.ss, self.rs, self.ls = send_sems, recv_sems, local_sems
        self.ci, self.c = 2 * x + y, c
        self.me = 2 * self.ci + c

    def _own(self, a, slot, rel):
        return _remote(self.srcs[a], self.outs[a].at[self.me], self.ss.at[a, slot], self.rs.at[a, slot], rel)

    def _block(self, a, idx, slot, rel):
        blk = self.outs[a].at[idx]
        return _remote(blk, blk, self.ss.at[a, slot], self.rs.at[a, slot], rel)

    def _local(self, a):
        return pltpu.make_async_copy(self.srcs[a], self.outs[a].at[self.me], self.ls.at[a])

    def start(self):
        for a in range(len(self.srcs)):
            self._local(a).start()
            self._own(a, 0, 1).start()
            for k in range(1, NCHIP):
                self._own(a, k, 2 * k).start()

    def forward(self):
        for a in range(len(self.srcs)):
            for k in range(1, NCHIP):
                idx = 2 * (self.ci ^ k) + self.c
                self._block(a, idx, k, 2 * k).wait_recv()
                self._block(a, idx, 3 + k, 1).start()

    def finish(self):
        for a in range(len(self.srcs)):
            self._block(a, 2 * self.ci + 1 - self.c, 0, 1).wait_recv()
            for k in range(1, NCHIP):
                self._block(a, 2 * (self.ci ^ k) + 1 - self.c, 3 + k, 1).wait_recv()
            self._own(a, 0, 1).wait_send()
            for k in range(1, NCHIP):
                self._own(a, k, 2 * k).wait_send()
                self._block(a, 2 * (self.ci ^ k) + self.c, 3 + k, 1).wait_send()
            self._local(a).wait()


def _start_in_proj(crow, w_ada, b_cols, w_in, pos, x, norm_pre, order):
    ts = 512
    nt = S // ts
    wc = crow.shape[1]

    def body(order_ref, crow_ref, wada_ref, b_ref, win_ref, pos_ref, freq_ref, x_ref, np_ref,
             g0_ref, mod_ref, wbf_ref, cos_ref, sin_ref, proj_ref, ht_ref,
             g0s, modp, modb, wbuf, hb_all, cs, cr, ms, mr, ws, wr, fs, fr, local_sems, ys, yr, osem):
        s, t = pl.program_id(0), pl.program_id(1)
        x, y, c = _me()
        ci = 2 * x + y
        me = 2 * ci + c
        wg = _WeightGather(win_ref, wbuf, ws, wr, fs, fr)

        @pl.when(jnp.logical_and(s == 0, t == 0))
        def _():
            wg.start(diagonal=False)
            mine = pltpu.make_async_copy(crow_ref, g0s.at[pl.ds(me, 1), :], local_sems.at[0])
            mine.start()
            csend = [_remote(crow_ref, g0s.at[pl.ds(me, 1), :], cs.at[r - 1], cr.at[r - 1], r) for r in range(1, NDEV)]
            for cp in csend:
                cp.start()
            cos_ref[...], sin_ref[...] = _cos_sin(pos_ref, freq_ref)
            for r in range(1, NDEV):
                px, py, pc = _peer(r)
                _remote(crow_ref, g0s.at[pl.ds(4 * px + 2 * py + pc, 1), :], cs.at[r - 1], cr.at[r - 1], r).wait_recv()
            mine.wait()
            cv = g0s[:, 0:D]
            sc = cv * _sigmoid(cv)
            scb = jnp.concatenate([sc, jnp.zeros_like(sc)], axis=0).astype(BF16)
            modp[...] = _dot(scb, wada_ref[...].astype(BF16))[0:NDEV, :] + b_ref[...]
            own = pltpu.make_async_copy(modp.at[pl.ds(me, 1), :], modb.at[ci], local_sems.at[1])
            own.start()
            msend = []
            for k in range(1, NCHIP):
                cp = _remote(modp.at[pl.ds(2 * (ci ^ k) + c, 1), :], modb.at[ci], ms.at[k - 1], mr.at[k - 1], 2 * k)
                cp.start()
                msend.append(cp)
            for k in range(1, NCHIP):
                _remote(modp.at[pl.ds(me, 1), :], modb.at[ci ^ k], ms.at[k - 1], mr.at[k - 1], 2 * k).wait_recv()
            own.wait()
            for j in range(NCHIP):
                mod_ref[:, j * EC:(j + 1) * EC] = modb[j]
            for cp in csend + msend:
                cp.wait_send()
            g0_ref[...] = g0s[...]

        def keep(k):
            return pltpu.make_async_copy(wbuf.at[ci ^ k], wbf_ref.at[ci ^ k], osem.at[k])

        @pl.when(jnp.logical_and(s == 1, t == 0))
        def _():
            keep(0).start()
            wg.neighbours_landed(ys, yr)
            wg.sibling_landed(1)
            keep(1).start()

        @pl.when(jnp.logical_and(s == 2, t == 0))
        def _():
            wg.sibling_landed(2)
            keep(2).start()

        @pl.when(jnp.logical_and(s == 3, t == 0))
        def _():
            wg.relay_sems = (ys, yr)
            wg.diagonal_landed()
            keep(3).start()

        rows = pl.ds(pl.multiple_of(t * ts, ts), ts)

        @pl.when(s == 0)
        def _():
            hp, _, _ = _rms_fwd(x_ref[...], np_ref[...])
            h = hp * (1.0 + mod_ref[:, D:2 * D]) + mod_ref[:, 0:D]
            hb_all[rows, :] = h.astype(BF16)
            ht_ref[...] = h.T.astype(BF16)

        proj_ref[...] = _dot(hb_all[rows, :], wbuf[ci ^ s])

        @pl.when(jnp.logical_and(s == NCHIP - 1, t == nt - 1))
        def _():
            wg.relay_sems = (ys, yr)
            wg.finish_relayed()
            for k in range(NCHIP):
                keep(k).wait()

    vm = pl.BlockSpec(memory_space=pltpu.VMEM)
    first_pass = lambda s, t: jnp.where(s == 0, t, nt - 1)
    grid_spec = pltpu.PrefetchScalarGridSpec(
        num_scalar_prefetch=1, grid=(NCHIP, nt),
        in_specs=[vm, vm, vm, vm, vm, vm, pl.BlockSpec((ts, D), lambda s, t, o: (first_pass(s, t), 0)),
                  pl.BlockSpec((1, D), lambda s, t, o: (0, 0))],
        out_specs=[vm, vm, pl.BlockSpec(memory_space=pl.ANY), vm, vm, pl.BlockSpec((ts, EC), lambda s, t, o: (t, o[s])),
                   pl.BlockSpec((D, ts), lambda s, t, o: (0, first_pass(s, t)))],
        scratch_shapes=[pltpu.VMEM((NDEV, wc), F32), pltpu.VMEM((NDEV, EC), F32), pltpu.VMEM((NCHIP, 1, EC), F32),
                        pltpu.VMEM((NCHIP, D, EC), BF16), pltpu.VMEM((S, D), BF16),
                        pltpu.SemaphoreType.DMA((NDEV - 1,)), pltpu.SemaphoreType.DMA((NDEV - 1,)),
                        pltpu.SemaphoreType.DMA((NCHIP - 1,)), pltpu.SemaphoreType.DMA((NCHIP - 1,))]
        + _WeightGather.SEMS + [pltpu.SemaphoreType.DMA((2,))] * 3 + [pltpu.SemaphoreType.DMA((NCHIP,))])
    return pl.pallas_call(
        body, name="start_in_proj", grid_spec=grid_spec,
        out_shape=[jax.ShapeDtypeStruct((NDEV, wc), F32), jax.ShapeDtypeStruct((1, 3 * D), F32),
                   jax.ShapeDtypeStruct((NCHIP, D, EC), BF16), jax.ShapeDtypeStruct((S, LANES), F32),
                   jax.ShapeDtypeStruct((S, LANES), F32), jax.ShapeDtypeStruct((S, E), F32),
                   jax.ShapeDtypeStruct((D, S), BF16)],
        compiler_params=_cp(("arbitrary", "arbitrary")),
    )(order, crow, w_ada, b_cols, w_in, pos, _rope_freq(), x, norm_pre)


class _ReduceScatter:
    @staticmethod
    def scratch(n_units, rows, ucols, max_owned):
        half = rows // 2
        return [pltpu.VMEM((n_units, half, ucols), F32), pltpu.VMEM((n_units, half, ucols), BF16),
                pltpu.VMEM((max_owned, NCHIP, half, ucols), BF16),
                pltpu.SemaphoreType.DMA((2,)), pltpu.SemaphoreType.DMA((n_units,)),
                pltpu.SemaphoreType.DMA((n_units, NCHIP)), pltpu.SemaphoreType.DMA((n_units,)),
                pltpu.SemaphoreType.DMA((n_units,))]

    def __init__(self, g_ref, out_ref, units, sib, stage, got, sem1, send2, recv2, send3, recv3):
        x, y, c = _me()
        self.c, self.ci = c, 2 * x + y
        self.g, self.out, self.units = g_ref, out_ref, units
        self.sib, self.stage, self.got = sib, stage, got
        self.sem1, self.send2, self.recv2, self.send3, self.recv3 = sem1, send2, recv2, send3, recv3
        self.half = g_ref.shape[1] // 2
        self.ucols = g_ref.shape[2]
        self.r0 = pl.multiple_of(c * self.half, self.half)
        self.r1 = pl.multiple_of((1 - c) * self.half, self.half)
        self.slot0 = units[0][0]
        assert [u[0] for u in units] == list(range(self.slot0, self.slot0 + len(units)))
        seen = {}
        self.local = []
        for _, owner, _ in units:
            self.local.append(seen.get(owner, 0))
            seen[owner] = seen.get(owner, 0) + 1

    def _halves(self):
        n = len(self.units)
        return _remote(self.g.at[pl.ds(self.slot0, n), pl.ds(self.r1, self.half), :], self.sib,
                       self.sem1.at[0], self.sem1.at[1], 1)

    def _partial(self, i, sender):
        _, owner, _ = self.units[i]
        return pltpu.make_async_remote_copy(
            src_ref=self.stage.at[i], dst_ref=self.got.at[self.local[i], sender],
            send_sem=self.send2.at[i], recv_sem=self.recv2.at[i, sender],
            device_id=(owner // 2, owner % 2, self.c), device_id_type=MESH)

    def _back(self, i, start):
        off = self.units[i][2]
        blk = self.out.at[pl.ds(start, self.half), off:off + self.ucols]
        return _remote(blk, blk, self.send3.at[i], self.recv3.at[i], 1)

    def start_halves(self):
        self._halves().start()

    def send_partials(self):
        self._halves().wait_recv()
        for i, (slot, owner, _) in enumerate(self.units):
            @pl.when(self.ci != owner)
            def _():
                self.stage[i] = (self.g[slot, pl.ds(self.r0, self.half), :] + self.sib[i]).astype(BF16)
                self._partial(i, self.ci).start()

    def reduce_owned(self):
        for i, (slot, owner, off) in enumerate(self.units):
            @pl.when(self.ci == owner)
            def _():
                rows, cols = pl.ds(self.r0, self.half), slice(off, off + self.ucols)
                self.out[rows, cols] = self.g[slot, pl.ds(self.r0, self.half), :] + self.sib[i]
                for s in range(NCHIP):
                    if s != owner:
                        self._partial(i, s).wait_recv()
                        self.out[rows, cols] += self.got[self.local[i], s].astype(F32)
                self._back(i, self.r0).start()

    def finish(self):
        self._halves().wait_send()
        for i, (_, owner, _) in enumerate(self.units):
            @pl.when(self.ci == owner)
            def _():
                self._back(i, self.r1).wait_recv()
                self._back(i, self.r0).wait_send()

            @pl.when(self.ci != owner)
            def _():
                self._partial(i, self.ci).wait_send()


def _silu_rows(c_ref):
    cv = c_ref[...]
    sc = cv * _sigmoid(cv)
    return jnp.concatenate([sc, jnp.zeros_like(sc)], axis=0).astype(BF16)


def _adamw(groups):
    steps = 4
    specs = [pl.BlockSpec((w.shape[0] // steps, w.shape[1]), lambda i: (i, 0)) for w, _, _, _ in groups]

    def body(*refs):
        ins, outs = refs[:4 * len(groups)], refs[4 * len(groups):]
        for j in range(len(groups)):
            w_ref, g_ref, m_ref, v_ref = ins[4 * j:4 * j + 4]
            d_ref, nm_ref, nv_ref = outs[3 * j:3 * j + 3]
            d_ref[...], nm_ref[...], nv_ref[...] = _adamw_values(w_ref[...], g_ref[...], m_ref[...], v_ref[...])

    res = pl.pallas_call(
        body, name="adamw_big", grid=(steps,),
        in_specs=[s for s in specs for _ in range(4)], out_specs=[s for s in specs for _ in range(3)],
        out_shape=[jax.ShapeDtypeStruct(w.shape, F32) for w, _, _, _ in groups for _ in range(3)],
        compiler_params=_cp(("parallel",)),
    )(*[a for grp in groups for a in grp])
    return [res[3 * j:3 * j + 3] for j in range(len(groups))]


def _adamw_values(w, g, m, v):
    nm = B1 * m + (1.0 - B1) * g
    nv = B2 * v + (1.0 - B2) * (g * g)
    m_hat = nm / (1.0 - B1 ** STEP)
    v_hat = nv / (1.0 - B2 ** STEP)
    return (-LR) * (m_hat / (jnp.sqrt(v_hat) + ADAM_EPS) + WD * w), nm, nv


NB = R // HEAD
SMALL = (("b_ada", (1, 3 * D)), ("norm_pre", (1, D)), ("norm_post", (1, D)), ("conv_w", (4, R // NCHIP)),
         ("conv_b", (1, R)), ("w_rg_a", (NB, HEAD, HEAD)), ("b_rg_a", (1, R)), ("w_rg_x", (NB, HEAD, HEAD)),
         ("b_rg_x", (1, R)), ("lru_lambda", (1, R)), ("norm_rec", (1, R)), ("norm_att", (1, R)))


def _small_update(ao8, sm8, dwa8, dwx8, ai8, cg, params):
    n = len(SMALL)

    def body(ao_ref, sm_ref, dwa_ref, dwx_ref, ai_ref, cg_ref, *refs):
        pin, pout, (gada_ref, loss_ref, dmod) = refs[:3 * n], refs[3 * n:7 * n], refs[7 * n:]
        xx, yy, _ = _me()
        ci = 2 * xx + yy

        def total(ref, *idx):
            acc = ref[(0,) + idx].astype(F32)
            for d in range(1, NDEV):
                acc = acc + ref[(d,) + idx].astype(F32)
            return acc

        row = lambda ref, r, lanes=slice(None): total(ref, slice(r, r + 1), lanes)
        mine = lambda parts: sum(jnp.where(ci == j, part, 0.0) for j, part in enumerate(parts))
        cw = R // NCHIP
        grads = {
            "b_ada": [jnp.concatenate([row(ai_ref, 0), row(ai_ref, 1), row(ao_ref, 0)], axis=1)],
            "norm_pre": [row(ai_ref, 2)], "norm_post": [row(ao_ref, 1)],
            "conv_w": [mine([row(sm_ref, 8 + r, slice(j * cw, (j + 1) * cw)) for j in range(NCHIP)]) for r in range(4)],
            "conv_b": [row(sm_ref, 4)], "b_rg_a": [row(sm_ref, 0)], "b_rg_x": [row(sm_ref, 1)],
            "lru_lambda": [row(sm_ref, 2)], "norm_rec": [row(sm_ref, 3)], "norm_att": [row(ao_ref, 2, slice(0, R))],
            "w_rg_a": [total(dwa_ref, h) for h in range(NB)], "w_rg_x": [total(dwx_ref, h) for h in range(NB)],
        }
        loss_ref[...] = row(ao_ref, 3, slice(0, LANES)) * (0.5 / D)
        for k, (name, shape) in enumerate(SMALL):
            w_ref, m_ref, v_ref = pin[3 * k:3 * k + 3]
            outs = pout[4 * k:4 * k + 4]
            for r, g in enumerate(grads[name]):
                at = (slice(None),) if len(grads[name]) == 1 else ((r,) if len(shape) == 3 else (slice(r, r + 1),))
                res = (g,) + _adamw_values(w_ref[at], g, m_ref[at], v_ref[at])
                for o_ref, val in zip(outs, res):
                    o_ref[at] = val
        for d in range(NDEV):
            dmod[d:d + 1, :] = jnp.concatenate([ai_ref[d, 0:1, :], ai_ref[d, 1:2, :], ao_ref[d, 0:1, :]], axis=1)
        cols = mine([dmod[:, j * EC:(j + 1) * EC] for j in range(NCHIP)])
        colsb = jnp.concatenate([cols, jnp.zeros_like(cols)], axis=0).astype(BF16)
        gada_ref[...] = _dot_tn(_silu_rows(cg_ref), colsb)

    shapes = [jax.ShapeDtypeStruct(s, F32) for _, s in SMALL]
    outs = pl.pallas_call(
        body, name="small_update",
        out_shape=[s for s in shapes for _ in range(4)] + [jax.ShapeDtypeStruct((D, EC), F32),
                                                           jax.ShapeDtypeStruct((1, LANES), F32)],
        scratch_shapes=[pltpu.VMEM((NDEV, 3 * D), F32)],
        compiler_params=_cp(),
    )(ao8, sm8, dwa8, dwx8, ai8, cg, *params)
    return outs[:4 * n], outs[4 * n], outs[4 * n + 1]


BIG = ("w_ada", "w_in", "w_out")
WEIGHTS = ("w_ada", "b_ada", "norm_pre", "norm_post", "w_in", "conv_w", "conv_b", "w_rg_a", "b_rg_a", "w_rg_x",
           "b_rg_x", "lru_lambda", "norm_rec", "norm_att", "w_out")


def kernel(x, c, positions, w_ada, b_ada, norm_pre, norm_post, w_in, conv_w, conv_b, w_rg_a, b_rg_a, w_rg_x, b_rg_x, lru_lambda, norm_rec, norm_att, w_out, loss_target, m_w_ada, m_b_ada, m_norm_pre, m_norm_post, m_w_in, m_conv_w, m_conv_b, m_w_rg_a, m_b_rg_a, m_w_rg_x, m_b_rg_x, m_lru_lambda, m_norm_rec, m_norm_att, m_w_out, v_w_ada, v_b_ada, v_norm_pre, v_norm_post, v_w_in, v_conv_w, v_conv_b, v_w_rg_a, v_b_rg_a, v_w_rg_x, v_b_rg_x, v_lru_lambda, v_norm_rec, v_norm_att, v_w_out):
    given = dict(locals())
    wts = {n: given[n] for n in WEIGHTS}
    ms = {n: given["m_" + n] for n in WEIGHTS}
    vs = {n: given["v_" + n] for n in WEIGHTS}
    xi, yi, _ = _me()
    chip = 2 * xi + yi
    cw_loc = R // NCHIP

    b_cols = lax.dynamic_slice(b_ada, (0, chip * EC), (1, EC))
    order = (chip ^ jnp.arange(NCHIP, dtype=jnp.int32)).astype(jnp.int32)
    g0, mod, w_in_bf, cos, sin, proj, ht = _start_in_proj(
        jnp.concatenate([c, conv_w.reshape(1, 4 * cw_loc)], axis=1), w_ada[0], b_cols, w_in[0],
        positions.reshape(S, 1), x[0], norm_pre, order)
    cg = g0[:, 0:D]
    conv_full = g0[0::2, D:].reshape(NCHIP, 4, cw_loc).transpose(1, 0, 2).reshape(4, R)

    p = dict(norm_pre=norm_pre, norm_post=norm_post, conv_b=conv_b, b_rg_a=b_rg_a, b_rg_x=b_rg_x,
             lru_lambda=lru_lambda, norm_rec=norm_rec, norm_att=norm_att, w_rg_a=w_rg_a[0], w_rg_x=w_rg_x[0])
    grad_x, g_in, g_out, gathered = _local_step(
        x[0], cos, sin, loss_target[0], mod, w_in_bf, proj, ht, w_out[0], conv_full, p)

    params = [d[n].reshape(shape) for n, shape in SMALL for d in (wts, ms, vs)]
    small_out, g_ada, loss_row = _small_update(*gathered, cg, params)
    grads = {"w_out": g_out, "w_in": g_in, "w_ada": g_ada}
    delta, new_m, new_v = {}, {}, {}
    for k, (n, _) in enumerate(SMALL):
        grads[n], delta[n], new_m[n], new_v[n] = small_out[4 * k:4 * k + 4]
    for n, res in zip(BIG, _adamw([(wts[n][0], grads[n], ms[n][0], vs[n][0]) for n in BIG])):
        delta[n], new_m[n], new_v[n] = res
    out = lambda d: [d[n].reshape(wts[n].shape) for n in WEIGHTS]
    return (loss_row[0, 0], grad_x.reshape(x.shape), *out(grads), *out(delta), *out(new_m), *out(new_v))
```

```python
import numpy as np
import jax
import jax.numpy as jnp
from jax import lax
from jax.experimental import pallas as pl
from jax.experimental.pallas import tpu as pltpu

F32 = jnp.float32
BF16 = jnp.bfloat16

S = 2048
D = 1024
E = 3072
R = 512
NDEV = 8
NCHIP = 4
EC = 768
LRU_C = 8.0
EPS = 1e-6
NEG = -1e30
HEAD = 64
BLK = 128
PATTERNS = (1, 4, 16)
ROPE_THETA = 10000.0
LANES = 128
VMEM_LIMIT = 56 * 1024 * 1024

B1, B2, LR, WD, ADAM_EPS, STEP = 0.9, 0.999, 0.001, 0.01, 1e-8, 10
MESH = pl.DeviceIdType.MESH


def _cp(sem=None, **kw):
    return pltpu.CompilerParams(dimension_semantics=sem, vmem_limit_bytes=VMEM_LIMIT, **kw)


def _dot(a, b):
    return jnp.dot(a, b, preferred_element_type=F32)


def _dot_nt(a, b):
    return lax.dot_general(a, b, (((1,), (1,)), ((), ())), preferred_element_type=F32)


def _dot_tn(a, b):
    return lax.dot_general(a, b, (((0,), (0,)), ((), ())), preferred_element_type=F32)


def _sigmoid(x):
    return 1.0 / (1.0 + jnp.exp(-x))


def _one_minus_exp(x, ex):
    poly = -x * (1.0 + x * (0.5 + x * (1.0 / 6 + x * (1.0 / 24))))
    return jnp.where(x > -1.0 / 16, poly, 1.0 - ex)


def _rms_fwd(v, g):
    rstd = lax.rsqrt(jnp.mean(v * v, axis=-1, keepdims=True) + EPS)
    vn = v * rstd
    return vn * g, vn, rstd


def _rms_bwd(dy, vn, rstd, g):
    dvn = dy * g
    dv = rstd * (dvn - vn * jnp.mean(dvn * vn, axis=-1, keepdims=True))
    return dv, jnp.sum(dy * vn, axis=0, keepdims=True)


RT = 256


def _shift_down(cur, prev8, j, row):
    if j == 0:
        return cur
    rolled = pltpu.roll(cur, j, 0)
    top = jnp.where(row[0:8] >= j, rolled[0:8], pltpu.roll(prev8, j, 0))
    return jnp.concatenate([top, rolled[8:]], axis=0)


def _shift_up(cur, next8, j, row):
    if j == 0:
        return cur
    rolled = pltpu.roll(cur, RT - j, 0)
    bot = jnp.where(row[RT - 8:] < RT - j, rolled[RT - 8:], pltpu.roll(next8, 8 - j, 0))
    return jnp.concatenate([rolled[:RT - 8], bot], axis=0)


def _rec_gates(xp, xprev8, row, cw_ref, cb_ref, wa_ref, ba_ref, wx_ref, bx_ref, lam_ref):
    xa = cb_ref[...] + sum(cw_ref[3 - j:4 - j, :] * _shift_down(xp, xprev8, j, row) for j in range(4))
    xab = xa.astype(BF16)
    r = _sigmoid(_dot(xab, wa_ref[...]) + ba_ref[...])
    ig = _sigmoid(_dot(xab, wx_ref[...]) + bx_ref[...])
    nl = -lam_ref[...]
    sp = jnp.maximum(nl, 0.0) + jnp.log1p(jnp.exp(-jnp.abs(nl)))
    la = (-LRU_C) * r * sp
    a = jnp.exp(la)
    mult = jnp.sqrt(_one_minus_exp(2.0 * la, a * a))
    return dict(xa=xa, xab=xab, r=r, ig=ig, sp=sp, la=la, a=a, mult=mult)


def _scan_fwd(a, u, row):
    sh = 1
    while sh < RT:
        a_s = jnp.where(row >= sh, pltpu.roll(a, sh, 0), 1.0)
        u_s = jnp.where(row >= sh, pltpu.roll(u, sh, 0), 0.0)
        u = a * u_s + u
        a = a * a_s
        sh *= 2
    return a, u


def _scan_bwd(al, g, row):
    sh = 1
    while sh < RT:
        al_s = jnp.where(row < RT - sh, pltpu.roll(al, RT - sh, 0), 1.0)
        g_s = jnp.where(row < RT - sh, pltpu.roll(g, RT - sh, 0), 0.0)
        g = g + al * g_s
        al = al * al_s
        sh *= 2
    return g


def _dense_from_blocks(blocks_ref, dense_ref):
    dense_ref[...] = jnp.zeros_like(dense_ref)
    for h in range(R // HEAD):
        dense_ref[h * HEAD:(h + 1) * HEAD, h * HEAD:(h + 1) * HEAD] = blocks_ref[h].astype(dense_ref.dtype)


def _rec_fwd(proj, conv_w, conv_b, wa_b, ba, wx_b, bx, lam, norm_rec):
    nt = S // RT

    def body(p_ref, cw_ref, cb_ref, wa_ref, ba_ref, wx_ref, bx_ref, lam_ref, nr_ref,
             h_ref, ya_ref, prev8, hc, wad, wxd):
        i = pl.program_id(0)

        @pl.when(i == 0)
        def _():
            prev8[...] = jnp.zeros_like(prev8)
            hc[...] = jnp.zeros_like(hc)
            _dense_from_blocks(wa_ref, wad)
            _dense_from_blocks(wx_ref, wxd)

        row = lax.broadcasted_iota(jnp.int32, (RT, R), 0)
        xp = p_ref[:, 0:R]
        ga = p_ref[:, R:2 * R]
        f = _rec_gates(xp, prev8[...], row, cw_ref, cb_ref, wad, ba_ref, wxd, bx_ref, lam_ref)
        u = f["mult"] * (f["ig"] * f["xa"])
        acum, hh = _scan_fwd(f["a"], u, row)
        h = hh + acum * hc[0:1, :]
        h_ref[...] = h
        hc[0:1, :] = h_ref[RT - 1:RT, :]
        prev8[...] = p_ref[RT - 8:RT, 0:R]
        yp = h * (ga * _sigmoid(ga))
        ya, _, _ = _rms_fwd(yp, nr_ref[...])
        ya_ref[...] = ya.astype(BF16)

    row1 = lambda n: pl.BlockSpec((1, n), lambda i: (0, 0))
    blocks = pl.BlockSpec((R // HEAD, HEAD, HEAD), lambda i: (0, 0, 0))
    return pl.pallas_call(
        body, name="rec_fwd", grid=(nt,),
        in_specs=[pl.BlockSpec((RT, 2 * R), lambda i: (i, 0)), pl.BlockSpec((4, R), lambda i: (0, 0)), row1(R),
                  blocks, row1(R), blocks, row1(R), row1(R), row1(R)],
        out_specs=[pl.BlockSpec((RT, R), lambda i: (i, 0)), pl.BlockSpec((RT, R), lambda i: (i, 0))],
        out_shape=[jax.ShapeDtypeStruct((S, R), F32), jax.ShapeDtypeStruct((S, R), BF16)],
        scratch_shapes=[pltpu.VMEM((8, R), F32), pltpu.VMEM((8, R), F32), pltpu.VMEM((R, R), BF16),
                        pltpu.VMEM((R, R), BF16)],
        compiler_params=_cp(("arbitrary",)),
    )(proj, conv_w, conv_b, wa_b, ba, wx_b, bx, lam, norm_rec)


def _rec_bwd(dproj, d_ya, proj, h_all, conv_w, conv_b, wa_b, ba, wx_b, bx, lam, norm_rec):
    nt = S // RT

    def body(dp_in, dya_ref, p_ref, pprev_ref, h_ref, hprev_ref, cw_ref, cb_ref, wab_ref, ba_ref, wxb_ref, bx_ref,
             lam_ref, nr_ref, dp_ref, dwab_ref, dwxb_ref, sm_ref, nxt8, cg, wa_ref, wx_ref, dwa_ref, dwx_ref):
        i = pl.program_id(0)
        ti = nt - 1 - i

        @pl.when(i == 0)
        def _():
            nxt8[...] = jnp.zeros_like(nxt8)
            cg[...] = jnp.zeros_like(cg)
            dwa_ref[...] = jnp.zeros_like(dwa_ref)
            dwx_ref[...] = jnp.zeros_like(dwx_ref)
            sm_ref[...] = jnp.zeros_like(sm_ref)
            _dense_from_blocks(wab_ref, wa_ref)
            _dense_from_blocks(wxb_ref, wx_ref)

        row = lax.broadcasted_iota(jnp.int32, (RT, R), 0)
        first = (ti > 0).astype(F32)
        xprev8 = pprev_ref[...] * first
        hprev8 = hprev_ref[...] * first
        xp = p_ref[:, 0:R]
        ga = p_ref[:, R:2 * R]
        f = _rec_gates(xp, xprev8, row, cw_ref, cb_ref, wa_ref, ba_ref, wx_ref, bx_ref, lam_ref)
        xa, r, ig, a, mult = f["xa"], f["r"], f["ig"], f["a"], f["mult"]
        h = h_ref[...]
        sg = _sigmoid(ga)
        gate = ga * sg
        yp = h * gate
        _, ypn, rstd = _rms_fwd(yp, nr_ref[...])
        d_yp, dnr = _rms_bwd(dya_ref[...], ypn, rstd, nr_ref[...])
        d_ga = d_yp * h * (sg * (1.0 + ga * (1.0 - sg)))
        dh = d_yp * gate + jnp.where(row == RT - 1, cg[0:1, :], 0.0)
        al = jnp.where(row < RT - 1, pltpu.roll(a, RT - 1, 0), 0.0)
        g = _scan_bwd(al, dh, row)
        cg[0:1, :] = jnp.sum(jnp.where(row == 0, a * g, 0.0), axis=0, keepdims=True)
        h_m1 = _shift_down(h, hprev8, 1, row)
        da = g * h_m1
        ix = ig * xa
        d_mult = g * ix
        d_ig = g * mult * xa
        d_xa = g * mult * ig
        d_la = da * a - d_mult * (a * a) / mult
        d_r = d_la * ((-LRU_C) * f["sp"])
        dsp = jnp.sum(d_la * ((-LRU_C) * r), axis=0, keepdims=True)
        dlam = dsp * (-_sigmoid(-lam_ref[...]))
        d_za = d_r * r * (1.0 - r)
        d_zx = d_ig * ig * (1.0 - ig)
        dzab = d_za.astype(BF16)
        dzxb = d_zx.astype(BF16)
        dwa_ref[...] += _dot_tn(f["xab"], dzab)
        dwx_ref[...] += _dot_tn(f["xab"], dzxb)
        d_xa = d_xa + _dot_nt(dzab, wa_ref[...]) + _dot_nt(dzxb, wx_ref[...])
        d_xp = sum(cw_ref[3 - j:4 - j, :] * _shift_up(d_xa, nxt8[...], j, row) for j in range(4))
        dcw = [jnp.sum(d_xa * _shift_down(xp, xprev8, 3 - k, row), axis=0, keepdims=True) for k in range(4)]
        dp_ref[:, 0:R] = d_xp.astype(BF16)
        dp_ref[:, R:2 * R] = d_ga.astype(BF16)
        dp8 = d_xa[0:8, :]
        nxt8[...] = dp8
        sm_ref[0:1, :] += jnp.sum(d_za, axis=0, keepdims=True)
        sm_ref[1:2, :] += jnp.sum(d_zx, axis=0, keepdims=True)
        sm_ref[2:3, :] += dlam
        sm_ref[3:4, :] += dnr
        sm_ref[4:5, :] += jnp.sum(d_xa, axis=0, keepdims=True)
        for k in range(4):
            sm_ref[8 + k:9 + k, :] += dcw[k]

        @pl.when(i == nt - 1)
        def _():
            for h in range(R // HEAD):
                dwab_ref[h] = dwa_ref[h * HEAD:(h + 1) * HEAD, h * HEAD:(h + 1) * HEAD].astype(BF16)
                dwxb_ref[h] = dwx_ref[h * HEAD:(h + 1) * HEAD, h * HEAD:(h + 1) * HEAD].astype(BF16)

    c0 = lambda shape: pl.BlockSpec(shape, lambda i: (0, 0))
    blocks = pl.BlockSpec((R // HEAD, HEAD, HEAD), lambda i: (0, 0, 0))
    rev = lambda i: nt - 1 - i
    prev8 = lambda i: (jnp.maximum((nt - 1 - i) * (RT // 8) - 1, 0), 0)
    return pl.pallas_call(
        body, name="rec_bwd", grid=(nt,),
        in_specs=[pl.BlockSpec(memory_space=pl.ANY),
                  pl.BlockSpec((RT, R), lambda i: (rev(i), 0)),
                  pl.BlockSpec((RT, 2 * R), lambda i: (rev(i), 0)), pl.BlockSpec((8, R), prev8),
                  pl.BlockSpec((RT, R), lambda i: (rev(i), 0)), pl.BlockSpec((8, R), prev8),
                  c0((4, R)), c0((1, R)), blocks, c0((1, R)), blocks, c0((1, R)), c0((1, R)), c0((1, R))],
        out_specs=[pl.BlockSpec((RT, 2 * R), lambda i: (rev(i), 0)), blocks, blocks, c0((16, R))],
        out_shape=[jax.ShapeDtypeStruct((S, E), BF16), jax.ShapeDtypeStruct((R // HEAD, HEAD, HEAD), BF16),
                   jax.ShapeDtypeStruct((R // HEAD, HEAD, HEAD), BF16), jax.ShapeDtypeStruct((16, R), F32)],
        scratch_shapes=[pltpu.VMEM((8, R), F32), pltpu.VMEM((8, R), F32), pltpu.VMEM((R, R), BF16),
                        pltpu.VMEM((R, R), BF16), pltpu.VMEM((R, R), F32), pltpu.VMEM((R, R), F32)],
        input_output_aliases={0: 0},
        compiler_params=_cp(("arbitrary",)),
    )(dproj, d_ya, proj, proj, h_all, h_all, conv_w, conv_b, wa_b, ba, wx_b, bx, lam, norm_rec)


NPAIR = R // LANES
QB, KB, VB, GB = 2 * R // LANES, 3 * R // LANES, 4 * R // LANES, 5 * R // LANES


def _rope_freq():
    half = HEAD // 2
    inv = np.float32(ROPE_THETA) ** (-(np.arange(half, dtype=np.float32) / np.float32(half)))
    return jnp.asarray(np.tile(inv.astype(np.float32), LANES // half)[None, :])


def _rot_half(x, first):
    return jnp.where(first, -pltpu.roll(x, LANES - HEAD // 2, 1), pltpu.roll(x, HEAD // 2, 1))


def _cos_sin(pos_ref, freq_ref):
    ang = pos_ref[...].astype(F32) * freq_ref[...]
    return jnp.cos(ang), jnp.sin(ang)


SUB = 4


def _stages(d):
    assert d in (1, SUB, SUB * SUB)
    return d > SUB


def _strided_rows(src_ref, d, tmp):
    n = S // d
    if not _stages(d):
        for r in range(d):
            yield r * n, (src_ref[pl.ds(r, n, stride=d), :] if d > 1 else src_ref[...])
        return
    m = S // SUB
    for r in range(SUB):
        tmp[r * m:(r + 1) * m, :] = src_ref[pl.ds(r, m, stride=SUB), :]
    for r in range(SUB):
        for q in range(SUB):
            yield (r + SUB * q) * n, tmp[pl.ds(r * m + q, n, stride=SUB), :]


def _deint(src_ref, dst_ref, d, tmp):
    n = S // d
    for row0, v in _strided_rows(src_ref, d, tmp):
        dst_ref[row0:row0 + n, :] = v.astype(dst_ref.dtype)


def _reint(src_ref, dst_ref, d, accumulate, tmp):
    if _stages(d):
        n, m = S // d, S // SUB
        for r in range(SUB):
            for q in range(SUB):
                tmp[pl.ds(r * m + q, n, stride=SUB), :] = src_ref[(r + SUB * q) * n:(r + SUB * q + 1) * n, :]
        src_ref, d = tmp, SUB
    n = S // d
    for r in range(d):
        idx = (pl.ds(r, n, stride=d), slice(None)) if d > 1 else (slice(None), slice(None))
        v = src_ref[r * n:(r + 1) * n, :]
        if accumulate:
            dst_ref[idx] = dst_ref[idx] + v
        else:
            dst_ref[idx] = v


def _deint_heads(src_ref, dst0, dst1, d, tmp):
    n = S // d
    hm0 = lax.broadcasted_iota(jnp.int32, (n, LANES), 1) < HEAD
    for row0, v in _strided_rows(src_ref, d, tmp):
        dst0[row0:row0 + n, :] = jnp.where(hm0, v, 0.0).astype(BF16)
        dst1[row0:row0 + n, :] = jnp.where(hm0, 0.0, v).astype(BF16)


def _reint_prev(src_ref, dst_ref, d):
    n = S // d
    if n == BLK:
        return
    for r in range(d):
        idx = (pl.ds(r, n - BLK, stride=d), slice(None)) if d > 1 else (slice(0, n - BLK), slice(None))
        dst_ref[idx] = dst_ref[idx] + src_ref[r * n + BLK:(r + 1) * n, :]


def _pair_masks():
    qi = lax.broadcasted_iota(jnp.int32, (BLK, 2 * BLK), 0)
    ki = lax.broadcasted_iota(jnp.int32, (BLK, 2 * BLK), 1) & (BLK - 1)
    return ki <= qi, ki >= qi


def _two(ref0, ref1, st, axis):
    return jnp.concatenate([ref0[pl.ds(st, BLK), :], ref1[pl.ds(st, BLK), :]], axis=axis)


ATT_UNROLL = 8


def _att_fwd(proj, cos, sin, w_out):
    def body(q_ref, k_ref, v_ref, cos_ref, sin_ref, w_ref, att_ref, qr_ref, kr_ref, lse_ref, wbf_ref,
             qd, kd0, kd1, vd0, vd1, od, ld, tmp, on, ln, wbuf, *wsems):
        wg = _WeightGather(w_ref, wbuf, *wsems)
        pl.when(pl.program_id(0) == 0)(wg.start)
        pl.when(pl.program_id(0) == 1)(wg.forward)
        lane = lax.broadcasted_iota(jnp.int32, (S, LANES), 1)
        first = (lane & (HEAD // 2)) == 0
        cos, sin = cos_ref[...], sin_ref[...]
        q = q_ref[...]
        k = k_ref[...]
        qr_ref[...] = (q * cos + _rot_half(q, first) * sin) * (HEAD ** -0.5)
        kr_ref[...] = k * cos + _rot_half(k, first) * sin
        hm0 = lax.broadcasted_iota(jnp.int32, (BLK, LANES), 1) < HEAD
        top = lax.broadcasted_iota(jnp.int32, (2 * BLK, LANES), 0) < BLK
        ones2 = (top == (lax.broadcasted_iota(jnp.int32, (2 * BLK, LANES), 1) < HEAD)).astype(BF16)
        mc2, mp2 = _pair_masks()

        for pi, d in enumerate(PATTERNS):
            nb = S // d // BLK
            _deint(qr_ref, qd, d, tmp)
            _deint_heads(kr_ref, kd0, kd1, d, tmp)
            _deint_heads(v_ref, vd0, vd1, d, tmp)

            def blk(b, carry):
                st = pl.multiple_of(b * BLK, BLK)
                qb = qd[pl.ds(st, BLK), :]
                sc = jnp.where(mc2, _dot_nt(qb, _two(kd0, kd1, st, 0)), NEG)
                mx = sc
                if nb > 1:
                    stp = pl.multiple_of(jnp.maximum(b - 1, 0) * BLK, BLK)
                    mp = jnp.logical_and(mp2, lax.rem(b, nb) != 0)
                    sp = jnp.where(mp, _dot_nt(qb, _two(kd0, kd1, stp, 0)), NEG)
                    mx = jnp.maximum(sc, sp)
                m0 = jnp.max(mx[:, 0:BLK], axis=1, keepdims=True)
                m1 = jnp.max(mx[:, BLK:2 * BLK], axis=1, keepdims=True)
                mf = jnp.concatenate([jnp.broadcast_to(m0, (BLK, BLK)), jnp.broadcast_to(m1, (BLK, BLK))], axis=1)
                o = _dot(jnp.exp(sc - mf).astype(BF16), jnp.concatenate([_two(vd0, vd1, st, 0), ones2], axis=1))
                if nb > 1:
                    o = o + _dot(jnp.exp(sp - mf).astype(BF16), jnp.concatenate([_two(vd0, vd1, stp, 0), ones2], axis=1))
                l = o[:, LANES:2 * LANES]
                od[pl.ds(st, BLK), :] = o[:, 0:LANES] / l
                ld[pl.ds(st, BLK), :] = jnp.where(hm0, m0, m1) + jnp.log(l)
                return carry

            lax.fori_loop(0, S // BLK, blk, 0, unroll=ATT_UNROLL)
            _reint(od, on.at[pi], d, False, tmp)
            _reint(ld, ln.at[pi], d, False, tmp)

        l0, l1, l2 = ln[0], ln[1], ln[2]
        m = jnp.maximum(jnp.maximum(l0, l1), l2)
        e0, e1, e2 = jnp.exp(l0 - m), jnp.exp(l1 - m), jnp.exp(l2 - m)
        den = e0 + e1 + e2
        att_ref[...] = (e0 * on[0] + e1 * on[1] + e2 * on[2]) / den
        lse_ref[...] = m + jnp.log(den)

        @pl.when(pl.program_id(0) == NPAIR - 1)
        def _():
            wg.finish()
            wbf_ref[...] = wbuf[...]

    col = lambda c0: pl.BlockSpec((S, LANES), lambda p: (0, c0 + p))
    out = pl.BlockSpec((S, LANES), lambda p: (0, p))
    tab = pl.BlockSpec((S, LANES), lambda p: (0, 0))
    vm = pl.BlockSpec(memory_space=pltpu.VMEM)
    return pl.pallas_call(
        body, name="att_fwd", grid=(NPAIR,),
        in_specs=[col(QB), col(KB), col(VB), tab, tab, vm],
        out_specs=[out, out, out, out, vm],
        out_shape=[jax.ShapeDtypeStruct((S, R), F32)] * 4 + [jax.ShapeDtypeStruct((NCHIP,) + w_out.shape, BF16)],
        scratch_shapes=[pltpu.VMEM((S, LANES), BF16)] * 5 + [pltpu.VMEM((S, LANES), F32)] * 3
        + [pltpu.VMEM((3, S, LANES), F32)] * 2 + [pltpu.VMEM((NCHIP,) + w_out.shape, BF16)] + _WeightGather.SEMS,
        compiler_params=_cp(("arbitrary",)),
    )(proj, proj, proj, cos, sin, w_out)


def _att_bwd(dproj, d_att, att, lse, qr, kr, proj, cos, sin, gw_out4):
    out_units = [(j, j, 0) for j in range(NCHIP)]

    nblk = S // BLK

    def body(dp_in, do_ref, o_ref, lse_ref, qr_ref, kr_ref, v_ref, cos_ref, sin_ref, gw_ref, dp_ref, gout_ref,
             qd, kd0, kd1, vd0, vd1, dod, kt, packn, packd, dqd, dkcd, dkpd, dvcd, dvpd,
             dqn, dkn, dvn, tmp, rows, trs, pts, dss, stage, sems, gred, *rs_scratch):
        p = pl.program_id(0)
        rs = _ReduceScatter(gw_ref, gred, out_units, *rs_scratch)
        for step, piece in enumerate((rs.start_halves, rs.send_partials, rs.reduce_owned)):
            pl.when(p == step)(piece)

        @pl.when(p == NPAIR - 1)
        def _():
            rs.finish()
            gout_ref[...] = gred[...]

        lane = lax.broadcasted_iota(jnp.int32, (S, LANES), 1)
        hms = lane < HEAD
        prod = do_ref[...] * o_ref[...]
        d0 = jnp.sum(jnp.where(hms, prod, 0.0), axis=1, keepdims=True)
        d1 = jnp.sum(jnp.where(hms, 0.0, prod), axis=1, keepdims=True)
        lse = lse_ref[...]
        quarter = HEAD // 2
        packn[...] = jnp.where(lane < quarter, lse,
                               jnp.where(hms, pltpu.roll(lse, LANES - quarter, 1), jnp.where(lane < 3 * quarter, d0, d1)))
        dqn[...] = jnp.zeros_like(dqn)
        dkn[...] = jnp.zeros_like(dkn)
        dvn[...] = jnp.zeros_like(dvn)
        hm0 = lax.broadcasted_iota(jnp.int32, (BLK, LANES), 1) < HEAD
        key = lax.broadcasted_iota(jnp.int32, (2 * BLK, BLK), 0) & (BLK - 1)
        qry = lax.broadcasted_iota(jnp.int32, (2 * BLK, BLK), 1)
        mct, mpt = key <= qry, key >= qry

        for d in PATTERNS:
            nb = S // d // BLK
            _deint(qr_ref, qd, d, tmp)
            _deint_heads(kr_ref, kd0, kd1, d, tmp)
            _deint_heads(v_ref, vd0, vd1, d, tmp)
            _deint(do_ref, dod, d, tmp)
            _deint(packn, packd, d, tmp)

            sides = (0, 1) if nb > 1 else (0,)

            def probs(b, carry):
                st = pl.multiple_of(b * BLK, BLK)
                kt[b] = _two(kd0, kd1, st, 0).astype(F32).T.astype(BF16)
                trs[b] = packd[pl.ds(st, BLK), :].T
                for j in range(4):
                    rows[b, j:j + 1, :] = trs[b, j * quarter:j * quarter + 1, :]
                qb, dob = qd[pl.ds(st, BLK), :], dod[pl.ds(st, BLK), :]
                both = lambda j: jnp.concatenate([jnp.broadcast_to(rows[b, j:j + 1, :], (BLK, BLK)),
                                                  jnp.broadcast_to(rows[b, j + 1:j + 2, :], (BLK, BLK))], axis=0)
                lbt, dlt = both(0), both(2)
                for sd in sides:
                    stk = pl.multiple_of(jnp.maximum(b - sd, 0) * BLK, BLK)
                    mask = mct if sd == 0 else jnp.logical_and(mpt, lax.rem(b, nb) != 0)
                    k2, v2 = _two(kd0, kd1, stk, 0), _two(vd0, vd1, stk, 0)
                    pt = jnp.where(mask, jnp.exp(_dot_nt(k2, qb) - lbt), 0.0)
                    pts[b, sd] = pt.astype(BF16)
                    dss[b, sd] = (pt * (_dot_nt(v2, dob) - dlt)).astype(BF16)
                return carry

            lax.fori_loop(0, nblk, probs, 0, unroll=ATT_UNROLL)

            def prods(b, carry):
                st = pl.multiple_of(b * BLK, BLK)
                qb, dob = qd[pl.ds(st, BLK), :], dod[pl.ds(st, BLK), :]
                dq_t = None
                for sd in sides:
                    dst, ptb = dss[b, sd], pts[b, sd]
                    rk, rv = _dot(dst, qb), _dot(ptb, dob)
                    dqs = _dot(kt[jnp.maximum(b - sd, 0)], dst)
                    dq_t = dqs if dq_t is None else dq_t + dqs
                    dk, dv = (dkcd, dvcd) if sd == 0 else (dkpd, dvpd)
                    dk[pl.ds(st, BLK), :] = jnp.where(hm0, rk[0:BLK], rk[BLK:2 * BLK])
                    dv[pl.ds(st, BLK), :] = jnp.where(hm0, rv[0:BLK], rv[BLK:2 * BLK])
                dqd[pl.ds(st, BLK), :] = dq_t.T
                return carry

            lax.fori_loop(0, nblk, prods, 0, unroll=ATT_UNROLL)
            _reint(dqd, dqn, d, True, tmp)
            _reint(dkcd, dkn, d, True, tmp)
            _reint(dvcd, dvn, d, True, tmp)
            _reint_prev(dkpd, dkn, d)
            _reint_prev(dvpd, dvn, d)

        lane = lax.broadcasted_iota(jnp.int32, (S, LANES), 1)
        first = (lane & (HEAD // 2)) == 0
        cos, sin = cos_ref[...], sin_ref[...]
        dq = dqn[...] * (HEAD ** -0.5)
        dk = dkn[...]
        stage[0] = (dq * cos - _rot_half(dq, first) * sin).astype(BF16)
        stage[1] = (dk * cos - _rot_half(dk, first) * sin).astype(BF16)
        stage[2] = dvn[...].astype(BF16)
        copies = [pltpu.make_async_copy(stage.at[j], dp_ref.at[:, pl.ds((2 + j) * R + p * LANES, LANES)], sems.at[j])
                  for j in range(3)]
        for cp in copies:
            cp.start()
        for cp in copies:
            cp.wait()

    blk = pl.BlockSpec((S, LANES), lambda p: (0, p))
    tab = pl.BlockSpec((S, LANES), lambda p: (0, 0))
    vm = pl.BlockSpec(memory_space=pltpu.VMEM)
    _, orows, ocols = gw_out4.shape
    return pl.pallas_call(
        body, name="att_bwd", grid=(NPAIR,),
        in_specs=[pl.BlockSpec(memory_space=pl.ANY), blk, blk, blk, blk, blk,
                  pl.BlockSpec((S, LANES), lambda p: (0, VB + p)), tab, tab, vm],
        out_specs=[pl.BlockSpec(memory_space=pl.ANY), vm],
        out_shape=[jax.ShapeDtypeStruct((S, E), BF16), jax.ShapeDtypeStruct((orows, ocols), F32)],
        scratch_shapes=[pltpu.VMEM((S, LANES), BF16)] * 6 + [pltpu.VMEM((nblk, LANES, 2 * BLK), BF16)]
        + [pltpu.VMEM((S, LANES), F32)] * 11
        + [pltpu.VMEM((nblk, 8, BLK), F32), pltpu.VMEM((nblk, LANES, BLK), F32)]
        + [pltpu.VMEM((nblk, 2, 2 * BLK, BLK), BF16)] * 2
        + [pltpu.VMEM((3, S, LANES), BF16), pltpu.SemaphoreType.DMA((3,)), pltpu.VMEM((orows, ocols), F32)]
        + _ReduceScatter.scratch(NCHIP, orows, ocols, 1),
        input_output_aliases={0: 0},
        compiler_params=_cp(("arbitrary",)),
    )(dproj, d_att, att, lse, qr, kr, proj, cos, sin, gw_out4)


def _out_fwd_bwd(ya, att, proj, w_out_bf, x, target, mod, norm_post, norm_att):
    ts = 512

    def body(ya_ref, att_ref, gb_ref, w_ref, x_ref, t_ref, mod_ref, npost_ref, natt_ref,
             gx_ref, dya_ref, datt_ref, dgb_ref, gw_ref, acc_ref):
        i = pl.program_id(0)

        @pl.when(i == 0)
        def _():
            gw_ref[...] = jnp.zeros_like(gw_ref)
            acc_ref[...] = jnp.zeros_like(acc_ref)

        gate = mod_ref[:, 2 * D:3 * D]
        att = att_ref[...]
        gb = gb_ref[...]
        sg = _sigmoid(gb)
        silu = gb * sg
        ybp = att * silu
        yb, ybn, rstd_b = _rms_fwd(ybp, natt_ref[...])
        cat = jnp.concatenate([ya_ref[...], yb.astype(BF16)], axis=1)
        mix = _dot(cat, w_ref[...])
        rn, mn, rstd_m = _rms_fwd(mix, npost_ref[...])
        err = x_ref[...] + gate * rn - t_ref[...]
        dy = err * (1.0 / D)
        gx_ref[...] = dy
        dmix, dnpost = _rms_bwd(dy * gate, mn, rstd_m, npost_ref[...])
        dmb = dmix.astype(BF16)
        gw_ref[...] += _dot_tn(cat, dmb)
        dcat = _dot_nt(dmb, w_ref[...])
        dya_ref[...] = dcat[:, 0:R]
        dybp, dnatt = _rms_bwd(dcat[:, R:2 * R], ybn, rstd_b, natt_ref[...])
        datt_ref[...] = dybp * silu
        dgb_ref[...] = (dybp * att * (sg * (1.0 + gb * (1.0 - sg)))).astype(BF16)
        acc_ref[0:1, :] += jnp.sum(dy * rn, axis=0, keepdims=True)
        acc_ref[1:2, :] += dnpost
        acc_ref[2:3, 0:R] += dnatt
        acc_ref[3:4, :] += jnp.sum(jnp.sum(err * err, axis=1, keepdims=True), axis=0, keepdims=True)

    tile = lambda w: pl.BlockSpec((ts, w), lambda i: (i, 0))
    c0 = lambda shape: pl.BlockSpec(shape, lambda i: (0, 0))
    return pl.pallas_call(
        body, name="out_fwd_bwd", grid=(S // ts,),
        in_specs=[tile(R), tile(R), pl.BlockSpec((ts, R), lambda i: (i, 5)), c0((D, D)), tile(D), tile(D),
                  c0((1, 3 * D)), c0((1, D)), c0((1, R))],
        out_specs=[tile(D), tile(R), tile(R), pl.BlockSpec((ts, R), lambda i: (i, 5)), c0((D, D)), c0((8, D))],
        out_shape=[jax.ShapeDtypeStruct((S, D), F32), jax.ShapeDtypeStruct((S, R), F32),
                   jax.ShapeDtypeStruct((S, R), F32), jax.ShapeDtypeStruct((S, E), BF16),
                   jax.ShapeDtypeStruct((D, D), F32), jax.ShapeDtypeStruct((8, D), F32)],
        compiler_params=_cp(("arbitrary",)),
    )(ya, att, proj, w_out_bf, x, target, mod, norm_post, norm_att)


UC = 256
UPC = EC // UC


NU = E // UC


def _unit_of_step(i):
    return (i % NCHIP) * UPC + i // NCHIP


def _in_proj_bwd(ht, dproj, w_in_bf, x, gx1, mod, norm_pre, smalls):
    ts = 256
    nt = S // ts
    half = D // 2
    units = [_unit_of_step(k) for k in range(NU)]
    owners = [u // UPC for u in units]
    ns = len(smalls)

    def body(*refs):
        (ht_ref, dpu_ref, dp_ref, w_hbm, x_ref, gx1_ref, mod_ref, np_ref), refs = refs[:8], refs[8:]
        small_in, refs = refs[:ns], refs[ns:]
        (gx_ref, gin_ref), refs = refs[:2], refs[2:]
        small_out, (acc_out,), refs = refs[:ns], refs[ns:ns + 1], refs[ns + 1:]
        mine, sib, tmp, stage, got, red, acc_ref, hs, hr, ps, pr, bs, br = refs[:13]
        early = _SmallGather(small_in, small_out, *refs[13:16])
        late = _SmallGather([acc_ref], [acc_out], *refs[16:19])
        w_ref, w_sem = refs[19:21]
        i = pl.program_id(0)
        w_copy = pltpu.make_async_copy(w_hbm, w_ref, w_sem)
        pl.when(i == 0)(w_copy.start)
        pl.when(i == NU)(w_copy.wait)
        xx, yy, c = _me()
        ci = 2 * xx + yy
        r0 = pl.multiple_of(c * half, half)
        r1 = pl.multiple_of((1 - c) * half, half)
        pl.when(i == 0)(early.start)
        pl.when(i == NU)(early.forward)

        def exch(k):
            return _remote(tmp.at[k % 2], sib.at[k], hs.at[k], hr.at[k], 1)

        def partial(k, sender):
            return pltpu.make_async_remote_copy(
                src_ref=stage.at[k], dst_ref=got.at[units[k] % UPC, sender], send_sem=ps.at[k],
                recv_sem=pr.at[k, sender], device_id=(owners[k] // 2, owners[k] % 2, c), device_id_type=MESH)

        def back(k, start):
            off = (units[k] % UPC) * UC
            blk = red.at[pl.ds(start, half), off:off + UC]
            return _remote(blk, blk, bs.at[k], br.at[k], 1)

        for k in range(NU + 1):
            @pl.when(i == k)
            def _():
                if k < NU:
                    if k >= 2:
                        exch(k - 2).wait_send()
                    dpu = dpu_ref[...]
                    tmp[k % 2] = _dot(ht_ref[pl.ds(r1, half), :], dpu)
                    exch(k).start()
                    mine[k] = _dot(ht_ref[pl.ds(r0, half), :], dpu)
                if k >= 1:
                    exch(k - 1).wait_recv()
                    mine[k - 1] += sib[k - 1]

                    @pl.when(ci != owners[k - 1])
                    def _():
                        stage[k - 1] = mine[k - 1].astype(BF16)
                        partial(k - 1, ci).start()

        @pl.when(i == NU)
        def _():
            acc_ref[...] = jnp.zeros_like(acc_ref)

        @pl.when(i >= NU)
        def _():
            dh = sum(_dot_nt(dp_ref[:, j * EC:(j + 1) * EC], w_ref[j]) for j in range(NCHIP))
            hp, xn, rstd = _rms_fwd(x_ref[...], np_ref[...])
            dx, dnp = _rms_bwd(dh * (1.0 + mod_ref[:, D:2 * D]), xn, rstd, np_ref[...])
            gx_ref[...] = gx1_ref[...] + dx
            acc_ref[0:1, :] += jnp.sum(dh, axis=0, keepdims=True)
            acc_ref[1:2, :] += jnp.sum(dh * hp, axis=0, keepdims=True)
            acc_ref[2:3, :] += dnp

        for t in range(UPC):
            @pl.when(i == NU + 1 + 2 * t)
            def _():
                for k in range(NCHIP * t, NCHIP * (t + 1)):
                    @pl.when(ci == owners[k])
                    def _():
                        off = (units[k] % UPC) * UC
                        red[pl.ds(r0, half), off:off + UC] = mine[k]
                        for s in range(NCHIP):
                            if s != owners[k]:
                                partial(k, s).wait_recv()
                                red[pl.ds(r0, half), off:off + UC] += got[units[k] % UPC, s].astype(F32)
                        back(k, r0).start()

        @pl.when(i == NU + nt - 1)
        def _():
            late.start()
            exch(NU - 2).wait_send()
            exch(NU - 1).wait_send()
            for k in range(NU):
                @pl.when(ci == owners[k])
                def _():
                    back(k, r1).wait_recv()
                    back(k, r0).wait_send()

                @pl.when(ci != owners[k])
                def _():
                    partial(k, ci).wait_send()
            gin_ref[...] = red[...]
            early.finish()
            late.forward()
            late.finish()

    tile = lambda w: pl.BlockSpec((ts, w), lambda i: (jnp.maximum(i - NU, 0), 0))
    c0 = lambda shape: pl.BlockSpec(shape, lambda i: (0, 0))
    vm = pl.BlockSpec(memory_space=pltpu.VMEM)
    hbm = pl.BlockSpec(memory_space=pl.ANY)
    gathered = [jax.ShapeDtypeStruct((NDEV,) + a.shape, a.dtype) for a in smalls] + [jax.ShapeDtypeStruct((NDEV, 8, D), F32)]
    return pl.pallas_call(
        body, name="in_proj_bwd", grid=(NU + nt,),
        in_specs=[vm, pl.BlockSpec((S, UC), lambda i: (0, _unit_of_step(jnp.minimum(i, NU - 1)))), tile(E),
                  hbm, tile(D), tile(D), c0((1, 3 * D)), c0((1, D))] + [vm] * ns,
        out_specs=[tile(D), vm] + [hbm] * (ns + 1),
        out_shape=[jax.ShapeDtypeStruct((S, D), F32), jax.ShapeDtypeStruct((D, EC), F32)] + gathered,
        scratch_shapes=[pltpu.VMEM((NU, half, UC), F32), pltpu.VMEM((NU, half, UC), F32),
                        pltpu.VMEM((2, half, UC), F32), pltpu.VMEM((NU, half, UC), BF16),
                        pltpu.VMEM((UPC, NCHIP, half, UC), BF16), pltpu.VMEM((D, EC), F32), pltpu.VMEM((8, D), F32),
                        pltpu.SemaphoreType.DMA((NU,)), pltpu.SemaphoreType.DMA((NU,)),
                        pltpu.SemaphoreType.DMA((NU,)), pltpu.SemaphoreType.DMA((NU, NCHIP)),
                        pltpu.SemaphoreType.DMA((NU,)), pltpu.SemaphoreType.DMA((NU,))]
        + _SmallGather.sems(ns) + _SmallGather.sems(1)
        + [pltpu.VMEM((NCHIP, D, EC), BF16), pltpu.SemaphoreType.DMA],
        compiler_params=_cp(("arbitrary",)),
    )(ht, dproj, dproj, w_in_bf, x, gx1, mod, norm_pre, *smalls)


def _local_step(x, cos, sin, target, mod, w_in_bf, proj, ht, w_out, conv_w, p):
    rec_p = (conv_w, p["conv_b"], p["w_rg_a"], p["b_rg_a"], p["w_rg_x"], p["b_rg_x"], p["lru_lambda"], p["norm_rec"])
    h_all, ya = _rec_fwd(proj, *rec_p)
    att, qr, kr, lse, w_out_bf = _att_fwd(proj, cos, sin, w_out)
    gx1, d_ya, d_att, dproj, gw_out, acc_o = _out_fwd_bwd(ya, att, proj, w_out_bf.reshape(D, D), x, target, mod,
                                                           p["norm_post"], p["norm_att"])
    dproj, g_out = _att_bwd(dproj, d_att, att, lse, qr, kr, proj, cos, sin, gw_out.reshape(NCHIP, D // NCHIP, D))
    dproj, dwa, dwx, sm = _rec_bwd(dproj, d_ya, proj, h_all, *rec_p)
    grad_x, g_in, *gathered = _in_proj_bwd(ht, dproj, w_in_bf, x, gx1, mod, p["norm_pre"], [acc_o, sm, dwa, dwx])
    return grad_x, g_in, g_out, gathered


def _me():
    return lax.axis_index("x"), lax.axis_index("y"), lax.axis_index("c")


def _flip(v, bit):
    return 1 - v if bit else v


def _peer(rel):
    x, y, c = _me()
    return (_flip(x, rel & 4), _flip(y, rel & 2), _flip(c, rel & 1))


def _remote(src, dst, send_sem, recv_sem, rel):
    return pltpu.make_async_remote_copy(src_ref=src, dst_ref=dst, send_sem=send_sem, recv_sem=recv_sem,
                                        device_id=_peer(rel), device_id_type=MESH)


class _WeightGather:
    SEMS = [pltpu.SemaphoreType.DMA((NCHIP - 1,))] * 4

    def __init__(self, w_ref, out_ref, send_sems, recv_sems, fsend_sems, frecv_sems):
        x, y, c = _me()
        self.w, self.out, self.ci = w_ref, out_ref, 2 * x + y
        self.half = w_ref.shape[0] // 2
        self.r0 = pl.multiple_of(c * self.half, self.half)
        self.r1 = pl.multiple_of((1 - c) * self.half, self.half)
        self.sems = (send_sems, recv_sems, fsend_sems, frecv_sems)

    def _ici(self, chip, k):
        blk = self.out.at[chip, pl.ds(self.r0, self.half), :]
        return _remote(blk, blk, self.sems[0].at[k - 1], self.sems[1].at[k - 1], 2 * k)

    def _d2d(self, chip, start, k):
        blk = self.out.at[chip, pl.ds(start, self.half), :]
        return _remote(blk, blk, self.sems[2].at[k - 1], self.sems[3].at[k - 1], 1)

    def start(self, diagonal=True):
        self.out[self.ci] = self.w[...].astype(BF16)
        for k in range(1, NCHIP if diagonal else NCHIP - 1):
            self._ici(self.ci, k).start()

    def _relay(self, chip, piece, k):
        q = self.half // 2
        blk = self.out.at[chip, pl.ds(self.r0 + piece * q, q), :]
        return _remote(blk, blk, self.relay_sems[0].at[piece], self.relay_sems[1].at[piece], 2 * k)

    def neighbours_landed(self, relay_send_sems, relay_recv_sems):
        self.relay_sems = (relay_send_sems, relay_recv_sems)
        for k in (1, 2):
            self._ici(self.ci ^ k, k).wait_recv()
        self._relay(self.ci ^ 2, 0, 1).start()
        self._relay(self.ci ^ 1, 1, 2).start()
        for k in (1, 2):
            self._d2d(self.ci ^ k, self.r0, k).start()

    def sibling_landed(self, k):
        self._d2d(self.ci ^ k, self.r1, k).wait_recv()

    def diagonal_landed(self):
        for piece, k in ((0, 1), (1, 2)):
            self._relay(self.ci ^ 3, piece, k).wait_recv()
        self._d2d(self.ci ^ 3, self.r0, 3).start()
        self._d2d(self.ci ^ 3, self.r1, 3).wait_recv()

    def finish_relayed(self):
        for k in (1, 2):
            self._ici(self.ci, k).wait_send()
        self._relay(self.ci ^ 2, 0, 1).wait_send()
        self._relay(self.ci ^ 1, 1, 2).wait_send()
        for k in range(1, NCHIP):
            self._d2d(self.ci ^ k, self.r0, k).wait_send()

    def forward(self):
        for k in range(1, NCHIP):
            self._ici(self.ci ^ k, k).wait_recv()
            self._d2d(self.ci ^ k, self.r0, k).start()

    def finish(self):
        for k in range(1, NCHIP):
            self._d2d(self.ci ^ k, self.r1, k).wait_recv()
        self.finish_sends()

    def finish_sends(self):
        for k in range(1, NCHIP):
            self._ici(self.ci, k).wait_send()
            self._d2d(self.ci ^ k, self.r0, k).wait_send()


class _SmallGather:
    @staticmethod
    def sems(n):
        return [pltpu.SemaphoreType.DMA((n, 7)), pltpu.SemaphoreType.DMA((n, 7)), pltpu.SemaphoreType.DMA((n,))]

    def __init__(self, srcs, outs, send_sems, recv_sems, local_sems):
        x, y, c = _me()
        self.srcs, self.outs = list(srcs), list(outs)
        self.ss, self.rs, self.ls = send_sems, recv_sems, local_sems
        self.ci, self.c = 2 * x + y, c
        self.me = 2 * self.ci + c

    def _own(self, a, slot, rel):
        return _remote(self.srcs[a], self.outs[a].at[self.me], self.ss.at[a, slot], self.rs.at[a, slot], rel)

    def _block(self, a, idx, slot, rel):
        blk = self.outs[a].at[idx]
        return _remote(blk, blk, self.ss.at[a, slot], self.rs.at[a, slot], rel)

    def _local(self, a):
        return pltpu.make_async_copy(self.srcs[a], self.outs[a].at[self.me], self.ls.at[a])

    def start(self):
        for a in range(len(self.srcs)):
            self._local(a).start()
            self._own(a, 0, 1).start()
            for k in range(1, NCHIP):
                self._own(a, k, 2 * k).start()

    def forward(self):
        for a in range(len(self.srcs)):
            for k in range(1, NCHIP):
                idx = 2 * (self.ci ^ k) + self.c
                self._block(a, idx, k, 2 * k).wait_recv()
                self._block(a, idx, 3 + k, 1).start()

    def finish(self):
        for a in range(len(self.srcs)):
            self._block(a, 2 * self.ci + 1 - self.c, 0, 1).wait_recv()
            for k in range(1, NCHIP):
                self._block(a, 2 * (self.ci ^ k) + 1 - self.c, 3 + k, 1).wait_recv()
            self._own(a, 0, 1).wait_send()
            for k in range(1, NCHIP):
                self._own(a, k, 2 * k).wait_send()
                self._block(a, 2 * (self.ci ^ k) + self.c, 3 + k, 1).wait_send()
            self._local(a).wait()


def _start_in_proj(c, conv_w, w_ada, b_ada, w_in, pos, x, norm_pre, order):
    ts = 512
    nt = S // ts
    wc = D + conv_w.size

    def body(order_ref, c_ref, cw_ref, wada_ref, b_ref, win_ref, pos_ref, freq_ref, x_ref, np_ref,
             g0_ref, conv_ref, mod_ref, wbf_ref, cos_ref, sin_ref, proj_ref, ht_ref,
             crow_ref, g0s, modp, modb, wbuf, hb_all, cs, cr, ms, mr, ws, wr, fs, fr, local_sems, ys, yr, osem):
        s, t = pl.program_id(0), pl.program_id(1)
        x, y, c = _me()
        ci = 2 * x + y
        me = 2 * ci + c
        wg = _WeightGather(win_ref, wbuf, ws, wr, fs, fr)
        cw = R // NCHIP

        @pl.when(jnp.logical_and(s == 0, t == 0))
        def _():
            wg.start(diagonal=False)
            crow_ref[:, 0:D] = c_ref[...]
            for k in range(4):
                crow_ref[:, D + k * cw:D + (k + 1) * cw] = cw_ref[k:k + 1, :]
            mine = pltpu.make_async_copy(crow_ref, g0s.at[pl.ds(me, 1), :], local_sems.at[0])
            mine.start()
            csend = [_remote(crow_ref, g0s.at[pl.ds(me, 1), :], cs.at[r - 1], cr.at[r - 1], r) for r in range(1, NDEV)]
            for cp in csend:
                cp.start()
            cos_ref[...], sin_ref[...] = _cos_sin(pos_ref, freq_ref)
            for r in range(1, NDEV):
                px, py, pc = _peer(r)
                _remote(crow_ref, g0s.at[pl.ds(4 * px + 2 * py + pc, 1), :], cs.at[r - 1], cr.at[r - 1], r).wait_recv()
            mine.wait()
            cv = g0s[:, 0:D]
            sc = cv * _sigmoid(cv)
            scb = jnp.concatenate([sc, jnp.zeros_like(sc)], axis=0).astype(BF16)
            b_cols = sum(jnp.where(ci == j, b_ref[:, j * EC:(j + 1) * EC], 0.0) for j in range(NCHIP))
            modp[...] = _dot(scb, wada_ref[...].astype(BF16))[0:NDEV, :] + b_cols
            own = pltpu.make_async_copy(modp.at[pl.ds(me, 1), :], modb.at[ci], local_sems.at[1])
            own.start()
            msend = []
            for k in range(1, NCHIP):
                cp = _remote(modp.at[pl.ds(2 * (ci ^ k) + c, 1), :], modb.at[ci], ms.at[k - 1], mr.at[k - 1], 2 * k)
                cp.start()
                msend.append(cp)
            for k in range(1, NCHIP):
                _remote(modp.at[pl.ds(me, 1), :], modb.at[ci ^ k], ms.at[k - 1], mr.at[k - 1], 2 * k).wait_recv()
            own.wait()
            for j in range(NCHIP):
                mod_ref[:, j * EC:(j + 1) * EC] = modb[j]
            for cp in csend + msend:
                cp.wait_send()
            g0_ref[...] = g0s[...]
            for j in range(NCHIP):
                for k in range(4):
                    conv_ref[k:k + 1, j * cw:(j + 1) * cw] = g0s[2 * j:2 * j + 1, D + k * cw:D + (k + 1) * cw]

        def keep(k):
            return pltpu.make_async_copy(wbuf.at[ci ^ k], wbf_ref.at[ci ^ k], osem.at[k])

        @pl.when(jnp.logical_and(s == 1, t == 0))
        def _():
            keep(0).start()
            wg.neighbours_landed(ys, yr)
            wg.sibling_landed(1)
            keep(1).start()

        @pl.when(jnp.logical_and(s == 2, t == 0))
        def _():
            wg.sibling_landed(2)
            keep(2).start()

        @pl.when(jnp.logical_and(s == 3, t == 0))
        def _():
            wg.relay_sems = (ys, yr)
            wg.diagonal_landed()
            keep(3).start()

        rows = pl.ds(pl.multiple_of(t * ts, ts), ts)

        @pl.when(s == 0)
        def _():
            hp, _, _ = _rms_fwd(x_ref[...], np_ref[...])
            h = hp * (1.0 + mod_ref[:, D:2 * D]) + mod_ref[:, 0:D]
            hb_all[rows, :] = h.astype(BF16)
            ht_ref[...] = h.T.astype(BF16)

        proj_ref[...] = _dot(hb_all[rows, :], wbuf[ci ^ s])

        @pl.when(jnp.logical_and(s == NCHIP - 1, t == nt - 1))
        def _():
            wg.relay_sems = (ys, yr)
            wg.finish_relayed()
            for k in range(NCHIP):
                keep(k).wait()

    vm = pl.BlockSpec(memory_space=pltpu.VMEM)
    first_pass = lambda s, t: jnp.where(s == 0, t, nt - 1)
    grid_spec = pltpu.PrefetchScalarGridSpec(
        num_scalar_prefetch=1, grid=(NCHIP, nt),
        in_specs=[vm, vm, vm, vm, vm, vm, vm, pl.BlockSpec((ts, D), lambda s, t, o: (first_pass(s, t), 0)),
                  pl.BlockSpec((1, D), lambda s, t, o: (0, 0))],
        out_specs=[vm, vm, vm, pl.BlockSpec(memory_space=pl.ANY), vm, vm,
                   pl.BlockSpec((ts, EC), lambda s, t, o: (t, o[s])),
                   pl.BlockSpec((D, ts), lambda s, t, o: (0, first_pass(s, t)))],
        scratch_shapes=[pltpu.VMEM((1, wc), F32),
                        pltpu.VMEM((NDEV, wc), F32), pltpu.VMEM((NDEV, EC), F32), pltpu.VMEM((NCHIP, 1, EC), F32),
                        pltpu.VMEM((NCHIP, D, EC), BF16), pltpu.VMEM((S, D), BF16),
                        pltpu.SemaphoreType.DMA((NDEV - 1,)), pltpu.SemaphoreType.DMA((NDEV - 1,)),
                        pltpu.SemaphoreType.DMA((NCHIP - 1,)), pltpu.SemaphoreType.DMA((NCHIP - 1,))]
        + _WeightGather.SEMS + [pltpu.SemaphoreType.DMA((2,))] * 3 + [pltpu.SemaphoreType.DMA((NCHIP,))])
    return pl.pallas_call(
        body, name="start_in_proj", grid_spec=grid_spec,
        out_shape=[jax.ShapeDtypeStruct((NDEV, wc), F32), jax.ShapeDtypeStruct((4, R), F32),
                   jax.ShapeDtypeStruct((1, 3 * D), F32),
                   jax.ShapeDtypeStruct((NCHIP, D, EC), BF16), jax.ShapeDtypeStruct((S, LANES), F32),
                   jax.ShapeDtypeStruct((S, LANES), F32), jax.ShapeDtypeStruct((S, E), F32),
                   jax.ShapeDtypeStruct((D, S), BF16)],
        compiler_params=_cp(("arbitrary", "arbitrary")),
    )(order, c, conv_w, w_ada, b_ada, w_in, pos, _rope_freq(), x, norm_pre)


class _ReduceScatter:
    @staticmethod
    def scratch(n_units, rows, ucols, max_owned):
        half = rows // 2
        return [pltpu.VMEM((n_units, half, ucols), F32), pltpu.VMEM((n_units, half, ucols), BF16),
                pltpu.VMEM((max_owned, NCHIP, half, ucols), BF16),
                pltpu.SemaphoreType.DMA((2,)), pltpu.SemaphoreType.DMA((n_units,)),
                pltpu.SemaphoreType.DMA((n_units, NCHIP)), pltpu.SemaphoreType.DMA((n_units,)),
                pltpu.SemaphoreType.DMA((n_units,))]

    def __init__(self, g_ref, out_ref, units, sib, stage, got, sem1, send2, recv2, send3, recv3):
        x, y, c = _me()
        self.c, self.ci = c, 2 * x + y
        self.g, self.out, self.units = g_ref, out_ref, units
        self.sib, self.stage, self.got = sib, stage, got
        self.sem1, self.send2, self.recv2, self.send3, self.recv3 = sem1, send2, recv2, send3, recv3
        self.half = g_ref.shape[1] // 2
        self.ucols = g_ref.shape[2]
        self.r0 = pl.multiple_of(c * self.half, self.half)
        self.r1 = pl.multiple_of((1 - c) * self.half, self.half)
        self.slot0 = units[0][0]
        assert [u[0] for u in units] == list(range(self.slot0, self.slot0 + len(units)))
        seen = {}
        self.local = []
        for _, owner, _ in units:
            self.local.append(seen.get(owner, 0))
            seen[owner] = seen.get(owner, 0) + 1

    def _halves(self):
        n = len(self.units)
        return _remote(self.g.at[pl.ds(self.slot0, n), pl.ds(self.r1, self.half), :], self.sib,
                       self.sem1.at[0], self.sem1.at[1], 1)

    def _partial(self, i, sender):
        _, owner, _ = self.units[i]
        return pltpu.make_async_remote_copy(
            src_ref=self.stage.at[i], dst_ref=self.got.at[self.local[i], sender],
            send_sem=self.send2.at[i], recv_sem=self.recv2.at[i, sender],
            device_id=(owner // 2, owner % 2, self.c), device_id_type=MESH)

    def _back(self, i, start):
        off = self.units[i][2]
        blk = self.out.at[pl.ds(start, self.half), off:off + self.ucols]
        return _remote(blk, blk, self.send3.at[i], self.recv3.at[i], 1)

    def start_halves(self):
        self._halves().start()

    def send_partials(self):
        self._halves().wait_recv()
        for i, (slot, owner, _) in enumerate(self.units):
            @pl.when(self.ci != owner)
            def _():
                self.stage[i] = (self.g[slot, pl.ds(self.r0, self.half), :] + self.sib[i]).astype(BF16)
                self._partial(i, self.ci).start()

    def reduce_owned(self):
        for i, (slot, owner, off) in enumerate(self.units):
            @pl.when(self.ci == owner)
            def _():
                rows, cols = pl.ds(self.r0, self.half), slice(off, off + self.ucols)
                self.out[rows, cols] = self.g[slot, pl.ds(self.r0, self.half), :] + self.sib[i]
                for s in range(NCHIP):
                    if s != owner:
                        self._partial(i, s).wait_recv()
                        self.out[rows, cols] += self.got[self.local[i], s].astype(F32)
                self._back(i, self.r0).start()

    def finish(self):
        self._halves().wait_send()
        for i, (_, owner, _) in enumerate(self.units):
            @pl.when(self.ci == owner)
            def _():
                self._back(i, self.r1).wait_recv()
                self._back(i, self.r0).wait_send()

            @pl.when(self.ci != owner)
            def _():
                self._partial(i, self.ci).wait_send()


def _silu_rows(c_ref):
    cv = c_ref[:, 0:D]
    sc = cv * _sigmoid(cv)
    return jnp.concatenate([sc, jnp.zeros_like(sc)], axis=0).astype(BF16)


def _adamw(groups):
    steps = 4
    specs = [pl.BlockSpec((w.shape[0] // steps, w.shape[1]), lambda i: (i, 0)) for w, _, _, _ in groups]

    def body(*refs):
        ins, outs = refs[:4 * len(groups)], refs[4 * len(groups):]
        for j in range(len(groups)):
            w_ref, g_ref, m_ref, v_ref = ins[4 * j:4 * j + 4]
            d_ref, nm_ref, nv_ref = outs[3 * j:3 * j + 3]
            d_ref[...], nm_ref[...], nv_ref[...] = _adamw_values(w_ref[...], g_ref[...], m_ref[...], v_ref[...])

    res = pl.pallas_call(
        body, name="adamw_big", grid=(steps,),
        in_specs=[s for s in specs for _ in range(4)], out_specs=[s for s in specs for _ in range(3)],
        out_shape=[jax.ShapeDtypeStruct(w.shape, F32) for w, _, _, _ in groups for _ in range(3)],
        compiler_params=_cp(("parallel",)),
    )(*[a for grp in groups for a in grp])
    return [res[3 * j:3 * j + 3] for j in range(len(groups))]


def _adamw_values(w, g, m, v):
    nm = B1 * m + (1.0 - B1) * g
    nv = B2 * v + (1.0 - B2) * (g * g)
    m_hat = nm / (1.0 - B1 ** STEP)
    v_hat = nv / (1.0 - B2 ** STEP)
    return (-LR) * (m_hat / (jnp.sqrt(v_hat) + ADAM_EPS) + WD * w), nm, nv


NB = R // HEAD
SMALL = (("b_ada", (1, 3 * D)), ("norm_pre", (1, D)), ("norm_post", (1, D)), ("conv_w", (4, R // NCHIP)),
         ("conv_b", (1, R)), ("w_rg_a", (NB, HEAD, HEAD)), ("b_rg_a", (1, R)), ("w_rg_x", (NB, HEAD, HEAD)),
         ("b_rg_x", (1, R)), ("lru_lambda", (1, R)), ("norm_rec", (1, R)), ("norm_att", (1, R)))


def _small_update(ao8, sm8, dwa8, dwx8, ai8, cg, params):
    n = len(SMALL)

    def body(ao_ref, sm_ref, dwa_ref, dwx_ref, ai_ref, cg_ref, *refs):
        pin, pout, (gada_ref, loss_ref, dmod) = refs[:3 * n], refs[3 * n:7 * n], refs[7 * n:]
        xx, yy, _ = _me()
        ci = 2 * xx + yy

        def total(ref, *idx):
            acc = ref[(0,) + idx].astype(F32)
            for d in range(1, NDEV):
                acc = acc + ref[(d,) + idx].astype(F32)
            return acc

        row = lambda ref, r, lanes=slice(None): total(ref, slice(r, r + 1), lanes)
        mine = lambda parts: sum(jnp.where(ci == j, part, 0.0) for j, part in enumerate(parts))
        cw = R // NCHIP
        grads = {
            "b_ada": [jnp.concatenate([row(ai_ref, 0), row(ai_ref, 1), row(ao_ref, 0)], axis=1)],
            "norm_pre": [row(ai_ref, 2)], "norm_post": [row(ao_ref, 1)],
            "conv_w": [mine([row(sm_ref, 8 + r, slice(j * cw, (j + 1) * cw)) for j in range(NCHIP)]) for r in range(4)],
            "conv_b": [row(sm_ref, 4)], "b_rg_a": [row(sm_ref, 0)], "b_rg_x": [row(sm_ref, 1)],
            "lru_lambda": [row(sm_ref, 2)], "norm_rec": [row(sm_ref, 3)], "norm_att": [row(ao_ref, 2, slice(0, R))],
            "w_rg_a": [total(dwa_ref, h) for h in range(NB)], "w_rg_x": [total(dwx_ref, h) for h in range(NB)],
        }
        loss_ref[...] = row(ao_ref, 3, slice(0, LANES)) * (0.5 / D)
        for k, (name, shape) in enumerate(SMALL):
            w_ref, m_ref, v_ref = pin[3 * k:3 * k + 3]
            outs = pout[4 * k:4 * k + 4]
            for r, g in enumerate(grads[name]):
                at = (slice(None),) if len(grads[name]) == 1 else ((r,) if len(shape) == 3 else (slice(r, r + 1),))
                res = (g,) + _adamw_values(w_ref[at], g, m_ref[at], v_ref[at])
                for o_ref, val in zip(outs, res):
                    o_ref[at] = val
        for d in range(NDEV):
            dmod[d:d + 1, :] = jnp.concatenate([ai_ref[d, 0:1, :], ai_ref[d, 1:2, :], ao_ref[d, 0:1, :]], axis=1)
        cols = mine([dmod[:, j * EC:(j + 1) * EC] for j in range(NCHIP)])
        colsb = jnp.concatenate([cols, jnp.zeros_like(cols)], axis=0).astype(BF16)
        gada_ref[...] = _dot_tn(_silu_rows(cg_ref), colsb)

    shapes = [jax.ShapeDtypeStruct(s, F32) for _, s in SMALL]
    outs = pl.pallas_call(
        body, name="small_update",
        out_shape=[s for s in shapes for _ in range(4)] + [jax.ShapeDtypeStruct((D, EC), F32),
                                                           jax.ShapeDtypeStruct((1, LANES), F32)],
        scratch_shapes=[pltpu.VMEM((NDEV, 3 * D), F32)],
        compiler_params=_cp(),
    )(ao8, sm8, dwa8, dwx8, ai8, cg, *params)
    return outs[:4 * n], outs[4 * n], outs[4 * n + 1]


BIG = ("w_ada", "w_in", "w_out")
WEIGHTS = ("w_ada", "b_ada", "norm_pre", "norm_post", "w_in", "conv_w", "conv_b", "w_rg_a", "b_rg_a", "w_rg_x",
           "b_rg_x", "lru_lambda", "norm_rec", "norm_att", "w_out")


def kernel(x, c, positions, w_ada, b_ada, norm_pre, norm_post, w_in, conv_w, conv_b, w_rg_a, b_rg_a, w_rg_x, b_rg_x, lru_lambda, norm_rec, norm_att, w_out, loss_target, m_w_ada, m_b_ada, m_norm_pre, m_norm_post, m_w_in, m_conv_w, m_conv_b, m_w_rg_a, m_b_rg_a, m_w_rg_x, m_b_rg_x, m_lru_lambda, m_norm_rec, m_norm_att, m_w_out, v_w_ada, v_b_ada, v_norm_pre, v_norm_post, v_w_in, v_conv_w, v_conv_b, v_w_rg_a, v_b_rg_a, v_w_rg_x, v_b_rg_x, v_lru_lambda, v_norm_rec, v_norm_att, v_w_out):
    given = dict(locals())
    wts = {n: given[n] for n in WEIGHTS}
    ms = {n: given["m_" + n] for n in WEIGHTS}
    vs = {n: given["v_" + n] for n in WEIGHTS}
    xi, yi, _ = _me()
    chip = 2 * xi + yi

    order = (chip ^ jnp.arange(NCHIP, dtype=jnp.int32)).astype(jnp.int32)
    cg, conv_full, mod, w_in_bf, cos, sin, proj, ht = _start_in_proj(
        c, conv_w[0], w_ada[0], b_ada, w_in[0], positions.reshape(S, 1), x[0], norm_pre, order)

    p = dict(norm_pre=norm_pre, norm_post=norm_post, conv_b=conv_b, b_rg_a=b_rg_a, b_rg_x=b_rg_x,
             lru_lambda=lru_lambda, norm_rec=norm_rec, norm_att=norm_att, w_rg_a=w_rg_a[0], w_rg_x=w_rg_x[0])
    grad_x, g_in, g_out, gathered = _local_step(
        x[0], cos, sin, loss_target[0], mod, w_in_bf, proj, ht, w_out[0], conv_full, p)

    params = [d[n].reshape(shape) for n, shape in SMALL for d in (wts, ms, vs)]
    small_out, g_ada, loss_row = _small_update(*gathered, cg, params)
    grads = {"w_out": g_out, "w_in": g_in, "w_ada": g_ada}
    delta, new_m, new_v = {}, {}, {}
    for k, (n, _) in enumerate(SMALL):
        grads[n], delta[n], new_m[n], new_v[n] = small_out[4 * k:4 * k + 4]
    for n, res in zip(BIG, _adamw([(wts[n][0], grads[n], ms[n][0], vs[n][0]) for n in BIG])):
        delta[n], new_m[n], new_v[n] = res
    out = lambda d: [d[n].reshape(wts[n].shape) for n in WEIGHTS]
    return (loss_row[0, 0], grad_x.reshape(x.shape), *out(grads), *out(delta), *out(new_m), *out(new_v))
```

```python
import numpy as np
import jax
import jax.numpy as jnp
from jax import lax
from jax.experimental import pallas as pl
from jax.experimental.pallas import tpu as pltpu

F32 = jnp.float32
BF16 = jnp.bfloat16

S = 2048
D = 1024
E = 3072
R = 512
NDEV = 8
NCHIP = 4
EC = 768
LRU_C = 8.0
EPS = 1e-6
NEG = -1e30
HEAD = 64
BLK = 128
PATTERNS = (1, 4, 16)
ROPE_THETA = 10000.0
LANES = 128
VMEM_LIMIT = 56 * 1024 * 1024

B1, B2, LR, WD, ADAM_EPS, STEP = 0.9, 0.999, 0.001, 0.01, 1e-8, 10
MESH = pl.DeviceIdType.MESH


def _cp(sem=None, **kw):
    return pltpu.CompilerParams(dimension_semantics=sem, vmem_limit_bytes=VMEM_LIMIT, **kw)


def _dot(a, b):
    return jnp.dot(a, b, preferred_element_type=F32)


def _dot_nt(a, b):
    return lax.dot_general(a, b, (((1,), (1,)), ((), ())), preferred_element_type=F32)


def _dot_tn(a, b):
    return lax.dot_general(a, b, (((0,), (0,)), ((), ())), preferred_element_type=F32)


def _sigmoid(x):
    return 1.0 / (1.0 + jnp.exp(-x))


def _one_minus_exp(x, ex):
    poly = -x * (1.0 + x * (0.5 + x * (1.0 / 6 + x * (1.0 / 24))))
    return jnp.where(x > -1.0 / 16, poly, 1.0 - ex)


def _rms_fwd(v, g):
    rstd = lax.rsqrt(jnp.mean(v * v, axis=-1, keepdims=True) + EPS)
    vn = v * rstd
    return vn * g, vn, rstd


def _rms_bwd(dy, vn, rstd, g):
    dvn = dy * g
    dv = rstd * (dvn - vn * jnp.mean(dvn * vn, axis=-1, keepdims=True))
    return dv, jnp.sum(dy * vn, axis=0, keepdims=True)


RT = 256


def _shift_down(cur, prev8, j, row):
    if j == 0:
        return cur
    rolled = pltpu.roll(cur, j, 0)
    top = jnp.where(row[0:8] >= j, rolled[0:8], pltpu.roll(prev8, j, 0))
    return jnp.concatenate([top, rolled[8:]], axis=0)


def _shift_up(cur, next8, j, row):
    if j == 0:
        return cur
    rolled = pltpu.roll(cur, RT - j, 0)
    bot = jnp.where(row[RT - 8:] < RT - j, rolled[RT - 8:], pltpu.roll(next8, 8 - j, 0))
    return jnp.concatenate([rolled[:RT - 8], bot], axis=0)


def _rec_gates(xp, xprev8, row, cw_ref, cb_ref, wa_ref, ba_ref, wx_ref, bx_ref, lam_ref):
    xa = cb_ref[...] + sum(cw_ref[3 - j:4 - j, :] * _shift_down(xp, xprev8, j, row) for j in range(4))
    xab = xa.astype(BF16)
    r = _sigmoid(_dot(xab, wa_ref[...]) + ba_ref[...])
    ig = _sigmoid(_dot(xab, wx_ref[...]) + bx_ref[...])
    nl = -lam_ref[...]
    sp = jnp.maximum(nl, 0.0) + jnp.log1p(jnp.exp(-jnp.abs(nl)))
    la = (-LRU_C) * r * sp
    a = jnp.exp(la)
    mult = jnp.sqrt(_one_minus_exp(2.0 * la, a * a))
    return dict(xa=xa, xab=xab, r=r, ig=ig, sp=sp, la=la, a=a, mult=mult)


def _scan_fwd(a, u, row):
    sh = 1
    while sh < RT:
        a_s = jnp.where(row >= sh, pltpu.roll(a, sh, 0), 1.0)
        u_s = jnp.where(row >= sh, pltpu.roll(u, sh, 0), 0.0)
        u = a * u_s + u
        a = a * a_s
        sh *= 2
    return a, u


def _scan_bwd(al, g, row):
    sh = 1
    while sh < RT:
        al_s = jnp.where(row < RT - sh, pltpu.roll(al, RT - sh, 0), 1.0)
        g_s = jnp.where(row < RT - sh, pltpu.roll(g, RT - sh, 0), 0.0)
        g = g + al * g_s
        al = al * al_s
        sh *= 2
    return g


def _dense_from_blocks(blocks_ref, dense_ref):
    dense_ref[...] = jnp.zeros_like(dense_ref)
    for h in range(R // HEAD):
        dense_ref[h * HEAD:(h + 1) * HEAD, h * HEAD:(h + 1) * HEAD] = blocks_ref[h].astype(dense_ref.dtype)


def _rec_fwd(proj, conv_w, conv_b, wa_b, ba, wx_b, bx, lam, norm_rec):
    nt = S // RT

    def body(p_ref, cw_ref, cb_ref, wa_ref, ba_ref, wx_ref, bx_ref, lam_ref, nr_ref,
             h_ref, ya_ref, prev8, hc, wad, wxd):
        i = pl.program_id(0)

        @pl.when(i == 0)
        def _():
            prev8[...] = jnp.zeros_like(prev8)
            hc[...] = jnp.zeros_like(hc)
            _dense_from_blocks(wa_ref, wad)
            _dense_from_blocks(wx_ref, wxd)

        row = lax.broadcasted_iota(jnp.int32, (RT, R), 0)
        xp = p_ref[:, 0:R]
        ga = p_ref[:, R:2 * R]
        f = _rec_gates(xp, prev8[...], row, cw_ref, cb_ref, wad, ba_ref, wxd, bx_ref, lam_ref)
        u = f["mult"] * (f["ig"] * f["xa"])
        acum, hh = _scan_fwd(f["a"], u, row)
        h = hh + acum * hc[0:1, :]
        h_ref[...] = h
        hc[0:1, :] = h_ref[RT - 1:RT, :]
        prev8[...] = p_ref[RT - 8:RT, 0:R]
        yp = h * (ga * _sigmoid(ga))
        ya, _, _ = _rms_fwd(yp, nr_ref[...])
        ya_ref[...] = ya.astype(BF16)

    row1 = lambda n: pl.BlockSpec((1, n), lambda i: (0, 0))
    blocks = pl.BlockSpec((R // HEAD, HEAD, HEAD), lambda i: (0, 0, 0))
    return pl.pallas_call(
        body, name="rec_fwd", grid=(nt,),
        in_specs=[pl.BlockSpec((RT, 2 * R), lambda i: (i, 0)), pl.BlockSpec((4, R), lambda i: (0, 0)), row1(R),
                  blocks, row1(R), blocks, row1(R), row1(R), row1(R)],
        out_specs=[pl.BlockSpec((RT, R), lambda i: (i, 0)), pl.BlockSpec((RT, R), lambda i: (i, 0))],
        out_shape=[jax.ShapeDtypeStruct((S, R), F32), jax.ShapeDtypeStruct((S, R), BF16)],
        scratch_shapes=[pltpu.VMEM((8, R), F32), pltpu.VMEM((8, R), F32), pltpu.VMEM((R, R), BF16),
                        pltpu.VMEM((R, R), BF16)],
        compiler_params=_cp(("arbitrary",)),
    )(proj, conv_w, conv_b, wa_b, ba, wx_b, bx, lam, norm_rec)


def _rec_bwd(dproj, d_ya, proj, h_all, conv_w, conv_b, wa_b, ba, wx_b, bx, lam, norm_rec):
    nt = S // RT

    def body(dp_in, dya_ref, p_ref, pprev_ref, h_ref, hprev_ref, cw_ref, cb_ref, wab_ref, ba_ref, wxb_ref, bx_ref,
             lam_ref, nr_ref, dp_ref, dwab_ref, dwxb_ref, sm_ref, nxt8, cg, wa_ref, wx_ref, dwa_ref, dwx_ref):
        i = pl.program_id(0)
        ti = nt - 1 - i

        @pl.when(i == 0)
        def _():
            nxt8[...] = jnp.zeros_like(nxt8)
            cg[...] = jnp.zeros_like(cg)
            dwa_ref[...] = jnp.zeros_like(dwa_ref)
            dwx_ref[...] = jnp.zeros_like(dwx_ref)
            sm_ref[...] = jnp.zeros_like(sm_ref)
            _dense_from_blocks(wab_ref, wa_ref)
            _dense_from_blocks(wxb_ref, wx_ref)

        row = lax.broadcasted_iota(jnp.int32, (RT, R), 0)
        first = (ti > 0).astype(F32)
        xprev8 = pprev_ref[...] * first
        hprev8 = hprev_ref[...] * first
        xp = p_ref[:, 0:R]
        ga = p_ref[:, R:2 * R]
        f = _rec_gates(xp, xprev8, row, cw_ref, cb_ref, wa_ref, ba_ref, wx_ref, bx_ref, lam_ref)
        xa, r, ig, a, mult = f["xa"], f["r"], f["ig"], f["a"], f["mult"]
        h = h_ref[...]
        sg = _sigmoid(ga)
        gate = ga * sg
        yp = h * gate
        _, ypn, rstd = _rms_fwd(yp, nr_ref[...])
        d_yp, dnr = _rms_bwd(dya_ref[...], ypn, rstd, nr_ref[...])
        d_ga = d_yp * h * (sg * (1.0 + ga * (1.0 - sg)))
        dh = d_yp * gate + jnp.where(row == RT - 1, cg[0:1, :], 0.0)
        al = jnp.where(row < RT - 1, pltpu.roll(a, RT - 1, 0), 0.0)
        g = _scan_bwd(al, dh, row)
        cg[0:1, :] = jnp.sum(jnp.where(row == 0, a * g, 0.0), axis=0, keepdims=True)
        h_m1 = _shift_down(h, hprev8, 1, row)
        da = g * h_m1
        ix = ig * xa
        d_mult = g * ix
        d_ig = g * mult * xa
        d_xa = g * mult * ig
        d_la = da * a - d_mult * (a * a) / mult
        d_r = d_la * ((-LRU_C) * f["sp"])
        dsp = jnp.sum(d_la * ((-LRU_C) * r), axis=0, keepdims=True)
        dlam = dsp * (-_sigmoid(-lam_ref[...]))
        d_za = d_r * r * (1.0 - r)
        d_zx = d_ig * ig * (1.0 - ig)
        dzab = d_za.astype(BF16)
        dzxb = d_zx.astype(BF16)
        dwa_ref[...] += _dot_tn(f["xab"], dzab)
        dwx_ref[...] += _dot_tn(f["xab"], dzxb)
        d_xa = d_xa + _dot_nt(dzab, wa_ref[...]) + _dot_nt(dzxb, wx_ref[...])
        d_xp = sum(cw_ref[3 - j:4 - j, :] * _shift_up(d_xa, nxt8[...], j, row) for j in range(4))
        dcw = [jnp.sum(d_xa * _shift_down(xp, xprev8, 3 - k, row), axis=0, keepdims=True) for k in range(4)]
        dp_ref[:, 0:R] = d_xp.astype(BF16)
        dp_ref[:, R:2 * R] = d_ga.astype(BF16)
        dp8 = d_xa[0:8, :]
        nxt8[...] = dp8
        sm_ref[0:1, :] += jnp.sum(d_za, axis=0, keepdims=True)
        sm_ref[1:2, :] += jnp.sum(d_zx, axis=0, keepdims=True)
        sm_ref[2:3, :] += dlam
        sm_ref[3:4, :] += dnr
        sm_ref[4:5, :] += jnp.sum(d_xa, axis=0, keepdims=True)
        for k in range(4):
            sm_ref[8 + k:9 + k, :] += dcw[k]

        @pl.when(i == nt - 1)
        def _():
            for h in range(R // HEAD):
                dwab_ref[h] = dwa_ref[h * HEAD:(h + 1) * HEAD, h * HEAD:(h + 1) * HEAD].astype(BF16)
                dwxb_ref[h] = dwx_ref[h * HEAD:(h + 1) * HEAD, h * HEAD:(h + 1) * HEAD].astype(BF16)

    c0 = lambda shape: pl.BlockSpec(shape, lambda i: (0, 0))
    blocks = pl.BlockSpec((R // HEAD, HEAD, HEAD), lambda i: (0, 0, 0))
    rev = lambda i: nt - 1 - i
    prev8 = lambda i: (jnp.maximum((nt - 1 - i) * (RT // 8) - 1, 0), 0)
    return pl.pallas_call(
        body, name="rec_bwd", grid=(nt,),
        in_specs=[pl.BlockSpec(memory_space=pl.ANY),
                  pl.BlockSpec((RT, R), lambda i: (rev(i), 0)),
                  pl.BlockSpec((RT, 2 * R), lambda i: (rev(i), 0)), pl.BlockSpec((8, R), prev8),
                  pl.BlockSpec((RT, R), lambda i: (rev(i), 0)), pl.BlockSpec((8, R), prev8),
                  c0((4, R)), c0((1, R)), blocks, c0((1, R)), blocks, c0((1, R)), c0((1, R)), c0((1, R))],
        out_specs=[pl.BlockSpec((RT, 2 * R), lambda i: (rev(i), 0)), blocks, blocks, c0((16, R))],
        out_shape=[jax.ShapeDtypeStruct((S, E), BF16), jax.ShapeDtypeStruct((R // HEAD, HEAD, HEAD), BF16),
                   jax.ShapeDtypeStruct((R // HEAD, HEAD, HEAD), BF16), jax.ShapeDtypeStruct((16, R), F32)],
        scratch_shapes=[pltpu.VMEM((8, R), F32), pltpu.VMEM((8, R), F32), pltpu.VMEM((R, R), BF16),
                        pltpu.VMEM((R, R), BF16), pltpu.VMEM((R, R), F32), pltpu.VMEM((R, R), F32)],
        input_output_aliases={0: 0},
        compiler_params=_cp(("arbitrary",)),
    )(dproj, d_ya, proj, proj, h_all, h_all, conv_w, conv_b, wa_b, ba, wx_b, bx, lam, norm_rec)


NPAIR = R // LANES
QB, KB, VB, GB = 2 * R // LANES, 3 * R // LANES, 4 * R // LANES, 5 * R // LANES


def _rope_freq():
    half = HEAD // 2
    inv = np.float32(ROPE_THETA) ** (-(np.arange(half, dtype=np.float32) / np.float32(half)))
    return jnp.asarray(np.tile(inv.astype(np.float32), LANES // half)[None, :])


def _rot_half(x, first):
    return jnp.where(first, -pltpu.roll(x, LANES - HEAD // 2, 1), pltpu.roll(x, HEAD // 2, 1))


def _cos_sin(pos_ref, freq_ref):
    pos = jnp.broadcast_to(pos_ref[...].astype(F32), (LANES, S)).T
    ang = pos * freq_ref[...]
    return jnp.cos(ang), jnp.sin(ang)


SUB = 4


def _stages(d):
    assert d in (1, SUB, SUB * SUB)
    return d > SUB


def _strided_rows(src_ref, d, tmp):
    n = S // d
    if not _stages(d):
        for r in range(d):
            yield r * n, (src_ref[pl.ds(r, n, stride=d), :] if d > 1 else src_ref[...])
        return
    m = S // SUB
    for r in range(SUB):
        tmp[r * m:(r + 1) * m, :] = src_ref[pl.ds(r, m, stride=SUB), :]
    for r in range(SUB):
        for q in range(SUB):
            yield (r + SUB * q) * n, tmp[pl.ds(r * m + q, n, stride=SUB), :]


def _deint(src_ref, dst_ref, d, tmp):
    n = S // d
    for row0, v in _strided_rows(src_ref, d, tmp):
        dst_ref[row0:row0 + n, :] = v.astype(dst_ref.dtype)


def _reint(src_ref, dst_ref, d, accumulate, tmp):
    if _stages(d):
        n, m = S // d, S // SUB
        for r in range(SUB):
            for q in range(SUB):
                tmp[pl.ds(r * m + q, n, stride=SUB), :] = src_ref[(r + SUB * q) * n:(r + SUB * q + 1) * n, :]
        src_ref, d = tmp, SUB
    n = S // d
    for r in range(d):
        idx = (pl.ds(r, n, stride=d), slice(None)) if d > 1 else (slice(None), slice(None))
        v = src_ref[r * n:(r + 1) * n, :]
        if accumulate:
            dst_ref[idx] = dst_ref[idx] + v
        else:
            dst_ref[idx] = v


def _deint_heads(src_ref, dst0, dst1, d, tmp):
    n = S // d
    hm0 = lax.broadcasted_iota(jnp.int32, (n, LANES), 1) < HEAD
    for row0, v in _strided_rows(src_ref, d, tmp):
        dst0[row0:row0 + n, :] = jnp.where(hm0, v, 0.0).astype(BF16)
        dst1[row0:row0 + n, :] = jnp.where(hm0, 0.0, v).astype(BF16)


def _reint_prev(src_ref, dst_ref, d):
    n = S // d
    if n == BLK:
        return
    for r in range(d):
        idx = (pl.ds(r, n - BLK, stride=d), slice(None)) if d > 1 else (slice(0, n - BLK), slice(None))
        dst_ref[idx] = dst_ref[idx] + src_ref[r * n + BLK:(r + 1) * n, :]


def _pair_masks():
    qi = lax.broadcasted_iota(jnp.int32, (BLK, 2 * BLK), 0)
    ki = lax.broadcasted_iota(jnp.int32, (BLK, 2 * BLK), 1) & (BLK - 1)
    return ki <= qi, ki >= qi


def _two(ref0, ref1, st, axis):
    return jnp.concatenate([ref0[pl.ds(st, BLK), :], ref1[pl.ds(st, BLK), :]], axis=axis)


ATT_UNROLL = 8


def _att_fwd(proj, cos, sin, w_out):
    def body(q_ref, k_ref, v_ref, cos_ref, sin_ref, w_ref, att_ref, qr_ref, kr_ref, lse_ref, wbf_ref,
             qd, kd0, kd1, vd0, vd1, od, ld, tmp, on, ln, wbuf, *wsems):
        wg = _WeightGather(w_ref, wbuf, *wsems)
        pl.when(pl.program_id(0) == 0)(wg.start)
        pl.when(pl.program_id(0) == 1)(wg.forward)
        lane = lax.broadcasted_iota(jnp.int32, (S, LANES), 1)
        first = (lane & (HEAD // 2)) == 0
        cos, sin = cos_ref[...], sin_ref[...]
        q = q_ref[...]
        k = k_ref[...]
        qr_ref[...] = (q * cos + _rot_half(q, first) * sin) * (HEAD ** -0.5)
        kr_ref[...] = k * cos + _rot_half(k, first) * sin
        hm0 = lax.broadcasted_iota(jnp.int32, (BLK, LANES), 1) < HEAD
        top = lax.broadcasted_iota(jnp.int32, (2 * BLK, LANES), 0) < BLK
        ones2 = (top == (lax.broadcasted_iota(jnp.int32, (2 * BLK, LANES), 1) < HEAD)).astype(BF16)
        mc2, mp2 = _pair_masks()

        for pi, d in enumerate(PATTERNS):
            nb = S // d // BLK
            _deint(qr_ref, qd, d, tmp)
            _deint_heads(kr_ref, kd0, kd1, d, tmp)
            _deint_heads(v_ref, vd0, vd1, d, tmp)

            def blk(b, carry):
                st = pl.multiple_of(b * BLK, BLK)
                qb = qd[pl.ds(st, BLK), :]
                sc = jnp.where(mc2, _dot_nt(qb, _two(kd0, kd1, st, 0)), NEG)
                mx = sc
                if nb > 1:
                    stp = pl.multiple_of(jnp.maximum(b - 1, 0) * BLK, BLK)
                    mp = jnp.logical_and(mp2, lax.rem(b, nb) != 0)
                    sp = jnp.where(mp, _dot_nt(qb, _two(kd0, kd1, stp, 0)), NEG)
                    mx = jnp.maximum(sc, sp)
                m0 = jnp.max(mx[:, 0:BLK], axis=1, keepdims=True)
                m1 = jnp.max(mx[:, BLK:2 * BLK], axis=1, keepdims=True)
                mf = jnp.concatenate([jnp.broadcast_to(m0, (BLK, BLK)), jnp.broadcast_to(m1, (BLK, BLK))], axis=1)
                o = _dot(jnp.exp(sc - mf).astype(BF16), jnp.concatenate([_two(vd0, vd1, st, 0), ones2], axis=1))
                if nb > 1:
                    o = o + _dot(jnp.exp(sp - mf).astype(BF16), jnp.concatenate([_two(vd0, vd1, stp, 0), ones2], axis=1))
                l = o[:, LANES:2 * LANES]
                od[pl.ds(st, BLK), :] = o[:, 0:LANES] / l
                ld[pl.ds(st, BLK), :] = jnp.where(hm0, m0, m1) + jnp.log(l)
                return carry

            lax.fori_loop(0, S // BLK, blk, 0, unroll=ATT_UNROLL)
            _reint(od, on.at[pi], d, False, tmp)
            _reint(ld, ln.at[pi], d, False, tmp)

        l0, l1, l2 = ln[0], ln[1], ln[2]
        m = jnp.maximum(jnp.maximum(l0, l1), l2)
        e0, e1, e2 = jnp.exp(l0 - m), jnp.exp(l1 - m), jnp.exp(l2 - m)
        den = e0 + e1 + e2
        att_ref[...] = (e0 * on[0] + e1 * on[1] + e2 * on[2]) / den
        lse_ref[...] = m + jnp.log(den)

        @pl.when(pl.program_id(0) == NPAIR - 1)
        def _():
            wg.finish()
            wbf_ref[...] = wbuf[...]

    col = lambda c0: pl.BlockSpec((S, LANES), lambda p: (0, c0 + p))
    out = pl.BlockSpec((S, LANES), lambda p: (0, p))
    tab = pl.BlockSpec((S, LANES), lambda p: (0, 0))
    vm = pl.BlockSpec(memory_space=pltpu.VMEM)
    return pl.pallas_call(
        body, name="att_fwd", grid=(NPAIR,),
        in_specs=[col(QB), col(KB), col(VB), tab, tab, vm],
        out_specs=[out, out, out, out, vm],
        out_shape=[jax.ShapeDtypeStruct((S, R), F32)] * 4 + [jax.ShapeDtypeStruct((NCHIP,) + w_out.shape, BF16)],
        scratch_shapes=[pltpu.VMEM((S, LANES), BF16)] * 5 + [pltpu.VMEM((S, LANES), F32)] * 3
        + [pltpu.VMEM((3, S, LANES), F32)] * 2 + [pltpu.VMEM((NCHIP,) + w_out.shape, BF16)] + _WeightGather.SEMS,
        compiler_params=_cp(("arbitrary",)),
    )(proj, proj, proj, cos, sin, w_out)


def _att_bwd(dproj, d_att, att, lse, qr, kr, proj, cos, sin, gw_out4):
    out_units = [(j, j, 0) for j in range(NCHIP)]

    nblk = S // BLK

    def body(dp_in, do_ref, o_ref, lse_ref, qr_ref, kr_ref, v_ref, cos_ref, sin_ref, gw_ref, dp_ref, gout_ref,
             qd, kd0, kd1, vd0, vd1, dod, kt, packn, packd, dqd, dkcd, dkpd, dvcd, dvpd,
             dqn, dkn, dvn, tmp, rows, trs, pts, dss, stage, sems, gred, *rs_scratch):
        p = pl.program_id(0)
        rs = _ReduceScatter(gw_ref, gred, out_units, *rs_scratch)
        for step, piece in enumerate((rs.start_halves, rs.send_partials, rs.reduce_owned)):
            pl.when(p == step)(piece)

        @pl.when(p == NPAIR - 1)
        def _():
            rs.finish()
            gout_ref[...] = gred[...]

        lane = lax.broadcasted_iota(jnp.int32, (S, LANES), 1)
        hms = lane < HEAD
        prod = do_ref[...] * o_ref[...]
        d0 = jnp.sum(jnp.where(hms, prod, 0.0), axis=1, keepdims=True)
        d1 = jnp.sum(jnp.where(hms, 0.0, prod), axis=1, keepdims=True)
        lse = lse_ref[...]
        quarter = HEAD // 2
        packn[...] = jnp.where(lane < quarter, lse,
                               jnp.where(hms, pltpu.roll(lse, LANES - quarter, 1), jnp.where(lane < 3 * quarter, d0, d1)))
        dqn[...] = jnp.zeros_like(dqn)
        dkn[...] = jnp.zeros_like(dkn)
        dvn[...] = jnp.zeros_like(dvn)
        hm0 = lax.broadcasted_iota(jnp.int32, (BLK, LANES), 1) < HEAD
        key = lax.broadcasted_iota(jnp.int32, (2 * BLK, BLK), 0) & (BLK - 1)
        qry = lax.broadcasted_iota(jnp.int32, (2 * BLK, BLK), 1)
        mct, mpt = key <= qry, key >= qry

        for d in PATTERNS:
            nb = S // d // BLK
            _deint(qr_ref, qd, d, tmp)
            _deint_heads(kr_ref, kd0, kd1, d, tmp)
            _deint_heads(v_ref, vd0, vd1, d, tmp)
            _deint(do_ref, dod, d, tmp)
            _deint(packn, packd, d, tmp)

            sides = (0, 1) if nb > 1 else (0,)

            def probs(b, carry):
                st = pl.multiple_of(b * BLK, BLK)
                kt[b] = _two(kd0, kd1, st, 0).astype(F32).T.astype(BF16)
                trs[b] = packd[pl.ds(st, BLK), :].T
                for j in range(4):
                    rows[b, j:j + 1, :] = trs[b, j * quarter:j * quarter + 1, :]
                qb, dob = qd[pl.ds(st, BLK), :], dod[pl.ds(st, BLK), :]
                both = lambda j: jnp.concatenate([jnp.broadcast_to(rows[b, j:j + 1, :], (BLK, BLK)),
                                                  jnp.broadcast_to(rows[b, j + 1:j + 2, :], (BLK, BLK))], axis=0)
                lbt, dlt = both(0), both(2)
                for sd in sides:
                    stk = pl.multiple_of(jnp.maximum(b - sd, 0) * BLK, BLK)
                    mask = mct if sd == 0 else jnp.logical_and(mpt, lax.rem(b, nb) != 0)
                    k2, v2 = _two(kd0, kd1, stk, 0), _two(vd0, vd1, stk, 0)
                    pt = jnp.where(mask, jnp.exp(_dot_nt(k2, qb) - lbt), 0.0)
                    pts[b, sd] = pt.astype(BF16)
                    dss[b, sd] = (pt * (_dot_nt(v2, dob) - dlt)).astype(BF16)
                return carry

            lax.fori_loop(0, nblk, probs, 0, unroll=ATT_UNROLL)

            def prods(b, carry):
                st = pl.multiple_of(b * BLK, BLK)
                qb, dob = qd[pl.ds(st, BLK), :], dod[pl.ds(st, BLK), :]
                dq_t = None
                for sd in sides:
                    dst, ptb = dss[b, sd], pts[b, sd]
                    rk, rv = _dot(dst, qb), _dot(ptb, dob)
                    dqs = _dot(kt[jnp.maximum(b - sd, 0)], dst)
                    dq_t = dqs if dq_t is None else dq_t + dqs
                    dk, dv = (dkcd, dvcd) if sd == 0 else (dkpd, dvpd)
                    dk[pl.ds(st, BLK), :] = jnp.where(hm0, rk[0:BLK], rk[BLK:2 * BLK])
                    dv[pl.ds(st, BLK), :] = jnp.where(hm0, rv[0:BLK], rv[BLK:2 * BLK])
                dqd[pl.ds(st, BLK), :] = dq_t.T
                return carry

            lax.fori_loop(0, nblk, prods, 0, unroll=ATT_UNROLL)
            _reint(dqd, dqn, d, True, tmp)
            _reint(dkcd, dkn, d, True, tmp)
            _reint(dvcd, dvn, d, True, tmp)
            _reint_prev(dkpd, dkn, d)
            _reint_prev(dvpd, dvn, d)

        lane = lax.broadcasted_iota(jnp.int32, (S, LANES), 1)
        first = (lane & (HEAD // 2)) == 0
        cos, sin = cos_ref[...], sin_ref[...]
        dq = dqn[...] * (HEAD ** -0.5)
        dk = dkn[...]
        stage[0] = (dq * cos - _rot_half(dq, first) * sin).astype(BF16)
        stage[1] = (dk * cos - _rot_half(dk, first) * sin).astype(BF16)
        stage[2] = dvn[...].astype(BF16)
        copies = [pltpu.make_async_copy(stage.at[j], dp_ref.at[:, pl.ds((2 + j) * R + p * LANES, LANES)], sems.at[j])
                  for j in range(3)]
        for cp in copies:
            cp.start()
        for cp in copies:
            cp.wait()

    blk = pl.BlockSpec((S, LANES), lambda p: (0, p))
    tab = pl.BlockSpec((S, LANES), lambda p: (0, 0))
    vm = pl.BlockSpec(memory_space=pltpu.VMEM)
    _, orows, ocols = gw_out4.shape
    return pl.pallas_call(
        body, name="att_bwd", grid=(NPAIR,),
        in_specs=[pl.BlockSpec(memory_space=pl.ANY), blk, blk, blk, blk, blk,
                  pl.BlockSpec((S, LANES), lambda p: (0, VB + p)), tab, tab, vm],
        out_specs=[pl.BlockSpec(memory_space=pl.ANY), vm],
        out_shape=[jax.ShapeDtypeStruct((S, E), BF16), jax.ShapeDtypeStruct((orows, ocols), F32)],
        scratch_shapes=[pltpu.VMEM((S, LANES), BF16)] * 6 + [pltpu.VMEM((nblk, LANES, 2 * BLK), BF16)]
        + [pltpu.VMEM((S, LANES), F32)] * 11
        + [pltpu.VMEM((nblk, 8, BLK), F32), pltpu.VMEM((nblk, LANES, BLK), F32)]
        + [pltpu.VMEM((nblk, 2, 2 * BLK, BLK), BF16)] * 2
        + [pltpu.VMEM((3, S, LANES), BF16), pltpu.SemaphoreType.DMA((3,)), pltpu.VMEM((orows, ocols), F32)]
        + _ReduceScatter.scratch(NCHIP, orows, ocols, 1),
        input_output_aliases={0: 0},
        compiler_params=_cp(("arbitrary",)),
    )(dproj, d_att, att, lse, qr, kr, proj, cos, sin, gw_out4)


def _out_fwd_bwd(ya, att, proj, w_out_bf, x, target, mod, norm_post, norm_att):
    ts = 512

    def body(ya_ref, att_ref, gb_ref, w_ref, x_ref, t_ref, mod_ref, npost_ref, natt_ref,
             gx_ref, dya_ref, datt_ref, dgb_ref, gw_ref, acc_ref):
        i = pl.program_id(0)

        @pl.when(i == 0)
        def _():
            gw_ref[...] = jnp.zeros_like(gw_ref)
            acc_ref[...] = jnp.zeros_like(acc_ref)

        gate = mod_ref[:, 2 * D:3 * D]
        att = att_ref[...]
        gb = gb_ref[...]
        sg = _sigmoid(gb)
        silu = gb * sg
        ybp = att * silu
        yb, ybn, rstd_b = _rms_fwd(ybp, natt_ref[...])
        cat = jnp.concatenate([ya_ref[...], yb.astype(BF16)], axis=1)
        mix = _dot(cat, w_ref[...])
        rn, mn, rstd_m = _rms_fwd(mix, npost_ref[...])
        err = x_ref[...] + gate * rn - t_ref[...]
        dy = err * (1.0 / D)
        gx_ref[...] = dy
        dmix, dnpost = _rms_bwd(dy * gate, mn, rstd_m, npost_ref[...])
        dmb = dmix.astype(BF16)
        gw_ref[...] += _dot_tn(cat, dmb)
        dcat = _dot_nt(dmb, w_ref[...])
        dya_ref[...] = dcat[:, 0:R]
        dybp, dnatt = _rms_bwd(dcat[:, R:2 * R], ybn, rstd_b, natt_ref[...])
        datt_ref[...] = dybp * silu
        dgb_ref[...] = (dybp * att * (sg * (1.0 + gb * (1.0 - sg)))).astype(BF16)
        acc_ref[0:1, :] += jnp.sum(dy * rn, axis=0, keepdims=True)
        acc_ref[1:2, :] += dnpost
        acc_ref[2:3, 0:R] += dnatt
        acc_ref[3:4, :] += jnp.sum(jnp.sum(err * err, axis=1, keepdims=True), axis=0, keepdims=True)

    tile = lambda w: pl.BlockSpec((ts, w), lambda i: (i, 0))
    c0 = lambda shape: pl.BlockSpec(shape, lambda i: (0, 0))
    return pl.pallas_call(
        body, name="out_fwd_bwd", grid=(S // ts,),
        in_specs=[tile(R), tile(R), pl.BlockSpec((ts, R), lambda i: (i, 5)), c0((D, D)), tile(D), tile(D),
                  c0((1, 3 * D)), c0((1, D)), c0((1, R))],
        out_specs=[tile(D), tile(R), tile(R), pl.BlockSpec((ts, R), lambda i: (i, 5)), c0((D, D)), c0((8, D))],
        out_shape=[jax.ShapeDtypeStruct((S, D), F32), jax.ShapeDtypeStruct((S, R), F32),
                   jax.ShapeDtypeStruct((S, R), F32), jax.ShapeDtypeStruct((S, E), BF16),
                   jax.ShapeDtypeStruct((D, D), F32), jax.ShapeDtypeStruct((8, D), F32)],
        compiler_params=_cp(("arbitrary",)),
    )(ya, att, proj, w_out_bf, x, target, mod, norm_post, norm_att)


UC = 256
UPC = EC // UC


NU = E // UC


def _unit_of_step(i):
    return (i % NCHIP) * UPC + i // NCHIP


def _in_proj_bwd(ht, dproj, w_in_bf, x, gx1, mod, norm_pre, smalls):
    ts = 256
    nt = S // ts
    half = D // 2
    units = [_unit_of_step(k) for k in range(NU)]
    owners = [u // UPC for u in units]
    ns = len(smalls)

    def body(*refs):
        (ht_ref, dpu_ref, dp_ref, w_hbm, x_ref, gx1_ref, mod_ref, np_ref), refs = refs[:8], refs[8:]
        small_in, refs = refs[:ns], refs[ns:]
        (gx_ref, gin_ref), refs = refs[:2], refs[2:]
        small_out, (acc_out,), refs = refs[:ns], refs[ns:ns + 1], refs[ns + 1:]
        mine, sib, tmp, stage, got, red, acc_ref, hs, hr, ps, pr, bs, br = refs[:13]
        early = _SmallGather(small_in, small_out, *refs[13:16])
        late = _SmallGather([acc_ref], [acc_out], *refs[16:19])
        w_ref, w_sem = refs[19:21]
        i = pl.program_id(0)
        w_copy = pltpu.make_async_copy(w_hbm, w_ref, w_sem)
        pl.when(i == 0)(w_copy.start)
        pl.when(i == NU)(w_copy.wait)
        xx, yy, c = _me()
        ci = 2 * xx + yy
        r0 = pl.multiple_of(c * half, half)
        r1 = pl.multiple_of((1 - c) * half, half)
        pl.when(i == 0)(early.start)
        pl.when(i == NU)(early.forward)

        def exch(k):
            return _remote(tmp.at[k % 2], sib.at[k], hs.at[k], hr.at[k], 1)

        def partial(k, sender):
            return pltpu.make_async_remote_copy(
                src_ref=stage.at[k], dst_ref=got.at[units[k] % UPC, sender], send_sem=ps.at[k],
                recv_sem=pr.at[k, sender], device_id=(owners[k] // 2, owners[k] % 2, c), device_id_type=MESH)

        def back(k, start):
            off = (units[k] % UPC) * UC
            blk = red.at[pl.ds(start, half), off:off + UC]
            return _remote(blk, blk, bs.at[k], br.at[k], 1)

        for k in range(NU + 1):
            @pl.when(i == k)
            def _():
                if k < NU:
                    if k >= 2:
                        exch(k - 2).wait_send()
                    dpu = dpu_ref[...]
                    tmp[k % 2] = _dot(ht_ref[pl.ds(r1, half), :], dpu)
                    exch(k).start()
                    mine[k] = _dot(ht_ref[pl.ds(r0, half), :], dpu)
                if k >= 1:
                    exch(k - 1).wait_recv()
                    mine[k - 1] += sib[k - 1]

                    @pl.when(ci != owners[k - 1])
                    def _():
                        stage[k - 1] = mine[k - 1].astype(BF16)
                        partial(k - 1, ci).start()

        @pl.when(i == NU)
        def _():
            acc_ref[...] = jnp.zeros_like(acc_ref)

        @pl.when(i >= NU)
        def _():
            dh = sum(_dot_nt(dp_ref[:, j * EC:(j + 1) * EC], w_ref[j]) for j in range(NCHIP))
            hp, xn, rstd = _rms_fwd(x_ref[...], np_ref[...])
            dx, dnp = _rms_bwd(dh * (1.0 + mod_ref[:, D:2 * D]), xn, rstd, np_ref[...])
            gx_ref[...] = gx1_ref[...] + dx
            acc_ref[0:1, :] += jnp.sum(dh, axis=0, keepdims=True)
            acc_ref[1:2, :] += jnp.sum(dh * hp, axis=0, keepdims=True)
            acc_ref[2:3, :] += dnp

        for t in range(UPC):
            @pl.when(i == NU + 1 + 2 * t)
            def _():
                for k in range(NCHIP * t, NCHIP * (t + 1)):
                    @pl.when(ci == owners[k])
                    def _():
                        off = (units[k] % UPC) * UC
                        red[pl.ds(r0, half), off:off + UC] = mine[k]
                        for s in range(NCHIP):
                            if s != owners[k]:
                                partial(k, s).wait_recv()
                                red[pl.ds(r0, half), off:off + UC] += got[units[k] % UPC, s].astype(F32)
                        back(k, r0).start()

        @pl.when(i == NU + nt - 1)
        def _():
            late.start()
            exch(NU - 2).wait_send()
            exch(NU - 1).wait_send()
            for k in range(NU):
                @pl.when(ci == owners[k])
                def _():
                    back(k, r1).wait_recv()
                    back(k, r0).wait_send()

                @pl.when(ci != owners[k])
                def _():
                    partial(k, ci).wait_send()
            gin_ref[...] = red[...]
            early.finish()
            late.forward()
            late.finish()

    tile = lambda w: pl.BlockSpec((ts, w), lambda i: (jnp.maximum(i - NU, 0), 0))
    c0 = lambda shape: pl.BlockSpec(shape, lambda i: (0, 0))
    vm = pl.BlockSpec(memory_space=pltpu.VMEM)
    hbm = pl.BlockSpec(memory_space=pl.ANY)
    gathered = [jax.ShapeDtypeStruct((NDEV,) + a.shape, a.dtype) for a in smalls] + [jax.ShapeDtypeStruct((NDEV, 8, D), F32)]
    return pl.pallas_call(
        body, name="in_proj_bwd", grid=(NU + nt,),
        in_specs=[vm, pl.BlockSpec((S, UC), lambda i: (0, _unit_of_step(jnp.minimum(i, NU - 1)))), tile(E),
                  hbm, tile(D), tile(D), c0((1, 3 * D)), c0((1, D))] + [vm] * ns,
        out_specs=[tile(D), vm] + [hbm] * (ns + 1),
        out_shape=[jax.ShapeDtypeStruct((S, D), F32), jax.ShapeDtypeStruct((D, EC), F32)] + gathered,
        scratch_shapes=[pltpu.VMEM((NU, half, UC), F32), pltpu.VMEM((NU, half, UC), F32),
                        pltpu.VMEM((2, half, UC), F32), pltpu.VMEM((NU, half, UC), BF16),
                        pltpu.VMEM((UPC, NCHIP, half, UC), BF16), pltpu.VMEM((D, EC), F32), pltpu.VMEM((8, D), F32),
                        pltpu.SemaphoreType.DMA((NU,)), pltpu.SemaphoreType.DMA((NU,)),
                        pltpu.SemaphoreType.DMA((NU,)), pltpu.SemaphoreType.DMA((NU, NCHIP)),
                        pltpu.SemaphoreType.DMA((NU,)), pltpu.SemaphoreType.DMA((NU,))]
        + _SmallGather.sems(ns) + _SmallGather.sems(1)
        + [pltpu.VMEM((NCHIP, D, EC), BF16), pltpu.SemaphoreType.DMA],
        compiler_params=_cp(("arbitrary",)),
    )(ht, dproj, dproj, w_in_bf, x, gx1, mod, norm_pre, *smalls)


def _local_step(x, cos, sin, target, mod, w_in_bf, proj, ht, w_out, conv_w, p):
    rec_p = (conv_w, p["conv_b"], p["w_rg_a"], p["b_rg_a"], p["w_rg_x"], p["b_rg_x"], p["lru_lambda"], p["norm_rec"])
    h_all, ya = _rec_fwd(proj, *rec_p)
    att, qr, kr, lse, w_out_bf = _att_fwd(proj, cos, sin, w_out)
    gx1, d_ya, d_att, dproj, gw_out, acc_o = _out_fwd_bwd(ya, att, proj, w_out_bf.reshape(D, D), x, target, mod,
                                                           p["norm_post"], p["norm_att"])
    dproj, g_out = _att_bwd(dproj, d_att, att, lse, qr, kr, proj, cos, sin, gw_out.reshape(NCHIP, D // NCHIP, D))
    dproj, dwa, dwx, sm = _rec_bwd(dproj, d_ya, proj, h_all, *rec_p)
    grad_x, g_in, *gathered = _in_proj_bwd(ht, dproj, w_in_bf, x, gx1, mod, p["norm_pre"], [acc_o, sm, dwa, dwx])
    return grad_x, g_in, g_out, gathered


def _me():
    return lax.axis_index("x"), lax.axis_index("y"), lax.axis_index("c")


def _flip(v, bit):
    return 1 - v if bit else v


def _peer(rel):
    x, y, c = _me()
    return (_flip(x, rel & 4), _flip(y, rel & 2), _flip(c, rel & 1))


def _remote(src, dst, send_sem, recv_sem, rel):
    return pltpu.make_async_remote_copy(src_ref=src, dst_ref=dst, send_sem=send_sem, recv_sem=recv_sem,
                                        device_id=_peer(rel), device_id_type=MESH)


class _WeightGather:
    SEMS = [pltpu.SemaphoreType.DMA((NCHIP - 1,))] * 4

    def __init__(self, w_ref, out_ref, send_sems, recv_sems, fsend_sems, frecv_sems):
        x, y, c = _me()
        self.w, self.out, self.ci = w_ref, out_ref, 2 * x + y
        self.half = w_ref.shape[0] // 2
        self.r0 = pl.multiple_of(c * self.half, self.half)
        self.r1 = pl.multiple_of((1 - c) * self.half, self.half)
        self.sems = (send_sems, recv_sems, fsend_sems, frecv_sems)

    def _ici(self, chip, k):
        blk = self.out.at[chip, pl.ds(self.r0, self.half), :]
        return _remote(blk, blk, self.sems[0].at[k - 1], self.sems[1].at[k - 1], 2 * k)

    def _d2d(self, chip, start, k):
        blk = self.out.at[chip, pl.ds(start, self.half), :]
        return _remote(blk, blk, self.sems[2].at[k - 1], self.sems[3].at[k - 1], 1)

    def start(self, diagonal=True):
        self.out[self.ci] = self.w[...].astype(BF16)
        for k in range(1, NCHIP if diagonal else NCHIP - 1):
            self._ici(self.ci, k).start()

    def _relay(self, chip, piece, k):
        q = self.half // 2
        blk = self.out.at[chip, pl.ds(self.r0 + piece * q, q), :]
        return _remote(blk, blk, self.relay_sems[0].at[piece], self.relay_sems[1].at[piece], 2 * k)

    def neighbours_landed(self, relay_send_sems, relay_recv_sems):
        self.relay_sems = (relay_send_sems, relay_recv_sems)
        for k in (1, 2):
            self._ici(self.ci ^ k, k).wait_recv()
        self._relay(self.ci ^ 2, 0, 1).start()
        self._relay(self.ci ^ 1, 1, 2).start()
        for k in (1, 2):
            self._d2d(self.ci ^ k, self.r0, k).start()

    def sibling_landed(self, k):
        self._d2d(self.ci ^ k, self.r1, k).wait_recv()

    def diagonal_landed(self):
        for piece, k in ((0, 1), (1, 2)):
            self._relay(self.ci ^ 3, piece, k).wait_recv()
        self._d2d(self.ci ^ 3, self.r0, 3).start()
        self._d2d(self.ci ^ 3, self.r1, 3).wait_recv()

    def finish_relayed(self):
        for k in (1, 2):
            self._ici(self.ci, k).wait_send()
        self._relay(self.ci ^ 2, 0, 1).wait_send()
        self._relay(self.ci ^ 1, 1, 2).wait_send()
        for k in range(1, NCHIP):
            self._d2d(self.ci ^ k, self.r0, k).wait_send()

    def forward(self):
        for k in range(1, NCHIP):
            self._ici(self.ci ^ k, k).wait_recv()
            self._d2d(self.ci ^ k, self.r0, k).start()

    def finish(self):
        for k in range(1, NCHIP):
            self._d2d(self.ci ^ k, self.r1, k).wait_recv()
        self.finish_sends()

    def finish_sends(self):
        for k in range(1, NCHIP):
            self._ici(self.ci, k).wait_send()
            self._d2d(self.ci ^ k, self.r0, k).wait_send()


class _SmallGather:
    @staticmethod
    def sems(n):
        return [pltpu.SemaphoreType.DMA((n, 7)), pltpu.SemaphoreType.DMA((n, 7)), pltpu.SemaphoreType.DMA((n,))]

    def __init__(self, srcs, outs, send_sems, recv_sems, local_sems):
        x, y, c = _me()
        self.srcs, self.outs = list(srcs), list(outs)
        self.ss, self.rs, self.ls = send_sems, recv_sems, local_sems
        self.ci, self.c = 2 * x + y, c
        self.me = 2 * self.ci + c

    def _own(self, a, slot, rel):
        return _remote(self.srcs[a], self.outs[a].at[self.me], self.ss.at[a, slot], self.rs.at[a, slot], rel)

    def _block(self, a, idx, slot, rel):
        blk = self.outs[a].at[idx]
        return _remote(blk, blk, self.ss.at[a, slot], self.rs.at[a, slot], rel)

    def _local(self, a):
        return pltpu.make_async_copy(self.srcs[a], self.outs[a].at[self.me], self.ls.at[a])

    def start(self):
        for a in range(len(self.srcs)):
            self._local(a).start()
            self._own(a, 0, 1).start()
            for k in range(1, NCHIP):
                self._own(a, k, 2 * k).start()

    def forward(self):
        for a in range(len(self.srcs)):
            for k in range(1, NCHIP):
                idx = 2 * (self.ci ^ k) + self.c
                self._block(a, idx, k, 2 * k).wait_recv()
                self._block(a, idx, 3 + k, 1).start()

    def finish(self):
        for a in range(len(self.srcs)):
            self._block(a, 2 * self.ci + 1 - self.c, 0, 1).wait_recv()
            for k in range(1, NCHIP):
                self._block(a, 2 * (self.ci ^ k) + 1 - self.c, 3 + k, 1).wait_recv()
            self._own(a, 0, 1).wait_send()
            for k in range(1, NCHIP):
                self._own(a, k, 2 * k).wait_send()
                self._block(a, 2 * (self.ci ^ k) + self.c, 3 + k, 1).wait_send()
            self._local(a).wait()


def _start_in_proj(c, conv_w, w_ada, b_ada, w_in, pos, x, norm_pre, order):
    ts = 512
    nt = S // ts
    wc = D + conv_w.size

    def body(order_ref, c_ref, cw_ref, wada_ref, b_ref, win_ref, pos_ref, freq_ref, x_ref, np_ref,
             g0_ref, conv_ref, mod_ref, wbf_ref, cos_ref, sin_ref, proj_ref, ht_ref,
             crow_ref, g0s, modp, modb, wbuf, hb_all, cs, cr, ms, mr, ws, wr, fs, fr, local_sems, ys, yr, osem):
        s, t = pl.program_id(0), pl.program_id(1)
        x, y, c = _me()
        ci = 2 * x + y
        me = 2 * ci + c
        wg = _WeightGather(win_ref, wbuf, ws, wr, fs, fr)
        cw = R // NCHIP

        @pl.when(jnp.logical_and(s == 0, t == 0))
        def _():
            wg.start(diagonal=False)
            crow_ref[:, 0:D] = c_ref[...]
            for k in range(4):
                crow_ref[:, D + k * cw:D + (k + 1) * cw] = cw_ref[k:k + 1, :]
            mine = pltpu.make_async_copy(crow_ref, g0s.at[pl.ds(me, 1), :], local_sems.at[0])
            mine.start()
            csend = [_remote(crow_ref, g0s.at[pl.ds(me, 1), :], cs.at[r - 1], cr.at[r - 1], r) for r in range(1, NDEV)]
            for cp in csend:
                cp.start()
            cos_ref[...], sin_ref[...] = _cos_sin(pos_ref, freq_ref)
            for r in range(1, NDEV):
                px, py, pc = _peer(r)
                _remote(crow_ref, g0s.at[pl.ds(4 * px + 2 * py + pc, 1), :], cs.at[r - 1], cr.at[r - 1], r).wait_recv()
            mine.wait()
            cv = g0s[:, 0:D]
            sc = cv * _sigmoid(cv)
            scb = jnp.concatenate([sc, jnp.zeros_like(sc)], axis=0).astype(BF16)
            b_cols = sum(jnp.where(ci == j, b_ref[:, j * EC:(j + 1) * EC], 0.0) for j in range(NCHIP))
            modp[...] = _dot(scb, wada_ref[...].astype(BF16))[0:NDEV, :] + b_cols
            own = pltpu.make_async_copy(modp.at[pl.ds(me, 1), :], modb.at[ci], local_sems.at[1])
            own.start()
            msend = []
            for k in range(1, NCHIP):
                cp = _remote(modp.at[pl.ds(2 * (ci ^ k) + c, 1), :], modb.at[ci], ms.at[k - 1], mr.at[k - 1], 2 * k)
                cp.start()
                msend.append(cp)
            for k in range(1, NCHIP):
                _remote(modp.at[pl.ds(me, 1), :], modb.at[ci ^ k], ms.at[k - 1], mr.at[k - 1], 2 * k).wait_recv()
            own.wait()
            for j in range(NCHIP):
                mod_ref[:, j * EC:(j + 1) * EC] = modb[j]
            for cp in csend + msend:
                cp.wait_send()
            g0_ref[...] = g0s[...]
            for j in range(NCHIP):
                for k in range(4):
                    conv_ref[k:k + 1, j * cw:(j + 1) * cw] = g0s[2 * j:2 * j + 1, D + k * cw:D + (k + 1) * cw]

        def keep(k):
            return pltpu.make_async_copy(wbuf.at[ci ^ k], wbf_ref.at[ci ^ k], osem.at[k])

        @pl.when(jnp.logical_and(s == 1, t == 0))
        def _():
            keep(0).start()
            wg.neighbours_landed(ys, yr)
            wg.sibling_landed(1)
            keep(1).start()

        @pl.when(jnp.logical_and(s == 2, t == 0))
        def _():
            wg.sibling_landed(2)
            keep(2).start()

        @pl.when(jnp.logical_and(s == 3, t == 0))
        def _():
            wg.relay_sems = (ys, yr)
            wg.diagonal_landed()
            keep(3).start()

        rows = pl.ds(pl.multiple_of(t * ts, ts), ts)

        @pl.when(s == 0)
        def _():
            hp, _, _ = _rms_fwd(x_ref[...], np_ref[...])
            h = hp * (1.0 + mod_ref[:, D:2 * D]) + mod_ref[:, 0:D]
            hb_all[rows, :] = h.astype(BF16)
            ht_ref[...] = h.T.astype(BF16)

        proj_ref[...] = _dot(hb_all[rows, :], wbuf[ci ^ s])

        @pl.when(jnp.logical_and(s == NCHIP - 1, t == nt - 1))
        def _():
            wg.relay_sems = (ys, yr)
            wg.finish_relayed()
            for k in range(NCHIP):
                keep(k).wait()

    vm = pl.BlockSpec(memory_space=pltpu.VMEM)
    first_pass = lambda s, t: jnp.where(s == 0, t, nt - 1)
    grid_spec = pltpu.PrefetchScalarGridSpec(
        num_scalar_prefetch=1, grid=(NCHIP, nt),
        in_specs=[vm, vm, vm, vm, vm, vm, vm, pl.BlockSpec((ts, D), lambda s, t, o: (first_pass(s, t), 0)),
                  pl.BlockSpec((1, D), lambda s, t, o: (0, 0))],
        out_specs=[vm, vm, vm, pl.BlockSpec(memory_space=pl.ANY), vm, vm,
                   pl.BlockSpec((ts, EC), lambda s, t, o: (t, o[s])),
                   pl.BlockSpec((D, ts), lambda s, t, o: (0, first_pass(s, t)))],
        scratch_shapes=[pltpu.VMEM((1, wc), F32),
                        pltpu.VMEM((NDEV, wc), F32), pltpu.VMEM((NDEV, EC), F32), pltpu.VMEM((NCHIP, 1, EC), F32),
                        pltpu.VMEM((NCHIP, D, EC), BF16), pltpu.VMEM((S, D), BF16),
                        pltpu.SemaphoreType.DMA((NDEV - 1,)), pltpu.SemaphoreType.DMA((NDEV - 1,)),
                        pltpu.SemaphoreType.DMA((NCHIP - 1,)), pltpu.SemaphoreType.DMA((NCHIP - 1,))]
        + _WeightGather.SEMS + [pltpu.SemaphoreType.DMA((2,))] * 3 + [pltpu.SemaphoreType.DMA((NCHIP,))])
    return pl.pallas_call(
        body, name="start_in_proj", grid_spec=grid_spec,
        out_shape=[jax.ShapeDtypeStruct((NDEV, wc), F32), jax.ShapeDtypeStruct((4, R), F32),
                   jax.ShapeDtypeStruct((1, 3 * D), F32),
                   jax.ShapeDtypeStruct((NCHIP, D, EC), BF16), jax.ShapeDtypeStruct((S, LANES), F32),
                   jax.ShapeDtypeStruct((S, LANES), F32), jax.ShapeDtypeStruct((S, E), F32),
                   jax.ShapeDtypeStruct((D, S), BF16)],
        compiler_params=_cp(("arbitrary", "arbitrary")),
    )(order, c, conv_w, w_ada, b_ada, w_in, pos, _rope_freq(), x, norm_pre)


class _ReduceScatter:
    @staticmethod
    def scratch(n_units, rows, ucols, max_owned):
        half = rows // 2
        return [pltpu.VMEM((n_units, half, ucols), F32), pltpu.VMEM((n_units, half, ucols), BF16),
                pltpu.VMEM((max_owned, NCHIP, half, ucols), BF16),
                pltpu.SemaphoreType.DMA((2,)), pltpu.SemaphoreType.DMA((n_units,)),
                pltpu.SemaphoreType.DMA((n_units, NCHIP)), pltpu.SemaphoreType.DMA((n_units,)),
                pltpu.SemaphoreType.DMA((n_units,))]

    def __init__(self, g_ref, out_ref, units, sib, stage, got, sem1, send2, recv2, send3, recv3):
        x, y, c = _me()
        self.c, self.ci = c, 2 * x + y
        self.g, self.out, self.units = g_ref, out_ref, units
        self.sib, self.stage, self.got = sib, stage, got
        self.sem1, self.send2, self.recv2, self.send3, self.recv3 = sem1, send2, recv2, send3, recv3
        self.half = g_ref.shape[1] // 2
        self.ucols = g_ref.shape[2]
        self.r0 = pl.multiple_of(c * self.half, self.half)
        self.r1 = pl.multiple_of((1 - c) * self.half, self.half)
        self.slot0 = units[0][0]
        assert [u[0] for u in units] == list(range(self.slot0, self.slot0 + len(units)))
        seen = {}
        self.local = []
        for _, owner, _ in units:
            self.local.append(seen.get(owner, 0))
            seen[owner] = seen.get(owner, 0) + 1

    def _halves(self):
        n = len(self.units)
        return _remote(self.g.at[pl.ds(self.slot0, n), pl.ds(self.r1, self.half), :], self.sib,
                       self.sem1.at[0], self.sem1.at[1], 1)

    def _partial(self, i, sender):
        _, owner, _ = self.units[i]
        return pltpu.make_async_remote_copy(
            src_ref=self.stage.at[i], dst_ref=self.got.at[self.local[i], sender],
            send_sem=self.send2.at[i], recv_sem=self.recv2.at[i, sender],
            device_id=(owner // 2, owner % 2, self.c), device_id_type=MESH)

    def _back(self, i, start):
        off = self.units[i][2]
        blk = self.out.at[pl.ds(start, self.half), off:off + self.ucols]
        return _remote(blk, blk, self.send3.at[i], self.recv3.at[i], 1)

    def start_halves(self):
        self._halves().start()

    def send_partials(self):
        self._halves().wait_recv()
        for i, (slot, owner, _) in enumerate(self.units):
            @pl.when(self.ci != owner)
            def _():
                self.stage[i] = (self.g[slot, pl.ds(self.r0, self.half), :] + self.sib[i]).astype(BF16)
                self._partial(i, self.ci).start()

    def reduce_owned(self):
        for i, (slot, owner, off) in enumerate(self.units):
            @pl.when(self.ci == owner)
            def _():
                rows, cols = pl.ds(self.r0, self.half), slice(off, off + self.ucols)
                self.out[rows, cols] = self.g[slot, pl.ds(self.r0, self.half), :] + self.sib[i]
                for s in range(NCHIP):
                    if s != owner:
                        self._partial(i, s).wait_recv()
                        self.out[rows, cols] += self.got[self.local[i], s].astype(F32)
                self._back(i, self.r0).start()

    def finish(self):
        self._halves().wait_send()
        for i, (_, owner, _) in enumerate(self.units):
            @pl.when(self.ci == owner)
            def _():
                self._back(i, self.r1).wait_recv()
                self._back(i, self.r0).wait_send()

            @pl.when(self.ci != owner)
            def _():
                self._partial(i, self.ci).wait_send()


def _silu_rows(c_ref):
    cv = c_ref[:, 0:D]
    sc = cv * _sigmoid(cv)
    return jnp.concatenate([sc, jnp.zeros_like(sc)], axis=0).astype(BF16)


def _adamw(groups):
    steps = 4
    specs = [pl.BlockSpec((w.shape[0] // steps, w.shape[1]), lambda i: (i, 0)) for w, _, _, _ in groups]

    def body(*refs):
        ins, outs = refs[:4 * len(groups)], refs[4 * len(groups):]
        for j in range(len(groups)):
            w_ref, g_ref, m_ref, v_ref = ins[4 * j:4 * j + 4]
            d_ref, nm_ref, nv_ref = outs[3 * j:3 * j + 3]
            d_ref[...], nm_ref[...], nv_ref[...] = _adamw_values(w_ref[...], g_ref[...], m_ref[...], v_ref[...])

    res = pl.pallas_call(
        body, name="adamw_big", grid=(steps,),
        in_specs=[s for s in specs for _ in range(4)], out_specs=[s for s in specs for _ in range(3)],
        out_shape=[jax.ShapeDtypeStruct(w.shape, F32) for w, _, _, _ in groups for _ in range(3)],
        compiler_params=_cp(("parallel",)),
    )(*[a for grp in groups for a in grp])
    return [res[3 * j:3 * j + 3] for j in range(len(groups))]


def _adamw_values(w, g, m, v):
    nm = B1 * m + (1.0 - B1) * g
    nv = B2 * v + (1.0 - B2) * (g * g)
    m_hat = nm / (1.0 - B1 ** STEP)
    v_hat = nv / (1.0 - B2 ** STEP)
    return (-LR) * (m_hat / (jnp.sqrt(v_hat) + ADAM_EPS) + WD * w), nm, nv


NB = R // HEAD
SMALL = (("b_ada", (1, 3 * D)), ("norm_pre", (1, D)), ("norm_post", (1, D)), ("conv_w", (4, R // NCHIP)),
         ("conv_b", (1, R)), ("w_rg_a", (NB, HEAD, HEAD)), ("b_rg_a", (1, R)), ("w_rg_x", (NB, HEAD, HEAD)),
         ("b_rg_x", (1, R)), ("lru_lambda", (1, R)), ("norm_rec", (1, R)), ("norm_att", (1, R)))


def _small_update(ao8, sm8, dwa8, dwx8, ai8, cg, params):
    n = len(SMALL)

    def body(ao_ref, sm_ref, dwa_ref, dwx_ref, ai_ref, cg_ref, *refs):
        pin, pout, (gada_ref, loss_ref, dmod) = refs[:3 * n], refs[3 * n:7 * n], refs[7 * n:]
        xx, yy, _ = _me()
        ci = 2 * xx + yy

        def total(ref, *idx):
            acc = ref[(0,) + idx].astype(F32)
            for d in range(1, NDEV):
                acc = acc + ref[(d,) + idx].astype(F32)
            return acc

        row = lambda ref, r, lanes=slice(None): total(ref, slice(r, r + 1), lanes)
        mine = lambda parts: sum(jnp.where(ci == j, part, 0.0) for j, part in enumerate(parts))
        cw = R // NCHIP
        grads = {
            "b_ada": [jnp.concatenate([row(ai_ref, 0), row(ai_ref, 1), row(ao_ref, 0)], axis=1)],
            "norm_pre": [row(ai_ref, 2)], "norm_post": [row(ao_ref, 1)],
            "conv_w": [mine([row(sm_ref, 8 + r, slice(j * cw, (j + 1) * cw)) for j in range(NCHIP)]) for r in range(4)],
            "conv_b": [row(sm_ref, 4)], "b_rg_a": [row(sm_ref, 0)], "b_rg_x": [row(sm_ref, 1)],
            "lru_lambda": [row(sm_ref, 2)], "norm_rec": [row(sm_ref, 3)], "norm_att": [row(ao_ref, 2, slice(0, R))],
            "w_rg_a": [total(dwa_ref, h) for h in range(NB)], "w_rg_x": [total(dwx_ref, h) for h in range(NB)],
        }
        loss_ref[...] = row(ao_ref, 3, slice(0, LANES)) * (0.5 / D)
        for k, (name, shape) in enumerate(SMALL):
            w_ref, m_ref, v_ref = pin[3 * k:3 * k + 3]
            outs = pout[4 * k:4 * k + 4]
            for r, g in enumerate(grads[name]):
                at = (slice(None),) if len(grads[name]) == 1 else ((r,) if len(shape) == 3 else (slice(r, r + 1),))
                res = (g,) + _adamw_values(w_ref[at], g, m_ref[at], v_ref[at])
                for o_ref, val in zip(outs, res):
                    o_ref[at] = val
        for d in range(NDEV):
            dmod[d:d + 1, :] = jnp.concatenate([ai_ref[d, 0:1, :], ai_ref[d, 1:2, :], ao_ref[d, 0:1, :]], axis=1)
        cols = mine([dmod[:, j * EC:(j + 1) * EC] for j in range(NCHIP)])
        colsb = jnp.concatenate([cols, jnp.zeros_like(cols)], axis=0).astype(BF16)
        gada_ref[...] = _dot_tn(_silu_rows(cg_ref), colsb)

    shapes = [jax.ShapeDtypeStruct(s, F32) for _, s in SMALL]
    outs = pl.pallas_call(
        body, name="small_update",
        out_shape=[s for s in shapes for _ in range(4)] + [jax.ShapeDtypeStruct((D, EC), F32),
                                                           jax.ShapeDtypeStruct((1, LANES), F32)],
        scratch_shapes=[pltpu.VMEM((NDEV, 3 * D), F32)],
        compiler_params=_cp(),
    )(ao8, sm8, dwa8, dwx8, ai8, cg, *params)
    return outs[:4 * n], outs[4 * n], outs[4 * n + 1]


BIG = ("w_ada", "w_in", "w_out")
WEIGHTS = ("w_ada", "b_ada", "norm_pre", "norm_post", "w_in", "conv_w", "conv_b", "w_rg_a", "b_rg_a", "w_rg_x",
           "b_rg_x", "lru_lambda", "norm_rec", "norm_att", "w_out")


def kernel(x, c, positions, w_ada, b_ada, norm_pre, norm_post, w_in, conv_w, conv_b, w_rg_a, b_rg_a, w_rg_x, b_rg_x, lru_lambda, norm_rec, norm_att, w_out, loss_target, m_w_ada, m_b_ada, m_norm_pre, m_norm_post, m_w_in, m_conv_w, m_conv_b, m_w_rg_a, m_b_rg_a, m_w_rg_x, m_b_rg_x, m_lru_lambda, m_norm_rec, m_norm_att, m_w_out, v_w_ada, v_b_ada, v_norm_pre, v_norm_post, v_w_in, v_conv_w, v_conv_b, v_w_rg_a, v_b_rg_a, v_w_rg_x, v_b_rg_x, v_lru_lambda, v_norm_rec, v_norm_att, v_w_out):
    given = dict(locals())
    wts = {n: given[n] for n in WEIGHTS}
    ms = {n: given["m_" + n] for n in WEIGHTS}
    vs = {n: given["v_" + n] for n in WEIGHTS}
    xi, yi, _ = _me()
    chip = 2 * xi + yi

    order = (chip ^ jnp.arange(NCHIP, dtype=jnp.int32)).astype(jnp.int32)
    cg, conv_full, mod, w_in_bf, cos, sin, proj, ht = _start_in_proj(
        c, conv_w[0], w_ada[0], b_ada, w_in[0], positions, x[0], norm_pre, order)

    p = dict(norm_pre=norm_pre, norm_post=norm_post, conv_b=conv_b, b_rg_a=b_rg_a, b_rg_x=b_rg_x,
             lru_lambda=lru_lambda, norm_rec=norm_rec, norm_att=norm_att, w_rg_a=w_rg_a[0], w_rg_x=w_rg_x[0])
    grad_x, g_in, g_out, gathered = _local_step(
        x[0], cos, sin, loss_target[0], mod, w_in_bf, proj, ht, w_out[0], conv_full, p)

    params = [d[n].reshape(shape) for n, shape in SMALL for d in (wts, ms, vs)]
    small_out, g_ada, loss_row = _small_update(*gathered, cg, params)
    grads = {"w_out": g_out, "w_in": g_in, "w_ada": g_ada}
    delta, new_m, new_v = {}, {}, {}
    for k, (n, _) in enumerate(SMALL):
        grads[n], delta[n], new_m[n], new_v[n] = small_out[4 * k:4 * k + 4]
    for n, res in zip(BIG, _adamw([(wts[n][0], grads[n], ms[n][0], vs[n][0]) for n in BIG])):
        delta[n], new_m[n], new_v[n] = res
    out = lambda d: [d[n].reshape(wts[n].shape) for n in WEIGHTS]
    return (loss_row[0, 0], grad_x.reshape(x.shape), *out(grads), *out(delta), *out(new_m), *out(new_v))
```

```python
import numpy as np
import jax
import jax.numpy as jnp
from jax import lax
from jax.experimental import pallas as pl
from jax.experimental.pallas import tpu as pltpu

F32 = jnp.float32
BF16 = jnp.bfloat16

S = 2048
D = 1024
E = 3072
R = 512
NDEV = 8
NCHIP = 4
EC = 768
LRU_C = 8.0
EPS = 1e-6
NEG = -1e30
HEAD = 64
BLK = 128
PATTERNS = (1, 4, 16)
ROPE_THETA = 10000.0
LANES = 128
VMEM_LIMIT = 56 * 1024 * 1024

B1, B2, LR, WD, ADAM_EPS, STEP = 0.9, 0.999, 0.001, 0.01, 1e-8, 10
MESH = pl.DeviceIdType.MESH


def _cp(sem=None, **kw):
    return pltpu.CompilerParams(dimension_semantics=sem, vmem_limit_bytes=VMEM_LIMIT, **kw)


def _dot(a, b):
    return jnp.dot(a, b, preferred_element_type=F32)


def _dot_nt(a, b):
    return lax.dot_general(a, b, (((1,), (1,)), ((), ())), preferred_element_type=F32)


def _dot_tn(a, b):
    return lax.dot_general(a, b, (((0,), (0,)), ((), ())), preferred_element_type=F32)


def _sigmoid(x):
    return 1.0 / (1.0 + jnp.exp(-x))


def _one_minus_exp(x, ex):
    poly = -x * (1.0 + x * (0.5 + x * (1.0 / 6 + x * (1.0 / 24))))
    return jnp.where(x > -1.0 / 16, poly, 1.0 - ex)


def _rms_fwd(v, g):
    rstd = lax.rsqrt(jnp.mean(v * v, axis=-1, keepdims=True) + EPS)
    vn = v * rstd
    return vn * g, vn, rstd


def _rms_bwd(dy, vn, rstd, g):
    dvn = dy * g
    dv = rstd * (dvn - vn * jnp.mean(dvn * vn, axis=-1, keepdims=True))
    return dv, jnp.sum(dy * vn, axis=0, keepdims=True)


RT = 256


def _shift_down(cur, prev8, j, row):
    if j == 0:
        return cur
    rolled = pltpu.roll(cur, j, 0)
    top = jnp.where(row[0:8] >= j, rolled[0:8], pltpu.roll(prev8, j, 0))
    return jnp.concatenate([top, rolled[8:]], axis=0)


def _shift_up(cur, next8, j, row):
    if j == 0:
        return cur
    rolled = pltpu.roll(cur, RT - j, 0)
    bot = jnp.where(row[RT - 8:] < RT - j, rolled[RT - 8:], pltpu.roll(next8, 8 - j, 0))
    return jnp.concatenate([rolled[:RT - 8], bot], axis=0)


def _rec_gates(xp, xprev8, row, cw_ref, cb_ref, wa_ref, ba_ref, wx_ref, bx_ref, lam_ref):
    xa = cb_ref[...] + sum(cw_ref[3 - j:4 - j, :] * _shift_down(xp, xprev8, j, row) for j in range(4))
    xab = xa.astype(BF16)
    r = _sigmoid(_dot(xab, wa_ref[...]) + ba_ref[...])
    ig = _sigmoid(_dot(xab, wx_ref[...]) + bx_ref[...])
    nl = -lam_ref[...]
    sp = jnp.maximum(nl, 0.0) + jnp.log1p(jnp.exp(-jnp.abs(nl)))
    la = (-LRU_C) * r * sp
    a = jnp.exp(la)
    mult = jnp.sqrt(_one_minus_exp(2.0 * la, a * a))
    return dict(xa=xa, xab=xab, r=r, ig=ig, sp=sp, la=la, a=a, mult=mult)


def _scan_fwd(a, u, row):
    sh = 1
    while sh < RT:
        a_s = jnp.where(row >= sh, pltpu.roll(a, sh, 0), 1.0)
        u_s = jnp.where(row >= sh, pltpu.roll(u, sh, 0), 0.0)
        u = a * u_s + u
        a = a * a_s
        sh *= 2
    return a, u


def _scan_bwd(al, g, row):
    sh = 1
    while sh < RT:
        al_s = jnp.where(row < RT - sh, pltpu.roll(al, RT - sh, 0), 1.0)
        g_s = jnp.where(row < RT - sh, pltpu.roll(g, RT - sh, 0), 0.0)
        g = g + al * g_s
        al = al * al_s
        sh *= 2
    return g


def _dense_from_blocks(blocks_ref, dense_ref):
    dense_ref[...] = jnp.zeros_like(dense_ref)
    for h in range(R // HEAD):
        dense_ref[h * HEAD:(h + 1) * HEAD, h * HEAD:(h + 1) * HEAD] = blocks_ref[h].astype(dense_ref.dtype)


def _rec_fwd(proj, conv_w, conv_b, wa_b, ba, wx_b, bx, lam, norm_rec):
    nt = S // RT

    def body(p_ref, cw_ref, cb_ref, wa_ref, ba_ref, wx_ref, bx_ref, lam_ref, nr_ref,
             h_ref, ya_ref, prev8, hc, wad, wxd):
        i = pl.program_id(0)

        @pl.when(i == 0)
        def _():
            prev8[...] = jnp.zeros_like(prev8)
            hc[...] = jnp.zeros_like(hc)
            _dense_from_blocks(wa_ref, wad)
            _dense_from_blocks(wx_ref, wxd)

        row = lax.broadcasted_iota(jnp.int32, (RT, R), 0)
        xp = p_ref[:, 0:R]
        ga = p_ref[:, R:2 * R]
        f = _rec_gates(xp, prev8[...], row, cw_ref, cb_ref, wad, ba_ref, wxd, bx_ref, lam_ref)
        u = f["mult"] * (f["ig"] * f["xa"])
        acum, hh = _scan_fwd(f["a"], u, row)
        h = hh + acum * hc[0:1, :]
        h_ref[...] = h
        hc[0:1, :] = h_ref[RT - 1:RT, :]
        prev8[...] = p_ref[RT - 8:RT, 0:R]
        yp = h * (ga * _sigmoid(ga))
        ya, _, _ = _rms_fwd(yp, nr_ref[...])
        ya_ref[...] = ya.astype(BF16)

    row1 = lambda n: pl.BlockSpec((1, n), lambda i: (0, 0))
    blocks = pl.BlockSpec((R // HEAD, HEAD, HEAD), lambda i: (0, 0, 0))
    return pl.pallas_call(
        body, name="rec_fwd", grid=(nt,),
        in_specs=[pl.BlockSpec((RT, 2 * R), lambda i: (i, 0)), pl.BlockSpec((4, R), lambda i: (0, 0)), row1(R),
                  blocks, row1(R), blocks, row1(R), row1(R), row1(R)],
        out_specs=[pl.BlockSpec((RT, R), lambda i: (i, 0)), pl.BlockSpec((RT, R), lambda i: (i, 0))],
        out_shape=[jax.ShapeDtypeStruct((S, R), F32), jax.ShapeDtypeStruct((S, R), BF16)],
        scratch_shapes=[pltpu.VMEM((8, R), F32), pltpu.VMEM((8, R), F32), pltpu.VMEM((R, R), BF16),
                        pltpu.VMEM((R, R), BF16)],
        compiler_params=_cp(("arbitrary",)),
    )(proj, conv_w, conv_b, wa_b, ba, wx_b, bx, lam, norm_rec)


def _rec_bwd(dproj, d_ya, proj, h_all, conv_w, conv_b, wa_b, ba, wx_b, bx, lam, norm_rec):
    nt = S // RT

    def body(dp_in, dya_ref, p_ref, pprev_ref, h_ref, hprev_ref, cw_ref, cb_ref, wab_ref, ba_ref, wxb_ref, bx_ref,
             lam_ref, nr_ref, dp_ref, dwab_ref, dwxb_ref, sm_ref, nxt8, cg, wa_ref, wx_ref, dwa_ref, dwx_ref):
        i = pl.program_id(0)
        ti = nt - 1 - i

        @pl.when(i == 0)
        def _():
            nxt8[...] = jnp.zeros_like(nxt8)
            cg[...] = jnp.zeros_like(cg)
            dwa_ref[...] = jnp.zeros_like(dwa_ref)
            dwx_ref[...] = jnp.zeros_like(dwx_ref)
            sm_ref[...] = jnp.zeros_like(sm_ref)
            _dense_from_blocks(wab_ref, wa_ref)
            _dense_from_blocks(wxb_ref, wx_ref)

        row = lax.broadcasted_iota(jnp.int32, (RT, R), 0)
        first = (ti > 0).astype(F32)
        xprev8 = pprev_ref[...] * first
        hprev8 = hprev_ref[...] * first
        xp = p_ref[:, 0:R]
        ga = p_ref[:, R:2 * R]
        f = _rec_gates(xp, xprev8, row, cw_ref, cb_ref, wa_ref, ba_ref, wx_ref, bx_ref, lam_ref)
        xa, r, ig, a, mult = f["xa"], f["r"], f["ig"], f["a"], f["mult"]
        h = h_ref[...]
        sg = _sigmoid(ga)
        gate = ga * sg
        yp = h * gate
        _, ypn, rstd = _rms_fwd(yp, nr_ref[...])
        d_yp, dnr = _rms_bwd(dya_ref[...], ypn, rstd, nr_ref[...])
        d_ga = d_yp * h * (sg * (1.0 + ga * (1.0 - sg)))
        dh = d_yp * gate + jnp.where(row == RT - 1, cg[0:1, :], 0.0)
        al = jnp.where(row < RT - 1, pltpu.roll(a, RT - 1, 0), 0.0)
        g = _scan_bwd(al, dh, row)
        cg[0:1, :] = jnp.sum(jnp.where(row == 0, a * g, 0.0), axis=0, keepdims=True)
        h_m1 = _shift_down(h, hprev8, 1, row)
        da = g * h_m1
        ix = ig * xa
        d_mult = g * ix
        d_ig = g * mult * xa
        d_xa = g * mult * ig
        d_la = da * a - d_mult * (a * a) / mult
        d_r = d_la * ((-LRU_C) * f["sp"])
        dsp = jnp.sum(d_la * ((-LRU_C) * r), axis=0, keepdims=True)
        dlam = dsp * (-_sigmoid(-lam_ref[...]))
        d_za = d_r * r * (1.0 - r)
        d_zx = d_ig * ig * (1.0 - ig)
        dzab = d_za.astype(BF16)
        dzxb = d_zx.astype(BF16)
        dwa_ref[...] += _dot_tn(f["xab"], dzab)
        dwx_ref[...] += _dot_tn(f["xab"], dzxb)
        d_xa = d_xa + _dot_nt(dzab, wa_ref[...]) + _dot_nt(dzxb, wx_ref[...])
        d_xp = sum(cw_ref[3 - j:4 - j, :] * _shift_up(d_xa, nxt8[...], j, row) for j in range(4))
        dcw = [jnp.sum(d_xa * _shift_down(xp, xprev8, 3 - k, row), axis=0, keepdims=True) for k in range(4)]
        dp_ref[:, 0:R] = d_xp.astype(BF16)
        dp_ref[:, R:2 * R] = d_ga.astype(BF16)
        dp8 = d_xa[0:8, :]
        nxt8[...] = dp8
        sm_ref[0:1, :] += jnp.sum(d_za, axis=0, keepdims=True)
        sm_ref[1:2, :] += jnp.sum(d_zx, axis=0, keepdims=True)
        sm_ref[2:3, :] += dlam
        sm_ref[3:4, :] += dnr
        sm_ref[4:5, :] += jnp.sum(d_xa, axis=0, keepdims=True)
        for k in range(4):
            sm_ref[8 + k:9 + k, :] += dcw[k]

        @pl.when(i == nt - 1)
        def _():
            for h in range(R // HEAD):
                dwab_ref[h] = dwa_ref[h * HEAD:(h + 1) * HEAD, h * HEAD:(h + 1) * HEAD].astype(BF16)
                dwxb_ref[h] = dwx_ref[h * HEAD:(h + 1) * HEAD, h * HEAD:(h + 1) * HEAD].astype(BF16)

    c0 = lambda shape: pl.BlockSpec(shape, lambda i: (0, 0))
    blocks = pl.BlockSpec((R // HEAD, HEAD, HEAD), lambda i: (0, 0, 0))
    rev = lambda i: nt - 1 - i
    prev8 = lambda i: (jnp.maximum((nt - 1 - i) * (RT // 8) - 1, 0), 0)
    return pl.pallas_call(
        body, name="rec_bwd", grid=(nt,),
        in_specs=[pl.BlockSpec(memory_space=pl.ANY),
                  pl.BlockSpec((RT, R), lambda i: (rev(i), 0)),
                  pl.BlockSpec((RT, 2 * R), lambda i: (rev(i), 0)), pl.BlockSpec((8, R), prev8),
                  pl.BlockSpec((RT, R), lambda i: (rev(i), 0)), pl.BlockSpec((8, R), prev8),
                  c0((4, R)), c0((1, R)), blocks, c0((1, R)), blocks, c0((1, R)), c0((1, R)), c0((1, R))],
        out_specs=[pl.BlockSpec((RT, 2 * R), lambda i: (rev(i), 0)), blocks, blocks, c0((16, R))],
        out_shape=[jax.ShapeDtypeStruct((S, E), BF16), jax.ShapeDtypeStruct((R // HEAD, HEAD, HEAD), BF16),
                   jax.ShapeDtypeStruct((R // HEAD, HEAD, HEAD), BF16), jax.ShapeDtypeStruct((16, R), F32)],
        scratch_shapes=[pltpu.VMEM((8, R), F32), pltpu.VMEM((8, R), F32), pltpu.VMEM((R, R), BF16),
                        pltpu.VMEM((R, R), BF16), pltpu.VMEM((R, R), F32), pltpu.VMEM((R, R), F32)],
        input_output_aliases={0: 0},
        compiler_params=_cp(("arbitrary",)),
    )(dproj, d_ya, proj, proj, h_all, h_all, conv_w, conv_b, wa_b, ba, wx_b, bx, lam, norm_rec)


NPAIR = R // LANES
QB, KB, VB, GB = 2 * R // LANES, 3 * R // LANES, 4 * R // LANES, 5 * R // LANES


def _rope_freq():
    half = HEAD // 2
    inv = np.float32(ROPE_THETA) ** (-(np.arange(half, dtype=np.float32) / np.float32(half)))
    return jnp.asarray(np.tile(inv.astype(np.float32), LANES // half)[None, :])


def _rot_half(x, first):
    return jnp.where(first, -pltpu.roll(x, LANES - HEAD // 2, 1), pltpu.roll(x, HEAD // 2, 1))


def _cos_sin(pos_ref, freq_ref):
    pos = jnp.broadcast_to(pos_ref[...].astype(F32), (LANES, S)).T
    ang = pos * freq_ref[...]
    return jnp.cos(ang), jnp.sin(ang)


SUB = 4


def _stages(d):
    assert d in (1, SUB, SUB * SUB)
    return d > SUB


def _strided_rows(src_ref, d, tmp):
    n = S // d
    if not _stages(d):
        for r in range(d):
            yield r * n, (src_ref[pl.ds(r, n, stride=d), :] if d > 1 else src_ref[...])
        return
    m = S // SUB
    for r in range(SUB):
        tmp[r * m:(r + 1) * m, :] = src_ref[pl.ds(r, m, stride=SUB), :]
    for r in range(SUB):
        for q in range(SUB):
            yield (r + SUB * q) * n, tmp[pl.ds(r * m + q, n, stride=SUB), :]


def _deint(src_ref, dst_ref, d, tmp):
    n = S // d
    for row0, v in _strided_rows(src_ref, d, tmp):
        dst_ref[row0:row0 + n, :] = v.astype(dst_ref.dtype)


def _reint(src_ref, dst_ref, d, accumulate, tmp):
    if _stages(d):
        n, m = S // d, S // SUB
        for r in range(SUB):
            for q in range(SUB):
                tmp[pl.ds(r * m + q, n, stride=SUB), :] = src_ref[(r + SUB * q) * n:(r + SUB * q + 1) * n, :]
        src_ref, d = tmp, SUB
    n = S // d
    for r in range(d):
        idx = (pl.ds(r, n, stride=d), slice(None)) if d > 1 else (slice(None), slice(None))
        v = src_ref[r * n:(r + 1) * n, :]
        if accumulate:
            dst_ref[idx] = dst_ref[idx] + v
        else:
            dst_ref[idx] = v


def _deint_heads(src_ref, dst0, dst1, d, tmp):
    n = S // d
    hm0 = lax.broadcasted_iota(jnp.int32, (n, LANES), 1) < HEAD
    for row0, v in _strided_rows(src_ref, d, tmp):
        dst0[row0:row0 + n, :] = jnp.where(hm0, v, 0.0).astype(BF16)
        dst1[row0:row0 + n, :] = jnp.where(hm0, 0.0, v).astype(BF16)


def _reint_prev(src_ref, dst_ref, d):
    n = S // d
    if n == BLK:
        return
    for r in range(d):
        idx = (pl.ds(r, n - BLK, stride=d), slice(None)) if d > 1 else (slice(0, n - BLK), slice(None))
        dst_ref[idx] = dst_ref[idx] + src_ref[r * n + BLK:(r + 1) * n, :]


def _pair_masks():
    qi = lax.broadcasted_iota(jnp.int32, (BLK, 2 * BLK), 0)
    ki = lax.broadcasted_iota(jnp.int32, (BLK, 2 * BLK), 1) & (BLK - 1)
    return ki <= qi, ki >= qi


def _two(ref0, ref1, st, axis):
    return jnp.concatenate([ref0[pl.ds(st, BLK), :], ref1[pl.ds(st, BLK), :]], axis=axis)


ATT_UNROLL = 8


def _att_fwd(proj, cos, sin, w_out):
    def body(q_ref, k_ref, v_ref, cos_ref, sin_ref, w_ref, att_ref, qr_ref, kr_ref, lse_ref, wbf_ref,
             qd, kd0, kd1, vd0, vd1, od, ld, tmp, on, ln, wbuf, *wsems):
        wg = _WeightGather(w_ref, wbuf, *wsems)
        pl.when(pl.program_id(0) == 0)(wg.start)
        pl.when(pl.program_id(0) == 1)(wg.forward)
        lane = lax.broadcasted_iota(jnp.int32, (S, LANES), 1)
        first = (lane & (HEAD // 2)) == 0
        cos, sin = cos_ref[...], sin_ref[...]
        q = q_ref[...]
        k = k_ref[...]
        qr_ref[...] = (q * cos + _rot_half(q, first) * sin) * (HEAD ** -0.5)
        kr_ref[...] = k * cos + _rot_half(k, first) * sin
        hm0 = lax.broadcasted_iota(jnp.int32, (BLK, LANES), 1) < HEAD
        top = lax.broadcasted_iota(jnp.int32, (2 * BLK, LANES), 0) < BLK
        ones2 = (top == (lax.broadcasted_iota(jnp.int32, (2 * BLK, LANES), 1) < HEAD)).astype(BF16)
        mc2, mp2 = _pair_masks()

        for pi, d in enumerate(PATTERNS):
            nb = S // d // BLK
            _deint(qr_ref, qd, d, tmp)
            _deint_heads(kr_ref, kd0, kd1, d, tmp)
            _deint_heads(v_ref, vd0, vd1, d, tmp)

            def blk(b, carry):
                st = pl.multiple_of(b * BLK, BLK)
                qb = qd[pl.ds(st, BLK), :]
                sc = jnp.where(mc2, _dot_nt(qb, _two(kd0, kd1, st, 0)), NEG)
                mx = sc
                if nb > 1:
                    stp = pl.multiple_of(jnp.maximum(b - 1, 0) * BLK, BLK)
                    mp = jnp.logical_and(mp2, lax.rem(b, nb) != 0)
                    sp = jnp.where(mp, _dot_nt(qb, _two(kd0, kd1, stp, 0)), NEG)
                    mx = jnp.maximum(sc, sp)
                m0 = jnp.max(mx[:, 0:BLK], axis=1, keepdims=True)
                m1 = jnp.max(mx[:, BLK:2 * BLK], axis=1, keepdims=True)
                mf = jnp.concatenate([jnp.broadcast_to(m0, (BLK, BLK)), jnp.broadcast_to(m1, (BLK, BLK))], axis=1)
                o = _dot(jnp.exp(sc - mf).astype(BF16), jnp.concatenate([_two(vd0, vd1, st, 0), ones2], axis=1))
                if nb > 1:
                    o = o + _dot(jnp.exp(sp - mf).astype(BF16), jnp.concatenate([_two(vd0, vd1, stp, 0), ones2], axis=1))
                l = o[:, LANES:2 * LANES]
                od[pl.ds(st, BLK), :] = o[:, 0:LANES] / l
                ld[pl.ds(st, BLK), :] = jnp.where(hm0, m0, m1) + jnp.log(l)
                return carry

            lax.fori_loop(0, S // BLK, blk, 0, unroll=ATT_UNROLL)
            _reint(od, on.at[pi], d, False, tmp)
            _reint(ld, ln.at[pi], d, False, tmp)

        l0, l1, l2 = ln[0], ln[1], ln[2]
        m = jnp.maximum(jnp.maximum(l0, l1), l2)
        e0, e1, e2 = jnp.exp(l0 - m), jnp.exp(l1 - m), jnp.exp(l2 - m)
        den = e0 + e1 + e2
        att_ref[...] = (e0 * on[0] + e1 * on[1] + e2 * on[2]) / den
        lse_ref[...] = m + jnp.log(den)

        @pl.when(pl.program_id(0) == NPAIR - 1)
        def _():
            wg.finish()
            wbf_ref[...] = wbuf[...]

    col = lambda c0: pl.BlockSpec((S, LANES), lambda p: (0, c0 + p))
    out = pl.BlockSpec((S, LANES), lambda p: (0, p))
    tab = pl.BlockSpec((S, LANES), lambda p: (0, 0))
    vm = pl.BlockSpec(memory_space=pltpu.VMEM)
    return pl.pallas_call(
        body, name="att_fwd", grid=(NPAIR,),
        in_specs=[col(QB), col(KB), col(VB), tab, tab, vm],
        out_specs=[out, out, out, out, vm],
        out_shape=[jax.ShapeDtypeStruct((S, R), F32)] * 4 + [jax.ShapeDtypeStruct((NCHIP,) + w_out.shape, BF16)],
        scratch_shapes=[pltpu.VMEM((S, LANES), BF16)] * 5 + [pltpu.VMEM((S, LANES), F32)] * 3
        + [pltpu.VMEM((3, S, LANES), F32)] * 2 + [pltpu.VMEM((NCHIP,) + w_out.shape, BF16)] + _WeightGather.SEMS,
        compiler_params=_cp(("arbitrary",)),
    )(proj, proj, proj, cos, sin, w_out)


def _att_bwd(dproj, d_att, att, lse, qr, kr, proj, cos, sin, gw_out4):
    out_units = [(j, j, 0) for j in range(NCHIP)]

    nblk = S // BLK

    def body(dp_in, do_ref, o_ref, lse_ref, qr_ref, kr_ref, v_ref, cos_ref, sin_ref, gw_ref, dp_ref, gout_ref,
             qd, kd0, kd1, vd0, vd1, dod, kt, packn, packd, dqd, dkcd, dkpd, dvcd, dvpd,
             dqn, dkn, dvn, tmp, rows, trs, pts, dss, stage, sems, gred, *rs_scratch):
        p = pl.program_id(0)
        rs = _ReduceScatter(gw_ref, gred, out_units, *rs_scratch)
        for step, piece in enumerate((rs.start_halves, rs.send_partials, rs.reduce_owned)):
            pl.when(p == step)(piece)

        @pl.when(p == NPAIR - 1)
        def _():
            rs.finish()
            gout_ref[...] = gred[...]

        lane = lax.broadcasted_iota(jnp.int32, (S, LANES), 1)
        hms = lane < HEAD
        prod = do_ref[...] * o_ref[...]
        d0 = jnp.sum(jnp.where(hms, prod, 0.0), axis=1, keepdims=True)
        d1 = jnp.sum(jnp.where(hms, 0.0, prod), axis=1, keepdims=True)
        lse = lse_ref[...]
        quarter = HEAD // 2
        packn[...] = jnp.where(lane < quarter, lse,
                               jnp.where(hms, pltpu.roll(lse, LANES - quarter, 1), jnp.where(lane < 3 * quarter, d0, d1)))
        dqn[...] = jnp.zeros_like(dqn)
        dkn[...] = jnp.zeros_like(dkn)
        dvn[...] = jnp.zeros_like(dvn)
        hm0 = lax.broadcasted_iota(jnp.int32, (BLK, LANES), 1) < HEAD
        key = lax.broadcasted_iota(jnp.int32, (2 * BLK, BLK), 0) & (BLK - 1)
        qry = lax.broadcasted_iota(jnp.int32, (2 * BLK, BLK), 1)
        mct, mpt = key <= qry, key >= qry

        for d in PATTERNS:
            nb = S // d // BLK
            _deint(qr_ref, qd, d, tmp)
            _deint_heads(kr_ref, kd0, kd1, d, tmp)
            _deint_heads(v_ref, vd0, vd1, d, tmp)
            _deint(do_ref, dod, d, tmp)
            _deint(packn, packd, d, tmp)

            sides = (0, 1) if nb > 1 else (0,)

            def probs(b, carry):
                st = pl.multiple_of(b * BLK, BLK)
                kt[b] = _two(kd0, kd1, st, 0).astype(F32).T.astype(BF16)
                trs[b] = packd[pl.ds(st, BLK), :].T
                for j in range(4):
                    rows[b, j:j + 1, :] = trs[b, j * quarter:j * quarter + 1, :]
                qb, dob = qd[pl.ds(st, BLK), :], dod[pl.ds(st, BLK), :]
                both = lambda j: jnp.concatenate([jnp.broadcast_to(rows[b, j:j + 1, :], (BLK, BLK)),
                                                  jnp.broadcast_to(rows[b, j + 1:j + 2, :], (BLK, BLK))], axis=0)
                lbt, dlt = both(0), both(2)
                for sd in sides:
                    stk = pl.multiple_of(jnp.maximum(b - sd, 0) * BLK, BLK)
                    mask = mct if sd == 0 else jnp.logical_and(mpt, lax.rem(b, nb) != 0)
                    k2, v2 = _two(kd0, kd1, stk, 0), _two(vd0, vd1, stk, 0)
                    pt = jnp.where(mask, jnp.exp(_dot_nt(k2, qb) - lbt), 0.0)
                    pts[b, sd] = pt.astype(BF16)
                    dss[b, sd] = (pt * (_dot_nt(v2, dob) - dlt)).astype(BF16)
                return carry

            lax.fori_loop(0, nblk, probs, 0, unroll=2 * ATT_UNROLL)

            def prods(b, carry):
                st = pl.multiple_of(b * BLK, BLK)
                qb, dob = qd[pl.ds(st, BLK), :], dod[pl.ds(st, BLK), :]
                dq_t = None
                for sd in sides:
                    dst, ptb = dss[b, sd], pts[b, sd]
                    rk, rv = _dot(dst, qb), _dot(ptb, dob)
                    dqs = _dot(kt[jnp.maximum(b - sd, 0)], dst)
                    dq_t = dqs if dq_t is None else dq_t + dqs
                    dk, dv = (dkcd, dvcd) if sd == 0 else (dkpd, dvpd)
                    dk[pl.ds(st, BLK), :] = jnp.where(hm0, rk[0:BLK], rk[BLK:2 * BLK])
                    dv[pl.ds(st, BLK), :] = jnp.where(hm0, rv[0:BLK], rv[BLK:2 * BLK])
                dqd[pl.ds(st, BLK), :] = dq_t.T
                return carry

            lax.fori_loop(0, nblk, prods, 0, unroll=2 * ATT_UNROLL)
            _reint(dqd, dqn, d, True, tmp)
            _reint(dkcd, dkn, d, True, tmp)
            _reint(dvcd, dvn, d, True, tmp)
            _reint_prev(dkpd, dkn, d)
            _reint_prev(dvpd, dvn, d)

        lane = lax.broadcasted_iota(jnp.int32, (S, LANES), 1)
        first = (lane & (HEAD // 2)) == 0
        cos, sin = cos_ref[...], sin_ref[...]
        dq = dqn[...] * (HEAD ** -0.5)
        dk = dkn[...]
        stage[0] = (dq * cos - _rot_half(dq, first) * sin).astype(BF16)
        stage[1] = (dk * cos - _rot_half(dk, first) * sin).astype(BF16)
        stage[2] = dvn[...].astype(BF16)
        copies = [pltpu.make_async_copy(stage.at[j], dp_ref.at[:, pl.ds((2 + j) * R + p * LANES, LANES)], sems.at[j])
                  for j in range(3)]
        for cp in copies:
            cp.start()
        for cp in copies:
            cp.wait()

    blk = pl.BlockSpec((S, LANES), lambda p: (0, p))
    tab = pl.BlockSpec((S, LANES), lambda p: (0, 0))
    vm = pl.BlockSpec(memory_space=pltpu.VMEM)
    _, orows, ocols = gw_out4.shape
    return pl.pallas_call(
        body, name="att_bwd", grid=(NPAIR,),
        in_specs=[pl.BlockSpec(memory_space=pl.ANY), blk, blk, blk, blk, blk,
                  pl.BlockSpec((S, LANES), lambda p: (0, VB + p)), tab, tab, vm],
        out_specs=[pl.BlockSpec(memory_space=pl.ANY), vm],
        out_shape=[jax.ShapeDtypeStruct((S, E), BF16), jax.ShapeDtypeStruct((orows, ocols), F32)],
        scratch_shapes=[pltpu.VMEM((S, LANES), BF16)] * 6 + [pltpu.VMEM((nblk, LANES, 2 * BLK), BF16)]
        + [pltpu.VMEM((S, LANES), F32)] * 11
        + [pltpu.VMEM((nblk, 8, BLK), F32), pltpu.VMEM((nblk, LANES, BLK), F32)]
        + [pltpu.VMEM((nblk, 2, 2 * BLK, BLK), BF16)] * 2
        + [pltpu.VMEM((3, S, LANES), BF16), pltpu.SemaphoreType.DMA((3,)), pltpu.VMEM((orows, ocols), F32)]
        + _ReduceScatter.scratch(NCHIP, orows, ocols, 1),
        input_output_aliases={0: 0},
        compiler_params=_cp(("arbitrary",)),
    )(dproj, d_att, att, lse, qr, kr, proj, cos, sin, gw_out4)


def _out_fwd_bwd(ya, att, proj, w_out_bf, x, target, mod, norm_post, norm_att):
    ts = 512

    def body(ya_ref, att_ref, gb_ref, w_ref, x_ref, t_ref, mod_ref, npost_ref, natt_ref,
             gx_ref, dya_ref, datt_ref, dgb_ref, gw_ref, acc_ref):
        i = pl.program_id(0)

        @pl.when(i == 0)
        def _():
            gw_ref[...] = jnp.zeros_like(gw_ref)
            acc_ref[...] = jnp.zeros_like(acc_ref)

        gate = mod_ref[:, 2 * D:3 * D]
        att = att_ref[...]
        gb = gb_ref[...]
        sg = _sigmoid(gb)
        silu = gb * sg
        ybp = att * silu
        yb, ybn, rstd_b = _rms_fwd(ybp, natt_ref[...])
        cat = jnp.concatenate([ya_ref[...], yb.astype(BF16)], axis=1)
        mix = _dot(cat, w_ref[...])
        rn, mn, rstd_m = _rms_fwd(mix, npost_ref[...])
        err = x_ref[...] + gate * rn - t_ref[...]
        dy = err * (1.0 / D)
        gx_ref[...] = dy
        dmix, dnpost = _rms_bwd(dy * gate, mn, rstd_m, npost_ref[...])
        dmb = dmix.astype(BF16)
        gw_ref[...] += _dot_tn(cat, dmb)
        dcat = _dot_nt(dmb, w_ref[...])
        dya_ref[...] = dcat[:, 0:R]
        dybp, dnatt = _rms_bwd(dcat[:, R:2 * R], ybn, rstd_b, natt_ref[...])
        datt_ref[...] = dybp * silu
        dgb_ref[...] = (dybp * att * (sg * (1.0 + gb * (1.0 - sg)))).astype(BF16)
        acc_ref[0:1, :] += jnp.sum(dy * rn, axis=0, keepdims=True)
        acc_ref[1:2, :] += dnpost
        acc_ref[2:3, 0:R] += dnatt
        acc_ref[3:4, :] += jnp.sum(jnp.sum(err * err, axis=1, keepdims=True), axis=0, keepdims=True)

    tile = lambda w: pl.BlockSpec((ts, w), lambda i: (i, 0))
    c0 = lambda shape: pl.BlockSpec(shape, lambda i: (0, 0))
    return pl.pallas_call(
        body, name="out_fwd_bwd", grid=(S // ts,),
        in_specs=[tile(R), tile(R), pl.BlockSpec((ts, R), lambda i: (i, 5)), c0((D, D)), tile(D), tile(D),
                  c0((1, 3 * D)), c0((1, D)), c0((1, R))],
        out_specs=[tile(D), tile(R), tile(R), pl.BlockSpec((ts, R), lambda i: (i, 5)), c0((D, D)), c0((8, D))],
        out_shape=[jax.ShapeDtypeStruct((S, D), F32), jax.ShapeDtypeStruct((S, R), F32),
                   jax.ShapeDtypeStruct((S, R), F32), jax.ShapeDtypeStruct((S, E), BF16),
                   jax.ShapeDtypeStruct((D, D), F32), jax.ShapeDtypeStruct((8, D), F32)],
        compiler_params=_cp(("arbitrary",)),
    )(ya, att, proj, w_out_bf, x, target, mod, norm_post, norm_att)


UC = 256
UPC = EC // UC


NU = E // UC


def _unit_of_step(i):
    return (i % NCHIP) * UPC + i // NCHIP


def _in_proj_bwd(ht, dproj, w_in_bf, x, gx1, mod, norm_pre, smalls):
    ts = 256
    nt = S // ts
    half = D // 2
    units = [_unit_of_step(k) for k in range(NU)]
    owners = [u // UPC for u in units]
    ns = len(smalls)

    def body(*refs):
        (ht_ref, dpu_ref, dp_ref, w_hbm, x_ref, gx1_ref, mod_ref, np_ref), refs = refs[:8], refs[8:]
        small_in, refs = refs[:ns], refs[ns:]
        (gx_ref, gin_ref), refs = refs[:2], refs[2:]
        small_out, (acc_out,), refs = refs[:ns], refs[ns:ns + 1], refs[ns + 1:]
        mine, sib, tmp, stage, got, red, acc_ref, hs, hr, ps, pr, bs, br = refs[:13]
        early = _SmallGather(small_in, small_out, *refs[13:16])
        late = _SmallGather([acc_ref], [acc_out], *refs[16:19])
        w_ref, w_sem = refs[19:21]
        i = pl.program_id(0)
        w_copy = pltpu.make_async_copy(w_hbm, w_ref, w_sem)
        pl.when(i == 0)(w_copy.start)
        pl.when(i == NU)(w_copy.wait)
        xx, yy, c = _me()
        ci = 2 * xx + yy
        r0 = pl.multiple_of(c * half, half)
        r1 = pl.multiple_of((1 - c) * half, half)
        pl.when(i == 0)(early.start)
        pl.when(i == NU)(early.forward)

        def exch(k):
            return _remote(tmp.at[k % 2], sib.at[k], hs.at[k], hr.at[k], 1)

        def partial(k, sender):
            return pltpu.make_async_remote_copy(
                src_ref=stage.at[k], dst_ref=got.at[units[k] % UPC, sender], send_sem=ps.at[k],
                recv_sem=pr.at[k, sender], device_id=(owners[k] // 2, owners[k] % 2, c), device_id_type=MESH)

        def back(k, start):
            off = (units[k] % UPC) * UC
            blk = red.at[pl.ds(start, half), off:off + UC]
            return _remote(blk, blk, bs.at[k], br.at[k], 1)

        for k in range(NU + 1):
            @pl.when(i == k)
            def _():
                if k < NU:
                    if k >= 2:
                        exch(k - 2).wait_send()
                    dpu = dpu_ref[...]
                    tmp[k % 2] = _dot(ht_ref[pl.ds(r1, half), :], dpu)
                    exch(k).start()
                    mine[k] = _dot(ht_ref[pl.ds(r0, half), :], dpu)
                if k >= 1:
                    exch(k - 1).wait_recv()
                    mine[k - 1] += sib[k - 1]

                    @pl.when(ci != owners[k - 1])
                    def _():
                        stage[k - 1] = mine[k - 1].astype(BF16)
                        partial(k - 1, ci).start()

        @pl.when(i == NU)
        def _():
            acc_ref[...] = jnp.zeros_like(acc_ref)

        @pl.when(i >= NU)
        def _():
            dh = sum(_dot_nt(dp_ref[:, j * EC:(j + 1) * EC], w_ref[j]) for j in range(NCHIP))
            hp, xn, rstd = _rms_fwd(x_ref[...], np_ref[...])
            dx, dnp = _rms_bwd(dh * (1.0 + mod_ref[:, D:2 * D]), xn, rstd, np_ref[...])
            gx_ref[...] = gx1_ref[...] + dx
            acc_ref[0:1, :] += jnp.sum(dh, axis=0, keepdims=True)
            acc_ref[1:2, :] += jnp.sum(dh * hp, axis=0, keepdims=True)
            acc_ref[2:3, :] += dnp

        for t in range(UPC):
            @pl.when(i == NU + 1 + 2 * t)
            def _():
                for k in range(NCHIP * t, NCHIP * (t + 1)):
                    @pl.when(ci == owners[k])
                    def _():
                        off = (units[k] % UPC) * UC
                        red[pl.ds(r0, half), off:off + UC] = mine[k]
                        for s in range(NCHIP):
                            if s != owners[k]:
                                partial(k, s).wait_recv()
                                red[pl.ds(r0, half), off:off + UC] += got[units[k] % UPC, s].astype(F32)
                        back(k, r0).start()

        @pl.when(i == NU + nt - 1)
        def _():
            late.start()
            exch(NU - 2).wait_send()
            exch(NU - 1).wait_send()
            for k in range(NU):
                @pl.when(ci == owners[k])
                def _():
                    back(k, r1).wait_recv()
                    back(k, r0).wait_send()

                @pl.when(ci != owners[k])
                def _():
                    partial(k, ci).wait_send()
            gin_ref[...] = red[...]
            early.finish()
            late.forward()
            late.finish()

    tile = lambda w: pl.BlockSpec((ts, w), lambda i: (jnp.maximum(i - NU, 0), 0))
    c0 = lambda shape: pl.BlockSpec(shape, lambda i: (0, 0))
    vm = pl.BlockSpec(memory_space=pltpu.VMEM)
    hbm = pl.BlockSpec(memory_space=pl.ANY)
    gathered = [jax.ShapeDtypeStruct((NDEV,) + a.shape, a.dtype) for a in smalls] + [jax.ShapeDtypeStruct((NDEV, 8, D), F32)]
    return pl.pallas_call(
        body, name="in_proj_bwd", grid=(NU + nt,),
        in_specs=[vm, pl.BlockSpec((S, UC), lambda i: (0, _unit_of_step(jnp.minimum(i, NU - 1)))), tile(E),
                  hbm, tile(D), tile(D), c0((1, 3 * D)), c0((1, D))] + [vm] * ns,
        out_specs=[tile(D), vm] + [hbm] * (ns + 1),
        out_shape=[jax.ShapeDtypeStruct((S, D), F32), jax.ShapeDtypeStruct((D, EC), F32)] + gathered,
        scratch_shapes=[pltpu.VMEM((NU, half, UC), F32), pltpu.VMEM((NU, half, UC), F32),
                        pltpu.VMEM((2, half, UC), F32), pltpu.VMEM((NU, half, UC), BF16),
                        pltpu.VMEM((UPC, NCHIP, half, UC), BF16), pltpu.VMEM((D, EC), F32), pltpu.VMEM((8, D), F32),
                        pltpu.SemaphoreType.DMA((NU,)), pltpu.SemaphoreType.DMA((NU,)),
                        pltpu.SemaphoreType.DMA((NU,)), pltpu.SemaphoreType.DMA((NU, NCHIP)),
                        pltpu.SemaphoreType.DMA((NU,)), pltpu.SemaphoreType.DMA((NU,))]
        + _SmallGather.sems(ns) + _SmallGather.sems(1)
        + [pltpu.VMEM((NCHIP, D, EC), BF16), pltpu.SemaphoreType.DMA],
        compiler_params=_cp(("arbitrary",)),
    )(ht, dproj, dproj, w_in_bf, x, gx1, mod, norm_pre, *smalls)


def _local_step(x, cos, sin, target, mod, w_in_bf, proj, ht, w_out, conv_w, p):
    rec_p = (conv_w, p["conv_b"], p["w_rg_a"], p["b_rg_a"], p["w_rg_x"], p["b_rg_x"], p["lru_lambda"], p["norm_rec"])
    h_all, ya = _rec_fwd(proj, *rec_p)
    att, qr, kr, lse, w_out_bf = _att_fwd(proj, cos, sin, w_out)
    gx1, d_ya, d_att, dproj, gw_out, acc_o = _out_fwd_bwd(ya, att, proj, w_out_bf.reshape(D, D), x, target, mod,
                                                           p["norm_post"], p["norm_att"])
    dproj, g_out = _att_bwd(dproj, d_att, att, lse, qr, kr, proj, cos, sin, gw_out.reshape(NCHIP, D // NCHIP, D))
    dproj, dwa, dwx, sm = _rec_bwd(dproj, d_ya, proj, h_all, *rec_p)
    grad_x, g_in, *gathered = _in_proj_bwd(ht, dproj, w_in_bf, x, gx1, mod, p["norm_pre"], [acc_o, sm, dwa, dwx])
    return grad_x, g_in, g_out, gathered


def _me():
    return lax.axis_index("x"), lax.axis_index("y"), lax.axis_index("c")


def _flip(v, bit):
    return 1 - v if bit else v


def _peer(rel):
    x, y, c = _me()
    return (_flip(x, rel & 4), _flip(y, rel & 2), _flip(c, rel & 1))


def _remote(src, dst, send_sem, recv_sem, rel):
    return pltpu.make_async_remote_copy(src_ref=src, dst_ref=dst, send_sem=send_sem, recv_sem=recv_sem,
                                        device_id=_peer(rel), device_id_type=MESH)


class _WeightGather:
    SEMS = [pltpu.SemaphoreType.DMA((NCHIP - 1,))] * 4

    def __init__(self, w_ref, out_ref, send_sems, recv_sems, fsend_sems, frecv_sems):
        x, y, c = _me()
        self.w, self.out, self.ci = w_ref, out_ref, 2 * x + y
        self.half = w_ref.shape[0] // 2
        self.r0 = pl.multiple_of(c * self.half, self.half)
        self.r1 = pl.multiple_of((1 - c) * self.half, self.half)
        self.sems = (send_sems, recv_sems, fsend_sems, frecv_sems)

    def _ici(self, chip, k):
        blk = self.out.at[chip, pl.ds(self.r0, self.half), :]
        return _remote(blk, blk, self.sems[0].at[k - 1], self.sems[1].at[k - 1], 2 * k)

    def _d2d(self, chip, start, k):
        blk = self.out.at[chip, pl.ds(start, self.half), :]
        return _remote(blk, blk, self.sems[2].at[k - 1], self.sems[3].at[k - 1], 1)

    def start(self, diagonal=True):
        self.out[self.ci] = self.w[...].astype(BF16)
        for k in range(1, NCHIP if diagonal else NCHIP - 1):
            self._ici(self.ci, k).start()

    def _relay(self, chip, piece, k):
        q = self.half // 2
        blk = self.out.at[chip, pl.ds(self.r0 + piece * q, q), :]
        return _remote(blk, blk, self.relay_sems[0].at[piece], self.relay_sems[1].at[piece], 2 * k)

    def neighbours_landed(self, relay_send_sems, relay_recv_sems):
        self.relay_sems = (relay_send_sems, relay_recv_sems)
        for k in (1, 2):
            self._ici(self.ci ^ k, k).wait_recv()
        self._relay(self.ci ^ 2, 0, 1).start()
        self._relay(self.ci ^ 1, 1, 2).start()
        for k in (1, 2):
            self._d2d(self.ci ^ k, self.r0, k).start()

    def sibling_landed(self, k):
        self._d2d(self.ci ^ k, self.r1, k).wait_recv()

    def diagonal_landed(self):
        for piece, k in ((0, 1), (1, 2)):
            self._relay(self.ci ^ 3, piece, k).wait_recv()
        self._d2d(self.ci ^ 3, self.r0, 3).start()
        self._d2d(self.ci ^ 3, self.r1, 3).wait_recv()

    def finish_relayed(self):
        for k in (1, 2):
            self._ici(self.ci, k).wait_send()
        self._relay(self.ci ^ 2, 0, 1).wait_send()
        self._relay(self.ci ^ 1, 1, 2).wait_send()
        for k in range(1, NCHIP):
            self._d2d(self.ci ^ k, self.r0, k).wait_send()

    def forward(self):
        for k in range(1, NCHIP):
            self._ici(self.ci ^ k, k).wait_recv()
            self._d2d(self.ci ^ k, self.r0, k).start()

    def finish(self):
        for k in range(1, NCHIP):
            self._d2d(self.ci ^ k, self.r1, k).wait_recv()
        self.finish_sends()

    def finish_sends(self):
        for k in range(1, NCHIP):
            self._ici(self.ci, k).wait_send()
            self._d2d(self.ci ^ k, self.r0, k).wait_send()


class _SmallGather:
    @staticmethod
    def sems(n):
        return [pltpu.SemaphoreType.DMA((n, 7)), pltpu.SemaphoreType.DMA((n, 7)), pltpu.SemaphoreType.DMA((n,))]

    def __init__(self, srcs, outs, send_sems, recv_sems, local_sems):
        x, y, c = _me()
        self.srcs, self.outs = list(srcs), list(outs)
        self.ss, self.rs, self.ls = send_sems, recv_sems, local_sems
        self.ci, self.c = 2 * x + y, c
        self.me = 2 * self.ci + c

    def _own(self, a, slot, rel):
        return _remote(self.srcs[a], self.outs[a].at[self.me], self.ss.at[a, slot], self.rs.at[a, slot], rel)

    def _block(self, a, idx, slot, rel):
        blk = self.outs[a].at[idx]
        return _remote(blk, blk, self.ss.at[a, slot], self.rs.at[a, slot], rel)

    def _local(self, a):
        return pltpu.make_async_copy(self.srcs[a], self.outs[a].at[self.me], self.ls.at[a])

    def start(self):
        for a in range(len(self.srcs)):
            self._local(a).start()
            self._own(a, 0, 1).start()
            for k in range(1, NCHIP):
                self._own(a, k, 2 * k).start()

    def forward(self):
        for a in range(len(self.srcs)):
            for k in range(1, NCHIP):
                idx = 2 * (self.ci ^ k) + self.c
                self._block(a, idx, k, 2 * k).wait_recv()
                self._block(a, idx, 3 + k, 1).start()

    def finish(self):
        for a in range(len(self.srcs)):
            self._block(a, 2 * self.ci + 1 - self.c, 0, 1).wait_recv()
            for k in range(1, NCHIP):
                self._block(a, 2 * (self.ci ^ k) + 1 - self.c, 3 + k, 1).wait_recv()
            self._own(a, 0, 1).wait_send()
            for k in range(1, NCHIP):
                self._own(a, k, 2 * k).wait_send()
                self._block(a, 2 * (self.ci ^ k) + self.c, 3 + k, 1).wait_send()
            self._local(a).wait()


def _start_in_proj(c, conv_w, w_ada, b_ada, w_in, pos, x, norm_pre, order):
    ts = 512
    nt = S // ts
    wc = D + conv_w.size

    def body(order_ref, c_ref, cw_ref, wada_ref, b_ref, win_ref, pos_ref, freq_ref, x_ref, np_ref,
             g0_ref, conv_ref, mod_ref, wbf_ref, cos_ref, sin_ref, proj_ref, ht_ref,
             crow_ref, g0s, modp, modb, wbuf, hb_all, cs, cr, ms, mr, ws, wr, fs, fr, local_sems, ys, yr, osem):
        s, t = pl.program_id(0), pl.program_id(1)
        x, y, c = _me()
        ci = 2 * x + y
        me = 2 * ci + c
        wg = _WeightGather(win_ref, wbuf, ws, wr, fs, fr)
        cw = R // NCHIP

        @pl.when(jnp.logical_and(s == 0, t == 0))
        def _():
            wg.start(diagonal=False)
            crow_ref[:, 0:D] = c_ref[...]
            for k in range(4):
                crow_ref[:, D + k * cw:D + (k + 1) * cw] = cw_ref[k:k + 1, :]
            mine = pltpu.make_async_copy(crow_ref, g0s.at[pl.ds(me, 1), :], local_sems.at[0])
            mine.start()
            csend = [_remote(crow_ref, g0s.at[pl.ds(me, 1), :], cs.at[r - 1], cr.at[r - 1], r) for r in range(1, NDEV)]
            for cp in csend:
                cp.start()
            cos_ref[...], sin_ref[...] = _cos_sin(pos_ref, freq_ref)
            for r in range(1, NDEV):
                px, py, pc = _peer(r)
                _remote(crow_ref, g0s.at[pl.ds(4 * px + 2 * py + pc, 1), :], cs.at[r - 1], cr.at[r - 1], r).wait_recv()
            mine.wait()
            cv = g0s[:, 0:D]
            sc = cv * _sigmoid(cv)
            scb = jnp.concatenate([sc, jnp.zeros_like(sc)], axis=0).astype(BF16)
            b_cols = sum(jnp.where(ci == j, b_ref[:, j * EC:(j + 1) * EC], 0.0) for j in range(NCHIP))
            modp[...] = _dot(scb, wada_ref[...].astype(BF16))[0:NDEV, :] + b_cols
            own = pltpu.make_async_copy(modp.at[pl.ds(me, 1), :], modb.at[ci], local_sems.at[1])
            own.start()
            msend = []
            for k in range(1, NCHIP):
                cp = _remote(modp.at[pl.ds(2 * (ci ^ k) + c, 1), :], modb.at[ci], ms.at[k - 1], mr.at[k - 1], 2 * k)
                cp.start()
                msend.append(cp)
            for k in range(1, NCHIP):
                _remote(modp.at[pl.ds(me, 1), :], modb.at[ci ^ k], ms.at[k - 1], mr.at[k - 1], 2 * k).wait_recv()
            own.wait()
            for j in range(NCHIP):
                mod_ref[:, j * EC:(j + 1) * EC] = modb[j]
            for cp in csend + msend:
                cp.wait_send()
            g0_ref[...] = g0s[...]
            for j in range(NCHIP):
                for k in range(4):
                    conv_ref[k:k + 1, j * cw:(j + 1) * cw] = g0s[2 * j:2 * j + 1, D + k * cw:D + (k + 1) * cw]

        def keep(k):
            return pltpu.make_async_copy(wbuf.at[ci ^ k], wbf_ref.at[ci ^ k], osem.at[k])

        @pl.when(jnp.logical_and(s == 1, t == 0))
        def _():
            keep(0).start()
            wg.neighbours_landed(ys, yr)
            wg.sibling_landed(1)
            keep(1).start()

        @pl.when(jnp.logical_and(s == 2, t == 0))
        def _():
            wg.sibling_landed(2)
            keep(2).start()

        @pl.when(jnp.logical_and(s == 3, t == 0))
        def _():
            wg.relay_sems = (ys, yr)
            wg.diagonal_landed()
            keep(3).start()

        rows = pl.ds(pl.multiple_of(t * ts, ts), ts)

        @pl.when(s == 0)
        def _():
            hp, _, _ = _rms_fwd(x_ref[...], np_ref[...])
            h = hp * (1.0 + mod_ref[:, D:2 * D]) + mod_ref[:, 0:D]
            hb_all[rows, :] = h.astype(BF16)
            ht_ref[...] = h.T.astype(BF16)

        proj_ref[...] = _dot(hb_all[rows, :], wbuf[ci ^ s])

        @pl.when(jnp.logical_and(s == NCHIP - 1, t == nt - 1))
        def _():
            wg.relay_sems = (ys, yr)
            wg.finish_relayed()
            for k in range(NCHIP):
                keep(k).wait()

    vm = pl.BlockSpec(memory_space=pltpu.VMEM)
    first_pass = lambda s, t: jnp.where(s == 0, t, nt - 1)
    grid_spec = pltpu.PrefetchScalarGridSpec(
        num_scalar_prefetch=1, grid=(NCHIP, nt),
        in_specs=[vm, vm, vm, vm, vm, vm, vm, pl.BlockSpec((ts, D), lambda s, t, o: (first_pass(s, t), 0)),
                  pl.BlockSpec((1, D), lambda s, t, o: (0, 0))],
        out_specs=[vm, vm, vm, pl.BlockSpec(memory_space=pl.ANY), vm, vm,
                   pl.BlockSpec((ts, EC), lambda s, t, o: (t, o[s])),
                   pl.BlockSpec((D, ts), lambda s, t, o: (0, first_pass(s, t)))],
        scratch_shapes=[pltpu.VMEM((1, wc), F32),
                        pltpu.VMEM((NDEV, wc), F32), pltpu.VMEM((NDEV, EC), F32), pltpu.VMEM((NCHIP, 1, EC), F32),
                        pltpu.VMEM((NCHIP, D, EC), BF16), pltpu.VMEM((S, D), BF16),
                        pltpu.SemaphoreType.DMA((NDEV - 1,)), pltpu.SemaphoreType.DMA((NDEV - 1,)),
                        pltpu.SemaphoreType.DMA((NCHIP - 1,)), pltpu.SemaphoreType.DMA((NCHIP - 1,))]
        + _WeightGather.SEMS + [pltpu.SemaphoreType.DMA((2,))] * 3 + [pltpu.SemaphoreType.DMA((NCHIP,))])
    return pl.pallas_call(
        body, name="start_in_proj", grid_spec=grid_spec,
        out_shape=[jax.ShapeDtypeStruct((NDEV, wc), F32), jax.ShapeDtypeStruct((4, R), F32),
                   jax.ShapeDtypeStruct((1, 3 * D), F32),
                   jax.ShapeDtypeStruct((NCHIP, D, EC), BF16), jax.ShapeDtypeStruct((S, LANES), F32),
                   jax.ShapeDtypeStruct((S, LANES), F32), jax.ShapeDtypeStruct((S, E), F32),
                   jax.ShapeDtypeStruct((D, S), BF16)],
        compiler_params=_cp(("arbitrary", "arbitrary")),
    )(order, c, conv_w, w_ada, b_ada, w_in, pos, _rope_freq(), x, norm_pre)


class _ReduceScatter:
    @staticmethod
    def scratch(n_units, rows, ucols, max_owned):
        half = rows // 2
        return [pltpu.VMEM((n_units, half, ucols), F32), pltpu.VMEM((n_units, half, ucols), BF16),
                pltpu.VMEM((max_owned, NCHIP, half, ucols), BF16),
                pltpu.SemaphoreType.DMA((2,)), pltpu.SemaphoreType.DMA((n_units,)),
                pltpu.SemaphoreType.DMA((n_units, NCHIP)), pltpu.SemaphoreType.DMA((n_units,)),
                pltpu.SemaphoreType.DMA((n_units,))]

    def __init__(self, g_ref, out_ref, units, sib, stage, got, sem1, send2, recv2, send3, recv3):
        x, y, c = _me()
        self.c, self.ci = c, 2 * x + y
        self.g, self.out, self.units = g_ref, out_ref, units
        self.sib, self.stage, self.got = sib, stage, got
        self.sem1, self.send2, self.recv2, self.send3, self.recv3 = sem1, send2, recv2, send3, recv3
        self.half = g_ref.shape[1] // 2
        self.ucols = g_ref.shape[2]
        self.r0 = pl.multiple_of(c * self.half, self.half)
        self.r1 = pl.multiple_of((1 - c) * self.half, self.half)
        self.slot0 = units[0][0]
        assert [u[0] for u in units] == list(range(self.slot0, self.slot0 + len(units)))
        seen = {}
        self.local = []
        for _, owner, _ in units:
            self.local.append(seen.get(owner, 0))
            seen[owner] = seen.get(owner, 0) + 1

    def _halves(self):
        n = len(self.units)
        return _remote(self.g.at[pl.ds(self.slot0, n), pl.ds(self.r1, self.half), :], self.sib,
                       self.sem1.at[0], self.sem1.at[1], 1)

    def _partial(self, i, sender):
        _, owner, _ = self.units[i]
        return pltpu.make_async_remote_copy(
            src_ref=self.stage.at[i], dst_ref=self.got.at[self.local[i], sender],
            send_sem=self.send2.at[i], recv_sem=self.recv2.at[i, sender],
            device_id=(owner // 2, owner % 2, self.c), device_id_type=MESH)

    def _back(self, i, start):
        off = self.units[i][2]
        blk = self.out.at[pl.ds(start, self.half), off:off + self.ucols]
        return _remote(blk, blk, self.send3.at[i], self.recv3.at[i], 1)

    def start_halves(self):
        self._halves().start()

    def send_partials(self):
        self._halves().wait_recv()
        for i, (slot, owner, _) in enumerate(self.units):
            @pl.when(self.ci != owner)
            def _():
                self.stage[i] = (self.g[slot, pl.ds(self.r0, self.half), :] + self.sib[i]).astype(BF16)
                self._partial(i, self.ci).start()

    def reduce_owned(self):
        for i, (slot, owner, off) in enumerate(self.units):
            @pl.when(self.ci == owner)
            def _():
                rows, cols = pl.ds(self.r0, self.half), slice(off, off + self.ucols)
                self.out[rows, cols] = self.g[slot, pl.ds(self.r0, self.half), :] + self.sib[i]
                for s in range(NCHIP):
                    if s != owner:
                        self._partial(i, s).wait_recv()
                        self.out[rows, cols] += self.got[self.local[i], s].astype(F32)
                self._back(i, self.r0).start()

    def finish(self):
        self._halves().wait_send()
        for i, (_, owner, _) in enumerate(self.units):
            @pl.when(self.ci == owner)
            def _():
                self._back(i, self.r1).wait_recv()
                self._back(i, self.r0).wait_send()

            @pl.when(self.ci != owner)
            def _():
                self._partial(i, self.ci).wait_send()


def _silu_rows(c_ref):
    cv = c_ref[:, 0:D]
    sc = cv * _sigmoid(cv)
    return jnp.concatenate([sc, jnp.zeros_like(sc)], axis=0).astype(BF16)


def _adamw(groups):
    steps = 4
    specs = [pl.BlockSpec((w.shape[0] // steps, w.shape[1]), lambda i: (i, 0)) for w, _, _, _ in groups]

    def body(*refs):
        ins, outs = refs[:4 * len(groups)], refs[4 * len(groups):]
        for j in range(len(groups)):
            w_ref, g_ref, m_ref, v_ref = ins[4 * j:4 * j + 4]
            d_ref, nm_ref, nv_ref = outs[3 * j:3 * j + 3]
            d_ref[...], nm_ref[...], nv_ref[...] = _adamw_values(w_ref[...], g_ref[...], m_ref[...], v_ref[...])

    res = pl.pallas_call(
        body, name="adamw_big", grid=(steps,),
        in_specs=[s for s in specs for _ in range(4)], out_specs=[s for s in specs for _ in range(3)],
        out_shape=[jax.ShapeDtypeStruct(w.shape, F32) for w, _, _, _ in groups for _ in range(3)],
        compiler_params=_cp(("parallel",)),
    )(*[a for grp in groups for a in grp])
    return [res[3 * j:3 * j + 3] for j in range(len(groups))]


def _adamw_values(w, g, m, v):
    nm = B1 * m + (1.0 - B1) * g
    nv = B2 * v + (1.0 - B2) * (g * g)
    m_hat = nm / (1.0 - B1 ** STEP)
    v_hat = nv / (1.0 - B2 ** STEP)
    return (-LR) * (m_hat / (jnp.sqrt(v_hat) + ADAM_EPS) + WD * w), nm, nv


NB = R // HEAD
SMALL = (("b_ada", (1, 3 * D)), ("norm_pre", (1, D)), ("norm_post", (1, D)), ("conv_w", (4, R // NCHIP)),
         ("conv_b", (1, R)), ("w_rg_a", (NB, HEAD, HEAD)), ("b_rg_a", (1, R)), ("w_rg_x", (NB, HEAD, HEAD)),
         ("b_rg_x", (1, R)), ("lru_lambda", (1, R)), ("norm_rec", (1, R)), ("norm_att", (1, R)))


def _small_update(ao8, sm8, dwa8, dwx8, ai8, cg, params):
    n = len(SMALL)

    def body(ao_ref, sm_ref, dwa_ref, dwx_ref, ai_ref, cg_ref, *refs):
        pin, pout, (gada_ref, loss_ref, dmod) = refs[:3 * n], refs[3 * n:7 * n], refs[7 * n:]
        xx, yy, _ = _me()
        ci = 2 * xx + yy

        def total(ref, *idx):
            acc = ref[(0,) + idx].astype(F32)
            for d in range(1, NDEV):
                acc = acc + ref[(d,) + idx].astype(F32)
            return acc

        row = lambda ref, r, lanes=slice(None): total(ref, slice(r, r + 1), lanes)
        mine = lambda parts: sum(jnp.where(ci == j, part, 0.0) for j, part in enumerate(parts))
        cw = R // NCHIP
        grads = {
            "b_ada": [jnp.concatenate([row(ai_ref, 0), row(ai_ref, 1), row(ao_ref, 0)], axis=1)],
            "norm_pre": [row(ai_ref, 2)], "norm_post": [row(ao_ref, 1)],
            "conv_w": [mine([row(sm_ref, 8 + r, slice(j * cw, (j + 1) * cw)) for j in range(NCHIP)]) for r in range(4)],
            "conv_b": [row(sm_ref, 4)], "b_rg_a": [row(sm_ref, 0)], "b_rg_x": [row(sm_ref, 1)],
            "lru_lambda": [row(sm_ref, 2)], "norm_rec": [row(sm_ref, 3)], "norm_att": [row(ao_ref, 2, slice(0, R))],
            "w_rg_a": [total(dwa_ref, h) for h in range(NB)], "w_rg_x": [total(dwx_ref, h) for h in range(NB)],
        }
        loss_ref[...] = row(ao_ref, 3, slice(0, LANES)) * (0.5 / D)
        for k, (name, shape) in enumerate(SMALL):
            w_ref, m_ref, v_ref = pin[3 * k:3 * k + 3]
            outs = pout[4 * k:4 * k + 4]
            for r, g in enumerate(grads[name]):
                at = (slice(None),) if len(grads[name]) == 1 else ((r,) if len(shape) == 3 else (slice(r, r + 1),))
                res = (g,) + _adamw_values(w_ref[at], g, m_ref[at], v_ref[at])
                for o_ref, val in zip(outs, res):
                    o_ref[at] = val
        for d in range(NDEV):
            dmod[d:d + 1, :] = jnp.concatenate([ai_ref[d, 0:1, :], ai_ref[d, 1:2, :], ao_ref[d, 0:1, :]], axis=1)
        cols = mine([dmod[:, j * EC:(j + 1) * EC] for j in range(NCHIP)])
        colsb = jnp.concatenate([cols, jnp.zeros_like(cols)], axis=0).astype(BF16)
        gada_ref[...] = _dot_tn(_silu_rows(cg_ref), colsb)

    shapes = [jax.ShapeDtypeStruct(s, F32) for _, s in SMALL]
    outs = pl.pallas_call(
        body, name="small_update",
        out_shape=[s for s in shapes for _ in range(4)] + [jax.ShapeDtypeStruct((D, EC), F32),
                                                           jax.ShapeDtypeStruct((1, LANES), F32)],
        scratch_shapes=[pltpu.VMEM((NDEV, 3 * D), F32)],
        compiler_params=_cp(),
    )(ao8, sm8, dwa8, dwx8, ai8, cg, *params)
    return outs[:4 * n], outs[4 * n], outs[4 * n + 1]


BIG = ("w_ada", "w_in", "w_out")
WEIGHTS = ("w_ada", "b_ada", "norm_pre", "norm_post", "w_in", "conv_w", "conv_b", "w_rg_a", "b_rg_a", "w_rg_x",
           "b_rg_x", "lru_lambda", "norm_rec", "norm_att", "w_out")


def kernel(x, c, positions, w_ada, b_ada, norm_pre, norm_post, w_in, conv_w, conv_b, w_rg_a, b_rg_a, w_rg_x, b_rg_x, lru_lambda, norm_rec, norm_att, w_out, loss_target, m_w_ada, m_b_ada, m_norm_pre, m_norm_post, m_w_in, m_conv_w, m_conv_b, m_w_rg_a, m_b_rg_a, m_w_rg_x, m_b_rg_x, m_lru_lambda, m_norm_rec, m_norm_att, m_w_out, v_w_ada, v_b_ada, v_norm_pre, v_norm_post, v_w_in, v_conv_w, v_conv_b, v_w_rg_a, v_b_rg_a, v_w_rg_x, v_b_rg_x, v_lru_lambda, v_norm_rec, v_norm_att, v_w_out):
    given = dict(locals())
    wts = {n: given[n] for n in WEIGHTS}
    ms = {n: given["m_" + n] for n in WEIGHTS}
    vs = {n: given["v_" + n] for n in WEIGHTS}
    xi, yi, _ = _me()
    chip = 2 * xi + yi

    order = (chip ^ jnp.arange(NCHIP, dtype=jnp.int32)).astype(jnp.int32)
    cg, conv_full, mod, w_in_bf, cos, sin, proj, ht = _start_in_proj(
        c, conv_w[0], w_ada[0], b_ada, w_in[0], positions, x[0], norm_pre, order)

    p = dict(norm_pre=norm_pre, norm_post=norm_post, conv_b=conv_b, b_rg_a=b_rg_a, b_rg_x=b_rg_x,
             lru_lambda=lru_lambda, norm_rec=norm_rec, norm_att=norm_att, w_rg_a=w_rg_a[0], w_rg_x=w_rg_x[0])
    grad_x, g_in, g_out, gathered = _local_step(
        x[0], cos, sin, loss_target[0], mod, w_in_bf, proj, ht, w_out[0], conv_full, p)

    params = [d[n].reshape(shape) for n, shape in SMALL for d in (wts, ms, vs)]
    small_out, g_ada, loss_row = _small_update(*gathered, cg, params)
    grads = {"w_out": g_out, "w_in": g_in, "w_ada": g_ada}
    delta, new_m, new_v = {}, {}, {}
    for k, (n, _) in enumerate(SMALL):
        grads[n], delta[n], new_m[n], new_v[n] = small_out[4 * k:4 * k + 4]
    for n, res in zip(BIG, _adamw([(wts[n][0], grads[n], ms[n][0], vs[n][0]) for n in BIG])):
        delta[n], new_m[n], new_v[n] = res
    out = lambda d: [d[n].reshape(wts[n].shape) for n in WEIGHTS]
    return (loss_row[0, 0], grad_x.reshape(x.shape), *out(grads), *out(delta), *out(new_m), *out(new_v))
```

```python
import numpy as np
import jax
import jax.numpy as jnp
from jax import lax
from jax.experimental import pallas as pl
from jax.experimental.pallas import tpu as pltpu

F32 = jnp.float32
BF16 = jnp.bfloat16

S = 2048
D = 1024
E = 3072
R = 512
NDEV = 8
NCHIP = 4
EC = 768
LRU_C = 8.0
EPS = 1e-6
NEG = -1e30
HEAD = 64
BLK = 128
PATTERNS = (1, 4, 16)
ROPE_THETA = 10000.0
LANES = 128
VMEM_LIMIT = 56 * 1024 * 1024

B1, B2, LR, WD, ADAM_EPS, STEP = 0.9, 0.999, 0.001, 0.01, 1e-8, 10
MESH = pl.DeviceIdType.MESH


def _cp(sem=None, **kw):
    return pltpu.CompilerParams(dimension_semantics=sem, vmem_limit_bytes=VMEM_LIMIT, **kw)


def _dot(a, b):
    return jnp.dot(a, b, preferred_element_type=F32)


def _dot_nt(a, b):
    return lax.dot_general(a, b, (((1,), (1,)), ((), ())), preferred_element_type=F32)


def _dot_tn(a, b):
    return lax.dot_general(a, b, (((0,), (0,)), ((), ())), preferred_element_type=F32)


def _sigmoid(x):
    return 1.0 / (1.0 + jnp.exp(-x))


def _one_minus_exp(x, ex):
    poly = -x * (1.0 + x * (0.5 + x * (1.0 / 6 + x * (1.0 / 24))))
    return jnp.where(x > -1.0 / 16, poly, 1.0 - ex)


def _rms_fwd(v, g):
    rstd = lax.rsqrt(jnp.mean(v * v, axis=-1, keepdims=True) + EPS)
    vn = v * rstd
    return vn * g, vn, rstd


def _rms_bwd(dy, vn, rstd, g):
    dvn = dy * g
    dv = rstd * (dvn - vn * jnp.mean(dvn * vn, axis=-1, keepdims=True))
    return dv, jnp.sum(dy * vn, axis=0, keepdims=True)


RT = 256


def _shift_down(cur, prev8, j, row):
    if j == 0:
        return cur
    rolled = pltpu.roll(cur, j, 0)
    top = jnp.where(row[0:8] >= j, rolled[0:8], pltpu.roll(prev8, j, 0))
    return jnp.concatenate([top, rolled[8:]], axis=0)


def _shift_up(cur, next8, j, row):
    if j == 0:
        return cur
    rolled = pltpu.roll(cur, RT - j, 0)
    bot = jnp.where(row[RT - 8:] < RT - j, rolled[RT - 8:], pltpu.roll(next8, 8 - j, 0))
    return jnp.concatenate([rolled[:RT - 8], bot], axis=0)


def _rec_gates(xp, xprev8, row, cw_ref, cb_ref, wa_ref, ba_ref, wx_ref, bx_ref, lam_ref):
    xa = cb_ref[...] + sum(cw_ref[3 - j:4 - j, :] * _shift_down(xp, xprev8, j, row) for j in range(4))
    xab = xa.astype(BF16)
    r = _sigmoid(_dot(xab, wa_ref[...]) + ba_ref[...])
    ig = _sigmoid(_dot(xab, wx_ref[...]) + bx_ref[...])
    nl = -lam_ref[...]
    sp = jnp.maximum(nl, 0.0) + jnp.log1p(jnp.exp(-jnp.abs(nl)))
    la = (-LRU_C) * r * sp
    a = jnp.exp(la)
    mult = jnp.sqrt(_one_minus_exp(2.0 * la, a * a))
    return dict(xa=xa, xab=xab, r=r, ig=ig, sp=sp, la=la, a=a, mult=mult)


def _scan_fwd(a, u, row):
    sh = 1
    while sh < RT:
        a_s = jnp.where(row >= sh, pltpu.roll(a, sh, 0), 1.0)
        u_s = jnp.where(row >= sh, pltpu.roll(u, sh, 0), 0.0)
        u = a * u_s + u
        a = a * a_s
        sh *= 2
    return a, u


def _scan_bwd(al, g, row):
    sh = 1
    while sh < RT:
        al_s = jnp.where(row < RT - sh, pltpu.roll(al, RT - sh, 0), 1.0)
        g_s = jnp.where(row < RT - sh, pltpu.roll(g, RT - sh, 0), 0.0)
        g = g + al * g_s
        al = al * al_s
        sh *= 2
    return g


def _dense_from_blocks(blocks_ref, dense_ref):
    dense_ref[...] = jnp.zeros_like(dense_ref)
    for h in range(R // HEAD):
        dense_ref[h * HEAD:(h + 1) * HEAD, h * HEAD:(h + 1) * HEAD] = blocks_ref[h].astype(dense_ref.dtype)


def _rec_fwd(proj, conv_w, conv_b, wa_b, ba, wx_b, bx, lam, norm_rec):
    nt = S // RT

    def body(p_ref, cw_ref, cb_ref, wa_ref, ba_ref, wx_ref, bx_ref, lam_ref, nr_ref,
             h_ref, ya_ref, prev8, hc, wad, wxd):
        i = pl.program_id(0)

        @pl.when(i == 0)
        def _():
            prev8[...] = jnp.zeros_like(prev8)
            hc[...] = jnp.zeros_like(hc)
            _dense_from_blocks(wa_ref, wad)
            _dense_from_blocks(wx_ref, wxd)

        row = lax.broadcasted_iota(jnp.int32, (RT, R), 0)
        xp = p_ref[:, 0:R]
        ga = p_ref[:, R:2 * R]
        f = _rec_gates(xp, prev8[...], row, cw_ref, cb_ref, wad, ba_ref, wxd, bx_ref, lam_ref)
        u = f["mult"] * (f["ig"] * f["xa"])
        acum, hh = _scan_fwd(f["a"], u, row)
        h = hh + acum * hc[0:1, :]
        h_ref[...] = h
        hc[0:1, :] = h_ref[RT - 1:RT, :]
        prev8[...] = p_ref[RT - 8:RT, 0:R]
        yp = h * (ga * _sigmoid(ga))
        ya, _, _ = _rms_fwd(yp, nr_ref[...])
        ya_ref[...] = ya.astype(BF16)

    row1 = lambda n: pl.BlockSpec((1, n), lambda i: (0, 0))
    blocks = pl.BlockSpec((R // HEAD, HEAD, HEAD), lambda i: (0, 0, 0))
    return pl.pallas_call(
        body, name="rec_fwd", grid=(nt,),
        in_specs=[pl.BlockSpec((RT, 2 * R), lambda i: (i, 0)), pl.BlockSpec((4, R), lambda i: (0, 0)), row1(R),
                  blocks, row1(R), blocks, row1(R), row1(R), row1(R)],
        out_specs=[pl.BlockSpec((RT, R), lambda i: (i, 0)), pl.BlockSpec((RT, R), lambda i: (i, 0))],
        out_shape=[jax.ShapeDtypeStruct((S, R), F32), jax.ShapeDtypeStruct((S, R), BF16)],
        scratch_shapes=[pltpu.VMEM((8, R), F32), pltpu.VMEM((8, R), F32), pltpu.VMEM((R, R), BF16),
                        pltpu.VMEM((R, R), BF16)],
        compiler_params=_cp(("arbitrary",)),
    )(proj, conv_w, conv_b, wa_b, ba, wx_b, bx, lam, norm_rec)


def _rec_bwd(dproj, d_ya, proj, h_all, conv_w, conv_b, wa_b, ba, wx_b, bx, lam, norm_rec):
    nt = S // RT

    def body(dp_in, dya_ref, p_ref, pprev_ref, h_ref, hprev_ref, cw_ref, cb_ref, wab_ref, ba_ref, wxb_ref, bx_ref,
             lam_ref, nr_ref, dp_ref, dwab_ref, dwxb_ref, sm_ref, nxt8, cg, wa_ref, wx_ref, dwa_ref, dwx_ref):
        i = pl.program_id(0)
        ti = nt - 1 - i

        @pl.when(i == 0)
        def _():
            nxt8[...] = jnp.zeros_like(nxt8)
            cg[...] = jnp.zeros_like(cg)
            dwa_ref[...] = jnp.zeros_like(dwa_ref)
            dwx_ref[...] = jnp.zeros_like(dwx_ref)
            sm_ref[...] = jnp.zeros_like(sm_ref)
            _dense_from_blocks(wab_ref, wa_ref)
            _dense_from_blocks(wxb_ref, wx_ref)

        row = lax.broadcasted_iota(jnp.int32, (RT, R), 0)
        first = (ti > 0).astype(F32)
        xprev8 = pprev_ref[...] * first
        hprev8 = hprev_ref[...] * first
        xp = p_ref[:, 0:R]
        ga = p_ref[:, R:2 * R]
        f = _rec_gates(xp, xprev8, row, cw_ref, cb_ref, wa_ref, ba_ref, wx_ref, bx_ref, lam_ref)
        xa, r, ig, a, mult = f["xa"], f["r"], f["ig"], f["a"], f["mult"]
        h = h_ref[...]
        sg = _sigmoid(ga)
        gate = ga * sg
        yp = h * gate
        _, ypn, rstd = _rms_fwd(yp, nr_ref[...])
        d_yp, dnr = _rms_bwd(dya_ref[...], ypn, rstd, nr_ref[...])
        d_ga = d_yp * h * (sg * (1.0 + ga * (1.0 - sg)))
        dh = d_yp * gate + jnp.where(row == RT - 1, cg[0:1, :], 0.0)
        al = jnp.where(row < RT - 1, pltpu.roll(a, RT - 1, 0), 0.0)
        g = _scan_bwd(al, dh, row)
        cg[0:1, :] = jnp.sum(jnp.where(row == 0, a * g, 0.0), axis=0, keepdims=True)
        h_m1 = _shift_down(h, hprev8, 1, row)
        da = g * h_m1
        ix = ig * xa
        d_mult = g * ix
        d_ig = g * mult * xa
        d_xa = g * mult * ig
        d_la = da * a - d_mult * (a * a) / mult
        d_r = d_la * ((-LRU_C) * f["sp"])
        dsp = jnp.sum(d_la * ((-LRU_C) * r), axis=0, keepdims=True)
        dlam = dsp * (-_sigmoid(-lam_ref[...]))
        d_za = d_r * r * (1.0 - r)
        d_zx = d_ig * ig * (1.0 - ig)
        dzab = d_za.astype(BF16)
        dzxb = d_zx.astype(BF16)
        dwa_ref[...] += _dot_tn(f["xab"], dzab)
        dwx_ref[...] += _dot_tn(f["xab"], dzxb)
        d_xa = d_xa + _dot_nt(dzab, wa_ref[...]) + _dot_nt(dzxb, wx_ref[...])
        d_xp = sum(cw_ref[3 - j:4 - j, :] * _shift_up(d_xa, nxt8[...], j, row) for j in range(4))
        dcw = [jnp.sum(d_xa * _shift_down(xp, xprev8, 3 - k, row), axis=0, keepdims=True) for k in range(4)]
        dp_ref[:, 0:R] = d_xp.astype(BF16)
        dp_ref[:, R:2 * R] = d_ga.astype(BF16)
        dp8 = d_xa[0:8, :]
        nxt8[...] = dp8
        sm_ref[0:1, :] += jnp.sum(d_za, axis=0, keepdims=True)
        sm_ref[1:2, :] += jnp.sum(d_zx, axis=0, keepdims=True)
        sm_ref[2:3, :] += dlam
        sm_ref[3:4, :] += dnr
        sm_ref[4:5, :] += jnp.sum(d_xa, axis=0, keepdims=True)
        for k in range(4):
            sm_ref[8 + k:9 + k, :] += dcw[k]

        @pl.when(i == nt - 1)
        def _():
            for h in range(R // HEAD):
                dwab_ref[h] = dwa_ref[h * HEAD:(h + 1) * HEAD, h * HEAD:(h + 1) * HEAD].astype(BF16)
                dwxb_ref[h] = dwx_ref[h * HEAD:(h + 1) * HEAD, h * HEAD:(h + 1) * HEAD].astype(BF16)

    c0 = lambda shape: pl.BlockSpec(shape, lambda i: (0, 0))
    blocks = pl.BlockSpec((R // HEAD, HEAD, HEAD), lambda i: (0, 0, 0))
    rev = lambda i: nt - 1 - i
    prev8 = lambda i: (jnp.maximum((nt - 1 - i) * (RT // 8) - 1, 0), 0)
    return pl.pallas_call(
        body, name="rec_bwd", grid=(nt,),
        in_specs=[pl.BlockSpec(memory_space=pl.ANY),
                  pl.BlockSpec((RT, R), lambda i: (rev(i), 0)),
                  pl.BlockSpec((RT, 2 * R), lambda i: (rev(i), 0)), pl.BlockSpec((8, R), prev8),
                  pl.BlockSpec((RT, R), lambda i: (rev(i), 0)), pl.BlockSpec((8, R), prev8),
                  c0((4, R)), c0((1, R)), blocks, c0((1, R)), blocks, c0((1, R)), c0((1, R)), c0((1, R))],
        out_specs=[pl.BlockSpec((RT, 2 * R), lambda i: (rev(i), 0)), blocks, blocks, c0((16, R))],
        out_shape=[jax.ShapeDtypeStruct((S, E), BF16), jax.ShapeDtypeStruct((R // HEAD, HEAD, HEAD), BF16),
                   jax.ShapeDtypeStruct((R // HEAD, HEAD, HEAD), BF16), jax.ShapeDtypeStruct((16, R), F32)],
        scratch_shapes=[pltpu.VMEM((8, R), F32), pltpu.VMEM((8, R), F32), pltpu.VMEM((R, R), BF16),
                        pltpu.VMEM((R, R), BF16), pltpu.VMEM((R, R), F32), pltpu.VMEM((R, R), F32)],
        input_output_aliases={0: 0},
        compiler_params=_cp(("arbitrary",)),
    )(dproj, d_ya, proj, proj, h_all, h_all, conv_w, conv_b, wa_b, ba, wx_b, bx, lam, norm_rec)


NPAIR = R // LANES
QB, KB, VB, GB = 2 * R // LANES, 3 * R // LANES, 4 * R // LANES, 5 * R // LANES


def _rope_freq():
    half = HEAD // 2
    inv = np.float32(ROPE_THETA) ** (-(np.arange(half, dtype=np.float32) / np.float32(half)))
    return jnp.asarray(np.tile(inv.astype(np.float32), LANES // half)[None, :])


def _rot_half(x, first):
    return jnp.where(first, -pltpu.roll(x, LANES - HEAD // 2, 1), pltpu.roll(x, HEAD // 2, 1))


def _cos_sin(pos_ref, freq_ref):
    pos = jnp.broadcast_to(pos_ref[...].astype(F32), (LANES, S)).T
    ang = pos * freq_ref[...]
    return jnp.cos(ang), jnp.sin(ang)


SUB = 4


def _stages(d):
    assert d in (1, SUB, SUB * SUB)
    return d > SUB


def _strided_rows(src_ref, d, tmp):
    n = S // d
    if not _stages(d):
        for r in range(d):
            yield r * n, (src_ref[pl.ds(r, n, stride=d), :] if d > 1 else src_ref[...])
        return
    m = S // SUB
    for r in range(SUB):
        tmp[r * m:(r + 1) * m, :] = src_ref[pl.ds(r, m, stride=SUB), :]
    for r in range(SUB):
        for q in range(SUB):
            yield (r + SUB * q) * n, tmp[pl.ds(r * m + q, n, stride=SUB), :]


def _deint(src_ref, dst_ref, d, tmp):
    n = S // d
    for row0, v in _strided_rows(src_ref, d, tmp):
        dst_ref[row0:row0 + n, :] = v.astype(dst_ref.dtype)


def _reint(src_ref, dst_ref, d, accumulate, tmp):
    if _stages(d):
        n, m = S // d, S // SUB
        for r in range(SUB):
            for q in range(SUB):
                tmp[pl.ds(r * m + q, n, stride=SUB), :] = src_ref[(r + SUB * q) * n:(r + SUB * q + 1) * n, :]
        src_ref, d = tmp, SUB
    n = S // d
    for r in range(d):
        idx = (pl.ds(r, n, stride=d), slice(None)) if d > 1 else (slice(None), slice(None))
        v = src_ref[r * n:(r + 1) * n, :]
        if accumulate:
            dst_ref[idx] = dst_ref[idx] + v
        else:
            dst_ref[idx] = v


def _deint_heads(src_ref, dst0, dst1, d, tmp):
    n = S // d
    hm0 = lax.broadcasted_iota(jnp.int32, (n, LANES), 1) < HEAD
    for row0, v in _strided_rows(src_ref, d, tmp):
        dst0[row0:row0 + n, :] = jnp.where(hm0, v, 0.0).astype(BF16)
        dst1[row0:row0 + n, :] = jnp.where(hm0, 0.0, v).astype(BF16)


def _reint_prev(src_ref, dst_ref, d):
    n = S // d
    if n == BLK:
        return
    for r in range(d):
        idx = (pl.ds(r, n - BLK, stride=d), slice(None)) if d > 1 else (slice(0, n - BLK), slice(None))
        dst_ref[idx] = dst_ref[idx] + src_ref[r * n + BLK:(r + 1) * n, :]


def _pair_masks():
    qi = lax.broadcasted_iota(jnp.int32, (BLK, 2 * BLK), 0)
    ki = lax.broadcasted_iota(jnp.int32, (BLK, 2 * BLK), 1) & (BLK - 1)
    return ki <= qi, ki >= qi


def _two(ref0, ref1, st, axis):
    return jnp.concatenate([ref0[pl.ds(st, BLK), :], ref1[pl.ds(st, BLK), :]], axis=axis)


ATT_UNROLL = 8


def _att_fwd(proj, cos, sin, w_out):
    def body(q_ref, k_ref, v_ref, cos_ref, sin_ref, w_ref, att_ref, qr_ref, kr_ref, lse_ref, wbf_ref,
             qd, kd0, kd1, vd0, vd1, od, ld, tmp, on, ln, pcs, wbuf, *wsems):
        wg = _WeightGather(w_ref, wbuf, *wsems)
        pl.when(pl.program_id(0) == 0)(wg.start)
        pl.when(pl.program_id(0) == 1)(wg.forward)
        lane = lax.broadcasted_iota(jnp.int32, (S, LANES), 1)
        first = (lane & (HEAD // 2)) == 0
        cos, sin = cos_ref[...], sin_ref[...]
        q = q_ref[...]
        k = k_ref[...]
        qr_ref[...] = (q * cos + _rot_half(q, first) * sin) * (HEAD ** -0.5)
        kr_ref[...] = k * cos + _rot_half(k, first) * sin
        hm0 = lax.broadcasted_iota(jnp.int32, (BLK, LANES), 1) < HEAD
        top = lax.broadcasted_iota(jnp.int32, (2 * BLK, LANES), 0) < BLK
        ones2 = (top == (lax.broadcasted_iota(jnp.int32, (2 * BLK, LANES), 1) < HEAD)).astype(BF16)
        mc2, mp2 = _pair_masks()

        for pi, d in enumerate(PATTERNS):
            nb = S // d // BLK
            _deint(qr_ref, qd, d, tmp)
            _deint_heads(kr_ref, kd0, kd1, d, tmp)
            _deint_heads(v_ref, vd0, vd1, d, tmp)

            def blk(b, carry):
                st = pl.multiple_of(b * BLK, BLK)
                qb = qd[pl.ds(st, BLK), :]
                sc = jnp.where(mc2, _dot_nt(qb, _two(kd0, kd1, st, 0)), NEG)
                mx = sc
                if nb > 1:
                    stp = pl.multiple_of(jnp.maximum(b - 1, 0) * BLK, BLK)
                    mp = jnp.logical_and(mp2, lax.rem(b, nb) != 0)
                    sp = jnp.where(mp, _dot_nt(qb, _two(kd0, kd1, stp, 0)), NEG)
                    mx = jnp.maximum(sc, sp)
                m0 = jnp.max(mx[:, 0:BLK], axis=1, keepdims=True)
                m1 = jnp.max(mx[:, BLK:2 * BLK], axis=1, keepdims=True)
                mf = jnp.concatenate([jnp.broadcast_to(m0, (BLK, BLK)), jnp.broadcast_to(m1, (BLK, BLK))], axis=1)
                pcs[b, 0] = jnp.exp(sc - mf).astype(BF16)
                if nb > 1:
                    pcs[b, 1] = jnp.exp(sp - mf).astype(BF16)
                ld[pl.ds(st, BLK), :] = jnp.where(hm0, m0, m1)
                return carry

            lax.fori_loop(0, S // BLK, blk, 0, unroll=2 * ATT_UNROLL)

            def prods(b, carry):
                st = pl.multiple_of(b * BLK, BLK)
                o = _dot(pcs[b, 0], jnp.concatenate([_two(vd0, vd1, st, 0), ones2], axis=1))
                if nb > 1:
                    stp = pl.multiple_of(jnp.maximum(b - 1, 0) * BLK, BLK)
                    o = o + _dot(pcs[b, 1], jnp.concatenate([_two(vd0, vd1, stp, 0), ones2], axis=1))
                l = o[:, LANES:2 * LANES]
                od[pl.ds(st, BLK), :] = o[:, 0:LANES] / l
                ld[pl.ds(st, BLK), :] += jnp.log(l)
                return carry

            lax.fori_loop(0, S // BLK, prods, 0, unroll=2 * ATT_UNROLL)
            _reint(od, on.at[pi], d, False, tmp)
            _reint(ld, ln.at[pi], d, False, tmp)

        l0, l1, l2 = ln[0], ln[1], ln[2]
        m = jnp.maximum(jnp.maximum(l0, l1), l2)
        e0, e1, e2 = jnp.exp(l0 - m), jnp.exp(l1 - m), jnp.exp(l2 - m)
        den = e0 + e1 + e2
        att_ref[...] = (e0 * on[0] + e1 * on[1] + e2 * on[2]) / den
        lse_ref[...] = m + jnp.log(den)

        @pl.when(pl.program_id(0) == NPAIR - 1)
        def _():
            wg.finish()
            wbf_ref[...] = wbuf[...]

    col = lambda c0: pl.BlockSpec((S, LANES), lambda p: (0, c0 + p))
    out = pl.BlockSpec((S, LANES), lambda p: (0, p))
    tab = pl.BlockSpec((S, LANES), lambda p: (0, 0))
    vm = pl.BlockSpec(memory_space=pltpu.VMEM)
    return pl.pallas_call(
        body, name="att_fwd", grid=(NPAIR,),
        in_specs=[col(QB), col(KB), col(VB), tab, tab, vm],
        out_specs=[out, out, out, out, vm],
        out_shape=[jax.ShapeDtypeStruct((S, R), F32)] * 4 + [jax.ShapeDtypeStruct((NCHIP,) + w_out.shape, BF16)],
        scratch_shapes=[pltpu.VMEM((S, LANES), BF16)] * 5 + [pltpu.VMEM((S, LANES), F32)] * 3
        + [pltpu.VMEM((3, S, LANES), F32)] * 2 + [pltpu.VMEM((S // BLK, 2, BLK, 2 * BLK), BF16)]
        + [pltpu.VMEM((NCHIP,) + w_out.shape, BF16)] + _WeightGather.SEMS,
        compiler_params=_cp(("arbitrary",)),
    )(proj, proj, proj, cos, sin, w_out)


def _att_bwd(dproj, d_att, att, lse, qr, kr, proj, cos, sin, gw_out4):
    out_units = [(j, j, 0) for j in range(NCHIP)]

    nblk = S // BLK

    def body(dp_in, do_ref, o_ref, lse_ref, qr_ref, kr_ref, v_ref, cos_ref, sin_ref, gw_ref, dp_ref, gout_ref,
             qd, kd0, kd1, vd0, vd1, dod, kt, packn, packd, dqd, dkcd, dkpd, dvcd, dvpd,
             dqn, dkn, dvn, tmp, rows, trs, pts, dss, stage, sems, gred, *rs_scratch):
        p = pl.program_id(0)
        rs = _ReduceScatter(gw_ref, gred, out_units, *rs_scratch)
        for step, piece in enumerate((rs.start_halves, rs.send_partials, rs.reduce_owned)):
            pl.when(p == step)(piece)

        @pl.when(p == NPAIR - 1)
        def _():
            rs.finish()
            gout_ref[...] = gred[...]

        lane = lax.broadcasted_iota(jnp.int32, (S, LANES), 1)
        hms = lane < HEAD
        prod = do_ref[...] * o_ref[...]
        d0 = jnp.sum(jnp.where(hms, prod, 0.0), axis=1, keepdims=True)
        d1 = jnp.sum(jnp.where(hms, 0.0, prod), axis=1, keepdims=True)
        lse = lse_ref[...]
        quarter = HEAD // 2
        packn[...] = jnp.where(lane < quarter, lse,
                               jnp.where(hms, pltpu.roll(lse, LANES - quarter, 1), jnp.where(lane < 3 * quarter, d0, d1)))
        dqn[...] = jnp.zeros_like(dqn)
        dkn[...] = jnp.zeros_like(dkn)
        dvn[...] = jnp.zeros_like(dvn)
        hm0 = lax.broadcasted_iota(jnp.int32, (BLK, LANES), 1) < HEAD
        key = lax.broadcasted_iota(jnp.int32, (2 * BLK, BLK), 0) & (BLK - 1)
        qry = lax.broadcasted_iota(jnp.int32, (2 * BLK, BLK), 1)
        mct, mpt = key <= qry, key >= qry

        for d in PATTERNS:
            nb = S // d // BLK
            _deint(qr_ref, qd, d, tmp)
            _deint_heads(kr_ref, kd0, kd1, d, tmp)
            _deint_heads(v_ref, vd0, vd1, d, tmp)
            _deint(do_ref, dod, d, tmp)
            _deint(packn, packd, d, tmp)

            sides = (0, 1) if nb > 1 else (0,)

            def probs(b, carry):
                st = pl.multiple_of(b * BLK, BLK)
                kt[b] = _two(kd0, kd1, st, 0).astype(F32).T.astype(BF16)
                trs[b] = packd[pl.ds(st, BLK), :].T
                for j in range(4):
                    rows[b, j:j + 1, :] = trs[b, j * quarter:j * quarter + 1, :]
                qb, dob = qd[pl.ds(st, BLK), :], dod[pl.ds(st, BLK), :]
                both = lambda j: jnp.concatenate([jnp.broadcast_to(rows[b, j:j + 1, :], (BLK, BLK)),
                                                  jnp.broadcast_to(rows[b, j + 1:j + 2, :], (BLK, BLK))], axis=0)
                lbt, dlt = both(0), both(2)
                for sd in sides:
                    stk = pl.multiple_of(jnp.maximum(b - sd, 0) * BLK, BLK)
                    mask = mct if sd == 0 else jnp.logical_and(mpt, lax.rem(b, nb) != 0)
                    k2, v2 = _two(kd0, kd1, stk, 0), _two(vd0, vd1, stk, 0)
                    pt = jnp.where(mask, jnp.exp(_dot_nt(k2, qb) - lbt), 0.0)
                    pts[b, sd] = pt.astype(BF16)
                    dss[b, sd] = (pt * (_dot_nt(v2, dob) - dlt)).astype(BF16)
                return carry

            lax.fori_loop(0, nblk, probs, 0, unroll=2 * ATT_UNROLL)

            def prods(b, carry):
                st = pl.multiple_of(b * BLK, BLK)
                qb, dob = qd[pl.ds(st, BLK), :], dod[pl.ds(st, BLK), :]
                dq_t = None
                for sd in sides:
                    dst, ptb = dss[b, sd], pts[b, sd]
                    rk, rv = _dot(dst, qb), _dot(ptb, dob)
                    dqs = _dot(kt[jnp.maximum(b - sd, 0)], dst)
                    dq_t = dqs if dq_t is None else dq_t + dqs
                    dk, dv = (dkcd, dvcd) if sd == 0 else (dkpd, dvpd)
                    dk[pl.ds(st, BLK), :] = jnp.where(hm0, rk[0:BLK], rk[BLK:2 * BLK])
                    dv[pl.ds(st, BLK), :] = jnp.where(hm0, rv[0:BLK], rv[BLK:2 * BLK])
                dqd[pl.ds(st, BLK), :] = dq_t.T
                return carry

            lax.fori_loop(0, nblk, prods, 0, unroll=2 * ATT_UNROLL)
            _reint(dqd, dqn, d, True, tmp)
            _reint(dkcd, dkn, d, True, tmp)
            _reint(dvcd, dvn, d, True, tmp)
            _reint_prev(dkpd, dkn, d)
            _reint_prev(dvpd, dvn, d)

        lane = lax.broadcasted_iota(jnp.int32, (S, LANES), 1)
        first = (lane & (HEAD // 2)) == 0
        cos, sin = cos_ref[...], sin_ref[...]
        dq = dqn[...] * (HEAD ** -0.5)
        dk = dkn[...]
        stage[0] = (dq * cos - _rot_half(dq, first) * sin).astype(BF16)
        stage[1] = (dk * cos - _rot_half(dk, first) * sin).astype(BF16)
        stage[2] = dvn[...].astype(BF16)
        copies = [pltpu.make_async_copy(stage.at[j], dp_ref.at[:, pl.ds((2 + j) * R + p * LANES, LANES)], sems.at[j])
                  for j in range(3)]
        for cp in copies:
            cp.start()
        for cp in copies:
            cp.wait()

    blk = pl.BlockSpec((S, LANES), lambda p: (0, p))
    tab = pl.BlockSpec((S, LANES), lambda p: (0, 0))
    vm = pl.BlockSpec(memory_space=pltpu.VMEM)
    _, orows, ocols = gw_out4.shape
    return pl.pallas_call(
        body, name="att_bwd", grid=(NPAIR,),
        in_specs=[pl.BlockSpec(memory_space=pl.ANY), blk, blk, blk, blk, blk,
                  pl.BlockSpec((S, LANES), lambda p: (0, VB + p)), tab, tab, vm],
        out_specs=[pl.BlockSpec(memory_space=pl.ANY), vm],
        out_shape=[jax.ShapeDtypeStruct((S, E), BF16), jax.ShapeDtypeStruct((orows, ocols), F32)],
        scratch_shapes=[pltpu.VMEM((S, LANES), BF16)] * 6 + [pltpu.VMEM((nblk, LANES, 2 * BLK), BF16)]
        + [pltpu.VMEM((S, LANES), F32)] * 11
        + [pltpu.VMEM((nblk, 8, BLK), F32), pltpu.VMEM((nblk, LANES, BLK), F32)]
        + [pltpu.VMEM((nblk, 2, 2 * BLK, BLK), BF16)] * 2
        + [pltpu.VMEM((3, S, LANES), BF16), pltpu.SemaphoreType.DMA((3,)), pltpu.VMEM((orows, ocols), F32)]
        + _ReduceScatter.scratch(NCHIP, orows, ocols, 1),
        input_output_aliases={0: 0},
        compiler_params=_cp(("arbitrary",)),
    )(dproj, d_att, att, lse, qr, kr, proj, cos, sin, gw_out4)


def _out_fwd_bwd(ya, att, proj, w_out_bf, x, target, mod, norm_post, norm_att):
    ts = 512

    def body(ya_ref, att_ref, gb_ref, w_ref, x_ref, t_ref, mod_ref, npost_ref, natt_ref,
             gx_ref, dya_ref, datt_ref, dgb_ref, gw_ref, acc_ref):
        i = pl.program_id(0)

        @pl.when(i == 0)
        def _():
            gw_ref[...] = jnp.zeros_like(gw_ref)
            acc_ref[...] = jnp.zeros_like(acc_ref)

        gate = mod_ref[:, 2 * D:3 * D]
        att = att_ref[...]
        gb = gb_ref[...]
        sg = _sigmoid(gb)
        silu = gb * sg
        ybp = att * silu
        yb, ybn, rstd_b = _rms_fwd(ybp, natt_ref[...])
        cat = jnp.concatenate([ya_ref[...], yb.astype(BF16)], axis=1)
        mix = _dot(cat, w_ref[...])
        rn, mn, rstd_m = _rms_fwd(mix, npost_ref[...])
        err = x_ref[...] + gate * rn - t_ref[...]
        dy = err * (1.0 / D)
        gx_ref[...] = dy
        dmix, dnpost = _rms_bwd(dy * gate, mn, rstd_m, npost_ref[...])
        dmb = dmix.astype(BF16)
        gw_ref[...] += _dot_tn(cat, dmb)
        dcat = _dot_nt(dmb, w_ref[...])
        dya_ref[...] = dcat[:, 0:R]
        dybp, dnatt = _rms_bwd(dcat[:, R:2 * R], ybn, rstd_b, natt_ref[...])
        datt_ref[...] = dybp * silu
        dgb_ref[...] = (dybp * att * (sg * (1.0 + gb * (1.0 - sg)))).astype(BF16)
        acc_ref[0:1, :] += jnp.sum(dy * rn, axis=0, keepdims=True)
        acc_ref[1:2, :] += dnpost
        acc_ref[2:3, 0:R] += dnatt
        acc_ref[3:4, :] += jnp.sum(jnp.sum(err * err, axis=1, keepdims=True), axis=0, keepdims=True)

    tile = lambda w: pl.BlockSpec((ts, w), lambda i: (i, 0))
    c0 = lambda shape: pl.BlockSpec(shape, lambda i: (0, 0))
    return pl.pallas_call(
        body, name="out_fwd_bwd", grid=(S // ts,),
        in_specs=[tile(R), tile(R), pl.BlockSpec((ts, R), lambda i: (i, 5)), c0((D, D)), tile(D), tile(D),
                  c0((1, 3 * D)), c0((1, D)), c0((1, R))],
        out_specs=[tile(D), tile(R), tile(R), pl.BlockSpec((ts, R), lambda i: (i, 5)), c0((D, D)), c0((8, D))],
        out_shape=[jax.ShapeDtypeStruct((S, D), F32), jax.ShapeDtypeStruct((S, R), F32),
                   jax.ShapeDtypeStruct((S, R), F32), jax.ShapeDtypeStruct((S, E), BF16),
                   jax.ShapeDtypeStruct((D, D), F32), jax.ShapeDtypeStruct((8, D), F32)],
        compiler_params=_cp(("arbitrary",)),
    )(ya, att, proj, w_out_bf, x, target, mod, norm_post, norm_att)


UC = 256
UPC = EC // UC


NU = E // UC


def _unit_of_step(i):
    return (i % NCHIP) * UPC + i // NCHIP


def _in_proj_bwd(ht, dproj, w_in_bf, x, gx1, mod, norm_pre, smalls):
    ts = 256
    nt = S // ts
    half = D // 2
    units = [_unit_of_step(k) for k in range(NU)]
    owners = [u // UPC for u in units]
    ns = len(smalls)

    def body(*refs):
        (ht_ref, dpu_ref, dp_ref, w_hbm, x_ref, gx1_ref, mod_ref, np_ref), refs = refs[:8], refs[8:]
        small_in, refs = refs[:ns], refs[ns:]
        (gx_ref, gin_ref), refs = refs[:2], refs[2:]
        small_out, (acc_out,), refs = refs[:ns], refs[ns:ns + 1], refs[ns + 1:]
        mine, sib, tmp, stage, got, red, acc_ref, hs, hr, ps, pr, bs, br = refs[:13]
        early = _SmallGather(small_in, small_out, *refs[13:16])
        late = _SmallGather([acc_ref], [acc_out], *refs[16:19])
        w_ref, w_sem = refs[19:21]
        i = pl.program_id(0)
        w_copy = pltpu.make_async_copy(w_hbm, w_ref, w_sem)
        pl.when(i == 0)(w_copy.start)
        pl.when(i == NU)(w_copy.wait)
        xx, yy, c = _me()
        ci = 2 * xx + yy
        r0 = pl.multiple_of(c * half, half)
        r1 = pl.multiple_of((1 - c) * half, half)
        pl.when(i == 0)(early.start)
        pl.when(i == NU)(early.forward)

        def exch(k):
            return _remote(tmp.at[k % 2], sib.at[k], hs.at[k], hr.at[k], 1)

        def partial(k, sender):
            return pltpu.make_async_remote_copy(
                src_ref=stage.at[k], dst_ref=got.at[units[k] % UPC, sender], send_sem=ps.at[k],
                recv_sem=pr.at[k, sender], device_id=(owners[k] // 2, owners[k] % 2, c), device_id_type=MESH)

        def back(k, start):
            off = (units[k] % UPC) * UC
            blk = red.at[pl.ds(start, half), off:off + UC]
            return _remote(blk, blk, bs.at[k], br.at[k], 1)

        for k in range(NU + 1):
            @pl.when(i == k)
            def _():
                if k < NU:
                    if k >= 2:
                        exch(k - 2).wait_send()
                    dpu = dpu_ref[...]
                    tmp[k % 2] = _dot(ht_ref[pl.ds(r1, half), :], dpu)
                    exch(k).start()
                    mine[k] = _dot(ht_ref[pl.ds(r0, half), :], dpu)
                if k >= 1:
                    exch(k - 1).wait_recv()
                    mine[k - 1] += sib[k - 1]

                    @pl.when(ci != owners[k - 1])
                    def _():
                        stage[k - 1] = mine[k - 1].astype(BF16)
                        partial(k - 1, ci).start()

        @pl.when(i == NU)
        def _():
            acc_ref[...] = jnp.zeros_like(acc_ref)

        @pl.when(i >= NU)
        def _():
            dh = sum(_dot_nt(dp_ref[:, j * EC:(j + 1) * EC], w_ref[j]) for j in range(NCHIP))
            hp, xn, rstd = _rms_fwd(x_ref[...], np_ref[...])
            dx, dnp = _rms_bwd(dh * (1.0 + mod_ref[:, D:2 * D]), xn, rstd, np_ref[...])
            gx_ref[...] = gx1_ref[...] + dx
            acc_ref[0:1, :] += jnp.sum(dh, axis=0, keepdims=True)
            acc_ref[1:2, :] += jnp.sum(dh * hp, axis=0, keepdims=True)
            acc_ref[2:3, :] += dnp

        for t in range(UPC):
            @pl.when(i == NU + 1 + 2 * t)
            def _():
                for k in range(NCHIP * t, NCHIP * (t + 1)):
                    @pl.when(ci == owners[k])
                    def _():
                        off = (units[k] % UPC) * UC
                        red[pl.ds(r0, half), off:off + UC] = mine[k]
                        for s in range(NCHIP):
                            if s != owners[k]:
                                partial(k, s).wait_recv()
                                red[pl.ds(r0, half), off:off + UC] += got[units[k] % UPC, s].astype(F32)
                        back(k, r0).start()

        @pl.when(i == NU + nt - 1)
        def _():
            late.start()
            exch(NU - 2).wait_send()
            exch(NU - 1).wait_send()
            for k in range(NU):
                @pl.when(ci == owners[k])
                def _():
                    back(k, r1).wait_recv()
                    back(k, r0).wait_send()

                @pl.when(ci != owners[k])
                def _():
                    partial(k, ci).wait_send()
            gin_ref[...] = red[...]
            early.finish()
            late.forward()
            late.finish()

    tile = lambda w: pl.BlockSpec((ts, w), lambda i: (jnp.maximum(i - NU, 0), 0))
    c0 = lambda shape: pl.BlockSpec(shape, lambda i: (0, 0))
    vm = pl.BlockSpec(memory_space=pltpu.VMEM)
    hbm = pl.BlockSpec(memory_space=pl.ANY)
    gathered = [jax.ShapeDtypeStruct((NDEV,) + a.shape, a.dtype) for a in smalls] + [jax.ShapeDtypeStruct((NDEV, 8, D), F32)]
    return pl.pallas_call(
        body, name="in_proj_bwd", grid=(NU + nt,),
        in_specs=[vm, pl.BlockSpec((S, UC), lambda i: (0, _unit_of_step(jnp.minimum(i, NU - 1)))), tile(E),
                  hbm, tile(D), tile(D), c0((1, 3 * D)), c0((1, D))] + [vm] * ns,
        out_specs=[tile(D), vm] + [hbm] * (ns + 1),
        out_shape=[jax.ShapeDtypeStruct((S, D), F32), jax.ShapeDtypeStruct((D, EC), F32)] + gathered,
        scratch_shapes=[pltpu.VMEM((NU, half, UC), F32), pltpu.VMEM((NU, half, UC), F32),
                        pltpu.VMEM((2, half, UC), F32), pltpu.VMEM((NU, half, UC), BF16),
                        pltpu.VMEM((UPC, NCHIP, half, UC), BF16), pltpu.VMEM((D, EC), F32), pltpu.VMEM((8, D), F32),
                        pltpu.SemaphoreType.DMA((NU,)), pltpu.SemaphoreType.DMA((NU,)),
                        pltpu.SemaphoreType.DMA((NU,)), pltpu.SemaphoreType.DMA((NU, NCHIP)),
                        pltpu.SemaphoreType.DMA((NU,)), pltpu.SemaphoreType.DMA((NU,))]
        + _SmallGather.sems(ns) + _SmallGather.sems(1)
        + [pltpu.VMEM((NCHIP, D, EC), BF16), pltpu.SemaphoreType.DMA],
        compiler_params=_cp(("arbitrary",)),
    )(ht, dproj, dproj, w_in_bf, x, gx1, mod, norm_pre, *smalls)


def _local_step(x, cos, sin, target, mod, w_in_bf, proj, ht, w_out, conv_w, p):
    rec_p = (conv_w, p["conv_b"], p["w_rg_a"], p["b_rg_a"], p["w_rg_x"], p["b_rg_x"], p["lru_lambda"], p["norm_rec"])
    h_all, ya = _rec_fwd(proj, *rec_p)
    att, qr, kr, lse, w_out_bf = _att_fwd(proj, cos, sin, w_out)
    gx1, d_ya, d_att, dproj, gw_out, acc_o = _out_fwd_bwd(ya, att, proj, w_out_bf.reshape(D, D), x, target, mod,
                                                           p["norm_post"], p["norm_att"])
    dproj, g_out = _att_bwd(dproj, d_att, att, lse, qr, kr, proj, cos, sin, gw_out.reshape(NCHIP, D // NCHIP, D))
    dproj, dwa, dwx, sm = _rec_bwd(dproj, d_ya, proj, h_all, *rec_p)
    grad_x, g_in, *gathered = _in_proj_bwd(ht, dproj, w_in_bf, x, gx1, mod, p["norm_pre"], [acc_o, sm, dwa, dwx])
    return grad_x, g_in, g_out, gathered


def _me():
    return lax.axis_index("x"), lax.axis_index("y"), lax.axis_index("c")


def _flip(v, bit):
    return 1 - v if bit else v


def _peer(rel):
    x, y, c = _me()
    return (_flip(x, rel & 4), _flip(y, rel & 2), _flip(c, rel & 1))


def _remote(src, dst, send_sem, recv_sem, rel):
    return pltpu.make_async_remote_copy(src_ref=src, dst_ref=dst, send_sem=send_sem, recv_sem=recv_sem,
                                        device_id=_peer(rel), device_id_type=MESH)


class _WeightGather:
    SEMS = [pltpu.SemaphoreType.DMA((NCHIP - 1,))] * 4

    def __init__(self, w_ref, out_ref, send_sems, recv_sems, fsend_sems, frecv_sems):
        x, y, c = _me()
        self.w, self.out, self.ci = w_ref, out_ref, 2 * x + y
        self.half = w_ref.shape[0] // 2
        self.r0 = pl.multiple_of(c * self.half, self.half)
        self.r1 = pl.multiple_of((1 - c) * self.half, self.half)
        self.sems = (send_sems, recv_sems, fsend_sems, frecv_sems)

    def _ici(self, chip, k):
        blk = self.out.at[chip, pl.ds(self.r0, self.half), :]
        return _remote(blk, blk, self.sems[0].at[k - 1], self.sems[1].at[k - 1], 2 * k)

    def _d2d(self, chip, start, k):
        blk = self.out.at[chip, pl.ds(start, self.half), :]
        return _remote(blk, blk, self.sems[2].at[k - 1], self.sems[3].at[k - 1], 1)

    def start(self, diagonal=True):
        self.out[self.ci] = self.w[...].astype(BF16)
        for k in range(1, NCHIP if diagonal else NCHIP - 1):
            self._ici(self.ci, k).start()

    def _relay(self, chip, piece, k):
        q = self.half // 2
        blk = self.out.at[chip, pl.ds(self.r0 + piece * q, q), :]
        return _remote(blk, blk, self.relay_sems[0].at[piece], self.relay_sems[1].at[piece], 2 * k)

    def neighbours_landed(self, relay_send_sems, relay_recv_sems):
        self.relay_sems = (relay_send_sems, relay_recv_sems)
        for k in (1, 2):
            self._ici(self.ci ^ k, k).wait_recv()
        self._relay(self.ci ^ 2, 0, 1).start()
        self._relay(self.ci ^ 1, 1, 2).start()
        for k in (1, 2):
            self._d2d(self.ci ^ k, self.r0, k).start()

    def sibling_landed(self, k):
        self._d2d(self.ci ^ k, self.r1, k).wait_recv()

    def diagonal_landed(self):
        for piece, k in ((0, 1), (1, 2)):
            self._relay(self.ci ^ 3, piece, k).wait_recv()
        self._d2d(self.ci ^ 3, self.r0, 3).start()
        self._d2d(self.ci ^ 3, self.r1, 3).wait_recv()

    def finish_relayed(self):
        for k in (1, 2):
            self._ici(self.ci, k).wait_send()
        self._relay(self.ci ^ 2, 0, 1).wait_send()
        self._relay(self.ci ^ 1, 1, 2).wait_send()
        for k in range(1, NCHIP):
            self._d2d(self.ci ^ k, self.r0, k).wait_send()

    def forward(self):
        for k in range(1, NCHIP):
            self._ici(self.ci ^ k, k).wait_recv()
            self._d2d(self.ci ^ k, self.r0, k).start()

    def finish(self):
        for k in range(1, NCHIP):
            self._d2d(self.ci ^ k, self.r1, k).wait_recv()
        self.finish_sends()

    def finish_sends(self):
        for k in range(1, NCHIP):
            self._ici(self.ci, k).wait_send()
            self._d2d(self.ci ^ k, self.r0, k).wait_send()


class _SmallGather:
    @staticmethod
    def sems(n):
        return [pltpu.SemaphoreType.DMA((n, 7)), pltpu.SemaphoreType.DMA((n, 7)), pltpu.SemaphoreType.DMA((n,))]

    def __init__(self, srcs, outs, send_sems, recv_sems, local_sems):
        x, y, c = _me()
        self.srcs, self.outs = list(srcs), list(outs)
        self.ss, self.rs, self.ls = send_sems, recv_sems, local_sems
        self.ci, self.c = 2 * x + y, c
        self.me = 2 * self.ci + c

    def _own(self, a, slot, rel):
        return _remote(self.srcs[a], self.outs[a].at[self.me], self.ss.at[a, slot], self.rs.at[a, slot], rel)

    def _block(self, a, idx, slot, rel):
        blk = self.outs[a].at[idx]
        return _remote(blk, blk, self.ss.at[a, slot], self.rs.at[a, slot], rel)

    def _local(self, a):
        return pltpu.make_async_copy(self.srcs[a], self.outs[a].at[self.me], self.ls.at[a])

    def start(self):
        for a in range(len(self.srcs)):
            self._local(a).start()
            self._own(a, 0, 1).start()
            for k in range(1, NCHIP):
                self._own(a, k, 2 * k).start()

    def forward(self):
        for a in range(len(self.srcs)):
            for k in range(1, NCHIP):
                idx = 2 * (self.ci ^ k) + self.c
                self._block(a, idx, k, 2 * k).wait_recv()
                self._block(a, idx, 3 + k, 1).start()

    def finish(self):
        for a in range(len(self.srcs)):
            self._block(a, 2 * self.ci + 1 - self.c, 0, 1).wait_recv()
            for k in range(1, NCHIP):
                self._block(a, 2 * (self.ci ^ k) + 1 - self.c, 3 + k, 1).wait_recv()
            self._own(a, 0, 1).wait_send()
            for k in range(1, NCHIP):
                self._own(a, k, 2 * k).wait_send()
                self._block(a, 2 * (self.ci ^ k) + self.c, 3 + k, 1).wait_send()
            self._local(a).wait()


def _start_in_proj(c, conv_w, w_ada, b_ada, w_in, pos, x, norm_pre, order):
    ts = 512
    nt = S // ts
    wc = D + conv_w.size

    def body(order_ref, c_ref, cw_ref, wada_ref, b_ref, win_ref, pos_ref, freq_ref, x_ref, np_ref,
             g0_ref, conv_ref, mod_ref, wbf_ref, cos_ref, sin_ref, proj_ref, ht_ref,
             crow_ref, g0s, modp, modb, wbuf, hb_all, cs, cr, ms, mr, ws, wr, fs, fr, local_sems, ys, yr, osem):
        s, t = pl.program_id(0), pl.program_id(1)
        x, y, c = _me()
        ci = 2 * x + y
        me = 2 * ci + c
        wg = _WeightGather(win_ref, wbuf, ws, wr, fs, fr)
        cw = R // NCHIP

        @pl.when(jnp.logical_and(s == 0, t == 0))
        def _():
            wg.start(diagonal=False)
            crow_ref[:, 0:D] = c_ref[...]
            for k in range(4):
                crow_ref[:, D + k * cw:D + (k + 1) * cw] = cw_ref[k:k + 1, :]
            mine = pltpu.make_async_copy(crow_ref, g0s.at[pl.ds(me, 1), :], local_sems.at[0])
            mine.start()
            csend = [_remote(crow_ref, g0s.at[pl.ds(me, 1), :], cs.at[r - 1], cr.at[r - 1], r) for r in range(1, NDEV)]
            for cp in csend:
                cp.start()
            cos_ref[...], sin_ref[...] = _cos_sin(pos_ref, freq_ref)
            for r in range(1, NDEV):
                px, py, pc = _peer(r)
                _remote(crow_ref, g0s.at[pl.ds(4 * px + 2 * py + pc, 1), :], cs.at[r - 1], cr.at[r - 1], r).wait_recv()
            mine.wait()
            cv = g0s[:, 0:D]
            sc = cv * _sigmoid(cv)
            scb = jnp.concatenate([sc, jnp.zeros_like(sc)], axis=0).astype(BF16)
            b_cols = sum(jnp.where(ci == j, b_ref[:, j * EC:(j + 1) * EC], 0.0) for j in range(NCHIP))
            modp[...] = _dot(scb, wada_ref[...].astype(BF16))[0:NDEV, :] + b_cols
            own = pltpu.make_async_copy(modp.at[pl.ds(me, 1), :], modb.at[ci], local_sems.at[1])
            own.start()
            msend = []
            for k in range(1, NCHIP):
                cp = _remote(modp.at[pl.ds(2 * (ci ^ k) + c, 1), :], modb.at[ci], ms.at[k - 1], mr.at[k - 1], 2 * k)
                cp.start()
                msend.append(cp)
            for k in range(1, NCHIP):
                _remote(modp.at[pl.ds(me, 1), :], modb.at[ci ^ k], ms.at[k - 1], mr.at[k - 1], 2 * k).wait_recv()
            own.wait()
            for j in range(NCHIP):
                mod_ref[:, j * EC:(j + 1) * EC] = modb[j]
            for cp in csend + msend:
                cp.wait_send()
            g0_ref[...] = g0s[...]
            for j in range(NCHIP):
                for k in range(4):
                    conv_ref[k:k + 1, j * cw:(j + 1) * cw] = g0s[2 * j:2 * j + 1, D + k * cw:D + (k + 1) * cw]

        def keep(k):
            return pltpu.make_async_copy(wbuf.at[ci ^ k], wbf_ref.at[ci ^ k], osem.at[k])

        @pl.when(jnp.logical_and(s == 1, t == 0))
        def _():
            keep(0).start()
            wg.neighbours_landed(ys, yr)
            wg.sibling_landed(1)
            keep(1).start()

        @pl.when(jnp.logical_and(s == 2, t == 0))
        def _():
            wg.sibling_landed(2)
            keep(2).start()

        @pl.when(jnp.logical_and(s == 3, t == 0))
        def _():
            wg.relay_sems = (ys, yr)
            wg.diagonal_landed()
            keep(3).start()

        rows = pl.ds(pl.multiple_of(t * ts, ts), ts)

        @pl.when(s == 0)
        def _():
            hp, _, _ = _rms_fwd(x_ref[...], np_ref[...])
            h = hp * (1.0 + mod_ref[:, D:2 * D]) + mod_ref[:, 0:D]
            hb_all[rows, :] = h.astype(BF16)
            ht_ref[...] = h.T.astype(BF16)

        proj_ref[...] = _dot(hb_all[rows, :], wbuf[ci ^ s])

        @pl.when(jnp.logical_and(s == NCHIP - 1, t == nt - 1))
        def _():
            wg.relay_sems = (ys, yr)
            wg.finish_relayed()
            for k in range(NCHIP):
                keep(k).wait()

    vm = pl.BlockSpec(memory_space=pltpu.VMEM)
    first_pass = lambda s, t: jnp.where(s == 0, t, nt - 1)
    grid_spec = pltpu.PrefetchScalarGridSpec(
        num_scalar_prefetch=1, grid=(NCHIP, nt),
        in_specs=[vm, vm, vm, vm, vm, vm, vm, pl.BlockSpec((ts, D), lambda s, t, o: (first_pass(s, t), 0)),
                  pl.BlockSpec((1, D), lambda s, t, o: (0, 0))],
        out_specs=[vm, vm, vm, pl.BlockSpec(memory_space=pl.ANY), vm, vm,
                   pl.BlockSpec((ts, EC), lambda s, t, o: (t, o[s])),
                   pl.BlockSpec((D, ts), lambda s, t, o: (0, first_pass(s, t)))],
        scratch_shapes=[pltpu.VMEM((1, wc), F32),
                        pltpu.VMEM((NDEV, wc), F32), pltpu.VMEM((NDEV, EC), F32), pltpu.VMEM((NCHIP, 1, EC), F32),
                        pltpu.VMEM((NCHIP, D, EC), BF16), pltpu.VMEM((S, D), BF16),
                        pltpu.SemaphoreType.DMA((NDEV - 1,)), pltpu.SemaphoreType.DMA((NDEV - 1,)),
                        pltpu.SemaphoreType.DMA((NCHIP - 1,)), pltpu.SemaphoreType.DMA((NCHIP - 1,))]
        + _WeightGather.SEMS + [pltpu.SemaphoreType.DMA((2,))] * 3 + [pltpu.SemaphoreType.DMA((NCHIP,))])
    return pl.pallas_call(
        body, name="start_in_proj", grid_spec=grid_spec,
        out_shape=[jax.ShapeDtypeStruct((NDEV, wc), F32), jax.ShapeDtypeStruct((4, R), F32),
                   jax.ShapeDtypeStruct((1, 3 * D), F32),
                   jax.ShapeDtypeStruct((NCHIP, D, EC), BF16), jax.ShapeDtypeStruct((S, LANES), F32),
                   jax.ShapeDtypeStruct((S, LANES), F32), jax.ShapeDtypeStruct((S, E), F32),
                   jax.ShapeDtypeStruct((D, S), BF16)],
        compiler_params=_cp(("arbitrary", "arbitrary")),
    )(order, c, conv_w, w_ada, b_ada, w_in, pos, _rope_freq(), x, norm_pre)


class _ReduceScatter:
    @staticmethod
    def scratch(n_units, rows, ucols, max_owned):
        half = rows // 2
        return [pltpu.VMEM((n_units, half, ucols), F32), pltpu.VMEM((n_units, half, ucols), BF16),
                pltpu.VMEM((max_owned, NCHIP, half, ucols), BF16),
                pltpu.SemaphoreType.DMA((2,)), pltpu.SemaphoreType.DMA((n_units,)),
                pltpu.SemaphoreType.DMA((n_units, NCHIP)), pltpu.SemaphoreType.DMA((n_units,)),
                pltpu.SemaphoreType.DMA((n_units,))]

    def __init__(self, g_ref, out_ref, units, sib, stage, got, sem1, send2, recv2, send3, recv3):
        x, y, c = _me()
        self.c, self.ci = c, 2 * x + y
        self.g, self.out, self.units = g_ref, out_ref, units
        self.sib, self.stage, self.got = sib, stage, got
        self.sem1, self.send2, self.recv2, self.send3, self.recv3 = sem1, send2, recv2, send3, recv3
        self.half = g_ref.shape[1] // 2
        self.ucols = g_ref.shape[2]
        self.r0 = pl.multiple_of(c * self.half, self.half)
        self.r1 = pl.multiple_of((1 - c) * self.half, self.half)
        self.slot0 = units[0][0]
        assert [u[0] for u in units] == list(range(self.slot0, self.slot0 + len(units)))
        seen = {}
        self.local = []
        for _, owner, _ in units:
            self.local.append(seen.get(owner, 0))
            seen[owner] = seen.get(owner, 0) + 1

    def _halves(self):
        n = len(self.units)
        return _remote(self.g.at[pl.ds(self.slot0, n), pl.ds(self.r1, self.half), :], self.sib,
                       self.sem1.at[0], self.sem1.at[1], 1)

    def _partial(self, i, sender):
        _, owner, _ = self.units[i]
        return pltpu.make_async_remote_copy(
            src_ref=self.stage.at[i], dst_ref=self.got.at[self.local[i], sender],
            send_sem=self.send2.at[i], recv_sem=self.recv2.at[i, sender],
            device_id=(owner // 2, owner % 2, self.c), device_id_type=MESH)

    def _back(self, i, start):
        off = self.units[i][2]
        blk = self.out.at[pl.ds(start, self.half), off:off + self.ucols]
        return _remote(blk, blk, self.send3.at[i], self.recv3.at[i], 1)

    def start_halves(self):
        self._halves().start()

    def send_partials(self):
        self._halves().wait_recv()
        for i, (slot, owner, _) in enumerate(self.units):
            @pl.when(self.ci != owner)
            def _():
                self.stage[i] = (self.g[slot, pl.ds(self.r0, self.half), :] + self.sib[i]).astype(BF16)
                self._partial(i, self.ci).start()

    def reduce_owned(self):
        for i, (slot, owner, off) in enumerate(self.units):
            @pl.when(self.ci == owner)
            def _():
                rows, cols = pl.ds(self.r0, self.half), slice(off, off + self.ucols)
                self.out[rows, cols] = self.g[slot, pl.ds(self.r0, self.half), :] + self.sib[i]
                for s in range(NCHIP):
                    if s != owner:
                        self._partial(i, s).wait_recv()
                        self.out[rows, cols] += self.got[self.local[i], s].astype(F32)
                self._back(i, self.r0).start()

    def finish(self):
        self._halves().wait_send()
        for i, (_, owner, _) in enumerate(self.units):
            @pl.when(self.ci == owner)
            def _():
                self._back(i, self.r1).wait_recv()
                self._back(i, self.r0).wait_send()

            @pl.when(self.ci != owner)
            def _():
                self._partial(i, self.ci).wait_send()


def _silu_rows(c_ref):
    cv = c_ref[:, 0:D]
    sc = cv * _sigmoid(cv)
    return jnp.concatenate([sc, jnp.zeros_like(sc)], axis=0).astype(BF16)


def _adamw(groups):
    steps = 4
    specs = [pl.BlockSpec((w.shape[0] // steps, w.shape[1]), lambda i: (i, 0)) for w, _, _, _ in groups]

    def body(*refs):
        ins, outs = refs[:4 * len(groups)], refs[4 * len(groups):]
        for j in range(len(groups)):
            w_ref, g_ref, m_ref, v_ref = ins[4 * j:4 * j + 4]
            d_ref, nm_ref, nv_ref = outs[3 * j:3 * j + 3]
            d_ref[...], nm_ref[...], nv_ref[...] = _adamw_values(w_ref[...], g_ref[...], m_ref[...], v_ref[...])

    res = pl.pallas_call(
        body, name="adamw_big", grid=(steps,),
        in_specs=[s for s in specs for _ in range(4)], out_specs=[s for s in specs for _ in range(3)],
        out_shape=[jax.ShapeDtypeStruct(w.shape, F32) for w, _, _, _ in groups for _ in range(3)],
        compiler_params=_cp(("parallel",)),
    )(*[a for grp in groups for a in grp])
    return [res[3 * j:3 * j + 3] for j in range(len(groups))]


def _adamw_values(w, g, m, v):
    nm = B1 * m + (1.0 - B1) * g
    nv = B2 * v + (1.0 - B2) * (g * g)
    m_hat = nm / (1.0 - B1 ** STEP)
    v_hat = nv / (1.0 - B2 ** STEP)
    return (-LR) * (m_hat / (jnp.sqrt(v_hat) + ADAM_EPS) + WD * w), nm, nv


NB = R // HEAD
SMALL = (("b_ada", (1, 3 * D)), ("norm_pre", (1, D)), ("norm_post", (1, D)), ("conv_w", (4, R // NCHIP)),
         ("conv_b", (1, R)), ("w_rg_a", (NB, HEAD, HEAD)), ("b_rg_a", (1, R)), ("w_rg_x", (NB, HEAD, HEAD)),
         ("b_rg_x", (1, R)), ("lru_lambda", (1, R)), ("norm_rec", (1, R)), ("norm_att", (1, R)))


def _small_update(ao8, sm8, dwa8, dwx8, ai8, cg, params):
    n = len(SMALL)

    def body(ao_ref, sm_ref, dwa_ref, dwx_ref, ai_ref, cg_ref, *refs):
        pin, pout, (gada_ref, loss_ref, dmod) = refs[:3 * n], refs[3 * n:7 * n], refs[7 * n:]
        xx, yy, _ = _me()
        ci = 2 * xx + yy

        def total(ref, *idx):
            acc = ref[(0,) + idx].astype(F32)
            for d in range(1, NDEV):
                acc = acc + ref[(d,) + idx].astype(F32)
            return acc

        row = lambda ref, r, lanes=slice(None): total(ref, slice(r, r + 1), lanes)
        mine = lambda parts: sum(jnp.where(ci == j, part, 0.0) for j, part in enumerate(parts))
        cw = R // NCHIP
        grads = {
            "b_ada": [jnp.concatenate([row(ai_ref, 0), row(ai_ref, 1), row(ao_ref, 0)], axis=1)],
            "norm_pre": [row(ai_ref, 2)], "norm_post": [row(ao_ref, 1)],
            "conv_w": [mine([row(sm_ref, 8 + r, slice(j * cw, (j + 1) * cw)) for j in range(NCHIP)]) for r in range(4)],
            "conv_b": [row(sm_ref, 4)], "b_rg_a": [row(sm_ref, 0)], "b_rg_x": [row(sm_ref, 1)],
            "lru_lambda": [row(sm_ref, 2)], "norm_rec": [row(sm_ref, 3)], "norm_att": [row(ao_ref, 2, slice(0, R))],
            "w_rg_a": [total(dwa_ref, h) for h in range(NB)], "w_rg_x": [total(dwx_ref, h) for h in range(NB)],
        }
        loss_ref[...] = row(ao_ref, 3, slice(0, LANES)) * (0.5 / D)
        for k, (name, shape) in enumerate(SMALL):
            w_ref, m_ref, v_ref = pin[3 * k:3 * k + 3]
            outs = pout[4 * k:4 * k + 4]
            for r, g in enumerate(grads[name]):
                at = (slice(None),) if len(grads[name]) == 1 else ((r,) if len(shape) == 3 else (slice(r, r + 1),))
                res = (g,) + _adamw_values(w_ref[at], g, m_ref[at], v_ref[at])
                for o_ref, val in zip(outs, res):
                    o_ref[at] = val
        for d in range(NDEV):
            dmod[d:d + 1, :] = jnp.concatenate([ai_ref[d, 0:1, :], ai_ref[d, 1:2, :], ao_ref[d, 0:1, :]], axis=1)
        cols = mine([dmod[:, j * EC:(j + 1) * EC] for j in range(NCHIP)])
        colsb = jnp.concatenate([cols, jnp.zeros_like(cols)], axis=0).astype(BF16)
        gada_ref[...] = _dot_tn(_silu_rows(cg_ref), colsb)

    shapes = [jax.ShapeDtypeStruct(s, F32) for _, s in SMALL]
    outs = pl.pallas_call(
        body, name="small_update",
        out_shape=[s for s in shapes for _ in range(4)] + [jax.ShapeDtypeStruct((D, EC), F32),
                                                           jax.ShapeDtypeStruct((1, LANES), F32)],
        scratch_shapes=[pltpu.VMEM((NDEV, 3 * D), F32)],
        compiler_params=_cp(),
    )(ao8, sm8, dwa8, dwx8, ai8, cg, *params)
    return outs[:4 * n], outs[4 * n], outs[4 * n + 1]


BIG = ("w_ada", "w_in", "w_out")
WEIGHTS = ("w_ada", "b_ada", "norm_pre", "norm_post", "w_in", "conv_w", "conv_b", "w_rg_a", "b_rg_a", "w_rg_x",
           "b_rg_x", "lru_lambda", "norm_rec", "norm_att", "w_out")


def kernel(x, c, positions, w_ada, b_ada, norm_pre, norm_post, w_in, conv_w, conv_b, w_rg_a, b_rg_a, w_rg_x, b_rg_x, lru_lambda, norm_rec, norm_att, w_out, loss_target, m_w_ada, m_b_ada, m_norm_pre, m_norm_post, m_w_in, m_conv_w, m_conv_b, m_w_rg_a, m_b_rg_a, m_w_rg_x, m_b_rg_x, m_lru_lambda, m_norm_rec, m_norm_att, m_w_out, v_w_ada, v_b_ada, v_norm_pre, v_norm_post, v_w_in, v_conv_w, v_conv_b, v_w_rg_a, v_b_rg_a, v_w_rg_x, v_b_rg_x, v_lru_lambda, v_norm_rec, v_norm_att, v_w_out):
    given = dict(locals())
    wts = {n: given[n] for n in WEIGHTS}
    ms = {n: given["m_" + n] for n in WEIGHTS}
    vs = {n: given["v_" + n] for n in WEIGHTS}
    xi, yi, _ = _me()
    chip = 2 * xi + yi

    order = (chip ^ jnp.arange(NCHIP, dtype=jnp.int32)).astype(jnp.int32)
    cg, conv_full, mod, w_in_bf, cos, sin, proj, ht = _start_in_proj(
        c, conv_w[0], w_ada[0], b_ada, w_in[0], positions, x[0], norm_pre, order)

    p = dict(norm_pre=norm_pre, norm_post=norm_post, conv_b=conv_b, b_rg_a=b_rg_a, b_rg_x=b_rg_x,
             lru_lambda=lru_lambda, norm_rec=norm_rec, norm_att=norm_att, w_rg_a=w_rg_a[0], w_rg_x=w_rg_x[0])
    grad_x, g_in, g_out, gathered = _local_step(
        x[0], cos, sin, loss_target[0], mod, w_in_bf, proj, ht, w_out[0], conv_full, p)

    params = [d[n].reshape(shape) for n, shape in SMALL for d in (wts, ms, vs)]
    small_out, g_ada, loss_row = _small_update(*gathered, cg, params)
    grads = {"w_out": g_out, "w_in": g_in, "w_ada": g_ada}
    delta, new_m, new_v = {}, {}, {}
    for k, (n, _) in enumerate(SMALL):
        grads[n], delta[n], new_m[n], new_v[n] = small_out[4 * k:4 * k + 4]
    for n, res in zip(BIG, _adamw([(wts[n][0], grads[n], ms[n][0], vs[n][0]) for n in BIG])):
        delta[n], new_m[n], new_v[n] = res
    out = lambda d: [d[n].reshape(wts[n].shape) for n in WEIGHTS]
    return (loss_row[0, 0], grad_x.reshape(x.shape), *out(grads), *out(delta), *out(new_m), *out(new_v))
```

```python
import numpy as np
import jax
import jax.numpy as jnp
from jax import lax
from jax.experimental import pallas as pl
from jax.experimental.pallas import tpu as pltpu

F32 = jnp.float32
BF16 = jnp.bfloat16

S = 2048
D = 1024
E = 3072
R = 512
NDEV = 8
NCHIP = 4
EC = 768
LRU_C = 8.0
EPS = 1e-6
NEG = -1e30
HEAD = 64
BLK = 128
PATTERNS = (1, 4, 16)
ROPE_THETA = 10000.0
LANES = 128
VMEM_LIMIT = 56 * 1024 * 1024

B1, B2, LR, WD, ADAM_EPS, STEP = 0.9, 0.999, 0.001, 0.01, 1e-8, 10
MESH = pl.DeviceIdType.MESH


def _cp(sem=None, **kw):
    return pltpu.CompilerParams(dimension_semantics=sem, vmem_limit_bytes=VMEM_LIMIT, **kw)


def _dot(a, b):
    return jnp.dot(a, b, preferred_element_type=F32)


def _dot_nt(a, b):
    return lax.dot_general(a, b, (((1,), (1,)), ((), ())), preferred_element_type=F32)


def _dot_tn(a, b):
    return lax.dot_general(a, b, (((0,), (0,)), ((), ())), preferred_element_type=F32)


def _sigmoid(x):
    return 1.0 / (1.0 + jnp.exp(-x))


def _one_minus_exp(x, ex):
    poly = -x * (1.0 + x * (0.5 + x * (1.0 / 6 + x * (1.0 / 24))))
    return jnp.where(x > -1.0 / 16, poly, 1.0 - ex)


def _rms_fwd(v, g):
    rstd = lax.rsqrt(jnp.mean(v * v, axis=-1, keepdims=True) + EPS)
    vn = v * rstd
    return vn * g, vn, rstd


def _rms_bwd(dy, vn, rstd, g):
    dvn = dy * g
    dv = rstd * (dvn - vn * jnp.mean(dvn * vn, axis=-1, keepdims=True))
    return dv, jnp.sum(dy * vn, axis=0, keepdims=True)


RT = 256


def _shift_down(cur, prev8, j, row):
    if j == 0:
        return cur
    rolled = pltpu.roll(cur, j, 0)
    top = jnp.where(row[0:8] >= j, rolled[0:8], pltpu.roll(prev8, j, 0))
    return jnp.concatenate([top, rolled[8:]], axis=0)


def _shift_up(cur, next8, j, row):
    if j == 0:
        return cur
    rolled = pltpu.roll(cur, RT - j, 0)
    bot = jnp.where(row[RT - 8:] < RT - j, rolled[RT - 8:], pltpu.roll(next8, 8 - j, 0))
    return jnp.concatenate([rolled[:RT - 8], bot], axis=0)


def _rec_gates(xp, xprev8, row, cw_ref, cb_ref, wa_ref, ba_ref, wx_ref, bx_ref, lam_ref):
    xa = cb_ref[...] + sum(cw_ref[3 - j:4 - j, :] * _shift_down(xp, xprev8, j, row) for j in range(4))
    xab = xa.astype(BF16)
    r = _sigmoid(_dot(xab, wa_ref[...]) + ba_ref[...])
    ig = _sigmoid(_dot(xab, wx_ref[...]) + bx_ref[...])
    nl = -lam_ref[...]
    sp = jnp.maximum(nl, 0.0) + jnp.log1p(jnp.exp(-jnp.abs(nl)))
    la = (-LRU_C) * r * sp
    a = jnp.exp(la)
    mult = jnp.sqrt(_one_minus_exp(2.0 * la, a * a))
    return dict(xa=xa, xab=xab, r=r, ig=ig, sp=sp, la=la, a=a, mult=mult)


def _scan_fwd(a, u, row):
    sh = 1
    while sh < RT:
        a_s = jnp.where(row >= sh, pltpu.roll(a, sh, 0), 1.0)
        u_s = jnp.where(row >= sh, pltpu.roll(u, sh, 0), 0.0)
        u = a * u_s + u
        a = a * a_s
        sh *= 2
    return a, u


def _scan_bwd(al, g, row):
    sh = 1
    while sh < RT:
        al_s = jnp.where(row < RT - sh, pltpu.roll(al, RT - sh, 0), 1.0)
        g_s = jnp.where(row < RT - sh, pltpu.roll(g, RT - sh, 0), 0.0)
        g = g + al * g_s
        al = al * al_s
        sh *= 2
    return g


def _dense_from_blocks(blocks_ref, dense_ref):
    dense_ref[...] = jnp.zeros_like(dense_ref)
    for h in range(R // HEAD):
        dense_ref[h * HEAD:(h + 1) * HEAD, h * HEAD:(h + 1) * HEAD] = blocks_ref[h].astype(dense_ref.dtype)


def _rec_fwd(proj, conv_w, conv_b, wa_b, ba, wx_b, bx, lam, norm_rec):
    nt = S // RT

    def body(p_ref, cw_ref, cb_ref, wa_ref, ba_ref, wx_ref, bx_ref, lam_ref, nr_ref,
             h_ref, ya_ref, prev8, hc, wad, wxd):
        i = pl.program_id(0)

        @pl.when(i == 0)
        def _():
            prev8[...] = jnp.zeros_like(prev8)
            hc[...] = jnp.zeros_like(hc)
            _dense_from_blocks(wa_ref, wad)
            _dense_from_blocks(wx_ref, wxd)

        row = lax.broadcasted_iota(jnp.int32, (RT, R), 0)
        xp = p_ref[:, 0:R]
        ga = p_ref[:, R:2 * R]
        f = _rec_gates(xp, prev8[...], row, cw_ref, cb_ref, wad, ba_ref, wxd, bx_ref, lam_ref)
        u = f["mult"] * (f["ig"] * f["xa"])
        acum, hh = _scan_fwd(f["a"], u, row)
        h = hh + acum * hc[0:1, :]
        h_ref[...] = h
        hc[0:1, :] = h_ref[RT - 1:RT, :]
        prev8[...] = p_ref[RT - 8:RT, 0:R]
        yp = h * (ga * _sigmoid(ga))
        ya, _, _ = _rms_fwd(yp, nr_ref[...])
        ya_ref[...] = ya.astype(BF16)

    row1 = lambda n: pl.BlockSpec((1, n), lambda i: (0, 0))
    blocks = pl.BlockSpec((R // HEAD, HEAD, HEAD), lambda i: (0, 0, 0))
    return pl.pallas_call(
        body, name="rec_fwd", grid=(nt,),
        in_specs=[pl.BlockSpec((RT, 2 * R), lambda i: (i, 0)), pl.BlockSpec((4, R), lambda i: (0, 0)), row1(R),
                  blocks, row1(R), blocks, row1(R), row1(R), row1(R)],
        out_specs=[pl.BlockSpec((RT, R), lambda i: (i, 0)), pl.BlockSpec((RT, R), lambda i: (i, 0))],
        out_shape=[jax.ShapeDtypeStruct((S, R), F32), jax.ShapeDtypeStruct((S, R), BF16)],
        scratch_shapes=[pltpu.VMEM((8, R), F32), pltpu.VMEM((8, R), F32), pltpu.VMEM((R, R), BF16),
                        pltpu.VMEM((R, R), BF16)],
        compiler_params=_cp(("arbitrary",)),
    )(proj, conv_w, conv_b, wa_b, ba, wx_b, bx, lam, norm_rec)


def _rec_bwd(dproj, d_ya, proj, h_all, conv_w, conv_b, wa_b, ba, wx_b, bx, lam, norm_rec):
    nt = S // RT

    def body(dp_in, dya_ref, p_ref, pprev_ref, h_ref, hprev_ref, cw_ref, cb_ref, wab_ref, ba_ref, wxb_ref, bx_ref,
             lam_ref, nr_ref, dp_ref, dwab_ref, dwxb_ref, sm_ref, nxt8, cg, wa_ref, wx_ref, dwa_ref, dwx_ref):
        i = pl.program_id(0)
        ti = nt - 1 - i

        @pl.when(i == 0)
        def _():
            nxt8[...] = jnp.zeros_like(nxt8)
            cg[...] = jnp.zeros_like(cg)
            dwa_ref[...] = jnp.zeros_like(dwa_ref)
            dwx_ref[...] = jnp.zeros_like(dwx_ref)
            sm_ref[...] = jnp.zeros_like(sm_ref)
            _dense_from_blocks(wab_ref, wa_ref)
            _dense_from_blocks(wxb_ref, wx_ref)

        row = lax.broadcasted_iota(jnp.int32, (RT, R), 0)
        first = (ti > 0).astype(F32)
        xprev8 = pprev_ref[...] * first
        hprev8 = hprev_ref[...] * first
        xp = p_ref[:, 0:R]
        ga = p_ref[:, R:2 * R]
        f = _rec_gates(xp, xprev8, row, cw_ref, cb_ref, wa_ref, ba_ref, wx_ref, bx_ref, lam_ref)
        xa, r, ig, a, mult = f["xa"], f["r"], f["ig"], f["a"], f["mult"]
        h = h_ref[...]
        sg = _sigmoid(ga)
        gate = ga * sg
        yp = h * gate
        _, ypn, rstd = _rms_fwd(yp, nr_ref[...])
        d_yp, dnr = _rms_bwd(dya_ref[...], ypn, rstd, nr_ref[...])
        d_ga = d_yp * h * (sg * (1.0 + ga * (1.0 - sg)))
        dh = d_yp * gate + jnp.where(row == RT - 1, cg[0:1, :], 0.0)
        al = jnp.where(row < RT - 1, pltpu.roll(a, RT - 1, 0), 0.0)
        g = _scan_bwd(al, dh, row)
        cg[0:1, :] = jnp.sum(jnp.where(row == 0, a * g, 0.0), axis=0, keepdims=True)
        h_m1 = _shift_down(h, hprev8, 1, row)
        da = g * h_m1
        ix = ig * xa
        d_mult = g * ix
        d_ig = g * mult * xa
        d_xa = g * mult * ig
        d_la = da * a - d_mult * (a * a) / mult
        d_r = d_la * ((-LRU_C) * f["sp"])
        dsp = jnp.sum(d_la * ((-LRU_C) * r), axis=0, keepdims=True)
        dlam = dsp * (-_sigmoid(-lam_ref[...]))
        d_za = d_r * r * (1.0 - r)
        d_zx = d_ig * ig * (1.0 - ig)
        dzab = d_za.astype(BF16)
        dzxb = d_zx.astype(BF16)
        dwa_ref[...] += _dot_tn(f["xab"], dzab)
        dwx_ref[...] += _dot_tn(f["xab"], dzxb)
        d_xa = d_xa + _dot_nt(dzab, wa_ref[...]) + _dot_nt(dzxb, wx_ref[...])
        d_xp = sum(cw_ref[3 - j:4 - j, :] * _shift_up(d_xa, nxt8[...], j, row) for j in range(4))
        dcw = [jnp.sum(d_xa * _shift_down(xp, xprev8, 3 - k, row), axis=0, keepdims=True) for k in range(4)]
        dp_ref[:, 0:R] = d_xp.astype(BF16)
        dp_ref[:, R:2 * R] = d_ga.astype(BF16)
        dp8 = d_xa[0:8, :]
        nxt8[...] = dp8
        sm_ref[0:1, :] += jnp.sum(d_za, axis=0, keepdims=True)
        sm_ref[1:2, :] += jnp.sum(d_zx, axis=0, keepdims=True)
        sm_ref[2:3, :] += dlam
        sm_ref[3:4, :] += dnr
        sm_ref[4:5, :] += jnp.sum(d_xa, axis=0, keepdims=True)
        for k in range(4):
            sm_ref[8 + k:9 + k, :] += dcw[k]

        @pl.when(i == nt - 1)
        def _():
            for h in range(R // HEAD):
                dwab_ref[h] = dwa_ref[h * HEAD:(h + 1) * HEAD, h * HEAD:(h + 1) * HEAD].astype(BF16)
                dwxb_ref[h] = dwx_ref[h * HEAD:(h + 1) * HEAD, h * HEAD:(h + 1) * HEAD].astype(BF16)

    c0 = lambda shape: pl.BlockSpec(shape, lambda i: (0, 0))
    blocks = pl.BlockSpec((R // HEAD, HEAD, HEAD), lambda i: (0, 0, 0))
    rev = lambda i: nt - 1 - i
    prev8 = lambda i: (jnp.maximum((nt - 1 - i) * (RT // 8) - 1, 0), 0)
    return pl.pallas_call(
        body, name="rec_bwd", grid=(nt,),
        in_specs=[pl.BlockSpec(memory_space=pl.ANY),
                  pl.BlockSpec((RT, R), lambda i: (rev(i), 0)),
                  pl.BlockSpec((RT, 2 * R), lambda i: (rev(i), 0)), pl.BlockSpec((8, R), prev8),
                  pl.BlockSpec((RT, R), lambda i: (rev(i), 0)), pl.BlockSpec((8, R), prev8),
                  c0((4, R)), c0((1, R)), blocks, c0((1, R)), blocks, c0((1, R)), c0((1, R)), c0((1, R))],
        out_specs=[pl.BlockSpec((RT, 2 * R), lambda i: (rev(i), 0)), blocks, blocks, c0((16, R))],
        out_shape=[jax.ShapeDtypeStruct((S, E), BF16), jax.ShapeDtypeStruct((R // HEAD, HEAD, HEAD), BF16),
                   jax.ShapeDtypeStruct((R // HEAD, HEAD, HEAD), BF16), jax.ShapeDtypeStruct((16, R), F32)],
        scratch_shapes=[pltpu.VMEM((8, R), F32), pltpu.VMEM((8, R), F32), pltpu.VMEM((R, R), BF16),
                        pltpu.VMEM((R, R), BF16), pltpu.VMEM((R, R), F32), pltpu.VMEM((R, R), F32)],
        input_output_aliases={0: 0},
        compiler_params=_cp(("arbitrary",)),
    )(dproj, d_ya, proj, proj, h_all, h_all, conv_w, conv_b, wa_b, ba, wx_b, bx, lam, norm_rec)


NPAIR = R // LANES
QB, KB, VB, GB = 2 * R // LANES, 3 * R // LANES, 4 * R // LANES, 5 * R // LANES


def _rope_freq():
    half = HEAD // 2
    inv = np.float32(ROPE_THETA) ** (-(np.arange(half, dtype=np.float32) / np.float32(half)))
    return jnp.asarray(np.tile(inv.astype(np.float32), LANES // half)[None, :])


def _rot_half(x, first):
    return jnp.where(first, -pltpu.roll(x, LANES - HEAD // 2, 1), pltpu.roll(x, HEAD // 2, 1))


def _cos_sin(pos_ref, freq_ref):
    pos = jnp.broadcast_to(pos_ref[...].astype(F32), (LANES, S)).T
    ang = pos * freq_ref[...]
    return jnp.cos(ang), jnp.sin(ang)


SUB = 4


def _stages(d):
    assert d in (1, SUB, SUB * SUB)
    return d > SUB


def _strided_rows(src_ref, d, tmp):
    n = S // d
    if not _stages(d):
        for r in range(d):
            yield r * n, (src_ref[pl.ds(r, n, stride=d), :] if d > 1 else src_ref[...])
        return
    m = S // SUB
    for r in range(SUB):
        tmp[r * m:(r + 1) * m, :] = src_ref[pl.ds(r, m, stride=SUB), :]
    for r in range(SUB):
        for q in range(SUB):
            yield (r + SUB * q) * n, tmp[pl.ds(r * m + q, n, stride=SUB), :]


def _deint(src_ref, dst_ref, d, tmp):
    n = S // d
    for row0, v in _strided_rows(src_ref, d, tmp):
        dst_ref[row0:row0 + n, :] = v.astype(dst_ref.dtype)


def _reint(src_ref, dst_ref, d, accumulate, tmp):
    if _stages(d):
        n, m = S // d, S // SUB
        for r in range(SUB):
            for q in range(SUB):
                tmp[pl.ds(r * m + q, n, stride=SUB), :] = src_ref[(r + SUB * q) * n:(r + SUB * q + 1) * n, :]
        src_ref, d = tmp, SUB
    n = S // d
    for r in range(d):
        idx = (pl.ds(r, n, stride=d), slice(None)) if d > 1 else (slice(None), slice(None))
        v = src_ref[r * n:(r + 1) * n, :]
        if accumulate:
            dst_ref[idx] = dst_ref[idx] + v
        else:
            dst_ref[idx] = v


def _deint_heads(src_ref, dst0, dst1, d, tmp):
    n = S // d
    hm0 = lax.broadcasted_iota(jnp.int32, (n, LANES), 1) < HEAD
    for row0, v in _strided_rows(src_ref, d, tmp):
        dst0[row0:row0 + n, :] = jnp.where(hm0, v, 0.0).astype(BF16)
        dst1[row0:row0 + n, :] = jnp.where(hm0, 0.0, v).astype(BF16)


def _reint_prev(src_ref, dst_ref, d):
    n = S // d
    if n == BLK:
        return
    for r in range(d):
        idx = (pl.ds(r, n - BLK, stride=d), slice(None)) if d > 1 else (slice(0, n - BLK), slice(None))
        dst_ref[idx] = dst_ref[idx] + src_ref[r * n + BLK:(r + 1) * n, :]


def _pair_masks():
    qi = lax.broadcasted_iota(jnp.int32, (BLK, 2 * BLK), 0)
    ki = lax.broadcasted_iota(jnp.int32, (BLK, 2 * BLK), 1) & (BLK - 1)
    return ki <= qi, ki >= qi


def _two(ref0, ref1, st, axis):
    return jnp.concatenate([ref0[pl.ds(st, BLK), :], ref1[pl.ds(st, BLK), :]], axis=axis)


ATT_UNROLL = 8


def _att_fwd(proj, cos, sin, w_out):
    def body(q_ref, k_ref, v_ref, cos_ref, sin_ref, w_ref, att_ref, qr_ref, kr_ref, lse_ref, wbf_ref,
             qd, kd0, kd1, vd0, vd1, od, ld, tmp, on, ln, pcs, wbuf, *wsems):
        wg = _WeightGather(w_ref, wbuf, *wsems)
        pl.when(pl.program_id(0) == 0)(wg.start)
        pl.when(pl.program_id(0) == 2)(wg.forward)
        lane = lax.broadcasted_iota(jnp.int32, (S, LANES), 1)
        first = (lane & (HEAD // 2)) == 0
        cos, sin = cos_ref[...], sin_ref[...]
        q = q_ref[...]
        k = k_ref[...]
        qr_ref[...] = (q * cos + _rot_half(q, first) * sin) * (HEAD ** -0.5)
        kr_ref[...] = k * cos + _rot_half(k, first) * sin
        hm0 = lax.broadcasted_iota(jnp.int32, (BLK, LANES), 1) < HEAD
        top = lax.broadcasted_iota(jnp.int32, (2 * BLK, LANES), 0) < BLK
        ones2 = (top == (lax.broadcasted_iota(jnp.int32, (2 * BLK, LANES), 1) < HEAD)).astype(BF16)
        mc2, mp2 = _pair_masks()

        for pi, d in enumerate(PATTERNS):
            nb = S // d // BLK
            _deint(qr_ref, qd, d, tmp)
            _deint_heads(kr_ref, kd0, kd1, d, tmp)
            _deint_heads(v_ref, vd0, vd1, d, tmp)

            def blk(b, carry):
                st = pl.multiple_of(b * BLK, BLK)
                qb = qd[pl.ds(st, BLK), :]
                sc = jnp.where(mc2, _dot_nt(qb, _two(kd0, kd1, st, 0)), NEG)
                mx = sc
                if nb > 1:
                    stp = pl.multiple_of(jnp.maximum(b - 1, 0) * BLK, BLK)
                    mp = jnp.logical_and(mp2, lax.rem(b, nb) != 0)
                    sp = jnp.where(mp, _dot_nt(qb, _two(kd0, kd1, stp, 0)), NEG)
                    mx = jnp.maximum(sc, sp)
                m0 = jnp.max(mx[:, 0:BLK], axis=1, keepdims=True)
                m1 = jnp.max(mx[:, BLK:2 * BLK], axis=1, keepdims=True)
                mf = jnp.concatenate([jnp.broadcast_to(m0, (BLK, BLK)), jnp.broadcast_to(m1, (BLK, BLK))], axis=1)
                pcs[b, 0] = jnp.exp(sc - mf).astype(BF16)
                if nb > 1:
                    pcs[b, 1] = jnp.exp(sp - mf).astype(BF16)
                ld[pl.ds(st, BLK), :] = jnp.where(hm0, m0, m1)
                return carry

            lax.fori_loop(0, S // BLK, blk, 0, unroll=2 * ATT_UNROLL)

            def prods(b, carry):
                st = pl.multiple_of(b * BLK, BLK)
                o = _dot(pcs[b, 0], jnp.concatenate([_two(vd0, vd1, st, 0), ones2], axis=1))
                if nb > 1:
                    stp = pl.multiple_of(jnp.maximum(b - 1, 0) * BLK, BLK)
                    o = o + _dot(pcs[b, 1], jnp.concatenate([_two(vd0, vd1, stp, 0), ones2], axis=1))
                l = o[:, LANES:2 * LANES]
                od[pl.ds(st, BLK), :] = o[:, 0:LANES] / l
                ld[pl.ds(st, BLK), :] += jnp.log(l)
                return carry

            lax.fori_loop(0, S // BLK, prods, 0, unroll=2 * ATT_UNROLL)
            _reint(od, on.at[pi], d, False, tmp)
            _reint(ld, ln.at[pi], d, False, tmp)

        l0, l1, l2 = ln[0], ln[1], ln[2]
        m = jnp.maximum(jnp.maximum(l0, l1), l2)
        e0, e1, e2 = jnp.exp(l0 - m), jnp.exp(l1 - m), jnp.exp(l2 - m)
        den = e0 + e1 + e2
        att_ref[...] = (e0 * on[0] + e1 * on[1] + e2 * on[2]) / den
        lse_ref[...] = m + jnp.log(den)

        @pl.when(pl.program_id(0) == NPAIR - 1)
        def _():
            wg.finish()
            wbf_ref[...] = wbuf[...]

    col = lambda c0: pl.BlockSpec((S, LANES), lambda p: (0, c0 + p))
    out = pl.BlockSpec((S, LANES), lambda p: (0, p))
    tab = pl.BlockSpec((S, LANES), lambda p: (0, 0))
    vm = pl.BlockSpec(memory_space=pltpu.VMEM)
    return pl.pallas_call(
        body, name="att_fwd", grid=(NPAIR,),
        in_specs=[col(QB), col(KB), col(VB), tab, tab, vm],
        out_specs=[out, out, out, out, vm],
        out_shape=[jax.ShapeDtypeStruct((S, R), F32)] * 4 + [jax.ShapeDtypeStruct((NCHIP,) + w_out.shape, BF16)],
        scratch_shapes=[pltpu.VMEM((S, LANES), BF16)] * 5 + [pltpu.VMEM((S, LANES), F32)] * 3
        + [pltpu.VMEM((3, S, LANES), F32)] * 2 + [pltpu.VMEM((S // BLK, 2, BLK, 2 * BLK), BF16)]
        + [pltpu.VMEM((NCHIP,) + w_out.shape, BF16)] + _WeightGather.SEMS,
        compiler_params=_cp(("arbitrary",)),
    )(proj, proj, proj, cos, sin, w_out)


def _att_bwd(dproj, d_att, att, lse, qr, kr, proj, cos, sin, gw_out4):
    out_units = [(j, j, 0) for j in range(NCHIP)]

    nblk = S // BLK

    def body(dp_in, do_ref, o_ref, lse_ref, qr_ref, kr_ref, v_ref, cos_ref, sin_ref, gw_ref, dp_ref, gout_ref,
             qd, kd0, kd1, vd0, vd1, dod, kt, packn, packd, dqd, dkcd, dkpd, dvcd, dvpd,
             dqn, dkn, dvn, tmp, rows, trs, pts, dss, stage, sems, gred, *rs_scratch):
        p = pl.program_id(0)
        rs = _ReduceScatter(gw_ref, gred, out_units, *rs_scratch)
        for step, piece in enumerate((rs.start_halves, rs.send_partials, rs.reduce_owned)):
            pl.when(p == step)(piece)

        @pl.when(p == NPAIR - 1)
        def _():
            rs.finish()
            gout_ref[...] = gred[...]

        lane = lax.broadcasted_iota(jnp.int32, (S, LANES), 1)
        hms = lane < HEAD
        prod = do_ref[...] * o_ref[...]
        d0 = jnp.sum(jnp.where(hms, prod, 0.0), axis=1, keepdims=True)
        d1 = jnp.sum(jnp.where(hms, 0.0, prod), axis=1, keepdims=True)
        lse = lse_ref[...]
        quarter = HEAD // 2
        packn[...] = jnp.where(lane < quarter, lse,
                               jnp.where(hms, pltpu.roll(lse, LANES - quarter, 1), jnp.where(lane < 3 * quarter, d0, d1)))
        dqn[...] = jnp.zeros_like(dqn)
        dkn[...] = jnp.zeros_like(dkn)
        dvn[...] = jnp.zeros_like(dvn)
        hm0 = lax.broadcasted_iota(jnp.int32, (BLK, LANES), 1) < HEAD
        key = lax.broadcasted_iota(jnp.int32, (2 * BLK, BLK), 0) & (BLK - 1)
        qry = lax.broadcasted_iota(jnp.int32, (2 * BLK, BLK), 1)
        mct, mpt = key <= qry, key >= qry

        for d in PATTERNS:
            nb = S // d // BLK
            _deint(qr_ref, qd, d, tmp)
            _deint_heads(kr_ref, kd0, kd1, d, tmp)
            _deint_heads(v_ref, vd0, vd1, d, tmp)
            _deint(do_ref, dod, d, tmp)
            _deint(packn, packd, d, tmp)

            sides = (0, 1) if nb > 1 else (0,)

            def probs(b, carry):
                st = pl.multiple_of(b * BLK, BLK)
                kt[b] = _two(kd0, kd1, st, 0).astype(F32).T.astype(BF16)
                trs[b] = packd[pl.ds(st, BLK), :].T
                for j in range(4):
                    rows[b, j:j + 1, :] = trs[b, j * quarter:j * quarter + 1, :]
                qb, dob = qd[pl.ds(st, BLK), :], dod[pl.ds(st, BLK), :]
                both = lambda j: jnp.concatenate([jnp.broadcast_to(rows[b, j:j + 1, :], (BLK, BLK)),
                                                  jnp.broadcast_to(rows[b, j + 1:j + 2, :], (BLK, BLK))], axis=0)
                lbt, dlt = both(0), both(2)
                for sd in sides:
                    stk = pl.multiple_of(jnp.maximum(b - sd, 0) * BLK, BLK)
                    mask = mct if sd == 0 else jnp.logical_and(mpt, lax.rem(b, nb) != 0)
                    k2, v2 = _two(kd0, kd1, stk, 0), _two(vd0, vd1, stk, 0)
                    pt = jnp.where(mask, jnp.exp(_dot_nt(k2, qb) - lbt), 0.0)
                    pts[b, sd] = pt.astype(BF16)
                    dss[b, sd] = (pt * (_dot_nt(v2, dob) - dlt)).astype(BF16)
                return carry

            lax.fori_loop(0, nblk, probs, 0, unroll=2 * ATT_UNROLL)

            def prods(b, carry):
                st = pl.multiple_of(b * BLK, BLK)
                qb, dob = qd[pl.ds(st, BLK), :], dod[pl.ds(st, BLK), :]
                dq_t = None
                for sd in sides:
                    dst, ptb = dss[b, sd], pts[b, sd]
                    rk, rv = _dot(dst, qb), _dot(ptb, dob)
                    dqs = _dot(kt[jnp.maximum(b - sd, 0)], dst)
                    dq_t = dqs if dq_t is None else dq_t + dqs
                    dk, dv = (dkcd, dvcd) if sd == 0 else (dkpd, dvpd)
                    dk[pl.ds(st, BLK), :] = jnp.where(hm0, rk[0:BLK], rk[BLK:2 * BLK])
                    dv[pl.ds(st, BLK), :] = jnp.where(hm0, rv[0:BLK], rv[BLK:2 * BLK])
                dqd[pl.ds(st, BLK), :] = dq_t.T
                return carry

            lax.fori_loop(0, nblk, prods, 0, unroll=2 * ATT_UNROLL)
            _reint(dqd, dqn, d, True, tmp)
            _reint(dkcd, dkn, d, True, tmp)
            _reint(dvcd, dvn, d, True, tmp)
            _reint_prev(dkpd, dkn, d)
            _reint_prev(dvpd, dvn, d)

        lane = lax.broadcasted_iota(jnp.int32, (S, LANES), 1)
        first = (lane & (HEAD // 2)) == 0
        cos, sin = cos_ref[...], sin_ref[...]
        dq = dqn[...] * (HEAD ** -0.5)
        dk = dkn[...]
        stage[0] = (dq * cos - _rot_half(dq, first) * sin).astype(BF16)
        stage[1] = (dk * cos - _rot_half(dk, first) * sin).astype(BF16)
        stage[2] = dvn[...].astype(BF16)
        copies = [pltpu.make_async_copy(stage.at[j], dp_ref.at[:, pl.ds((2 + j) * R + p * LANES, LANES)], sems.at[j])
                  for j in range(3)]
        for cp in copies:
            cp.start()
        for cp in copies:
            cp.wait()

    blk = pl.BlockSpec((S, LANES), lambda p: (0, p))
    tab = pl.BlockSpec((S, LANES), lambda p: (0, 0))
    vm = pl.BlockSpec(memory_space=pltpu.VMEM)
    _, orows, ocols = gw_out4.shape
    return pl.pallas_call(
        body, name="att_bwd", grid=(NPAIR,),
        in_specs=[pl.BlockSpec(memory_space=pl.ANY), blk, blk, blk, blk, blk,
                  pl.BlockSpec((S, LANES), lambda p: (0, VB + p)), tab, tab, vm],
        out_specs=[pl.BlockSpec(memory_space=pl.ANY), vm],
        out_shape=[jax.ShapeDtypeStruct((S, E), BF16), jax.ShapeDtypeStruct((orows, ocols), F32)],
        scratch_shapes=[pltpu.VMEM((S, LANES), BF16)] * 6 + [pltpu.VMEM((nblk, LANES, 2 * BLK), BF16)]
        + [pltpu.VMEM((S, LANES), F32)] * 11
        + [pltpu.VMEM((nblk, 8, BLK), F32), pltpu.VMEM((nblk, LANES, BLK), F32)]
        + [pltpu.VMEM((nblk, 2, 2 * BLK, BLK), BF16)] * 2
        + [pltpu.VMEM((3, S, LANES), BF16), pltpu.SemaphoreType.DMA((3,)), pltpu.VMEM((orows, ocols), F32)]
        + _ReduceScatter.scratch(NCHIP, orows, ocols, 1),
        input_output_aliases={0: 0},
        compiler_params=_cp(("arbitrary",)),
    )(dproj, d_att, att, lse, qr, kr, proj, cos, sin, gw_out4)


def _out_fwd_bwd(ya, att, proj, w_out_bf, x, target, mod, norm_post, norm_att):
    ts = 512

    def body(ya_ref, att_ref, gb_ref, w_ref, x_ref, t_ref, mod_ref, npost_ref, natt_ref,
             gx_ref, dya_ref, datt_ref, dgb_ref, gw_ref, acc_ref):
        i = pl.program_id(0)

        @pl.when(i == 0)
        def _():
            gw_ref[...] = jnp.zeros_like(gw_ref)
            acc_ref[...] = jnp.zeros_like(acc_ref)

        gate = mod_ref[:, 2 * D:3 * D]
        att = att_ref[...]
        gb = gb_ref[...]
        sg = _sigmoid(gb)
        silu = gb * sg
        ybp = att * silu
        yb, ybn, rstd_b = _rms_fwd(ybp, natt_ref[...])
        cat = jnp.concatenate([ya_ref[...], yb.astype(BF16)], axis=1)
        mix = _dot(cat, w_ref[...])
        rn, mn, rstd_m = _rms_fwd(mix, npost_ref[...])
        err = x_ref[...] + gate * rn - t_ref[...]
        dy = err * (1.0 / D)
        gx_ref[...] = dy
        dmix, dnpost = _rms_bwd(dy * gate, mn, rstd_m, npost_ref[...])
        dmb = dmix.astype(BF16)
        gw_ref[...] += _dot_tn(cat, dmb)
        dcat = _dot_nt(dmb, w_ref[...])
        dya_ref[...] = dcat[:, 0:R]
        dybp, dnatt = _rms_bwd(dcat[:, R:2 * R], ybn, rstd_b, natt_ref[...])
        datt_ref[...] = dybp * silu
        dgb_ref[...] = (dybp * att * (sg * (1.0 + gb * (1.0 - sg)))).astype(BF16)
        acc_ref[0:1, :] += jnp.sum(dy * rn, axis=0, keepdims=True)
        acc_ref[1:2, :] += dnpost
        acc_ref[2:3, 0:R] += dnatt
        acc_ref[3:4, :] += jnp.sum(jnp.sum(err * err, axis=1, keepdims=True), axis=0, keepdims=True)

    tile = lambda w: pl.BlockSpec((ts, w), lambda i: (i, 0))
    c0 = lambda shape: pl.BlockSpec(shape, lambda i: (0, 0))
    return pl.pallas_call(
        body, name="out_fwd_bwd", grid=(S // ts,),
        in_specs=[tile(R), tile(R), pl.BlockSpec((ts, R), lambda i: (i, 5)), c0((D, D)), tile(D), tile(D),
                  c0((1, 3 * D)), c0((1, D)), c0((1, R))],
        out_specs=[tile(D), tile(R), tile(R), pl.BlockSpec((ts, R), lambda i: (i, 5)), c0((D, D)), c0((8, D))],
        out_shape=[jax.ShapeDtypeStruct((S, D), F32), jax.ShapeDtypeStruct((S, R), F32),
                   jax.ShapeDtypeStruct((S, R), F32), jax.ShapeDtypeStruct((S, E), BF16),
                   jax.ShapeDtypeStruct((D, D), F32), jax.ShapeDtypeStruct((8, D), F32)],
        compiler_params=_cp(("arbitrary",)),
    )(ya, att, proj, w_out_bf, x, target, mod, norm_post, norm_att)


UC = 256
UPC = EC // UC


NU = E // UC


def _unit_of_step(i):
    return (i % NCHIP) * UPC + i // NCHIP


def _in_proj_bwd(ht, dproj, w_in_bf, x, gx1, mod, norm_pre, smalls):
    ts = 256
    nt = S // ts
    half = D // 2
    units = [_unit_of_step(k) for k in range(NU)]
    owners = [u // UPC for u in units]
    ns = len(smalls)

    def body(*refs):
        (ht_ref, dpu_ref, dp_ref, w_hbm, x_ref, gx1_ref, mod_ref, np_ref), refs = refs[:8], refs[8:]
        small_in, refs = refs[:ns], refs[ns:]
        (gx_ref, gin_ref), refs = refs[:2], refs[2:]
        small_out, (acc_out,), refs = refs[:ns], refs[ns:ns + 1], refs[ns + 1:]
        mine, sib, tmp, stage, got, red, acc_ref, hs, hr, ps, pr, bs, br = refs[:13]
        early = _SmallGather(small_in, small_out, *refs[13:16])
        late = _SmallGather([acc_ref], [acc_out], *refs[16:19])
        w_ref, w_sem = refs[19:21]
        i = pl.program_id(0)
        w_copy = pltpu.make_async_copy(w_hbm, w_ref, w_sem)
        pl.when(i == 0)(w_copy.start)
        pl.when(i == NU)(w_copy.wait)
        xx, yy, c = _me()
        ci = 2 * xx + yy
        r0 = pl.multiple_of(c * half, half)
        r1 = pl.multiple_of((1 - c) * half, half)
        pl.when(i == 0)(early.start)
        pl.when(i == NU)(early.forward)

        def exch(k):
            return _remote(tmp.at[k % 2], sib.at[k], hs.at[k], hr.at[k], 1)

        def partial(k, sender):
            return pltpu.make_async_remote_copy(
                src_ref=stage.at[k], dst_ref=got.at[units[k] % UPC, sender], send_sem=ps.at[k],
                recv_sem=pr.at[k, sender], device_id=(owners[k] // 2, owners[k] % 2, c), device_id_type=MESH)

        def back(k, start):
            off = (units[k] % UPC) * UC
            blk = red.at[pl.ds(start, half), off:off + UC]
            return _remote(blk, blk, bs.at[k], br.at[k], 1)

        for k in range(NU + 1):
            @pl.when(i == k)
            def _():
                if k < NU:
                    if k >= 2:
                        exch(k - 2).wait_send()
                    dpu = dpu_ref[...]
                    tmp[k % 2] = _dot(ht_ref[pl.ds(r1, half), :], dpu)
                    exch(k).start()
                    mine[k] = _dot(ht_ref[pl.ds(r0, half), :], dpu)
                if k >= 1:
                    exch(k - 1).wait_recv()
                    mine[k - 1] += sib[k - 1]

                    @pl.when(ci != owners[k - 1])
                    def _():
                        stage[k - 1] = mine[k - 1].astype(BF16)
                        partial(k - 1, ci).start()

        @pl.when(i == NU)
        def _():
            acc_ref[...] = jnp.zeros_like(acc_ref)

        @pl.when(i >= NU)
        def _():
            dh = sum(_dot_nt(dp_ref[:, j * EC:(j + 1) * EC], w_ref[j]) for j in range(NCHIP))
            hp, xn, rstd = _rms_fwd(x_ref[...], np_ref[...])
            dx, dnp = _rms_bwd(dh * (1.0 + mod_ref[:, D:2 * D]), xn, rstd, np_ref[...])
            gx_ref[...] = gx1_ref[...] + dx
            acc_ref[0:1, :] += jnp.sum(dh, axis=0, keepdims=True)
            acc_ref[1:2, :] += jnp.sum(dh * hp, axis=0, keepdims=True)
            acc_ref[2:3, :] += dnp

        for t in range(UPC):
            @pl.when(i == NU + 1 + 2 * t)
            def _():
                for k in range(NCHIP * t, NCHIP * (t + 1)):
                    @pl.when(ci == owners[k])
                    def _():
                        off = (units[k] % UPC) * UC
                        red[pl.ds(r0, half), off:off + UC] = mine[k]
                        for s in range(NCHIP):
                            if s != owners[k]:
                                partial(k, s).wait_recv()
                                red[pl.ds(r0, half), off:off + UC] += got[units[k] % UPC, s].astype(F32)
                        back(k, r0).start()

        @pl.when(i == NU + nt - 1)
        def _():
            late.start()
            exch(NU - 2).wait_send()
            exch(NU - 1).wait_send()
            for k in range(NU):
                @pl.when(ci == owners[k])
                def _():
                    back(k, r1).wait_recv()
                    back(k, r0).wait_send()

                @pl.when(ci != owners[k])
                def _():
                    partial(k, ci).wait_send()
            gin_ref[...] = red[...]
            early.finish()
            late.forward()
            late.finish()

    tile = lambda w: pl.BlockSpec((ts, w), lambda i: (jnp.maximum(i - NU, 0), 0))
    c0 = lambda shape: pl.BlockSpec(shape, lambda i: (0, 0))
    vm = pl.BlockSpec(memory_space=pltpu.VMEM)
    hbm = pl.BlockSpec(memory_space=pl.ANY)
    gathered = [jax.ShapeDtypeStruct((NDEV,) + a.shape, a.dtype) for a in smalls] + [jax.ShapeDtypeStruct((NDEV, 8, D), F32)]
    return pl.pallas_call(
        body, name="in_proj_bwd", grid=(NU + nt,),
        in_specs=[vm, pl.BlockSpec((S, UC), lambda i: (0, _unit_of_step(jnp.minimum(i, NU - 1)))), tile(E),
                  hbm, tile(D), tile(D), c0((1, 3 * D)), c0((1, D))] + [vm] * ns,
        out_specs=[tile(D), vm] + [hbm] * (ns + 1),
        out_shape=[jax.ShapeDtypeStruct((S, D), F32), jax.ShapeDtypeStruct((D, EC), F32)] + gathered,
        scratch_shapes=[pltpu.VMEM((NU, half, UC), F32), pltpu.VMEM((NU, half, UC), F32),
                        pltpu.VMEM((2, half, UC), F32), pltpu.VMEM((NU, half, UC), BF16),
                        pltpu.VMEM((UPC, NCHIP, half, UC), BF16), pltpu.VMEM((D, EC), F32), pltpu.VMEM((8, D), F32),
                        pltpu.SemaphoreType.DMA((NU,)), pltpu.SemaphoreType.DMA((NU,)),
                        pltpu.SemaphoreType.DMA((NU,)), pltpu.SemaphoreType.DMA((NU, NCHIP)),
                        pltpu.SemaphoreType.DMA((NU,)), pltpu.SemaphoreType.DMA((NU,))]
        + _SmallGather.sems(ns) + _SmallGather.sems(1)
        + [pltpu.VMEM((NCHIP, D, EC), BF16), pltpu.SemaphoreType.DMA],
        compiler_params=_cp(("arbitrary",)),
    )(ht, dproj, dproj, w_in_bf, x, gx1, mod, norm_pre, *smalls)


def _local_step(x, cos, sin, target, mod, w_in_bf, proj, ht, w_out, conv_w, p):
    rec_p = (conv_w, p["conv_b"], p["w_rg_a"], p["b_rg_a"], p["w_rg_x"], p["b_rg_x"], p["lru_lambda"], p["norm_rec"])
    h_all, ya = _rec_fwd(proj, *rec_p)
    att, qr, kr, lse, w_out_bf = _att_fwd(proj, cos, sin, w_out)
    gx1, d_ya, d_att, dproj, gw_out, acc_o = _out_fwd_bwd(ya, att, proj, w_out_bf.reshape(D, D), x, target, mod,
                                                           p["norm_post"], p["norm_att"])
    dproj, g_out = _att_bwd(dproj, d_att, att, lse, qr, kr, proj, cos, sin, gw_out.reshape(NCHIP, D // NCHIP, D))
    dproj, dwa, dwx, sm = _rec_bwd(dproj, d_ya, proj, h_all, *rec_p)
    grad_x, g_in, *gathered = _in_proj_bwd(ht, dproj, w_in_bf, x, gx1, mod, p["norm_pre"], [acc_o, sm, dwa, dwx])
    return grad_x, g_in, g_out, gathered


def _me():
    return lax.axis_index("x"), lax.axis_index("y"), lax.axis_index("c")


def _flip(v, bit):
    return 1 - v if bit else v


def _peer(rel):
    x, y, c = _me()
    return (_flip(x, rel & 4), _flip(y, rel & 2), _flip(c, rel & 1))


def _remote(src, dst, send_sem, recv_sem, rel):
    return pltpu.make_async_remote_copy(src_ref=src, dst_ref=dst, send_sem=send_sem, recv_sem=recv_sem,
                                        device_id=_peer(rel), device_id_type=MESH)


class _WeightGather:
    SEMS = [pltpu.SemaphoreType.DMA((NCHIP - 1,))] * 4

    def __init__(self, w_ref, out_ref, send_sems, recv_sems, fsend_sems, frecv_sems):
        x, y, c = _me()
        self.w, self.out, self.ci = w_ref, out_ref, 2 * x + y
        self.half = w_ref.shape[0] // 2
        self.r0 = pl.multiple_of(c * self.half, self.half)
        self.r1 = pl.multiple_of((1 - c) * self.half, self.half)
        self.sems = (send_sems, recv_sems, fsend_sems, frecv_sems)

    def _ici(self, chip, k):
        blk = self.out.at[chip, pl.ds(self.r0, self.half), :]
        return _remote(blk, blk, self.sems[0].at[k - 1], self.sems[1].at[k - 1], 2 * k)

    def _d2d(self, chip, start, k):
        blk = self.out.at[chip, pl.ds(start, self.half), :]
        return _remote(blk, blk, self.sems[2].at[k - 1], self.sems[3].at[k - 1], 1)

    def start(self, diagonal=True):
        self.out[self.ci] = self.w[...].astype(BF16)
        for k in range(1, NCHIP if diagonal else NCHIP - 1):
            self._ici(self.ci, k).start()

    def _relay(self, chip, piece, k):
        q = self.half // 2
        blk = self.out.at[chip, pl.ds(self.r0 + piece * q, q), :]
        return _remote(blk, blk, self.relay_sems[0].at[piece], self.relay_sems[1].at[piece], 2 * k)

    def neighbours_landed(self, relay_send_sems, relay_recv_sems):
        self.relay_sems = (relay_send_sems, relay_recv_sems)
        for k in (1, 2):
            self._ici(self.ci ^ k, k).wait_recv()
        self._relay(self.ci ^ 2, 0, 1).start()
        self._relay(self.ci ^ 1, 1, 2).start()
        for k in (1, 2):
            self._d2d(self.ci ^ k, self.r0, k).start()

    def sibling_landed(self, k):
        self._d2d(self.ci ^ k, self.r1, k).wait_recv()

    def diagonal_landed(self):
        for piece, k in ((0, 1), (1, 2)):
            self._relay(self.ci ^ 3, piece, k).wait_recv()
        self._d2d(self.ci ^ 3, self.r0, 3).start()
        self._d2d(self.ci ^ 3, self.r1, 3).wait_recv()

    def finish_relayed(self):
        for k in (1, 2):
            self._ici(self.ci, k).wait_send()
        self._relay(self.ci ^ 2, 0, 1).wait_send()
        self._relay(self.ci ^ 1, 1, 2).wait_send()
        for k in range(1, NCHIP):
            self._d2d(self.ci ^ k, self.r0, k).wait_send()

    def forward(self):
        for k in range(1, NCHIP):
            self._ici(self.ci ^ k, k).wait_recv()
            self._d2d(self.ci ^ k, self.r0, k).start()

    def finish(self):
        for k in range(1, NCHIP):
            self._d2d(self.ci ^ k, self.r1, k).wait_recv()
        self.finish_sends()

    def finish_sends(self):
        for k in range(1, NCHIP):
            self._ici(self.ci, k).wait_send()
            self._d2d(self.ci ^ k, self.r0, k).wait_send()


class _SmallGather:
    @staticmethod
    def sems(n):
        return [pltpu.SemaphoreType.DMA((n, 7)), pltpu.SemaphoreType.DMA((n, 7)), pltpu.SemaphoreType.DMA((n,))]

    def __init__(self, srcs, outs, send_sems, recv_sems, local_sems):
        x, y, c = _me()
        self.srcs, self.outs = list(srcs), list(outs)
        self.ss, self.rs, self.ls = send_sems, recv_sems, local_sems
        self.ci, self.c = 2 * x + y, c
        self.me = 2 * self.ci + c

    def _own(self, a, slot, rel):
        return _remote(self.srcs[a], self.outs[a].at[self.me], self.ss.at[a, slot], self.rs.at[a, slot], rel)

    def _block(self, a, idx, slot, rel):
        blk = self.outs[a].at[idx]
        return _remote(blk, blk, self.ss.at[a, slot], self.rs.at[a, slot], rel)

    def _local(self, a):
        return pltpu.make_async_copy(self.srcs[a], self.outs[a].at[self.me], self.ls.at[a])

    def start(self):
        for a in range(len(self.srcs)):
            self._local(a).start()
            self._own(a, 0, 1).start()
            for k in range(1, NCHIP):
                self._own(a, k, 2 * k).start()

    def forward(self):
        for a in range(len(self.srcs)):
            for k in range(1, NCHIP):
                idx = 2 * (self.ci ^ k) + self.c
                self._block(a, idx, k, 2 * k).wait_recv()
                self._block(a, idx, 3 + k, 1).start()

    def finish(self):
        for a in range(len(self.srcs)):
            self._block(a, 2 * self.ci + 1 - self.c, 0, 1).wait_recv()
            for k in range(1, NCHIP):
                self._block(a, 2 * (self.ci ^ k) + 1 - self.c, 3 + k, 1).wait_recv()
            self._own(a, 0, 1).wait_send()
            for k in range(1, NCHIP):
                self._own(a, k, 2 * k).wait_send()
                self._block(a, 2 * (self.ci ^ k) + self.c, 3 + k, 1).wait_send()
            self._local(a).wait()


def _start_in_proj(c, conv_w, w_ada, b_ada, w_in, pos, x, norm_pre, order):
    ts = 512
    nt = S // ts
    wc = D + conv_w.size

    def body(order_ref, c_ref, cw_ref, wada_ref, b_ref, win_ref, pos_ref, freq_ref, x_ref, np_ref,
             g0_ref, conv_ref, mod_ref, wbf_ref, cos_ref, sin_ref, proj_ref, ht_ref,
             crow_ref, g0s, modp, modb, wbuf, hb_all, cs, cr, ms, mr, ws, wr, fs, fr, local_sems, ys, yr, osem):
        s, t = pl.program_id(0), pl.program_id(1)
        x, y, c = _me()
        ci = 2 * x + y
        me = 2 * ci + c
        wg = _WeightGather(win_ref, wbuf, ws, wr, fs, fr)
        cw = R // NCHIP

        @pl.when(jnp.logical_and(s == 0, t == 0))
        def _():
            wg.start(diagonal=False)
            crow_ref[:, 0:D] = c_ref[...]
            for k in range(4):
                crow_ref[:, D + k * cw:D + (k + 1) * cw] = cw_ref[k:k + 1, :]
            mine = pltpu.make_async_copy(crow_ref, g0s.at[pl.ds(me, 1), :], local_sems.at[0])
            mine.start()
            csend = [_remote(crow_ref, g0s.at[pl.ds(me, 1), :], cs.at[r - 1], cr.at[r - 1], r) for r in range(1, NDEV)]
            for cp in csend:
                cp.start()
            cos_ref[...], sin_ref[...] = _cos_sin(pos_ref, freq_ref)
            for r in range(1, NDEV):
                px, py, pc = _peer(r)
                _remote(crow_ref, g0s.at[pl.ds(4 * px + 2 * py + pc, 1), :], cs.at[r - 1], cr.at[r - 1], r).wait_recv()
            mine.wait()
            cv = g0s[:, 0:D]
            sc = cv * _sigmoid(cv)
            scb = jnp.concatenate([sc, jnp.zeros_like(sc)], axis=0).astype(BF16)
            b_cols = sum(jnp.where(ci == j, b_ref[:, j * EC:(j + 1) * EC], 0.0) for j in range(NCHIP))
            modp[...] = _dot(scb, wada_ref[...].astype(BF16))[0:NDEV, :] + b_cols
            own = pltpu.make_async_copy(modp.at[pl.ds(me, 1), :], modb.at[ci], local_sems.at[1])
            own.start()
            msend = []
            for k in range(1, NCHIP):
                cp = _remote(modp.at[pl.ds(2 * (ci ^ k) + c, 1), :], modb.at[ci], ms.at[k - 1], mr.at[k - 1], 2 * k)
                cp.start()
                msend.append(cp)
            for k in range(1, NCHIP):
                _remote(modp.at[pl.ds(me, 1), :], modb.at[ci ^ k], ms.at[k - 1], mr.at[k - 1], 2 * k).wait_recv()
            own.wait()
            for j in range(NCHIP):
                mod_ref[:, j * EC:(j + 1) * EC] = modb[j]
            for cp in csend + msend:
                cp.wait_send()
            g0_ref[...] = g0s[...]
            for j in range(NCHIP):
                for k in range(4):
                    conv_ref[k:k + 1, j * cw:(j + 1) * cw] = g0s[2 * j:2 * j + 1, D + k * cw:D + (k + 1) * cw]

        def keep(k):
            return pltpu.make_async_copy(wbuf.at[ci ^ k], wbf_ref.at[ci ^ k], osem.at[k])

        @pl.when(jnp.logical_and(s == 1, t == 0))
        def _():
            keep(0).start()
            wg.neighbours_landed(ys, yr)
            wg.sibling_landed(1)
            keep(1).start()

        @pl.when(jnp.logical_and(s == 2, t == 0))
        def _():
            wg.sibling_landed(2)
            keep(2).start()

        @pl.when(jnp.logical_and(s == 3, t == 0))
        def _():
            wg.relay_sems = (ys, yr)
            wg.diagonal_landed()
            keep(3).start()

        rows = pl.ds(pl.multiple_of(t * ts, ts), ts)

        @pl.when(s == 0)
        def _():
            hp, _, _ = _rms_fwd(x_ref[...], np_ref[...])
            h = hp * (1.0 + mod_ref[:, D:2 * D]) + mod_ref[:, 0:D]
            hb_all[rows, :] = h.astype(BF16)
            ht_ref[...] = h.T.astype(BF16)

        proj_ref[...] = _dot(hb_all[rows, :], wbuf[ci ^ s])

        @pl.when(jnp.logical_and(s == NCHIP - 1, t == nt - 1))
        def _():
            wg.relay_sems = (ys, yr)
            wg.finish_relayed()
            for k in range(NCHIP):
                keep(k).wait()

    vm = pl.BlockSpec(memory_space=pltpu.VMEM)
    first_pass = lambda s, t: jnp.where(s == 0, t, nt - 1)
    grid_spec = pltpu.PrefetchScalarGridSpec(
        num_scalar_prefetch=1, grid=(NCHIP, nt),
        in_specs=[vm, vm, vm, vm, vm, vm, vm, pl.BlockSpec((ts, D), lambda s, t, o: (first_pass(s, t), 0)),
                  pl.BlockSpec((1, D), lambda s, t, o: (0, 0))],
        out_specs=[vm, vm, vm, pl.BlockSpec(memory_space=pl.ANY), vm, vm,
                   pl.BlockSpec((ts, EC), lambda s, t, o: (t, o[s])),
                   pl.BlockSpec((D, ts), lambda s, t, o: (0, first_pass(s, t)))],
        scratch_shapes=[pltpu.VMEM((1, wc), F32),
                        pltpu.VMEM((NDEV, wc), F32), pltpu.VMEM((NDEV, EC), F32), pltpu.VMEM((NCHIP, 1, EC), F32),
                        pltpu.VMEM((NCHIP, D, EC), BF16), pltpu.VMEM((S, D), BF16),
                        pltpu.SemaphoreType.DMA((NDEV - 1,)), pltpu.SemaphoreType.DMA((NDEV - 1,)),
                        pltpu.SemaphoreType.DMA((NCHIP - 1,)), pltpu.SemaphoreType.DMA((NCHIP - 1,))]
        + _WeightGather.SEMS + [pltpu.SemaphoreType.DMA((2,))] * 3 + [pltpu.SemaphoreType.DMA((NCHIP,))])
    return pl.pallas_call(
        body, name="start_in_proj", grid_spec=grid_spec,
        out_shape=[jax.ShapeDtypeStruct((NDEV, wc), F32), jax.ShapeDtypeStruct((4, R), F32),
                   jax.ShapeDtypeStruct((1, 3 * D), F32),
                   jax.ShapeDtypeStruct((NCHIP, D, EC), BF16), jax.ShapeDtypeStruct((S, LANES), F32),
                   jax.ShapeDtypeStruct((S, LANES), F32), jax.ShapeDtypeStruct((S, E), F32),
                   jax.ShapeDtypeStruct((D, S), BF16)],
        compiler_params=_cp(("arbitrary", "arbitrary")),
    )(order, c, conv_w, w_ada, b_ada, w_in, pos, _rope_freq(), x, norm_pre)


class _ReduceScatter:
    @staticmethod
    def scratch(n_units, rows, ucols, max_owned):
        half = rows // 2
        return [pltpu.VMEM((n_units, half, ucols), F32), pltpu.VMEM((n_units, half, ucols), BF16),
                pltpu.VMEM((max_owned, NCHIP, half, ucols), BF16),
                pltpu.SemaphoreType.DMA((2,)), pltpu.SemaphoreType.DMA((n_units,)),
                pltpu.SemaphoreType.DMA((n_units, NCHIP)), pltpu.SemaphoreType.DMA((n_units,)),
                pltpu.SemaphoreType.DMA((n_units,))]

    def __init__(self, g_ref, out_ref, units, sib, stage, got, sem1, send2, recv2, send3, recv3):
        x, y, c = _me()
        self.c, self.ci = c, 2 * x + y
        self.g, self.out, self.units = g_ref, out_ref, units
        self.sib, self.stage, self.got = sib, stage, got
        self.sem1, self.send2, self.recv2, self.send3, self.recv3 = sem1, send2, recv2, send3, recv3
        self.half = g_ref.shape[1] // 2
        self.ucols = g_ref.shape[2]
        self.r0 = pl.multiple_of(c * self.half, self.half)
        self.r1 = pl.multiple_of((1 - c) * self.half, self.half)
        self.slot0 = units[0][0]
        assert [u[0] for u in units] == list(range(self.slot0, self.slot0 + len(units)))
        seen = {}
        self.local = []
        for _, owner, _ in units:
            self.local.append(seen.get(owner, 0))
            seen[owner] = seen.get(owner, 0) + 1

    def _halves(self):
        n = len(self.units)
        return _remote(self.g.at[pl.ds(self.slot0, n), pl.ds(self.r1, self.half), :], self.sib,
                       self.sem1.at[0], self.sem1.at[1], 1)

    def _partial(self, i, sender):
        _, owner, _ = self.units[i]
        return pltpu.make_async_remote_copy(
            src_ref=self.stage.at[i], dst_ref=self.got.at[self.local[i], sender],
            send_sem=self.send2.at[i], recv_sem=self.recv2.at[i, sender],
            device_id=(owner // 2, owner % 2, self.c), device_id_type=MESH)

    def _back(self, i, start):
        off = self.units[i][2]
        blk = self.out.at[pl.ds(start, self.half), off:off + self.ucols]
        return _remote(blk, blk, self.send3.at[i], self.recv3.at[i], 1)

    def start_halves(self):
        self._halves().start()

    def send_partials(self):
        self._halves().wait_recv()
        for i, (slot, owner, _) in enumerate(self.units):
            @pl.when(self.ci != owner)
            def _():
                self.stage[i] = (self.g[slot, pl.ds(self.r0, self.half), :] + self.sib[i]).astype(BF16)
                self._partial(i, self.ci).start()

    def reduce_owned(self):
        for i, (slot, owner, off) in enumerate(self.units):
            @pl.when(self.ci == owner)
            def _():
                rows, cols = pl.ds(self.r0, self.half), slice(off, off + self.ucols)
                self.out[rows, cols] = self.g[slot, pl.ds(self.r0, self.half), :] + self.sib[i]
                for s in range(NCHIP):
                    if s != owner:
                        self._partial(i, s).wait_recv()
                        self.out[rows, cols] += self.got[self.local[i], s].astype(F32)
                self._back(i, self.r0).start()

    def finish(self):
        self._halves().wait_send()
        for i, (_, owner, _) in enumerate(self.units):
            @pl.when(self.ci == owner)
            def _():
                self._back(i, self.r1).wait_recv()
                self._back(i, self.r0).wait_send()

            @pl.when(self.ci != owner)
            def _():
                self._partial(i, self.ci).wait_send()


def _silu_rows(c_ref):
    cv = c_ref[:, 0:D]
    sc = cv * _sigmoid(cv)
    return jnp.concatenate([sc, jnp.zeros_like(sc)], axis=0).astype(BF16)


def _adamw(groups):
    steps = 4
    specs = [pl.BlockSpec((w.shape[0] // steps, w.shape[1]), lambda i: (i, 0)) for w, _, _, _ in groups]

    def body(*refs):
        ins, outs = refs[:4 * len(groups)], refs[4 * len(groups):]
        for j in range(len(groups)):
            w_ref, g_ref, m_ref, v_ref = ins[4 * j:4 * j + 4]
            d_ref, nm_ref, nv_ref = outs[3 * j:3 * j + 3]
            d_ref[...], nm_ref[...], nv_ref[...] = _adamw_values(w_ref[...], g_ref[...], m_ref[...], v_ref[...])

    res = pl.pallas_call(
        body, name="adamw_big", grid=(steps,),
        in_specs=[s for s in specs for _ in range(4)], out_specs=[s for s in specs for _ in range(3)],
        out_shape=[jax.ShapeDtypeStruct(w.shape, F32) for w, _, _, _ in groups for _ in range(3)],
        compiler_params=_cp(("parallel",)),
    )(*[a for grp in groups for a in grp])
    return [res[3 * j:3 * j + 3] for j in range(len(groups))]


def _adamw_values(w, g, m, v):
    nm = B1 * m + (1.0 - B1) * g
    nv = B2 * v + (1.0 - B2) * (g * g)
    m_hat = nm / (1.0 - B1 ** STEP)
    v_hat = nv / (1.0 - B2 ** STEP)
    return (-LR) * (m_hat / (jnp.sqrt(v_hat) + ADAM_EPS) + WD * w), nm, nv


NB = R // HEAD
SMALL = (("b_ada", (1, 3 * D)), ("norm_pre", (1, D)), ("norm_post", (1, D)), ("conv_w", (4, R // NCHIP)),
         ("conv_b", (1, R)), ("w_rg_a", (NB, HEAD, HEAD)), ("b_rg_a", (1, R)), ("w_rg_x", (NB, HEAD, HEAD)),
         ("b_rg_x", (1, R)), ("lru_lambda", (1, R)), ("norm_rec", (1, R)), ("norm_att", (1, R)))


def _small_update(ao8, sm8, dwa8, dwx8, ai8, cg, params):
    n = len(SMALL)

    def body(ao_ref, sm_ref, dwa_ref, dwx_ref, ai_ref, cg_ref, *refs):
        pin, pout, (gada_ref, loss_ref, dmod) = refs[:3 * n], refs[3 * n:7 * n], refs[7 * n:]
        xx, yy, _ = _me()
        ci = 2 * xx + yy

        def total(ref, *idx):
            acc = ref[(0,) + idx].astype(F32)
            for d in range(1, NDEV):
                acc = acc + ref[(d,) + idx].astype(F32)
            return acc

        row = lambda ref, r, lanes=slice(None): total(ref, slice(r, r + 1), lanes)
        mine = lambda parts: sum(jnp.where(ci == j, part, 0.0) for j, part in enumerate(parts))
        cw = R // NCHIP
        grads = {
            "b_ada": [jnp.concatenate([row(ai_ref, 0), row(ai_ref, 1), row(ao_ref, 0)], axis=1)],
            "norm_pre": [row(ai_ref, 2)], "norm_post": [row(ao_ref, 1)],
            "conv_w": [mine([row(sm_ref, 8 + r, slice(j * cw, (j + 1) * cw)) for j in range(NCHIP)]) for r in range(4)],
            "conv_b": [row(sm_ref, 4)], "b_rg_a": [row(sm_ref, 0)], "b_rg_x": [row(sm_ref, 1)],
            "lru_lambda": [row(sm_ref, 2)], "norm_rec": [row(sm_ref, 3)], "norm_att": [row(ao_ref, 2, slice(0, R))],
            "w_rg_a": [total(dwa_ref, h) for h in range(NB)], "w_rg_x": [total(dwx_ref, h) for h in range(NB)],
        }
        loss_ref[...] = row(ao_ref, 3, slice(0, LANES)) * (0.5 / D)
        for k, (name, shape) in enumerate(SMALL):
            w_ref, m_ref, v_ref = pin[3 * k:3 * k + 3]
            outs = pout[4 * k:4 * k + 4]
            for r, g in enumerate(grads[name]):
                at = (slice(None),) if len(grads[name]) == 1 else ((r,) if len(shape) == 3 else (slice(r, r + 1),))
                res = (g,) + _adamw_values(w_ref[at], g, m_ref[at], v_ref[at])
                for o_ref, val in zip(outs, res):
                    o_ref[at] = val
        for d in range(NDEV):
            dmod[d:d + 1, :] = jnp.concatenate([ai_ref[d, 0:1, :], ai_ref[d, 1:2, :], ao_ref[d, 0:1, :]], axis=1)
        cols = mine([dmod[:, j * EC:(j + 1) * EC] for j in range(NCHIP)])
        colsb = jnp.concatenate([cols, jnp.zeros_like(cols)], axis=0).astype(BF16)
        gada_ref[...] = _dot_tn(_silu_rows(cg_ref), colsb)

    shapes = [jax.ShapeDtypeStruct(s, F32) for _, s in SMALL]
    outs = pl.pallas_call(
        body, name="small_update",
        out_shape=[s for s in shapes for _ in range(4)] + [jax.ShapeDtypeStruct((D, EC), F32),
                                                           jax.ShapeDtypeStruct((1, LANES), F32)],
        scratch_shapes=[pltpu.VMEM((NDEV, 3 * D), F32)],
        compiler_params=_cp(),
    )(ao8, sm8, dwa8, dwx8, ai8, cg, *params)
    return outs[:4 * n], outs[4 * n], outs[4 * n + 1]


BIG = ("w_ada", "w_in", "w_out")
WEIGHTS = ("w_ada", "b_ada", "norm_pre", "norm_post", "w_in", "conv_w", "conv_b", "w_rg_a", "b_rg_a", "w_rg_x",
           "b_rg_x", "lru_lambda", "norm_rec", "norm_att", "w_out")


def kernel(x, c, positions, w_ada, b_ada, norm_pre, norm_post, w_in, conv_w, conv_b, w_rg_a, b_rg_a, w_rg_x, b_rg_x, lru_lambda, norm_rec, norm_att, w_out, loss_target, m_w_ada, m_b_ada, m_norm_pre, m_norm_post, m_w_in, m_conv_w, m_conv_b, m_w_rg_a, m_b_rg_a, m_w_rg_x, m_b_rg_x, m_lru_lambda, m_norm_rec, m_norm_att, m_w_out, v_w_ada, v_b_ada, v_norm_pre, v_norm_post, v_w_in, v_conv_w, v_conv_b, v_w_rg_a, v_b_rg_a, v_w_rg_x, v_b_rg_x, v_lru_lambda, v_norm_rec, v_norm_att, v_w_out):
    given = dict(locals())
    wts = {n: given[n] for n in WEIGHTS}
    ms = {n: given["m_" + n] for n in WEIGHTS}
    vs = {n: given["v_" + n] for n in WEIGHTS}
    xi, yi, _ = _me()
    chip = 2 * xi + yi

    order = (chip ^ jnp.arange(NCHIP, dtype=jnp.int32)).astype(jnp.int32)
    cg, conv_full, mod, w_in_bf, cos, sin, proj, ht = _start_in_proj(
        c, conv_w[0], w_ada[0], b_ada, w_in[0], positions, x[0], norm_pre, order)

    p = dict(norm_pre=norm_pre, norm_post=norm_post, conv_b=conv_b, b_rg_a=b_rg_a, b_rg_x=b_rg_x,
             lru_lambda=lru_lambda, norm_rec=norm_rec, norm_att=norm_att, w_rg_a=w_rg_a[0], w_rg_x=w_rg_x[0])
    grad_x, g_in, g_out, gathered = _local_step(
        x[0], cos, sin, loss_target[0], mod, w_in_bf, proj, ht, w_out[0], conv_full, p)

    params = [d[n].reshape(shape) for n, shape in SMALL for d in (wts, ms, vs)]
    small_out, g_ada, loss_row = _small_update(*gathered, cg, params)
    grads = {"w_out": g_out, "w_in": g_in, "w_ada": g_ada}
    delta, new_m, new_v = {}, {}, {}
    for k, (n, _) in enumerate(SMALL):
        grads[n], delta[n], new_m[n], new_v[n] = small_out[4 * k:4 * k + 4]
    for n, res in zip(BIG, _adamw([(wts[n][0], grads[n], ms[n][0], vs[n][0]) for n in BIG])):
        delta[n], new_m[n], new_v[n] = res
    out = lambda d: [d[n].reshape(wts[n].shape) for n in WEIGHTS]
    return (loss_row[0, 0], grad_x.reshape(x.shape), *out(grads), *out(delta), *out(new_m), *out(new_v))
```

```python
import numpy as np
import jax
import jax.numpy as jnp
from jax import lax
from jax.experimental import pallas as pl
from jax.experimental.pallas import tpu as pltpu

F32 = jnp.float32
BF16 = jnp.bfloat16

S = 2048
D = 1024
E = 3072
R = 512
NDEV = 8
NCHIP = 4
EC = 768
LRU_C = 8.0
EPS = 1e-6
NEG = -1e30
HEAD = 64
BLK = 128
PATTERNS = (1, 4, 16)
ROPE_THETA = 10000.0
LANES = 128
VMEM_LIMIT = 56 * 1024 * 1024

B1, B2, LR, WD, ADAM_EPS, STEP = 0.9, 0.999, 0.001, 0.01, 1e-8, 10
MESH = pl.DeviceIdType.MESH


def _cp(sem=None, **kw):
    return pltpu.CompilerParams(dimension_semantics=sem, vmem_limit_bytes=VMEM_LIMIT, **kw)


def _dot(a, b):
    return jnp.dot(a, b, preferred_element_type=F32)


def _dot_nt(a, b):
    return lax.dot_general(a, b, (((1,), (1,)), ((), ())), preferred_element_type=F32)


def _dot_tn(a, b):
    return lax.dot_general(a, b, (((0,), (0,)), ((), ())), preferred_element_type=F32)


def _sigmoid(x):
    return 1.0 / (1.0 + jnp.exp(-x))


def _one_minus_exp(x, ex):
    poly = -x * (1.0 + x * (0.5 + x * (1.0 / 6 + x * (1.0 / 24))))
    return jnp.where(x > -1.0 / 16, poly, 1.0 - ex)


def _rms_fwd(v, g):
    rstd = lax.rsqrt(jnp.mean(v * v, axis=-1, keepdims=True) + EPS)
    vn = v * rstd
    return vn * g, vn, rstd


def _rms_bwd(dy, vn, rstd, g):
    dvn = dy * g
    dv = rstd * (dvn - vn * jnp.mean(dvn * vn, axis=-1, keepdims=True))
    return dv, jnp.sum(dy * vn, axis=0, keepdims=True)


RT = 256


def _shift_down(cur, prev8, j, row):
    if j == 0:
        return cur
    rolled = pltpu.roll(cur, j, 0)
    top = jnp.where(row[0:8] >= j, rolled[0:8], pltpu.roll(prev8, j, 0))
    return jnp.concatenate([top, rolled[8:]], axis=0)


def _shift_up(cur, next8, j, row):
    if j == 0:
        return cur
    rolled = pltpu.roll(cur, RT - j, 0)
    bot = jnp.where(row[RT - 8:] < RT - j, rolled[RT - 8:], pltpu.roll(next8, 8 - j, 0))
    return jnp.concatenate([rolled[:RT - 8], bot], axis=0)


def _rec_gates(xp, xprev8, row, cw_ref, cb_ref, wa_ref, ba_ref, wx_ref, bx_ref, lam_ref):
    xa = cb_ref[...] + sum(cw_ref[3 - j:4 - j, :] * _shift_down(xp, xprev8, j, row) for j in range(4))
    xab = xa.astype(BF16)
    r = _sigmoid(_dot(xab, wa_ref[...]) + ba_ref[...])
    ig = _sigmoid(_dot(xab, wx_ref[...]) + bx_ref[...])
    nl = -lam_ref[...]
    sp = jnp.maximum(nl, 0.0) + jnp.log1p(jnp.exp(-jnp.abs(nl)))
    la = (-LRU_C) * r * sp
    a = jnp.exp(la)
    mult = jnp.sqrt(_one_minus_exp(2.0 * la, a * a))
    return dict(xa=xa, xab=xab, r=r, ig=ig, sp=sp, la=la, a=a, mult=mult)


def _scan_fwd(a, u, row):
    sh = 1
    while sh < RT:
        a_s = jnp.where(row >= sh, pltpu.roll(a, sh, 0), 1.0)
        u_s = jnp.where(row >= sh, pltpu.roll(u, sh, 0), 0.0)
        u = a * u_s + u
        a = a * a_s
        sh *= 2
    return a, u


def _scan_bwd(al, g, row):
    sh = 1
    while sh < RT:
        al_s = jnp.where(row < RT - sh, pltpu.roll(al, RT - sh, 0), 1.0)
        g_s = jnp.where(row < RT - sh, pltpu.roll(g, RT - sh, 0), 0.0)
        g = g + al * g_s
        al = al * al_s
        sh *= 2
    return g


def _dense_from_blocks(blocks_ref, dense_ref):
    dense_ref[...] = jnp.zeros_like(dense_ref)
    for h in range(R // HEAD):
        dense_ref[h * HEAD:(h + 1) * HEAD, h * HEAD:(h + 1) * HEAD] = blocks_ref[h].astype(dense_ref.dtype)


def _rec_fwd(proj, conv_w, conv_b, wa_b, ba, wx_b, bx, lam, norm_rec):
    nt = S // RT

    def body(p_ref, cw_ref, cb_ref, wa_ref, ba_ref, wx_ref, bx_ref, lam_ref, nr_ref,
             h_ref, ya_ref, prev8, hc, wad, wxd):
        i = pl.program_id(0)

        @pl.when(i == 0)
        def _():
            prev8[...] = jnp.zeros_like(prev8)
            hc[...] = jnp.zeros_like(hc)
            _dense_from_blocks(wa_ref, wad)
            _dense_from_blocks(wx_ref, wxd)

        row = lax.broadcasted_iota(jnp.int32, (RT, R), 0)
        xp = p_ref[:, 0:R]
        ga = p_ref[:, R:2 * R]
        f = _rec_gates(xp, prev8[...], row, cw_ref, cb_ref, wad, ba_ref, wxd, bx_ref, lam_ref)
        u = f["mult"] * (f["ig"] * f["xa"])
        acum, hh = _scan_fwd(f["a"], u, row)
        h = hh + acum * hc[0:1, :]
        h_ref[...] = h
        hc[0:1, :] = h_ref[RT - 1:RT, :]
        prev8[...] = p_ref[RT - 8:RT, 0:R]
        yp = h * (ga * _sigmoid(ga))
        ya, _, _ = _rms_fwd(yp, nr_ref[...])
        ya_ref[...] = ya.astype(BF16)

    row1 = lambda n: pl.BlockSpec((1, n), lambda i: (0, 0))
    blocks = pl.BlockSpec((R // HEAD, HEAD, HEAD), lambda i: (0, 0, 0))
    return pl.pallas_call(
        body, name="rec_fwd", grid=(nt,),
        in_specs=[pl.BlockSpec((RT, 2 * R), lambda i: (i, 0)), pl.BlockSpec((4, R), lambda i: (0, 0)), row1(R),
                  blocks, row1(R), blocks, row1(R), row1(R), row1(R)],
        out_specs=[pl.BlockSpec((RT, R), lambda i: (i, 0)), pl.BlockSpec((RT, R), lambda i: (i, 0))],
        out_shape=[jax.ShapeDtypeStruct((S, R), F32), jax.ShapeDtypeStruct((S, R), BF16)],
        scratch_shapes=[pltpu.VMEM((8, R), F32), pltpu.VMEM((8, R), F32), pltpu.VMEM((R, R), BF16),
                        pltpu.VMEM((R, R), BF16)],
        compiler_params=_cp(("arbitrary",)),
    )(proj, conv_w, conv_b, wa_b, ba, wx_b, bx, lam, norm_rec)


def _rec_bwd(dproj, d_ya, proj, h_all, conv_w, conv_b, wa_b, ba, wx_b, bx, lam, norm_rec):
    nt = S // RT

    def body(dp_in, dya_ref, p_ref, pprev_ref, h_ref, hprev_ref, cw_ref, cb_ref, wab_ref, ba_ref, wxb_ref, bx_ref,
             lam_ref, nr_ref, dp_ref, dwab_ref, dwxb_ref, sm_ref, nxt8, cg, wa_ref, wx_ref, dwa_ref, dwx_ref):
        i = pl.program_id(0)
        ti = nt - 1 - i

        @pl.when(i == 0)
        def _():
            nxt8[...] = jnp.zeros_like(nxt8)
            cg[...] = jnp.zeros_like(cg)
            dwa_ref[...] = jnp.zeros_like(dwa_ref)
            dwx_ref[...] = jnp.zeros_like(dwx_ref)
            sm_ref[...] = jnp.zeros_like(sm_ref)
            _dense_from_blocks(wab_ref, wa_ref)
            _dense_from_blocks(wxb_ref, wx_ref)

        row = lax.broadcasted_iota(jnp.int32, (RT, R), 0)
        first = (ti > 0).astype(F32)
        xprev8 = pprev_ref[...] * first
        hprev8 = hprev_ref[...] * first
        xp = p_ref[:, 0:R]
        ga = p_ref[:, R:2 * R]
        f = _rec_gates(xp, xprev8, row, cw_ref, cb_ref, wa_ref, ba_ref, wx_ref, bx_ref, lam_ref)
        xa, r, ig, a, mult = f["xa"], f["r"], f["ig"], f["a"], f["mult"]
        h = h_ref[...]
        sg = _sigmoid(ga)
        gate = ga * sg
        yp = h * gate
        _, ypn, rstd = _rms_fwd(yp, nr_ref[...])
        d_yp, dnr = _rms_bwd(dya_ref[...], ypn, rstd, nr_ref[...])
        d_ga = d_yp * h * (sg * (1.0 + ga * (1.0 - sg)))
        dh = d_yp * gate + jnp.where(row == RT - 1, cg[0:1, :], 0.0)
        al = jnp.where(row < RT - 1, pltpu.roll(a, RT - 1, 0), 0.0)
        g = _scan_bwd(al, dh, row)
        cg[0:1, :] = jnp.sum(jnp.where(row == 0, a * g, 0.0), axis=0, keepdims=True)
        h_m1 = _shift_down(h, hprev8, 1, row)
        da = g * h_m1
        ix = ig * xa
        d_mult = g * ix
        d_ig = g * mult * xa
        d_xa = g * mult * ig
        d_la = da * a - d_mult * (a * a) / mult
        d_r = d_la * ((-LRU_C) * f["sp"])
        dsp = jnp.sum(d_la * ((-LRU_C) * r), axis=0, keepdims=True)
        dlam = dsp * (-_sigmoid(-lam_ref[...]))
        d_za = d_r * r * (1.0 - r)
        d_zx = d_ig * ig * (1.0 - ig)
        dzab = d_za.astype(BF16)
        dzxb = d_zx.astype(BF16)
        dwa_ref[...] += _dot_tn(f["xab"], dzab)
        dwx_ref[...] += _dot_tn(f["xab"], dzxb)
        d_xa = d_xa + _dot_nt(dzab, wa_ref[...]) + _dot_nt(dzxb, wx_ref[...])
        d_xp = sum(cw_ref[3 - j:4 - j, :] * _shift_up(d_xa, nxt8[...], j, row) for j in range(4))
        dcw = [jnp.sum(d_xa * _shift_down(xp, xprev8, 3 - k, row), axis=0, keepdims=True) for k in range(4)]
        dp_ref[:, 0:R] = d_xp.astype(BF16)
        dp_ref[:, R:2 * R] = d_ga.astype(BF16)
        dp8 = d_xa[0:8, :]
        nxt8[...] = dp8
        sm_ref[0:1, :] += jnp.sum(d_za, axis=0, keepdims=True)
        sm_ref[1:2, :] += jnp.sum(d_zx, axis=0, keepdims=True)
        sm_ref[2:3, :] += dlam
        sm_ref[3:4, :] += dnr
        sm_ref[4:5, :] += jnp.sum(d_xa, axis=0, keepdims=True)
        for k in range(4):
            sm_ref[8 + k:9 + k, :] += dcw[k]

        @pl.when(i == nt - 1)
        def _():
            for h in range(R // HEAD):
                dwab_ref[h] = dwa_ref[h * HEAD:(h + 1) * HEAD, h * HEAD:(h + 1) * HEAD].astype(BF16)
                dwxb_ref[h] = dwx_ref[h * HEAD:(h + 1) * HEAD, h * HEAD:(h + 1) * HEAD].astype(BF16)

    c0 = lambda shape: pl.BlockSpec(shape, lambda i: (0, 0))
    blocks = pl.BlockSpec((R // HEAD, HEAD, HEAD), lambda i: (0, 0, 0))
    rev = lambda i: nt - 1 - i
    prev8 = lambda i: (jnp.maximum((nt - 1 - i) * (RT // 8) - 1, 0), 0)
    return pl.pallas_call(
        body, name="rec_bwd", grid=(nt,),
        in_specs=[pl.BlockSpec(memory_space=pl.ANY),
                  pl.BlockSpec((RT, R), lambda i: (rev(i), 0)),
                  pl.BlockSpec((RT, 2 * R), lambda i: (rev(i), 0)), pl.BlockSpec((8, R), prev8),
                  pl.BlockSpec((RT, R), lambda i: (rev(i), 0)), pl.BlockSpec((8, R), prev8),
                  c0((4, R)), c0((1, R)), blocks, c0((1, R)), blocks, c0((1, R)), c0((1, R)), c0((1, R))],
        out_specs=[pl.BlockSpec((RT, 2 * R), lambda i: (rev(i), 0)), blocks, blocks, c0((16, R))],
        out_shape=[jax.ShapeDtypeStruct((S, E), BF16), jax.ShapeDtypeStruct((R // HEAD, HEAD, HEAD), BF16),
                   jax.ShapeDtypeStruct((R // HEAD, HEAD, HEAD), BF16), jax.ShapeDtypeStruct((16, R), F32)],
        scratch_shapes=[pltpu.VMEM((8, R), F32), pltpu.VMEM((8, R), F32), pltpu.VMEM((R, R), BF16),
                        pltpu.VMEM((R, R), BF16), pltpu.VMEM((R, R), F32), pltpu.VMEM((R, R), F32)],
        input_output_aliases={0: 0},
        compiler_params=_cp(("arbitrary",)),
    )(dproj, d_ya, proj, proj, h_all, h_all, conv_w, conv_b, wa_b, ba, wx_b, bx, lam, norm_rec)


NPAIR = R // LANES
QB, KB, VB, GB = 2 * R // LANES, 3 * R // LANES, 4 * R // LANES, 5 * R // LANES


def _rope_freq():
    half = HEAD // 2
    inv = np.float32(ROPE_THETA) ** (-(np.arange(half, dtype=np.float32) / np.float32(half)))
    return jnp.asarray(np.tile(inv.astype(np.float32), LANES // half)[None, :])


def _rot_half(x, first):
    return jnp.where(first, -pltpu.roll(x, LANES - HEAD // 2, 1), pltpu.roll(x, HEAD // 2, 1))


def _cos_sin(pos_ref, freq_ref):
    pos = jnp.broadcast_to(pos_ref[...].astype(F32), (LANES, S)).T
    ang = pos * freq_ref[...]
    return jnp.cos(ang), jnp.sin(ang)


SUB = 4


def _stages(d):
    assert d in (1, SUB, SUB * SUB)
    return d > SUB


def _strided_rows(src_ref, d, tmp):
    n = S // d
    if not _stages(d):
        for r in range(d):
            yield r * n, (src_ref[pl.ds(r, n, stride=d), :] if d > 1 else src_ref[...])
        return
    m = S // SUB
    for r in range(SUB):
        tmp[r * m:(r + 1) * m, :] = src_ref[pl.ds(r, m, stride=SUB), :]
    for r in range(SUB):
        for q in range(SUB):
            yield (r + SUB * q) * n, tmp[pl.ds(r * m + q, n, stride=SUB), :]


def _deint(src_ref, dst_ref, d, tmp):
    n = S // d
    for row0, v in _strided_rows(src_ref, d, tmp):
        dst_ref[row0:row0 + n, :] = v.astype(dst_ref.dtype)


def _reint(src_ref, dst_ref, d, accumulate, tmp):
    if _stages(d):
        n, m = S // d, S // SUB
        for r in range(SUB):
            for q in range(SUB):
                tmp[pl.ds(r * m + q, n, stride=SUB), :] = src_ref[(r + SUB * q) * n:(r + SUB * q + 1) * n, :]
        src_ref, d = tmp, SUB
    n = S // d
    for r in range(d):
        idx = (pl.ds(r, n, stride=d), slice(None)) if d > 1 else (slice(None), slice(None))
        v = src_ref[r * n:(r + 1) * n, :]
        if accumulate:
            dst_ref[idx] = dst_ref[idx] + v
        else:
            dst_ref[idx] = v


def _deint_heads(src_ref, dst0, dst1, d, tmp):
    n = S // d
    hm0 = lax.broadcasted_iota(jnp.int32, (n, LANES), 1) < HEAD
    for row0, v in _strided_rows(src_ref, d, tmp):
        dst0[row0:row0 + n, :] = jnp.where(hm0, v, 0.0).astype(BF16)
        dst1[row0:row0 + n, :] = jnp.where(hm0, 0.0, v).astype(BF16)


def _reint_prev(src_ref, dst_ref, d):
    n = S // d
    if n == BLK:
        return
    for r in range(d):
        idx = (pl.ds(r, n - BLK, stride=d), slice(None)) if d > 1 else (slice(0, n - BLK), slice(None))
        dst_ref[idx] = dst_ref[idx] + src_ref[r * n + BLK:(r + 1) * n, :]


def _pair_masks():
    qi = lax.broadcasted_iota(jnp.int32, (BLK, 2 * BLK), 0)
    ki = lax.broadcasted_iota(jnp.int32, (BLK, 2 * BLK), 1) & (BLK - 1)
    return ki <= qi, ki >= qi


def _two(ref0, ref1, st, axis):
    return jnp.concatenate([ref0[pl.ds(st, BLK), :], ref1[pl.ds(st, BLK), :]], axis=axis)


ATT_UNROLL = S // BLK


def _att_fwd(proj, cos, sin, w_out):
    def body(q_ref, k_ref, v_ref, cos_ref, sin_ref, w_ref, att_ref, qr_ref, kr_ref, lse_ref, wbf_ref,
             qd, kd0, kd1, vd0, vd1, od, ld, tmp, on, ln, pcs, wbuf, *wsems):
        wg = _WeightGather(w_ref, wbuf, *wsems)
        pl.when(pl.program_id(0) == 0)(wg.start)
        pl.when(pl.program_id(0) == 2)(wg.forward)
        lane = lax.broadcasted_iota(jnp.int32, (S, LANES), 1)
        first = (lane & (HEAD // 2)) == 0
        cos, sin = cos_ref[...], sin_ref[...]
        q = q_ref[...]
        k = k_ref[...]
        qr_ref[...] = (q * cos + _rot_half(q, first) * sin) * (HEAD ** -0.5)
        kr_ref[...] = k * cos + _rot_half(k, first) * sin
        hm0 = lax.broadcasted_iota(jnp.int32, (BLK, LANES), 1) < HEAD
        top = lax.broadcasted_iota(jnp.int32, (2 * BLK, LANES), 0) < BLK
        ones2 = (top == (lax.broadcasted_iota(jnp.int32, (2 * BLK, LANES), 1) < HEAD)).astype(BF16)
        mc2, mp2 = _pair_masks()

        for pi, d in enumerate(PATTERNS):
            nb = S // d // BLK
            _deint(qr_ref, qd, d, tmp)
            _deint_heads(kr_ref, kd0, kd1, d, tmp)
            _deint_heads(v_ref, vd0, vd1, d, tmp)

            def blk(b, carry):
                st = pl.multiple_of(b * BLK, BLK)
                qb = qd[pl.ds(st, BLK), :]
                sc = jnp.where(mc2, _dot_nt(qb, _two(kd0, kd1, st, 0)), NEG)
                mx = sc
                if nb > 1:
                    stp = pl.multiple_of(jnp.maximum(b - 1, 0) * BLK, BLK)
                    mp = jnp.logical_and(mp2, lax.rem(b, nb) != 0)
                    sp = jnp.where(mp, _dot_nt(qb, _two(kd0, kd1, stp, 0)), NEG)
                    mx = jnp.maximum(sc, sp)
                m0 = jnp.max(mx[:, 0:BLK], axis=1, keepdims=True)
                m1 = jnp.max(mx[:, BLK:2 * BLK], axis=1, keepdims=True)
                mf = jnp.concatenate([jnp.broadcast_to(m0, (BLK, BLK)), jnp.broadcast_to(m1, (BLK, BLK))], axis=1)
                pcs[b, 0] = jnp.exp(sc - mf).astype(BF16)
                if nb > 1:
                    pcs[b, 1] = jnp.exp(sp - mf).astype(BF16)
                ld[pl.ds(st, BLK), :] = jnp.where(hm0, m0, m1)
                return carry

            lax.fori_loop(0, S // BLK, blk, 0, unroll=ATT_UNROLL)

            def prods(b, carry):
                st = pl.multiple_of(b * BLK, BLK)
                o = _dot(pcs[b, 0], jnp.concatenate([_two(vd0, vd1, st, 0), ones2], axis=1))
                if nb > 1:
                    stp = pl.multiple_of(jnp.maximum(b - 1, 0) * BLK, BLK)
                    o = o + _dot(pcs[b, 1], jnp.concatenate([_two(vd0, vd1, stp, 0), ones2], axis=1))
                l = o[:, LANES:2 * LANES]
                od[pl.ds(st, BLK), :] = o[:, 0:LANES] / l
                ld[pl.ds(st, BLK), :] += jnp.log(l)
                return carry

            lax.fori_loop(0, S // BLK, prods, 0, unroll=ATT_UNROLL)
            _reint(od, on.at[pi], d, False, tmp)
            _reint(ld, ln.at[pi], d, False, tmp)

        l0, l1, l2 = ln[0], ln[1], ln[2]
        m = jnp.maximum(jnp.maximum(l0, l1), l2)
        e0, e1, e2 = jnp.exp(l0 - m), jnp.exp(l1 - m), jnp.exp(l2 - m)
        den = e0 + e1 + e2
        att_ref[...] = (e0 * on[0] + e1 * on[1] + e2 * on[2]) / den
        lse_ref[...] = m + jnp.log(den)

        @pl.when(pl.program_id(0) == NPAIR - 1)
        def _():
            wg.finish()
            wbf_ref[...] = wbuf[...]

    col = lambda c0: pl.BlockSpec((S, LANES), lambda p: (0, c0 + p))
    out = pl.BlockSpec((S, LANES), lambda p: (0, p))
    tab = pl.BlockSpec((S, LANES), lambda p: (0, 0))
    vm = pl.BlockSpec(memory_space=pltpu.VMEM)
    return pl.pallas_call(
        body, name="att_fwd", grid=(NPAIR,),
        in_specs=[col(QB), col(KB), col(VB), tab, tab, vm],
        out_specs=[out, out, out, out, vm],
        out_shape=[jax.ShapeDtypeStruct((S, R), F32)] * 4 + [jax.ShapeDtypeStruct((NCHIP,) + w_out.shape, BF16)],
        scratch_shapes=[pltpu.VMEM((S, LANES), BF16)] * 5 + [pltpu.VMEM((S, LANES), F32)] * 3
        + [pltpu.VMEM((3, S, LANES), F32)] * 2 + [pltpu.VMEM((S // BLK, 2, BLK, 2 * BLK), BF16)]
        + [pltpu.VMEM((NCHIP,) + w_out.shape, BF16)] + _WeightGather.SEMS,
        compiler_params=_cp(("arbitrary",)),
    )(proj, proj, proj, cos, sin, w_out)


def _att_bwd(dproj, d_att, att, lse, qr, kr, proj, cos, sin, gw_out4):
    out_units = [(j, j, 0) for j in range(NCHIP)]

    nblk = S // BLK

    def body(dp_in, do_ref, o_ref, lse_ref, qr_ref, kr_ref, v_ref, cos_ref, sin_ref, gw_ref, dp_ref, gout_ref,
             qd, kd0, kd1, vd0, vd1, dod, kt, packn, packd, dqd, dkcd, dkpd, dvcd, dvpd,
             dqn, dkn, dvn, tmp, rows, trs, pts, dss, stage, sems, gred, *rs_scratch):
        p = pl.program_id(0)
        rs = _ReduceScatter(gw_ref, gred, out_units, *rs_scratch)
        for step, piece in enumerate((rs.start_halves, rs.send_partials, rs.reduce_owned)):
            pl.when(p == step)(piece)

        @pl.when(p == NPAIR - 1)
        def _():
            rs.finish()
            gout_ref[...] = gred[...]

        lane = lax.broadcasted_iota(jnp.int32, (S, LANES), 1)
        hms = lane < HEAD
        prod = do_ref[...] * o_ref[...]
        d0 = jnp.sum(jnp.where(hms, prod, 0.0), axis=1, keepdims=True)
        d1 = jnp.sum(jnp.where(hms, 0.0, prod), axis=1, keepdims=True)
        lse = lse_ref[...]
        quarter = HEAD // 2
        packn[...] = jnp.where(lane < quarter, lse,
                               jnp.where(hms, pltpu.roll(lse, LANES - quarter, 1), jnp.where(lane < 3 * quarter, d0, d1)))
        hm0 = lax.broadcasted_iota(jnp.int32, (BLK, LANES), 1) < HEAD
        key = lax.broadcasted_iota(jnp.int32, (2 * BLK, BLK), 0) & (BLK - 1)
        qry = lax.broadcasted_iota(jnp.int32, (2 * BLK, BLK), 1)
        mct, mpt = key <= qry, key >= qry

        for d in PATTERNS:
            nb = S // d // BLK
            _deint(qr_ref, qd, d, tmp)
            _deint_heads(kr_ref, kd0, kd1, d, tmp)
            _deint_heads(v_ref, vd0, vd1, d, tmp)
            _deint(do_ref, dod, d, tmp)
            _deint(packn, packd, d, tmp)

            sides = (0, 1) if nb > 1 else (0,)

            def probs(b, carry):
                st = pl.multiple_of(b * BLK, BLK)
                kt[b] = _two(kd0, kd1, st, 0).astype(F32).T.astype(BF16)
                trs[b] = packd[pl.ds(st, BLK), :].T
                for j in range(4):
                    rows[b, j:j + 1, :] = trs[b, j * quarter:j * quarter + 1, :]
                qb, dob = qd[pl.ds(st, BLK), :], dod[pl.ds(st, BLK), :]
                both = lambda j: jnp.concatenate([jnp.broadcast_to(rows[b, j:j + 1, :], (BLK, BLK)),
                                                  jnp.broadcast_to(rows[b, j + 1:j + 2, :], (BLK, BLK))], axis=0)
                lbt, dlt = both(0), both(2)
                for sd in sides:
                    stk = pl.multiple_of(jnp.maximum(b - sd, 0) * BLK, BLK)
                    mask = mct if sd == 0 else jnp.logical_and(mpt, lax.rem(b, nb) != 0)
                    k2, v2 = _two(kd0, kd1, stk, 0), _two(vd0, vd1, stk, 0)
                    pt = jnp.where(mask, jnp.exp(_dot_nt(k2, qb) - lbt), 0.0)
                    pts[b, sd] = pt.astype(BF16)
                    dss[b, sd] = (pt * (_dot_nt(v2, dob) - dlt)).astype(BF16)
                return carry

            lax.fori_loop(0, nblk, probs, 0, unroll=ATT_UNROLL)

            def prods(b, carry):
                st = pl.multiple_of(b * BLK, BLK)
                qb, dob = qd[pl.ds(st, BLK), :], dod[pl.ds(st, BLK), :]
                dq_t = None
                for sd in sides:
                    dst, ptb = dss[b, sd], pts[b, sd]
                    rk, rv = _dot(dst, qb), _dot(ptb, dob)
                    dqs = _dot(kt[jnp.maximum(b - sd, 0)], dst)
                    dq_t = dqs if dq_t is None else dq_t + dqs
                    dk, dv = (dkcd, dvcd) if sd == 0 else (dkpd, dvpd)
                    dk[pl.ds(st, BLK), :] = jnp.where(hm0, rk[0:BLK], rk[BLK:2 * BLK])
                    dv[pl.ds(st, BLK), :] = jnp.where(hm0, rv[0:BLK], rv[BLK:2 * BLK])
                dqd[pl.ds(st, BLK), :] = dq_t.T
                return carry

            lax.fori_loop(0, nblk, prods, 0, unroll=ATT_UNROLL)
            later = d != PATTERNS[0]
            _reint(dqd, dqn, d, later, tmp)
            _reint(dkcd, dkn, d, later, tmp)
            _reint(dvcd, dvn, d, later, tmp)
            _reint_prev(dkpd, dkn, d)
            _reint_prev(dvpd, dvn, d)

        lane = lax.broadcasted_iota(jnp.int32, (S, LANES), 1)
        first = (lane & (HEAD // 2)) == 0
        cos, sin = cos_ref[...], sin_ref[...]
        dq = dqn[...] * (HEAD ** -0.5)
        dk = dkn[...]
        stage[0] = (dq * cos - _rot_half(dq, first) * sin).astype(BF16)
        stage[1] = (dk * cos - _rot_half(dk, first) * sin).astype(BF16)
        stage[2] = dvn[...].astype(BF16)
        copies = [pltpu.make_async_copy(stage.at[j], dp_ref.at[:, pl.ds((2 + j) * R + p * LANES, LANES)], sems.at[j])
                  for j in range(3)]
        for cp in copies:
            cp.start()
        for cp in copies:
            cp.wait()

    blk = pl.BlockSpec((S, LANES), lambda p: (0, p))
    tab = pl.BlockSpec((S, LANES), lambda p: (0, 0))
    vm = pl.BlockSpec(memory_space=pltpu.VMEM)
    _, orows, ocols = gw_out4.shape
    return pl.pallas_call(
        body, name="att_bwd", grid=(NPAIR,),
        in_specs=[pl.BlockSpec(memory_space=pl.ANY), blk, blk, blk, blk, blk,
                  pl.BlockSpec((S, LANES), lambda p: (0, VB + p)), tab, tab, vm],
        out_specs=[pl.BlockSpec(memory_space=pl.ANY), vm],
        out_shape=[jax.ShapeDtypeStruct((S, E), BF16), jax.ShapeDtypeStruct((orows, ocols), F32)],
        scratch_shapes=[pltpu.VMEM((S, LANES), BF16)] * 6 + [pltpu.VMEM((nblk, LANES, 2 * BLK), BF16)]
        + [pltpu.VMEM((S, LANES), F32)] * 11
        + [pltpu.VMEM((nblk, 8, BLK), F32), pltpu.VMEM((nblk, LANES, BLK), F32)]
        + [pltpu.VMEM((nblk, 2, 2 * BLK, BLK), BF16)] * 2
        + [pltpu.VMEM((3, S, LANES), BF16), pltpu.SemaphoreType.DMA((3,)), pltpu.VMEM((orows, ocols), F32)]
        + _ReduceScatter.scratch(NCHIP, orows, ocols, 1),
        input_output_aliases={0: 0},
        compiler_params=_cp(("arbitrary",)),
    )(dproj, d_att, att, lse, qr, kr, proj, cos, sin, gw_out4)


def _out_fwd_bwd(ya, att, proj, w_out_bf, x, target, mod, norm_post, norm_att):
    ts = 512

    def body(ya_ref, att_ref, gb_ref, w_ref, x_ref, t_ref, mod_ref, npost_ref, natt_ref,
             gx_ref, dya_ref, datt_ref, dgb_ref, gw_ref, acc_ref):
        i = pl.program_id(0)

        @pl.when(i == 0)
        def _():
            gw_ref[...] = jnp.zeros_like(gw_ref)
            acc_ref[...] = jnp.zeros_like(acc_ref)

        gate = mod_ref[:, 2 * D:3 * D]
        att = att_ref[...]
        gb = gb_ref[...]
        sg = _sigmoid(gb)
        silu = gb * sg
        ybp = att * silu
        yb, ybn, rstd_b = _rms_fwd(ybp, natt_ref[...])
        cat = jnp.concatenate([ya_ref[...], yb.astype(BF16)], axis=1)
        mix = _dot(cat, w_ref[...])
        rn, mn, rstd_m = _rms_fwd(mix, npost_ref[...])
        err = x_ref[...] + gate * rn - t_ref[...]
        dy = err * (1.0 / D)
        gx_ref[...] = dy
        dmix, dnpost = _rms_bwd(dy * gate, mn, rstd_m, npost_ref[...])
        dmb = dmix.astype(BF16)
        gw_ref[...] += _dot_tn(cat, dmb)
        dcat = _dot_nt(dmb, w_ref[...])
        dya_ref[...] = dcat[:, 0:R]
        dybp, dnatt = _rms_bwd(dcat[:, R:2 * R], ybn, rstd_b, natt_ref[...])
        datt_ref[...] = dybp * silu
        dgb_ref[...] = (dybp * att * (sg * (1.0 + gb * (1.0 - sg)))).astype(BF16)
        acc_ref[0:1, :] += jnp.sum(dy * rn, axis=0, keepdims=True)
        acc_ref[1:2, :] += dnpost
        acc_ref[2:3, 0:R] += dnatt
        acc_ref[3:4, :] += jnp.sum(jnp.sum(err * err, axis=1, keepdims=True), axis=0, keepdims=True)

    tile = lambda w: pl.BlockSpec((ts, w), lambda i: (i, 0))
    c0 = lambda shape: pl.BlockSpec(shape, lambda i: (0, 0))
    return pl.pallas_call(
        body, name="out_fwd_bwd", grid=(S // ts,),
        in_specs=[tile(R), tile(R), pl.BlockSpec((ts, R), lambda i: (i, 5)), c0((D, D)), tile(D), tile(D),
                  c0((1, 3 * D)), c0((1, D)), c0((1, R))],
        out_specs=[tile(D), tile(R), tile(R), pl.BlockSpec((ts, R), lambda i: (i, 5)), c0((D, D)), c0((8, D))],
        out_shape=[jax.ShapeDtypeStruct((S, D), F32), jax.ShapeDtypeStruct((S, R), F32),
                   jax.ShapeDtypeStruct((S, R), F32), jax.ShapeDtypeStruct((S, E), BF16),
                   jax.ShapeDtypeStruct((D, D), F32), jax.ShapeDtypeStruct((8, D), F32)],
        compiler_params=_cp(("arbitrary",)),
    )(ya, att, proj, w_out_bf, x, target, mod, norm_post, norm_att)


UC = 256
UPC = EC // UC


NU = E // UC


def _unit_of_step(i):
    return (i % NCHIP) * UPC + i // NCHIP


def _in_proj_bwd(ht, dproj, w_in_bf, x, gx1, mod, norm_pre, smalls):
    ts = 256
    nt = S // ts
    half = D // 2
    units = [_unit_of_step(k) for k in range(NU)]
    owners = [u // UPC for u in units]
    ns = len(smalls)

    def body(*refs):
        (ht_ref, dpu_ref, dp_ref, w_hbm, x_ref, gx1_ref, mod_ref, np_ref), refs = refs[:8], refs[8:]
        small_in, refs = refs[:ns], refs[ns:]
        (gx_ref, gin_ref), refs = refs[:2], refs[2:]
        small_out, (acc_out,), refs = refs[:ns], refs[ns:ns + 1], refs[ns + 1:]
        mine, sib, tmp, stage, got, red, acc_ref, hs, hr, ps, pr, bs, br = refs[:13]
        early = _SmallGather(small_in, small_out, *refs[13:16])
        late = _SmallGather([acc_ref], [acc_out], *refs[16:19])
        w_ref, w_sem = refs[19:21]
        i = pl.program_id(0)
        w_copy = pltpu.make_async_copy(w_hbm, w_ref, w_sem)
        pl.when(i == 0)(w_copy.start)
        pl.when(i == NU)(w_copy.wait)
        xx, yy, c = _me()
        ci = 2 * xx + yy
        r0 = pl.multiple_of(c * half, half)
        r1 = pl.multiple_of((1 - c) * half, half)
        pl.when(i == 0)(early.start)
        pl.when(i == NU)(early.forward)

        def exch(k):
            return _remote(tmp.at[k % 2], sib.at[k], hs.at[k], hr.at[k], 1)

        def partial(k, sender):
            return pltpu.make_async_remote_copy(
                src_ref=stage.at[k], dst_ref=got.at[units[k] % UPC, sender], send_sem=ps.at[k],
                recv_sem=pr.at[k, sender], device_id=(owners[k] // 2, owners[k] % 2, c), device_id_type=MESH)

        def back(k, start):
            off = (units[k] % UPC) * UC
            blk = red.at[pl.ds(start, half), off:off + UC]
            return _remote(blk, blk, bs.at[k], br.at[k], 1)

        for k in range(NU + 1):
            @pl.when(i == k)
            def _():
                if k < NU:
                    if k >= 2:
                        exch(k - 2).wait_send()
                    dpu = dpu_ref[...]
                    tmp[k % 2] = _dot(ht_ref[pl.ds(r1, half), :], dpu)
                    exch(k).start()
                    mine[k] = _dot(ht_ref[pl.ds(r0, half), :], dpu)
                if k >= 1:
                    exch(k - 1).wait_recv()
                    mine[k - 1] += sib[k - 1]

                    @pl.when(ci != owners[k - 1])
                    def _():
                        stage[k - 1] = mine[k - 1].astype(BF16)
                        partial(k - 1, ci).start()

        @pl.when(i == NU)
        def _():
            acc_ref[...] = jnp.zeros_like(acc_ref)

        @pl.when(i >= NU)
        def _():
            dh = sum(_dot_nt(dp_ref[:, j * EC:(j + 1) * EC], w_ref[j]) for j in range(NCHIP))
            hp, xn, rstd = _rms_fwd(x_ref[...], np_ref[...])
            dx, dnp = _rms_bwd(dh * (1.0 + mod_ref[:, D:2 * D]), xn, rstd, np_ref[...])
            gx_ref[...] = gx1_ref[...] + dx
            acc_ref[0:1, :] += jnp.sum(dh, axis=0, keepdims=True)
            acc_ref[1:2, :] += jnp.sum(dh * hp, axis=0, keepdims=True)
            acc_ref[2:3, :] += dnp

        for t in range(UPC):
            @pl.when(i == NU + 1 + 2 * t)
            def _():
                for k in range(NCHIP * t, NCHIP * (t + 1)):
                    @pl.when(ci == owners[k])
                    def _():
                        off = (units[k] % UPC) * UC
                        red[pl.ds(r0, half), off:off + UC] = mine[k]
                        for s in range(NCHIP):
                            if s != owners[k]:
                                partial(k, s).wait_recv()
                                red[pl.ds(r0, half), off:off + UC] += got[units[k] % UPC, s].astype(F32)
                        back(k, r0).start()

        @pl.when(i == NU + nt - 1)
        def _():
            late.start()
            exch(NU - 2).wait_send()
            exch(NU - 1).wait_send()
            for k in range(NU):
                @pl.when(ci == owners[k])
                def _():
                    back(k, r1).wait_recv()
                    back(k, r0).wait_send()

                @pl.when(ci != owners[k])
                def _():
                    partial(k, ci).wait_send()
            gin_ref[...] = red[...]
            early.finish()
            late.forward()
            late.finish()

    tile = lambda w: pl.BlockSpec((ts, w), lambda i: (jnp.maximum(i - NU, 0), 0))
    c0 = lambda shape: pl.BlockSpec(shape, lambda i: (0, 0))
    vm = pl.BlockSpec(memory_space=pltpu.VMEM)
    hbm = pl.BlockSpec(memory_space=pl.ANY)
    gathered = [jax.ShapeDtypeStruct((NDEV,) + a.shape, a.dtype) for a in smalls] + [jax.ShapeDtypeStruct((NDEV, 8, D), F32)]
    return pl.pallas_call(
        body, name="in_proj_bwd", grid=(NU + nt,),
        in_specs=[vm, pl.BlockSpec((S, UC), lambda i: (0, _unit_of_step(jnp.minimum(i, NU - 1)))), tile(E),
                  hbm, tile(D), tile(D), c0((1, 3 * D)), c0((1, D))] + [vm] * ns,
        out_specs=[tile(D), vm] + [hbm] * (ns + 1),
        out_shape=[jax.ShapeDtypeStruct((S, D), F32), jax.ShapeDtypeStruct((D, EC), F32)] + gathered,
        scratch_shapes=[pltpu.VMEM((NU, half, UC), F32), pltpu.VMEM((NU, half, UC), F32),
                        pltpu.VMEM((2, half, UC), F32), pltpu.VMEM((NU, half, UC), BF16),
                        pltpu.VMEM((UPC, NCHIP, half, UC), BF16), pltpu.VMEM((D, EC), F32), pltpu.VMEM((8, D), F32),
                        pltpu.SemaphoreType.DMA((NU,)), pltpu.SemaphoreType.DMA((NU,)),
                        pltpu.SemaphoreType.DMA((NU,)), pltpu.SemaphoreType.DMA((NU, NCHIP)),
                        pltpu.SemaphoreType.DMA((NU,)), pltpu.SemaphoreType.DMA((NU,))]
        + _SmallGather.sems(ns) + _SmallGather.sems(1)
        + [pltpu.VMEM((NCHIP, D, EC), BF16), pltpu.SemaphoreType.DMA],
        compiler_params=_cp(("arbitrary",)),
    )(ht, dproj, dproj, w_in_bf, x, gx1, mod, norm_pre, *smalls)


def _local_step(x, cos, sin, target, mod, w_in_bf, proj, ht, w_out, conv_w, p):
    rec_p = (conv_w, p["conv_b"], p["w_rg_a"], p["b_rg_a"], p["w_rg_x"], p["b_rg_x"], p["lru_lambda"], p["norm_rec"])
    h_all, ya = _rec_fwd(proj, *rec_p)
    att, qr, kr, lse, w_out_bf = _att_fwd(proj, cos, sin, w_out)
    gx1, d_ya, d_att, dproj, gw_out, acc_o = _out_fwd_bwd(ya, att, proj, w_out_bf.reshape(D, D), x, target, mod,
                                                           p["norm_post"], p["norm_att"])
    dproj, g_out = _att_bwd(dproj, d_att, att, lse, qr, kr, proj, cos, sin, gw_out.reshape(NCHIP, D // NCHIP, D))
    dproj, dwa, dwx, sm = _rec_bwd(dproj, d_ya, proj, h_all, *rec_p)
    grad_x, g_in, *gathered = _in_proj_bwd(ht, dproj, w_in_bf, x, gx1, mod, p["norm_pre"], [acc_o, sm, dwa, dwx])
    return grad_x, g_in, g_out, gathered


def _me():
    return lax.axis_index("x"), lax.axis_index("y"), lax.axis_index("c")


def _flip(v, bit):
    return 1 - v if bit else v


def _peer(rel):
    x, y, c = _me()
    return (_flip(x, rel & 4), _flip(y, rel & 2), _flip(c, rel & 1))


def _remote(src, dst, send_sem, recv_sem, rel):
    return pltpu.make_async_remote_copy(src_ref=src, dst_ref=dst, send_sem=send_sem, recv_sem=recv_sem,
                                        device_id=_peer(rel), device_id_type=MESH)


class _WeightGather:
    SEMS = [pltpu.SemaphoreType.DMA((NCHIP - 1,))] * 4

    def __init__(self, w_ref, out_ref, send_sems, recv_sems, fsend_sems, frecv_sems):
        x, y, c = _me()
        self.w, self.out, self.ci = w_ref, out_ref, 2 * x + y
        self.half = w_ref.shape[0] // 2
        self.r0 = pl.multiple_of(c * self.half, self.half)
        self.r1 = pl.multiple_of((1 - c) * self.half, self.half)
        self.sems = (send_sems, recv_sems, fsend_sems, frecv_sems)

    def _ici(self, chip, k):
        blk = self.out.at[chip, pl.ds(self.r0, self.half), :]
        return _remote(blk, blk, self.sems[0].at[k - 1], self.sems[1].at[k - 1], 2 * k)

    def _d2d(self, chip, start, k):
        blk = self.out.at[chip, pl.ds(start, self.half), :]
        return _remote(blk, blk, self.sems[2].at[k - 1], self.sems[3].at[k - 1], 1)

    def start(self, diagonal=True):
        self.out[self.ci] = self.w[...].astype(BF16)
        for k in range(1, NCHIP if diagonal else NCHIP - 1):
            self._ici(self.ci, k).start()

    def _relay(self, chip, piece, k):
        q = self.half // 2
        blk = self.out.at[chip, pl.ds(self.r0 + piece * q, q), :]
        return _remote(blk, blk, self.relay_sems[0].at[piece], self.relay_sems[1].at[piece], 2 * k)

    def neighbours_landed(self, relay_send_sems, relay_recv_sems):
        self.relay_sems = (relay_send_sems, relay_recv_sems)
        for k in (1, 2):
            self._ici(self.ci ^ k, k).wait_recv()
        self._relay(self.ci ^ 2, 0, 1).start()
        self._relay(self.ci ^ 1, 1, 2).start()
        for k in (1, 2):
            self._d2d(self.ci ^ k, self.r0, k).start()

    def sibling_landed(self, k):
        self._d2d(self.ci ^ k, self.r1, k).wait_recv()

    def diagonal_landed(self):
        for piece, k in ((0, 1), (1, 2)):
            self._relay(self.ci ^ 3, piece, k).wait_recv()
        self._d2d(self.ci ^ 3, self.r0, 3).start()
        self._d2d(self.ci ^ 3, self.r1, 3).wait_recv()

    def finish_relayed(self):
        for k in (1, 2):
            self._ici(self.ci, k).wait_send()
        self._relay(self.ci ^ 2, 0, 1).wait_send()
        self._relay(self.ci ^ 1, 1, 2).wait_send()
        for k in range(1, NCHIP):
            self._d2d(self.ci ^ k, self.r0, k).wait_send()

    def forward(self):
        for k in range(1, NCHIP):
            self._ici(self.ci ^ k, k).wait_recv()
            self._d2d(self.ci ^ k, self.r0, k).start()

    def finish(self):
        for k in range(1, NCHIP):
            self._d2d(self.ci ^ k, self.r1, k).wait_recv()
        self.finish_sends()

    def finish_sends(self):
        for k in range(1, NCHIP):
            self._ici(self.ci, k).wait_send()
            self._d2d(self.ci ^ k, self.r0, k).wait_send()


class _SmallGather:
    @staticmethod
    def sems(n):
        return [pltpu.SemaphoreType.DMA((n, 7)), pltpu.SemaphoreType.DMA((n, 7)), pltpu.SemaphoreType.DMA((n,))]

    def __init__(self, srcs, outs, send_sems, recv_sems, local_sems):
        x, y, c = _me()
        self.srcs, self.outs = list(srcs), list(outs)
        self.ss, self.rs, self.ls = send_sems, recv_sems, local_sems
        self.ci, self.c = 2 * x + y, c
        self.me = 2 * self.ci + c

    def _own(self, a, slot, rel):
        return _remote(self.srcs[a], self.outs[a].at[self.me], self.ss.at[a, slot], self.rs.at[a, slot], rel)

    def _block(self, a, idx, slot, rel):
        blk = self.outs[a].at[idx]
        return _remote(blk, blk, self.ss.at[a, slot], self.rs.at[a, slot], rel)

    def _local(self, a):
        return pltpu.make_async_copy(self.srcs[a], self.outs[a].at[self.me], self.ls.at[a])

    def start(self):
        for a in range(len(self.srcs)):
            self._local(a).start()
            self._own(a, 0, 1).start()
            for k in range(1, NCHIP):
                self._own(a, k, 2 * k).start()

    def forward(self):
        for a in range(len(self.srcs)):
            for k in range(1, NCHIP):
                idx = 2 * (self.ci ^ k) + self.c
                self._block(a, idx, k, 2 * k).wait_recv()
                self._block(a, idx, 3 + k, 1).start()

    def finish(self):
        for a in range(len(self.srcs)):
            self._block(a, 2 * self.ci + 1 - self.c, 0, 1).wait_recv()
            for k in range(1, NCHIP):
                self._block(a, 2 * (self.ci ^ k) + 1 - self.c, 3 + k, 1).wait_recv()
            self._own(a, 0, 1).wait_send()
            for k in range(1, NCHIP):
                self._own(a, k, 2 * k).wait_send()
                self._block(a, 2 * (self.ci ^ k) + self.c, 3 + k, 1).wait_send()
            self._local(a).wait()


def _start_in_proj(c, conv_w, w_ada, b_ada, w_in, pos, x, norm_pre, order):
    ts = 512
    nt = S // ts
    wc = D + conv_w.size

    def body(order_ref, c_ref, cw_ref, wada_ref, b_ref, win_ref, pos_ref, freq_ref, x_ref, np_ref,
             g0_ref, conv_ref, mod_ref, wbf_ref, cos_ref, sin_ref, proj_ref, ht_ref,
             crow_ref, g0s, modp, modb, wbuf, hb_all, cs, cr, ms, mr, ws, wr, fs, fr, local_sems, ys, yr, osem):
        s, t = pl.program_id(0), pl.program_id(1)
        x, y, c = _me()
        ci = 2 * x + y
        me = 2 * ci + c
        wg = _WeightGather(win_ref, wbuf, ws, wr, fs, fr)
        cw = R // NCHIP

        @pl.when(jnp.logical_and(s == 0, t == 0))
        def _():
            wg.start(diagonal=False)
            crow_ref[:, 0:D] = c_ref[...]
            for k in range(4):
                crow_ref[:, D + k * cw:D + (k + 1) * cw] = cw_ref[k:k + 1, :]
            mine = pltpu.make_async_copy(crow_ref, g0s.at[pl.ds(me, 1), :], local_sems.at[0])
            mine.start()
            csend = [_remote(crow_ref, g0s.at[pl.ds(me, 1), :], cs.at[r - 1], cr.at[r - 1], r) for r in range(1, NDEV)]
            for cp in csend:
                cp.start()
            cos_ref[...], sin_ref[...] = _cos_sin(pos_ref, freq_ref)
            for r in range(1, NDEV):
                px, py, pc = _peer(r)
                _remote(crow_ref, g0s.at[pl.ds(4 * px + 2 * py + pc, 1), :], cs.at[r - 1], cr.at[r - 1], r).wait_recv()
            mine.wait()
            cv = g0s[:, 0:D]
            sc = cv * _sigmoid(cv)
            scb = jnp.concatenate([sc, jnp.zeros_like(sc)], axis=0).astype(BF16)
            b_cols = sum(jnp.where(ci == j, b_ref[:, j * EC:(j + 1) * EC], 0.0) for j in range(NCHIP))
            modp[...] = _dot(scb, wada_ref[...].astype(BF16))[0:NDEV, :] + b_cols
            own = pltpu.make_async_copy(modp.at[pl.ds(me, 1), :], modb.at[ci], local_sems.at[1])
            own.start()
            msend = []
            for k in range(1, NCHIP):
                cp = _remote(modp.at[pl.ds(2 * (ci ^ k) + c, 1), :], modb.at[ci], ms.at[k - 1], mr.at[k - 1], 2 * k)
                cp.start()
                msend.append(cp)
            for k in range(1, NCHIP):
                _remote(modp.at[pl.ds(me, 1), :], modb.at[ci ^ k], ms.at[k - 1], mr.at[k - 1], 2 * k).wait_recv()
            own.wait()
            for j in range(NCHIP):
                mod_ref[:, j * EC:(j + 1) * EC] = modb[j]
            for cp in csend + msend:
                cp.wait_send()
            g0_ref[...] = g0s[...]
            for j in range(NCHIP):
                for k in range(4):
                    conv_ref[k:k + 1, j * cw:(j + 1) * cw] = g0s[2 * j:2 * j + 1, D + k * cw:D + (k + 1) * cw]

        def keep(k):
            return pltpu.make_async_copy(wbuf.at[ci ^ k], wbf_ref.at[ci ^ k], osem.at[k])

        @pl.when(jnp.logical_and(s == 1, t == 0))
        def _():
            keep(0).start()
            wg.neighbours_landed(ys, yr)
            wg.sibling_landed(1)
            keep(1).start()

        @pl.when(jnp.logical_and(s == 2, t == 0))
        def _():
            wg.sibling_landed(2)
            keep(2).start()

        @pl.when(jnp.logical_and(s == 3, t == 0))
        def _():
            wg.relay_sems = (ys, yr)
            wg.diagonal_landed()
            keep(3).start()

        rows = pl.ds(pl.multiple_of(t * ts, ts), ts)

        @pl.when(s == 0)
        def _():
            hp, _, _ = _rms_fwd(x_ref[...], np_ref[...])
            h = hp * (1.0 + mod_ref[:, D:2 * D]) + mod_ref[:, 0:D]
            hb_all[rows, :] = h.astype(BF16)
            ht_ref[...] = h.T.astype(BF16)

        proj_ref[...] = _dot(hb_all[rows, :], wbuf[ci ^ s])

        @pl.when(jnp.logical_and(s == NCHIP - 1, t == nt - 1))
        def _():
            wg.relay_sems = (ys, yr)
            wg.finish_relayed()
            for k in range(NCHIP):
                keep(k).wait()

    vm = pl.BlockSpec(memory_space=pltpu.VMEM)
    first_pass = lambda s, t: jnp.where(s == 0, t, nt - 1)
    grid_spec = pltpu.PrefetchScalarGridSpec(
        num_scalar_prefetch=1, grid=(NCHIP, nt),
        in_specs=[vm, vm, vm, vm, vm, vm, vm, pl.BlockSpec((ts, D), lambda s, t, o: (first_pass(s, t), 0)),
                  pl.BlockSpec((1, D), lambda s, t, o: (0, 0))],
        out_specs=[vm, vm, vm, pl.BlockSpec(memory_space=pl.ANY), vm, vm,
                   pl.BlockSpec((ts, EC), lambda s, t, o: (t, o[s])),
                   pl.BlockSpec((D, ts), lambda s, t, o: (0, first_pass(s, t)))],
        scratch_shapes=[pltpu.VMEM((1, wc), F32),
                        pltpu.VMEM((NDEV, wc), F32), pltpu.VMEM((NDEV, EC), F32), pltpu.VMEM((NCHIP, 1, EC), F32),
                        pltpu.VMEM((NCHIP, D, EC), BF16), pltpu.VMEM((S, D), BF16),
                        pltpu.SemaphoreType.DMA((NDEV - 1,)), pltpu.SemaphoreType.DMA((NDEV - 1,)),
                        pltpu.SemaphoreType.DMA((NCHIP - 1,)), pltpu.SemaphoreType.DMA((NCHIP - 1,))]
        + _WeightGather.SEMS + [pltpu.SemaphoreType.DMA((2,))] * 3 + [pltpu.SemaphoreType.DMA((NCHIP,))])
    return pl.pallas_call(
        body, name="start_in_proj", grid_spec=grid_spec,
        out_shape=[jax.ShapeDtypeStruct((NDEV, wc), F32), jax.ShapeDtypeStruct((4, R), F32),
                   jax.ShapeDtypeStruct((1, 3 * D), F32),
                   jax.ShapeDtypeStruct((NCHIP, D, EC), BF16), jax.ShapeDtypeStruct((S, LANES), F32),
                   jax.ShapeDtypeStruct((S, LANES), F32), jax.ShapeDtypeStruct((S, E), F32),
                   jax.ShapeDtypeStruct((D, S), BF16)],
        compiler_params=_cp(("arbitrary", "arbitrary")),
    )(order, c, conv_w, w_ada, b_ada, w_in, pos, _rope_freq(), x, norm_pre)


class _ReduceScatter:
    @staticmethod
    def scratch(n_units, rows, ucols, max_owned):
        half = rows // 2
        return [pltpu.VMEM((n_units, half, ucols), F32), pltpu.VMEM((n_units, half, ucols), BF16),
                pltpu.VMEM((max_owned, NCHIP, half, ucols), BF16),
                pltpu.SemaphoreType.DMA((2,)), pltpu.SemaphoreType.DMA((n_units,)),
                pltpu.SemaphoreType.DMA((n_units, NCHIP)), pltpu.SemaphoreType.DMA((n_units,)),
                pltpu.SemaphoreType.DMA((n_units,))]

    def __init__(self, g_ref, out_ref, units, sib, stage, got, sem1, send2, recv2, send3, recv3):
        x, y, c = _me()
        self.c, self.ci = c, 2 * x + y
        self.g, self.out, self.units = g_ref, out_ref, units
        self.sib, self.stage, self.got = sib, stage, got
        self.sem1, self.send2, self.recv2, self.send3, self.recv3 = sem1, send2, recv2, send3, recv3
        self.half = g_ref.shape[1] // 2
        self.ucols = g_ref.shape[2]
        self.r0 = pl.multiple_of(c * self.half, self.half)
        self.r1 = pl.multiple_of((1 - c) * self.half, self.half)
        self.slot0 = units[0][0]
        assert [u[0] for u in units] == list(range(self.slot0, self.slot0 + len(units)))
        seen = {}
        self.local = []
        for _, owner, _ in units:
            self.local.append(seen.get(owner, 0))
            seen[owner] = seen.get(owner, 0) + 1

    def _halves(self):
        n = len(self.units)
        return _remote(self.g.at[pl.ds(self.slot0, n), pl.ds(self.r1, self.half), :], self.sib,
                       self.sem1.at[0], self.sem1.at[1], 1)

    def _partial(self, i, sender):
        _, owner, _ = self.units[i]
        return pltpu.make_async_remote_copy(
            src_ref=self.stage.at[i], dst_ref=self.got.at[self.local[i], sender],
            send_sem=self.send2.at[i], recv_sem=self.recv2.at[i, sender],
            device_id=(owner // 2, owner % 2, self.c), device_id_type=MESH)

    def _back(self, i, start):
        off = self.units[i][2]
        blk = self.out.at[pl.ds(start, self.half), off:off + self.ucols]
        return _remote(blk, blk, self.send3.at[i], self.recv3.at[i], 1)

    def start_halves(self):
        self._halves().start()

    def send_partials(self):
        self._halves().wait_recv()
        for i, (slot, owner, _) in enumerate(self.units):
            @pl.when(self.ci != owner)
            def _():
                self.stage[i] = (self.g[slot, pl.ds(self.r0, self.half), :] + self.sib[i]).astype(BF16)
                self._partial(i, self.ci).start()

    def reduce_owned(self):
        for i, (slot, owner, off) in enumerate(self.units):
            @pl.when(self.ci == owner)
            def _():
                rows, cols = pl.ds(self.r0, self.half), slice(off, off + self.ucols)
                self.out[rows, cols] = self.g[slot, pl.ds(self.r0, self.half), :] + self.sib[i]
                for s in range(NCHIP):
                    if s != owner:
                        self._partial(i, s).wait_recv()
                        self.out[rows, cols] += self.got[self.local[i], s].astype(F32)
                self._back(i, self.r0).start()

    def finish(self):
        self._halves().wait_send()
        for i, (_, owner, _) in enumerate(self.units):
            @pl.when(self.ci == owner)
            def _():
                self._back(i, self.r1).wait_recv()
                self._back(i, self.r0).wait_send()

            @pl.when(self.ci != owner)
            def _():
                self._partial(i, self.ci).wait_send()


def _silu_rows(c_ref):
    cv = c_ref[:, 0:D]
    sc = cv * _sigmoid(cv)
    return jnp.concatenate([sc, jnp.zeros_like(sc)], axis=0).astype(BF16)


def _adamw(groups):
    steps = 4
    specs = [pl.BlockSpec((w.shape[0] // steps, w.shape[1]), lambda i: (i, 0)) for w, _, _, _ in groups]

    def body(*refs):
        ins, outs = refs[:4 * len(groups)], refs[4 * len(groups):]
        for j in range(len(groups)):
            w_ref, g_ref, m_ref, v_ref = ins[4 * j:4 * j + 4]
            d_ref, nm_ref, nv_ref = outs[3 * j:3 * j + 3]
            d_ref[...], nm_ref[...], nv_ref[...] = _adamw_values(w_ref[...], g_ref[...], m_ref[...], v_ref[...])

    res = pl.pallas_call(
        body, name="adamw_big", grid=(steps,),
        in_specs=[s for s in specs for _ in range(4)], out_specs=[s for s in specs for _ in range(3)],
        out_shape=[jax.ShapeDtypeStruct(w.shape, F32) for w, _, _, _ in groups for _ in range(3)],
        compiler_params=_cp(("parallel",)),
    )(*[a for grp in groups for a in grp])
    return [res[3 * j:3 * j + 3] for j in range(len(groups))]


def _adamw_values(w, g, m, v):
    nm = B1 * m + (1.0 - B1) * g
    nv = B2 * v + (1.0 - B2) * (g * g)
    m_hat = nm / (1.0 - B1 ** STEP)
    v_hat = nv / (1.0 - B2 ** STEP)
    return (-LR) * (m_hat / (jnp.sqrt(v_hat) + ADAM_EPS) + WD * w), nm, nv


NB = R // HEAD
SMALL = (("b_ada", (1, 3 * D)), ("norm_pre", (1, D)), ("norm_post", (1, D)), ("conv_w", (4, R // NCHIP)),
         ("conv_b", (1, R)), ("w_rg_a", (NB, HEAD, HEAD)), ("b_rg_a", (1, R)), ("w_rg_x", (NB, HEAD, HEAD)),
         ("b_rg_x", (1, R)), ("lru_lambda", (1, R)), ("norm_rec", (1, R)), ("norm_att", (1, R)))


def _small_update(ao8, sm8, dwa8, dwx8, ai8, cg, params):
    n = len(SMALL)

    def body(ao_ref, sm_ref, dwa_ref, dwx_ref, ai_ref, cg_ref, *refs):
        pin, pout, (gada_ref, loss_ref, dmod) = refs[:3 * n], refs[3 * n:7 * n], refs[7 * n:]
        xx, yy, _ = _me()
        ci = 2 * xx + yy

        def total(ref, *idx):
            acc = ref[(0,) + idx].astype(F32)
            for d in range(1, NDEV):
                acc = acc + ref[(d,) + idx].astype(F32)
            return acc

        row = lambda ref, r, lanes=slice(None): total(ref, slice(r, r + 1), lanes)
        mine = lambda parts: sum(jnp.where(ci == j, part, 0.0) for j, part in enumerate(parts))
        cw = R // NCHIP
        grads = {
            "b_ada": [jnp.concatenate([row(ai_ref, 0), row(ai_ref, 1), row(ao_ref, 0)], axis=1)],
            "norm_pre": [row(ai_ref, 2)], "norm_post": [row(ao_ref, 1)],
            "conv_w": [mine([row(sm_ref, 8 + r, slice(j * cw, (j + 1) * cw)) for j in range(NCHIP)]) for r in range(4)],
            "conv_b": [row(sm_ref, 4)], "b_rg_a": [row(sm_ref, 0)], "b_rg_x": [row(sm_ref, 1)],
            "lru_lambda": [row(sm_ref, 2)], "norm_rec": [row(sm_ref, 3)], "norm_att": [row(ao_ref, 2, slice(0, R))],
            "w_rg_a": [total(dwa_ref, h) for h in range(NB)], "w_rg_x": [total(dwx_ref, h) for h in range(NB)],
        }
        loss_ref[...] = row(ao_ref, 3, slice(0, LANES)) * (0.5 / D)
        for k, (name, shape) in enumerate(SMALL):
            w_ref, m_ref, v_ref = pin[3 * k:3 * k + 3]
            outs = pout[4 * k:4 * k + 4]
            for r, g in enumerate(grads[name]):
                at = (slice(None),) if len(grads[name]) == 1 else ((r,) if len(shape) == 3 else (slice(r, r + 1),))
                res = (g,) + _adamw_values(w_ref[at], g, m_ref[at], v_ref[at])
                for o_ref, val in zip(outs, res):
                    o_ref[at] = val
        for d in range(NDEV):
            dmod[d:d + 1, :] = jnp.concatenate([ai_ref[d, 0:1, :], ai_ref[d, 1:2, :], ao_ref[d, 0:1, :]], axis=1)
        cols = mine([dmod[:, j * EC:(j + 1) * EC] for j in range(NCHIP)])
        colsb = jnp.concatenate([cols, jnp.zeros_like(cols)], axis=0).astype(BF16)
        gada_ref[...] = _dot_tn(_silu_rows(cg_ref), colsb)

    shapes = [jax.ShapeDtypeStruct(s, F32) for _, s in SMALL]
    outs = pl.pallas_call(
        body, name="small_update",
        out_shape=[s for s in shapes for _ in range(4)] + [jax.ShapeDtypeStruct((D, EC), F32),
                                                           jax.ShapeDtypeStruct((1, LANES), F32)],
        scratch_shapes=[pltpu.VMEM((NDEV, 3 * D), F32)],
        compiler_params=_cp(),
    )(ao8, sm8, dwa8, dwx8, ai8, cg, *params)
    return outs[:4 * n], outs[4 * n], outs[4 * n + 1]


BIG = ("w_ada", "w_in", "w_out")
WEIGHTS = ("w_ada", "b_ada", "norm_pre", "norm_post", "w_in", "conv_w", "conv_b", "w_rg_a", "b_rg_a", "w_rg_x",
           "b_rg_x", "lru_lambda", "norm_rec", "norm_att", "w_out")


def kernel(x, c, positions, w_ada, b_ada, norm_pre, norm_post, w_in, conv_w, conv_b, w_rg_a, b_rg_a, w_rg_x, b_rg_x, lru_lambda, norm_rec, norm_att, w_out, loss_target, m_w_ada, m_b_ada, m_norm_pre, m_norm_post, m_w_in, m_conv_w, m_conv_b, m_w_rg_a, m_b_rg_a, m_w_rg_x, m_b_rg_x, m_lru_lambda, m_norm_rec, m_norm_att, m_w_out, v_w_ada, v_b_ada, v_norm_pre, v_norm_post, v_w_in, v_conv_w, v_conv_b, v_w_rg_a, v_b_rg_a, v_w_rg_x, v_b_rg_x, v_lru_lambda, v_norm_rec, v_norm_att, v_w_out):
    given = dict(locals())
    wts = {n: given[n] for n in WEIGHTS}
    ms = {n: given["m_" + n] for n in WEIGHTS}
    vs = {n: given["v_" + n] for n in WEIGHTS}
    xi, yi, _ = _me()
    chip = 2 * xi + yi

    order = (chip ^ jnp.arange(NCHIP, dtype=jnp.int32)).astype(jnp.int32)
    cg, conv_full, mod, w_in_bf, cos, sin, proj, ht = _start_in_proj(
        c, conv_w[0], w_ada[0], b_ada, w_in[0], positions, x[0], norm_pre, order)

    p = dict(norm_pre=norm_pre, norm_post=norm_post, conv_b=conv_b, b_rg_a=b_rg_a, b_rg_x=b_rg_x,
             lru_lambda=lru_lambda, norm_rec=norm_rec, norm_att=norm_att, w_rg_a=w_rg_a[0], w_rg_x=w_rg_x[0])
    grad_x, g_in, g_out, gathered = _local_step(
        x[0], cos, sin, loss_target[0], mod, w_in_bf, proj, ht, w_out[0], conv_full, p)

    params = [d[n].reshape(shape) for n, shape in SMALL for d in (wts, ms, vs)]
    small_out, g_ada, loss_row = _small_update(*gathered, cg, params)
    grads = {"w_out": g_out, "w_in": g_in, "w_ada": g_ada}
    delta, new_m, new_v = {}, {}, {}
    for k, (n, _) in enumerate(SMALL):
        grads[n], delta[n], new_m[n], new_v[n] = small_out[4 * k:4 * k + 4]
    for n, res in zip(BIG, _adamw([(wts[n][0], grads[n], ms[n][0], vs[n][0]) for n in BIG])):
        delta[n], new_m[n], new_v[n] = res
    out = lambda d: [d[n].reshape(wts[n].shape) for n in WEIGHTS]
    return (loss_row[0, 0], grad_x.reshape(x.shape), *out(grads), *out(delta), *out(new_m), *out(new_v))
```
